```python
import math
import jax, jax.numpy as jnp
from jax import lax
import numpy as np

D_MODEL = 1024
BATCH = 16
SEQ = 4096
DEPTH = 2

N_A_LAYERS = DEPTH // 2
N_B_LAYERS = DEPTH - N_A_LAYERS
HEAD_DIM = 64
MEM_LEN = 256
MEM_HEADS = 4
MEM_WIDTH = MEM_HEADS * HEAD_DIM
MIX_WIDTH = D_MODEL
TOK_WIDTH = MIX_WIDTH - MEM_WIDTH
POOL_WINDOWS = (2, 4, 8, 16)
POOL_GROUP = TOK_WIDTH // len(POOL_WINDOWS)
FOX_HEADS = TOK_WIDTH // HEAD_DIM
Q_BLOCK = 128
D_FF = ((8 * D_MODEL // 3 + 63) // 64) * 64
CONV_WIDTH = 3
DN_ALPHA = (2.0 * DEPTH) ** 0.25
DN_BETA = (8.0 * DEPTH) ** -0.25
LN_EPS = 1e-5

kernel_name = "yoco_pool_fox_memory_convffn"


def layer_norm(x, g, b):
    xf = x.astype(jnp.float32)
    mu = jnp.mean(xf, axis=-1, keepdims=True)
    var = jnp.mean(jnp.square(xf - mu), axis=-1, keepdims=True)
    y = (xf - mu) * lax.rsqrt(var + LN_EPS) * g.astype(jnp.float32) + b.astype(jnp.float32)
    return y.astype(x.dtype)


def causal_mean_minus_self(u, w):
    S = u.shape[1]
    uf = u.astype(jnp.float32)
    csum = jnp.cumsum(uf, axis=1)
    lag = jnp.pad(csum, ((0, 0), (w, 0), (0, 0)))[:, :S]
    count = jnp.minimum(jnp.arange(1, S + 1), w).astype(jnp.float32)[None, :, None]
    return ((csum - lag) / count - uf).astype(u.dtype)


def multiscale_pool(u, pool_w, pool_scale):
    B, S, _ = u.shape
    ug = u.reshape(B, S, len(POOL_WINDOWS), POOL_GROUP)
    pooled = jnp.stack([causal_mean_minus_self(ug[:, :, i], w)
                        for i, w in enumerate(POOL_WINDOWS)], axis=2)
    mixed = jnp.einsum('bsgc,gcd->bsgd', pooled, pool_w)
    return mixed.reshape(B, S, TOK_WIDTH) * pool_scale


def memory_attention(q_mem, mem_k, mem_v):
    B, S = q_mem.shape[:2]
    logits = jnp.einsum('bshd,bmhd->bhsm', q_mem, mem_k).astype(jnp.float32) * (HEAD_DIM ** -0.5)
    p = jax.nn.softmax(logits, axis=-1)
    out = jnp.einsum('bhsm,bmhd->bshd', p.astype(mem_v.dtype), mem_v)
    return out.reshape(B, S, MEM_WIDTH)


def forgetting_attention(q, k, v, F):
    B, S, H, Dh = q.shape
    nb = S // Q_BLOCK
    scale = Dh ** -0.5
    qb = q.reshape(B, nb, Q_BLOCK, H, Dh).transpose(1, 0, 2, 3, 4)
    Fqb = F.reshape(B, nb, Q_BLOCK, H).transpose(1, 0, 3, 2)
    Fk = F.transpose(0, 2, 1)[:, :, None, :]
    k_pos = jnp.arange(S)

    def block(args):
        qi, Fqi, start = args
        logits = jnp.einsum('bqhd,bkhd->bhqk', qi, k).astype(jnp.float32) * scale
        logits = logits + Fqi[..., None] - Fk
        q_pos = start + jnp.arange(Q_BLOCK)
        mask = q_pos[:, None] >= k_pos[None, :]
        logits = jnp.where(mask, logits, -jnp.inf)
        p = jax.nn.softmax(logits, axis=-1)
        return jnp.einsum('bhqk,bkhd->bqhd', p.astype(v.dtype), v)

    starts = jnp.arange(nb) * Q_BLOCK
    out = lax.map(block, (qb, Fqb, starts))
    return out.transpose(1, 0, 2, 3, 4).reshape(B, S, H * Dh)


def conv_ffn(x, w_up, conv_w, conv_b, w_down):
    h = x @ w_up
    C = h.shape[-1]
    h = lax.conv_general_dilated(h, conv_w[:, None, :].astype(h.dtype), window_strides=(1,),
                                 padding=[(CONV_WIDTH - 1, 0)],
                                 dimension_numbers=('NWC', 'WIO', 'NWC'),
                                 feature_group_count=C) + conv_b
    u, g = jnp.split(h, 2, axis=-1)
    return (jax.nn.silu(g) * u) @ w_down


def _fwd_setup_inputs(seed: int = 0) -> dict:
    key = jax.random.key(seed)
    ks = jax.random.split(key, 24)
    f32 = jnp.float32

    def nrm(k, shape, fan_in, gain=1.0):
        return jax.random.normal(k, shape, f32) * (gain * fan_in ** -0.5)

    x = jax.random.normal(ks[0], (BATCH, SEQ, D_MODEL), f32)
    mem = jax.random.normal(ks[1], (BATCH, MEM_LEN, D_MODEL), f32)
    a_w_in = nrm(ks[2], (N_A_LAYERS, D_MODEL, MIX_WIDTH), D_MODEL)
    a_pool_w = nrm(ks[3], (N_A_LAYERS, len(POOL_WINDOWS), POOL_GROUP, POOL_GROUP), POOL_GROUP)
    a_pool_scale = 1.0 + 0.1 * jax.random.normal(ks[4], (N_A_LAYERS, TOK_WIDTH), f32)
    a_w_out = nrm(ks[5], (N_A_LAYERS, MIX_WIDTH, D_MODEL), MIX_WIDTH, DN_BETA)
    b_w_q = nrm(ks[6], (N_B_LAYERS, D_MODEL, MIX_WIDTH), D_MODEL)
    b_w_out = nrm(ks[7], (N_B_LAYERS, MIX_WIDTH, D_MODEL), MIX_WIDTH, DN_BETA)
    kv_w = jnp.concatenate([nrm(ks[8], (D_MODEL, 2 * TOK_WIDTH), D_MODEL),
                            nrm(ks[9], (D_MODEL, FOX_HEADS), D_MODEL, 0.5)], axis=-1)
    f_b = jax.random.uniform(ks[10], (FOX_HEADS,), f32, 1.0, 4.0)
    mem_w_kv = nrm(ks[11], (DEPTH, D_MODEL, 2 * MEM_WIDTH), D_MODEL)
    ln1_g = 1.0 + 0.05 * jax.random.normal(ks[12], (DEPTH, D_MODEL), f32)
    ln1_b = 0.02 * jax.random.normal(ks[13], (DEPTH, D_MODEL), f32)
    ln2_g = 1.0 + 0.05 * jax.random.normal(ks[14], (DEPTH, D_MODEL), f32)
    ln2_b = 0.02 * jax.random.normal(ks[15], (DEPTH, D_MODEL), f32)
    ffn_w_up = nrm(ks[16], (DEPTH, D_MODEL, 2 * D_FF), D_MODEL)
    ffn_conv_w = nrm(ks[17], (DEPTH, CONV_WIDTH, 2 * D_FF), CONV_WIDTH)
    ffn_conv_b = 0.02 * jax.random.normal(ks[18], (DEPTH, 2 * D_FF), f32)
    ffn_w_down = nrm(ks[19], (DEPTH, D_FF, D_MODEL), D_FF, DN_BETA)
    return {"x": x, "mem": mem, "a_w_in": a_w_in, "a_pool_w": a_pool_w,
            "a_pool_scale": a_pool_scale, "a_w_out": a_w_out, "b_w_q": b_w_q,
            "b_w_out": b_w_out, "kv_w": kv_w, "f_b": f_b, "mem_w_kv": mem_w_kv,
            "ln1_g": ln1_g, "ln1_b": ln1_b, "ln2_g": ln2_g, "ln2_b": ln2_b,
            "ffn_w_up": ffn_w_up, "ffn_conv_w": ffn_conv_w, "ffn_conv_b": ffn_conv_b,
            "ffn_w_down": ffn_w_down}


def _fwd_reference(x, mem, a_w_in, a_pool_w, a_pool_scale, a_w_out, b_w_q, b_w_out, kv_w, f_b,
              mem_w_kv, ln1_g, ln1_b, ln2_g, ln2_b, ffn_w_up, ffn_conv_w, ffn_conv_b,
              ffn_w_down):
    B, S, _ = x.shape
    M = mem.shape[1]
    k_sh = v_sh = F_sh = None
    for l in range(DEPTH):
        mem_kv = mem @ mem_w_kv[l]
        mem_k = mem_kv[..., :MEM_WIDTH].reshape(B, M, MEM_HEADS, HEAD_DIM)
        mem_v = mem_kv[..., MEM_WIDTH:].reshape(B, M, MEM_HEADS, HEAD_DIM)

        if l < N_A_LAYERS:
            proj = x @ a_w_in[l]
            tok = multiscale_pool(proj[..., :TOK_WIDTH], a_pool_w[l], a_pool_scale[l])
            w_out = a_w_out[l]
        else:
            if l == N_A_LAYERS:
                kvf = x @ kv_w
                k_sh = kvf[..., :TOK_WIDTH].reshape(B, S, FOX_HEADS, HEAD_DIM)
                v_sh = kvf[..., TOK_WIDTH:2 * TOK_WIDTH].reshape(B, S, FOX_HEADS, HEAD_DIM)
                log_f = jax.nn.log_sigmoid(kvf[..., 2 * TOK_WIDTH:].astype(jnp.float32)
                                           + f_b.astype(jnp.float32))
                F_sh = jnp.cumsum(log_f, axis=1)
            j = l - N_A_LAYERS
            proj = x @ b_w_q[j]
            q = proj[..., :TOK_WIDTH].reshape(B, S, FOX_HEADS, HEAD_DIM)
            tok = forgetting_attention(q, k_sh, v_sh, F_sh)
            w_out = b_w_out[j]

        q_mem = proj[..., TOK_WIDTH:].reshape(B, S, MEM_HEADS, HEAD_DIM)
        mem_out = memory_attention(q_mem, mem_k, mem_v)
        mix = jnp.concatenate([tok, mem_out], axis=-1) @ w_out
        x = layer_norm(DN_ALPHA * x + mix, ln1_g[l], ln1_b[l])

        ffn = conv_ffn(x, ffn_w_up[l], ffn_conv_w[l], ffn_conv_b[l], ffn_w_down[l])
        x = layer_norm(DN_ALPHA * x + ffn, ln2_g[l], ln2_b[l])
    return x


import jax as _jax
import jax.numpy as _jnp

TWIN_FORMAT = 'train_step'
FWD_PARAMS = ['x', 'mem', 'a_w_in', 'a_pool_w', 'a_pool_scale', 'a_w_out', 'b_w_q', 'b_w_out', 'kv_w', 'f_b', 'mem_w_kv', 'ln1_g', 'ln1_b', 'ln2_g', 'ln2_b', 'ffn_w_up', 'ffn_conv_w', 'ffn_conv_b', 'ffn_w_down']
TWIN_WEIGHTS = ['a_w_in', 'a_pool_w', 'a_pool_scale', 'a_w_out', 'b_w_q', 'b_w_out', 'kv_w', 'f_b', 'mem_w_kv', 'ln1_g', 'ln1_b', 'ln2_g', 'ln2_b', 'ffn_w_up', 'ffn_conv_w', 'ffn_conv_b', 'ffn_w_down']
TWIN_DIFF_INPUT = 'x'
TWIN_INPUTS = ['x', 'mem', 'a_w_in', 'a_pool_w', 'a_pool_scale', 'a_w_out', 'b_w_q', 'b_w_out', 'kv_w', 'f_b', 'mem_w_kv', 'ln1_g', 'ln1_b', 'ln2_g', 'ln2_b', 'ffn_w_up', 'ffn_conv_w', 'ffn_conv_b', 'ffn_w_down', 'loss_target', 'm_a_w_in', 'm_a_pool_w', 'm_a_pool_scale', 'm_a_w_out', 'm_b_w_q', 'm_b_w_out', 'm_kv_w', 'm_f_b', 'm_mem_w_kv', 'm_ln1_g', 'm_ln1_b', 'm_ln2_g', 'm_ln2_b', 'm_ffn_w_up', 'm_ffn_conv_w', 'm_ffn_conv_b', 'm_ffn_w_down', 'v_a_w_in', 'v_a_pool_w', 'v_a_pool_scale', 'v_a_w_out', 'v_b_w_q', 'v_b_w_out', 'v_kv_w', 'v_f_b', 'v_mem_w_kv', 'v_ln1_g', 'v_ln1_b', 'v_ln2_g', 'v_ln2_b', 'v_ffn_w_up', 'v_ffn_conv_w', 'v_ffn_conv_b', 'v_ffn_w_down']
TWIN_OUTPUTS = ['loss', 'grad_x', 'grad_a_w_in', 'grad_a_pool_w', 'grad_a_pool_scale', 'grad_a_w_out', 'grad_b_w_q', 'grad_b_w_out', 'grad_kv_w', 'grad_f_b', 'grad_mem_w_kv', 'grad_ln1_g', 'grad_ln1_b', 'grad_ln2_g', 'grad_ln2_b', 'grad_ffn_w_up', 'grad_ffn_conv_w', 'grad_ffn_conv_b', 'grad_ffn_w_down', 'delta_a_w_in', 'delta_a_pool_w', 'delta_a_pool_scale', 'delta_a_w_out', 'delta_b_w_q', 'delta_b_w_out', 'delta_kv_w', 'delta_f_b', 'delta_mem_w_kv', 'delta_ln1_g', 'delta_ln1_b', 'delta_ln2_g', 'delta_ln2_b', 'delta_ffn_w_up', 'delta_ffn_conv_w', 'delta_ffn_conv_b', 'delta_ffn_w_down', 'new_m_a_w_in', 'new_m_a_pool_w', 'new_m_a_pool_scale', 'new_m_a_w_out', 'new_m_b_w_q', 'new_m_b_w_out', 'new_m_kv_w', 'new_m_f_b', 'new_m_mem_w_kv', 'new_m_ln1_g', 'new_m_ln1_b', 'new_m_ln2_g', 'new_m_ln2_b', 'new_m_ffn_w_up', 'new_m_ffn_conv_w', 'new_m_ffn_conv_b', 'new_m_ffn_w_down', 'new_v_a_w_in', 'new_v_a_pool_w', 'new_v_a_pool_scale', 'new_v_a_w_out', 'new_v_b_w_q', 'new_v_b_w_out', 'new_v_kv_w', 'new_v_f_b', 'new_v_mem_w_kv', 'new_v_ln1_g', 'new_v_ln1_b', 'new_v_ln2_g', 'new_v_ln2_b', 'new_v_ffn_w_up', 'new_v_ffn_conv_w', 'new_v_ffn_conv_b', 'new_v_ffn_w_down']
TWIN_LEAF_KINDS = {'loss': 'loss', 'grad_x': 'grad_x', 'grad_a_w_in': 'grad_w', 'grad_a_pool_w': 'grad_w', 'grad_a_pool_scale': 'grad_w', 'grad_a_w_out': 'grad_w', 'grad_b_w_q': 'grad_w', 'grad_b_w_out': 'grad_w', 'grad_kv_w': 'grad_w', 'grad_f_b': 'grad_w', 'grad_mem_w_kv': 'grad_w', 'grad_ln1_g': 'grad_w', 'grad_ln1_b': 'grad_w', 'grad_ln2_g': 'grad_w', 'grad_ln2_b': 'grad_w', 'grad_ffn_w_up': 'grad_w', 'grad_ffn_conv_w': 'grad_w', 'grad_ffn_conv_b': 'grad_w', 'grad_ffn_w_down': 'grad_w', 'delta_a_w_in': 'delta_w', 'delta_a_pool_w': 'delta_w', 'delta_a_pool_scale': 'delta_w', 'delta_a_w_out': 'delta_w', 'delta_b_w_q': 'delta_w', 'delta_b_w_out': 'delta_w', 'delta_kv_w': 'delta_w', 'delta_f_b': 'delta_w', 'delta_mem_w_kv': 'delta_w', 'delta_ln1_g': 'delta_w', 'delta_ln1_b': 'delta_w', 'delta_ln2_g': 'delta_w', 'delta_ln2_b': 'delta_w', 'delta_ffn_w_up': 'delta_w', 'delta_ffn_conv_w': 'delta_w', 'delta_ffn_conv_b': 'delta_w', 'delta_ffn_w_down': 'delta_w', 'new_m_a_w_in': 'new_m', 'new_m_a_pool_w': 'new_m', 'new_m_a_pool_scale': 'new_m', 'new_m_a_w_out': 'new_m', 'new_m_b_w_q': 'new_m', 'new_m_b_w_out': 'new_m', 'new_m_kv_w': 'new_m', 'new_m_f_b': 'new_m', 'new_m_mem_w_kv': 'new_m', 'new_m_ln1_g': 'new_m', 'new_m_ln1_b': 'new_m', 'new_m_ln2_g': 'new_m', 'new_m_ln2_b': 'new_m', 'new_m_ffn_w_up': 'new_m', 'new_m_ffn_conv_w': 'new_m', 'new_m_ffn_conv_b': 'new_m', 'new_m_ffn_w_down': 'new_m', 'new_v_a_w_in': 'new_v', 'new_v_a_pool_w': 'new_v', 'new_v_a_pool_scale': 'new_v', 'new_v_a_w_out': 'new_v', 'new_v_b_w_q': 'new_v', 'new_v_b_w_out': 'new_v', 'new_v_kv_w': 'new_v', 'new_v_f_b': 'new_v', 'new_v_mem_w_kv': 'new_v', 'new_v_ln1_g': 'new_v', 'new_v_ln1_b': 'new_v', 'new_v_ln2_g': 'new_v', 'new_v_ln2_b': 'new_v', 'new_v_ffn_w_up': 'new_v', 'new_v_ffn_conv_w': 'new_v', 'new_v_ffn_conv_b': 'new_v', 'new_v_ffn_w_down': 'new_v'}


def _forward(args):
    return _fwd_reference(*[args[k] for k in FWD_PARAMS])


def _output_shape():
    out = _jax.eval_shape(lambda: _forward(_fwd_setup_inputs(0)))
    return out.shape, out.dtype

N_MICROBATCH = 1
ADAM_LR = 0.001
ADAM_B1 = 0.9
ADAM_B2 = 0.999
ADAM_EPS = 1e-08
ADAM_WD = 0.01
ADAM_STEP = 10
PER_EXAMPLE_BATCH_AXIS = {'x': 0, 'mem': 0, 'loss_target': 0}
SHARED_INPUTS = []
_WEIGHT_DTYPES = {'a_w_in': _jnp.float32, 'a_pool_w': _jnp.float32, 'a_pool_scale': _jnp.float32, 'a_w_out': _jnp.float32, 'b_w_q': _jnp.float32, 'b_w_out': _jnp.float32, 'kv_w': _jnp.float32, 'f_b': _jnp.float32, 'mem_w_kv': _jnp.float32, 'ln1_g': _jnp.float32, 'ln1_b': _jnp.float32, 'ln2_g': _jnp.float32, 'ln2_b': _jnp.float32, 'ffn_w_up': _jnp.float32, 'ffn_conv_w': _jnp.float32, 'ffn_conv_b': _jnp.float32, 'ffn_w_down': _jnp.float32}
MOMENT_SCALE = {'a_w_in': 7.051897e-02, 'a_pool_w': 8.087543e-02, 'a_pool_scale': 7.989262e-02, 'a_w_out': 1.416387e-01, 'b_w_q': 2.216255e-02, 'b_w_out': 5.544739e-02, 'kv_w': 2.856637e-02, 'f_b': 1.089869e-01, 'mem_w_kv': 1.061359e-02, 'ln1_g': 5.623938e+00, 'ln1_b': 7.813650e-01, 'ln2_g': 4.583741e+01, 'ln2_b': 1.398515e+00, 'ffn_w_up': 3.369840e-02, 'ffn_conv_w': 3.399012e-02, 'ffn_conv_b': 3.920115e-02, 'ffn_w_down': 1.085257e-01}


def _to_microbatches(a, axis):
    t = _jnp.moveaxis(a, axis, 0)
    t = t.reshape((N_MICROBATCH, t.shape[0] // N_MICROBATCH) + t.shape[1:])
    return _jnp.moveaxis(t, 1, axis + 1)


def setup_inputs(seed: int = 0) -> dict:
    inp = _fwd_setup_inputs(seed)
    key = _jax.random.fold_in(_jax.random.key(seed), 7919)
    shape, _ = _output_shape()
    out = dict(inp)
    out["loss_target"] = _jax.random.normal(_jax.random.fold_in(key, 0), shape, _jnp.float32)
    for i, name in enumerate(TWIN_WEIGHTS):
        w = inp[name].astype(_jnp.float32)
        if MOMENT_SCALE is None:
            s = _jnp.sqrt(_jnp.mean(_jnp.square(w)) + 1e-30)
        else:
            s = MOMENT_SCALE[name]
        km, kv = _jax.random.split(_jax.random.fold_in(key, i + 1))
        out[name] = w
        out["m_" + name] = s * _jax.random.normal(km, w.shape, _jnp.float32)
        out["v_" + name] = (s * s) * _jax.random.uniform(kv, w.shape, _jnp.float32, 0.5, 1.5)
    if N_MICROBATCH > 1:
        for name, axis in PER_EXAMPLE_BATCH_AXIS.items():
            out[name] = _to_microbatches(out[name], axis)
    return {'x': out['x'], 'mem': out['mem'], 'a_w_in': out['a_w_in'], 'a_pool_w': out['a_pool_w'], 'a_pool_scale': out['a_pool_scale'], 'a_w_out': out['a_w_out'], 'b_w_q': out['b_w_q'], 'b_w_out': out['b_w_out'], 'kv_w': out['kv_w'], 'f_b': out['f_b'], 'mem_w_kv': out['mem_w_kv'], 'ln1_g': out['ln1_g'], 'ln1_b': out['ln1_b'], 'ln2_g': out['ln2_g'], 'ln2_b': out['ln2_b'], 'ffn_w_up': out['ffn_w_up'], 'ffn_conv_w': out['ffn_conv_w'], 'ffn_conv_b': out['ffn_conv_b'], 'ffn_w_down': out['ffn_w_down'], 'loss_target': out['loss_target'], 'm_a_w_in': out['m_a_w_in'], 'm_a_pool_w': out['m_a_pool_w'], 'm_a_pool_scale': out['m_a_pool_scale'], 'm_a_w_out': out['m_a_w_out'], 'm_b_w_q': out['m_b_w_q'], 'm_b_w_out': out['m_b_w_out'], 'm_kv_w': out['m_kv_w'], 'm_f_b': out['m_f_b'], 'm_mem_w_kv': out['m_mem_w_kv'], 'm_ln1_g': out['m_ln1_g'], 'm_ln1_b': out['m_ln1_b'], 'm_ln2_g': out['m_ln2_g'], 'm_ln2_b': out['m_ln2_b'], 'm_ffn_w_up': out['m_ffn_w_up'], 'm_ffn_conv_w': out['m_ffn_conv_w'], 'm_ffn_conv_b': out['m_ffn_conv_b'], 'm_ffn_w_down': out['m_ffn_w_down'], 'v_a_w_in': out['v_a_w_in'], 'v_a_pool_w': out['v_a_pool_w'], 'v_a_pool_scale': out['v_a_pool_scale'], 'v_a_w_out': out['v_a_w_out'], 'v_b_w_q': out['v_b_w_q'], 'v_b_w_out': out['v_b_w_out'], 'v_kv_w': out['v_kv_w'], 'v_f_b': out['v_f_b'], 'v_mem_w_kv': out['v_mem_w_kv'], 'v_ln1_g': out['v_ln1_g'], 'v_ln1_b': out['v_ln1_b'], 'v_ln2_g': out['v_ln2_g'], 'v_ln2_b': out['v_ln2_b'], 'v_ffn_w_up': out['v_ffn_w_up'], 'v_ffn_conv_w': out['v_ffn_conv_w'], 'v_ffn_conv_b': out['v_ffn_conv_b'], 'v_ffn_w_down': out['v_ffn_w_down']}


def _loss(weights, diff, rest, loss_target):
    with _jax.named_scope("forward"):
        args = {**rest, TWIN_DIFF_INPUT: diff, **{k: w.astype(_WEIGHT_DTYPES[k]) for k, w in weights.items()}}
        y = _forward(args)
    with _jax.named_scope("loss_head"):
        err = _jnp.square(y.astype(_jnp.float32) - loss_target)
        return 0.5 * _jnp.sum(_jnp.mean(err, axis=-1)) if err.ndim else 0.5 * err


def _adamw(w, g, m, v):
    m = ADAM_B1 * m + (1.0 - ADAM_B1) * g
    v = ADAM_B2 * v + (1.0 - ADAM_B2) * _jnp.square(g)
    m_hat = m / (1.0 - ADAM_B1 ** ADAM_STEP)
    v_hat = v / (1.0 - ADAM_B2 ** ADAM_STEP)
    delta = -ADAM_LR * (m_hat / (_jnp.sqrt(v_hat) + ADAM_EPS) + ADAM_WD * w)
    return delta, m, v


def reference(x, mem, a_w_in, a_pool_w, a_pool_scale, a_w_out, b_w_q, b_w_out, kv_w, f_b, mem_w_kv, ln1_g, ln1_b, ln2_g, ln2_b, ffn_w_up, ffn_conv_w, ffn_conv_b, ffn_w_down, loss_target, m_a_w_in, m_a_pool_w, m_a_pool_scale, m_a_w_out, m_b_w_q, m_b_w_out, m_kv_w, m_f_b, m_mem_w_kv, m_ln1_g, m_ln1_b, m_ln2_g, m_ln2_b, m_ffn_w_up, m_ffn_conv_w, m_ffn_conv_b, m_ffn_w_down, v_a_w_in, v_a_pool_w, v_a_pool_scale, v_a_w_out, v_b_w_q, v_b_w_out, v_kv_w, v_f_b, v_mem_w_kv, v_ln1_g, v_ln1_b, v_ln2_g, v_ln2_b, v_ffn_w_up, v_ffn_conv_w, v_ffn_conv_b, v_ffn_w_down):
    given = dict(x=x, mem=mem, a_w_in=a_w_in, a_pool_w=a_pool_w, a_pool_scale=a_pool_scale, a_w_out=a_w_out, b_w_q=b_w_q, b_w_out=b_w_out, kv_w=kv_w, f_b=f_b, mem_w_kv=mem_w_kv, ln1_g=ln1_g, ln1_b=ln1_b, ln2_g=ln2_g, ln2_b=ln2_b, ffn_w_up=ffn_w_up, ffn_conv_w=ffn_conv_w, ffn_conv_b=ffn_conv_b, ffn_w_down=ffn_w_down, loss_target=loss_target, m_a_w_in=m_a_w_in, m_a_pool_w=m_a_pool_w, m_a_pool_scale=m_a_pool_scale, m_a_w_out=m_a_w_out, m_b_w_q=m_b_w_q, m_b_w_out=m_b_w_out, m_kv_w=m_kv_w, m_f_b=m_f_b, m_mem_w_kv=m_mem_w_kv, m_ln1_g=m_ln1_g, m_ln1_b=m_ln1_b, m_ln2_g=m_ln2_g, m_ln2_b=m_ln2_b, m_ffn_w_up=m_ffn_w_up, m_ffn_conv_w=m_ffn_conv_w, m_ffn_conv_b=m_ffn_conv_b, m_ffn_w_down=m_ffn_w_down, v_a_w_in=v_a_w_in, v_a_pool_w=v_a_pool_w, v_a_pool_scale=v_a_pool_scale, v_a_w_out=v_a_w_out, v_b_w_q=v_b_w_q, v_b_w_out=v_b_w_out, v_kv_w=v_kv_w, v_f_b=v_f_b, v_mem_w_kv=v_mem_w_kv, v_ln1_g=v_ln1_g, v_ln1_b=v_ln1_b, v_ln2_g=v_ln2_g, v_ln2_b=v_ln2_b, v_ffn_w_up=v_ffn_w_up, v_ffn_conv_w=v_ffn_conv_w, v_ffn_conv_b=v_ffn_conv_b, v_ffn_w_down=v_ffn_w_down)
    weights = {n: given[n] for n in TWIN_WEIGHTS}
    shared = {n: given[n] for n in SHARED_INPUTS}
    per_example = {n: given[n] for n in ['x', 'mem']}
    grad_fn = _jax.value_and_grad(_loss, argnums=(0, 1))

    def one_microbatch(ex, loss_target):
        ex = dict(ex)
        diff = ex.pop(TWIN_DIFF_INPUT)
        return grad_fn(weights, diff, {**shared, **ex}, loss_target)

    if N_MICROBATCH == 1:
        loss, (grad_w, grad_x) = one_microbatch(per_example, given["loss_target"])
    else:
        def body(carry, xs):
            loss_sum, grad_sum = carry
            l_k, (gw_k, gx_k) = one_microbatch(xs[0], xs[1])
            with _jax.named_scope("update"):
                return (loss_sum + l_k, _jax.tree.map(_jnp.add, grad_sum, gw_k)), gx_k

        init = (_jnp.zeros((), _jnp.float32), _jax.tree.map(_jnp.zeros_like, weights))
        (loss, grad_w), grad_x = _jax.lax.scan(body, init, (per_example, given["loss_target"]))
    with _jax.named_scope("update"):
        delta_w, new_m, new_v = {}, {}, {}
        for n in TWIN_WEIGHTS:
            delta_w[n], new_m[n], new_v[n] = _adamw(weights[n], grad_w[n], given["m_" + n], given["v_" + n])
    return (loss, grad_x, *[grad_w[n] for n in TWIN_WEIGHTS], *[delta_w[n] for n in TWIN_WEIGHTS],
            *[new_m[n] for n in TWIN_WEIGHTS], *[new_v[n] for n in TWIN_WEIGHTS])
```

```python
import functools

import jax
import jax.numpy as jnp
from jax import lax
from jax.experimental import pallas as pl
from jax.experimental.pallas import tpu as pltpu

F32 = jnp.float32
BF16 = jnp.bfloat16
SDS = jax.ShapeDtypeStruct

N_DEV = 8
D_MODEL = 1024
TOK_WIDTH = 768
MEM_WIDTH = 256
MEM_LEN = 256
MEM_HEADS = 4
HEAD_DIM = 64
FOX_HEADS = 12
POOL_GROUP = 192
D_FF = 2752
FF_BLOCK = 688
FF_BLOCK_PAD = 768
FF_PAIRS = 4
KV_COLS = 1548
KV_COLS_PAD = 1664
LANES = 128
DEPTH = 2
DN_ALPHA = (2.0 * DEPTH) ** 0.25
LN_EPS = 1e-5
QK_SCALE = HEAD_DIM ** -0.5
NEG_BIG = -1e30

ADAM_LR = 0.001
ADAM_B1 = 0.9
ADAM_B2 = 0.999
ADAM_EPS = 1e-08
ADAM_WD = 0.01
ADAM_STEP = 10

VMEM_LIMIT_BYTES = 56 * 1024 * 1024
TM = 512
TS = 256
TF = 256
TC = 256
HALO_POOL = 16
HALO_CONV = 8

NT_DIMS = (((1,), (1,)), ((), ()))
TN_DIMS = (((0,), (0,)), ((), ()))


def _params(sem=None):
    return pltpu.CompilerParams(dimension_semantics=sem, vmem_limit_bytes=VMEM_LIMIT_BYTES)


def _sigmoid(z):
    return 1.0 / (1.0 + jnp.exp(-z))


def _pick_tn(n):
    if n <= 2048:
        return n
    for t in (1024, 768, 512, 256, 128):
        if n % t == 0:
            return t
    return n


def mm_nn(a, b, out_dtype, name, addend=None, add_scale=1.0, also_bf16=False):
    m, k = a.shape
    _, n = b.shape
    tm = min(TM, m)
    tn = _pick_tn(n)
    tk = k if k <= 2048 else 1024
    nk = k // tk
    has_add = addend is not None

    def body(*refs):
        a_ref, b_ref = refs[0], refs[1]
        pos = 2
        c_ref = None
        if has_add:
            c_ref = refs[pos]
            pos += 1
        o_ref = refs[pos]
        ob_ref = refs[pos + 1] if also_bf16 else None
        acc = refs[-1]
        kk = pl.program_id(2)

        @pl.when(kk == 0)
        def _():
            acc[...] = jnp.zeros_like(acc)

        acc[...] += jnp.dot(a_ref[...].astype(BF16), b_ref[...].astype(BF16), preferred_element_type=F32)

        @pl.when(kk == nk - 1)
        def _():
            r = acc[...]
            if has_add:
                r = r + add_scale * c_ref[...]
            o_ref[...] = r.astype(out_dtype)
            if also_bf16:
                ob_ref[...] = r.astype(BF16)

    in_specs = [pl.BlockSpec((tm, tk), lambda i, j, kk: (i, kk)),
                pl.BlockSpec((tk, tn), lambda i, j, kk: (kk, j))]
    ops = [a, b]
    if has_add:
        in_specs.append(pl.BlockSpec((tm, tn), lambda i, j, kk: (i, j)))
        ops.append(addend)
    out_shape = [SDS((m, n), out_dtype)]
    out_specs = [pl.BlockSpec((tm, tn), lambda i, j, kk: (i, j))]
    if also_bf16:
        out_shape.append(SDS((m, n), BF16))
        out_specs.append(pl.BlockSpec((tm, tn), lambda i, j, kk: (i, j)))
    res = pl.pallas_call(
        body, name=name, grid=(m // tm, n // tn, nk), in_specs=in_specs, out_specs=out_specs, out_shape=out_shape,
        scratch_shapes=[pltpu.VMEM((tm, tn), F32)],
        compiler_params=_params(("parallel", "parallel", "arbitrary")))(*ops)
    return tuple(res) if also_bf16 else res[0]


def mm_tn(a, b, name, blocked=False):
    t, m = a.shape
    _, n = b.shape
    tt = min(TM, t)
    tm = 1024 if m % 1024 == 0 else m
    tn = FF_BLOCK_PAD if blocked else _pick_tn(n)
    nt = t // tt

    def body(a_ref, b_ref, o_ref):
        kk = pl.program_id(2)
        r = lax.dot_general(a_ref[...].astype(BF16), b_ref[...].astype(BF16), TN_DIMS, preferred_element_type=F32)
        if blocked:
            r = r[None]

        @pl.when(kk == 0)
        def _():
            o_ref[...] = r

        @pl.when(kk != 0)
        def _():
            o_ref[...] += r

    if blocked:
        out_shape = SDS((n // tn, m, tn), F32)
        out_spec = pl.BlockSpec((1, tm, tn), lambda i, j, kk: (j, i, 0))
    else:
        out_shape = SDS((m, n), F32)
        out_spec = pl.BlockSpec((tm, tn), lambda i, j, kk: (i, j))
    return pl.pallas_call(
        body, name=name, grid=(m // tm, n // tn, nt),
        in_specs=[pl.BlockSpec((tt, tm), lambda i, j, kk: (kk, i)), pl.BlockSpec((tt, tn), lambda i, j, kk: (kk, j))],
        out_specs=out_spec, out_shape=out_shape,
        compiler_params=_params(("parallel", "parallel", "arbitrary")))(a, b)


def ln_fwd(xprev, delta, g, b, name):
    t, d = xprev.shape
    tm = min(TM, t)

    def body(xp_ref, dl_ref, g_ref, b_ref, y_ref, yb_ref, xh_ref, rs_ref):
        r = DN_ALPHA * xp_ref[...] + dl_ref[...]
        mu = jnp.mean(r, axis=1, keepdims=True)
        xc = r - mu
        var = jnp.mean(xc * xc, axis=1, keepdims=True)
        rstd = lax.rsqrt(var + LN_EPS)
        xh = xc * rstd
        y = xh * g_ref[...] + b_ref[...]
        y_ref[...] = y
        yb_ref[...] = y.astype(BF16)
        xh_ref[...] = xh
        rs_ref[...] = jnp.broadcast_to(rstd, (tm, LANES))

    row = pl.BlockSpec((tm, d), lambda i: (i, 0))
    vec = pl.BlockSpec((1, d), lambda i: (0, 0))
    return pl.pallas_call(
        body, name=name, grid=(t // tm,), in_specs=[row, row, vec, vec],
        out_specs=[row, row, row, pl.BlockSpec((tm, LANES), lambda i: (i, 0))],
        out_shape=[SDS((t, d), F32), SDS((t, d), BF16), SDS((t, d), F32), SDS((t, LANES), F32)],
        compiler_params=_params(("parallel",)))(xprev, delta, g, b)


def ln_bwd(dy, xhat, rstd, g, name):
    t, d = dy.shape
    tm = min(TM, t)

    def body(dy_ref, xh_ref, rs_ref, g_ref, dr_ref, drb_ref, dg_ref, db_ref):
        i = pl.program_id(0)
        dyv = dy_ref[...]
        xh = xh_ref[...]
        dxh = dyv * g_ref[...]
        m1 = jnp.mean(dxh, axis=1, keepdims=True)
        m2 = jnp.mean(dxh * xh, axis=1, keepdims=True)
        dr = rs_ref[:, 0:1] * (dxh - m1 - xh * m2)
        dr_ref[...] = dr
        drb_ref[...] = dr.astype(BF16)

        @pl.when(i == 0)
        def _():
            dg_ref[...] = jnp.zeros_like(dg_ref)
            db_ref[...] = jnp.zeros_like(db_ref)

        dg_ref[...] += jnp.sum(dyv * xh, axis=0, keepdims=True)
        db_ref[...] += jnp.sum(dyv, axis=0, keepdims=True)

    row = pl.BlockSpec((tm, d), lambda i: (i, 0))
    vec = pl.BlockSpec((1, d), lambda i: (0, 0))
    return pl.pallas_call(
        body, name=name, grid=(t // tm,),
        in_specs=[row, row, pl.BlockSpec((tm, LANES), lambda i: (i, 0)), vec],
        out_specs=[row, row, vec, vec],
        out_shape=[SDS((t, d), F32), SDS((t, d), BF16), SDS((1, d), F32), SDS((1, d), F32)],
        compiler_params=_params(("arbitrary",)))(dy, xhat, rstd, g)


def loss_head(y, target):
    t, d = y.shape
    tm = min(TM, t)
    nsteps = t // tm

    def body(y_ref, t_ref, dy_ref, l_ref, acc):
        i = pl.program_id(0)
        diff = y_ref[...] - t_ref[...]
        dy_ref[...] = diff * (1.0 / d)

        @pl.when(i == 0)
        def _():
            acc[...] = jnp.zeros_like(acc)

        acc[...] += jnp.sum(diff * diff, axis=0, keepdims=True)

        @pl.when(i == nsteps - 1)
        def _():
            tot = jnp.sum(acc[...], axis=1, keepdims=True) * (0.5 / d)
            l_ref[...] = jnp.broadcast_to(tot, (1, LANES))

    row = pl.BlockSpec((tm, d), lambda i: (i, 0))
    return pl.pallas_call(
        body, name="loss_head", grid=(nsteps,), in_specs=[row, row],
        out_specs=[row, pl.BlockSpec((1, LANES), lambda i: (0, 0))],
        out_shape=[SDS((t, d), F32), SDS((1, LANES), F32)],
        scratch_shapes=[pltpu.VMEM((1, d), F32)],
        compiler_params=_params(("arbitrary",)))(y, target)


def memattn_fwd(proj, memkv, nb, s, name):
    ts = min(TS, s)
    nq = s // ts

    def body(q_ref, kv_ref, o_ref):
        for h in range(MEM_HEADS):
            lo, hi = h * HEAD_DIM, (h + 1) * HEAD_DIM
            qh = q_ref[:, lo:hi].astype(BF16)
            kh = kv_ref[:, lo:hi]
            vh = kv_ref[:, MEM_WIDTH + lo:MEM_WIDTH + hi]
            sc = lax.dot_general(qh, kh, NT_DIMS, preferred_element_type=F32) * QK_SCALE
            p = jnp.exp(sc - jnp.max(sc, axis=1, keepdims=True))
            p = p / jnp.sum(p, axis=1, keepdims=True)
            o_ref[:, lo:hi] = jnp.dot(p.astype(BF16), vh, preferred_element_type=F32).astype(BF16)

    return pl.pallas_call(
        body, name=name, grid=(nb, nq),
        in_specs=[pl.BlockSpec((ts, MEM_WIDTH), lambda b, i: (b * nq + i, 3)),
                  pl.BlockSpec((MEM_LEN, 2 * MEM_WIDTH), lambda b, i: (b, 0))],
        out_specs=pl.BlockSpec((ts, MEM_WIDTH), lambda b, i: (b * nq + i, 0)),
        out_shape=SDS((nb * s, MEM_WIDTH), BF16),
        compiler_params=_params(("parallel", "parallel")))(proj, memkv)


def memattn_bwd(proj, memkv, dcat, nb, s, name):
    ts = min(TS, s)
    nq = s // ts

    def body(q_ref, kv_ref, do_ref, dq_ref, dkv_ref):
        i = pl.program_id(1)

        @pl.when(i == 0)
        def _():
            dkv_ref[...] = jnp.zeros_like(dkv_ref)

        for h in range(MEM_HEADS):
            lo, hi = h * HEAD_DIM, (h + 1) * HEAD_DIM
            qh = q_ref[:, lo:hi].astype(BF16)
            kh = kv_ref[:, lo:hi]
            vh = kv_ref[:, MEM_WIDTH + lo:MEM_WIDTH + hi]
            doh = do_ref[:, lo:hi].astype(BF16)
            sc = lax.dot_general(qh, kh, NT_DIMS, preferred_element_type=F32) * QK_SCALE
            p = jnp.exp(sc - jnp.max(sc, axis=1, keepdims=True))
            p = p / jnp.sum(p, axis=1, keepdims=True)
            dv = lax.dot_general(p.astype(BF16), doh, TN_DIMS, preferred_element_type=F32)
            dp = lax.dot_general(doh, vh, NT_DIMS, preferred_element_type=F32)
            dl = jnp.sum(p * dp, axis=1, keepdims=True)
            ds = (p * (dp - dl) * QK_SCALE).astype(BF16)
            dq_ref[:, lo:hi] = jnp.dot(ds, kh, preferred_element_type=F32).astype(BF16)
            dkv_ref[:, lo:hi] += lax.dot_general(ds, qh, TN_DIMS, preferred_element_type=F32)
            dkv_ref[:, MEM_WIDTH + lo:MEM_WIDTH + hi] += dv

    return pl.pallas_call(
        body, name=name, grid=(nb, nq),
        in_specs=[pl.BlockSpec((ts, MEM_WIDTH), lambda b, i: (b * nq + i, 3)),
                  pl.BlockSpec((MEM_LEN, 2 * MEM_WIDTH), lambda b, i: (b, 0)),
                  pl.BlockSpec((ts, MEM_WIDTH), lambda b, i: (b * nq + i, 3))],
        out_specs=[pl.BlockSpec((ts, MEM_WIDTH), lambda b, i: (b * nq + i, 0)),
                   pl.BlockSpec((MEM_LEN, 2 * MEM_WIDTH), lambda b, i: (b, 0))],
        out_shape=[SDS((nb * s, MEM_WIDTH), BF16), SDS((nb * MEM_LEN, 2 * MEM_WIDTH), F32)],
        compiler_params=_params(("parallel", "arbitrary")))(proj, memkv, dcat)


def _pool_select(shape, s2, s4, s8, s16):
    lane = lax.broadcasted_iota(jnp.int32, shape, 1)
    return jnp.where(lane < POOL_GROUP, s2, jnp.where(lane < 2 * POOL_GROUP, s4, jnp.where(lane < 3 * POOL_GROUP, s8, s16)))


def _pool_count(shape, first_pos):
    pos = first_pos + lax.broadcasted_iota(jnp.int32, shape, 0)
    win = _pool_select(shape, 2, 4, 8, 16)
    return jnp.minimum(pos + 1, win).astype(F32)


def pool_fwd(proj, pw_bd, pscale, nb, s):
    ts = min(TS, s)
    nq = s // ts
    w = TOK_WIDTH

    def body(c_ref, h_ref, w_ref, sc_ref, pooled_ref, tok_ref):
        i = pl.program_id(0) % nq
        cur = c_ref[...]
        halo = jnp.where(i == 0, 0.0, h_ref[...])
        xe = jnp.concatenate([halo, cur], axis=0)
        s2 = xe + pltpu.roll(xe, 1, axis=0)
        s4 = s2 + pltpu.roll(s2, 2, axis=0)
        s8 = s4 + pltpu.roll(s4, 4, axis=0)
        s16 = s8 + pltpu.roll(s8, 8, axis=0)
        hp = HALO_POOL
        ws = _pool_select((ts, w), s2[hp:], s4[hp:], s8[hp:], s16[hp:])
        pooled = (ws / _pool_count((ts, w), i * ts) - cur).astype(BF16)
        pooled_ref[...] = pooled
        mixed = jnp.dot(pooled, w_ref[...], preferred_element_type=F32)
        tok_ref[...] = (mixed * sc_ref[...]).astype(BF16)

    row = pl.BlockSpec((ts, w), lambda r: (r, 0))
    return pl.pallas_call(
        body, name="pool_fwd", grid=(nb * nq,),
        in_specs=[row, pl.BlockSpec((HALO_POOL, w), lambda r: (jnp.maximum(r * (ts // HALO_POOL) - 1, 0), 0)),
                  pl.BlockSpec((w, w), lambda r: (0, 0)), pl.BlockSpec((1, w), lambda r: (0, 0))],
        out_specs=[row, row], out_shape=[SDS((nb * s, w), BF16), SDS((nb * s, w), BF16)],
        compiler_params=_params(("parallel",)))(proj, proj, pw_bd, pscale)


def pool_bwd_mix(dcat, pooled, pw_bd, pw_bd_t, pscale, nb, s):
    ts = min(TS, s)
    w = TOK_WIDTH

    def body(dt_ref, p_ref, w_ref, wt_ref, sc_ref, dm_ref, dp_ref, ds_ref):
        r = pl.program_id(0)
        dtok = dt_ref[...]
        mixed = jnp.dot(p_ref[...], w_ref[...], preferred_element_type=F32)

        @pl.when(r == 0)
        def _():
            ds_ref[...] = jnp.zeros_like(ds_ref)

        ds_ref[...] += jnp.sum(dtok * mixed, axis=0, keepdims=True)
        dmx = (dtok * sc_ref[...]).astype(BF16)
        dm_ref[...] = dmx
        dp_ref[...] = jnp.dot(dmx, wt_ref[...], preferred_element_type=F32)

    row = pl.BlockSpec((ts, w), lambda r: (r, 0))
    mat = pl.BlockSpec((w, w), lambda r: (0, 0))
    vec = pl.BlockSpec((1, w), lambda r: (0, 0))
    return pl.pallas_call(
        body, name="pool_bwd_mix", grid=(nb * s // ts,), in_specs=[row, row, mat, mat, vec],
        out_specs=[row, row, vec], out_shape=[SDS((nb * s, w), BF16), SDS((nb * s, w), F32), SDS((1, w), F32)],
        compiler_params=_params(("arbitrary",)))(dcat, pooled, pw_bd, pw_bd_t, pscale)


def pool_bwd_window(dpooled, nb, s):
    ts = min(TS, s)
    nq = s // ts
    w = TOK_WIDTH
    n_ext = ts + HALO_POOL
    n_halo_blocks = nb * s // HALO_POOL

    def body(c_ref, n_ref, du_ref):
        i = pl.program_id(0) % nq
        cur = c_ref[...]
        nxt = jnp.where(i == nq - 1, 0.0, n_ref[...])
        ze = jnp.concatenate([cur, nxt], axis=0) / _pool_count((n_ext, w), i * ts)
        s2 = ze + pltpu.roll(ze, n_ext - 1, axis=0)
        s4 = s2 + pltpu.roll(s2, n_ext - 2, axis=0)
        s8 = s4 + pltpu.roll(s4, n_ext - 4, axis=0)
        s16 = s8 + pltpu.roll(s8, n_ext - 8, axis=0)
        ws = _pool_select((ts, w), s2[:ts], s4[:ts], s8[:ts], s16[:ts])
        du_ref[...] = (ws - cur).astype(BF16)

    row = pl.BlockSpec((ts, w), lambda r: (r, 0))
    return pl.pallas_call(
        body, name="pool_bwd_window", grid=(nb * nq,),
        in_specs=[row, pl.BlockSpec((HALO_POOL, w),
                                    lambda r: (jnp.minimum((r + 1) * (ts // HALO_POOL), n_halo_blocks - 1), 0))],
        out_specs=row, out_shape=SDS((nb * s, w), BF16),
        compiler_params=_params(("parallel",)))(dpooled, dpooled)


def _conv_rows(xe, w_ref):
    return (w_ref[0, 2:3, :] * xe + w_ref[0, 1:2, :] * pltpu.roll(xe, 1, axis=0)
            + w_ref[0, 0:1, :] * pltpu.roll(xe, 2, axis=0) + w_ref[0, 3:4, :])


def convgate_fwd(h, cw, nb, s, name):
    ts = min(TS, s)
    nq = s // ts
    w = FF_BLOCK_PAD
    hc = HALO_CONV

    def body(uc_ref, uh_ref, gc_ref, gh_ref, wu_ref, wg_ref, o_ref):
        first = (pl.program_id(0) % nq) == 0
        xu = jnp.concatenate([jnp.where(first, 0.0, uh_ref[...]), uc_ref[...]], axis=0)
        xg = jnp.concatenate([jnp.where(first, 0.0, gh_ref[...]), gc_ref[...]], axis=0)
        cu = _conv_rows(xu, wu_ref)[hc:]
        cg = _conv_rows(xg, wg_ref)[hc:]
        o_ref[...] = (cg * _sigmoid(cg) * cu).astype(BF16)

    def cur(off):
        return pl.BlockSpec((ts, w), lambda r, j: (r, j + off))

    def halo(off):
        return pl.BlockSpec((hc, w), lambda r, j: (jnp.maximum(r * (ts // hc) - 1, 0), j + off))

    def wspec(off):
        return pl.BlockSpec((1, 8, w), lambda r, j: (j + off, 0, 0))

    return pl.pallas_call(
        body, name=name, grid=(nb * nq, FF_PAIRS),
        in_specs=[cur(0), halo(0), cur(FF_PAIRS), halo(FF_PAIRS), wspec(0), wspec(FF_PAIRS)],
        out_specs=pl.BlockSpec((ts, w), lambda r, j: (r, j)), out_shape=SDS((nb * s, FF_PAIRS * w), BF16),
        compiler_params=_params(("parallel", "parallel")))(h, h, h, h, cw, cw)


def convgate_bwd(h, dact, cw, nb, s, name):
    ts = min(TS, s)
    nq = s // ts
    w = FF_BLOCK_PAD
    hc = HALO_CONV
    n_ext = ts + hc
    n_halo_blocks = nb * s // hc

    def body(uc_ref, up_ref, un_ref, gc_ref, gp_ref, gn_ref, dc_ref, dn_ref, wu_ref, wg_ref,
             dhu_ref, dhg_ref, dwu_ref, dwg_ref):
        r = pl.program_id(1)
        i = r % nq
        first = i == 0
        last = i == nq - 1
        xu = jnp.concatenate([jnp.where(first, 0.0, up_ref[...]), uc_ref[...], un_ref[...]], axis=0)
        xg = jnp.concatenate([jnp.where(first, 0.0, gp_ref[...]), gc_ref[...], gn_ref[...]], axis=0)
        cu = _conv_rows(xu, wu_ref)[hc:]
        cg = _conv_rows(xg, wg_ref)[hc:]
        da = jnp.concatenate([dc_ref[...].astype(F32), jnp.where(last, 0.0, dn_ref[...].astype(F32)[:hc])], axis=0)
        sg = _sigmoid(cg)
        dcu = da * (cg * sg)
        dcg = da * cu * (sg * (1.0 + cg * (1.0 - sg)))

        def conv_t(dcv, w_ref):
            return (w_ref[0, 2:3, :] * dcv + w_ref[0, 1:2, :] * pltpu.roll(dcv, n_ext - 1, axis=0)
                    + w_ref[0, 0:1, :] * pltpu.roll(dcv, n_ext - 2, axis=0))[:ts]

        dhu_ref[...] = conv_t(dcu, wu_ref).astype(BF16)
        dhg_ref[...] = conv_t(dcg, wg_ref).astype(BF16)

        def tap_grads(xe, dcv):
            d0 = dcv[:ts]
            x0 = xe[hc:hc + ts]
            x1 = pltpu.roll(xe, 1, axis=0)[hc:hc + ts]
            x2 = pltpu.roll(xe, 2, axis=0)[hc:hc + ts]
            rows = [jnp.sum(d0 * x2, axis=0, keepdims=True), jnp.sum(d0 * x1, axis=0, keepdims=True),
                    jnp.sum(d0 * x0, axis=0, keepdims=True), jnp.sum(d0, axis=0, keepdims=True)]
            sub = lax.broadcasted_iota(jnp.int32, (8, w), 0)
            upd = jnp.zeros((8, w), F32)
            for k, rv in enumerate(rows):
                upd = jnp.where(sub == k, rv, upd)
            return upd[None]

        @pl.when(r == 0)
        def _():
            dwu_ref[...] = jnp.zeros_like(dwu_ref)
            dwg_ref[...] = jnp.zeros_like(dwg_ref)

        dwu_ref[...] += tap_grads(xu, dcu)
        dwg_ref[...] += tap_grads(xg, dcg)

    def cur(off):
        return pl.BlockSpec((ts, w), lambda j, r: (r, j + off))

    def prev(off):
        return pl.BlockSpec((hc, w), lambda j, r: (jnp.maximum(r * (ts // hc) - 1, 0), j + off))

    def nxt(off):
        return pl.BlockSpec((hc, w), lambda j, r: (jnp.minimum((r + 1) * (ts // hc), n_halo_blocks - 1), j + off))

    def wspec(off):
        return pl.BlockSpec((1, 8, w), lambda j, r: (j + off, 0, 0))

    hb = 2 * hc
    dact_next = pl.BlockSpec((hb, w), lambda j, r: (jnp.minimum((r + 1) * (ts // hb), nb * s // hb - 1), j))

    p = FF_PAIRS
    dh_spec = pl.BlockSpec((ts, w), lambda j, r: (r, j))
    dw_spec = pl.BlockSpec((1, 8, w), lambda j, r: (j, 0, 0))
    return pl.pallas_call(
        body, name=name, grid=(p, nb * nq),
        in_specs=[cur(0), prev(0), nxt(0), cur(p), prev(p), nxt(p), cur(0), dact_next, wspec(0), wspec(p)],
        out_specs=[dh_spec, dh_spec, dw_spec, dw_spec],
        out_shape=[SDS((nb * s, p * w), BF16), SDS((nb * s, p * w), BF16), SDS((p, 8, w), F32), SDS((p, 8, w), F32)],
        compiler_params=_params(("parallel", "arbitrary")))(h, h, h, h, h, h, dact, dact, cw, cw)


def _tri(n, upper):
    r = lax.broadcasted_iota(jnp.int32, (n, n), 0)
    c = lax.broadcasted_iota(jnp.int32, (n, n), 1)
    return ((r <= c) if upper else (r >= c)).astype(F32)


def fgate_fwd(fl, fb, nb, s):
    tc = min(TC, s)
    nq = s // tc

    def body(fl_ref, fb_ref, f_ref, carry):
        @pl.when(pl.program_id(1) == 0)
        def _():
            carry[...] = jnp.zeros_like(carry)

        z = fl_ref[...] + fb_ref[...]
        logf = jnp.minimum(z, 0.0) - jnp.log(1.0 + jnp.exp(-jnp.abs(z)))
        f_ref[...] = jnp.dot(_tri(tc, False), logf, preferred_element_type=F32,
                             precision=lax.Precision.HIGHEST) + carry[...]
        carry[...] += jnp.sum(logf, axis=0, keepdims=True)

    row = pl.BlockSpec((tc, LANES), lambda b, i: (b * nq + i, 0))
    return pl.pallas_call(
        body, name="fgate_fwd", grid=(nb, nq), in_specs=[row, pl.BlockSpec((1, LANES), lambda b, i: (0, 0))],
        out_specs=row, out_shape=SDS((nb * s, LANES), F32), scratch_shapes=[pltpu.VMEM((1, LANES), F32)],
        compiler_params=_params(("arbitrary", "arbitrary")))(fl, fb)


def fgate_bwd(d_cum_q, d_cum_k, fl, fb, nb, s):
    tc = min(TC, s)
    nq = s // tc

    def body(dfq_ref, dfk_ref, fl_ref, fb_ref, dfl_ref, dfb_ref, carry):
        b = pl.program_id(0)
        i = pl.program_id(1)

        @pl.when(i == 0)
        def _():
            carry[...] = jnp.zeros_like(carry)

        @pl.when(jnp.logical_and(b == 0, i == 0))
        def _():
            dfb_ref[...] = jnp.zeros_like(dfb_ref)

        dfv = dfq_ref[...] + dfk_ref[...]
        dlog = jnp.dot(_tri(tc, True), dfv, preferred_element_type=F32,
                       precision=lax.Precision.HIGHEST) + carry[...]
        carry[...] += jnp.sum(dfv, axis=0, keepdims=True)
        z = fl_ref[...] + fb_ref[...]
        dfl = dlog / (1.0 + jnp.exp(z))
        dfl_ref[...] = dfl
        dfb_ref[...] += jnp.sum(dfl, axis=0, keepdims=True)

    row = pl.BlockSpec((tc, LANES), lambda b, i: (b * nq + nq - 1 - i, 0))
    vec = pl.BlockSpec((1, LANES), lambda b, i: (0, 0))
    return pl.pallas_call(
        body, name="fgate_bwd", grid=(nb, nq), in_specs=[row, row, row, vec], out_specs=[row, vec],
        out_shape=[SDS((nb * s, LANES), F32), SDS((1, LANES), F32)], scratch_shapes=[pltpu.VMEM((1, LANES), F32)],
        compiler_params=_params(("arbitrary", "arbitrary")))(d_cum_q, d_cum_k, fl, fb)


def _lane_put(shape, h, col):
    lane = lax.broadcasted_iota(jnp.int32, shape, 1)
    return jnp.where(lane == h, col, 0.0)


def fox_fwd(pq, kv, fcum, fcum_t, nb, s):
    tf = min(TF, s)
    n = s // tf
    w = TOK_WIDTH

    def body(q_ref, k_ref, v_ref, f_ref, ft_ref, ob_ref, of_ref, lse_ref, m_scr, l_scr, acc_scr):
        i = pl.program_id(1)
        j = pl.program_id(2)

        @pl.when(j == 0)
        def _():
            m_scr[...] = jnp.full(m_scr.shape, NEG_BIG, F32)
            l_scr[...] = jnp.zeros_like(l_scr)
            acc_scr[...] = jnp.zeros_like(acc_scr)

        @pl.when(j <= i)
        def _():
            row = lax.broadcasted_iota(jnp.int32, (tf, tf), 0)
            col = lax.broadcasted_iota(jnp.int32, (tf, tf), 1)
            keep = jnp.logical_or(j < i, row >= col)
            for h in range(FOX_HEADS):
                lo, hi = h * HEAD_DIM, (h + 1) * HEAD_DIM
                sc = lax.dot_general(q_ref[:, lo:hi], k_ref[:, lo:hi], NT_DIMS, preferred_element_type=F32) * QK_SCALE
                sc = sc + f_ref[:, h:h + 1] - ft_ref[h:h + 1, :]
                sc = jnp.where(keep, sc, NEG_BIG)
                m_prev = m_scr[h]
                m_new = jnp.maximum(m_prev, jnp.max(sc, axis=1, keepdims=True))
                a = jnp.exp(m_prev - m_new)
                p = jnp.exp(sc - m_new)
                l_scr[h] = a * l_scr[h] + jnp.sum(p, axis=1, keepdims=True)
                acc_scr[:, lo:hi] = a * acc_scr[:, lo:hi] + jnp.dot(p.astype(BF16), v_ref[:, lo:hi],
                                                                   preferred_element_type=F32)
                m_scr[h] = m_new

        @pl.when(j == n - 1)
        def _():
            lse = jnp.zeros((tf, LANES), F32)
            for h in range(FOX_HEADS):
                lo, hi = h * HEAD_DIM, (h + 1) * HEAD_DIM
                o = acc_scr[:, lo:hi] / l_scr[h]
                ob_ref[:, lo:hi] = o.astype(BF16)
                of_ref[:, lo:hi] = o
                lse = lse + _lane_put((tf, LANES), h, m_scr[h] + jnp.log(l_scr[h]))
            lse_ref[...] = lse

    qrow = lambda b, i, j: (b * n + i, 0)
    return pl.pallas_call(
        body, name="fox_fwd", grid=(nb, n, n),
        in_specs=[pl.BlockSpec((tf, w), qrow),
                  pl.BlockSpec((tf, w), lambda b, i, j: (b * n + jnp.minimum(j, i), 0)),
                  pl.BlockSpec((tf, w), lambda b, i, j: (b * n + jnp.minimum(j, i), 1)),
                  pl.BlockSpec((tf, LANES), qrow),
                  pl.BlockSpec((16, tf), lambda b, i, j: (b, jnp.minimum(j, i)))],
        out_specs=[pl.BlockSpec((tf, w), qrow), pl.BlockSpec((tf, w), qrow), pl.BlockSpec((tf, LANES), qrow)],
        out_shape=[SDS((nb * s, w), BF16), SDS((nb * s, w), F32), SDS((nb * s, LANES), F32)],
        scratch_shapes=[pltpu.VMEM((FOX_HEADS, tf, 1), F32), pltpu.VMEM((FOX_HEADS, tf, 1), F32),
                        pltpu.VMEM((tf, w), F32)],
        compiler_params=_params(("parallel", "parallel", "arbitrary")))(pq, kv, kv, fcum, fcum_t)


def fox_delta(dcat, o, nb, s):
    tf = min(TM, s)
    w = TOK_WIDTH

    def body(do_ref, o_ref, dl_ref):
        out = jnp.zeros((tf, LANES), F32)
        for h in range(FOX_HEADS):
            lo, hi = h * HEAD_DIM, (h + 1) * HEAD_DIM
            out = out + _lane_put((tf, LANES), h, jnp.sum(do_ref[:, lo:hi] * o_ref[:, lo:hi], axis=1, keepdims=True))
        dl_ref[...] = out

    row = pl.BlockSpec((tf, w), lambda r: (r, 0))
    return pl.pallas_call(
        body, name="fox_delta", grid=(nb * s // tf,), in_specs=[row, row],
        out_specs=pl.BlockSpec((tf, LANES), lambda r: (r, 0)), out_shape=SDS((nb * s, LANES), F32),
        compiler_params=_params(("parallel",)))(dcat, o)


def fox_bwd_dq(pq, kv, fcum, fcum_t, dcat_bf, lse, delta, nb, s):
    tf = min(TF, s)
    n = s // tf
    w = TOK_WIDTH

    def body(q_ref, k_ref, v_ref, f_ref, ft_ref, do_ref, lse_ref, dl_ref, dq_ref, df_ref, acc_scr, df_scr):
        i = pl.program_id(1)
        j = pl.program_id(2)

        @pl.when(j == 0)
        def _():
            acc_scr[...] = jnp.zeros_like(acc_scr)
            df_scr[...] = jnp.zeros_like(df_scr)

        @pl.when(j <= i)
        def _():
            row = lax.broadcasted_iota(jnp.int32, (tf, tf), 0)
            col = lax.broadcasted_iota(jnp.int32, (tf, tf), 1)
            keep = jnp.logical_or(j < i, row >= col)
            dfacc = jnp.zeros((tf, LANES), F32)
            for h in range(FOX_HEADS):
                lo, hi = h * HEAD_DIM, (h + 1) * HEAD_DIM
                kh = k_ref[:, lo:hi]
                sc = lax.dot_general(q_ref[:, lo:hi], kh, NT_DIMS, preferred_element_type=F32) * QK_SCALE
                sc = sc + (f_ref[:, h:h + 1] - lse_ref[:, h:h + 1]) - ft_ref[h:h + 1, :]
                p = jnp.exp(jnp.where(keep, sc, NEG_BIG))
                dp = lax.dot_general(do_ref[:, lo:hi], v_ref[:, lo:hi], NT_DIMS, preferred_element_type=F32)
                ds = p * (dp - dl_ref[:, h:h + 1])
                dfacc = dfacc + _lane_put((tf, LANES), h, jnp.sum(ds, axis=1, keepdims=True))
                acc_scr[:, lo:hi] += jnp.dot(ds.astype(BF16), kh, preferred_element_type=F32)
            df_scr[...] += dfacc

        @pl.when(j == n - 1)
        def _():
            dq_ref[...] = (acc_scr[...] * QK_SCALE).astype(BF16)
            df_ref[...] = df_scr[...]

    qrow = lambda b, i, j: (b * n + i, 0)
    krow = lambda b, i, j: (b * n + jnp.minimum(j, i), 0)
    return pl.pallas_call(
        body, name="fox_bwd_dq", grid=(nb, n, n),
        in_specs=[pl.BlockSpec((tf, w), qrow), pl.BlockSpec((tf, w), krow),
                  pl.BlockSpec((tf, w), lambda b, i, j: (b * n + jnp.minimum(j, i), 1)),
                  pl.BlockSpec((tf, LANES), qrow),
                  pl.BlockSpec((16, tf), lambda b, i, j: (b, jnp.minimum(j, i))),
                  pl.BlockSpec((tf, w), qrow), pl.BlockSpec((tf, LANES), qrow), pl.BlockSpec((tf, LANES), qrow)],
        out_specs=[pl.BlockSpec((tf, w), qrow), pl.BlockSpec((tf, LANES), qrow)],
        out_shape=[SDS((nb * s, w), BF16), SDS((nb * s, LANES), F32)],
        scratch_shapes=[pltpu.VMEM((tf, w), F32), pltpu.VMEM((tf, LANES), F32)],
        compiler_params=_params(("parallel", "parallel", "arbitrary")))(pq, kv, kv, fcum, fcum_t, dcat_bf, lse, delta)


def fox_bwd_dkv(pq, kv, fcum, fcum_t, dcat_bf, lse_t, delta_t, nb, s):
    tf = min(TF, s)
    n = s // tf
    w = TOK_WIDTH

    def body(q_ref, k_ref, v_ref, f_ref, ft_ref, do_ref, lse_ref, dl_ref, dk_ref, dv_ref, df_ref, dk_scr, dv_scr, df_scr):
        j = pl.program_id(1)
        i = pl.program_id(2)

        @pl.when(i == 0)
        def _():
            dk_scr[...] = jnp.zeros_like(dk_scr)
            dv_scr[...] = jnp.zeros_like(dv_scr)
            df_scr[...] = jnp.zeros_like(df_scr)

        @pl.when(i >= j)
        def _():
            krow = lax.broadcasted_iota(jnp.int32, (tf, tf), 0)
            qcol = lax.broadcasted_iota(jnp.int32, (tf, tf), 1)
            keep = jnp.logical_or(i > j, qcol >= krow)
            dfacc = jnp.zeros((tf, LANES), F32)
            for h in range(FOX_HEADS):
                lo, hi = h * HEAD_DIM, (h + 1) * HEAD_DIM
                qh = q_ref[:, lo:hi]
                doh = do_ref[:, lo:hi]
                sc = lax.dot_general(k_ref[:, lo:hi], qh, NT_DIMS, preferred_element_type=F32) * QK_SCALE
                sc = sc + (ft_ref[h:h + 1, :] - lse_ref[h:h + 1, :]) - f_ref[:, h:h + 1]
                p = jnp.exp(jnp.where(keep, sc, NEG_BIG))
                dv_scr[:, lo:hi] += jnp.dot(p.astype(BF16), doh, preferred_element_type=F32)
                dp = lax.dot_general(v_ref[:, lo:hi], doh, NT_DIMS, preferred_element_type=F32)
                ds = p * (dp - dl_ref[h:h + 1, :])
                dk_scr[:, lo:hi] += jnp.dot(ds.astype(BF16), qh, preferred_element_type=F32)
                dfacc = dfacc - _lane_put((tf, LANES), h, jnp.sum(ds, axis=1, keepdims=True))
            df_scr[...] += dfacc

        @pl.when(i == n - 1)
        def _():
            dk_ref[...] = (dk_scr[...] * QK_SCALE).astype(BF16)
            dv_ref[...] = dv_scr[...].astype(BF16)
            df_ref[...] = df_scr[...]

    krow_map = lambda b, j, i: (b * n + j, 0)
    qrow_map = lambda b, j, i: (b * n + jnp.maximum(i, j), 0)
    qcol_map = lambda b, j, i: (b, jnp.maximum(i, j))
    return pl.pallas_call(
        body, name="fox_bwd_dkv", grid=(nb, n, n),
        in_specs=[pl.BlockSpec((tf, w), qrow_map), pl.BlockSpec((tf, w), krow_map),
                  pl.BlockSpec((tf, w), lambda b, j, i: (b * n + j, 1)),
                  pl.BlockSpec((tf, LANES), krow_map), pl.BlockSpec((16, tf), qcol_map),
                  pl.BlockSpec((tf, w), qrow_map), pl.BlockSpec((16, tf), qcol_map), pl.BlockSpec((16, tf), qcol_map)],
        out_specs=[pl.BlockSpec((tf, w), krow_map), pl.BlockSpec((tf, w), krow_map), pl.BlockSpec((tf, LANES), krow_map)],
        out_shape=[SDS((nb * s, w), BF16), SDS((nb * s, w), BF16), SDS((nb * s, LANES), F32)],
        scratch_shapes=[pltpu.VMEM((tf, w), F32), pltpu.VMEM((tf, w), F32), pltpu.VMEM((tf, LANES), F32)],
        compiler_params=_params(("parallel", "parallel", "arbitrary")))(pq, kv, kv, fcum, fcum_t, dcat_bf, lse_t, delta_t)


def reduce_adamw(parts, w, m, v, name):
    _, r, c = parts.shape
    tr = r
    for cand in range(8, r + 1, 8):
        if r % cand == 0 and cand * c <= 128 * 1024:
            tr = cand
    if r * c <= 128 * 1024:
        tr = r
    c1 = 1.0 - ADAM_B1 ** ADAM_STEP
    c2 = 1.0 - ADAM_B2 ** ADAM_STEP

    def body(p_ref, w_ref, m_ref, v_ref, g_out, d_out, m_out, v_out):
        g = p_ref[0]
        for k in range(1, N_DEV):
            g = g + p_ref[k]
        mn = ADAM_B1 * m_ref[...] + (1.0 - ADAM_B1) * g
        vn = ADAM_B2 * v_ref[...] + (1.0 - ADAM_B2) * (g * g)
        g_out[...] = g
        m_out[...] = mn
        v_out[...] = vn
        d_out[...] = -ADAM_LR * ((mn / c1) / (jnp.sqrt(vn / c2) + ADAM_EPS) + ADAM_WD * w_ref[...])

    row = pl.BlockSpec((tr, c), lambda i: (i, 0))
    return pl.pallas_call(
        body, name=name, grid=(r // tr,),
        in_specs=[pl.BlockSpec((N_DEV, tr, c), lambda i: (0, i, 0)), row, row, row],
        out_specs=[row, row, row, row], out_shape=[SDS((r, c), F32)] * 4,
        compiler_params=_params(("parallel",)))(parts, w, m, v)


def exchange(arrs, scatter, name):
    na = len(arrs)
    npeer = N_DEV - 1

    def body(*refs):
        ins = refs[:na]
        outs = refs[na:2 * na]
        send_sems, recv_sems, local_sems = refs[2 * na:]
        x, y, c = lax.axis_index("x"), lax.axis_index("y"), lax.axis_index("c")
        me = 4 * x + 2 * y + c
        peers = []
        for k in range(1, N_DEV):
            px = 1 - x if (k >> 2) & 1 else x
            py = 1 - y if (k >> 1) & 1 else y
            pc = 1 - c if k & 1 else c
            peers.append(((px, py, pc), 4 * px + 2 * py + pc))
        local = []
        remote = []
        for a in range(na):
            src_me = ins[a].at[me] if scatter else ins[a]
            lc = pltpu.make_async_copy(src_me, outs[a].at[me], local_sems.at[a])
            lc.start()
            local.append(lc)
            for k, (dev, idx) in enumerate(peers):
                cp = pltpu.make_async_remote_copy(
                    src_ref=ins[a].at[idx] if scatter else ins[a], dst_ref=outs[a].at[me],
                    send_sem=send_sems.at[a * npeer + k], recv_sem=recv_sems.at[a * npeer + k],
                    device_id=dev, device_id_type=pl.DeviceIdType.MESH)
                cp.start()
                remote.append(cp)
        for a in range(na):
            for k, (dev, idx) in enumerate(peers):
                src_me = ins[a].at[me] if scatter else ins[a]
                arrival = pltpu.make_async_remote_copy(
                    src_ref=src_me, dst_ref=outs[a].at[idx],
                    send_sem=send_sems.at[a * npeer + k], recv_sem=recv_sems.at[a * npeer + k],
                    device_id=dev, device_id_type=pl.DeviceIdType.MESH)
                arrival.wait_recv()
        for cp in remote:
            cp.wait_send()
        for lc in local:
            lc.wait()

    hbm = pl.BlockSpec(memory_space=pltpu.HBM)
    out_shape = [SDS(((a.shape[1:] if scatter else a.shape)), a.dtype) for a in arrs]
    out_shape = [SDS((N_DEV,) + s_.shape, s_.dtype) for s_ in out_shape]
    return pl.pallas_call(
        body, name=name, in_specs=[hbm] * na, out_specs=[hbm] * na, out_shape=out_shape,
        scratch_shapes=[pltpu.SemaphoreType.DMA((na * npeer,)), pltpu.SemaphoreType.DMA((na * npeer,)),
                        pltpu.SemaphoreType.DMA((na,))],
        )(*arrs)


def _block_diag(pw):
    out = jnp.zeros((TOK_WIDTH, TOK_WIDTH), pw.dtype)
    for g in range(4):
        out = lax.dynamic_update_slice(out, pw[g], (g * POOL_GROUP, g * POOL_GROUP))
    return out


def local_step(x, mem, target, wts, nb, s):
    g = {}
    saved = []
    mem_bf = mem.astype(BF16)
    xin = x
    xin_bf = x.astype(BF16)
    for l in range(DEPTH):
        sv = {"xin_bf": xin_bf}
        memkv = mm_nn(mem_bf, wts["memw"][l], BF16, f"memkv{l}")
        sv["memkv"] = memkv
        if l == 0:
            proj = mm_nn(xin_bf, wts["win_a"], F32, "proj_a")
            pooled, tok = pool_fwd(proj, wts["pw_bd"], wts["pscale"], nb, s)
            sv["pooled"] = pooled
        else:
            kv = mm_nn(xin_bf, wts["kvw"][:, :2 * TOK_WIDTH], BF16, "kv_proj")
            fl = mm_nn(xin_bf, wts["kvw"][:, 2 * TOK_WIDTH:], F32, "gate_proj")
            fcum = fgate_fwd(fl, wts["fb"], nb, s)
            fcum_t = _to_head_rows(fcum, nb, s)
            proj = mm_nn(xin_bf, wts["wq"], BF16, "proj_b")
            tok, o_f32, lse = fox_fwd(proj, kv, fcum, fcum_t, nb, s)
            sv.update(kv=kv, fl=fl, fcum=fcum, fcum_t=fcum_t, o_f32=o_f32, lse=lse)
        sv["proj"] = proj
        mem_out = memattn_fwd(proj, memkv, nb, s, f"memattn_fwd{l}")
        cat = jnp.concatenate([tok, mem_out], axis=1)
        sv["cat"] = cat
        mix = mm_nn(cat, wts["wout"][l], F32, f"out_proj{l}")
        x1, x1_bf, xh1, rs1 = ln_fwd(xin, mix, wts["ln1_g"][l], wts["ln1_b"][l], f"ln1_fwd{l}")
        sv.update(x1_bf=x1_bf, xh1=xh1, rs1=rs1)
        h = mm_nn(x1_bf, wts["wup"][l], F32, f"ffn_up{l}")
        act = convgate_fwd(h, wts["cw"][l], nb, s, f"convgate_fwd{l}")
        sv.update(h=h, act=act)
        ffn = mm_nn(act, wts["wdown"][l], F32, f"ffn_down{l}")
        x2, x2_bf, xh2, rs2 = ln_fwd(x1, ffn, wts["ln2_g"][l], wts["ln2_b"][l], f"ln2_fwd{l}")
        sv.update(xh2=xh2, rs2=rs2)
        saved.append(sv)
        xin, xin_bf = x2, x2_bf

    dy, loss_row = loss_head(xin, target)
    loss = loss_row[0, 0]

    for l in reversed(range(DEPTH)):
        sv = saved[l]
        dr2, dr2_bf, dg2, db2 = ln_bwd(dy, sv["xh2"], sv["rs2"], wts["ln2_g"][l], f"ln2_bwd{l}")
        g[f"ln2_g{l}"], g[f"ln2_b{l}"] = dg2, db2
        dact = mm_nn(dr2_bf, wts["wdown_t"][l], BF16, f"ffn_down_dx{l}")
        g[f"wdown{l}"] = mm_tn(sv["act"], dr2_bf, f"ffn_down_dw{l}")
        dh_u, dh_g, dcw_u, dcw_g = convgate_bwd(sv["h"], dact, wts["cw"][l], nb, s, f"convgate_bwd{l}")
        g[f"cw{l}"] = jnp.concatenate([dcw_u, dcw_g], axis=0)
        half = FF_PAIRS * FF_BLOCK_PAD
        dx1 = mm_nn(dh_u, wts["wup_t"][l][:half], F32, f"ffn_up_dx_u{l}", addend=dr2, add_scale=DN_ALPHA)
        dx1 = mm_nn(dh_g, wts["wup_t"][l][half:], F32, f"ffn_up_dx_g{l}", addend=dx1)
        g[f"wup{l}"] = jnp.concatenate([mm_tn(sv["x1_bf"], dh_u, f"ffn_up_dw_u{l}", blocked=True),
                                        mm_tn(sv["x1_bf"], dh_g, f"ffn_up_dw_g{l}", blocked=True)], axis=0)
        dr1, dr1_bf, dg1, db1 = ln_bwd(dx1, sv["xh1"], sv["rs1"], wts["ln1_g"][l], f"ln1_bwd{l}")
        g[f"ln1_g{l}"], g[f"ln1_b{l}"] = dg1, db1
        dcat, dcat_bf = mm_nn(dr1_bf, wts["wout_t"][l], F32, f"out_proj_dx{l}", also_bf16=True)
        g[f"wout{l}"] = mm_tn(sv["cat"], dr1_bf, f"out_proj_dw{l}")
        dqm, dmemkv = memattn_bwd(sv["proj"], sv["memkv"], dcat, nb, s, f"memattn_bwd{l}")
        g[f"memw{l}"] = mm_tn(mem_bf, dmemkv, f"memkv_dw{l}")
        if l == 0:
            dmixed, dpooled, dscale = pool_bwd_mix(dcat, sv["pooled"], wts["pw_bd"], wts["pw_bd_t"], wts["pscale"], nb, s)
            g["pscale"] = dscale
            g["pw_full"] = mm_tn(sv["pooled"], dmixed, "pool_dw")
            du = pool_bwd_window(dpooled, nb, s)
            dproj = jnp.concatenate([du, dqm], axis=1)
            dy = mm_nn(dproj, wts["win_a_t"], F32, "proj_a_dx", addend=dr1, add_scale=DN_ALPHA)
            g["win_a"] = mm_tn(sv["xin_bf"], dproj, "proj_a_dw")
        else:
            delta = fox_delta(dcat, sv["o_f32"], nb, s)
            dq, dfcum_q = fox_bwd_dq(sv["proj"], sv["kv"], sv["fcum"], sv["fcum_t"], dcat_bf, sv["lse"], delta, nb, s)
            dk, dv, dfcum_k = fox_bwd_dkv(sv["proj"], sv["kv"], sv["fcum"], sv["fcum_t"], dcat_bf,
                                          _to_head_rows(sv["lse"], nb, s), _to_head_rows(delta, nb, s), nb, s)
            dfl, dfb = fgate_bwd(dfcum_q, dfcum_k, sv["fl"], wts["fb"], nb, s)
            g["fb"] = dfb
            dproj = jnp.concatenate([dq, dqm], axis=1)
            dkvf = jnp.concatenate([dk, dv, dfl.astype(BF16)], axis=1)
            dy = mm_nn(dproj, wts["wq_t"], F32, "proj_b_dx", addend=dr1, add_scale=DN_ALPHA)
            dy = mm_nn(dkvf, wts["kvw_t"], F32, "kv_proj_dx", addend=dy)
            g["wq"] = mm_tn(sv["xin_bf"], dproj, "proj_b_dw")
            g["kvw"] = mm_tn(sv["xin_bf"], dkvf, "kv_proj_dw")
    return loss, dy, g


def _to_head_rows(a, nb, s):
    return a.reshape(nb, s, LANES)[:, :, :16].transpose(0, 2, 1).reshape(nb * 16, s)


REP_ROWS = 168


def pack_replicated(pool_w, ln1_g, ln1_b, ln2_g, ln2_b, conv_b, f_b):
    cb = jnp.pad(conv_b, ((0, 0), (0, 6144 - 5504))).reshape(12, D_MODEL)
    fb = jnp.pad(f_b.reshape(1, FOX_HEADS), ((0, 3), (0, D_MODEL - FOX_HEADS)))
    return jnp.concatenate([pool_w.reshape(144, D_MODEL), ln1_g, ln1_b, ln2_g, ln2_b, cb, fb], axis=0)


def unpack_replicated(buf):
    pool_w = buf[:144].reshape(1, 4, POOL_GROUP, POOL_GROUP)
    ln = [buf[144 + 2 * k:146 + 2 * k] for k in range(4)]
    conv_b = buf[152:164].reshape(2, 6144)[:, :5504]
    f_b = buf[164, :FOX_HEADS]
    return pool_w, ln[0], ln[1], ln[2], ln[3], conv_b, f_b


def prepare_weights(gath, a_pool_w, f_b, ln1_g, ln1_b, ln2_g, ln2_b, ffn_conv_b):
    sq = gath["sq"]
    full = [sq[:, k].reshape(D_MODEL, D_MODEL) for k in range(4)]
    w = {"win_a": full[0], "win_a_t": full[0].T, "wq": full[2], "wq_t": full[2].T,
         "wout": [full[1], full[3]], "wout_t": [full[1].T, full[3].T]}
    kvw = gath["kvw"].reshape(D_MODEL, KV_COLS_PAD)
    w["kvw"], w["kvw_t"] = kvw, kvw.T
    memw = gath["memw"]
    w["memw"] = [memw[:, l].reshape(D_MODEL, 2 * MEM_WIDTH) for l in range(DEPTH)]
    pad_c = FF_BLOCK_PAD - FF_BLOCK
    wup = jnp.pad(gath["wup"], ((0, 0), (0, 0), (0, 0), (0, pad_c)))
    w["wup"] = [wup[:, l].transpose(1, 0, 2).reshape(D_MODEL, N_DEV * FF_BLOCK_PAD) for l in range(DEPTH)]
    w["wup_t"] = [m_.T for m_ in w["wup"]]
    wd = gath["wdown"]
    w["wdown"] = []
    for l in range(DEPTH):
        blocks = wd[:, l].reshape(FF_PAIRS, FF_BLOCK, D_MODEL)
        w["wdown"].append(jnp.pad(blocks, ((0, 0), (0, pad_c), (0, 0))).reshape(FF_PAIRS * FF_BLOCK_PAD, D_MODEL))
    w["wdown_t"] = [m_.T for m_ in w["wdown"]]
    small = gath["small"]
    cb = jnp.pad(ffn_conv_b.reshape(DEPTH, N_DEV, FF_BLOCK), ((0, 0), (0, 0), (0, pad_c)))
    w["cw"] = []
    for l in range(DEPTH):
        taps = small[:, 3 * l:3 * l + 3, :]
        w["cw"].append(jnp.concatenate([taps, cb[l][:, None, :], jnp.zeros((N_DEV, 4, FF_BLOCK_PAD), F32)], axis=1))
    w["pscale"] = small[:, 8, :96].reshape(1, TOK_WIDTH)
    pw_bd = _block_diag(a_pool_w[0])
    w["pw_bd"], w["pw_bd_t"] = pw_bd.astype(BF16), pw_bd.T.astype(BF16)
    w["fb"] = jnp.pad(f_b.reshape(1, FOX_HEADS), ((0, 0), (0, LANES - FOX_HEADS)))
    w["ln1_g"] = [ln1_g[l:l + 1] for l in range(DEPTH)]
    w["ln1_b"] = [ln1_b[l:l + 1] for l in range(DEPTH)]
    w["ln2_g"] = [ln2_g[l:l + 1] for l in range(DEPTH)]
    w["ln2_b"] = [ln2_b[l:l + 1] for l in range(DEPTH)]
    return w


def shard_weights_for_gather(a_w_in, a_w_out, b_w_q, b_w_out, kv_w, mem_w_kv, ffn_w_up, ffn_w_down, ffn_conv_w, a_pool_scale):
    sq = jnp.stack([a_w_in[0], a_w_out[0], b_w_q[0], b_w_out[0]], axis=0).astype(BF16)
    kvw = jnp.pad(kv_w, ((0, 0), (0, KV_COLS_PAD - KV_COLS))).astype(BF16)
    small = jnp.zeros((16, FF_BLOCK_PAD), F32)
    small = lax.dynamic_update_slice(small, ffn_conv_w.reshape(6, FF_BLOCK), (0, 0))
    small = lax.dynamic_update_slice(small, a_pool_scale, (8, 0))
    return {"sq": sq, "kvw": kvw, "memw": mem_w_kv.astype(BF16), "wup": ffn_w_up.astype(BF16),
            "wdown": ffn_w_down.astype(BF16), "small": small}


def grads_to_owner_blocks(g):
    out = {}
    out["sq"] = jnp.stack([g["win_a"], g["wout0"], g["wq"], g["wout1"]], axis=0).reshape(4, N_DEV, 128, D_MODEL).transpose(1, 0, 2, 3)
    out["kvw"] = g["kvw"][:, :KV_COLS].reshape(N_DEV, 128, KV_COLS)
    out["memw"] = jnp.stack([g["memw0"], g["memw1"]], axis=0).reshape(DEPTH, N_DEV, 128, 2 * MEM_WIDTH).transpose(1, 0, 2, 3).reshape(N_DEV, 256, 2 * MEM_WIDTH)
    out["wup"] = jnp.stack([g["wup0"][:, :, :FF_BLOCK], g["wup1"][:, :, :FF_BLOCK]], axis=1).reshape(N_DEV, DEPTH * D_MODEL, FF_BLOCK)
    wd = [g[f"wdown{l}"].reshape(FF_PAIRS, FF_BLOCK_PAD, D_MODEL)[:, :FF_BLOCK].reshape(N_DEV, 344, D_MODEL) for l in range(DEPTH)]
    out["wdown"] = jnp.stack(wd, axis=1).reshape(N_DEV, DEPTH * 344, D_MODEL)
    taps = jnp.stack([g[f"cw{l}"][:, :3, :] for l in range(DEPTH)], axis=1).reshape(N_DEV, 6, FF_BLOCK_PAD)
    small = jnp.zeros((N_DEV, 16, FF_BLOCK_PAD), F32)
    small = lax.dynamic_update_slice(small, taps, (0, 0, 0))
    small = lax.dynamic_update_slice(small, g["pscale"].reshape(N_DEV, 1, 96), (0, 8, 0))
    out["small"] = small
    return out


def replicated_grads(g):
    pw = jnp.stack([g["pw_full"][k * POOL_GROUP:(k + 1) * POOL_GROUP, k * POOL_GROUP:(k + 1) * POOL_GROUP] for k in range(4)])
    conv_b = jnp.stack([g[f"cw{l}"][:, 3, :FF_BLOCK].reshape(N_DEV * FF_BLOCK) for l in range(DEPTH)])
    ln = [jnp.concatenate([g[f"{n}{l}"] for l in range(DEPTH)], axis=0) for n in ("ln1_g", "ln1_b", "ln2_g", "ln2_b")]
    return pack_replicated(pw[None], ln[0], ln[1], ln[2], ln[3], conv_b, g["fb"][0, :FOX_HEADS])


def kernel(x, mem, a_w_in, a_pool_w, a_pool_scale, a_w_out, b_w_q, b_w_out, kv_w, f_b, mem_w_kv, ln1_g, ln1_b, ln2_g, ln2_b, ffn_w_up, ffn_conv_w, ffn_conv_b, ffn_w_down, loss_target, m_a_w_in, m_a_pool_w, m_a_pool_scale, m_a_w_out, m_b_w_q, m_b_w_out, m_kv_w, m_f_b, m_mem_w_kv, m_ln1_g, m_ln1_b, m_ln2_g, m_ln2_b, m_ffn_w_up, m_ffn_conv_w, m_ffn_conv_b, m_ffn_w_down, v_a_w_in, v_a_pool_w, v_a_pool_scale, v_a_w_out, v_b_w_q, v_b_w_out, v_kv_w, v_f_b, v_mem_w_kv, v_ln1_g, v_ln1_b, v_ln2_g, v_ln2_b, v_ffn_w_up, v_ffn_conv_w, v_ffn_conv_b, v_ffn_w_down):
    nb, s, d = x.shape
    t = nb * s

    shards = shard_weights_for_gather(a_w_in, a_w_out, b_w_q, b_w_out, kv_w, mem_w_kv, ffn_w_up, ffn_w_down,
                                      ffn_conv_w, a_pool_scale)
    names = ["sq", "kvw", "memw", "wup", "wdown", "small"]
    gathered = exchange([shards[n] for n in names], False, "gather_weights")
    wts = prepare_weights(dict(zip(names, gathered)), a_pool_w, f_b, ln1_g, ln1_b, ln2_g, ln2_b, ffn_conv_b)

    loss_local, grad_x, g = local_step(x.reshape(t, d), mem.reshape(nb * MEM_LEN, d), loss_target.reshape(t, d), wts, nb, s)
    loss = lax.psum(loss_local, ("x", "y", "c"))

    blocks = grads_to_owner_blocks(g)
    parts = dict(zip(names, exchange([blocks[n] for n in names], True, "scatter_grads")))
    rep_parts = exchange([replicated_grads(g)], False, "gather_small_grads")[0]

    def upd(pname, w2, m2, v2, kname):
        return reduce_adamw(parts[pname] if isinstance(pname, str) else pname, w2, m2, v2, kname)

    res = {}
    sq_w = [a_w_in, a_w_out, b_w_q, b_w_out]
    sq_m = [m_a_w_in, m_a_w_out, m_b_w_q, m_b_w_out]
    sq_v = [v_a_w_in, v_a_w_out, v_b_w_q, v_b_w_out]
    for k, nm in enumerate(["a_w_in", "a_w_out", "b_w_q", "b_w_out"]):
        outs = upd(parts["sq"][:, k], sq_w[k][0], sq_m[k][0], sq_v[k][0], f"adamw_{nm}")
        res[nm] = [o[None] for o in outs]
    res["kv_w"] = list(upd("kvw", kv_w, m_kv_w, v_kv_w, "adamw_kv_w"))
    outs = upd("memw", mem_w_kv.reshape(256, 512), m_mem_w_kv.reshape(256, 512), v_mem_w_kv.reshape(256, 512), "adamw_mem_w_kv")
    res["mem_w_kv"] = [o.reshape(DEPTH, 128, 512) for o in outs]
    r2 = DEPTH * D_MODEL
    outs = upd("wup", ffn_w_up.reshape(r2, FF_BLOCK), m_ffn_w_up.reshape(r2, FF_BLOCK), v_ffn_w_up.reshape(r2, FF_BLOCK), "adamw_ffn_w_up")
    res["ffn_w_up"] = [o.reshape(DEPTH, D_MODEL, FF_BLOCK) for o in outs]
    r3 = DEPTH * 344
    outs = upd("wdown", ffn_w_down.reshape(r3, d), m_ffn_w_down.reshape(r3, d), v_ffn_w_down.reshape(r3, d), "adamw_ffn_w_down")
    res["ffn_w_down"] = [o.reshape(DEPTH, 344, d) for o in outs]

    def small_pack(conv_w_, pscale_):
        buf = jnp.zeros((16, FF_BLOCK_PAD), F32)
        buf = lax.dynamic_update_slice(buf, conv_w_.reshape(6, FF_BLOCK), (0, 0))
        return lax.dynamic_update_slice(buf, pscale_, (8, 0))

    outs = upd("small", small_pack(ffn_conv_w, a_pool_scale), small_pack(m_ffn_conv_w, m_a_pool_scale),
               small_pack(v_ffn_conv_w, v_a_pool_scale), "adamw_small")
    res["ffn_conv_w"] = [o[:6, :FF_BLOCK].reshape(DEPTH, 3, FF_BLOCK) for o in outs]
    res["a_pool_scale"] = [o[8:9, :96] for o in outs]

    outs = upd(rep_parts, pack_replicated(a_pool_w, ln1_g, ln1_b, ln2_g, ln2_b, ffn_conv_b, f_b),
               pack_replicated(m_a_pool_w, m_ln1_g, m_ln1_b, m_ln2_g, m_ln2_b, m_ffn_conv_b, m_f_b),
               pack_replicated(v_a_pool_w, v_ln1_g, v_ln1_b, v_ln2_g, v_ln2_b, v_ffn_conv_b, v_f_b), "adamw_replicated")
    rep_names = ["a_pool_w", "ln1_g", "ln1_b", "ln2_g", "ln2_b", "ffn_conv_b", "f_b"]
    for nm in rep_names:
        res[nm] = []
    for o in outs:
        for nm, val in zip(rep_names, unpack_replicated(o)):
            res[nm].append(val)

    order = ["a_w_in", "a_pool_w", "a_pool_scale", "a_w_out", "b_w_q", "b_w_out", "kv_w", "f_b", "mem_w_kv",
             "ln1_g", "ln1_b", "ln2_g", "ln2_b", "ffn_w_up", "ffn_conv_w", "ffn_conv_b", "ffn_w_down"]
    out = [loss, grad_x.reshape(nb, s, d)]
    for kind in range(4):
        out.extend(res[nm][kind] for nm in order)
    return tuple(out)
```

```python
import functools

import jax
import jax.numpy as jnp
from jax import lax
from jax.experimental import pallas as pl
from jax.experimental.pallas import tpu as pltpu

F32 = jnp.float32
BF16 = jnp.bfloat16
SDS = jax.ShapeDtypeStruct

N_DEV = 8
D_MODEL = 1024
TOK_WIDTH = 768
MEM_WIDTH = 256
MEM_LEN = 256
MEM_HEADS = 4
HEAD_DIM = 64
FOX_HEADS = 12
POOL_GROUP = 192
D_FF = 2752
FF_BLOCK = 688
FF_BLOCK_PAD = 768
FF_PAIRS = 4
KV_COLS = 1548
KV_COLS_PAD = 1664
LANES = 128
DEPTH = 2
DN_ALPHA = (2.0 * DEPTH) ** 0.25
LN_EPS = 1e-5
QK_SCALE = HEAD_DIM ** -0.5
NEG_BIG = -1e30

ADAM_LR = 0.001
ADAM_B1 = 0.9
ADAM_B2 = 0.999
ADAM_EPS = 1e-08
ADAM_WD = 0.01
ADAM_STEP = 10

VMEM_LIMIT_BYTES = 56 * 1024 * 1024
TM = 512
TS = 256
TF = 256
TC = 256
HALO_POOL = 16
HALO_CONV = 8

NT_DIMS = (((1,), (1,)), ((), ()))
TN_DIMS = (((0,), (0,)), ((), ()))


def _params(sem=None):
    return pltpu.CompilerParams(dimension_semantics=sem, vmem_limit_bytes=VMEM_LIMIT_BYTES)


def _sigmoid(z):
    return 1.0 / (1.0 + jnp.exp(-z))


def _pick_tn(n):
    if n <= 2048:
        return n
    for t in (1024, 768, 512, 256, 128):
        if n % t == 0:
            return t
    return n


def mm_nn(a, b, out_dtype, name, addend=None, add_scale=1.0, also_bf16=False):
    m, k = a.shape
    _, n = b.shape
    tm = min(TM, m)
    tn = _pick_tn(n)
    tk = k if k <= 2048 else 1024
    nk = k // tk
    has_add = addend is not None

    def body(*refs):
        a_ref, b_ref = refs[0], refs[1]
        pos = 2
        c_ref = None
        if has_add:
            c_ref = refs[pos]
            pos += 1
        o_ref = refs[pos]
        ob_ref = refs[pos + 1] if also_bf16 else None
        acc = refs[-1]
        kk = pl.program_id(2)

        @pl.when(kk == 0)
        def _():
            acc[...] = jnp.zeros_like(acc)

        acc[...] += jnp.dot(a_ref[...].astype(BF16), b_ref[...].astype(BF16), preferred_element_type=F32)

        @pl.when(kk == nk - 1)
        def _():
            r = acc[...]
            if has_add:
                r = r + add_scale * c_ref[...]
            o_ref[...] = r.astype(out_dtype)
            if also_bf16:
                ob_ref[...] = r.astype(BF16)

    in_specs = [pl.BlockSpec((tm, tk), lambda i, j, kk: (i, kk)),
                pl.BlockSpec((tk, tn), lambda i, j, kk: (kk, j))]
    ops = [a, b]
    if has_add:
        in_specs.append(pl.BlockSpec((tm, tn), lambda i, j, kk: (i, j)))
        ops.append(addend)
    out_shape = [SDS((m, n), out_dtype)]
    out_specs = [pl.BlockSpec((tm, tn), lambda i, j, kk: (i, j))]
    if also_bf16:
        out_shape.append(SDS((m, n), BF16))
        out_specs.append(pl.BlockSpec((tm, tn), lambda i, j, kk: (i, j)))
    res = pl.pallas_call(
        body, name=name, grid=(m // tm, n // tn, nk), in_specs=in_specs, out_specs=out_specs, out_shape=out_shape,
        scratch_shapes=[pltpu.VMEM((tm, tn), F32)],
        compiler_params=_params(("parallel", "parallel", "arbitrary")))(*ops)
    return tuple(res) if also_bf16 else res[0]


def mm_tn(a, b, name, blocked=False):
    t, m = a.shape
    _, n = b.shape
    tt = min(TM, t)
    tm = 1024 if m % 1024 == 0 else m
    tn = FF_BLOCK_PAD if blocked else _pick_tn(n)
    nt = t // tt

    def body(a_ref, b_ref, o_ref):
        kk = pl.program_id(2)
        r = lax.dot_general(a_ref[...].astype(BF16), b_ref[...].astype(BF16), TN_DIMS, preferred_element_type=F32)
        if blocked:
            r = r[None]

        @pl.when(kk == 0)
        def _():
            o_ref[...] = r

        @pl.when(kk != 0)
        def _():
            o_ref[...] += r

    if blocked:
        out_shape = SDS((n // tn, m, tn), F32)
        out_spec = pl.BlockSpec((1, tm, tn), lambda i, j, kk: (j, i, 0))
    else:
        out_shape = SDS((m, n), F32)
        out_spec = pl.BlockSpec((tm, tn), lambda i, j, kk: (i, j))
    return pl.pallas_call(
        body, name=name, grid=(m // tm, n // tn, nt),
        in_specs=[pl.BlockSpec((tt, tm), lambda i, j, kk: (kk, i)), pl.BlockSpec((tt, tn), lambda i, j, kk: (kk, j))],
        out_specs=out_spec, out_shape=out_shape,
        compiler_params=_params(("parallel", "parallel", "arbitrary")))(a, b)


def ln_fwd(xprev, delta, g, b, name):
    t, d = xprev.shape
    tm = min(TM, t)

    def body(xp_ref, dl_ref, g_ref, b_ref, y_ref, yb_ref, xh_ref, rs_ref):
        r = DN_ALPHA * xp_ref[...] + dl_ref[...]
        mu = jnp.mean(r, axis=1, keepdims=True)
        xc = r - mu
        var = jnp.mean(xc * xc, axis=1, keepdims=True)
        rstd = lax.rsqrt(var + LN_EPS)
        xh = xc * rstd
        y = xh * g_ref[...] + b_ref[...]
        y_ref[...] = y
        yb_ref[...] = y.astype(BF16)
        xh_ref[...] = xh
        rs_ref[...] = jnp.broadcast_to(rstd, (tm, LANES))

    row = pl.BlockSpec((tm, d), lambda i: (i, 0))
    vec = pl.BlockSpec((1, d), lambda i: (0, 0))
    return pl.pallas_call(
        body, name=name, grid=(t // tm,), in_specs=[row, row, vec, vec],
        out_specs=[row, row, row, pl.BlockSpec((tm, LANES), lambda i: (i, 0))],
        out_shape=[SDS((t, d), F32), SDS((t, d), BF16), SDS((t, d), F32), SDS((t, LANES), F32)],
        compiler_params=_params(("parallel",)))(xprev, delta, g, b)


def ln_bwd(dy, xhat, rstd, g, name):
    t, d = dy.shape
    tm = min(TM, t)

    def body(dy_ref, xh_ref, rs_ref, g_ref, dr_ref, drb_ref, dg_ref, db_ref):
        i = pl.program_id(0)
        dyv = dy_ref[...]
        xh = xh_ref[...]
        dxh = dyv * g_ref[...]
        m1 = jnp.mean(dxh, axis=1, keepdims=True)
        m2 = jnp.mean(dxh * xh, axis=1, keepdims=True)
        dr = rs_ref[:, 0:1] * (dxh - m1 - xh * m2)
        dr_ref[...] = dr
        drb_ref[...] = dr.astype(BF16)

        @pl.when(i == 0)
        def _():
            dg_ref[...] = jnp.zeros_like(dg_ref)
            db_ref[...] = jnp.zeros_like(db_ref)

        dg_ref[...] += jnp.sum(dyv * xh, axis=0, keepdims=True)
        db_ref[...] += jnp.sum(dyv, axis=0, keepdims=True)

    row = pl.BlockSpec((tm, d), lambda i: (i, 0))
    vec = pl.BlockSpec((1, d), lambda i: (0, 0))
    return pl.pallas_call(
        body, name=name, grid=(t // tm,),
        in_specs=[row, row, pl.BlockSpec((tm, LANES), lambda i: (i, 0)), vec],
        out_specs=[row, row, vec, vec],
        out_shape=[SDS((t, d), F32), SDS((t, d), BF16), SDS((1, d), F32), SDS((1, d), F32)],
        compiler_params=_params(("arbitrary",)))(dy, xhat, rstd, g)


def loss_head(y, target):
    t, d = y.shape
    tm = min(TM, t)
    nsteps = t // tm

    def body(y_ref, t_ref, dy_ref, l_ref, acc):
        i = pl.program_id(0)
        diff = y_ref[...] - t_ref[...]
        dy_ref[...] = diff * (1.0 / d)

        @pl.when(i == 0)
        def _():
            acc[...] = jnp.zeros_like(acc)

        acc[...] += jnp.sum(diff * diff, axis=0, keepdims=True)

        @pl.when(i == nsteps - 1)
        def _():
            tot = jnp.sum(acc[...], axis=1, keepdims=True) * (0.5 / d)
            l_ref[...] = jnp.broadcast_to(tot, (1, LANES))

    row = pl.BlockSpec((tm, d), lambda i: (i, 0))
    return pl.pallas_call(
        body, name="loss_head", grid=(nsteps,), in_specs=[row, row],
        out_specs=[row, pl.BlockSpec((1, LANES), lambda i: (0, 0))],
        out_shape=[SDS((t, d), F32), SDS((1, LANES), F32)],
        scratch_shapes=[pltpu.VMEM((1, d), F32)],
        compiler_params=_params(("arbitrary",)))(y, target)


def memattn_fwd(proj, memkv, nb, s, name):
    ts = min(TS, s)
    nq = s // ts

    def body(q_ref, kv_ref, o_ref):
        for h in range(MEM_HEADS):
            lo, hi = h * HEAD_DIM, (h + 1) * HEAD_DIM
            qh = q_ref[:, lo:hi].astype(BF16)
            kh = kv_ref[:, lo:hi]
            vh = kv_ref[:, MEM_WIDTH + lo:MEM_WIDTH + hi]
            sc = lax.dot_general(qh, kh, NT_DIMS, preferred_element_type=F32) * QK_SCALE
            p = jnp.exp(sc - jnp.max(sc, axis=1, keepdims=True))
            p = p / jnp.sum(p, axis=1, keepdims=True)
            o_ref[:, lo:hi] = jnp.dot(p.astype(BF16), vh, preferred_element_type=F32).astype(BF16)

    return pl.pallas_call(
        body, name=name, grid=(nb, nq),
        in_specs=[pl.BlockSpec((ts, MEM_WIDTH), lambda b, i: (b * nq + i, 3)),
                  pl.BlockSpec((MEM_LEN, 2 * MEM_WIDTH), lambda b, i: (b, 0))],
        out_specs=pl.BlockSpec((ts, MEM_WIDTH), lambda b, i: (b * nq + i, 0)),
        out_shape=SDS((nb * s, MEM_WIDTH), BF16),
        compiler_params=_params(("parallel", "parallel")))(proj, memkv)


def memattn_bwd(proj, memkv, dcat, nb, s, name):
    ts = min(TS, s)
    nq = s // ts

    def body(q_ref, kv_ref, do_ref, dq_ref, dkv_ref):
        i = pl.program_id(1)

        @pl.when(i == 0)
        def _():
            dkv_ref[...] = jnp.zeros_like(dkv_ref)

        for h in range(MEM_HEADS):
            lo, hi = h * HEAD_DIM, (h + 1) * HEAD_DIM
            qh = q_ref[:, lo:hi].astype(BF16)
            kh = kv_ref[:, lo:hi]
            vh = kv_ref[:, MEM_WIDTH + lo:MEM_WIDTH + hi]
            doh = do_ref[:, lo:hi].astype(BF16)
            sc = lax.dot_general(qh, kh, NT_DIMS, preferred_element_type=F32) * QK_SCALE
            p = jnp.exp(sc - jnp.max(sc, axis=1, keepdims=True))
            p = p / jnp.sum(p, axis=1, keepdims=True)
            dv = lax.dot_general(p.astype(BF16), doh, TN_DIMS, preferred_element_type=F32)
            dp = lax.dot_general(doh, vh, NT_DIMS, preferred_element_type=F32)
            dl = jnp.sum(p * dp, axis=1, keepdims=True)
            ds = (p * (dp - dl) * QK_SCALE).astype(BF16)
            dq_ref[:, lo:hi] = jnp.dot(ds, kh, preferred_element_type=F32).astype(BF16)
            dkv_ref[:, lo:hi] += lax.dot_general(ds, qh, TN_DIMS, preferred_element_type=F32)
            dkv_ref[:, MEM_WIDTH + lo:MEM_WIDTH + hi] += dv

    return pl.pallas_call(
        body, name=name, grid=(nb, nq),
        in_specs=[pl.BlockSpec((ts, MEM_WIDTH), lambda b, i: (b * nq + i, 3)),
                  pl.BlockSpec((MEM_LEN, 2 * MEM_WIDTH), lambda b, i: (b, 0)),
                  pl.BlockSpec((ts, MEM_WIDTH), lambda b, i: (b * nq + i, 3))],
        out_specs=[pl.BlockSpec((ts, MEM_WIDTH), lambda b, i: (b * nq + i, 0)),
                   pl.BlockSpec((MEM_LEN, 2 * MEM_WIDTH), lambda b, i: (b, 0))],
        out_shape=[SDS((nb * s, MEM_WIDTH), BF16), SDS((nb * MEM_LEN, 2 * MEM_WIDTH), F32)],
        compiler_params=_params(("parallel", "arbitrary")))(proj, memkv, dcat)


def _pool_select(shape, s2, s4, s8, s16):
    lane = lax.broadcasted_iota(jnp.int32, shape, 1)
    return jnp.where(lane < POOL_GROUP, s2, jnp.where(lane < 2 * POOL_GROUP, s4, jnp.where(lane < 3 * POOL_GROUP, s8, s16)))


def _pool_count(shape, first_pos):
    pos = first_pos + lax.broadcasted_iota(jnp.int32, shape, 0)
    win = _pool_select(shape, 2, 4, 8, 16)
    return jnp.minimum(pos + 1, win).astype(F32)


def pool_fwd(proj, pw_bd, pscale, nb, s):
    ts = min(TS, s)
    nq = s // ts
    w = TOK_WIDTH

    def body(c_ref, h_ref, w_ref, sc_ref, pooled_ref, tok_ref):
        i = pl.program_id(0) % nq
        cur = c_ref[...]
        halo = jnp.where(i == 0, 0.0, h_ref[...])
        xe = jnp.concatenate([halo, cur], axis=0)
        s2 = xe + pltpu.roll(xe, 1, axis=0)
        s4 = s2 + pltpu.roll(s2, 2, axis=0)
        s8 = s4 + pltpu.roll(s4, 4, axis=0)
        s16 = s8 + pltpu.roll(s8, 8, axis=0)
        hp = HALO_POOL
        ws = _pool_select((ts, w), s2[hp:], s4[hp:], s8[hp:], s16[hp:])
        pooled = (ws / _pool_count((ts, w), i * ts) - cur).astype(BF16)
        pooled_ref[...] = pooled
        mixed = jnp.dot(pooled, w_ref[...], preferred_element_type=F32)
        tok_ref[...] = (mixed * sc_ref[...]).astype(BF16)

    row = pl.BlockSpec((ts, w), lambda r: (r, 0))
    return pl.pallas_call(
        body, name="pool_fwd", grid=(nb * nq,),
        in_specs=[row, pl.BlockSpec((HALO_POOL, w), lambda r: (jnp.maximum(r * (ts // HALO_POOL) - 1, 0), 0)),
                  pl.BlockSpec((w, w), lambda r: (0, 0)), pl.BlockSpec((1, w), lambda r: (0, 0))],
        out_specs=[row, row], out_shape=[SDS((nb * s, w), BF16), SDS((nb * s, w), BF16)],
        compiler_params=_params(("parallel",)))(proj, proj, pw_bd, pscale)


def pool_bwd_mix(dcat, pooled, pw_bd, pw_bd_t, pscale, nb, s):
    ts = min(TS, s)
    w = TOK_WIDTH

    def body(dt_ref, p_ref, w_ref, wt_ref, sc_ref, dm_ref, dp_ref, ds_ref):
        r = pl.program_id(0)
        dtok = dt_ref[...]
        mixed = jnp.dot(p_ref[...], w_ref[...], preferred_element_type=F32)

        @pl.when(r == 0)
        def _():
            ds_ref[...] = jnp.zeros_like(ds_ref)

        ds_ref[...] += jnp.sum(dtok * mixed, axis=0, keepdims=True)
        dmx = (dtok * sc_ref[...]).astype(BF16)
        dm_ref[...] = dmx
        dp_ref[...] = jnp.dot(dmx, wt_ref[...], preferred_element_type=F32)

    row = pl.BlockSpec((ts, w), lambda r: (r, 0))
    mat = pl.BlockSpec((w, w), lambda r: (0, 0))
    vec = pl.BlockSpec((1, w), lambda r: (0, 0))
    return pl.pallas_call(
        body, name="pool_bwd_mix", grid=(nb * s // ts,), in_specs=[row, row, mat, mat, vec],
        out_specs=[row, row, vec], out_shape=[SDS((nb * s, w), BF16), SDS((nb * s, w), F32), SDS((1, w), F32)],
        compiler_params=_params(("arbitrary",)))(dcat, pooled, pw_bd, pw_bd_t, pscale)


def pool_bwd_window(dpooled, nb, s):
    ts = min(TS, s)
    nq = s // ts
    w = TOK_WIDTH
    n_ext = ts + HALO_POOL
    n_halo_blocks = nb * s // HALO_POOL

    def body(c_ref, n_ref, du_ref):
        i = pl.program_id(0) % nq
        cur = c_ref[...]
        nxt = jnp.where(i == nq - 1, 0.0, n_ref[...])
        ze = jnp.concatenate([cur, nxt], axis=0) / _pool_count((n_ext, w), i * ts)
        s2 = ze + pltpu.roll(ze, n_ext - 1, axis=0)
        s4 = s2 + pltpu.roll(s2, n_ext - 2, axis=0)
        s8 = s4 + pltpu.roll(s4, n_ext - 4, axis=0)
        s16 = s8 + pltpu.roll(s8, n_ext - 8, axis=0)
        ws = _pool_select((ts, w), s2[:ts], s4[:ts], s8[:ts], s16[:ts])
        du_ref[...] = (ws - cur).astype(BF16)

    row = pl.BlockSpec((ts, w), lambda r: (r, 0))
    return pl.pallas_call(
        body, name="pool_bwd_window", grid=(nb * nq,),
        in_specs=[row, pl.BlockSpec((HALO_POOL, w),
                                    lambda r: (jnp.minimum((r + 1) * (ts // HALO_POOL), n_halo_blocks - 1), 0))],
        out_specs=row, out_shape=SDS((nb * s, w), BF16),
        compiler_params=_params(("parallel",)))(dpooled, dpooled)


def _conv_rows(xe, w_ref):
    return (w_ref[0, 2:3, :] * xe + w_ref[0, 1:2, :] * pltpu.roll(xe, 1, axis=0)
            + w_ref[0, 0:1, :] * pltpu.roll(xe, 2, axis=0) + w_ref[0, 3:4, :])


def convgate_fwd(h, cw, nb, s, name):
    ts = min(TS, s)
    nq = s // ts
    w = FF_BLOCK_PAD
    hc = HALO_CONV

    def body(uc_ref, uh_ref, gc_ref, gh_ref, wu_ref, wg_ref, o_ref):
        first = (pl.program_id(0) % nq) == 0
        xu = jnp.concatenate([jnp.where(first, 0.0, uh_ref[...]), uc_ref[...]], axis=0)
        xg = jnp.concatenate([jnp.where(first, 0.0, gh_ref[...]), gc_ref[...]], axis=0)
        cu = _conv_rows(xu, wu_ref)[hc:]
        cg = _conv_rows(xg, wg_ref)[hc:]
        o_ref[...] = (cg * _sigmoid(cg) * cu).astype(BF16)

    def cur(off):
        return pl.BlockSpec((ts, w), lambda r, j: (r, j + off))

    def halo(off):
        return pl.BlockSpec((hc, w), lambda r, j: (jnp.maximum(r * (ts // hc) - 1, 0), j + off))

    def wspec(off):
        return pl.BlockSpec((1, 8, w), lambda r, j: (j + off, 0, 0))

    return pl.pallas_call(
        body, name=name, grid=(nb * nq, FF_PAIRS),
        in_specs=[cur(0), halo(0), cur(FF_PAIRS), halo(FF_PAIRS), wspec(0), wspec(FF_PAIRS)],
        out_specs=pl.BlockSpec((ts, w), lambda r, j: (r, j)), out_shape=SDS((nb * s, FF_PAIRS * w), BF16),
        compiler_params=_params(("parallel", "parallel")))(h, h, h, h, cw, cw)


def convgate_bwd(h, dact, cw, nb, s, name):
    ts = min(TS, s)
    nq = s // ts
    w = FF_BLOCK_PAD
    hc = HALO_CONV
    n_ext = ts + hc
    n_halo_blocks = nb * s // hc

    def body(uc_ref, up_ref, un_ref, gc_ref, gp_ref, gn_ref, dc_ref, dn_ref, wu_ref, wg_ref,
             dhu_ref, dhg_ref, dwu_ref, dwg_ref):
        r = pl.program_id(1)
        i = r % nq
        first = i == 0
        last = i == nq - 1
        xu = jnp.concatenate([jnp.where(first, 0.0, up_ref[...]), uc_ref[...], un_ref[...]], axis=0)
        xg = jnp.concatenate([jnp.where(first, 0.0, gp_ref[...]), gc_ref[...], gn_ref[...]], axis=0)
        cu = _conv_rows(xu, wu_ref)[hc:]
        cg = _conv_rows(xg, wg_ref)[hc:]
        da = jnp.concatenate([dc_ref[...].astype(F32), jnp.where(last, 0.0, dn_ref[...].astype(F32)[:hc])], axis=0)
        sg = _sigmoid(cg)
        dcu = da * (cg * sg)
        dcg = da * cu * (sg * (1.0 + cg * (1.0 - sg)))

        def conv_t(dcv, w_ref):
            return (w_ref[0, 2:3, :] * dcv + w_ref[0, 1:2, :] * pltpu.roll(dcv, n_ext - 1, axis=0)
                    + w_ref[0, 0:1, :] * pltpu.roll(dcv, n_ext - 2, axis=0))[:ts]

        dhu_ref[...] = conv_t(dcu, wu_ref).astype(BF16)
        dhg_ref[...] = conv_t(dcg, wg_ref).astype(BF16)

        def tap_grads(xe, dcv):
            d0 = dcv[:ts]
            x0 = xe[hc:hc + ts]
            x1 = pltpu.roll(xe, 1, axis=0)[hc:hc + ts]
            x2 = pltpu.roll(xe, 2, axis=0)[hc:hc + ts]
            rows = [jnp.sum(d0 * x2, axis=0, keepdims=True), jnp.sum(d0 * x1, axis=0, keepdims=True),
                    jnp.sum(d0 * x0, axis=0, keepdims=True), jnp.sum(d0, axis=0, keepdims=True)]
            sub = lax.broadcasted_iota(jnp.int32, (8, w), 0)
            upd = jnp.zeros((8, w), F32)
            for k, rv in enumerate(rows):
                upd = jnp.where(sub == k, rv, upd)
            return upd[None]

        @pl.when(r == 0)
        def _():
            dwu_ref[...] = jnp.zeros_like(dwu_ref)
            dwg_ref[...] = jnp.zeros_like(dwg_ref)

        dwu_ref[...] += tap_grads(xu, dcu)
        dwg_ref[...] += tap_grads(xg, dcg)

    def cur(off):
        return pl.BlockSpec((ts, w), lambda j, r: (r, j + off))

    def prev(off):
        return pl.BlockSpec((hc, w), lambda j, r: (jnp.maximum(r * (ts // hc) - 1, 0), j + off))

    def nxt(off):
        return pl.BlockSpec((hc, w), lambda j, r: (jnp.minimum((r + 1) * (ts // hc), n_halo_blocks - 1), j + off))

    def wspec(off):
        return pl.BlockSpec((1, 8, w), lambda j, r: (j + off, 0, 0))

    hb = 2 * hc
    dact_next = pl.BlockSpec((hb, w), lambda j, r: (jnp.minimum((r + 1) * (ts // hb), nb * s // hb - 1), j))

    p = FF_PAIRS
    dh_spec = pl.BlockSpec((ts, w), lambda j, r: (r, j))
    dw_spec = pl.BlockSpec((1, 8, w), lambda j, r: (j, 0, 0))
    return pl.pallas_call(
        body, name=name, grid=(p, nb * nq),
        in_specs=[cur(0), prev(0), nxt(0), cur(p), prev(p), nxt(p), cur(0), dact_next, wspec(0), wspec(p)],
        out_specs=[dh_spec, dh_spec, dw_spec, dw_spec],
        out_shape=[SDS((nb * s, p * w), BF16), SDS((nb * s, p * w), BF16), SDS((p, 8, w), F32), SDS((p, 8, w), F32)],
        compiler_params=_params(("parallel", "arbitrary")))(h, h, h, h, h, h, dact, dact, cw, cw)


def _tri(n, upper):
    r = lax.broadcasted_iota(jnp.int32, (n, n), 0)
    c = lax.broadcasted_iota(jnp.int32, (n, n), 1)
    return ((r <= c) if upper else (r >= c)).astype(F32)


def fgate_fwd(fl, fb, nb, s):
    tc = min(TC, s)
    nq = s // tc

    def body(fl_ref, fb_ref, f_ref, carry):
        @pl.when(pl.program_id(1) == 0)
        def _():
            carry[...] = jnp.zeros_like(carry)

        z = fl_ref[...] + fb_ref[...]
        logf = jnp.minimum(z, 0.0) - jnp.log(1.0 + jnp.exp(-jnp.abs(z)))
        f_ref[...] = jnp.dot(_tri(tc, False), logf, preferred_element_type=F32,
                             precision=lax.Precision.HIGHEST) + carry[...]
        carry[...] += jnp.sum(logf, axis=0, keepdims=True)

    row = pl.BlockSpec((tc, LANES), lambda b, i: (b * nq + i, 0))
    return pl.pallas_call(
        body, name="fgate_fwd", grid=(nb, nq), in_specs=[row, pl.BlockSpec((1, LANES), lambda b, i: (0, 0))],
        out_specs=row, out_shape=SDS((nb * s, LANES), F32), scratch_shapes=[pltpu.VMEM((1, LANES), F32)],
        compiler_params=_params(("arbitrary", "arbitrary")))(fl, fb)


def fgate_bwd(d_cum_q, d_cum_k, fl, fb, nb, s):
    tc = min(TC, s)
    nq = s // tc

    def body(dfq_ref, dfk_ref, fl_ref, fb_ref, dfl_ref, dfb_ref, carry):
        b = pl.program_id(0)
        i = pl.program_id(1)

        @pl.when(i == 0)
        def _():
            carry[...] = jnp.zeros_like(carry)

        @pl.when(jnp.logical_and(b == 0, i == 0))
        def _():
            dfb_ref[...] = jnp.zeros_like(dfb_ref)

        dfv = dfq_ref[...] + dfk_ref[...]
        dlog = jnp.dot(_tri(tc, True), dfv, preferred_element_type=F32,
                       precision=lax.Precision.HIGHEST) + carry[...]
        carry[...] += jnp.sum(dfv, axis=0, keepdims=True)
        z = fl_ref[...] + fb_ref[...]
        dfl = dlog / (1.0 + jnp.exp(z))
        dfl_ref[...] = dfl
        dfb_ref[...] += jnp.sum(dfl, axis=0, keepdims=True)

    row = pl.BlockSpec((tc, LANES), lambda b, i: (b * nq + nq - 1 - i, 0))
    vec = pl.BlockSpec((1, LANES), lambda b, i: (0, 0))
    return pl.pallas_call(
        body, name="fgate_bwd", grid=(nb, nq), in_specs=[row, row, row, vec], out_specs=[row, vec],
        out_shape=[SDS((nb * s, LANES), F32), SDS((1, LANES), F32)], scratch_shapes=[pltpu.VMEM((1, LANES), F32)],
        compiler_params=_params(("arbitrary", "arbitrary")))(d_cum_q, d_cum_k, fl, fb)


def _lane_put(shape, h, col):
    lane = lax.broadcasted_iota(jnp.int32, shape, 1)
    return jnp.where(lane == h, col, 0.0)


PAIR = 2 * HEAD_DIM
N_PAIRS = FOX_HEADS // 2


def _half_masks(rows):
    lane = lax.broadcasted_iota(jnp.int32, (rows, PAIR), 1)
    return lane < HEAD_DIM


def _split_pair(x, scale=None):
    if scale is not None:
        x = x * scale
    lo = _half_masks(x.shape[0])
    zero = jnp.zeros_like(x)
    return jnp.where(lo, x, zero), jnp.where(lo, zero, x)


def make_vaug(kv):
    t = kv.shape[0]
    v = kv[:, TOK_WIDTH:].reshape(t, N_PAIRS, 2, HEAD_DIM)
    one = jnp.ones((t, N_PAIRS, 1), kv.dtype)
    z63 = jnp.zeros((t, N_PAIRS, HEAD_DIM - 1), kv.dtype)
    even = jnp.concatenate([v[:, :, 0], one, z63], axis=-1)
    odd = jnp.concatenate([one, z63, v[:, :, 1]], axis=-1)
    return jnp.stack([even, odd], axis=2).reshape(t, FOX_HEADS * PAIR)


def _to_tile_rows(a, nb, s, tf):
    return a.reshape(nb * s // tf, tf, LANES)[:, :, :16].transpose(0, 2, 1)


def fox_fwd(pq, kv, vaug, fneg_rows, nb, s):
    tf = min(TF, s)
    n = s // tf
    w = TOK_WIDTH

    def body(q_ref, k_ref, v_ref, ft_ref, ob_ref, of_ref, lse_ref, qm_scr, m_scr, acc_scr):
        i = pl.program_id(1)
        for p in range(N_PAIRS):
            qe, qo = _split_pair(q_ref[:, p * PAIR:(p + 1) * PAIR], QK_SCALE)
            qm_scr[2 * p] = qe
            qm_scr[2 * p + 1] = qo
        m_scr[...] = jnp.full(m_scr.shape, NEG_BIG, F32)
        acc_scr[...] = jnp.zeros_like(acc_scr)

        def tile(j, masked):
            ks = pl.multiple_of(j * tf, tf)
            if masked:
                keep = lax.broadcasted_iota(jnp.int32, (tf, tf), 0) >= lax.broadcasted_iota(jnp.int32, (tf, tf), 1)
            for h in range(FOX_HEADS):
                p = h // 2
                kp = k_ref[pl.ds(ks, tf), p * PAIR:(p + 1) * PAIR]
                sc = lax.dot_general(qm_scr[h], kp, NT_DIMS, preferred_element_type=F32) + ft_ref[j, h:h + 1, :]
                if masked:
                    sc = jnp.where(keep, sc, NEG_BIG)
                m_prev = m_scr[h]
                m_new = jnp.maximum(m_prev, jnp.max(sc, axis=1, keepdims=True))
                pr = jnp.exp(sc - m_new).astype(BF16)
                pv = jnp.dot(pr, v_ref[pl.ds(ks, tf), h * PAIR:(h + 1) * PAIR], preferred_element_type=F32)
                acc_scr[h] = jnp.exp(m_prev - m_new) * acc_scr[h] + pv
                m_scr[h] = m_new

        def step(j, carry):
            tile(j, False)
            return carry

        lax.fori_loop(0, i, step, 0)
        tile(i, True)

        lo = _half_masks(tf)
        lse = jnp.zeros((tf, LANES), F32)
        for p in range(N_PAIRS):
            he, ho = 2 * p, 2 * p + 1
            le = acc_scr[he, :, HEAD_DIM:HEAD_DIM + 1]
            lod = acc_scr[ho, :, 0:1]
            o = jnp.where(lo, acc_scr[he] / le, acc_scr[ho] / lod)
            ob_ref[:, p * PAIR:(p + 1) * PAIR] = o.astype(BF16)
            of_ref[:, p * PAIR:(p + 1) * PAIR] = o
            lse = lse + _lane_put((tf, LANES), he, m_scr[he] + jnp.log(le))
            lse = lse + _lane_put((tf, LANES), ho, m_scr[ho] + jnp.log(lod))
        lse_ref[...] = lse

    qrow = lambda b, i: (b * n + i, 0)
    return pl.pallas_call(
        body, name="fox_fwd", grid=(nb, n),
        in_specs=[pl.BlockSpec((tf, w), qrow),
                  pl.BlockSpec((s, w), lambda b, i: (b, 0)),
                  pl.BlockSpec((s, FOX_HEADS * PAIR), lambda b, i: (b, 0)),
                  pl.BlockSpec((n, 16, tf), lambda b, i: (b, 0, 0))],
        out_specs=[pl.BlockSpec((tf, w), qrow), pl.BlockSpec((tf, w), qrow), pl.BlockSpec((tf, LANES), qrow)],
        out_shape=[SDS((nb * s, w), BF16), SDS((nb * s, w), F32), SDS((nb * s, LANES), F32)],
        scratch_shapes=[pltpu.VMEM((FOX_HEADS, tf, PAIR), BF16), pltpu.VMEM((FOX_HEADS, tf, 1), F32),
                        pltpu.VMEM((FOX_HEADS, tf, PAIR), F32)],
        compiler_params=_params(("parallel", "arbitrary")))(pq, kv, vaug, fneg_rows)


def fox_bwd_dq(pq, kv, fneg_rows, dcat_bf, lse, delta, nb, s):
    tf = min(TF, s)
    n = s // tf
    w = TOK_WIDTH

    def body(q_ref, k_ref, v_ref, ft_ref, do_ref, lse_ref, dl_ref, dq_ref, df_ref, qm_scr, dom_scr, acc_scr, rs_scr):
        i = pl.program_id(1)
        for p in range(N_PAIRS):
            qe, qo = _split_pair(q_ref[:, p * PAIR:(p + 1) * PAIR], QK_SCALE)
            qm_scr[2 * p] = qe
            qm_scr[2 * p + 1] = qo
            de, dod = _split_pair(do_ref[:, p * PAIR:(p + 1) * PAIR])
            dom_scr[2 * p] = de
            dom_scr[2 * p + 1] = dod
        acc_scr[...] = jnp.zeros_like(acc_scr)
        rs_scr[...] = jnp.zeros_like(rs_scr)

        def tile(j, masked):
            ks = pl.multiple_of(j * tf, tf)
            if masked:
                keep = lax.broadcasted_iota(jnp.int32, (tf, tf), 0) >= lax.broadcasted_iota(jnp.int32, (tf, tf), 1)
            for h in range(FOX_HEADS):
                p = h // 2
                kp = k_ref[pl.ds(ks, tf), p * PAIR:(p + 1) * PAIR]
                vp = v_ref[pl.ds(ks, tf), p * PAIR:(p + 1) * PAIR]
                sc = lax.dot_general(qm_scr[h], kp, NT_DIMS, preferred_element_type=F32) + ft_ref[j, h:h + 1, :]
                sc = sc - lse_ref[:, h:h + 1]
                if masked:
                    sc = jnp.where(keep, sc, NEG_BIG)
                pr = jnp.exp(sc)
                dp = lax.dot_general(dom_scr[h], vp, NT_DIMS, preferred_element_type=F32)
                ds = pr * (dp - dl_ref[:, h:h + 1])
                part = ds[:, :LANES]
                for c in range(1, tf // LANES):
                    part = part + ds[:, c * LANES:(c + 1) * LANES]
                rs_scr[h] += part
                acc_scr[h] += jnp.dot(ds.astype(BF16), kp, preferred_element_type=F32)

        def step(j, carry):
            tile(j, False)
            return carry

        lax.fori_loop(0, i, step, 0)
        tile(i, True)

        lo = _half_masks(tf)
        dfq = jnp.zeros((tf, LANES), F32)
        for p in range(N_PAIRS):
            dq = jnp.where(lo, acc_scr[2 * p], acc_scr[2 * p + 1]) * QK_SCALE
            dq_ref[:, p * PAIR:(p + 1) * PAIR] = dq.astype(BF16)
            for h in (2 * p, 2 * p + 1):
                dfq = dfq + _lane_put((tf, LANES), h, jnp.sum(rs_scr[h], axis=1, keepdims=True))
        df_ref[...] = dfq

    qrow = lambda b, i: (b * n + i, 0)
    stat = pl.BlockSpec((tf, LANES), qrow)
    return pl.pallas_call(
        body, name="fox_bwd_dq", grid=(nb, n),
        in_specs=[pl.BlockSpec((tf, w), qrow), pl.BlockSpec((s, w), lambda b, i: (b, 0)),
                  pl.BlockSpec((s, w), lambda b, i: (b, 1)), pl.BlockSpec((n, 16, tf), lambda b, i: (b, 0, 0)),
                  pl.BlockSpec((tf, w), qrow), stat, stat],
        out_specs=[pl.BlockSpec((tf, w), qrow), stat],
        out_shape=[SDS((nb * s, w), BF16), SDS((nb * s, LANES), F32)],
        scratch_shapes=[pltpu.VMEM((FOX_HEADS, tf, PAIR), BF16), pltpu.VMEM((FOX_HEADS, tf, PAIR), BF16),
                        pltpu.VMEM((FOX_HEADS, tf, PAIR), F32), pltpu.VMEM((FOX_HEADS, tf, LANES), F32)],
        compiler_params=_params(("parallel", "arbitrary")))(pq, kv, kv, fneg_rows, dcat_bf, lse, delta)


def fox_bwd_dkv(pq, kv, fneg, dcat_bf, lse_rows, delta_rows, nb, s):
    tf = min(TF, s)
    n = s // tf
    w = TOK_WIDTH

    def body(q_ref, k_ref, v_ref, f_ref, do_ref, lse_ref, dl_ref, dk_ref, dv_ref, df_ref,
             km_scr, vm_scr, fk_scr, dk_scr, dv_scr, rs_scr):
        j = pl.program_id(1)
        for p in range(N_PAIRS):
            ke, ko = _split_pair(k_ref[:, p * PAIR:(p + 1) * PAIR], QK_SCALE)
            km_scr[2 * p] = ke
            km_scr[2 * p + 1] = ko
            ve, vo = _split_pair(v_ref[:, p * PAIR:(p + 1) * PAIR])
            vm_scr[2 * p] = ve
            vm_scr[2 * p + 1] = vo
        for h in range(FOX_HEADS):
            fk_scr[h] = jnp.broadcast_to(f_ref[:, h:h + 1], (tf, tf))
        dk_scr[...] = jnp.zeros_like(dk_scr)
        dv_scr[...] = jnp.zeros_like(dv_scr)
        rs_scr[...] = jnp.zeros_like(rs_scr)

        def tile(i, masked):
            qs = pl.multiple_of(i * tf, tf)
            if masked:
                keep = lax.broadcasted_iota(jnp.int32, (tf, tf), 1) >= lax.broadcasted_iota(jnp.int32, (tf, tf), 0)
            for h in range(FOX_HEADS):
                p = h // 2
                qp = q_ref[pl.ds(qs, tf), p * PAIR:(p + 1) * PAIR]
                dop = do_ref[pl.ds(qs, tf), p * PAIR:(p + 1) * PAIR]
                sc = lax.dot_general(km_scr[h], qp, NT_DIMS, preferred_element_type=F32) + fk_scr[h]
                sc = sc - lse_ref[i, h:h + 1, :]
                if masked:
                    sc = jnp.where(keep, sc, NEG_BIG)
                pr = jnp.exp(sc)
                dv_scr[h] += jnp.dot(pr.astype(BF16), dop, preferred_element_type=F32)
                dp = lax.dot_general(vm_scr[h], dop, NT_DIMS, preferred_element_type=F32)
                ds = pr * (dp - dl_ref[i, h:h + 1, :])
                part = ds[:, :LANES]
                for c in range(1, tf // LANES):
                    part = part + ds[:, c * LANES:(c + 1) * LANES]
                rs_scr[h] += part
                dk_scr[h] += jnp.dot(ds.astype(BF16), qp, preferred_element_type=F32)

        def step(i, carry):
            tile(i, False)
            return carry

        tile(j, True)
        lax.fori_loop(j + 1, n, step, 0)

        lo = _half_masks(tf)
        dfk = jnp.zeros((tf, LANES), F32)
        for p in range(N_PAIRS):
            dk = jnp.where(lo, dk_scr[2 * p], dk_scr[2 * p + 1]) * QK_SCALE
            dk_ref[:, p * PAIR:(p + 1) * PAIR] = dk.astype(BF16)
            dv_ref[:, p * PAIR:(p + 1) * PAIR] = jnp.where(lo, dv_scr[2 * p], dv_scr[2 * p + 1]).astype(BF16)
            for h in (2 * p, 2 * p + 1):
                dfk = dfk - _lane_put((tf, LANES), h, jnp.sum(rs_scr[h], axis=1, keepdims=True))
        df_ref[...] = dfk

    krow = lambda b, j: (b * n + j, 0)
    rows = pl.BlockSpec((n, 16, tf), lambda b, j: (b, 0, 0))
    return pl.pallas_call(
        body, name="fox_bwd_dkv", grid=(nb, n),
        in_specs=[pl.BlockSpec((s, w), lambda b, j: (b, 0)), pl.BlockSpec((tf, w), krow),
                  pl.BlockSpec((tf, w), lambda b, j: (b * n + j, 1)), pl.BlockSpec((tf, LANES), krow),
                  pl.BlockSpec((s, w), lambda b, j: (b, 0)), rows, rows],
        out_specs=[pl.BlockSpec((tf, w), krow), pl.BlockSpec((tf, w), krow), pl.BlockSpec((tf, LANES), krow)],
        out_shape=[SDS((nb * s, w), BF16), SDS((nb * s, w), BF16), SDS((nb * s, LANES), F32)],
        scratch_shapes=[pltpu.VMEM((FOX_HEADS, tf, PAIR), BF16), pltpu.VMEM((FOX_HEADS, tf, PAIR), BF16),
                        pltpu.VMEM((FOX_HEADS, tf, tf), F32), pltpu.VMEM((FOX_HEADS, tf, PAIR), F32),
                        pltpu.VMEM((FOX_HEADS, tf, PAIR), F32), pltpu.VMEM((FOX_HEADS, tf, LANES), F32)],
        compiler_params=_params(("parallel", "arbitrary")))(pq, kv, kv, fneg, dcat_bf, lse_rows, delta_rows)


def _old_fox_fwd(pq, kv, fcum, fcum_t, nb, s):
    tf = min(TF, s)
    n = s // tf
    w = TOK_WIDTH

    def body(q_ref, k_ref, v_ref, f_ref, ft_ref, ob_ref, of_ref, lse_ref, m_scr, l_scr, acc_scr):
        i = pl.program_id(1)
        j = pl.program_id(2)

        @pl.when(j == 0)
        def _():
            m_scr[...] = jnp.full(m_scr.shape, NEG_BIG, F32)
            l_scr[...] = jnp.zeros_like(l_scr)
            acc_scr[...] = jnp.zeros_like(acc_scr)

        @pl.when(j <= i)
        def _():
            row = lax.broadcasted_iota(jnp.int32, (tf, tf), 0)
            col = lax.broadcasted_iota(jnp.int32, (tf, tf), 1)
            keep = jnp.logical_or(j < i, row >= col)
            for h in range(FOX_HEADS):
                lo, hi = h * HEAD_DIM, (h + 1) * HEAD_DIM
                sc = lax.dot_general(q_ref[:, lo:hi], k_ref[:, lo:hi], NT_DIMS, preferred_element_type=F32) * QK_SCALE
                sc = sc + f_ref[:, h:h + 1] - ft_ref[h:h + 1, :]
                sc = jnp.where(keep, sc, NEG_BIG)
                m_prev = m_scr[h]
                m_new = jnp.maximum(m_prev, jnp.max(sc, axis=1, keepdims=True))
                a = jnp.exp(m_prev - m_new)
                p = jnp.exp(sc - m_new)
                l_scr[h] = a * l_scr[h] + jnp.sum(p, axis=1, keepdims=True)
                acc_scr[:, lo:hi] = a * acc_scr[:, lo:hi] + jnp.dot(p.astype(BF16), v_ref[:, lo:hi],
                                                                   preferred_element_type=F32)
                m_scr[h] = m_new

        @pl.when(j == n - 1)
        def _():
            lse = jnp.zeros((tf, LANES), F32)
            for h in range(FOX_HEADS):
                lo, hi = h * HEAD_DIM, (h + 1) * HEAD_DIM
                o = acc_scr[:, lo:hi] / l_scr[h]
                ob_ref[:, lo:hi] = o.astype(BF16)
                of_ref[:, lo:hi] = o
                lse = lse + _lane_put((tf, LANES), h, m_scr[h] + jnp.log(l_scr[h]))
            lse_ref[...] = lse

    qrow = lambda b, i, j: (b * n + i, 0)
    return pl.pallas_call(
        body, name="fox_fwd", grid=(nb, n, n),
        in_specs=[pl.BlockSpec((tf, w), qrow),
                  pl.BlockSpec((tf, w), lambda b, i, j: (b * n + jnp.minimum(j, i), 0)),
                  pl.BlockSpec((tf, w), lambda b, i, j: (b * n + jnp.minimum(j, i), 1)),
                  pl.BlockSpec((tf, LANES), qrow),
                  pl.BlockSpec((16, tf), lambda b, i, j: (b, jnp.minimum(j, i)))],
        out_specs=[pl.BlockSpec((tf, w), qrow), pl.BlockSpec((tf, w), qrow), pl.BlockSpec((tf, LANES), qrow)],
        out_shape=[SDS((nb * s, w), BF16), SDS((nb * s, w), F32), SDS((nb * s, LANES), F32)],
        scratch_shapes=[pltpu.VMEM((FOX_HEADS, tf, 1), F32), pltpu.VMEM((FOX_HEADS, tf, 1), F32),
                        pltpu.VMEM((tf, w), F32)],
        compiler_params=_params(("parallel", "parallel", "arbitrary")))(pq, kv, kv, fcum, fcum_t)


def fox_delta(dcat, o, nb, s):
    tf = min(TM, s)
    w = TOK_WIDTH

    def body(do_ref, o_ref, dl_ref):
        out = jnp.zeros((tf, LANES), F32)
        for h in range(FOX_HEADS):
            lo, hi = h * HEAD_DIM, (h + 1) * HEAD_DIM
            out = out + _lane_put((tf, LANES), h, jnp.sum(do_ref[:, lo:hi] * o_ref[:, lo:hi], axis=1, keepdims=True))
        dl_ref[...] = out

    row = pl.BlockSpec((tf, w), lambda r: (r, 0))
    return pl.pallas_call(
        body, name="fox_delta", grid=(nb * s // tf,), in_specs=[row, row],
        out_specs=pl.BlockSpec((tf, LANES), lambda r: (r, 0)), out_shape=SDS((nb * s, LANES), F32),
        compiler_params=_params(("parallel",)))(dcat, o)


def _old_fox_bwd_dq(pq, kv, fcum, fcum_t, dcat_bf, lse, delta, nb, s):
    tf = min(TF, s)
    n = s // tf
    w = TOK_WIDTH

    def body(q_ref, k_ref, v_ref, f_ref, ft_ref, do_ref, lse_ref, dl_ref, dq_ref, df_ref, acc_scr, df_scr):
        i = pl.program_id(1)
        j = pl.program_id(2)

        @pl.when(j == 0)
        def _():
            acc_scr[...] = jnp.zeros_like(acc_scr)
            df_scr[...] = jnp.zeros_like(df_scr)

        @pl.when(j <= i)
        def _():
            row = lax.broadcasted_iota(jnp.int32, (tf, tf), 0)
            col = lax.broadcasted_iota(jnp.int32, (tf, tf), 1)
            keep = jnp.logical_or(j < i, row >= col)
            dfacc = jnp.zeros((tf, LANES), F32)
            for h in range(FOX_HEADS):
                lo, hi = h * HEAD_DIM, (h + 1) * HEAD_DIM
                kh = k_ref[:, lo:hi]
                sc = lax.dot_general(q_ref[:, lo:hi], kh, NT_DIMS, preferred_element_type=F32) * QK_SCALE
                sc = sc + (f_ref[:, h:h + 1] - lse_ref[:, h:h + 1]) - ft_ref[h:h + 1, :]
                p = jnp.exp(jnp.where(keep, sc, NEG_BIG))
                dp = lax.dot_general(do_ref[:, lo:hi], v_ref[:, lo:hi], NT_DIMS, preferred_element_type=F32)
                ds = p * (dp - dl_ref[:, h:h + 1])
                dfacc = dfacc + _lane_put((tf, LANES), h, jnp.sum(ds, axis=1, keepdims=True))
                acc_scr[:, lo:hi] += jnp.dot(ds.astype(BF16), kh, preferred_element_type=F32)
            df_scr[...] += dfacc

        @pl.when(j == n - 1)
        def _():
            dq_ref[...] = (acc_scr[...] * QK_SCALE).astype(BF16)
            df_ref[...] = df_scr[...]

    qrow = lambda b, i, j: (b * n + i, 0)
    krow = lambda b, i, j: (b * n + jnp.minimum(j, i), 0)
    return pl.pallas_call(
        body, name="fox_bwd_dq", grid=(nb, n, n),
        in_specs=[pl.BlockSpec((tf, w), qrow), pl.BlockSpec((tf, w), krow),
                  pl.BlockSpec((tf, w), lambda b, i, j: (b * n + jnp.minimum(j, i), 1)),
                  pl.BlockSpec((tf, LANES), qrow),
                  pl.BlockSpec((16, tf), lambda b, i, j: (b, jnp.minimum(j, i))),
                  pl.BlockSpec((tf, w), qrow), pl.BlockSpec((tf, LANES), qrow), pl.BlockSpec((tf, LANES), qrow)],
        out_specs=[pl.BlockSpec((tf, w), qrow), pl.BlockSpec((tf, LANES), qrow)],
        out_shape=[SDS((nb * s, w), BF16), SDS((nb * s, LANES), F32)],
        scratch_shapes=[pltpu.VMEM((tf, w), F32), pltpu.VMEM((tf, LANES), F32)],
        compiler_params=_params(("parallel", "parallel", "arbitrary")))(pq, kv, kv, fcum, fcum_t, dcat_bf, lse, delta)


def _old_fox_bwd_dkv(pq, kv, fcum, fcum_t, dcat_bf, lse_t, delta_t, nb, s):
    tf = min(TF, s)
    n = s // tf
    w = TOK_WIDTH

    def body(q_ref, k_ref, v_ref, f_ref, ft_ref, do_ref, lse_ref, dl_ref, dk_ref, dv_ref, df_ref, dk_scr, dv_scr, df_scr):
        j = pl.program_id(1)
        i = pl.program_id(2)

        @pl.when(i == 0)
        def _():
            dk_scr[...] = jnp.zeros_like(dk_scr)
            dv_scr[...] = jnp.zeros_like(dv_scr)
            df_scr[...] = jnp.zeros_like(df_scr)

        @pl.when(i >= j)
        def _():
            krow = lax.broadcasted_iota(jnp.int32, (tf, tf), 0)
            qcol = lax.broadcasted_iota(jnp.int32, (tf, tf), 1)
            keep = jnp.logical_or(i > j, qcol >= krow)
            dfacc = jnp.zeros((tf, LANES), F32)
            for h in range(FOX_HEADS):
                lo, hi = h * HEAD_DIM, (h + 1) * HEAD_DIM
                qh = q_ref[:, lo:hi]
                doh = do_ref[:, lo:hi]
                sc = lax.dot_general(k_ref[:, lo:hi], qh, NT_DIMS, preferred_element_type=F32) * QK_SCALE
                sc = sc + (ft_ref[h:h + 1, :] - lse_ref[h:h + 1, :]) - f_ref[:, h:h + 1]
                p = jnp.exp(jnp.where(keep, sc, NEG_BIG))
                dv_scr[:, lo:hi] += jnp.dot(p.astype(BF16), doh, preferred_element_type=F32)
                dp = lax.dot_general(v_ref[:, lo:hi], doh, NT_DIMS, preferred_element_type=F32)
                ds = p * (dp - dl_ref[h:h + 1, :])
                dk_scr[:, lo:hi] += jnp.dot(ds.astype(BF16), qh, preferred_element_type=F32)
                dfacc = dfacc - _lane_put((tf, LANES), h, jnp.sum(ds, axis=1, keepdims=True))
            df_scr[...] += dfacc

        @pl.when(i == n - 1)
        def _():
            dk_ref[...] = (dk_scr[...] * QK_SCALE).astype(BF16)
            dv_ref[...] = dv_scr[...].astype(BF16)
            df_ref[...] = df_scr[...]

    krow_map = lambda b, j, i: (b * n + j, 0)
    qrow_map = lambda b, j, i: (b * n + jnp.maximum(i, j), 0)
    qcol_map = lambda b, j, i: (b, jnp.maximum(i, j))
    return pl.pallas_call(
        body, name="fox_bwd_dkv", grid=(nb, n, n),
        in_specs=[pl.BlockSpec((tf, w), qrow_map), pl.BlockSpec((tf, w), krow_map),
                  pl.BlockSpec((tf, w), lambda b, j, i: (b * n + j, 1)),
                  pl.BlockSpec((tf, LANES), krow_map), pl.BlockSpec((16, tf), qcol_map),
                  pl.BlockSpec((tf, w), qrow_map), pl.BlockSpec((16, tf), qcol_map), pl.BlockSpec((16, tf), qcol_map)],
        out_specs=[pl.BlockSpec((tf, w), krow_map), pl.BlockSpec((tf, w), krow_map), pl.BlockSpec((tf, LANES), krow_map)],
        out_shape=[SDS((nb * s, w), BF16), SDS((nb * s, w), BF16), SDS((nb * s, LANES), F32)],
        scratch_shapes=[pltpu.VMEM((tf, w), F32), pltpu.VMEM((tf, w), F32), pltpu.VMEM((tf, LANES), F32)],
        compiler_params=_params(("parallel", "parallel", "arbitrary")))(pq, kv, kv, fcum, fcum_t, dcat_bf, lse_t, delta_t)


def reduce_adamw(parts, w, m, v, name):
    _, r, c = parts.shape
    tr = r
    for cand in range(8, r + 1, 8):
        if r % cand == 0 and cand * c <= 128 * 1024:
            tr = cand
    if r * c <= 128 * 1024:
        tr = r
    c1 = 1.0 - ADAM_B1 ** ADAM_STEP
    c2 = 1.0 - ADAM_B2 ** ADAM_STEP

    def body(p_ref, w_ref, m_ref, v_ref, g_out, d_out, m_out, v_out):
        g = p_ref[0]
        for k in range(1, N_DEV):
            g = g + p_ref[k]
        mn = ADAM_B1 * m_ref[...] + (1.0 - ADAM_B1) * g
        vn = ADAM_B2 * v_ref[...] + (1.0 - ADAM_B2) * (g * g)
        g_out[...] = g
        m_out[...] = mn
        v_out[...] = vn
        d_out[...] = -ADAM_LR * ((mn / c1) / (jnp.sqrt(vn / c2) + ADAM_EPS) + ADAM_WD * w_ref[...])

    row = pl.BlockSpec((tr, c), lambda i: (i, 0))
    return pl.pallas_call(
        body, name=name, grid=(r // tr,),
        in_specs=[pl.BlockSpec((N_DEV, tr, c), lambda i: (0, i, 0)), row, row, row],
        out_specs=[row, row, row, row], out_shape=[SDS((r, c), F32)] * 4,
        compiler_params=_params(("parallel",)))(parts, w, m, v)


def exchange(arrs, scatter, name):
    na = len(arrs)
    npeer = N_DEV - 1

    def body(*refs):
        ins = refs[:na]
        outs = refs[na:2 * na]
        send_sems, recv_sems, local_sems = refs[2 * na:]
        x, y, c = lax.axis_index("x"), lax.axis_index("y"), lax.axis_index("c")
        me = 4 * x + 2 * y + c
        peers = []
        for k in range(1, N_DEV):
            px = 1 - x if (k >> 2) & 1 else x
            py = 1 - y if (k >> 1) & 1 else y
            pc = 1 - c if k & 1 else c
            peers.append(((px, py, pc), 4 * px + 2 * py + pc))
        local = []
        remote = []
        for a in range(na):
            src_me = ins[a].at[me] if scatter else ins[a]
            lc = pltpu.make_async_copy(src_me, outs[a].at[me], local_sems.at[a])
            lc.start()
            local.append(lc)
            for k, (dev, idx) in enumerate(peers):
                cp = pltpu.make_async_remote_copy(
                    src_ref=ins[a].at[idx] if scatter else ins[a], dst_ref=outs[a].at[me],
                    send_sem=send_sems.at[a * npeer + k], recv_sem=recv_sems.at[a * npeer + k],
                    device_id=dev, device_id_type=pl.DeviceIdType.MESH)
                cp.start()
                remote.append(cp)
        for a in range(na):
            for k, (dev, idx) in enumerate(peers):
                src_me = ins[a].at[me] if scatter else ins[a]
                arrival = pltpu.make_async_remote_copy(
                    src_ref=src_me, dst_ref=outs[a].at[idx],
                    send_sem=send_sems.at[a * npeer + k], recv_sem=recv_sems.at[a * npeer + k],
                    device_id=dev, device_id_type=pl.DeviceIdType.MESH)
                arrival.wait_recv()
        for cp in remote:
            cp.wait_send()
        for lc in local:
            lc.wait()

    hbm = pl.BlockSpec(memory_space=pltpu.HBM)
    out_shape = [SDS(((a.shape[1:] if scatter else a.shape)), a.dtype) for a in arrs]
    out_shape = [SDS((N_DEV,) + s_.shape, s_.dtype) for s_ in out_shape]
    return pl.pallas_call(
        body, name=name, in_specs=[hbm] * na, out_specs=[hbm] * na, out_shape=out_shape,
        scratch_shapes=[pltpu.SemaphoreType.DMA((na * npeer,)), pltpu.SemaphoreType.DMA((na * npeer,)),
                        pltpu.SemaphoreType.DMA((na,))],
        )(*arrs)


def _block_diag(pw):
    out = jnp.zeros((TOK_WIDTH, TOK_WIDTH), pw.dtype)
    for g in range(4):
        out = lax.dynamic_update_slice(out, pw[g], (g * POOL_GROUP, g * POOL_GROUP))
    return out


def local_step(x, mem, target, wts, nb, s):
    g = {}
    saved = []
    mem_bf = mem.astype(BF16)
    xin = x
    xin_bf = x.astype(BF16)
    for l in range(DEPTH):
        sv = {"xin_bf": xin_bf}
        memkv = mm_nn(mem_bf, wts["memw"][l], BF16, f"memkv{l}")
        sv["memkv"] = memkv
        if l == 0:
            proj = mm_nn(xin_bf, wts["win_a"], F32, "proj_a")
            pooled, tok = pool_fwd(proj, wts["pw_bd"], wts["pscale"], nb, s)
            sv["pooled"] = pooled
        else:
            kv = mm_nn(xin_bf, wts["kvw"][:, :2 * TOK_WIDTH], BF16, "kv_proj")
            fl = mm_nn(xin_bf, wts["kvw"][:, 2 * TOK_WIDTH:], F32, "gate_proj")
            fneg = -fgate_fwd(fl, wts["fb"], nb, s)
            fneg_rows = _to_tile_rows(fneg, nb, s, min(TF, s))
            proj = mm_nn(xin_bf, wts["wq"], BF16, "proj_b")
            tok, o_f32, lse = fox_fwd(proj, kv, make_vaug(kv), fneg_rows, nb, s)
            sv.update(kv=kv, fl=fl, fneg=fneg, fneg_rows=fneg_rows, o_f32=o_f32, lse=lse)
        sv["proj"] = proj
        mem_out = memattn_fwd(proj, memkv, nb, s, f"memattn_fwd{l}")
        cat = jnp.concatenate([tok, mem_out], axis=1)
        sv["cat"] = cat
        mix = mm_nn(cat, wts["wout"][l], F32, f"out_proj{l}")
        x1, x1_bf, xh1, rs1 = ln_fwd(xin, mix, wts["ln1_g"][l], wts["ln1_b"][l], f"ln1_fwd{l}")
        sv.update(x1_bf=x1_bf, xh1=xh1, rs1=rs1)
        h = mm_nn(x1_bf, wts["wup"][l], F32, f"ffn_up{l}")
        act = convgate_fwd(h, wts["cw"][l], nb, s, f"convgate_fwd{l}")
        sv.update(h=h, act=act)
        ffn = mm_nn(act, wts["wdown"][l], F32, f"ffn_down{l}")
        x2, x2_bf, xh2, rs2 = ln_fwd(x1, ffn, wts["ln2_g"][l], wts["ln2_b"][l], f"ln2_fwd{l}")
        sv.update(xh2=xh2, rs2=rs2)
        saved.append(sv)
        xin, xin_bf = x2, x2_bf

    dy, loss_row = loss_head(xin, target)
    loss = loss_row[0, 0]

    for l in reversed(range(DEPTH)):
        sv = saved[l]
        dr2, dr2_bf, dg2, db2 = ln_bwd(dy, sv["xh2"], sv["rs2"], wts["ln2_g"][l], f"ln2_bwd{l}")
        g[f"ln2_g{l}"], g[f"ln2_b{l}"] = dg2, db2
        dact = mm_nn(dr2_bf, wts["wdown_t"][l], BF16, f"ffn_down_dx{l}")
        g[f"wdown{l}"] = mm_tn(sv["act"], dr2_bf, f"ffn_down_dw{l}")
        dh_u, dh_g, dcw_u, dcw_g = convgate_bwd(sv["h"], dact, wts["cw"][l], nb, s, f"convgate_bwd{l}")
        g[f"cw{l}"] = jnp.concatenate([dcw_u, dcw_g], axis=0)
        half = FF_PAIRS * FF_BLOCK_PAD
        dx1 = mm_nn(dh_u, wts["wup_t"][l][:half], F32, f"ffn_up_dx_u{l}", addend=dr2, add_scale=DN_ALPHA)
        dx1 = mm_nn(dh_g, wts["wup_t"][l][half:], F32, f"ffn_up_dx_g{l}", addend=dx1)
        g[f"wup{l}"] = jnp.concatenate([mm_tn(sv["x1_bf"], dh_u, f"ffn_up_dw_u{l}", blocked=True),
                                        mm_tn(sv["x1_bf"], dh_g, f"ffn_up_dw_g{l}", blocked=True)], axis=0)
        dr1, dr1_bf, dg1, db1 = ln_bwd(dx1, sv["xh1"], sv["rs1"], wts["ln1_g"][l], f"ln1_bwd{l}")
        g[f"ln1_g{l}"], g[f"ln1_b{l}"] = dg1, db1
        dcat, dcat_bf = mm_nn(dr1_bf, wts["wout_t"][l], F32, f"out_proj_dx{l}", also_bf16=True)
        g[f"wout{l}"] = mm_tn(sv["cat"], dr1_bf, f"out_proj_dw{l}")
        dqm, dmemkv = memattn_bwd(sv["proj"], sv["memkv"], dcat, nb, s, f"memattn_bwd{l}")
        g[f"memw{l}"] = mm_tn(mem_bf, dmemkv, f"memkv_dw{l}")
        if l == 0:
            dmixed, dpooled, dscale = pool_bwd_mix(dcat, sv["pooled"], wts["pw_bd"], wts["pw_bd_t"], wts["pscale"], nb, s)
            g["pscale"] = dscale
            g["pw_full"] = mm_tn(sv["pooled"], dmixed, "pool_dw")
            du = pool_bwd_window(dpooled, nb, s)
            dproj = jnp.concatenate([du, dqm], axis=1)
            dy = mm_nn(dproj, wts["win_a_t"], F32, "proj_a_dx", addend=dr1, add_scale=DN_ALPHA)
            g["win_a"] = mm_tn(sv["xin_bf"], dproj, "proj_a_dw")
        else:
            delta = fox_delta(dcat, sv["o_f32"], nb, s)
            tf = min(TF, s)
            dq, dfcum_q = fox_bwd_dq(sv["proj"], sv["kv"], sv["fneg_rows"], dcat_bf, sv["lse"], delta, nb, s)
            dk, dv, dfcum_k = fox_bwd_dkv(sv["proj"], sv["kv"], sv["fneg"], dcat_bf,
                                          _to_tile_rows(sv["lse"], nb, s, tf), _to_tile_rows(delta, nb, s, tf), nb, s)
            dfl, dfb = fgate_bwd(dfcum_q, dfcum_k, sv["fl"], wts["fb"], nb, s)
            g["fb"] = dfb
            dproj = jnp.concatenate([dq, dqm], axis=1)
            dkvf = jnp.concatenate([dk, dv, dfl.astype(BF16)], axis=1)
            dy = mm_nn(dproj, wts["wq_t"], F32, "proj_b_dx", addend=dr1, add_scale=DN_ALPHA)
            dy = mm_nn(dkvf, wts["kvw_t"], F32, "kv_proj_dx", addend=dy)
            g["wq"] = mm_tn(sv["xin_bf"], dproj, "proj_b_dw")
            g["kvw"] = mm_tn(sv["xin_bf"], dkvf, "kv_proj_dw")
    return loss, dy, g


def _to_head_rows(a, nb, s):
    return a.reshape(nb, s, LANES)[:, :, :16].transpose(0, 2, 1).reshape(nb * 16, s)


REP_ROWS = 168


def pack_replicated(pool_w, ln1_g, ln1_b, ln2_g, ln2_b, conv_b, f_b):
    cb = jnp.pad(conv_b, ((0, 0), (0, 6144 - 5504))).reshape(12, D_MODEL)
    fb = jnp.pad(f_b.reshape(1, FOX_HEADS), ((0, 3), (0, D_MODEL - FOX_HEADS)))
    return jnp.concatenate([pool_w.reshape(144, D_MODEL), ln1_g, ln1_b, ln2_g, ln2_b, cb, fb], axis=0)


def unpack_replicated(buf):
    pool_w = buf[:144].reshape(1, 4, POOL_GROUP, POOL_GROUP)
    ln = [buf[144 + 2 * k:146 + 2 * k] for k in range(4)]
    conv_b = buf[152:164].reshape(2, 6144)[:, :5504]
    f_b = buf[164, :FOX_HEADS]
    return pool_w, ln[0], ln[1], ln[2], ln[3], conv_b, f_b


def prepare_weights(gath, a_pool_w, f_b, ln1_g, ln1_b, ln2_g, ln2_b, ffn_conv_b):
    sq = gath["sq"]
    full = [sq[:, k].reshape(D_MODEL, D_MODEL) for k in range(4)]
    w = {"win_a": full[0], "win_a_t": full[0].T, "wq": full[2], "wq_t": full[2].T,
         "wout": [full[1], full[3]], "wout_t": [full[1].T, full[3].T]}
    kvw = gath["kvw"].reshape(D_MODEL, KV_COLS_PAD)
    w["kvw"], w["kvw_t"] = kvw, kvw.T
    memw = gath["memw"]
    w["memw"] = [memw[:, l].reshape(D_MODEL, 2 * MEM_WIDTH) for l in range(DEPTH)]
    pad_c = FF_BLOCK_PAD - FF_BLOCK
    wup = jnp.pad(gath["wup"], ((0, 0), (0, 0), (0, 0), (0, pad_c)))
    w["wup"] = [wup[:, l].transpose(1, 0, 2).reshape(D_MODEL, N_DEV * FF_BLOCK_PAD) for l in range(DEPTH)]
    w["wup_t"] = [m_.T for m_ in w["wup"]]
    wd = gath["wdown"]
    w["wdown"] = []
    for l in range(DEPTH):
        blocks = wd[:, l].reshape(FF_PAIRS, FF_BLOCK, D_MODEL)
        w["wdown"].append(jnp.pad(blocks, ((0, 0), (0, pad_c), (0, 0))).reshape(FF_PAIRS * FF_BLOCK_PAD, D_MODEL))
    w["wdown_t"] = [m_.T for m_ in w["wdown"]]
    small = gath["small"]
    cb = jnp.pad(ffn_conv_b.reshape(DEPTH, N_DEV, FF_BLOCK), ((0, 0), (0, 0), (0, pad_c)))
    w["cw"] = []
    for l in range(DEPTH):
        taps = small[:, 3 * l:3 * l + 3, :]
        w["cw"].append(jnp.concatenate([taps, cb[l][:, None, :], jnp.zeros((N_DEV, 4, FF_BLOCK_PAD), F32)], axis=1))
    w["pscale"] = small[:, 8, :96].reshape(1, TOK_WIDTH)
    pw_bd = _block_diag(a_pool_w[0])
    w["pw_bd"], w["pw_bd_t"] = pw_bd.astype(BF16), pw_bd.T.astype(BF16)
    w["fb"] = jnp.pad(f_b.reshape(1, FOX_HEADS), ((0, 0), (0, LANES - FOX_HEADS)))
    w["ln1_g"] = [ln1_g[l:l + 1] for l in range(DEPTH)]
    w["ln1_b"] = [ln1_b[l:l + 1] for l in range(DEPTH)]
    w["ln2_g"] = [ln2_g[l:l + 1] for l in range(DEPTH)]
    w["ln2_b"] = [ln2_b[l:l + 1] for l in range(DEPTH)]
    return w


def shard_weights_for_gather(a_w_in, a_w_out, b_w_q, b_w_out, kv_w, mem_w_kv, ffn_w_up, ffn_w_down, ffn_conv_w, a_pool_scale):
    sq = jnp.stack([a_w_in[0], a_w_out[0], b_w_q[0], b_w_out[0]], axis=0).astype(BF16)
    kvw = jnp.pad(kv_w, ((0, 0), (0, KV_COLS_PAD - KV_COLS))).astype(BF16)
    small = jnp.zeros((16, FF_BLOCK_PAD), F32)
    small = lax.dynamic_update_slice(small, ffn_conv_w.reshape(6, FF_BLOCK), (0, 0))
    small = lax.dynamic_update_slice(small, a_pool_scale, (8, 0))
    return {"sq": sq, "kvw": kvw, "memw": mem_w_kv.astype(BF16), "wup": ffn_w_up.astype(BF16),
            "wdown": ffn_w_down.astype(BF16), "small": small}


def grads_to_owner_blocks(g):
    out = {}
    out["sq"] = jnp.stack([g["win_a"], g["wout0"], g["wq"], g["wout1"]], axis=0).reshape(4, N_DEV, 128, D_MODEL).transpose(1, 0, 2, 3)
    out["kvw"] = g["kvw"][:, :KV_COLS].reshape(N_DEV, 128, KV_COLS)
    out["memw"] = jnp.stack([g["memw0"], g["memw1"]], axis=0).reshape(DEPTH, N_DEV, 128, 2 * MEM_WIDTH).transpose(1, 0, 2, 3).reshape(N_DEV, 256, 2 * MEM_WIDTH)
    out["wup"] = jnp.stack([g["wup0"][:, :, :FF_BLOCK], g["wup1"][:, :, :FF_BLOCK]], axis=1).reshape(N_DEV, DEPTH * D_MODEL, FF_BLOCK)
    wd = [g[f"wdown{l}"].reshape(FF_PAIRS, FF_BLOCK_PAD, D_MODEL)[:, :FF_BLOCK].reshape(N_DEV, 344, D_MODEL) for l in range(DEPTH)]
    out["wdown"] = jnp.stack(wd, axis=1).reshape(N_DEV, DEPTH * 344, D_MODEL)
    taps = jnp.stack([g[f"cw{l}"][:, :3, :] for l in range(DEPTH)], axis=1).reshape(N_DEV, 6, FF_BLOCK_PAD)
    small = jnp.zeros((N_DEV, 16, FF_BLOCK_PAD), F32)
    small = lax.dynamic_update_slice(small, taps, (0, 0, 0))
    small = lax.dynamic_update_slice(small, g["pscale"].reshape(N_DEV, 1, 96), (0, 8, 0))
    out["small"] = small
    return out


def replicated_grads(g):
    pw = jnp.stack([g["pw_full"][k * POOL_GROUP:(k + 1) * POOL_GROUP, k * POOL_GROUP:(k + 1) * POOL_GROUP] for k in range(4)])
    conv_b = jnp.stack([g[f"cw{l}"][:, 3, :FF_BLOCK].reshape(N_DEV * FF_BLOCK) for l in range(DEPTH)])
    ln = [jnp.concatenate([g[f"{n}{l}"] for l in range(DEPTH)], axis=0) for n in ("ln1_g", "ln1_b", "ln2_g", "ln2_b")]
    return pack_replicated(pw[None], ln[0], ln[1], ln[2], ln[3], conv_b, g["fb"][0, :FOX_HEADS])


def kernel(x, mem, a_w_in, a_pool_w, a_pool_scale, a_w_out, b_w_q, b_w_out, kv_w, f_b, mem_w_kv, ln1_g, ln1_b, ln2_g, ln2_b, ffn_w_up, ffn_conv_w, ffn_conv_b, ffn_w_down, loss_target, m_a_w_in, m_a_pool_w, m_a_pool_scale, m_a_w_out, m_b_w_q, m_b_w_out, m_kv_w, m_f_b, m_mem_w_kv, m_ln1_g, m_ln1_b, m_ln2_g, m_ln2_b, m_ffn_w_up, m_ffn_conv_w, m_ffn_conv_b, m_ffn_w_down, v_a_w_in, v_a_pool_w, v_a_pool_scale, v_a_w_out, v_b_w_q, v_b_w_out, v_kv_w, v_f_b, v_mem_w_kv, v_ln1_g, v_ln1_b, v_ln2_g, v_ln2_b, v_ffn_w_up, v_ffn_conv_w, v_ffn_conv_b, v_ffn_w_down):
    nb, s, d = x.shape
    t = nb * s

    shards = shard_weights_for_gather(a_w_in, a_w_out, b_w_q, b_w_out, kv_w, mem_w_kv, ffn_w_up, ffn_w_down,
                                      ffn_conv_w, a_pool_scale)
    names = ["sq", "kvw", "memw", "wup", "wdown", "small"]
    gathered = exchange([shards[n] for n in names], False, "gather_weights")
    wts = prepare_weights(dict(zip(names, gathered)), a_pool_w, f_b, ln1_g, ln1_b, ln2_g, ln2_b, ffn_conv_b)

    loss_local, grad_x, g = local_step(x.reshape(t, d), mem.reshape(nb * MEM_LEN, d), loss_target.reshape(t, d), wts, nb, s)
    loss = lax.psum(loss_local, ("x", "y", "c"))

    blocks = grads_to_owner_blocks(g)
    parts = dict(zip(names, exchange([blocks[n] for n in names], True, "scatter_grads")))
    rep_parts = exchange([replicated_grads(g)], False, "gather_small_grads")[0]

    def upd(pname, w2, m2, v2, kname):
        return reduce_adamw(parts[pname] if isinstance(pname, str) else pname, w2, m2, v2, kname)

    res = {}
    sq_w = [a_w_in, a_w_out, b_w_q, b_w_out]
    sq_m = [m_a_w_in, m_a_w_out, m_b_w_q, m_b_w_out]
    sq_v = [v_a_w_in, v_a_w_out, v_b_w_q, v_b_w_out]
    for k, nm in enumerate(["a_w_in", "a_w_out", "b_w_q", "b_w_out"]):
        outs = upd(parts["sq"][:, k], sq_w[k][0], sq_m[k][0], sq_v[k][0], f"adamw_{nm}")
        res[nm] = [o[None] for o in outs]
    res["kv_w"] = list(upd("kvw", kv_w, m_kv_w, v_kv_w, "adamw_kv_w"))
    outs = upd("memw", mem_w_kv.reshape(256, 512), m_mem_w_kv.reshape(256, 512), v_mem_w_kv.reshape(256, 512), "adamw_mem_w_kv")
    res["mem_w_kv"] = [o.reshape(DEPTH, 128, 512) for o in outs]
    r2 = DEPTH * D_MODEL
    outs = upd("wup", ffn_w_up.reshape(r2, FF_BLOCK), m_ffn_w_up.reshape(r2, FF_BLOCK), v_ffn_w_up.reshape(r2, FF_BLOCK), "adamw_ffn_w_up")
    res["ffn_w_up"] = [o.reshape(DEPTH, D_MODEL, FF_BLOCK) for o in outs]
    r3 = DEPTH * 344
    outs = upd("wdown", ffn_w_down.reshape(r3, d), m_ffn_w_down.reshape(r3, d), v_ffn_w_down.reshape(r3, d), "adamw_ffn_w_down")
    res["ffn_w_down"] = [o.reshape(DEPTH, 344, d) for o in outs]

    def small_pack(conv_w_, pscale_):
        buf = jnp.zeros((16, FF_BLOCK_PAD), F32)
        buf = lax.dynamic_update_slice(buf, conv_w_.reshape(6, FF_BLOCK), (0, 0))
        return lax.dynamic_update_slice(buf, pscale_, (8, 0))

    outs = upd("small", small_pack(ffn_conv_w, a_pool_scale), small_pack(m_ffn_conv_w, m_a_pool_scale),
               small_pack(v_ffn_conv_w, v_a_pool_scale), "adamw_small")
    res["ffn_conv_w"] = [o[:6, :FF_BLOCK].reshape(DEPTH, 3, FF_BLOCK) for o in outs]
    res["a_pool_scale"] = [o[8:9, :96] for o in outs]

    outs = upd(rep_parts, pack_replicated(a_pool_w, ln1_g, ln1_b, ln2_g, ln2_b, ffn_conv_b, f_b),
               pack_replicated(m_a_pool_w, m_ln1_g, m_ln1_b, m_ln2_g, m_ln2_b, m_ffn_conv_b, m_f_b),
               pack_replicated(v_a_pool_w, v_ln1_g, v_ln1_b, v_ln2_g, v_ln2_b, v_ffn_conv_b, v_f_b), "adamw_replicated")
    rep_names = ["a_pool_w", "ln1_g", "ln1_b", "ln2_g", "ln2_b", "ffn_conv_b", "f_b"]
    for nm in rep_names:
        res[nm] = []
    for o in outs:
        for nm, val in zip(rep_names, unpack_replicated(o)):
            res[nm].append(val)

    order = ["a_w_in", "a_pool_w", "a_pool_scale", "a_w_out", "b_w_q", "b_w_out", "kv_w", "f_b", "mem_w_kv",
             "ln1_g", "ln1_b", "ln2_g", "ln2_b", "ffn_w_up", "ffn_conv_w", "ffn_conv_b", "ffn_w_down"]
    out = [loss, grad_x.reshape(nb, s, d)]
    for kind in range(4):
        out.extend(res[nm][kind] for nm in order)
    return tuple(out)
```

```python
import jax
import jax.numpy as jnp
from jax import lax
from jax.experimental import pallas as pl
from jax.experimental.pallas import tpu as pltpu

F32 = jnp.float32
BF16 = jnp.bfloat16
SDS = jax.ShapeDtypeStruct

N_DEV = 8
D_MODEL = 1024
TOK_WIDTH = 768
MEM_WIDTH = 256
MEM_LEN = 256
MEM_HEADS = 4
HEAD_DIM = 64
FOX_HEADS = 12
POOL_GROUP = 192
FF_BLOCK = 688
FF_BLOCK_PAD = 768
FF_PAIRS = 4
FF_ROWS = 344
KV_COLS = 1548
KV_COLS_PAD = 1664
LANES = 128
DEPTH = 2
DN_ALPHA = (2.0 * DEPTH) ** 0.25
LN_EPS = 1e-5
QK_SCALE = HEAD_DIM ** -0.5
NEG_BIG = -1e30

ADAM_LR = 0.001
ADAM_B1 = 0.9
ADAM_B2 = 0.999
ADAM_EPS = 1e-08
ADAM_WD = 0.01
ADAM_STEP = 10

VMEM_LIMIT_BYTES = 56 * 1024 * 1024
TM = 512
TS = 256
TF = 256
TC = 256
HALO_POOL = 16
HALO_CONV = 8

NT_DIMS = (((1,), (1,)), ((), ()))
TN_DIMS = (((0,), (0,)), ((), ()))


def _params(sem=None):
    return pltpu.CompilerParams(dimension_semantics=sem, vmem_limit_bytes=VMEM_LIMIT_BYTES)


def _sigmoid(z):
    return 1.0 / (1.0 + jnp.exp(-z))


def _pick_tn(n):
    if n <= 2048:
        return n
    for t in (1024, 768, 512, 256, 128):
        if n % t == 0:
            return t
    return n


def mm_nn(a, b, out_dtype, name, addend=None, add_scale=1.0, also_bf16=False):
    m, k = a.shape
    _, n = b.shape
    tm = min(TM, m)
    tn = _pick_tn(n)
    tk = k if k <= 2048 else 1024
    nk = k // tk
    has_add = addend is not None

    def body(*refs):
        a_ref, b_ref = refs[0], refs[1]
        pos = 2
        c_ref = None
        if has_add:
            c_ref = refs[pos]
            pos += 1
        o_ref = refs[pos]
        ob_ref = refs[pos + 1] if also_bf16 else None
        acc = refs[-1]
        kk = pl.program_id(2)

        @pl.when(kk == 0)
        def _():
            acc[...] = jnp.zeros_like(acc)

        acc[...] += jnp.dot(a_ref[...].astype(BF16), b_ref[...].astype(BF16), preferred_element_type=F32)

        @pl.when(kk == nk - 1)
        def _():
            r = acc[...]
            if has_add:
                r = r + add_scale * c_ref[...]
            o_ref[...] = r.astype(out_dtype)
            if also_bf16:
                ob_ref[...] = r.astype(BF16)

    in_specs = [pl.BlockSpec((tm, tk), lambda i, j, kk: (i, kk)),
                pl.BlockSpec((tk, tn), lambda i, j, kk: (kk, j))]
    ops = [a, b]
    if has_add:
        in_specs.append(pl.BlockSpec((tm, tn), lambda i, j, kk: (i, j)))
        ops.append(addend)
    out_shape = [SDS((m, n), out_dtype)]
    out_specs = [pl.BlockSpec((tm, tn), lambda i, j, kk: (i, j))]
    if also_bf16:
        out_shape.append(SDS((m, n), BF16))
        out_specs.append(pl.BlockSpec((tm, tn), lambda i, j, kk: (i, j)))
    res = pl.pallas_call(
        body, name=name, grid=(m // tm, n // tn, nk), in_specs=in_specs, out_specs=out_specs, out_shape=out_shape,
        scratch_shapes=[pltpu.VMEM((tm, tn), F32)],
        compiler_params=_params(("parallel", "parallel", "arbitrary")))(*ops)
    return tuple(res) if also_bf16 else res[0]


def mm_tn(a, b, name, blocked=False):
    t, m = a.shape
    _, n = b.shape
    tt = min(TM, t)
    tm = 1024 if m % 1024 == 0 else m
    tn = FF_BLOCK_PAD if blocked else _pick_tn(n)
    nt = t // tt

    def body(a_ref, b_ref, o_ref):
        kk = pl.program_id(2)
        r = lax.dot_general(a_ref[...].astype(BF16), b_ref[...].astype(BF16), TN_DIMS, preferred_element_type=F32)
        if blocked:
            r = r[None]

        @pl.when(kk == 0)
        def _():
            o_ref[...] = r

        @pl.when(kk != 0)
        def _():
            o_ref[...] += r

    if blocked:
        out_shape = SDS((n // tn, m, tn), F32)
        out_spec = pl.BlockSpec((1, tm, tn), lambda i, j, kk: (j, i, 0))
    else:
        out_shape = SDS((m, n), F32)
        out_spec = pl.BlockSpec((tm, tn), lambda i, j, kk: (i, j))
    return pl.pallas_call(
        body, name=name, grid=(m // tm, n // tn, nt),
        in_specs=[pl.BlockSpec((tt, tm), lambda i, j, kk: (kk, i)), pl.BlockSpec((tt, tn), lambda i, j, kk: (kk, j))],
        out_specs=out_spec, out_shape=out_shape,
        compiler_params=_params(("parallel", "parallel", "arbitrary")))(a, b)


def ln_fwd(xprev, delta, g, b, name):
    t, d = xprev.shape
    tm = min(TM, t)

    def body(xp_ref, dl_ref, g_ref, b_ref, y_ref, yb_ref, xh_ref, rs_ref):
        r = DN_ALPHA * xp_ref[...] + dl_ref[...]
        mu = jnp.mean(r, axis=1, keepdims=True)
        xc = r - mu
        var = jnp.mean(xc * xc, axis=1, keepdims=True)
        rstd = lax.rsqrt(var + LN_EPS)
        xh = xc * rstd
        y = xh * g_ref[...] + b_ref[...]
        y_ref[...] = y
        yb_ref[...] = y.astype(BF16)
        xh_ref[...] = xh
        rs_ref[...] = jnp.broadcast_to(rstd, (tm, LANES))

    row = pl.BlockSpec((tm, d), lambda i: (i, 0))
    vec = pl.BlockSpec((1, d), lambda i: (0, 0))
    return pl.pallas_call(
        body, name=name, grid=(t // tm,), in_specs=[row, row, vec, vec],
        out_specs=[row, row, row, pl.BlockSpec((tm, LANES), lambda i: (i, 0))],
        out_shape=[SDS((t, d), F32), SDS((t, d), BF16), SDS((t, d), F32), SDS((t, LANES), F32)],
        compiler_params=_params(("parallel",)))(xprev, delta, g, b)


def ln_bwd(dy, xhat, rstd, g, name):
    t, d = dy.shape
    tm = min(TM, t)

    def body(dy_ref, xh_ref, rs_ref, g_ref, dr_ref, drb_ref, dg_ref, db_ref):
        i = pl.program_id(0)
        dyv = dy_ref[...]
        xh = xh_ref[...]
        dxh = dyv * g_ref[...]
        m1 = jnp.mean(dxh, axis=1, keepdims=True)
        m2 = jnp.mean(dxh * xh, axis=1, keepdims=True)
        dr = rs_ref[:, 0:1] * (dxh - m1 - xh * m2)
        dr_ref[...] = dr
        drb_ref[...] = dr.astype(BF16)

        @pl.when(i == 0)
        def _():
            dg_ref[...] = jnp.zeros_like(dg_ref)
            db_ref[...] = jnp.zeros_like(db_ref)

        dg_ref[...] += jnp.sum(dyv * xh, axis=0, keepdims=True)
        db_ref[...] += jnp.sum(dyv, axis=0, keepdims=True)

    row = pl.BlockSpec((tm, d), lambda i: (i, 0))
    vec = pl.BlockSpec((1, d), lambda i: (0, 0))
    return pl.pallas_call(
        body, name=name, grid=(t // tm,),
        in_specs=[row, row, pl.BlockSpec((tm, LANES), lambda i: (i, 0)), vec],
        out_specs=[row, row, vec, vec],
        out_shape=[SDS((t, d), F32), SDS((t, d), BF16), SDS((1, d), F32), SDS((1, d), F32)],
        compiler_params=_params(("arbitrary",)))(dy, xhat, rstd, g)


def loss_head(y, target):
    t, d = y.shape
    tm = min(TM, t)
    nsteps = t // tm

    def body(y_ref, t_ref, dy_ref, l_ref, acc):
        i = pl.program_id(0)
        diff = y_ref[...] - t_ref[...]
        dy_ref[...] = diff * (1.0 / d)

        @pl.when(i == 0)
        def _():
            acc[...] = jnp.zeros_like(acc)

        acc[...] += jnp.sum(diff * diff, axis=0, keepdims=True)

        @pl.when(i == nsteps - 1)
        def _():
            tot = jnp.sum(acc[...], axis=1, keepdims=True) * (0.5 / d)
            l_ref[...] = jnp.broadcast_to(tot, (1, LANES))

    row = pl.BlockSpec((tm, d), lambda i: (i, 0))
    return pl.pallas_call(
        body, name="loss_head", grid=(nsteps,), in_specs=[row, row],
        out_specs=[row, pl.BlockSpec((1, LANES), lambda i: (0, 0))],
        out_shape=[SDS((t, d), F32), SDS((1, LANES), F32)],
        scratch_shapes=[pltpu.VMEM((1, d), F32)],
        compiler_params=_params(("arbitrary",)))(y, target)


def memattn_fwd(proj, memkv, nb, s, name):
    ts = min(TS, s)
    nq = s // ts

    def body(q_ref, kv_ref, o_ref):
        for h in range(MEM_HEADS):
            lo, hi = h * HEAD_DIM, (h + 1) * HEAD_DIM
            qh = q_ref[:, lo:hi].astype(BF16)
            kh = kv_ref[:, lo:hi]
            vh = kv_ref[:, MEM_WIDTH + lo:MEM_WIDTH + hi]
            sc = lax.dot_general(qh, kh, NT_DIMS, preferred_element_type=F32) * QK_SCALE
            p = jnp.exp(sc - jnp.max(sc, axis=1, keepdims=True))
            p = p / jnp.sum(p, axis=1, keepdims=True)
            o_ref[:, lo:hi] = jnp.dot(p.astype(BF16), vh, preferred_element_type=F32).astype(BF16)

    return pl.pallas_call(
        body, name=name, grid=(nb, nq),
        in_specs=[pl.BlockSpec((ts, MEM_WIDTH), lambda b, i: (b * nq + i, 3)),
                  pl.BlockSpec((MEM_LEN, 2 * MEM_WIDTH), lambda b, i: (b, 0))],
        out_specs=pl.BlockSpec((ts, MEM_WIDTH), lambda b, i: (b * nq + i, 0)),
        out_shape=SDS((nb * s, MEM_WIDTH), BF16),
        compiler_params=_params(("parallel", "parallel")))(proj, memkv)


def memattn_bwd(proj, memkv, dcat, nb, s, name):
    ts = min(TS, s)
    nq = s // ts

    def body(q_ref, kv_ref, do_ref, dq_ref, dkv_ref):
        i = pl.program_id(1)

        @pl.when(i == 0)
        def _():
            dkv_ref[...] = jnp.zeros_like(dkv_ref)

        for h in range(MEM_HEADS):
            lo, hi = h * HEAD_DIM, (h + 1) * HEAD_DIM
            qh = q_ref[:, lo:hi].astype(BF16)
            kh = kv_ref[:, lo:hi]
            vh = kv_ref[:, MEM_WIDTH + lo:MEM_WIDTH + hi]
            doh = do_ref[:, lo:hi].astype(BF16)
            sc = lax.dot_general(qh, kh, NT_DIMS, preferred_element_type=F32) * QK_SCALE
            p = jnp.exp(sc - jnp.max(sc, axis=1, keepdims=True))
            p = p / jnp.sum(p, axis=1, keepdims=True)
            dv = lax.dot_general(p.astype(BF16), doh, TN_DIMS, preferred_element_type=F32)
            dp = lax.dot_general(doh, vh, NT_DIMS, preferred_element_type=F32)
            dl = jnp.sum(p * dp, axis=1, keepdims=True)
            ds = (p * (dp - dl) * QK_SCALE).astype(BF16)
            dq_ref[:, lo:hi] = jnp.dot(ds, kh, preferred_element_type=F32).astype(BF16)
            dkv_ref[:, lo:hi] += lax.dot_general(ds, qh, TN_DIMS, preferred_element_type=F32)
            dkv_ref[:, MEM_WIDTH + lo:MEM_WIDTH + hi] += dv

    return pl.pallas_call(
        body, name=name, grid=(nb, nq),
        in_specs=[pl.BlockSpec((ts, MEM_WIDTH), lambda b, i: (b * nq + i, 3)),
                  pl.BlockSpec((MEM_LEN, 2 * MEM_WIDTH), lambda b, i: (b, 0)),
                  pl.BlockSpec((ts, MEM_WIDTH), lambda b, i: (b * nq + i, 3))],
        out_specs=[pl.BlockSpec((ts, MEM_WIDTH), lambda b, i: (b * nq + i, 0)),
                   pl.BlockSpec((MEM_LEN, 2 * MEM_WIDTH), lambda b, i: (b, 0))],
        out_shape=[SDS((nb * s, MEM_WIDTH), BF16), SDS((nb * MEM_LEN, 2 * MEM_WIDTH), F32)],
        compiler_params=_params(("parallel", "arbitrary")))(proj, memkv, dcat)


def _pool_select(shape, s2, s4, s8, s16):
    lane = lax.broadcasted_iota(jnp.int32, shape, 1)
    return jnp.where(lane < POOL_GROUP, s2, jnp.where(lane < 2 * POOL_GROUP, s4, jnp.where(lane < 3 * POOL_GROUP, s8, s16)))


def _pool_count(shape, first_pos):
    pos = first_pos + lax.broadcasted_iota(jnp.int32, shape, 0)
    win = _pool_select(shape, 2, 4, 8, 16)
    return jnp.minimum(pos + 1, win).astype(F32)


def pool_fwd(proj, pw_bd, pscale, nb, s):
    ts = min(TS, s)
    nq = s // ts
    w = TOK_WIDTH

    def body(c_ref, h_ref, w_ref, sc_ref, pooled_ref, tok_ref):
        i = pl.program_id(0) % nq
        cur = c_ref[...]
        halo = jnp.where(i == 0, 0.0, h_ref[...])
        xe = jnp.concatenate([halo, cur], axis=0)
        s2 = xe + pltpu.roll(xe, 1, axis=0)
        s4 = s2 + pltpu.roll(s2, 2, axis=0)
        s8 = s4 + pltpu.roll(s4, 4, axis=0)
        s16 = s8 + pltpu.roll(s8, 8, axis=0)
        hp = HALO_POOL
        ws = _pool_select((ts, w), s2[hp:], s4[hp:], s8[hp:], s16[hp:])
        pooled = (ws / _pool_count((ts, w), i * ts) - cur).astype(BF16)
        pooled_ref[...] = pooled
        mixed = jnp.dot(pooled, w_ref[...], preferred_element_type=F32)
        tok_ref[...] = (mixed * sc_ref[...]).astype(BF16)

    row = pl.BlockSpec((ts, w), lambda r: (r, 0))
    return pl.pallas_call(
        body, name="pool_fwd", grid=(nb * nq,),
        in_specs=[row, pl.BlockSpec((HALO_POOL, w), lambda r: (jnp.maximum(r * (ts // HALO_POOL) - 1, 0), 0)),
                  pl.BlockSpec((w, w), lambda r: (0, 0)), pl.BlockSpec((1, w), lambda r: (0, 0))],
        out_specs=[row, row], out_shape=[SDS((nb * s, w), BF16), SDS((nb * s, w), BF16)],
        compiler_params=_params(("parallel",)))(proj, proj, pw_bd, pscale)


def pool_bwd_mix(dcat, pooled, pw_bd, pw_bd_t, pscale, nb, s):
    ts = min(TS, s)
    w = TOK_WIDTH

    def body(dt_ref, p_ref, w_ref, wt_ref, sc_ref, dm_ref, dp_ref, ds_ref):
        r = pl.program_id(0)
        dtok = dt_ref[...]
        mixed = jnp.dot(p_ref[...], w_ref[...], preferred_element_type=F32)

        @pl.when(r == 0)
        def _():
            ds_ref[...] = jnp.zeros_like(ds_ref)

        ds_ref[...] += jnp.sum(dtok * mixed, axis=0, keepdims=True)
        dmx = (dtok * sc_ref[...]).astype(BF16)
        dm_ref[...] = dmx
        dp_ref[...] = jnp.dot(dmx, wt_ref[...], preferred_element_type=F32)

    row = pl.BlockSpec((ts, w), lambda r: (r, 0))
    mat = pl.BlockSpec((w, w), lambda r: (0, 0))
    vec = pl.BlockSpec((1, w), lambda r: (0, 0))
    return pl.pallas_call(
        body, name="pool_bwd_mix", grid=(nb * s // ts,), in_specs=[row, row, mat, mat, vec],
        out_specs=[row, row, vec], out_shape=[SDS((nb * s, w), BF16), SDS((nb * s, w), F32), SDS((1, w), F32)],
        compiler_params=_params(("arbitrary",)))(dcat, pooled, pw_bd, pw_bd_t, pscale)


def pool_bwd_window(dpooled, nb, s):
    ts = min(TS, s)
    nq = s // ts
    w = TOK_WIDTH
    n_ext = ts + HALO_POOL
    n_halo_blocks = nb * s // HALO_POOL

    def body(c_ref, n_ref, du_ref):
        i = pl.program_id(0) % nq
        cur = c_ref[...]
        nxt = jnp.where(i == nq - 1, 0.0, n_ref[...])
        ze = jnp.concatenate([cur, nxt], axis=0) / _pool_count((n_ext, w), i * ts)
        s2 = ze + pltpu.roll(ze, n_ext - 1, axis=0)
        s4 = s2 + pltpu.roll(s2, n_ext - 2, axis=0)
        s8 = s4 + pltpu.roll(s4, n_ext - 4, axis=0)
        s16 = s8 + pltpu.roll(s8, n_ext - 8, axis=0)
        ws = _pool_select((ts, w), s2[:ts], s4[:ts], s8[:ts], s16[:ts])
        du_ref[...] = (ws - cur).astype(BF16)

    row = pl.BlockSpec((ts, w), lambda r: (r, 0))
    return pl.pallas_call(
        body, name="pool_bwd_window", grid=(nb * nq,),
        in_specs=[row, pl.BlockSpec((HALO_POOL, w),
                                    lambda r: (jnp.minimum((r + 1) * (ts // HALO_POOL), n_halo_blocks - 1), 0))],
        out_specs=row, out_shape=SDS((nb * s, w), BF16),
        compiler_params=_params(("parallel",)))(dpooled, dpooled)


def _conv_rows(xe, w_ref):
    return (w_ref[0, 2:3, :] * xe + w_ref[0, 1:2, :] * pltpu.roll(xe, 1, axis=0)
            + w_ref[0, 0:1, :] * pltpu.roll(xe, 2, axis=0) + w_ref[0, 3:4, :])


def convgate_fwd(h, cw, nb, s, name):
    ts = min(TS, s)
    nq = s // ts
    w = FF_BLOCK_PAD
    hc = HALO_CONV

    def body(uc_ref, uh_ref, gc_ref, gh_ref, wu_ref, wg_ref, o_ref):
        first = (pl.program_id(0) % nq) == 0
        xu = jnp.concatenate([jnp.where(first, 0.0, uh_ref[...]), uc_ref[...]], axis=0)
        xg = jnp.concatenate([jnp.where(first, 0.0, gh_ref[...]), gc_ref[...]], axis=0)
        cu = _conv_rows(xu, wu_ref)[hc:]
        cg = _conv_rows(xg, wg_ref)[hc:]
        o_ref[...] = (cg * _sigmoid(cg) * cu).astype(BF16)

    def cur(off):
        return pl.BlockSpec((ts, w), lambda r, j: (r, j + off))

    def halo(off):
        return pl.BlockSpec((hc, w), lambda r, j: (jnp.maximum(r * (ts // hc) - 1, 0), j + off))

    def wspec(off):
        return pl.BlockSpec((1, 8, w), lambda r, j: (j + off, 0, 0))

    return pl.pallas_call(
        body, name=name, grid=(nb * nq, FF_PAIRS),
        in_specs=[cur(0), halo(0), cur(FF_PAIRS), halo(FF_PAIRS), wspec(0), wspec(FF_PAIRS)],
        out_specs=pl.BlockSpec((ts, w), lambda r, j: (r, j)), out_shape=SDS((nb * s, FF_PAIRS * w), BF16),
        compiler_params=_params(("parallel", "parallel")))(h, h, h, h, cw, cw)


def convgate_bwd(h, dact, cw, nb, s, name):
    ts = min(TS, s)
    nq = s // ts
    w = FF_BLOCK_PAD
    hc = HALO_CONV
    n_ext = ts + hc
    n_halo_blocks = nb * s // hc

    def body(uc_ref, up_ref, un_ref, gc_ref, gp_ref, gn_ref, dc_ref, dn_ref, wu_ref, wg_ref,
             dhu_ref, dhg_ref, dwu_ref, dwg_ref):
        r = pl.program_id(1)
        i = r % nq
        first = i == 0
        last = i == nq - 1
        xu = jnp.concatenate([jnp.where(first, 0.0, up_ref[...]), uc_ref[...], un_ref[...]], axis=0)
        xg = jnp.concatenate([jnp.where(first, 0.0, gp_ref[...]), gc_ref[...], gn_ref[...]], axis=0)
        cu = _conv_rows(xu, wu_ref)[hc:]
        cg = _conv_rows(xg, wg_ref)[hc:]
        da = jnp.concatenate([dc_ref[...].astype(F32), jnp.where(last, 0.0, dn_ref[...].astype(F32)[:hc])], axis=0)
        sg = _sigmoid(cg)
        dcu = da * (cg * sg)
        dcg = da * cu * (sg * (1.0 + cg * (1.0 - sg)))

        def conv_t(dcv, w_ref):
            return (w_ref[0, 2:3, :] * dcv + w_ref[0, 1:2, :] * pltpu.roll(dcv, n_ext - 1, axis=0)
                    + w_ref[0, 0:1, :] * pltpu.roll(dcv, n_ext - 2, axis=0))[:ts]

        dhu_ref[...] = conv_t(dcu, wu_ref).astype(BF16)
        dhg_ref[...] = conv_t(dcg, wg_ref).astype(BF16)

        def tap_grads(xe, dcv):
            d0 = dcv[:ts]
            x0 = xe[hc:hc + ts]
            x1 = pltpu.roll(xe, 1, axis=0)[hc:hc + ts]
            x2 = pltpu.roll(xe, 2, axis=0)[hc:hc + ts]
            rows = [jnp.sum(d0 * x2, axis=0, keepdims=True), jnp.sum(d0 * x1, axis=0, keepdims=True),
                    jnp.sum(d0 * x0, axis=0, keepdims=True), jnp.sum(d0, axis=0, keepdims=True)]
            sub = lax.broadcasted_iota(jnp.int32, (8, w), 0)
            upd = jnp.zeros((8, w), F32)
            for k, rv in enumerate(rows):
                upd = jnp.where(sub == k, rv, upd)
            return upd[None]

        @pl.when(r == 0)
        def _():
            dwu_ref[...] = jnp.zeros_like(dwu_ref)
            dwg_ref[...] = jnp.zeros_like(dwg_ref)

        dwu_ref[...] += tap_grads(xu, dcu)
        dwg_ref[...] += tap_grads(xg, dcg)

    def cur(off):
        return pl.BlockSpec((ts, w), lambda j, r: (r, j + off))

    def prev(off):
        return pl.BlockSpec((hc, w), lambda j, r: (jnp.maximum(r * (ts // hc) - 1, 0), j + off))

    def nxt(off):
        return pl.BlockSpec((hc, w), lambda j, r: (jnp.minimum((r + 1) * (ts // hc), n_halo_blocks - 1), j + off))

    def wspec(off):
        return pl.BlockSpec((1, 8, w), lambda j, r: (j + off, 0, 0))

    hb = 2 * hc
    dact_next = pl.BlockSpec((hb, w), lambda j, r: (jnp.minimum((r + 1) * (ts // hb), nb * s // hb - 1), j))

    p = FF_PAIRS
    dh_spec = pl.BlockSpec((ts, w), lambda j, r: (r, j))
    dw_spec = pl.BlockSpec((1, 8, w), lambda j, r: (j, 0, 0))
    return pl.pallas_call(
        body, name=name, grid=(p, nb * nq),
        in_specs=[cur(0), prev(0), nxt(0), cur(p), prev(p), nxt(p), cur(0), dact_next, wspec(0), wspec(p)],
        out_specs=[dh_spec, dh_spec, dw_spec, dw_spec],
        out_shape=[SDS((nb * s, p * w), BF16), SDS((nb * s, p * w), BF16), SDS((p, 8, w), F32), SDS((p, 8, w), F32)],
        compiler_params=_params(("parallel", "arbitrary")))(h, h, h, h, h, h, dact, dact, cw, cw)


def _tri(n, upper):
    r = lax.broadcasted_iota(jnp.int32, (n, n), 0)
    c = lax.broadcasted_iota(jnp.int32, (n, n), 1)
    return ((r <= c) if upper else (r >= c)).astype(F32)


def fgate_fwd(fl, fb, nb, s):
    tc = min(TC, s)
    nq = s // tc

    def body(fl_ref, fb_ref, f_ref, carry):
        @pl.when(pl.program_id(1) == 0)
        def _():
            carry[...] = jnp.zeros_like(carry)

        z = fl_ref[...] + fb_ref[...]
        logf = jnp.minimum(z, 0.0) - jnp.log(1.0 + jnp.exp(-jnp.abs(z)))
        f_ref[...] = jnp.dot(_tri(tc, False), logf, preferred_element_type=F32,
                             precision=lax.Precision.HIGHEST) + carry[...]
        carry[...] += jnp.sum(logf, axis=0, keepdims=True)

    row = pl.BlockSpec((tc, LANES), lambda b, i: (b * nq + i, 0))
    return pl.pallas_call(
        body, name="fgate_fwd", grid=(nb, nq), in_specs=[row, pl.BlockSpec((1, LANES), lambda b, i: (0, 0))],
        out_specs=row, out_shape=SDS((nb * s, LANES), F32), scratch_shapes=[pltpu.VMEM((1, LANES), F32)],
        compiler_params=_params(("arbitrary", "arbitrary")))(fl, fb)


def fgate_bwd(d_cum_q, d_cum_k, fl, fb, nb, s):
    tc = min(TC, s)
    nq = s // tc

    def body(dfq_ref, dfk_ref, fl_ref, fb_ref, dfl_ref, dfb_ref, carry):
        b = pl.program_id(0)
        i = pl.program_id(1)

        @pl.when(i == 0)
        def _():
            carry[...] = jnp.zeros_like(carry)

        @pl.when(jnp.logical_and(b == 0, i == 0))
        def _():
            dfb_ref[...] = jnp.zeros_like(dfb_ref)

        dfv = dfq_ref[...] + dfk_ref[...]
        dlog = jnp.dot(_tri(tc, True), dfv, preferred_element_type=F32,
                       precision=lax.Precision.HIGHEST) + carry[...]
        carry[...] += jnp.sum(dfv, axis=0, keepdims=True)
        z = fl_ref[...] + fb_ref[...]
        dfl = dlog / (1.0 + jnp.exp(z))
        dfl_ref[...] = dfl
        dfb_ref[...] += jnp.sum(dfl, axis=0, keepdims=True)

    row = pl.BlockSpec((tc, LANES), lambda b, i: (b * nq + nq - 1 - i, 0))
    vec = pl.BlockSpec((1, LANES), lambda b, i: (0, 0))
    return pl.pallas_call(
        body, name="fgate_bwd", grid=(nb, nq), in_specs=[row, row, row, vec], out_specs=[row, vec],
        out_shape=[SDS((nb * s, LANES), F32), SDS((1, LANES), F32)], scratch_shapes=[pltpu.VMEM((1, LANES), F32)],
        compiler_params=_params(("arbitrary", "arbitrary")))(d_cum_q, d_cum_k, fl, fb)


PAIR = 2 * HEAD_DIM
N_PAIRS = FOX_HEADS // 2


def _lane_put(shape, h, col):
    lane = lax.broadcasted_iota(jnp.int32, shape, 1)
    return jnp.where(lane == h, col, 0.0)


def _half_masks(rows):
    lane = lax.broadcasted_iota(jnp.int32, (rows, PAIR), 1)
    return lane < HEAD_DIM


def _split_pair(x, scale=None):
    if scale is not None:
        x = x * scale
    lo = _half_masks(x.shape[0])
    zero = jnp.zeros_like(x)
    return jnp.where(lo, x, zero), jnp.where(lo, zero, x)


def make_vaug(kv):
    t = kv.shape[0]
    v = kv[:, TOK_WIDTH:].reshape(t, N_PAIRS, 2, HEAD_DIM)
    one = jnp.ones((t, N_PAIRS, 1), kv.dtype)
    z63 = jnp.zeros((t, N_PAIRS, HEAD_DIM - 1), kv.dtype)
    even = jnp.concatenate([v[:, :, 0], one, z63], axis=-1)
    odd = jnp.concatenate([one, z63, v[:, :, 1]], axis=-1)
    return jnp.stack([even, odd], axis=2).reshape(t, FOX_HEADS * PAIR)


def _to_tile_rows(a, nb, s, tf):
    return a.reshape(nb * s // tf, tf, LANES)[:, :, :16].transpose(0, 2, 1)


def fox_fwd(pq, kv, vaug, fneg_rows, nb, s):
    tf = min(TF, s)
    n = s // tf
    w = TOK_WIDTH

    def body(q_ref, k_ref, v_ref, ft_ref, ob_ref, of_ref, lse_ref, qm_scr, m_scr, acc_scr):
        i = pl.program_id(1)
        for p in range(N_PAIRS):
            qe, qo = _split_pair(q_ref[:, p * PAIR:(p + 1) * PAIR], QK_SCALE)
            qm_scr[2 * p] = qe
            qm_scr[2 * p + 1] = qo
        m_scr[...] = jnp.full(m_scr.shape, NEG_BIG, F32)
        acc_scr[...] = jnp.zeros_like(acc_scr)

        def tile(j, masked):
            ks = pl.multiple_of(j * tf, tf)
            if masked:
                keep = lax.broadcasted_iota(jnp.int32, (tf, tf), 0) >= lax.broadcasted_iota(jnp.int32, (tf, tf), 1)
            for h in range(FOX_HEADS):
                p = h // 2
                kp = k_ref[pl.ds(ks, tf), p * PAIR:(p + 1) * PAIR]
                sc = lax.dot_general(qm_scr[h], kp, NT_DIMS, preferred_element_type=F32) + ft_ref[j, h:h + 1, :]
                if masked:
                    sc = jnp.where(keep, sc, NEG_BIG)
                m_prev = m_scr[h]
                m_new = jnp.maximum(m_prev, jnp.max(sc, axis=1, keepdims=True))
                pr = jnp.exp(sc - m_new).astype(BF16)
                pv = jnp.dot(pr, v_ref[pl.ds(ks, tf), h * PAIR:(h + 1) * PAIR], preferred_element_type=F32)
                acc_scr[h] = jnp.exp(m_prev - m_new) * acc_scr[h] + pv
                m_scr[h] = m_new

        def step(j, carry):
            tile(j, False)
            return carry

        lax.fori_loop(0, i, step, 0)
        tile(i, True)

        lo = _half_masks(tf)
        lse = jnp.zeros((tf, LANES), F32)
        for p in range(N_PAIRS):
            he, ho = 2 * p, 2 * p + 1
            le = acc_scr[he, :, HEAD_DIM:HEAD_DIM + 1]
            lod = acc_scr[ho, :, 0:1]
            o = jnp.where(lo, acc_scr[he] / le, acc_scr[ho] / lod)
            ob_ref[:, p * PAIR:(p + 1) * PAIR] = o.astype(BF16)
            of_ref[:, p * PAIR:(p + 1) * PAIR] = o
            lse = lse + _lane_put((tf, LANES), he, m_scr[he] + jnp.log(le))
            lse = lse + _lane_put((tf, LANES), ho, m_scr[ho] + jnp.log(lod))
        lse_ref[...] = lse

    qrow = lambda b, i: (b * n + i, 0)
    return pl.pallas_call(
        body, name="fox_fwd", grid=(nb, n),
        in_specs=[pl.BlockSpec((tf, w), qrow),
                  pl.BlockSpec((s, w), lambda b, i: (b, 0)),
                  pl.BlockSpec((s, FOX_HEADS * PAIR), lambda b, i: (b, 0)),
                  pl.BlockSpec((n, 16, tf), lambda b, i: (b, 0, 0))],
        out_specs=[pl.BlockSpec((tf, w), qrow), pl.BlockSpec((tf, w), qrow), pl.BlockSpec((tf, LANES), qrow)],
        out_shape=[SDS((nb * s, w), BF16), SDS((nb * s, w), F32), SDS((nb * s, LANES), F32)],
        scratch_shapes=[pltpu.VMEM((FOX_HEADS, tf, PAIR), BF16), pltpu.VMEM((FOX_HEADS, tf, 1), F32),
                        pltpu.VMEM((FOX_HEADS, tf, PAIR), F32)],
        compiler_params=_params(("parallel", "arbitrary")))(pq, kv, vaug, fneg_rows)


def fox_delta(dcat, o, nb, s):
    tf = min(TM, s)
    w = TOK_WIDTH

    def body(do_ref, o_ref, dl_ref):
        out = jnp.zeros((tf, LANES), F32)
        for h in range(FOX_HEADS):
            lo, hi = h * HEAD_DIM, (h + 1) * HEAD_DIM
            out = out + _lane_put((tf, LANES), h, jnp.sum(do_ref[:, lo:hi] * o_ref[:, lo:hi], axis=1, keepdims=True))
        dl_ref[...] = out

    row = pl.BlockSpec((tf, w), lambda r: (r, 0))
    return pl.pallas_call(
        body, name="fox_delta", grid=(nb * s // tf,), in_specs=[row, row],
        out_specs=pl.BlockSpec((tf, LANES), lambda r: (r, 0)), out_shape=SDS((nb * s, LANES), F32),
        compiler_params=_params(("parallel",)))(dcat, o)


def fox_bwd_dq(pq, kv, fneg_rows, dcat_bf, lse, delta, nb, s):
    tf = min(TF, s)
    n = s // tf
    w = TOK_WIDTH

    def body(q_ref, k_ref, v_ref, ft_ref, do_ref, lse_ref, dl_ref, dq_ref, df_ref, qm_scr, dom_scr, acc_scr, rs_scr):
        i = pl.program_id(1)
        for p in range(N_PAIRS):
            qe, qo = _split_pair(q_ref[:, p * PAIR:(p + 1) * PAIR], QK_SCALE)
            qm_scr[2 * p] = qe
            qm_scr[2 * p + 1] = qo
            de, dod = _split_pair(do_ref[:, p * PAIR:(p + 1) * PAIR])
            dom_scr[2 * p] = de
            dom_scr[2 * p + 1] = dod
        acc_scr[...] = jnp.zeros_like(acc_scr)
        rs_scr[...] = jnp.zeros_like(rs_scr)

        def tile(j, masked):
            ks = pl.multiple_of(j * tf, tf)
            if masked:
                keep = lax.broadcasted_iota(jnp.int32, (tf, tf), 0) >= lax.broadcasted_iota(jnp.int32, (tf, tf), 1)
            for h in range(FOX_HEADS):
                p = h // 2
                kp = k_ref[pl.ds(ks, tf), p * PAIR:(p + 1) * PAIR]
                vp = v_ref[pl.ds(ks, tf), p * PAIR:(p + 1) * PAIR]
                sc = lax.dot_general(qm_scr[h], kp, NT_DIMS, preferred_element_type=F32) + ft_ref[j, h:h + 1, :]
                sc = sc - lse_ref[:, h:h + 1]
                if masked:
                    sc = jnp.where(keep, sc, NEG_BIG)
                pr = jnp.exp(sc)
                dp = lax.dot_general(dom_scr[h], vp, NT_DIMS, preferred_element_type=F32)
                ds = pr * (dp - dl_ref[:, h:h + 1])
                part = ds[:, :LANES]
                for c in range(1, tf // LANES):
                    part = part + ds[:, c * LANES:(c + 1) * LANES]
                rs_scr[h] += part
                acc_scr[h] += jnp.dot(ds.astype(BF16), kp, preferred_element_type=F32)

        def step(j, carry):
            tile(j, False)
            return carry

        lax.fori_loop(0, i, step, 0)
        tile(i, True)

        lo = _half_masks(tf)
        dfq = jnp.zeros((tf, LANES), F32)
        for p in range(N_PAIRS):
            dq = jnp.where(lo, acc_scr[2 * p], acc_scr[2 * p + 1]) * QK_SCALE
            dq_ref[:, p * PAIR:(p + 1) * PAIR] = dq.astype(BF16)
            for h in (2 * p, 2 * p + 1):
                dfq = dfq + _lane_put((tf, LANES), h, jnp.sum(rs_scr[h], axis=1, keepdims=True))
        df_ref[...] = dfq

    qrow = lambda b, i: (b * n + i, 0)
    stat = pl.BlockSpec((tf, LANES), qrow)
    return pl.pallas_call(
        body, name="fox_bwd_dq", grid=(nb, n),
        in_specs=[pl.BlockSpec((tf, w), qrow), pl.BlockSpec((s, w), lambda b, i: (b, 0)),
                  pl.BlockSpec((s, w), lambda b, i: (b, 1)), pl.BlockSpec((n, 16, tf), lambda b, i: (b, 0, 0)),
                  pl.BlockSpec((tf, w), qrow), stat, stat],
        out_specs=[pl.BlockSpec((tf, w), qrow), stat],
        out_shape=[SDS((nb * s, w), BF16), SDS((nb * s, LANES), F32)],
        scratch_shapes=[pltpu.VMEM((FOX_HEADS, tf, PAIR), BF16), pltpu.VMEM((FOX_HEADS, tf, PAIR), BF16),
                        pltpu.VMEM((FOX_HEADS, tf, PAIR), F32), pltpu.VMEM((FOX_HEADS, tf, LANES), F32)],
        compiler_params=_params(("parallel", "arbitrary")))(pq, kv, kv, fneg_rows, dcat_bf, lse, delta)


def fox_bwd_dkv(pq, kv, fneg, dcat_bf, lse_rows, delta_rows, nb, s):
    tf = min(TF, s)
    n = s // tf
    w = TOK_WIDTH

    def body(q_ref, k_ref, v_ref, f_ref, do_ref, lse_ref, dl_ref, dk_ref, dv_ref, df_ref,
             km_scr, vm_scr, fk_scr, dk_scr, dv_scr, rs_scr):
        j = pl.program_id(1)
        for p in range(N_PAIRS):
            ke, ko = _split_pair(k_ref[:, p * PAIR:(p + 1) * PAIR], QK_SCALE)
            km_scr[2 * p] = ke
            km_scr[2 * p + 1] = ko
            ve, vo = _split_pair(v_ref[:, p * PAIR:(p + 1) * PAIR])
            vm_scr[2 * p] = ve
            vm_scr[2 * p + 1] = vo
        for h in range(FOX_HEADS):
            fk_scr[h] = jnp.broadcast_to(f_ref[:, h:h + 1], (tf, tf))
        dk_scr[...] = jnp.zeros_like(dk_scr)
        dv_scr[...] = jnp.zeros_like(dv_scr)
        rs_scr[...] = jnp.zeros_like(rs_scr)

        def tile(i, masked):
            qs = pl.multiple_of(i * tf, tf)
            if masked:
                keep = lax.broadcasted_iota(jnp.int32, (tf, tf), 1) >= lax.broadcasted_iota(jnp.int32, (tf, tf), 0)
            for h in range(FOX_HEADS):
                p = h // 2
                qp = q_ref[pl.ds(qs, tf), p * PAIR:(p + 1) * PAIR]
                dop = do_ref[pl.ds(qs, tf), p * PAIR:(p + 1) * PAIR]
                sc = lax.dot_general(km_scr[h], qp, NT_DIMS, preferred_element_type=F32) + fk_scr[h]
                sc = sc - lse_ref[i, h:h + 1, :]
                if masked:
                    sc = jnp.where(keep, sc, NEG_BIG)
                pr = jnp.exp(sc)
                dv_scr[h] += jnp.dot(pr.astype(BF16), dop, preferred_element_type=F32)
                dp = lax.dot_general(vm_scr[h], dop, NT_DIMS, preferred_element_type=F32)
                ds = pr * (dp - dl_ref[i, h:h + 1, :])
                part = ds[:, :LANES]
                for c in range(1, tf // LANES):
                    part = part + ds[:, c * LANES:(c + 1) * LANES]
                rs_scr[h] += part
                dk_scr[h] += jnp.dot(ds.astype(BF16), qp, preferred_element_type=F32)

        def step(i, carry):
            tile(i, False)
            return carry

        tile(j, True)
        lax.fori_loop(j + 1, n, step, 0)

        lo = _half_masks(tf)
        dfk = jnp.zeros((tf, LANES), F32)
        for p in range(N_PAIRS):
            dk = jnp.where(lo, dk_scr[2 * p], dk_scr[2 * p + 1]) * QK_SCALE
            dk_ref[:, p * PAIR:(p + 1) * PAIR] = dk.astype(BF16)
            dv_ref[:, p * PAIR:(p + 1) * PAIR] = jnp.where(lo, dv_scr[2 * p], dv_scr[2 * p + 1]).astype(BF16)
            for h in (2 * p, 2 * p + 1):
                dfk = dfk - _lane_put((tf, LANES), h, jnp.sum(rs_scr[h], axis=1, keepdims=True))
        df_ref[...] = dfk

    krow = lambda b, j: (b * n + j, 0)
    rows = pl.BlockSpec((n, 16, tf), lambda b, j: (b, 0, 0))
    return pl.pallas_call(
        body, name="fox_bwd_dkv", grid=(nb, n),
        in_specs=[pl.BlockSpec((s, w), lambda b, j: (b, 0)), pl.BlockSpec((tf, w), krow),
                  pl.BlockSpec((tf, w), lambda b, j: (b * n + j, 1)), pl.BlockSpec((tf, LANES), krow),
                  pl.BlockSpec((s, w), lambda b, j: (b, 0)), rows, rows],
        out_specs=[pl.BlockSpec((tf, w), krow), pl.BlockSpec((tf, w), krow), pl.BlockSpec((tf, LANES), krow)],
        out_shape=[SDS((nb * s, w), BF16), SDS((nb * s, w), BF16), SDS((nb * s, LANES), F32)],
        scratch_shapes=[pltpu.VMEM((FOX_HEADS, tf, PAIR), BF16), pltpu.VMEM((FOX_HEADS, tf, PAIR), BF16),
                        pltpu.VMEM((FOX_HEADS, tf, tf), F32), pltpu.VMEM((FOX_HEADS, tf, PAIR), F32),
                        pltpu.VMEM((FOX_HEADS, tf, PAIR), F32), pltpu.VMEM((FOX_HEADS, tf, LANES), F32)],
        compiler_params=_params(("parallel", "arbitrary")))(pq, kv, kv, fneg, dcat_bf, lse_rows, delta_rows)


def reduce_adamw(parts, w, m, v, name):
    _, r, c = parts.shape
    tr = r
    for cand in range(16, r, 16):
        if r % cand == 0 and cand * c <= 128 * 1024:
            tr = cand
    c1 = 1.0 - ADAM_B1 ** ADAM_STEP
    c2 = 1.0 - ADAM_B2 ** ADAM_STEP

    def body(p_ref, w_ref, m_ref, v_ref, g_out, d_out, m_out, v_out):
        g = p_ref[0].astype(F32)
        for k in range(1, N_DEV):
            g = g + p_ref[k].astype(F32)
        mn = ADAM_B1 * m_ref[...] + (1.0 - ADAM_B1) * g
        vn = ADAM_B2 * v_ref[...] + (1.0 - ADAM_B2) * (g * g)
        g_out[...] = g
        m_out[...] = mn
        v_out[...] = vn
        d_out[...] = -ADAM_LR * ((mn / c1) / (jnp.sqrt(vn / c2) + ADAM_EPS) + ADAM_WD * w_ref[...])

    row = pl.BlockSpec((tr, c), lambda i: (i, 0))
    return pl.pallas_call(
        body, name=name, grid=(r // tr,),
        in_specs=[pl.BlockSpec((N_DEV, tr, c), lambda i: (0, i, 0)), row, row, row],
        out_specs=[row, row, row, row], out_shape=[SDS((r, c), F32)] * 4,
        compiler_params=_params(("parallel",)))(parts, w, m, v)


N_PEERS = N_DEV - 1
HBM_SPEC = pl.BlockSpec(memory_space=pltpu.HBM)
SEM_SPEC = pl.BlockSpec(memory_space=pltpu.SEMAPHORE)
ANY_SPEC = pl.BlockSpec(memory_space=pl.ANY)
SPLIT_EFFECT = pltpu.SideEffectType.DATAFLOW_SIDE_EFFECTING


def _my_index():
    return 4 * lax.axis_index("x") + 2 * lax.axis_index("y") + lax.axis_index("c")


def _peers():
    x, y, c = lax.axis_index("x"), lax.axis_index("y"), lax.axis_index("c")
    peers = []
    for k in range(1, N_DEV):
        px = 1 - x if (k >> 2) & 1 else x
        py = 1 - y if (k >> 1) & 1 else y
        pc = 1 - c if k & 1 else c
        peers.append(((px, py, pc), 4 * px + 2 * py + pc))
    return 4 * x + 2 * y + c, peers


def _push(src, dst, send_sems, recv_sems, slot, dev):
    return pltpu.make_async_remote_copy(src_ref=src, dst_ref=dst, send_sem=send_sems.at[slot], recv_sem=recv_sems.at[slot],
                                        device_id=dev, device_id_type=pl.DeviceIdType.MESH)


def _landing_shapes(arrs, scatter):
    return [SDS((N_DEV,) + tuple(a.shape[1:] if sc else a.shape), a.dtype) for a, sc in zip(arrs, scatter)]


def exchange(arrs, scatter, name):
    na = len(arrs)

    def body(*refs):
        ins = refs[:na]
        outs = refs[na:2 * na]
        send_sems, recv_sems, local_sems = refs[2 * na:]
        me, peers = _peers()
        local = []
        remote = []
        for a in range(na):
            lc = pltpu.make_async_copy(ins[a].at[me] if scatter[a] else ins[a], outs[a].at[me], local_sems.at[a])
            lc.start()
            local.append(lc)
            for k, (dev, idx) in enumerate(peers):
                cp = _push(ins[a].at[idx] if scatter[a] else ins[a], outs[a].at[me], send_sems, recv_sems,
                           a * N_PEERS + k, dev)
                cp.start()
                remote.append(cp)
        for a in range(na):
            for k, (dev, idx) in enumerate(peers):
                _push(ins[a].at[me] if scatter[a] else ins[a], outs[a].at[idx], send_sems, recv_sems,
                      a * N_PEERS + k, dev).wait_recv()
        for cp in remote:
            cp.wait_send()
        for lc in local:
            lc.wait()

    return pl.pallas_call(
        body, name=name, in_specs=[HBM_SPEC] * na, out_specs=[HBM_SPEC] * na, out_shape=_landing_shapes(arrs, scatter),
        scratch_shapes=[pltpu.SemaphoreType.DMA((na * N_PEERS,)), pltpu.SemaphoreType.DMA((na * N_PEERS,)),
                        pltpu.SemaphoreType.DMA((na,))])(*arrs)


def exchange_start(arrs, scatter, after, name):
    na = len(arrs)
    lands = [lax.empty(l.shape, l.dtype) for l in _landing_shapes(arrs, scatter)]

    def body(*refs):
        ins = refs[:na]
        land = refs[na:2 * na]
        send_sems, recv_sems = refs[2 * na + 1], refs[2 * na + 2]
        token = refs[-1]
        me, peers = _peers()
        for a in range(na):
            for k, (dev, idx) in enumerate(peers):
                _push(ins[a].at[idx] if scatter[a] else ins[a], land[a].at[me], send_sems, recv_sems,
                      a * N_PEERS + k, dev).start()
        token[...] = jnp.zeros_like(token)

    thru = [pltpu.HBM(a.shape, a.dtype) for a in arrs] + [pltpu.HBM(l.shape, l.dtype) for l in lands]
    res = pl.pallas_call(
        body, name=name,
        out_shape=(pltpu.SemaphoreType.DMA((na * N_PEERS,)), pltpu.SemaphoreType.DMA((na * N_PEERS,)), *thru,
                   SDS((8, LANES), F32)),
        in_specs=[HBM_SPEC] * (2 * na) + [ANY_SPEC],
        out_specs=(SEM_SPEC, SEM_SPEC, *([HBM_SPEC] * (2 * na)), pl.BlockSpec(memory_space=pltpu.VMEM)),
        input_output_aliases={i: 2 + i for i in range(2 * na)},
        compiler_params=pltpu.CompilerParams(has_side_effects=SPLIT_EFFECT),
    )(*[pltpu.with_memory_space_constraint(a, pltpu.HBM) for a in arrs],
      *[pltpu.with_memory_space_constraint(l, pltpu.HBM) for l in lands], after)
    return {"send": res[0], "recv": res[1], "src": res[2:2 + na], "land": res[2 + na:2 + 2 * na],
            "token": res[-1][0, 0], "scatter": scatter}


def exchange_wait(handle, after, name):
    scatter = handle["scatter"]
    na = len(scatter)

    def body(*refs):
        src = refs[:na]
        land = refs[na:2 * na]
        send_sems, recv_sems = refs[2 * na], refs[2 * na + 1]
        me, peers = _peers()
        for a in range(na):
            for k, (dev, idx) in enumerate(peers):
                cp = _push(src[a].at[me] if scatter[a] else src[a], land[a].at[idx], send_sems, recv_sems,
                           a * N_PEERS + k, dev)
                cp.wait_send()
                cp.wait_recv()

    ops = list(handle["src"]) + list(handle["land"])
    res = pl.pallas_call(
        body, name=name, out_shape=tuple(pltpu.HBM(o.shape, o.dtype) for o in ops),
        in_specs=[HBM_SPEC] * (2 * na) + [SEM_SPEC, SEM_SPEC, ANY_SPEC], out_specs=tuple([HBM_SPEC] * (2 * na)),
        input_output_aliases={i: i for i in range(2 * na)},
        compiler_params=pltpu.CompilerParams(has_side_effects=SPLIT_EFFECT),
    )(*ops, handle["send"], handle["recv"], after)
    me = _my_index()
    out = []
    for a in range(na):
        own = lax.dynamic_index_in_dim(res[a], me, 0, keepdims=True) if scatter[a] else res[a][None]
        out.append(lax.dynamic_update_slice(res[na + a], own, (me,) + (0,) * (own.ndim - 1)))
    return out


def forward_layer(l, xin, xin_bf, mem_bf, wt, nb, s):
    sv = {"xin_bf": xin_bf}
    memkv = mm_nn(mem_bf, wt["memw"], BF16, f"memkv{l}")
    sv["memkv"] = memkv
    if l == 0:
        proj = mm_nn(xin_bf, wt["win_a"], F32, "proj_a")
        pooled, tok = pool_fwd(proj, wt["pw_bd"], wt["pscale"], nb, s)
        sv["pooled"] = pooled
    else:
        kv = mm_nn(xin_bf, wt["kvw"][:, :2 * TOK_WIDTH], BF16, "kv_proj")
        fl = mm_nn(xin_bf, wt["kvw"][:, 2 * TOK_WIDTH:], F32, "gate_proj")
        fneg = -fgate_fwd(fl, wt["fb"], nb, s)
        fneg_rows = _to_tile_rows(fneg, nb, s, min(TF, s))
        proj = mm_nn(xin_bf, wt["wq"], BF16, "proj_b")
        tok, o_f32, lse = fox_fwd(proj, kv, make_vaug(kv), fneg_rows, nb, s)
        sv.update(kv=kv, fl=fl, fneg=fneg, fneg_rows=fneg_rows, o_f32=o_f32, lse=lse)
    sv["proj"] = proj
    mem_out = memattn_fwd(proj, memkv, nb, s, f"memattn_fwd{l}")
    cat = jnp.concatenate([tok, mem_out], axis=1)
    sv["cat"] = cat
    mix = mm_nn(cat, wt["wout"], F32, f"out_proj{l}")
    x1, x1_bf, xh1, rs1 = ln_fwd(xin, mix, wt["ln1_g"], wt["ln1_b"], f"ln1_fwd{l}")
    sv.update(x1_bf=x1_bf, xh1=xh1, rs1=rs1)
    h = mm_nn(x1_bf, wt["wup"], F32, f"ffn_up{l}")
    act = convgate_fwd(h, wt["cw"], nb, s, f"convgate_fwd{l}")
    sv.update(h=h, act=act)
    ffn = mm_nn(act, wt["wdown"], F32, f"ffn_down{l}")
    x2, x2_bf, xh2, rs2 = ln_fwd(x1, ffn, wt["ln2_g"], wt["ln2_b"], f"ln2_fwd{l}")
    sv.update(xh2=xh2, rs2=rs2)
    return x2, x2_bf, sv


def backward_layer(l, dy, sv, mem_bf, wt, nb, s):
    g = {}
    dr2, dr2_bf, g["ln2_g"], g["ln2_b"] = ln_bwd(dy, sv["xh2"], sv["rs2"], wt["ln2_g"], f"ln2_bwd{l}")
    dact = mm_nn(dr2_bf, wt["wdown_t"], BF16, f"ffn_down_dx{l}")
    g["wdown"] = mm_tn(sv["act"], dr2_bf, f"ffn_down_dw{l}")
    dh_u, dh_g, dcw_u, dcw_g = convgate_bwd(sv["h"], dact, wt["cw"], nb, s, f"convgate_bwd{l}")
    g["cw"] = jnp.concatenate([dcw_u, dcw_g], axis=0)
    half = FF_PAIRS * FF_BLOCK_PAD
    dx1 = mm_nn(dh_u, wt["wup_t"][:half], F32, f"ffn_up_dx_u{l}", addend=dr2, add_scale=DN_ALPHA)
    dx1 = mm_nn(dh_g, wt["wup_t"][half:], F32, f"ffn_up_dx_g{l}", addend=dx1)
    g["wup"] = jnp.concatenate([mm_tn(sv["x1_bf"], dh_u, f"ffn_up_dw_u{l}", blocked=True),
                                mm_tn(sv["x1_bf"], dh_g, f"ffn_up_dw_g{l}", blocked=True)], axis=0)
    dr1, dr1_bf, g["ln1_g"], g["ln1_b"] = ln_bwd(dx1, sv["xh1"], sv["rs1"], wt["ln1_g"], f"ln1_bwd{l}")
    dcat, dcat_bf = mm_nn(dr1_bf, wt["wout_t"], F32, f"out_proj_dx{l}", also_bf16=True)
    g["wout"] = mm_tn(sv["cat"], dr1_bf, f"out_proj_dw{l}")
    dqm, dmemkv = memattn_bwd(sv["proj"], sv["memkv"], dcat, nb, s, f"memattn_bwd{l}")
    g["memw"] = mm_tn(mem_bf, dmemkv, f"memkv_dw{l}")
    if l == 0:
        dmixed, dpooled, g["pscale"] = pool_bwd_mix(dcat, sv["pooled"], wt["pw_bd"], wt["pw_bd_t"], wt["pscale"], nb, s)
        g["pw_full"] = mm_tn(sv["pooled"], dmixed, "pool_dw")
        du = pool_bwd_window(dpooled, nb, s)
        dproj = jnp.concatenate([du, dqm], axis=1)
        dx = mm_nn(dproj, wt["win_a_t"], F32, "proj_a_dx", addend=dr1, add_scale=DN_ALPHA)
        g["win_a"] = mm_tn(sv["xin_bf"], dproj, "proj_a_dw")
    else:
        delta = fox_delta(dcat, sv["o_f32"], nb, s)
        tf = min(TF, s)
        dq, dfcum_q = fox_bwd_dq(sv["proj"], sv["kv"], sv["fneg_rows"], dcat_bf, sv["lse"], delta, nb, s)
        dk, dv, dfcum_k = fox_bwd_dkv(sv["proj"], sv["kv"], sv["fneg"], dcat_bf,
                                      _to_tile_rows(sv["lse"], nb, s, tf), _to_tile_rows(delta, nb, s, tf), nb, s)
        dfl, g["fb"] = fgate_bwd(dfcum_q, dfcum_k, sv["fl"], wt["fb"], nb, s)
        dproj = jnp.concatenate([dq, dqm], axis=1)
        dkvf = jnp.concatenate([dk, dv, dfl.astype(BF16)], axis=1)
        dx = mm_nn(dproj, wt["wq_t"], F32, "proj_b_dx", addend=dr1, add_scale=DN_ALPHA)
        dx = mm_nn(dkvf, wt["kvw_t"], F32, "kv_proj_dx", addend=dx)
        g["wq"] = mm_tn(sv["xin_bf"], dproj, "proj_b_dw")
        g["kvw"] = mm_tn(sv["xin_bf"], dkvf, "kv_proj_dw")
    return dx, g


def pack_replicated(pool_w, ln1_g, ln1_b, ln2_g, ln2_b, conv_b, f_b):
    cb = jnp.pad(conv_b, ((0, 0), (0, 6144 - 5504))).reshape(12, D_MODEL)
    fb = jnp.pad(f_b.reshape(1, FOX_HEADS), ((0, 3), (0, D_MODEL - FOX_HEADS)))
    return jnp.concatenate([pool_w.reshape(144, D_MODEL), ln1_g, ln1_b, ln2_g, ln2_b, cb, fb], axis=0)


def unpack_replicated(buf):
    pool_w = buf[:144].reshape(1, 4, POOL_GROUP, POOL_GROUP)
    ln = [buf[144 + 2 * k:146 + 2 * k] for k in range(4)]
    conv_b = buf[152:164].reshape(2, 6144)[:, :5504]
    f_b = buf[164, :FOX_HEADS]
    return pool_w, ln[0], ln[1], ln[2], ln[3], conv_b, f_b


def pack_small(conv_w, pool_scale):
    buf = jnp.zeros((16, FF_BLOCK_PAD), F32)
    buf = lax.dynamic_update_slice(buf, conv_w.reshape(DEPTH * 3, FF_BLOCK), (0, 0))
    return lax.dynamic_update_slice(buf, pool_scale, (8, 0))


def _block_diag(pw):
    out = jnp.zeros((TOK_WIDTH, TOK_WIDTH), pw.dtype)
    for g in range(4):
        out = lax.dynamic_update_slice(out, pw[g], (g * POOL_GROUP, g * POOL_GROUP))
    return out


def layer_shards(l, sq_a, sq_b, mem_w_kv, ffn_w_up, ffn_w_down):
    return [sq_a[0].astype(BF16), sq_b[0].astype(BF16), mem_w_kv[l].astype(BF16), ffn_w_up[l].astype(BF16),
            ffn_w_down[l].astype(BF16)]


def layer_weights(l, gath, small, conv_b, ln1_g, ln1_b, ln2_g, ln2_b):
    w_in = gath[0].reshape(D_MODEL, D_MODEL)
    w_out = gath[1].reshape(D_MODEL, D_MODEL)
    pad_c = FF_BLOCK_PAD - FF_BLOCK
    wup = jnp.pad(gath[3], ((0, 0), (0, 0), (0, pad_c))).transpose(1, 0, 2).reshape(D_MODEL, N_DEV * FF_BLOCK_PAD)
    wdown = jnp.pad(gath[4].reshape(FF_PAIRS, FF_BLOCK, D_MODEL), ((0, 0), (0, pad_c), (0, 0)))
    wdown = wdown.reshape(FF_PAIRS * FF_BLOCK_PAD, D_MODEL)
    cb = jnp.pad(conv_b[l].reshape(N_DEV, FF_BLOCK), ((0, 0), (0, pad_c)))
    cw = jnp.concatenate([small[:, 3 * l:3 * l + 3, :], cb[:, None, :], jnp.zeros((N_DEV, 4, FF_BLOCK_PAD), F32)], axis=1)
    wt = {"memw": gath[2].reshape(D_MODEL, 2 * MEM_WIDTH), "wout": w_out, "wout_t": w_out.T,
          "wup": wup, "wup_t": wup.T, "wdown": wdown, "wdown_t": wdown.T, "cw": cw,
          "ln1_g": ln1_g[l:l + 1], "ln1_b": ln1_b[l:l + 1], "ln2_g": ln2_g[l:l + 1], "ln2_b": ln2_b[l:l + 1]}
    return wt, w_in


def layer_grad_blocks(g, w_in_grad):
    wup = g["wup"][:, :, :FF_BLOCK]
    wdown = g["wdown"].reshape(FF_PAIRS, FF_BLOCK_PAD, D_MODEL)[:, :FF_BLOCK].reshape(N_DEV, FF_ROWS, D_MODEL)
    blocks = [w_in_grad.reshape(N_DEV, 128, D_MODEL), g["wout"].reshape(N_DEV, 128, D_MODEL),
              g["memw"].reshape(N_DEV, 128, 2 * MEM_WIDTH), wup, wdown]
    return [b.astype(BF16) for b in blocks]


def small_grad_blocks(g0, g1):
    taps = jnp.stack([g0["cw"][:, :3, :], g1["cw"][:, :3, :]], axis=1).reshape(N_DEV, DEPTH * 3, FF_BLOCK_PAD)
    small = jnp.zeros((N_DEV, 16, FF_BLOCK_PAD), F32)
    small = lax.dynamic_update_slice(small, taps, (0, 0, 0))
    return lax.dynamic_update_slice(small, g0["pscale"].reshape(N_DEV, 1, 96), (0, 8, 0))


def replicated_grads(g0, g1):
    pw = jnp.stack([g0["pw_full"][k * POOL_GROUP:(k + 1) * POOL_GROUP, k * POOL_GROUP:(k + 1) * POOL_GROUP] for k in range(4)])
    conv_b = jnp.stack([g_["cw"][:, 3, :FF_BLOCK].reshape(N_DEV * FF_BLOCK) for g_ in (g0, g1)])
    ln = [jnp.concatenate([g0[n], g1[n]], axis=0) for n in ("ln1_g", "ln1_b", "ln2_g", "ln2_b")]
    return pack_replicated(pw[None], ln[0], ln[1], ln[2], ln[3], conv_b, g1["fb"][0, :FOX_HEADS])


def kernel(x, mem, a_w_in, a_pool_w, a_pool_scale, a_w_out, b_w_q, b_w_out, kv_w, f_b, mem_w_kv, ln1_g, ln1_b, ln2_g, ln2_b, ffn_w_up, ffn_conv_w, ffn_conv_b, ffn_w_down, loss_target, m_a_w_in, m_a_pool_w, m_a_pool_scale, m_a_w_out, m_b_w_q, m_b_w_out, m_kv_w, m_f_b, m_mem_w_kv, m_ln1_g, m_ln1_b, m_ln2_g, m_ln2_b, m_ffn_w_up, m_ffn_conv_w, m_ffn_conv_b, m_ffn_w_down, v_a_w_in, v_a_pool_w, v_a_pool_scale, v_a_w_out, v_b_w_q, v_b_w_out, v_kv_w, v_f_b, v_mem_w_kv, v_ln1_g, v_ln1_b, v_ln2_g, v_ln2_b, v_ffn_w_up, v_ffn_conv_w, v_ffn_conv_b, v_ffn_w_down):
    nb, s, d = x.shape
    t = nb * s
    x2d, mem_bf, target = x.reshape(t, d), mem.reshape(nb * MEM_LEN, d).astype(BF16), loss_target.reshape(t, d)

    shards0 = layer_shards(0, a_w_in, a_w_out, mem_w_kv, ffn_w_up, ffn_w_down) + [pack_small(ffn_conv_w, a_pool_scale)]
    shards1 = layer_shards(1, b_w_q, b_w_out, mem_w_kv, ffn_w_up, ffn_w_down)
    shards1.append(jnp.pad(kv_w, ((0, 0), (0, KV_COLS_PAD - KV_COLS))).astype(BF16))
    gath0 = exchange(shards0, [False] * 6, "gather_w0")
    pull1 = exchange_start(shards1, [False] * 6, gath0[0], "gather_w1_start")
    small = gath0[5]
    wt0, w_in = layer_weights(0, gath0, small, ffn_conv_b, ln1_g + pull1["token"], ln1_b, ln2_g, ln2_b)
    pw_bd = _block_diag(a_pool_w[0])
    wt0.update(win_a=w_in, win_a_t=w_in.T, pw_bd=pw_bd.astype(BF16), pw_bd_t=pw_bd.T.astype(BF16),
               pscale=small[:, 8, :96].reshape(1, TOK_WIDTH))

    x1, x1_bf, sv0 = forward_layer(0, x2d, x2d.astype(BF16), mem_bf, wt0, nb, s)
    gath1 = exchange_wait(pull1, x1_bf, "gather_w1_wait")
    wt1, w_q = layer_weights(1, gath1, small, ffn_conv_b, ln1_g, ln1_b, ln2_g, ln2_b)
    kvw = gath1[5].reshape(D_MODEL, KV_COLS_PAD)
    wt1.update(wq=w_q, wq_t=w_q.T, kvw=kvw, kvw_t=kvw.T,
               fb=jnp.pad(f_b.reshape(1, FOX_HEADS), ((0, 0), (0, LANES - FOX_HEADS))))
    y, _, sv1 = forward_layer(1, x1, x1_bf, mem_bf, wt1, nb, s)
    dy, loss_row = loss_head(y, target)
    loss = lax.psum(loss_row[0, 0], ("x", "y", "c"))

    dx1, g1 = backward_layer(1, dy, sv1, mem_bf, wt1, nb, s)
    blocks1 = layer_grad_blocks(g1, g1["wq"]) + [g1["kvw"][:, :KV_COLS].reshape(N_DEV, 128, KV_COLS).astype(BF16)]
    push1 = exchange_start(blocks1, [True] * 6, dx1, "scatter_g1_start")
    wt0["ln2_g"] = wt0["ln2_g"] + push1["token"]
    grad_x, g0 = backward_layer(0, dx1, sv0, mem_bf, wt0, nb, s)
    blocks0 = layer_grad_blocks(g0, g0["win_a"]) + [small_grad_blocks(g0, g1), replicated_grads(g0, g1)]
    parts0 = exchange(blocks0, [True] * 6 + [False], "scatter_g0")
    parts1 = exchange_wait(push1, parts0[0], "scatter_g1_wait")

    res = {}

    def upd(nm, parts, w2, m2, v2):
        res[nm] = reduce_adamw(parts, w2, m2, v2, f"adamw_{nm}")

    upd("a_w_in", parts0[0], a_w_in[0], m_a_w_in[0], v_a_w_in[0])
    upd("a_w_out", parts0[1], a_w_out[0], m_a_w_out[0], v_a_w_out[0])
    upd("b_w_q", parts1[0], b_w_q[0], m_b_w_q[0], v_b_w_q[0])
    upd("b_w_out", parts1[1], b_w_out[0], m_b_w_out[0], v_b_w_out[0])
    upd("kv_w", parts1[5], kv_w, m_kv_w, v_kv_w)
    for l, parts in enumerate((parts0, parts1)):
        upd(f"mem_w_kv{l}", parts[2], mem_w_kv[l], m_mem_w_kv[l], v_mem_w_kv[l])
        upd(f"ffn_w_up{l}", parts[3], ffn_w_up[l], m_ffn_w_up[l], v_ffn_w_up[l])
        upd(f"ffn_w_down{l}", parts[4], ffn_w_down[l], m_ffn_w_down[l], v_ffn_w_down[l])
    upd("small", parts0[5], pack_small(ffn_conv_w, a_pool_scale), pack_small(m_ffn_conv_w, m_a_pool_scale),
        pack_small(v_ffn_conv_w, v_a_pool_scale))
    upd("replicated", parts0[6], pack_replicated(a_pool_w, ln1_g, ln1_b, ln2_g, ln2_b, ffn_conv_b, f_b),
        pack_replicated(m_a_pool_w, m_ln1_g, m_ln1_b, m_ln2_g, m_ln2_b, m_ffn_conv_b, m_f_b),
        pack_replicated(v_a_pool_w, v_ln1_g, v_ln1_b, v_ln2_g, v_ln2_b, v_ffn_conv_b, v_f_b))

    for nm in ("a_w_in", "a_w_out", "b_w_q", "b_w_out"):
        res[nm] = [o[None] for o in res[nm]]
    for nm in ("mem_w_kv", "ffn_w_up", "ffn_w_down"):
        res[nm] = [jnp.stack([a0, a1]) for a0, a1 in zip(res[nm + "0"], res[nm + "1"])]
    res["ffn_conv_w"] = [o[:DEPTH * 3, :FF_BLOCK].reshape(DEPTH, 3, FF_BLOCK) for o in res["small"]]
    res["a_pool_scale"] = [o[8:9, :96] for o in res["small"]]
    rep_names = ["a_pool_w", "ln1_g", "ln1_b", "ln2_g", "ln2_b", "ffn_conv_b", "f_b"]
    for nm in rep_names:
        res[nm] = []
    for o in res["replicated"]:
        for nm, val in zip(rep_names, unpack_replicated(o)):
            res[nm].append(val)

    order = ["a_w_in", "a_pool_w", "a_pool_scale", "a_w_out", "b_w_q", "b_w_out", "kv_w", "f_b", "mem_w_kv",
             "ln1_g", "ln1_b", "ln2_g", "ln2_b", "ffn_w_up", "ffn_conv_w", "ffn_conv_b", "ffn_w_down"]
    out = [loss, grad_x.reshape(nb, s, d)]
    for kind in range(4):
        out.extend(res[nm][kind] for nm in order)
    return tuple(out)
```

```python
import jax
import jax.numpy as jnp
from jax import lax
from jax.experimental import pallas as pl
from jax.experimental.pallas import tpu as pltpu

F32 = jnp.float32
BF16 = jnp.bfloat16
SDS = jax.ShapeDtypeStruct

N_DEV = 8
D_MODEL = 1024
TOK_WIDTH = 768
MEM_WIDTH = 256
MEM_LEN = 256
MEM_HEADS = 4
HEAD_DIM = 64
FOX_HEADS = 12
POOL_GROUP = 192
FF_BLOCK = 688
FF_BLOCK_PAD = 768
FF_PAIRS = 4
FF_ROWS = 344
KV_COLS = 1548
KV_COLS_PAD = 1664
LANES = 128
DEPTH = 2
DN_ALPHA = (2.0 * DEPTH) ** 0.25
LN_EPS = 1e-5
QK_SCALE = HEAD_DIM ** -0.5
NEG_BIG = -1e30

ADAM_LR = 0.001
ADAM_B1 = 0.9
ADAM_B2 = 0.999
ADAM_EPS = 1e-08
ADAM_WD = 0.01
ADAM_STEP = 10

VMEM_LIMIT_BYTES = 56 * 1024 * 1024
TM = 512
TS = 256
TF = 256
TC = 256
HALO_POOL = 16
HALO_CONV = 8

NT_DIMS = (((1,), (1,)), ((), ()))
TN_DIMS = (((0,), (0,)), ((), ()))


def _params(sem=None):
    return pltpu.CompilerParams(dimension_semantics=sem, vmem_limit_bytes=VMEM_LIMIT_BYTES)


def _sigmoid(z):
    return 1.0 / (1.0 + jnp.exp(-z))


def _pick_tn(n):
    if n <= 2048:
        return n
    for t in (1024, 768, 512, 256, 128):
        if n % t == 0:
            return t
    return n


def mm_nn(a, b, out_dtype, name, addend=None, add_scale=1.0, also_bf16=False):
    m, k = a.shape
    _, n = b.shape
    tm = min(TM, m)
    tn = _pick_tn(n)
    tk = k if k <= 2048 else 1024
    nk = k // tk
    has_add = addend is not None

    def body(*refs):
        a_ref, b_ref = refs[0], refs[1]
        pos = 2
        c_ref = None
        if has_add:
            c_ref = refs[pos]
            pos += 1
        o_ref = refs[pos]
        ob_ref = refs[pos + 1] if also_bf16 else None
        acc = refs[-1]
        kk = pl.program_id(2)

        @pl.when(kk == 0)
        def _():
            acc[...] = jnp.zeros_like(acc)

        acc[...] += jnp.dot(a_ref[...].astype(BF16), b_ref[...].astype(BF16), preferred_element_type=F32)

        @pl.when(kk == nk - 1)
        def _():
            r = acc[...]
            if has_add:
                r = r + add_scale * c_ref[...]
            o_ref[...] = r.astype(out_dtype)
            if also_bf16:
                ob_ref[...] = r.astype(BF16)

    in_specs = [pl.BlockSpec((tm, tk), lambda i, j, kk: (i, kk)),
                pl.BlockSpec((tk, tn), lambda i, j, kk: (kk, j))]
    ops = [a, b]
    if has_add:
        in_specs.append(pl.BlockSpec((tm, tn), lambda i, j, kk: (i, j)))
        ops.append(addend)
    out_shape = [SDS((m, n), out_dtype)]
    out_specs = [pl.BlockSpec((tm, tn), lambda i, j, kk: (i, j))]
    if also_bf16:
        out_shape.append(SDS((m, n), BF16))
        out_specs.append(pl.BlockSpec((tm, tn), lambda i, j, kk: (i, j)))
    res = pl.pallas_call(
        body, name=name, grid=(m // tm, n // tn, nk), in_specs=in_specs, out_specs=out_specs, out_shape=out_shape,
        scratch_shapes=[pltpu.VMEM((tm, tn), F32)],
        compiler_params=_params(("parallel", "parallel", "arbitrary")))(*ops)
    return tuple(res) if also_bf16 else res[0]


def mm_tn(a, b, name, blocked=False):
    t, m = a.shape
    _, n = b.shape
    tt = min(TM, t)
    tm = 1024 if m % 1024 == 0 else m
    tn = FF_BLOCK_PAD if blocked else _pick_tn(n)
    nt = t // tt

    def body(a_ref, b_ref, o_ref):
        kk = pl.program_id(2)
        r = lax.dot_general(a_ref[...].astype(BF16), b_ref[...].astype(BF16), TN_DIMS, preferred_element_type=F32)
        if blocked:
            r = r[None]

        @pl.when(kk == 0)
        def _():
            o_ref[...] = r

        @pl.when(kk != 0)
        def _():
            o_ref[...] += r

    if blocked:
        out_shape = SDS((n // tn, m, tn), F32)
        out_spec = pl.BlockSpec((1, tm, tn), lambda i, j, kk: (j, i, 0))
    else:
        out_shape = SDS((m, n), F32)
        out_spec = pl.BlockSpec((tm, tn), lambda i, j, kk: (i, j))
    return pl.pallas_call(
        body, name=name, grid=(m // tm, n // tn, nt),
        in_specs=[pl.BlockSpec((tt, tm), lambda i, j, kk: (kk, i)), pl.BlockSpec((tt, tn), lambda i, j, kk: (kk, j))],
        out_specs=out_spec, out_shape=out_shape,
        compiler_params=_params(("parallel", "parallel", "arbitrary")))(a, b)


def ln_fwd(xprev, delta, g, b, name):
    t, d = xprev.shape
    tm = min(TM, t)

    def body(xp_ref, dl_ref, g_ref, b_ref, y_ref, yb_ref, xh_ref, rs_ref):
        r = DN_ALPHA * xp_ref[...] + dl_ref[...]
        mu = jnp.mean(r, axis=1, keepdims=True)
        xc = r - mu
        var = jnp.mean(xc * xc, axis=1, keepdims=True)
        rstd = lax.rsqrt(var + LN_EPS)
        xh = xc * rstd
        y = xh * g_ref[...] + b_ref[...]
        y_ref[...] = y
        yb_ref[...] = y.astype(BF16)
        xh_ref[...] = xh
        rs_ref[...] = jnp.broadcast_to(rstd, (tm, LANES))

    row = pl.BlockSpec((tm, d), lambda i: (i, 0))
    vec = pl.BlockSpec((1, d), lambda i: (0, 0))
    return pl.pallas_call(
        body, name=name, grid=(t // tm,), in_specs=[row, row, vec, vec],
        out_specs=[row, row, row, pl.BlockSpec((tm, LANES), lambda i: (i, 0))],
        out_shape=[SDS((t, d), F32), SDS((t, d), BF16), SDS((t, d), F32), SDS((t, LANES), F32)],
        compiler_params=_params(("parallel",)))(xprev, delta, g, b)


def ln_bwd(dy, xhat, rstd, g, name):
    t, d = dy.shape
    tm = min(TM, t)

    def body(dy_ref, xh_ref, rs_ref, g_ref, dr_ref, drb_ref, dg_ref, db_ref):
        i = pl.program_id(0)
        dyv = dy_ref[...]
        xh = xh_ref[...]
        dxh = dyv * g_ref[...]
        m1 = jnp.mean(dxh, axis=1, keepdims=True)
        m2 = jnp.mean(dxh * xh, axis=1, keepdims=True)
        dr = rs_ref[:, 0:1] * (dxh - m1 - xh * m2)
        dr_ref[...] = dr
        drb_ref[...] = dr.astype(BF16)

        @pl.when(i == 0)
        def _():
            dg_ref[...] = jnp.zeros_like(dg_ref)
            db_ref[...] = jnp.zeros_like(db_ref)

        dg_ref[...] += jnp.sum(dyv * xh, axis=0, keepdims=True)
        db_ref[...] += jnp.sum(dyv, axis=0, keepdims=True)

    row = pl.BlockSpec((tm, d), lambda i: (i, 0))
    vec = pl.BlockSpec((1, d), lambda i: (0, 0))
    return pl.pallas_call(
        body, name=name, grid=(t // tm,),
        in_specs=[row, row, pl.BlockSpec((tm, LANES), lambda i: (i, 0)), vec],
        out_specs=[row, row, vec, vec],
        out_shape=[SDS((t, d), F32), SDS((t, d), BF16), SDS((1, d), F32), SDS((1, d), F32)],
        compiler_params=_params(("arbitrary",)))(dy, xhat, rstd, g)


def loss_head(y, target):
    t, d = y.shape
    tm = min(TM, t)
    nsteps = t // tm

    def body(y_ref, t_ref, dy_ref, l_ref, acc):
        i = pl.program_id(0)
        diff = y_ref[...] - t_ref[...]
        dy_ref[...] = diff * (1.0 / d)

        @pl.when(i == 0)
        def _():
            acc[...] = jnp.zeros_like(acc)

        acc[...] += jnp.sum(diff * diff, axis=0, keepdims=True)

        @pl.when(i == nsteps - 1)
        def _():
            tot = jnp.sum(acc[...], axis=1, keepdims=True) * (0.5 / d)
            l_ref[...] = jnp.broadcast_to(tot, (1, LANES))

    row = pl.BlockSpec((tm, d), lambda i: (i, 0))
    return pl.pallas_call(
        body, name="loss_head", grid=(nsteps,), in_specs=[row, row],
        out_specs=[row, pl.BlockSpec((1, LANES), lambda i: (0, 0))],
        out_shape=[SDS((t, d), F32), SDS((1, LANES), F32)],
        scratch_shapes=[pltpu.VMEM((1, d), F32)],
        compiler_params=_params(("arbitrary",)))(y, target)


def memattn_fwd(proj, memkv, nb, s, name):
    ts = min(TS, s)
    nq = s // ts

    def body(q_ref, kv_ref, o_ref):
        for h in range(MEM_HEADS):
            lo, hi = h * HEAD_DIM, (h + 1) * HEAD_DIM
            qh = q_ref[:, lo:hi].astype(BF16)
            kh = kv_ref[:, lo:hi]
            vh = kv_ref[:, MEM_WIDTH + lo:MEM_WIDTH + hi]
            sc = lax.dot_general(qh, kh, NT_DIMS, preferred_element_type=F32) * QK_SCALE
            p = jnp.exp(sc - jnp.max(sc, axis=1, keepdims=True))
            p = p / jnp.sum(p, axis=1, keepdims=True)
            o_ref[:, lo:hi] = jnp.dot(p.astype(BF16), vh, preferred_element_type=F32).astype(BF16)

    return pl.pallas_call(
        body, name=name, grid=(nb, nq),
        in_specs=[pl.BlockSpec((ts, MEM_WIDTH), lambda b, i: (b * nq + i, 3)),
                  pl.BlockSpec((MEM_LEN, 2 * MEM_WIDTH), lambda b, i: (b, 0))],
        out_specs=pl.BlockSpec((ts, MEM_WIDTH), lambda b, i: (b * nq + i, 0)),
        out_shape=SDS((nb * s, MEM_WIDTH), BF16),
        compiler_params=_params(("parallel", "parallel")))(proj, memkv)


def memattn_bwd(proj, memkv, dcat, nb, s, name):
    ts = min(TS, s)
    nq = s // ts

    def body(q_ref, kv_ref, do_ref, dq_ref, dkv_ref):
        i = pl.program_id(1)

        @pl.when(i == 0)
        def _():
            dkv_ref[...] = jnp.zeros_like(dkv_ref)

        for h in range(MEM_HEADS):
            lo, hi = h * HEAD_DIM, (h + 1) * HEAD_DIM
            qh = q_ref[:, lo:hi].astype(BF16)
            kh = kv_ref[:, lo:hi]
            vh = kv_ref[:, MEM_WIDTH + lo:MEM_WIDTH + hi]
            doh = do_ref[:, lo:hi].astype(BF16)
            sc = lax.dot_general(qh, kh, NT_DIMS, preferred_element_type=F32) * QK_SCALE
            p = jnp.exp(sc - jnp.max(sc, axis=1, keepdims=True))
            p = p / jnp.sum(p, axis=1, keepdims=True)
            dv = lax.dot_general(p.astype(BF16), doh, TN_DIMS, preferred_element_type=F32)
            dp = lax.dot_general(doh, vh, NT_DIMS, preferred_element_type=F32)
            dl = jnp.sum(p * dp, axis=1, keepdims=True)
            ds = (p * (dp - dl) * QK_SCALE).astype(BF16)
            dq_ref[:, lo:hi] = jnp.dot(ds, kh, preferred_element_type=F32).astype(BF16)
            dkv_ref[:, lo:hi] += lax.dot_general(ds, qh, TN_DIMS, preferred_element_type=F32)
            dkv_ref[:, MEM_WIDTH + lo:MEM_WIDTH + hi] += dv

    return pl.pallas_call(
        body, name=name, grid=(nb, nq),
        in_specs=[pl.BlockSpec((ts, MEM_WIDTH), lambda b, i: (b * nq + i, 3)),
                  pl.BlockSpec((MEM_LEN, 2 * MEM_WIDTH), lambda b, i: (b, 0)),
                  pl.BlockSpec((ts, MEM_WIDTH), lambda b, i: (b * nq + i, 3))],
        out_specs=[pl.BlockSpec((ts, MEM_WIDTH), lambda b, i: (b * nq + i, 0)),
                   pl.BlockSpec((MEM_LEN, 2 * MEM_WIDTH), lambda b, i: (b, 0))],
        out_shape=[SDS((nb * s, MEM_WIDTH), BF16), SDS((nb * MEM_LEN, 2 * MEM_WIDTH), F32)],
        compiler_params=_params(("parallel", "arbitrary")))(proj, memkv, dcat)


def _pool_select(shape, s2, s4, s8, s16):
    lane = lax.broadcasted_iota(jnp.int32, shape, 1)
    return jnp.where(lane < POOL_GROUP, s2, jnp.where(lane < 2 * POOL_GROUP, s4, jnp.where(lane < 3 * POOL_GROUP, s8, s16)))


def _pool_count(shape, first_pos):
    pos = first_pos + lax.broadcasted_iota(jnp.int32, shape, 0)
    win = _pool_select(shape, 2, 4, 8, 16)
    return jnp.minimum(pos + 1, win).astype(F32)


def pool_fwd(proj, pw_bd, pscale, nb, s):
    ts = min(TS, s)
    nq = s // ts
    w = TOK_WIDTH

    def body(c_ref, h_ref, w_ref, sc_ref, pooled_ref, tok_ref):
        i = pl.program_id(0) % nq
        cur = c_ref[...]
        halo = jnp.where(i == 0, 0.0, h_ref[...])
        xe = jnp.concatenate([halo, cur], axis=0)
        s2 = xe + pltpu.roll(xe, 1, axis=0)
        s4 = s2 + pltpu.roll(s2, 2, axis=0)
        s8 = s4 + pltpu.roll(s4, 4, axis=0)
        s16 = s8 + pltpu.roll(s8, 8, axis=0)
        hp = HALO_POOL
        ws = _pool_select((ts, w), s2[hp:], s4[hp:], s8[hp:], s16[hp:])
        pooled = (ws / _pool_count((ts, w), i * ts) - cur).astype(BF16)
        pooled_ref[...] = pooled
        mixed = jnp.dot(pooled, w_ref[...], preferred_element_type=F32)
        tok_ref[...] = (mixed * sc_ref[...]).astype(BF16)

    row = pl.BlockSpec((ts, w), lambda r: (r, 0))
    return pl.pallas_call(
        body, name="pool_fwd", grid=(nb * nq,),
        in_specs=[row, pl.BlockSpec((HALO_POOL, w), lambda r: (jnp.maximum(r * (ts // HALO_POOL) - 1, 0), 0)),
                  pl.BlockSpec((w, w), lambda r: (0, 0)), pl.BlockSpec((1, w), lambda r: (0, 0))],
        out_specs=[row, row], out_shape=[SDS((nb * s, w), BF16), SDS((nb * s, w), BF16)],
        compiler_params=_params(("parallel",)))(proj, proj, pw_bd, pscale)


def pool_bwd_mix(dcat, pooled, pw_bd, pw_bd_t, pscale, nb, s):
    ts = min(TS, s)
    w = TOK_WIDTH

    def body(dt_ref, p_ref, w_ref, wt_ref, sc_ref, dm_ref, dp_ref, ds_ref):
        r = pl.program_id(0)
        dtok = dt_ref[...]
        mixed = jnp.dot(p_ref[...], w_ref[...], preferred_element_type=F32)

        @pl.when(r == 0)
        def _():
            ds_ref[...] = jnp.zeros_like(ds_ref)

        ds_ref[...] += jnp.sum(dtok * mixed, axis=0, keepdims=True)
        dmx = (dtok * sc_ref[...]).astype(BF16)
        dm_ref[...] = dmx
        dp_ref[...] = jnp.dot(dmx, wt_ref[...], preferred_element_type=F32)

    row = pl.BlockSpec((ts, w), lambda r: (r, 0))
    mat = pl.BlockSpec((w, w), lambda r: (0, 0))
    vec = pl.BlockSpec((1, w), lambda r: (0, 0))
    return pl.pallas_call(
        body, name="pool_bwd_mix", grid=(nb * s // ts,), in_specs=[row, row, mat, mat, vec],
        out_specs=[row, row, vec], out_shape=[SDS((nb * s, w), BF16), SDS((nb * s, w), F32), SDS((1, w), F32)],
        compiler_params=_params(("arbitrary",)))(dcat, pooled, pw_bd, pw_bd_t, pscale)


def pool_bwd_window(dpooled, nb, s):
    ts = min(TS, s)
    nq = s // ts
    w = TOK_WIDTH
    n_ext = ts + HALO_POOL
    n_halo_blocks = nb * s // HALO_POOL

    def body(c_ref, n_ref, du_ref):
        i = pl.program_id(0) % nq
        cur = c_ref[...]
        nxt = jnp.where(i == nq - 1, 0.0, n_ref[...])
        ze = jnp.concatenate([cur, nxt], axis=0) / _pool_count((n_ext, w), i * ts)
        s2 = ze + pltpu.roll(ze, n_ext - 1, axis=0)
        s4 = s2 + pltpu.roll(s2, n_ext - 2, axis=0)
        s8 = s4 + pltpu.roll(s4, n_ext - 4, axis=0)
        s16 = s8 + pltpu.roll(s8, n_ext - 8, axis=0)
        ws = _pool_select((ts, w), s2[:ts], s4[:ts], s8[:ts], s16[:ts])
        du_ref[...] = (ws - cur).astype(BF16)

    row = pl.BlockSpec((ts, w), lambda r: (r, 0))
    return pl.pallas_call(
        body, name="pool_bwd_window", grid=(nb * nq,),
        in_specs=[row, pl.BlockSpec((HALO_POOL, w),
                                    lambda r: (jnp.minimum((r + 1) * (ts // HALO_POOL), n_halo_blocks - 1), 0))],
        out_specs=row, out_shape=SDS((nb * s, w), BF16),
        compiler_params=_params(("parallel",)))(dpooled, dpooled)


def _conv_rows(xe, w_ref):
    return (w_ref[0, 2:3, :] * xe + w_ref[0, 1:2, :] * pltpu.roll(xe, 1, axis=0)
            + w_ref[0, 0:1, :] * pltpu.roll(xe, 2, axis=0) + w_ref[0, 3:4, :])


def convgate_fwd(h, cw, nb, s, name):
    ts = min(TS, s)
    nq = s // ts
    w = FF_BLOCK_PAD
    hc = HALO_CONV

    def body(uc_ref, uh_ref, gc_ref, gh_ref, wu_ref, wg_ref, o_ref):
        first = (pl.program_id(0) % nq) == 0
        xu = jnp.concatenate([jnp.where(first, 0.0, uh_ref[...]), uc_ref[...]], axis=0)
        xg = jnp.concatenate([jnp.where(first, 0.0, gh_ref[...]), gc_ref[...]], axis=0)
        cu = _conv_rows(xu, wu_ref)[hc:]
        cg = _conv_rows(xg, wg_ref)[hc:]
        o_ref[...] = (cg * _sigmoid(cg) * cu).astype(BF16)

    def cur(off):
        return pl.BlockSpec((ts, w), lambda r, j: (r, j + off))

    def halo(off):
        return pl.BlockSpec((hc, w), lambda r, j: (jnp.maximum(r * (ts // hc) - 1, 0), j + off))

    def wspec(off):
        return pl.BlockSpec((1, 8, w), lambda r, j: (j + off, 0, 0))

    return pl.pallas_call(
        body, name=name, grid=(nb * nq, FF_PAIRS),
        in_specs=[cur(0), halo(0), cur(FF_PAIRS), halo(FF_PAIRS), wspec(0), wspec(FF_PAIRS)],
        out_specs=pl.BlockSpec((ts, w), lambda r, j: (r, j)), out_shape=SDS((nb * s, FF_PAIRS * w), BF16),
        compiler_params=_params(("parallel", "parallel")))(h, h, h, h, cw, cw)


def convgate_bwd(h, dact, cw, nb, s, name):
    ts = min(TS, s)
    nq = s // ts
    w = FF_BLOCK_PAD
    hc = HALO_CONV
    n_ext = ts + hc
    n_halo_blocks = nb * s // hc

    def body(uc_ref, up_ref, un_ref, gc_ref, gp_ref, gn_ref, dc_ref, dn_ref, wu_ref, wg_ref,
             dhu_ref, dhg_ref, dwu_ref, dwg_ref):
        r = pl.program_id(1)
        i = r % nq
        first = i == 0
        last = i == nq - 1
        xu = jnp.concatenate([jnp.where(first, 0.0, up_ref[...]), uc_ref[...], un_ref[...]], axis=0)
        xg = jnp.concatenate([jnp.where(first, 0.0, gp_ref[...]), gc_ref[...], gn_ref[...]], axis=0)
        cu = _conv_rows(xu, wu_ref)[hc:]
        cg = _conv_rows(xg, wg_ref)[hc:]
        da = jnp.concatenate([dc_ref[...].astype(F32), jnp.where(last, 0.0, dn_ref[...].astype(F32)[:hc])], axis=0)
        sg = _sigmoid(cg)
        dcu = da * (cg * sg)
        dcg = da * cu * (sg * (1.0 + cg * (1.0 - sg)))

        def conv_t(dcv, w_ref):
            return (w_ref[0, 2:3, :] * dcv + w_ref[0, 1:2, :] * pltpu.roll(dcv, n_ext - 1, axis=0)
                    + w_ref[0, 0:1, :] * pltpu.roll(dcv, n_ext - 2, axis=0))[:ts]

        dhu_ref[...] = conv_t(dcu, wu_ref).astype(BF16)
        dhg_ref[...] = conv_t(dcg, wg_ref).astype(BF16)

        def tap_grads(xe, dcv):
            d0 = dcv[:ts]
            x0 = xe[hc:hc + ts]
            x1 = pltpu.roll(xe, 1, axis=0)[hc:hc + ts]
            x2 = pltpu.roll(xe, 2, axis=0)[hc:hc + ts]
            rows = [jnp.sum(d0 * x2, axis=0, keepdims=True), jnp.sum(d0 * x1, axis=0, keepdims=True),
                    jnp.sum(d0 * x0, axis=0, keepdims=True), jnp.sum(d0, axis=0, keepdims=True)]
            sub = lax.broadcasted_iota(jnp.int32, (8, w), 0)
            upd = jnp.zeros((8, w), F32)
            for k, rv in enumerate(rows):
                upd = jnp.where(sub == k, rv, upd)
            return upd[None]

        @pl.when(r == 0)
        def _():
            dwu_ref[...] = jnp.zeros_like(dwu_ref)
            dwg_ref[...] = jnp.zeros_like(dwg_ref)

        dwu_ref[...] += tap_grads(xu, dcu)
        dwg_ref[...] += tap_grads(xg, dcg)

    def cur(off):
        return pl.BlockSpec((ts, w), lambda j, r: (r, j + off))

    def prev(off):
        return pl.BlockSpec((hc, w), lambda j, r: (jnp.maximum(r * (ts // hc) - 1, 0), j + off))

    def nxt(off):
        return pl.BlockSpec((hc, w), lambda j, r: (jnp.minimum((r + 1) * (ts // hc), n_halo_blocks - 1), j + off))

    def wspec(off):
        return pl.BlockSpec((1, 8, w), lambda j, r: (j + off, 0, 0))

    hb = 2 * hc
    dact_next = pl.BlockSpec((hb, w), lambda j, r: (jnp.minimum((r + 1) * (ts // hb), nb * s // hb - 1), j))

    p = FF_PAIRS
    dh_spec = pl.BlockSpec((ts, w), lambda j, r: (r, j))
    dw_spec = pl.BlockSpec((1, 8, w), lambda j, r: (j, 0, 0))
    return pl.pallas_call(
        body, name=name, grid=(p, nb * nq),
        in_specs=[cur(0), prev(0), nxt(0), cur(p), prev(p), nxt(p), cur(0), dact_next, wspec(0), wspec(p)],
        out_specs=[dh_spec, dh_spec, dw_spec, dw_spec],
        out_shape=[SDS((nb * s, p * w), BF16), SDS((nb * s, p * w), BF16), SDS((p, 8, w), F32), SDS((p, 8, w), F32)],
        compiler_params=_params(("parallel", "arbitrary")))(h, h, h, h, h, h, dact, dact, cw, cw)


def _tri(n, upper):
    r = lax.broadcasted_iota(jnp.int32, (n, n), 0)
    c = lax.broadcasted_iota(jnp.int32, (n, n), 1)
    return ((r <= c) if upper else (r >= c)).astype(F32)


def fgate_fwd(fl, fb, nb, s):
    tc = min(TC, s)
    nq = s // tc

    def body(fl_ref, fb_ref, f_ref, carry):
        @pl.when(pl.program_id(1) == 0)
        def _():
            carry[...] = jnp.zeros_like(carry)

        z = fl_ref[...] + fb_ref[...]
        logf = jnp.minimum(z, 0.0) - jnp.log(1.0 + jnp.exp(-jnp.abs(z)))
        f_ref[...] = jnp.dot(_tri(tc, False), logf, preferred_element_type=F32,
                             precision=lax.Precision.HIGHEST) + carry[...]
        carry[...] += jnp.sum(logf, axis=0, keepdims=True)

    row = pl.BlockSpec((tc, LANES), lambda b, i: (b * nq + i, 0))
    return pl.pallas_call(
        body, name="fgate_fwd", grid=(nb, nq), in_specs=[row, pl.BlockSpec((1, LANES), lambda b, i: (0, 0))],
        out_specs=row, out_shape=SDS((nb * s, LANES), F32), scratch_shapes=[pltpu.VMEM((1, LANES), F32)],
        compiler_params=_params(("arbitrary", "arbitrary")))(fl, fb)


def fgate_bwd(d_cum_q, d_cum_k, fl, fb, nb, s):
    tc = min(TC, s)
    nq = s // tc

    def body(dfq_ref, dfk_ref, fl_ref, fb_ref, dfl_ref, dfb_ref, carry):
        b = pl.program_id(0)
        i = pl.program_id(1)

        @pl.when(i == 0)
        def _():
            carry[...] = jnp.zeros_like(carry)

        @pl.when(jnp.logical_and(b == 0, i == 0))
        def _():
            dfb_ref[...] = jnp.zeros_like(dfb_ref)

        dfv = dfq_ref[...] + dfk_ref[...]
        dlog = jnp.dot(_tri(tc, True), dfv, preferred_element_type=F32,
                       precision=lax.Precision.HIGHEST) + carry[...]
        carry[...] += jnp.sum(dfv, axis=0, keepdims=True)
        z = fl_ref[...] + fb_ref[...]
        dfl = dlog / (1.0 + jnp.exp(z))
        dfl_ref[...] = dfl
        dfb_ref[...] += jnp.sum(dfl, axis=0, keepdims=True)

    row = pl.BlockSpec((tc, LANES), lambda b, i: (b * nq + nq - 1 - i, 0))
    vec = pl.BlockSpec((1, LANES), lambda b, i: (0, 0))
    return pl.pallas_call(
        body, name="fgate_bwd", grid=(nb, nq), in_specs=[row, row, row, vec], out_specs=[row, vec],
        out_shape=[SDS((nb * s, LANES), F32), SDS((1, LANES), F32)], scratch_shapes=[pltpu.VMEM((1, LANES), F32)],
        compiler_params=_params(("arbitrary", "arbitrary")))(d_cum_q, d_cum_k, fl, fb)


PAIR = 2 * HEAD_DIM
N_PAIRS = FOX_HEADS // 2


def _lane_put(shape, h, col):
    lane = lax.broadcasted_iota(jnp.int32, shape, 1)
    return jnp.where(lane == h, col, 0.0)


def _half_masks(rows):
    lane = lax.broadcasted_iota(jnp.int32, (rows, PAIR), 1)
    return lane < HEAD_DIM


def _split_pair(x, scale=None):
    if scale is not None:
        x = x * scale
    lo = _half_masks(x.shape[0])
    zero = jnp.zeros_like(x)
    return jnp.where(lo, x, zero), jnp.where(lo, zero, x)


def _to_tile_rows(a, nb, s, tf):
    return a.reshape(nb * s // tf, tf, LANES)[:, :, :16].transpose(0, 2, 1)


def _from_tile_rows(a):
    tiles, _, tf = a.shape
    return jnp.pad(a.transpose(0, 2, 1), ((0, 0), (0, 0), (0, LANES - 16))).reshape(tiles * tf, LANES)


BIAS_TERMS = 3
LOOKAHEAD = 4
LOOKAHEAD_BWD = 4
LOOKAHEAD_DKV = 2


def _bias_lane(h):
    return HEAD_DIM if h % 2 == 0 else 0


def _placement():
    rows = jnp.arange(LANES)[:, None]
    cols = jnp.arange(FOX_HEADS * PAIR)[None, :]
    head, lane = cols // PAIR, cols % PAIR
    first = jnp.where(head % 2 == 0, HEAD_DIM, 0)
    term = lane - first
    hit = (term >= 0) & (term < BIAS_TERMS) & (rows == 16 * term + head)
    return hit.astype(BF16)


def fox_prep(kv, fneg, nb, s):
    tf = min(TF, s)
    w = TOK_WIDTH

    def body(k_ref, v_ref, f_ref, pl_ref, ka_ref, vt_ref):
        lane = lax.broadcasted_iota(jnp.int32, (tf, LANES), 1)
        lo = lane < HEAD_DIM
        f = jnp.where(lane < FOX_HEADS, f_ref[...], 0.0)
        hi = f.astype(BF16).astype(F32)
        mid = (f - hi).astype(BF16).astype(F32)
        low = (f - hi - mid).astype(BF16).astype(F32)
        terms = (hi + pltpu.roll(mid, 16, axis=1) + pltpu.roll(low, 32, axis=1)).astype(BF16)
        placed = jnp.dot(terms, pl_ref[...], preferred_element_type=F32).astype(BF16)
        one = jnp.ones((tf, LANES), BF16)
        zero = jnp.zeros((tf, LANES), BF16)
        for p in range(N_PAIRS):
            kp = k_ref[:, p * PAIR:(p + 1) * PAIR] * QK_SCALE
            vp = v_ref[:, p * PAIR:(p + 1) * PAIR]
            he, ho = 2 * p, 2 * p + 1
            ka_ref[:, he * PAIR:(he + 1) * PAIR] = jnp.where(lo, kp, placed[:, he * PAIR:(he + 1) * PAIR])
            ka_ref[:, ho * PAIR:(ho + 1) * PAIR] = jnp.where(lo, placed[:, ho * PAIR:(ho + 1) * PAIR], kp)
            ve = jnp.where(lo, vp, jnp.where(lane == HEAD_DIM, one, zero))
            vo = jnp.where(lo, jnp.where(lane == 0, one, zero), vp)
            vt_ref[0, he * PAIR:(he + 1) * PAIR, :] = ve.astype(F32).T.astype(BF16)
            vt_ref[0, ho * PAIR:(ho + 1) * PAIR, :] = vo.astype(F32).T.astype(BF16)

    return pl.pallas_call(
        body, name="fox_prep", grid=(nb * s // tf,),
        in_specs=[pl.BlockSpec((tf, w), lambda r: (r, 0)), pl.BlockSpec((tf, w), lambda r: (r, 1)),
                  pl.BlockSpec((tf, LANES), lambda r: (r, 0)), pl.BlockSpec((LANES, FOX_HEADS * PAIR), lambda r: (0, 0))],
        out_specs=[pl.BlockSpec((tf, FOX_HEADS * PAIR), lambda r: (r, 0)),
                   pl.BlockSpec((1, FOX_HEADS * PAIR, tf), lambda r: (r, 0, 0))],
        out_shape=[SDS((nb * s, FOX_HEADS * PAIR), BF16), SDS((nb * s // tf, FOX_HEADS * PAIR, tf), BF16)],
        compiler_params=_params(("parallel",)))(kv, kv, fneg, _placement())


def fox_fwd_t(pq, kaug, vaug_t, nb, s):
    tf = min(TF, s)
    n = s // tf
    w = TOK_WIDTH
    wa = FOX_HEADS * PAIR

    def body(q_ref, k_hbm, vt_hbm, ob_ref, of_ref, lse_ref, k_vm, vt_vm, qx_scr, m_scr, acc_scr, sems):
        b = pl.program_id(0)
        i = pl.program_id(1)

        @pl.when(i == 0)
        def _():
            ck = pltpu.make_async_copy(k_hbm.at[pl.ds(pl.multiple_of(b * s, tf), s)], k_vm, sems.at[0])
            cv = pltpu.make_async_copy(vt_hbm.at[pl.ds(b * n, n)], vt_vm, sems.at[1])
            ck.start()
            cv.start()
            ck.wait()
            cv.wait()

        lane = lax.broadcasted_iota(jnp.int32, (tf, PAIR), 1)
        one = jnp.ones((tf, PAIR), BF16)
        zero = jnp.zeros((tf, PAIR), BF16)
        for p in range(N_PAIRS):
            qp = q_ref[:, p * PAIR:(p + 1) * PAIR]
            be, bo = _bias_lane(2 * p), _bias_lane(2 * p + 1)
            ones_e = jnp.where((lane >= be) & (lane < be + BIAS_TERMS), one, zero)
            ones_o = jnp.where((lane >= bo) & (lane < bo + BIAS_TERMS), one, zero)
            qx_scr[2 * p] = jnp.where(lane < HEAD_DIM, qp, ones_e)
            qx_scr[2 * p + 1] = jnp.where(lane < HEAD_DIM, ones_o, qp)
        m_scr[...] = jnp.full(m_scr.shape, NEG_BIG, F32)
        acc_scr[...] = jnp.zeros_like(acc_scr)

        def tile(j, masked):
            ks = pl.multiple_of(j * tf, tf)
            if masked:
                keep = lax.broadcasted_iota(jnp.int32, (tf, tf), 1) >= lax.broadcasted_iota(jnp.int32, (tf, tf), 0)
            def scores(h):
                kx = k_vm[pl.ds(ks, tf), h * PAIR:(h + 1) * PAIR]
                return lax.dot_general(kx, qx_scr[h], NT_DIMS, preferred_element_type=F32)

            ahead = [scores(h) for h in range(LOOKAHEAD)]
            for h in range(FOX_HEADS):
                sc = ahead.pop(0)
                if h + LOOKAHEAD < FOX_HEADS:
                    ahead.append(scores(h + LOOKAHEAD))
                if masked:
                    sc = jnp.where(keep, sc, NEG_BIG)
                m_prev = m_scr[h]
                m_new = jnp.maximum(m_prev, jnp.max(sc, axis=0, keepdims=True))
                pr = jnp.exp(sc - m_new).astype(BF16)
                pv = jnp.dot(vt_vm[j, h * PAIR:(h + 1) * PAIR, :], pr, preferred_element_type=F32)
                acc_scr[h] = jnp.exp(m_prev - m_new) * acc_scr[h] + pv
                m_scr[h] = m_new

        def step(j, carry):
            tile(j, False)
            return carry

        lax.fori_loop(0, i, step, 0)
        tile(i, True)

        top = lax.broadcasted_iota(jnp.int32, (PAIR, tf), 0) < HEAD_DIM
        sub = lax.broadcasted_iota(jnp.int32, (16, tf), 0)
        lse = jnp.zeros((16, tf), F32)
        for p in range(N_PAIRS):
            he, ho = 2 * p, 2 * p + 1
            le = acc_scr[he, HEAD_DIM:HEAD_DIM + 1, :]
            lod = acc_scr[ho, 0:1, :]
            o = jnp.where(top, acc_scr[he] / le, acc_scr[ho] / lod).T
            ob_ref[:, p * PAIR:(p + 1) * PAIR] = o.astype(BF16)
            of_ref[:, p * PAIR:(p + 1) * PAIR] = o
            lse = jnp.where(sub == he, m_scr[he] + jnp.log(le), lse)
            lse = jnp.where(sub == ho, m_scr[ho] + jnp.log(lod), lse)
        lse_ref[0] = lse

    qrow = lambda b, i: (b * n + i, 0)
    return pl.pallas_call(
        body, name="fox_fwd", grid=(nb, n),
        in_specs=[pl.BlockSpec((tf, w), qrow), ANY_SPEC, ANY_SPEC],
        out_specs=[pl.BlockSpec((tf, w), qrow), pl.BlockSpec((tf, w), qrow),
                   pl.BlockSpec((1, 16, tf), lambda b, i: (b * n + i, 0, 0))],
        out_shape=[SDS((nb * s, w), BF16), SDS((nb * s, w), F32), SDS((nb * n, 16, tf), F32)],
        scratch_shapes=[pltpu.VMEM((s, wa), BF16), pltpu.VMEM((n, wa, tf), BF16),
                        pltpu.VMEM((FOX_HEADS, tf, PAIR), BF16), pltpu.VMEM((FOX_HEADS, 1, tf), F32),
                        pltpu.VMEM((FOX_HEADS, PAIR, tf), F32), pltpu.SemaphoreType.DMA((2,))],
        compiler_params=_params(("arbitrary", "arbitrary")))(pq, kaug, vaug_t)


def fox_delta(dcat, o, nb, s):
    tf = min(TM, s)
    w = TOK_WIDTH

    def body(do_ref, o_ref, dl_ref):
        out = jnp.zeros((tf, LANES), F32)
        for h in range(FOX_HEADS):
            lo, hi = h * HEAD_DIM, (h + 1) * HEAD_DIM
            out = out + _lane_put((tf, LANES), h, jnp.sum(do_ref[:, lo:hi] * o_ref[:, lo:hi], axis=1, keepdims=True))
        dl_ref[...] = out

    row = pl.BlockSpec((tf, w), lambda r: (r, 0))
    return pl.pallas_call(
        body, name="fox_delta", grid=(nb * s // tf,), in_specs=[row, row],
        out_specs=pl.BlockSpec((tf, LANES), lambda r: (r, 0)), out_shape=SDS((nb * s, LANES), F32),
        compiler_params=_params(("parallel",)))(dcat, o)


def fox_bwd_dq(pq, kv, fneg_rows, dcat_bf, lse, delta, nb, s):
    tf = min(TF, s)
    n = s // tf
    w = TOK_WIDTH

    def body(q_ref, k_ref, v_ref, ft_ref, do_ref, lse_ref, dl_ref, dq_ref, df_ref, qm_scr, dom_scr, acc_scr, rs_scr):
        i = pl.program_id(1)
        for p in range(N_PAIRS):
            qe, qo = _split_pair(q_ref[:, p * PAIR:(p + 1) * PAIR], QK_SCALE)
            qm_scr[2 * p] = qe
            qm_scr[2 * p + 1] = qo
            de, dod = _split_pair(do_ref[:, p * PAIR:(p + 1) * PAIR])
            dom_scr[2 * p] = de
            dom_scr[2 * p + 1] = dod
        acc_scr[...] = jnp.zeros_like(acc_scr)
        rs_scr[...] = jnp.zeros_like(rs_scr)

        def tile(j, masked):
            ks = pl.multiple_of(j * tf, tf)
            if masked:
                keep = lax.broadcasted_iota(jnp.int32, (tf, tf), 0) >= lax.broadcasted_iota(jnp.int32, (tf, tf), 1)
            def products(h):
                p = h // 2
                kp = k_ref[pl.ds(ks, tf), p * PAIR:(p + 1) * PAIR]
                vp = v_ref[pl.ds(ks, tf), p * PAIR:(p + 1) * PAIR]
                return (lax.dot_general(qm_scr[h], kp, NT_DIMS, preferred_element_type=F32),
                        lax.dot_general(dom_scr[h], vp, NT_DIMS, preferred_element_type=F32))

            ahead = [products(h) for h in range(LOOKAHEAD_BWD)]
            for h in range(FOX_HEADS):
                sc, dp = ahead.pop(0)
                if h + LOOKAHEAD_BWD < FOX_HEADS:
                    ahead.append(products(h + LOOKAHEAD_BWD))
                kp = k_ref[pl.ds(ks, tf), (h // 2) * PAIR:(h // 2 + 1) * PAIR]
                sc = sc + ft_ref[j, h:h + 1, :] - lse_ref[:, h:h + 1]
                if masked:
                    sc = jnp.where(keep, sc, NEG_BIG)
                pr = jnp.exp(sc)
                ds = pr * (dp - dl_ref[:, h:h + 1])
                part = ds[:, :LANES]
                for c in range(1, tf // LANES):
                    part = part + ds[:, c * LANES:(c + 1) * LANES]
                rs_scr[h] += part
                acc_scr[h] += jnp.dot(ds.astype(BF16), kp, preferred_element_type=F32)

        def step(j, carry):
            tile(j, False)
            return carry

        lax.fori_loop(0, i, step, 0)
        tile(i, True)

        lo = _half_masks(tf)
        dfq = jnp.zeros((tf, LANES), F32)
        for p in range(N_PAIRS):
            dq = jnp.where(lo, acc_scr[2 * p], acc_scr[2 * p + 1]) * QK_SCALE
            dq_ref[:, p * PAIR:(p + 1) * PAIR] = dq.astype(BF16)
            for h in (2 * p, 2 * p + 1):
                dfq = dfq + _lane_put((tf, LANES), h, jnp.sum(rs_scr[h], axis=1, keepdims=True))
        df_ref[...] = dfq

    qrow = lambda b, i: (b * n + i, 0)
    stat = pl.BlockSpec((tf, LANES), qrow)
    return pl.pallas_call(
        body, name="fox_bwd_dq", grid=(nb, n),
        in_specs=[pl.BlockSpec((tf, w), qrow), pl.BlockSpec((s, w), lambda b, i: (b, 0)),
                  pl.BlockSpec((s, w), lambda b, i: (b, 1)), pl.BlockSpec((n, 16, tf), lambda b, i: (b, 0, 0)),
                  pl.BlockSpec((tf, w), qrow), stat, stat],
        out_specs=[pl.BlockSpec((tf, w), qrow), stat],
        out_shape=[SDS((nb * s, w), BF16), SDS((nb * s, LANES), F32)],
        scratch_shapes=[pltpu.VMEM((FOX_HEADS, tf, PAIR), BF16), pltpu.VMEM((FOX_HEADS, tf, PAIR), BF16),
                        pltpu.VMEM((FOX_HEADS, tf, PAIR), F32), pltpu.VMEM((FOX_HEADS, tf, LANES), F32)],
        compiler_params=_params(("parallel", "arbitrary")))(pq, kv, kv, fneg_rows, dcat_bf, lse, delta)


def fox_bwd_dkv(pq, kv, fneg, dcat_bf, lse_rows, delta_rows, nb, s):
    tf = min(TF, s)
    n = s // tf
    w = TOK_WIDTH

    def body(q_ref, k_ref, v_ref, f_ref, do_ref, lse_ref, dl_ref, dk_ref, dv_ref, df_ref,
             km_scr, vm_scr, fk_scr, dk_scr, dv_scr, rs_scr):
        j = pl.program_id(1)
        for p in range(N_PAIRS):
            ke, ko = _split_pair(k_ref[:, p * PAIR:(p + 1) * PAIR], QK_SCALE)
            km_scr[2 * p] = ke
            km_scr[2 * p + 1] = ko
            ve, vo = _split_pair(v_ref[:, p * PAIR:(p + 1) * PAIR])
            vm_scr[2 * p] = ve
            vm_scr[2 * p + 1] = vo
        for h in range(FOX_HEADS):
            fk_scr[h] = jnp.broadcast_to(f_ref[:, h:h + 1], (tf, tf))
        dk_scr[...] = jnp.zeros_like(dk_scr)
        dv_scr[...] = jnp.zeros_like(dv_scr)
        rs_scr[...] = jnp.zeros_like(rs_scr)

        def tile(i, masked):
            qs = pl.multiple_of(i * tf, tf)
            if masked:
                keep = lax.broadcasted_iota(jnp.int32, (tf, tf), 1) >= lax.broadcasted_iota(jnp.int32, (tf, tf), 0)
            def scores(h):
                qp = q_ref[pl.ds(qs, tf), (h // 2) * PAIR:(h // 2 + 1) * PAIR]
                return lax.dot_general(km_scr[h], qp, NT_DIMS, preferred_element_type=F32)

            ahead = [scores(h) for h in range(LOOKAHEAD_DKV)]
            for h in range(FOX_HEADS):
                sc = ahead.pop(0)
                if h + LOOKAHEAD_DKV < FOX_HEADS:
                    ahead.append(scores(h + LOOKAHEAD_DKV))
                p = h // 2
                qp = q_ref[pl.ds(qs, tf), p * PAIR:(p + 1) * PAIR]
                dop = do_ref[pl.ds(qs, tf), p * PAIR:(p + 1) * PAIR]
                sc = sc + fk_scr[h] - lse_ref[i, h:h + 1, :]
                if masked:
                    sc = jnp.where(keep, sc, NEG_BIG)
                pr = jnp.exp(sc)
                dv_scr[h] += jnp.dot(pr.astype(BF16), dop, preferred_element_type=F32)
                dp = lax.dot_general(vm_scr[h], dop, NT_DIMS, preferred_element_type=F32)
                ds = pr * (dp - dl_ref[i, h:h + 1, :])
                part = ds[:, :LANES]
                for c in range(1, tf // LANES):
                    part = part + ds[:, c * LANES:(c + 1) * LANES]
                rs_scr[h] += part
                dk_scr[h] += jnp.dot(ds.astype(BF16), qp, preferred_element_type=F32)

        def step(i, carry):
            tile(i, False)
            return carry

        tile(j, True)
        lax.fori_loop(j + 1, n, step, 0)

        lo = _half_masks(tf)
        dfk = jnp.zeros((tf, LANES), F32)
        for p in range(N_PAIRS):
            dk = jnp.where(lo, dk_scr[2 * p], dk_scr[2 * p + 1]) * QK_SCALE
            dk_ref[:, p * PAIR:(p + 1) * PAIR] = dk.astype(BF16)
            dv_ref[:, p * PAIR:(p + 1) * PAIR] = jnp.where(lo, dv_scr[2 * p], dv_scr[2 * p + 1]).astype(BF16)
            for h in (2 * p, 2 * p + 1):
                dfk = dfk - _lane_put((tf, LANES), h, jnp.sum(rs_scr[h], axis=1, keepdims=True))
        df_ref[...] = dfk

    krow = lambda b, j: (b * n + j, 0)
    rows = pl.BlockSpec((n, 16, tf), lambda b, j: (b, 0, 0))
    return pl.pallas_call(
        body, name="fox_bwd_dkv", grid=(nb, n),
        in_specs=[pl.BlockSpec((s, w), lambda b, j: (b, 0)), pl.BlockSpec((tf, w), krow),
                  pl.BlockSpec((tf, w), lambda b, j: (b * n + j, 1)), pl.BlockSpec((tf, LANES), krow),
                  pl.BlockSpec((s, w), lambda b, j: (b, 0)), rows, rows],
        out_specs=[pl.BlockSpec((tf, w), krow), pl.BlockSpec((tf, w), krow), pl.BlockSpec((tf, LANES), krow)],
        out_shape=[SDS((nb * s, w), BF16), SDS((nb * s, w), BF16), SDS((nb * s, LANES), F32)],
        scratch_shapes=[pltpu.VMEM((FOX_HEADS, tf, PAIR), BF16), pltpu.VMEM((FOX_HEADS, tf, PAIR), BF16),
                        pltpu.VMEM((FOX_HEADS, tf, tf), F32), pltpu.VMEM((FOX_HEADS, tf, PAIR), F32),
                        pltpu.VMEM((FOX_HEADS, tf, PAIR), F32), pltpu.VMEM((FOX_HEADS, tf, LANES), F32)],
        compiler_params=_params(("parallel", "arbitrary")))(pq, kv, kv, fneg, dcat_bf, lse_rows, delta_rows)


def reduce_adamw(parts, w, m, v, name):
    _, r, c = parts.shape
    tr = r
    for cand in range(16, r, 16):
        if r % cand == 0 and cand * c <= 128 * 1024:
            tr = cand
    c1 = 1.0 - ADAM_B1 ** ADAM_STEP
    c2 = 1.0 - ADAM_B2 ** ADAM_STEP

    def body(p_ref, w_ref, m_ref, v_ref, g_out, d_out, m_out, v_out):
        g = p_ref[0].astype(F32)
        for k in range(1, N_DEV):
            g = g + p_ref[k].astype(F32)
        mn = ADAM_B1 * m_ref[...] + (1.0 - ADAM_B1) * g
        vn = ADAM_B2 * v_ref[...] + (1.0 - ADAM_B2) * (g * g)
        g_out[...] = g
        m_out[...] = mn
        v_out[...] = vn
        d_out[...] = -ADAM_LR * ((mn / c1) / (jnp.sqrt(vn / c2) + ADAM_EPS) + ADAM_WD * w_ref[...])

    row = pl.BlockSpec((tr, c), lambda i: (i, 0))
    return pl.pallas_call(
        body, name=name, grid=(r // tr,),
        in_specs=[pl.BlockSpec((N_DEV, tr, c), lambda i: (0, i, 0)), row, row, row],
        out_specs=[row, row, row, row], out_shape=[SDS((r, c), F32)] * 4,
        compiler_params=_params(("parallel",)))(parts, w, m, v)


N_PEERS = N_DEV - 1
HBM_SPEC = pl.BlockSpec(memory_space=pltpu.HBM)
SEM_SPEC = pl.BlockSpec(memory_space=pltpu.SEMAPHORE)
ANY_SPEC = pl.BlockSpec(memory_space=pl.ANY)
SPLIT_EFFECT = pltpu.SideEffectType.DATAFLOW_SIDE_EFFECTING


def _my_index():
    return 4 * lax.axis_index("x") + 2 * lax.axis_index("y") + lax.axis_index("c")


def _peers():
    x, y, c = lax.axis_index("x"), lax.axis_index("y"), lax.axis_index("c")
    peers = []
    for k in range(1, N_DEV):
        px = 1 - x if (k >> 2) & 1 else x
        py = 1 - y if (k >> 1) & 1 else y
        pc = 1 - c if k & 1 else c
        peers.append(((px, py, pc), 4 * px + 2 * py + pc))
    return 4 * x + 2 * y + c, peers


def _push(src, dst, send_sems, recv_sems, slot, dev):
    return pltpu.make_async_remote_copy(src_ref=src, dst_ref=dst, send_sem=send_sems.at[slot], recv_sem=recv_sems.at[slot],
                                        device_id=dev, device_id_type=pl.DeviceIdType.MESH)


def _landing_shapes(arrs, scatter):
    return [SDS((N_DEV,) + tuple(a.shape[1:] if sc else a.shape), a.dtype) for a, sc in zip(arrs, scatter)]


def exchange(arrs, scatter, name):
    na = len(arrs)

    def body(*refs):
        ins = refs[:na]
        outs = refs[na:2 * na]
        send_sems, recv_sems, local_sems = refs[2 * na:]
        me, peers = _peers()
        local = []
        remote = []
        for a in range(na):
            lc = pltpu.make_async_copy(ins[a].at[me] if scatter[a] else ins[a], outs[a].at[me], local_sems.at[a])
            lc.start()
            local.append(lc)
            for k, (dev, idx) in enumerate(peers):
                cp = _push(ins[a].at[idx] if scatter[a] else ins[a], outs[a].at[me], send_sems, recv_sems,
                           a * N_PEERS + k, dev)
                cp.start()
                remote.append(cp)
        for a in range(na):
            for k, (dev, idx) in enumerate(peers):
                _push(ins[a].at[me] if scatter[a] else ins[a], outs[a].at[idx], send_sems, recv_sems,
                      a * N_PEERS + k, dev).wait_recv()
        for cp in remote:
            cp.wait_send()
        for lc in local:
            lc.wait()

    return pl.pallas_call(
        body, name=name, in_specs=[HBM_SPEC] * na, out_specs=[HBM_SPEC] * na, out_shape=_landing_shapes(arrs, scatter),
        scratch_shapes=[pltpu.SemaphoreType.DMA((na * N_PEERS,)), pltpu.SemaphoreType.DMA((na * N_PEERS,)),
                        pltpu.SemaphoreType.DMA((na,))])(*arrs)


def exchange_start(arrs, scatter, after, name):
    na = len(arrs)
    lands = [lax.empty(l.shape, l.dtype) for l in _landing_shapes(arrs, scatter)]

    def body(*refs):
        ins = refs[:na]
        land = refs[na:2 * na]
        send_sems, recv_sems = refs[2 * na + 1], refs[2 * na + 2]
        token = refs[-1]
        me, peers = _peers()
        for a in range(na):
            for k, (dev, idx) in enumerate(peers):
                _push(ins[a].at[idx] if scatter[a] else ins[a], land[a].at[me], send_sems, recv_sems,
                      a * N_PEERS + k, dev).start()
        token[...] = jnp.zeros_like(token)

    thru = [pltpu.HBM(a.shape, a.dtype) for a in arrs] + [pltpu.HBM(l.shape, l.dtype) for l in lands]
    res = pl.pallas_call(
        body, name=name,
        out_shape=(pltpu.SemaphoreType.DMA((na * N_PEERS,)), pltpu.SemaphoreType.DMA((na * N_PEERS,)), *thru,
                   SDS((8, LANES), F32)),
        in_specs=[HBM_SPEC] * (2 * na) + [ANY_SPEC],
        out_specs=(SEM_SPEC, SEM_SPEC, *([HBM_SPEC] * (2 * na)), pl.BlockSpec(memory_space=pltpu.VMEM)),
        input_output_aliases={i: 2 + i for i in range(2 * na)},
        compiler_params=pltpu.CompilerParams(has_side_effects=SPLIT_EFFECT),
    )(*[pltpu.with_memory_space_constraint(a, pltpu.HBM) for a in arrs],
      *[pltpu.with_memory_space_constraint(l, pltpu.HBM) for l in lands], after)
    return {"send": res[0], "recv": res[1], "src": res[2:2 + na], "land": res[2 + na:2 + 2 * na],
            "token": res[-1][0, 0], "scatter": scatter}


def exchange_wait(handle, after, name):
    scatter = handle["scatter"]
    na = len(scatter)

    def body(*refs):
        src = refs[:na]
        land = refs[na:2 * na]
        send_sems, recv_sems = refs[2 * na], refs[2 * na + 1]
        me, peers = _peers()
        for a in range(na):
            for k, (dev, idx) in enumerate(peers):
                cp = _push(src[a].at[me] if scatter[a] else src[a], land[a].at[idx], send_sems, recv_sems,
                           a * N_PEERS + k, dev)
                cp.wait_send()
                cp.wait_recv()

    ops = list(handle["src"]) + list(handle["land"])
    res = pl.pallas_call(
        body, name=name, out_shape=tuple(pltpu.HBM(o.shape, o.dtype) for o in ops),
        in_specs=[HBM_SPEC] * (2 * na) + [SEM_SPEC, SEM_SPEC, ANY_SPEC], out_specs=tuple([HBM_SPEC] * (2 * na)),
        input_output_aliases={i: i for i in range(2 * na)},
        compiler_params=pltpu.CompilerParams(has_side_effects=SPLIT_EFFECT),
    )(*ops, handle["send"], handle["recv"], after)
    me = _my_index()
    out = []
    for a in range(na):
        own = lax.dynamic_index_in_dim(res[a], me, 0, keepdims=True) if scatter[a] else res[a][None]
        out.append(lax.dynamic_update_slice(res[na + a], own, (me,) + (0,) * (own.ndim - 1)))
    return out


def forward_layer(l, xin, xin_bf, mem_bf, wt, nb, s):
    sv = {"xin_bf": xin_bf}
    memkv = mm_nn(mem_bf, wt["memw"], BF16, f"memkv{l}")
    sv["memkv"] = memkv
    if l == 0:
        proj = mm_nn(xin_bf, wt["win_a"], F32, "proj_a")
        pooled, tok = pool_fwd(proj, wt["pw_bd"], wt["pscale"], nb, s)
        sv["pooled"] = pooled
    else:
        kv = mm_nn(xin_bf, wt["kvw"][:, :2 * TOK_WIDTH], BF16, "kv_proj")
        fl = mm_nn(xin_bf, wt["kvw"][:, 2 * TOK_WIDTH:], F32, "gate_proj")
        fneg = -fgate_fwd(fl, wt["fb"], nb, s)
        fneg_rows = _to_tile_rows(fneg, nb, s, min(TF, s))
        proj = mm_nn(xin_bf, wt["wq"], BF16, "proj_b")
        kaug, vaug_t = fox_prep(kv, fneg, nb, s)
        tok, o_f32, lse_rows = fox_fwd_t(proj, kaug, vaug_t, nb, s)
        sv.update(kv=kv, fl=fl, fneg=fneg, fneg_rows=fneg_rows, o_f32=o_f32, lse=_from_tile_rows(lse_rows), lse_rows=lse_rows)
    sv["proj"] = proj
    mem_out = memattn_fwd(proj, memkv, nb, s, f"memattn_fwd{l}")
    cat = jnp.concatenate([tok, mem_out], axis=1)
    sv["cat"] = cat
    mix = mm_nn(cat, wt["wout"], F32, f"out_proj{l}")
    x1, x1_bf, xh1, rs1 = ln_fwd(xin, mix, wt["ln1_g"], wt["ln1_b"], f"ln1_fwd{l}")
    sv.update(x1_bf=x1_bf, xh1=xh1, rs1=rs1)
    h = mm_nn(x1_bf, wt["wup"], F32, f"ffn_up{l}")
    act = convgate_fwd(h, wt["cw"], nb, s, f"convgate_fwd{l}")
    sv.update(h=h, act=act)
    ffn = mm_nn(act, wt["wdown"], F32, f"ffn_down{l}")
    x2, x2_bf, xh2, rs2 = ln_fwd(x1, ffn, wt["ln2_g"], wt["ln2_b"], f"ln2_fwd{l}")
    sv.update(xh2=xh2, rs2=rs2)
    return x2, x2_bf, sv


def backward_layer(l, dy, sv, mem_bf, wt, nb, s):
    g = {}
    dr2, dr2_bf, g["ln2_g"], g["ln2_b"] = ln_bwd(dy, sv["xh2"], sv["rs2"], wt["ln2_g"], f"ln2_bwd{l}")
    dact = mm_nn(dr2_bf, wt["wdown_t"], BF16, f"ffn_down_dx{l}")
    g["wdown"] = mm_tn(sv["act"], dr2_bf, f"ffn_down_dw{l}")
    dh_u, dh_g, dcw_u, dcw_g = convgate_bwd(sv["h"], dact, wt["cw"], nb, s, f"convgate_bwd{l}")
    g["cw"] = jnp.concatenate([dcw_u, dcw_g], axis=0)
    half = FF_PAIRS * FF_BLOCK_PAD
    dx1 = mm_nn(dh_u, wt["wup_t"][:half], F32, f"ffn_up_dx_u{l}", addend=dr2, add_scale=DN_ALPHA)
    dx1 = mm_nn(dh_g, wt["wup_t"][half:], F32, f"ffn_up_dx_g{l}", addend=dx1)
    g["wup"] = jnp.concatenate([mm_tn(sv["x1_bf"], dh_u, f"ffn_up_dw_u{l}", blocked=True),
                                mm_tn(sv["x1_bf"], dh_g, f"ffn_up_dw_g{l}", blocked=True)], axis=0)
    dr1, dr1_bf, g["ln1_g"], g["ln1_b"] = ln_bwd(dx1, sv["xh1"], sv["rs1"], wt["ln1_g"], f"ln1_bwd{l}")
    dcat, dcat_bf = mm_nn(dr1_bf, wt["wout_t"], F32, f"out_proj_dx{l}", also_bf16=True)
    g["wout"] = mm_tn(sv["cat"], dr1_bf, f"out_proj_dw{l}")
    dqm, dmemkv = memattn_bwd(sv["proj"], sv["memkv"], dcat, nb, s, f"memattn_bwd{l}")
    g["memw"] = mm_tn(mem_bf, dmemkv, f"memkv_dw{l}")
    if l == 0:
        dmixed, dpooled, g["pscale"] = pool_bwd_mix(dcat, sv["pooled"], wt["pw_bd"], wt["pw_bd_t"], wt["pscale"], nb, s)
        g["pw_full"] = mm_tn(sv["pooled"], dmixed, "pool_dw")
        du = pool_bwd_window(dpooled, nb, s)
        dproj = jnp.concatenate([du, dqm], axis=1)
        dx = mm_nn(dproj, wt["win_a_t"], F32, "proj_a_dx", addend=dr1, add_scale=DN_ALPHA)
        g["win_a"] = mm_tn(sv["xin_bf"], dproj, "proj_a_dw")
    else:
        delta = fox_delta(dcat, sv["o_f32"], nb, s)
        tf = min(TF, s)
        dq, dfcum_q = fox_bwd_dq(sv["proj"], sv["kv"], sv["fneg_rows"], dcat_bf, sv["lse"], delta, nb, s)
        dk, dv, dfcum_k = fox_bwd_dkv(sv["proj"], sv["kv"], sv["fneg"], dcat_bf,
                                      sv["lse_rows"], _to_tile_rows(delta, nb, s, tf), nb, s)
        dfl, g["fb"] = fgate_bwd(dfcum_q, dfcum_k, sv["fl"], wt["fb"], nb, s)
        dproj = jnp.concatenate([dq, dqm], axis=1)
        dkvf = jnp.concatenate([dk, dv, dfl.astype(BF16)], axis=1)
        dx = mm_nn(dproj, wt["wq_t"], F32, "proj_b_dx", addend=dr1, add_scale=DN_ALPHA)
        dx = mm_nn(dkvf, wt["kvw_t"], F32, "kv_proj_dx", addend=dx)
        g["wq"] = mm_tn(sv["xin_bf"], dproj, "proj_b_dw")
        g["kvw"] = mm_tn(sv["xin_bf"], dkvf, "kv_proj_dw")
    return dx, g


def pack_replicated(pool_w, ln1_g, ln1_b, ln2_g, ln2_b, conv_b, f_b):
    cb = jnp.pad(conv_b, ((0, 0), (0, 6144 - 5504))).reshape(12, D_MODEL)
    fb = jnp.pad(f_b.reshape(1, FOX_HEADS), ((0, 3), (0, D_MODEL - FOX_HEADS)))
    return jnp.concatenate([pool_w.reshape(144, D_MODEL), ln1_g, ln1_b, ln2_g, ln2_b, cb, fb], axis=0)


def unpack_replicated(buf):
    pool_w = buf[:144].reshape(1, 4, POOL_GROUP, POOL_GROUP)
    ln = [buf[144 + 2 * k:146 + 2 * k] for k in range(4)]
    conv_b = buf[152:164].reshape(2, 6144)[:, :5504]
    f_b = buf[164, :FOX_HEADS]
    return pool_w, ln[0], ln[1], ln[2], ln[3], conv_b, f_b


def pack_small(conv_w, pool_scale):
    buf = jnp.zeros((16, FF_BLOCK_PAD), F32)
    buf = lax.dynamic_update_slice(buf, conv_w.reshape(DEPTH * 3, FF_BLOCK), (0, 0))
    return lax.dynamic_update_slice(buf, pool_scale, (8, 0))


def _block_diag(pw):
    out = jnp.zeros((TOK_WIDTH, TOK_WIDTH), pw.dtype)
    for g in range(4):
        out = lax.dynamic_update_slice(out, pw[g], (g * POOL_GROUP, g * POOL_GROUP))
    return out


def layer_shards(l, sq_a, sq_b, mem_w_kv, ffn_w_up, ffn_w_down):
    return [sq_a[0].astype(BF16), sq_b[0].astype(BF16), mem_w_kv[l].astype(BF16), ffn_w_up[l].astype(BF16),
            ffn_w_down[l].astype(BF16)]


def layer_weights(l, gath, small, conv_b, ln1_g, ln1_b, ln2_g, ln2_b):
    w_in = gath[0].reshape(D_MODEL, D_MODEL)
    w_out = gath[1].reshape(D_MODEL, D_MODEL)
    pad_c = FF_BLOCK_PAD - FF_BLOCK
    wup = jnp.pad(gath[3], ((0, 0), (0, 0), (0, pad_c))).transpose(1, 0, 2).reshape(D_MODEL, N_DEV * FF_BLOCK_PAD)
    wdown = jnp.pad(gath[4].reshape(FF_PAIRS, FF_BLOCK, D_MODEL), ((0, 0), (0, pad_c), (0, 0)))
    wdown = wdown.reshape(FF_PAIRS * FF_BLOCK_PAD, D_MODEL)
    cb = jnp.pad(conv_b[l].reshape(N_DEV, FF_BLOCK), ((0, 0), (0, pad_c)))
    cw = jnp.concatenate([small[:, 3 * l:3 * l + 3, :], cb[:, None, :], jnp.zeros((N_DEV, 4, FF_BLOCK_PAD), F32)], axis=1)
    wt = {"memw": gath[2].reshape(D_MODEL, 2 * MEM_WIDTH), "wout": w_out, "wout_t": w_out.T,
          "wup": wup, "wup_t": wup.T, "wdown": wdown, "wdown_t": wdown.T, "cw": cw,
          "ln1_g": ln1_g[l:l + 1], "ln1_b": ln1_b[l:l + 1], "ln2_g": ln2_g[l:l + 1], "ln2_b": ln2_b[l:l + 1]}
    return wt, w_in


def layer_grad_blocks(g, w_in_grad):
    wup = g["wup"][:, :, :FF_BLOCK]
    wdown = g["wdown"].reshape(FF_PAIRS, FF_BLOCK_PAD, D_MODEL)[:, :FF_BLOCK].reshape(N_DEV, FF_ROWS, D_MODEL)
    blocks = [w_in_grad.reshape(N_DEV, 128, D_MODEL), g["wout"].reshape(N_DEV, 128, D_MODEL),
              g["memw"].reshape(N_DEV, 128, 2 * MEM_WIDTH), wup, wdown]
    return [b.astype(BF16) for b in blocks]


def small_grad_blocks(g0, g1):
    taps = jnp.stack([g0["cw"][:, :3, :], g1["cw"][:, :3, :]], axis=1).reshape(N_DEV, DEPTH * 3, FF_BLOCK_PAD)
    small = jnp.zeros((N_DEV, 16, FF_BLOCK_PAD), F32)
    small = lax.dynamic_update_slice(small, taps, (0, 0, 0))
    return lax.dynamic_update_slice(small, g0["pscale"].reshape(N_DEV, 1, 96), (0, 8, 0))


def replicated_grads(g0, g1):
    pw = jnp.stack([g0["pw_full"][k * POOL_GROUP:(k + 1) * POOL_GROUP, k * POOL_GROUP:(k + 1) * POOL_GROUP] for k in range(4)])
    conv_b = jnp.stack([g_["cw"][:, 3, :FF_BLOCK].reshape(N_DEV * FF_BLOCK) for g_ in (g0, g1)])
    ln = [jnp.concatenate([g0[n], g1[n]], axis=0) for n in ("ln1_g", "ln1_b", "ln2_g", "ln2_b")]
    return pack_replicated(pw[None], ln[0], ln[1], ln[2], ln[3], conv_b, g1["fb"][0, :FOX_HEADS])


def kernel(x, mem, a_w_in, a_pool_w, a_pool_scale, a_w_out, b_w_q, b_w_out, kv_w, f_b, mem_w_kv, ln1_g, ln1_b, ln2_g, ln2_b, ffn_w_up, ffn_conv_w, ffn_conv_b, ffn_w_down, loss_target, m_a_w_in, m_a_pool_w, m_a_pool_scale, m_a_w_out, m_b_w_q, m_b_w_out, m_kv_w, m_f_b, m_mem_w_kv, m_ln1_g, m_ln1_b, m_ln2_g, m_ln2_b, m_ffn_w_up, m_ffn_conv_w, m_ffn_conv_b, m_ffn_w_down, v_a_w_in, v_a_pool_w, v_a_pool_scale, v_a_w_out, v_b_w_q, v_b_w_out, v_kv_w, v_f_b, v_mem_w_kv, v_ln1_g, v_ln1_b, v_ln2_g, v_ln2_b, v_ffn_w_up, v_ffn_conv_w, v_ffn_conv_b, v_ffn_w_down):
    nb, s, d = x.shape
    t = nb * s
    x2d, mem_bf, target = x.reshape(t, d), mem.reshape(nb * MEM_LEN, d).astype(BF16), loss_target.reshape(t, d)

    shards0 = layer_shards(0, a_w_in, a_w_out, mem_w_kv, ffn_w_up, ffn_w_down) + [pack_small(ffn_conv_w, a_pool_scale)]
    shards1 = layer_shards(1, b_w_q, b_w_out, mem_w_kv, ffn_w_up, ffn_w_down)
    shards1.append(jnp.pad(kv_w, ((0, 0), (0, KV_COLS_PAD - KV_COLS))).astype(BF16))
    gath0 = exchange(shards0, [False] * 6, "gather_w0")
    pull1 = exchange_start(shards1, [False] * 6, gath0[0], "gather_w1_start")
    small = gath0[5]
    wt0, w_in = layer_weights(0, gath0, small, ffn_conv_b, ln1_g + pull1["token"], ln1_b, ln2_g, ln2_b)
    pw_bd = _block_diag(a_pool_w[0])
    wt0.update(win_a=w_in, win_a_t=w_in.T, pw_bd=pw_bd.astype(BF16), pw_bd_t=pw_bd.T.astype(BF16),
               pscale=small[:, 8, :96].reshape(1, TOK_WIDTH))

    x1, x1_bf, sv0 = forward_layer(0, x2d, x2d.astype(BF16), mem_bf, wt0, nb, s)
    gath1 = exchange_wait(pull1, x1_bf, "gather_w1_wait")
    wt1, w_q = layer_weights(1, gath1, small, ffn_conv_b, ln1_g, ln1_b, ln2_g, ln2_b)
    kvw = gath1[5].reshape(D_MODEL, KV_COLS_PAD)
    wt1.update(wq=w_q, wq_t=w_q.T, kvw=kvw, kvw_t=kvw.T,
               fb=jnp.pad(f_b.reshape(1, FOX_HEADS), ((0, 0), (0, LANES - FOX_HEADS))))
    y, _, sv1 = forward_layer(1, x1, x1_bf, mem_bf, wt1, nb, s)
    dy, loss_row = loss_head(y, target)
    loss = lax.psum(loss_row[0, 0], ("x", "y", "c"))

    dx1, g1 = backward_layer(1, dy, sv1, mem_bf, wt1, nb, s)
    blocks1 = layer_grad_blocks(g1, g1["wq"]) + [g1["kvw"][:, :KV_COLS].reshape(N_DEV, 128, KV_COLS).astype(BF16)]
    push1 = exchange_start(blocks1, [True] * 6, dx1, "scatter_g1_start")
    wt0["ln2_g"] = wt0["ln2_g"] + push1["token"]
    grad_x, g0 = backward_layer(0, dx1, sv0, mem_bf, wt0, nb, s)
    blocks0 = layer_grad_blocks(g0, g0["win_a"]) + [small_grad_blocks(g0, g1), replicated_grads(g0, g1)]
    parts0 = exchange(blocks0, [True] * 6 + [False], "scatter_g0")
    parts1 = exchange_wait(push1, parts0[0], "scatter_g1_wait")

    res = {}

    def upd(nm, parts, w2, m2, v2):
        res[nm] = reduce_adamw(parts, w2, m2, v2, f"adamw_{nm}")

    upd("a_w_in", parts0[0], a_w_in[0], m_a_w_in[0], v_a_w_in[0])
    upd("a_w_out", parts0[1], a_w_out[0], m_a_w_out[0], v_a_w_out[0])
    upd("b_w_q", parts1[0], b_w_q[0], m_b_w_q[0], v_b_w_q[0])
    upd("b_w_out", parts1[1], b_w_out[0], m_b_w_out[0], v_b_w_out[0])
    upd("kv_w", parts1[5], kv_w, m_kv_w, v_kv_w)
    for l, parts in enumerate((parts0, parts1)):
        upd(f"mem_w_kv{l}", parts[2], mem_w_kv[l], m_mem_w_kv[l], v_mem_w_kv[l])
        upd(f"ffn_w_up{l}", parts[3], ffn_w_up[l], m_ffn_w_up[l], v_ffn_w_up[l])
        upd(f"ffn_w_down{l}", parts[4], ffn_w_down[l], m_ffn_w_down[l], v_ffn_w_down[l])
    upd("small", parts0[5], pack_small(ffn_conv_w, a_pool_scale), pack_small(m_ffn_conv_w, m_a_pool_scale),
        pack_small(v_ffn_conv_w, v_a_pool_scale))
    upd("replicated", parts0[6], pack_replicated(a_pool_w, ln1_g, ln1_b, ln2_g, ln2_b, ffn_conv_b, f_b),
        pack_replicated(m_a_pool_w, m_ln1_g, m_ln1_b, m_ln2_g, m_ln2_b, m_ffn_conv_b, m_f_b),
        pack_replicated(v_a_pool_w, v_ln1_g, v_ln1_b, v_ln2_g, v_ln2_b, v_ffn_conv_b, v_f_b))

    for nm in ("a_w_in", "a_w_out", "b_w_q", "b_w_out"):
        res[nm] = [o[None] for o in res[nm]]
    for nm in ("mem_w_kv", "ffn_w_up", "ffn_w_down"):
        res[nm] = [jnp.stack([a0, a1]) for a0, a1 in zip(res[nm + "0"], res[nm + "1"])]
    res["ffn_conv_w"] = [o[:DEPTH * 3, :FF_BLOCK].reshape(DEPTH, 3, FF_BLOCK) for o in res["small"]]
    res["a_pool_scale"] = [o[8:9, :96] for o in res["small"]]
    rep_names = ["a_pool_w", "ln1_g", "ln1_b", "ln2_g", "ln2_b", "ffn_conv_b", "f_b"]
    for nm in rep_names:
        res[nm] = []
    for o in res["replicated"]:
        for nm, val in zip(rep_names, unpack_replicated(o)):
            res[nm].append(val)

    order = ["a_w_in", "a_pool_w", "a_pool_scale", "a_w_out", "b_w_q", "b_w_out", "kv_w", "f_b", "mem_w_kv",
             "ln1_g", "ln1_b", "ln2_g", "ln2_b", "ffn_w_up", "ffn_conv_w", "ffn_conv_b", "ffn_w_down"]
    out = [loss, grad_x.reshape(nb, s, d)]
    for kind in range(4):
        out.extend(res[nm][kind] for nm in order)
    return tuple(out)
```

```python
import jax
import jax.numpy as jnp
from jax import lax
from jax.experimental import pallas as pl
from jax.experimental.pallas import tpu as pltpu

F32 = jnp.float32
BF16 = jnp.bfloat16
SDS = jax.ShapeDtypeStruct

N_DEV = 8
D_MODEL = 1024
TOK_WIDTH = 768
MEM_WIDTH = 256
MEM_LEN = 256
MEM_HEADS = 4
HEAD_DIM = 64
FOX_HEADS = 12
POOL_GROUP = 192
FF_BLOCK = 688
FF_BLOCK_PAD = 768
FF_PAIRS = 4
FF_ROWS = 344
KV_COLS = 1548
KV_COLS_PAD = 1664
LANES = 128
DEPTH = 2
DN_ALPHA = (2.0 * DEPTH) ** 0.25
LN_EPS = 1e-5
QK_SCALE = HEAD_DIM ** -0.5
NEG_BIG = -1e30

ADAM_LR = 0.001
ADAM_B1 = 0.9
ADAM_B2 = 0.999
ADAM_EPS = 1e-08
ADAM_WD = 0.01
ADAM_STEP = 10

VMEM_LIMIT_BYTES = 56 * 1024 * 1024
TM = 512
TS = 256
TF = 256
TC = 256
HALO_POOL = 16
HALO_CONV = 8

NT_DIMS = (((1,), (1,)), ((), ()))
TN_DIMS = (((0,), (0,)), ((), ()))


def _params(sem=None):
    return pltpu.CompilerParams(dimension_semantics=sem, vmem_limit_bytes=VMEM_LIMIT_BYTES)


def _sigmoid(z):
    return 1.0 / (1.0 + jnp.exp(-z))


def _pick_tn(n):
    if n <= 2048:
        return n
    for t in (1024, 768, 512, 256, 128):
        if n % t == 0:
            return t
    return n


def mm_nn(a, b, out_dtype, name, addend=None, add_scale=1.0, also_bf16=False):
    m, k = a.shape
    _, n = b.shape
    tm = min(TM, m)
    tn = _pick_tn(n)
    tk = k if k <= 2048 else 1024
    nk = k // tk
    has_add = addend is not None

    def body(*refs):
        a_ref, b_ref = refs[0], refs[1]
        pos = 2
        c_ref = None
        if has_add:
            c_ref = refs[pos]
            pos += 1
        o_ref = refs[pos]
        ob_ref = refs[pos + 1] if also_bf16 else None
        acc = refs[-1]
        kk = pl.program_id(2)

        @pl.when(kk == 0)
        def _():
            acc[...] = jnp.zeros_like(acc)

        acc[...] += jnp.dot(a_ref[...].astype(BF16), b_ref[...].astype(BF16), preferred_element_type=F32)

        @pl.when(kk == nk - 1)
        def _():
            r = acc[...]
            if has_add:
                r = r + add_scale * c_ref[...]
            o_ref[...] = r.astype(out_dtype)
            if also_bf16:
                ob_ref[...] = r.astype(BF16)

    in_specs = [pl.BlockSpec((tm, tk), lambda i, j, kk: (i, kk)),
                pl.BlockSpec((tk, tn), lambda i, j, kk: (kk, j))]
    ops = [a, b]
    if has_add:
        in_specs.append(pl.BlockSpec((tm, tn), lambda i, j, kk: (i, j)))
        ops.append(addend)
    out_shape = [SDS((m, n), out_dtype)]
    out_specs = [pl.BlockSpec((tm, tn), lambda i, j, kk: (i, j))]
    if also_bf16:
        out_shape.append(SDS((m, n), BF16))
        out_specs.append(pl.BlockSpec((tm, tn), lambda i, j, kk: (i, j)))
    res = pl.pallas_call(
        body, name=name, grid=(m // tm, n // tn, nk), in_specs=in_specs, out_specs=out_specs, out_shape=out_shape,
        scratch_shapes=[pltpu.VMEM((tm, tn), F32)],
        compiler_params=_params(("parallel", "parallel", "arbitrary")))(*ops)
    return tuple(res) if also_bf16 else res[0]


def mm_tn(a, b, name, blocked=False):
    t, m = a.shape
    _, n = b.shape
    tt = min(TM, t)
    tm = 1024 if m % 1024 == 0 else m
    tn = FF_BLOCK_PAD if blocked else _pick_tn(n)
    nt = t // tt

    def body(a_ref, b_ref, o_ref):
        kk = pl.program_id(2)
        r = lax.dot_general(a_ref[...].astype(BF16), b_ref[...].astype(BF16), TN_DIMS, preferred_element_type=F32)
        if blocked:
            r = r[None]

        @pl.when(kk == 0)
        def _():
            o_ref[...] = r

        @pl.when(kk != 0)
        def _():
            o_ref[...] += r

    if blocked:
        out_shape = SDS((n // tn, m, tn), F32)
        out_spec = pl.BlockSpec((1, tm, tn), lambda i, j, kk: (j, i, 0))
    else:
        out_shape = SDS((m, n), F32)
        out_spec = pl.BlockSpec((tm, tn), lambda i, j, kk: (i, j))
    return pl.pallas_call(
        body, name=name, grid=(m // tm, n // tn, nt),
        in_specs=[pl.BlockSpec((tt, tm), lambda i, j, kk: (kk, i)), pl.BlockSpec((tt, tn), lambda i, j, kk: (kk, j))],
        out_specs=out_spec, out_shape=out_shape,
        compiler_params=_params(("parallel", "parallel", "arbitrary")))(a, b)


def ln_fwd(xprev, delta, g, b, name):
    t, d = xprev.shape
    tm = min(TM, t)

    def body(xp_ref, dl_ref, g_ref, b_ref, y_ref, yb_ref, xh_ref, rs_ref):
        r = DN_ALPHA * xp_ref[...] + dl_ref[...]
        mu = jnp.mean(r, axis=1, keepdims=True)
        xc = r - mu
        var = jnp.mean(xc * xc, axis=1, keepdims=True)
        rstd = lax.rsqrt(var + LN_EPS)
        xh = xc * rstd
        y = xh * g_ref[...] + b_ref[...]
        y_ref[...] = y
        yb_ref[...] = y.astype(BF16)
        xh_ref[...] = xh
        rs_ref[...] = jnp.broadcast_to(rstd, (tm, LANES))

    row = pl.BlockSpec((tm, d), lambda i: (i, 0))
    vec = pl.BlockSpec((1, d), lambda i: (0, 0))
    return pl.pallas_call(
        body, name=name, grid=(t // tm,), in_specs=[row, row, vec, vec],
        out_specs=[row, row, row, pl.BlockSpec((tm, LANES), lambda i: (i, 0))],
        out_shape=[SDS((t, d), F32), SDS((t, d), BF16), SDS((t, d), F32), SDS((t, LANES), F32)],
        compiler_params=_params(("parallel",)))(xprev, delta, g, b)


def ln_bwd(dy, xhat, rstd, g, name):
    t, d = dy.shape
    tm = min(TM, t)

    def body(dy_ref, xh_ref, rs_ref, g_ref, dr_ref, drb_ref, dg_ref, db_ref):
        i = pl.program_id(0)
        dyv = dy_ref[...]
        xh = xh_ref[...]
        dxh = dyv * g_ref[...]
        m1 = jnp.mean(dxh, axis=1, keepdims=True)
        m2 = jnp.mean(dxh * xh, axis=1, keepdims=True)
        dr = rs_ref[:, 0:1] * (dxh - m1 - xh * m2)
        dr_ref[...] = dr
        drb_ref[...] = dr.astype(BF16)

        @pl.when(i == 0)
        def _():
            dg_ref[...] = jnp.zeros_like(dg_ref)
            db_ref[...] = jnp.zeros_like(db_ref)

        dg_ref[...] += jnp.sum(dyv * xh, axis=0, keepdims=True)
        db_ref[...] += jnp.sum(dyv, axis=0, keepdims=True)

    row = pl.BlockSpec((tm, d), lambda i: (i, 0))
    vec = pl.BlockSpec((1, d), lambda i: (0, 0))
    return pl.pallas_call(
        body, name=name, grid=(t // tm,),
        in_specs=[row, row, pl.BlockSpec((tm, LANES), lambda i: (i, 0)), vec],
        out_specs=[row, row, vec, vec],
        out_shape=[SDS((t, d), F32), SDS((t, d), BF16), SDS((1, d), F32), SDS((1, d), F32)],
        compiler_params=_params(("arbitrary",)))(dy, xhat, rstd, g)


def loss_head(y, target):
    t, d = y.shape
    tm = min(TM, t)
    nsteps = t // tm

    def body(y_ref, t_ref, dy_ref, l_ref, acc):
        i = pl.program_id(0)
        diff = y_ref[...] - t_ref[...]
        dy_ref[...] = diff * (1.0 / d)

        @pl.when(i == 0)
        def _():
            acc[...] = jnp.zeros_like(acc)

        acc[...] += jnp.sum(diff * diff, axis=0, keepdims=True)

        @pl.when(i == nsteps - 1)
        def _():
            tot = jnp.sum(acc[...], axis=1, keepdims=True) * (0.5 / d)
            l_ref[...] = jnp.broadcast_to(tot, (1, LANES))

    row = pl.BlockSpec((tm, d), lambda i: (i, 0))
    return pl.pallas_call(
        body, name="loss_head", grid=(nsteps,), in_specs=[row, row],
        out_specs=[row, pl.BlockSpec((1, LANES), lambda i: (0, 0))],
        out_shape=[SDS((t, d), F32), SDS((1, LANES), F32)],
        scratch_shapes=[pltpu.VMEM((1, d), F32)],
        compiler_params=_params(("arbitrary",)))(y, target)


def memattn_fwd(proj, memkv, nb, s, name):
    ts = min(TS, s)
    nq = s // ts

    def body(q_ref, kv_ref, o_ref):
        for h in range(MEM_HEADS):
            lo, hi = h * HEAD_DIM, (h + 1) * HEAD_DIM
            qh = q_ref[:, lo:hi].astype(BF16)
            kh = kv_ref[:, lo:hi]
            vh = kv_ref[:, MEM_WIDTH + lo:MEM_WIDTH + hi]
            sc = lax.dot_general(qh, kh, NT_DIMS, preferred_element_type=F32) * QK_SCALE
            p = jnp.exp(sc - jnp.max(sc, axis=1, keepdims=True))
            p = p / jnp.sum(p, axis=1, keepdims=True)
            o_ref[:, lo:hi] = jnp.dot(p.astype(BF16), vh, preferred_element_type=F32).astype(BF16)

    return pl.pallas_call(
        body, name=name, grid=(nb, nq),
        in_specs=[pl.BlockSpec((ts, MEM_WIDTH), lambda b, i: (b * nq + i, 3)),
                  pl.BlockSpec((MEM_LEN, 2 * MEM_WIDTH), lambda b, i: (b, 0))],
        out_specs=pl.BlockSpec((ts, MEM_WIDTH), lambda b, i: (b * nq + i, 0)),
        out_shape=SDS((nb * s, MEM_WIDTH), BF16),
        compiler_params=_params(("parallel", "parallel")))(proj, memkv)


def memattn_bwd(proj, memkv, dcat, nb, s, name):
    ts = min(TS, s)
    nq = s // ts

    def body(q_ref, kv_ref, do_ref, dq_ref, dkv_ref):
        i = pl.program_id(1)

        @pl.when(i == 0)
        def _():
            dkv_ref[...] = jnp.zeros_like(dkv_ref)

        for h in range(MEM_HEADS):
            lo, hi = h * HEAD_DIM, (h + 1) * HEAD_DIM
            qh = q_ref[:, lo:hi].astype(BF16)
            kh = kv_ref[:, lo:hi]
            vh = kv_ref[:, MEM_WIDTH + lo:MEM_WIDTH + hi]
            doh = do_ref[:, lo:hi].astype(BF16)
            sc = lax.dot_general(qh, kh, NT_DIMS, preferred_element_type=F32) * QK_SCALE
            p = jnp.exp(sc - jnp.max(sc, axis=1, keepdims=True))
            p = p / jnp.sum(p, axis=1, keepdims=True)
            dv = lax.dot_general(p.astype(BF16), doh, TN_DIMS, preferred_element_type=F32)
            dp = lax.dot_general(doh, vh, NT_DIMS, preferred_element_type=F32)
            dl = jnp.sum(p * dp, axis=1, keepdims=True)
            ds = (p * (dp - dl) * QK_SCALE).astype(BF16)
            dq_ref[:, lo:hi] = jnp.dot(ds, kh, preferred_element_type=F32).astype(BF16)
            dkv_ref[:, lo:hi] += lax.dot_general(ds, qh, TN_DIMS, preferred_element_type=F32)
            dkv_ref[:, MEM_WIDTH + lo:MEM_WIDTH + hi] += dv

    return pl.pallas_call(
        body, name=name, grid=(nb, nq),
        in_specs=[pl.BlockSpec((ts, MEM_WIDTH), lambda b, i: (b * nq + i, 3)),
                  pl.BlockSpec((MEM_LEN, 2 * MEM_WIDTH), lambda b, i: (b, 0)),
                  pl.BlockSpec((ts, MEM_WIDTH), lambda b, i: (b * nq + i, 3))],
        out_specs=[pl.BlockSpec((ts, MEM_WIDTH), lambda b, i: (b * nq + i, 0)),
                   pl.BlockSpec((MEM_LEN, 2 * MEM_WIDTH), lambda b, i: (b, 0))],
        out_shape=[SDS((nb * s, MEM_WIDTH), BF16), SDS((nb * MEM_LEN, 2 * MEM_WIDTH), F32)],
        compiler_params=_params(("parallel", "arbitrary")))(proj, memkv, dcat)


def _pool_select(shape, s2, s4, s8, s16):
    lane = lax.broadcasted_iota(jnp.int32, shape, 1)
    return jnp.where(lane < POOL_GROUP, s2, jnp.where(lane < 2 * POOL_GROUP, s4, jnp.where(lane < 3 * POOL_GROUP, s8, s16)))


def _pool_count(shape, first_pos):
    pos = first_pos + lax.broadcasted_iota(jnp.int32, shape, 0)
    win = _pool_select(shape, 2, 4, 8, 16)
    return jnp.minimum(pos + 1, win).astype(F32)


def pool_fwd(proj, pw_bd, pscale, nb, s):
    ts = min(TS, s)
    nq = s // ts
    w = TOK_WIDTH

    def body(c_ref, h_ref, w_ref, sc_ref, pooled_ref, tok_ref):
        i = pl.program_id(0) % nq
        cur = c_ref[...]
        halo = jnp.where(i == 0, 0.0, h_ref[...])
        xe = jnp.concatenate([halo, cur], axis=0)
        s2 = xe + pltpu.roll(xe, 1, axis=0)
        s4 = s2 + pltpu.roll(s2, 2, axis=0)
        s8 = s4 + pltpu.roll(s4, 4, axis=0)
        s16 = s8 + pltpu.roll(s8, 8, axis=0)
        hp = HALO_POOL
        ws = _pool_select((ts, w), s2[hp:], s4[hp:], s8[hp:], s16[hp:])
        pooled = (ws / _pool_count((ts, w), i * ts) - cur).astype(BF16)
        pooled_ref[...] = pooled
        mixed = jnp.dot(pooled, w_ref[...], preferred_element_type=F32)
        tok_ref[...] = (mixed * sc_ref[...]).astype(BF16)

    row = pl.BlockSpec((ts, w), lambda r: (r, 0))
    return pl.pallas_call(
        body, name="pool_fwd", grid=(nb * nq,),
        in_specs=[row, pl.BlockSpec((HALO_POOL, w), lambda r: (jnp.maximum(r * (ts // HALO_POOL) - 1, 0), 0)),
                  pl.BlockSpec((w, w), lambda r: (0, 0)), pl.BlockSpec((1, w), lambda r: (0, 0))],
        out_specs=[row, row], out_shape=[SDS((nb * s, w), BF16), SDS((nb * s, w), BF16)],
        compiler_params=_params(("parallel",)))(proj, proj, pw_bd, pscale)


def pool_bwd_mix(dcat, pooled, pw_bd, pw_bd_t, pscale, nb, s):
    ts = min(TS, s)
    w = TOK_WIDTH

    def body(dt_ref, p_ref, w_ref, wt_ref, sc_ref, dm_ref, dp_ref, ds_ref):
        r = pl.program_id(0)
        dtok = dt_ref[...]
        mixed = jnp.dot(p_ref[...], w_ref[...], preferred_element_type=F32)

        @pl.when(r == 0)
        def _():
            ds_ref[...] = jnp.zeros_like(ds_ref)

        ds_ref[...] += jnp.sum(dtok * mixed, axis=0, keepdims=True)
        dmx = (dtok * sc_ref[...]).astype(BF16)
        dm_ref[...] = dmx
        dp_ref[...] = jnp.dot(dmx, wt_ref[...], preferred_element_type=F32)

    row = pl.BlockSpec((ts, w), lambda r: (r, 0))
    mat = pl.BlockSpec((w, w), lambda r: (0, 0))
    vec = pl.BlockSpec((1, w), lambda r: (0, 0))
    return pl.pallas_call(
        body, name="pool_bwd_mix", grid=(nb * s // ts,), in_specs=[row, row, mat, mat, vec],
        out_specs=[row, row, vec], out_shape=[SDS((nb * s, w), BF16), SDS((nb * s, w), F32), SDS((1, w), F32)],
        compiler_params=_params(("arbitrary",)))(dcat, pooled, pw_bd, pw_bd_t, pscale)


def pool_bwd_window(dpooled, nb, s):
    ts = min(TS, s)
    nq = s // ts
    w = TOK_WIDTH
    n_ext = ts + HALO_POOL
    n_halo_blocks = nb * s // HALO_POOL

    def body(c_ref, n_ref, du_ref):
        i = pl.program_id(0) % nq
        cur = c_ref[...]
        nxt = jnp.where(i == nq - 1, 0.0, n_ref[...])
        ze = jnp.concatenate([cur, nxt], axis=0) / _pool_count((n_ext, w), i * ts)
        s2 = ze + pltpu.roll(ze, n_ext - 1, axis=0)
        s4 = s2 + pltpu.roll(s2, n_ext - 2, axis=0)
        s8 = s4 + pltpu.roll(s4, n_ext - 4, axis=0)
        s16 = s8 + pltpu.roll(s8, n_ext - 8, axis=0)
        ws = _pool_select((ts, w), s2[:ts], s4[:ts], s8[:ts], s16[:ts])
        du_ref[...] = (ws - cur).astype(BF16)

    row = pl.BlockSpec((ts, w), lambda r: (r, 0))
    return pl.pallas_call(
        body, name="pool_bwd_window", grid=(nb * nq,),
        in_specs=[row, pl.BlockSpec((HALO_POOL, w),
                                    lambda r: (jnp.minimum((r + 1) * (ts // HALO_POOL), n_halo_blocks - 1), 0))],
        out_specs=row, out_shape=SDS((nb * s, w), BF16),
        compiler_params=_params(("parallel",)))(dpooled, dpooled)


def _conv_rows(xe, w_ref):
    return (w_ref[0, 2:3, :] * xe + w_ref[0, 1:2, :] * pltpu.roll(xe, 1, axis=0)
            + w_ref[0, 0:1, :] * pltpu.roll(xe, 2, axis=0) + w_ref[0, 3:4, :])


def convgate_fwd(h, cw, nb, s, name):
    ts = min(TS, s)
    nq = s // ts
    w = FF_BLOCK_PAD
    hc = HALO_CONV

    def body(uc_ref, uh_ref, gc_ref, gh_ref, wu_ref, wg_ref, o_ref):
        first = (pl.program_id(0) % nq) == 0
        xu = jnp.concatenate([jnp.where(first, 0.0, uh_ref[...]), uc_ref[...]], axis=0)
        xg = jnp.concatenate([jnp.where(first, 0.0, gh_ref[...]), gc_ref[...]], axis=0)
        cu = _conv_rows(xu, wu_ref)[hc:]
        cg = _conv_rows(xg, wg_ref)[hc:]
        o_ref[...] = (cg * _sigmoid(cg) * cu).astype(BF16)

    def cur(off):
        return pl.BlockSpec((ts, w), lambda r, j: (r, j + off))

    def halo(off):
        return pl.BlockSpec((hc, w), lambda r, j: (jnp.maximum(r * (ts // hc) - 1, 0), j + off))

    def wspec(off):
        return pl.BlockSpec((1, 8, w), lambda r, j: (j + off, 0, 0))

    return pl.pallas_call(
        body, name=name, grid=(nb * nq, FF_PAIRS),
        in_specs=[cur(0), halo(0), cur(FF_PAIRS), halo(FF_PAIRS), wspec(0), wspec(FF_PAIRS)],
        out_specs=pl.BlockSpec((ts, w), lambda r, j: (r, j)), out_shape=SDS((nb * s, FF_PAIRS * w), BF16),
        compiler_params=_params(("parallel", "parallel")))(h, h, h, h, cw, cw)


def convgate_bwd(h, dact, cw, nb, s, name):
    ts = min(TS, s)
    nq = s // ts
    w = FF_BLOCK_PAD
    hc = HALO_CONV
    n_ext = ts + hc
    n_halo_blocks = nb * s // hc

    def body(uc_ref, up_ref, un_ref, gc_ref, gp_ref, gn_ref, dc_ref, dn_ref, wu_ref, wg_ref,
             dhu_ref, dhg_ref, dwu_ref, dwg_ref):
        r = pl.program_id(1)
        i = r % nq
        first = i == 0
        last = i == nq - 1
        xu = jnp.concatenate([jnp.where(first, 0.0, up_ref[...]), uc_ref[...], un_ref[...]], axis=0)
        xg = jnp.concatenate([jnp.where(first, 0.0, gp_ref[...]), gc_ref[...], gn_ref[...]], axis=0)
        cu = _conv_rows(xu, wu_ref)[hc:]
        cg = _conv_rows(xg, wg_ref)[hc:]
        da = jnp.concatenate([dc_ref[...].astype(F32), jnp.where(last, 0.0, dn_ref[...].astype(F32)[:hc])], axis=0)
        sg = _sigmoid(cg)
        dcu = da * (cg * sg)
        dcg = da * cu * (sg * (1.0 + cg * (1.0 - sg)))

        def conv_t(dcv, w_ref):
            return (w_ref[0, 2:3, :] * dcv + w_ref[0, 1:2, :] * pltpu.roll(dcv, n_ext - 1, axis=0)
                    + w_ref[0, 0:1, :] * pltpu.roll(dcv, n_ext - 2, axis=0))[:ts]

        dhu_ref[...] = conv_t(dcu, wu_ref).astype(BF16)
        dhg_ref[...] = conv_t(dcg, wg_ref).astype(BF16)

        def tap_grads(xe, dcv):
            d0 = dcv[:ts]
            x0 = xe[hc:hc + ts]
            x1 = pltpu.roll(xe, 1, axis=0)[hc:hc + ts]
            x2 = pltpu.roll(xe, 2, axis=0)[hc:hc + ts]
            rows = [jnp.sum(d0 * x2, axis=0, keepdims=True), jnp.sum(d0 * x1, axis=0, keepdims=True),
                    jnp.sum(d0 * x0, axis=0, keepdims=True), jnp.sum(d0, axis=0, keepdims=True)]
            sub = lax.broadcasted_iota(jnp.int32, (8, w), 0)
            upd = jnp.zeros((8, w), F32)
            for k, rv in enumerate(rows):
                upd = jnp.where(sub == k, rv, upd)
            return upd[None]

        @pl.when(r == 0)
        def _():
            dwu_ref[...] = jnp.zeros_like(dwu_ref)
            dwg_ref[...] = jnp.zeros_like(dwg_ref)

        dwu_ref[...] += tap_grads(xu, dcu)
        dwg_ref[...] += tap_grads(xg, dcg)

    def cur(off):
        return pl.BlockSpec((ts, w), lambda j, r: (r, j + off))

    def prev(off):
        return pl.BlockSpec((hc, w), lambda j, r: (jnp.maximum(r * (ts // hc) - 1, 0), j + off))

    def nxt(off):
        return pl.BlockSpec((hc, w), lambda j, r: (jnp.minimum((r + 1) * (ts // hc), n_halo_blocks - 1), j + off))

    def wspec(off):
        return pl.BlockSpec((1, 8, w), lambda j, r: (j + off, 0, 0))

    hb = 2 * hc
    dact_next = pl.BlockSpec((hb, w), lambda j, r: (jnp.minimum((r + 1) * (ts // hb), nb * s // hb - 1), j))

    p = FF_PAIRS
    dh_spec = pl.BlockSpec((ts, w), lambda j, r: (r, j))
    dw_spec = pl.BlockSpec((1, 8, w), lambda j, r: (j, 0, 0))
    return pl.pallas_call(
        body, name=name, grid=(p, nb * nq),
        in_specs=[cur(0), prev(0), nxt(0), cur(p), prev(p), nxt(p), cur(0), dact_next, wspec(0), wspec(p)],
        out_specs=[dh_spec, dh_spec, dw_spec, dw_spec],
        out_shape=[SDS((nb * s, p * w), BF16), SDS((nb * s, p * w), BF16), SDS((p, 8, w), F32), SDS((p, 8, w), F32)],
        compiler_params=_params(("parallel", "arbitrary")))(h, h, h, h, h, h, dact, dact, cw, cw)


def _tri(n, upper):
    r = lax.broadcasted_iota(jnp.int32, (n, n), 0)
    c = lax.broadcasted_iota(jnp.int32, (n, n), 1)
    return ((r <= c) if upper else (r >= c)).astype(F32)


def fgate_fwd(fl, fb, nb, s):
    tc = min(TC, s)
    nq = s // tc

    def body(fl_ref, fb_ref, f_ref, carry):
        @pl.when(pl.program_id(1) == 0)
        def _():
            carry[...] = jnp.zeros_like(carry)

        z = fl_ref[...] + fb_ref[...]
        logf = jnp.minimum(z, 0.0) - jnp.log(1.0 + jnp.exp(-jnp.abs(z)))
        f_ref[...] = jnp.dot(_tri(tc, False), logf, preferred_element_type=F32,
                             precision=lax.Precision.HIGHEST) + carry[...]
        carry[...] += jnp.sum(logf, axis=0, keepdims=True)

    row = pl.BlockSpec((tc, LANES), lambda b, i: (b * nq + i, 0))
    return pl.pallas_call(
        body, name="fgate_fwd", grid=(nb, nq), in_specs=[row, pl.BlockSpec((1, LANES), lambda b, i: (0, 0))],
        out_specs=row, out_shape=SDS((nb * s, LANES), F32), scratch_shapes=[pltpu.VMEM((1, LANES), F32)],
        compiler_params=_params(("arbitrary", "arbitrary")))(fl, fb)


def fgate_bwd(d_cum_q, d_cum_k, fl, fb, nb, s):
    tc = min(TC, s)
    nq = s // tc

    def body(dfq_ref, dfk_ref, fl_ref, fb_ref, dfl_ref, dfb_ref, carry):
        b = pl.program_id(0)
        i = pl.program_id(1)

        @pl.when(i == 0)
        def _():
            carry[...] = jnp.zeros_like(carry)

        @pl.when(jnp.logical_and(b == 0, i == 0))
        def _():
            dfb_ref[...] = jnp.zeros_like(dfb_ref)

        dfv = dfq_ref[...] + dfk_ref[...]
        dlog = jnp.dot(_tri(tc, True), dfv, preferred_element_type=F32,
                       precision=lax.Precision.HIGHEST) + carry[...]
        carry[...] += jnp.sum(dfv, axis=0, keepdims=True)
        z = fl_ref[...] + fb_ref[...]
        dfl = dlog / (1.0 + jnp.exp(z))
        dfl_ref[...] = dfl
        dfb_ref[...] += jnp.sum(dfl, axis=0, keepdims=True)

    row = pl.BlockSpec((tc, LANES), lambda b, i: (b * nq + nq - 1 - i, 0))
    vec = pl.BlockSpec((1, LANES), lambda b, i: (0, 0))
    return pl.pallas_call(
        body, name="fgate_bwd", grid=(nb, nq), in_specs=[row, row, row, vec], out_specs=[row, vec],
        out_shape=[SDS((nb * s, LANES), F32), SDS((1, LANES), F32)], scratch_shapes=[pltpu.VMEM((1, LANES), F32)],
        compiler_params=_params(("arbitrary", "arbitrary")))(d_cum_q, d_cum_k, fl, fb)


PAIR = 2 * HEAD_DIM
N_PAIRS = FOX_HEADS // 2


def _lane_put(shape, h, col):
    lane = lax.broadcasted_iota(jnp.int32, shape, 1)
    return jnp.where(lane == h, col, 0.0)


def _half_masks(rows):
    lane = lax.broadcasted_iota(jnp.int32, (rows, PAIR), 1)
    return lane < HEAD_DIM


def _split_pair(x, scale=None):
    if scale is not None:
        x = x * scale
    lo = _half_masks(x.shape[0])
    zero = jnp.zeros_like(x)
    return jnp.where(lo, x, zero), jnp.where(lo, zero, x)


def _to_tile_rows(a, nb, s, tf):
    return a.reshape(nb * s // tf, tf, LANES)[:, :, :16].transpose(0, 2, 1)


def _from_tile_rows(a):
    tiles, _, tf = a.shape
    return jnp.pad(a.transpose(0, 2, 1), ((0, 0), (0, 0), (0, LANES - 16))).reshape(tiles * tf, LANES)


BIAS_TERMS = 3
LOOKAHEAD = 4
LOOKAHEAD_BWD = 4
LOOKAHEAD_DKV = 2


def _bias_lane(h):
    return HEAD_DIM if h % 2 == 0 else 0


def _placement():
    rows = jnp.arange(LANES)[:, None]
    cols = jnp.arange(FOX_HEADS * PAIR)[None, :]
    head, lane = cols // PAIR, cols % PAIR
    first = jnp.where(head % 2 == 0, HEAD_DIM, 0)
    term = lane - first
    hit = (term >= 0) & (term < BIAS_TERMS) & (rows == 16 * term + head)
    return hit.astype(BF16)


def fox_prep(kv, fneg, nb, s):
    tf = min(TF, s)
    w = TOK_WIDTH

    def body(k_ref, v_ref, f_ref, pl_ref, ka_ref, vt_ref):
        lane = lax.broadcasted_iota(jnp.int32, (tf, LANES), 1)
        lo = lane < HEAD_DIM
        f = jnp.where(lane < FOX_HEADS, f_ref[...], 0.0)
        hi = f.astype(BF16).astype(F32)
        mid = (f - hi).astype(BF16).astype(F32)
        low = (f - hi - mid).astype(BF16).astype(F32)
        terms = (hi + pltpu.roll(mid, 16, axis=1) + pltpu.roll(low, 32, axis=1)).astype(BF16)
        placed = jnp.dot(terms, pl_ref[...], preferred_element_type=F32).astype(BF16)
        one = jnp.ones((tf, LANES), BF16)
        zero = jnp.zeros((tf, LANES), BF16)
        for p in range(N_PAIRS):
            kp = k_ref[:, p * PAIR:(p + 1) * PAIR] * QK_SCALE
            vp = v_ref[:, p * PAIR:(p + 1) * PAIR]
            he, ho = 2 * p, 2 * p + 1
            ka_ref[:, he * PAIR:(he + 1) * PAIR] = jnp.where(lo, kp, placed[:, he * PAIR:(he + 1) * PAIR])
            ka_ref[:, ho * PAIR:(ho + 1) * PAIR] = jnp.where(lo, placed[:, ho * PAIR:(ho + 1) * PAIR], kp)
            ve = jnp.where(lo, vp, jnp.where(lane == HEAD_DIM, one, zero))
            vo = jnp.where(lo, jnp.where(lane == 0, one, zero), vp)
            vt_ref[0, he * PAIR:(he + 1) * PAIR, :] = ve.astype(F32).T.astype(BF16)
            vt_ref[0, ho * PAIR:(ho + 1) * PAIR, :] = vo.astype(F32).T.astype(BF16)

    return pl.pallas_call(
        body, name="fox_prep", grid=(nb * s // tf,),
        in_specs=[pl.BlockSpec((tf, w), lambda r: (r, 0)), pl.BlockSpec((tf, w), lambda r: (r, 1)),
                  pl.BlockSpec((tf, LANES), lambda r: (r, 0)), pl.BlockSpec((LANES, FOX_HEADS * PAIR), lambda r: (0, 0))],
        out_specs=[pl.BlockSpec((tf, FOX_HEADS * PAIR), lambda r: (r, 0)),
                   pl.BlockSpec((1, FOX_HEADS * PAIR, tf), lambda r: (r, 0, 0))],
        out_shape=[SDS((nb * s, FOX_HEADS * PAIR), BF16), SDS((nb * s // tf, FOX_HEADS * PAIR, tf), BF16)],
        compiler_params=_params(("parallel",)))(kv, kv, fneg, _placement())


def fox_fwd_t(pq, kaug, vaug_t, nb, s):
    tf = min(TF, s)
    n = s // tf
    w = TOK_WIDTH
    wa = FOX_HEADS * PAIR

    def body(q_ref, k_hbm, vt_hbm, ob_ref, of_ref, lse_ref, k_vm, vt_vm, qx_scr, m_scr, acc_scr, sems):
        b = pl.program_id(0)
        i = pl.program_id(1)

        @pl.when(i == 0)
        def _():
            ck = pltpu.make_async_copy(k_hbm.at[pl.ds(pl.multiple_of(b * s, tf), s)], k_vm, sems.at[0])
            cv = pltpu.make_async_copy(vt_hbm.at[pl.ds(b * n, n)], vt_vm, sems.at[1])
            ck.start()
            cv.start()
            ck.wait()
            cv.wait()

        lane = lax.broadcasted_iota(jnp.int32, (tf, PAIR), 1)
        one = jnp.ones((tf, PAIR), BF16)
        zero = jnp.zeros((tf, PAIR), BF16)
        for p in range(N_PAIRS):
            qp = q_ref[:, p * PAIR:(p + 1) * PAIR]
            be, bo = _bias_lane(2 * p), _bias_lane(2 * p + 1)
            ones_e = jnp.where((lane >= be) & (lane < be + BIAS_TERMS), one, zero)
            ones_o = jnp.where((lane >= bo) & (lane < bo + BIAS_TERMS), one, zero)
            qx_scr[2 * p] = jnp.where(lane < HEAD_DIM, qp, ones_e)
            qx_scr[2 * p + 1] = jnp.where(lane < HEAD_DIM, ones_o, qp)
        m_scr[...] = jnp.full(m_scr.shape, NEG_BIG, F32)
        acc_scr[...] = jnp.zeros_like(acc_scr)

        def tile(j, masked):
            ks = pl.multiple_of(j * tf, tf)
            if masked:
                keep = lax.broadcasted_iota(jnp.int32, (tf, tf), 1) >= lax.broadcasted_iota(jnp.int32, (tf, tf), 0)
            def scores(h):
                kx = k_vm[pl.ds(ks, tf), h * PAIR:(h + 1) * PAIR]
                return lax.dot_general(kx, qx_scr[h], NT_DIMS, preferred_element_type=F32)

            ahead = [scores(h) for h in range(LOOKAHEAD)]
            for h in range(FOX_HEADS):
                sc = ahead.pop(0)
                if h + LOOKAHEAD < FOX_HEADS:
                    ahead.append(scores(h + LOOKAHEAD))
                if masked:
                    sc = jnp.where(keep, sc, NEG_BIG)
                m_prev = m_scr[h]
                m_new = jnp.maximum(m_prev, jnp.max(sc, axis=0, keepdims=True))
                pr = jnp.exp(sc - m_new).astype(BF16)
                pv = jnp.dot(vt_vm[j, h * PAIR:(h + 1) * PAIR, :], pr, preferred_element_type=F32)
                acc_scr[h] = jnp.exp(m_prev - m_new) * acc_scr[h] + pv
                m_scr[h] = m_new

        def step(j, carry):
            tile(j, False)
            return carry

        lax.fori_loop(0, i, step, 0)
        tile(i, True)

        top = lax.broadcasted_iota(jnp.int32, (PAIR, tf), 0) < HEAD_DIM
        sub = lax.broadcasted_iota(jnp.int32, (16, tf), 0)
        lse = jnp.zeros((16, tf), F32)
        for p in range(N_PAIRS):
            he, ho = 2 * p, 2 * p + 1
            le = acc_scr[he, HEAD_DIM:HEAD_DIM + 1, :]
            lod = acc_scr[ho, 0:1, :]
            o = jnp.where(top, acc_scr[he] / le, acc_scr[ho] / lod).T
            ob_ref[:, p * PAIR:(p + 1) * PAIR] = o.astype(BF16)
            of_ref[:, p * PAIR:(p + 1) * PAIR] = o
            lse = jnp.where(sub == he, m_scr[he] + jnp.log(le), lse)
            lse = jnp.where(sub == ho, m_scr[ho] + jnp.log(lod), lse)
        lse_ref[0] = lse

    qrow = lambda b, i: (b * n + i, 0)
    return pl.pallas_call(
        body, name="fox_fwd", grid=(nb, n),
        in_specs=[pl.BlockSpec((tf, w), qrow), ANY_SPEC, ANY_SPEC],
        out_specs=[pl.BlockSpec((tf, w), qrow), pl.BlockSpec((tf, w), qrow),
                   pl.BlockSpec((1, 16, tf), lambda b, i: (b * n + i, 0, 0))],
        out_shape=[SDS((nb * s, w), BF16), SDS((nb * s, w), F32), SDS((nb * n, 16, tf), F32)],
        scratch_shapes=[pltpu.VMEM((s, wa), BF16), pltpu.VMEM((n, wa, tf), BF16),
                        pltpu.VMEM((FOX_HEADS, tf, PAIR), BF16), pltpu.VMEM((FOX_HEADS, 1, tf), F32),
                        pltpu.VMEM((FOX_HEADS, PAIR, tf), F32), pltpu.SemaphoreType.DMA((2,))],
        compiler_params=_params(("arbitrary", "arbitrary")))(pq, kaug, vaug_t)


def fox_delta(dcat, o, nb, s):
    tf = min(TM, s)
    w = TOK_WIDTH

    def body(do_ref, o_ref, dl_ref):
        out = jnp.zeros((tf, LANES), F32)
        for h in range(FOX_HEADS):
            lo, hi = h * HEAD_DIM, (h + 1) * HEAD_DIM
            out = out + _lane_put((tf, LANES), h, jnp.sum(do_ref[:, lo:hi] * o_ref[:, lo:hi], axis=1, keepdims=True))
        dl_ref[...] = out

    row = pl.BlockSpec((tf, w), lambda r: (r, 0))
    return pl.pallas_call(
        body, name="fox_delta", grid=(nb * s // tf,), in_specs=[row, row],
        out_specs=pl.BlockSpec((tf, LANES), lambda r: (r, 0)), out_shape=SDS((nb * s, LANES), F32),
        compiler_params=_params(("parallel",)))(dcat, o)


def fox_bwd_dq(pq, kv, fneg_rows, dcat_bf, lse, delta, nb, s):
    tf = min(TF, s)
    n = s // tf
    w = TOK_WIDTH

    def body(q_ref, k_ref, v_ref, ft_ref, do_ref, lse_ref, dl_ref, dq_ref, df_ref, qm_scr, dom_scr, acc_scr, rs_scr):
        i = pl.program_id(1)
        for p in range(N_PAIRS):
            qe, qo = _split_pair(q_ref[:, p * PAIR:(p + 1) * PAIR], QK_SCALE)
            qm_scr[2 * p] = qe
            qm_scr[2 * p + 1] = qo
            de, dod = _split_pair(do_ref[:, p * PAIR:(p + 1) * PAIR])
            dom_scr[2 * p] = de
            dom_scr[2 * p + 1] = dod
        acc_scr[...] = jnp.zeros_like(acc_scr)
        rs_scr[...] = jnp.zeros_like(rs_scr)

        def tile(j, masked):
            ks = pl.multiple_of(j * tf, tf)
            if masked:
                keep = lax.broadcasted_iota(jnp.int32, (tf, tf), 0) >= lax.broadcasted_iota(jnp.int32, (tf, tf), 1)
            def products(h):
                p = h // 2
                kp = k_ref[pl.ds(ks, tf), p * PAIR:(p + 1) * PAIR]
                vp = v_ref[pl.ds(ks, tf), p * PAIR:(p + 1) * PAIR]
                return (lax.dot_general(qm_scr[h], kp, NT_DIMS, preferred_element_type=F32),
                        lax.dot_general(dom_scr[h], vp, NT_DIMS, preferred_element_type=F32))

            ahead = [products(h) for h in range(LOOKAHEAD_BWD)]
            for h in range(FOX_HEADS):
                sc, dp = ahead.pop(0)
                if h + LOOKAHEAD_BWD < FOX_HEADS:
                    ahead.append(products(h + LOOKAHEAD_BWD))
                kp = k_ref[pl.ds(ks, tf), (h // 2) * PAIR:(h // 2 + 1) * PAIR]
                sc = sc + ft_ref[j, h:h + 1, :] - lse_ref[:, h:h + 1]
                if masked:
                    sc = jnp.where(keep, sc, NEG_BIG)
                pr = jnp.exp(sc)
                ds = pr * (dp - dl_ref[:, h:h + 1])
                part = ds[:, :LANES]
                for c in range(1, tf // LANES):
                    part = part + ds[:, c * LANES:(c + 1) * LANES]
                rs_scr[h] += part
                acc_scr[h] += jnp.dot(ds.astype(BF16), kp, preferred_element_type=F32)

        def step(j, carry):
            tile(j, False)
            return carry

        lax.fori_loop(0, i, step, 0)
        tile(i, True)

        lo = _half_masks(tf)
        dfq = jnp.zeros((tf, LANES), F32)
        for p in range(N_PAIRS):
            dq = jnp.where(lo, acc_scr[2 * p], acc_scr[2 * p + 1]) * QK_SCALE
            dq_ref[:, p * PAIR:(p + 1) * PAIR] = dq.astype(BF16)
            for h in (2 * p, 2 * p + 1):
                dfq = dfq + _lane_put((tf, LANES), h, jnp.sum(rs_scr[h], axis=1, keepdims=True))
        df_ref[...] = dfq

    qrow = lambda b, i: (b * n + i, 0)
    stat = pl.BlockSpec((tf, LANES), qrow)
    return pl.pallas_call(
        body, name="fox_bwd_dq", grid=(nb, n),
        in_specs=[pl.BlockSpec((tf, w), qrow), pl.BlockSpec((s, w), lambda b, i: (b, 0)),
                  pl.BlockSpec((s, w), lambda b, i: (b, 1)), pl.BlockSpec((n, 16, tf), lambda b, i: (b, 0, 0)),
                  pl.BlockSpec((tf, w), qrow), stat, stat],
        out_specs=[pl.BlockSpec((tf, w), qrow), stat],
        out_shape=[SDS((nb * s, w), BF16), SDS((nb * s, LANES), F32)],
        scratch_shapes=[pltpu.VMEM((FOX_HEADS, tf, PAIR), BF16), pltpu.VMEM((FOX_HEADS, tf, PAIR), BF16),
                        pltpu.VMEM((FOX_HEADS, tf, PAIR), F32), pltpu.VMEM((FOX_HEADS, tf, LANES), F32)],
        compiler_params=_params(("parallel", "arbitrary")))(pq, kv, kv, fneg_rows, dcat_bf, lse, delta)


def fox_bwd_dkv(pq, kv, fneg, dcat_bf, lse_rows, delta_rows, nb, s):
    tf = min(TF, s)
    n = s // tf
    w = TOK_WIDTH

    def body(q_ref, k_ref, v_ref, f_ref, do_ref, lse_ref, dl_ref, dk_ref, dv_ref, df_ref,
             km_scr, vm_scr, fk_scr, dk_scr, dv_scr, rs_scr):
        j = pl.program_id(1)
        for p in range(N_PAIRS):
            ke, ko = _split_pair(k_ref[:, p * PAIR:(p + 1) * PAIR], QK_SCALE)
            km_scr[2 * p] = ke
            km_scr[2 * p + 1] = ko
            ve, vo = _split_pair(v_ref[:, p * PAIR:(p + 1) * PAIR])
            vm_scr[2 * p] = ve
            vm_scr[2 * p + 1] = vo
        for h in range(FOX_HEADS):
            fk_scr[h] = jnp.broadcast_to(f_ref[:, h:h + 1], (tf, tf))
        dk_scr[...] = jnp.zeros_like(dk_scr)
        dv_scr[...] = jnp.zeros_like(dv_scr)
        rs_scr[...] = jnp.zeros_like(rs_scr)

        def tile(i, masked):
            qs = pl.multiple_of(i * tf, tf)
            if masked:
                keep = lax.broadcasted_iota(jnp.int32, (tf, tf), 1) >= lax.broadcasted_iota(jnp.int32, (tf, tf), 0)
            def scores(h):
                qp = q_ref[pl.ds(qs, tf), (h // 2) * PAIR:(h // 2 + 1) * PAIR]
                return lax.dot_general(km_scr[h], qp, NT_DIMS, preferred_element_type=F32)

            ahead = [scores(h) for h in range(LOOKAHEAD_DKV)]
            for h in range(FOX_HEADS):
                sc = ahead.pop(0)
                if h + LOOKAHEAD_DKV < FOX_HEADS:
                    ahead.append(scores(h + LOOKAHEAD_DKV))
                p = h // 2
                qp = q_ref[pl.ds(qs, tf), p * PAIR:(p + 1) * PAIR]
                dop = do_ref[pl.ds(qs, tf), p * PAIR:(p + 1) * PAIR]
                sc = sc + fk_scr[h] - lse_ref[i, h:h + 1, :]
                if masked:
                    sc = jnp.where(keep, sc, NEG_BIG)
                pr = jnp.exp(sc)
                dv_scr[h] += jnp.dot(pr.astype(BF16), dop, preferred_element_type=F32)
                dp = lax.dot_general(vm_scr[h], dop, NT_DIMS, preferred_element_type=F32)
                ds = pr * (dp - dl_ref[i, h:h + 1, :])
                part = ds[:, :LANES]
                for c in range(1, tf // LANES):
                    part = part + ds[:, c * LANES:(c + 1) * LANES]
                rs_scr[h] += part
                dk_scr[h] += jnp.dot(ds.astype(BF16), qp, preferred_element_type=F32)

        def step(i, carry):
            tile(i, False)
            return carry

        tile(j, True)
        lax.fori_loop(j + 1, n, step, 0)

        lo = _half_masks(tf)
        dfk = jnp.zeros((tf, LANES), F32)
        for p in range(N_PAIRS):
            dk = jnp.where(lo, dk_scr[2 * p], dk_scr[2 * p + 1]) * QK_SCALE
            dk_ref[:, p * PAIR:(p + 1) * PAIR] = dk.astype(BF16)
            dv_ref[:, p * PAIR:(p + 1) * PAIR] = jnp.where(lo, dv_scr[2 * p], dv_scr[2 * p + 1]).astype(BF16)
            for h in (2 * p, 2 * p + 1):
                dfk = dfk - _lane_put((tf, LANES), h, jnp.sum(rs_scr[h], axis=1, keepdims=True))
        df_ref[...] = dfk

    krow = lambda b, j: (b * n + j, 0)
    rows = pl.BlockSpec((n, 16, tf), lambda b, j: (b, 0, 0))
    return pl.pallas_call(
        body, name="fox_bwd_dkv", grid=(nb, n),
        in_specs=[pl.BlockSpec((s, w), lambda b, j: (b, 0)), pl.BlockSpec((tf, w), krow),
                  pl.BlockSpec((tf, w), lambda b, j: (b * n + j, 1)), pl.BlockSpec((tf, LANES), krow),
                  pl.BlockSpec((s, w), lambda b, j: (b, 0)), rows, rows],
        out_specs=[pl.BlockSpec((tf, w), krow), pl.BlockSpec((tf, w), krow), pl.BlockSpec((tf, LANES), krow)],
        out_shape=[SDS((nb * s, w), BF16), SDS((nb * s, w), BF16), SDS((nb * s, LANES), F32)],
        scratch_shapes=[pltpu.VMEM((FOX_HEADS, tf, PAIR), BF16), pltpu.VMEM((FOX_HEADS, tf, PAIR), BF16),
                        pltpu.VMEM((FOX_HEADS, tf, tf), F32), pltpu.VMEM((FOX_HEADS, tf, PAIR), F32),
                        pltpu.VMEM((FOX_HEADS, tf, PAIR), F32), pltpu.VMEM((FOX_HEADS, tf, LANES), F32)],
        compiler_params=_params(("parallel", "arbitrary")))(pq, kv, kv, fneg, dcat_bf, lse_rows, delta_rows)


def reduce_adamw(parts, w, m, v, name):
    _, r, c = parts.shape
    tr = r
    for cand in range(16, r, 16):
        if r % cand == 0 and cand * c <= 128 * 1024:
            tr = cand
    c1 = 1.0 - ADAM_B1 ** ADAM_STEP
    c2 = 1.0 - ADAM_B2 ** ADAM_STEP

    def body(p_ref, w_ref, m_ref, v_ref, g_out, d_out, m_out, v_out):
        g = p_ref[0].astype(F32)
        for k in range(1, N_DEV):
            g = g + p_ref[k].astype(F32)
        mn = ADAM_B1 * m_ref[...] + (1.0 - ADAM_B1) * g
        vn = ADAM_B2 * v_ref[...] + (1.0 - ADAM_B2) * (g * g)
        g_out[...] = g
        m_out[...] = mn
        v_out[...] = vn
        d_out[...] = -ADAM_LR * ((mn / c1) / (jnp.sqrt(vn / c2) + ADAM_EPS) + ADAM_WD * w_ref[...])

    row = pl.BlockSpec((tr, c), lambda i: (i, 0))
    return pl.pallas_call(
        body, name=name, grid=(r // tr,),
        in_specs=[pl.BlockSpec((N_DEV, tr, c), lambda i: (0, i, 0)), row, row, row],
        out_specs=[row, row, row, row], out_shape=[SDS((r, c), F32)] * 4,
        compiler_params=_params(("parallel",)))(parts, w, m, v)


N_PEERS = N_DEV - 1
HBM_SPEC = pl.BlockSpec(memory_space=pltpu.HBM)
SEM_SPEC = pl.BlockSpec(memory_space=pltpu.SEMAPHORE)
ANY_SPEC = pl.BlockSpec(memory_space=pl.ANY)
SPLIT_EFFECT = pltpu.SideEffectType.DATAFLOW_SIDE_EFFECTING


def _my_index():
    return 4 * lax.axis_index("x") + 2 * lax.axis_index("y") + lax.axis_index("c")


def _peers():
    x, y, c = lax.axis_index("x"), lax.axis_index("y"), lax.axis_index("c")
    peers = []
    for k in range(1, N_DEV):
        px = 1 - x if (k >> 2) & 1 else x
        py = 1 - y if (k >> 1) & 1 else y
        pc = 1 - c if k & 1 else c
        peers.append(((px, py, pc), 4 * px + 2 * py + pc))
    return 4 * x + 2 * y + c, peers


def _push(src, dst, send_sems, recv_sems, slot, dev):
    return pltpu.make_async_remote_copy(src_ref=src, dst_ref=dst, send_sem=send_sems.at[slot], recv_sem=recv_sems.at[slot],
                                        device_id=dev, device_id_type=pl.DeviceIdType.MESH)


def _landing_shapes(arrs, scatter):
    return [SDS((N_DEV,) + tuple(a.shape[1:] if sc else a.shape), a.dtype) for a, sc in zip(arrs, scatter)]


def exchange(arrs, scatter, name):
    na = len(arrs)

    def body(*refs):
        ins = refs[:na]
        outs = refs[na:2 * na]
        send_sems, recv_sems, local_sems = refs[2 * na:]
        me, peers = _peers()
        local = []
        remote = []
        for a in range(na):
            lc = pltpu.make_async_copy(ins[a].at[me] if scatter[a] else ins[a], outs[a].at[me], local_sems.at[a])
            lc.start()
            local.append(lc)
            for k, (dev, idx) in enumerate(peers):
                cp = _push(ins[a].at[idx] if scatter[a] else ins[a], outs[a].at[me], send_sems, recv_sems,
                           a * N_PEERS + k, dev)
                cp.start()
                remote.append(cp)
        for a in range(na):
            for k, (dev, idx) in enumerate(peers):
                _push(ins[a].at[me] if scatter[a] else ins[a], outs[a].at[idx], send_sems, recv_sems,
                      a * N_PEERS + k, dev).wait_recv()
        for cp in remote:
            cp.wait_send()
        for lc in local:
            lc.wait()

    return pl.pallas_call(
        body, name=name, in_specs=[HBM_SPEC] * na, out_specs=[HBM_SPEC] * na, out_shape=_landing_shapes(arrs, scatter),
        scratch_shapes=[pltpu.SemaphoreType.DMA((na * N_PEERS,)), pltpu.SemaphoreType.DMA((na * N_PEERS,)),
                        pltpu.SemaphoreType.DMA((na,))])(*arrs)


def exchange_start(arrs, scatter, after, name):
    na = len(arrs)
    lands = [lax.empty(l.shape, l.dtype) for l in _landing_shapes(arrs, scatter)]

    def body(*refs):
        ins = refs[:na]
        land = refs[na:2 * na]
        send_sems, recv_sems = refs[2 * na + 1], refs[2 * na + 2]
        token = refs[-1]
        me, peers = _peers()
        for a in range(na):
            for k, (dev, idx) in enumerate(peers):
                _push(ins[a].at[idx] if scatter[a] else ins[a], land[a].at[me], send_sems, recv_sems,
                      a * N_PEERS + k, dev).start()
        token[...] = jnp.zeros_like(token)

    thru = [pltpu.HBM(a.shape, a.dtype) for a in arrs] + [pltpu.HBM(l.shape, l.dtype) for l in lands]
    res = pl.pallas_call(
        body, name=name,
        out_shape=(pltpu.SemaphoreType.DMA((na * N_PEERS,)), pltpu.SemaphoreType.DMA((na * N_PEERS,)), *thru,
                   SDS((8, LANES), F32)),
        in_specs=[HBM_SPEC] * (2 * na) + [ANY_SPEC],
        out_specs=(SEM_SPEC, SEM_SPEC, *([HBM_SPEC] * (2 * na)), pl.BlockSpec(memory_space=pltpu.VMEM)),
        input_output_aliases={i: 2 + i for i in range(2 * na)},
        compiler_params=pltpu.CompilerParams(has_side_effects=SPLIT_EFFECT),
    )(*[pltpu.with_memory_space_constraint(a, pltpu.HBM) for a in arrs],
      *[pltpu.with_memory_space_constraint(l, pltpu.HBM) for l in lands], after)
    return {"send": res[0], "recv": res[1], "src": res[2:2 + na], "land": res[2 + na:2 + 2 * na],
            "token": res[-1][0, 0], "scatter": scatter}


def exchange_wait(handle, after, name):
    scatter = handle["scatter"]
    na = len(scatter)

    def body(*refs):
        src = refs[:na]
        land = refs[na:2 * na]
        send_sems, recv_sems = refs[2 * na], refs[2 * na + 1]
        me, peers = _peers()
        for a in range(na):
            for k, (dev, idx) in enumerate(peers):
                cp = _push(src[a].at[me] if scatter[a] else src[a], land[a].at[idx], send_sems, recv_sems,
                           a * N_PEERS + k, dev)
                cp.wait_send()
                cp.wait_recv()

    ops = list(handle["src"]) + list(handle["land"])
    res = pl.pallas_call(
        body, name=name, out_shape=tuple(pltpu.HBM(o.shape, o.dtype) for o in ops),
        in_specs=[HBM_SPEC] * (2 * na) + [SEM_SPEC, SEM_SPEC, ANY_SPEC], out_specs=tuple([HBM_SPEC] * (2 * na)),
        input_output_aliases={i: i for i in range(2 * na)},
        compiler_params=pltpu.CompilerParams(has_side_effects=SPLIT_EFFECT),
    )(*ops, handle["send"], handle["recv"], after)
    me = _my_index()
    out = []
    for a in range(na):
        own = lax.dynamic_index_in_dim(res[a], me, 0, keepdims=True) if scatter[a] else res[a][None]
        out.append(lax.dynamic_update_slice(res[na + a], own, (me,) + (0,) * (own.ndim - 1)))
    return out


def forward_layer(l, xin, xin_bf, mem_bf, wt, nb, s, ffn_weights=None):
    sv = {"xin_bf": xin_bf}
    memkv = mm_nn(mem_bf, wt["memw"], BF16, f"memkv{l}")
    sv["memkv"] = memkv
    if l == 0:
        proj = mm_nn(xin_bf, wt["win_a"], F32, "proj_a")
        pooled, tok = pool_fwd(proj, wt["pw_bd"], wt["pscale"], nb, s)
        sv["pooled"] = pooled
    else:
        kv = mm_nn(xin_bf, wt["kvw"][:, :2 * TOK_WIDTH], BF16, "kv_proj")
        fl = mm_nn(xin_bf, wt["kvw"][:, 2 * TOK_WIDTH:], F32, "gate_proj")
        fneg = -fgate_fwd(fl, wt["fb"], nb, s)
        fneg_rows = _to_tile_rows(fneg, nb, s, min(TF, s))
        proj = mm_nn(xin_bf, wt["wq"], BF16, "proj_b")
        kaug, vaug_t = fox_prep(kv, fneg, nb, s)
        tok, o_f32, lse_rows = fox_fwd_t(proj, kaug, vaug_t, nb, s)
        sv.update(kv=kv, fl=fl, fneg=fneg, fneg_rows=fneg_rows, o_f32=o_f32, lse=_from_tile_rows(lse_rows), lse_rows=lse_rows)
    sv["proj"] = proj
    mem_out = memattn_fwd(proj, memkv, nb, s, f"memattn_fwd{l}")
    cat = jnp.concatenate([tok, mem_out], axis=1)
    sv["cat"] = cat
    mix = mm_nn(cat, wt["wout"], F32, f"out_proj{l}")
    x1, x1_bf, xh1, rs1 = ln_fwd(xin, mix, wt["ln1_g"], wt["ln1_b"], f"ln1_fwd{l}")
    sv.update(x1_bf=x1_bf, xh1=xh1, rs1=rs1)
    if ffn_weights is not None:
        wt.update(ffn_weights(x1_bf))
    h = mm_nn(x1_bf, wt["wup"], F32, f"ffn_up{l}")
    act = convgate_fwd(h, wt["cw"], nb, s, f"convgate_fwd{l}")
    sv.update(h=h, act=act)
    ffn = mm_nn(act, wt["wdown"], F32, f"ffn_down{l}")
    x2, x2_bf, xh2, rs2 = ln_fwd(x1, ffn, wt["ln2_g"], wt["ln2_b"], f"ln2_fwd{l}")
    sv.update(xh2=xh2, rs2=rs2)
    return x2, x2_bf, sv


def backward_layer(l, dy, sv, mem_bf, wt, nb, s, after_ffn=None):
    g = {}
    dr2, dr2_bf, g["ln2_g"], g["ln2_b"] = ln_bwd(dy, sv["xh2"], sv["rs2"], wt["ln2_g"], f"ln2_bwd{l}")
    dact = mm_nn(dr2_bf, wt["wdown_t"], BF16, f"ffn_down_dx{l}")
    g["wdown"] = mm_tn(sv["act"], dr2_bf, f"ffn_down_dw{l}")
    dh_u, dh_g, dcw_u, dcw_g = convgate_bwd(sv["h"], dact, wt["cw"], nb, s, f"convgate_bwd{l}")
    g["cw"] = jnp.concatenate([dcw_u, dcw_g], axis=0)
    half = FF_PAIRS * FF_BLOCK_PAD
    dx1 = mm_nn(dh_u, wt["wup_t"][:half], F32, f"ffn_up_dx_u{l}", addend=dr2, add_scale=DN_ALPHA)
    dx1 = mm_nn(dh_g, wt["wup_t"][half:], F32, f"ffn_up_dx_g{l}", addend=dx1)
    g["wup"] = jnp.concatenate([mm_tn(sv["x1_bf"], dh_u, f"ffn_up_dw_u{l}", blocked=True),
                                mm_tn(sv["x1_bf"], dh_g, f"ffn_up_dw_g{l}", blocked=True)], axis=0)
    ln1_g = wt["ln1_g"] if after_ffn is None else wt["ln1_g"] + after_ffn(g, dx1)
    dr1, dr1_bf, g["ln1_g"], g["ln1_b"] = ln_bwd(dx1, sv["xh1"], sv["rs1"], ln1_g, f"ln1_bwd{l}")
    dcat, dcat_bf = mm_nn(dr1_bf, wt["wout_t"], F32, f"out_proj_dx{l}", also_bf16=True)
    g["wout"] = mm_tn(sv["cat"], dr1_bf, f"out_proj_dw{l}")
    dqm, dmemkv = memattn_bwd(sv["proj"], sv["memkv"], dcat, nb, s, f"memattn_bwd{l}")
    g["memw"] = mm_tn(mem_bf, dmemkv, f"memkv_dw{l}")
    if l == 0:
        dmixed, dpooled, g["pscale"] = pool_bwd_mix(dcat, sv["pooled"], wt["pw_bd"], wt["pw_bd_t"], wt["pscale"], nb, s)
        g["pw_full"] = mm_tn(sv["pooled"], dmixed, "pool_dw")
        du = pool_bwd_window(dpooled, nb, s)
        dproj = jnp.concatenate([du, dqm], axis=1)
        dx = mm_nn(dproj, wt["win_a_t"], F32, "proj_a_dx", addend=dr1, add_scale=DN_ALPHA)
        g["win_a"] = mm_tn(sv["xin_bf"], dproj, "proj_a_dw")
    else:
        delta = fox_delta(dcat, sv["o_f32"], nb, s)
        tf = min(TF, s)
        dq, dfcum_q = fox_bwd_dq(sv["proj"], sv["kv"], sv["fneg_rows"], dcat_bf, sv["lse"], delta, nb, s)
        dk, dv, dfcum_k = fox_bwd_dkv(sv["proj"], sv["kv"], sv["fneg"], dcat_bf,
                                      sv["lse_rows"], _to_tile_rows(delta, nb, s, tf), nb, s)
        dfl, g["fb"] = fgate_bwd(dfcum_q, dfcum_k, sv["fl"], wt["fb"], nb, s)
        dproj = jnp.concatenate([dq, dqm], axis=1)
        dkvf = jnp.concatenate([dk, dv, dfl.astype(BF16)], axis=1)
        dx = mm_nn(dproj, wt["wq_t"], F32, "proj_b_dx", addend=dr1, add_scale=DN_ALPHA)
        dx = mm_nn(dkvf, wt["kvw_t"], F32, "kv_proj_dx", addend=dx)
        g["wq"] = mm_tn(sv["xin_bf"], dproj, "proj_b_dw")
        g["kvw"] = mm_tn(sv["xin_bf"], dkvf, "kv_proj_dw")
    return dx, g


def pack_replicated(pool_w, ln1_g, ln1_b, ln2_g, ln2_b, conv_b, f_b):
    cb = jnp.pad(conv_b, ((0, 0), (0, 6144 - 5504))).reshape(12, D_MODEL)
    fb = jnp.pad(f_b.reshape(1, FOX_HEADS), ((0, 3), (0, D_MODEL - FOX_HEADS)))
    return jnp.concatenate([pool_w.reshape(144, D_MODEL), ln1_g, ln1_b, ln2_g, ln2_b, cb, fb], axis=0)


def unpack_replicated(buf):
    pool_w = buf[:144].reshape(1, 4, POOL_GROUP, POOL_GROUP)
    ln = [buf[144 + 2 * k:146 + 2 * k] for k in range(4)]
    conv_b = buf[152:164].reshape(2, 6144)[:, :5504]
    f_b = buf[164, :FOX_HEADS]
    return pool_w, ln[0], ln[1], ln[2], ln[3], conv_b, f_b


def pack_small(conv_w, pool_scale):
    buf = jnp.zeros((16, FF_BLOCK_PAD), F32)
    buf = lax.dynamic_update_slice(buf, conv_w.reshape(DEPTH * 3, FF_BLOCK), (0, 0))
    return lax.dynamic_update_slice(buf, pool_scale, (8, 0))


def _block_diag(pw):
    out = jnp.zeros((TOK_WIDTH, TOK_WIDTH), pw.dtype)
    for g in range(4):
        out = lax.dynamic_update_slice(out, pw[g], (g * POOL_GROUP, g * POOL_GROUP))
    return out


def layer_shards(l, sq_a, sq_b, mem_w_kv, ffn_w_up, ffn_w_down):
    return [sq_a[0].astype(BF16), sq_b[0].astype(BF16), mem_w_kv[l].astype(BF16), ffn_w_up[l].astype(BF16),
            ffn_w_down[l].astype(BF16)]


def mixer_weights(l, gath, ln1_g, ln1_b, ln2_g, ln2_b):
    w_out = gath[1].reshape(D_MODEL, D_MODEL)
    wt = {"memw": gath[2].reshape(D_MODEL, 2 * MEM_WIDTH), "wout": w_out, "wout_t": w_out.T,
          "ln1_g": ln1_g[l:l + 1], "ln1_b": ln1_b[l:l + 1], "ln2_g": ln2_g[l:l + 1], "ln2_b": ln2_b[l:l + 1]}
    return wt, gath[0].reshape(D_MODEL, D_MODEL)


def ffn_weights(l, wup_g, wdown_g, small, conv_b):
    pad_c = FF_BLOCK_PAD - FF_BLOCK
    wup = jnp.pad(wup_g, ((0, 0), (0, 0), (0, pad_c))).transpose(1, 0, 2).reshape(D_MODEL, N_DEV * FF_BLOCK_PAD)
    wdown = jnp.pad(wdown_g.reshape(FF_PAIRS, FF_BLOCK, D_MODEL), ((0, 0), (0, pad_c), (0, 0)))
    wdown = wdown.reshape(FF_PAIRS * FF_BLOCK_PAD, D_MODEL)
    cb = jnp.pad(conv_b[l].reshape(N_DEV, FF_BLOCK), ((0, 0), (0, pad_c)))
    cw = jnp.concatenate([small[:, 3 * l:3 * l + 3, :], cb[:, None, :], jnp.zeros((N_DEV, 4, FF_BLOCK_PAD), F32)], axis=1)
    return {"wup": wup, "wup_t": wup.T, "wdown": wdown, "wdown_t": wdown.T, "cw": cw}


def mixer_grad_blocks(g, w_in_grad):
    blocks = [w_in_grad.reshape(N_DEV, 128, D_MODEL), g["wout"].reshape(N_DEV, 128, D_MODEL),
              g["memw"].reshape(N_DEV, 128, 2 * MEM_WIDTH)]
    return [b.astype(BF16) for b in blocks]


def ffn_grad_blocks(g):
    wup = g["wup"][:, :, :FF_BLOCK]
    wdown = g["wdown"].reshape(FF_PAIRS, FF_BLOCK_PAD, D_MODEL)[:, :FF_BLOCK].reshape(N_DEV, FF_ROWS, D_MODEL)
    return [wup.astype(BF16), wdown.astype(BF16)]


def small_grad_blocks(g0, g1):
    taps = jnp.stack([g0["cw"][:, :3, :], g1["cw"][:, :3, :]], axis=1).reshape(N_DEV, DEPTH * 3, FF_BLOCK_PAD)
    small = jnp.zeros((N_DEV, 16, FF_BLOCK_PAD), F32)
    small = lax.dynamic_update_slice(small, taps, (0, 0, 0))
    return lax.dynamic_update_slice(small, g0["pscale"].reshape(N_DEV, 1, 96), (0, 8, 0))


def replicated_grads(g0, g1):
    pw = jnp.stack([g0["pw_full"][k * POOL_GROUP:(k + 1) * POOL_GROUP, k * POOL_GROUP:(k + 1) * POOL_GROUP] for k in range(4)])
    conv_b = jnp.stack([g_["cw"][:, 3, :FF_BLOCK].reshape(N_DEV * FF_BLOCK) for g_ in (g0, g1)])
    ln = [jnp.concatenate([g0[n], g1[n]], axis=0) for n in ("ln1_g", "ln1_b", "ln2_g", "ln2_b")]
    return pack_replicated(pw[None], ln[0], ln[1], ln[2], ln[3], conv_b, g1["fb"][0, :FOX_HEADS])


def kernel(x, mem, a_w_in, a_pool_w, a_pool_scale, a_w_out, b_w_q, b_w_out, kv_w, f_b, mem_w_kv, ln1_g, ln1_b, ln2_g, ln2_b, ffn_w_up, ffn_conv_w, ffn_conv_b, ffn_w_down, loss_target, m_a_w_in, m_a_pool_w, m_a_pool_scale, m_a_w_out, m_b_w_q, m_b_w_out, m_kv_w, m_f_b, m_mem_w_kv, m_ln1_g, m_ln1_b, m_ln2_g, m_ln2_b, m_ffn_w_up, m_ffn_conv_w, m_ffn_conv_b, m_ffn_w_down, v_a_w_in, v_a_pool_w, v_a_pool_scale, v_a_w_out, v_b_w_q, v_b_w_out, v_kv_w, v_f_b, v_mem_w_kv, v_ln1_g, v_ln1_b, v_ln2_g, v_ln2_b, v_ffn_w_up, v_ffn_conv_w, v_ffn_conv_b, v_ffn_w_down):
    nb, s, d = x.shape
    t = nb * s
    x2d, mem_bf, target = x.reshape(t, d), mem.reshape(nb * MEM_LEN, d).astype(BF16), loss_target.reshape(t, d)

    shards0 = layer_shards(0, a_w_in, a_w_out, mem_w_kv, ffn_w_up, ffn_w_down)
    shards1 = layer_shards(1, b_w_q, b_w_out, mem_w_kv, ffn_w_up, ffn_w_down)
    shards1.append(jnp.pad(kv_w, ((0, 0), (0, KV_COLS_PAD - KV_COLS))).astype(BF16))
    gath0 = exchange(shards0[:3] + [pack_small(ffn_conv_w, a_pool_scale)], [False] * 4, "gather_w0_mixer")
    pending = {"ffn0": exchange_start(shards0[3:], [False] * 2, gath0[0], "gather_w0_ffn_start")}
    small = gath0[3]
    wt0, w_in = mixer_weights(0, gath0, ln1_g + pending["ffn0"]["token"], ln1_b, ln2_g, ln2_b)
    pw_bd = _block_diag(a_pool_w[0])
    wt0.update(win_a=w_in, win_a_t=w_in.T, pw_bd=pw_bd.astype(BF16), pw_bd_t=pw_bd.T.astype(BF16),
               pscale=small[:, 8, :96].reshape(1, TOK_WIDTH) + pending["ffn0"]["token"])

    def ffn0_weights(x1_bf):
        got = exchange_wait(pending["ffn0"], x1_bf, "gather_w0_ffn_wait")
        pending["w1"] = exchange_start(shards1, [False] * 6, got[0], "gather_w1_start")
        w = ffn_weights(0, got[0], got[1], small, ffn_conv_b)
        w["cw"] = w["cw"] + pending["w1"]["token"]
        return w

    x1, x1_bf, sv0 = forward_layer(0, x2d, x2d.astype(BF16), mem_bf, wt0, nb, s, ffn_weights=ffn0_weights)
    gath1 = exchange_wait(pending["w1"], x1_bf, "gather_w1_wait")
    wt1, w_q = mixer_weights(1, gath1, ln1_g, ln1_b, ln2_g, ln2_b)
    wt1.update(ffn_weights(1, gath1[3], gath1[4], small, ffn_conv_b))
    kvw = gath1[5].reshape(D_MODEL, KV_COLS_PAD)
    wt1.update(wq=w_q, wq_t=w_q.T, kvw=kvw, kvw_t=kvw.T,
               fb=jnp.pad(f_b.reshape(1, FOX_HEADS), ((0, 0), (0, LANES - FOX_HEADS))))
    y, _, sv1 = forward_layer(1, x1, x1_bf, mem_bf, wt1, nb, s)
    dy, loss_row = loss_head(y, target)
    loss = lax.psum(loss_row[0, 0], ("x", "y", "c"))

    dx1, g1 = backward_layer(1, dy, sv1, mem_bf, wt1, nb, s)
    blocks1 = (mixer_grad_blocks(g1, g1["wq"]) + ffn_grad_blocks(g1)
               + [g1["kvw"][:, :KV_COLS].reshape(N_DEV, 128, KV_COLS).astype(BF16)])
    pending["g1"] = exchange_start(blocks1, [True] * 6, dx1, "scatter_g1_start")
    wt0["ln2_g"] = wt0["ln2_g"] + pending["g1"]["token"]

    def after_ffn0(g, dxm):
        pending["gf0"] = exchange_start(ffn_grad_blocks(g), [True] * 2, dxm, "scatter_g0_ffn_start")
        return pending["gf0"]["token"]

    grad_x, g0 = backward_layer(0, dx1, sv0, mem_bf, wt0, nb, s, after_ffn=after_ffn0)
    last = mixer_grad_blocks(g0, g0["win_a"]) + [small_grad_blocks(g0, g1), replicated_grads(g0, g1)]
    parts_m0 = exchange(last, [True] * 4 + [False], "scatter_g0_mixer")
    parts_f0 = exchange_wait(pending["gf0"], parts_m0[0], "scatter_g0_ffn_wait")
    parts1 = exchange_wait(pending["g1"], parts_f0[0], "scatter_g1_wait")
    parts0 = list(parts_m0[:3]) + list(parts_f0) + list(parts_m0[3:])

    res = {}

    def upd(nm, parts, w2, m2, v2):
        res[nm] = reduce_adamw(parts, w2, m2, v2, f"adamw_{nm}")

    upd("a_w_in", parts0[0], a_w_in[0], m_a_w_in[0], v_a_w_in[0])
    upd("a_w_out", parts0[1], a_w_out[0], m_a_w_out[0], v_a_w_out[0])
    upd("b_w_q", parts1[0], b_w_q[0], m_b_w_q[0], v_b_w_q[0])
    upd("b_w_out", parts1[1], b_w_out[0], m_b_w_out[0], v_b_w_out[0])
    upd("kv_w", parts1[5], kv_w, m_kv_w, v_kv_w)
    for l, parts in enumerate((parts0, parts1)):
        upd(f"mem_w_kv{l}", parts[2], mem_w_kv[l], m_mem_w_kv[l], v_mem_w_kv[l])
        upd(f"ffn_w_up{l}", parts[3], ffn_w_up[l], m_ffn_w_up[l], v_ffn_w_up[l])
        upd(f"ffn_w_down{l}", parts[4], ffn_w_down[l], m_ffn_w_down[l], v_ffn_w_down[l])
    upd("small", parts0[5], pack_small(ffn_conv_w, a_pool_scale), pack_small(m_ffn_conv_w, m_a_pool_scale),
        pack_small(v_ffn_conv_w, v_a_pool_scale))
    upd("replicated", parts0[6], pack_replicated(a_pool_w, ln1_g, ln1_b, ln2_g, ln2_b, ffn_conv_b, f_b),
        pack_replicated(m_a_pool_w, m_ln1_g, m_ln1_b, m_ln2_g, m_ln2_b, m_ffn_conv_b, m_f_b),
        pack_replicated(v_a_pool_w, v_ln1_g, v_ln1_b, v_ln2_g, v_ln2_b, v_ffn_conv_b, v_f_b))

    for nm in ("a_w_in", "a_w_out", "b_w_q", "b_w_out"):
        res[nm] = [o[None] for o in res[nm]]
    for nm in ("mem_w_kv", "ffn_w_up", "ffn_w_down"):
        res[nm] = [jnp.stack([a0, a1]) for a0, a1 in zip(res[nm + "0"], res[nm + "1"])]
    res["ffn_conv_w"] = [o[:DEPTH * 3, :FF_BLOCK].reshape(DEPTH, 3, FF_BLOCK) for o in res["small"]]
    res["a_pool_scale"] = [o[8:9, :96] for o in res["small"]]
    rep_names = ["a_pool_w", "ln1_g", "ln1_b", "ln2_g", "ln2_b", "ffn_conv_b", "f_b"]
    for nm in rep_names:
        res[nm] = []
    for o in res["replicated"]:
        for nm, val in zip(rep_names, unpack_replicated(o)):
            res[nm].append(val)

    order = ["a_w_in", "a_pool_w", "a_pool_scale", "a_w_out", "b_w_q", "b_w_out", "kv_w", "f_b", "mem_w_kv",
             "ln1_g", "ln1_b", "ln2_g", "ln2_b", "ffn_w_up", "ffn_conv_w", "ffn_conv_b", "ffn_w_down"]
    out = [loss, grad_x.reshape(nb, s, d)]
    for kind in range(4):
        out.extend(res[nm][kind] for nm in order)
    return tuple(out)
```

```python
import jax
import jax.numpy as jnp
from jax import lax
from jax.experimental import pallas as pl
from jax.experimental.pallas import tpu as pltpu

F32 = jnp.float32
BF16 = jnp.bfloat16
SDS = jax.ShapeDtypeStruct

N_DEV = 8
D_MODEL = 1024
TOK_WIDTH = 768
MEM_WIDTH = 256
MEM_LEN = 256
MEM_HEADS = 4
HEAD_DIM = 64
FOX_HEADS = 12
POOL_GROUP = 192
FF_BLOCK = 688
FF_BLOCK_PAD = 768
FF_PAIRS = 4
FF_ROWS = 344
KV_COLS = 1548
KV_COLS_PAD = 1664
LANES = 128
DEPTH = 2
DN_ALPHA = (2.0 * DEPTH) ** 0.25
LN_EPS = 1e-5
QK_SCALE = HEAD_DIM ** -0.5
NEG_BIG = -1e30

ADAM_LR = 0.001
ADAM_B1 = 0.9
ADAM_B2 = 0.999
ADAM_EPS = 1e-08
ADAM_WD = 0.01
ADAM_STEP = 10

VMEM_LIMIT_BYTES = 56 * 1024 * 1024
MM_BLOCK_BYTES = 6 * 1024 * 1024
TM = 512
TS = 256
TF = 256
TC = 256
HALO_POOL = 16
HALO_CONV = 8

NT_DIMS = (((1,), (1,)), ((), ()))
TN_DIMS = (((0,), (0,)), ((), ()))


def _params(sem=None):
    return pltpu.CompilerParams(dimension_semantics=sem, vmem_limit_bytes=VMEM_LIMIT_BYTES)


def _sigmoid(z):
    return 1.0 / (1.0 + jnp.exp(-z))


def _pick_tn(n):
    if n <= 2048:
        return n
    for t in (1024, 768, 512, 256, 128):
        if n % t == 0:
            return t
    return n


def mm_nn(a, b, out_dtype, name, addend=None, add_scale=1.0, also_bf16=False):
    m, k = a.shape
    _, n = b.shape
    tm = min(TM, m)
    tn = n
    while k * tn * 2 > MM_BLOCK_BYTES or tm * tn * 4 > MM_BLOCK_BYTES:
        tn //= 2
    chunk = tn if tn <= 2048 else _pick_tn(tn)
    has_add = addend is not None

    def body(*refs):
        a_ref, b_ref = refs[0], refs[1]
        c_ref = refs[2] if has_add else None
        o_ref = refs[3] if has_add else refs[2]
        ob_ref = refs[-1] if also_bf16 else None
        av = a_ref[...].astype(BF16)
        for c in range(tn // chunk):
            cols = slice(c * chunk, (c + 1) * chunk)
            r = jnp.dot(av, b_ref[:, cols].astype(BF16), preferred_element_type=F32)
            if has_add:
                r = r + add_scale * c_ref[:, cols]
            o_ref[:, cols] = r.astype(out_dtype)
            if also_bf16:
                ob_ref[:, cols] = r.astype(BF16)

    in_specs = [pl.BlockSpec((tm, k), lambda j, i: (i, 0)), pl.BlockSpec((k, tn), lambda j, i: (0, j))]
    ops = [a, b]
    tile = pl.BlockSpec((tm, tn), lambda j, i: (i, j))
    if has_add:
        in_specs.append(tile)
        ops.append(addend)
    out_shape = [SDS((m, n), out_dtype)]
    out_specs = [tile]
    if also_bf16:
        out_shape.append(SDS((m, n), BF16))
        out_specs.append(tile)
    res = pl.pallas_call(
        body, name=name, grid=(n // tn, m // tm), in_specs=in_specs, out_specs=out_specs, out_shape=out_shape,
        compiler_params=_params(("parallel", "parallel")))(*ops)
    return tuple(res) if also_bf16 else res[0]


def mm_tn(a, b, name, blocked=False):
    t, m = a.shape
    _, n = b.shape
    tt = min(2 * TM, t)
    tm = 1024 if m % 1024 == 0 else m
    tn = FF_BLOCK_PAD if blocked else _pick_tn(n)
    nt = t // tt

    def body(a_ref, b_ref, o_ref):
        kk = pl.program_id(2)
        r = lax.dot_general(a_ref[...].astype(BF16), b_ref[...].astype(BF16), TN_DIMS, preferred_element_type=F32)
        if blocked:
            r = r[None]

        @pl.when(kk == 0)
        def _():
            o_ref[...] = r

        @pl.when(kk != 0)
        def _():
            o_ref[...] += r

    if blocked:
        out_shape = SDS((n // tn, m, tn), F32)
        out_spec = pl.BlockSpec((1, tm, tn), lambda i, j, kk: (j, i, 0))
    else:
        out_shape = SDS((m, n), F32)
        out_spec = pl.BlockSpec((tm, tn), lambda i, j, kk: (i, j))
    return pl.pallas_call(
        body, name=name, grid=(m // tm, n // tn, nt),
        in_specs=[pl.BlockSpec((tt, tm), lambda i, j, kk: (kk, i)), pl.BlockSpec((tt, tn), lambda i, j, kk: (kk, j))],
        out_specs=out_spec, out_shape=out_shape,
        compiler_params=_params(("parallel", "parallel", "arbitrary")))(a, b)


def ln_fwd(xprev, delta, g, b, name):
    t, d = xprev.shape
    tm = min(TM, t)

    def body(xp_ref, dl_ref, g_ref, b_ref, y_ref, yb_ref, xh_ref, rs_ref):
        r = DN_ALPHA * xp_ref[...] + dl_ref[...]
        mu = jnp.mean(r, axis=1, keepdims=True)
        xc = r - mu
        var = jnp.mean(xc * xc, axis=1, keepdims=True)
        rstd = lax.rsqrt(var + LN_EPS)
        xh = xc * rstd
        y = xh * g_ref[...] + b_ref[...]
        y_ref[...] = y
        yb_ref[...] = y.astype(BF16)
        xh_ref[...] = xh
        rs_ref[...] = jnp.broadcast_to(rstd, (tm, LANES))

    row = pl.BlockSpec((tm, d), lambda i: (i, 0))
    vec = pl.BlockSpec((1, d), lambda i: (0, 0))
    return pl.pallas_call(
        body, name=name, grid=(t // tm,), in_specs=[row, row, vec, vec],
        out_specs=[row, row, row, pl.BlockSpec((tm, LANES), lambda i: (i, 0))],
        out_shape=[SDS((t, d), F32), SDS((t, d), BF16), SDS((t, d), F32), SDS((t, LANES), F32)],
        compiler_params=_params(("parallel",)))(xprev, delta, g, b)


def ln_bwd(dy, xhat, rstd, g, name):
    t, d = dy.shape
    tm = min(TM, t)

    def body(dy_ref, xh_ref, rs_ref, g_ref, dr_ref, drb_ref, dg_ref, db_ref):
        i = pl.program_id(0)
        dyv = dy_ref[...]
        xh = xh_ref[...]
        dxh = dyv * g_ref[...]
        m1 = jnp.mean(dxh, axis=1, keepdims=True)
        m2 = jnp.mean(dxh * xh, axis=1, keepdims=True)
        dr = rs_ref[:, 0:1] * (dxh - m1 - xh * m2)
        dr_ref[...] = dr
        drb_ref[...] = dr.astype(BF16)

        @pl.when(i == 0)
        def _():
            dg_ref[...] = jnp.zeros_like(dg_ref)
            db_ref[...] = jnp.zeros_like(db_ref)

        dg_ref[...] += jnp.sum(dyv * xh, axis=0, keepdims=True)
        db_ref[...] += jnp.sum(dyv, axis=0, keepdims=True)

    row = pl.BlockSpec((tm, d), lambda i: (i, 0))
    vec = pl.BlockSpec((1, d), lambda i: (0, 0))
    return pl.pallas_call(
        body, name=name, grid=(t // tm,),
        in_specs=[row, row, pl.BlockSpec((tm, LANES), lambda i: (i, 0)), vec],
        out_specs=[row, row, vec, vec],
        out_shape=[SDS((t, d), F32), SDS((t, d), BF16), SDS((1, d), F32), SDS((1, d), F32)],
        compiler_params=_params(("arbitrary",)))(dy, xhat, rstd, g)


def loss_head(y, target):
    t, d = y.shape
    tm = min(TM, t)
    nsteps = t // tm

    def body(y_ref, t_ref, dy_ref, l_ref, acc):
        i = pl.program_id(0)
        diff = y_ref[...] - t_ref[...]
        dy_ref[...] = diff * (1.0 / d)

        @pl.when(i == 0)
        def _():
            acc[...] = jnp.zeros_like(acc)

        acc[...] += jnp.sum(diff * diff, axis=0, keepdims=True)

        @pl.when(i == nsteps - 1)
        def _():
            tot = jnp.sum(acc[...], axis=1, keepdims=True) * (0.5 / d)
            l_ref[...] = jnp.broadcast_to(tot, (1, LANES))

    row = pl.BlockSpec((tm, d), lambda i: (i, 0))
    return pl.pallas_call(
        body, name="loss_head", grid=(nsteps,), in_specs=[row, row],
        out_specs=[row, pl.BlockSpec((1, LANES), lambda i: (0, 0))],
        out_shape=[SDS((t, d), F32), SDS((1, LANES), F32)],
        scratch_shapes=[pltpu.VMEM((1, d), F32)],
        compiler_params=_params(("arbitrary",)))(y, target)


def memattn_fwd(proj, memkv, nb, s, name):
    ts = min(TS, s)
    nq = s // ts

    def body(q_ref, kv_ref, o_ref):
        for h in range(MEM_HEADS):
            lo, hi = h * HEAD_DIM, (h + 1) * HEAD_DIM
            qh = q_ref[:, lo:hi].astype(BF16)
            kh = kv_ref[:, lo:hi]
            vh = kv_ref[:, MEM_WIDTH + lo:MEM_WIDTH + hi]
            sc = lax.dot_general(qh, kh, NT_DIMS, preferred_element_type=F32) * QK_SCALE
            p = jnp.exp(sc - jnp.max(sc, axis=1, keepdims=True))
            p = p / jnp.sum(p, axis=1, keepdims=True)
            o_ref[:, lo:hi] = jnp.dot(p.astype(BF16), vh, preferred_element_type=F32).astype(BF16)

    return pl.pallas_call(
        body, name=name, grid=(nb, nq),
        in_specs=[pl.BlockSpec((ts, MEM_WIDTH), lambda b, i: (b * nq + i, 3)),
                  pl.BlockSpec((MEM_LEN, 2 * MEM_WIDTH), lambda b, i: (b, 0))],
        out_specs=pl.BlockSpec((ts, MEM_WIDTH), lambda b, i: (b * nq + i, 0)),
        out_shape=SDS((nb * s, MEM_WIDTH), BF16),
        compiler_params=_params(("parallel", "parallel")))(proj, memkv)


def memattn_bwd(proj, memkv, dcat, nb, s, name):
    ts = min(TS, s)
    nq = s // ts

    def body(q_ref, kv_ref, do_ref, dq_ref, dkv_ref):
        i = pl.program_id(1)

        @pl.when(i == 0)
        def _():
            dkv_ref[...] = jnp.zeros_like(dkv_ref)

        for h in range(MEM_HEADS):
            lo, hi = h * HEAD_DIM, (h + 1) * HEAD_DIM
            qh = q_ref[:, lo:hi].astype(BF16)
            kh = kv_ref[:, lo:hi]
            vh = kv_ref[:, MEM_WIDTH + lo:MEM_WIDTH + hi]
            doh = do_ref[:, lo:hi].astype(BF16)
            sc = lax.dot_general(qh, kh, NT_DIMS, preferred_element_type=F32) * QK_SCALE
            p = jnp.exp(sc - jnp.max(sc, axis=1, keepdims=True))
            p = p / jnp.sum(p, axis=1, keepdims=True)
            dv = lax.dot_general(p.astype(BF16), doh, TN_DIMS, preferred_element_type=F32)
            dp = lax.dot_general(doh, vh, NT_DIMS, preferred_element_type=F32)
            dl = jnp.sum(p * dp, axis=1, keepdims=True)
            ds = (p * (dp - dl) * QK_SCALE).astype(BF16)
            dq_ref[:, lo:hi] = jnp.dot(ds, kh, preferred_element_type=F32).astype(BF16)
            dkv_ref[:, lo:hi] += lax.dot_general(ds, qh, TN_DIMS, preferred_element_type=F32)
            dkv_ref[:, MEM_WIDTH + lo:MEM_WIDTH + hi] += dv

    return pl.pallas_call(
        body, name=name, grid=(nb, nq),
        in_specs=[pl.BlockSpec((ts, MEM_WIDTH), lambda b, i: (b * nq + i, 3)),
                  pl.BlockSpec((MEM_LEN, 2 * MEM_WIDTH), lambda b, i: (b, 0)),
                  pl.BlockSpec((ts, MEM_WIDTH), lambda b, i: (b * nq + i, 3))],
        out_specs=[pl.BlockSpec((ts, MEM_WIDTH), lambda b, i: (b * nq + i, 0)),
                   pl.BlockSpec((MEM_LEN, 2 * MEM_WIDTH), lambda b, i: (b, 0))],
        out_shape=[SDS((nb * s, MEM_WIDTH), BF16), SDS((nb * MEM_LEN, 2 * MEM_WIDTH), F32)],
        compiler_params=_params(("parallel", "arbitrary")))(proj, memkv, dcat)


def _pool_select(shape, s2, s4, s8, s16):
    lane = lax.broadcasted_iota(jnp.int32, shape, 1)
    return jnp.where(lane < POOL_GROUP, s2, jnp.where(lane < 2 * POOL_GROUP, s4, jnp.where(lane < 3 * POOL_GROUP, s8, s16)))


def _pool_count(shape, first_pos):
    pos = first_pos + lax.broadcasted_iota(jnp.int32, shape, 0)
    win = _pool_select(shape, 2, 4, 8, 16)
    return jnp.minimum(pos + 1, win).astype(F32)


def pool_fwd(proj, pw_bd, pscale, nb, s):
    ts = min(TS, s)
    nq = s // ts
    w = TOK_WIDTH

    def body(c_ref, h_ref, w_ref, sc_ref, pooled_ref, tok_ref):
        i = pl.program_id(0) % nq
        cur = c_ref[...]
        halo = jnp.where(i == 0, 0.0, h_ref[...])
        xe = jnp.concatenate([halo, cur], axis=0)
        s2 = xe + pltpu.roll(xe, 1, axis=0)
        s4 = s2 + pltpu.roll(s2, 2, axis=0)
        s8 = s4 + pltpu.roll(s4, 4, axis=0)
        s16 = s8 + pltpu.roll(s8, 8, axis=0)
        hp = HALO_POOL
        ws = _pool_select((ts, w), s2[hp:], s4[hp:], s8[hp:], s16[hp:])
        pooled = (ws / _pool_count((ts, w), i * ts) - cur).astype(BF16)
        pooled_ref[...] = pooled
        mixed = jnp.dot(pooled, w_ref[...], preferred_element_type=F32)
        tok_ref[...] = (mixed * sc_ref[...]).astype(BF16)

    row = pl.BlockSpec((ts, w), lambda r: (r, 0))
    return pl.pallas_call(
        body, name="pool_fwd", grid=(nb * nq,),
        in_specs=[row, pl.BlockSpec((HALO_POOL, w), lambda r: (jnp.maximum(r * (ts // HALO_POOL) - 1, 0), 0)),
                  pl.BlockSpec((w, w), lambda r: (0, 0)), pl.BlockSpec((1, w), lambda r: (0, 0))],
        out_specs=[row, row], out_shape=[SDS((nb * s, w), BF16), SDS((nb * s, w), BF16)],
        compiler_params=_params(("parallel",)))(proj, proj, pw_bd, pscale)


def pool_bwd_mix(dcat, pooled, pw_bd, pw_bd_t, pscale, nb, s):
    ts = min(TS, s)
    w = TOK_WIDTH

    def body(dt_ref, p_ref, w_ref, wt_ref, sc_ref, dm_ref, dp_ref, ds_ref):
        r = pl.program_id(0)
        dtok = dt_ref[...]
        mixed = jnp.dot(p_ref[...], w_ref[...], preferred_element_type=F32)

        @pl.when(r == 0)
        def _():
            ds_ref[...] = jnp.zeros_like(ds_ref)

        ds_ref[...] += jnp.sum(dtok * mixed, axis=0, keepdims=True)
        dmx = (dtok * sc_ref[...]).astype(BF16)
        dm_ref[...] = dmx
        dp_ref[...] = jnp.dot(dmx, wt_ref[...], preferred_element_type=F32)

    row = pl.BlockSpec((ts, w), lambda r: (r, 0))
    mat = pl.BlockSpec((w, w), lambda r: (0, 0))
    vec = pl.BlockSpec((1, w), lambda r: (0, 0))
    return pl.pallas_call(
        body, name="pool_bwd_mix", grid=(nb * s // ts,), in_specs=[row, row, mat, mat, vec],
        out_specs=[row, row, vec], out_shape=[SDS((nb * s, w), BF16), SDS((nb * s, w), F32), SDS((1, w), F32)],
        compiler_params=_params(("arbitrary",)))(dcat, pooled, pw_bd, pw_bd_t, pscale)


def pool_bwd_window(dpooled, nb, s):
    ts = min(TS, s)
    nq = s // ts
    w = TOK_WIDTH
    n_ext = ts + HALO_POOL
    n_halo_blocks = nb * s // HALO_POOL

    def body(c_ref, n_ref, du_ref):
        i = pl.program_id(0) % nq
        cur = c_ref[...]
        nxt = jnp.where(i == nq - 1, 0.0, n_ref[...])
        ze = jnp.concatenate([cur, nxt], axis=0) / _pool_count((n_ext, w), i * ts)
        s2 = ze + pltpu.roll(ze, n_ext - 1, axis=0)
        s4 = s2 + pltpu.roll(s2, n_ext - 2, axis=0)
        s8 = s4 + pltpu.roll(s4, n_ext - 4, axis=0)
        s16 = s8 + pltpu.roll(s8, n_ext - 8, axis=0)
        ws = _pool_select((ts, w), s2[:ts], s4[:ts], s8[:ts], s16[:ts])
        du_ref[...] = (ws - cur).astype(BF16)

    row = pl.BlockSpec((ts, w), lambda r: (r, 0))
    return pl.pallas_call(
        body, name="pool_bwd_window", grid=(nb * nq,),
        in_specs=[row, pl.BlockSpec((HALO_POOL, w),
                                    lambda r: (jnp.minimum((r + 1) * (ts // HALO_POOL), n_halo_blocks - 1), 0))],
        out_specs=row, out_shape=SDS((nb * s, w), BF16),
        compiler_params=_params(("parallel",)))(dpooled, dpooled)


def _conv_rows(xe, w_ref):
    return (w_ref[0, 2:3, :] * xe + w_ref[0, 1:2, :] * pltpu.roll(xe, 1, axis=0)
            + w_ref[0, 0:1, :] * pltpu.roll(xe, 2, axis=0) + w_ref[0, 3:4, :])


def convgate_fwd(h, cw, nb, s, name):
    ts = min(TS, s)
    nq = s // ts
    w = FF_BLOCK_PAD
    hc = HALO_CONV

    def body(uc_ref, uh_ref, gc_ref, gh_ref, wu_ref, wg_ref, o_ref):
        first = (pl.program_id(0) % nq) == 0
        xu = jnp.concatenate([jnp.where(first, 0.0, uh_ref[...]), uc_ref[...]], axis=0)
        xg = jnp.concatenate([jnp.where(first, 0.0, gh_ref[...]), gc_ref[...]], axis=0)
        cu = _conv_rows(xu, wu_ref)[hc:]
        cg = _conv_rows(xg, wg_ref)[hc:]
        o_ref[...] = (cg * _sigmoid(cg) * cu).astype(BF16)

    def cur(off):
        return pl.BlockSpec((ts, w), lambda r, j: (r, j + off))

    def halo(off):
        return pl.BlockSpec((hc, w), lambda r, j: (jnp.maximum(r * (ts // hc) - 1, 0), j + off))

    def wspec(off):
        return pl.BlockSpec((1, 8, w), lambda r, j: (j + off, 0, 0))

    return pl.pallas_call(
        body, name=name, grid=(nb * nq, FF_PAIRS),
        in_specs=[cur(0), halo(0), cur(FF_PAIRS), halo(FF_PAIRS), wspec(0), wspec(FF_PAIRS)],
        out_specs=pl.BlockSpec((ts, w), lambda r, j: (r, j)), out_shape=SDS((nb * s, FF_PAIRS * w), BF16),
        compiler_params=_params(("parallel", "parallel")))(h, h, h, h, cw, cw)


def convgate_bwd(h, dact, cw, nb, s, name):
    ts = min(TS, s)
    nq = s // ts
    w = FF_BLOCK_PAD
    hc = HALO_CONV
    n_ext = ts + hc
    n_halo_blocks = nb * s // hc

    def body(uc_ref, up_ref, un_ref, gc_ref, gp_ref, gn_ref, dc_ref, dn_ref, wu_ref, wg_ref,
             dhu_ref, dhg_ref, dwu_ref, dwg_ref):
        r = pl.program_id(1)
        i = r % nq
        first = i == 0
        last = i == nq - 1
        xu = jnp.concatenate([jnp.where(first, 0.0, up_ref[...]), uc_ref[...], un_ref[...]], axis=0)
        xg = jnp.concatenate([jnp.where(first, 0.0, gp_ref[...]), gc_ref[...], gn_ref[...]], axis=0)
        cu = _conv_rows(xu, wu_ref)[hc:]
        cg = _conv_rows(xg, wg_ref)[hc:]
        da = jnp.concatenate([dc_ref[...].astype(F32), jnp.where(last, 0.0, dn_ref[...].astype(F32)[:hc])], axis=0)
        sg = _sigmoid(cg)
        dcu = da * (cg * sg)
        dcg = da * cu * (sg * (1.0 + cg * (1.0 - sg)))

        def conv_t(dcv, w_ref):
            return (w_ref[0, 2:3, :] * dcv + w_ref[0, 1:2, :] * pltpu.roll(dcv, n_ext - 1, axis=0)
                    + w_ref[0, 0:1, :] * pltpu.roll(dcv, n_ext - 2, axis=0))[:ts]

        dhu_ref[...] = conv_t(dcu, wu_ref).astype(BF16)
        dhg_ref[...] = conv_t(dcg, wg_ref).astype(BF16)

        def tap_grads(xe, dcv):
            d0 = dcv[:ts]
            x0 = xe[hc:hc + ts]
            x1 = pltpu.roll(xe, 1, axis=0)[hc:hc + ts]
            x2 = pltpu.roll(xe, 2, axis=0)[hc:hc + ts]
            rows = [jnp.sum(d0 * x2, axis=0, keepdims=True), jnp.sum(d0 * x1, axis=0, keepdims=True),
                    jnp.sum(d0 * x0, axis=0, keepdims=True), jnp.sum(d0, axis=0, keepdims=True)]
            sub = lax.broadcasted_iota(jnp.int32, (8, w), 0)
            upd = jnp.zeros((8, w), F32)
            for k, rv in enumerate(rows):
                upd = jnp.where(sub == k, rv, upd)
            return upd[None]

        @pl.when(r == 0)
        def _():
            dwu_ref[...] = jnp.zeros_like(dwu_ref)
            dwg_ref[...] = jnp.zeros_like(dwg_ref)

        dwu_ref[...] += tap_grads(xu, dcu)
        dwg_ref[...] += tap_grads(xg, dcg)

    def cur(off):
        return pl.BlockSpec((ts, w), lambda j, r: (r, j + off))

    def prev(off):
        return pl.BlockSpec((hc, w), lambda j, r: (jnp.maximum(r * (ts // hc) - 1, 0), j + off))

    def nxt(off):
        return pl.BlockSpec((hc, w), lambda j, r: (jnp.minimum((r + 1) * (ts // hc), n_halo_blocks - 1), j + off))

    def wspec(off):
        return pl.BlockSpec((1, 8, w), lambda j, r: (j + off, 0, 0))

    hb = 2 * hc
    dact_next = pl.BlockSpec((hb, w), lambda j, r: (jnp.minimum((r + 1) * (ts // hb), nb * s // hb - 1), j))

    p = FF_PAIRS
    dh_spec = pl.BlockSpec((ts, w), lambda j, r: (r, j))
    dw_spec = pl.BlockSpec((1, 8, w), lambda j, r: (j, 0, 0))
    return pl.pallas_call(
        body, name=name, grid=(p, nb * nq),
        in_specs=[cur(0), prev(0), nxt(0), cur(p), prev(p), nxt(p), cur(0), dact_next, wspec(0), wspec(p)],
        out_specs=[dh_spec, dh_spec, dw_spec, dw_spec],
        out_shape=[SDS((nb * s, p * w), BF16), SDS((nb * s, p * w), BF16), SDS((p, 8, w), F32), SDS((p, 8, w), F32)],
        compiler_params=_params(("parallel", "arbitrary")))(h, h, h, h, h, h, dact, dact, cw, cw)


def _tri(n, upper):
    r = lax.broadcasted_iota(jnp.int32, (n, n), 0)
    c = lax.broadcasted_iota(jnp.int32, (n, n), 1)
    return ((r <= c) if upper else (r >= c)).astype(F32)


def fgate_fwd(fl, fb, nb, s):
    tc = min(TC, s)
    nq = s // tc

    def body(fl_ref, fb_ref, f_ref, carry):
        @pl.when(pl.program_id(1) == 0)
        def _():
            carry[...] = jnp.zeros_like(carry)

        z = fl_ref[...] + fb_ref[...]
        logf = jnp.minimum(z, 0.0) - jnp.log(1.0 + jnp.exp(-jnp.abs(z)))
        f_ref[...] = jnp.dot(_tri(tc, False), logf, preferred_element_type=F32,
                             precision=lax.Precision.HIGHEST) + carry[...]
        carry[...] += jnp.sum(logf, axis=0, keepdims=True)

    row = pl.BlockSpec((tc, LANES), lambda b, i: (b * nq + i, 0))
    return pl.pallas_call(
        body, name="fgate_fwd", grid=(nb, nq), in_specs=[row, pl.BlockSpec((1, LANES), lambda b, i: (0, 0))],
        out_specs=row, out_shape=SDS((nb * s, LANES), F32), scratch_shapes=[pltpu.VMEM((1, LANES), F32)],
        compiler_params=_params(("arbitrary", "arbitrary")))(fl, fb)


def fgate_bwd(d_cum_q, d_cum_k, fl, fb, nb, s):
    tc = min(TC, s)
    nq = s // tc

    def body(dfq_ref, dfk_ref, fl_ref, fb_ref, dfl_ref, dfb_ref, carry):
        b = pl.program_id(0)
        i = pl.program_id(1)

        @pl.when(i == 0)
        def _():
            carry[...] = jnp.zeros_like(carry)

        @pl.when(jnp.logical_and(b == 0, i == 0))
        def _():
            dfb_ref[...] = jnp.zeros_like(dfb_ref)

        dfv = dfq_ref[...] + dfk_ref[...]
        dlog = jnp.dot(_tri(tc, True), dfv, preferred_element_type=F32,
                       precision=lax.Precision.HIGHEST) + carry[...]
        carry[...] += jnp.sum(dfv, axis=0, keepdims=True)
        z = fl_ref[...] + fb_ref[...]
        dfl = dlog / (1.0 + jnp.exp(z))
        dfl_ref[...] = dfl
        dfb_ref[...] += jnp.sum(dfl, axis=0, keepdims=True)

    row = pl.BlockSpec((tc, LANES), lambda b, i: (b * nq + nq - 1 - i, 0))
    vec = pl.BlockSpec((1, LANES), lambda b, i: (0, 0))
    return pl.pallas_call(
        body, name="fgate_bwd", grid=(nb, nq), in_specs=[row, row, row, vec], out_specs=[row, vec],
        out_shape=[SDS((nb * s, LANES), F32), SDS((1, LANES), F32)], scratch_shapes=[pltpu.VMEM((1, LANES), F32)],
        compiler_params=_params(("arbitrary", "arbitrary")))(d_cum_q, d_cum_k, fl, fb)


PAIR = 2 * HEAD_DIM
N_PAIRS = FOX_HEADS // 2


def _lane_put(shape, h, col):
    lane = lax.broadcasted_iota(jnp.int32, shape, 1)
    return jnp.where(lane == h, col, 0.0)


def _half_masks(rows):
    lane = lax.broadcasted_iota(jnp.int32, (rows, PAIR), 1)
    return lane < HEAD_DIM


def _split_pair(x, scale=None):
    if scale is not None:
        x = x * scale
    lo = _half_masks(x.shape[0])
    zero = jnp.zeros_like(x)
    return jnp.where(lo, x, zero), jnp.where(lo, zero, x)


def _to_tile_rows(a, nb, s, tf):
    return a.reshape(nb * s // tf, tf, LANES)[:, :, :16].transpose(0, 2, 1)


def _from_tile_rows(a):
    tiles, _, tf = a.shape
    return jnp.pad(a.transpose(0, 2, 1), ((0, 0), (0, 0), (0, LANES - 16))).reshape(tiles * tf, LANES)


BIAS_TERMS = 3
LOOKAHEAD = 4
LOOKAHEAD_BWD = 4
LOOKAHEAD_DKV = 2


def _bias_lane(h):
    return HEAD_DIM if h % 2 == 0 else 0


def _placement():
    rows = jnp.arange(LANES)[:, None]
    cols = jnp.arange(FOX_HEADS * PAIR)[None, :]
    head, lane = cols // PAIR, cols % PAIR
    first = jnp.where(head % 2 == 0, HEAD_DIM, 0)
    term = lane - first
    hit = (term >= 0) & (term < BIAS_TERMS) & (rows == 16 * term + head)
    return hit.astype(BF16)


def fox_prep(kv, fneg, nb, s):
    tf = min(TF, s)
    w = TOK_WIDTH

    def body(k_ref, v_ref, f_ref, pl_ref, ka_ref, vt_ref):
        lane = lax.broadcasted_iota(jnp.int32, (tf, LANES), 1)
        lo = lane < HEAD_DIM
        f = jnp.where(lane < FOX_HEADS, f_ref[...], 0.0)
        hi = f.astype(BF16).astype(F32)
        mid = (f - hi).astype(BF16).astype(F32)
        low = (f - hi - mid).astype(BF16).astype(F32)
        terms = (hi + pltpu.roll(mid, 16, axis=1) + pltpu.roll(low, 32, axis=1)).astype(BF16)
        placed = jnp.dot(terms, pl_ref[...], preferred_element_type=F32).astype(BF16)
        one = jnp.ones((tf, LANES), BF16)
        zero = jnp.zeros((tf, LANES), BF16)
        for p in range(N_PAIRS):
            kp = k_ref[:, p * PAIR:(p + 1) * PAIR] * QK_SCALE
            vp = v_ref[:, p * PAIR:(p + 1) * PAIR]
            he, ho = 2 * p, 2 * p + 1
            ka_ref[:, he * PAIR:(he + 1) * PAIR] = jnp.where(lo, kp, placed[:, he * PAIR:(he + 1) * PAIR])
            ka_ref[:, ho * PAIR:(ho + 1) * PAIR] = jnp.where(lo, placed[:, ho * PAIR:(ho + 1) * PAIR], kp)
            ve = jnp.where(lo, vp, jnp.where(lane == HEAD_DIM, one, zero))
            vo = jnp.where(lo, jnp.where(lane == 0, one, zero), vp)
            vt_ref[0, he * PAIR:(he + 1) * PAIR, :] = ve.astype(F32).T.astype(BF16)
            vt_ref[0, ho * PAIR:(ho + 1) * PAIR, :] = vo.astype(F32).T.astype(BF16)

    return pl.pallas_call(
        body, name="fox_prep", grid=(nb * s // tf,),
        in_specs=[pl.BlockSpec((tf, w), lambda r: (r, 0)), pl.BlockSpec((tf, w), lambda r: (r, 1)),
                  pl.BlockSpec((tf, LANES), lambda r: (r, 0)), pl.BlockSpec((LANES, FOX_HEADS * PAIR), lambda r: (0, 0))],
        out_specs=[pl.BlockSpec((tf, FOX_HEADS * PAIR), lambda r: (r, 0)),
                   pl.BlockSpec((1, FOX_HEADS * PAIR, tf), lambda r: (r, 0, 0))],
        out_shape=[SDS((nb * s, FOX_HEADS * PAIR), BF16), SDS((nb * s // tf, FOX_HEADS * PAIR, tf), BF16)],
        compiler_params=_params(("parallel",)))(kv, kv, fneg, _placement())


def fox_fwd_t(pq, kaug, vaug_t, nb, s):
    tf = min(TF, s)
    n = s // tf
    w = TOK_WIDTH
    wa = FOX_HEADS * PAIR

    def body(q_ref, k_hbm, vt_hbm, ob_ref, of_ref, lse_ref, k_vm, vt_vm, qx_scr, m_scr, acc_scr, sems):
        b = pl.program_id(0)
        i = pl.program_id(1)

        @pl.when(i == 0)
        def _():
            ck = pltpu.make_async_copy(k_hbm.at[pl.ds(pl.multiple_of(b * s, tf), s)], k_vm, sems.at[0])
            cv = pltpu.make_async_copy(vt_hbm.at[pl.ds(b * n, n)], vt_vm, sems.at[1])
            ck.start()
            cv.start()
            ck.wait()
            cv.wait()

        lane = lax.broadcasted_iota(jnp.int32, (tf, PAIR), 1)
        one = jnp.ones((tf, PAIR), BF16)
        zero = jnp.zeros((tf, PAIR), BF16)
        for p in range(N_PAIRS):
            qp = q_ref[:, p * PAIR:(p + 1) * PAIR]
            be, bo = _bias_lane(2 * p), _bias_lane(2 * p + 1)
            ones_e = jnp.where((lane >= be) & (lane < be + BIAS_TERMS), one, zero)
            ones_o = jnp.where((lane >= bo) & (lane < bo + BIAS_TERMS), one, zero)
            qx_scr[2 * p] = jnp.where(lane < HEAD_DIM, qp, ones_e)
            qx_scr[2 * p + 1] = jnp.where(lane < HEAD_DIM, ones_o, qp)
        m_scr[...] = jnp.full(m_scr.shape, NEG_BIG, F32)
        acc_scr[...] = jnp.zeros_like(acc_scr)

        def tile(j, masked):
            ks = pl.multiple_of(j * tf, tf)
            if masked:
                keep = lax.broadcasted_iota(jnp.int32, (tf, tf), 1) >= lax.broadcasted_iota(jnp.int32, (tf, tf), 0)
            def scores(h):
                kx = k_vm[pl.ds(ks, tf), h * PAIR:(h + 1) * PAIR]
                return lax.dot_general(kx, qx_scr[h], NT_DIMS, preferred_element_type=F32)

            ahead = [scores(h) for h in range(LOOKAHEAD)]
            for h in range(FOX_HEADS):
                sc = ahead.pop(0)
                if h + LOOKAHEAD < FOX_HEADS:
                    ahead.append(scores(h + LOOKAHEAD))
                if masked:
                    sc = jnp.where(keep, sc, NEG_BIG)
                m_prev = m_scr[h]
                m_new = jnp.maximum(m_prev, jnp.max(sc, axis=0, keepdims=True))
                pr = jnp.exp(sc - m_new).astype(BF16)
                pv = jnp.dot(vt_vm[j, h * PAIR:(h + 1) * PAIR, :], pr, preferred_element_type=F32)
                acc_scr[h] = jnp.exp(m_prev - m_new) * acc_scr[h] + pv
                m_scr[h] = m_new

        def step(j, carry):
            tile(j, False)
            return carry

        lax.fori_loop(0, i, step, 0)
        tile(i, True)

        top = lax.broadcasted_iota(jnp.int32, (PAIR, tf), 0) < HEAD_DIM
        sub = lax.broadcasted_iota(jnp.int32, (16, tf), 0)
        lse = jnp.zeros((16, tf), F32)
        for p in range(N_PAIRS):
            he, ho = 2 * p, 2 * p + 1
            le = acc_scr[he, HEAD_DIM:HEAD_DIM + 1, :]
            lod = acc_scr[ho, 0:1, :]
            o = jnp.where(top, acc_scr[he] / le, acc_scr[ho] / lod).T
            ob_ref[:, p * PAIR:(p + 1) * PAIR] = o.astype(BF16)
            of_ref[:, p * PAIR:(p + 1) * PAIR] = o
            lse = jnp.where(sub == he, m_scr[he] + jnp.log(le), lse)
            lse = jnp.where(sub == ho, m_scr[ho] + jnp.log(lod), lse)
        lse_ref[0] = lse

    qrow = lambda b, i: (b * n + i, 0)
    return pl.pallas_call(
        body, name="fox_fwd", grid=(nb, n),
        in_specs=[pl.BlockSpec((tf, w), qrow), ANY_SPEC, ANY_SPEC],
        out_specs=[pl.BlockSpec((tf, w), qrow), pl.BlockSpec((tf, w), qrow),
                   pl.BlockSpec((1, 16, tf), lambda b, i: (b * n + i, 0, 0))],
        out_shape=[SDS((nb * s, w), BF16), SDS((nb * s, w), F32), SDS((nb * n, 16, tf), F32)],
        scratch_shapes=[pltpu.VMEM((s, wa), BF16), pltpu.VMEM((n, wa, tf), BF16),
                        pltpu.VMEM((FOX_HEADS, tf, PAIR), BF16), pltpu.VMEM((FOX_HEADS, 1, tf), F32),
                        pltpu.VMEM((FOX_HEADS, PAIR, tf), F32), pltpu.SemaphoreType.DMA((2,))],
        compiler_params=_params(("arbitrary", "arbitrary")))(pq, kaug, vaug_t)


def fox_delta(dcat, o, nb, s):
    tf = min(TM, s)
    w = TOK_WIDTH

    def body(do_ref, o_ref, dl_ref):
        out = jnp.zeros((tf, LANES), F32)
        for h in range(FOX_HEADS):
            lo, hi = h * HEAD_DIM, (h + 1) * HEAD_DIM
            out = out + _lane_put((tf, LANES), h, jnp.sum(do_ref[:, lo:hi] * o_ref[:, lo:hi], axis=1, keepdims=True))
        dl_ref[...] = out

    row = pl.BlockSpec((tf, w), lambda r: (r, 0))
    return pl.pallas_call(
        body, name="fox_delta", grid=(nb * s // tf,), in_specs=[row, row],
        out_specs=pl.BlockSpec((tf, LANES), lambda r: (r, 0)), out_shape=SDS((nb * s, LANES), F32),
        compiler_params=_params(("parallel",)))(dcat, o)


def fox_bwd_dq(pq, kv, fneg_rows, dcat_bf, lse, delta, nb, s):
    tf = min(TF, s)
    n = s // tf
    w = TOK_WIDTH

    def body(q_ref, k_ref, v_ref, ft_ref, do_ref, lse_ref, dl_ref, dq_ref, df_ref, qm_scr, dom_scr, acc_scr, rs_scr):
        i = pl.program_id(1)
        for p in range(N_PAIRS):
            qe, qo = _split_pair(q_ref[:, p * PAIR:(p + 1) * PAIR], QK_SCALE)
            qm_scr[2 * p] = qe
            qm_scr[2 * p + 1] = qo
            de, dod = _split_pair(do_ref[:, p * PAIR:(p + 1) * PAIR])
            dom_scr[2 * p] = de
            dom_scr[2 * p + 1] = dod
        acc_scr[...] = jnp.zeros_like(acc_scr)
        rs_scr[...] = jnp.zeros_like(rs_scr)

        def tile(j, masked):
            ks = pl.multiple_of(j * tf, tf)
            if masked:
                keep = lax.broadcasted_iota(jnp.int32, (tf, tf), 0) >= lax.broadcasted_iota(jnp.int32, (tf, tf), 1)
            def products(h):
                p = h // 2
                kp = k_ref[pl.ds(ks, tf), p * PAIR:(p + 1) * PAIR]
                vp = v_ref[pl.ds(ks, tf), p * PAIR:(p + 1) * PAIR]
                return (lax.dot_general(qm_scr[h], kp, NT_DIMS, preferred_element_type=F32),
                        lax.dot_general(dom_scr[h], vp, NT_DIMS, preferred_element_type=F32))

            ahead = [products(h) for h in range(LOOKAHEAD_BWD)]
            for h in range(FOX_HEADS):
                sc, dp = ahead.pop(0)
                if h + LOOKAHEAD_BWD < FOX_HEADS:
                    ahead.append(products(h + LOOKAHEAD_BWD))
                kp = k_ref[pl.ds(ks, tf), (h // 2) * PAIR:(h // 2 + 1) * PAIR]
                sc = sc + ft_ref[j, h:h + 1, :] - lse_ref[:, h:h + 1]
                if masked:
                    sc = jnp.where(keep, sc, NEG_BIG)
                pr = jnp.exp(sc)
                ds = pr * (dp - dl_ref[:, h:h + 1])
                part = ds[:, :LANES]
                for c in range(1, tf // LANES):
                    part = part + ds[:, c * LANES:(c + 1) * LANES]
                rs_scr[h] += part
                acc_scr[h] += jnp.dot(ds.astype(BF16), kp, preferred_element_type=F32)

        def step(j, carry):
            tile(j, False)
            return carry

        lax.fori_loop(0, i, step, 0)
        tile(i, True)

        lo = _half_masks(tf)
        dfq = jnp.zeros((tf, LANES), F32)
        for p in range(N_PAIRS):
            dq = jnp.where(lo, acc_scr[2 * p], acc_scr[2 * p + 1]) * QK_SCALE
            dq_ref[:, p * PAIR:(p + 1) * PAIR] = dq.astype(BF16)
            for h in (2 * p, 2 * p + 1):
                dfq = dfq + _lane_put((tf, LANES), h, jnp.sum(rs_scr[h], axis=1, keepdims=True))
        df_ref[...] = dfq

    qrow = lambda b, i: (b * n + i, 0)
    stat = pl.BlockSpec((tf, LANES), qrow)
    return pl.pallas_call(
        body, name="fox_bwd_dq", grid=(nb, n),
        in_specs=[pl.BlockSpec((tf, w), qrow), pl.BlockSpec((s, w), lambda b, i: (b, 0)),
                  pl.BlockSpec((s, w), lambda b, i: (b, 1)), pl.BlockSpec((n, 16, tf), lambda b, i: (b, 0, 0)),
                  pl.BlockSpec((tf, w), qrow), stat, stat],
        out_specs=[pl.BlockSpec((tf, w), qrow), stat],
        out_shape=[SDS((nb * s, w), BF16), SDS((nb * s, LANES), F32)],
        scratch_shapes=[pltpu.VMEM((FOX_HEADS, tf, PAIR), BF16), pltpu.VMEM((FOX_HEADS, tf, PAIR), BF16),
                        pltpu.VMEM((FOX_HEADS, tf, PAIR), F32), pltpu.VMEM((FOX_HEADS, tf, LANES), F32)],
        compiler_params=_params(("parallel", "arbitrary")))(pq, kv, kv, fneg_rows, dcat_bf, lse, delta)


def fox_bwd_dkv(pq, kv, fneg, dcat_bf, lse_rows, delta_rows, nb, s):
    tf = min(TF, s)
    n = s // tf
    w = TOK_WIDTH

    def body(q_ref, k_ref, v_ref, f_ref, do_ref, lse_ref, dl_ref, dk_ref, dv_ref, df_ref,
             km_scr, vm_scr, fk_scr, dk_scr, dv_scr, rs_scr):
        j = pl.program_id(1)
        for p in range(N_PAIRS):
            ke, ko = _split_pair(k_ref[:, p * PAIR:(p + 1) * PAIR], QK_SCALE)
            km_scr[2 * p] = ke
            km_scr[2 * p + 1] = ko
            ve, vo = _split_pair(v_ref[:, p * PAIR:(p + 1) * PAIR])
            vm_scr[2 * p] = ve
            vm_scr[2 * p + 1] = vo
        for h in range(FOX_HEADS):
            fk_scr[h] = jnp.broadcast_to(f_ref[:, h:h + 1], (tf, tf))
        dk_scr[...] = jnp.zeros_like(dk_scr)
        dv_scr[...] = jnp.zeros_like(dv_scr)
        rs_scr[...] = jnp.zeros_like(rs_scr)

        def tile(i, masked):
            qs = pl.multiple_of(i * tf, tf)
            if masked:
                keep = lax.broadcasted_iota(jnp.int32, (tf, tf), 1) >= lax.broadcasted_iota(jnp.int32, (tf, tf), 0)
            def scores(h):
                qp = q_ref[pl.ds(qs, tf), (h // 2) * PAIR:(h // 2 + 1) * PAIR]
                return lax.dot_general(km_scr[h], qp, NT_DIMS, preferred_element_type=F32)

            ahead = [scores(h) for h in range(LOOKAHEAD_DKV)]
            for h in range(FOX_HEADS):
                sc = ahead.pop(0)
                if h + LOOKAHEAD_DKV < FOX_HEADS:
                    ahead.append(scores(h + LOOKAHEAD_DKV))
                p = h // 2
                qp = q_ref[pl.ds(qs, tf), p * PAIR:(p + 1) * PAIR]
                dop = do_ref[pl.ds(qs, tf), p * PAIR:(p + 1) * PAIR]
                sc = sc + fk_scr[h] - lse_ref[i, h:h + 1, :]
                if masked:
                    sc = jnp.where(keep, sc, NEG_BIG)
                pr = jnp.exp(sc)
                dv_scr[h] += jnp.dot(pr.astype(BF16), dop, preferred_element_type=F32)
                dp = lax.dot_general(vm_scr[h], dop, NT_DIMS, preferred_element_type=F32)
                ds = pr * (dp - dl_ref[i, h:h + 1, :])
                part = ds[:, :LANES]
                for c in range(1, tf // LANES):
                    part = part + ds[:, c * LANES:(c + 1) * LANES]
                rs_scr[h] += part
                dk_scr[h] += jnp.dot(ds.astype(BF16), qp, preferred_element_type=F32)

        def step(i, carry):
            tile(i, False)
            return carry

        tile(j, True)
        lax.fori_loop(j + 1, n, step, 0)

        lo = _half_masks(tf)
        dfk = jnp.zeros((tf, LANES), F32)
        for p in range(N_PAIRS):
            dk = jnp.where(lo, dk_scr[2 * p], dk_scr[2 * p + 1]) * QK_SCALE
            dk_ref[:, p * PAIR:(p + 1) * PAIR] = dk.astype(BF16)
            dv_ref[:, p * PAIR:(p + 1) * PAIR] = jnp.where(lo, dv_scr[2 * p], dv_scr[2 * p + 1]).astype(BF16)
            for h in (2 * p, 2 * p + 1):
                dfk = dfk - _lane_put((tf, LANES), h, jnp.sum(rs_scr[h], axis=1, keepdims=True))
        df_ref[...] = dfk

    krow = lambda b, j: (b * n + j, 0)
    rows = pl.BlockSpec((n, 16, tf), lambda b, j: (b, 0, 0))
    return pl.pallas_call(
        body, name="fox_bwd_dkv", grid=(nb, n),
        in_specs=[pl.BlockSpec((s, w), lambda b, j: (b, 0)), pl.BlockSpec((tf, w), krow),
                  pl.BlockSpec((tf, w), lambda b, j: (b * n + j, 1)), pl.BlockSpec((tf, LANES), krow),
                  pl.BlockSpec((s, w), lambda b, j: (b, 0)), rows, rows],
        out_specs=[pl.BlockSpec((tf, w), krow), pl.BlockSpec((tf, w), krow), pl.BlockSpec((tf, LANES), krow)],
        out_shape=[SDS((nb * s, w), BF16), SDS((nb * s, w), BF16), SDS((nb * s, LANES), F32)],
        scratch_shapes=[pltpu.VMEM((FOX_HEADS, tf, PAIR), BF16), pltpu.VMEM((FOX_HEADS, tf, PAIR), BF16),
                        pltpu.VMEM((FOX_HEADS, tf, tf), F32), pltpu.VMEM((FOX_HEADS, tf, PAIR), F32),
                        pltpu.VMEM((FOX_HEADS, tf, PAIR), F32), pltpu.VMEM((FOX_HEADS, tf, LANES), F32)],
        compiler_params=_params(("parallel", "arbitrary")))(pq, kv, kv, fneg, dcat_bf, lse_rows, delta_rows)


def reduce_adamw(parts, w, m, v, name):
    _, r, c = parts.shape
    tr = r
    for cand in range(16, r, 16):
        if r % cand == 0 and cand * c <= 128 * 1024:
            tr = cand
    c1 = 1.0 - ADAM_B1 ** ADAM_STEP
    c2 = 1.0 - ADAM_B2 ** ADAM_STEP

    def body(p_ref, w_ref, m_ref, v_ref, g_out, d_out, m_out, v_out):
        g = p_ref[0].astype(F32)
        for k in range(1, N_DEV):
            g = g + p_ref[k].astype(F32)
        mn = ADAM_B1 * m_ref[...] + (1.0 - ADAM_B1) * g
        vn = ADAM_B2 * v_ref[...] + (1.0 - ADAM_B2) * (g * g)
        g_out[...] = g
        m_out[...] = mn
        v_out[...] = vn
        d_out[...] = -ADAM_LR * ((mn / c1) / (jnp.sqrt(vn / c2) + ADAM_EPS) + ADAM_WD * w_ref[...])

    row = pl.BlockSpec((tr, c), lambda i: (i, 0))
    return pl.pallas_call(
        body, name=name, grid=(r // tr,),
        in_specs=[pl.BlockSpec((N_DEV, tr, c), lambda i: (0, i, 0)), row, row, row],
        out_specs=[row, row, row, row], out_shape=[SDS((r, c), F32)] * 4,
        compiler_params=_params(("parallel",)))(parts, w, m, v)


N_PEERS = N_DEV - 1
HBM_SPEC = pl.BlockSpec(memory_space=pltpu.HBM)
SEM_SPEC = pl.BlockSpec(memory_space=pltpu.SEMAPHORE)
ANY_SPEC = pl.BlockSpec(memory_space=pl.ANY)
SPLIT_EFFECT = pltpu.SideEffectType.DATAFLOW_SIDE_EFFECTING


def _my_index():
    return 4 * lax.axis_index("x") + 2 * lax.axis_index("y") + lax.axis_index("c")


def _peers():
    x, y, c = lax.axis_index("x"), lax.axis_index("y"), lax.axis_index("c")
    peers = []
    for k in range(1, N_DEV):
        px = 1 - x if (k >> 2) & 1 else x
        py = 1 - y if (k >> 1) & 1 else y
        pc = 1 - c if k & 1 else c
        peers.append(((px, py, pc), 4 * px + 2 * py + pc))
    return 4 * x + 2 * y + c, peers


def _push(src, dst, send_sems, recv_sems, slot, dev):
    return pltpu.make_async_remote_copy(src_ref=src, dst_ref=dst, send_sem=send_sems.at[slot], recv_sem=recv_sems.at[slot],
                                        device_id=dev, device_id_type=pl.DeviceIdType.MESH)


def _landing_shapes(arrs, scatter):
    return [SDS((N_DEV,) + tuple(a.shape[1:] if sc else a.shape), a.dtype) for a, sc in zip(arrs, scatter)]


def exchange(arrs, scatter, name):
    na = len(arrs)

    def body(*refs):
        ins = refs[:na]
        outs = refs[na:2 * na]
        send_sems, recv_sems, local_sems = refs[2 * na:]
        me, peers = _peers()
        local = []
        remote = []
        for a in range(na):
            lc = pltpu.make_async_copy(ins[a].at[me] if scatter[a] else ins[a], outs[a].at[me], local_sems.at[a])
            lc.start()
            local.append(lc)
            for k, (dev, idx) in enumerate(peers):
                cp = _push(ins[a].at[idx] if scatter[a] else ins[a], outs[a].at[me], send_sems, recv_sems,
                           a * N_PEERS + k, dev)
                cp.start()
                remote.append(cp)
        for a in range(na):
            for k, (dev, idx) in enumerate(peers):
                _push(ins[a].at[me] if scatter[a] else ins[a], outs[a].at[idx], send_sems, recv_sems,
                      a * N_PEERS + k, dev).wait_recv()
        for cp in remote:
            cp.wait_send()
        for lc in local:
            lc.wait()

    return pl.pallas_call(
        body, name=name, in_specs=[HBM_SPEC] * na, out_specs=[HBM_SPEC] * na, out_shape=_landing_shapes(arrs, scatter),
        scratch_shapes=[pltpu.SemaphoreType.DMA((na * N_PEERS,)), pltpu.SemaphoreType.DMA((na * N_PEERS,)),
                        pltpu.SemaphoreType.DMA((na,))])(*arrs)


def exchange_start(arrs, scatter, after, name):
    na = len(arrs)
    lands = [lax.empty(l.shape, l.dtype) for l in _landing_shapes(arrs, scatter)]

    def body(*refs):
        ins = refs[:na]
        land = refs[na:2 * na]
        send_sems, recv_sems = refs[2 * na + 1], refs[2 * na + 2]
        token = refs[-1]
        me, peers = _peers()
        for a in range(na):
            for k, (dev, idx) in enumerate(peers):
                _push(ins[a].at[idx] if scatter[a] else ins[a], land[a].at[me], send_sems, recv_sems,
                      a * N_PEERS + k, dev).start()
        token[...] = jnp.zeros_like(token)

    thru = [pltpu.HBM(a.shape, a.dtype) for a in arrs] + [pltpu.HBM(l.shape, l.dtype) for l in lands]
    res = pl.pallas_call(
        body, name=name,
        out_shape=(pltpu.SemaphoreType.DMA((na * N_PEERS,)), pltpu.SemaphoreType.DMA((na * N_PEERS,)), *thru,
                   SDS((8, LANES), F32)),
        in_specs=[HBM_SPEC] * (2 * na) + [ANY_SPEC],
        out_specs=(SEM_SPEC, SEM_SPEC, *([HBM_SPEC] * (2 * na)), pl.BlockSpec(memory_space=pltpu.VMEM)),
        input_output_aliases={i: 2 + i for i in range(2 * na)},
        compiler_params=pltpu.CompilerParams(has_side_effects=SPLIT_EFFECT),
    )(*[pltpu.with_memory_space_constraint(a, pltpu.HBM) for a in arrs],
      *[pltpu.with_memory_space_constraint(l, pltpu.HBM) for l in lands], after)
    return {"send": res[0], "recv": res[1], "src": res[2:2 + na], "land": res[2 + na:2 + 2 * na],
            "token": res[-1][0, 0], "scatter": scatter}


def exchange_wait(handle, after, name):
    scatter = handle["scatter"]
    na = len(scatter)

    def body(*refs):
        src = refs[:na]
        land = refs[na:2 * na]
        send_sems, recv_sems = refs[2 * na], refs[2 * na + 1]
        me, peers = _peers()
        for a in range(na):
            for k, (dev, idx) in enumerate(peers):
                cp = _push(src[a].at[me] if scatter[a] else src[a], land[a].at[idx], send_sems, recv_sems,
                           a * N_PEERS + k, dev)
                cp.wait_send()
                cp.wait_recv()

    ops = list(handle["src"]) + list(handle["land"])
    res = pl.pallas_call(
        body, name=name, out_shape=tuple(pltpu.HBM(o.shape, o.dtype) for o in ops),
        in_specs=[HBM_SPEC] * (2 * na) + [SEM_SPEC, SEM_SPEC, ANY_SPEC], out_specs=tuple([HBM_SPEC] * (2 * na)),
        input_output_aliases={i: i for i in range(2 * na)},
        compiler_params=pltpu.CompilerParams(has_side_effects=SPLIT_EFFECT),
    )(*ops, handle["send"], handle["recv"], after)
    me = _my_index()
    out = []
    for a in range(na):
        own = lax.dynamic_index_in_dim(res[a], me, 0, keepdims=True) if scatter[a] else res[a][None]
        out.append(lax.dynamic_update_slice(res[na + a], own, (me,) + (0,) * (own.ndim - 1)))
    return out


def forward_layer(l, xin, xin_bf, mem_bf, wt, nb, s, ffn_weights=None):
    sv = {"xin_bf": xin_bf}
    memkv = mm_nn(mem_bf, wt["memw"], BF16, f"memkv{l}")
    sv["memkv"] = memkv
    if l == 0:
        proj = mm_nn(xin_bf, wt["win_a"], F32, "proj_a")
        pooled, tok = pool_fwd(proj, wt["pw_bd"], wt["pscale"], nb, s)
        sv["pooled"] = pooled
    else:
        kv = mm_nn(xin_bf, wt["kvw"][:, :2 * TOK_WIDTH], BF16, "kv_proj")
        fl = mm_nn(xin_bf, wt["kvw"][:, 2 * TOK_WIDTH:], F32, "gate_proj")
        fneg = -fgate_fwd(fl, wt["fb"], nb, s)
        fneg_rows = _to_tile_rows(fneg, nb, s, min(TF, s))
        proj = mm_nn(xin_bf, wt["wq"], BF16, "proj_b")
        kaug, vaug_t = fox_prep(kv, fneg, nb, s)
        tok, o_f32, lse_rows = fox_fwd_t(proj, kaug, vaug_t, nb, s)
        sv.update(kv=kv, fl=fl, fneg=fneg, fneg_rows=fneg_rows, o_f32=o_f32, lse=_from_tile_rows(lse_rows), lse_rows=lse_rows)
    sv["proj"] = proj
    mem_out = memattn_fwd(proj, memkv, nb, s, f"memattn_fwd{l}")
    cat = jnp.concatenate([tok, mem_out], axis=1)
    sv["cat"] = cat
    mix = mm_nn(cat, wt["wout"], F32, f"out_proj{l}")
    x1, x1_bf, xh1, rs1 = ln_fwd(xin, mix, wt["ln1_g"], wt["ln1_b"], f"ln1_fwd{l}")
    sv.update(x1_bf=x1_bf, xh1=xh1, rs1=rs1)
    if ffn_weights is not None:
        wt.update(ffn_weights(x1_bf))
    h = mm_nn(x1_bf, wt["wup"], F32, f"ffn_up{l}")
    act = convgate_fwd(h, wt["cw"], nb, s, f"convgate_fwd{l}")
    sv.update(h=h, act=act)
    ffn = mm_nn(act, wt["wdown"], F32, f"ffn_down{l}")
    x2, x2_bf, xh2, rs2 = ln_fwd(x1, ffn, wt["ln2_g"], wt["ln2_b"], f"ln2_fwd{l}")
    sv.update(xh2=xh2, rs2=rs2)
    return x2, x2_bf, sv


def backward_layer(l, dy, sv, mem_bf, wt, nb, s, after_ffn=None):
    g = {}
    dr2, dr2_bf, g["ln2_g"], g["ln2_b"] = ln_bwd(dy, sv["xh2"], sv["rs2"], wt["ln2_g"], f"ln2_bwd{l}")
    dact = mm_nn(dr2_bf, wt["wdown_t"], BF16, f"ffn_down_dx{l}")
    g["wdown"] = mm_tn(sv["act"], dr2_bf, f"ffn_down_dw{l}")
    dh_u, dh_g, dcw_u, dcw_g = convgate_bwd(sv["h"], dact, wt["cw"], nb, s, f"convgate_bwd{l}")
    g["cw"] = jnp.concatenate([dcw_u, dcw_g], axis=0)
    half = FF_PAIRS * FF_BLOCK_PAD
    dx1 = mm_nn(dh_u, wt["wup_t"][:half], F32, f"ffn_up_dx_u{l}", addend=dr2, add_scale=DN_ALPHA)
    dx1 = mm_nn(dh_g, wt["wup_t"][half:], F32, f"ffn_up_dx_g{l}", addend=dx1)
    g["wup"] = jnp.concatenate([mm_tn(sv["x1_bf"], dh_u, f"ffn_up_dw_u{l}", blocked=True),
                                mm_tn(sv["x1_bf"], dh_g, f"ffn_up_dw_g{l}", blocked=True)], axis=0)
    ln1_g = wt["ln1_g"] if after_ffn is None else wt["ln1_g"] + after_ffn(g, dx1)
    dr1, dr1_bf, g["ln1_g"], g["ln1_b"] = ln_bwd(dx1, sv["xh1"], sv["rs1"], ln1_g, f"ln1_bwd{l}")
    dcat, dcat_bf = mm_nn(dr1_bf, wt["wout_t"], F32, f"out_proj_dx{l}", also_bf16=True)
    g["wout"] = mm_tn(sv["cat"], dr1_bf, f"out_proj_dw{l}")
    dqm, dmemkv = memattn_bwd(sv["proj"], sv["memkv"], dcat, nb, s, f"memattn_bwd{l}")
    g["memw"] = mm_tn(mem_bf, dmemkv, f"memkv_dw{l}")
    if l == 0:
        dmixed, dpooled, g["pscale"] = pool_bwd_mix(dcat, sv["pooled"], wt["pw_bd"], wt["pw_bd_t"], wt["pscale"], nb, s)
        g["pw_full"] = mm_tn(sv["pooled"], dmixed, "pool_dw")
        du = pool_bwd_window(dpooled, nb, s)
        dproj = jnp.concatenate([du, dqm], axis=1)
        dx = mm_nn(dproj, wt["win_a_t"], F32, "proj_a_dx", addend=dr1, add_scale=DN_ALPHA)
        g["win_a"] = mm_tn(sv["xin_bf"], dproj, "proj_a_dw")
    else:
        delta = fox_delta(dcat, sv["o_f32"], nb, s)
        tf = min(TF, s)
        dq, dfcum_q = fox_bwd_dq(sv["proj"], sv["kv"], sv["fneg_rows"], dcat_bf, sv["lse"], delta, nb, s)
        dk, dv, dfcum_k = fox_bwd_dkv(sv["proj"], sv["kv"], sv["fneg"], dcat_bf,
                                      sv["lse_rows"], _to_tile_rows(delta, nb, s, tf), nb, s)
        dfl, g["fb"] = fgate_bwd(dfcum_q, dfcum_k, sv["fl"], wt["fb"], nb, s)
        dproj = jnp.concatenate([dq, dqm], axis=1)
        dkvf = jnp.concatenate([dk, dv, dfl.astype(BF16)], axis=1)
        dx = mm_nn(dproj, wt["wq_t"], F32, "proj_b_dx", addend=dr1, add_scale=DN_ALPHA)
        dx = mm_nn(dkvf, wt["kvw_t"], F32, "kv_proj_dx", addend=dx)
        g["wq"] = mm_tn(sv["xin_bf"], dproj, "proj_b_dw")
        g["kvw"] = mm_tn(sv["xin_bf"], dkvf, "kv_proj_dw")
    return dx, g


def pack_replicated(pool_w, ln1_g, ln1_b, ln2_g, ln2_b, conv_b, f_b):
    cb = jnp.pad(conv_b, ((0, 0), (0, 6144 - 5504))).reshape(12, D_MODEL)
    fb = jnp.pad(f_b.reshape(1, FOX_HEADS), ((0, 3), (0, D_MODEL - FOX_HEADS)))
    return jnp.concatenate([pool_w.reshape(144, D_MODEL), ln1_g, ln1_b, ln2_g, ln2_b, cb, fb], axis=0)


def unpack_replicated(buf):
    pool_w = buf[:144].reshape(1, 4, POOL_GROUP, POOL_GROUP)
    ln = [buf[144 + 2 * k:146 + 2 * k] for k in range(4)]
    conv_b = buf[152:164].reshape(2, 6144)[:, :5504]
    f_b = buf[164, :FOX_HEADS]
    return pool_w, ln[0], ln[1], ln[2], ln[3], conv_b, f_b


def pack_small(conv_w, pool_scale):
    buf = jnp.zeros((16, FF_BLOCK_PAD), F32)
    buf = lax.dynamic_update_slice(buf, conv_w.reshape(DEPTH * 3, FF_BLOCK), (0, 0))
    return lax.dynamic_update_slice(buf, pool_scale, (8, 0))


def _block_diag(pw):
    out = jnp.zeros((TOK_WIDTH, TOK_WIDTH), pw.dtype)
    for g in range(4):
        out = lax.dynamic_update_slice(out, pw[g], (g * POOL_GROUP, g * POOL_GROUP))
    return out


def layer_shards(l, sq_a, sq_b, mem_w_kv, ffn_w_up, ffn_w_down):
    return [sq_a[0].astype(BF16), sq_b[0].astype(BF16), mem_w_kv[l].astype(BF16), ffn_w_up[l].astype(BF16),
            ffn_w_down[l].astype(BF16)]


def mixer_weights(l, gath, ln1_g, ln1_b, ln2_g, ln2_b):
    w_out = gath[1].reshape(D_MODEL, D_MODEL)
    wt = {"memw": gath[2].reshape(D_MODEL, 2 * MEM_WIDTH), "wout": w_out, "wout_t": w_out.T,
          "ln1_g": ln1_g[l:l + 1], "ln1_b": ln1_b[l:l + 1], "ln2_g": ln2_g[l:l + 1], "ln2_b": ln2_b[l:l + 1]}
    return wt, gath[0].reshape(D_MODEL, D_MODEL)


def ffn_weights(l, wup_g, wdown_g, small, conv_b):
    pad_c = FF_BLOCK_PAD - FF_BLOCK
    wup = jnp.pad(wup_g, ((0, 0), (0, 0), (0, pad_c))).transpose(1, 0, 2).reshape(D_MODEL, N_DEV * FF_BLOCK_PAD)
    wdown = jnp.pad(wdown_g.reshape(FF_PAIRS, FF_BLOCK, D_MODEL), ((0, 0), (0, pad_c), (0, 0)))
    wdown = wdown.reshape(FF_PAIRS * FF_BLOCK_PAD, D_MODEL)
    cb = jnp.pad(conv_b[l].reshape(N_DEV, FF_BLOCK), ((0, 0), (0, pad_c)))
    cw = jnp.concatenate([small[:, 3 * l:3 * l + 3, :], cb[:, None, :], jnp.zeros((N_DEV, 4, FF_BLOCK_PAD), F32)], axis=1)
    return {"wup": wup, "wup_t": wup.T, "wdown": wdown, "wdown_t": wdown.T, "cw": cw}


def mixer_grad_blocks(g, w_in_grad):
    blocks = [w_in_grad.reshape(N_DEV, 128, D_MODEL), g["wout"].reshape(N_DEV, 128, D_MODEL),
              g["memw"].reshape(N_DEV, 128, 2 * MEM_WIDTH)]
    return [b.astype(BF16) for b in blocks]


def ffn_grad_blocks(g):
    wup = g["wup"][:, :, :FF_BLOCK]
    wdown = g["wdown"].reshape(FF_PAIRS, FF_BLOCK_PAD, D_MODEL)[:, :FF_BLOCK].reshape(N_DEV, FF_ROWS, D_MODEL)
    return [wup.astype(BF16), wdown.astype(BF16)]


def small_grad_blocks(g0, g1):
    taps = jnp.stack([g0["cw"][:, :3, :], g1["cw"][:, :3, :]], axis=1).reshape(N_DEV, DEPTH * 3, FF_BLOCK_PAD)
    small = jnp.zeros((N_DEV, 16, FF_BLOCK_PAD), F32)
    small = lax.dynamic_update_slice(small, taps, (0, 0, 0))
    return lax.dynamic_update_slice(small, g0["pscale"].reshape(N_DEV, 1, 96), (0, 8, 0))


def replicated_grads(g0, g1):
    pw = jnp.stack([g0["pw_full"][k * POOL_GROUP:(k + 1) * POOL_GROUP, k * POOL_GROUP:(k + 1) * POOL_GROUP] for k in range(4)])
    conv_b = jnp.stack([g_["cw"][:, 3, :FF_BLOCK].reshape(N_DEV * FF_BLOCK) for g_ in (g0, g1)])
    ln = [jnp.concatenate([g0[n], g1[n]], axis=0) for n in ("ln1_g", "ln1_b", "ln2_g", "ln2_b")]
    return pack_replicated(pw[None], ln[0], ln[1], ln[2], ln[3], conv_b, g1["fb"][0, :FOX_HEADS])


def kernel(x, mem, a_w_in, a_pool_w, a_pool_scale, a_w_out, b_w_q, b_w_out, kv_w, f_b, mem_w_kv, ln1_g, ln1_b, ln2_g, ln2_b, ffn_w_up, ffn_conv_w, ffn_conv_b, ffn_w_down, loss_target, m_a_w_in, m_a_pool_w, m_a_pool_scale, m_a_w_out, m_b_w_q, m_b_w_out, m_kv_w, m_f_b, m_mem_w_kv, m_ln1_g, m_ln1_b, m_ln2_g, m_ln2_b, m_ffn_w_up, m_ffn_conv_w, m_ffn_conv_b, m_ffn_w_down, v_a_w_in, v_a_pool_w, v_a_pool_scale, v_a_w_out, v_b_w_q, v_b_w_out, v_kv_w, v_f_b, v_mem_w_kv, v_ln1_g, v_ln1_b, v_ln2_g, v_ln2_b, v_ffn_w_up, v_ffn_conv_w, v_ffn_conv_b, v_ffn_w_down):
    nb, s, d = x.shape
    t = nb * s
    x2d, mem_bf, target = x.reshape(t, d), mem.reshape(nb * MEM_LEN, d).astype(BF16), loss_target.reshape(t, d)

    shards0 = layer_shards(0, a_w_in, a_w_out, mem_w_kv, ffn_w_up, ffn_w_down)
    shards1 = layer_shards(1, b_w_q, b_w_out, mem_w_kv, ffn_w_up, ffn_w_down)
    shards1.append(jnp.pad(kv_w, ((0, 0), (0, KV_COLS_PAD - KV_COLS))).astype(BF16))
    gath0 = exchange(shards0[:3] + [pack_small(ffn_conv_w, a_pool_scale)], [False] * 4, "gather_w0_mixer")
    pending = {"ffn0": exchange_start(shards0[3:], [False] * 2, gath0[0], "gather_w0_ffn_start")}
    small = gath0[3]
    wt0, w_in = mixer_weights(0, gath0, ln1_g + pending["ffn0"]["token"], ln1_b, ln2_g, ln2_b)
    pw_bd = _block_diag(a_pool_w[0])
    wt0.update(win_a=w_in, win_a_t=w_in.T, pw_bd=pw_bd.astype(BF16), pw_bd_t=pw_bd.T.astype(BF16),
               pscale=small[:, 8, :96].reshape(1, TOK_WIDTH) + pending["ffn0"]["token"])

    def ffn0_weights(x1_bf):
        got = exchange_wait(pending["ffn0"], x1_bf, "gather_w0_ffn_wait")
        pending["w1"] = exchange_start(shards1, [False] * 6, got[0], "gather_w1_start")
        w = ffn_weights(0, got[0], got[1], small, ffn_conv_b)
        w["cw"] = w["cw"] + pending["w1"]["token"]
        return w

    x1, x1_bf, sv0 = forward_layer(0, x2d, x2d.astype(BF16), mem_bf, wt0, nb, s, ffn_weights=ffn0_weights)
    gath1 = exchange_wait(pending["w1"], x1_bf, "gather_w1_wait")
    wt1, w_q = mixer_weights(1, gath1, ln1_g, ln1_b, ln2_g, ln2_b)
    wt1.update(ffn_weights(1, gath1[3], gath1[4], small, ffn_conv_b))
    kvw = gath1[5].reshape(D_MODEL, KV_COLS_PAD)
    wt1.update(wq=w_q, wq_t=w_q.T, kvw=kvw, kvw_t=kvw.T,
               fb=jnp.pad(f_b.reshape(1, FOX_HEADS), ((0, 0), (0, LANES - FOX_HEADS))))
    y, _, sv1 = forward_layer(1, x1, x1_bf, mem_bf, wt1, nb, s)
    dy, loss_row = loss_head(y, target)
    loss = lax.psum(loss_row[0, 0], ("x", "y", "c"))

    dx1, g1 = backward_layer(1, dy, sv1, mem_bf, wt1, nb, s)
    blocks1 = (mixer_grad_blocks(g1, g1["wq"]) + ffn_grad_blocks(g1)
               + [g1["kvw"][:, :KV_COLS].reshape(N_DEV, 128, KV_COLS).astype(BF16)])
    pending["g1"] = exchange_start(blocks1, [True] * 6, dx1, "scatter_g1_start")
    wt0["ln2_g"] = wt0["ln2_g"] + pending["g1"]["token"]

    def after_ffn0(g, dxm):
        pending["gf0"] = exchange_start(ffn_grad_blocks(g), [True] * 2, dxm, "scatter_g0_ffn_start")
        return pending["gf0"]["token"]

    grad_x, g0 = backward_layer(0, dx1, sv0, mem_bf, wt0, nb, s, after_ffn=after_ffn0)
    last = mixer_grad_blocks(g0, g0["win_a"]) + [small_grad_blocks(g0, g1), replicated_grads(g0, g1)]
    parts_m0 = exchange(last, [True] * 4 + [False], "scatter_g0_mixer")
    parts_f0 = exchange_wait(pending["gf0"], parts_m0[0], "scatter_g0_ffn_wait")
    parts1 = exchange_wait(pending["g1"], parts_f0[0], "scatter_g1_wait")
    parts0 = list(parts_m0[:3]) + list(parts_f0) + list(parts_m0[3:])

    res = {}

    def upd(nm, parts, w2, m2, v2):
        res[nm] = reduce_adamw(parts, w2, m2, v2, f"adamw_{nm}")

    upd("a_w_in", parts0[0], a_w_in[0], m_a_w_in[0], v_a_w_in[0])
    upd("a_w_out", parts0[1], a_w_out[0], m_a_w_out[0], v_a_w_out[0])
    upd("b_w_q", parts1[0], b_w_q[0], m_b_w_q[0], v_b_w_q[0])
    upd("b_w_out", parts1[1], b_w_out[0], m_b_w_out[0], v_b_w_out[0])
    upd("kv_w", parts1[5], kv_w, m_kv_w, v_kv_w)
    for l, parts in enumerate((parts0, parts1)):
        upd(f"mem_w_kv{l}", parts[2], mem_w_kv[l], m_mem_w_kv[l], v_mem_w_kv[l])
        upd(f"ffn_w_up{l}", parts[3], ffn_w_up[l], m_ffn_w_up[l], v_ffn_w_up[l])
        upd(f"ffn_w_down{l}", parts[4], ffn_w_down[l], m_ffn_w_down[l], v_ffn_w_down[l])
    upd("small", parts0[5], pack_small(ffn_conv_w, a_pool_scale), pack_small(m_ffn_conv_w, m_a_pool_scale),
        pack_small(v_ffn_conv_w, v_a_pool_scale))
    upd("replicated", parts0[6], pack_replicated(a_pool_w, ln1_g, ln1_b, ln2_g, ln2_b, ffn_conv_b, f_b),
        pack_replicated(m_a_pool_w, m_ln1_g, m_ln1_b, m_ln2_g, m_ln2_b, m_ffn_conv_b, m_f_b),
        pack_replicated(v_a_pool_w, v_ln1_g, v_ln1_b, v_ln2_g, v_ln2_b, v_ffn_conv_b, v_f_b))

    for nm in ("a_w_in", "a_w_out", "b_w_q", "b_w_out"):
        res[nm] = [o[None] for o in res[nm]]
    for nm in ("mem_w_kv", "ffn_w_up", "ffn_w_down"):
        res[nm] = [jnp.stack([a0, a1]) for a0, a1 in zip(res[nm + "0"], res[nm + "1"])]
    res["ffn_conv_w"] = [o[:DEPTH * 3, :FF_BLOCK].reshape(DEPTH, 3, FF_BLOCK) for o in res["small"]]
    res["a_pool_scale"] = [o[8:9, :96] for o in res["small"]]
    rep_names = ["a_pool_w", "ln1_g", "ln1_b", "ln2_g", "ln2_b", "ffn_conv_b", "f_b"]
    for nm in rep_names:
        res[nm] = []
    for o in res["replicated"]:
        for nm, val in zip(rep_names, unpack_replicated(o)):
            res[nm].append(val)

    order = ["a_w_in", "a_pool_w", "a_pool_scale", "a_w_out", "b_w_q", "b_w_out", "kv_w", "f_b", "mem_w_kv",
             "ln1_g", "ln1_b", "ln2_g", "ln2_b", "ffn_w_up", "ffn_conv_w", "ffn_conv_b", "ffn_w_down"]
    out = [loss, grad_x.reshape(nb, s, d)]
    for kind in range(4):
        out.extend(res[nm][kind] for nm in order)
    return tuple(out)
```

```python
import jax
import jax.numpy as jnp
from jax import lax
from jax.experimental import pallas as pl
from jax.experimental.pallas import tpu as pltpu

F32 = jnp.float32
BF16 = jnp.bfloat16
SDS = jax.ShapeDtypeStruct

N_DEV = 8
D_MODEL = 1024
TOK_WIDTH = 768
MEM_WIDTH = 256
MEM_LEN = 256
MEM_HEADS = 4
HEAD_DIM = 64
FOX_HEADS = 12
POOL_GROUP = 192
FF_BLOCK = 688
FF_BLOCK_PAD = 768
FF_PAIRS = 4
FF_ROWS = 344
KV_COLS = 1548
KV_COLS_PAD = 1664
LANES = 128
DEPTH = 2
DN_ALPHA = (2.0 * DEPTH) ** 0.25
LN_EPS = 1e-5
QK_SCALE = HEAD_DIM ** -0.5
NEG_BIG = -1e30

ADAM_LR = 0.001
ADAM_B1 = 0.9
ADAM_B2 = 0.999
ADAM_EPS = 1e-08
ADAM_WD = 0.01
ADAM_STEP = 10

VMEM_LIMIT_BYTES = 56 * 1024 * 1024
MM_BLOCK_BYTES = 6 * 1024 * 1024
TM = 512
TS = 256
TF = 256
TC = 256
HALO_POOL = 16
HALO_CONV = 8

NT_DIMS = (((1,), (1,)), ((), ()))
TN_DIMS = (((0,), (0,)), ((), ()))


def _params(sem=None):
    return pltpu.CompilerParams(dimension_semantics=sem, vmem_limit_bytes=VMEM_LIMIT_BYTES)


def _sigmoid(z):
    return 1.0 / (1.0 + jnp.exp(-z))


def _pick_tn(n):
    if n <= 2048:
        return n
    for t in (1024, 768, 512, 256, 128):
        if n % t == 0:
            return t
    return n


def mm_nn(a, b, out_dtype, name, addend=None, add_scale=1.0, also_bf16=False):
    m, k = a.shape
    _, n = b.shape
    tm = min(TM, m)
    tn = n
    while k * tn * 2 > MM_BLOCK_BYTES or tm * tn * 4 > MM_BLOCK_BYTES:
        tn //= 2
    chunk = tn if tn <= 2048 else _pick_tn(tn)
    has_add = addend is not None

    def body(*refs):
        a_ref, b_ref = refs[0], refs[1]
        c_ref = refs[2] if has_add else None
        o_ref = refs[3] if has_add else refs[2]
        ob_ref = refs[-1] if also_bf16 else None
        av = a_ref[...].astype(BF16)
        for c in range(tn // chunk):
            cols = slice(c * chunk, (c + 1) * chunk)
            r = jnp.dot(av, b_ref[:, cols].astype(BF16), preferred_element_type=F32)
            if has_add:
                r = r + add_scale * c_ref[:, cols]
            o_ref[:, cols] = r.astype(out_dtype)
            if also_bf16:
                ob_ref[:, cols] = r.astype(BF16)

    in_specs = [pl.BlockSpec((tm, k), lambda j, i: (i, 0)), pl.BlockSpec((k, tn), lambda j, i: (0, j))]
    ops = [a, b]
    tile = pl.BlockSpec((tm, tn), lambda j, i: (i, j))
    if has_add:
        in_specs.append(tile)
        ops.append(addend)
    out_shape = [SDS((m, n), out_dtype)]
    out_specs = [tile]
    if also_bf16:
        out_shape.append(SDS((m, n), BF16))
        out_specs.append(tile)
    res = pl.pallas_call(
        body, name=name, grid=(n // tn, m // tm), in_specs=in_specs, out_specs=out_specs, out_shape=out_shape,
        compiler_params=_params(("parallel", "parallel")))(*ops)
    return tuple(res) if also_bf16 else res[0]


def mm_tn(a, b, name, blocked=False):
    t, m = a.shape
    _, n = b.shape
    tt = min(2 * TM, t)
    tm = 1024 if m % 1024 == 0 else m
    tn = FF_BLOCK_PAD if blocked else _pick_tn(n)
    nt = t // tt

    def body(a_ref, b_ref, o_ref):
        kk = pl.program_id(2)
        r = lax.dot_general(a_ref[...].astype(BF16), b_ref[...].astype(BF16), TN_DIMS, preferred_element_type=F32)
        if blocked:
            r = r[None]

        @pl.when(kk == 0)
        def _():
            o_ref[...] = r

        @pl.when(kk != 0)
        def _():
            o_ref[...] += r

    if blocked:
        out_shape = SDS((n // tn, m, tn), F32)
        out_spec = pl.BlockSpec((1, tm, tn), lambda i, j, kk: (j, i, 0))
    else:
        out_shape = SDS((m, n), F32)
        out_spec = pl.BlockSpec((tm, tn), lambda i, j, kk: (i, j))
    return pl.pallas_call(
        body, name=name, grid=(m // tm, n // tn, nt),
        in_specs=[pl.BlockSpec((tt, tm), lambda i, j, kk: (kk, i)), pl.BlockSpec((tt, tn), lambda i, j, kk: (kk, j))],
        out_specs=out_spec, out_shape=out_shape,
        compiler_params=_params(("parallel", "parallel", "arbitrary")))(a, b)


def ln_fwd(xprev, delta, g, b, name):
    t, d = xprev.shape
    tm = min(TM, t)

    def body(xp_ref, dl_ref, g_ref, b_ref, y_ref, yb_ref, xh_ref, rs_ref):
        r = DN_ALPHA * xp_ref[...] + dl_ref[...]
        mu = jnp.mean(r, axis=1, keepdims=True)
        xc = r - mu
        var = jnp.mean(xc * xc, axis=1, keepdims=True)
        rstd = lax.rsqrt(var + LN_EPS)
        xh = xc * rstd
        y = xh * g_ref[...] + b_ref[...]
        y_ref[...] = y
        yb_ref[...] = y.astype(BF16)
        xh_ref[...] = xh
        rs_ref[...] = jnp.broadcast_to(rstd, (tm, LANES))

    row = pl.BlockSpec((tm, d), lambda i: (i, 0))
    vec = pl.BlockSpec((1, d), lambda i: (0, 0))
    return pl.pallas_call(
        body, name=name, grid=(t // tm,), in_specs=[row, row, vec, vec],
        out_specs=[row, row, row, pl.BlockSpec((tm, LANES), lambda i: (i, 0))],
        out_shape=[SDS((t, d), F32), SDS((t, d), BF16), SDS((t, d), F32), SDS((t, LANES), F32)],
        compiler_params=_params(("parallel",)))(xprev, delta, g, b)


def ln_bwd(dy, xhat, rstd, g, name):
    t, d = dy.shape
    tm = min(TM, t)

    def body(dy_ref, xh_ref, rs_ref, g_ref, dr_ref, drb_ref, dg_ref, db_ref):
        i = pl.program_id(0)
        dyv = dy_ref[...]
        xh = xh_ref[...]
        dxh = dyv * g_ref[...]
        m1 = jnp.mean(dxh, axis=1, keepdims=True)
        m2 = jnp.mean(dxh * xh, axis=1, keepdims=True)
        dr = rs_ref[:, 0:1] * (dxh - m1 - xh * m2)
        dr_ref[...] = dr
        drb_ref[...] = dr.astype(BF16)

        @pl.when(i == 0)
        def _():
            dg_ref[...] = jnp.zeros_like(dg_ref)
            db_ref[...] = jnp.zeros_like(db_ref)

        dg_ref[...] += jnp.sum(dyv * xh, axis=0, keepdims=True)
        db_ref[...] += jnp.sum(dyv, axis=0, keepdims=True)

    row = pl.BlockSpec((tm, d), lambda i: (i, 0))
    vec = pl.BlockSpec((1, d), lambda i: (0, 0))
    return pl.pallas_call(
        body, name=name, grid=(t // tm,),
        in_specs=[row, row, pl.BlockSpec((tm, LANES), lambda i: (i, 0)), vec],
        out_specs=[row, row, vec, vec],
        out_shape=[SDS((t, d), F32), SDS((t, d), BF16), SDS((1, d), F32), SDS((1, d), F32)],
        compiler_params=_params(("arbitrary",)))(dy, xhat, rstd, g)


def loss_head(y, target):
    t, d = y.shape
    tm = min(TM, t)
    nsteps = t // tm

    def body(y_ref, t_ref, dy_ref, l_ref, acc):
        i = pl.program_id(0)
        diff = y_ref[...] - t_ref[...]
        dy_ref[...] = diff * (1.0 / d)

        @pl.when(i == 0)
        def _():
            acc[...] = jnp.zeros_like(acc)

        acc[...] += jnp.sum(diff * diff, axis=0, keepdims=True)

        @pl.when(i == nsteps - 1)
        def _():
            tot = jnp.sum(acc[...], axis=1, keepdims=True) * (0.5 / d)
            l_ref[...] = jnp.broadcast_to(tot, (1, LANES))

    row = pl.BlockSpec((tm, d), lambda i: (i, 0))
    return pl.pallas_call(
        body, name="loss_head", grid=(nsteps,), in_specs=[row, row],
        out_specs=[row, pl.BlockSpec((1, LANES), lambda i: (0, 0))],
        out_shape=[SDS((t, d), F32), SDS((1, LANES), F32)],
        scratch_shapes=[pltpu.VMEM((1, d), F32)],
        compiler_params=_params(("arbitrary",)))(y, target)


def memattn_fwd(proj, memkv, nb, s, name):
    ts = min(TS, s)
    nq = s // ts

    def body(q_ref, kv_ref, o_ref):
        for h in range(MEM_HEADS):
            lo, hi = h * HEAD_DIM, (h + 1) * HEAD_DIM
            qh = q_ref[:, lo:hi].astype(BF16)
            kh = kv_ref[:, lo:hi]
            vh = kv_ref[:, MEM_WIDTH + lo:MEM_WIDTH + hi]
            sc = lax.dot_general(qh, kh, NT_DIMS, preferred_element_type=F32) * QK_SCALE
            p = jnp.exp(sc - jnp.max(sc, axis=1, keepdims=True))
            p = p / jnp.sum(p, axis=1, keepdims=True)
            o_ref[:, lo:hi] = jnp.dot(p.astype(BF16), vh, preferred_element_type=F32).astype(BF16)

    return pl.pallas_call(
        body, name=name, grid=(nb, nq),
        in_specs=[pl.BlockSpec((ts, MEM_WIDTH), lambda b, i: (b * nq + i, 3)),
                  pl.BlockSpec((MEM_LEN, 2 * MEM_WIDTH), lambda b, i: (b, 0))],
        out_specs=pl.BlockSpec((ts, MEM_WIDTH), lambda b, i: (b * nq + i, 0)),
        out_shape=SDS((nb * s, MEM_WIDTH), BF16),
        compiler_params=_params(("parallel", "parallel")))(proj, memkv)


def memattn_bwd(proj, memkv, dcat, nb, s, name):
    ts = min(TS, s)
    nq = s // ts

    def body(q_ref, kv_ref, do_ref, dq_ref, dkv_ref):
        i = pl.program_id(1)

        @pl.when(i == 0)
        def _():
            dkv_ref[...] = jnp.zeros_like(dkv_ref)

        for h in range(MEM_HEADS):
            lo, hi = h * HEAD_DIM, (h + 1) * HEAD_DIM
            qh = q_ref[:, lo:hi].astype(BF16)
            kh = kv_ref[:, lo:hi]
            vh = kv_ref[:, MEM_WIDTH + lo:MEM_WIDTH + hi]
            doh = do_ref[:, lo:hi].astype(BF16)
            sc = lax.dot_general(qh, kh, NT_DIMS, preferred_element_type=F32) * QK_SCALE
            p = jnp.exp(sc - jnp.max(sc, axis=1, keepdims=True))
            p = p / jnp.sum(p, axis=1, keepdims=True)
            dv = lax.dot_general(p.astype(BF16), doh, TN_DIMS, preferred_element_type=F32)
            dp = lax.dot_general(doh, vh, NT_DIMS, preferred_element_type=F32)
            dl = jnp.sum(p * dp, axis=1, keepdims=True)
            ds = (p * (dp - dl) * QK_SCALE).astype(BF16)
            dq_ref[:, lo:hi] = jnp.dot(ds, kh, preferred_element_type=F32).astype(BF16)
            dkv_ref[:, lo:hi] += lax.dot_general(ds, qh, TN_DIMS, preferred_element_type=F32)
            dkv_ref[:, MEM_WIDTH + lo:MEM_WIDTH + hi] += dv

    return pl.pallas_call(
        body, name=name, grid=(nb, nq),
        in_specs=[pl.BlockSpec((ts, MEM_WIDTH), lambda b, i: (b * nq + i, 3)),
                  pl.BlockSpec((MEM_LEN, 2 * MEM_WIDTH), lambda b, i: (b, 0)),
                  pl.BlockSpec((ts, MEM_WIDTH), lambda b, i: (b * nq + i, 3))],
        out_specs=[pl.BlockSpec((ts, MEM_WIDTH), lambda b, i: (b * nq + i, 0)),
                   pl.BlockSpec((MEM_LEN, 2 * MEM_WIDTH), lambda b, i: (b, 0))],
        out_shape=[SDS((nb * s, MEM_WIDTH), BF16), SDS((nb * MEM_LEN, 2 * MEM_WIDTH), F32)],
        compiler_params=_params(("parallel", "arbitrary")))(proj, memkv, dcat)


def _pool_select(shape, s2, s4, s8, s16):
    lane = lax.broadcasted_iota(jnp.int32, shape, 1)
    return jnp.where(lane < POOL_GROUP, s2, jnp.where(lane < 2 * POOL_GROUP, s4, jnp.where(lane < 3 * POOL_GROUP, s8, s16)))


def _pool_count(shape, first_pos):
    pos = first_pos + lax.broadcasted_iota(jnp.int32, shape, 0)
    win = _pool_select(shape, 2, 4, 8, 16)
    return jnp.minimum(pos + 1, win).astype(F32)


def pool_fwd(proj, pw_bd, pscale, nb, s):
    ts = min(TS, s)
    nq = s // ts
    w = TOK_WIDTH

    def body(c_ref, h_ref, w_ref, sc_ref, pooled_ref, tok_ref):
        i = pl.program_id(0) % nq
        cur = c_ref[...]
        halo = jnp.where(i == 0, 0.0, h_ref[...])
        xe = jnp.concatenate([halo, cur], axis=0)
        s2 = xe + pltpu.roll(xe, 1, axis=0)
        s4 = s2 + pltpu.roll(s2, 2, axis=0)
        s8 = s4 + pltpu.roll(s4, 4, axis=0)
        s16 = s8 + pltpu.roll(s8, 8, axis=0)
        hp = HALO_POOL
        ws = _pool_select((ts, w), s2[hp:], s4[hp:], s8[hp:], s16[hp:])
        pooled = (ws / _pool_count((ts, w), i * ts) - cur).astype(BF16)
        pooled_ref[...] = pooled
        mixed = jnp.dot(pooled, w_ref[...], preferred_element_type=F32)
        tok_ref[...] = (mixed * sc_ref[...]).astype(BF16)

    row = pl.BlockSpec((ts, w), lambda r: (r, 0))
    return pl.pallas_call(
        body, name="pool_fwd", grid=(nb * nq,),
        in_specs=[row, pl.BlockSpec((HALO_POOL, w), lambda r: (jnp.maximum(r * (ts // HALO_POOL) - 1, 0), 0)),
                  pl.BlockSpec((w, w), lambda r: (0, 0)), pl.BlockSpec((1, w), lambda r: (0, 0))],
        out_specs=[row, row], out_shape=[SDS((nb * s, w), BF16), SDS((nb * s, w), BF16)],
        compiler_params=_params(("parallel",)))(proj, proj, pw_bd, pscale)


def pool_bwd_mix(dcat, pooled, pw_bd, pw_bd_t, pscale, nb, s):
    ts = min(TS, s)
    w = TOK_WIDTH

    def body(dt_ref, p_ref, w_ref, wt_ref, sc_ref, dm_ref, dp_ref, ds_ref):
        r = pl.program_id(0)
        dtok = dt_ref[...]
        mixed = jnp.dot(p_ref[...], w_ref[...], preferred_element_type=F32)

        @pl.when(r == 0)
        def _():
            ds_ref[...] = jnp.zeros_like(ds_ref)

        ds_ref[...] += jnp.sum(dtok * mixed, axis=0, keepdims=True)
        dmx = (dtok * sc_ref[...]).astype(BF16)
        dm_ref[...] = dmx
        dp_ref[...] = jnp.dot(dmx, wt_ref[...], preferred_element_type=F32)

    row = pl.BlockSpec((ts, w), lambda r: (r, 0))
    mat = pl.BlockSpec((w, w), lambda r: (0, 0))
    vec = pl.BlockSpec((1, w), lambda r: (0, 0))
    return pl.pallas_call(
        body, name="pool_bwd_mix", grid=(nb * s // ts,), in_specs=[row, row, mat, mat, vec],
        out_specs=[row, row, vec], out_shape=[SDS((nb * s, w), BF16), SDS((nb * s, w), F32), SDS((1, w), F32)],
        compiler_params=_params(("arbitrary",)))(dcat, pooled, pw_bd, pw_bd_t, pscale)


def pool_bwd_window(dpooled, nb, s):
    ts = min(TS, s)
    nq = s // ts
    w = TOK_WIDTH
    n_ext = ts + HALO_POOL
    n_halo_blocks = nb * s // HALO_POOL

    def body(c_ref, n_ref, du_ref):
        i = pl.program_id(0) % nq
        cur = c_ref[...]
        nxt = jnp.where(i == nq - 1, 0.0, n_ref[...])
        ze = jnp.concatenate([cur, nxt], axis=0) / _pool_count((n_ext, w), i * ts)
        s2 = ze + pltpu.roll(ze, n_ext - 1, axis=0)
        s4 = s2 + pltpu.roll(s2, n_ext - 2, axis=0)
        s8 = s4 + pltpu.roll(s4, n_ext - 4, axis=0)
        s16 = s8 + pltpu.roll(s8, n_ext - 8, axis=0)
        ws = _pool_select((ts, w), s2[:ts], s4[:ts], s8[:ts], s16[:ts])
        du_ref[...] = (ws - cur).astype(BF16)

    row = pl.BlockSpec((ts, w), lambda r: (r, 0))
    return pl.pallas_call(
        body, name="pool_bwd_window", grid=(nb * nq,),
        in_specs=[row, pl.BlockSpec((HALO_POOL, w),
                                    lambda r: (jnp.minimum((r + 1) * (ts // HALO_POOL), n_halo_blocks - 1), 0))],
        out_specs=row, out_shape=SDS((nb * s, w), BF16),
        compiler_params=_params(("parallel",)))(dpooled, dpooled)


def _conv_rows(xe, w_ref):
    return (w_ref[0, 2:3, :] * xe + w_ref[0, 1:2, :] * pltpu.roll(xe, 1, axis=0)
            + w_ref[0, 0:1, :] * pltpu.roll(xe, 2, axis=0) + w_ref[0, 3:4, :])


def convgate_fwd(h, cw, nb, s, name):
    ts = min(TS, s)
    nq = s // ts
    w = FF_BLOCK_PAD
    hc = HALO_CONV

    def body(uc_ref, uh_ref, gc_ref, gh_ref, wu_ref, wg_ref, o_ref):
        first = (pl.program_id(0) % nq) == 0
        xu = jnp.concatenate([jnp.where(first, 0.0, uh_ref[...]), uc_ref[...]], axis=0)
        xg = jnp.concatenate([jnp.where(first, 0.0, gh_ref[...]), gc_ref[...]], axis=0)
        cu = _conv_rows(xu, wu_ref)[hc:]
        cg = _conv_rows(xg, wg_ref)[hc:]
        o_ref[...] = (cg * _sigmoid(cg) * cu).astype(BF16)

    def cur(off):
        return pl.BlockSpec((ts, w), lambda r, j: (r, j + off))

    def halo(off):
        return pl.BlockSpec((hc, w), lambda r, j: (jnp.maximum(r * (ts // hc) - 1, 0), j + off))

    def wspec(off):
        return pl.BlockSpec((1, 8, w), lambda r, j: (j + off, 0, 0))

    return pl.pallas_call(
        body, name=name, grid=(nb * nq, FF_PAIRS),
        in_specs=[cur(0), halo(0), cur(FF_PAIRS), halo(FF_PAIRS), wspec(0), wspec(FF_PAIRS)],
        out_specs=pl.BlockSpec((ts, w), lambda r, j: (r, j)), out_shape=SDS((nb * s, FF_PAIRS * w), BF16),
        compiler_params=_params(("parallel", "parallel")))(h, h, h, h, cw, cw)


def convgate_bwd(h, dact, cw, nb, s, name):
    ts = min(TS, s)
    nq = s // ts
    w = FF_BLOCK_PAD
    hc = HALO_CONV
    n_ext = ts + hc
    n_halo_blocks = nb * s // hc

    def body(uc_ref, up_ref, un_ref, gc_ref, gp_ref, gn_ref, dc_ref, dn_ref, wu_ref, wg_ref,
             dhu_ref, dhg_ref, dwu_ref, dwg_ref):
        r = pl.program_id(1)
        i = r % nq
        first = i == 0
        last = i == nq - 1
        xu = jnp.concatenate([jnp.where(first, 0.0, up_ref[...]), uc_ref[...], un_ref[...]], axis=0)
        xg = jnp.concatenate([jnp.where(first, 0.0, gp_ref[...]), gc_ref[...], gn_ref[...]], axis=0)
        cu = _conv_rows(xu, wu_ref)[hc:]
        cg = _conv_rows(xg, wg_ref)[hc:]
        da = jnp.concatenate([dc_ref[...].astype(F32), jnp.where(last, 0.0, dn_ref[...].astype(F32)[:hc])], axis=0)
        sg = _sigmoid(cg)
        dcu = da * (cg * sg)
        dcg = da * cu * (sg * (1.0 + cg * (1.0 - sg)))

        def conv_t(dcv, w_ref):
            return (w_ref[0, 2:3, :] * dcv + w_ref[0, 1:2, :] * pltpu.roll(dcv, n_ext - 1, axis=0)
                    + w_ref[0, 0:1, :] * pltpu.roll(dcv, n_ext - 2, axis=0))[:ts]

        dhu_ref[...] = conv_t(dcu, wu_ref).astype(BF16)
        dhg_ref[...] = conv_t(dcg, wg_ref).astype(BF16)

        def tap_grads(xe, dcv):
            d0 = dcv[:ts]
            x0 = xe[hc:hc + ts]
            x1 = pltpu.roll(xe, 1, axis=0)[hc:hc + ts]
            x2 = pltpu.roll(xe, 2, axis=0)[hc:hc + ts]
            rows = [jnp.sum(d0 * x2, axis=0, keepdims=True), jnp.sum(d0 * x1, axis=0, keepdims=True),
                    jnp.sum(d0 * x0, axis=0, keepdims=True), jnp.sum(d0, axis=0, keepdims=True)]
            sub = lax.broadcasted_iota(jnp.int32, (8, w), 0)
            upd = jnp.zeros((8, w), F32)
            for k, rv in enumerate(rows):
                upd = jnp.where(sub == k, rv, upd)
            return upd[None]

        @pl.when(r == 0)
        def _():
            dwu_ref[...] = jnp.zeros_like(dwu_ref)
            dwg_ref[...] = jnp.zeros_like(dwg_ref)

        dwu_ref[...] += tap_grads(xu, dcu)
        dwg_ref[...] += tap_grads(xg, dcg)

    def cur(off):
        return pl.BlockSpec((ts, w), lambda j, r: (r, j + off))

    def prev(off):
        return pl.BlockSpec((hc, w), lambda j, r: (jnp.maximum(r * (ts // hc) - 1, 0), j + off))

    def nxt(off):
        return pl.BlockSpec((hc, w), lambda j, r: (jnp.minimum((r + 1) * (ts // hc), n_halo_blocks - 1), j + off))

    def wspec(off):
        return pl.BlockSpec((1, 8, w), lambda j, r: (j + off, 0, 0))

    hb = 2 * hc
    dact_next = pl.BlockSpec((hb, w), lambda j, r: (jnp.minimum((r + 1) * (ts // hb), nb * s // hb - 1), j))

    p = FF_PAIRS
    dh_spec = pl.BlockSpec((ts, w), lambda j, r: (r, j))
    dw_spec = pl.BlockSpec((1, 8, w), lambda j, r: (j, 0, 0))
    return pl.pallas_call(
        body, name=name, grid=(p, nb * nq),
        in_specs=[cur(0), prev(0), nxt(0), cur(p), prev(p), nxt(p), cur(0), dact_next, wspec(0), wspec(p)],
        out_specs=[dh_spec, dh_spec, dw_spec, dw_spec],
        out_shape=[SDS((nb * s, p * w), BF16), SDS((nb * s, p * w), BF16), SDS((p, 8, w), F32), SDS((p, 8, w), F32)],
        compiler_params=_params(("parallel", "arbitrary")))(h, h, h, h, h, h, dact, dact, cw, cw)


def _tri(n, upper):
    r = lax.broadcasted_iota(jnp.int32, (n, n), 0)
    c = lax.broadcasted_iota(jnp.int32, (n, n), 1)
    return ((r <= c) if upper else (r >= c)).astype(F32)


def fgate_fwd(fl, fb, nb, s):
    tc = min(TC, s)
    nq = s // tc

    def body(fl_ref, fb_ref, f_ref, carry):
        @pl.when(pl.program_id(1) == 0)
        def _():
            carry[...] = jnp.zeros_like(carry)

        z = fl_ref[...] + fb_ref[...]
        logf = jnp.minimum(z, 0.0) - jnp.log(1.0 + jnp.exp(-jnp.abs(z)))
        f_ref[...] = jnp.dot(_tri(tc, False), logf, preferred_element_type=F32,
                             precision=lax.Precision.HIGHEST) + carry[...]
        carry[...] += jnp.sum(logf, axis=0, keepdims=True)

    row = pl.BlockSpec((tc, LANES), lambda b, i: (b * nq + i, 0))
    return pl.pallas_call(
        body, name="fgate_fwd", grid=(nb, nq), in_specs=[row, pl.BlockSpec((1, LANES), lambda b, i: (0, 0))],
        out_specs=row, out_shape=SDS((nb * s, LANES), F32), scratch_shapes=[pltpu.VMEM((1, LANES), F32)],
        compiler_params=_params(("arbitrary", "arbitrary")))(fl, fb)


def fgate_bwd(d_cum_q, d_cum_k, fl, fb, nb, s):
    tc = min(TC, s)
    nq = s // tc

    def body(dfq_ref, dfk_ref, fl_ref, fb_ref, dfl_ref, dfb_ref, carry):
        b = pl.program_id(0)
        i = pl.program_id(1)

        @pl.when(i == 0)
        def _():
            carry[...] = jnp.zeros_like(carry)

        @pl.when(jnp.logical_and(b == 0, i == 0))
        def _():
            dfb_ref[...] = jnp.zeros_like(dfb_ref)

        dfv = dfq_ref[...] + dfk_ref[...]
        dlog = jnp.dot(_tri(tc, True), dfv, preferred_element_type=F32,
                       precision=lax.Precision.HIGHEST) + carry[...]
        carry[...] += jnp.sum(dfv, axis=0, keepdims=True)
        z = fl_ref[...] + fb_ref[...]
        dfl = dlog / (1.0 + jnp.exp(z))
        dfl_ref[...] = dfl
        dfb_ref[...] += jnp.sum(dfl, axis=0, keepdims=True)

    row = pl.BlockSpec((tc, LANES), lambda b, i: (b * nq + nq - 1 - i, 0))
    vec = pl.BlockSpec((1, LANES), lambda b, i: (0, 0))
    return pl.pallas_call(
        body, name="fgate_bwd", grid=(nb, nq), in_specs=[row, row, row, vec], out_specs=[row, vec],
        out_shape=[SDS((nb * s, LANES), F32), SDS((1, LANES), F32)], scratch_shapes=[pltpu.VMEM((1, LANES), F32)],
        compiler_params=_params(("arbitrary", "arbitrary")))(d_cum_q, d_cum_k, fl, fb)


PAIR = 2 * HEAD_DIM
N_PAIRS = FOX_HEADS // 2


def _lane_put(shape, h, col):
    lane = lax.broadcasted_iota(jnp.int32, shape, 1)
    return jnp.where(lane == h, col, 0.0)


def _half_masks(rows):
    lane = lax.broadcasted_iota(jnp.int32, (rows, PAIR), 1)
    return lane < HEAD_DIM


def _split_pair(x, scale=None):
    if scale is not None:
        x = x * scale
    lo = _half_masks(x.shape[0])
    zero = jnp.zeros_like(x)
    return jnp.where(lo, x, zero), jnp.where(lo, zero, x)


def _to_tile_rows(a, nb, s, tf):
    return a.reshape(nb * s // tf, tf, LANES)[:, :, :16].transpose(0, 2, 1)


def _from_tile_rows(a):
    tiles, _, tf = a.shape
    return jnp.pad(a.transpose(0, 2, 1), ((0, 0), (0, 0), (0, LANES - 16))).reshape(tiles * tf, LANES)


BIAS_TERMS = 3
LOOKAHEAD = 4
FOLLOW_FWD = 1
LOOKAHEAD_BWD = 2
FOLLOW_BWD = 1


def _bias_lane(h):
    return HEAD_DIM if h % 2 == 0 else 0


def _placement():
    rows = jnp.arange(LANES)[:, None]
    cols = jnp.arange(FOX_HEADS * PAIR)[None, :]
    head, lane = cols // PAIR, cols % PAIR
    first = jnp.where(head % 2 == 0, HEAD_DIM, 0)
    term = lane - first
    hit = (term >= 0) & (term < BIAS_TERMS) & (rows == 16 * term + head)
    return hit.astype(BF16)


def fox_prep(kv, fneg, nb, s):
    tf = min(TF, s)
    w = TOK_WIDTH

    def body(k_ref, v_ref, f_ref, pl_ref, ka_ref, vt_ref):
        lane = lax.broadcasted_iota(jnp.int32, (tf, LANES), 1)
        lo = lane < HEAD_DIM
        f = jnp.where(lane < FOX_HEADS, f_ref[...], 0.0)
        hi = f.astype(BF16).astype(F32)
        mid = (f - hi).astype(BF16).astype(F32)
        low = (f - hi - mid).astype(BF16).astype(F32)
        terms = (hi + pltpu.roll(mid, 16, axis=1) + pltpu.roll(low, 32, axis=1)).astype(BF16)
        placed = jnp.dot(terms, pl_ref[...], preferred_element_type=F32).astype(BF16)
        one = jnp.ones((tf, LANES), BF16)
        zero = jnp.zeros((tf, LANES), BF16)
        for p in range(N_PAIRS):
            kp = k_ref[:, p * PAIR:(p + 1) * PAIR] * QK_SCALE
            vp = v_ref[:, p * PAIR:(p + 1) * PAIR]
            he, ho = 2 * p, 2 * p + 1
            ka_ref[:, he * PAIR:(he + 1) * PAIR] = jnp.where(lo, kp, placed[:, he * PAIR:(he + 1) * PAIR])
            ka_ref[:, ho * PAIR:(ho + 1) * PAIR] = jnp.where(lo, placed[:, ho * PAIR:(ho + 1) * PAIR], kp)
            ve = jnp.where(lo, vp, jnp.where(lane == HEAD_DIM, one, zero))
            vo = jnp.where(lo, jnp.where(lane == 0, one, zero), vp)
            vt_ref[0, he * PAIR:(he + 1) * PAIR, :] = ve.astype(F32).T.astype(BF16)
            vt_ref[0, ho * PAIR:(ho + 1) * PAIR, :] = vo.astype(F32).T.astype(BF16)

    return pl.pallas_call(
        body, name="fox_prep", grid=(nb * s // tf,),
        in_specs=[pl.BlockSpec((tf, w), lambda r: (r, 0)), pl.BlockSpec((tf, w), lambda r: (r, 1)),
                  pl.BlockSpec((tf, LANES), lambda r: (r, 0)), pl.BlockSpec((LANES, FOX_HEADS * PAIR), lambda r: (0, 0))],
        out_specs=[pl.BlockSpec((tf, FOX_HEADS * PAIR), lambda r: (r, 0)),
                   pl.BlockSpec((1, FOX_HEADS * PAIR, tf), lambda r: (r, 0, 0))],
        out_shape=[SDS((nb * s, FOX_HEADS * PAIR), BF16), SDS((nb * s // tf, FOX_HEADS * PAIR, tf), BF16)],
        compiler_params=_params(("parallel",)))(kv, kv, fneg, _placement())


def fox_fwd_t(pq, kaug, vaug_t, nb, s):
    tf = min(TF, s)
    n = s // tf
    w = TOK_WIDTH
    wa = FOX_HEADS * PAIR

    def body(q_ref, k_hbm, vt_hbm, ob_ref, of_ref, lse_ref, k_vm, vt_vm, qx_scr, m_scr, acc_scr, sems):
        b = pl.program_id(0)
        i = pl.program_id(1)

        @pl.when(i == 0)
        def _():
            ck = pltpu.make_async_copy(k_hbm.at[pl.ds(pl.multiple_of(b * s, tf), s)], k_vm, sems.at[0])
            cv = pltpu.make_async_copy(vt_hbm.at[pl.ds(b * n, n)], vt_vm, sems.at[1])
            ck.start()
            cv.start()
            ck.wait()
            cv.wait()

        lane = lax.broadcasted_iota(jnp.int32, (tf, PAIR), 1)
        one = jnp.ones((tf, PAIR), BF16)
        zero = jnp.zeros((tf, PAIR), BF16)
        for p in range(N_PAIRS):
            qp = q_ref[:, p * PAIR:(p + 1) * PAIR]
            be, bo = _bias_lane(2 * p), _bias_lane(2 * p + 1)
            ones_e = jnp.where((lane >= be) & (lane < be + BIAS_TERMS), one, zero)
            ones_o = jnp.where((lane >= bo) & (lane < bo + BIAS_TERMS), one, zero)
            qx_scr[2 * p] = jnp.where(lane < HEAD_DIM, qp, ones_e)
            qx_scr[2 * p + 1] = jnp.where(lane < HEAD_DIM, ones_o, qp)
        m_scr[...] = jnp.full(m_scr.shape, NEG_BIG, F32)
        acc_scr[...] = jnp.zeros_like(acc_scr)

        def tile(j, masked):
            ks = pl.multiple_of(j * tf, tf)
            if masked:
                keep = lax.broadcasted_iota(jnp.int32, (tf, tf), 1) >= lax.broadcasted_iota(jnp.int32, (tf, tf), 0)
            def scores(h):
                kx = k_vm[pl.ds(ks, tf), h * PAIR:(h + 1) * PAIR]
                return lax.dot_general(kx, qx_scr[h], NT_DIMS, preferred_element_type=F32)

            def values(h, pr, a):
                pv = jnp.dot(vt_vm[j, h * PAIR:(h + 1) * PAIR, :], pr, preferred_element_type=F32)
                acc_scr[h] = a * acc_scr[h] + pv

            ahead = [scores(h) for h in range(LOOKAHEAD)]
            behind = []
            for h in range(FOX_HEADS):
                sc = ahead.pop(0)
                if h + LOOKAHEAD < FOX_HEADS:
                    ahead.append(scores(h + LOOKAHEAD))
                if masked:
                    sc = jnp.where(keep, sc, NEG_BIG)
                m_prev = m_scr[h]
                m_new = jnp.maximum(m_prev, jnp.max(sc, axis=0, keepdims=True))
                m_scr[h] = m_new
                behind.append((h, jnp.exp(sc - m_new).astype(BF16), jnp.exp(m_prev - m_new)))
                if len(behind) > FOLLOW_FWD:
                    values(*behind.pop(0))
            for item in behind:
                values(*item)

        def step(j, carry):
            tile(j, False)
            return carry

        lax.fori_loop(0, i, step, 0)
        tile(i, True)

        top = lax.broadcasted_iota(jnp.int32, (PAIR, tf), 0) < HEAD_DIM
        sub = lax.broadcasted_iota(jnp.int32, (16, tf), 0)
        lse = jnp.zeros((16, tf), F32)
        for p in range(N_PAIRS):
            he, ho = 2 * p, 2 * p + 1
            le = acc_scr[he, HEAD_DIM:HEAD_DIM + 1, :]
            lod = acc_scr[ho, 0:1, :]
            o = jnp.where(top, acc_scr[he] / le, acc_scr[ho] / lod).T
            ob_ref[:, p * PAIR:(p + 1) * PAIR] = o.astype(BF16)
            of_ref[:, p * PAIR:(p + 1) * PAIR] = o
            lse = jnp.where(sub == he, m_scr[he] + jnp.log(le), lse)
            lse = jnp.where(sub == ho, m_scr[ho] + jnp.log(lod), lse)
        lse_ref[0] = lse

    qrow = lambda b, i: (b * n + i, 0)
    return pl.pallas_call(
        body, name="fox_fwd", grid=(nb, n),
        in_specs=[pl.BlockSpec((tf, w), qrow), ANY_SPEC, ANY_SPEC],
        out_specs=[pl.BlockSpec((tf, w), qrow), pl.BlockSpec((tf, w), qrow),
                   pl.BlockSpec((1, 16, tf), lambda b, i: (b * n + i, 0, 0))],
        out_shape=[SDS((nb * s, w), BF16), SDS((nb * s, w), F32), SDS((nb * n, 16, tf), F32)],
        scratch_shapes=[pltpu.VMEM((s, wa), BF16), pltpu.VMEM((n, wa, tf), BF16),
                        pltpu.VMEM((FOX_HEADS, tf, PAIR), BF16), pltpu.VMEM((FOX_HEADS, 1, tf), F32),
                        pltpu.VMEM((FOX_HEADS, PAIR, tf), F32), pltpu.SemaphoreType.DMA((2,))],
        compiler_params=_params(("arbitrary", "arbitrary")))(pq, kaug, vaug_t)


def fox_delta(dcat, o, nb, s):
    tf = min(TM, s)
    w = TOK_WIDTH

    def body(do_ref, o_ref, dl_ref):
        out = jnp.zeros((tf, LANES), F32)
        for h in range(FOX_HEADS):
            lo, hi = h * HEAD_DIM, (h + 1) * HEAD_DIM
            out = out + _lane_put((tf, LANES), h, jnp.sum(do_ref[:, lo:hi] * o_ref[:, lo:hi], axis=1, keepdims=True))
        dl_ref[...] = out

    row = pl.BlockSpec((tf, w), lambda r: (r, 0))
    return pl.pallas_call(
        body, name="fox_delta", grid=(nb * s // tf,), in_specs=[row, row],
        out_specs=pl.BlockSpec((tf, LANES), lambda r: (r, 0)), out_shape=SDS((nb * s, LANES), F32),
        compiler_params=_params(("parallel",)))(dcat, o)


def fox_bwd(pq, kv, fneg, dcat_bf, lse_rows, delta_rows, nb, s):
    tf = min(TF, s)
    n = s // tf
    w = TOK_WIDTH

    def body(q_hbm, k_ref, v_ref, f_ref, do_hbm, lse_ref, dl_ref, dq_ref, dk_ref, dv_ref, dfk_ref, dfq_ref,
             q_vm, do_vm, km_scr, vm_scr, kt_scr, fk_scr, dk_scr, dv_scr, rs_scr, dq_scr, fq_scr, sems):
        b = pl.program_id(0)
        j = pl.program_id(1)

        @pl.when(j == 0)
        def _():
            rows = pl.ds(pl.multiple_of(b * s, tf), s)
            cq = pltpu.make_async_copy(q_hbm.at[rows, pl.ds(0, w)], q_vm, sems.at[0])
            cd = pltpu.make_async_copy(do_hbm.at[rows, pl.ds(0, w)], do_vm, sems.at[1])
            cq.start()
            cd.start()
            dq_scr[...] = jnp.zeros_like(dq_scr)
            fq_scr[...] = jnp.zeros_like(fq_scr)
            cq.wait()
            cd.wait()

        for p in range(N_PAIRS):
            kp = k_ref[:, p * PAIR:(p + 1) * PAIR] * QK_SCALE
            ke, ko = _split_pair(kp)
            km_scr[2 * p] = ke
            km_scr[2 * p + 1] = ko
            kt_scr[p] = kp.astype(F32).T.astype(BF16)
            ve, vo = _split_pair(v_ref[:, p * PAIR:(p + 1) * PAIR])
            vm_scr[2 * p] = ve
            vm_scr[2 * p + 1] = vo
        for h in range(FOX_HEADS):
            fk_scr[h] = jnp.broadcast_to(f_ref[:, h:h + 1], (tf, tf))
        dk_scr[...] = jnp.zeros_like(dk_scr)
        dv_scr[...] = jnp.zeros_like(dv_scr)
        rs_scr[...] = jnp.zeros_like(rs_scr)

        def tile(i, masked):
            qs = pl.multiple_of(i * tf, tf)
            if masked:
                keep = lax.broadcasted_iota(jnp.int32, (tf, tf), 1) >= lax.broadcasted_iota(jnp.int32, (tf, tf), 0)
            def products(h):
                qp = q_vm[pl.ds(qs, tf), (h // 2) * PAIR:(h // 2 + 1) * PAIR]
                dop = do_vm[pl.ds(qs, tf), (h // 2) * PAIR:(h // 2 + 1) * PAIR]
                return (lax.dot_general(km_scr[h], qp, NT_DIMS, preferred_element_type=F32),
                        lax.dot_general(vm_scr[h], dop, NT_DIMS, preferred_element_type=F32))

            def dependents(h, prb, dsb):
                p = h // 2
                half = slice((h % 2) * HEAD_DIM, (h % 2 + 1) * HEAD_DIM)
                qp = q_vm[pl.ds(qs, tf), p * PAIR:(p + 1) * PAIR]
                dop = do_vm[pl.ds(qs, tf), p * PAIR:(p + 1) * PAIR]
                dv_scr[h] += jnp.dot(prb, dop, preferred_element_type=F32)
                dk_scr[h] += jnp.dot(dsb, qp, preferred_element_type=F32)
                dqt = jnp.dot(kt_scr[p], dsb, preferred_element_type=F32)
                dq_scr[i, p, half, :] += dqt[(h % 2) * HEAD_DIM:(h % 2 + 1) * HEAD_DIM]

            ahead = [products(h) for h in range(LOOKAHEAD_BWD)]
            behind = []
            for h in range(FOX_HEADS):
                sc, dp = ahead.pop(0)
                if h + LOOKAHEAD_BWD < FOX_HEADS:
                    ahead.append(products(h + LOOKAHEAD_BWD))
                sc = sc + fk_scr[h] - lse_ref[i, h:h + 1, :]
                if masked:
                    sc = jnp.where(keep, sc, NEG_BIG)
                pr = jnp.exp(sc)
                ds = pr * (dp - dl_ref[i, h:h + 1, :])
                part = ds[:, :LANES]
                for c in range(1, tf // LANES):
                    part = part + ds[:, c * LANES:(c + 1) * LANES]
                rs_scr[h] += part
                fq_scr[i, h:h + 1, :] += jnp.sum(ds, axis=0, keepdims=True)
                behind.append((h, pr.astype(BF16), ds.astype(BF16)))
                if len(behind) > FOLLOW_BWD:
                    dependents(*behind.pop(0))
            for item in behind:
                dependents(*item)

        def step(i, carry):
            tile(i, False)
            return carry

        tile(j, True)
        for p in range(N_PAIRS):
            dq_ref[:, p * PAIR:(p + 1) * PAIR] = dq_scr[j, p].T.astype(BF16)
        dfq_ref[0] = fq_scr[j]
        lax.fori_loop(j + 1, n, step, 0)

        lo = _half_masks(tf)
        dfk = jnp.zeros((tf, LANES), F32)
        for p in range(N_PAIRS):
            dk = jnp.where(lo, dk_scr[2 * p], dk_scr[2 * p + 1]) * QK_SCALE
            dk_ref[:, p * PAIR:(p + 1) * PAIR] = dk.astype(BF16)
            dv_ref[:, p * PAIR:(p + 1) * PAIR] = jnp.where(lo, dv_scr[2 * p], dv_scr[2 * p + 1]).astype(BF16)
            for h in (2 * p, 2 * p + 1):
                dfk = dfk - _lane_put((tf, LANES), h, jnp.sum(rs_scr[h], axis=1, keepdims=True))
        dfk_ref[...] = dfk

    krow = lambda b, j: (b * n + j, 0)
    rows = pl.BlockSpec((n, 16, tf), lambda b, j: (b, 0, 0))
    tile_out = pl.BlockSpec((tf, w), krow)
    return pl.pallas_call(
        body, name="fox_bwd", grid=(nb, n),
        in_specs=[ANY_SPEC, pl.BlockSpec((tf, w), krow), pl.BlockSpec((tf, w), lambda b, j: (b * n + j, 1)),
                  pl.BlockSpec((tf, LANES), krow), ANY_SPEC, rows, rows],
        out_specs=[tile_out, tile_out, tile_out, pl.BlockSpec((tf, LANES), krow),
                   pl.BlockSpec((1, 16, tf), lambda b, j: (b * n + j, 0, 0))],
        out_shape=[SDS((nb * s, w), BF16), SDS((nb * s, w), BF16), SDS((nb * s, w), BF16), SDS((nb * s, LANES), F32),
                   SDS((nb * n, 16, tf), F32)],
        scratch_shapes=[pltpu.VMEM((s, w), BF16), pltpu.VMEM((s, w), BF16),
                        pltpu.VMEM((FOX_HEADS, tf, PAIR), BF16), pltpu.VMEM((FOX_HEADS, tf, PAIR), BF16),
                        pltpu.VMEM((N_PAIRS, PAIR, tf), BF16), pltpu.VMEM((FOX_HEADS, tf, tf), F32),
                        pltpu.VMEM((FOX_HEADS, tf, PAIR), F32), pltpu.VMEM((FOX_HEADS, tf, PAIR), F32),
                        pltpu.VMEM((FOX_HEADS, tf, LANES), F32), pltpu.VMEM((n, N_PAIRS, PAIR, tf), F32),
                        pltpu.VMEM((n, 16, tf), F32), pltpu.SemaphoreType.DMA((2,))],
        compiler_params=_params(("arbitrary", "arbitrary")))(pq, kv, kv, fneg, dcat_bf, lse_rows, delta_rows)


def reduce_adamw(parts, w, m, v, name):
    _, r, c = parts.shape
    tr = r
    for cand in range(16, r, 16):
        if r % cand == 0 and cand * c <= 128 * 1024:
            tr = cand
    c1 = 1.0 - ADAM_B1 ** ADAM_STEP
    c2 = 1.0 - ADAM_B2 ** ADAM_STEP

    def body(p_ref, w_ref, m_ref, v_ref, g_out, d_out, m_out, v_out):
        g = p_ref[0].astype(F32)
        for k in range(1, N_DEV):
            g = g + p_ref[k].astype(F32)
        mn = ADAM_B1 * m_ref[...] + (1.0 - ADAM_B1) * g
        vn = ADAM_B2 * v_ref[...] + (1.0 - ADAM_B2) * (g * g)
        g_out[...] = g
        m_out[...] = mn
        v_out[...] = vn
        d_out[...] = -ADAM_LR * ((mn / c1) / (jnp.sqrt(vn / c2) + ADAM_EPS) + ADAM_WD * w_ref[...])

    row = pl.BlockSpec((tr, c), lambda i: (i, 0))
    return pl.pallas_call(
        body, name=name, grid=(r // tr,),
        in_specs=[pl.BlockSpec((N_DEV, tr, c), lambda i: (0, i, 0)), row, row, row],
        out_specs=[row, row, row, row], out_shape=[SDS((r, c), F32)] * 4,
        compiler_params=_params(("parallel",)))(parts, w, m, v)


N_PEERS = N_DEV - 1
HBM_SPEC = pl.BlockSpec(memory_space=pltpu.HBM)
SEM_SPEC = pl.BlockSpec(memory_space=pltpu.SEMAPHORE)
ANY_SPEC = pl.BlockSpec(memory_space=pl.ANY)
SPLIT_EFFECT = pltpu.SideEffectType.DATAFLOW_SIDE_EFFECTING


def _my_index():
    return 4 * lax.axis_index("x") + 2 * lax.axis_index("y") + lax.axis_index("c")


def _peers():
    x, y, c = lax.axis_index("x"), lax.axis_index("y"), lax.axis_index("c")
    peers = []
    for k in range(1, N_DEV):
        px = 1 - x if (k >> 2) & 1 else x
        py = 1 - y if (k >> 1) & 1 else y
        pc = 1 - c if k & 1 else c
        peers.append(((px, py, pc), 4 * px + 2 * py + pc))
    return 4 * x + 2 * y + c, peers


def _push(src, dst, send_sems, recv_sems, slot, dev):
    return pltpu.make_async_remote_copy(src_ref=src, dst_ref=dst, send_sem=send_sems.at[slot], recv_sem=recv_sems.at[slot],
                                        device_id=dev, device_id_type=pl.DeviceIdType.MESH)


def _landing_shapes(arrs, scatter):
    return [SDS((N_DEV,) + tuple(a.shape[1:] if sc else a.shape), a.dtype) for a, sc in zip(arrs, scatter)]


def exchange(arrs, scatter, name):
    na = len(arrs)

    def body(*refs):
        ins = refs[:na]
        outs = refs[na:2 * na]
        send_sems, recv_sems, local_sems = refs[2 * na:]
        me, peers = _peers()
        local = []
        remote = []
        for a in range(na):
            lc = pltpu.make_async_copy(ins[a].at[me] if scatter[a] else ins[a], outs[a].at[me], local_sems.at[a])
            lc.start()
            local.append(lc)
            for k, (dev, idx) in enumerate(peers):
                cp = _push(ins[a].at[idx] if scatter[a] else ins[a], outs[a].at[me], send_sems, recv_sems,
                           a * N_PEERS + k, dev)
                cp.start()
                remote.append(cp)
        for a in range(na):
            for k, (dev, idx) in enumerate(peers):
                _push(ins[a].at[me] if scatter[a] else ins[a], outs[a].at[idx], send_sems, recv_sems,
                      a * N_PEERS + k, dev).wait_recv()
        for cp in remote:
            cp.wait_send()
        for lc in local:
            lc.wait()

    return pl.pallas_call(
        body, name=name, in_specs=[HBM_SPEC] * na, out_specs=[HBM_SPEC] * na, out_shape=_landing_shapes(arrs, scatter),
        scratch_shapes=[pltpu.SemaphoreType.DMA((na * N_PEERS,)), pltpu.SemaphoreType.DMA((na * N_PEERS,)),
                        pltpu.SemaphoreType.DMA((na,))])(*arrs)


def exchange_start(arrs, scatter, after, name):
    na = len(arrs)
    lands = [lax.empty(l.shape, l.dtype) for l in _landing_shapes(arrs, scatter)]

    def body(*refs):
        ins = refs[:na]
        land = refs[na:2 * na]
        send_sems, recv_sems = refs[2 * na + 1], refs[2 * na + 2]
        token = refs[-1]
        me, peers = _peers()
        for a in range(na):
            for k, (dev, idx) in enumerate(peers):
                _push(ins[a].at[idx] if scatter[a] else ins[a], land[a].at[me], send_sems, recv_sems,
                      a * N_PEERS + k, dev).start()
        token[...] = jnp.zeros_like(token)

    thru = [pltpu.HBM(a.shape, a.dtype) for a in arrs] + [pltpu.HBM(l.shape, l.dtype) for l in lands]
    res = pl.pallas_call(
        body, name=name,
        out_shape=(pltpu.SemaphoreType.DMA((na * N_PEERS,)), pltpu.SemaphoreType.DMA((na * N_PEERS,)), *thru,
                   SDS((8, LANES), F32)),
        in_specs=[HBM_SPEC] * (2 * na) + [ANY_SPEC],
        out_specs=(SEM_SPEC, SEM_SPEC, *([HBM_SPEC] * (2 * na)), pl.BlockSpec(memory_space=pltpu.VMEM)),
        input_output_aliases={i: 2 + i for i in range(2 * na)},
        compiler_params=pltpu.CompilerParams(has_side_effects=SPLIT_EFFECT),
    )(*[pltpu.with_memory_space_constraint(a, pltpu.HBM) for a in arrs],
      *[pltpu.with_memory_space_constraint(l, pltpu.HBM) for l in lands], after)
    return {"send": res[0], "recv": res[1], "src": res[2:2 + na], "land": res[2 + na:2 + 2 * na],
            "token": res[-1][0, 0], "scatter": scatter}


def exchange_wait(handle, after, name):
    scatter = handle["scatter"]
    na = len(scatter)

    def body(*refs):
        src = refs[:na]
        land = refs[na:2 * na]
        send_sems, recv_sems = refs[2 * na], refs[2 * na + 1]
        me, peers = _peers()
        for a in range(na):
            for k, (dev, idx) in enumerate(peers):
                cp = _push(src[a].at[me] if scatter[a] else src[a], land[a].at[idx], send_sems, recv_sems,
                           a * N_PEERS + k, dev)
                cp.wait_send()
                cp.wait_recv()

    ops = list(handle["src"]) + list(handle["land"])
    res = pl.pallas_call(
        body, name=name, out_shape=tuple(pltpu.HBM(o.shape, o.dtype) for o in ops),
        in_specs=[HBM_SPEC] * (2 * na) + [SEM_SPEC, SEM_SPEC, ANY_SPEC], out_specs=tuple([HBM_SPEC] * (2 * na)),
        input_output_aliases={i: i for i in range(2 * na)},
        compiler_params=pltpu.CompilerParams(has_side_effects=SPLIT_EFFECT),
    )(*ops, handle["send"], handle["recv"], after)
    me = _my_index()
    out = []
    for a in range(na):
        own = lax.dynamic_index_in_dim(res[a], me, 0, keepdims=True) if scatter[a] else res[a][None]
        out.append(lax.dynamic_update_slice(res[na + a], own, (me,) + (0,) * (own.ndim - 1)))
    return out


def forward_layer(l, xin, xin_bf, mem_bf, wt, nb, s, ffn_weights=None):
    sv = {"xin_bf": xin_bf}
    memkv = mm_nn(mem_bf, wt["memw"], BF16, f"memkv{l}")
    sv["memkv"] = memkv
    if l == 0:
        proj = mm_nn(xin_bf, wt["win_a"], F32, "proj_a")
        pooled, tok = pool_fwd(proj, wt["pw_bd"], wt["pscale"], nb, s)
        sv["pooled"] = pooled
    else:
        kv = mm_nn(xin_bf, wt["kvw"][:, :2 * TOK_WIDTH], BF16, "kv_proj")
        fl = mm_nn(xin_bf, wt["kvw"][:, 2 * TOK_WIDTH:], F32, "gate_proj")
        fneg = -fgate_fwd(fl, wt["fb"], nb, s)
        proj = mm_nn(xin_bf, wt["wq"], BF16, "proj_b")
        kaug, vaug_t = fox_prep(kv, fneg, nb, s)
        tok, o_f32, lse_rows = fox_fwd_t(proj, kaug, vaug_t, nb, s)
        sv.update(kv=kv, fl=fl, fneg=fneg, o_f32=o_f32, lse_rows=lse_rows)
    sv["proj"] = proj
    mem_out = memattn_fwd(proj, memkv, nb, s, f"memattn_fwd{l}")
    cat = jnp.concatenate([tok, mem_out], axis=1)
    sv["cat"] = cat
    mix = mm_nn(cat, wt["wout"], F32, f"out_proj{l}")
    x1, x1_bf, xh1, rs1 = ln_fwd(xin, mix, wt["ln1_g"], wt["ln1_b"], f"ln1_fwd{l}")
    sv.update(x1_bf=x1_bf, xh1=xh1, rs1=rs1)
    if ffn_weights is not None:
        wt.update(ffn_weights(x1_bf))
    h = mm_nn(x1_bf, wt["wup"], F32, f"ffn_up{l}")
    act = convgate_fwd(h, wt["cw"], nb, s, f"convgate_fwd{l}")
    sv.update(h=h, act=act)
    ffn = mm_nn(act, wt["wdown"], F32, f"ffn_down{l}")
    x2, x2_bf, xh2, rs2 = ln_fwd(x1, ffn, wt["ln2_g"], wt["ln2_b"], f"ln2_fwd{l}")
    sv.update(xh2=xh2, rs2=rs2)
    return x2, x2_bf, sv


def backward_layer(l, dy, sv, mem_bf, wt, nb, s, after_ffn=None):
    g = {}
    dr2, dr2_bf, g["ln2_g"], g["ln2_b"] = ln_bwd(dy, sv["xh2"], sv["rs2"], wt["ln2_g"], f"ln2_bwd{l}")
    dact = mm_nn(dr2_bf, wt["wdown_t"], BF16, f"ffn_down_dx{l}")
    g["wdown"] = mm_tn(sv["act"], dr2_bf, f"ffn_down_dw{l}")
    dh_u, dh_g, dcw_u, dcw_g = convgate_bwd(sv["h"], dact, wt["cw"], nb, s, f"convgate_bwd{l}")
    g["cw"] = jnp.concatenate([dcw_u, dcw_g], axis=0)
    half = FF_PAIRS * FF_BLOCK_PAD
    dx1 = mm_nn(dh_u, wt["wup_t"][:half], F32, f"ffn_up_dx_u{l}", addend=dr2, add_scale=DN_ALPHA)
    dx1 = mm_nn(dh_g, wt["wup_t"][half:], F32, f"ffn_up_dx_g{l}", addend=dx1)
    g["wup"] = jnp.concatenate([mm_tn(sv["x1_bf"], dh_u, f"ffn_up_dw_u{l}", blocked=True),
                                mm_tn(sv["x1_bf"], dh_g, f"ffn_up_dw_g{l}", blocked=True)], axis=0)
    ln1_g = wt["ln1_g"] if after_ffn is None else wt["ln1_g"] + after_ffn(g, dx1)
    dr1, dr1_bf, g["ln1_g"], g["ln1_b"] = ln_bwd(dx1, sv["xh1"], sv["rs1"], ln1_g, f"ln1_bwd{l}")
    dcat, dcat_bf = mm_nn(dr1_bf, wt["wout_t"], F32, f"out_proj_dx{l}", also_bf16=True)
    g["wout"] = mm_tn(sv["cat"], dr1_bf, f"out_proj_dw{l}")
    dqm, dmemkv = memattn_bwd(sv["proj"], sv["memkv"], dcat, nb, s, f"memattn_bwd{l}")
    g["memw"] = mm_tn(mem_bf, dmemkv, f"memkv_dw{l}")
    if l == 0:
        dmixed, dpooled, g["pscale"] = pool_bwd_mix(dcat, sv["pooled"], wt["pw_bd"], wt["pw_bd_t"], wt["pscale"], nb, s)
        g["pw_full"] = mm_tn(sv["pooled"], dmixed, "pool_dw")
        du = pool_bwd_window(dpooled, nb, s)
        dproj = jnp.concatenate([du, dqm], axis=1)
        dx = mm_nn(dproj, wt["win_a_t"], F32, "proj_a_dx", addend=dr1, add_scale=DN_ALPHA)
        g["win_a"] = mm_tn(sv["xin_bf"], dproj, "proj_a_dw")
    else:
        delta = fox_delta(dcat, sv["o_f32"], nb, s)
        tf = min(TF, s)
        dq, dk, dv, dfcum_k, dfq_rows = fox_bwd(sv["proj"], sv["kv"], sv["fneg"], dcat_bf,
                                                sv["lse_rows"], _to_tile_rows(delta, nb, s, tf), nb, s)
        dfl, g["fb"] = fgate_bwd(_from_tile_rows(dfq_rows), dfcum_k, sv["fl"], wt["fb"], nb, s)
        dproj = jnp.concatenate([dq, dqm], axis=1)
        dkvf = jnp.concatenate([dk, dv, dfl.astype(BF16)], axis=1)
        dx = mm_nn(dproj, wt["wq_t"], F32, "proj_b_dx", addend=dr1, add_scale=DN_ALPHA)
        dx = mm_nn(dkvf, wt["kvw_t"], F32, "kv_proj_dx", addend=dx)
        g["wq"] = mm_tn(sv["xin_bf"], dproj, "proj_b_dw")
        g["kvw"] = mm_tn(sv["xin_bf"], dkvf, "kv_proj_dw")
    return dx, g


def pack_replicated(pool_w, ln1_g, ln1_b, ln2_g, ln2_b, conv_b, f_b):
    cb = jnp.pad(conv_b, ((0, 0), (0, 6144 - 5504))).reshape(12, D_MODEL)
    fb = jnp.pad(f_b.reshape(1, FOX_HEADS), ((0, 3), (0, D_MODEL - FOX_HEADS)))
    return jnp.concatenate([pool_w.reshape(144, D_MODEL), ln1_g, ln1_b, ln2_g, ln2_b, cb, fb], axis=0)


def unpack_replicated(buf):
    pool_w = buf[:144].reshape(1, 4, POOL_GROUP, POOL_GROUP)
    ln = [buf[144 + 2 * k:146 + 2 * k] for k in range(4)]
    conv_b = buf[152:164].reshape(2, 6144)[:, :5504]
    f_b = buf[164, :FOX_HEADS]
    return pool_w, ln[0], ln[1], ln[2], ln[3], conv_b, f_b


def pack_small(conv_w, pool_scale):
    buf = jnp.zeros((16, FF_BLOCK_PAD), F32)
    buf = lax.dynamic_update_slice(buf, conv_w.reshape(DEPTH * 3, FF_BLOCK), (0, 0))
    return lax.dynamic_update_slice(buf, pool_scale, (8, 0))


def _block_diag(pw):
    out = jnp.zeros((TOK_WIDTH, TOK_WIDTH), pw.dtype)
    for g in range(4):
        out = lax.dynamic_update_slice(out, pw[g], (g * POOL_GROUP, g * POOL_GROUP))
    return out


def layer_shards(l, sq_a, sq_b, mem_w_kv, ffn_w_up, ffn_w_down):
    return [sq_a[0].astype(BF16), sq_b[0].astype(BF16), mem_w_kv[l].astype(BF16), ffn_w_up[l].astype(BF16),
            ffn_w_down[l].astype(BF16)]


def mixer_weights(l, gath, ln1_g, ln1_b, ln2_g, ln2_b):
    w_out = gath[1].reshape(D_MODEL, D_MODEL)
    wt = {"memw": gath[2].reshape(D_MODEL, 2 * MEM_WIDTH), "wout": w_out, "wout_t": w_out.T,
          "ln1_g": ln1_g[l:l + 1], "ln1_b": ln1_b[l:l + 1], "ln2_g": ln2_g[l:l + 1], "ln2_b": ln2_b[l:l + 1]}
    return wt, gath[0].reshape(D_MODEL, D_MODEL)


def ffn_weights(l, wup_g, wdown_g, small, conv_b):
    pad_c = FF_BLOCK_PAD - FF_BLOCK
    wup = jnp.pad(wup_g, ((0, 0), (0, 0), (0, pad_c))).transpose(1, 0, 2).reshape(D_MODEL, N_DEV * FF_BLOCK_PAD)
    wdown = jnp.pad(wdown_g.reshape(FF_PAIRS, FF_BLOCK, D_MODEL), ((0, 0), (0, pad_c), (0, 0)))
    wdown = wdown.reshape(FF_PAIRS * FF_BLOCK_PAD, D_MODEL)
    cb = jnp.pad(conv_b[l].reshape(N_DEV, FF_BLOCK), ((0, 0), (0, pad_c)))
    cw = jnp.concatenate([small[:, 3 * l:3 * l + 3, :], cb[:, None, :], jnp.zeros((N_DEV, 4, FF_BLOCK_PAD), F32)], axis=1)
    return {"wup": wup, "wup_t": wup.T, "wdown": wdown, "wdown_t": wdown.T, "cw": cw}


def mixer_grad_blocks(g, w_in_grad):
    blocks = [w_in_grad.reshape(N_DEV, 128, D_MODEL), g["wout"].reshape(N_DEV, 128, D_MODEL),
              g["memw"].reshape(N_DEV, 128, 2 * MEM_WIDTH)]
    return [b.astype(BF16) for b in blocks]


def ffn_grad_blocks(g):
    wup = g["wup"][:, :, :FF_BLOCK]
    wdown = g["wdown"].reshape(FF_PAIRS, FF_BLOCK_PAD, D_MODEL)[:, :FF_BLOCK].reshape(N_DEV, FF_ROWS, D_MODEL)
    return [wup.astype(BF16), wdown.astype(BF16)]


def small_grad_blocks(g0, g1):
    taps = jnp.stack([g0["cw"][:, :3, :], g1["cw"][:, :3, :]], axis=1).reshape(N_DEV, DEPTH * 3, FF_BLOCK_PAD)
    small = jnp.zeros((N_DEV, 16, FF_BLOCK_PAD), F32)
    small = lax.dynamic_update_slice(small, taps, (0, 0, 0))
    return lax.dynamic_update_slice(small, g0["pscale"].reshape(N_DEV, 1, 96), (0, 8, 0))


def replicated_grads(g0, g1):
    pw = jnp.stack([g0["pw_full"][k * POOL_GROUP:(k + 1) * POOL_GROUP, k * POOL_GROUP:(k + 1) * POOL_GROUP] for k in range(4)])
    conv_b = jnp.stack([g_["cw"][:, 3, :FF_BLOCK].reshape(N_DEV * FF_BLOCK) for g_ in (g0, g1)])
    ln = [jnp.concatenate([g0[n], g1[n]], axis=0) for n in ("ln1_g", "ln1_b", "ln2_g", "ln2_b")]
    return pack_replicated(pw[None], ln[0], ln[1], ln[2], ln[3], conv_b, g1["fb"][0, :FOX_HEADS])


def kernel(x, mem, a_w_in, a_pool_w, a_pool_scale, a_w_out, b_w_q, b_w_out, kv_w, f_b, mem_w_kv, ln1_g, ln1_b, ln2_g, ln2_b, ffn_w_up, ffn_conv_w, ffn_conv_b, ffn_w_down, loss_target, m_a_w_in, m_a_pool_w, m_a_pool_scale, m_a_w_out, m_b_w_q, m_b_w_out, m_kv_w, m_f_b, m_mem_w_kv, m_ln1_g, m_ln1_b, m_ln2_g, m_ln2_b, m_ffn_w_up, m_ffn_conv_w, m_ffn_conv_b, m_ffn_w_down, v_a_w_in, v_a_pool_w, v_a_pool_scale, v_a_w_out, v_b_w_q, v_b_w_out, v_kv_w, v_f_b, v_mem_w_kv, v_ln1_g, v_ln1_b, v_ln2_g, v_ln2_b, v_ffn_w_up, v_ffn_conv_w, v_ffn_conv_b, v_ffn_w_down):
    nb, s, d = x.shape
    t = nb * s
    x2d, mem_bf, target = x.reshape(t, d), mem.reshape(nb * MEM_LEN, d).astype(BF16), loss_target.reshape(t, d)

    shards0 = layer_shards(0, a_w_in, a_w_out, mem_w_kv, ffn_w_up, ffn_w_down)
    shards1 = layer_shards(1, b_w_q, b_w_out, mem_w_kv, ffn_w_up, ffn_w_down)
    shards1.append(jnp.pad(kv_w, ((0, 0), (0, KV_COLS_PAD - KV_COLS))).astype(BF16))
    gath0 = exchange(shards0[:3] + [pack_small(ffn_conv_w, a_pool_scale)], [False] * 4, "gather_w0_mixer")
    pending = {"ffn0": exchange_start(shards0[3:], [False] * 2, gath0[0], "gather_w0_ffn_start")}
    small = gath0[3]
    wt0, w_in = mixer_weights(0, gath0, ln1_g + pending["ffn0"]["token"], ln1_b, ln2_g, ln2_b)
    pw_bd = _block_diag(a_pool_w[0])
    wt0.update(win_a=w_in, win_a_t=w_in.T, pw_bd=pw_bd.astype(BF16), pw_bd_t=pw_bd.T.astype(BF16),
               pscale=small[:, 8, :96].reshape(1, TOK_WIDTH) + pending["ffn0"]["token"])

    def ffn0_weights(x1_bf):
        got = exchange_wait(pending["ffn0"], x1_bf, "gather_w0_ffn_wait")
        pending["w1"] = exchange_start(shards1, [False] * 6, got[0], "gather_w1_start")
        w = ffn_weights(0, got[0], got[1], small, ffn_conv_b)
        w["cw"] = w["cw"] + pending["w1"]["token"]
        return w

    x1, x1_bf, sv0 = forward_layer(0, x2d, x2d.astype(BF16), mem_bf, wt0, nb, s, ffn_weights=ffn0_weights)
    gath1 = exchange_wait(pending["w1"], x1_bf, "gather_w1_wait")
    wt1, w_q = mixer_weights(1, gath1, ln1_g, ln1_b, ln2_g, ln2_b)
    wt1.update(ffn_weights(1, gath1[3], gath1[4], small, ffn_conv_b))
    kvw = gath1[5].reshape(D_MODEL, KV_COLS_PAD)
    wt1.update(wq=w_q, wq_t=w_q.T, kvw=kvw, kvw_t=kvw.T,
               fb=jnp.pad(f_b.reshape(1, FOX_HEADS), ((0, 0), (0, LANES - FOX_HEADS))))
    y, _, sv1 = forward_layer(1, x1, x1_bf, mem_bf, wt1, nb, s)
    dy, loss_row = loss_head(y, target)
    loss = lax.psum(loss_row[0, 0], ("x", "y", "c"))

    dx1, g1 = backward_layer(1, dy, sv1, mem_bf, wt1, nb, s)
    blocks1 = (mixer_grad_blocks(g1, g1["wq"]) + ffn_grad_blocks(g1)
               + [g1["kvw"][:, :KV_COLS].reshape(N_DEV, 128, KV_COLS).astype(BF16)])
    pending["g1"] = exchange_start(blocks1, [True] * 6, dx1, "scatter_g1_start")
    wt0["ln2_g"] = wt0["ln2_g"] + pending["g1"]["token"]

    def after_ffn0(g, dxm):
        pending["gf0"] = exchange_start(ffn_grad_blocks(g), [True] * 2, dxm, "scatter_g0_ffn_start")
        return pending["gf0"]["token"]

    grad_x, g0 = backward_layer(0, dx1, sv0, mem_bf, wt0, nb, s, after_ffn=after_ffn0)
    last = mixer_grad_blocks(g0, g0["win_a"]) + [small_grad_blocks(g0, g1), replicated_grads(g0, g1)]
    parts_m0 = exchange(last, [True] * 4 + [False], "scatter_g0_mixer")
    parts_f0 = exchange_wait(pending["gf0"], parts_m0[0], "scatter_g0_ffn_wait")
    parts1 = exchange_wait(pending["g1"], parts_f0[0], "scatter_g1_wait")
    parts0 = list(parts_m0[:3]) + list(parts_f0) + list(parts_m0[3:])

    res = {}

    def upd(nm, parts, w2, m2, v2):
        res[nm] = reduce_adamw(parts, w2, m2, v2, f"adamw_{nm}")

    upd("a_w_in", parts0[0], a_w_in[0], m_a_w_in[0], v_a_w_in[0])
    upd("a_w_out", parts0[1], a_w_out[0], m_a_w_out[0], v_a_w_out[0])
    upd("b_w_q", parts1[0], b_w_q[0], m_b_w_q[0], v_b_w_q[0])
    upd("b_w_out", parts1[1], b_w_out[0], m_b_w_out[0], v_b_w_out[0])
    upd("kv_w", parts1[5], kv_w, m_kv_w, v_kv_w)
    for l, parts in enumerate((parts0, parts1)):
        upd(f"mem_w_kv{l}", parts[2], mem_w_kv[l], m_mem_w_kv[l], v_mem_w_kv[l])
        upd(f"ffn_w_up{l}", parts[3], ffn_w_up[l], m_ffn_w_up[l], v_ffn_w_up[l])
        upd(f"ffn_w_down{l}", parts[4], ffn_w_down[l], m_ffn_w_down[l], v_ffn_w_down[l])
    upd("small", parts0[5], pack_small(ffn_conv_w, a_pool_scale), pack_small(m_ffn_conv_w, m_a_pool_scale),
        pack_small(v_ffn_conv_w, v_a_pool_scale))
    upd("replicated", parts0[6], pack_replicated(a_pool_w, ln1_g, ln1_b, ln2_g, ln2_b, ffn_conv_b, f_b),
        pack_replicated(m_a_pool_w, m_ln1_g, m_ln1_b, m_ln2_g, m_ln2_b, m_ffn_conv_b, m_f_b),
        pack_replicated(v_a_pool_w, v_ln1_g, v_ln1_b, v_ln2_g, v_ln2_b, v_ffn_conv_b, v_f_b))

    for nm in ("a_w_in", "a_w_out", "b_w_q", "b_w_out"):
        res[nm] = [o[None] for o in res[nm]]
    for nm in ("mem_w_kv", "ffn_w_up", "ffn_w_down"):
        res[nm] = [jnp.stack([a0, a1]) for a0, a1 in zip(res[nm + "0"], res[nm + "1"])]
    res["ffn_conv_w"] = [o[:DEPTH * 3, :FF_BLOCK].reshape(DEPTH, 3, FF_BLOCK) for o in res["small"]]
    res["a_pool_scale"] = [o[8:9, :96] for o in res["small"]]
    rep_names = ["a_pool_w", "ln1_g", "ln1_b", "ln2_g", "ln2_b", "ffn_conv_b", "f_b"]
    for nm in rep_names:
        res[nm] = []
    for o in res["replicated"]:
        for nm, val in zip(rep_names, unpack_replicated(o)):
            res[nm].append(val)

    order = ["a_w_in", "a_pool_w", "a_pool_scale", "a_w_out", "b_w_q", "b_w_out", "kv_w", "f_b", "mem_w_kv",
             "ln1_g", "ln1_b", "ln2_g", "ln2_b", "ffn_w_up", "ffn_conv_w", "ffn_conv_b", "ffn_w_down"]
    out = [loss, grad_x.reshape(nb, s, d)]
    for kind in range(4):
        out.extend(res[nm][kind] for nm in order)
    return tuple(out)
```

```python
import jax
import jax.numpy as jnp
from jax import lax
from jax.experimental import pallas as pl
from jax.experimental.pallas import tpu as pltpu

F32 = jnp.float32
BF16 = jnp.bfloat16
SDS = jax.ShapeDtypeStruct

N_DEV = 8
D_MODEL = 1024
TOK_WIDTH = 768
MEM_WIDTH = 256
MEM_LEN = 256
MEM_HEADS = 4
HEAD_DIM = 64
FOX_HEADS = 12
POOL_GROUP = 192
FF_BLOCK = 688
FF_BLOCK_PAD = 768
FF_PAIRS = 4
FF_ROWS = 344
KV_COLS = 1548
KV_COLS_PAD = 1664
LANES = 128
DEPTH = 2
DN_ALPHA = (2.0 * DEPTH) ** 0.25
LN_EPS = 1e-5
QK_SCALE = HEAD_DIM ** -0.5
NEG_BIG = -1e30

ADAM_LR = 0.001
ADAM_B1 = 0.9
ADAM_B2 = 0.999
ADAM_EPS = 1e-08
ADAM_WD = 0.01
ADAM_STEP = 10

VMEM_LIMIT_BYTES = 56 * 1024 * 1024
MM_BLOCK_BYTES = 6 * 1024 * 1024
TM = 512
TS = 256
TF = 256
TC = 256
HALO_POOL = 16
HALO_CONV = 8

NT_DIMS = (((1,), (1,)), ((), ()))
TN_DIMS = (((0,), (0,)), ((), ()))


def _params(sem=None):
    return pltpu.CompilerParams(dimension_semantics=sem, vmem_limit_bytes=VMEM_LIMIT_BYTES)


def _sigmoid(z):
    return 1.0 / (1.0 + jnp.exp(-z))


def _pick_tn(n):
    if n <= 2048:
        return n
    for t in (1024, 768, 512, 256, 128):
        if n % t == 0:
            return t
    return n


def mm_nn(a, b, out_dtype, name, addend=None, add_scale=1.0, also_bf16=False):
    m, k = a.shape
    _, n = b.shape
    tm = min(TM, m)
    tn = n
    while k * tn * 2 > MM_BLOCK_BYTES or tm * tn * 4 > MM_BLOCK_BYTES:
        tn //= 2
    chunk = tn if tn <= 2048 else _pick_tn(tn)
    has_add = addend is not None

    def body(*refs):
        a_ref, b_ref = refs[0], refs[1]
        c_ref = refs[2] if has_add else None
        o_ref = refs[3] if has_add else refs[2]
        ob_ref = refs[-1] if also_bf16 else None
        av = a_ref[...].astype(BF16)
        for c in range(tn // chunk):
            cols = slice(c * chunk, (c + 1) * chunk)
            r = jnp.dot(av, b_ref[:, cols].astype(BF16), preferred_element_type=F32)
            if has_add:
                r = r + add_scale * c_ref[:, cols]
            o_ref[:, cols] = r.astype(out_dtype)
            if also_bf16:
                ob_ref[:, cols] = r.astype(BF16)

    in_specs = [pl.BlockSpec((tm, k), lambda j, i: (i, 0)), pl.BlockSpec((k, tn), lambda j, i: (0, j))]
    ops = [a, b]
    tile = pl.BlockSpec((tm, tn), lambda j, i: (i, j))
    if has_add:
        in_specs.append(tile)
        ops.append(addend)
    out_shape = [SDS((m, n), out_dtype)]
    out_specs = [tile]
    if also_bf16:
        out_shape.append(SDS((m, n), BF16))
        out_specs.append(tile)
    res = pl.pallas_call(
        body, name=name, grid=(n // tn, m // tm), in_specs=in_specs, out_specs=out_specs, out_shape=out_shape,
        compiler_params=_params(("parallel", "parallel")))(*ops)
    return tuple(res) if also_bf16 else res[0]


def mm_tn(a, b, name, blocked=False):
    t, m = a.shape
    _, n = b.shape
    tt = min(2 * TM, t)
    tm = 1024 if m % 1024 == 0 else m
    tn = FF_BLOCK_PAD if blocked else _pick_tn(n)
    nt = t // tt

    def body(a_ref, b_ref, o_ref):
        kk = pl.program_id(2)
        r = lax.dot_general(a_ref[...].astype(BF16), b_ref[...].astype(BF16), TN_DIMS, preferred_element_type=F32)
        if blocked:
            r = r[None]

        @pl.when(kk == 0)
        def _():
            o_ref[...] = r

        @pl.when(kk != 0)
        def _():
            o_ref[...] += r

    if blocked:
        out_shape = SDS((n // tn, m, tn), F32)
        out_spec = pl.BlockSpec((1, tm, tn), lambda i, j, kk: (j, i, 0))
    else:
        out_shape = SDS((m, n), F32)
        out_spec = pl.BlockSpec((tm, tn), lambda i, j, kk: (i, j))
    return pl.pallas_call(
        body, name=name, grid=(m // tm, n // tn, nt),
        in_specs=[pl.BlockSpec((tt, tm), lambda i, j, kk: (kk, i)), pl.BlockSpec((tt, tn), lambda i, j, kk: (kk, j))],
        out_specs=out_spec, out_shape=out_shape,
        compiler_params=_params(("parallel", "parallel", "arbitrary")))(a, b)


def ln_fwd(xprev, delta, g, b, name):
    t, d = xprev.shape
    tm = min(TM, t)

    def body(xp_ref, dl_ref, g_ref, b_ref, y_ref, yb_ref, xh_ref, rs_ref):
        r = DN_ALPHA * xp_ref[...] + dl_ref[...]
        mu = jnp.mean(r, axis=1, keepdims=True)
        xc = r - mu
        var = jnp.mean(xc * xc, axis=1, keepdims=True)
        rstd = lax.rsqrt(var + LN_EPS)
        xh = xc * rstd
        y = xh * g_ref[...] + b_ref[...]
        y_ref[...] = y
        yb_ref[...] = y.astype(BF16)
        xh_ref[...] = xh
        rs_ref[...] = jnp.broadcast_to(rstd, (tm, LANES))

    row = pl.BlockSpec((tm, d), lambda i: (i, 0))
    vec = pl.BlockSpec((1, d), lambda i: (0, 0))
    return pl.pallas_call(
        body, name=name, grid=(t // tm,), in_specs=[row, row, vec, vec],
        out_specs=[row, row, row, pl.BlockSpec((tm, LANES), lambda i: (i, 0))],
        out_shape=[SDS((t, d), F32), SDS((t, d), BF16), SDS((t, d), F32), SDS((t, LANES), F32)],
        compiler_params=_params(("parallel",)))(xprev, delta, g, b)


def ln_bwd(dy, xhat, rstd, g, name):
    t, d = dy.shape
    tm = min(TM, t)

    def body(dy_ref, xh_ref, rs_ref, g_ref, dr_ref, drb_ref, dg_ref, db_ref):
        i = pl.program_id(0)
        dyv = dy_ref[...]
        xh = xh_ref[...]
        dxh = dyv * g_ref[...]
        m1 = jnp.mean(dxh, axis=1, keepdims=True)
        m2 = jnp.mean(dxh * xh, axis=1, keepdims=True)
        dr = rs_ref[:, 0:1] * (dxh - m1 - xh * m2)
        dr_ref[...] = dr
        drb_ref[...] = dr.astype(BF16)

        @pl.when(i == 0)
        def _():
            dg_ref[...] = jnp.zeros_like(dg_ref)
            db_ref[...] = jnp.zeros_like(db_ref)

        dg_ref[...] += jnp.sum(dyv * xh, axis=0, keepdims=True)
        db_ref[...] += jnp.sum(dyv, axis=0, keepdims=True)

    row = pl.BlockSpec((tm, d), lambda i: (i, 0))
    vec = pl.BlockSpec((1, d), lambda i: (0, 0))
    return pl.pallas_call(
        body, name=name, grid=(t // tm,),
        in_specs=[row, row, pl.BlockSpec((tm, LANES), lambda i: (i, 0)), vec],
        out_specs=[row, row, vec, vec],
        out_shape=[SDS((t, d), F32), SDS((t, d), BF16), SDS((1, d), F32), SDS((1, d), F32)],
        compiler_params=_params(("arbitrary",)))(dy, xhat, rstd, g)


def loss_head(y, target):
    t, d = y.shape
    tm = min(TM, t)
    nsteps = t // tm

    def body(y_ref, t_ref, dy_ref, l_ref, acc):
        i = pl.program_id(0)
        diff = y_ref[...] - t_ref[...]
        dy_ref[...] = diff * (1.0 / d)

        @pl.when(i == 0)
        def _():
            acc[...] = jnp.zeros_like(acc)

        acc[...] += jnp.sum(diff * diff, axis=0, keepdims=True)

        @pl.when(i == nsteps - 1)
        def _():
            tot = jnp.sum(acc[...], axis=1, keepdims=True) * (0.5 / d)
            l_ref[...] = jnp.broadcast_to(tot, (1, LANES))

    row = pl.BlockSpec((tm, d), lambda i: (i, 0))
    return pl.pallas_call(
        body, name="loss_head", grid=(nsteps,), in_specs=[row, row],
        out_specs=[row, pl.BlockSpec((1, LANES), lambda i: (0, 0))],
        out_shape=[SDS((t, d), F32), SDS((1, LANES), F32)],
        scratch_shapes=[pltpu.VMEM((1, d), F32)],
        compiler_params=_params(("arbitrary",)))(y, target)


def memattn_fwd(proj, memkv, nb, s, name):
    ts = min(TS, s)
    nq = s // ts

    def body(q_ref, kv_ref, o_ref):
        for h in range(MEM_HEADS):
            lo, hi = h * HEAD_DIM, (h + 1) * HEAD_DIM
            qh = q_ref[:, lo:hi].astype(BF16)
            kh = kv_ref[:, lo:hi]
            vh = kv_ref[:, MEM_WIDTH + lo:MEM_WIDTH + hi]
            sc = lax.dot_general(qh, kh, NT_DIMS, preferred_element_type=F32) * QK_SCALE
            p = jnp.exp(sc - jnp.max(sc, axis=1, keepdims=True))
            p = p / jnp.sum(p, axis=1, keepdims=True)
            o_ref[:, lo:hi] = jnp.dot(p.astype(BF16), vh, preferred_element_type=F32).astype(BF16)

    return pl.pallas_call(
        body, name=name, grid=(nb, nq),
        in_specs=[pl.BlockSpec((ts, MEM_WIDTH), lambda b, i: (b * nq + i, 3)),
                  pl.BlockSpec((MEM_LEN, 2 * MEM_WIDTH), lambda b, i: (b, 0))],
        out_specs=pl.BlockSpec((ts, MEM_WIDTH), lambda b, i: (b * nq + i, 0)),
        out_shape=SDS((nb * s, MEM_WIDTH), BF16),
        compiler_params=_params(("parallel", "parallel")))(proj, memkv)


def memattn_bwd(proj, memkv, dcat, nb, s, name):
    ts = min(TS, s)
    nq = s // ts

    def body(q_ref, kv_ref, do_ref, dq_ref, dkv_ref):
        i = pl.program_id(1)

        @pl.when(i == 0)
        def _():
            dkv_ref[...] = jnp.zeros_like(dkv_ref)

        for h in range(MEM_HEADS):
            lo, hi = h * HEAD_DIM, (h + 1) * HEAD_DIM
            qh = q_ref[:, lo:hi].astype(BF16)
            kh = kv_ref[:, lo:hi]
            vh = kv_ref[:, MEM_WIDTH + lo:MEM_WIDTH + hi]
            doh = do_ref[:, lo:hi].astype(BF16)
            sc = lax.dot_general(qh, kh, NT_DIMS, preferred_element_type=F32) * QK_SCALE
            p = jnp.exp(sc - jnp.max(sc, axis=1, keepdims=True))
            p = p / jnp.sum(p, axis=1, keepdims=True)
            dv = lax.dot_general(p.astype(BF16), doh, TN_DIMS, preferred_element_type=F32)
            dp = lax.dot_general(doh, vh, NT_DIMS, preferred_element_type=F32)
            dl = jnp.sum(p * dp, axis=1, keepdims=True)
            ds = (p * (dp - dl) * QK_SCALE).astype(BF16)
            dq_ref[:, lo:hi] = jnp.dot(ds, kh, preferred_element_type=F32).astype(BF16)
            dkv_ref[:, lo:hi] += lax.dot_general(ds, qh, TN_DIMS, preferred_element_type=F32)
            dkv_ref[:, MEM_WIDTH + lo:MEM_WIDTH + hi] += dv

    return pl.pallas_call(
        body, name=name, grid=(nb, nq),
        in_specs=[pl.BlockSpec((ts, MEM_WIDTH), lambda b, i: (b * nq + i, 3)),
                  pl.BlockSpec((MEM_LEN, 2 * MEM_WIDTH), lambda b, i: (b, 0)),
                  pl.BlockSpec((ts, MEM_WIDTH), lambda b, i: (b * nq + i, 3))],
        out_specs=[pl.BlockSpec((ts, MEM_WIDTH), lambda b, i: (b * nq + i, 0)),
                   pl.BlockSpec((MEM_LEN, 2 * MEM_WIDTH), lambda b, i: (b, 0))],
        out_shape=[SDS((nb * s, MEM_WIDTH), BF16), SDS((nb * MEM_LEN, 2 * MEM_WIDTH), F32)],
        compiler_params=_params(("parallel", "arbitrary")))(proj, memkv, dcat)


def _pool_select(shape, s2, s4, s8, s16):
    lane = lax.broadcasted_iota(jnp.int32, shape, 1)
    return jnp.where(lane < POOL_GROUP, s2, jnp.where(lane < 2 * POOL_GROUP, s4, jnp.where(lane < 3 * POOL_GROUP, s8, s16)))


def _pool_count(shape, first_pos):
    pos = first_pos + lax.broadcasted_iota(jnp.int32, shape, 0)
    win = _pool_select(shape, 2, 4, 8, 16)
    return jnp.minimum(pos + 1, win).astype(F32)


def pool_fwd(proj, pw_bd, pscale, nb, s):
    ts = min(TS, s)
    nq = s // ts
    w = TOK_WIDTH

    def body(c_ref, h_ref, w_ref, sc_ref, pooled_ref, tok_ref):
        i = pl.program_id(0) % nq
        cur = c_ref[...]
        halo = jnp.where(i == 0, 0.0, h_ref[...])
        xe = jnp.concatenate([halo, cur], axis=0)
        s2 = xe + pltpu.roll(xe, 1, axis=0)
        s4 = s2 + pltpu.roll(s2, 2, axis=0)
        s8 = s4 + pltpu.roll(s4, 4, axis=0)
        s16 = s8 + pltpu.roll(s8, 8, axis=0)
        hp = HALO_POOL
        ws = _pool_select((ts, w), s2[hp:], s4[hp:], s8[hp:], s16[hp:])
        pooled = (ws / _pool_count((ts, w), i * ts) - cur).astype(BF16)
        pooled_ref[...] = pooled
        mixed = jnp.dot(pooled, w_ref[...], preferred_element_type=F32)
        tok_ref[...] = (mixed * sc_ref[...]).astype(BF16)

    row = pl.BlockSpec((ts, w), lambda r: (r, 0))
    return pl.pallas_call(
        body, name="pool_fwd", grid=(nb * nq,),
        in_specs=[row, pl.BlockSpec((HALO_POOL, w), lambda r: (jnp.maximum(r * (ts // HALO_POOL) - 1, 0), 0)),
                  pl.BlockSpec((w, w), lambda r: (0, 0)), pl.BlockSpec((1, w), lambda r: (0, 0))],
        out_specs=[row, row], out_shape=[SDS((nb * s, w), BF16), SDS((nb * s, w), BF16)],
        compiler_params=_params(("parallel",)))(proj, proj, pw_bd, pscale)


def pool_bwd_mix(dcat, pooled, pw_bd, pw_bd_t, pscale, nb, s):
    ts = min(TS, s)
    w = TOK_WIDTH

    def body(dt_ref, p_ref, w_ref, wt_ref, sc_ref, dm_ref, dp_ref, ds_ref):
        r = pl.program_id(0)
        dtok = dt_ref[...]
        mixed = jnp.dot(p_ref[...], w_ref[...], preferred_element_type=F32)

        @pl.when(r == 0)
        def _():
            ds_ref[...] = jnp.zeros_like(ds_ref)

        ds_ref[...] += jnp.sum(dtok * mixed, axis=0, keepdims=True)
        dmx = (dtok * sc_ref[...]).astype(BF16)
        dm_ref[...] = dmx
        dp_ref[...] = jnp.dot(dmx, wt_ref[...], preferred_element_type=F32)

    row = pl.BlockSpec((ts, w), lambda r: (r, 0))
    mat = pl.BlockSpec((w, w), lambda r: (0, 0))
    vec = pl.BlockSpec((1, w), lambda r: (0, 0))
    return pl.pallas_call(
        body, name="pool_bwd_mix", grid=(nb * s // ts,), in_specs=[row, row, mat, mat, vec],
        out_specs=[row, row, vec], out_shape=[SDS((nb * s, w), BF16), SDS((nb * s, w), F32), SDS((1, w), F32)],
        compiler_params=_params(("arbitrary",)))(dcat, pooled, pw_bd, pw_bd_t, pscale)


def pool_bwd_window(dpooled, nb, s):
    ts = min(TS, s)
    nq = s // ts
    w = TOK_WIDTH
    n_ext = ts + HALO_POOL
    n_halo_blocks = nb * s // HALO_POOL

    def body(c_ref, n_ref, du_ref):
        i = pl.program_id(0) % nq
        cur = c_ref[...]
        nxt = jnp.where(i == nq - 1, 0.0, n_ref[...])
        ze = jnp.concatenate([cur, nxt], axis=0) / _pool_count((n_ext, w), i * ts)
        s2 = ze + pltpu.roll(ze, n_ext - 1, axis=0)
        s4 = s2 + pltpu.roll(s2, n_ext - 2, axis=0)
        s8 = s4 + pltpu.roll(s4, n_ext - 4, axis=0)
        s16 = s8 + pltpu.roll(s8, n_ext - 8, axis=0)
        ws = _pool_select((ts, w), s2[:ts], s4[:ts], s8[:ts], s16[:ts])
        du_ref[...] = (ws - cur).astype(BF16)

    row = pl.BlockSpec((ts, w), lambda r: (r, 0))
    return pl.pallas_call(
        body, name="pool_bwd_window", grid=(nb * nq,),
        in_specs=[row, pl.BlockSpec((HALO_POOL, w),
                                    lambda r: (jnp.minimum((r + 1) * (ts // HALO_POOL), n_halo_blocks - 1), 0))],
        out_specs=row, out_shape=SDS((nb * s, w), BF16),
        compiler_params=_params(("parallel",)))(dpooled, dpooled)


def _conv_rows(xe, w_ref):
    return (w_ref[0, 2:3, :] * xe + w_ref[0, 1:2, :] * pltpu.roll(xe, 1, axis=0)
            + w_ref[0, 0:1, :] * pltpu.roll(xe, 2, axis=0) + w_ref[0, 3:4, :])


def convgate_fwd(h, cw, nb, s, name):
    ts = min(TS, s)
    nq = s // ts
    w = FF_BLOCK_PAD
    hc = HALO_CONV

    def body(uc_ref, uh_ref, gc_ref, gh_ref, wu_ref, wg_ref, o_ref):
        first = (pl.program_id(0) % nq) == 0
        xu = jnp.concatenate([jnp.where(first, 0.0, uh_ref[...]), uc_ref[...]], axis=0)
        xg = jnp.concatenate([jnp.where(first, 0.0, gh_ref[...]), gc_ref[...]], axis=0)
        cu = _conv_rows(xu, wu_ref)[hc:]
        cg = _conv_rows(xg, wg_ref)[hc:]
        o_ref[...] = (cg * _sigmoid(cg) * cu).astype(BF16)

    def cur(off):
        return pl.BlockSpec((ts, w), lambda r, j: (r, j + off))

    def halo(off):
        return pl.BlockSpec((hc, w), lambda r, j: (jnp.maximum(r * (ts // hc) - 1, 0), j + off))

    def wspec(off):
        return pl.BlockSpec((1, 8, w), lambda r, j: (j + off, 0, 0))

    return pl.pallas_call(
        body, name=name, grid=(nb * nq, FF_PAIRS),
        in_specs=[cur(0), halo(0), cur(FF_PAIRS), halo(FF_PAIRS), wspec(0), wspec(FF_PAIRS)],
        out_specs=pl.BlockSpec((ts, w), lambda r, j: (r, j)), out_shape=SDS((nb * s, FF_PAIRS * w), BF16),
        compiler_params=_params(("parallel", "parallel")))(h, h, h, h, cw, cw)


def convgate_bwd(h, dact, cw, nb, s, name):
    ts = min(TS, s)
    nq = s // ts
    w = FF_BLOCK_PAD
    hc = HALO_CONV
    n_ext = ts + hc
    n_halo_blocks = nb * s // hc

    def body(uc_ref, up_ref, un_ref, gc_ref, gp_ref, gn_ref, dc_ref, dn_ref, wu_ref, wg_ref,
             dhu_ref, dhg_ref, dwu_ref, dwg_ref):
        r = pl.program_id(1)
        i = r % nq
        first = i == 0
        last = i == nq - 1
        xu = jnp.concatenate([jnp.where(first, 0.0, up_ref[...]), uc_ref[...], un_ref[...]], axis=0)
        xg = jnp.concatenate([jnp.where(first, 0.0, gp_ref[...]), gc_ref[...], gn_ref[...]], axis=0)
        cu = _conv_rows(xu, wu_ref)[hc:]
        cg = _conv_rows(xg, wg_ref)[hc:]
        da = jnp.concatenate([dc_ref[...].astype(F32), jnp.where(last, 0.0, dn_ref[...].astype(F32)[:hc])], axis=0)
        sg = _sigmoid(cg)
        dcu = da * (cg * sg)
        dcg = da * cu * (sg * (1.0 + cg * (1.0 - sg)))

        def conv_t(dcv, w_ref):
            return (w_ref[0, 2:3, :] * dcv + w_ref[0, 1:2, :] * pltpu.roll(dcv, n_ext - 1, axis=0)
                    + w_ref[0, 0:1, :] * pltpu.roll(dcv, n_ext - 2, axis=0))[:ts]

        dhu_ref[...] = conv_t(dcu, wu_ref).astype(BF16)
        dhg_ref[...] = conv_t(dcg, wg_ref).astype(BF16)

        def tap_grads(xe, dcv):
            d0 = dcv[:ts]
            x0 = xe[hc:hc + ts]
            x1 = pltpu.roll(xe, 1, axis=0)[hc:hc + ts]
            x2 = pltpu.roll(xe, 2, axis=0)[hc:hc + ts]
            rows = [jnp.sum(d0 * x2, axis=0, keepdims=True), jnp.sum(d0 * x1, axis=0, keepdims=True),
                    jnp.sum(d0 * x0, axis=0, keepdims=True), jnp.sum(d0, axis=0, keepdims=True)]
            sub = lax.broadcasted_iota(jnp.int32, (8, w), 0)
            upd = jnp.zeros((8, w), F32)
            for k, rv in enumerate(rows):
                upd = jnp.where(sub == k, rv, upd)
            return upd[None]

        @pl.when(r == 0)
        def _():
            dwu_ref[...] = jnp.zeros_like(dwu_ref)
            dwg_ref[...] = jnp.zeros_like(dwg_ref)

        dwu_ref[...] += tap_grads(xu, dcu)
        dwg_ref[...] += tap_grads(xg, dcg)

    def cur(off):
        return pl.BlockSpec((ts, w), lambda j, r: (r, j + off))

    def prev(off):
        return pl.BlockSpec((hc, w), lambda j, r: (jnp.maximum(r * (ts // hc) - 1, 0), j + off))

    def nxt(off):
        return pl.BlockSpec((hc, w), lambda j, r: (jnp.minimum((r + 1) * (ts // hc), n_halo_blocks - 1), j + off))

    def wspec(off):
        return pl.BlockSpec((1, 8, w), lambda j, r: (j + off, 0, 0))

    hb = 2 * hc
    dact_next = pl.BlockSpec((hb, w), lambda j, r: (jnp.minimum((r + 1) * (ts // hb), nb * s // hb - 1), j))

    p = FF_PAIRS
    dh_spec = pl.BlockSpec((ts, w), lambda j, r: (r, j))
    dw_spec = pl.BlockSpec((1, 8, w), lambda j, r: (j, 0, 0))
    return pl.pallas_call(
        body, name=name, grid=(p, nb * nq),
        in_specs=[cur(0), prev(0), nxt(0), cur(p), prev(p), nxt(p), cur(0), dact_next, wspec(0), wspec(p)],
        out_specs=[dh_spec, dh_spec, dw_spec, dw_spec],
        out_shape=[SDS((nb * s, p * w), BF16), SDS((nb * s, p * w), BF16), SDS((p, 8, w), F32), SDS((p, 8, w), F32)],
        compiler_params=_params(("parallel", "arbitrary")))(h, h, h, h, h, h, dact, dact, cw, cw)


def ffn_up_gate(x_bf, wup, cw, nb, s, name):
    tm = min(TM, s)
    nq = s // tm
    w = FF_BLOCK_PAD
    hr = 2 * HALO_CONV
    k = x_bf.shape[1]

    def body(xc_ref, xh_ref, wu_ref, wg_ref, cu_ref, cg_ref, act_ref, a_ref, b_ref, hu_ref, hg_ref):
        first = (pl.program_id(1) % nq) == 0
        xc = xc_ref[...]
        xh = xh_ref[...]

        def products(w_ref):
            return (jnp.dot(xc, w_ref[...], preferred_element_type=F32), jnp.dot(xh, w_ref[...], preferred_element_type=F32))

        def conv(hcur, hprev, c_ref, h_out):
            h_out[...] = hcur.astype(BF16)
            xe = jnp.concatenate([jnp.where(first, 0.0, hprev), hcur], axis=0)
            return _conv_rows(xe, c_ref)[hr:]

        pu, pg = products(wu_ref), products(wg_ref)
        cu = conv(*pu, cu_ref, hu_ref)
        cg = conv(*pg, cg_ref, hg_ref)
        sg = _sigmoid(cg)
        a = cg * sg
        act_ref[...] = (a * cu).astype(BF16)
        a_ref[...] = a.astype(BF16)
        b_ref[...] = (cu * (sg * (1.0 + cg * (1.0 - sg)))).astype(BF16)

    def wblock(off):
        return pl.BlockSpec((k, w), lambda j, r: (0, j + off))

    def cblock(off):
        return pl.BlockSpec((1, 8, w), lambda j, r: (j + off, 0, 0))

    tile = pl.BlockSpec((tm, w), lambda j, r: (r, j))
    out = SDS((nb * s, FF_PAIRS * w), BF16)
    return pl.pallas_call(
        body, name=name, grid=(FF_PAIRS, nb * nq),
        in_specs=[pl.BlockSpec((tm, k), lambda j, r: (r, 0)),
                  pl.BlockSpec((hr, k), lambda j, r: (jnp.maximum(r * (tm // hr) - 1, 0), 0)),
                  wblock(0), wblock(FF_PAIRS), cblock(0), cblock(FF_PAIRS)],
        out_specs=[tile] * 5, out_shape=[out] * 5,
        compiler_params=_params(("parallel", "parallel")))(x_bf, x_bf, wup, wup, cw, cw)


def gate_conv_bwd(dact, a, b, hu, hg, cw, nb, s, name):
    ts = min(TS, s)
    nq = s // ts
    w = FF_BLOCK_PAD
    hc = HALO_CONV
    hb = 2 * hc
    n_ext = ts + hc

    def body(dc_ref, dn_ref, ac_ref, an_ref, bc_ref, bn_ref, hu_ref, hg_ref, wu_ref, wg_ref,
             dhu_ref, dhg_ref, dwu_ref, dwg_ref):
        r = pl.program_id(1)
        last = (r % nq) == nq - 1

        def ext(c_ref, n_ref):
            return jnp.concatenate([c_ref[...].astype(F32), n_ref[...].astype(F32)[:hc]], axis=0)

        da = jnp.where(last & (lax.broadcasted_iota(jnp.int32, (n_ext, w), 0) >= ts), 0.0, ext(dc_ref, dn_ref))

        def branch(dcv, w_ref, h_ref, dh_ref, dw_ref):
            d0 = dcv[:ts]
            d1 = pltpu.roll(dcv, n_ext - 1, axis=0)[:ts]
            d2 = pltpu.roll(dcv, n_ext - 2, axis=0)[:ts]
            dh_ref[...] = (w_ref[0, 2:3, :] * d0 + w_ref[0, 1:2, :] * d1 + w_ref[0, 0:1, :] * d2).astype(BF16)
            hv = h_ref[...].astype(F32)
            rows = [jnp.sum(d2 * hv, axis=0, keepdims=True), jnp.sum(d1 * hv, axis=0, keepdims=True),
                    jnp.sum(d0 * hv, axis=0, keepdims=True), jnp.sum(d0, axis=0, keepdims=True)]
            sub = lax.broadcasted_iota(jnp.int32, (8, w), 0)
            upd = jnp.zeros((8, w), F32)
            for kk, rv in enumerate(rows):
                upd = jnp.where(sub == kk, rv, upd)

            @pl.when(r == 0)
            def _():
                dw_ref[...] = jnp.zeros_like(dw_ref)

            dw_ref[...] += upd[None]

        branch(da * ext(ac_ref, an_ref), wu_ref, hu_ref, dhu_ref, dwu_ref)
        branch(da * ext(bc_ref, bn_ref), wg_ref, hg_ref, dhg_ref, dwg_ref)

    cur = pl.BlockSpec((ts, w), lambda j, r: (r, j))
    nxt = pl.BlockSpec((hb, w), lambda j, r: (jnp.minimum((r + 1) * (ts // hb), nb * s // hb - 1), j))

    def wspec(off):
        return pl.BlockSpec((1, 8, w), lambda j, r: (j + off, 0, 0))

    p = FF_PAIRS
    dw_spec = pl.BlockSpec((1, 8, w), lambda j, r: (j, 0, 0))
    return pl.pallas_call(
        body, name=name, grid=(p, nb * nq),
        in_specs=[cur, nxt, cur, nxt, cur, nxt, cur, cur, wspec(0), wspec(p)],
        out_specs=[cur, cur, dw_spec, dw_spec],
        out_shape=[SDS((nb * s, p * w), BF16), SDS((nb * s, p * w), BF16), SDS((p, 8, w), F32), SDS((p, 8, w), F32)],
        compiler_params=_params(("parallel", "arbitrary")))(dact, dact, a, a, b, b, hu, hg, cw, cw)


def _tri(n, upper):
    r = lax.broadcasted_iota(jnp.int32, (n, n), 0)
    c = lax.broadcasted_iota(jnp.int32, (n, n), 1)
    return ((r <= c) if upper else (r >= c)).astype(F32)


def fgate_fwd(fl, fb, nb, s):
    tc = min(TC, s)
    nq = s // tc

    def body(fl_ref, fb_ref, f_ref, carry):
        @pl.when(pl.program_id(1) == 0)
        def _():
            carry[...] = jnp.zeros_like(carry)

        z = fl_ref[...] + fb_ref[...]
        logf = jnp.minimum(z, 0.0) - jnp.log(1.0 + jnp.exp(-jnp.abs(z)))
        f_ref[...] = jnp.dot(_tri(tc, False), logf, preferred_element_type=F32,
                             precision=lax.Precision.HIGHEST) + carry[...]
        carry[...] += jnp.sum(logf, axis=0, keepdims=True)

    row = pl.BlockSpec((tc, LANES), lambda b, i: (b * nq + i, 0))
    return pl.pallas_call(
        body, name="fgate_fwd", grid=(nb, nq), in_specs=[row, pl.BlockSpec((1, LANES), lambda b, i: (0, 0))],
        out_specs=row, out_shape=SDS((nb * s, LANES), F32), scratch_shapes=[pltpu.VMEM((1, LANES), F32)],
        compiler_params=_params(("arbitrary", "arbitrary")))(fl, fb)


def fgate_bwd(d_cum_q, d_cum_k, fl, fb, nb, s):
    tc = min(TC, s)
    nq = s // tc

    def body(dfq_ref, dfk_ref, fl_ref, fb_ref, dfl_ref, dfb_ref, carry):
        b = pl.program_id(0)
        i = pl.program_id(1)

        @pl.when(i == 0)
        def _():
            carry[...] = jnp.zeros_like(carry)

        @pl.when(jnp.logical_and(b == 0, i == 0))
        def _():
            dfb_ref[...] = jnp.zeros_like(dfb_ref)

        dfv = dfq_ref[...] + dfk_ref[...]
        dlog = jnp.dot(_tri(tc, True), dfv, preferred_element_type=F32,
                       precision=lax.Precision.HIGHEST) + carry[...]
        carry[...] += jnp.sum(dfv, axis=0, keepdims=True)
        z = fl_ref[...] + fb_ref[...]
        dfl = dlog / (1.0 + jnp.exp(z))
        dfl_ref[...] = dfl
        dfb_ref[...] += jnp.sum(dfl, axis=0, keepdims=True)

    row = pl.BlockSpec((tc, LANES), lambda b, i: (b * nq + nq - 1 - i, 0))
    vec = pl.BlockSpec((1, LANES), lambda b, i: (0, 0))
    return pl.pallas_call(
        body, name="fgate_bwd", grid=(nb, nq), in_specs=[row, row, row, vec], out_specs=[row, vec],
        out_shape=[SDS((nb * s, LANES), F32), SDS((1, LANES), F32)], scratch_shapes=[pltpu.VMEM((1, LANES), F32)],
        compiler_params=_params(("arbitrary", "arbitrary")))(d_cum_q, d_cum_k, fl, fb)


PAIR = 2 * HEAD_DIM
N_PAIRS = FOX_HEADS // 2


def _lane_put(shape, h, col):
    lane = lax.broadcasted_iota(jnp.int32, shape, 1)
    return jnp.where(lane == h, col, 0.0)


def _half_masks(rows):
    lane = lax.broadcasted_iota(jnp.int32, (rows, PAIR), 1)
    return lane < HEAD_DIM


def _split_pair(x, scale=None):
    if scale is not None:
        x = x * scale
    lo = _half_masks(x.shape[0])
    zero = jnp.zeros_like(x)
    return jnp.where(lo, x, zero), jnp.where(lo, zero, x)


def _to_tile_rows(a, nb, s, tf):
    return a.reshape(nb * s // tf, tf, LANES)[:, :, :16].transpose(0, 2, 1)


def _from_tile_rows(a):
    tiles, _, tf = a.shape
    return jnp.pad(a.transpose(0, 2, 1), ((0, 0), (0, 0), (0, LANES - 16))).reshape(tiles * tf, LANES)


BIAS_TERMS = 3
LOOKAHEAD = 4
FOLLOW_FWD = 1
LOOKAHEAD_BWD = 2
FOLLOW_BWD = 1


def _bias_lane(h):
    return HEAD_DIM if h % 2 == 0 else 0


def _placement():
    rows = jnp.arange(LANES)[:, None]
    cols = jnp.arange(FOX_HEADS * PAIR)[None, :]
    head, lane = cols // PAIR, cols % PAIR
    first = jnp.where(head % 2 == 0, HEAD_DIM, 0)
    term = lane - first
    hit = (term >= 0) & (term < BIAS_TERMS) & (rows == 16 * term + head)
    return hit.astype(BF16)


def fox_prep(kv, fneg, nb, s):
    tf = min(TF, s)
    w = TOK_WIDTH

    def body(k_ref, v_ref, f_ref, pl_ref, ka_ref, vt_ref):
        lane = lax.broadcasted_iota(jnp.int32, (tf, LANES), 1)
        lo = lane < HEAD_DIM
        f = jnp.where(lane < FOX_HEADS, f_ref[...], 0.0)
        hi = f.astype(BF16).astype(F32)
        mid = (f - hi).astype(BF16).astype(F32)
        low = (f - hi - mid).astype(BF16).astype(F32)
        terms = (hi + pltpu.roll(mid, 16, axis=1) + pltpu.roll(low, 32, axis=1)).astype(BF16)
        placed = jnp.dot(terms, pl_ref[...], preferred_element_type=F32).astype(BF16)
        one = jnp.ones((tf, LANES), BF16)
        zero = jnp.zeros((tf, LANES), BF16)
        for p in range(N_PAIRS):
            kp = k_ref[:, p * PAIR:(p + 1) * PAIR] * QK_SCALE
            vp = v_ref[:, p * PAIR:(p + 1) * PAIR]
            he, ho = 2 * p, 2 * p + 1
            ka_ref[:, he * PAIR:(he + 1) * PAIR] = jnp.where(lo, kp, placed[:, he * PAIR:(he + 1) * PAIR])
            ka_ref[:, ho * PAIR:(ho + 1) * PAIR] = jnp.where(lo, placed[:, ho * PAIR:(ho + 1) * PAIR], kp)
            ve = jnp.where(lo, vp, jnp.where(lane == HEAD_DIM, one, zero))
            vo = jnp.where(lo, jnp.where(lane == 0, one, zero), vp)
            vt_ref[0, he * PAIR:(he + 1) * PAIR, :] = ve.astype(F32).T.astype(BF16)
            vt_ref[0, ho * PAIR:(ho + 1) * PAIR, :] = vo.astype(F32).T.astype(BF16)

    return pl.pallas_call(
        body, name="fox_prep", grid=(nb * s // tf,),
        in_specs=[pl.BlockSpec((tf, w), lambda r: (r, 0)), pl.BlockSpec((tf, w), lambda r: (r, 1)),
                  pl.BlockSpec((tf, LANES), lambda r: (r, 0)), pl.BlockSpec((LANES, FOX_HEADS * PAIR), lambda r: (0, 0))],
        out_specs=[pl.BlockSpec((tf, FOX_HEADS * PAIR), lambda r: (r, 0)),
                   pl.BlockSpec((1, FOX_HEADS * PAIR, tf), lambda r: (r, 0, 0))],
        out_shape=[SDS((nb * s, FOX_HEADS * PAIR), BF16), SDS((nb * s // tf, FOX_HEADS * PAIR, tf), BF16)],
        compiler_params=_params(("parallel",)))(kv, kv, fneg, _placement())


def fox_fwd_t(pq, kaug, vaug_t, nb, s):
    tf = min(TF, s)
    n = s // tf
    w = TOK_WIDTH
    wa = FOX_HEADS * PAIR

    def body(q_ref, k_hbm, vt_hbm, ob_ref, of_ref, lse_ref, k_vm, vt_vm, qx_scr, m_scr, acc_scr, sems):
        b = pl.program_id(0)
        i = pl.program_id(1)

        @pl.when(i == 0)
        def _():
            ck = pltpu.make_async_copy(k_hbm.at[pl.ds(pl.multiple_of(b * s, tf), s)], k_vm, sems.at[0])
            cv = pltpu.make_async_copy(vt_hbm.at[pl.ds(b * n, n)], vt_vm, sems.at[1])
            ck.start()
            cv.start()
            ck.wait()
            cv.wait()

        lane = lax.broadcasted_iota(jnp.int32, (tf, PAIR), 1)
        one = jnp.ones((tf, PAIR), BF16)
        zero = jnp.zeros((tf, PAIR), BF16)
        for p in range(N_PAIRS):
            qp = q_ref[:, p * PAIR:(p + 1) * PAIR]
            be, bo = _bias_lane(2 * p), _bias_lane(2 * p + 1)
            ones_e = jnp.where((lane >= be) & (lane < be + BIAS_TERMS), one, zero)
            ones_o = jnp.where((lane >= bo) & (lane < bo + BIAS_TERMS), one, zero)
            qx_scr[2 * p] = jnp.where(lane < HEAD_DIM, qp, ones_e)
            qx_scr[2 * p + 1] = jnp.where(lane < HEAD_DIM, ones_o, qp)
        m_scr[...] = jnp.full(m_scr.shape, NEG_BIG, F32)
        acc_scr[...] = jnp.zeros_like(acc_scr)

        def tile(j, masked):
            ks = pl.multiple_of(j * tf, tf)
            if masked:
                keep = lax.broadcasted_iota(jnp.int32, (tf, tf), 1) >= lax.broadcasted_iota(jnp.int32, (tf, tf), 0)
            def scores(h):
                kx = k_vm[pl.ds(ks, tf), h * PAIR:(h + 1) * PAIR]
                return lax.dot_general(kx, qx_scr[h], NT_DIMS, preferred_element_type=F32)

            def values(h, pr, a):
                pv = jnp.dot(vt_vm[j, h * PAIR:(h + 1) * PAIR, :], pr, preferred_element_type=F32)
                acc_scr[h] = a * acc_scr[h] + pv

            ahead = [scores(h) for h in range(LOOKAHEAD)]
            behind = []
            for h in range(FOX_HEADS):
                sc = ahead.pop(0)
                if h + LOOKAHEAD < FOX_HEADS:
                    ahead.append(scores(h + LOOKAHEAD))
                if masked:
                    sc = jnp.where(keep, sc, NEG_BIG)
                m_prev = m_scr[h]
                m_new = jnp.maximum(m_prev, jnp.max(sc, axis=0, keepdims=True))
                m_scr[h] = m_new
                behind.append((h, jnp.exp(sc - m_new).astype(BF16), jnp.exp(m_prev - m_new)))
                if len(behind) > FOLLOW_FWD:
                    values(*behind.pop(0))
            for item in behind:
                values(*item)

        def step(j, carry):
            tile(j, False)
            return carry

        lax.fori_loop(0, i, step, 0)
        tile(i, True)

        top = lax.broadcasted_iota(jnp.int32, (PAIR, tf), 0) < HEAD_DIM
        sub = lax.broadcasted_iota(jnp.int32, (16, tf), 0)
        lse = jnp.zeros((16, tf), F32)
        for p in range(N_PAIRS):
            he, ho = 2 * p, 2 * p + 1
            le = acc_scr[he, HEAD_DIM:HEAD_DIM + 1, :]
            lod = acc_scr[ho, 0:1, :]
            o = jnp.where(top, acc_scr[he] / le, acc_scr[ho] / lod).T
            ob_ref[:, p * PAIR:(p + 1) * PAIR] = o.astype(BF16)
            of_ref[:, p * PAIR:(p + 1) * PAIR] = o
            lse = jnp.where(sub == he, m_scr[he] + jnp.log(le), lse)
            lse = jnp.where(sub == ho, m_scr[ho] + jnp.log(lod), lse)
        lse_ref[0] = lse

    qrow = lambda b, i: (b * n + i, 0)
    return pl.pallas_call(
        body, name="fox_fwd", grid=(nb, n),
        in_specs=[pl.BlockSpec((tf, w), qrow), ANY_SPEC, ANY_SPEC],
        out_specs=[pl.BlockSpec((tf, w), qrow), pl.BlockSpec((tf, w), qrow),
                   pl.BlockSpec((1, 16, tf), lambda b, i: (b * n + i, 0, 0))],
        out_shape=[SDS((nb * s, w), BF16), SDS((nb * s, w), F32), SDS((nb * n, 16, tf), F32)],
        scratch_shapes=[pltpu.VMEM((s, wa), BF16), pltpu.VMEM((n, wa, tf), BF16),
                        pltpu.VMEM((FOX_HEADS, tf, PAIR), BF16), pltpu.VMEM((FOX_HEADS, 1, tf), F32),
                        pltpu.VMEM((FOX_HEADS, PAIR, tf), F32), pltpu.SemaphoreType.DMA((2,))],
        compiler_params=_params(("arbitrary", "arbitrary")))(pq, kaug, vaug_t)


def fox_delta(dcat, o, nb, s):
    tf = min(TM, s)
    w = TOK_WIDTH

    def body(do_ref, o_ref, dl_ref):
        out = jnp.zeros((tf, LANES), F32)
        for h in range(FOX_HEADS):
            lo, hi = h * HEAD_DIM, (h + 1) * HEAD_DIM
            out = out + _lane_put((tf, LANES), h, jnp.sum(do_ref[:, lo:hi] * o_ref[:, lo:hi], axis=1, keepdims=True))
        dl_ref[...] = out

    row = pl.BlockSpec((tf, w), lambda r: (r, 0))
    return pl.pallas_call(
        body, name="fox_delta", grid=(nb * s // tf,), in_specs=[row, row],
        out_specs=pl.BlockSpec((tf, LANES), lambda r: (r, 0)), out_shape=SDS((nb * s, LANES), F32),
        compiler_params=_params(("parallel",)))(dcat, o)


def fox_bwd(pq, kv, fneg, dcat_bf, lse_rows, delta_rows, nb, s):
    tf = min(TF, s)
    n = s // tf
    w = TOK_WIDTH

    def body(q_hbm, k_ref, v_ref, f_ref, do_hbm, lse_ref, dl_ref, dq_ref, dk_ref, dv_ref, dfk_ref, dfq_ref,
             q_vm, do_vm, km_scr, vm_scr, kt_scr, fk_scr, dk_scr, dv_scr, rs_scr, dq_scr, fq_scr, sems):
        b = pl.program_id(0)
        j = pl.program_id(1)

        @pl.when(j == 0)
        def _():
            rows = pl.ds(pl.multiple_of(b * s, tf), s)
            cq = pltpu.make_async_copy(q_hbm.at[rows, pl.ds(0, w)], q_vm, sems.at[0])
            cd = pltpu.make_async_copy(do_hbm.at[rows, pl.ds(0, w)], do_vm, sems.at[1])
            cq.start()
            cd.start()
            dq_scr[...] = jnp.zeros_like(dq_scr)
            fq_scr[...] = jnp.zeros_like(fq_scr)
            cq.wait()
            cd.wait()

        for p in range(N_PAIRS):
            kp = k_ref[:, p * PAIR:(p + 1) * PAIR] * QK_SCALE
            ke, ko = _split_pair(kp)
            km_scr[2 * p] = ke
            km_scr[2 * p + 1] = ko
            kt_scr[p] = kp.astype(F32).T.astype(BF16)
            ve, vo = _split_pair(v_ref[:, p * PAIR:(p + 1) * PAIR])
            vm_scr[2 * p] = ve
            vm_scr[2 * p + 1] = vo
        for h in range(FOX_HEADS):
            fk_scr[h] = jnp.broadcast_to(f_ref[:, h:h + 1], (tf, tf))
        dk_scr[...] = jnp.zeros_like(dk_scr)
        dv_scr[...] = jnp.zeros_like(dv_scr)
        rs_scr[...] = jnp.zeros_like(rs_scr)

        def tile(i, masked):
            qs = pl.multiple_of(i * tf, tf)
            if masked:
                keep = lax.broadcasted_iota(jnp.int32, (tf, tf), 1) >= lax.broadcasted_iota(jnp.int32, (tf, tf), 0)
            def products(h):
                qp = q_vm[pl.ds(qs, tf), (h // 2) * PAIR:(h // 2 + 1) * PAIR]
                dop = do_vm[pl.ds(qs, tf), (h // 2) * PAIR:(h // 2 + 1) * PAIR]
                return (lax.dot_general(km_scr[h], qp, NT_DIMS, preferred_element_type=F32),
                        lax.dot_general(vm_scr[h], dop, NT_DIMS, preferred_element_type=F32))

            def dependents(h, prb, dsb):
                p = h // 2
                half = slice((h % 2) * HEAD_DIM, (h % 2 + 1) * HEAD_DIM)
                qp = q_vm[pl.ds(qs, tf), p * PAIR:(p + 1) * PAIR]
                dop = do_vm[pl.ds(qs, tf), p * PAIR:(p + 1) * PAIR]
                dv_scr[h] += jnp.dot(prb, dop, preferred_element_type=F32)
                dk_scr[h] += jnp.dot(dsb, qp, preferred_element_type=F32)
                dqt = jnp.dot(kt_scr[p], dsb, preferred_element_type=F32)
                dq_scr[i, p, half, :] += dqt[(h % 2) * HEAD_DIM:(h % 2 + 1) * HEAD_DIM]

            ahead = [products(h) for h in range(LOOKAHEAD_BWD)]
            behind = []
            for h in range(FOX_HEADS):
                sc, dp = ahead.pop(0)
                if h + LOOKAHEAD_BWD < FOX_HEADS:
                    ahead.append(products(h + LOOKAHEAD_BWD))
                sc = sc + fk_scr[h] - lse_ref[i, h:h + 1, :]
                if masked:
                    sc = jnp.where(keep, sc, NEG_BIG)
                pr = jnp.exp(sc)
                ds = pr * (dp - dl_ref[i, h:h + 1, :])
                part = ds[:, :LANES]
                for c in range(1, tf // LANES):
                    part = part + ds[:, c * LANES:(c + 1) * LANES]
                rs_scr[h] += part
                fq_scr[i, h:h + 1, :] += jnp.sum(ds, axis=0, keepdims=True)
                behind.append((h, pr.astype(BF16), ds.astype(BF16)))
                if len(behind) > FOLLOW_BWD:
                    dependents(*behind.pop(0))
            for item in behind:
                dependents(*item)

        def step(i, carry):
            tile(i, False)
            return carry

        tile(j, True)
        for p in range(N_PAIRS):
            dq_ref[:, p * PAIR:(p + 1) * PAIR] = dq_scr[j, p].T.astype(BF16)
        dfq_ref[0] = fq_scr[j]
        lax.fori_loop(j + 1, n, step, 0)

        lo = _half_masks(tf)
        dfk = jnp.zeros((tf, LANES), F32)
        for p in range(N_PAIRS):
            dk = jnp.where(lo, dk_scr[2 * p], dk_scr[2 * p + 1]) * QK_SCALE
            dk_ref[:, p * PAIR:(p + 1) * PAIR] = dk.astype(BF16)
            dv_ref[:, p * PAIR:(p + 1) * PAIR] = jnp.where(lo, dv_scr[2 * p], dv_scr[2 * p + 1]).astype(BF16)
            for h in (2 * p, 2 * p + 1):
                dfk = dfk - _lane_put((tf, LANES), h, jnp.sum(rs_scr[h], axis=1, keepdims=True))
        dfk_ref[...] = dfk

    krow = lambda b, j: (b * n + j, 0)
    rows = pl.BlockSpec((n, 16, tf), lambda b, j: (b, 0, 0))
    tile_out = pl.BlockSpec((tf, w), krow)
    return pl.pallas_call(
        body, name="fox_bwd", grid=(nb, n),
        in_specs=[ANY_SPEC, pl.BlockSpec((tf, w), krow), pl.BlockSpec((tf, w), lambda b, j: (b * n + j, 1)),
                  pl.BlockSpec((tf, LANES), krow), ANY_SPEC, rows, rows],
        out_specs=[tile_out, tile_out, tile_out, pl.BlockSpec((tf, LANES), krow),
                   pl.BlockSpec((1, 16, tf), lambda b, j: (b * n + j, 0, 0))],
        out_shape=[SDS((nb * s, w), BF16), SDS((nb * s, w), BF16), SDS((nb * s, w), BF16), SDS((nb * s, LANES), F32),
                   SDS((nb * n, 16, tf), F32)],
        scratch_shapes=[pltpu.VMEM((s, w), BF16), pltpu.VMEM((s, w), BF16),
                        pltpu.VMEM((FOX_HEADS, tf, PAIR), BF16), pltpu.VMEM((FOX_HEADS, tf, PAIR), BF16),
                        pltpu.VMEM((N_PAIRS, PAIR, tf), BF16), pltpu.VMEM((FOX_HEADS, tf, tf), F32),
                        pltpu.VMEM((FOX_HEADS, tf, PAIR), F32), pltpu.VMEM((FOX_HEADS, tf, PAIR), F32),
                        pltpu.VMEM((FOX_HEADS, tf, LANES), F32), pltpu.VMEM((n, N_PAIRS, PAIR, tf), F32),
                        pltpu.VMEM((n, 16, tf), F32), pltpu.SemaphoreType.DMA((2,))],
        compiler_params=_params(("arbitrary", "arbitrary")))(pq, kv, kv, fneg, dcat_bf, lse_rows, delta_rows)


def reduce_adamw(parts, w, m, v, name):
    _, r, c = parts.shape
    tr = r
    for cand in range(16, r, 16):
        if r % cand == 0 and cand * c <= 128 * 1024:
            tr = cand
    c1 = 1.0 - ADAM_B1 ** ADAM_STEP
    c2 = 1.0 - ADAM_B2 ** ADAM_STEP

    def body(p_ref, w_ref, m_ref, v_ref, g_out, d_out, m_out, v_out):
        g = p_ref[0].astype(F32)
        for k in range(1, N_DEV):
            g = g + p_ref[k].astype(F32)
        mn = ADAM_B1 * m_ref[...] + (1.0 - ADAM_B1) * g
        vn = ADAM_B2 * v_ref[...] + (1.0 - ADAM_B2) * (g * g)
        g_out[...] = g
        m_out[...] = mn
        v_out[...] = vn
        d_out[...] = -ADAM_LR * ((mn / c1) / (jnp.sqrt(vn / c2) + ADAM_EPS) + ADAM_WD * w_ref[...])

    row = pl.BlockSpec((tr, c), lambda i: (i, 0))
    return pl.pallas_call(
        body, name=name, grid=(r // tr,),
        in_specs=[pl.BlockSpec((N_DEV, tr, c), lambda i: (0, i, 0)), row, row, row],
        out_specs=[row, row, row, row], out_shape=[SDS((r, c), F32)] * 4,
        compiler_params=_params(("parallel",)))(parts, w, m, v)


N_PEERS = N_DEV - 1
HBM_SPEC = pl.BlockSpec(memory_space=pltpu.HBM)
SEM_SPEC = pl.BlockSpec(memory_space=pltpu.SEMAPHORE)
ANY_SPEC = pl.BlockSpec(memory_space=pl.ANY)
SPLIT_EFFECT = pltpu.SideEffectType.DATAFLOW_SIDE_EFFECTING


def _my_index():
    return 4 * lax.axis_index("x") + 2 * lax.axis_index("y") + lax.axis_index("c")


def _peers():
    x, y, c = lax.axis_index("x"), lax.axis_index("y"), lax.axis_index("c")
    peers = []
    for k in range(1, N_DEV):
        px = 1 - x if (k >> 2) & 1 else x
        py = 1 - y if (k >> 1) & 1 else y
        pc = 1 - c if k & 1 else c
        peers.append(((px, py, pc), 4 * px + 2 * py + pc))
    return 4 * x + 2 * y + c, peers


def _push(src, dst, send_sems, recv_sems, slot, dev):
    return pltpu.make_async_remote_copy(src_ref=src, dst_ref=dst, send_sem=send_sems.at[slot], recv_sem=recv_sems.at[slot],
                                        device_id=dev, device_id_type=pl.DeviceIdType.MESH)


def _landing_shapes(arrs, scatter):
    return [SDS((N_DEV,) + tuple(a.shape[1:] if sc else a.shape), a.dtype) for a, sc in zip(arrs, scatter)]


def exchange(arrs, scatter, name):
    na = len(arrs)

    def body(*refs):
        ins = refs[:na]
        outs = refs[na:2 * na]
        send_sems, recv_sems, local_sems = refs[2 * na:]
        me, peers = _peers()
        local = []
        remote = []
        for a in range(na):
            lc = pltpu.make_async_copy(ins[a].at[me] if scatter[a] else ins[a], outs[a].at[me], local_sems.at[a])
            lc.start()
            local.append(lc)
            for k, (dev, idx) in enumerate(peers):
                cp = _push(ins[a].at[idx] if scatter[a] else ins[a], outs[a].at[me], send_sems, recv_sems,
                           a * N_PEERS + k, dev)
                cp.start()
                remote.append(cp)
        for a in range(na):
            for k, (dev, idx) in enumerate(peers):
                _push(ins[a].at[me] if scatter[a] else ins[a], outs[a].at[idx], send_sems, recv_sems,
                      a * N_PEERS + k, dev).wait_recv()
        for cp in remote:
            cp.wait_send()
        for lc in local:
            lc.wait()

    return pl.pallas_call(
        body, name=name, in_specs=[HBM_SPEC] * na, out_specs=[HBM_SPEC] * na, out_shape=_landing_shapes(arrs, scatter),
        scratch_shapes=[pltpu.SemaphoreType.DMA((na * N_PEERS,)), pltpu.SemaphoreType.DMA((na * N_PEERS,)),
                        pltpu.SemaphoreType.DMA((na,))])(*arrs)


def exchange_start(arrs, scatter, after, name):
    na = len(arrs)
    lands = [lax.empty(l.shape, l.dtype) for l in _landing_shapes(arrs, scatter)]

    def body(*refs):
        ins = refs[:na]
        land = refs[na:2 * na]
        send_sems, recv_sems = refs[2 * na + 1], refs[2 * na + 2]
        token = refs[-1]
        me, peers = _peers()
        for a in range(na):
            for k, (dev, idx) in enumerate(peers):
                _push(ins[a].at[idx] if scatter[a] else ins[a], land[a].at[me], send_sems, recv_sems,
                      a * N_PEERS + k, dev).start()
        token[...] = jnp.zeros_like(token)

    thru = [pltpu.HBM(a.shape, a.dtype) for a in arrs] + [pltpu.HBM(l.shape, l.dtype) for l in lands]
    res = pl.pallas_call(
        body, name=name,
        out_shape=(pltpu.SemaphoreType.DMA((na * N_PEERS,)), pltpu.SemaphoreType.DMA((na * N_PEERS,)), *thru,
                   SDS((8, LANES), F32)),
        in_specs=[HBM_SPEC] * (2 * na) + [ANY_SPEC],
        out_specs=(SEM_SPEC, SEM_SPEC, *([HBM_SPEC] * (2 * na)), pl.BlockSpec(memory_space=pltpu.VMEM)),
        input_output_aliases={i: 2 + i for i in range(2 * na)},
        compiler_params=pltpu.CompilerParams(has_side_effects=SPLIT_EFFECT),
    )(*[pltpu.with_memory_space_constraint(a, pltpu.HBM) for a in arrs],
      *[pltpu.with_memory_space_constraint(l, pltpu.HBM) for l in lands], after)
    return {"send": res[0], "recv": res[1], "src": res[2:2 + na], "land": res[2 + na:2 + 2 * na],
            "token": res[-1][0, 0], "scatter": scatter}


def exchange_wait(handle, after, name):
    scatter = handle["scatter"]
    na = len(scatter)

    def body(*refs):
        src = refs[:na]
        land = refs[na:2 * na]
        send_sems, recv_sems = refs[2 * na], refs[2 * na + 1]
        me, peers = _peers()
        for a in range(na):
            for k, (dev, idx) in enumerate(peers):
                cp = _push(src[a].at[me] if scatter[a] else src[a], land[a].at[idx], send_sems, recv_sems,
                           a * N_PEERS + k, dev)
                cp.wait_send()
                cp.wait_recv()

    ops = list(handle["src"]) + list(handle["land"])
    res = pl.pallas_call(
        body, name=name, out_shape=tuple(pltpu.HBM(o.shape, o.dtype) for o in ops),
        in_specs=[HBM_SPEC] * (2 * na) + [SEM_SPEC, SEM_SPEC, ANY_SPEC], out_specs=tuple([HBM_SPEC] * (2 * na)),
        input_output_aliases={i: i for i in range(2 * na)},
        compiler_params=pltpu.CompilerParams(has_side_effects=SPLIT_EFFECT),
    )(*ops, handle["send"], handle["recv"], after)
    me = _my_index()
    out = []
    for a in range(na):
        own = lax.dynamic_index_in_dim(res[a], me, 0, keepdims=True) if scatter[a] else res[a][None]
        out.append(lax.dynamic_update_slice(res[na + a], own, (me,) + (0,) * (own.ndim - 1)))
    return out


def forward_layer(l, xin, xin_bf, mem_bf, wt, nb, s, ffn_weights=None):
    sv = {"xin_bf": xin_bf}
    memkv = mm_nn(mem_bf, wt["memw"], BF16, f"memkv{l}")
    sv["memkv"] = memkv
    if l == 0:
        proj = mm_nn(xin_bf, wt["win_a"], F32, "proj_a")
        pooled, tok = pool_fwd(proj, wt["pw_bd"], wt["pscale"], nb, s)
        sv["pooled"] = pooled
    else:
        kv = mm_nn(xin_bf, wt["kvw"][:, :2 * TOK_WIDTH], BF16, "kv_proj")
        fl = mm_nn(xin_bf, wt["kvw"][:, 2 * TOK_WIDTH:], F32, "gate_proj")
        fneg = -fgate_fwd(fl, wt["fb"], nb, s)
        proj = mm_nn(xin_bf, wt["wq"], BF16, "proj_b")
        kaug, vaug_t = fox_prep(kv, fneg, nb, s)
        tok, o_f32, lse_rows = fox_fwd_t(proj, kaug, vaug_t, nb, s)
        sv.update(kv=kv, fl=fl, fneg=fneg, o_f32=o_f32, lse_rows=lse_rows)
    sv["proj"] = proj
    mem_out = memattn_fwd(proj, memkv, nb, s, f"memattn_fwd{l}")
    cat = jnp.concatenate([tok, mem_out], axis=1)
    sv["cat"] = cat
    mix = mm_nn(cat, wt["wout"], F32, f"out_proj{l}")
    x1, x1_bf, xh1, rs1 = ln_fwd(xin, mix, wt["ln1_g"], wt["ln1_b"], f"ln1_fwd{l}")
    sv.update(x1_bf=x1_bf, xh1=xh1, rs1=rs1)
    if ffn_weights is not None:
        wt.update(ffn_weights(x1_bf))
    act, ga, gb, hu, hg = ffn_up_gate(x1_bf, wt["wup"], wt["cw"], nb, s, f"ffn_up_gate{l}")
    sv.update(act=act, ga=ga, gb=gb, hu=hu, hg=hg)
    ffn = mm_nn(act, wt["wdown"], F32, f"ffn_down{l}")
    x2, x2_bf, xh2, rs2 = ln_fwd(x1, ffn, wt["ln2_g"], wt["ln2_b"], f"ln2_fwd{l}")
    sv.update(xh2=xh2, rs2=rs2)
    return x2, x2_bf, sv


def backward_layer(l, dy, sv, mem_bf, wt, nb, s, after_ffn=None):
    g = {}
    dr2, dr2_bf, g["ln2_g"], g["ln2_b"] = ln_bwd(dy, sv["xh2"], sv["rs2"], wt["ln2_g"], f"ln2_bwd{l}")
    dact = mm_nn(dr2_bf, wt["wdown_t"], BF16, f"ffn_down_dx{l}")
    g["wdown"] = mm_tn(sv["act"], dr2_bf, f"ffn_down_dw{l}")
    dh_u, dh_g, dcw_u, dcw_g = gate_conv_bwd(dact, sv["ga"], sv["gb"], sv["hu"], sv["hg"], wt["cw"], nb, s,
                                             f"gate_conv_bwd{l}")
    g["cw"] = jnp.concatenate([dcw_u, dcw_g], axis=0)
    half = FF_PAIRS * FF_BLOCK_PAD
    dx1 = mm_nn(dh_u, wt["wup_t"][:half], F32, f"ffn_up_dx_u{l}", addend=dr2, add_scale=DN_ALPHA)
    dx1 = mm_nn(dh_g, wt["wup_t"][half:], F32, f"ffn_up_dx_g{l}", addend=dx1)
    g["wup"] = jnp.concatenate([mm_tn(sv["x1_bf"], dh_u, f"ffn_up_dw_u{l}", blocked=True),
                                mm_tn(sv["x1_bf"], dh_g, f"ffn_up_dw_g{l}", blocked=True)], axis=0)
    ln1_g = wt["ln1_g"] if after_ffn is None else wt["ln1_g"] + after_ffn(g, dx1)
    dr1, dr1_bf, g["ln1_g"], g["ln1_b"] = ln_bwd(dx1, sv["xh1"], sv["rs1"], ln1_g, f"ln1_bwd{l}")
    dcat, dcat_bf = mm_nn(dr1_bf, wt["wout_t"], F32, f"out_proj_dx{l}", also_bf16=True)
    g["wout"] = mm_tn(sv["cat"], dr1_bf, f"out_proj_dw{l}")
    dqm, dmemkv = memattn_bwd(sv["proj"], sv["memkv"], dcat, nb, s, f"memattn_bwd{l}")
    g["memw"] = mm_tn(mem_bf, dmemkv, f"memkv_dw{l}")
    if l == 0:
        dmixed, dpooled, g["pscale"] = pool_bwd_mix(dcat, sv["pooled"], wt["pw_bd"], wt["pw_bd_t"], wt["pscale"], nb, s)
        g["pw_full"] = mm_tn(sv["pooled"], dmixed, "pool_dw")
        du = pool_bwd_window(dpooled, nb, s)
        dproj = jnp.concatenate([du, dqm], axis=1)
        dx = mm_nn(dproj, wt["win_a_t"], F32, "proj_a_dx", addend=dr1, add_scale=DN_ALPHA)
        g["win_a"] = mm_tn(sv["xin_bf"], dproj, "proj_a_dw")
    else:
        delta = fox_delta(dcat, sv["o_f32"], nb, s)
        tf = min(TF, s)
        dq, dk, dv, dfcum_k, dfq_rows = fox_bwd(sv["proj"], sv["kv"], sv["fneg"], dcat_bf,
                                                sv["lse_rows"], _to_tile_rows(delta, nb, s, tf), nb, s)
        dfl, g["fb"] = fgate_bwd(_from_tile_rows(dfq_rows), dfcum_k, sv["fl"], wt["fb"], nb, s)
        dproj = jnp.concatenate([dq, dqm], axis=1)
        dkvf = jnp.concatenate([dk, dv, dfl.astype(BF16)], axis=1)
        dx = mm_nn(dproj, wt["wq_t"], F32, "proj_b_dx", addend=dr1, add_scale=DN_ALPHA)
        dx = mm_nn(dkvf, wt["kvw_t"], F32, "kv_proj_dx", addend=dx)
        g["wq"] = mm_tn(sv["xin_bf"], dproj, "proj_b_dw")
        g["kvw"] = mm_tn(sv["xin_bf"], dkvf, "kv_proj_dw")
    return dx, g


def pack_replicated(pool_w, ln1_g, ln1_b, ln2_g, ln2_b, conv_b, f_b):
    cb = jnp.pad(conv_b, ((0, 0), (0, 6144 - 5504))).reshape(12, D_MODEL)
    fb = jnp.pad(f_b.reshape(1, FOX_HEADS), ((0, 3), (0, D_MODEL - FOX_HEADS)))
    return jnp.concatenate([pool_w.reshape(144, D_MODEL), ln1_g, ln1_b, ln2_g, ln2_b, cb, fb], axis=0)


def unpack_replicated(buf):
    pool_w = buf[:144].reshape(1, 4, POOL_GROUP, POOL_GROUP)
    ln = [buf[144 + 2 * k:146 + 2 * k] for k in range(4)]
    conv_b = buf[152:164].reshape(2, 6144)[:, :5504]
    f_b = buf[164, :FOX_HEADS]
    return pool_w, ln[0], ln[1], ln[2], ln[3], conv_b, f_b


def pack_small(conv_w, pool_scale):
    buf = jnp.zeros((16, FF_BLOCK_PAD), F32)
    buf = lax.dynamic_update_slice(buf, conv_w.reshape(DEPTH * 3, FF_BLOCK), (0, 0))
    return lax.dynamic_update_slice(buf, pool_scale, (8, 0))


def _block_diag(pw):
    out = jnp.zeros((TOK_WIDTH, TOK_WIDTH), pw.dtype)
    for g in range(4):
        out = lax.dynamic_update_slice(out, pw[g], (g * POOL_GROUP, g * POOL_GROUP))
    return out


def layer_shards(l, sq_a, sq_b, mem_w_kv, ffn_w_up, ffn_w_down):
    return [sq_a[0].astype(BF16), sq_b[0].astype(BF16), mem_w_kv[l].astype(BF16), ffn_w_up[l].astype(BF16),
            ffn_w_down[l].astype(BF16)]


def mixer_weights(l, gath, ln1_g, ln1_b, ln2_g, ln2_b):
    w_out = gath[1].reshape(D_MODEL, D_MODEL)
    wt = {"memw": gath[2].reshape(D_MODEL, 2 * MEM_WIDTH), "wout": w_out, "wout_t": w_out.T,
          "ln1_g": ln1_g[l:l + 1], "ln1_b": ln1_b[l:l + 1], "ln2_g": ln2_g[l:l + 1], "ln2_b": ln2_b[l:l + 1]}
    return wt, gath[0].reshape(D_MODEL, D_MODEL)


def ffn_weights(l, wup_g, wdown_g, small, conv_b):
    pad_c = FF_BLOCK_PAD - FF_BLOCK
    wup = jnp.pad(wup_g, ((0, 0), (0, 0), (0, pad_c))).transpose(1, 0, 2).reshape(D_MODEL, N_DEV * FF_BLOCK_PAD)
    wdown = jnp.pad(wdown_g.reshape(FF_PAIRS, FF_BLOCK, D_MODEL), ((0, 0), (0, pad_c), (0, 0)))
    wdown = wdown.reshape(FF_PAIRS * FF_BLOCK_PAD, D_MODEL)
    cb = jnp.pad(conv_b[l].reshape(N_DEV, FF_BLOCK), ((0, 0), (0, pad_c)))
    cw = jnp.concatenate([small[:, 3 * l:3 * l + 3, :], cb[:, None, :], jnp.zeros((N_DEV, 4, FF_BLOCK_PAD), F32)], axis=1)
    return {"wup": wup, "wup_t": wup.T, "wdown": wdown, "wdown_t": wdown.T, "cw": cw}


def mixer_grad_blocks(g, w_in_grad):
    blocks = [w_in_grad.reshape(N_DEV, 128, D_MODEL), g["wout"].reshape(N_DEV, 128, D_MODEL),
              g["memw"].reshape(N_DEV, 128, 2 * MEM_WIDTH)]
    return [b.astype(BF16) for b in blocks]


def ffn_grad_blocks(g):
    wup = g["wup"][:, :, :FF_BLOCK]
    wdown = g["wdown"].reshape(FF_PAIRS, FF_BLOCK_PAD, D_MODEL)[:, :FF_BLOCK].reshape(N_DEV, FF_ROWS, D_MODEL)
    return [wup.astype(BF16), wdown.astype(BF16)]


def small_grad_blocks(g0, g1):
    taps = jnp.stack([g0["cw"][:, :3, :], g1["cw"][:, :3, :]], axis=1).reshape(N_DEV, DEPTH * 3, FF_BLOCK_PAD)
    small = jnp.zeros((N_DEV, 16, FF_BLOCK_PAD), F32)
    small = lax.dynamic_update_slice(small, taps, (0, 0, 0))
    return lax.dynamic_update_slice(small, g0["pscale"].reshape(N_DEV, 1, 96), (0, 8, 0))


def replicated_grads(g0, g1):
    pw = jnp.stack([g0["pw_full"][k * POOL_GROUP:(k + 1) * POOL_GROUP, k * POOL_GROUP:(k + 1) * POOL_GROUP] for k in range(4)])
    conv_b = jnp.stack([g_["cw"][:, 3, :FF_BLOCK].reshape(N_DEV * FF_BLOCK) for g_ in (g0, g1)])
    ln = [jnp.concatenate([g0[n], g1[n]], axis=0) for n in ("ln1_g", "ln1_b", "ln2_g", "ln2_b")]
    return pack_replicated(pw[None], ln[0], ln[1], ln[2], ln[3], conv_b, g1["fb"][0, :FOX_HEADS])


def kernel(x, mem, a_w_in, a_pool_w, a_pool_scale, a_w_out, b_w_q, b_w_out, kv_w, f_b, mem_w_kv, ln1_g, ln1_b, ln2_g, ln2_b, ffn_w_up, ffn_conv_w, ffn_conv_b, ffn_w_down, loss_target, m_a_w_in, m_a_pool_w, m_a_pool_scale, m_a_w_out, m_b_w_q, m_b_w_out, m_kv_w, m_f_b, m_mem_w_kv, m_ln1_g, m_ln1_b, m_ln2_g, m_ln2_b, m_ffn_w_up, m_ffn_conv_w, m_ffn_conv_b, m_ffn_w_down, v_a_w_in, v_a_pool_w, v_a_pool_scale, v_a_w_out, v_b_w_q, v_b_w_out, v_kv_w, v_f_b, v_mem_w_kv, v_ln1_g, v_ln1_b, v_ln2_g, v_ln2_b, v_ffn_w_up, v_ffn_conv_w, v_ffn_conv_b, v_ffn_w_down):
    nb, s, d = x.shape
    t = nb * s
    x2d, mem_bf, target = x.reshape(t, d), mem.reshape(nb * MEM_LEN, d).astype(BF16), loss_target.reshape(t, d)

    shards0 = layer_shards(0, a_w_in, a_w_out, mem_w_kv, ffn_w_up, ffn_w_down)
    shards1 = layer_shards(1, b_w_q, b_w_out, mem_w_kv, ffn_w_up, ffn_w_down)
    shards1.append(jnp.pad(kv_w, ((0, 0), (0, KV_COLS_PAD - KV_COLS))).astype(BF16))
    gath0 = exchange(shards0[:3] + [pack_small(ffn_conv_w, a_pool_scale)], [False] * 4, "gather_w0_mixer")
    pending = {"ffn0": exchange_start(shards0[3:], [False] * 2, gath0[0], "gather_w0_ffn_start")}
    small = gath0[3]
    wt0, w_in = mixer_weights(0, gath0, ln1_g + pending["ffn0"]["token"], ln1_b, ln2_g, ln2_b)
    pw_bd = _block_diag(a_pool_w[0])
    wt0.update(win_a=w_in, win_a_t=w_in.T, pw_bd=pw_bd.astype(BF16), pw_bd_t=pw_bd.T.astype(BF16),
               pscale=small[:, 8, :96].reshape(1, TOK_WIDTH) + pending["ffn0"]["token"])

    def ffn0_weights(x1_bf):
        got = exchange_wait(pending["ffn0"], x1_bf, "gather_w0_ffn_wait")
        pending["w1"] = exchange_start(shards1, [False] * 6, got[0], "gather_w1_start")
        w = ffn_weights(0, got[0], got[1], small, ffn_conv_b)
        w["cw"] = w["cw"] + pending["w1"]["token"]
        return w

    x1, x1_bf, sv0 = forward_layer(0, x2d, x2d.astype(BF16), mem_bf, wt0, nb, s, ffn_weights=ffn0_weights)
    gath1 = exchange_wait(pending["w1"], x1_bf, "gather_w1_wait")
    wt1, w_q = mixer_weights(1, gath1, ln1_g, ln1_b, ln2_g, ln2_b)
    wt1.update(ffn_weights(1, gath1[3], gath1[4], small, ffn_conv_b))
    kvw = gath1[5].reshape(D_MODEL, KV_COLS_PAD)
    wt1.update(wq=w_q, wq_t=w_q.T, kvw=kvw, kvw_t=kvw.T,
               fb=jnp.pad(f_b.reshape(1, FOX_HEADS), ((0, 0), (0, LANES - FOX_HEADS))))
    y, _, sv1 = forward_layer(1, x1, x1_bf, mem_bf, wt1, nb, s)
    dy, loss_row = loss_head(y, target)
    loss = lax.psum(loss_row[0, 0], ("x", "y", "c"))

    dx1, g1 = backward_layer(1, dy, sv1, mem_bf, wt1, nb, s)
    blocks1 = (mixer_grad_blocks(g1, g1["wq"]) + ffn_grad_blocks(g1)
               + [g1["kvw"][:, :KV_COLS].reshape(N_DEV, 128, KV_COLS).astype(BF16)])
    pending["g1"] = exchange_start(blocks1, [True] * 6, dx1, "scatter_g1_start")
    wt0["ln2_g"] = wt0["ln2_g"] + pending["g1"]["token"]

    def after_ffn0(g, dxm):
        pending["gf0"] = exchange_start(ffn_grad_blocks(g), [True] * 2, dxm, "scatter_g0_ffn_start")
        return pending["gf0"]["token"]

    grad_x, g0 = backward_layer(0, dx1, sv0, mem_bf, wt0, nb, s, after_ffn=after_ffn0)
    last = mixer_grad_blocks(g0, g0["win_a"]) + [small_grad_blocks(g0, g1), replicated_grads(g0, g1)]
    parts_m0 = exchange(last, [True] * 4 + [False], "scatter_g0_mixer")
    parts_f0 = exchange_wait(pending["gf0"], parts_m0[0], "scatter_g0_ffn_wait")
    parts1 = exchange_wait(pending["g1"], parts_f0[0], "scatter_g1_wait")
    parts0 = list(parts_m0[:3]) + list(parts_f0) + list(parts_m0[3:])

    res = {}

    def upd(nm, parts, w2, m2, v2):
        res[nm] = reduce_adamw(parts, w2, m2, v2, f"adamw_{nm}")

    upd("a_w_in", parts0[0], a_w_in[0], m_a_w_in[0], v_a_w_in[0])
    upd("a_w_out", parts0[1], a_w_out[0], m_a_w_out[0], v_a_w_out[0])
    upd("b_w_q", parts1[0], b_w_q[0], m_b_w_q[0], v_b_w_q[0])
    upd("b_w_out", parts1[1], b_w_out[0], m_b_w_out[0], v_b_w_out[0])
    upd("kv_w", parts1[5], kv_w, m_kv_w, v_kv_w)
    for l, parts in enumerate((parts0, parts1)):
        upd(f"mem_w_kv{l}", parts[2], mem_w_kv[l], m_mem_w_kv[l], v_mem_w_kv[l])
        upd(f"ffn_w_up{l}", parts[3], ffn_w_up[l], m_ffn_w_up[l], v_ffn_w_up[l])
        upd(f"ffn_w_down{l}", parts[4], ffn_w_down[l], m_ffn_w_down[l], v_ffn_w_down[l])
    upd("small", parts0[5], pack_small(ffn_conv_w, a_pool_scale), pack_small(m_ffn_conv_w, m_a_pool_scale),
        pack_small(v_ffn_conv_w, v_a_pool_scale))
    upd("replicated", parts0[6], pack_replicated(a_pool_w, ln1_g, ln1_b, ln2_g, ln2_b, ffn_conv_b, f_b),
        pack_replicated(m_a_pool_w, m_ln1_g, m_ln1_b, m_ln2_g, m_ln2_b, m_ffn_conv_b, m_f_b),
        pack_replicated(v_a_pool_w, v_ln1_g, v_ln1_b, v_ln2_g, v_ln2_b, v_ffn_conv_b, v_f_b))

    for nm in ("a_w_in", "a_w_out", "b_w_q", "b_w_out"):
        res[nm] = [o[None] for o in res[nm]]
    for nm in ("mem_w_kv", "ffn_w_up", "ffn_w_down"):
        res[nm] = [jnp.stack([a0, a1]) for a0, a1 in zip(res[nm + "0"], res[nm + "1"])]
    res["ffn_conv_w"] = [o[:DEPTH * 3, :FF_BLOCK].reshape(DEPTH, 3, FF_BLOCK) for o in res["small"]]
    res["a_pool_scale"] = [o[8:9, :96] for o in res["small"]]
    rep_names = ["a_pool_w", "ln1_g", "ln1_b", "ln2_g", "ln2_b", "ffn_conv_b", "f_b"]
    for nm in rep_names:
        res[nm] = []
    for o in res["replicated"]:
        for nm, val in zip(rep_names, unpack_replicated(o)):
            res[nm].append(val)

    order = ["a_w_in", "a_pool_w", "a_pool_scale", "a_w_out", "b_w_q", "b_w_out", "kv_w", "f_b", "mem_w_kv",
             "ln1_g", "ln1_b", "ln2_g", "ln2_b", "ffn_w_up", "ffn_conv_w", "ffn_conv_b", "ffn_w_down"]
    out = [loss, grad_x.reshape(nb, s, d)]
    for kind in range(4):
        out.extend(res[nm][kind] for nm in order)
    return tuple(out)
```

```python
import jax
import jax.numpy as jnp
from jax import lax
from jax.experimental import pallas as pl
from jax.experimental.pallas import tpu as pltpu

F32 = jnp.float32
BF16 = jnp.bfloat16
SDS = jax.ShapeDtypeStruct

N_DEV = 8
D_MODEL = 1024
TOK_WIDTH = 768
MEM_WIDTH = 256
MEM_LEN = 256
MEM_HEADS = 4
HEAD_DIM = 64
FOX_HEADS = 12
POOL_GROUP = 192
FF_BLOCK = 688
FF_BLOCK_PAD = 768
FF_PAIRS = 4
FF_ROWS = 344
KV_COLS = 1548
KV_COLS_PAD = 1664
LANES = 128
DEPTH = 2
DN_ALPHA = (2.0 * DEPTH) ** 0.25
LN_EPS = 1e-5
QK_SCALE = HEAD_DIM ** -0.5
NEG_BIG = -1e30

ADAM_LR = 0.001
ADAM_B1 = 0.9
ADAM_B2 = 0.999
ADAM_EPS = 1e-08
ADAM_WD = 0.01
ADAM_STEP = 10

VMEM_LIMIT_BYTES = 56 * 1024 * 1024
MM_BLOCK_BYTES = 6 * 1024 * 1024
TM = 512
TS = 256
TF = 256
TC = 256
HALO_POOL = 16
HALO_CONV = 8

NT_DIMS = (((1,), (1,)), ((), ()))
TN_DIMS = (((0,), (0,)), ((), ()))


def _params(sem=None):
    return pltpu.CompilerParams(dimension_semantics=sem, vmem_limit_bytes=VMEM_LIMIT_BYTES)


def _sigmoid(z):
    return 1.0 / (1.0 + jnp.exp(-z))


def _pick_tn(n):
    if n <= 2048:
        return n
    for t in (1024, 768, 512, 256, 128):
        if n % t == 0:
            return t
    return n


def mm_nn(a, b, out_dtype, name, addend=None, add_scale=1.0, also_bf16=False):
    m, k = a.shape
    _, n = b.shape
    tm = min(TM, m)
    tn = n
    while k * tn * 2 > MM_BLOCK_BYTES or tm * tn * 4 > MM_BLOCK_BYTES:
        tn //= 2
    chunk = tn if tn <= 2048 else _pick_tn(tn)
    has_add = addend is not None

    def body(*refs):
        a_ref, b_ref = refs[0], refs[1]
        c_ref = refs[2] if has_add else None
        o_ref = refs[3] if has_add else refs[2]
        ob_ref = refs[-1] if also_bf16 else None
        av = a_ref[...].astype(BF16)
        for c in range(tn // chunk):
            cols = slice(c * chunk, (c + 1) * chunk)
            r = jnp.dot(av, b_ref[:, cols].astype(BF16), preferred_element_type=F32)
            if has_add:
                r = r + add_scale * c_ref[:, cols]
            o_ref[:, cols] = r.astype(out_dtype)
            if also_bf16:
                ob_ref[:, cols] = r.astype(BF16)

    in_specs = [pl.BlockSpec((tm, k), lambda j, i: (i, 0)), pl.BlockSpec((k, tn), lambda j, i: (0, j))]
    ops = [a, b]
    tile = pl.BlockSpec((tm, tn), lambda j, i: (i, j))
    if has_add:
        in_specs.append(tile)
        ops.append(addend)
    out_shape = [SDS((m, n), out_dtype)]
    out_specs = [tile]
    if also_bf16:
        out_shape.append(SDS((m, n), BF16))
        out_specs.append(tile)
    res = pl.pallas_call(
        body, name=name, grid=(n // tn, m // tm), in_specs=in_specs, out_specs=out_specs, out_shape=out_shape,
        compiler_params=_params(("parallel", "parallel")))(*ops)
    return tuple(res) if also_bf16 else res[0]


def mm_tn(a, b, name, blocked=False):
    t, m = a.shape
    _, n = b.shape
    tt = min(2 * TM, t)
    tm = 1024 if m % 1024 == 0 else m
    tn = FF_BLOCK_PAD if blocked else _pick_tn(n)
    nt = t // tt

    def body(a_ref, b_ref, o_ref):
        kk = pl.program_id(2)
        r = lax.dot_general(a_ref[...].astype(BF16), b_ref[...].astype(BF16), TN_DIMS, preferred_element_type=F32)
        if blocked:
            r = r[None]

        @pl.when(kk == 0)
        def _():
            o_ref[...] = r

        @pl.when(kk != 0)
        def _():
            o_ref[...] += r

    if blocked:
        out_shape = SDS((n // tn, m, tn), F32)
        out_spec = pl.BlockSpec((1, tm, tn), lambda i, j, kk: (j, i, 0))
    else:
        out_shape = SDS((m, n), F32)
        out_spec = pl.BlockSpec((tm, tn), lambda i, j, kk: (i, j))
    return pl.pallas_call(
        body, name=name, grid=(m // tm, n // tn, nt),
        in_specs=[pl.BlockSpec((tt, tm), lambda i, j, kk: (kk, i)), pl.BlockSpec((tt, tn), lambda i, j, kk: (kk, j))],
        out_specs=out_spec, out_shape=out_shape,
        compiler_params=_params(("parallel", "parallel", "arbitrary")))(a, b)


def ln_fwd(xprev, delta, g, b, name):
    t, d = xprev.shape
    tm = min(TM, t)

    def body(xp_ref, dl_ref, g_ref, b_ref, y_ref, yb_ref, xh_ref, rs_ref):
        r = DN_ALPHA * xp_ref[...] + dl_ref[...]
        mu = jnp.mean(r, axis=1, keepdims=True)
        xc = r - mu
        var = jnp.mean(xc * xc, axis=1, keepdims=True)
        rstd = lax.rsqrt(var + LN_EPS)
        xh = xc * rstd
        y = xh * g_ref[...] + b_ref[...]
        y_ref[...] = y
        yb_ref[...] = y.astype(BF16)
        xh_ref[...] = xh
        rs_ref[...] = jnp.broadcast_to(rstd, (tm, LANES))

    row = pl.BlockSpec((tm, d), lambda i: (i, 0))
    vec = pl.BlockSpec((1, d), lambda i: (0, 0))
    return pl.pallas_call(
        body, name=name, grid=(t // tm,), in_specs=[row, row, vec, vec],
        out_specs=[row, row, row, pl.BlockSpec((tm, LANES), lambda i: (i, 0))],
        out_shape=[SDS((t, d), F32), SDS((t, d), BF16), SDS((t, d), F32), SDS((t, LANES), F32)],
        compiler_params=_params(("parallel",)))(xprev, delta, g, b)


def ln_bwd(dy, xhat, rstd, g, name):
    t, d = dy.shape
    tm = min(TM, t)

    def body(dy_ref, xh_ref, rs_ref, g_ref, dr_ref, drb_ref, dg_ref, db_ref):
        i = pl.program_id(0)
        dyv = dy_ref[...]
        xh = xh_ref[...]
        dxh = dyv * g_ref[...]
        m1 = jnp.mean(dxh, axis=1, keepdims=True)
        m2 = jnp.mean(dxh * xh, axis=1, keepdims=True)
        dr = rs_ref[:, 0:1] * (dxh - m1 - xh * m2)
        dr_ref[...] = dr
        drb_ref[...] = dr.astype(BF16)

        @pl.when(i == 0)
        def _():
            dg_ref[...] = jnp.zeros_like(dg_ref)
            db_ref[...] = jnp.zeros_like(db_ref)

        dg_ref[...] += jnp.sum(dyv * xh, axis=0, keepdims=True)
        db_ref[...] += jnp.sum(dyv, axis=0, keepdims=True)

    row = pl.BlockSpec((tm, d), lambda i: (i, 0))
    vec = pl.BlockSpec((1, d), lambda i: (0, 0))
    return pl.pallas_call(
        body, name=name, grid=(t // tm,),
        in_specs=[row, row, pl.BlockSpec((tm, LANES), lambda i: (i, 0)), vec],
        out_specs=[row, row, vec, vec],
        out_shape=[SDS((t, d), F32), SDS((t, d), BF16), SDS((1, d), F32), SDS((1, d), F32)],
        compiler_params=_params(("arbitrary",)))(dy, xhat, rstd, g)


def loss_head(y, target):
    t, d = y.shape
    tm = min(TM, t)
    nsteps = t // tm

    def body(y_ref, t_ref, dy_ref, l_ref, acc):
        i = pl.program_id(0)
        diff = y_ref[...] - t_ref[...]
        dy_ref[...] = diff * (1.0 / d)

        @pl.when(i == 0)
        def _():
            acc[...] = jnp.zeros_like(acc)

        acc[...] += jnp.sum(diff * diff, axis=0, keepdims=True)

        @pl.when(i == nsteps - 1)
        def _():
            tot = jnp.sum(acc[...], axis=1, keepdims=True) * (0.5 / d)
            l_ref[...] = jnp.broadcast_to(tot, (1, LANES))

    row = pl.BlockSpec((tm, d), lambda i: (i, 0))
    return pl.pallas_call(
        body, name="loss_head", grid=(nsteps,), in_specs=[row, row],
        out_specs=[row, pl.BlockSpec((1, LANES), lambda i: (0, 0))],
        out_shape=[SDS((t, d), F32), SDS((1, LANES), F32)],
        scratch_shapes=[pltpu.VMEM((1, d), F32)],
        compiler_params=_params(("arbitrary",)))(y, target)


def memattn_fwd(proj, memkv, nb, s, name):
    ts = min(TS, s)
    nq = s // ts

    def body(q_ref, kv_ref, o_ref):
        top = lax.broadcasted_iota(jnp.int32, (PAIR, ts), 0) < HEAD_DIM
        scores = []
        for p in range(MEM_HEADS // 2):
            qp = q_ref[:, p * PAIR:(p + 1) * PAIR].astype(BF16)
            ke, ko = _split_pair(kv_ref[:, p * PAIR:(p + 1) * PAIR], QK_SCALE)
            scores.append([lax.dot_general(km, qp, NT_DIMS, preferred_element_type=F32) for km in (ke, ko)])
        for p in range(MEM_HEADS // 2):
            vt = kv_ref[:, MEM_WIDTH + p * PAIR:MEM_WIDTH + (p + 1) * PAIR].astype(F32).T.astype(BF16)
            outs = []
            for sc in scores[p]:
                e = jnp.exp(sc - jnp.max(sc, axis=0, keepdims=True))
                pr = e / jnp.sum(e, axis=0, keepdims=True)
                outs.append(jnp.dot(vt, pr.astype(BF16), preferred_element_type=F32))
            o_ref[:, p * PAIR:(p + 1) * PAIR] = jnp.where(top, outs[0], outs[1]).T.astype(BF16)

    return pl.pallas_call(
        body, name=name, grid=(nb, nq),
        in_specs=[pl.BlockSpec((ts, MEM_WIDTH), lambda b, i: (b * nq + i, 3)),
                  pl.BlockSpec((MEM_LEN, 2 * MEM_WIDTH), lambda b, i: (b, 0))],
        out_specs=pl.BlockSpec((ts, MEM_WIDTH), lambda b, i: (b * nq + i, 0)),
        out_shape=SDS((nb * s, MEM_WIDTH), BF16),
        compiler_params=_params(("parallel", "parallel")))(proj, memkv)


def memattn_bwd(proj, memkv, dcat, nb, s, name):
    ts = min(TS, s)
    nq = s // ts

    def body(q_ref, kv_ref, do_ref, dq_ref, dkv_ref):
        i = pl.program_id(1)

        @pl.when(i == 0)
        def _():
            dkv_ref[...] = jnp.zeros_like(dkv_ref)

        lo = _half_masks(MEM_LEN)
        top = lax.broadcasted_iota(jnp.int32, (PAIR, ts), 0) < HEAD_DIM
        n_pairs = MEM_HEADS // 2
        qs, dos, kps, products = [], [], [], []
        for p in range(n_pairs):
            qp = q_ref[:, p * PAIR:(p + 1) * PAIR].astype(BF16)
            dop = do_ref[:, p * PAIR:(p + 1) * PAIR].astype(BF16)
            kp = kv_ref[:, p * PAIR:(p + 1) * PAIR] * QK_SCALE
            kms = _split_pair(kp)
            vms = _split_pair(kv_ref[:, MEM_WIDTH + p * PAIR:MEM_WIDTH + (p + 1) * PAIR])
            products.append([(lax.dot_general(km, qp, NT_DIMS, preferred_element_type=F32),
                              lax.dot_general(vm, dop, NT_DIMS, preferred_element_type=F32)) for km, vm in zip(kms, vms)])
            qs.append(qp)
            dos.append(dop)
            kps.append(kp)
        for p in range(n_pairs):
            kt = kps[p].astype(F32).T.astype(BF16)
            dks, dvs, dqs = [], [], []
            for sc, dp in products[p]:
                e = jnp.exp(sc - jnp.max(sc, axis=0, keepdims=True))
                pr = e / jnp.sum(e, axis=0, keepdims=True)
                dl = jnp.sum(pr * dp, axis=0, keepdims=True)
                ds = (pr * (dp - dl)).astype(BF16)
                dvs.append(jnp.dot(pr.astype(BF16), dos[p], preferred_element_type=F32))
                dks.append(jnp.dot(ds, qs[p], preferred_element_type=F32))
                dqs.append(jnp.dot(kt, ds, preferred_element_type=F32))
            dq_ref[:, p * PAIR:(p + 1) * PAIR] = jnp.where(top, dqs[0], dqs[1]).T.astype(BF16)
            dkv_ref[:, p * PAIR:(p + 1) * PAIR] += jnp.where(lo, dks[0], dks[1]) * QK_SCALE
            dkv_ref[:, MEM_WIDTH + p * PAIR:MEM_WIDTH + (p + 1) * PAIR] += jnp.where(lo, dvs[0], dvs[1])

    return pl.pallas_call(
        body, name=name, grid=(nb, nq),
        in_specs=[pl.BlockSpec((ts, MEM_WIDTH), lambda b, i: (b * nq + i, 3)),
                  pl.BlockSpec((MEM_LEN, 2 * MEM_WIDTH), lambda b, i: (b, 0)),
                  pl.BlockSpec((ts, MEM_WIDTH), lambda b, i: (b * nq + i, 3))],
        out_specs=[pl.BlockSpec((ts, MEM_WIDTH), lambda b, i: (b * nq + i, 0)),
                   pl.BlockSpec((MEM_LEN, 2 * MEM_WIDTH), lambda b, i: (b, 0))],
        out_shape=[SDS((nb * s, MEM_WIDTH), BF16), SDS((nb * MEM_LEN, 2 * MEM_WIDTH), F32)],
        compiler_params=_params(("parallel", "arbitrary")))(proj, memkv, dcat)


def _pool_select(shape, s2, s4, s8, s16):
    lane = lax.broadcasted_iota(jnp.int32, shape, 1)
    return jnp.where(lane < POOL_GROUP, s2, jnp.where(lane < 2 * POOL_GROUP, s4, jnp.where(lane < 3 * POOL_GROUP, s8, s16)))


def _pool_count(shape, first_pos):
    pos = first_pos + lax.broadcasted_iota(jnp.int32, shape, 0)
    win = _pool_select(shape, 2, 4, 8, 16)
    return jnp.minimum(pos + 1, win).astype(F32)


def pool_fwd(proj, pw_bd, pscale, nb, s):
    ts = min(TS, s)
    nq = s // ts
    w = TOK_WIDTH

    def body(c_ref, h_ref, w_ref, sc_ref, pooled_ref, tok_ref):
        i = pl.program_id(0) % nq
        cur = c_ref[...]
        halo = jnp.where(i == 0, 0.0, h_ref[...])
        xe = jnp.concatenate([halo, cur], axis=0)
        s2 = xe + pltpu.roll(xe, 1, axis=0)
        s4 = s2 + pltpu.roll(s2, 2, axis=0)
        s8 = s4 + pltpu.roll(s4, 4, axis=0)
        s16 = s8 + pltpu.roll(s8, 8, axis=0)
        hp = HALO_POOL
        ws = _pool_select((ts, w), s2[hp:], s4[hp:], s8[hp:], s16[hp:])
        pooled = (ws / _pool_count((ts, w), i * ts) - cur).astype(BF16)
        pooled_ref[...] = pooled
        mixed = jnp.dot(pooled, w_ref[...], preferred_element_type=F32)
        tok_ref[...] = (mixed * sc_ref[...]).astype(BF16)

    row = pl.BlockSpec((ts, w), lambda r: (r, 0))
    return pl.pallas_call(
        body, name="pool_fwd", grid=(nb * nq,),
        in_specs=[row, pl.BlockSpec((HALO_POOL, w), lambda r: (jnp.maximum(r * (ts // HALO_POOL) - 1, 0), 0)),
                  pl.BlockSpec((w, w), lambda r: (0, 0)), pl.BlockSpec((1, w), lambda r: (0, 0))],
        out_specs=[row, row], out_shape=[SDS((nb * s, w), BF16), SDS((nb * s, w), BF16)],
        compiler_params=_params(("parallel",)))(proj, proj, pw_bd, pscale)


def pool_bwd_mix(dcat, pooled, pw_bd, pw_bd_t, pscale, nb, s):
    ts = min(TS, s)
    w = TOK_WIDTH

    def body(dt_ref, p_ref, w_ref, wt_ref, sc_ref, dm_ref, dp_ref, ds_ref):
        r = pl.program_id(0)
        dtok = dt_ref[...]
        mixed = jnp.dot(p_ref[...], w_ref[...], preferred_element_type=F32)

        @pl.when(r == 0)
        def _():
            ds_ref[...] = jnp.zeros_like(ds_ref)

        ds_ref[...] += jnp.sum(dtok * mixed, axis=0, keepdims=True)
        dmx = (dtok * sc_ref[...]).astype(BF16)
        dm_ref[...] = dmx
        dp_ref[...] = jnp.dot(dmx, wt_ref[...], preferred_element_type=F32)

    row = pl.BlockSpec((ts, w), lambda r: (r, 0))
    mat = pl.BlockSpec((w, w), lambda r: (0, 0))
    vec = pl.BlockSpec((1, w), lambda r: (0, 0))
    return pl.pallas_call(
        body, name="pool_bwd_mix", grid=(nb * s // ts,), in_specs=[row, row, mat, mat, vec],
        out_specs=[row, row, vec], out_shape=[SDS((nb * s, w), BF16), SDS((nb * s, w), F32), SDS((1, w), F32)],
        compiler_params=_params(("arbitrary",)))(dcat, pooled, pw_bd, pw_bd_t, pscale)


def pool_bwd_window(dpooled, nb, s):
    ts = min(TS, s)
    nq = s // ts
    w = TOK_WIDTH
    n_ext = ts + HALO_POOL
    n_halo_blocks = nb * s // HALO_POOL

    def body(c_ref, n_ref, du_ref):
        i = pl.program_id(0) % nq
        cur = c_ref[...]
        nxt = jnp.where(i == nq - 1, 0.0, n_ref[...])
        ze = jnp.concatenate([cur, nxt], axis=0) / _pool_count((n_ext, w), i * ts)
        s2 = ze + pltpu.roll(ze, n_ext - 1, axis=0)
        s4 = s2 + pltpu.roll(s2, n_ext - 2, axis=0)
        s8 = s4 + pltpu.roll(s4, n_ext - 4, axis=0)
        s16 = s8 + pltpu.roll(s8, n_ext - 8, axis=0)
        ws = _pool_select((ts, w), s2[:ts], s4[:ts], s8[:ts], s16[:ts])
        du_ref[...] = (ws - cur).astype(BF16)

    row = pl.BlockSpec((ts, w), lambda r: (r, 0))
    return pl.pallas_call(
        body, name="pool_bwd_window", grid=(nb * nq,),
        in_specs=[row, pl.BlockSpec((HALO_POOL, w),
                                    lambda r: (jnp.minimum((r + 1) * (ts // HALO_POOL), n_halo_blocks - 1), 0))],
        out_specs=row, out_shape=SDS((nb * s, w), BF16),
        compiler_params=_params(("parallel",)))(dpooled, dpooled)


def _conv_rows(xe, w_ref):
    return (w_ref[0, 2:3, :] * xe + w_ref[0, 1:2, :] * pltpu.roll(xe, 1, axis=0)
            + w_ref[0, 0:1, :] * pltpu.roll(xe, 2, axis=0) + w_ref[0, 3:4, :])


def ffn_up_gate(x_bf, wup, cw, nb, s, name):
    tm = min(TM, s)
    nq = s // tm
    w = FF_BLOCK_PAD
    hr = 2 * HALO_CONV
    k = x_bf.shape[1]

    def body(xc_ref, xh_ref, wu_ref, wg_ref, cu_ref, cg_ref, act_ref, a_ref, b_ref, hu_ref, hg_ref):
        first = (pl.program_id(1) % nq) == 0
        xc = xc_ref[...]
        xh = xh_ref[...]

        def products(w_ref):
            return (jnp.dot(xc, w_ref[...], preferred_element_type=F32), jnp.dot(xh, w_ref[...], preferred_element_type=F32))

        def conv(hcur, hprev, c_ref, h_out):
            h_out[...] = hcur.astype(BF16)
            xe = jnp.concatenate([jnp.where(first, 0.0, hprev), hcur], axis=0)
            return _conv_rows(xe, c_ref)[hr:]

        pu, pg = products(wu_ref), products(wg_ref)
        cu = conv(*pu, cu_ref, hu_ref)
        cg = conv(*pg, cg_ref, hg_ref)
        sg = _sigmoid(cg)
        a = cg * sg
        act_ref[...] = (a * cu).astype(BF16)
        a_ref[...] = a.astype(BF16)
        b_ref[...] = (cu * (sg * (1.0 + cg * (1.0 - sg)))).astype(BF16)

    def wblock(off):
        return pl.BlockSpec((k, w), lambda j, r: (0, j + off))

    def cblock(off):
        return pl.BlockSpec((1, 8, w), lambda j, r: (j + off, 0, 0))

    tile = pl.BlockSpec((tm, w), lambda j, r: (r, j))
    out = SDS((nb * s, FF_PAIRS * w), BF16)
    return pl.pallas_call(
        body, name=name, grid=(FF_PAIRS, nb * nq),
        in_specs=[pl.BlockSpec((tm, k), lambda j, r: (r, 0)),
                  pl.BlockSpec((hr, k), lambda j, r: (jnp.maximum(r * (tm // hr) - 1, 0), 0)),
                  wblock(0), wblock(FF_PAIRS), cblock(0), cblock(FF_PAIRS)],
        out_specs=[tile] * 5, out_shape=[out] * 5,
        compiler_params=_params(("parallel", "parallel")))(x_bf, x_bf, wup, wup, cw, cw)


def gate_conv_bwd(dact, a, b, hu, hg, cw, nb, s, name):
    ts = min(TS, s)
    nq = s // ts
    w = FF_BLOCK_PAD
    hc = HALO_CONV
    hb = 2 * hc
    n_ext = ts + hc

    def body(dc_ref, dn_ref, ac_ref, an_ref, bc_ref, bn_ref, hu_ref, hg_ref, wu_ref, wg_ref,
             dhu_ref, dhg_ref, dwu_ref, dwg_ref):
        r = pl.program_id(1)
        last = (r % nq) == nq - 1

        def ext(c_ref, n_ref):
            return jnp.concatenate([c_ref[...].astype(F32), n_ref[...].astype(F32)[:hc]], axis=0)

        da = jnp.where(last & (lax.broadcasted_iota(jnp.int32, (n_ext, w), 0) >= ts), 0.0, ext(dc_ref, dn_ref))

        def branch(dcv, w_ref, h_ref, dh_ref, dw_ref):
            d0 = dcv[:ts]
            d1 = pltpu.roll(dcv, n_ext - 1, axis=0)[:ts]
            d2 = pltpu.roll(dcv, n_ext - 2, axis=0)[:ts]
            dh_ref[...] = (w_ref[0, 2:3, :] * d0 + w_ref[0, 1:2, :] * d1 + w_ref[0, 0:1, :] * d2).astype(BF16)
            hv = h_ref[...].astype(F32)
            rows = [jnp.sum(d2 * hv, axis=0, keepdims=True), jnp.sum(d1 * hv, axis=0, keepdims=True),
                    jnp.sum(d0 * hv, axis=0, keepdims=True), jnp.sum(d0, axis=0, keepdims=True)]
            sub = lax.broadcasted_iota(jnp.int32, (8, w), 0)
            upd = jnp.zeros((8, w), F32)
            for kk, rv in enumerate(rows):
                upd = jnp.where(sub == kk, rv, upd)

            @pl.when(r == 0)
            def _():
                dw_ref[...] = jnp.zeros_like(dw_ref)

            dw_ref[...] += upd[None]

        branch(da * ext(ac_ref, an_ref), wu_ref, hu_ref, dhu_ref, dwu_ref)
        branch(da * ext(bc_ref, bn_ref), wg_ref, hg_ref, dhg_ref, dwg_ref)

    cur = pl.BlockSpec((ts, w), lambda j, r: (r, j))
    nxt = pl.BlockSpec((hb, w), lambda j, r: (jnp.minimum((r + 1) * (ts // hb), nb * s // hb - 1), j))

    def wspec(off):
        return pl.BlockSpec((1, 8, w), lambda j, r: (j + off, 0, 0))

    p = FF_PAIRS
    dw_spec = pl.BlockSpec((1, 8, w), lambda j, r: (j, 0, 0))
    return pl.pallas_call(
        body, name=name, grid=(p, nb * nq),
        in_specs=[cur, nxt, cur, nxt, cur, nxt, cur, cur, wspec(0), wspec(p)],
        out_specs=[cur, cur, dw_spec, dw_spec],
        out_shape=[SDS((nb * s, p * w), BF16), SDS((nb * s, p * w), BF16), SDS((p, 8, w), F32), SDS((p, 8, w), F32)],
        compiler_params=_params(("parallel", "arbitrary")))(dact, dact, a, a, b, b, hu, hg, cw, cw)


def _tri(n, upper):
    r = lax.broadcasted_iota(jnp.int32, (n, n), 0)
    c = lax.broadcasted_iota(jnp.int32, (n, n), 1)
    return ((r <= c) if upper else (r >= c)).astype(F32)


def fgate_fwd(fl, fb, nb, s):
    tc = min(TC, s)
    nq = s // tc

    def body(fl_ref, fb_ref, f_ref, carry):
        @pl.when(pl.program_id(1) == 0)
        def _():
            carry[...] = jnp.zeros_like(carry)

        z = fl_ref[...] + fb_ref[...]
        logf = jnp.minimum(z, 0.0) - jnp.log(1.0 + jnp.exp(-jnp.abs(z)))
        f_ref[...] = jnp.dot(_tri(tc, False), logf, preferred_element_type=F32,
                             precision=lax.Precision.HIGHEST) + carry[...]
        carry[...] += jnp.sum(logf, axis=0, keepdims=True)

    row = pl.BlockSpec((tc, LANES), lambda b, i: (b * nq + i, 0))
    return pl.pallas_call(
        body, name="fgate_fwd", grid=(nb, nq), in_specs=[row, pl.BlockSpec((1, LANES), lambda b, i: (0, 0))],
        out_specs=row, out_shape=SDS((nb * s, LANES), F32), scratch_shapes=[pltpu.VMEM((1, LANES), F32)],
        compiler_params=_params(("arbitrary", "arbitrary")))(fl, fb)


def fgate_bwd(d_cum_q, d_cum_k, fl, fb, nb, s):
    tc = min(TC, s)
    nq = s // tc

    def body(dfq_ref, dfk_ref, fl_ref, fb_ref, dfl_ref, dfb_ref, carry):
        b = pl.program_id(0)
        i = pl.program_id(1)

        @pl.when(i == 0)
        def _():
            carry[...] = jnp.zeros_like(carry)

        @pl.when(jnp.logical_and(b == 0, i == 0))
        def _():
            dfb_ref[...] = jnp.zeros_like(dfb_ref)

        dfv = dfq_ref[...] + dfk_ref[...]
        dlog = jnp.dot(_tri(tc, True), dfv, preferred_element_type=F32,
                       precision=lax.Precision.HIGHEST) + carry[...]
        carry[...] += jnp.sum(dfv, axis=0, keepdims=True)
        z = fl_ref[...] + fb_ref[...]
        dfl = dlog / (1.0 + jnp.exp(z))
        dfl_ref[...] = dfl
        dfb_ref[...] += jnp.sum(dfl, axis=0, keepdims=True)

    row = pl.BlockSpec((tc, LANES), lambda b, i: (b * nq + nq - 1 - i, 0))
    vec = pl.BlockSpec((1, LANES), lambda b, i: (0, 0))
    return pl.pallas_call(
        body, name="fgate_bwd", grid=(nb, nq), in_specs=[row, row, row, vec], out_specs=[row, vec],
        out_shape=[SDS((nb * s, LANES), F32), SDS((1, LANES), F32)], scratch_shapes=[pltpu.VMEM((1, LANES), F32)],
        compiler_params=_params(("arbitrary", "arbitrary")))(d_cum_q, d_cum_k, fl, fb)


PAIR = 2 * HEAD_DIM
N_PAIRS = FOX_HEADS // 2


def _lane_put(shape, h, col):
    lane = lax.broadcasted_iota(jnp.int32, shape, 1)
    return jnp.where(lane == h, col, 0.0)


def _half_masks(rows):
    lane = lax.broadcasted_iota(jnp.int32, (rows, PAIR), 1)
    return lane < HEAD_DIM


def _split_pair(x, scale=None):
    if scale is not None:
        x = x * scale
    lo = _half_masks(x.shape[0])
    zero = jnp.zeros_like(x)
    return jnp.where(lo, x, zero), jnp.where(lo, zero, x)


def _to_tile_rows(a, nb, s, tf):
    return a.reshape(nb * s // tf, tf, LANES)[:, :, :16].transpose(0, 2, 1)


def _from_tile_rows(a):
    tiles, _, tf = a.shape
    return jnp.pad(a.transpose(0, 2, 1), ((0, 0), (0, 0), (0, LANES - 16))).reshape(tiles * tf, LANES)


BIAS_TERMS = 3
LOOKAHEAD = 4
FOLLOW_FWD = 1
LOOKAHEAD_BWD = 2
FOLLOW_BWD = 1


def _bias_lane(h):
    return HEAD_DIM if h % 2 == 0 else 0


def _placement():
    rows = jnp.arange(LANES)[:, None]
    cols = jnp.arange(FOX_HEADS * PAIR)[None, :]
    head, lane = cols // PAIR, cols % PAIR
    first = jnp.where(head % 2 == 0, HEAD_DIM, 0)
    term = lane - first
    hit = (term >= 0) & (term < BIAS_TERMS) & (rows == 16 * term + head)
    return hit.astype(BF16)


def fox_prep(kv, fneg, nb, s):
    tf = min(TF, s)
    w = TOK_WIDTH

    def body(k_ref, v_ref, f_ref, pl_ref, ka_ref, vt_ref):
        lane = lax.broadcasted_iota(jnp.int32, (tf, LANES), 1)
        lo = lane < HEAD_DIM
        f = jnp.where(lane < FOX_HEADS, f_ref[...], 0.0)
        hi = f.astype(BF16).astype(F32)
        mid = (f - hi).astype(BF16).astype(F32)
        low = (f - hi - mid).astype(BF16).astype(F32)
        terms = (hi + pltpu.roll(mid, 16, axis=1) + pltpu.roll(low, 32, axis=1)).astype(BF16)
        placed = jnp.dot(terms, pl_ref[...], preferred_element_type=F32).astype(BF16)
        one = jnp.ones((tf, LANES), BF16)
        zero = jnp.zeros((tf, LANES), BF16)
        for p in range(N_PAIRS):
            kp = k_ref[:, p * PAIR:(p + 1) * PAIR] * QK_SCALE
            vp = v_ref[:, p * PAIR:(p + 1) * PAIR]
            he, ho = 2 * p, 2 * p + 1
            ka_ref[:, he * PAIR:(he + 1) * PAIR] = jnp.where(lo, kp, placed[:, he * PAIR:(he + 1) * PAIR])
            ka_ref[:, ho * PAIR:(ho + 1) * PAIR] = jnp.where(lo, placed[:, ho * PAIR:(ho + 1) * PAIR], kp)
            ve = jnp.where(lo, vp, jnp.where(lane == HEAD_DIM, one, zero))
            vo = jnp.where(lo, jnp.where(lane == 0, one, zero), vp)
            vt_ref[0, he * PAIR:(he + 1) * PAIR, :] = ve.astype(F32).T.astype(BF16)
            vt_ref[0, ho * PAIR:(ho + 1) * PAIR, :] = vo.astype(F32).T.astype(BF16)

    return pl.pallas_call(
        body, name="fox_prep", grid=(nb * s // tf,),
        in_specs=[pl.BlockSpec((tf, w), lambda r: (r, 0)), pl.BlockSpec((tf, w), lambda r: (r, 1)),
                  pl.BlockSpec((tf, LANES), lambda r: (r, 0)), pl.BlockSpec((LANES, FOX_HEADS * PAIR), lambda r: (0, 0))],
        out_specs=[pl.BlockSpec((tf, FOX_HEADS * PAIR), lambda r: (r, 0)),
                   pl.BlockSpec((1, FOX_HEADS * PAIR, tf), lambda r: (r, 0, 0))],
        out_shape=[SDS((nb * s, FOX_HEADS * PAIR), BF16), SDS((nb * s // tf, FOX_HEADS * PAIR, tf), BF16)],
        compiler_params=_params(("parallel",)))(kv, kv, fneg, _placement())


def fox_fwd_t(pq, kaug, vaug_t, nb, s):
    tf = min(TF, s)
    n = s // tf
    w = TOK_WIDTH
    wa = FOX_HEADS * PAIR

    def body(q_ref, k_hbm, vt_hbm, ob_ref, of_ref, lse_ref, k_vm, vt_vm, qx_scr, m_scr, acc_scr, sems):
        b = pl.program_id(0)
        i = pl.program_id(1)

        @pl.when(i == 0)
        def _():
            ck = pltpu.make_async_copy(k_hbm.at[pl.ds(pl.multiple_of(b * s, tf), s)], k_vm, sems.at[0])
            cv = pltpu.make_async_copy(vt_hbm.at[pl.ds(b * n, n)], vt_vm, sems.at[1])
            ck.start()
            cv.start()
            ck.wait()
            cv.wait()

        lane = lax.broadcasted_iota(jnp.int32, (tf, PAIR), 1)
        one = jnp.ones((tf, PAIR), BF16)
        zero = jnp.zeros((tf, PAIR), BF16)
        for p in range(N_PAIRS):
            qp = q_ref[:, p * PAIR:(p + 1) * PAIR]
            be, bo = _bias_lane(2 * p), _bias_lane(2 * p + 1)
            ones_e = jnp.where((lane >= be) & (lane < be + BIAS_TERMS), one, zero)
            ones_o = jnp.where((lane >= bo) & (lane < bo + BIAS_TERMS), one, zero)
            qx_scr[2 * p] = jnp.where(lane < HEAD_DIM, qp, ones_e)
            qx_scr[2 * p + 1] = jnp.where(lane < HEAD_DIM, ones_o, qp)
        m_scr[...] = jnp.full(m_scr.shape, NEG_BIG, F32)
        acc_scr[...] = jnp.zeros_like(acc_scr)

        def tile(j, masked):
            ks = pl.multiple_of(j * tf, tf)
            if masked:
                keep = lax.broadcasted_iota(jnp.int32, (tf, tf), 1) >= lax.broadcasted_iota(jnp.int32, (tf, tf), 0)
            def scores(h):
                kx = k_vm[pl.ds(ks, tf), h * PAIR:(h + 1) * PAIR]
                return lax.dot_general(kx, qx_scr[h], NT_DIMS, preferred_element_type=F32)

            def values(h, pr, a):
                pv = jnp.dot(vt_vm[j, h * PAIR:(h + 1) * PAIR, :], pr, preferred_element_type=F32)
                acc_scr[h] = a * acc_scr[h] + pv

            ahead = [scores(h) for h in range(LOOKAHEAD)]
            behind = []
            for h in range(FOX_HEADS):
                sc = ahead.pop(0)
                if h + LOOKAHEAD < FOX_HEADS:
                    ahead.append(scores(h + LOOKAHEAD))
                if masked:
                    sc = jnp.where(keep, sc, NEG_BIG)
                m_prev = m_scr[h]
                m_new = jnp.maximum(m_prev, jnp.max(sc, axis=0, keepdims=True))
                m_scr[h] = m_new
                behind.append((h, jnp.exp(sc - m_new).astype(BF16), jnp.exp(m_prev - m_new)))
                if len(behind) > FOLLOW_FWD:
                    values(*behind.pop(0))
            for item in behind:
                values(*item)

        def step(j, carry):
            tile(j, False)
            return carry

        lax.fori_loop(0, i, step, 0)
        tile(i, True)

        top = lax.broadcasted_iota(jnp.int32, (PAIR, tf), 0) < HEAD_DIM
        sub = lax.broadcasted_iota(jnp.int32, (16, tf), 0)
        lse = jnp.zeros((16, tf), F32)
        for p in range(N_PAIRS):
            he, ho = 2 * p, 2 * p + 1
            le = acc_scr[he, HEAD_DIM:HEAD_DIM + 1, :]
            lod = acc_scr[ho, 0:1, :]
            o = jnp.where(top, acc_scr[he] / le, acc_scr[ho] / lod).T
            ob_ref[:, p * PAIR:(p + 1) * PAIR] = o.astype(BF16)
            of_ref[:, p * PAIR:(p + 1) * PAIR] = o
            lse = jnp.where(sub == he, m_scr[he] + jnp.log(le), lse)
            lse = jnp.where(sub == ho, m_scr[ho] + jnp.log(lod), lse)
        lse_ref[0] = lse

    qrow = lambda b, i: (b * n + i, 0)
    return pl.pallas_call(
        body, name="fox_fwd", grid=(nb, n),
        in_specs=[pl.BlockSpec((tf, w), qrow), ANY_SPEC, ANY_SPEC],
        out_specs=[pl.BlockSpec((tf, w), qrow), pl.BlockSpec((tf, w), qrow),
                   pl.BlockSpec((1, 16, tf), lambda b, i: (b * n + i, 0, 0))],
        out_shape=[SDS((nb * s, w), BF16), SDS((nb * s, w), F32), SDS((nb * n, 16, tf), F32)],
        scratch_shapes=[pltpu.VMEM((s, wa), BF16), pltpu.VMEM((n, wa, tf), BF16),
                        pltpu.VMEM((FOX_HEADS, tf, PAIR), BF16), pltpu.VMEM((FOX_HEADS, 1, tf), F32),
                        pltpu.VMEM((FOX_HEADS, PAIR, tf), F32), pltpu.SemaphoreType.DMA((2,))],
        compiler_params=_params(("arbitrary", "arbitrary")))(pq, kaug, vaug_t)


def fox_delta(dcat, o, nb, s):
    tf = min(TM, s)
    w = TOK_WIDTH

    def body(do_ref, o_ref, dl_ref):
        out = jnp.zeros((tf, LANES), F32)
        for h in range(FOX_HEADS):
            lo, hi = h * HEAD_DIM, (h + 1) * HEAD_DIM
            out = out + _lane_put((tf, LANES), h, jnp.sum(do_ref[:, lo:hi] * o_ref[:, lo:hi], axis=1, keepdims=True))
        dl_ref[...] = out

    row = pl.BlockSpec((tf, w), lambda r: (r, 0))
    return pl.pallas_call(
        body, name="fox_delta", grid=(nb * s // tf,), in_specs=[row, row],
        out_specs=pl.BlockSpec((tf, LANES), lambda r: (r, 0)), out_shape=SDS((nb * s, LANES), F32),
        compiler_params=_params(("parallel",)))(dcat, o)


def fox_bwd(pq, kv, fneg, dcat_bf, lse_rows, delta_rows, nb, s):
    tf = min(TF, s)
    n = s // tf
    w = TOK_WIDTH

    def body(q_hbm, k_ref, v_ref, f_ref, do_hbm, lse_ref, dl_ref, dq_ref, dk_ref, dv_ref, dfk_ref, dfq_ref,
             q_vm, do_vm, km_scr, vm_scr, kt_scr, fk_scr, dk_scr, dv_scr, rs_scr, dq_scr, fq_scr, sems):
        b = pl.program_id(0)
        j = pl.program_id(1)

        @pl.when(j == 0)
        def _():
            rows = pl.ds(pl.multiple_of(b * s, tf), s)
            cq = pltpu.make_async_copy(q_hbm.at[rows, pl.ds(0, w)], q_vm, sems.at[0])
            cd = pltpu.make_async_copy(do_hbm.at[rows, pl.ds(0, w)], do_vm, sems.at[1])
            cq.start()
            cd.start()
            dq_scr[...] = jnp.zeros_like(dq_scr)
            fq_scr[...] = jnp.zeros_like(fq_scr)
            cq.wait()
            cd.wait()

        for p in range(N_PAIRS):
            kp = k_ref[:, p * PAIR:(p + 1) * PAIR] * QK_SCALE
            ke, ko = _split_pair(kp)
            km_scr[2 * p] = ke
            km_scr[2 * p + 1] = ko
            kt_scr[p] = kp.astype(F32).T.astype(BF16)
            ve, vo = _split_pair(v_ref[:, p * PAIR:(p + 1) * PAIR])
            vm_scr[2 * p] = ve
            vm_scr[2 * p + 1] = vo
        for h in range(FOX_HEADS):
            fk_scr[h] = jnp.broadcast_to(f_ref[:, h:h + 1], (tf, tf))
        dk_scr[...] = jnp.zeros_like(dk_scr)
        dv_scr[...] = jnp.zeros_like(dv_scr)
        rs_scr[...] = jnp.zeros_like(rs_scr)

        def tile(i, masked):
            qs = pl.multiple_of(i * tf, tf)
            if masked:
                keep = lax.broadcasted_iota(jnp.int32, (tf, tf), 1) >= lax.broadcasted_iota(jnp.int32, (tf, tf), 0)
            def products(h):
                qp = q_vm[pl.ds(qs, tf), (h // 2) * PAIR:(h // 2 + 1) * PAIR]
                dop = do_vm[pl.ds(qs, tf), (h // 2) * PAIR:(h // 2 + 1) * PAIR]
                return (lax.dot_general(km_scr[h], qp, NT_DIMS, preferred_element_type=F32),
                        lax.dot_general(vm_scr[h], dop, NT_DIMS, preferred_element_type=F32))

            def dependents(h, prb, dsb):
                p = h // 2
                half = slice((h % 2) * HEAD_DIM, (h % 2 + 1) * HEAD_DIM)
                qp = q_vm[pl.ds(qs, tf), p * PAIR:(p + 1) * PAIR]
                dop = do_vm[pl.ds(qs, tf), p * PAIR:(p + 1) * PAIR]
                dv_scr[h] += jnp.dot(prb, dop, preferred_element_type=F32)
                dk_scr[h] += jnp.dot(dsb, qp, preferred_element_type=F32)
                dqt = jnp.dot(kt_scr[p], dsb, preferred_element_type=F32)
                dq_scr[i, p, half, :] += dqt[(h % 2) * HEAD_DIM:(h % 2 + 1) * HEAD_DIM]

            ahead = [products(h) for h in range(LOOKAHEAD_BWD)]
            behind = []
            for h in range(FOX_HEADS):
                sc, dp = ahead.pop(0)
                if h + LOOKAHEAD_BWD < FOX_HEADS:
                    ahead.append(products(h + LOOKAHEAD_BWD))
                sc = sc + fk_scr[h] - lse_ref[i, h:h + 1, :]
                if masked:
                    sc = jnp.where(keep, sc, NEG_BIG)
                pr = jnp.exp(sc)
                ds = pr * (dp - dl_ref[i, h:h + 1, :])
                part = ds[:, :LANES]
                for c in range(1, tf // LANES):
                    part = part + ds[:, c * LANES:(c + 1) * LANES]
                rs_scr[h] += part
                fq_scr[i, h:h + 1, :] += jnp.sum(ds, axis=0, keepdims=True)
                behind.append((h, pr.astype(BF16), ds.astype(BF16)))
                if len(behind) > FOLLOW_BWD:
                    dependents(*behind.pop(0))
            for item in behind:
                dependents(*item)

        def step(i, carry):
            tile(i, False)
            return carry

        tile(j, True)
        for p in range(N_PAIRS):
            dq_ref[:, p * PAIR:(p + 1) * PAIR] = dq_scr[j, p].T.astype(BF16)
        dfq_ref[0] = fq_scr[j]
        lax.fori_loop(j + 1, n, step, 0)

        lo = _half_masks(tf)
        dfk = jnp.zeros((tf, LANES), F32)
        for p in range(N_PAIRS):
            dk = jnp.where(lo, dk_scr[2 * p], dk_scr[2 * p + 1]) * QK_SCALE
            dk_ref[:, p * PAIR:(p + 1) * PAIR] = dk.astype(BF16)
            dv_ref[:, p * PAIR:(p + 1) * PAIR] = jnp.where(lo, dv_scr[2 * p], dv_scr[2 * p + 1]).astype(BF16)
            for h in (2 * p, 2 * p + 1):
                dfk = dfk - _lane_put((tf, LANES), h, jnp.sum(rs_scr[h], axis=1, keepdims=True))
        dfk_ref[...] = dfk

    krow = lambda b, j: (b * n + j, 0)
    rows = pl.BlockSpec((n, 16, tf), lambda b, j: (b, 0, 0))
    tile_out = pl.BlockSpec((tf, w), krow)
    return pl.pallas_call(
        body, name="fox_bwd", grid=(nb, n),
        in_specs=[ANY_SPEC, pl.BlockSpec((tf, w), krow), pl.BlockSpec((tf, w), lambda b, j: (b * n + j, 1)),
                  pl.BlockSpec((tf, LANES), krow), ANY_SPEC, rows, rows],
        out_specs=[tile_out, tile_out, tile_out, pl.BlockSpec((tf, LANES), krow),
                   pl.BlockSpec((1, 16, tf), lambda b, j: (b * n + j, 0, 0))],
        out_shape=[SDS((nb * s, w), BF16), SDS((nb * s, w), BF16), SDS((nb * s, w), BF16), SDS((nb * s, LANES), F32),
                   SDS((nb * n, 16, tf), F32)],
        scratch_shapes=[pltpu.VMEM((s, w), BF16), pltpu.VMEM((s, w), BF16),
                        pltpu.VMEM((FOX_HEADS, tf, PAIR), BF16), pltpu.VMEM((FOX_HEADS, tf, PAIR), BF16),
                        pltpu.VMEM((N_PAIRS, PAIR, tf), BF16), pltpu.VMEM((FOX_HEADS, tf, tf), F32),
                        pltpu.VMEM((FOX_HEADS, tf, PAIR), F32), pltpu.VMEM((FOX_HEADS, tf, PAIR), F32),
                        pltpu.VMEM((FOX_HEADS, tf, LANES), F32), pltpu.VMEM((n, N_PAIRS, PAIR, tf), F32),
                        pltpu.VMEM((n, 16, tf), F32), pltpu.SemaphoreType.DMA((2,))],
        compiler_params=_params(("arbitrary", "arbitrary")))(pq, kv, kv, fneg, dcat_bf, lse_rows, delta_rows)


def reduce_adamw(parts, w, m, v, name):
    _, r, c = parts.shape
    tr = r
    for cand in range(16, r, 16):
        if r % cand == 0 and cand * c <= 128 * 1024:
            tr = cand
    c1 = 1.0 - ADAM_B1 ** ADAM_STEP
    c2 = 1.0 - ADAM_B2 ** ADAM_STEP

    def body(p_ref, w_ref, m_ref, v_ref, g_out, d_out, m_out, v_out):
        g = p_ref[0].astype(F32)
        for k in range(1, N_DEV):
            g = g + p_ref[k].astype(F32)
        mn = ADAM_B1 * m_ref[...] + (1.0 - ADAM_B1) * g
        vn = ADAM_B2 * v_ref[...] + (1.0 - ADAM_B2) * (g * g)
        g_out[...] = g
        m_out[...] = mn
        v_out[...] = vn
        d_out[...] = -ADAM_LR * ((mn / c1) / (jnp.sqrt(vn / c2) + ADAM_EPS) + ADAM_WD * w_ref[...])

    row = pl.BlockSpec((tr, c), lambda i: (i, 0))
    return pl.pallas_call(
        body, name=name, grid=(r // tr,),
        in_specs=[pl.BlockSpec((N_DEV, tr, c), lambda i: (0, i, 0)), row, row, row],
        out_specs=[row, row, row, row], out_shape=[SDS((r, c), F32)] * 4,
        compiler_params=_params(("parallel",)))(parts, w, m, v)


N_PEERS = N_DEV - 1
HBM_SPEC = pl.BlockSpec(memory_space=pltpu.HBM)
SEM_SPEC = pl.BlockSpec(memory_space=pltpu.SEMAPHORE)
ANY_SPEC = pl.BlockSpec(memory_space=pl.ANY)
SPLIT_EFFECT = pltpu.SideEffectType.DATAFLOW_SIDE_EFFECTING


def _my_index():
    return 4 * lax.axis_index("x") + 2 * lax.axis_index("y") + lax.axis_index("c")


def _peers():
    x, y, c = lax.axis_index("x"), lax.axis_index("y"), lax.axis_index("c")
    peers = []
    for k in range(1, N_DEV):
        px = 1 - x if (k >> 2) & 1 else x
        py = 1 - y if (k >> 1) & 1 else y
        pc = 1 - c if k & 1 else c
        peers.append(((px, py, pc), 4 * px + 2 * py + pc))
    return 4 * x + 2 * y + c, peers


def _push(src, dst, send_sems, recv_sems, slot, dev):
    return pltpu.make_async_remote_copy(src_ref=src, dst_ref=dst, send_sem=send_sems.at[slot], recv_sem=recv_sems.at[slot],
                                        device_id=dev, device_id_type=pl.DeviceIdType.MESH)


def _landing_shapes(arrs, scatter):
    return [SDS((N_DEV,) + tuple(a.shape[1:] if sc else a.shape), a.dtype) for a, sc in zip(arrs, scatter)]


def exchange(arrs, scatter, name):
    na = len(arrs)

    def body(*refs):
        ins = refs[:na]
        outs = refs[na:2 * na]
        send_sems, recv_sems, local_sems = refs[2 * na:]
        me, peers = _peers()
        local = []
        remote = []
        for a in range(na):
            lc = pltpu.make_async_copy(ins[a].at[me] if scatter[a] else ins[a], outs[a].at[me], local_sems.at[a])
            lc.start()
            local.append(lc)
            for k, (dev, idx) in enumerate(peers):
                cp = _push(ins[a].at[idx] if scatter[a] else ins[a], outs[a].at[me], send_sems, recv_sems,
                           a * N_PEERS + k, dev)
                cp.start()
                remote.append(cp)
        for a in range(na):
            for k, (dev, idx) in enumerate(peers):
                _push(ins[a].at[me] if scatter[a] else ins[a], outs[a].at[idx], send_sems, recv_sems,
                      a * N_PEERS + k, dev).wait_recv()
        for cp in remote:
            cp.wait_send()
        for lc in local:
            lc.wait()

    return pl.pallas_call(
        body, name=name, in_specs=[HBM_SPEC] * na, out_specs=[HBM_SPEC] * na, out_shape=_landing_shapes(arrs, scatter),
        scratch_shapes=[pltpu.SemaphoreType.DMA((na * N_PEERS,)), pltpu.SemaphoreType.DMA((na * N_PEERS,)),
                        pltpu.SemaphoreType.DMA((na,))])(*arrs)


def exchange_start(arrs, scatter, after, name):
    na = len(arrs)
    lands = [lax.empty(l.shape, l.dtype) for l in _landing_shapes(arrs, scatter)]

    def body(*refs):
        ins = refs[:na]
        land = refs[na:2 * na]
        send_sems, recv_sems = refs[2 * na + 1], refs[2 * na + 2]
        token = refs[-1]
        me, peers = _peers()
        for a in range(na):
            for k, (dev, idx) in enumerate(peers):
                _push(ins[a].at[idx] if scatter[a] else ins[a], land[a].at[me], send_sems, recv_sems,
                      a * N_PEERS + k, dev).start()
        token[...] = jnp.zeros_like(token)

    thru = [pltpu.HBM(a.shape, a.dtype) for a in arrs] + [pltpu.HBM(l.shape, l.dtype) for l in lands]
    res = pl.pallas_call(
        body, name=name,
        out_shape=(pltpu.SemaphoreType.DMA((na * N_PEERS,)), pltpu.SemaphoreType.DMA((na * N_PEERS,)), *thru,
                   SDS((8, LANES), F32)),
        in_specs=[HBM_SPEC] * (2 * na) + [ANY_SPEC],
        out_specs=(SEM_SPEC, SEM_SPEC, *([HBM_SPEC] * (2 * na)), pl.BlockSpec(memory_space=pltpu.VMEM)),
        input_output_aliases={i: 2 + i for i in range(2 * na)},
        compiler_params=pltpu.CompilerParams(has_side_effects=SPLIT_EFFECT),
    )(*[pltpu.with_memory_space_constraint(a, pltpu.HBM) for a in arrs],
      *[pltpu.with_memory_space_constraint(l, pltpu.HBM) for l in lands], after)
    return {"send": res[0], "recv": res[1], "src": res[2:2 + na], "land": res[2 + na:2 + 2 * na],
            "token": res[-1][0, 0], "scatter": scatter}


def exchange_wait(handle, after, name):
    scatter = handle["scatter"]
    na = len(scatter)

    def body(*refs):
        src = refs[:na]
        land = refs[na:2 * na]
        send_sems, recv_sems = refs[2 * na], refs[2 * na + 1]
        me, peers = _peers()
        for a in range(na):
            for k, (dev, idx) in enumerate(peers):
                cp = _push(src[a].at[me] if scatter[a] else src[a], land[a].at[idx], send_sems, recv_sems,
                           a * N_PEERS + k, dev)
                cp.wait_send()
                cp.wait_recv()

    ops = list(handle["src"]) + list(handle["land"])
    res = pl.pallas_call(
        body, name=name, out_shape=tuple(pltpu.HBM(o.shape, o.dtype) for o in ops),
        in_specs=[HBM_SPEC] * (2 * na) + [SEM_SPEC, SEM_SPEC, ANY_SPEC], out_specs=tuple([HBM_SPEC] * (2 * na)),
        input_output_aliases={i: i for i in range(2 * na)},
        compiler_params=pltpu.CompilerParams(has_side_effects=SPLIT_EFFECT),
    )(*ops, handle["send"], handle["recv"], after)
    me = _my_index()
    out = []
    for a in range(na):
        own = lax.dynamic_index_in_dim(res[a], me, 0, keepdims=True) if scatter[a] else res[a][None]
        out.append(lax.dynamic_update_slice(res[na + a], own, (me,) + (0,) * (own.ndim - 1)))
    return out


def forward_layer(l, xin, xin_bf, mem_bf, wt, nb, s, ffn_weights=None):
    sv = {"xin_bf": xin_bf}
    memkv = mm_nn(mem_bf, wt["memw"], BF16, f"memkv{l}")
    sv["memkv"] = memkv
    if l == 0:
        proj = mm_nn(xin_bf, wt["win_a"], F32, "proj_a")
        pooled, tok = pool_fwd(proj, wt["pw_bd"], wt["pscale"], nb, s)
        sv["pooled"] = pooled
    else:
        kv = mm_nn(xin_bf, wt["kvw"][:, :2 * TOK_WIDTH], BF16, "kv_proj")
        fl = mm_nn(xin_bf, wt["kvw"][:, 2 * TOK_WIDTH:], F32, "gate_proj")
        fneg = -fgate_fwd(fl, wt["fb"], nb, s)
        proj = mm_nn(xin_bf, wt["wq"], BF16, "proj_b")
        kaug, vaug_t = fox_prep(kv, fneg, nb, s)
        tok, o_f32, lse_rows = fox_fwd_t(proj, kaug, vaug_t, nb, s)
        sv.update(kv=kv, fl=fl, fneg=fneg, o_f32=o_f32, lse_rows=lse_rows)
    sv["proj"] = proj
    mem_out = memattn_fwd(proj, memkv, nb, s, f"memattn_fwd{l}")
    cat = jnp.concatenate([tok, mem_out], axis=1)
    sv["cat"] = cat
    mix = mm_nn(cat, wt["wout"], F32, f"out_proj{l}")
    x1, x1_bf, xh1, rs1 = ln_fwd(xin, mix, wt["ln1_g"], wt["ln1_b"], f"ln1_fwd{l}")
    sv.update(x1_bf=x1_bf, xh1=xh1, rs1=rs1)
    if ffn_weights is not None:
        wt.update(ffn_weights(x1_bf))
    act, ga, gb, hu, hg = ffn_up_gate(x1_bf, wt["wup"], wt["cw"], nb, s, f"ffn_up_gate{l}")
    sv.update(act=act, ga=ga, gb=gb, hu=hu, hg=hg)
    ffn = mm_nn(act, wt["wdown"], F32, f"ffn_down{l}")
    x2, x2_bf, xh2, rs2 = ln_fwd(x1, ffn, wt["ln2_g"], wt["ln2_b"], f"ln2_fwd{l}")
    sv.update(xh2=xh2, rs2=rs2)
    return x2, x2_bf, sv


def backward_layer(l, dy, sv, mem_bf, wt, nb, s, after_ffn=None):
    g = {}
    dr2, dr2_bf, g["ln2_g"], g["ln2_b"] = ln_bwd(dy, sv["xh2"], sv["rs2"], wt["ln2_g"], f"ln2_bwd{l}")
    dact = mm_nn(dr2_bf, wt["wdown_t"], BF16, f"ffn_down_dx{l}")
    g["wdown"] = mm_tn(sv["act"], dr2_bf, f"ffn_down_dw{l}")
    dh_u, dh_g, dcw_u, dcw_g = gate_conv_bwd(dact, sv["ga"], sv["gb"], sv["hu"], sv["hg"], wt["cw"], nb, s,
                                             f"gate_conv_bwd{l}")
    g["cw"] = jnp.concatenate([dcw_u, dcw_g], axis=0)
    half = FF_PAIRS * FF_BLOCK_PAD
    dx1 = mm_nn(dh_u, wt["wup_t"][:half], F32, f"ffn_up_dx_u{l}", addend=dr2, add_scale=DN_ALPHA)
    dx1 = mm_nn(dh_g, wt["wup_t"][half:], F32, f"ffn_up_dx_g{l}", addend=dx1)
    g["wup"] = jnp.concatenate([mm_tn(sv["x1_bf"], dh_u, f"ffn_up_dw_u{l}", blocked=True),
                                mm_tn(sv["x1_bf"], dh_g, f"ffn_up_dw_g{l}", blocked=True)], axis=0)
    ln1_g = wt["ln1_g"] if after_ffn is None else wt["ln1_g"] + after_ffn(g, dx1)
    dr1, dr1_bf, g["ln1_g"], g["ln1_b"] = ln_bwd(dx1, sv["xh1"], sv["rs1"], ln1_g, f"ln1_bwd{l}")
    dcat, dcat_bf = mm_nn(dr1_bf, wt["wout_t"], F32, f"out_proj_dx{l}", also_bf16=True)
    g["wout"] = mm_tn(sv["cat"], dr1_bf, f"out_proj_dw{l}")
    dqm, dmemkv = memattn_bwd(sv["proj"], sv["memkv"], dcat, nb, s, f"memattn_bwd{l}")
    g["memw"] = mm_tn(mem_bf, dmemkv, f"memkv_dw{l}")
    if l == 0:
        dmixed, dpooled, g["pscale"] = pool_bwd_mix(dcat, sv["pooled"], wt["pw_bd"], wt["pw_bd_t"], wt["pscale"], nb, s)
        g["pw_full"] = mm_tn(sv["pooled"], dmixed, "pool_dw")
        du = pool_bwd_window(dpooled, nb, s)
        dproj = jnp.concatenate([du, dqm], axis=1)
        dx = mm_nn(dproj, wt["win_a_t"], F32, "proj_a_dx", addend=dr1, add_scale=DN_ALPHA)
        g["win_a"] = mm_tn(sv["xin_bf"], dproj, "proj_a_dw")
    else:
        delta = fox_delta(dcat, sv["o_f32"], nb, s)
        tf = min(TF, s)
        dq, dk, dv, dfcum_k, dfq_rows = fox_bwd(sv["proj"], sv["kv"], sv["fneg"], dcat_bf,
                                                sv["lse_rows"], _to_tile_rows(delta, nb, s, tf), nb, s)
        dfl, g["fb"] = fgate_bwd(_from_tile_rows(dfq_rows), dfcum_k, sv["fl"], wt["fb"], nb, s)
        dproj = jnp.concatenate([dq, dqm], axis=1)
        dkvf = jnp.concatenate([dk, dv, dfl.astype(BF16)], axis=1)
        dx = mm_nn(dproj, wt["wq_t"], F32, "proj_b_dx", addend=dr1, add_scale=DN_ALPHA)
        dx = mm_nn(dkvf, wt["kvw_t"], F32, "kv_proj_dx", addend=dx)
        g["wq"] = mm_tn(sv["xin_bf"], dproj, "proj_b_dw")
        g["kvw"] = mm_tn(sv["xin_bf"], dkvf, "kv_proj_dw")
    return dx, g


def pack_replicated(pool_w, ln1_g, ln1_b, ln2_g, ln2_b, conv_b, f_b):
    cb = jnp.pad(conv_b, ((0, 0), (0, 6144 - 5504))).reshape(12, D_MODEL)
    fb = jnp.pad(f_b.reshape(1, FOX_HEADS), ((0, 3), (0, D_MODEL - FOX_HEADS)))
    return jnp.concatenate([pool_w.reshape(144, D_MODEL), ln1_g, ln1_b, ln2_g, ln2_b, cb, fb], axis=0)


def unpack_replicated(buf):
    pool_w = buf[:144].reshape(1, 4, POOL_GROUP, POOL_GROUP)
    ln = [buf[144 + 2 * k:146 + 2 * k] for k in range(4)]
    conv_b = buf[152:164].reshape(2, 6144)[:, :5504]
    f_b = buf[164, :FOX_HEADS]
    return pool_w, ln[0], ln[1], ln[2], ln[3], conv_b, f_b


def pack_small(conv_w, pool_scale):
    buf = jnp.zeros((16, FF_BLOCK_PAD), F32)
    buf = lax.dynamic_update_slice(buf, conv_w.reshape(DEPTH * 3, FF_BLOCK), (0, 0))
    return lax.dynamic_update_slice(buf, pool_scale, (8, 0))


def _block_diag(pw):
    out = jnp.zeros((TOK_WIDTH, TOK_WIDTH), pw.dtype)
    for g in range(4):
        out = lax.dynamic_update_slice(out, pw[g], (g * POOL_GROUP, g * POOL_GROUP))
    return out


def layer_shards(l, sq_a, sq_b, mem_w_kv, ffn_w_up, ffn_w_down):
    return [sq_a[0].astype(BF16), sq_b[0].astype(BF16), mem_w_kv[l].astype(BF16), ffn_w_up[l].astype(BF16),
            ffn_w_down[l].astype(BF16)]


def mixer_weights(l, gath, ln1_g, ln1_b, ln2_g, ln2_b):
    w_out = gath[1].reshape(D_MODEL, D_MODEL)
    wt = {"memw": gath[2].reshape(D_MODEL, 2 * MEM_WIDTH), "wout": w_out, "wout_t": w_out.T,
          "ln1_g": ln1_g[l:l + 1], "ln1_b": ln1_b[l:l + 1], "ln2_g": ln2_g[l:l + 1], "ln2_b": ln2_b[l:l + 1]}
    return wt, gath[0].reshape(D_MODEL, D_MODEL)


def ffn_weights(l, wup_g, wdown_g, small, conv_b):
    pad_c = FF_BLOCK_PAD - FF_BLOCK
    wup = jnp.pad(wup_g, ((0, 0), (0, 0), (0, pad_c))).transpose(1, 0, 2).reshape(D_MODEL, N_DEV * FF_BLOCK_PAD)
    wdown = jnp.pad(wdown_g.reshape(FF_PAIRS, FF_BLOCK, D_MODEL), ((0, 0), (0, pad_c), (0, 0)))
    wdown = wdown.reshape(FF_PAIRS * FF_BLOCK_PAD, D_MODEL)
    cb = jnp.pad(conv_b[l].reshape(N_DEV, FF_BLOCK), ((0, 0), (0, pad_c)))
    cw = jnp.concatenate([small[:, 3 * l:3 * l + 3, :], cb[:, None, :], jnp.zeros((N_DEV, 4, FF_BLOCK_PAD), F32)], axis=1)
    return {"wup": wup, "wup_t": wup.T, "wdown": wdown, "wdown_t": wdown.T, "cw": cw}


def mixer_grad_blocks(g, w_in_grad):
    blocks = [w_in_grad.reshape(N_DEV, 128, D_MODEL), g["wout"].reshape(N_DEV, 128, D_MODEL),
              g["memw"].reshape(N_DEV, 128, 2 * MEM_WIDTH)]
    return [b.astype(BF16) for b in blocks]


def ffn_grad_blocks(g):
    wup = g["wup"][:, :, :FF_BLOCK]
    wdown = g["wdown"].reshape(FF_PAIRS, FF_BLOCK_PAD, D_MODEL)[:, :FF_BLOCK].reshape(N_DEV, FF_ROWS, D_MODEL)
    return [wup.astype(BF16), wdown.astype(BF16)]


def small_grad_blocks(g0, g1):
    taps = jnp.stack([g0["cw"][:, :3, :], g1["cw"][:, :3, :]], axis=1).reshape(N_DEV, DEPTH * 3, FF_BLOCK_PAD)
    small = jnp.zeros((N_DEV, 16, FF_BLOCK_PAD), F32)
    small = lax.dynamic_update_slice(small, taps, (0, 0, 0))
    return lax.dynamic_update_slice(small, g0["pscale"].reshape(N_DEV, 1, 96), (0, 8, 0))


def replicated_grads(g0, g1):
    pw = jnp.stack([g0["pw_full"][k * POOL_GROUP:(k + 1) * POOL_GROUP, k * POOL_GROUP:(k + 1) * POOL_GROUP] for k in range(4)])
    conv_b = jnp.stack([g_["cw"][:, 3, :FF_BLOCK].reshape(N_DEV * FF_BLOCK) for g_ in (g0, g1)])
    ln = [jnp.concatenate([g0[n], g1[n]], axis=0) for n in ("ln1_g", "ln1_b", "ln2_g", "ln2_b")]
    return pack_replicated(pw[None], ln[0], ln[1], ln[2], ln[3], conv_b, g1["fb"][0, :FOX_HEADS])


def kernel(x, mem, a_w_in, a_pool_w, a_pool_scale, a_w_out, b_w_q, b_w_out, kv_w, f_b, mem_w_kv, ln1_g, ln1_b, ln2_g, ln2_b, ffn_w_up, ffn_conv_w, ffn_conv_b, ffn_w_down, loss_target, m_a_w_in, m_a_pool_w, m_a_pool_scale, m_a_w_out, m_b_w_q, m_b_w_out, m_kv_w, m_f_b, m_mem_w_kv, m_ln1_g, m_ln1_b, m_ln2_g, m_ln2_b, m_ffn_w_up, m_ffn_conv_w, m_ffn_conv_b, m_ffn_w_down, v_a_w_in, v_a_pool_w, v_a_pool_scale, v_a_w_out, v_b_w_q, v_b_w_out, v_kv_w, v_f_b, v_mem_w_kv, v_ln1_g, v_ln1_b, v_ln2_g, v_ln2_b, v_ffn_w_up, v_ffn_conv_w, v_ffn_conv_b, v_ffn_w_down):
    nb, s, d = x.shape
    t = nb * s
    x2d, mem_bf, target = x.reshape(t, d), mem.reshape(nb * MEM_LEN, d).astype(BF16), loss_target.reshape(t, d)

    shards0 = layer_shards(0, a_w_in, a_w_out, mem_w_kv, ffn_w_up, ffn_w_down)
    shards1 = layer_shards(1, b_w_q, b_w_out, mem_w_kv, ffn_w_up, ffn_w_down)
    shards1.append(jnp.pad(kv_w, ((0, 0), (0, KV_COLS_PAD - KV_COLS))).astype(BF16))
    gath0 = exchange(shards0[:3] + [pack_small(ffn_conv_w, a_pool_scale)], [False] * 4, "gather_w0_mixer")
    pending = {"ffn0": exchange_start(shards0[3:], [False] * 2, gath0[0], "gather_w0_ffn_start")}
    small = gath0[3]
    wt0, w_in = mixer_weights(0, gath0, ln1_g + pending["ffn0"]["token"], ln1_b, ln2_g, ln2_b)
    pw_bd = _block_diag(a_pool_w[0])
    wt0.update(win_a=w_in, win_a_t=w_in.T, pw_bd=pw_bd.astype(BF16), pw_bd_t=pw_bd.T.astype(BF16),
               pscale=small[:, 8, :96].reshape(1, TOK_WIDTH) + pending["ffn0"]["token"])

    def ffn0_weights(x1_bf):
        got = exchange_wait(pending["ffn0"], x1_bf, "gather_w0_ffn_wait")
        pending["w1"] = exchange_start(shards1, [False] * 6, got[0], "gather_w1_start")
        w = ffn_weights(0, got[0], got[1], small, ffn_conv_b)
        w["cw"] = w["cw"] + pending["w1"]["token"]
        return w

    x1, x1_bf, sv0 = forward_layer(0, x2d, x2d.astype(BF16), mem_bf, wt0, nb, s, ffn_weights=ffn0_weights)
    gath1 = exchange_wait(pending["w1"], x1_bf, "gather_w1_wait")
    wt1, w_q = mixer_weights(1, gath1, ln1_g, ln1_b, ln2_g, ln2_b)
    wt1.update(ffn_weights(1, gath1[3], gath1[4], small, ffn_conv_b))
    kvw = gath1[5].reshape(D_MODEL, KV_COLS_PAD)
    wt1.update(wq=w_q, wq_t=w_q.T, kvw=kvw, kvw_t=kvw.T,
               fb=jnp.pad(f_b.reshape(1, FOX_HEADS), ((0, 0), (0, LANES - FOX_HEADS))))
    y, _, sv1 = forward_layer(1, x1, x1_bf, mem_bf, wt1, nb, s)
    dy, loss_row = loss_head(y, target)
    loss = lax.psum(loss_row[0, 0], ("x", "y", "c"))

    dx1, g1 = backward_layer(1, dy, sv1, mem_bf, wt1, nb, s)
    blocks1 = (mixer_grad_blocks(g1, g1["wq"]) + ffn_grad_blocks(g1)
               + [g1["kvw"][:, :KV_COLS].reshape(N_DEV, 128, KV_COLS).astype(BF16)])
    pending["g1"] = exchange_start(blocks1, [True] * 6, dx1, "scatter_g1_start")
    wt0["ln2_g"] = wt0["ln2_g"] + pending["g1"]["token"]

    def after_ffn0(g, dxm):
        pending["gf0"] = exchange_start(ffn_grad_blocks(g), [True] * 2, dxm, "scatter_g0_ffn_start")
        return pending["gf0"]["token"]

    grad_x, g0 = backward_layer(0, dx1, sv0, mem_bf, wt0, nb, s, after_ffn=after_ffn0)
    last = mixer_grad_blocks(g0, g0["win_a"]) + [small_grad_blocks(g0, g1), replicated_grads(g0, g1)]
    pending["gm0"] = exchange_start(last, [True] * 4 + [False], grad_x, "scatter_g0_mixer_start")
    parts_f0 = exchange_wait(pending["gf0"], jnp.zeros((8, LANES), F32) + pending["gm0"]["token"], "scatter_g0_ffn_wait")
    parts1 = exchange_wait(pending["g1"], parts_f0[0], "scatter_g1_wait")

    res = {}

    def upd(nm, parts, w2, m2, v2):
        res[nm] = reduce_adamw(parts, w2, m2, v2, f"adamw_{nm}")

    upd("b_w_q", parts1[0], b_w_q[0], m_b_w_q[0], v_b_w_q[0])
    upd("b_w_out", parts1[1], b_w_out[0], m_b_w_out[0], v_b_w_out[0])
    upd("kv_w", parts1[5], kv_w, m_kv_w, v_kv_w)
    upd("mem_w_kv1", parts1[2], mem_w_kv[1], m_mem_w_kv[1], v_mem_w_kv[1])
    for l, parts in enumerate((parts_f0, parts1[3:5])):
        upd(f"ffn_w_up{l}", parts[0], ffn_w_up[l], m_ffn_w_up[l], v_ffn_w_up[l])
        upd(f"ffn_w_down{l}", parts[1], ffn_w_down[l], m_ffn_w_down[l], v_ffn_w_down[l])
    parts_m0 = exchange_wait(pending["gm0"], res["ffn_w_down1"][0], "scatter_g0_mixer_wait")
    upd("a_w_in", parts_m0[0], a_w_in[0], m_a_w_in[0], v_a_w_in[0])
    upd("a_w_out", parts_m0[1], a_w_out[0], m_a_w_out[0], v_a_w_out[0])
    upd("mem_w_kv0", parts_m0[2], mem_w_kv[0], m_mem_w_kv[0], v_mem_w_kv[0])
    upd("small", parts_m0[3], pack_small(ffn_conv_w, a_pool_scale), pack_small(m_ffn_conv_w, m_a_pool_scale),
        pack_small(v_ffn_conv_w, v_a_pool_scale))
    upd("replicated", parts_m0[4], pack_replicated(a_pool_w, ln1_g, ln1_b, ln2_g, ln2_b, ffn_conv_b, f_b),
        pack_replicated(m_a_pool_w, m_ln1_g, m_ln1_b, m_ln2_g, m_ln2_b, m_ffn_conv_b, m_f_b),
        pack_replicated(v_a_pool_w, v_ln1_g, v_ln1_b, v_ln2_g, v_ln2_b, v_ffn_conv_b, v_f_b))

    for nm in ("a_w_in", "a_w_out", "b_w_q", "b_w_out"):
        res[nm] = [o[None] for o in res[nm]]
    for nm in ("mem_w_kv", "ffn_w_up", "ffn_w_down"):
        res[nm] = [jnp.stack([a0, a1]) for a0, a1 in zip(res[nm + "0"], res[nm + "1"])]
    res["ffn_conv_w"] = [o[:DEPTH * 3, :FF_BLOCK].reshape(DEPTH, 3, FF_BLOCK) for o in res["small"]]
    res["a_pool_scale"] = [o[8:9, :96] for o in res["small"]]
    rep_names = ["a_pool_w", "ln1_g", "ln1_b", "ln2_g", "ln2_b", "ffn_conv_b", "f_b"]
    for nm in rep_names:
        res[nm] = []
    for o in res["replicated"]:
        for nm, val in zip(rep_names, unpack_replicated(o)):
            res[nm].append(val)

    order = ["a_w_in", "a_pool_w", "a_pool_scale", "a_w_out", "b_w_q", "b_w_out", "kv_w", "f_b", "mem_w_kv",
             "ln1_g", "ln1_b", "ln2_g", "ln2_b", "ffn_w_up", "ffn_conv_w", "ffn_conv_b", "ffn_w_down"]
    out = [loss, grad_x.reshape(nb, s, d)]
    for kind in range(4):
        out.extend(res[nm][kind] for nm in order)
    return tuple(out)
```

```python
import jax
import jax.numpy as jnp
from jax import lax
from jax.experimental import pallas as pl
from jax.experimental.pallas import tpu as pltpu

F32 = jnp.float32
BF16 = jnp.bfloat16
SDS = jax.ShapeDtypeStruct

N_DEV = 8
D_MODEL = 1024
TOK_WIDTH = 768
MEM_WIDTH = 256
MEM_LEN = 256
MEM_HEADS = 4
HEAD_DIM = 64
FOX_HEADS = 12
POOL_GROUP = 192
FF_BLOCK = 688
FF_BLOCK_PAD = 768
FF_PAIRS = 4
FF_ROWS = 344
KV_COLS = 1548
KV_COLS_PAD = 1664
LANES = 128
DEPTH = 2
DN_ALPHA = (2.0 * DEPTH) ** 0.25
LN_EPS = 1e-5
QK_SCALE = HEAD_DIM ** -0.5
NEG_BIG = -1e30

ADAM_LR = 0.001
ADAM_B1 = 0.9
ADAM_B2 = 0.999
ADAM_EPS = 1e-08
ADAM_WD = 0.01
ADAM_STEP = 10

VMEM_LIMIT_BYTES = 56 * 1024 * 1024
MM_BLOCK_BYTES = 6 * 1024 * 1024
TM = 512
TS = 256
TF = 256
TC = 256
HALO_POOL = 16
HALO_CONV = 8

NT_DIMS = (((1,), (1,)), ((), ()))
TN_DIMS = (((0,), (0,)), ((), ()))


def _params(sem=None):
    return pltpu.CompilerParams(dimension_semantics=sem, vmem_limit_bytes=VMEM_LIMIT_BYTES)


def _sigmoid(z):
    return 1.0 / (1.0 + jnp.exp(-z))


def _pick_tn(n):
    if n <= 2048:
        return n
    for t in (1024, 768, 512, 256, 128):
        if n % t == 0:
            return t
    return n


def mm_nn(a, b, out_dtype, name, addend=None, add_scale=1.0, also_bf16=False):
    m, k = a.shape
    _, n = b.shape
    tm = min(TM, m)
    tn = n
    while k * tn * 2 > MM_BLOCK_BYTES or tm * tn * 4 > MM_BLOCK_BYTES:
        tn //= 2
    chunk = tn if tn <= 2048 else _pick_tn(tn)
    has_add = addend is not None

    def body(*refs):
        a_ref, b_ref = refs[0], refs[1]
        c_ref = refs[2] if has_add else None
        o_ref = refs[3] if has_add else refs[2]
        ob_ref = refs[-1] if also_bf16 else None
        av = a_ref[...].astype(BF16)
        for c in range(tn // chunk):
            cols = slice(c * chunk, (c + 1) * chunk)
            r = jnp.dot(av, b_ref[:, cols].astype(BF16), preferred_element_type=F32)
            if has_add:
                r = r + add_scale * c_ref[:, cols]
            o_ref[:, cols] = r.astype(out_dtype)
            if also_bf16:
                ob_ref[:, cols] = r.astype(BF16)

    in_specs = [pl.BlockSpec((tm, k), lambda j, i: (i, 0)), pl.BlockSpec((k, tn), lambda j, i: (0, j))]
    ops = [a, b]
    tile = pl.BlockSpec((tm, tn), lambda j, i: (i, j))
    if has_add:
        in_specs.append(tile)
        ops.append(addend)
    out_shape = [SDS((m, n), out_dtype)]
    out_specs = [tile]
    if also_bf16:
        out_shape.append(SDS((m, n), BF16))
        out_specs.append(tile)
    res = pl.pallas_call(
        body, name=name, grid=(n // tn, m // tm), in_specs=in_specs, out_specs=out_specs, out_shape=out_shape,
        compiler_params=_params(("parallel", "parallel")))(*ops)
    return tuple(res) if also_bf16 else res[0]


def mm_tn(a, b, name, blocked=False):
    t, m = a.shape
    _, n = b.shape
    tt = min(2 * TM, t)
    tm = 1024 if m % 1024 == 0 else m
    tn = FF_BLOCK_PAD if blocked else _pick_tn(n)
    nt = t // tt

    def body(a_ref, b_ref, o_ref):
        kk = pl.program_id(2)
        r = lax.dot_general(a_ref[...].astype(BF16), b_ref[...].astype(BF16), TN_DIMS, preferred_element_type=F32)
        if blocked:
            r = r[None]

        @pl.when(kk == 0)
        def _():
            o_ref[...] = r

        @pl.when(kk != 0)
        def _():
            o_ref[...] += r

    if blocked:
        out_shape = SDS((n // tn, m, tn), F32)
        out_spec = pl.BlockSpec((1, tm, tn), lambda i, j, kk: (j, i, 0))
    else:
        out_shape = SDS((m, n), F32)
        out_spec = pl.BlockSpec((tm, tn), lambda i, j, kk: (i, j))
    return pl.pallas_call(
        body, name=name, grid=(m // tm, n // tn, nt),
        in_specs=[pl.BlockSpec((tt, tm), lambda i, j, kk: (kk, i)), pl.BlockSpec((tt, tn), lambda i, j, kk: (kk, j))],
        out_specs=out_spec, out_shape=out_shape,
        compiler_params=_params(("parallel", "parallel", "arbitrary")))(a, b)


def ln_fwd(xprev, a, w, g, b, name):
    t, d = xprev.shape
    k = a.shape[1]
    tm = min(TM, t)

    def body(xp_ref, a_ref, w_ref, g_ref, b_ref, y_ref, yb_ref, xh_ref, rs_ref):
        r = DN_ALPHA * xp_ref[...] + jnp.dot(a_ref[...], w_ref[...], preferred_element_type=F32)
        mu = jnp.mean(r, axis=1, keepdims=True)
        xc = r - mu
        var = jnp.mean(xc * xc, axis=1, keepdims=True)
        rstd = lax.rsqrt(var + LN_EPS)
        xh = xc * rstd
        y = xh * g_ref[...] + b_ref[...]
        y_ref[...] = y
        yb_ref[...] = y.astype(BF16)
        xh_ref[...] = xh
        rs_ref[...] = jnp.broadcast_to(rstd, (tm, LANES))

    row = pl.BlockSpec((tm, d), lambda i: (i, 0))
    vec = pl.BlockSpec((1, d), lambda i: (0, 0))
    return pl.pallas_call(
        body, name=name, grid=(t // tm,),
        in_specs=[row, pl.BlockSpec((tm, k), lambda i: (i, 0)), pl.BlockSpec((k, d), lambda i: (0, 0)), vec, vec],
        out_specs=[row, row, row, pl.BlockSpec((tm, LANES), lambda i: (i, 0))],
        out_shape=[SDS((t, d), F32), SDS((t, d), BF16), SDS((t, d), F32), SDS((t, LANES), F32)],
        compiler_params=_params(("parallel",)))(xprev, a, w, g, b)


def ln_bwd(dy, xhat, rstd, g, name, product=None, dy_scale=1.0):
    t, d = dy.shape
    tm = min(TM, t)
    fused = product is not None

    def body(*refs):
        if fused:
            a_ref, w_ref = refs[0], refs[1]
            refs = refs[2:]
        dy_ref, xh_ref, rs_ref, g_ref, dr_ref, drb_ref, dg_ref, db_ref = refs
        i = pl.program_id(0)
        dyv = dy_ref[...] if dy_scale == 1.0 else dy_scale * dy_ref[...]
        if fused:
            dyv = dyv + jnp.dot(a_ref[...], w_ref[...], preferred_element_type=F32)
        xh = xh_ref[...]
        dxh = dyv * g_ref[...]
        m1 = jnp.mean(dxh, axis=1, keepdims=True)
        m2 = jnp.mean(dxh * xh, axis=1, keepdims=True)
        dr = rs_ref[:, 0:1] * (dxh - m1 - xh * m2)
        dr_ref[...] = dr
        drb_ref[...] = dr.astype(BF16)

        @pl.when(i == 0)
        def _():
            dg_ref[...] = jnp.zeros_like(dg_ref)
            db_ref[...] = jnp.zeros_like(db_ref)

        dg_ref[...] += jnp.sum(dyv * xh, axis=0, keepdims=True)
        db_ref[...] += jnp.sum(dyv, axis=0, keepdims=True)

    row = pl.BlockSpec((tm, d), lambda i: (i, 0))
    vec = pl.BlockSpec((1, d), lambda i: (0, 0))
    in_specs = [row, row, pl.BlockSpec((tm, LANES), lambda i: (i, 0)), vec]
    ops = [dy, xhat, rstd, g]
    if fused:
        k = product[0].shape[1]
        in_specs = [pl.BlockSpec((tm, k), lambda i: (i, 0)), pl.BlockSpec((k, d), lambda i: (0, 0))] + in_specs
        ops = list(product) + ops
    return pl.pallas_call(
        body, name=name, grid=(t // tm,), in_specs=in_specs, out_specs=[row, row, vec, vec],
        out_shape=[SDS((t, d), F32), SDS((t, d), BF16), SDS((1, d), F32), SDS((1, d), F32)],
        compiler_params=_params(("arbitrary",)))(*ops)


def loss_head(y, target):
    t, d = y.shape
    tm = min(TM, t)
    nsteps = t // tm

    def body(y_ref, t_ref, dy_ref, l_ref, acc):
        i = pl.program_id(0)
        diff = y_ref[...] - t_ref[...]
        dy_ref[...] = diff * (1.0 / d)

        @pl.when(i == 0)
        def _():
            acc[...] = jnp.zeros_like(acc)

        acc[...] += jnp.sum(diff * diff, axis=0, keepdims=True)

        @pl.when(i == nsteps - 1)
        def _():
            tot = jnp.sum(acc[...], axis=1, keepdims=True) * (0.5 / d)
            l_ref[...] = jnp.broadcast_to(tot, (1, LANES))

    row = pl.BlockSpec((tm, d), lambda i: (i, 0))
    return pl.pallas_call(
        body, name="loss_head", grid=(nsteps,), in_specs=[row, row],
        out_specs=[row, pl.BlockSpec((1, LANES), lambda i: (0, 0))],
        out_shape=[SDS((t, d), F32), SDS((1, LANES), F32)],
        scratch_shapes=[pltpu.VMEM((1, d), F32)],
        compiler_params=_params(("arbitrary",)))(y, target)


def memattn_fwd(proj, memkv, nb, s, name):
    ts = min(TS, s)
    nq = s // ts

    def body(q_ref, kv_ref, o_ref):
        top = lax.broadcasted_iota(jnp.int32, (PAIR, ts), 0) < HEAD_DIM
        scores = []
        for p in range(MEM_HEADS // 2):
            qp = q_ref[:, p * PAIR:(p + 1) * PAIR].astype(BF16)
            ke, ko = _split_pair(kv_ref[:, p * PAIR:(p + 1) * PAIR], QK_SCALE)
            scores.append([lax.dot_general(km, qp, NT_DIMS, preferred_element_type=F32) for km in (ke, ko)])
        for p in range(MEM_HEADS // 2):
            vt = kv_ref[:, MEM_WIDTH + p * PAIR:MEM_WIDTH + (p + 1) * PAIR].astype(F32).T.astype(BF16)
            outs = []
            for sc in scores[p]:
                e = jnp.exp(sc - jnp.max(sc, axis=0, keepdims=True))
                pr = e / jnp.sum(e, axis=0, keepdims=True)
                outs.append(jnp.dot(vt, pr.astype(BF16), preferred_element_type=F32))
            o_ref[:, p * PAIR:(p + 1) * PAIR] = jnp.where(top, outs[0], outs[1]).T.astype(BF16)

    return pl.pallas_call(
        body, name=name, grid=(nb, nq),
        in_specs=[pl.BlockSpec((ts, MEM_WIDTH), lambda b, i: (b * nq + i, 3)),
                  pl.BlockSpec((MEM_LEN, 2 * MEM_WIDTH), lambda b, i: (b, 0))],
        out_specs=pl.BlockSpec((ts, MEM_WIDTH), lambda b, i: (b * nq + i, 0)),
        out_shape=SDS((nb * s, MEM_WIDTH), BF16),
        compiler_params=_params(("parallel", "parallel")))(proj, memkv)


def memattn_bwd(proj, memkv, dcat, nb, s, name):
    ts = min(TS, s)
    nq = s // ts

    def body(q_ref, kv_ref, do_ref, dq_ref, dkv_ref):
        i = pl.program_id(1)

        @pl.when(i == 0)
        def _():
            dkv_ref[...] = jnp.zeros_like(dkv_ref)

        lo = _half_masks(MEM_LEN)
        top = lax.broadcasted_iota(jnp.int32, (PAIR, ts), 0) < HEAD_DIM
        n_pairs = MEM_HEADS // 2
        qs, dos, kps, products = [], [], [], []
        for p in range(n_pairs):
            qp = q_ref[:, p * PAIR:(p + 1) * PAIR].astype(BF16)
            dop = do_ref[:, p * PAIR:(p + 1) * PAIR].astype(BF16)
            kp = kv_ref[:, p * PAIR:(p + 1) * PAIR] * QK_SCALE
            kms = _split_pair(kp)
            vms = _split_pair(kv_ref[:, MEM_WIDTH + p * PAIR:MEM_WIDTH + (p + 1) * PAIR])
            products.append([(lax.dot_general(km, qp, NT_DIMS, preferred_element_type=F32),
                              lax.dot_general(vm, dop, NT_DIMS, preferred_element_type=F32)) for km, vm in zip(kms, vms)])
            qs.append(qp)
            dos.append(dop)
            kps.append(kp)
        for p in range(n_pairs):
            kt = kps[p].astype(F32).T.astype(BF16)
            dks, dvs, dqs = [], [], []
            for sc, dp in products[p]:
                e = jnp.exp(sc - jnp.max(sc, axis=0, keepdims=True))
                pr = e / jnp.sum(e, axis=0, keepdims=True)
                dl = jnp.sum(pr * dp, axis=0, keepdims=True)
                ds = (pr * (dp - dl)).astype(BF16)
                dvs.append(jnp.dot(pr.astype(BF16), dos[p], preferred_element_type=F32))
                dks.append(jnp.dot(ds, qs[p], preferred_element_type=F32))
                dqs.append(jnp.dot(kt, ds, preferred_element_type=F32))
            dq_ref[:, p * PAIR:(p + 1) * PAIR] = jnp.where(top, dqs[0], dqs[1]).T.astype(BF16)
            dkv_ref[:, p * PAIR:(p + 1) * PAIR] += jnp.where(lo, dks[0], dks[1]) * QK_SCALE
            dkv_ref[:, MEM_WIDTH + p * PAIR:MEM_WIDTH + (p + 1) * PAIR] += jnp.where(lo, dvs[0], dvs[1])

    return pl.pallas_call(
        body, name=name, grid=(nb, nq),
        in_specs=[pl.BlockSpec((ts, MEM_WIDTH), lambda b, i: (b * nq + i, 3)),
                  pl.BlockSpec((MEM_LEN, 2 * MEM_WIDTH), lambda b, i: (b, 0)),
                  pl.BlockSpec((ts, MEM_WIDTH), lambda b, i: (b * nq + i, 3))],
        out_specs=[pl.BlockSpec((ts, MEM_WIDTH), lambda b, i: (b * nq + i, 0)),
                   pl.BlockSpec((MEM_LEN, 2 * MEM_WIDTH), lambda b, i: (b, 0))],
        out_shape=[SDS((nb * s, MEM_WIDTH), BF16), SDS((nb * MEM_LEN, 2 * MEM_WIDTH), F32)],
        compiler_params=_params(("parallel", "arbitrary")))(proj, memkv, dcat)


def _pool_select(shape, s2, s4, s8, s16):
    lane = lax.broadcasted_iota(jnp.int32, shape, 1)
    return jnp.where(lane < POOL_GROUP, s2, jnp.where(lane < 2 * POOL_GROUP, s4, jnp.where(lane < 3 * POOL_GROUP, s8, s16)))


def _pool_count(shape, first_pos):
    pos = first_pos + lax.broadcasted_iota(jnp.int32, shape, 0)
    win = _pool_select(shape, 2, 4, 8, 16)
    return jnp.minimum(pos + 1, win).astype(F32)


def pool_fwd(proj, pw_bd, pscale, nb, s):
    ts = min(TS, s)
    nq = s // ts
    w = TOK_WIDTH

    def body(c_ref, h_ref, w_ref, sc_ref, pooled_ref, tok_ref):
        i = pl.program_id(0) % nq
        cur = c_ref[...]
        halo = jnp.where(i == 0, 0.0, h_ref[...])
        xe = jnp.concatenate([halo, cur], axis=0)
        s2 = xe + pltpu.roll(xe, 1, axis=0)
        s4 = s2 + pltpu.roll(s2, 2, axis=0)
        s8 = s4 + pltpu.roll(s4, 4, axis=0)
        s16 = s8 + pltpu.roll(s8, 8, axis=0)
        hp = HALO_POOL
        ws = _pool_select((ts, w), s2[hp:], s4[hp:], s8[hp:], s16[hp:])
        pooled = (ws / _pool_count((ts, w), i * ts) - cur).astype(BF16)
        pooled_ref[...] = pooled
        mixed = jnp.dot(pooled, w_ref[...], preferred_element_type=F32)
        tok_ref[...] = (mixed * sc_ref[...]).astype(BF16)

    row = pl.BlockSpec((ts, w), lambda r: (r, 0))
    return pl.pallas_call(
        body, name="pool_fwd", grid=(nb * nq,),
        in_specs=[row, pl.BlockSpec((HALO_POOL, w), lambda r: (jnp.maximum(r * (ts // HALO_POOL) - 1, 0), 0)),
                  pl.BlockSpec((w, w), lambda r: (0, 0)), pl.BlockSpec((1, w), lambda r: (0, 0))],
        out_specs=[row, row], out_shape=[SDS((nb * s, w), BF16), SDS((nb * s, w), BF16)],
        compiler_params=_params(("parallel",)))(proj, proj, pw_bd, pscale)


def pool_bwd_mix(dcat, pooled, pw_bd, pw_bd_t, pscale, nb, s):
    ts = min(TS, s)
    w = TOK_WIDTH

    def body(dt_ref, p_ref, w_ref, wt_ref, sc_ref, dm_ref, dp_ref, ds_ref):
        r = pl.program_id(0)
        dtok = dt_ref[...]
        mixed = jnp.dot(p_ref[...], w_ref[...], preferred_element_type=F32)

        @pl.when(r == 0)
        def _():
            ds_ref[...] = jnp.zeros_like(ds_ref)

        ds_ref[...] += jnp.sum(dtok * mixed, axis=0, keepdims=True)
        dmx = (dtok * sc_ref[...]).astype(BF16)
        dm_ref[...] = dmx
        dp_ref[...] = jnp.dot(dmx, wt_ref[...], preferred_element_type=F32)

    row = pl.BlockSpec((ts, w), lambda r: (r, 0))
    mat = pl.BlockSpec((w, w), lambda r: (0, 0))
    vec = pl.BlockSpec((1, w), lambda r: (0, 0))
    return pl.pallas_call(
        body, name="pool_bwd_mix", grid=(nb * s // ts,), in_specs=[row, row, mat, mat, vec],
        out_specs=[row, row, vec], out_shape=[SDS((nb * s, w), BF16), SDS((nb * s, w), F32), SDS((1, w), F32)],
        compiler_params=_params(("arbitrary",)))(dcat, pooled, pw_bd, pw_bd_t, pscale)


def pool_bwd_window(dpooled, nb, s):
    ts = min(TS, s)
    nq = s // ts
    w = TOK_WIDTH
    n_ext = ts + HALO_POOL
    n_halo_blocks = nb * s // HALO_POOL

    def body(c_ref, n_ref, du_ref):
        i = pl.program_id(0) % nq
        cur = c_ref[...]
        nxt = jnp.where(i == nq - 1, 0.0, n_ref[...])
        ze = jnp.concatenate([cur, nxt], axis=0) / _pool_count((n_ext, w), i * ts)
        s2 = ze + pltpu.roll(ze, n_ext - 1, axis=0)
        s4 = s2 + pltpu.roll(s2, n_ext - 2, axis=0)
        s8 = s4 + pltpu.roll(s4, n_ext - 4, axis=0)
        s16 = s8 + pltpu.roll(s8, n_ext - 8, axis=0)
        ws = _pool_select((ts, w), s2[:ts], s4[:ts], s8[:ts], s16[:ts])
        du_ref[...] = (ws - cur).astype(BF16)

    row = pl.BlockSpec((ts, w), lambda r: (r, 0))
    return pl.pallas_call(
        body, name="pool_bwd_window", grid=(nb * nq,),
        in_specs=[row, pl.BlockSpec((HALO_POOL, w),
                                    lambda r: (jnp.minimum((r + 1) * (ts // HALO_POOL), n_halo_blocks - 1), 0))],
        out_specs=row, out_shape=SDS((nb * s, w), BF16),
        compiler_params=_params(("parallel",)))(dpooled, dpooled)


def _conv_rows(xe, w_ref):
    return (w_ref[0, 2:3, :] * xe + w_ref[0, 1:2, :] * pltpu.roll(xe, 1, axis=0)
            + w_ref[0, 0:1, :] * pltpu.roll(xe, 2, axis=0) + w_ref[0, 3:4, :])


def ffn_up_gate(x_bf, wup, cw, nb, s, name):
    tm = min(TM, s)
    nq = s // tm
    w = FF_BLOCK_PAD
    hr = 2 * HALO_CONV
    k = x_bf.shape[1]

    def body(xc_ref, xh_ref, wu_ref, wg_ref, cu_ref, cg_ref, act_ref, a_ref, b_ref, hu_ref, hg_ref):
        first = (pl.program_id(1) % nq) == 0
        xc = xc_ref[...]
        xh = xh_ref[...]

        def products(w_ref):
            return (jnp.dot(xc, w_ref[...], preferred_element_type=F32), jnp.dot(xh, w_ref[...], preferred_element_type=F32))

        def conv(hcur, hprev, c_ref, h_out):
            h_out[...] = hcur.astype(BF16)
            xe = jnp.concatenate([jnp.where(first, 0.0, hprev), hcur], axis=0)
            return _conv_rows(xe, c_ref)[hr:]

        pu, pg = products(wu_ref), products(wg_ref)
        cu = conv(*pu, cu_ref, hu_ref)
        cg = conv(*pg, cg_ref, hg_ref)
        sg = _sigmoid(cg)
        a = cg * sg
        act_ref[...] = (a * cu).astype(BF16)
        a_ref[...] = a.astype(BF16)
        b_ref[...] = (cu * (sg * (1.0 + cg * (1.0 - sg)))).astype(BF16)

    def wblock(off):
        return pl.BlockSpec((k, w), lambda j, r: (0, j + off))

    def cblock(off):
        return pl.BlockSpec((1, 8, w), lambda j, r: (j + off, 0, 0))

    tile = pl.BlockSpec((tm, w), lambda j, r: (r, j))
    out = SDS((nb * s, FF_PAIRS * w), BF16)
    return pl.pallas_call(
        body, name=name, grid=(FF_PAIRS, nb * nq),
        in_specs=[pl.BlockSpec((tm, k), lambda j, r: (r, 0)),
                  pl.BlockSpec((hr, k), lambda j, r: (jnp.maximum(r * (tm // hr) - 1, 0), 0)),
                  wblock(0), wblock(FF_PAIRS), cblock(0), cblock(FF_PAIRS)],
        out_specs=[tile] * 5, out_shape=[out] * 5,
        compiler_params=_params(("parallel", "parallel")))(x_bf, x_bf, wup, wup, cw, cw)


def gate_conv_bwd(dact, a, b, hu, hg, cw, nb, s, name):
    ts = min(TS, s)
    nq = s // ts
    w = FF_BLOCK_PAD
    hc = HALO_CONV
    hb = 2 * hc
    n_ext = ts + hc

    def body(dc_ref, dn_ref, ac_ref, an_ref, bc_ref, bn_ref, hu_ref, hg_ref, wu_ref, wg_ref,
             dhu_ref, dhg_ref, dwu_ref, dwg_ref):
        r = pl.program_id(1)
        last = (r % nq) == nq - 1

        def ext(c_ref, n_ref):
            return jnp.concatenate([c_ref[...].astype(F32), n_ref[...].astype(F32)[:hc]], axis=0)

        da = jnp.where(last & (lax.broadcasted_iota(jnp.int32, (n_ext, w), 0) >= ts), 0.0, ext(dc_ref, dn_ref))

        def branch(dcv, w_ref, h_ref, dh_ref, dw_ref):
            d0 = dcv[:ts]
            d1 = pltpu.roll(dcv, n_ext - 1, axis=0)[:ts]
            d2 = pltpu.roll(dcv, n_ext - 2, axis=0)[:ts]
            dh_ref[...] = (w_ref[0, 2:3, :] * d0 + w_ref[0, 1:2, :] * d1 + w_ref[0, 0:1, :] * d2).astype(BF16)
            hv = h_ref[...].astype(F32)
            rows = [jnp.sum(d2 * hv, axis=0, keepdims=True), jnp.sum(d1 * hv, axis=0, keepdims=True),
                    jnp.sum(d0 * hv, axis=0, keepdims=True), jnp.sum(d0, axis=0, keepdims=True)]
            sub = lax.broadcasted_iota(jnp.int32, (8, w), 0)
            upd = jnp.zeros((8, w), F32)
            for kk, rv in enumerate(rows):
                upd = jnp.where(sub == kk, rv, upd)

            @pl.when(r == 0)
            def _():
                dw_ref[...] = jnp.zeros_like(dw_ref)

            dw_ref[...] += upd[None]

        branch(da * ext(ac_ref, an_ref), wu_ref, hu_ref, dhu_ref, dwu_ref)
        branch(da * ext(bc_ref, bn_ref), wg_ref, hg_ref, dhg_ref, dwg_ref)

    cur = pl.BlockSpec((ts, w), lambda j, r: (r, j))
    nxt = pl.BlockSpec((hb, w), lambda j, r: (jnp.minimum((r + 1) * (ts // hb), nb * s // hb - 1), j))

    def wspec(off):
        return pl.BlockSpec((1, 8, w), lambda j, r: (j + off, 0, 0))

    p = FF_PAIRS
    dw_spec = pl.BlockSpec((1, 8, w), lambda j, r: (j, 0, 0))
    return pl.pallas_call(
        body, name=name, grid=(p, nb * nq),
        in_specs=[cur, nxt, cur, nxt, cur, nxt, cur, cur, wspec(0), wspec(p)],
        out_specs=[cur, cur, dw_spec, dw_spec],
        out_shape=[SDS((nb * s, p * w), BF16), SDS((nb * s, p * w), BF16), SDS((p, 8, w), F32), SDS((p, 8, w), F32)],
        compiler_params=_params(("parallel", "arbitrary")))(dact, dact, a, a, b, b, hu, hg, cw, cw)


def _tri(n, upper):
    r = lax.broadcasted_iota(jnp.int32, (n, n), 0)
    c = lax.broadcasted_iota(jnp.int32, (n, n), 1)
    return ((r <= c) if upper else (r >= c)).astype(F32)


def fgate_fwd(fl, fb, nb, s):
    tc = min(TC, s)
    nq = s // tc

    def body(fl_ref, fb_ref, f_ref, carry):
        @pl.when(pl.program_id(1) == 0)
        def _():
            carry[...] = jnp.zeros_like(carry)

        z = fl_ref[...] + fb_ref[...]
        logf = jnp.minimum(z, 0.0) - jnp.log(1.0 + jnp.exp(-jnp.abs(z)))
        f_ref[...] = jnp.dot(_tri(tc, False), logf, preferred_element_type=F32,
                             precision=lax.Precision.HIGHEST) + carry[...]
        carry[...] += jnp.sum(logf, axis=0, keepdims=True)

    row = pl.BlockSpec((tc, LANES), lambda b, i: (b * nq + i, 0))
    return pl.pallas_call(
        body, name="fgate_fwd", grid=(nb, nq), in_specs=[row, pl.BlockSpec((1, LANES), lambda b, i: (0, 0))],
        out_specs=row, out_shape=SDS((nb * s, LANES), F32), scratch_shapes=[pltpu.VMEM((1, LANES), F32)],
        compiler_params=_params(("arbitrary", "arbitrary")))(fl, fb)


def fgate_bwd(d_cum_q, d_cum_k, fl, fb, nb, s):
    tc = min(TC, s)
    nq = s // tc

    def body(dfq_ref, dfk_ref, fl_ref, fb_ref, dfl_ref, dfb_ref, carry):
        b = pl.program_id(0)
        i = pl.program_id(1)

        @pl.when(i == 0)
        def _():
            carry[...] = jnp.zeros_like(carry)

        @pl.when(jnp.logical_and(b == 0, i == 0))
        def _():
            dfb_ref[...] = jnp.zeros_like(dfb_ref)

        dfv = dfq_ref[...] + dfk_ref[...]
        dlog = jnp.dot(_tri(tc, True), dfv, preferred_element_type=F32,
                       precision=lax.Precision.HIGHEST) + carry[...]
        carry[...] += jnp.sum(dfv, axis=0, keepdims=True)
        z = fl_ref[...] + fb_ref[...]
        dfl = dlog / (1.0 + jnp.exp(z))
        dfl_ref[...] = dfl
        dfb_ref[...] += jnp.sum(dfl, axis=0, keepdims=True)

    row = pl.BlockSpec((tc, LANES), lambda b, i: (b * nq + nq - 1 - i, 0))
    vec = pl.BlockSpec((1, LANES), lambda b, i: (0, 0))
    return pl.pallas_call(
        body, name="fgate_bwd", grid=(nb, nq), in_specs=[row, row, row, vec], out_specs=[row, vec],
        out_shape=[SDS((nb * s, LANES), F32), SDS((1, LANES), F32)], scratch_shapes=[pltpu.VMEM((1, LANES), F32)],
        compiler_params=_params(("arbitrary", "arbitrary")))(d_cum_q, d_cum_k, fl, fb)


PAIR = 2 * HEAD_DIM
N_PAIRS = FOX_HEADS // 2


def _lane_put(shape, h, col):
    lane = lax.broadcasted_iota(jnp.int32, shape, 1)
    return jnp.where(lane == h, col, 0.0)


def _half_masks(rows):
    lane = lax.broadcasted_iota(jnp.int32, (rows, PAIR), 1)
    return lane < HEAD_DIM


def _split_pair(x, scale=None):
    if scale is not None:
        x = x * scale
    lo = _half_masks(x.shape[0])
    zero = jnp.zeros_like(x)
    return jnp.where(lo, x, zero), jnp.where(lo, zero, x)


def _to_tile_rows(a, nb, s, tf):
    return a.reshape(nb * s // tf, tf, LANES)[:, :, :16].transpose(0, 2, 1)


def _from_tile_rows(a):
    tiles, _, tf = a.shape
    return jnp.pad(a.transpose(0, 2, 1), ((0, 0), (0, 0), (0, LANES - 16))).reshape(tiles * tf, LANES)


BIAS_TERMS = 3
LOOKAHEAD = 4
FOLLOW_FWD = 1
LOOKAHEAD_BWD = 2
FOLLOW_BWD = 1


def _bias_lane(h):
    return HEAD_DIM if h % 2 == 0 else 0


def _placement():
    rows = jnp.arange(LANES)[:, None]
    cols = jnp.arange(FOX_HEADS * PAIR)[None, :]
    head, lane = cols // PAIR, cols % PAIR
    first = jnp.where(head % 2 == 0, HEAD_DIM, 0)
    term = lane - first
    hit = (term >= 0) & (term < BIAS_TERMS) & (rows == 16 * term + head)
    return hit.astype(BF16)


def fox_prep(kv, fneg, nb, s):
    tf = min(TF, s)
    w = TOK_WIDTH

    def body(k_ref, v_ref, f_ref, pl_ref, ka_ref, vt_ref):
        lane = lax.broadcasted_iota(jnp.int32, (tf, LANES), 1)
        lo = lane < HEAD_DIM
        f = jnp.where(lane < FOX_HEADS, f_ref[...], 0.0)
        hi = f.astype(BF16).astype(F32)
        mid = (f - hi).astype(BF16).astype(F32)
        low = (f - hi - mid).astype(BF16).astype(F32)
        terms = (hi + pltpu.roll(mid, 16, axis=1) + pltpu.roll(low, 32, axis=1)).astype(BF16)
        placed = jnp.dot(terms, pl_ref[...], preferred_element_type=F32).astype(BF16)
        one = jnp.ones((tf, LANES), BF16)
        zero = jnp.zeros((tf, LANES), BF16)
        for p in range(N_PAIRS):
            kp = k_ref[:, p * PAIR:(p + 1) * PAIR] * QK_SCALE
            vp = v_ref[:, p * PAIR:(p + 1) * PAIR]
            he, ho = 2 * p, 2 * p + 1
            ka_ref[:, he * PAIR:(he + 1) * PAIR] = jnp.where(lo, kp, placed[:, he * PAIR:(he + 1) * PAIR])
            ka_ref[:, ho * PAIR:(ho + 1) * PAIR] = jnp.where(lo, placed[:, ho * PAIR:(ho + 1) * PAIR], kp)
            ve = jnp.where(lo, vp, jnp.where(lane == HEAD_DIM, one, zero))
            vo = jnp.where(lo, jnp.where(lane == 0, one, zero), vp)
            vt_ref[0, he * PAIR:(he + 1) * PAIR, :] = ve.astype(F32).T.astype(BF16)
            vt_ref[0, ho * PAIR:(ho + 1) * PAIR, :] = vo.astype(F32).T.astype(BF16)

    return pl.pallas_call(
        body, name="fox_prep", grid=(nb * s // tf,),
        in_specs=[pl.BlockSpec((tf, w), lambda r: (r, 0)), pl.BlockSpec((tf, w), lambda r: (r, 1)),
                  pl.BlockSpec((tf, LANES), lambda r: (r, 0)), pl.BlockSpec((LANES, FOX_HEADS * PAIR), lambda r: (0, 0))],
        out_specs=[pl.BlockSpec((tf, FOX_HEADS * PAIR), lambda r: (r, 0)),
                   pl.BlockSpec((1, FOX_HEADS * PAIR, tf), lambda r: (r, 0, 0))],
        out_shape=[SDS((nb * s, FOX_HEADS * PAIR), BF16), SDS((nb * s // tf, FOX_HEADS * PAIR, tf), BF16)],
        compiler_params=_params(("parallel",)))(kv, kv, fneg, _placement())


def fox_fwd_t(pq, kaug, vaug_t, nb, s):
    tf = min(TF, s)
    n = s // tf
    w = TOK_WIDTH
    wa = FOX_HEADS * PAIR

    def body(q_ref, k_hbm, vt_hbm, ob_ref, of_ref, lse_ref, k_vm, vt_vm, qx_scr, m_scr, acc_scr, sems):
        b = pl.program_id(0)
        i = pl.program_id(1)

        @pl.when(i == 0)
        def _():
            ck = pltpu.make_async_copy(k_hbm.at[pl.ds(pl.multiple_of(b * s, tf), s)], k_vm, sems.at[0])
            cv = pltpu.make_async_copy(vt_hbm.at[pl.ds(b * n, n)], vt_vm, sems.at[1])
            ck.start()
            cv.start()
            ck.wait()
            cv.wait()

        lane = lax.broadcasted_iota(jnp.int32, (tf, PAIR), 1)
        one = jnp.ones((tf, PAIR), BF16)
        zero = jnp.zeros((tf, PAIR), BF16)
        for p in range(N_PAIRS):
            qp = q_ref[:, p * PAIR:(p + 1) * PAIR]
            be, bo = _bias_lane(2 * p), _bias_lane(2 * p + 1)
            ones_e = jnp.where((lane >= be) & (lane < be + BIAS_TERMS), one, zero)
            ones_o = jnp.where((lane >= bo) & (lane < bo + BIAS_TERMS), one, zero)
            qx_scr[2 * p] = jnp.where(lane < HEAD_DIM, qp, ones_e)
            qx_scr[2 * p + 1] = jnp.where(lane < HEAD_DIM, ones_o, qp)
        m_scr[...] = jnp.full(m_scr.shape, NEG_BIG, F32)
        acc_scr[...] = jnp.zeros_like(acc_scr)

        def tile(j, masked):
            ks = pl.multiple_of(j * tf, tf)
            if masked:
                keep = lax.broadcasted_iota(jnp.int32, (tf, tf), 1) >= lax.broadcasted_iota(jnp.int32, (tf, tf), 0)
            def scores(h):
                kx = k_vm[pl.ds(ks, tf), h * PAIR:(h + 1) * PAIR]
                return lax.dot_general(kx, qx_scr[h], NT_DIMS, preferred_element_type=F32)

            def values(h, pr, a):
                pv = jnp.dot(vt_vm[j, h * PAIR:(h + 1) * PAIR, :], pr, preferred_element_type=F32)
                acc_scr[h] = a * acc_scr[h] + pv

            ahead = [scores(h) for h in range(LOOKAHEAD)]
            behind = []
            for h in range(FOX_HEADS):
                sc = ahead.pop(0)
                if h + LOOKAHEAD < FOX_HEADS:
                    ahead.append(scores(h + LOOKAHEAD))
                if masked:
                    sc = jnp.where(keep, sc, NEG_BIG)
                m_prev = m_scr[h]
                m_new = jnp.maximum(m_prev, jnp.max(sc, axis=0, keepdims=True))
                m_scr[h] = m_new
                behind.append((h, jnp.exp(sc - m_new).astype(BF16), jnp.exp(m_prev - m_new)))
                if len(behind) > FOLLOW_FWD:
                    values(*behind.pop(0))
            for item in behind:
                values(*item)

        def step(j, carry):
            tile(j, False)
            return carry

        lax.fori_loop(0, i, step, 0)
        tile(i, True)

        top = lax.broadcasted_iota(jnp.int32, (PAIR, tf), 0) < HEAD_DIM
        sub = lax.broadcasted_iota(jnp.int32, (16, tf), 0)
        lse = jnp.zeros((16, tf), F32)
        for p in range(N_PAIRS):
            he, ho = 2 * p, 2 * p + 1
            le = acc_scr[he, HEAD_DIM:HEAD_DIM + 1, :]
            lod = acc_scr[ho, 0:1, :]
            o = jnp.where(top, acc_scr[he] / le, acc_scr[ho] / lod).T
            ob_ref[:, p * PAIR:(p + 1) * PAIR] = o.astype(BF16)
            of_ref[:, p * PAIR:(p + 1) * PAIR] = o
            lse = jnp.where(sub == he, m_scr[he] + jnp.log(le), lse)
            lse = jnp.where(sub == ho, m_scr[ho] + jnp.log(lod), lse)
        lse_ref[0] = lse

    qrow = lambda b, i: (b * n + i, 0)
    return pl.pallas_call(
        body, name="fox_fwd", grid=(nb, n),
        in_specs=[pl.BlockSpec((tf, w), qrow), ANY_SPEC, ANY_SPEC],
        out_specs=[pl.BlockSpec((tf, w), qrow), pl.BlockSpec((tf, w), qrow),
                   pl.BlockSpec((1, 16, tf), lambda b, i: (b * n + i, 0, 0))],
        out_shape=[SDS((nb * s, w), BF16), SDS((nb * s, w), F32), SDS((nb * n, 16, tf), F32)],
        scratch_shapes=[pltpu.VMEM((s, wa), BF16), pltpu.VMEM((n, wa, tf), BF16),
                        pltpu.VMEM((FOX_HEADS, tf, PAIR), BF16), pltpu.VMEM((FOX_HEADS, 1, tf), F32),
                        pltpu.VMEM((FOX_HEADS, PAIR, tf), F32), pltpu.SemaphoreType.DMA((2,))],
        compiler_params=_params(("arbitrary", "arbitrary")))(pq, kaug, vaug_t)


def fox_delta(dcat, o, nb, s):
    tf = min(TM, s)
    w = TOK_WIDTH

    def body(do_ref, o_ref, dl_ref):
        out = jnp.zeros((tf, LANES), F32)
        for h in range(FOX_HEADS):
            lo, hi = h * HEAD_DIM, (h + 1) * HEAD_DIM
            out = out + _lane_put((tf, LANES), h, jnp.sum(do_ref[:, lo:hi] * o_ref[:, lo:hi], axis=1, keepdims=True))
        dl_ref[...] = out

    row = pl.BlockSpec((tf, w), lambda r: (r, 0))
    return pl.pallas_call(
        body, name="fox_delta", grid=(nb * s // tf,), in_specs=[row, row],
        out_specs=pl.BlockSpec((tf, LANES), lambda r: (r, 0)), out_shape=SDS((nb * s, LANES), F32),
        compiler_params=_params(("parallel",)))(dcat, o)


def fox_bwd(pq, kv, fneg, dcat_bf, lse_rows, delta_rows, nb, s):
    tf = min(TF, s)
    n = s // tf
    w = TOK_WIDTH

    def body(q_hbm, k_ref, v_ref, f_ref, do_hbm, lse_ref, dl_ref, dq_ref, dk_ref, dv_ref, dfk_ref, dfq_ref,
             q_vm, do_vm, km_scr, vm_scr, kt_scr, fk_scr, dk_scr, dv_scr, rs_scr, dq_scr, fq_scr, sems):
        b = pl.program_id(0)
        j = pl.program_id(1)

        @pl.when(j == 0)
        def _():
            rows = pl.ds(pl.multiple_of(b * s, tf), s)
            cq = pltpu.make_async_copy(q_hbm.at[rows, pl.ds(0, w)], q_vm, sems.at[0])
            cd = pltpu.make_async_copy(do_hbm.at[rows, pl.ds(0, w)], do_vm, sems.at[1])
            cq.start()
            cd.start()
            dq_scr[...] = jnp.zeros_like(dq_scr)
            fq_scr[...] = jnp.zeros_like(fq_scr)
            cq.wait()
            cd.wait()

        for p in range(N_PAIRS):
            kp = k_ref[:, p * PAIR:(p + 1) * PAIR] * QK_SCALE
            ke, ko = _split_pair(kp)
            km_scr[2 * p] = ke
            km_scr[2 * p + 1] = ko
            kt_scr[p] = kp.astype(F32).T.astype(BF16)
            ve, vo = _split_pair(v_ref[:, p * PAIR:(p + 1) * PAIR])
            vm_scr[2 * p] = ve
            vm_scr[2 * p + 1] = vo
        for h in range(FOX_HEADS):
            fk_scr[h] = jnp.broadcast_to(f_ref[:, h:h + 1], (tf, tf))
        dk_scr[...] = jnp.zeros_like(dk_scr)
        dv_scr[...] = jnp.zeros_like(dv_scr)
        rs_scr[...] = jnp.zeros_like(rs_scr)

        def tile(i, masked):
            qs = pl.multiple_of(i * tf, tf)
            if masked:
                keep = lax.broadcasted_iota(jnp.int32, (tf, tf), 1) >= lax.broadcasted_iota(jnp.int32, (tf, tf), 0)
            def products(h):
                qp = q_vm[pl.ds(qs, tf), (h // 2) * PAIR:(h // 2 + 1) * PAIR]
                dop = do_vm[pl.ds(qs, tf), (h // 2) * PAIR:(h // 2 + 1) * PAIR]
                return (lax.dot_general(km_scr[h], qp, NT_DIMS, preferred_element_type=F32),
                        lax.dot_general(vm_scr[h], dop, NT_DIMS, preferred_element_type=F32))

            def dependents(h, prb, dsb):
                p = h // 2
                half = slice((h % 2) * HEAD_DIM, (h % 2 + 1) * HEAD_DIM)
                qp = q_vm[pl.ds(qs, tf), p * PAIR:(p + 1) * PAIR]
                dop = do_vm[pl.ds(qs, tf), p * PAIR:(p + 1) * PAIR]
                dv_scr[h] += jnp.dot(prb, dop, preferred_element_type=F32)
                dk_scr[h] += jnp.dot(dsb, qp, preferred_element_type=F32)
                dqt = jnp.dot(kt_scr[p], dsb, preferred_element_type=F32)
                dq_scr[i, p, half, :] += dqt[(h % 2) * HEAD_DIM:(h % 2 + 1) * HEAD_DIM]

            ahead = [products(h) for h in range(LOOKAHEAD_BWD)]
            behind = []
            for h in range(FOX_HEADS):
                sc, dp = ahead.pop(0)
                if h + LOOKAHEAD_BWD < FOX_HEADS:
                    ahead.append(products(h + LOOKAHEAD_BWD))
                sc = sc + fk_scr[h] - lse_ref[i, h:h + 1, :]
                if masked:
                    sc = jnp.where(keep, sc, NEG_BIG)
                pr = jnp.exp(sc)
                ds = pr * (dp - dl_ref[i, h:h + 1, :])
                part = ds[:, :LANES]
                for c in range(1, tf // LANES):
                    part = part + ds[:, c * LANES:(c + 1) * LANES]
                rs_scr[h] += part
                fq_scr[i, h:h + 1, :] += jnp.sum(ds, axis=0, keepdims=True)
                behind.append((h, pr.astype(BF16), ds.astype(BF16)))
                if len(behind) > FOLLOW_BWD:
                    dependents(*behind.pop(0))
            for item in behind:
                dependents(*item)

        def step(i, carry):
            tile(i, False)
            return carry

        tile(j, True)
        for p in range(N_PAIRS):
            dq_ref[:, p * PAIR:(p + 1) * PAIR] = dq_scr[j, p].T.astype(BF16)
        dfq_ref[0] = fq_scr[j]
        lax.fori_loop(j + 1, n, step, 0)

        lo = _half_masks(tf)
        dfk = jnp.zeros((tf, LANES), F32)
        for p in range(N_PAIRS):
            dk = jnp.where(lo, dk_scr[2 * p], dk_scr[2 * p + 1]) * QK_SCALE
            dk_ref[:, p * PAIR:(p + 1) * PAIR] = dk.astype(BF16)
            dv_ref[:, p * PAIR:(p + 1) * PAIR] = jnp.where(lo, dv_scr[2 * p], dv_scr[2 * p + 1]).astype(BF16)
            for h in (2 * p, 2 * p + 1):
                dfk = dfk - _lane_put((tf, LANES), h, jnp.sum(rs_scr[h], axis=1, keepdims=True))
        dfk_ref[...] = dfk

    krow = lambda b, j: (b * n + j, 0)
    rows = pl.BlockSpec((n, 16, tf), lambda b, j: (b, 0, 0))
    tile_out = pl.BlockSpec((tf, w), krow)
    return pl.pallas_call(
        body, name="fox_bwd", grid=(nb, n),
        in_specs=[ANY_SPEC, pl.BlockSpec((tf, w), krow), pl.BlockSpec((tf, w), lambda b, j: (b * n + j, 1)),
                  pl.BlockSpec((tf, LANES), krow), ANY_SPEC, rows, rows],
        out_specs=[tile_out, tile_out, tile_out, pl.BlockSpec((tf, LANES), krow),
                   pl.BlockSpec((1, 16, tf), lambda b, j: (b * n + j, 0, 0))],
        out_shape=[SDS((nb * s, w), BF16), SDS((nb * s, w), BF16), SDS((nb * s, w), BF16), SDS((nb * s, LANES), F32),
                   SDS((nb * n, 16, tf), F32)],
        scratch_shapes=[pltpu.VMEM((s, w), BF16), pltpu.VMEM((s, w), BF16),
                        pltpu.VMEM((FOX_HEADS, tf, PAIR), BF16), pltpu.VMEM((FOX_HEADS, tf, PAIR), BF16),
                        pltpu.VMEM((N_PAIRS, PAIR, tf), BF16), pltpu.VMEM((FOX_HEADS, tf, tf), F32),
                        pltpu.VMEM((FOX_HEADS, tf, PAIR), F32), pltpu.VMEM((FOX_HEADS, tf, PAIR), F32),
                        pltpu.VMEM((FOX_HEADS, tf, LANES), F32), pltpu.VMEM((n, N_PAIRS, PAIR, tf), F32),
                        pltpu.VMEM((n, 16, tf), F32), pltpu.SemaphoreType.DMA((2,))],
        compiler_params=_params(("arbitrary", "arbitrary")))(pq, kv, kv, fneg, dcat_bf, lse_rows, delta_rows)


def reduce_adamw(parts, w, m, v, name):
    _, r, c = parts.shape
    tr = r
    for cand in range(16, r, 16):
        if r % cand == 0 and cand * c <= 128 * 1024:
            tr = cand
    c1 = 1.0 - ADAM_B1 ** ADAM_STEP
    c2 = 1.0 - ADAM_B2 ** ADAM_STEP

    def body(p_ref, w_ref, m_ref, v_ref, g_out, d_out, m_out, v_out):
        g = p_ref[0].astype(F32)
        for k in range(1, N_DEV):
            g = g + p_ref[k].astype(F32)
        mn = ADAM_B1 * m_ref[...] + (1.0 - ADAM_B1) * g
        vn = ADAM_B2 * v_ref[...] + (1.0 - ADAM_B2) * (g * g)
        g_out[...] = g
        m_out[...] = mn
        v_out[...] = vn
        d_out[...] = -ADAM_LR * ((mn / c1) / (jnp.sqrt(vn / c2) + ADAM_EPS) + ADAM_WD * w_ref[...])

    row = pl.BlockSpec((tr, c), lambda i: (i, 0))
    return pl.pallas_call(
        body, name=name, grid=(r // tr,),
        in_specs=[pl.BlockSpec((N_DEV, tr, c), lambda i: (0, i, 0)), row, row, row],
        out_specs=[row, row, row, row], out_shape=[SDS((r, c), F32)] * 4,
        compiler_params=_params(("parallel",)))(parts, w, m, v)


N_PEERS = N_DEV - 1
HBM_SPEC = pl.BlockSpec(memory_space=pltpu.HBM)
SEM_SPEC = pl.BlockSpec(memory_space=pltpu.SEMAPHORE)
ANY_SPEC = pl.BlockSpec(memory_space=pl.ANY)
SPLIT_EFFECT = pltpu.SideEffectType.DATAFLOW_SIDE_EFFECTING


def _my_index():
    return 4 * lax.axis_index("x") + 2 * lax.axis_index("y") + lax.axis_index("c")


def _peers():
    x, y, c = lax.axis_index("x"), lax.axis_index("y"), lax.axis_index("c")
    peers = []
    for k in range(1, N_DEV):
        px = 1 - x if (k >> 2) & 1 else x
        py = 1 - y if (k >> 1) & 1 else y
        pc = 1 - c if k & 1 else c
        peers.append(((px, py, pc), 4 * px + 2 * py + pc))
    return 4 * x + 2 * y + c, peers


def _push(src, dst, send_sems, recv_sems, slot, dev):
    return pltpu.make_async_remote_copy(src_ref=src, dst_ref=dst, send_sem=send_sems.at[slot], recv_sem=recv_sems.at[slot],
                                        device_id=dev, device_id_type=pl.DeviceIdType.MESH)


def _landing_shapes(arrs, scatter):
    return [SDS((N_DEV,) + tuple(a.shape[1:] if sc else a.shape), a.dtype) for a, sc in zip(arrs, scatter)]


def exchange(arrs, scatter, name):
    na = len(arrs)

    def body(*refs):
        ins = refs[:na]
        outs = refs[na:2 * na]
        send_sems, recv_sems, local_sems = refs[2 * na:]
        me, peers = _peers()
        local = []
        remote = []
        for a in range(na):
            lc = pltpu.make_async_copy(ins[a].at[me] if scatter[a] else ins[a], outs[a].at[me], local_sems.at[a])
            lc.start()
            local.append(lc)
            for k, (dev, idx) in enumerate(peers):
                cp = _push(ins[a].at[idx] if scatter[a] else ins[a], outs[a].at[me], send_sems, recv_sems,
                           a * N_PEERS + k, dev)
                cp.start()
                remote.append(cp)
        for a in range(na):
            for k, (dev, idx) in enumerate(peers):
                _push(ins[a].at[me] if scatter[a] else ins[a], outs[a].at[idx], send_sems, recv_sems,
                      a * N_PEERS + k, dev).wait_recv()
        for cp in remote:
            cp.wait_send()
        for lc in local:
            lc.wait()

    return pl.pallas_call(
        body, name=name, in_specs=[HBM_SPEC] * na, out_specs=[HBM_SPEC] * na, out_shape=_landing_shapes(arrs, scatter),
        scratch_shapes=[pltpu.SemaphoreType.DMA((na * N_PEERS,)), pltpu.SemaphoreType.DMA((na * N_PEERS,)),
                        pltpu.SemaphoreType.DMA((na,))])(*arrs)


def exchange_start(arrs, scatter, after, name):
    na = len(arrs)
    lands = [lax.empty(l.shape, l.dtype) for l in _landing_shapes(arrs, scatter)]

    def body(*refs):
        ins = refs[:na]
        land = refs[na:2 * na]
        send_sems, recv_sems = refs[2 * na + 1], refs[2 * na + 2]
        token = refs[-1]
        me, peers = _peers()
        for a in range(na):
            for k, (dev, idx) in enumerate(peers):
                _push(ins[a].at[idx] if scatter[a] else ins[a], land[a].at[me], send_sems, recv_sems,
                      a * N_PEERS + k, dev).start()
        token[...] = jnp.zeros_like(token)

    thru = [pltpu.HBM(a.shape, a.dtype) for a in arrs] + [pltpu.HBM(l.shape, l.dtype) for l in lands]
    res = pl.pallas_call(
        body, name=name,
        out_shape=(pltpu.SemaphoreType.DMA((na * N_PEERS,)), pltpu.SemaphoreType.DMA((na * N_PEERS,)), *thru,
                   SDS((8, LANES), F32)),
        in_specs=[HBM_SPEC] * (2 * na) + [ANY_SPEC],
        out_specs=(SEM_SPEC, SEM_SPEC, *([HBM_SPEC] * (2 * na)), pl.BlockSpec(memory_space=pltpu.VMEM)),
        input_output_aliases={i: 2 + i for i in range(2 * na)},
        compiler_params=pltpu.CompilerParams(has_side_effects=SPLIT_EFFECT),
    )(*[pltpu.with_memory_space_constraint(a, pltpu.HBM) for a in arrs],
      *[pltpu.with_memory_space_constraint(l, pltpu.HBM) for l in lands], after)
    return {"send": res[0], "recv": res[1], "src": res[2:2 + na], "land": res[2 + na:2 + 2 * na],
            "token": res[-1][0, 0], "scatter": scatter}


def exchange_wait(handle, after, name):
    scatter = handle["scatter"]
    na = len(scatter)

    def body(*refs):
        src = refs[:na]
        land = refs[na:2 * na]
        send_sems, recv_sems = refs[2 * na], refs[2 * na + 1]
        me, peers = _peers()
        for a in range(na):
            for k, (dev, idx) in enumerate(peers):
                cp = _push(src[a].at[me] if scatter[a] else src[a], land[a].at[idx], send_sems, recv_sems,
                           a * N_PEERS + k, dev)
                cp.wait_send()
                cp.wait_recv()

    ops = list(handle["src"]) + list(handle["land"])
    res = pl.pallas_call(
        body, name=name, out_shape=tuple(pltpu.HBM(o.shape, o.dtype) for o in ops),
        in_specs=[HBM_SPEC] * (2 * na) + [SEM_SPEC, SEM_SPEC, ANY_SPEC], out_specs=tuple([HBM_SPEC] * (2 * na)),
        input_output_aliases={i: i for i in range(2 * na)},
        compiler_params=pltpu.CompilerParams(has_side_effects=SPLIT_EFFECT),
    )(*ops, handle["send"], handle["recv"], after)
    me = _my_index()
    out = []
    for a in range(na):
        own = lax.dynamic_index_in_dim(res[a], me, 0, keepdims=True) if scatter[a] else res[a][None]
        out.append(lax.dynamic_update_slice(res[na + a], own, (me,) + (0,) * (own.ndim - 1)))
    return out


def forward_layer(l, xin, xin_bf, mem_bf, wt, nb, s, ffn_weights=None):
    sv = {"xin_bf": xin_bf}
    memkv = mm_nn(mem_bf, wt["memw"], BF16, f"memkv{l}")
    sv["memkv"] = memkv
    if l == 0:
        proj = mm_nn(xin_bf, wt["win_a"], F32, "proj_a")
        pooled, tok = pool_fwd(proj, wt["pw_bd"], wt["pscale"], nb, s)
        sv["pooled"] = pooled
    else:
        kv = mm_nn(xin_bf, wt["kvw"][:, :2 * TOK_WIDTH], BF16, "kv_proj")
        fl = mm_nn(xin_bf, wt["kvw"][:, 2 * TOK_WIDTH:], F32, "gate_proj")
        fneg = -fgate_fwd(fl, wt["fb"], nb, s)
        proj = mm_nn(xin_bf, wt["wq"], BF16, "proj_b")
        kaug, vaug_t = fox_prep(kv, fneg, nb, s)
        tok, o_f32, lse_rows = fox_fwd_t(proj, kaug, vaug_t, nb, s)
        sv.update(kv=kv, fl=fl, fneg=fneg, o_f32=o_f32, lse_rows=lse_rows)
    sv["proj"] = proj
    mem_out = memattn_fwd(proj, memkv, nb, s, f"memattn_fwd{l}")
    cat = jnp.concatenate([tok, mem_out], axis=1)
    sv["cat"] = cat
    x1, x1_bf, xh1, rs1 = ln_fwd(xin, cat, wt["wout"], wt["ln1_g"], wt["ln1_b"], f"out_proj_ln1_{l}")
    sv.update(x1_bf=x1_bf, xh1=xh1, rs1=rs1)
    if ffn_weights is not None:
        wt.update(ffn_weights(x1_bf))
    act, ga, gb, hu, hg = ffn_up_gate(x1_bf, wt["wup"], wt["cw"], nb, s, f"ffn_up_gate{l}")
    sv.update(act=act, ga=ga, gb=gb, hu=hu, hg=hg)
    x2, x2_bf, xh2, rs2 = ln_fwd(x1, act, wt["wdown"], wt["ln2_g"], wt["ln2_b"], f"ffn_down_ln2_{l}")
    sv.update(xh2=xh2, rs2=rs2)
    return x2, x2_bf, sv


def backward_layer(l, dy, sv, mem_bf, wt, nb, s, after_ffn=None, after_pool=None, dy_product=None):
    g = {}
    dr2, dr2_bf, g["ln2_g"], g["ln2_b"] = ln_bwd(dy, sv["xh2"], sv["rs2"], wt["ln2_g"], f"ln2_bwd{l}", product=dy_product)
    dact = mm_nn(dr2_bf, wt["wdown_t"], BF16, f"ffn_down_dx{l}")
    g["wdown"] = mm_tn(sv["act"], dr2_bf, f"ffn_down_dw{l}")
    dh_u, dh_g, dcw_u, dcw_g = gate_conv_bwd(dact, sv["ga"], sv["gb"], sv["hu"], sv["hg"], wt["cw"], nb, s,
                                             f"gate_conv_bwd{l}")
    g["cw"] = jnp.concatenate([dcw_u, dcw_g], axis=0)
    half = FF_PAIRS * FF_BLOCK_PAD
    dx1 = mm_nn(dh_u, wt["wup_t"][:half], F32, f"ffn_up_dx_u{l}", addend=dr2, add_scale=DN_ALPHA)
    g["wup"] = jnp.concatenate([mm_tn(sv["x1_bf"], dh_u, f"ffn_up_dw_u{l}", blocked=True),
                                mm_tn(sv["x1_bf"], dh_g, f"ffn_up_dw_g{l}", blocked=True)], axis=0)
    ln1_g = wt["ln1_g"] if after_ffn is None else wt["ln1_g"] + after_ffn(g, dx1)
    dr1, dr1_bf, g["ln1_g"], g["ln1_b"] = ln_bwd(dx1, sv["xh1"], sv["rs1"], ln1_g, f"ffn_up_dx_g_ln1_bwd{l}",
                                                 product=(dh_g, wt["wup_t"][half:]))
    dcat, dcat_bf = mm_nn(dr1_bf, wt["wout_t"], F32, f"out_proj_dx{l}", also_bf16=True)
    g["wout"] = mm_tn(sv["cat"], dr1_bf, f"out_proj_dw{l}")
    dqm, dmemkv = memattn_bwd(sv["proj"], sv["memkv"], dcat, nb, s, f"memattn_bwd{l}")
    g["memw"] = mm_tn(mem_bf, dmemkv, f"memkv_dw{l}")
    if l == 0:
        dmixed, dpooled, g["pscale"] = pool_bwd_mix(dcat, sv["pooled"], wt["pw_bd"], wt["pw_bd_t"], wt["pscale"], nb, s)
        g["pw_full"] = mm_tn(sv["pooled"], dmixed, "pool_dw")
        win_a_t = wt["win_a_t"] if after_pool is None else wt["win_a_t"] + after_pool(g, dmixed).astype(BF16)
        du = pool_bwd_window(dpooled, nb, s)
        dproj = jnp.concatenate([du, dqm], axis=1)
        dx = mm_nn(dproj, win_a_t, F32, "proj_a_dx", addend=dr1, add_scale=DN_ALPHA)
        g["win_a"] = mm_tn(sv["xin_bf"], dproj, "proj_a_dw")
        pending = None
    else:
        delta = fox_delta(dcat, sv["o_f32"], nb, s)
        tf = min(TF, s)
        dq, dk, dv, dfcum_k, dfq_rows = fox_bwd(sv["proj"], sv["kv"], sv["fneg"], dcat_bf,
                                                sv["lse_rows"], _to_tile_rows(delta, nb, s, tf), nb, s)
        dfl, g["fb"] = fgate_bwd(_from_tile_rows(dfq_rows), dfcum_k, sv["fl"], wt["fb"], nb, s)
        dproj = jnp.concatenate([dq, dqm], axis=1)
        dkvf = jnp.concatenate([dk, dv, dfl.astype(BF16)], axis=1)
        dx = mm_nn(dproj, wt["wq_t"], F32, "proj_b_dx", addend=dr1, add_scale=DN_ALPHA)
        pending = (dkvf, wt["kvw_t"])
        g["wq"] = mm_tn(sv["xin_bf"], dproj, "proj_b_dw")
        g["kvw"] = mm_tn(sv["xin_bf"], dkvf, "kv_proj_dw")
    return dx, pending, g


def pack_replicated(pool_w, ln1_g, ln1_b, ln2_g, ln2_b, conv_b, f_b):
    cb = jnp.pad(conv_b, ((0, 0), (0, 6144 - 5504))).reshape(12, D_MODEL)
    fb = jnp.pad(f_b.reshape(1, FOX_HEADS), ((0, 3), (0, D_MODEL - FOX_HEADS)))
    return jnp.concatenate([pool_w.reshape(144, D_MODEL), ln1_g, ln1_b, ln2_g, ln2_b, cb, fb], axis=0)


def unpack_replicated(buf):
    pool_w = buf[:144].reshape(1, 4, POOL_GROUP, POOL_GROUP)
    ln = [buf[144 + 2 * k:146 + 2 * k] for k in range(4)]
    conv_b = buf[152:164].reshape(2, 6144)[:, :5504]
    f_b = buf[164, :FOX_HEADS]
    return pool_w, ln[0], ln[1], ln[2], ln[3], conv_b, f_b


def pack_small(conv_w, pool_scale):
    buf = jnp.zeros((16, FF_BLOCK_PAD), F32)
    buf = lax.dynamic_update_slice(buf, conv_w.reshape(DEPTH * 3, FF_BLOCK), (0, 0))
    return lax.dynamic_update_slice(buf, pool_scale, (8, 0))


def _block_diag(pw):
    out = jnp.zeros((TOK_WIDTH, TOK_WIDTH), pw.dtype)
    for g in range(4):
        out = lax.dynamic_update_slice(out, pw[g], (g * POOL_GROUP, g * POOL_GROUP))
    return out


def layer_shards(l, sq_a, sq_b, mem_w_kv, ffn_w_up, ffn_w_down):
    return [sq_a[0].astype(BF16), sq_b[0].astype(BF16), mem_w_kv[l].astype(BF16), ffn_w_up[l].astype(BF16),
            ffn_w_down[l].astype(BF16)]


def mixer_weights(l, gath, ln1_g, ln1_b, ln2_g, ln2_b):
    w_out = gath[1].reshape(D_MODEL, D_MODEL)
    wt = {"memw": gath[2].reshape(D_MODEL, 2 * MEM_WIDTH), "wout": w_out, "wout_t": w_out.T,
          "ln1_g": ln1_g[l:l + 1], "ln1_b": ln1_b[l:l + 1], "ln2_g": ln2_g[l:l + 1], "ln2_b": ln2_b[l:l + 1]}
    return wt, gath[0].reshape(D_MODEL, D_MODEL)


def ffn_weights(l, wup_g, wdown_g, small, conv_b):
    pad_c = FF_BLOCK_PAD - FF_BLOCK
    wup = jnp.pad(wup_g, ((0, 0), (0, 0), (0, pad_c))).transpose(1, 0, 2).reshape(D_MODEL, N_DEV * FF_BLOCK_PAD)
    wdown = jnp.pad(wdown_g.reshape(FF_PAIRS, FF_BLOCK, D_MODEL), ((0, 0), (0, pad_c), (0, 0)))
    wdown = wdown.reshape(FF_PAIRS * FF_BLOCK_PAD, D_MODEL)
    cb = jnp.pad(conv_b[l].reshape(N_DEV, FF_BLOCK), ((0, 0), (0, pad_c)))
    cw = jnp.concatenate([small[:, 3 * l:3 * l + 3, :], cb[:, None, :], jnp.zeros((N_DEV, 4, FF_BLOCK_PAD), F32)], axis=1)
    return {"wup": wup, "wup_t": wup.T, "wdown": wdown, "wdown_t": wdown.T, "cw": cw}


def mixer_grad_blocks(g, w_in_grad):
    blocks = [] if w_in_grad is None else [w_in_grad.reshape(N_DEV, 128, D_MODEL)]
    blocks += [g["wout"].reshape(N_DEV, 128, D_MODEL), g["memw"].reshape(N_DEV, 128, 2 * MEM_WIDTH)]
    return [b.astype(BF16) for b in blocks]


def ffn_grad_blocks(g):
    wup = g["wup"][:, :, :FF_BLOCK]
    wdown = g["wdown"].reshape(FF_PAIRS, FF_BLOCK_PAD, D_MODEL)[:, :FF_BLOCK].reshape(N_DEV, FF_ROWS, D_MODEL)
    return [wup.astype(BF16), wdown.astype(BF16)]


def small_grad_blocks(g0, g1):
    taps = jnp.stack([g0["cw"][:, :3, :], g1["cw"][:, :3, :]], axis=1).reshape(N_DEV, DEPTH * 3, FF_BLOCK_PAD)
    small = jnp.zeros((N_DEV, 16, FF_BLOCK_PAD), F32)
    small = lax.dynamic_update_slice(small, taps, (0, 0, 0))
    return lax.dynamic_update_slice(small, g0["pscale"].reshape(N_DEV, 1, 96), (0, 8, 0))


def replicated_grads(g0, g1):
    pw = jnp.stack([g0["pw_full"][k * POOL_GROUP:(k + 1) * POOL_GROUP, k * POOL_GROUP:(k + 1) * POOL_GROUP] for k in range(4)])
    conv_b = jnp.stack([g_["cw"][:, 3, :FF_BLOCK].reshape(N_DEV * FF_BLOCK) for g_ in (g0, g1)])
    ln = [jnp.concatenate([g0[n], g1[n]], axis=0) for n in ("ln1_g", "ln1_b", "ln2_g", "ln2_b")]
    return pack_replicated(pw[None], ln[0], ln[1], ln[2], ln[3], conv_b, g1["fb"][0, :FOX_HEADS])


def kernel(x, mem, a_w_in, a_pool_w, a_pool_scale, a_w_out, b_w_q, b_w_out, kv_w, f_b, mem_w_kv, ln1_g, ln1_b, ln2_g, ln2_b, ffn_w_up, ffn_conv_w, ffn_conv_b, ffn_w_down, loss_target, m_a_w_in, m_a_pool_w, m_a_pool_scale, m_a_w_out, m_b_w_q, m_b_w_out, m_kv_w, m_f_b, m_mem_w_kv, m_ln1_g, m_ln1_b, m_ln2_g, m_ln2_b, m_ffn_w_up, m_ffn_conv_w, m_ffn_conv_b, m_ffn_w_down, v_a_w_in, v_a_pool_w, v_a_pool_scale, v_a_w_out, v_b_w_q, v_b_w_out, v_kv_w, v_f_b, v_mem_w_kv, v_ln1_g, v_ln1_b, v_ln2_g, v_ln2_b, v_ffn_w_up, v_ffn_conv_w, v_ffn_conv_b, v_ffn_w_down):
    nb, s, d = x.shape
    t = nb * s
    x2d, mem_bf, target = x.reshape(t, d), mem.reshape(nb * MEM_LEN, d).astype(BF16), loss_target.reshape(t, d)

    shards0 = layer_shards(0, a_w_in, a_w_out, mem_w_kv, ffn_w_up, ffn_w_down)
    shards1 = layer_shards(1, b_w_q, b_w_out, mem_w_kv, ffn_w_up, ffn_w_down)
    shards1.append(jnp.pad(kv_w, ((0, 0), (0, KV_COLS_PAD - KV_COLS))).astype(BF16))
    gath0 = exchange(shards0[:3] + [pack_small(ffn_conv_w, a_pool_scale)], [False] * 4, "gather_w0_mixer")
    pending = {"ffn0": exchange_start(shards0[3:], [False] * 2, gath0[0], "gather_w0_ffn_start")}
    small = gath0[3]
    wt0, w_in = mixer_weights(0, gath0, ln1_g + pending["ffn0"]["token"], ln1_b, ln2_g, ln2_b)
    pw_bd = _block_diag(a_pool_w[0])
    wt0.update(win_a=w_in, win_a_t=w_in.T, pw_bd=pw_bd.astype(BF16), pw_bd_t=pw_bd.T.astype(BF16),
               pscale=small[:, 8, :96].reshape(1, TOK_WIDTH) + pending["ffn0"]["token"])

    def ffn0_weights(x1_bf):
        got = exchange_wait(pending["ffn0"], x1_bf, "gather_w0_ffn_wait")
        pending["w1"] = exchange_start(shards1, [False] * 6, got[0], "gather_w1_start")
        w = ffn_weights(0, got[0], got[1], small, ffn_conv_b)
        w["cw"] = w["cw"] + pending["w1"]["token"]
        return w

    x1, x1_bf, sv0 = forward_layer(0, x2d, x2d.astype(BF16), mem_bf, wt0, nb, s, ffn_weights=ffn0_weights)
    gath1 = exchange_wait(pending["w1"], x1_bf, "gather_w1_wait")
    wt1, w_q = mixer_weights(1, gath1, ln1_g, ln1_b, ln2_g, ln2_b)
    wt1.update(ffn_weights(1, gath1[3], gath1[4], small, ffn_conv_b))
    kvw = gath1[5].reshape(D_MODEL, KV_COLS_PAD)
    wt1.update(wq=w_q, wq_t=w_q.T, kvw=kvw, kvw_t=kvw.T,
               fb=jnp.pad(f_b.reshape(1, FOX_HEADS), ((0, 0), (0, LANES - FOX_HEADS))))
    y, _, sv1 = forward_layer(1, x1, x1_bf, mem_bf, wt1, nb, s)
    dy, loss_row = loss_head(y, target)
    loss = lax.psum(loss_row[0, 0], ("x", "y", "c"))

    dx1, dx1_rest, g1 = backward_layer(1, dy, sv1, mem_bf, wt1, nb, s)
    blocks1 = (mixer_grad_blocks(g1, g1["wq"]) + ffn_grad_blocks(g1)
               + [g1["kvw"][:, :KV_COLS].reshape(N_DEV, 128, KV_COLS).astype(BF16)])
    pending["g1"] = exchange_start(blocks1, [True] * 6, dx1, "scatter_g1_start")
    wt0["ln2_g"] = wt0["ln2_g"] + pending["g1"]["token"]

    def after_ffn0(g, dxm):
        pending["gf0"] = exchange_start(ffn_grad_blocks(g), [True] * 2, dxm, "scatter_g0_ffn_start")
        return pending["gf0"]["token"]

    def after_pool0(g, x):
        blocks = mixer_grad_blocks(g, None) + [small_grad_blocks(g, g1), replicated_grads(g, g1)]
        pending["gm0"] = exchange_start(blocks, [True] * 3 + [False], x, "scatter_g0_mixer_start")
        return pending["gm0"]["token"]

    grad_x, _, g0 = backward_layer(0, dx1, sv0, mem_bf, wt0, nb, s, after_ffn=after_ffn0, after_pool=after_pool0,
                                   dy_product=dx1_rest)
    pending["gin"] = exchange_start([g0["win_a"].reshape(N_DEV, 128, D_MODEL).astype(BF16)], [True], grad_x,
                                    "scatter_g0_in_start")
    parts_f0 = exchange_wait(pending["gf0"], jnp.zeros((8, LANES), F32) + pending["gin"]["token"], "scatter_g0_ffn_wait")
    parts1 = exchange_wait(pending["g1"], parts_f0[0], "scatter_g1_wait")

    res = {}

    def upd(nm, parts, w2, m2, v2):
        res[nm] = reduce_adamw(parts, w2, m2, v2, f"adamw_{nm}")

    upd("b_w_q", parts1[0], b_w_q[0], m_b_w_q[0], v_b_w_q[0])
    upd("b_w_out", parts1[1], b_w_out[0], m_b_w_out[0], v_b_w_out[0])
    upd("kv_w", parts1[5], kv_w, m_kv_w, v_kv_w)
    upd("mem_w_kv1", parts1[2], mem_w_kv[1], m_mem_w_kv[1], v_mem_w_kv[1])
    for l, parts in enumerate((parts_f0, parts1[3:5])):
        upd(f"ffn_w_up{l}", parts[0], ffn_w_up[l], m_ffn_w_up[l], v_ffn_w_up[l])
        upd(f"ffn_w_down{l}", parts[1], ffn_w_down[l], m_ffn_w_down[l], v_ffn_w_down[l])
    parts_m0 = exchange_wait(pending["gm0"], res["ffn_w_down1"][0], "scatter_g0_mixer_wait")
    parts_in = exchange_wait(pending["gin"], parts_m0[0], "scatter_g0_in_wait")
    upd("a_w_in", parts_in[0], a_w_in[0], m_a_w_in[0], v_a_w_in[0])
    upd("a_w_out", parts_m0[0], a_w_out[0], m_a_w_out[0], v_a_w_out[0])
    upd("mem_w_kv0", parts_m0[1], mem_w_kv[0], m_mem_w_kv[0], v_mem_w_kv[0])
    upd("small", parts_m0[2], pack_small(ffn_conv_w, a_pool_scale), pack_small(m_ffn_conv_w, m_a_pool_scale),
        pack_small(v_ffn_conv_w, v_a_pool_scale))
    upd("replicated", parts_m0[3], pack_replicated(a_pool_w, ln1_g, ln1_b, ln2_g, ln2_b, ffn_conv_b, f_b),
        pack_replicated(m_a_pool_w, m_ln1_g, m_ln1_b, m_ln2_g, m_ln2_b, m_ffn_conv_b, m_f_b),
        pack_replicated(v_a_pool_w, v_ln1_g, v_ln1_b, v_ln2_g, v_ln2_b, v_ffn_conv_b, v_f_b))

    for nm in ("a_w_in", "a_w_out", "b_w_q", "b_w_out"):
        res[nm] = [o[None] for o in res[nm]]
    for nm in ("mem_w_kv", "ffn_w_up", "ffn_w_down"):
        res[nm] = [jnp.stack([a0, a1]) for a0, a1 in zip(res[nm + "0"], res[nm + "1"])]
    res["ffn_conv_w"] = [o[:DEPTH * 3, :FF_BLOCK].reshape(DEPTH, 3, FF_BLOCK) for o in res["small"]]
    res["a_pool_scale"] = [o[8:9, :96] for o in res["small"]]
    rep_names = ["a_pool_w", "ln1_g", "ln1_b", "ln2_g", "ln2_b", "ffn_conv_b", "f_b"]
    for nm in rep_names:
        res[nm] = []
    for o in res["replicated"]:
        for nm, val in zip(rep_names, unpack_replicated(o)):
            res[nm].append(val)

    order = ["a_w_in", "a_pool_w", "a_pool_scale", "a_w_out", "b_w_q", "b_w_out", "kv_w", "f_b", "mem_w_kv",
             "ln1_g", "ln1_b", "ln2_g", "ln2_b", "ffn_w_up", "ffn_conv_w", "ffn_conv_b", "ffn_w_down"]
    out = [loss, grad_x.reshape(nb, s, d)]
    for kind in range(4):
        out.extend(res[nm][kind] for nm in order)
    return tuple(out)
```

```python
import jax
import jax.numpy as jnp
from jax import lax
from jax.experimental import pallas as pl
from jax.experimental.pallas import tpu as pltpu

F32 = jnp.float32
BF16 = jnp.bfloat16
SDS = jax.ShapeDtypeStruct

N_DEV = 8
D_MODEL = 1024
TOK_WIDTH = 768
MEM_WIDTH = 256
MEM_LEN = 256
MEM_HEADS = 4
HEAD_DIM = 64
FOX_HEADS = 12
POOL_GROUP = 192
FF_BLOCK = 688
FF_BLOCK_PAD = 768
FF_PAIRS = 4
FF_ROWS = 344
KV_COLS = 1548
KV_COLS_PAD = 1664
LANES = 128
DEPTH = 2
DN_ALPHA = (2.0 * DEPTH) ** 0.25
LN_EPS = 1e-5
QK_SCALE = HEAD_DIM ** -0.5
NEG_BIG = -1e30

ADAM_LR = 0.001
ADAM_B1 = 0.9
ADAM_B2 = 0.999
ADAM_EPS = 1e-08
ADAM_WD = 0.01
ADAM_STEP = 10

VMEM_LIMIT_BYTES = 56 * 1024 * 1024
MM_BLOCK_BYTES = 6 * 1024 * 1024
TM = 512
TS = 256
TF = 256
TC = 256
HALO_POOL = 16
HALO_CONV = 8

NT_DIMS = (((1,), (1,)), ((), ()))
TN_DIMS = (((0,), (0,)), ((), ()))


def _params(sem=None):
    return pltpu.CompilerParams(dimension_semantics=sem, vmem_limit_bytes=VMEM_LIMIT_BYTES)


def _sigmoid(z):
    return 1.0 / (1.0 + jnp.exp(-z))


def _pick_tn(n):
    if n <= 2048:
        return n
    for t in (1024, 768, 512, 256, 128):
        if n % t == 0:
            return t
    return n


def mm_nn(a, b, out_dtype, name, addend=None, add_scale=1.0, also_bf16=False, trans_b=None):
    m, k = a.shape
    n = b.shape[1] if trans_b is None else b.shape[0]
    tm = min(TM, m)
    tn = n
    while k * tn * 2 > MM_BLOCK_BYTES or tm * tn * 4 > MM_BLOCK_BYTES:
        tn //= 2
    chunk = tn if tn <= 2048 else _pick_tn(tn)
    has_add = addend is not None

    def body(*refs):
        a_ref, b_ref = refs[0], refs[1]
        c_ref = refs[2] if has_add else None
        o_ref = refs[3] if has_add else refs[2]
        ob_ref = refs[-1] if also_bf16 else None
        av = a_ref[...].astype(BF16)
        for c in range(tn // chunk):
            cols = slice(c * chunk, (c + 1) * chunk)
            if trans_b is None:
                r = jnp.dot(av, b_ref[:, cols].astype(BF16), preferred_element_type=F32)
            else:
                r = lax.dot_general(av, b_ref[cols, :].astype(BF16), NT_DIMS, preferred_element_type=F32)
            if has_add:
                r = r + add_scale * c_ref[:, cols]
            o_ref[:, cols] = r.astype(out_dtype)
            if also_bf16:
                ob_ref[:, cols] = r.astype(BF16)

    b_spec = (pl.BlockSpec((k, tn), lambda j, i: (0, j)) if trans_b is None
              else pl.BlockSpec((tn, k), lambda j, i: (j, trans_b)))
    in_specs = [pl.BlockSpec((tm, k), lambda j, i: (i, 0)), b_spec]
    ops = [a, b]
    tile = pl.BlockSpec((tm, tn), lambda j, i: (i, j))
    if has_add:
        in_specs.append(tile)
        ops.append(addend)
    out_shape = [SDS((m, n), out_dtype)]
    out_specs = [tile]
    if also_bf16:
        out_shape.append(SDS((m, n), BF16))
        out_specs.append(tile)
    res = pl.pallas_call(
        body, name=name, grid=(n // tn, m // tm), in_specs=in_specs, out_specs=out_specs, out_shape=out_shape,
        compiler_params=_params(("parallel", "parallel")))(*ops)
    return tuple(res) if also_bf16 else res[0]


def mm_tn(a, b, name, blocked=False):
    t, m = a.shape
    _, n = b.shape
    tt = min(4 * TM, t)
    tm = 1024 if m % 1024 == 0 else m
    tn = FF_BLOCK_PAD if blocked else _pick_tn(n)
    nt = t // tt

    def body(a_ref, b_ref, o_ref):
        kk = pl.program_id(2)
        r = lax.dot_general(a_ref[...].astype(BF16), b_ref[...].astype(BF16), TN_DIMS, preferred_element_type=F32)
        if blocked:
            r = r[None]

        @pl.when(kk == 0)
        def _():
            o_ref[...] = r

        @pl.when(kk != 0)
        def _():
            o_ref[...] += r

    if blocked:
        out_shape = SDS((n // tn, m, tn), F32)
        out_spec = pl.BlockSpec((1, tm, tn), lambda i, j, kk: (j, i, 0))
    else:
        out_shape = SDS((m, n), F32)
        out_spec = pl.BlockSpec((tm, tn), lambda i, j, kk: (i, j))
    return pl.pallas_call(
        body, name=name, grid=(m // tm, n // tn, nt),
        in_specs=[pl.BlockSpec((tt, tm), lambda i, j, kk: (kk, i)), pl.BlockSpec((tt, tn), lambda i, j, kk: (kk, j))],
        out_specs=out_spec, out_shape=out_shape,
        compiler_params=_params(("parallel", "parallel", "arbitrary")))(a, b)


def ln_fwd(xprev, a, w, g, b, name):
    t, d = xprev.shape
    k = a.shape[1]
    tm = min(TM, t)

    def body(xp_ref, a_ref, w_ref, g_ref, b_ref, y_ref, yb_ref, xh_ref, rs_ref):
        r = DN_ALPHA * xp_ref[...] + jnp.dot(a_ref[...], w_ref[...], preferred_element_type=F32)
        mu = jnp.mean(r, axis=1, keepdims=True)
        xc = r - mu
        var = jnp.mean(xc * xc, axis=1, keepdims=True)
        rstd = lax.rsqrt(var + LN_EPS)
        xh = xc * rstd
        y = xh * g_ref[...] + b_ref[...]
        y_ref[...] = y
        yb_ref[...] = y.astype(BF16)
        xh_ref[...] = xh
        rs_ref[...] = jnp.broadcast_to(rstd, (tm, LANES))

    row = pl.BlockSpec((tm, d), lambda i: (i, 0))
    vec = pl.BlockSpec((1, d), lambda i: (0, 0))
    return pl.pallas_call(
        body, name=name, grid=(t // tm,),
        in_specs=[row, pl.BlockSpec((tm, k), lambda i: (i, 0)), pl.BlockSpec((k, d), lambda i: (0, 0)), vec, vec],
        out_specs=[row, row, row, pl.BlockSpec((tm, LANES), lambda i: (i, 0))],
        out_shape=[SDS((t, d), F32), SDS((t, d), BF16), SDS((t, d), F32), SDS((t, LANES), F32)],
        compiler_params=_params(("parallel",)))(xprev, a, w, g, b)


def ln_bwd(dy, xhat, rstd, g, name, product=None, dy_scale=1.0):
    t, d = dy.shape
    tm = min(TM, t)
    fused = product is not None

    def body(*refs):
        if fused:
            a_ref, w_ref = refs[0], refs[1]
            refs = refs[2:]
        dy_ref, xh_ref, rs_ref, g_ref, dr_ref, drb_ref, dg_ref, db_ref = refs
        i = pl.program_id(0)
        dyv = dy_ref[...] if dy_scale == 1.0 else dy_scale * dy_ref[...]
        if fused:
            dyv = dyv + lax.dot_general(a_ref[...], w_ref[...], NT_DIMS, preferred_element_type=F32)
        xh = xh_ref[...]
        dxh = dyv * g_ref[...]
        m1 = jnp.mean(dxh, axis=1, keepdims=True)
        m2 = jnp.mean(dxh * xh, axis=1, keepdims=True)
        dr = rs_ref[:, 0:1] * (dxh - m1 - xh * m2)
        dr_ref[...] = dr
        drb_ref[...] = dr.astype(BF16)

        @pl.when(i == 0)
        def _():
            dg_ref[...] = jnp.zeros_like(dg_ref)
            db_ref[...] = jnp.zeros_like(db_ref)

        dg_ref[...] += jnp.sum(dyv * xh, axis=0, keepdims=True)
        db_ref[...] += jnp.sum(dyv, axis=0, keepdims=True)

    row = pl.BlockSpec((tm, d), lambda i: (i, 0))
    vec = pl.BlockSpec((1, d), lambda i: (0, 0))
    in_specs = [row, row, pl.BlockSpec((tm, LANES), lambda i: (i, 0)), vec]
    ops = [dy, xhat, rstd, g]
    if fused:
        k = product[0].shape[1]
        col = product[2]
        in_specs = [pl.BlockSpec((tm, k), lambda i: (i, 0)), pl.BlockSpec((d, k), lambda i: (0, col))] + in_specs
        ops = list(product[:2]) + ops
    return pl.pallas_call(
        body, name=name, grid=(t // tm,), in_specs=in_specs, out_specs=[row, row, vec, vec],
        out_shape=[SDS((t, d), F32), SDS((t, d), BF16), SDS((1, d), F32), SDS((1, d), F32)],
        compiler_params=_params(("arbitrary",)))(*ops)


def loss_head(y, target):
    t, d = y.shape
    tm = min(TM, t)
    nsteps = t // tm

    def body(y_ref, t_ref, dy_ref, l_ref, acc):
        i = pl.program_id(0)
        diff = y_ref[...] - t_ref[...]
        dy_ref[...] = diff * (1.0 / d)

        @pl.when(i == 0)
        def _():
            acc[...] = jnp.zeros_like(acc)

        acc[...] += jnp.sum(diff * diff, axis=0, keepdims=True)

        @pl.when(i == nsteps - 1)
        def _():
            tot = jnp.sum(acc[...], axis=1, keepdims=True) * (0.5 / d)
            l_ref[...] = jnp.broadcast_to(tot, (1, LANES))

    row = pl.BlockSpec((tm, d), lambda i: (i, 0))
    return pl.pallas_call(
        body, name="loss_head", grid=(nsteps,), in_specs=[row, row],
        out_specs=[row, pl.BlockSpec((1, LANES), lambda i: (0, 0))],
        out_shape=[SDS((t, d), F32), SDS((1, LANES), F32)],
        scratch_shapes=[pltpu.VMEM((1, d), F32)],
        compiler_params=_params(("arbitrary",)))(y, target)


def memattn_fwd(proj, memkv, nb, s, name):
    ts = min(TS, s)
    nq = s // ts

    def body(q_ref, kv_ref, o_ref):
        top = lax.broadcasted_iota(jnp.int32, (PAIR, ts), 0) < HEAD_DIM
        scores = []
        for p in range(MEM_HEADS // 2):
            qp = q_ref[:, p * PAIR:(p + 1) * PAIR].astype(BF16)
            ke, ko = _split_pair(kv_ref[:, p * PAIR:(p + 1) * PAIR], QK_SCALE)
            scores.append([lax.dot_general(km, qp, NT_DIMS, preferred_element_type=F32) for km in (ke, ko)])
        for p in range(MEM_HEADS // 2):
            vt = kv_ref[:, MEM_WIDTH + p * PAIR:MEM_WIDTH + (p + 1) * PAIR].astype(F32).T.astype(BF16)
            outs = []
            for sc in scores[p]:
                e = jnp.exp(sc - jnp.max(sc, axis=0, keepdims=True))
                pr = e / jnp.sum(e, axis=0, keepdims=True)
                outs.append(jnp.dot(vt, pr.astype(BF16), preferred_element_type=F32))
            o_ref[:, p * PAIR:(p + 1) * PAIR] = jnp.where(top, outs[0], outs[1]).T.astype(BF16)

    return pl.pallas_call(
        body, name=name, grid=(nb, nq),
        in_specs=[pl.BlockSpec((ts, MEM_WIDTH), lambda b, i: (b * nq + i, 3)),
                  pl.BlockSpec((MEM_LEN, 2 * MEM_WIDTH), lambda b, i: (b, 0))],
        out_specs=pl.BlockSpec((ts, MEM_WIDTH), lambda b, i: (b * nq + i, 0)),
        out_shape=SDS((nb * s, MEM_WIDTH), BF16),
        compiler_params=_params(("parallel", "parallel")))(proj, memkv)


def memattn_bwd(proj, memkv, dcat, nb, s, name):
    ts = min(TS, s)
    nq = s // ts

    def body(q_ref, kv_ref, do_ref, dq_ref, dkv_ref):
        i = pl.program_id(1)

        @pl.when(i == 0)
        def _():
            dkv_ref[...] = jnp.zeros_like(dkv_ref)

        lo = _half_masks(MEM_LEN)
        top = lax.broadcasted_iota(jnp.int32, (PAIR, ts), 0) < HEAD_DIM
        n_pairs = MEM_HEADS // 2
        qs, dos, kps, products = [], [], [], []
        for p in range(n_pairs):
            qp = q_ref[:, p * PAIR:(p + 1) * PAIR].astype(BF16)
            dop = do_ref[:, p * PAIR:(p + 1) * PAIR].astype(BF16)
            kp = kv_ref[:, p * PAIR:(p + 1) * PAIR] * QK_SCALE
            kms = _split_pair(kp)
            vms = _split_pair(kv_ref[:, MEM_WIDTH + p * PAIR:MEM_WIDTH + (p + 1) * PAIR])
            products.append([(lax.dot_general(km, qp, NT_DIMS, preferred_element_type=F32),
                              lax.dot_general(vm, dop, NT_DIMS, preferred_element_type=F32)) for km, vm in zip(kms, vms)])
            qs.append(qp)
            dos.append(dop)
            kps.append(kp)
        for p in range(n_pairs):
            kt = kps[p].astype(F32).T.astype(BF16)
            dks, dvs, dqs = [], [], []
            for sc, dp in products[p]:
                e = jnp.exp(sc - jnp.max(sc, axis=0, keepdims=True))
                pr = e / jnp.sum(e, axis=0, keepdims=True)
                dl = jnp.sum(pr * dp, axis=0, keepdims=True)
                ds = (pr * (dp - dl)).astype(BF16)
                dvs.append(jnp.dot(pr.astype(BF16), dos[p], preferred_element_type=F32))
                dks.append(jnp.dot(ds, qs[p], preferred_element_type=F32))
                dqs.append(jnp.dot(kt, ds, preferred_element_type=F32))
            dq_ref[:, p * PAIR:(p + 1) * PAIR] = jnp.where(top, dqs[0], dqs[1]).T.astype(BF16)
            dkv_ref[:, p * PAIR:(p + 1) * PAIR] += jnp.where(lo, dks[0], dks[1]) * QK_SCALE
            dkv_ref[:, MEM_WIDTH + p * PAIR:MEM_WIDTH + (p + 1) * PAIR] += jnp.where(lo, dvs[0], dvs[1])

    return pl.pallas_call(
        body, name=name, grid=(nb, nq),
        in_specs=[pl.BlockSpec((ts, MEM_WIDTH), lambda b, i: (b * nq + i, 3)),
                  pl.BlockSpec((MEM_LEN, 2 * MEM_WIDTH), lambda b, i: (b, 0)),
                  pl.BlockSpec((ts, MEM_WIDTH), lambda b, i: (b * nq + i, 3))],
        out_specs=[pl.BlockSpec((ts, MEM_WIDTH), lambda b, i: (b * nq + i, 0)),
                   pl.BlockSpec((MEM_LEN, 2 * MEM_WIDTH), lambda b, i: (b, 0))],
        out_shape=[SDS((nb * s, MEM_WIDTH), BF16), SDS((nb * MEM_LEN, 2 * MEM_WIDTH), F32)],
        compiler_params=_params(("parallel", "arbitrary")))(proj, memkv, dcat)


def _pool_select(shape, s2, s4, s8, s16):
    lane = lax.broadcasted_iota(jnp.int32, shape, 1)
    return jnp.where(lane < POOL_GROUP, s2, jnp.where(lane < 2 * POOL_GROUP, s4, jnp.where(lane < 3 * POOL_GROUP, s8, s16)))


def _pool_count(shape, first_pos):
    pos = first_pos + lax.broadcasted_iota(jnp.int32, shape, 0)
    win = _pool_select(shape, 2, 4, 8, 16)
    return jnp.minimum(pos + 1, win).astype(F32)


def pool_fwd(proj, pw_bd, pscale, nb, s):
    ts = min(TS, s)
    nq = s // ts
    w = TOK_WIDTH

    def body(c_ref, h_ref, w_ref, sc_ref, pooled_ref, tok_ref):
        i = pl.program_id(0) % nq
        cur = c_ref[...]
        halo = jnp.where(i == 0, 0.0, h_ref[...])
        xe = jnp.concatenate([halo, cur], axis=0)
        s2 = xe + pltpu.roll(xe, 1, axis=0)
        s4 = s2 + pltpu.roll(s2, 2, axis=0)
        s8 = s4 + pltpu.roll(s4, 4, axis=0)
        s16 = s8 + pltpu.roll(s8, 8, axis=0)
        hp = HALO_POOL
        ws = _pool_select((ts, w), s2[hp:], s4[hp:], s8[hp:], s16[hp:])
        pooled = (ws / _pool_count((ts, w), i * ts) - cur).astype(BF16)
        pooled_ref[...] = pooled
        mixed = jnp.dot(pooled, w_ref[...], preferred_element_type=F32)
        tok_ref[...] = (mixed * sc_ref[...]).astype(BF16)

    row = pl.BlockSpec((ts, w), lambda r: (r, 0))
    return pl.pallas_call(
        body, name="pool_fwd", grid=(nb * nq,),
        in_specs=[row, pl.BlockSpec((HALO_POOL, w), lambda r: (jnp.maximum(r * (ts // HALO_POOL) - 1, 0), 0)),
                  pl.BlockSpec((w, w), lambda r: (0, 0)), pl.BlockSpec((1, w), lambda r: (0, 0))],
        out_specs=[row, row], out_shape=[SDS((nb * s, w), BF16), SDS((nb * s, w), BF16)],
        compiler_params=_params(("parallel",)))(proj, proj, pw_bd, pscale)


def pool_bwd_mix(dcat, pooled, pw_bd, pscale, nb, s):
    ts = min(TS, s)
    w = TOK_WIDTH

    def body(dt_ref, p_ref, w_ref, sc_ref, dm_ref, dp_ref, ds_ref):
        r = pl.program_id(0)
        dtok = dt_ref[...]
        mixed = jnp.dot(p_ref[...], w_ref[...], preferred_element_type=F32)

        @pl.when(r == 0)
        def _():
            ds_ref[...] = jnp.zeros_like(ds_ref)

        ds_ref[...] += jnp.sum(dtok * mixed, axis=0, keepdims=True)
        dmx = (dtok * sc_ref[...]).astype(BF16)
        dm_ref[...] = dmx
        dp_ref[...] = lax.dot_general(dmx, w_ref[...], NT_DIMS, preferred_element_type=F32)

    row = pl.BlockSpec((ts, w), lambda r: (r, 0))
    mat = pl.BlockSpec((w, w), lambda r: (0, 0))
    vec = pl.BlockSpec((1, w), lambda r: (0, 0))
    return pl.pallas_call(
        body, name="pool_bwd_mix", grid=(nb * s // ts,), in_specs=[row, row, mat, vec],
        out_specs=[row, row, vec], out_shape=[SDS((nb * s, w), BF16), SDS((nb * s, w), F32), SDS((1, w), F32)],
        compiler_params=_params(("arbitrary",)))(dcat, pooled, pw_bd, pscale)


def pool_bwd_window(dpooled, nb, s):
    ts = min(TS, s)
    nq = s // ts
    w = TOK_WIDTH
    n_ext = ts + HALO_POOL
    n_halo_blocks = nb * s // HALO_POOL

    def body(c_ref, n_ref, du_ref):
        i = pl.program_id(0) % nq
        cur = c_ref[...]
        nxt = jnp.where(i == nq - 1, 0.0, n_ref[...])
        ze = jnp.concatenate([cur, nxt], axis=0) / _pool_count((n_ext, w), i * ts)
        s2 = ze + pltpu.roll(ze, n_ext - 1, axis=0)
        s4 = s2 + pltpu.roll(s2, n_ext - 2, axis=0)
        s8 = s4 + pltpu.roll(s4, n_ext - 4, axis=0)
        s16 = s8 + pltpu.roll(s8, n_ext - 8, axis=0)
        ws = _pool_select((ts, w), s2[:ts], s4[:ts], s8[:ts], s16[:ts])
        du_ref[...] = (ws - cur).astype(BF16)

    row = pl.BlockSpec((ts, w), lambda r: (r, 0))
    return pl.pallas_call(
        body, name="pool_bwd_window", grid=(nb * nq,),
        in_specs=[row, pl.BlockSpec((HALO_POOL, w),
                                    lambda r: (jnp.minimum((r + 1) * (ts // HALO_POOL), n_halo_blocks - 1), 0))],
        out_specs=row, out_shape=SDS((nb * s, w), BF16),
        compiler_params=_params(("parallel",)))(dpooled, dpooled)


def _conv_rows(xe, w_ref):
    return (w_ref[0, 2:3, :] * xe + w_ref[0, 1:2, :] * pltpu.roll(xe, 1, axis=0)
            + w_ref[0, 0:1, :] * pltpu.roll(xe, 2, axis=0) + w_ref[0, 3:4, :])


def ffn_up_gate(x_bf, wup, cw, nb, s, name):
    tm = min(TM, s)
    nq = s // tm
    w = FF_BLOCK_PAD
    hr = 2 * HALO_CONV
    k = x_bf.shape[1]

    def body(xc_ref, xh_ref, wu_ref, wg_ref, cu_ref, cg_ref, act_ref, a_ref, b_ref, hu_ref, hg_ref):
        first = (pl.program_id(1) % nq) == 0
        xc = xc_ref[...]
        xh = xh_ref[...]

        def products(w_ref):
            return (jnp.dot(xc, w_ref[...], preferred_element_type=F32), jnp.dot(xh, w_ref[...], preferred_element_type=F32))

        def conv(hcur, hprev, c_ref, h_out):
            h_out[...] = hcur.astype(BF16)
            xe = jnp.concatenate([jnp.where(first, 0.0, hprev), hcur], axis=0)
            return _conv_rows(xe, c_ref)[hr:]

        pu, pg = products(wu_ref), products(wg_ref)
        cu = conv(*pu, cu_ref, hu_ref)
        cg = conv(*pg, cg_ref, hg_ref)
        sg = _sigmoid(cg)
        a = cg * sg
        act_ref[...] = (a * cu).astype(BF16)
        a_ref[...] = a.astype(BF16)
        b_ref[...] = (cu * (sg * (1.0 + cg * (1.0 - sg)))).astype(BF16)

    def wblock(off):
        return pl.BlockSpec((k, w), lambda j, r: (0, j + off))

    def cblock(off):
        return pl.BlockSpec((1, 8, w), lambda j, r: (j + off, 0, 0))

    tile = pl.BlockSpec((tm, w), lambda j, r: (r, j))
    out = SDS((nb * s, FF_PAIRS * w), BF16)
    return pl.pallas_call(
        body, name=name, grid=(FF_PAIRS, nb * nq),
        in_specs=[pl.BlockSpec((tm, k), lambda j, r: (r, 0)),
                  pl.BlockSpec((hr, k), lambda j, r: (jnp.maximum(r * (tm // hr) - 1, 0), 0)),
                  wblock(0), wblock(FF_PAIRS), cblock(0), cblock(FF_PAIRS)],
        out_specs=[tile] * 5, out_shape=[out] * 5,
        compiler_params=_params(("parallel", "parallel")))(x_bf, x_bf, wup, wup, cw, cw)


def gate_conv_bwd(dact, a, b, hu, hg, cw, nb, s, name):
    ts = min(TS, s)
    nq = s // ts
    w = FF_BLOCK_PAD
    hc = HALO_CONV
    hb = 2 * hc
    n_ext = ts + hc

    def body(dc_ref, dn_ref, ac_ref, an_ref, bc_ref, bn_ref, hu_ref, hg_ref, wu_ref, wg_ref,
             dhu_ref, dhg_ref, dwu_ref, dwg_ref):
        r = pl.program_id(1)
        last = (r % nq) == nq - 1

        def ext(c_ref, n_ref, mask_next=False):
            nxt = n_ref[...].astype(F32)[:hc]
            if mask_next:
                nxt = jnp.where(last, 0.0, nxt)
            return jnp.concatenate([c_ref[...].astype(F32), nxt], axis=0)

        da = ext(dc_ref, dn_ref, mask_next=True)

        def branch(dcv, w_ref, h_ref, dh_ref, dw_ref):
            d0 = dcv[:ts]
            d1 = pltpu.roll(dcv, n_ext - 1, axis=0)[:ts]
            d2 = pltpu.roll(dcv, n_ext - 2, axis=0)[:ts]
            dh_ref[...] = (w_ref[0, 2:3, :] * d0 + w_ref[0, 1:2, :] * d1 + w_ref[0, 0:1, :] * d2).astype(BF16)
            hv = h_ref[...].astype(F32)
            rows = [jnp.sum(d2 * hv, axis=0, keepdims=True), jnp.sum(d1 * hv, axis=0, keepdims=True),
                    jnp.sum(d0 * hv, axis=0, keepdims=True), jnp.sum(d0, axis=0, keepdims=True)]
            sub = lax.broadcasted_iota(jnp.int32, (8, w), 0)
            upd = jnp.zeros((8, w), F32)
            for kk, rv in enumerate(rows):
                upd = jnp.where(sub == kk, rv, upd)

            @pl.when(r == 0)
            def _():
                dw_ref[...] = jnp.zeros_like(dw_ref)

            dw_ref[...] += upd[None]

        branch(da * ext(ac_ref, an_ref), wu_ref, hu_ref, dhu_ref, dwu_ref)
        branch(da * ext(bc_ref, bn_ref), wg_ref, hg_ref, dhg_ref, dwg_ref)

    cur = pl.BlockSpec((ts, w), lambda j, r: (r, j))
    nxt = pl.BlockSpec((hb, w), lambda j, r: (jnp.minimum((r + 1) * (ts // hb), nb * s // hb - 1), j))

    def wspec(off):
        return pl.BlockSpec((1, 8, w), lambda j, r: (j + off, 0, 0))

    p = FF_PAIRS
    dw_spec = pl.BlockSpec((1, 8, w), lambda j, r: (j, 0, 0))
    return pl.pallas_call(
        body, name=name, grid=(p, nb * nq),
        in_specs=[cur, nxt, cur, nxt, cur, nxt, cur, cur, wspec(0), wspec(p)],
        out_specs=[cur, cur, dw_spec, dw_spec],
        out_shape=[SDS((nb * s, p * w), BF16), SDS((nb * s, p * w), BF16), SDS((p, 8, w), F32), SDS((p, 8, w), F32)],
        compiler_params=_params(("parallel", "arbitrary")))(dact, dact, a, a, b, b, hu, hg, cw, cw)


def _tri(n, upper):
    r = lax.broadcasted_iota(jnp.int32, (n, n), 0)
    c = lax.broadcasted_iota(jnp.int32, (n, n), 1)
    return ((r <= c) if upper else (r >= c)).astype(F32)


def fgate_fwd(fl, fb, nb, s):
    tc = min(TC, s)
    nq = s // tc

    def body(fl_ref, fb_ref, f_ref, carry):
        @pl.when(pl.program_id(1) == 0)
        def _():
            carry[...] = jnp.zeros_like(carry)

        z = fl_ref[...] + fb_ref[...]
        logf = jnp.minimum(z, 0.0) - jnp.log(1.0 + jnp.exp(-jnp.abs(z)))
        f_ref[...] = jnp.dot(_tri(tc, False), logf, preferred_element_type=F32,
                             precision=lax.Precision.HIGHEST) + carry[...]
        carry[...] += jnp.sum(logf, axis=0, keepdims=True)

    row = pl.BlockSpec((tc, LANES), lambda b, i: (b * nq + i, 0))
    return pl.pallas_call(
        body, name="fgate_fwd", grid=(nb, nq), in_specs=[row, pl.BlockSpec((1, LANES), lambda b, i: (0, 0))],
        out_specs=row, out_shape=SDS((nb * s, LANES), F32), scratch_shapes=[pltpu.VMEM((1, LANES), F32)],
        compiler_params=_params(("arbitrary", "arbitrary")))(fl, fb)


def fgate_bwd(d_cum_q, d_cum_k, fl, fb, nb, s):
    tc = min(TC, s)
    nq = s // tc

    def body(dfq_ref, dfk_ref, fl_ref, fb_ref, dfl_ref, dfb_ref, carry):
        b = pl.program_id(0)
        i = pl.program_id(1)

        @pl.when(i == 0)
        def _():
            carry[...] = jnp.zeros_like(carry)

        @pl.when(jnp.logical_and(b == 0, i == 0))
        def _():
            dfb_ref[...] = jnp.zeros_like(dfb_ref)

        dfv = dfq_ref[...] + dfk_ref[...]
        dlog = jnp.dot(_tri(tc, True), dfv, preferred_element_type=F32,
                       precision=lax.Precision.HIGHEST) + carry[...]
        carry[...] += jnp.sum(dfv, axis=0, keepdims=True)
        z = fl_ref[...] + fb_ref[...]
        dfl = dlog / (1.0 + jnp.exp(z))
        dfl_ref[...] = dfl
        dfb_ref[...] += jnp.sum(dfl, axis=0, keepdims=True)

    row = pl.BlockSpec((tc, LANES), lambda b, i: (b * nq + nq - 1 - i, 0))
    vec = pl.BlockSpec((1, LANES), lambda b, i: (0, 0))
    return pl.pallas_call(
        body, name="fgate_bwd", grid=(nb, nq), in_specs=[row, row, row, vec], out_specs=[row, vec],
        out_shape=[SDS((nb * s, LANES), F32), SDS((1, LANES), F32)], scratch_shapes=[pltpu.VMEM((1, LANES), F32)],
        compiler_params=_params(("arbitrary", "arbitrary")))(d_cum_q, d_cum_k, fl, fb)


PAIR = 2 * HEAD_DIM
N_PAIRS = FOX_HEADS // 2


def _lane_put(shape, h, col):
    lane = lax.broadcasted_iota(jnp.int32, shape, 1)
    return jnp.where(lane == h, col, 0.0)


def _half_masks(rows):
    lane = lax.broadcasted_iota(jnp.int32, (rows, PAIR), 1)
    return lane < HEAD_DIM


def _split_pair(x, scale=None):
    if scale is not None:
        x = x * scale
    lo = _half_masks(x.shape[0])
    zero = jnp.zeros_like(x)
    return jnp.where(lo, x, zero), jnp.where(lo, zero, x)


def _to_tile_rows(a, nb, s, tf):
    return a.reshape(nb * s // tf, tf, LANES)[:, :, :16].transpose(0, 2, 1)


def _from_tile_rows(a):
    tiles, _, tf = a.shape
    return jnp.pad(a.transpose(0, 2, 1), ((0, 0), (0, 0), (0, LANES - 16))).reshape(tiles * tf, LANES)


BIAS_TERMS = 3
LOOKAHEAD = 4
FOLLOW_FWD = 1
LOOKAHEAD_BWD = 2
FOLLOW_BWD = 1


def _bias_lane(h):
    return HEAD_DIM if h % 2 == 0 else 0


def _placement():
    rows = jnp.arange(LANES)[:, None]
    cols = jnp.arange(FOX_HEADS * PAIR)[None, :]
    head, lane = cols // PAIR, cols % PAIR
    first = jnp.where(head % 2 == 0, HEAD_DIM, 0)
    term = lane - first
    hit = (term >= 0) & (term < BIAS_TERMS) & (rows == 16 * term + head)
    return hit.astype(BF16)


def fox_prep(kv, fneg, nb, s):
    tf = min(TF, s)
    w = TOK_WIDTH

    def body(k_ref, v_ref, f_ref, pl_ref, ka_ref, vt_ref):
        lane = lax.broadcasted_iota(jnp.int32, (tf, LANES), 1)
        lo = lane < HEAD_DIM
        f = jnp.where(lane < FOX_HEADS, f_ref[...], 0.0)
        hi = f.astype(BF16).astype(F32)
        mid = (f - hi).astype(BF16).astype(F32)
        low = (f - hi - mid).astype(BF16).astype(F32)
        terms = (hi + pltpu.roll(mid, 16, axis=1) + pltpu.roll(low, 32, axis=1)).astype(BF16)
        placed = jnp.dot(terms, pl_ref[...], preferred_element_type=F32).astype(BF16)
        one = jnp.ones((tf, LANES), BF16)
        zero = jnp.zeros((tf, LANES), BF16)
        for p in range(N_PAIRS):
            kp = k_ref[:, p * PAIR:(p + 1) * PAIR] * QK_SCALE
            vp = v_ref[:, p * PAIR:(p + 1) * PAIR]
            he, ho = 2 * p, 2 * p + 1
            ka_ref[:, he * PAIR:(he + 1) * PAIR] = jnp.where(lo, kp, placed[:, he * PAIR:(he + 1) * PAIR])
            ka_ref[:, ho * PAIR:(ho + 1) * PAIR] = jnp.where(lo, placed[:, ho * PAIR:(ho + 1) * PAIR], kp)
            ve = jnp.where(lo, vp, jnp.where(lane == HEAD_DIM, one, zero))
            vo = jnp.where(lo, jnp.where(lane == 0, one, zero), vp)
            vt_ref[0, he * PAIR:(he + 1) * PAIR, :] = ve.astype(F32).T.astype(BF16)
            vt_ref[0, ho * PAIR:(ho + 1) * PAIR, :] = vo.astype(F32).T.astype(BF16)

    return pl.pallas_call(
        body, name="fox_prep", grid=(nb * s // tf,),
        in_specs=[pl.BlockSpec((tf, w), lambda r: (r, 0)), pl.BlockSpec((tf, w), lambda r: (r, 1)),
                  pl.BlockSpec((tf, LANES), lambda r: (r, 0)), pl.BlockSpec((LANES, FOX_HEADS * PAIR), lambda r: (0, 0))],
        out_specs=[pl.BlockSpec((tf, FOX_HEADS * PAIR), lambda r: (r, 0)),
                   pl.BlockSpec((1, FOX_HEADS * PAIR, tf), lambda r: (r, 0, 0))],
        out_shape=[SDS((nb * s, FOX_HEADS * PAIR), BF16), SDS((nb * s // tf, FOX_HEADS * PAIR, tf), BF16)],
        compiler_params=_params(("parallel",)))(kv, kv, fneg, _placement())


def fox_fwd_t(pq, kaug, vaug_t, nb, s):
    tf = min(TF, s)
    n = s // tf
    w = TOK_WIDTH
    wa = FOX_HEADS * PAIR

    def body(q_ref, k_hbm, vt_hbm, ob_ref, of_ref, lse_ref, k_vm, vt_vm, qx_scr, m_scr, acc_scr, sems):
        b = pl.program_id(0)
        i = pl.program_id(1)

        @pl.when(i == 0)
        def _():
            ck = pltpu.make_async_copy(k_hbm.at[pl.ds(pl.multiple_of(b * s, tf), s)], k_vm, sems.at[0])
            cv = pltpu.make_async_copy(vt_hbm.at[pl.ds(b * n, n)], vt_vm, sems.at[1])
            ck.start()
            cv.start()
            ck.wait()
            cv.wait()

        lane = lax.broadcasted_iota(jnp.int32, (tf, PAIR), 1)
        one = jnp.ones((tf, PAIR), BF16)
        zero = jnp.zeros((tf, PAIR), BF16)
        for p in range(N_PAIRS):
            qp = q_ref[:, p * PAIR:(p + 1) * PAIR]
            be, bo = _bias_lane(2 * p), _bias_lane(2 * p + 1)
            ones_e = jnp.where((lane >= be) & (lane < be + BIAS_TERMS), one, zero)
            ones_o = jnp.where((lane >= bo) & (lane < bo + BIAS_TERMS), one, zero)
            qx_scr[2 * p] = jnp.where(lane < HEAD_DIM, qp, ones_e)
            qx_scr[2 * p + 1] = jnp.where(lane < HEAD_DIM, ones_o, qp)
        m_scr[...] = jnp.full(m_scr.shape, NEG_BIG, F32)
        acc_scr[...] = jnp.zeros_like(acc_scr)

        def tile(j, masked):
            ks = pl.multiple_of(j * tf, tf)
            if masked:
                keep = lax.broadcasted_iota(jnp.int32, (tf, tf), 1) >= lax.broadcasted_iota(jnp.int32, (tf, tf), 0)
            def scores(h):
                kx = k_vm[pl.ds(ks, tf), h * PAIR:(h + 1) * PAIR]
                return lax.dot_general(kx, qx_scr[h], NT_DIMS, preferred_element_type=F32)

            def values(h, pr, a):
                pv = jnp.dot(vt_vm[j, h * PAIR:(h + 1) * PAIR, :], pr, preferred_element_type=F32)
                acc_scr[h] = a * acc_scr[h] + pv

            ahead = [scores(h) for h in range(LOOKAHEAD)]
            behind = []
            for h in range(FOX_HEADS):
                sc = ahead.pop(0)
                if h + LOOKAHEAD < FOX_HEADS:
                    ahead.append(scores(h + LOOKAHEAD))
                if masked:
                    sc = jnp.where(keep, sc, NEG_BIG)
                m_prev = m_scr[h]
                m_new = jnp.maximum(m_prev, jnp.max(sc, axis=0, keepdims=True))
                m_scr[h] = m_new
                behind.append((h, jnp.exp(sc - m_new).astype(BF16), jnp.exp(m_prev - m_new)))
                if len(behind) > FOLLOW_FWD:
                    values(*behind.pop(0))
            for item in behind:
                values(*item)

        def step(j, carry):
            tile(j, False)
            return carry

        lax.fori_loop(0, i, step, 0)
        tile(i, True)

        top = lax.broadcasted_iota(jnp.int32, (PAIR, tf), 0) < HEAD_DIM
        sub = lax.broadcasted_iota(jnp.int32, (16, tf), 0)
        lse = jnp.zeros((16, tf), F32)
        for p in range(N_PAIRS):
            he, ho = 2 * p, 2 * p + 1
            le = acc_scr[he, HEAD_DIM:HEAD_DIM + 1, :]
            lod = acc_scr[ho, 0:1, :]
            o = jnp.where(top, acc_scr[he] / le, acc_scr[ho] / lod).T
            ob_ref[:, p * PAIR:(p + 1) * PAIR] = o.astype(BF16)
            of_ref[:, p * PAIR:(p + 1) * PAIR] = o
            lse = jnp.where(sub == he, m_scr[he] + jnp.log(le), lse)
            lse = jnp.where(sub == ho, m_scr[ho] + jnp.log(lod), lse)
        lse_ref[0] = lse

    qrow = lambda b, i: (b * n + i, 0)
    return pl.pallas_call(
        body, name="fox_fwd", grid=(nb, n),
        in_specs=[pl.BlockSpec((tf, w), qrow), ANY_SPEC, ANY_SPEC],
        out_specs=[pl.BlockSpec((tf, w), qrow), pl.BlockSpec((tf, w), qrow),
                   pl.BlockSpec((1, 16, tf), lambda b, i: (b * n + i, 0, 0))],
        out_shape=[SDS((nb * s, w), BF16), SDS((nb * s, w), F32), SDS((nb * n, 16, tf), F32)],
        scratch_shapes=[pltpu.VMEM((s, wa), BF16), pltpu.VMEM((n, wa, tf), BF16),
                        pltpu.VMEM((FOX_HEADS, tf, PAIR), BF16), pltpu.VMEM((FOX_HEADS, 1, tf), F32),
                        pltpu.VMEM((FOX_HEADS, PAIR, tf), F32), pltpu.SemaphoreType.DMA((2,))],
        compiler_params=_params(("arbitrary", "arbitrary")))(pq, kaug, vaug_t)


def fox_delta(dcat, o, nb, s):
    tf = min(TM, s)
    w = TOK_WIDTH

    def body(do_ref, o_ref, dl_ref):
        out = jnp.zeros((tf, LANES), F32)
        for h in range(FOX_HEADS):
            lo, hi = h * HEAD_DIM, (h + 1) * HEAD_DIM
            out = out + _lane_put((tf, LANES), h, jnp.sum(do_ref[:, lo:hi] * o_ref[:, lo:hi], axis=1, keepdims=True))
        dl_ref[...] = out

    row = pl.BlockSpec((tf, w), lambda r: (r, 0))
    return pl.pallas_call(
        body, name="fox_delta", grid=(nb * s // tf,), in_specs=[row, row],
        out_specs=pl.BlockSpec((tf, LANES), lambda r: (r, 0)), out_shape=SDS((nb * s, LANES), F32),
        compiler_params=_params(("parallel",)))(dcat, o)


def fox_bwd(pq, kv, fneg, dcat_bf, lse_rows, delta_rows, nb, s):
    tf = min(TF, s)
    n = s // tf
    w = TOK_WIDTH

    def body(q_hbm, k_ref, v_ref, f_ref, do_hbm, lse_ref, dl_ref, dq_ref, dk_ref, dv_ref, dfk_ref, dfq_ref,
             q_vm, do_vm, km_scr, vm_scr, kt_scr, fk_scr, dk_scr, dv_scr, rs_scr, dq_scr, fq_scr, sems):
        b = pl.program_id(0)
        j = pl.program_id(1)

        @pl.when(j == 0)
        def _():
            rows = pl.ds(pl.multiple_of(b * s, tf), s)
            cq = pltpu.make_async_copy(q_hbm.at[rows, pl.ds(0, w)], q_vm, sems.at[0])
            cd = pltpu.make_async_copy(do_hbm.at[rows, pl.ds(0, w)], do_vm, sems.at[1])
            cq.start()
            cd.start()
            dq_scr[...] = jnp.zeros_like(dq_scr)
            fq_scr[...] = jnp.zeros_like(fq_scr)
            cq.wait()
            cd.wait()

        for p in range(N_PAIRS):
            kp = k_ref[:, p * PAIR:(p + 1) * PAIR] * QK_SCALE
            ke, ko = _split_pair(kp)
            km_scr[2 * p] = ke
            km_scr[2 * p + 1] = ko
            kt_scr[p] = kp.astype(F32).T.astype(BF16)
            ve, vo = _split_pair(v_ref[:, p * PAIR:(p + 1) * PAIR])
            vm_scr[2 * p] = ve
            vm_scr[2 * p + 1] = vo
        for h in range(FOX_HEADS):
            fk_scr[h] = jnp.broadcast_to(f_ref[:, h:h + 1], (tf, tf))
        dk_scr[...] = jnp.zeros_like(dk_scr)
        dv_scr[...] = jnp.zeros_like(dv_scr)
        rs_scr[...] = jnp.zeros_like(rs_scr)

        def tile(i, masked):
            qs = pl.multiple_of(i * tf, tf)
            if masked:
                keep = lax.broadcasted_iota(jnp.int32, (tf, tf), 1) >= lax.broadcasted_iota(jnp.int32, (tf, tf), 0)
            def products(h):
                qp = q_vm[pl.ds(qs, tf), (h // 2) * PAIR:(h // 2 + 1) * PAIR]
                dop = do_vm[pl.ds(qs, tf), (h // 2) * PAIR:(h // 2 + 1) * PAIR]
                return (lax.dot_general(km_scr[h], qp, NT_DIMS, preferred_element_type=F32),
                        lax.dot_general(vm_scr[h], dop, NT_DIMS, preferred_element_type=F32))

            def dependents(h, prb, dsb):
                p = h // 2
                half = slice((h % 2) * HEAD_DIM, (h % 2 + 1) * HEAD_DIM)
                qp = q_vm[pl.ds(qs, tf), p * PAIR:(p + 1) * PAIR]
                dop = do_vm[pl.ds(qs, tf), p * PAIR:(p + 1) * PAIR]
                dv_scr[h] += jnp.dot(prb, dop, preferred_element_type=F32)
                dk_scr[h] += jnp.dot(dsb, qp, preferred_element_type=F32)
                dqt = jnp.dot(kt_scr[p], dsb, preferred_element_type=F32)
                dq_scr[i, p, half, :] += dqt[(h % 2) * HEAD_DIM:(h % 2 + 1) * HEAD_DIM]

            ahead = [products(h) for h in range(LOOKAHEAD_BWD)]
            behind = []
            for h in range(FOX_HEADS):
                sc, dp = ahead.pop(0)
                if h + LOOKAHEAD_BWD < FOX_HEADS:
                    ahead.append(products(h + LOOKAHEAD_BWD))
                sc = sc + fk_scr[h] - lse_ref[i, h:h + 1, :]
                if masked:
                    sc = jnp.where(keep, sc, NEG_BIG)
                pr = jnp.exp(sc)
                ds = pr * (dp - dl_ref[i, h:h + 1, :])
                part = ds[:, :LANES]
                for c in range(1, tf // LANES):
                    part = part + ds[:, c * LANES:(c + 1) * LANES]
                rs_scr[h] += part
                fq_scr[i, h:h + 1, :] += jnp.sum(ds, axis=0, keepdims=True)
                behind.append((h, pr.astype(BF16), ds.astype(BF16)))
                if len(behind) > FOLLOW_BWD:
                    dependents(*behind.pop(0))
            for item in behind:
                dependents(*item)

        def step(i, carry):
            tile(i, False)
            return carry

        tile(j, True)
        for p in range(N_PAIRS):
            dq_ref[:, p * PAIR:(p + 1) * PAIR] = dq_scr[j, p].T.astype(BF16)
        dfq_ref[0] = fq_scr[j]
        lax.fori_loop(j + 1, n, step, 0)

        lo = _half_masks(tf)
        dfk = jnp.zeros((tf, LANES), F32)
        for p in range(N_PAIRS):
            dk = jnp.where(lo, dk_scr[2 * p], dk_scr[2 * p + 1]) * QK_SCALE
            dk_ref[:, p * PAIR:(p + 1) * PAIR] = dk.astype(BF16)
            dv_ref[:, p * PAIR:(p + 1) * PAIR] = jnp.where(lo, dv_scr[2 * p], dv_scr[2 * p + 1]).astype(BF16)
            for h in (2 * p, 2 * p + 1):
                dfk = dfk - _lane_put((tf, LANES), h, jnp.sum(rs_scr[h], axis=1, keepdims=True))
        dfk_ref[...] = dfk

    krow = lambda b, j: (b * n + j, 0)
    rows = pl.BlockSpec((n, 16, tf), lambda b, j: (b, 0, 0))
    tile_out = pl.BlockSpec((tf, w), krow)
    return pl.pallas_call(
        body, name="fox_bwd", grid=(nb, n),
        in_specs=[ANY_SPEC, pl.BlockSpec((tf, w), krow), pl.BlockSpec((tf, w), lambda b, j: (b * n + j, 1)),
                  pl.BlockSpec((tf, LANES), krow), ANY_SPEC, rows, rows],
        out_specs=[tile_out, tile_out, tile_out, pl.BlockSpec((tf, LANES), krow),
                   pl.BlockSpec((1, 16, tf), lambda b, j: (b * n + j, 0, 0))],
        out_shape=[SDS((nb * s, w), BF16), SDS((nb * s, w), BF16), SDS((nb * s, w), BF16), SDS((nb * s, LANES), F32),
                   SDS((nb * n, 16, tf), F32)],
        scratch_shapes=[pltpu.VMEM((s, w), BF16), pltpu.VMEM((s, w), BF16),
                        pltpu.VMEM((FOX_HEADS, tf, PAIR), BF16), pltpu.VMEM((FOX_HEADS, tf, PAIR), BF16),
                        pltpu.VMEM((N_PAIRS, PAIR, tf), BF16), pltpu.VMEM((FOX_HEADS, tf, tf), F32),
                        pltpu.VMEM((FOX_HEADS, tf, PAIR), F32), pltpu.VMEM((FOX_HEADS, tf, PAIR), F32),
                        pltpu.VMEM((FOX_HEADS, tf, LANES), F32), pltpu.VMEM((n, N_PAIRS, PAIR, tf), F32),
                        pltpu.VMEM((n, 16, tf), F32), pltpu.SemaphoreType.DMA((2,))],
        compiler_params=_params(("arbitrary", "arbitrary")))(pq, kv, kv, fneg, dcat_bf, lse_rows, delta_rows)


def reduce_adamw(parts, w, m, v, name):
    _, r, c = parts.shape
    tr = r
    for cand in range(16, r, 16):
        if r % cand == 0 and cand * c <= 128 * 1024:
            tr = cand
    c1 = 1.0 - ADAM_B1 ** ADAM_STEP
    c2 = 1.0 - ADAM_B2 ** ADAM_STEP

    def body(p_ref, w_ref, m_ref, v_ref, g_out, d_out, m_out, v_out):
        g = p_ref[0].astype(F32)
        for k in range(1, N_DEV):
            g = g + p_ref[k].astype(F32)
        mn = ADAM_B1 * m_ref[...] + (1.0 - ADAM_B1) * g
        vn = ADAM_B2 * v_ref[...] + (1.0 - ADAM_B2) * (g * g)
        g_out[...] = g
        m_out[...] = mn
        v_out[...] = vn
        d_out[...] = -ADAM_LR * ((mn / c1) / (jnp.sqrt(vn / c2) + ADAM_EPS) + ADAM_WD * w_ref[...])

    row = pl.BlockSpec((tr, c), lambda i: (i, 0))
    return pl.pallas_call(
        body, name=name, grid=(r // tr,),
        in_specs=[pl.BlockSpec((N_DEV, tr, c), lambda i: (0, i, 0)), row, row, row],
        out_specs=[row, row, row, row], out_shape=[SDS((r, c), F32)] * 4,
        compiler_params=_params(("parallel",)))(parts, w, m, v)


N_PEERS = N_DEV - 1
HBM_SPEC = pl.BlockSpec(memory_space=pltpu.HBM)
SEM_SPEC = pl.BlockSpec(memory_space=pltpu.SEMAPHORE)
ANY_SPEC = pl.BlockSpec(memory_space=pl.ANY)
SPLIT_EFFECT = pltpu.SideEffectType.DATAFLOW_SIDE_EFFECTING


def _my_index():
    return 4 * lax.axis_index("x") + 2 * lax.axis_index("y") + lax.axis_index("c")


def _peers():
    x, y, c = lax.axis_index("x"), lax.axis_index("y"), lax.axis_index("c")
    peers = []
    for k in range(1, N_DEV):
        px = 1 - x if (k >> 2) & 1 else x
        py = 1 - y if (k >> 1) & 1 else y
        pc = 1 - c if k & 1 else c
        peers.append(((px, py, pc), 4 * px + 2 * py + pc))
    return 4 * x + 2 * y + c, peers


def _push(src, dst, send_sems, recv_sems, slot, dev):
    return pltpu.make_async_remote_copy(src_ref=src, dst_ref=dst, send_sem=send_sems.at[slot], recv_sem=recv_sems.at[slot],
                                        device_id=dev, device_id_type=pl.DeviceIdType.MESH)


def _landing_shapes(arrs, scatter):
    return [SDS((N_DEV,) + tuple(a.shape[1:] if sc else a.shape), a.dtype) for a, sc in zip(arrs, scatter)]


def exchange(arrs, scatter, name):
    na = len(arrs)

    def body(*refs):
        ins = refs[:na]
        outs = refs[na:2 * na]
        send_sems, recv_sems, local_sems = refs[2 * na:]
        me, peers = _peers()
        local = []
        remote = []
        for a in range(na):
            lc = pltpu.make_async_copy(ins[a].at[me] if scatter[a] else ins[a], outs[a].at[me], local_sems.at[a])
            lc.start()
            local.append(lc)
            for k, (dev, idx) in enumerate(peers):
                cp = _push(ins[a].at[idx] if scatter[a] else ins[a], outs[a].at[me], send_sems, recv_sems,
                           a * N_PEERS + k, dev)
                cp.start()
                remote.append(cp)
        for a in range(na):
            for k, (dev, idx) in enumerate(peers):
                _push(ins[a].at[me] if scatter[a] else ins[a], outs[a].at[idx], send_sems, recv_sems,
                      a * N_PEERS + k, dev).wait_recv()
        for cp in remote:
            cp.wait_send()
        for lc in local:
            lc.wait()

    return pl.pallas_call(
        body, name=name, in_specs=[HBM_SPEC] * na, out_specs=[HBM_SPEC] * na, out_shape=_landing_shapes(arrs, scatter),
        scratch_shapes=[pltpu.SemaphoreType.DMA((na * N_PEERS,)), pltpu.SemaphoreType.DMA((na * N_PEERS,)),
                        pltpu.SemaphoreType.DMA((na,))])(*arrs)


def exchange_start(arrs, scatter, after, name):
    na = len(arrs)
    lands = [lax.empty(l.shape, l.dtype) for l in _landing_shapes(arrs, scatter)]

    def body(*refs):
        ins = refs[:na]
        land = refs[na:2 * na]
        send_sems, recv_sems = refs[2 * na + 1], refs[2 * na + 2]
        token = refs[-1]
        me, peers = _peers()
        for a in range(na):
            for k, (dev, idx) in enumerate(peers):
                _push(ins[a].at[idx] if scatter[a] else ins[a], land[a].at[me], send_sems, recv_sems,
                      a * N_PEERS + k, dev).start()
        token[...] = jnp.zeros_like(token)

    thru = [pltpu.HBM(a.shape, a.dtype) for a in arrs] + [pltpu.HBM(l.shape, l.dtype) for l in lands]
    res = pl.pallas_call(
        body, name=name,
        out_shape=(pltpu.SemaphoreType.DMA((na * N_PEERS,)), pltpu.SemaphoreType.DMA((na * N_PEERS,)), *thru,
                   SDS((8, LANES), F32)),
        in_specs=[HBM_SPEC] * (2 * na) + [ANY_SPEC],
        out_specs=(SEM_SPEC, SEM_SPEC, *([HBM_SPEC] * (2 * na)), pl.BlockSpec(memory_space=pltpu.VMEM)),
        input_output_aliases={i: 2 + i for i in range(2 * na)},
        compiler_params=pltpu.CompilerParams(has_side_effects=SPLIT_EFFECT),
    )(*[pltpu.with_memory_space_constraint(a, pltpu.HBM) for a in arrs],
      *[pltpu.with_memory_space_constraint(l, pltpu.HBM) for l in lands], after)
    return {"send": res[0], "recv": res[1], "src": res[2:2 + na], "land": res[2 + na:2 + 2 * na],
            "token": res[-1][0, 0], "scatter": scatter}


def exchange_wait(handle, after, name):
    scatter = handle["scatter"]
    na = len(scatter)

    def body(*refs):
        src = refs[:na]
        land = refs[na:2 * na]
        send_sems, recv_sems = refs[2 * na], refs[2 * na + 1]
        me, peers = _peers()
        for a in range(na):
            for k, (dev, idx) in enumerate(peers):
                cp = _push(src[a].at[me] if scatter[a] else src[a], land[a].at[idx], send_sems, recv_sems,
                           a * N_PEERS + k, dev)
                cp.wait_send()
                cp.wait_recv()

    ops = list(handle["src"]) + list(handle["land"])
    res = pl.pallas_call(
        body, name=name, out_shape=tuple(pltpu.HBM(o.shape, o.dtype) for o in ops),
        in_specs=[HBM_SPEC] * (2 * na) + [SEM_SPEC, SEM_SPEC, ANY_SPEC], out_specs=tuple([HBM_SPEC] * (2 * na)),
        input_output_aliases={i: i for i in range(2 * na)},
        compiler_params=pltpu.CompilerParams(has_side_effects=SPLIT_EFFECT),
    )(*ops, handle["send"], handle["recv"], after)
    me = _my_index()
    out = []
    for a in range(na):
        own = lax.dynamic_index_in_dim(res[a], me, 0, keepdims=True) if scatter[a] else res[a][None]
        out.append(lax.dynamic_update_slice(res[na + a], own, (me,) + (0,) * (own.ndim - 1)))
    return out


def forward_layer(l, xin, xin_bf, mem_bf, wt, nb, s, ffn_weights=None):
    sv = {"xin_bf": xin_bf}
    memkv = mm_nn(mem_bf, wt["memw"], BF16, f"memkv{l}")
    sv["memkv"] = memkv
    if l == 0:
        proj = mm_nn(xin_bf, wt["win_a"], F32, "proj_a")
        pooled, tok = pool_fwd(proj, wt["pw_bd"], wt["pscale"], nb, s)
        sv["pooled"] = pooled
    else:
        kv = mm_nn(xin_bf, wt["kvw"][:, :2 * TOK_WIDTH], BF16, "kv_proj")
        fl = mm_nn(xin_bf, wt["kvw"][:, 2 * TOK_WIDTH:], F32, "gate_proj")
        fneg = -fgate_fwd(fl, wt["fb"], nb, s)
        proj = mm_nn(xin_bf, wt["wq"], BF16, "proj_b")
        kaug, vaug_t = fox_prep(kv, fneg, nb, s)
        tok, o_f32, lse_rows = fox_fwd_t(proj, kaug, vaug_t, nb, s)
        sv.update(kv=kv, fl=fl, fneg=fneg, o_f32=o_f32, lse_rows=lse_rows)
    sv["proj"] = proj
    mem_out = memattn_fwd(proj, memkv, nb, s, f"memattn_fwd{l}")
    cat = jnp.concatenate([tok, mem_out], axis=1)
    sv["cat"] = cat
    x1, x1_bf, xh1, rs1 = ln_fwd(xin, cat, wt["wout"], wt["ln1_g"], wt["ln1_b"], f"out_proj_ln1_{l}")
    sv.update(x1_bf=x1_bf, xh1=xh1, rs1=rs1)
    if ffn_weights is not None:
        wt.update(ffn_weights(x1_bf))
    act, ga, gb, hu, hg = ffn_up_gate(x1_bf, wt["wup"], wt["cw"], nb, s, f"ffn_up_gate{l}")
    sv.update(act=act, ga=ga, gb=gb, hu=hu, hg=hg)
    x2, x2_bf, xh2, rs2 = ln_fwd(x1, act, wt["wdown"], wt["ln2_g"], wt["ln2_b"], f"ffn_down_ln2_{l}")
    sv.update(xh2=xh2, rs2=rs2)
    return x2, x2_bf, sv


def backward_layer(l, dy, sv, mem_bf, wt, nb, s, after_ffn=None, after_pool=None, dy_product=None):
    g = {}
    dr2, dr2_bf, g["ln2_g"], g["ln2_b"] = ln_bwd(dy, sv["xh2"], sv["rs2"], wt["ln2_g"], f"ln2_bwd{l}", product=dy_product)
    dact = mm_nn(dr2_bf, wt["wdown"], BF16, f"ffn_down_dx{l}", trans_b=0)
    g["wdown"] = mm_tn(sv["act"], dr2_bf, f"ffn_down_dw{l}")
    dh_u, dh_g, dcw_u, dcw_g = gate_conv_bwd(dact, sv["ga"], sv["gb"], sv["hu"], sv["hg"], wt["cw"], nb, s,
                                             f"gate_conv_bwd{l}")
    g["cw"] = jnp.concatenate([dcw_u, dcw_g], axis=0)
    dx1 = mm_nn(dh_u, wt["wup"], F32, f"ffn_up_dx_u{l}", addend=dr2, add_scale=DN_ALPHA, trans_b=0)
    g["wup"] = jnp.concatenate([mm_tn(sv["x1_bf"], dh_u, f"ffn_up_dw_u{l}", blocked=True),
                                mm_tn(sv["x1_bf"], dh_g, f"ffn_up_dw_g{l}", blocked=True)], axis=0)
    ln1_g = wt["ln1_g"] if after_ffn is None else wt["ln1_g"] + after_ffn(g, dx1)
    dr1, dr1_bf, g["ln1_g"], g["ln1_b"] = ln_bwd(dx1, sv["xh1"], sv["rs1"], ln1_g, f"ffn_up_dx_g_ln1_bwd{l}",
                                                 product=(dh_g, wt["wup"], 1))
    dcat, dcat_bf = mm_nn(dr1_bf, wt["wout"], F32, f"out_proj_dx{l}", also_bf16=True, trans_b=0)
    g["wout"] = mm_tn(sv["cat"], dr1_bf, f"out_proj_dw{l}")
    dqm, dmemkv = memattn_bwd(sv["proj"], sv["memkv"], dcat, nb, s, f"memattn_bwd{l}")
    g["memw"] = mm_tn(mem_bf, dmemkv, f"memkv_dw{l}")
    if l == 0:
        dmixed, dpooled, g["pscale"] = pool_bwd_mix(dcat, sv["pooled"], wt["pw_bd"], wt["pscale"], nb, s)
        g["pw_full"] = mm_tn(sv["pooled"], dmixed, "pool_dw")
        win_a = wt["win_a"] if after_pool is None else wt["win_a"] + after_pool(g, dmixed).astype(BF16)
        du = pool_bwd_window(dpooled, nb, s)
        dproj = jnp.concatenate([du, dqm], axis=1)
        dx = mm_nn(dproj, win_a, F32, "proj_a_dx", addend=dr1, add_scale=DN_ALPHA, trans_b=0)
        g["win_a"] = mm_tn(sv["xin_bf"], dproj, "proj_a_dw")
        pending = None
    else:
        delta = fox_delta(dcat, sv["o_f32"], nb, s)
        tf = min(TF, s)
        dq, dk, dv, dfcum_k, dfq_rows = fox_bwd(sv["proj"], sv["kv"], sv["fneg"], dcat_bf,
                                                sv["lse_rows"], _to_tile_rows(delta, nb, s, tf), nb, s)
        dfl, g["fb"] = fgate_bwd(_from_tile_rows(dfq_rows), dfcum_k, sv["fl"], wt["fb"], nb, s)
        dproj = jnp.concatenate([dq, dqm], axis=1)
        dkvf = jnp.concatenate([dk, dv, dfl.astype(BF16)], axis=1)
        dx = mm_nn(dproj, wt["wq"], F32, "proj_b_dx", addend=dr1, add_scale=DN_ALPHA, trans_b=0)
        pending = (dkvf, wt["kvw"], 0)
        g["wq"] = mm_tn(sv["xin_bf"], dproj, "proj_b_dw")
        g["kvw"] = mm_tn(sv["xin_bf"], dkvf, "kv_proj_dw")
    return dx, pending, g


def pack_replicated(pool_w, ln1_g, ln1_b, ln2_g, ln2_b, conv_b, f_b):
    cb = jnp.pad(conv_b, ((0, 0), (0, 6144 - 5504))).reshape(12, D_MODEL)
    fb = jnp.pad(f_b.reshape(1, FOX_HEADS), ((0, 3), (0, D_MODEL - FOX_HEADS)))
    return jnp.concatenate([pool_w.reshape(144, D_MODEL), ln1_g, ln1_b, ln2_g, ln2_b, cb, fb], axis=0)


def unpack_replicated(buf):
    pool_w = buf[:144].reshape(1, 4, POOL_GROUP, POOL_GROUP)
    ln = [buf[144 + 2 * k:146 + 2 * k] for k in range(4)]
    conv_b = buf[152:164].reshape(2, 6144)[:, :5504]
    f_b = buf[164, :FOX_HEADS]
    return pool_w, ln[0], ln[1], ln[2], ln[3], conv_b, f_b


def pack_small(conv_w, pool_scale):
    buf = jnp.zeros((16, FF_BLOCK_PAD), F32)
    buf = lax.dynamic_update_slice(buf, conv_w.reshape(DEPTH * 3, FF_BLOCK), (0, 0))
    return lax.dynamic_update_slice(buf, pool_scale, (8, 0))


def _block_diag(pw):
    out = jnp.zeros((TOK_WIDTH, TOK_WIDTH), pw.dtype)
    for g in range(4):
        out = lax.dynamic_update_slice(out, pw[g], (g * POOL_GROUP, g * POOL_GROUP))
    return out


def layer_shards(l, sq_a, sq_b, mem_w_kv, ffn_w_up, ffn_w_down):
    return [sq_a[0].astype(BF16), sq_b[0].astype(BF16), mem_w_kv[l].astype(BF16), ffn_w_up[l].astype(BF16),
            ffn_w_down[l].astype(BF16)]


def mixer_weights(l, gath, ln1_g, ln1_b, ln2_g, ln2_b):
    w_out = gath[1].reshape(D_MODEL, D_MODEL)
    wt = {"memw": gath[2].reshape(D_MODEL, 2 * MEM_WIDTH), "wout": w_out,
          "ln1_g": ln1_g[l:l + 1], "ln1_b": ln1_b[l:l + 1], "ln2_g": ln2_g[l:l + 1], "ln2_b": ln2_b[l:l + 1]}
    return wt, gath[0].reshape(D_MODEL, D_MODEL)


def ffn_weights(l, wup_g, wdown_g, small, conv_b):
    pad_c = FF_BLOCK_PAD - FF_BLOCK
    wup = jnp.pad(wup_g, ((0, 0), (0, 0), (0, pad_c))).transpose(1, 0, 2).reshape(D_MODEL, N_DEV * FF_BLOCK_PAD)
    wdown = jnp.pad(wdown_g.reshape(FF_PAIRS, FF_BLOCK, D_MODEL), ((0, 0), (0, pad_c), (0, 0)))
    wdown = wdown.reshape(FF_PAIRS * FF_BLOCK_PAD, D_MODEL)
    cb = jnp.pad(conv_b[l].reshape(N_DEV, FF_BLOCK), ((0, 0), (0, pad_c)))
    cw = jnp.concatenate([small[:, 3 * l:3 * l + 3, :], cb[:, None, :], jnp.zeros((N_DEV, 4, FF_BLOCK_PAD), F32)], axis=1)
    return {"wup": wup, "wdown": wdown, "cw": cw}


def mixer_grad_blocks(g, w_in_grad):
    blocks = [] if w_in_grad is None else [w_in_grad.reshape(N_DEV, 128, D_MODEL)]
    blocks += [g["wout"].reshape(N_DEV, 128, D_MODEL), g["memw"].reshape(N_DEV, 128, 2 * MEM_WIDTH)]
    return [b.astype(BF16) for b in blocks]


def ffn_grad_blocks(g):
    wup = g["wup"][:, :, :FF_BLOCK]
    wdown = g["wdown"].reshape(FF_PAIRS, FF_BLOCK_PAD, D_MODEL)[:, :FF_BLOCK].reshape(N_DEV, FF_ROWS, D_MODEL)
    return [wup.astype(BF16), wdown.astype(BF16)]


def small_grad_blocks(g0, g1):
    taps = jnp.stack([g0["cw"][:, :3, :], g1["cw"][:, :3, :]], axis=1).reshape(N_DEV, DEPTH * 3, FF_BLOCK_PAD)
    small = jnp.zeros((N_DEV, 16, FF_BLOCK_PAD), F32)
    small = lax.dynamic_update_slice(small, taps, (0, 0, 0))
    return lax.dynamic_update_slice(small, g0["pscale"].reshape(N_DEV, 1, 96), (0, 8, 0))


def replicated_grads(g0, g1):
    pw = jnp.stack([g0["pw_full"][k * POOL_GROUP:(k + 1) * POOL_GROUP, k * POOL_GROUP:(k + 1) * POOL_GROUP] for k in range(4)])
    conv_b = jnp.stack([g_["cw"][:, 3, :FF_BLOCK].reshape(N_DEV * FF_BLOCK) for g_ in (g0, g1)])
    ln = [jnp.concatenate([g0[n], g1[n]], axis=0) for n in ("ln1_g", "ln1_b", "ln2_g", "ln2_b")]
    return pack_replicated(pw[None], ln[0], ln[1], ln[2], ln[3], conv_b, g1["fb"][0, :FOX_HEADS])


def kernel(x, mem, a_w_in, a_pool_w, a_pool_scale, a_w_out, b_w_q, b_w_out, kv_w, f_b, mem_w_kv, ln1_g, ln1_b, ln2_g, ln2_b, ffn_w_up, ffn_conv_w, ffn_conv_b, ffn_w_down, loss_target, m_a_w_in, m_a_pool_w, m_a_pool_scale, m_a_w_out, m_b_w_q, m_b_w_out, m_kv_w, m_f_b, m_mem_w_kv, m_ln1_g, m_ln1_b, m_ln2_g, m_ln2_b, m_ffn_w_up, m_ffn_conv_w, m_ffn_conv_b, m_ffn_w_down, v_a_w_in, v_a_pool_w, v_a_pool_scale, v_a_w_out, v_b_w_q, v_b_w_out, v_kv_w, v_f_b, v_mem_w_kv, v_ln1_g, v_ln1_b, v_ln2_g, v_ln2_b, v_ffn_w_up, v_ffn_conv_w, v_ffn_conv_b, v_ffn_w_down):
    nb, s, d = x.shape
    t = nb * s
    x2d, mem_bf, target = x.reshape(t, d), mem.reshape(nb * MEM_LEN, d).astype(BF16), loss_target.reshape(t, d)

    shards0 = layer_shards(0, a_w_in, a_w_out, mem_w_kv, ffn_w_up, ffn_w_down)
    shards1 = layer_shards(1, b_w_q, b_w_out, mem_w_kv, ffn_w_up, ffn_w_down)
    shards1.append(jnp.pad(kv_w, ((0, 0), (0, KV_COLS_PAD - KV_COLS))).astype(BF16))
    gath0 = exchange(shards0[:3] + [pack_small(ffn_conv_w, a_pool_scale)], [False] * 4, "gather_w0_mixer")
    pending = {"ffn0": exchange_start(shards0[3:], [False] * 2, gath0[0], "gather_w0_ffn_start")}
    small = gath0[3]
    wt0, w_in = mixer_weights(0, gath0, ln1_g + pending["ffn0"]["token"], ln1_b, ln2_g, ln2_b)
    pw_bd = _block_diag(a_pool_w[0])
    wt0.update(win_a=w_in, pw_bd=pw_bd.astype(BF16),
               pscale=small[:, 8, :96].reshape(1, TOK_WIDTH) + pending["ffn0"]["token"])

    def ffn0_weights(x1_bf):
        got = exchange_wait(pending["ffn0"], x1_bf, "gather_w0_ffn_wait")
        pending["w1"] = exchange_start(shards1, [False] * 6, got[0], "gather_w1_start")
        w = ffn_weights(0, got[0], got[1], small, ffn_conv_b)
        w["cw"] = w["cw"] + pending["w1"]["token"]
        return w

    x1, x1_bf, sv0 = forward_layer(0, x2d, x2d.astype(BF16), mem_bf, wt0, nb, s, ffn_weights=ffn0_weights)
    gath1 = exchange_wait(pending["w1"], x1_bf, "gather_w1_wait")
    wt1, w_q = mixer_weights(1, gath1, ln1_g, ln1_b, ln2_g, ln2_b)
    wt1.update(ffn_weights(1, gath1[3], gath1[4], small, ffn_conv_b))
    kvw = gath1[5].reshape(D_MODEL, KV_COLS_PAD)
    wt1.update(wq=w_q, kvw=kvw,
               fb=jnp.pad(f_b.reshape(1, FOX_HEADS), ((0, 0), (0, LANES - FOX_HEADS))))
    y, _, sv1 = forward_layer(1, x1, x1_bf, mem_bf, wt1, nb, s)
    dy, loss_row = loss_head(y, target)
    loss = lax.psum(loss_row[0, 0], ("x", "y", "c"))

    dx1, dx1_rest, g1 = backward_layer(1, dy, sv1, mem_bf, wt1, nb, s)
    blocks1 = (mixer_grad_blocks(g1, g1["wq"]) + ffn_grad_blocks(g1)
               + [g1["kvw"][:, :KV_COLS].reshape(N_DEV, 128, KV_COLS).astype(BF16)])
    pending["g1"] = exchange_start(blocks1, [True] * 6, dx1, "scatter_g1_start")
    wt0["ln2_g"] = wt0["ln2_g"] + pending["g1"]["token"]

    def after_ffn0(g, dxm):
        pending["gf0"] = exchange_start(ffn_grad_blocks(g), [True] * 2, dxm, "scatter_g0_ffn_start")
        return pending["gf0"]["token"]

    def after_pool0(g, x):
        blocks = mixer_grad_blocks(g, None) + [small_grad_blocks(g, g1), replicated_grads(g, g1)]
        pending["gm0"] = exchange_start(blocks, [True] * 3 + [False], x, "scatter_g0_mixer_start")
        return pending["gm0"]["token"]

    grad_x, _, g0 = backward_layer(0, dx1, sv0, mem_bf, wt0, nb, s, after_ffn=after_ffn0, after_pool=after_pool0,
                                   dy_product=dx1_rest)
    pending["gin"] = exchange_start([g0["win_a"].reshape(N_DEV, 128, D_MODEL).astype(BF16)], [True], grad_x,
                                    "scatter_g0_in_start")
    parts_f0 = exchange_wait(pending["gf0"], jnp.zeros((8, LANES), F32) + pending["gin"]["token"], "scatter_g0_ffn_wait")
    parts1 = exchange_wait(pending["g1"], parts_f0[0], "scatter_g1_wait")

    res = {}

    def upd(nm, parts, w2, m2, v2):
        res[nm] = reduce_adamw(parts, w2, m2, v2, f"adamw_{nm}")

    upd("b_w_q", parts1[0], b_w_q[0], m_b_w_q[0], v_b_w_q[0])
    upd("b_w_out", parts1[1], b_w_out[0], m_b_w_out[0], v_b_w_out[0])
    upd("kv_w", parts1[5], kv_w, m_kv_w, v_kv_w)
    upd("mem_w_kv1", parts1[2], mem_w_kv[1], m_mem_w_kv[1], v_mem_w_kv[1])
    for l, parts in enumerate((parts_f0, parts1[3:5])):
        upd(f"ffn_w_up{l}", parts[0], ffn_w_up[l], m_ffn_w_up[l], v_ffn_w_up[l])
        upd(f"ffn_w_down{l}", parts[1], ffn_w_down[l], m_ffn_w_down[l], v_ffn_w_down[l])
    parts_m0 = exchange_wait(pending["gm0"], res["ffn_w_down1"][0], "scatter_g0_mixer_wait")
    parts_in = exchange_wait(pending["gin"], parts_m0[0], "scatter_g0_in_wait")
    upd("a_w_in", parts_in[0], a_w_in[0], m_a_w_in[0], v_a_w_in[0])
    upd("a_w_out", parts_m0[0], a_w_out[0], m_a_w_out[0], v_a_w_out[0])
    upd("mem_w_kv0", parts_m0[1], mem_w_kv[0], m_mem_w_kv[0], v_mem_w_kv[0])
    upd("small", parts_m0[2], pack_small(ffn_conv_w, a_pool_scale), pack_small(m_ffn_conv_w, m_a_pool_scale),
        pack_small(v_ffn_conv_w, v_a_pool_scale))
    upd("replicated", parts_m0[3], pack_replicated(a_pool_w, ln1_g, ln1_b, ln2_g, ln2_b, ffn_conv_b, f_b),
        pack_replicated(m_a_pool_w, m_ln1_g, m_ln1_b, m_ln2_g, m_ln2_b, m_ffn_conv_b, m_f_b),
        pack_replicated(v_a_pool_w, v_ln1_g, v_ln1_b, v_ln2_g, v_ln2_b, v_ffn_conv_b, v_f_b))

    for nm in ("a_w_in", "a_w_out", "b_w_q", "b_w_out"):
        res[nm] = [o[None] for o in res[nm]]
    for nm in ("mem_w_kv", "ffn_w_up", "ffn_w_down"):
        res[nm] = [jnp.stack([a0, a1]) for a0, a1 in zip(res[nm + "0"], res[nm + "1"])]
    res["ffn_conv_w"] = [o[:DEPTH * 3, :FF_BLOCK].reshape(DEPTH, 3, FF_BLOCK) for o in res["small"]]
    res["a_pool_scale"] = [o[8:9, :96] for o in res["small"]]
    rep_names = ["a_pool_w", "ln1_g", "ln1_b", "ln2_g", "ln2_b", "ffn_conv_b", "f_b"]
    for nm in rep_names:
        res[nm] = []
    for o in res["replicated"]:
        for nm, val in zip(rep_names, unpack_replicated(o)):
            res[nm].append(val)

    order = ["a_w_in", "a_pool_w", "a_pool_scale", "a_w_out", "b_w_q", "b_w_out", "kv_w", "f_b", "mem_w_kv",
             "ln1_g", "ln1_b", "ln2_g", "ln2_b", "ffn_w_up", "ffn_conv_w", "ffn_conv_b", "ffn_w_down"]
    out = [loss, grad_x.reshape(nb, s, d)]
    for kind in range(4):
        out.extend(res[nm][kind] for nm in order)
    return tuple(out)
```

```python
import jax
import jax.numpy as jnp
from jax import lax
from jax.experimental import pallas as pl
from jax.experimental.pallas import tpu as pltpu

F32 = jnp.float32
BF16 = jnp.bfloat16
SDS = jax.ShapeDtypeStruct

N_DEV = 8
D_MODEL = 1024
TOK_WIDTH = 768
MEM_WIDTH = 256
MEM_LEN = 256
MEM_HEADS = 4
HEAD_DIM = 64
FOX_HEADS = 12
POOL_GROUP = 192
FF_BLOCK = 688
FF_BLOCK_PAD = 768
FF_PAIRS = 4
FF_ROWS = 344
KV_COLS = 1548
KV_COLS_PAD = 1664
LANES = 128
DEPTH = 2
DN_ALPHA = (2.0 * DEPTH) ** 0.25
LN_EPS = 1e-5
QK_SCALE = HEAD_DIM ** -0.5
NEG_BIG = -1e30

ADAM_LR = 0.001
ADAM_B1 = 0.9
ADAM_B2 = 0.999
ADAM_EPS = 1e-08
ADAM_WD = 0.01
ADAM_STEP = 10

VMEM_LIMIT_BYTES = 56 * 1024 * 1024
MM_BLOCK_BYTES = 6 * 1024 * 1024
TM = 512
TS = 256
TF = 256
TC = 256
HALO_POOL = 16
HALO_CONV = 8

NT_DIMS = (((1,), (1,)), ((), ()))
TN_DIMS = (((0,), (0,)), ((), ()))


def _params(sem=None):
    return pltpu.CompilerParams(dimension_semantics=sem, vmem_limit_bytes=VMEM_LIMIT_BYTES)


def _sigmoid(z):
    return 1.0 / (1.0 + jnp.exp(-z))


def _pick_tn(n):
    if n <= 2048:
        return n
    for t in (1024, 768, 512, 256, 128):
        if n % t == 0:
            return t
    return n


def mm_nn(a, b, out_dtype, name, addend=None, add_scale=1.0, also_bf16=False, trans_b=None):
    m, k = a.shape
    n = b.shape[1] if trans_b is None else b.shape[0]
    tm = min(TM, m)
    tn = n
    while k * tn * 2 > MM_BLOCK_BYTES or tm * tn * 4 > MM_BLOCK_BYTES:
        tn //= 2
    chunk = tn if tn <= 2048 else _pick_tn(tn)
    has_add = addend is not None

    def body(*refs):
        a_ref, b_ref = refs[0], refs[1]
        c_ref = refs[2] if has_add else None
        o_ref = refs[3] if has_add else refs[2]
        ob_ref = refs[-1] if also_bf16 else None
        av = a_ref[...].astype(BF16)
        for c in range(tn // chunk):
            cols = slice(c * chunk, (c + 1) * chunk)
            if trans_b is None:
                r = jnp.dot(av, b_ref[:, cols].astype(BF16), preferred_element_type=F32)
            else:
                r = lax.dot_general(av, b_ref[cols, :].astype(BF16), NT_DIMS, preferred_element_type=F32)
            if has_add:
                r = r + add_scale * c_ref[:, cols]
            o_ref[:, cols] = r.astype(out_dtype)
            if also_bf16:
                ob_ref[:, cols] = r.astype(BF16)

    b_spec = (pl.BlockSpec((k, tn), lambda j, i: (0, j)) if trans_b is None
              else pl.BlockSpec((tn, k), lambda j, i: (j, trans_b)))
    in_specs = [pl.BlockSpec((tm, k), lambda j, i: (i, 0)), b_spec]
    ops = [a, b]
    tile = pl.BlockSpec((tm, tn), lambda j, i: (i, j))
    if has_add:
        in_specs.append(tile)
        ops.append(addend)
    out_shape = [SDS((m, n), out_dtype)]
    out_specs = [tile]
    if also_bf16:
        out_shape.append(SDS((m, n), BF16))
        out_specs.append(tile)
    res = pl.pallas_call(
        body, name=name, grid=(n // tn, m // tm), in_specs=in_specs, out_specs=out_specs, out_shape=out_shape,
        compiler_params=_params(("parallel", "parallel")))(*ops)
    return tuple(res) if also_bf16 else res[0]


def mm_tn(a, b, name, blocked=False):
    t, m = a.shape
    _, n = b.shape
    tt = min(4 * TM, t)
    tm = 1024 if m % 1024 == 0 else m
    tn = FF_BLOCK_PAD if blocked else _pick_tn(n)
    nt = t // tt

    def body(a_ref, b_ref, o_ref):
        kk = pl.program_id(2)
        r = lax.dot_general(a_ref[...].astype(BF16), b_ref[...].astype(BF16), TN_DIMS, preferred_element_type=F32)
        if blocked:
            r = r[None]

        @pl.when(kk == 0)
        def _():
            o_ref[...] = r

        @pl.when(kk != 0)
        def _():
            o_ref[...] += r

    if blocked:
        out_shape = SDS((n // tn, m, tn), F32)
        out_spec = pl.BlockSpec((1, tm, tn), lambda i, j, kk: (j, i, 0))
    else:
        out_shape = SDS((m, n), F32)
        out_spec = pl.BlockSpec((tm, tn), lambda i, j, kk: (i, j))
    return pl.pallas_call(
        body, name=name, grid=(m // tm, n // tn, nt),
        in_specs=[pl.BlockSpec((tt, tm), lambda i, j, kk: (kk, i)), pl.BlockSpec((tt, tn), lambda i, j, kk: (kk, j))],
        out_specs=out_spec, out_shape=out_shape,
        compiler_params=_params(("parallel", "parallel", "arbitrary")))(a, b)


def ln_fwd(xprev, a, w, g, b, name):
    t, d = xprev.shape
    k = a.shape[1]
    tm = min(TM, t)

    def body(xp_ref, a_ref, w_ref, g_ref, b_ref, y_ref, yb_ref, xh_ref, rs_ref):
        r = DN_ALPHA * xp_ref[...] + jnp.dot(a_ref[...], w_ref[...], preferred_element_type=F32)
        mu = jnp.mean(r, axis=1, keepdims=True)
        xc = r - mu
        var = jnp.mean(xc * xc, axis=1, keepdims=True)
        rstd = lax.rsqrt(var + LN_EPS)
        xh = xc * rstd
        y = xh * g_ref[...] + b_ref[...]
        y_ref[...] = y
        yb_ref[...] = y.astype(BF16)
        xh_ref[...] = xh
        rs_ref[...] = jnp.broadcast_to(rstd, (tm, LANES))

    row = pl.BlockSpec((tm, d), lambda i: (i, 0))
    vec = pl.BlockSpec((1, d), lambda i: (0, 0))
    return pl.pallas_call(
        body, name=name, grid=(t // tm,),
        in_specs=[row, pl.BlockSpec((tm, k), lambda i: (i, 0)), pl.BlockSpec((k, d), lambda i: (0, 0)), vec, vec],
        out_specs=[row, row, row, pl.BlockSpec((tm, LANES), lambda i: (i, 0))],
        out_shape=[SDS((t, d), F32), SDS((t, d), BF16), SDS((t, d), F32), SDS((t, LANES), F32)],
        compiler_params=_params(("parallel",)))(xprev, a, w, g, b)


def ln_bwd(dy, xhat, rstd, g, name, product=None, dy_scale=1.0):
    t, d = dy.shape
    tm = min(TM, t)
    fused = product is not None

    def body(*refs):
        if fused:
            a_ref, w_ref = refs[0], refs[1]
            refs = refs[2:]
        dy_ref, xh_ref, rs_ref, g_ref, dr_ref, drb_ref, dg_ref, db_ref = refs
        i = pl.program_id(0)
        dyv = dy_ref[...] if dy_scale == 1.0 else dy_scale * dy_ref[...]
        if fused:
            dyv = dyv + lax.dot_general(a_ref[...], w_ref[...], NT_DIMS, preferred_element_type=F32)
        xh = xh_ref[...]
        dxh = dyv * g_ref[...]
        m1 = jnp.mean(dxh, axis=1, keepdims=True)
        m2 = jnp.mean(dxh * xh, axis=1, keepdims=True)
        dr = rs_ref[:, 0:1] * (dxh - m1 - xh * m2)
        dr_ref[...] = dr
        drb_ref[...] = dr.astype(BF16)

        @pl.when(i == 0)
        def _():
            dg_ref[...] = jnp.zeros_like(dg_ref)
            db_ref[...] = jnp.zeros_like(db_ref)

        dg_ref[...] += jnp.sum(dyv * xh, axis=0, keepdims=True)
        db_ref[...] += jnp.sum(dyv, axis=0, keepdims=True)

    row = pl.BlockSpec((tm, d), lambda i: (i, 0))
    vec = pl.BlockSpec((1, d), lambda i: (0, 0))
    in_specs = [row, row, pl.BlockSpec((tm, LANES), lambda i: (i, 0)), vec]
    ops = [dy, xhat, rstd, g]
    if fused:
        k = product[0].shape[1]
        col = product[2]
        in_specs = [pl.BlockSpec((tm, k), lambda i: (i, 0)), pl.BlockSpec((d, k), lambda i: (0, col))] + in_specs
        ops = list(product[:2]) + ops
    return pl.pallas_call(
        body, name=name, grid=(t // tm,), in_specs=in_specs, out_specs=[row, row, vec, vec],
        out_shape=[SDS((t, d), F32), SDS((t, d), BF16), SDS((1, d), F32), SDS((1, d), F32)],
        compiler_params=_params(("arbitrary",)))(*ops)


def loss_head(y, target):
    t, d = y.shape
    tm = min(TM, t)
    nsteps = t // tm

    def body(y_ref, t_ref, dy_ref, l_ref, acc):
        i = pl.program_id(0)
        diff = y_ref[...] - t_ref[...]
        dy_ref[...] = diff * (1.0 / d)

        @pl.when(i == 0)
        def _():
            acc[...] = jnp.zeros_like(acc)

        acc[...] += jnp.sum(diff * diff, axis=0, keepdims=True)

        @pl.when(i == nsteps - 1)
        def _():
            tot = jnp.sum(acc[...], axis=1, keepdims=True) * (0.5 / d)
            l_ref[...] = jnp.broadcast_to(tot, (1, LANES))

    row = pl.BlockSpec((tm, d), lambda i: (i, 0))
    return pl.pallas_call(
        body, name="loss_head", grid=(nsteps,), in_specs=[row, row],
        out_specs=[row, pl.BlockSpec((1, LANES), lambda i: (0, 0))],
        out_shape=[SDS((t, d), F32), SDS((1, LANES), F32)],
        scratch_shapes=[pltpu.VMEM((1, d), F32)],
        compiler_params=_params(("arbitrary",)))(y, target)


def memattn_fwd(proj, memkv, nb, s, name):
    ts = min(TS, s)
    nq = s // ts

    def body(q_ref, kv_ref, o_ref):
        top = lax.broadcasted_iota(jnp.int32, (PAIR, ts), 0) < HEAD_DIM
        scores = []
        for p in range(MEM_HEADS // 2):
            qp = q_ref[:, p * PAIR:(p + 1) * PAIR].astype(BF16)
            ke, ko = _split_pair(kv_ref[:, p * PAIR:(p + 1) * PAIR], QK_SCALE)
            scores.append([lax.dot_general(km, qp, NT_DIMS, preferred_element_type=F32) for km in (ke, ko)])
        for p in range(MEM_HEADS // 2):
            vt = kv_ref[:, MEM_WIDTH + p * PAIR:MEM_WIDTH + (p + 1) * PAIR].astype(F32).T.astype(BF16)
            outs = []
            for sc in scores[p]:
                e = jnp.exp(sc - jnp.max(sc, axis=0, keepdims=True))
                pr = e / jnp.sum(e, axis=0, keepdims=True)
                outs.append(jnp.dot(vt, pr.astype(BF16), preferred_element_type=F32))
            o_ref[:, p * PAIR:(p + 1) * PAIR] = jnp.where(top, outs[0], outs[1]).T.astype(BF16)

    return pl.pallas_call(
        body, name=name, grid=(nb, nq),
        in_specs=[pl.BlockSpec((ts, MEM_WIDTH), lambda b, i: (b * nq + i, 3)),
                  pl.BlockSpec((MEM_LEN, 2 * MEM_WIDTH), lambda b, i: (b, 0))],
        out_specs=pl.BlockSpec((ts, MEM_WIDTH), lambda b, i: (b * nq + i, 0)),
        out_shape=SDS((nb * s, MEM_WIDTH), BF16),
        compiler_params=_params(("parallel", "parallel")))(proj, memkv)


def memattn_bwd(proj, memkv, dcat, nb, s, name):
    ts = min(TS, s)
    nq = s // ts

    def body(q_ref, kv_ref, do_ref, dq_ref, dkv_ref):
        i = pl.program_id(1)

        @pl.when(i == 0)
        def _():
            dkv_ref[...] = jnp.zeros_like(dkv_ref)

        lo = _half_masks(MEM_LEN)
        top = lax.broadcasted_iota(jnp.int32, (PAIR, ts), 0) < HEAD_DIM
        n_pairs = MEM_HEADS // 2
        qs, dos, kps, products = [], [], [], []
        for p in range(n_pairs):
            qp = q_ref[:, p * PAIR:(p + 1) * PAIR].astype(BF16)
            dop = do_ref[:, p * PAIR:(p + 1) * PAIR].astype(BF16)
            kp = kv_ref[:, p * PAIR:(p + 1) * PAIR] * QK_SCALE
            kms = _split_pair(kp)
            vms = _split_pair(kv_ref[:, MEM_WIDTH + p * PAIR:MEM_WIDTH + (p + 1) * PAIR])
            products.append([(lax.dot_general(km, qp, NT_DIMS, preferred_element_type=F32),
                              lax.dot_general(vm, dop, NT_DIMS, preferred_element_type=F32)) for km, vm in zip(kms, vms)])
            qs.append(qp)
            dos.append(dop)
            kps.append(kp)
        for p in range(n_pairs):
            kt = kps[p].astype(F32).T.astype(BF16)
            dks, dvs, dqs = [], [], []
            for sc, dp in products[p]:
                e = jnp.exp(sc - jnp.max(sc, axis=0, keepdims=True))
                pr = e / jnp.sum(e, axis=0, keepdims=True)
                dl = jnp.sum(pr * dp, axis=0, keepdims=True)
                ds = (pr * (dp - dl)).astype(BF16)
                dvs.append(jnp.dot(pr.astype(BF16), dos[p], preferred_element_type=F32))
                dks.append(jnp.dot(ds, qs[p], preferred_element_type=F32))
                dqs.append(jnp.dot(kt, ds, preferred_element_type=F32))
            dq_ref[:, p * PAIR:(p + 1) * PAIR] = jnp.where(top, dqs[0], dqs[1]).T.astype(BF16)
            dkv_ref[:, p * PAIR:(p + 1) * PAIR] += jnp.where(lo, dks[0], dks[1]) * QK_SCALE
            dkv_ref[:, MEM_WIDTH + p * PAIR:MEM_WIDTH + (p + 1) * PAIR] += jnp.where(lo, dvs[0], dvs[1])

    return pl.pallas_call(
        body, name=name, grid=(nb, nq),
        in_specs=[pl.BlockSpec((ts, MEM_WIDTH), lambda b, i: (b * nq + i, 3)),
                  pl.BlockSpec((MEM_LEN, 2 * MEM_WIDTH), lambda b, i: (b, 0)),
                  pl.BlockSpec((ts, MEM_WIDTH), lambda b, i: (b * nq + i, 3))],
        out_specs=[pl.BlockSpec((ts, MEM_WIDTH), lambda b, i: (b * nq + i, 0)),
                   pl.BlockSpec((MEM_LEN, 2 * MEM_WIDTH), lambda b, i: (b, 0))],
        out_shape=[SDS((nb * s, MEM_WIDTH), BF16), SDS((nb * MEM_LEN, 2 * MEM_WIDTH), F32)],
        compiler_params=_params(("parallel", "arbitrary")))(proj, memkv, dcat)


def _pool_select(shape, s2, s4, s8, s16):
    lane = lax.broadcasted_iota(jnp.int32, shape, 1)
    return jnp.where(lane < POOL_GROUP, s2, jnp.where(lane < 2 * POOL_GROUP, s4, jnp.where(lane < 3 * POOL_GROUP, s8, s16)))


def _pool_count(shape, first_pos):
    pos = first_pos + lax.broadcasted_iota(jnp.int32, shape, 0)
    win = _pool_select(shape, 2, 4, 8, 16)
    return jnp.minimum(pos + 1, win).astype(F32)


def pool_fwd(proj, pw_bd, pscale, nb, s):
    ts = min(TS, s)
    nq = s // ts
    w = TOK_WIDTH

    def body(c_ref, h_ref, w_ref, sc_ref, pooled_ref, tok_ref):
        i = pl.program_id(0) % nq
        cur = c_ref[...]
        halo = jnp.where(i == 0, 0.0, h_ref[...])
        xe = jnp.concatenate([halo, cur], axis=0)
        s2 = xe + pltpu.roll(xe, 1, axis=0)
        s4 = s2 + pltpu.roll(s2, 2, axis=0)
        s8 = s4 + pltpu.roll(s4, 4, axis=0)
        s16 = s8 + pltpu.roll(s8, 8, axis=0)
        hp = HALO_POOL
        ws = _pool_select((ts, w), s2[hp:], s4[hp:], s8[hp:], s16[hp:])
        pooled = (ws / _pool_count((ts, w), i * ts) - cur).astype(BF16)
        pooled_ref[...] = pooled
        mixed = jnp.dot(pooled, w_ref[...], preferred_element_type=F32)
        tok_ref[...] = (mixed * sc_ref[...]).astype(BF16)

    row = pl.BlockSpec((ts, w), lambda r: (r, 0))
    return pl.pallas_call(
        body, name="pool_fwd", grid=(nb * nq,),
        in_specs=[row, pl.BlockSpec((HALO_POOL, w), lambda r: (jnp.maximum(r * (ts // HALO_POOL) - 1, 0), 0)),
                  pl.BlockSpec((w, w), lambda r: (0, 0)), pl.BlockSpec((1, w), lambda r: (0, 0))],
        out_specs=[row, row], out_shape=[SDS((nb * s, w), BF16), SDS((nb * s, w), BF16)],
        compiler_params=_params(("parallel",)))(proj, proj, pw_bd, pscale)


def pool_bwd_mix(dcat, pooled, pw_bd, pscale, nb, s):
    ts = min(TS, s)
    w = TOK_WIDTH

    def body(dt_ref, p_ref, w_ref, sc_ref, dm_ref, dp_ref, ds_ref):
        r = pl.program_id(0)
        dtok = dt_ref[...]
        mixed = jnp.dot(p_ref[...], w_ref[...], preferred_element_type=F32)

        @pl.when(r == 0)
        def _():
            ds_ref[...] = jnp.zeros_like(ds_ref)

        ds_ref[...] += jnp.sum(dtok * mixed, axis=0, keepdims=True)
        dmx = (dtok * sc_ref[...]).astype(BF16)
        dm_ref[...] = dmx
        dp_ref[...] = lax.dot_general(dmx, w_ref[...], NT_DIMS, preferred_element_type=F32)

    row = pl.BlockSpec((ts, w), lambda r: (r, 0))
    mat = pl.BlockSpec((w, w), lambda r: (0, 0))
    vec = pl.BlockSpec((1, w), lambda r: (0, 0))
    return pl.pallas_call(
        body, name="pool_bwd_mix", grid=(nb * s // ts,), in_specs=[row, row, mat, vec],
        out_specs=[row, row, vec], out_shape=[SDS((nb * s, w), BF16), SDS((nb * s, w), F32), SDS((1, w), F32)],
        compiler_params=_params(("arbitrary",)))(dcat, pooled, pw_bd, pscale)


def pool_bwd_window(dpooled, nb, s):
    ts = min(TS, s)
    nq = s // ts
    w = TOK_WIDTH
    n_ext = ts + HALO_POOL
    n_halo_blocks = nb * s // HALO_POOL

    def body(c_ref, n_ref, du_ref):
        i = pl.program_id(0) % nq
        cur = c_ref[...]
        nxt = jnp.where(i == nq - 1, 0.0, n_ref[...])
        ze = jnp.concatenate([cur, nxt], axis=0) / _pool_count((n_ext, w), i * ts)
        s2 = ze + pltpu.roll(ze, n_ext - 1, axis=0)
        s4 = s2 + pltpu.roll(s2, n_ext - 2, axis=0)
        s8 = s4 + pltpu.roll(s4, n_ext - 4, axis=0)
        s16 = s8 + pltpu.roll(s8, n_ext - 8, axis=0)
        ws = _pool_select((ts, w), s2[:ts], s4[:ts], s8[:ts], s16[:ts])
        du_ref[...] = (ws - cur).astype(BF16)

    row = pl.BlockSpec((ts, w), lambda r: (r, 0))
    return pl.pallas_call(
        body, name="pool_bwd_window", grid=(nb * nq,),
        in_specs=[row, pl.BlockSpec((HALO_POOL, w),
                                    lambda r: (jnp.minimum((r + 1) * (ts // HALO_POOL), n_halo_blocks - 1), 0))],
        out_specs=row, out_shape=SDS((nb * s, w), BF16),
        compiler_params=_params(("parallel",)))(dpooled, dpooled)


def _conv_rows(xe, w_ref):
    return (w_ref[0, 2:3, :] * xe + w_ref[0, 1:2, :] * pltpu.roll(xe, 1, axis=0)
            + w_ref[0, 0:1, :] * pltpu.roll(xe, 2, axis=0) + w_ref[0, 3:4, :])


def ffn_up_gate(x_bf, wup, cw, nb, s, name):
    tm = min(TM, s)
    nq = s // tm
    w = FF_BLOCK_PAD
    hr = 2 * HALO_CONV
    k = x_bf.shape[1]

    def body(xc_ref, xh_ref, wu_ref, wg_ref, cu_ref, cg_ref, act_ref, a_ref, b_ref, hu_ref, hg_ref):
        first = (pl.program_id(1) % nq) == 0
        xc = xc_ref[...]
        xh = xh_ref[...]

        def products(w_ref):
            return (jnp.dot(xc, w_ref[...], preferred_element_type=F32), jnp.dot(xh, w_ref[...], preferred_element_type=F32))

        def conv(hcur, hprev, c_ref, h_out):
            h_out[...] = hcur.astype(BF16)
            xe = jnp.concatenate([jnp.where(first, 0.0, hprev), hcur], axis=0)
            return _conv_rows(xe, c_ref)[hr:]

        pu, pg = products(wu_ref), products(wg_ref)
        cu = conv(*pu, cu_ref, hu_ref)
        cg = conv(*pg, cg_ref, hg_ref)
        sg = _sigmoid(cg)
        a = cg * sg
        act_ref[...] = (a * cu).astype(BF16)
        a_ref[...] = a.astype(BF16)
        b_ref[...] = (cu * (sg * (1.0 + cg * (1.0 - sg)))).astype(BF16)

    def wblock(off):
        return pl.BlockSpec((k, w), lambda j, r: (0, j + off))

    def cblock(off):
        return pl.BlockSpec((1, 8, w), lambda j, r: (j + off, 0, 0))

    tile = pl.BlockSpec((tm, w), lambda j, r: (r, j))
    out = SDS((nb * s, FF_PAIRS * w), BF16)
    return pl.pallas_call(
        body, name=name, grid=(FF_PAIRS, nb * nq),
        in_specs=[pl.BlockSpec((tm, k), lambda j, r: (r, 0)),
                  pl.BlockSpec((hr, k), lambda j, r: (jnp.maximum(r * (tm // hr) - 1, 0), 0)),
                  wblock(0), wblock(FF_PAIRS), cblock(0), cblock(FF_PAIRS)],
        out_specs=[tile] * 5, out_shape=[out] * 5,
        compiler_params=_params(("parallel", "parallel")))(x_bf, x_bf, wup, wup, cw, cw)


def gate_conv_bwd(dact, a, b, hu, hg, cw, nb, s, name):
    ts = min(TS, s)
    nq = s // ts
    w = FF_BLOCK_PAD
    hc = HALO_CONV
    hb = 2 * hc
    n_ext = ts + hc

    def body(dc_ref, dn_ref, ac_ref, an_ref, bc_ref, bn_ref, hu_ref, hg_ref, wu_ref, wg_ref,
             dhu_ref, dhg_ref, dwu_ref, dwg_ref):
        r = pl.program_id(1)
        last = (r % nq) == nq - 1

        def ext(c_ref, n_ref, mask_next=False):
            nxt = n_ref[...].astype(F32)[:hc]
            if mask_next:
                nxt = jnp.where(last, 0.0, nxt)
            return jnp.concatenate([c_ref[...].astype(F32), nxt], axis=0)

        da = ext(dc_ref, dn_ref, mask_next=True)

        def branch(dcv, w_ref, h_ref, dh_ref, dw_ref):
            d0 = dcv[:ts]
            d1 = pltpu.roll(dcv, n_ext - 1, axis=0)[:ts]
            d2 = pltpu.roll(dcv, n_ext - 2, axis=0)[:ts]
            dh_ref[...] = (w_ref[0, 2:3, :] * d0 + w_ref[0, 1:2, :] * d1 + w_ref[0, 0:1, :] * d2).astype(BF16)
            hv = h_ref[...].astype(F32)
            rows = [jnp.sum(d2 * hv, axis=0, keepdims=True), jnp.sum(d1 * hv, axis=0, keepdims=True),
                    jnp.sum(d0 * hv, axis=0, keepdims=True), jnp.sum(d0, axis=0, keepdims=True)]
            sub = lax.broadcasted_iota(jnp.int32, (8, w), 0)
            upd = jnp.zeros((8, w), F32)
            for kk, rv in enumerate(rows):
                upd = jnp.where(sub == kk, rv, upd)

            @pl.when(r == 0)
            def _():
                dw_ref[...] = jnp.zeros_like(dw_ref)

            dw_ref[...] += upd[None]

        branch(da * ext(ac_ref, an_ref), wu_ref, hu_ref, dhu_ref, dwu_ref)
        branch(da * ext(bc_ref, bn_ref), wg_ref, hg_ref, dhg_ref, dwg_ref)

    cur = pl.BlockSpec((ts, w), lambda j, r: (r, j))
    nxt = pl.BlockSpec((hb, w), lambda j, r: (jnp.minimum((r + 1) * (ts // hb), nb * s // hb - 1), j))

    def wspec(off):
        return pl.BlockSpec((1, 8, w), lambda j, r: (j + off, 0, 0))

    p = FF_PAIRS
    dw_spec = pl.BlockSpec((1, 8, w), lambda j, r: (j, 0, 0))
    return pl.pallas_call(
        body, name=name, grid=(p, nb * nq),
        in_specs=[cur, nxt, cur, nxt, cur, nxt, cur, cur, wspec(0), wspec(p)],
        out_specs=[cur, cur, dw_spec, dw_spec],
        out_shape=[SDS((nb * s, p * w), BF16), SDS((nb * s, p * w), BF16), SDS((p, 8, w), F32), SDS((p, 8, w), F32)],
        compiler_params=_params(("parallel", "arbitrary")))(dact, dact, a, a, b, b, hu, hg, cw, cw)


def _tri(n, upper):
    r = lax.broadcasted_iota(jnp.int32, (n, n), 0)
    c = lax.broadcasted_iota(jnp.int32, (n, n), 1)
    return ((r <= c) if upper else (r >= c)).astype(F32)


def fgate_fwd(fl, fb, nb, s):
    tc = min(TC, s)
    nq = s // tc

    def body(fl_ref, fb_ref, f_ref, carry):
        @pl.when(pl.program_id(1) == 0)
        def _():
            carry[...] = jnp.zeros_like(carry)

        z = fl_ref[...] + fb_ref[...]
        logf = jnp.minimum(z, 0.0) - jnp.log(1.0 + jnp.exp(-jnp.abs(z)))
        f_ref[...] = jnp.dot(_tri(tc, False), logf, preferred_element_type=F32,
                             precision=lax.Precision.HIGHEST) + carry[...]
        carry[...] += jnp.sum(logf, axis=0, keepdims=True)

    row = pl.BlockSpec((tc, LANES), lambda b, i: (b * nq + i, 0))
    return pl.pallas_call(
        body, name="fgate_fwd", grid=(nb, nq), in_specs=[row, pl.BlockSpec((1, LANES), lambda b, i: (0, 0))],
        out_specs=row, out_shape=SDS((nb * s, LANES), F32), scratch_shapes=[pltpu.VMEM((1, LANES), F32)],
        compiler_params=_params(("arbitrary", "arbitrary")))(fl, fb)


def fgate_bwd(d_cum_q, d_cum_k, fl, fb, nb, s):
    tc = min(TC, s)
    nq = s // tc

    def body(dfq_ref, dfk_ref, fl_ref, fb_ref, dfl_ref, dfb_ref, carry):
        b = pl.program_id(0)
        i = pl.program_id(1)

        @pl.when(i == 0)
        def _():
            carry[...] = jnp.zeros_like(carry)

        @pl.when(jnp.logical_and(b == 0, i == 0))
        def _():
            dfb_ref[...] = jnp.zeros_like(dfb_ref)

        dfv = dfq_ref[...] + dfk_ref[...]
        dlog = jnp.dot(_tri(tc, True), dfv, preferred_element_type=F32,
                       precision=lax.Precision.HIGHEST) + carry[...]
        carry[...] += jnp.sum(dfv, axis=0, keepdims=True)
        z = fl_ref[...] + fb_ref[...]
        dfl = dlog / (1.0 + jnp.exp(z))
        dfl_ref[...] = dfl
        dfb_ref[...] += jnp.sum(dfl, axis=0, keepdims=True)

    row = pl.BlockSpec((tc, LANES), lambda b, i: (b * nq + nq - 1 - i, 0))
    vec = pl.BlockSpec((1, LANES), lambda b, i: (0, 0))
    return pl.pallas_call(
        body, name="fgate_bwd", grid=(nb, nq), in_specs=[row, row, row, vec], out_specs=[row, vec],
        out_shape=[SDS((nb * s, LANES), F32), SDS((1, LANES), F32)], scratch_shapes=[pltpu.VMEM((1, LANES), F32)],
        compiler_params=_params(("arbitrary", "arbitrary")))(d_cum_q, d_cum_k, fl, fb)


PAIR = 2 * HEAD_DIM
N_PAIRS = FOX_HEADS // 2


def _lane_put(shape, h, col):
    lane = lax.broadcasted_iota(jnp.int32, shape, 1)
    return jnp.where(lane == h, col, 0.0)


def _half_masks(rows):
    lane = lax.broadcasted_iota(jnp.int32, (rows, PAIR), 1)
    return lane < HEAD_DIM


def _split_pair(x, scale=None):
    if scale is not None:
        x = x * scale
    lo = _half_masks(x.shape[0])
    zero = jnp.zeros_like(x)
    return jnp.where(lo, x, zero), jnp.where(lo, zero, x)


def _to_tile_rows(a, nb, s, tf):
    return a.reshape(nb * s // tf, tf, LANES)[:, :, :16].transpose(0, 2, 1)


def _from_tile_rows(a):
    tiles, _, tf = a.shape
    return jnp.pad(a.transpose(0, 2, 1), ((0, 0), (0, 0), (0, LANES - 16))).reshape(tiles * tf, LANES)


BIAS_TERMS = 3
LOOKAHEAD = 4
FOLLOW_FWD = 1
LOOKAHEAD_BWD = 2
FOLLOW_BWD = 1


def _bias_lane(h):
    return HEAD_DIM if h % 2 == 0 else 0


def _placement():
    rows = jnp.arange(LANES)[:, None]
    cols = jnp.arange(FOX_HEADS * PAIR)[None, :]
    head, lane = cols // PAIR, cols % PAIR
    first = jnp.where(head % 2 == 0, HEAD_DIM, 0)
    term = lane - first
    hit = (term >= 0) & (term < BIAS_TERMS) & (rows == 16 * term + head)
    return hit.astype(BF16)


def fox_prep(kv, fneg, nb, s):
    tf = min(TF, s)
    w = TOK_WIDTH

    def body(k_ref, v_ref, f_ref, pl_ref, ka_ref, vt_ref):
        lane = lax.broadcasted_iota(jnp.int32, (tf, LANES), 1)
        lo = lane < HEAD_DIM
        f = jnp.where(lane < FOX_HEADS, f_ref[...], 0.0)
        hi = f.astype(BF16).astype(F32)
        mid = (f - hi).astype(BF16).astype(F32)
        low = (f - hi - mid).astype(BF16).astype(F32)
        terms = (hi + pltpu.roll(mid, 16, axis=1) + pltpu.roll(low, 32, axis=1)).astype(BF16)
        placed = jnp.dot(terms, pl_ref[...], preferred_element_type=F32).astype(BF16)
        one = jnp.ones((tf, LANES), BF16)
        zero = jnp.zeros((tf, LANES), BF16)
        for p in range(N_PAIRS):
            kp = k_ref[:, p * PAIR:(p + 1) * PAIR] * QK_SCALE
            vp = v_ref[:, p * PAIR:(p + 1) * PAIR]
            he, ho = 2 * p, 2 * p + 1
            ka_ref[:, he * PAIR:(he + 1) * PAIR] = jnp.where(lo, kp, placed[:, he * PAIR:(he + 1) * PAIR])
            ka_ref[:, ho * PAIR:(ho + 1) * PAIR] = jnp.where(lo, placed[:, ho * PAIR:(ho + 1) * PAIR], kp)
            ve = jnp.where(lo, vp, jnp.where(lane == HEAD_DIM, one, zero))
            vo = jnp.where(lo, jnp.where(lane == 0, one, zero), vp)
            vt_ref[0, he * PAIR:(he + 1) * PAIR, :] = ve.astype(F32).T.astype(BF16)
            vt_ref[0, ho * PAIR:(ho + 1) * PAIR, :] = vo.astype(F32).T.astype(BF16)

    return pl.pallas_call(
        body, name="fox_prep", grid=(nb * s // tf,),
        in_specs=[pl.BlockSpec((tf, w), lambda r: (r, 0)), pl.BlockSpec((tf, w), lambda r: (r, 1)),
                  pl.BlockSpec((tf, LANES), lambda r: (r, 0)), pl.BlockSpec((LANES, FOX_HEADS * PAIR), lambda r: (0, 0))],
        out_specs=[pl.BlockSpec((tf, FOX_HEADS * PAIR), lambda r: (r, 0)),
                   pl.BlockSpec((1, FOX_HEADS * PAIR, tf), lambda r: (r, 0, 0))],
        out_shape=[SDS((nb * s, FOX_HEADS * PAIR), BF16), SDS((nb * s // tf, FOX_HEADS * PAIR, tf), BF16)],
        compiler_params=_params(("parallel",)))(kv, kv, fneg, _placement())


def fox_fwd_t(pq, kaug, vaug_t, nb, s):
    tf = min(TF, s)
    n = s // tf
    w = TOK_WIDTH
    wa = FOX_HEADS * PAIR

    def body(q_ref, k_hbm, vt_hbm, ob_ref, of_ref, lse_ref, k_vm, vt_vm, qx_scr, m_scr, acc_scr, sems):
        b = pl.program_id(0)
        i = pl.program_id(1)

        @pl.when(i == 0)
        def _():
            ck = pltpu.make_async_copy(k_hbm.at[pl.ds(pl.multiple_of(b * s, tf), s)], k_vm, sems.at[0])
            cv = pltpu.make_async_copy(vt_hbm.at[pl.ds(b * n, n)], vt_vm, sems.at[1])
            ck.start()
            cv.start()
            ck.wait()
            cv.wait()

        lane = lax.broadcasted_iota(jnp.int32, (tf, PAIR), 1)
        one = jnp.ones((tf, PAIR), BF16)
        zero = jnp.zeros((tf, PAIR), BF16)
        for p in range(N_PAIRS):
            qp = q_ref[:, p * PAIR:(p + 1) * PAIR]
            be, bo = _bias_lane(2 * p), _bias_lane(2 * p + 1)
            ones_e = jnp.where((lane >= be) & (lane < be + BIAS_TERMS), one, zero)
            ones_o = jnp.where((lane >= bo) & (lane < bo + BIAS_TERMS), one, zero)
            qx_scr[2 * p] = jnp.where(lane < HEAD_DIM, qp, ones_e)
            qx_scr[2 * p + 1] = jnp.where(lane < HEAD_DIM, ones_o, qp)
        m_scr[...] = jnp.full(m_scr.shape, NEG_BIG, F32)
        acc_scr[...] = jnp.zeros_like(acc_scr)

        def tile(j, masked):
            ks = pl.multiple_of(j * tf, tf)
            if masked:
                keep = lax.broadcasted_iota(jnp.int32, (tf, tf), 1) >= lax.broadcasted_iota(jnp.int32, (tf, tf), 0)
            def scores(h):
                kx = k_vm[pl.ds(ks, tf), h * PAIR:(h + 1) * PAIR]
                return lax.dot_general(kx, qx_scr[h], NT_DIMS, preferred_element_type=F32)

            def values(h, pr, a):
                pv = jnp.dot(vt_vm[j, h * PAIR:(h + 1) * PAIR, :], pr, preferred_element_type=F32)
                acc_scr[h] = a * acc_scr[h] + pv

            ahead = [scores(h) for h in range(LOOKAHEAD)]
            behind = []
            for h in range(FOX_HEADS):
                sc = ahead.pop(0)
                if h + LOOKAHEAD < FOX_HEADS:
                    ahead.append(scores(h + LOOKAHEAD))
                if masked:
                    sc = jnp.where(keep, sc, NEG_BIG)
                m_prev = m_scr[h]
                m_new = jnp.maximum(m_prev, jnp.max(sc, axis=0, keepdims=True))
                m_scr[h] = m_new
                behind.append((h, jnp.exp(sc - m_new).astype(BF16), jnp.exp(m_prev - m_new)))
                if len(behind) > FOLLOW_FWD:
                    values(*behind.pop(0))
            for item in behind:
                values(*item)

        def step(j, carry):
            tile(j, False)
            return carry

        lax.fori_loop(0, i, step, 0)
        tile(i, True)

        top = lax.broadcasted_iota(jnp.int32, (PAIR, tf), 0) < HEAD_DIM
        sub = lax.broadcasted_iota(jnp.int32, (16, tf), 0)
        lse = jnp.zeros((16, tf), F32)
        for p in range(N_PAIRS):
            he, ho = 2 * p, 2 * p + 1
            le = acc_scr[he, HEAD_DIM:HEAD_DIM + 1, :]
            lod = acc_scr[ho, 0:1, :]
            o = jnp.where(top, acc_scr[he] / le, acc_scr[ho] / lod).T
            ob_ref[:, p * PAIR:(p + 1) * PAIR] = o.astype(BF16)
            of_ref[:, p * PAIR:(p + 1) * PAIR] = o
            lse = jnp.where(sub == he, m_scr[he] + jnp.log(le), lse)
            lse = jnp.where(sub == ho, m_scr[ho] + jnp.log(lod), lse)
        lse_ref[0] = lse

    qrow = lambda b, i: (b * n + i, 0)
    return pl.pallas_call(
        body, name="fox_fwd", grid=(nb, n),
        in_specs=[pl.BlockSpec((tf, w), qrow), ANY_SPEC, ANY_SPEC],
        out_specs=[pl.BlockSpec((tf, w), qrow), pl.BlockSpec((tf, w), qrow),
                   pl.BlockSpec((1, 16, tf), lambda b, i: (b * n + i, 0, 0))],
        out_shape=[SDS((nb * s, w), BF16), SDS((nb * s, w), F32), SDS((nb * n, 16, tf), F32)],
        scratch_shapes=[pltpu.VMEM((s, wa), BF16), pltpu.VMEM((n, wa, tf), BF16),
                        pltpu.VMEM((FOX_HEADS, tf, PAIR), BF16), pltpu.VMEM((FOX_HEADS, 1, tf), F32),
                        pltpu.VMEM((FOX_HEADS, PAIR, tf), F32), pltpu.SemaphoreType.DMA((2,))],
        compiler_params=_params(("arbitrary", "arbitrary")))(pq, kaug, vaug_t)


def fox_delta(dcat, o, nb, s):
    tf = min(TM, s)
    w = TOK_WIDTH

    def body(do_ref, o_ref, dl_ref):
        out = jnp.zeros((tf, LANES), F32)
        for h in range(FOX_HEADS):
            lo, hi = h * HEAD_DIM, (h + 1) * HEAD_DIM
            out = out + _lane_put((tf, LANES), h, jnp.sum(do_ref[:, lo:hi] * o_ref[:, lo:hi], axis=1, keepdims=True))
        dl_ref[...] = out

    row = pl.BlockSpec((tf, w), lambda r: (r, 0))
    return pl.pallas_call(
        body, name="fox_delta", grid=(nb * s // tf,), in_specs=[row, row],
        out_specs=pl.BlockSpec((tf, LANES), lambda r: (r, 0)), out_shape=SDS((nb * s, LANES), F32),
        compiler_params=_params(("parallel",)))(dcat, o)


def fox_bwd(pq, kv, fneg, dcat_bf, lse_rows, delta_rows, nb, s):
    tf = min(TF, s)
    n = s // tf
    w = TOK_WIDTH

    def body(q_hbm, k_ref, v_ref, f_ref, do_hbm, lse_ref, dl_ref, dq_ref, dk_ref, dv_ref, dfk_ref, dfq_ref,
             q_vm, do_vm, km_scr, vm_scr, kt_scr, fk_scr, dk_scr, dv_scr, rs_scr, dq_scr, fq_scr, sems):
        b = pl.program_id(0)
        j = pl.program_id(1)

        @pl.when(j == 0)
        def _():
            rows = pl.ds(pl.multiple_of(b * s, tf), s)
            cq = pltpu.make_async_copy(q_hbm.at[rows, pl.ds(0, w)], q_vm, sems.at[0])
            cd = pltpu.make_async_copy(do_hbm.at[rows, pl.ds(0, w)], do_vm, sems.at[1])
            cq.start()
            cd.start()
            dq_scr[...] = jnp.zeros_like(dq_scr)
            fq_scr[...] = jnp.zeros_like(fq_scr)
            cq.wait()
            cd.wait()

        for p in range(N_PAIRS):
            kp = k_ref[:, p * PAIR:(p + 1) * PAIR] * QK_SCALE
            ke, ko = _split_pair(kp)
            km_scr[2 * p] = ke
            km_scr[2 * p + 1] = ko
            kt_scr[p] = kp.astype(F32).T.astype(BF16)
            ve, vo = _split_pair(v_ref[:, p * PAIR:(p + 1) * PAIR])
            vm_scr[2 * p] = ve
            vm_scr[2 * p + 1] = vo
        for h in range(FOX_HEADS):
            fk_scr[h] = jnp.broadcast_to(f_ref[:, h:h + 1], (tf, tf))
        dk_scr[...] = jnp.zeros_like(dk_scr)
        dv_scr[...] = jnp.zeros_like(dv_scr)
        rs_scr[...] = jnp.zeros_like(rs_scr)

        def tile(i, masked):
            qs = pl.multiple_of(i * tf, tf)
            if masked:
                keep = lax.broadcasted_iota(jnp.int32, (tf, tf), 1) >= lax.broadcasted_iota(jnp.int32, (tf, tf), 0)
            def products(h):
                qp = q_vm[pl.ds(qs, tf), (h // 2) * PAIR:(h // 2 + 1) * PAIR]
                dop = do_vm[pl.ds(qs, tf), (h // 2) * PAIR:(h // 2 + 1) * PAIR]
                return (lax.dot_general(km_scr[h], qp, NT_DIMS, preferred_element_type=F32),
                        lax.dot_general(vm_scr[h], dop, NT_DIMS, preferred_element_type=F32))

            def dependents(h, prb, dsb):
                p = h // 2
                half = slice((h % 2) * HEAD_DIM, (h % 2 + 1) * HEAD_DIM)
                qp = q_vm[pl.ds(qs, tf), p * PAIR:(p + 1) * PAIR]
                dop = do_vm[pl.ds(qs, tf), p * PAIR:(p + 1) * PAIR]
                dv_scr[h] += jnp.dot(prb, dop, preferred_element_type=F32)
                dk_scr[h] += jnp.dot(dsb, qp, preferred_element_type=F32)
                dqt = jnp.dot(kt_scr[p], dsb, preferred_element_type=F32)
                dq_scr[i, p, half, :] += dqt[(h % 2) * HEAD_DIM:(h % 2 + 1) * HEAD_DIM]

            ahead = [products(h) for h in range(LOOKAHEAD_BWD)]
            behind = []
            for h in range(FOX_HEADS):
                sc, dp = ahead.pop(0)
                if h + LOOKAHEAD_BWD < FOX_HEADS:
                    ahead.append(products(h + LOOKAHEAD_BWD))
                sc = sc + fk_scr[h] - lse_ref[i, h:h + 1, :]
                if masked:
                    sc = jnp.where(keep, sc, NEG_BIG)
                pr = jnp.exp(sc)
                ds = pr * (dp - dl_ref[i, h:h + 1, :])
                part = ds[:, :LANES]
                for c in range(1, tf // LANES):
                    part = part + ds[:, c * LANES:(c + 1) * LANES]
                rs_scr[h] += part
                fq_scr[i, h:h + 1, :] += jnp.sum(ds, axis=0, keepdims=True)
                behind.append((h, pr.astype(BF16), ds.astype(BF16)))
                if len(behind) > FOLLOW_BWD:
                    dependents(*behind.pop(0))
            for item in behind:
                dependents(*item)

        def step(i, carry):
            tile(i, False)
            return carry

        tile(j, True)
        for p in range(N_PAIRS):
            dq_ref[:, p * PAIR:(p + 1) * PAIR] = dq_scr[j, p].T.astype(BF16)
        dfq_ref[0] = fq_scr[j]
        lax.fori_loop(j + 1, n, step, 0)

        lo = _half_masks(tf)
        dfk = jnp.zeros((tf, LANES), F32)
        for p in range(N_PAIRS):
            dk = jnp.where(lo, dk_scr[2 * p], dk_scr[2 * p + 1]) * QK_SCALE
            dk_ref[:, p * PAIR:(p + 1) * PAIR] = dk.astype(BF16)
            dv_ref[:, p * PAIR:(p + 1) * PAIR] = jnp.where(lo, dv_scr[2 * p], dv_scr[2 * p + 1]).astype(BF16)
            for h in (2 * p, 2 * p + 1):
                dfk = dfk - _lane_put((tf, LANES), h, jnp.sum(rs_scr[h], axis=1, keepdims=True))
        dfk_ref[...] = dfk

    krow = lambda b, j: (b * n + j, 0)
    rows = pl.BlockSpec((n, 16, tf), lambda b, j: (b, 0, 0))
    tile_out = pl.BlockSpec((tf, w), krow)
    return pl.pallas_call(
        body, name="fox_bwd", grid=(nb, n),
        in_specs=[ANY_SPEC, pl.BlockSpec((tf, w), krow), pl.BlockSpec((tf, w), lambda b, j: (b * n + j, 1)),
                  pl.BlockSpec((tf, LANES), krow), ANY_SPEC, rows, rows],
        out_specs=[tile_out, tile_out, tile_out, pl.BlockSpec((tf, LANES), krow),
                   pl.BlockSpec((1, 16, tf), lambda b, j: (b * n + j, 0, 0))],
        out_shape=[SDS((nb * s, w), BF16), SDS((nb * s, w), BF16), SDS((nb * s, w), BF16), SDS((nb * s, LANES), F32),
                   SDS((nb * n, 16, tf), F32)],
        scratch_shapes=[pltpu.VMEM((s, w), BF16), pltpu.VMEM((s, w), BF16),
                        pltpu.VMEM((FOX_HEADS, tf, PAIR), BF16), pltpu.VMEM((FOX_HEADS, tf, PAIR), BF16),
                        pltpu.VMEM((N_PAIRS, PAIR, tf), BF16), pltpu.VMEM((FOX_HEADS, tf, tf), F32),
                        pltpu.VMEM((FOX_HEADS, tf, PAIR), F32), pltpu.VMEM((FOX_HEADS, tf, PAIR), F32),
                        pltpu.VMEM((FOX_HEADS, tf, LANES), F32), pltpu.VMEM((n, N_PAIRS, PAIR, tf), F32),
                        pltpu.VMEM((n, 16, tf), F32), pltpu.SemaphoreType.DMA((2,))],
        compiler_params=_params(("arbitrary", "arbitrary")))(pq, kv, kv, fneg, dcat_bf, lse_rows, delta_rows)


ADAMW_TILE_ELEMS = 128 * 1024


def reduce_adamw(parts, w, m, v, name):
    layers, r, c = w.shape
    tr = r
    for cand in range(16, r, 16):
        if r % cand == 0 and cand * c <= ADAMW_TILE_ELEMS:
            tr = cand
    c1 = 1.0 - ADAM_B1 ** ADAM_STEP
    c2 = 1.0 - ADAM_B2 ** ADAM_STEP

    def body(*refs):
        p_refs = refs[:layers]
        w_ref, m_ref, v_ref, g_out, d_out, m_out, v_out = refs[layers:]

        def update(p_ref):
            g = p_ref[0].astype(F32)
            for k in range(1, N_DEV):
                g = g + p_ref[k].astype(F32)
            mn = ADAM_B1 * m_ref[0] + (1.0 - ADAM_B1) * g
            vn = ADAM_B2 * v_ref[0] + (1.0 - ADAM_B2) * (g * g)
            g_out[0] = g
            m_out[0] = mn
            v_out[0] = vn
            d_out[0] = -ADAM_LR * ((mn / c1) / (jnp.sqrt(vn / c2) + ADAM_EPS) + ADAM_WD * w_ref[0])

        if layers == 1:
            update(p_refs[0])
        else:
            for layer in range(layers):
                pl.when(pl.program_id(0) == layer)(lambda layer=layer: update(p_refs[layer]))

    row = pl.BlockSpec((1, tr, c), lambda l, i: (l, i, 0))
    return pl.pallas_call(
        body, name=name, grid=(layers, r // tr),
        in_specs=[pl.BlockSpec((N_DEV, tr, c), lambda l, i: (0, i, 0))] * layers + [row, row, row],
        out_specs=[row, row, row, row], out_shape=[SDS((layers, r, c), F32)] * 4,
        compiler_params=_params(("parallel", "parallel")))(*parts, w, m, v)


N_PEERS = N_DEV - 1
HBM_SPEC = pl.BlockSpec(memory_space=pltpu.HBM)
SEM_SPEC = pl.BlockSpec(memory_space=pltpu.SEMAPHORE)
ANY_SPEC = pl.BlockSpec(memory_space=pl.ANY)
SPLIT_EFFECT = pltpu.SideEffectType.DATAFLOW_SIDE_EFFECTING


def _my_index():
    return 4 * lax.axis_index("x") + 2 * lax.axis_index("y") + lax.axis_index("c")


def _peers():
    x, y, c = lax.axis_index("x"), lax.axis_index("y"), lax.axis_index("c")
    peers = []
    for k in range(1, N_DEV):
        px = 1 - x if (k >> 2) & 1 else x
        py = 1 - y if (k >> 1) & 1 else y
        pc = 1 - c if k & 1 else c
        peers.append(((px, py, pc), 4 * px + 2 * py + pc))
    return 4 * x + 2 * y + c, peers


def _push(src, dst, send_sems, recv_sems, slot, dev):
    return pltpu.make_async_remote_copy(src_ref=src, dst_ref=dst, send_sem=send_sems.at[slot], recv_sem=recv_sems.at[slot],
                                        device_id=dev, device_id_type=pl.DeviceIdType.MESH)


def _landing_shapes(arrs, scatter):
    return [SDS((N_DEV,) + tuple(a.shape[1:] if sc else a.shape), a.dtype) for a, sc in zip(arrs, scatter)]


def exchange(arrs, scatter, name):
    na = len(arrs)

    def body(*refs):
        ins = refs[:na]
        outs = refs[na:2 * na]
        send_sems, recv_sems, local_sems = refs[2 * na:]
        me, peers = _peers()
        local = []
        remote = []
        for a in range(na):
            lc = pltpu.make_async_copy(ins[a].at[me] if scatter[a] else ins[a], outs[a].at[me], local_sems.at[a])
            lc.start()
            local.append(lc)
            for k, (dev, idx) in enumerate(peers):
                cp = _push(ins[a].at[idx] if scatter[a] else ins[a], outs[a].at[me], send_sems, recv_sems,
                           a * N_PEERS + k, dev)
                cp.start()
                remote.append(cp)
        for a in range(na):
            for k, (dev, idx) in enumerate(peers):
                _push(ins[a].at[me] if scatter[a] else ins[a], outs[a].at[idx], send_sems, recv_sems,
                      a * N_PEERS + k, dev).wait_recv()
        for cp in remote:
            cp.wait_send()
        for lc in local:
            lc.wait()

    return pl.pallas_call(
        body, name=name, in_specs=[HBM_SPEC] * na, out_specs=[HBM_SPEC] * na, out_shape=_landing_shapes(arrs, scatter),
        scratch_shapes=[pltpu.SemaphoreType.DMA((na * N_PEERS,)), pltpu.SemaphoreType.DMA((na * N_PEERS,)),
                        pltpu.SemaphoreType.DMA((na,))])(*arrs)


def exchange_start(arrs, scatter, after, name):
    na = len(arrs)
    lands = [lax.empty(l.shape, l.dtype) for l in _landing_shapes(arrs, scatter)]

    def body(*refs):
        ins = refs[:na]
        land = refs[na:2 * na]
        send_sems, recv_sems = refs[2 * na + 1], refs[2 * na + 2]
        token = refs[-1]
        me, peers = _peers()
        for a in range(na):
            for k, (dev, idx) in enumerate(peers):
                _push(ins[a].at[idx] if scatter[a] else ins[a], land[a].at[me], send_sems, recv_sems,
                      a * N_PEERS + k, dev).start()
        token[...] = jnp.zeros_like(token)

    thru = [pltpu.HBM(a.shape, a.dtype) for a in arrs] + [pltpu.HBM(l.shape, l.dtype) for l in lands]
    res = pl.pallas_call(
        body, name=name,
        out_shape=(pltpu.SemaphoreType.DMA((na * N_PEERS,)), pltpu.SemaphoreType.DMA((na * N_PEERS,)), *thru,
                   SDS((8, LANES), F32)),
        in_specs=[HBM_SPEC] * (2 * na) + [ANY_SPEC],
        out_specs=(SEM_SPEC, SEM_SPEC, *([HBM_SPEC] * (2 * na)), pl.BlockSpec(memory_space=pltpu.VMEM)),
        input_output_aliases={i: 2 + i for i in range(2 * na)},
        compiler_params=pltpu.CompilerParams(has_side_effects=SPLIT_EFFECT),
    )(*[pltpu.with_memory_space_constraint(a, pltpu.HBM) for a in arrs],
      *[pltpu.with_memory_space_constraint(l, pltpu.HBM) for l in lands], after)
    return {"send": res[0], "recv": res[1], "src": res[2:2 + na], "land": res[2 + na:2 + 2 * na],
            "token": res[-1][0, 0], "scatter": scatter}


def exchange_wait(handle, after, name):
    scatter = handle["scatter"]
    na = len(scatter)

    def body(*refs):
        src = refs[:na]
        land = refs[na:2 * na]
        send_sems, recv_sems = refs[2 * na], refs[2 * na + 1]
        me, peers = _peers()
        for a in range(na):
            for k, (dev, idx) in enumerate(peers):
                cp = _push(src[a].at[me] if scatter[a] else src[a], land[a].at[idx], send_sems, recv_sems,
                           a * N_PEERS + k, dev)
                cp.wait_send()
                cp.wait_recv()

    ops = list(handle["src"]) + list(handle["land"])
    res = pl.pallas_call(
        body, name=name, out_shape=tuple(pltpu.HBM(o.shape, o.dtype) for o in ops),
        in_specs=[HBM_SPEC] * (2 * na) + [SEM_SPEC, SEM_SPEC, ANY_SPEC], out_specs=tuple([HBM_SPEC] * (2 * na)),
        input_output_aliases={i: i for i in range(2 * na)},
        compiler_params=pltpu.CompilerParams(has_side_effects=SPLIT_EFFECT),
    )(*ops, handle["send"], handle["recv"], after)
    me = _my_index()
    out = []
    for a in range(na):
        own = lax.dynamic_index_in_dim(res[a], me, 0, keepdims=True) if scatter[a] else res[a][None]
        out.append(lax.dynamic_update_slice(res[na + a], own, (me,) + (0,) * (own.ndim - 1)))
    return out


def forward_layer(l, xin, xin_bf, mem_bf, wt, nb, s, ffn_weights=None):
    sv = {"xin_bf": xin_bf}
    memkv = mm_nn(mem_bf, wt["memw"], BF16, f"memkv{l}")
    sv["memkv"] = memkv
    if l == 0:
        proj = mm_nn(xin_bf, wt["win_a"], F32, "proj_a")
        pooled, tok = pool_fwd(proj, wt["pw_bd"], wt["pscale"], nb, s)
        sv["pooled"] = pooled
    else:
        kv = mm_nn(xin_bf, wt["kvw"][:, :2 * TOK_WIDTH], BF16, "kv_proj")
        fl = mm_nn(xin_bf, wt["kvw"][:, 2 * TOK_WIDTH:], F32, "gate_proj")
        fneg = -fgate_fwd(fl, wt["fb"], nb, s)
        proj = mm_nn(xin_bf, wt["wq"], BF16, "proj_b")
        kaug, vaug_t = fox_prep(kv, fneg, nb, s)
        tok, o_f32, lse_rows = fox_fwd_t(proj, kaug, vaug_t, nb, s)
        sv.update(kv=kv, fl=fl, fneg=fneg, o_f32=o_f32, lse_rows=lse_rows)
    sv["proj"] = proj
    mem_out = memattn_fwd(proj, memkv, nb, s, f"memattn_fwd{l}")
    cat = jnp.concatenate([tok, mem_out], axis=1)
    sv["cat"] = cat
    x1, x1_bf, xh1, rs1 = ln_fwd(xin, cat, wt["wout"], wt["ln1_g"], wt["ln1_b"], f"out_proj_ln1_{l}")
    sv.update(x1_bf=x1_bf, xh1=xh1, rs1=rs1)
    if ffn_weights is not None:
        wt.update(ffn_weights(x1_bf))
    act, ga, gb, hu, hg = ffn_up_gate(x1_bf, wt["wup"], wt["cw"], nb, s, f"ffn_up_gate{l}")
    sv.update(act=act, ga=ga, gb=gb, hu=hu, hg=hg)
    x2, x2_bf, xh2, rs2 = ln_fwd(x1, act, wt["wdown"], wt["ln2_g"], wt["ln2_b"], f"ffn_down_ln2_{l}")
    sv.update(xh2=xh2, rs2=rs2)
    return x2, x2_bf, sv


def backward_layer(l, dy, sv, mem_bf, wt, nb, s, after_ffn=None, after_pool=None, dy_product=None):
    g = {}
    dr2, dr2_bf, g["ln2_g"], g["ln2_b"] = ln_bwd(dy, sv["xh2"], sv["rs2"], wt["ln2_g"], f"ln2_bwd{l}", product=dy_product)
    dact = mm_nn(dr2_bf, wt["wdown"], BF16, f"ffn_down_dx{l}", trans_b=0)
    g["wdown"] = mm_tn(sv["act"], dr2_bf, f"ffn_down_dw{l}")
    dh_u, dh_g, dcw_u, dcw_g = gate_conv_bwd(dact, sv["ga"], sv["gb"], sv["hu"], sv["hg"], wt["cw"], nb, s,
                                             f"gate_conv_bwd{l}")
    g["cw"] = jnp.concatenate([dcw_u, dcw_g], axis=0)
    dx1 = mm_nn(dh_u, wt["wup"], F32, f"ffn_up_dx_u{l}", addend=dr2, add_scale=DN_ALPHA, trans_b=0)
    g["wup"] = jnp.concatenate([mm_tn(sv["x1_bf"], dh_u, f"ffn_up_dw_u{l}", blocked=True),
                                mm_tn(sv["x1_bf"], dh_g, f"ffn_up_dw_g{l}", blocked=True)], axis=0)
    ln1_g = wt["ln1_g"] if after_ffn is None else wt["ln1_g"] + after_ffn(g, dx1)
    dr1, dr1_bf, g["ln1_g"], g["ln1_b"] = ln_bwd(dx1, sv["xh1"], sv["rs1"], ln1_g, f"ffn_up_dx_g_ln1_bwd{l}",
                                                 product=(dh_g, wt["wup"], 1))
    dcat, dcat_bf = mm_nn(dr1_bf, wt["wout"], F32, f"out_proj_dx{l}", also_bf16=True, trans_b=0)
    g["wout"] = mm_tn(sv["cat"], dr1_bf, f"out_proj_dw{l}")
    dqm, dmemkv = memattn_bwd(sv["proj"], sv["memkv"], dcat, nb, s, f"memattn_bwd{l}")
    g["memw"] = mm_tn(mem_bf, dmemkv, f"memkv_dw{l}")
    if l == 0:
        dmixed, dpooled, g["pscale"] = pool_bwd_mix(dcat, sv["pooled"], wt["pw_bd"], wt["pscale"], nb, s)
        g["pw_full"] = mm_tn(sv["pooled"], dmixed, "pool_dw")
        win_a = wt["win_a"] if after_pool is None else wt["win_a"] + after_pool(g, dmixed).astype(BF16)
        du = pool_bwd_window(dpooled, nb, s)
        dproj = jnp.concatenate([du, dqm], axis=1)
        dx = mm_nn(dproj, win_a, F32, "proj_a_dx", addend=dr1, add_scale=DN_ALPHA, trans_b=0)
        g["win_a"] = mm_tn(sv["xin_bf"], dproj, "proj_a_dw")
        pending = None
    else:
        delta = fox_delta(dcat, sv["o_f32"], nb, s)
        tf = min(TF, s)
        dq, dk, dv, dfcum_k, dfq_rows = fox_bwd(sv["proj"], sv["kv"], sv["fneg"], dcat_bf,
                                                sv["lse_rows"], _to_tile_rows(delta, nb, s, tf), nb, s)
        dfl, g["fb"] = fgate_bwd(_from_tile_rows(dfq_rows), dfcum_k, sv["fl"], wt["fb"], nb, s)
        dproj = jnp.concatenate([dq, dqm], axis=1)
        dkvf = jnp.concatenate([dk, dv, dfl.astype(BF16)], axis=1)
        dx = mm_nn(dproj, wt["wq"], F32, "proj_b_dx", addend=dr1, add_scale=DN_ALPHA, trans_b=0)
        pending = (dkvf, wt["kvw"], 0)
        g["wq"] = mm_tn(sv["xin_bf"], dproj, "proj_b_dw")
        g["kvw"] = mm_tn(sv["xin_bf"], dkvf, "kv_proj_dw")
    return dx, pending, g


def pack_replicated(pool_w, ln1_g, ln1_b, ln2_g, ln2_b, conv_b, f_b):
    cb = jnp.pad(conv_b, ((0, 0), (0, 6144 - 5504))).reshape(12, D_MODEL)
    fb = jnp.pad(f_b.reshape(1, FOX_HEADS), ((0, 3), (0, D_MODEL - FOX_HEADS)))
    return jnp.concatenate([pool_w.reshape(144, D_MODEL), ln1_g, ln1_b, ln2_g, ln2_b, cb, fb], axis=0)


def unpack_replicated(buf):
    pool_w = buf[:144].reshape(1, 4, POOL_GROUP, POOL_GROUP)
    ln = [buf[144 + 2 * k:146 + 2 * k] for k in range(4)]
    conv_b = buf[152:164].reshape(2, 6144)[:, :5504]
    f_b = buf[164, :FOX_HEADS]
    return pool_w, ln[0], ln[1], ln[2], ln[3], conv_b, f_b


def pack_small(conv_w, pool_scale):
    buf = jnp.zeros((16, FF_BLOCK_PAD), F32)
    buf = lax.dynamic_update_slice(buf, conv_w.reshape(DEPTH * 3, FF_BLOCK), (0, 0))
    return lax.dynamic_update_slice(buf, pool_scale, (8, 0))


def _block_diag(pw):
    out = jnp.zeros((TOK_WIDTH, TOK_WIDTH), pw.dtype)
    for g in range(4):
        out = lax.dynamic_update_slice(out, pw[g], (g * POOL_GROUP, g * POOL_GROUP))
    return out


def layer_shards(l, sq_a, sq_b, mem_w_kv, ffn_w_up, ffn_w_down):
    return [sq_a[0].astype(BF16), sq_b[0].astype(BF16), mem_w_kv[l].astype(BF16), ffn_w_up[l].astype(BF16),
            ffn_w_down[l].astype(BF16)]


def mixer_weights(l, gath, ln1_g, ln1_b, ln2_g, ln2_b):
    w_out = gath[1].reshape(D_MODEL, D_MODEL)
    wt = {"memw": gath[2].reshape(D_MODEL, 2 * MEM_WIDTH), "wout": w_out,
          "ln1_g": ln1_g[l:l + 1], "ln1_b": ln1_b[l:l + 1], "ln2_g": ln2_g[l:l + 1], "ln2_b": ln2_b[l:l + 1]}
    return wt, gath[0].reshape(D_MODEL, D_MODEL)


def ffn_weights(l, wup_g, wdown_g, small, conv_b):
    pad_c = FF_BLOCK_PAD - FF_BLOCK
    wup = jnp.pad(wup_g, ((0, 0), (0, 0), (0, pad_c))).transpose(1, 0, 2).reshape(D_MODEL, N_DEV * FF_BLOCK_PAD)
    wdown = jnp.pad(wdown_g.reshape(FF_PAIRS, FF_BLOCK, D_MODEL), ((0, 0), (0, pad_c), (0, 0)))
    wdown = wdown.reshape(FF_PAIRS * FF_BLOCK_PAD, D_MODEL)
    cb = jnp.pad(conv_b[l].reshape(N_DEV, FF_BLOCK), ((0, 0), (0, pad_c)))
    cw = jnp.concatenate([small[:, 3 * l:3 * l + 3, :], cb[:, None, :], jnp.zeros((N_DEV, 4, FF_BLOCK_PAD), F32)], axis=1)
    return {"wup": wup, "wdown": wdown, "cw": cw}


def mixer_grad_blocks(g, w_in_grad):
    blocks = [] if w_in_grad is None else [w_in_grad.reshape(N_DEV, 128, D_MODEL)]
    blocks += [g["wout"].reshape(N_DEV, 128, D_MODEL), g["memw"].reshape(N_DEV, 128, 2 * MEM_WIDTH)]
    return [b.astype(BF16) for b in blocks]


def ffn_grad_blocks(g):
    wup = g["wup"][:, :, :FF_BLOCK]
    wdown = g["wdown"].reshape(FF_PAIRS, FF_BLOCK_PAD, D_MODEL)[:, :FF_BLOCK].reshape(N_DEV, FF_ROWS, D_MODEL)
    return [wup.astype(BF16), wdown.astype(BF16)]


def small_grad_blocks(g0, g1):
    taps = jnp.stack([g0["cw"][:, :3, :], g1["cw"][:, :3, :]], axis=1).reshape(N_DEV, DEPTH * 3, FF_BLOCK_PAD)
    small = jnp.zeros((N_DEV, 16, FF_BLOCK_PAD), F32)
    small = lax.dynamic_update_slice(small, taps, (0, 0, 0))
    return lax.dynamic_update_slice(small, g0["pscale"].reshape(N_DEV, 1, 96), (0, 8, 0))


def replicated_grads(g0, g1):
    pw = jnp.stack([g0["pw_full"][k * POOL_GROUP:(k + 1) * POOL_GROUP, k * POOL_GROUP:(k + 1) * POOL_GROUP] for k in range(4)])
    conv_b = jnp.stack([g_["cw"][:, 3, :FF_BLOCK].reshape(N_DEV * FF_BLOCK) for g_ in (g0, g1)])
    ln = [jnp.concatenate([g0[n], g1[n]], axis=0) for n in ("ln1_g", "ln1_b", "ln2_g", "ln2_b")]
    return pack_replicated(pw[None], ln[0], ln[1], ln[2], ln[3], conv_b, g1["fb"][0, :FOX_HEADS])


def kernel(x, mem, a_w_in, a_pool_w, a_pool_scale, a_w_out, b_w_q, b_w_out, kv_w, f_b, mem_w_kv, ln1_g, ln1_b, ln2_g, ln2_b, ffn_w_up, ffn_conv_w, ffn_conv_b, ffn_w_down, loss_target, m_a_w_in, m_a_pool_w, m_a_pool_scale, m_a_w_out, m_b_w_q, m_b_w_out, m_kv_w, m_f_b, m_mem_w_kv, m_ln1_g, m_ln1_b, m_ln2_g, m_ln2_b, m_ffn_w_up, m_ffn_conv_w, m_ffn_conv_b, m_ffn_w_down, v_a_w_in, v_a_pool_w, v_a_pool_scale, v_a_w_out, v_b_w_q, v_b_w_out, v_kv_w, v_f_b, v_mem_w_kv, v_ln1_g, v_ln1_b, v_ln2_g, v_ln2_b, v_ffn_w_up, v_ffn_conv_w, v_ffn_conv_b, v_ffn_w_down):
    nb, s, d = x.shape
    t = nb * s
    x2d, mem_bf, target = x.reshape(t, d), mem.reshape(nb * MEM_LEN, d).astype(BF16), loss_target.reshape(t, d)

    shards0 = layer_shards(0, a_w_in, a_w_out, mem_w_kv, ffn_w_up, ffn_w_down)
    shards1 = layer_shards(1, b_w_q, b_w_out, mem_w_kv, ffn_w_up, ffn_w_down)
    shards1.append(jnp.pad(kv_w, ((0, 0), (0, KV_COLS_PAD - KV_COLS))).astype(BF16))
    gath0 = exchange(shards0[:3] + [pack_small(ffn_conv_w, a_pool_scale)], [False] * 4, "gather_w0_mixer")
    pending = {"ffn0": exchange_start(shards0[3:], [False] * 2, gath0[0], "gather_w0_ffn_start")}
    small = gath0[3]
    wt0, w_in = mixer_weights(0, gath0, ln1_g + pending["ffn0"]["token"], ln1_b, ln2_g, ln2_b)
    pw_bd = _block_diag(a_pool_w[0])
    wt0.update(win_a=w_in, pw_bd=pw_bd.astype(BF16),
               pscale=small[:, 8, :96].reshape(1, TOK_WIDTH) + pending["ffn0"]["token"])

    def ffn0_weights(x1_bf):
        got = exchange_wait(pending["ffn0"], x1_bf, "gather_w0_ffn_wait")
        pending["w1"] = exchange_start(shards1, [False] * 6, got[0], "gather_w1_start")
        w = ffn_weights(0, got[0], got[1], small, ffn_conv_b)
        w["cw"] = w["cw"] + pending["w1"]["token"]
        return w

    x1, x1_bf, sv0 = forward_layer(0, x2d, x2d, mem_bf, wt0, nb, s, ffn_weights=ffn0_weights)
    gath1 = exchange_wait(pending["w1"], x1_bf, "gather_w1_wait")
    wt1, w_q = mixer_weights(1, gath1, ln1_g, ln1_b, ln2_g, ln2_b)
    wt1.update(ffn_weights(1, gath1[3], gath1[4], small, ffn_conv_b))
    kvw = gath1[5].reshape(D_MODEL, KV_COLS_PAD)
    wt1.update(wq=w_q, kvw=kvw,
               fb=jnp.pad(f_b.reshape(1, FOX_HEADS), ((0, 0), (0, LANES - FOX_HEADS))))
    y, _, sv1 = forward_layer(1, x1, x1_bf, mem_bf, wt1, nb, s)
    dy, loss_row = loss_head(y, target)
    loss = lax.psum(loss_row[0, 0], ("x", "y", "c"))

    dx1, dx1_rest, g1 = backward_layer(1, dy, sv1, mem_bf, wt1, nb, s)
    blocks1 = (mixer_grad_blocks(g1, g1["wq"]) + ffn_grad_blocks(g1)
               + [g1["kvw"][:, :KV_COLS].reshape(N_DEV, 128, KV_COLS).astype(BF16)])
    pending["g1"] = exchange_start(blocks1, [True] * 6, dx1, "scatter_g1_start")
    wt0["ln2_g"] = wt0["ln2_g"] + pending["g1"]["token"]

    def after_ffn0(g, dxm):
        pending["gf0"] = exchange_start(ffn_grad_blocks(g), [True] * 2, dxm, "scatter_g0_ffn_start")
        return pending["gf0"]["token"]

    def after_pool0(g, x):
        blocks = mixer_grad_blocks(g, None) + [small_grad_blocks(g, g1), replicated_grads(g, g1)]
        pending["gm0"] = exchange_start(blocks, [True] * 3 + [False], x, "scatter_g0_mixer_start")
        return pending["gm0"]["token"]

    grad_x, _, g0 = backward_layer(0, dx1, sv0, mem_bf, wt0, nb, s, after_ffn=after_ffn0, after_pool=after_pool0,
                                   dy_product=dx1_rest)
    pending["gin"] = exchange_start([g0["win_a"].reshape(N_DEV, 128, D_MODEL).astype(BF16)], [True], grad_x,
                                    "scatter_g0_in_start")
    parts_f0 = exchange_wait(pending["gf0"], jnp.zeros((8, LANES), F32) + pending["gin"]["token"], "scatter_g0_ffn_wait")
    parts1 = exchange_wait(pending["g1"], parts_f0[0], "scatter_g1_wait")

    res = {}

    def upd(nm, parts, w2, m2, v2):
        res[nm] = reduce_adamw(parts, w2, m2, v2, f"adamw_{nm}")

    upd("b_w_q", [parts1[0]], b_w_q, m_b_w_q, v_b_w_q)
    upd("b_w_out", [parts1[1]], b_w_out, m_b_w_out, v_b_w_out)
    upd("kv_w", [parts1[5]], kv_w[None], m_kv_w[None], v_kv_w[None])
    upd("ffn_w_up", [parts_f0[0], parts1[3]], ffn_w_up, m_ffn_w_up, v_ffn_w_up)
    upd("ffn_w_down", [parts_f0[1], parts1[4]], ffn_w_down, m_ffn_w_down, v_ffn_w_down)
    parts_m0 = exchange_wait(pending["gm0"], res["ffn_w_down"][0], "scatter_g0_mixer_wait")
    parts_in = exchange_wait(pending["gin"], parts_m0[0], "scatter_g0_in_wait")
    upd("a_w_in", [parts_in[0]], a_w_in, m_a_w_in, v_a_w_in)
    upd("a_w_out", [parts_m0[0]], a_w_out, m_a_w_out, v_a_w_out)
    upd("mem_w_kv", [parts_m0[1], parts1[2]], mem_w_kv, m_mem_w_kv, v_mem_w_kv)
    upd("small", [parts_m0[2]], pack_small(ffn_conv_w, a_pool_scale)[None], pack_small(m_ffn_conv_w, m_a_pool_scale)[None],
        pack_small(v_ffn_conv_w, v_a_pool_scale)[None])
    upd("replicated", [parts_m0[3]], pack_replicated(a_pool_w, ln1_g, ln1_b, ln2_g, ln2_b, ffn_conv_b, f_b)[None],
        pack_replicated(m_a_pool_w, m_ln1_g, m_ln1_b, m_ln2_g, m_ln2_b, m_ffn_conv_b, m_f_b)[None],
        pack_replicated(v_a_pool_w, v_ln1_g, v_ln1_b, v_ln2_g, v_ln2_b, v_ffn_conv_b, v_f_b)[None])

    res["kv_w"] = [o[0] for o in res["kv_w"]]
    res["ffn_conv_w"] = [o[0, :DEPTH * 3, :FF_BLOCK].reshape(DEPTH, 3, FF_BLOCK) for o in res["small"]]
    res["a_pool_scale"] = [o[0, 8:9, :96] for o in res["small"]]
    rep_names = ["a_pool_w", "ln1_g", "ln1_b", "ln2_g", "ln2_b", "ffn_conv_b", "f_b"]
    for nm in rep_names:
        res[nm] = []
    for o in res["replicated"]:
        for nm, val in zip(rep_names, unpack_replicated(o[0])):
            res[nm].append(val)

    order = ["a_w_in", "a_pool_w", "a_pool_scale", "a_w_out", "b_w_q", "b_w_out", "kv_w", "f_b", "mem_w_kv",
             "ln1_g", "ln1_b", "ln2_g", "ln2_b", "ffn_w_up", "ffn_conv_w", "ffn_conv_b", "ffn_w_down"]
    out = [loss, grad_x.reshape(nb, s, d)]
    for kind in range(4):
        out.extend(res[nm][kind] for nm in order)
    return tuple(out)
```

```python
import jax
import jax.numpy as jnp
from jax import lax
from jax.experimental import pallas as pl
from jax.experimental.pallas import tpu as pltpu

F32 = jnp.float32
BF16 = jnp.bfloat16
SDS = jax.ShapeDtypeStruct

N_DEV = 8
D_MODEL = 1024
TOK_WIDTH = 768
MEM_WIDTH = 256
MEM_LEN = 256
MEM_HEADS = 4
HEAD_DIM = 64
FOX_HEADS = 12
POOL_GROUP = 192
FF_BLOCK = 688
FF_BLOCK_PAD = 768
FF_PAIRS = 4
FF_ROWS = 344
KV_COLS = 1548
KV_COLS_PAD = 1664
LANES = 128
DEPTH = 2
DN_ALPHA = (2.0 * DEPTH) ** 0.25
LN_EPS = 1e-5
QK_SCALE = HEAD_DIM ** -0.5
NEG_BIG = -1e30

ADAM_LR = 0.001
ADAM_B1 = 0.9
ADAM_B2 = 0.999
ADAM_EPS = 1e-08
ADAM_WD = 0.01
ADAM_STEP = 10

VMEM_LIMIT_BYTES = 56 * 1024 * 1024
MM_BLOCK_BYTES = 6 * 1024 * 1024
TM = 512
TS = 256
TF = 256
TC = 256
HALO_POOL = 16
HALO_CONV = 8

NT_DIMS = (((1,), (1,)), ((), ()))
TN_DIMS = (((0,), (0,)), ((), ()))


def _params(sem=None):
    return pltpu.CompilerParams(dimension_semantics=sem, vmem_limit_bytes=VMEM_LIMIT_BYTES)


def _sigmoid(z):
    return 1.0 / (1.0 + jnp.exp(-z))


def _pick_tn(n):
    if n <= 2048:
        return n
    for t in (1024, 768, 512, 256, 128):
        if n % t == 0:
            return t
    return n


def mm_nn(a, b, out_dtype, name, addend=None, add_scale=1.0, also_bf16=False, trans_b=None):
    m, k = a.shape
    n = b.shape[1] if trans_b is None else b.shape[0]
    tm = min(TM, m)
    tn = n
    while k * tn * 2 > MM_BLOCK_BYTES or tm * tn * 4 > MM_BLOCK_BYTES:
        tn //= 2
    chunk = tn if tn <= 2048 else _pick_tn(tn)
    has_add = addend is not None

    def body(*refs):
        a_ref, b_ref = refs[0], refs[1]
        c_ref = refs[2] if has_add else None
        o_ref = refs[3] if has_add else refs[2]
        ob_ref = refs[-1] if also_bf16 else None
        av = a_ref[...].astype(BF16)
        for c in range(tn // chunk):
            cols = slice(c * chunk, (c + 1) * chunk)
            if trans_b is None:
                r = jnp.dot(av, b_ref[:, cols].astype(BF16), preferred_element_type=F32)
            else:
                r = lax.dot_general(av, b_ref[cols, :].astype(BF16), NT_DIMS, preferred_element_type=F32)
            if has_add:
                r = r + add_scale * c_ref[:, cols]
            o_ref[:, cols] = r.astype(out_dtype)
            if also_bf16:
                ob_ref[:, cols] = r.astype(BF16)

    b_spec = (pl.BlockSpec((k, tn), lambda j, i: (0, j)) if trans_b is None
              else pl.BlockSpec((tn, k), lambda j, i: (j, trans_b)))
    in_specs = [pl.BlockSpec((tm, k), lambda j, i: (i, 0)), b_spec]
    ops = [a, b]
    tile = pl.BlockSpec((tm, tn), lambda j, i: (i, j))
    if has_add:
        in_specs.append(tile)
        ops.append(addend)
    out_shape = [SDS((m, n), out_dtype)]
    out_specs = [tile]
    if also_bf16:
        out_shape.append(SDS((m, n), BF16))
        out_specs.append(tile)
    res = pl.pallas_call(
        body, name=name, grid=(n // tn, m // tm), in_specs=in_specs, out_specs=out_specs, out_shape=out_shape,
        compiler_params=_params(("parallel", "parallel")))(*ops)
    return tuple(res) if also_bf16 else res[0]


def mm_tn(a, b, name, blocked=False):
    t, m = a.shape
    _, n = b.shape
    tt = min(4 * TM, t)
    tm = 1024 if m % 1024 == 0 else m
    tn = FF_BLOCK_PAD if blocked else _pick_tn(n)
    nt = t // tt

    def body(a_ref, b_ref, o_ref):
        kk = pl.program_id(2)
        r = lax.dot_general(a_ref[...].astype(BF16), b_ref[...].astype(BF16), TN_DIMS, preferred_element_type=F32)
        if blocked:
            r = r[None]

        @pl.when(kk == 0)
        def _():
            o_ref[...] = r

        @pl.when(kk != 0)
        def _():
            o_ref[...] += r

    if blocked:
        out_shape = SDS((n // tn, m, tn), F32)
        out_spec = pl.BlockSpec((1, tm, tn), lambda i, j, kk: (j, i, 0))
    else:
        out_shape = SDS((m, n), F32)
        out_spec = pl.BlockSpec((tm, tn), lambda i, j, kk: (i, j))
    return pl.pallas_call(
        body, name=name, grid=(m // tm, n // tn, nt),
        in_specs=[pl.BlockSpec((tt, tm), lambda i, j, kk: (kk, i)), pl.BlockSpec((tt, tn), lambda i, j, kk: (kk, j))],
        out_specs=out_spec, out_shape=out_shape,
        compiler_params=_params(("parallel", "parallel", "arbitrary")))(a, b)


def ln_fwd(xprev, a, w, g, b, name):
    t, d = xprev.shape
    k = a.shape[1]
    tm = min(TM, t)

    def body(xp_ref, a_ref, w_ref, g_ref, b_ref, y_ref, yb_ref, xh_ref, rs_ref):
        r = DN_ALPHA * xp_ref[...] + jnp.dot(a_ref[...], w_ref[...], preferred_element_type=F32)
        mu = jnp.mean(r, axis=1, keepdims=True)
        xc = r - mu
        var = jnp.mean(xc * xc, axis=1, keepdims=True)
        rstd = lax.rsqrt(var + LN_EPS)
        xh = xc * rstd
        y = xh * g_ref[...] + b_ref[...]
        y_ref[...] = y
        yb_ref[...] = y.astype(BF16)
        xh_ref[...] = xh
        rs_ref[...] = jnp.broadcast_to(rstd, (tm, LANES))

    row = pl.BlockSpec((tm, d), lambda i: (i, 0))
    vec = pl.BlockSpec((1, d), lambda i: (0, 0))
    return pl.pallas_call(
        body, name=name, grid=(t // tm,),
        in_specs=[row, pl.BlockSpec((tm, k), lambda i: (i, 0)), pl.BlockSpec((k, d), lambda i: (0, 0)), vec, vec],
        out_specs=[row, row, row, pl.BlockSpec((tm, LANES), lambda i: (i, 0))],
        out_shape=[SDS((t, d), F32), SDS((t, d), BF16), SDS((t, d), F32), SDS((t, LANES), F32)],
        compiler_params=_params(("parallel",)))(xprev, a, w, g, b)


def ln_bwd(dy, xhat, rstd, g, name, product=None, dy_scale=1.0):
    t, d = dy.shape
    tm = min(TM, t)
    fused = product is not None

    def body(*refs):
        if fused:
            a_ref, w_ref = refs[0], refs[1]
            refs = refs[2:]
        dy_ref, xh_ref, rs_ref, g_ref, dr_ref, drb_ref, dg_ref, db_ref = refs
        i = pl.program_id(0)
        dyv = dy_ref[...] if dy_scale == 1.0 else dy_scale * dy_ref[...]
        if fused:
            dyv = dyv + lax.dot_general(a_ref[...], w_ref[...], NT_DIMS, preferred_element_type=F32)
        xh = xh_ref[...]
        dxh = dyv * g_ref[...]
        m1 = jnp.mean(dxh, axis=1, keepdims=True)
        m2 = jnp.mean(dxh * xh, axis=1, keepdims=True)
        dr = rs_ref[:, 0:1] * (dxh - m1 - xh * m2)
        dr_ref[...] = dr
        drb_ref[...] = dr.astype(BF16)

        @pl.when(i == 0)
        def _():
            dg_ref[...] = jnp.zeros_like(dg_ref)
            db_ref[...] = jnp.zeros_like(db_ref)

        dg_ref[...] += jnp.sum(dyv * xh, axis=0, keepdims=True)
        db_ref[...] += jnp.sum(dyv, axis=0, keepdims=True)

    row = pl.BlockSpec((tm, d), lambda i: (i, 0))
    vec = pl.BlockSpec((1, d), lambda i: (0, 0))
    in_specs = [row, row, pl.BlockSpec((tm, LANES), lambda i: (i, 0)), vec]
    ops = [dy, xhat, rstd, g]
    if fused:
        k = product[0].shape[1]
        col = product[2]
        in_specs = [pl.BlockSpec((tm, k), lambda i: (i, 0)), pl.BlockSpec((d, k), lambda i: (0, col))] + in_specs
        ops = list(product[:2]) + ops
    return pl.pallas_call(
        body, name=name, grid=(t // tm,), in_specs=in_specs, out_specs=[row, row, vec, vec],
        out_shape=[SDS((t, d), F32), SDS((t, d), BF16), SDS((1, d), F32), SDS((1, d), F32)],
        compiler_params=_params(("arbitrary",)))(*ops)


def loss_head(y, target):
    t, d = y.shape
    tm = min(TM, t)
    nsteps = t // tm

    def body(y_ref, t_ref, dy_ref, l_ref, acc):
        i = pl.program_id(0)
        diff = y_ref[...] - t_ref[...]
        dy_ref[...] = diff * (1.0 / d)

        @pl.when(i == 0)
        def _():
            acc[...] = jnp.zeros_like(acc)

        acc[...] += jnp.sum(diff * diff, axis=0, keepdims=True)

        @pl.when(i == nsteps - 1)
        def _():
            tot = jnp.sum(acc[...], axis=1, keepdims=True) * (0.5 / d)
            l_ref[...] = jnp.broadcast_to(tot, (1, LANES))

    row = pl.BlockSpec((tm, d), lambda i: (i, 0))
    return pl.pallas_call(
        body, name="loss_head", grid=(nsteps,), in_specs=[row, row],
        out_specs=[row, pl.BlockSpec((1, LANES), lambda i: (0, 0))],
        out_shape=[SDS((t, d), F32), SDS((1, LANES), F32)],
        scratch_shapes=[pltpu.VMEM((1, d), F32)],
        compiler_params=_params(("arbitrary",)))(y, target)


def memattn_fwd(proj, memkv, nb, s, name):
    ts = min(TS, s)
    nq = s // ts

    def body(q_ref, kv_ref, o_ref):
        top = lax.broadcasted_iota(jnp.int32, (PAIR, ts), 0) < HEAD_DIM
        scores = []
        for p in range(MEM_HEADS // 2):
            qp = q_ref[:, p * PAIR:(p + 1) * PAIR].astype(BF16)
            ke, ko = _split_pair(kv_ref[:, p * PAIR:(p + 1) * PAIR], QK_SCALE)
            scores.append([lax.dot_general(km, qp, NT_DIMS, preferred_element_type=F32) for km in (ke, ko)])
        for p in range(MEM_HEADS // 2):
            vt = kv_ref[:, MEM_WIDTH + p * PAIR:MEM_WIDTH + (p + 1) * PAIR].astype(F32).T.astype(BF16)
            outs = []
            for sc in scores[p]:
                e = jnp.exp(sc - jnp.max(sc, axis=0, keepdims=True))
                pr = e / jnp.sum(e, axis=0, keepdims=True)
                outs.append(jnp.dot(vt, pr.astype(BF16), preferred_element_type=F32))
            o_ref[:, p * PAIR:(p + 1) * PAIR] = jnp.where(top, outs[0], outs[1]).T.astype(BF16)

    return pl.pallas_call(
        body, name=name, grid=(nb, nq),
        in_specs=[pl.BlockSpec((ts, MEM_WIDTH), lambda b, i: (b * nq + i, 3)),
                  pl.BlockSpec((MEM_LEN, 2 * MEM_WIDTH), lambda b, i: (b, 0))],
        out_specs=pl.BlockSpec((ts, MEM_WIDTH), lambda b, i: (b * nq + i, 0)),
        out_shape=SDS((nb * s, MEM_WIDTH), BF16),
        compiler_params=_params(("parallel", "parallel")))(proj, memkv)


def memattn_bwd(proj, memkv, dcat, nb, s, name):
    ts = min(TS, s)
    nq = s // ts

    def body(q_ref, kv_ref, do_ref, dq_ref, dkv_ref):
        i = pl.program_id(1)

        @pl.when(i == 0)
        def _():
            dkv_ref[...] = jnp.zeros_like(dkv_ref)

        lo = _half_masks(MEM_LEN)
        top = lax.broadcasted_iota(jnp.int32, (PAIR, ts), 0) < HEAD_DIM
        n_pairs = MEM_HEADS // 2
        qs, dos, kps, products = [], [], [], []
        for p in range(n_pairs):
            qp = q_ref[:, p * PAIR:(p + 1) * PAIR].astype(BF16)
            dop = do_ref[:, p * PAIR:(p + 1) * PAIR].astype(BF16)
            kp = kv_ref[:, p * PAIR:(p + 1) * PAIR] * QK_SCALE
            kms = _split_pair(kp)
            vms = _split_pair(kv_ref[:, MEM_WIDTH + p * PAIR:MEM_WIDTH + (p + 1) * PAIR])
            products.append([(lax.dot_general(km, qp, NT_DIMS, preferred_element_type=F32),
                              lax.dot_general(vm, dop, NT_DIMS, preferred_element_type=F32)) for km, vm in zip(kms, vms)])
            qs.append(qp)
            dos.append(dop)
            kps.append(kp)
        for p in range(n_pairs):
            kt = kps[p].astype(F32).T.astype(BF16)
            dks, dvs, dqs = [], [], []
            for sc, dp in products[p]:
                e = jnp.exp(sc - jnp.max(sc, axis=0, keepdims=True))
                pr = e / jnp.sum(e, axis=0, keepdims=True)
                dl = jnp.sum(pr * dp, axis=0, keepdims=True)
                ds = (pr * (dp - dl)).astype(BF16)
                dvs.append(jnp.dot(pr.astype(BF16), dos[p], preferred_element_type=F32))
                dks.append(jnp.dot(ds, qs[p], preferred_element_type=F32))
                dqs.append(jnp.dot(kt, ds, preferred_element_type=F32))
            dq_ref[:, p * PAIR:(p + 1) * PAIR] = jnp.where(top, dqs[0], dqs[1]).T.astype(BF16)
            dkv_ref[:, p * PAIR:(p + 1) * PAIR] += jnp.where(lo, dks[0], dks[1]) * QK_SCALE
            dkv_ref[:, MEM_WIDTH + p * PAIR:MEM_WIDTH + (p + 1) * PAIR] += jnp.where(lo, dvs[0], dvs[1])

    return pl.pallas_call(
        body, name=name, grid=(nb, nq),
        in_specs=[pl.BlockSpec((ts, MEM_WIDTH), lambda b, i: (b * nq + i, 3)),
                  pl.BlockSpec((MEM_LEN, 2 * MEM_WIDTH), lambda b, i: (b, 0)),
                  pl.BlockSpec((ts, MEM_WIDTH), lambda b, i: (b * nq + i, 3))],
        out_specs=[pl.BlockSpec((ts, MEM_WIDTH), lambda b, i: (b * nq + i, 0)),
                   pl.BlockSpec((MEM_LEN, 2 * MEM_WIDTH), lambda b, i: (b, 0))],
        out_shape=[SDS((nb * s, MEM_WIDTH), BF16), SDS((nb * MEM_LEN, 2 * MEM_WIDTH), F32)],
        compiler_params=_params(("parallel", "arbitrary")))(proj, memkv, dcat)


def _pool_select(shape, s2, s4, s8, s16):
    lane = lax.broadcasted_iota(jnp.int32, shape, 1)
    return jnp.where(lane < POOL_GROUP, s2, jnp.where(lane < 2 * POOL_GROUP, s4, jnp.where(lane < 3 * POOL_GROUP, s8, s16)))


def _pool_count(shape, first_pos):
    pos = first_pos + lax.broadcasted_iota(jnp.int32, shape, 0)
    win = _pool_select(shape, 2, 4, 8, 16)
    return jnp.minimum(pos + 1, win).astype(F32)


def pool_fwd(proj, pw_bd, pscale, nb, s):
    ts = min(TS, s)
    nq = s // ts
    w = TOK_WIDTH

    def body(c_ref, h_ref, w_ref, sc_ref, pooled_ref, tok_ref):
        i = pl.program_id(0) % nq
        cur = c_ref[...]
        halo = jnp.where(i == 0, 0.0, h_ref[...])
        xe = jnp.concatenate([halo, cur], axis=0)
        s2 = xe + pltpu.roll(xe, 1, axis=0)
        s4 = s2 + pltpu.roll(s2, 2, axis=0)
        s8 = s4 + pltpu.roll(s4, 4, axis=0)
        s16 = s8 + pltpu.roll(s8, 8, axis=0)
        hp = HALO_POOL
        ws = _pool_select((ts, w), s2[hp:], s4[hp:], s8[hp:], s16[hp:])
        pooled = (ws / _pool_count((ts, w), i * ts) - cur).astype(BF16)
        pooled_ref[...] = pooled
        mixed = jnp.dot(pooled, w_ref[...], preferred_element_type=F32)
        tok_ref[...] = (mixed * sc_ref[...]).astype(BF16)

    row = pl.BlockSpec((ts, w), lambda r: (r, 0))
    return pl.pallas_call(
        body, name="pool_fwd", grid=(nb * nq,),
        in_specs=[row, pl.BlockSpec((HALO_POOL, w), lambda r: (jnp.maximum(r * (ts // HALO_POOL) - 1, 0), 0)),
                  pl.BlockSpec((w, w), lambda r: (0, 0)), pl.BlockSpec((1, w), lambda r: (0, 0))],
        out_specs=[row, row], out_shape=[SDS((nb * s, w), BF16), SDS((nb * s, w), BF16)],
        compiler_params=_params(("parallel",)))(proj, proj, pw_bd, pscale)


def pool_bwd_mix(dcat, pooled, pw_bd, pscale, nb, s):
    ts = min(TS, s)
    w = TOK_WIDTH

    def body(dt_ref, p_ref, w_ref, sc_ref, dm_ref, dp_ref, ds_ref):
        r = pl.program_id(0)
        dtok = dt_ref[...]
        mixed = jnp.dot(p_ref[...], w_ref[...], preferred_element_type=F32)

        @pl.when(r == 0)
        def _():
            ds_ref[...] = jnp.zeros_like(ds_ref)

        ds_ref[...] += jnp.sum(dtok * mixed, axis=0, keepdims=True)
        dmx = (dtok * sc_ref[...]).astype(BF16)
        dm_ref[...] = dmx
        dp_ref[...] = lax.dot_general(dmx, w_ref[...], NT_DIMS, preferred_element_type=F32)

    row = pl.BlockSpec((ts, w), lambda r: (r, 0))
    mat = pl.BlockSpec((w, w), lambda r: (0, 0))
    vec = pl.BlockSpec((1, w), lambda r: (0, 0))
    return pl.pallas_call(
        body, name="pool_bwd_mix", grid=(nb * s // ts,), in_specs=[row, row, mat, vec],
        out_specs=[row, row, vec], out_shape=[SDS((nb * s, w), BF16), SDS((nb * s, w), F32), SDS((1, w), F32)],
        compiler_params=_params(("arbitrary",)))(dcat, pooled, pw_bd, pscale)


def pool_bwd_window(dpooled, nb, s):
    ts = min(TS, s)
    nq = s // ts
    w = TOK_WIDTH
    n_ext = ts + HALO_POOL
    n_halo_blocks = nb * s // HALO_POOL

    def body(c_ref, n_ref, du_ref):
        i = pl.program_id(0) % nq
        cur = c_ref[...]
        nxt = jnp.where(i == nq - 1, 0.0, n_ref[...])
        ze = jnp.concatenate([cur, nxt], axis=0) / _pool_count((n_ext, w), i * ts)
        s2 = ze + pltpu.roll(ze, n_ext - 1, axis=0)
        s4 = s2 + pltpu.roll(s2, n_ext - 2, axis=0)
        s8 = s4 + pltpu.roll(s4, n_ext - 4, axis=0)
        s16 = s8 + pltpu.roll(s8, n_ext - 8, axis=0)
        ws = _pool_select((ts, w), s2[:ts], s4[:ts], s8[:ts], s16[:ts])
        du_ref[...] = (ws - cur).astype(BF16)

    row = pl.BlockSpec((ts, w), lambda r: (r, 0))
    return pl.pallas_call(
        body, name="pool_bwd_window", grid=(nb * nq,),
        in_specs=[row, pl.BlockSpec((HALO_POOL, w),
                                    lambda r: (jnp.minimum((r + 1) * (ts // HALO_POOL), n_halo_blocks - 1), 0))],
        out_specs=row, out_shape=SDS((nb * s, w), BF16),
        compiler_params=_params(("parallel",)))(dpooled, dpooled)


def _conv_rows(xe, w_ref):
    return (w_ref[0, 2:3, :] * xe + w_ref[0, 1:2, :] * pltpu.roll(xe, 1, axis=0)
            + w_ref[0, 0:1, :] * pltpu.roll(xe, 2, axis=0) + w_ref[0, 3:4, :])


def ffn_up_gate(x_bf, wup, cw, nb, s, name):
    tm = min(TM, s)
    nq = s // tm
    w = FF_BLOCK_PAD
    hr = 2 * HALO_CONV
    k = x_bf.shape[1]

    def body(xc_ref, xh_ref, wu_ref, wg_ref, cu_ref, cg_ref, act_ref, a_ref, b_ref, hu_ref, hg_ref):
        first = (pl.program_id(1) % nq) == 0
        xc = xc_ref[...]
        xh = xh_ref[...]

        def products(w_ref):
            return (jnp.dot(xc, w_ref[...], preferred_element_type=F32), jnp.dot(xh, w_ref[...], preferred_element_type=F32))

        def conv(hcur, hprev, c_ref, h_out):
            h_out[...] = hcur.astype(BF16)
            xe = jnp.concatenate([jnp.where(first, 0.0, hprev), hcur], axis=0)
            return _conv_rows(xe, c_ref)[hr:]

        pu, pg = products(wu_ref), products(wg_ref)
        cu = conv(*pu, cu_ref, hu_ref)
        cg = conv(*pg, cg_ref, hg_ref)
        sg = _sigmoid(cg)
        a = cg * sg
        act_ref[...] = (a * cu).astype(BF16)
        a_ref[...] = a.astype(BF16)
        b_ref[...] = (cu * (sg * (1.0 + cg * (1.0 - sg)))).astype(BF16)

    def wblock(off):
        return pl.BlockSpec((k, w), lambda j, r: (0, j + off))

    def cblock(off):
        return pl.BlockSpec((1, 8, w), lambda j, r: (j + off, 0, 0))

    tile = pl.BlockSpec((tm, w), lambda j, r: (r, j))
    out = SDS((nb * s, FF_PAIRS * w), BF16)
    return pl.pallas_call(
        body, name=name, grid=(FF_PAIRS, nb * nq),
        in_specs=[pl.BlockSpec((tm, k), lambda j, r: (r, 0)),
                  pl.BlockSpec((hr, k), lambda j, r: (jnp.maximum(r * (tm // hr) - 1, 0), 0)),
                  wblock(0), wblock(FF_PAIRS), cblock(0), cblock(FF_PAIRS)],
        out_specs=[tile] * 5, out_shape=[out] * 5,
        compiler_params=_params(("parallel", "parallel")))(x_bf, x_bf, wup, wup, cw, cw)


def gate_conv_bwd(dact, a, b, hu, hg, cw, nb, s, name):
    ts = min(TS, s)
    nq = s // ts
    w = FF_BLOCK_PAD
    hc = HALO_CONV
    hb = 2 * hc
    n_ext = ts + hc

    def body(dc_ref, dn_ref, ac_ref, an_ref, bc_ref, bn_ref, hu_ref, hg_ref, wu_ref, wg_ref,
             dhu_ref, dhg_ref, dwu_ref, dwg_ref):
        r = pl.program_id(1)
        last = (r % nq) == nq - 1

        def ext(c_ref, n_ref, mask_next=False):
            nxt = n_ref[...].astype(F32)[:hc]
            if mask_next:
                nxt = jnp.where(last, 0.0, nxt)
            return jnp.concatenate([c_ref[...].astype(F32), nxt], axis=0)

        da = ext(dc_ref, dn_ref, mask_next=True)

        def branch(dcv, w_ref, h_ref, dh_ref, dw_ref):
            d0 = dcv[:ts]
            d1 = pltpu.roll(dcv, n_ext - 1, axis=0)[:ts]
            d2 = pltpu.roll(dcv, n_ext - 2, axis=0)[:ts]
            dh_ref[...] = (w_ref[0, 2:3, :] * d0 + w_ref[0, 1:2, :] * d1 + w_ref[0, 0:1, :] * d2).astype(BF16)
            hv = h_ref[...].astype(F32)
            rows = [jnp.sum(d2 * hv, axis=0, keepdims=True), jnp.sum(d1 * hv, axis=0, keepdims=True),
                    jnp.sum(d0 * hv, axis=0, keepdims=True), jnp.sum(d0, axis=0, keepdims=True)]
            sub = lax.broadcasted_iota(jnp.int32, (8, w), 0)
            upd = jnp.zeros((8, w), F32)
            for kk, rv in enumerate(rows):
                upd = jnp.where(sub == kk, rv, upd)

            @pl.when(r == 0)
            def _():
                dw_ref[...] = jnp.zeros_like(dw_ref)

            dw_ref[...] += upd[None]

        branch(da * ext(ac_ref, an_ref), wu_ref, hu_ref, dhu_ref, dwu_ref)
        branch(da * ext(bc_ref, bn_ref), wg_ref, hg_ref, dhg_ref, dwg_ref)

    cur = pl.BlockSpec((ts, w), lambda j, r: (r, j))
    nxt = pl.BlockSpec((hb, w), lambda j, r: (jnp.minimum((r + 1) * (ts // hb), nb * s // hb - 1), j))

    def wspec(off):
        return pl.BlockSpec((1, 8, w), lambda j, r: (j + off, 0, 0))

    p = FF_PAIRS
    dw_spec = pl.BlockSpec((1, 8, w), lambda j, r: (j, 0, 0))
    return pl.pallas_call(
        body, name=name, grid=(p, nb * nq),
        in_specs=[cur, nxt, cur, nxt, cur, nxt, cur, cur, wspec(0), wspec(p)],
        out_specs=[cur, cur, dw_spec, dw_spec],
        out_shape=[SDS((nb * s, p * w), BF16), SDS((nb * s, p * w), BF16), SDS((p, 8, w), F32), SDS((p, 8, w), F32)],
        compiler_params=_params(("parallel", "arbitrary")))(dact, dact, a, a, b, b, hu, hg, cw, cw)


def _tri(n, upper):
    r = lax.broadcasted_iota(jnp.int32, (n, n), 0)
    c = lax.broadcasted_iota(jnp.int32, (n, n), 1)
    return ((r <= c) if upper else (r >= c)).astype(F32)


def fgate_fwd(fl, fb, nb, s):
    tc = min(TC, s)
    nq = s // tc

    def body(fl_ref, fb_ref, f_ref, carry):
        @pl.when(pl.program_id(1) == 0)
        def _():
            carry[...] = jnp.zeros_like(carry)

        z = fl_ref[...] + fb_ref[...]
        logf = jnp.minimum(z, 0.0) - jnp.log(1.0 + jnp.exp(-jnp.abs(z)))
        f_ref[...] = jnp.dot(_tri(tc, False), logf, preferred_element_type=F32,
                             precision=lax.Precision.HIGHEST) + carry[...]
        carry[...] += jnp.sum(logf, axis=0, keepdims=True)

    row = pl.BlockSpec((tc, LANES), lambda b, i: (b * nq + i, 0))
    return pl.pallas_call(
        body, name="fgate_fwd", grid=(nb, nq), in_specs=[row, pl.BlockSpec((1, LANES), lambda b, i: (0, 0))],
        out_specs=row, out_shape=SDS((nb * s, LANES), F32), scratch_shapes=[pltpu.VMEM((1, LANES), F32)],
        compiler_params=_params(("arbitrary", "arbitrary")))(fl, fb)


def fgate_bwd(d_cum_q, d_cum_k, fl, fb, nb, s):
    tc = min(TC, s)
    nq = s // tc

    def body(dfq_ref, dfk_ref, fl_ref, fb_ref, dfl_ref, dfb_ref, carry):
        b = pl.program_id(0)
        i = pl.program_id(1)

        @pl.when(i == 0)
        def _():
            carry[...] = jnp.zeros_like(carry)

        @pl.when(jnp.logical_and(b == 0, i == 0))
        def _():
            dfb_ref[...] = jnp.zeros_like(dfb_ref)

        dfv = dfq_ref[...] + dfk_ref[...]
        dlog = jnp.dot(_tri(tc, True), dfv, preferred_element_type=F32,
                       precision=lax.Precision.HIGHEST) + carry[...]
        carry[...] += jnp.sum(dfv, axis=0, keepdims=True)
        z = fl_ref[...] + fb_ref[...]
        dfl = dlog / (1.0 + jnp.exp(z))
        dfl_ref[...] = dfl
        dfb_ref[...] += jnp.sum(dfl, axis=0, keepdims=True)

    row = pl.BlockSpec((tc, LANES), lambda b, i: (b * nq + nq - 1 - i, 0))
    vec = pl.BlockSpec((1, LANES), lambda b, i: (0, 0))
    return pl.pallas_call(
        body, name="fgate_bwd", grid=(nb, nq), in_specs=[row, row, row, vec], out_specs=[row, vec],
        out_shape=[SDS((nb * s, LANES), F32), SDS((1, LANES), F32)], scratch_shapes=[pltpu.VMEM((1, LANES), F32)],
        compiler_params=_params(("arbitrary", "arbitrary")))(d_cum_q, d_cum_k, fl, fb)


PAIR = 2 * HEAD_DIM
N_PAIRS = FOX_HEADS // 2


def _lane_put(shape, h, col):
    lane = lax.broadcasted_iota(jnp.int32, shape, 1)
    return jnp.where(lane == h, col, 0.0)


def _half_masks(rows):
    lane = lax.broadcasted_iota(jnp.int32, (rows, PAIR), 1)
    return lane < HEAD_DIM


def _split_pair(x, scale=None):
    if scale is not None:
        x = x * scale
    lo = _half_masks(x.shape[0])
    zero = jnp.zeros_like(x)
    return jnp.where(lo, x, zero), jnp.where(lo, zero, x)


def _to_tile_rows(a, nb, s, tf):
    return a.reshape(nb * s // tf, tf, LANES)[:, :, :16].transpose(0, 2, 1)


def _from_tile_rows(a):
    tiles, _, tf = a.shape
    return jnp.pad(a.transpose(0, 2, 1), ((0, 0), (0, 0), (0, LANES - 16))).reshape(tiles * tf, LANES)


BIAS_TERMS = 3
LOOKAHEAD = 4
FOLLOW_FWD = 1
LOOKAHEAD_BWD = 2
FOLLOW_BWD = 1


def _bias_lane(h):
    return HEAD_DIM if h % 2 == 0 else 0


def _placement():
    rows = jnp.arange(LANES)[:, None]
    cols = jnp.arange(FOX_HEADS * PAIR)[None, :]
    head, lane = cols // PAIR, cols % PAIR
    first = jnp.where(head % 2 == 0, HEAD_DIM, 0)
    term = lane - first
    hit = (term >= 0) & (term < BIAS_TERMS) & (rows == 16 * term + head)
    return hit.astype(BF16)


def fox_prep(kv, fneg, nb, s):
    tf = min(TF, s)
    w = TOK_WIDTH

    def body(k_ref, v_ref, f_ref, pl_ref, ka_ref, vt_ref):
        lane = lax.broadcasted_iota(jnp.int32, (tf, LANES), 1)
        lo = lane < HEAD_DIM
        f = jnp.where(lane < FOX_HEADS, f_ref[...], 0.0)
        hi = f.astype(BF16).astype(F32)
        mid = (f - hi).astype(BF16).astype(F32)
        low = (f - hi - mid).astype(BF16).astype(F32)
        terms = (hi + pltpu.roll(mid, 16, axis=1) + pltpu.roll(low, 32, axis=1)).astype(BF16)
        placed = jnp.dot(terms, pl_ref[...], preferred_element_type=F32).astype(BF16)
        one = jnp.ones((tf, LANES), BF16)
        zero = jnp.zeros((tf, LANES), BF16)
        for p in range(N_PAIRS):
            kp = k_ref[:, p * PAIR:(p + 1) * PAIR] * QK_SCALE
            vp = v_ref[:, p * PAIR:(p + 1) * PAIR]
            he, ho = 2 * p, 2 * p + 1
            ka_ref[:, he * PAIR:(he + 1) * PAIR] = jnp.where(lo, kp, placed[:, he * PAIR:(he + 1) * PAIR])
            ka_ref[:, ho * PAIR:(ho + 1) * PAIR] = jnp.where(lo, placed[:, ho * PAIR:(ho + 1) * PAIR], kp)
            ve = jnp.where(lo, vp, jnp.where(lane == HEAD_DIM, one, zero))
            vo = jnp.where(lo, jnp.where(lane == 0, one, zero), vp)
            vt_ref[0, he * PAIR:(he + 1) * PAIR, :] = ve.astype(F32).T.astype(BF16)
            vt_ref[0, ho * PAIR:(ho + 1) * PAIR, :] = vo.astype(F32).T.astype(BF16)

    return pl.pallas_call(
        body, name="fox_prep", grid=(nb * s // tf,),
        in_specs=[pl.BlockSpec((tf, w), lambda r: (r, 0)), pl.BlockSpec((tf, w), lambda r: (r, 1)),
                  pl.BlockSpec((tf, LANES), lambda r: (r, 0)), pl.BlockSpec((LANES, FOX_HEADS * PAIR), lambda r: (0, 0))],
        out_specs=[pl.BlockSpec((tf, FOX_HEADS * PAIR), lambda r: (r, 0)),
                   pl.BlockSpec((1, FOX_HEADS * PAIR, tf), lambda r: (r, 0, 0))],
        out_shape=[SDS((nb * s, FOX_HEADS * PAIR), BF16), SDS((nb * s // tf, FOX_HEADS * PAIR, tf), BF16)],
        compiler_params=_params(("parallel",)))(kv, kv, fneg, _placement())


def fox_fwd_t(pq, kaug, vaug_t, nb, s):
    tf = min(TF, s)
    n = s // tf
    w = TOK_WIDTH
    wa = FOX_HEADS * PAIR

    def body(q_ref, k_hbm, vt_hbm, ob_ref, of_ref, lse_ref, k_vm, vt_vm, qx_scr, m_scr, acc_scr, sems):
        b = pl.program_id(0)
        i = pl.program_id(1)

        @pl.when(i == 0)
        def _():
            ck = pltpu.make_async_copy(k_hbm.at[pl.ds(pl.multiple_of(b * s, tf), s)], k_vm, sems.at[0])
            cv = pltpu.make_async_copy(vt_hbm.at[pl.ds(b * n, n)], vt_vm, sems.at[1])
            ck.start()
            cv.start()
            ck.wait()
            cv.wait()

        lane = lax.broadcasted_iota(jnp.int32, (tf, PAIR), 1)
        one = jnp.ones((tf, PAIR), BF16)
        zero = jnp.zeros((tf, PAIR), BF16)
        for p in range(N_PAIRS):
            qp = q_ref[:, p * PAIR:(p + 1) * PAIR]
            be, bo = _bias_lane(2 * p), _bias_lane(2 * p + 1)
            ones_e = jnp.where((lane >= be) & (lane < be + BIAS_TERMS), one, zero)
            ones_o = jnp.where((lane >= bo) & (lane < bo + BIAS_TERMS), one, zero)
            qx_scr[2 * p] = jnp.where(lane < HEAD_DIM, qp, ones_e)
            qx_scr[2 * p + 1] = jnp.where(lane < HEAD_DIM, ones_o, qp)
        m_scr[...] = jnp.full(m_scr.shape, NEG_BIG, F32)
        acc_scr[...] = jnp.zeros_like(acc_scr)

        def tile(j, masked):
            ks = pl.multiple_of(j * tf, tf)
            if masked:
                keep = lax.broadcasted_iota(jnp.int32, (tf, tf), 1) >= lax.broadcasted_iota(jnp.int32, (tf, tf), 0)
            def scores(h):
                kx = k_vm[pl.ds(ks, tf), h * PAIR:(h + 1) * PAIR]
                return lax.dot_general(kx, qx_scr[h], NT_DIMS, preferred_element_type=F32)

            def values(h, pr, a):
                pv = jnp.dot(vt_vm[j, h * PAIR:(h + 1) * PAIR, :], pr, preferred_element_type=F32)
                acc_scr[h] = a * acc_scr[h] + pv

            ahead = [scores(h) for h in range(LOOKAHEAD)]
            behind = []
            for h in range(FOX_HEADS):
                sc = ahead.pop(0)
                if h + LOOKAHEAD < FOX_HEADS:
                    ahead.append(scores(h + LOOKAHEAD))
                if masked:
                    sc = jnp.where(keep, sc, NEG_BIG)
                m_prev = m_scr[h]
                m_new = jnp.maximum(m_prev, jnp.max(sc, axis=0, keepdims=True))
                m_scr[h] = m_new
                behind.append((h, jnp.exp(sc - m_new).astype(BF16), jnp.exp(m_prev - m_new)))
                if len(behind) > FOLLOW_FWD:
                    values(*behind.pop(0))
            for item in behind:
                values(*item)

        def step(j, carry):
            tile(j, False)
            return carry

        lax.fori_loop(0, i, step, 0)
        tile(i, True)

        top = lax.broadcasted_iota(jnp.int32, (PAIR, tf), 0) < HEAD_DIM
        sub = lax.broadcasted_iota(jnp.int32, (16, tf), 0)
        lse = jnp.zeros((16, tf), F32)
        for p in range(N_PAIRS):
            he, ho = 2 * p, 2 * p + 1
            le = acc_scr[he, HEAD_DIM:HEAD_DIM + 1, :]
            lod = acc_scr[ho, 0:1, :]
            o = jnp.where(top, acc_scr[he] / le, acc_scr[ho] / lod).T
            ob_ref[:, p * PAIR:(p + 1) * PAIR] = o.astype(BF16)
            of_ref[:, p * PAIR:(p + 1) * PAIR] = o
            lse = jnp.where(sub == he, m_scr[he] + jnp.log(le), lse)
            lse = jnp.where(sub == ho, m_scr[ho] + jnp.log(lod), lse)
        lse_ref[0] = lse

    qrow = lambda b, i: (b * n + i, 0)
    return pl.pallas_call(
        body, name="fox_fwd", grid=(nb, n),
        in_specs=[pl.BlockSpec((tf, w), qrow), ANY_SPEC, ANY_SPEC],
        out_specs=[pl.BlockSpec((tf, w), qrow), pl.BlockSpec((tf, w), qrow),
                   pl.BlockSpec((1, 16, tf), lambda b, i: (b * n + i, 0, 0))],
        out_shape=[SDS((nb * s, w), BF16), SDS((nb * s, w), F32), SDS((nb * n, 16, tf), F32)],
        scratch_shapes=[pltpu.VMEM((s, wa), BF16), pltpu.VMEM((n, wa, tf), BF16),
                        pltpu.VMEM((FOX_HEADS, tf, PAIR), BF16), pltpu.VMEM((FOX_HEADS, 1, tf), F32),
                        pltpu.VMEM((FOX_HEADS, PAIR, tf), F32), pltpu.SemaphoreType.DMA((2,))],
        compiler_params=_params(("arbitrary", "arbitrary")))(pq, kaug, vaug_t)


def fox_delta(dcat, o, nb, s):
    tf = min(TM, s)
    w = TOK_WIDTH

    def body(do_ref, o_ref, dl_ref):
        out = jnp.zeros((tf, LANES), F32)
        for h in range(FOX_HEADS):
            lo, hi = h * HEAD_DIM, (h + 1) * HEAD_DIM
            out = out + _lane_put((tf, LANES), h, jnp.sum(do_ref[:, lo:hi] * o_ref[:, lo:hi], axis=1, keepdims=True))
        dl_ref[...] = out

    row = pl.BlockSpec((tf, w), lambda r: (r, 0))
    return pl.pallas_call(
        body, name="fox_delta", grid=(nb * s // tf,), in_specs=[row, row],
        out_specs=pl.BlockSpec((tf, LANES), lambda r: (r, 0)), out_shape=SDS((nb * s, LANES), F32),
        compiler_params=_params(("parallel",)))(dcat, o)


def fox_bwd(pq, kv, fneg, dcat_bf, lse_rows, delta_rows, nb, s):
    tf = min(TF, s)
    n = s // tf
    w = TOK_WIDTH

    def body(q_hbm, k_ref, v_ref, f_ref, do_hbm, lse_ref, dl_ref, dq_ref, dk_ref, dv_ref, dfk_ref, dfq_ref,
             q_vm, do_vm, km_scr, vm_scr, kt_scr, fk_scr, dk_scr, dv_scr, rs_scr, dq_scr, fq_scr, sems):
        b = pl.program_id(0)
        j = pl.program_id(1)

        @pl.when(j == 0)
        def _():
            rows = pl.ds(pl.multiple_of(b * s, tf), s)
            cq = pltpu.make_async_copy(q_hbm.at[rows, pl.ds(0, w)], q_vm, sems.at[0])
            cd = pltpu.make_async_copy(do_hbm.at[rows, pl.ds(0, w)], do_vm, sems.at[1])
            cq.start()
            cd.start()
            dq_scr[...] = jnp.zeros_like(dq_scr)
            fq_scr[...] = jnp.zeros_like(fq_scr)
            cq.wait()
            cd.wait()

        for p in range(N_PAIRS):
            kp = k_ref[:, p * PAIR:(p + 1) * PAIR] * QK_SCALE
            ke, ko = _split_pair(kp)
            km_scr[2 * p] = ke
            km_scr[2 * p + 1] = ko
            kt_scr[p] = kp.astype(F32).T.astype(BF16)
            ve, vo = _split_pair(v_ref[:, p * PAIR:(p + 1) * PAIR])
            vm_scr[2 * p] = ve
            vm_scr[2 * p + 1] = vo
        for h in range(FOX_HEADS):
            fk_scr[h] = jnp.broadcast_to(f_ref[:, h:h + 1], (tf, tf))
        dk_scr[...] = jnp.zeros_like(dk_scr)
        dv_scr[...] = jnp.zeros_like(dv_scr)
        rs_scr[...] = jnp.zeros_like(rs_scr)

        def tile(i, masked):
            qs = pl.multiple_of(i * tf, tf)
            if masked:
                keep = lax.broadcasted_iota(jnp.int32, (tf, tf), 1) >= lax.broadcasted_iota(jnp.int32, (tf, tf), 0)
            def products(h):
                qp = q_vm[pl.ds(qs, tf), (h // 2) * PAIR:(h // 2 + 1) * PAIR]
                dop = do_vm[pl.ds(qs, tf), (h // 2) * PAIR:(h // 2 + 1) * PAIR]
                return (lax.dot_general(km_scr[h], qp, NT_DIMS, preferred_element_type=F32),
                        lax.dot_general(vm_scr[h], dop, NT_DIMS, preferred_element_type=F32))

            def dependents(h, prb, dsb):
                p = h // 2
                half = slice((h % 2) * HEAD_DIM, (h % 2 + 1) * HEAD_DIM)
                qp = q_vm[pl.ds(qs, tf), p * PAIR:(p + 1) * PAIR]
                dop = do_vm[pl.ds(qs, tf), p * PAIR:(p + 1) * PAIR]
                dv_scr[h] += jnp.dot(prb, dop, preferred_element_type=F32)
                dk_scr[h] += jnp.dot(dsb, qp, preferred_element_type=F32)
                dqt = jnp.dot(kt_scr[p], dsb, preferred_element_type=F32)
                dq_scr[i, p, half, :] += dqt[(h % 2) * HEAD_DIM:(h % 2 + 1) * HEAD_DIM]

            ahead = [products(h) for h in range(LOOKAHEAD_BWD)]
            behind = []
            for h in range(FOX_HEADS):
                sc, dp = ahead.pop(0)
                if h + LOOKAHEAD_BWD < FOX_HEADS:
                    ahead.append(products(h + LOOKAHEAD_BWD))
                sc = sc + fk_scr[h] - lse_ref[i, h:h + 1, :]
                if masked:
                    sc = jnp.where(keep, sc, NEG_BIG)
                pr = jnp.exp(sc)
                ds = pr * (dp - dl_ref[i, h:h + 1, :])
                part = ds[:, :LANES]
                for c in range(1, tf // LANES):
                    part = part + ds[:, c * LANES:(c + 1) * LANES]
                rs_scr[h] += part
                fq_scr[i, h:h + 1, :] += jnp.sum(ds, axis=0, keepdims=True)
                behind.append((h, pr.astype(BF16), ds.astype(BF16)))
                if len(behind) > FOLLOW_BWD:
                    dependents(*behind.pop(0))
            for item in behind:
                dependents(*item)

        def step(i, carry):
            tile(i, False)
            return carry

        tile(j, True)
        for p in range(N_PAIRS):
            dq_ref[:, p * PAIR:(p + 1) * PAIR] = dq_scr[j, p].T.astype(BF16)
        dfq_ref[0] = fq_scr[j]
        lax.fori_loop(j + 1, n, step, 0)

        lo = _half_masks(tf)
        dfk = jnp.zeros((tf, LANES), F32)
        for p in range(N_PAIRS):
            dk = jnp.where(lo, dk_scr[2 * p], dk_scr[2 * p + 1]) * QK_SCALE
            dk_ref[:, p * PAIR:(p + 1) * PAIR] = dk.astype(BF16)
            dv_ref[:, p * PAIR:(p + 1) * PAIR] = jnp.where(lo, dv_scr[2 * p], dv_scr[2 * p + 1]).astype(BF16)
            for h in (2 * p, 2 * p + 1):
                dfk = dfk - _lane_put((tf, LANES), h, jnp.sum(rs_scr[h], axis=1, keepdims=True))
        dfk_ref[...] = dfk

    krow = lambda b, j: (b * n + j, 0)
    rows = pl.BlockSpec((n, 16, tf), lambda b, j: (b, 0, 0))
    tile_out = pl.BlockSpec((tf, w), krow)
    return pl.pallas_call(
        body, name="fox_bwd", grid=(nb, n),
        in_specs=[ANY_SPEC, pl.BlockSpec((tf, w), krow), pl.BlockSpec((tf, w), lambda b, j: (b * n + j, 1)),
                  pl.BlockSpec((tf, LANES), krow), ANY_SPEC, rows, rows],
        out_specs=[tile_out, tile_out, tile_out, pl.BlockSpec((tf, LANES), krow),
                   pl.BlockSpec((1, 16, tf), lambda b, j: (b * n + j, 0, 0))],
        out_shape=[SDS((nb * s, w), BF16), SDS((nb * s, w), BF16), SDS((nb * s, w), BF16), SDS((nb * s, LANES), F32),
                   SDS((nb * n, 16, tf), F32)],
        scratch_shapes=[pltpu.VMEM((s, w), BF16), pltpu.VMEM((s, w), BF16),
                        pltpu.VMEM((FOX_HEADS, tf, PAIR), BF16), pltpu.VMEM((FOX_HEADS, tf, PAIR), BF16),
                        pltpu.VMEM((N_PAIRS, PAIR, tf), BF16), pltpu.VMEM((FOX_HEADS, tf, tf), F32),
                        pltpu.VMEM((FOX_HEADS, tf, PAIR), F32), pltpu.VMEM((FOX_HEADS, tf, PAIR), F32),
                        pltpu.VMEM((FOX_HEADS, tf, LANES), F32), pltpu.VMEM((n, N_PAIRS, PAIR, tf), F32),
                        pltpu.VMEM((n, 16, tf), F32), pltpu.SemaphoreType.DMA((2,))],
        compiler_params=_params(("arbitrary", "arbitrary")))(pq, kv, kv, fneg, dcat_bf, lse_rows, delta_rows)


ADAMW_TILE_ELEMS = 128 * 1024


def reduce_adamw(parts, w, m, v, name):
    layers, r, c = w.shape
    tr = r
    for cand in range(16, r, 16):
        if r % cand == 0 and cand * c <= ADAMW_TILE_ELEMS:
            tr = cand
    c1 = 1.0 - ADAM_B1 ** ADAM_STEP
    c2 = 1.0 - ADAM_B2 ** ADAM_STEP

    def body(*refs):
        p_refs = refs[:layers]
        w_ref, m_ref, v_ref, g_out, d_out, m_out, v_out = refs[layers:]

        def update(p_ref):
            g = p_ref[0].astype(F32)
            for k in range(1, N_DEV):
                g = g + p_ref[k].astype(F32)
            mn = ADAM_B1 * m_ref[0] + (1.0 - ADAM_B1) * g
            vn = ADAM_B2 * v_ref[0] + (1.0 - ADAM_B2) * (g * g)
            g_out[0] = g
            m_out[0] = mn
            v_out[0] = vn
            d_out[0] = -ADAM_LR * ((mn / c1) / (jnp.sqrt(vn / c2) + ADAM_EPS) + ADAM_WD * w_ref[0])

        if layers == 1:
            update(p_refs[0])
        else:
            for layer in range(layers):
                pl.when(pl.program_id(0) == layer)(lambda layer=layer: update(p_refs[layer]))

    row = pl.BlockSpec((1, tr, c), lambda l, i: (l, i, 0))
    return pl.pallas_call(
        body, name=name, grid=(layers, r // tr),
        in_specs=[pl.BlockSpec((N_DEV, tr, c), lambda l, i: (0, i, 0))] * layers + [row, row, row],
        out_specs=[row, row, row, row], out_shape=[SDS((layers, r, c), F32)] * 4,
        compiler_params=_params(("parallel", "parallel")))(*parts, w, m, v)


N_PEERS = N_DEV - 1
HBM_SPEC = pl.BlockSpec(memory_space=pltpu.HBM)
SEM_SPEC = pl.BlockSpec(memory_space=pltpu.SEMAPHORE)
ANY_SPEC = pl.BlockSpec(memory_space=pl.ANY)
SPLIT_EFFECT = pltpu.SideEffectType.DATAFLOW_SIDE_EFFECTING


def _peers(with_self=False):
    x, y, c = lax.axis_index("x"), lax.axis_index("y"), lax.axis_index("c")
    peers = []
    for k in range(0 if with_self else 1, N_DEV):
        px = 1 - x if (k >> 2) & 1 else x
        py = 1 - y if (k >> 1) & 1 else y
        pc = 1 - c if k & 1 else c
        peers.append(((px, py, pc), 4 * px + 2 * py + pc))
    return 4 * x + 2 * y + c, peers


def _push(src, dst, send_sems, recv_sems, slot, dev):
    return pltpu.make_async_remote_copy(src_ref=src, dst_ref=dst, send_sem=send_sems.at[slot], recv_sem=recv_sems.at[slot],
                                        device_id=dev, device_id_type=pl.DeviceIdType.MESH)


def _landing_shapes(arrs, scatter):
    return [SDS((N_DEV,) + tuple(a.shape[1:] if sc else a.shape), a.dtype) for a, sc in zip(arrs, scatter)]


def exchange(arrs, scatter, name):
    na = len(arrs)

    def body(*refs):
        ins = refs[:na]
        outs = refs[na:2 * na]
        send_sems, recv_sems, local_sems = refs[2 * na:]
        me, peers = _peers()
        local = []
        remote = []
        for a in range(na):
            lc = pltpu.make_async_copy(ins[a].at[me] if scatter[a] else ins[a], outs[a].at[me], local_sems.at[a])
            lc.start()
            local.append(lc)
            for k, (dev, idx) in enumerate(peers):
                cp = _push(ins[a].at[idx] if scatter[a] else ins[a], outs[a].at[me], send_sems, recv_sems,
                           a * N_PEERS + k, dev)
                cp.start()
                remote.append(cp)
        for a in range(na):
            for k, (dev, idx) in enumerate(peers):
                _push(ins[a].at[me] if scatter[a] else ins[a], outs[a].at[idx], send_sems, recv_sems,
                      a * N_PEERS + k, dev).wait_recv()
        for cp in remote:
            cp.wait_send()
        for lc in local:
            lc.wait()

    return pl.pallas_call(
        body, name=name, in_specs=[HBM_SPEC] * na, out_specs=[HBM_SPEC] * na, out_shape=_landing_shapes(arrs, scatter),
        scratch_shapes=[pltpu.SemaphoreType.DMA((na * N_PEERS,)), pltpu.SemaphoreType.DMA((na * N_PEERS,)),
                        pltpu.SemaphoreType.DMA((na,))])(*arrs)


def exchange_start(arrs, scatter, after, name):
    na = len(arrs)
    lands = [lax.empty(l.shape, l.dtype) for l in _landing_shapes(arrs, scatter)]

    def body(*refs):
        ins = refs[:na]
        land = refs[na:2 * na]
        send_sems, recv_sems = refs[2 * na + 1], refs[2 * na + 2]
        token = refs[-1]
        me, peers = _peers(with_self=True)
        for a in range(na):
            for k, (dev, idx) in enumerate(peers):
                _push(ins[a].at[idx] if scatter[a] else ins[a], land[a].at[me], send_sems, recv_sems,
                      a * N_DEV + k, dev).start()
        token[...] = jnp.zeros_like(token)

    thru = [pltpu.HBM(a.shape, a.dtype) for a in arrs] + [pltpu.HBM(l.shape, l.dtype) for l in lands]
    res = pl.pallas_call(
        body, name=name,
        out_shape=(pltpu.SemaphoreType.DMA((na * N_DEV,)), pltpu.SemaphoreType.DMA((na * N_DEV,)), *thru,
                   SDS((8, LANES), F32)),
        in_specs=[HBM_SPEC] * (2 * na) + [ANY_SPEC],
        out_specs=(SEM_SPEC, SEM_SPEC, *([HBM_SPEC] * (2 * na)), pl.BlockSpec(memory_space=pltpu.VMEM)),
        input_output_aliases={i: 2 + i for i in range(2 * na)},
        compiler_params=pltpu.CompilerParams(has_side_effects=SPLIT_EFFECT),
    )(*[pltpu.with_memory_space_constraint(a, pltpu.HBM) for a in arrs],
      *[pltpu.with_memory_space_constraint(l, pltpu.HBM) for l in lands], after)
    return {"send": res[0], "recv": res[1], "src": res[2:2 + na], "land": res[2 + na:2 + 2 * na],
            "token": res[-1][0, 0], "scatter": scatter}


def exchange_wait(handle, after, name):
    scatter = handle["scatter"]
    na = len(scatter)

    def body(*refs):
        src = refs[:na]
        land = refs[na:2 * na]
        send_sems, recv_sems = refs[2 * na], refs[2 * na + 1]
        me, peers = _peers(with_self=True)
        for a in range(na):
            for k, (dev, idx) in enumerate(peers):
                cp = _push(src[a].at[me] if scatter[a] else src[a], land[a].at[idx], send_sems, recv_sems,
                           a * N_DEV + k, dev)
                cp.wait_send()
                cp.wait_recv()

    ops = list(handle["src"]) + list(handle["land"])
    res = pl.pallas_call(
        body, name=name, out_shape=tuple(pltpu.HBM(o.shape, o.dtype) for o in ops),
        in_specs=[HBM_SPEC] * (2 * na) + [SEM_SPEC, SEM_SPEC, ANY_SPEC], out_specs=tuple([HBM_SPEC] * (2 * na)),
        input_output_aliases={i: i for i in range(2 * na)},
        compiler_params=pltpu.CompilerParams(has_side_effects=SPLIT_EFFECT),
    )(*ops, handle["send"], handle["recv"], after)
    return list(res[na:])


def forward_layer(l, xin, xin_bf, mem_bf, wt, nb, s, ffn_weights=None):
    sv = {"xin_bf": xin_bf}
    memkv = mm_nn(mem_bf, wt["memw"], BF16, f"memkv{l}")
    sv["memkv"] = memkv
    if l == 0:
        proj = mm_nn(xin_bf, wt["win_a"], F32, "proj_a")
        pooled, tok = pool_fwd(proj, wt["pw_bd"], wt["pscale"], nb, s)
        sv["pooled"] = pooled
    else:
        kv = mm_nn(xin_bf, wt["kvw"][:, :2 * TOK_WIDTH], BF16, "kv_proj")
        fl = mm_nn(xin_bf, wt["kvw"][:, 2 * TOK_WIDTH:], F32, "gate_proj")
        fneg = -fgate_fwd(fl, wt["fb"], nb, s)
        proj = mm_nn(xin_bf, wt["wq"], BF16, "proj_b")
        kaug, vaug_t = fox_prep(kv, fneg, nb, s)
        tok, o_f32, lse_rows = fox_fwd_t(proj, kaug, vaug_t, nb, s)
        sv.update(kv=kv, fl=fl, fneg=fneg, o_f32=o_f32, lse_rows=lse_rows)
    sv["proj"] = proj
    mem_out = memattn_fwd(proj, memkv, nb, s, f"memattn_fwd{l}")
    cat = jnp.concatenate([tok, mem_out], axis=1)
    sv["cat"] = cat
    x1, x1_bf, xh1, rs1 = ln_fwd(xin, cat, wt["wout"], wt["ln1_g"], wt["ln1_b"], f"out_proj_ln1_{l}")
    sv.update(x1_bf=x1_bf, xh1=xh1, rs1=rs1)
    if ffn_weights is not None:
        wt.update(ffn_weights(x1_bf))
    act, ga, gb, hu, hg = ffn_up_gate(x1_bf, wt["wup"], wt["cw"], nb, s, f"ffn_up_gate{l}")
    sv.update(act=act, ga=ga, gb=gb, hu=hu, hg=hg)
    x2, x2_bf, xh2, rs2 = ln_fwd(x1, act, wt["wdown"], wt["ln2_g"], wt["ln2_b"], f"ffn_down_ln2_{l}")
    sv.update(xh2=xh2, rs2=rs2)
    return x2, x2_bf, sv


def backward_layer(l, dy, sv, mem_bf, wt, nb, s, after_ffn=None, after_pool=None, dy_product=None):
    g = {}
    dr2, dr2_bf, g["ln2_g"], g["ln2_b"] = ln_bwd(dy, sv["xh2"], sv["rs2"], wt["ln2_g"], f"ln2_bwd{l}", product=dy_product)
    dact = mm_nn(dr2_bf, wt["wdown"], BF16, f"ffn_down_dx{l}", trans_b=0)
    g["wdown"] = mm_tn(sv["act"], dr2_bf, f"ffn_down_dw{l}")
    dh_u, dh_g, dcw_u, dcw_g = gate_conv_bwd(dact, sv["ga"], sv["gb"], sv["hu"], sv["hg"], wt["cw"], nb, s,
                                             f"gate_conv_bwd{l}")
    g["cw"] = jnp.concatenate([dcw_u, dcw_g], axis=0)
    dx1 = mm_nn(dh_u, wt["wup"], F32, f"ffn_up_dx_u{l}", addend=dr2, add_scale=DN_ALPHA, trans_b=0)
    g["wup"] = jnp.concatenate([mm_tn(sv["x1_bf"], dh_u, f"ffn_up_dw_u{l}", blocked=True),
                                mm_tn(sv["x1_bf"], dh_g, f"ffn_up_dw_g{l}", blocked=True)], axis=0)
    ln1_g = wt["ln1_g"] if after_ffn is None else wt["ln1_g"] + after_ffn(g, dx1)
    dr1, dr1_bf, g["ln1_g"], g["ln1_b"] = ln_bwd(dx1, sv["xh1"], sv["rs1"], ln1_g, f"ffn_up_dx_g_ln1_bwd{l}",
                                                 product=(dh_g, wt["wup"], 1))
    dcat, dcat_bf = mm_nn(dr1_bf, wt["wout"], F32, f"out_proj_dx{l}", also_bf16=True, trans_b=0)
    g["wout"] = mm_tn(sv["cat"], dr1_bf, f"out_proj_dw{l}")
    dqm, dmemkv = memattn_bwd(sv["proj"], sv["memkv"], dcat, nb, s, f"memattn_bwd{l}")
    g["memw"] = mm_tn(mem_bf, dmemkv, f"memkv_dw{l}")
    if l == 0:
        dmixed, dpooled, g["pscale"] = pool_bwd_mix(dcat, sv["pooled"], wt["pw_bd"], wt["pscale"], nb, s)
        g["pw_full"] = mm_tn(sv["pooled"], dmixed, "pool_dw")
        win_a = wt["win_a"] if after_pool is None else wt["win_a"] + after_pool(g, dmixed).astype(BF16)
        du = pool_bwd_window(dpooled, nb, s)
        dproj = jnp.concatenate([du, dqm], axis=1)
        dx = mm_nn(dproj, win_a, F32, "proj_a_dx", addend=dr1, add_scale=DN_ALPHA, trans_b=0)
        g["win_a"] = mm_tn(sv["xin_bf"], dproj, "proj_a_dw")
        pending = None
    else:
        delta = fox_delta(dcat, sv["o_f32"], nb, s)
        tf = min(TF, s)
        dq, dk, dv, dfcum_k, dfq_rows = fox_bwd(sv["proj"], sv["kv"], sv["fneg"], dcat_bf,
                                                sv["lse_rows"], _to_tile_rows(delta, nb, s, tf), nb, s)
        dfl, g["fb"] = fgate_bwd(_from_tile_rows(dfq_rows), dfcum_k, sv["fl"], wt["fb"], nb, s)
        dproj = jnp.concatenate([dq, dqm], axis=1)
        dkvf = jnp.concatenate([dk, dv, dfl.astype(BF16)], axis=1)
        dx = mm_nn(dproj, wt["wq"], F32, "proj_b_dx", addend=dr1, add_scale=DN_ALPHA, trans_b=0)
        pending = (dkvf, wt["kvw"], 0)
        g["wq"] = mm_tn(sv["xin_bf"], dproj, "proj_b_dw")
        g["kvw"] = mm_tn(sv["xin_bf"], dkvf, "kv_proj_dw")
    return dx, pending, g


def pack_replicated(pool_w, ln1_g, ln1_b, ln2_g, ln2_b, conv_b, f_b):
    cb = jnp.pad(conv_b, ((0, 0), (0, 6144 - 5504))).reshape(12, D_MODEL)
    fb = jnp.pad(f_b.reshape(1, FOX_HEADS), ((0, 3), (0, D_MODEL - FOX_HEADS)))
    return jnp.concatenate([pool_w.reshape(144, D_MODEL), ln1_g, ln1_b, ln2_g, ln2_b, cb, fb], axis=0)


def unpack_replicated(buf):
    pool_w = buf[:144].reshape(1, 4, POOL_GROUP, POOL_GROUP)
    ln = [buf[144 + 2 * k:146 + 2 * k] for k in range(4)]
    conv_b = buf[152:164].reshape(2, 6144)[:, :5504]
    f_b = buf[164, :FOX_HEADS]
    return pool_w, ln[0], ln[1], ln[2], ln[3], conv_b, f_b


def pack_small(conv_w, pool_scale):
    buf = jnp.zeros((16, FF_BLOCK_PAD), F32)
    buf = lax.dynamic_update_slice(buf, conv_w.reshape(DEPTH * 3, FF_BLOCK), (0, 0))
    return lax.dynamic_update_slice(buf, pool_scale, (8, 0))


def _block_diag(pw):
    out = jnp.zeros((TOK_WIDTH, TOK_WIDTH), pw.dtype)
    for g in range(4):
        out = lax.dynamic_update_slice(out, pw[g], (g * POOL_GROUP, g * POOL_GROUP))
    return out


def layer_shards(l, sq_a, sq_b, mem_w_kv, ffn_w_up, ffn_w_down):
    return [sq_a[0].astype(BF16), sq_b[0].astype(BF16), mem_w_kv[l].astype(BF16), ffn_w_up[l].astype(BF16),
            ffn_w_down[l].astype(BF16)]


def mixer_weights(l, gath, ln1_g, ln1_b, ln2_g, ln2_b):
    w_out = gath[1].reshape(D_MODEL, D_MODEL)
    wt = {"memw": gath[2].reshape(D_MODEL, 2 * MEM_WIDTH), "wout": w_out,
          "ln1_g": ln1_g[l:l + 1], "ln1_b": ln1_b[l:l + 1], "ln2_g": ln2_g[l:l + 1], "ln2_b": ln2_b[l:l + 1]}
    return wt, gath[0].reshape(D_MODEL, D_MODEL)


def ffn_weights(l, wup_g, wdown_g, small, conv_b):
    pad_c = FF_BLOCK_PAD - FF_BLOCK
    wup = jnp.pad(wup_g, ((0, 0), (0, 0), (0, pad_c))).transpose(1, 0, 2).reshape(D_MODEL, N_DEV * FF_BLOCK_PAD)
    wdown = jnp.pad(wdown_g.reshape(FF_PAIRS, FF_BLOCK, D_MODEL), ((0, 0), (0, pad_c), (0, 0)))
    wdown = wdown.reshape(FF_PAIRS * FF_BLOCK_PAD, D_MODEL)
    cb = jnp.pad(conv_b[l].reshape(N_DEV, FF_BLOCK), ((0, 0), (0, pad_c)))
    cw = jnp.concatenate([small[:, 3 * l:3 * l + 3, :], cb[:, None, :], jnp.zeros((N_DEV, 4, FF_BLOCK_PAD), F32)], axis=1)
    return {"wup": wup, "wdown": wdown, "cw": cw}


def mixer_grad_blocks(g, w_in_grad):
    blocks = [] if w_in_grad is None else [w_in_grad.reshape(N_DEV, 128, D_MODEL)]
    blocks += [g["wout"].reshape(N_DEV, 128, D_MODEL), g["memw"].reshape(N_DEV, 128, 2 * MEM_WIDTH)]
    return [b.astype(BF16) for b in blocks]


def ffn_grad_blocks(g):
    wup = g["wup"][:, :, :FF_BLOCK]
    wdown = g["wdown"].reshape(FF_PAIRS, FF_BLOCK_PAD, D_MODEL)[:, :FF_BLOCK].reshape(N_DEV, FF_ROWS, D_MODEL)
    return [wup.astype(BF16), wdown.astype(BF16)]


def small_grad_blocks(g0, g1):
    taps = jnp.stack([g0["cw"][:, :3, :], g1["cw"][:, :3, :]], axis=1).reshape(N_DEV, DEPTH * 3, FF_BLOCK_PAD)
    small = jnp.zeros((N_DEV, 16, FF_BLOCK_PAD), F32)
    small = lax.dynamic_update_slice(small, taps, (0, 0, 0))
    return lax.dynamic_update_slice(small, g0["pscale"].reshape(N_DEV, 1, 96), (0, 8, 0))


def replicated_grads(g0, g1):
    pw = jnp.stack([g0["pw_full"][k * POOL_GROUP:(k + 1) * POOL_GROUP, k * POOL_GROUP:(k + 1) * POOL_GROUP] for k in range(4)])
    conv_b = jnp.stack([g_["cw"][:, 3, :FF_BLOCK].reshape(N_DEV * FF_BLOCK) for g_ in (g0, g1)])
    ln = [jnp.concatenate([g0[n], g1[n]], axis=0) for n in ("ln1_g", "ln1_b", "ln2_g", "ln2_b")]
    return pack_replicated(pw[None], ln[0], ln[1], ln[2], ln[3], conv_b, g1["fb"][0, :FOX_HEADS])


def kernel(x, mem, a_w_in, a_pool_w, a_pool_scale, a_w_out, b_w_q, b_w_out, kv_w, f_b, mem_w_kv, ln1_g, ln1_b, ln2_g, ln2_b, ffn_w_up, ffn_conv_w, ffn_conv_b, ffn_w_down, loss_target, m_a_w_in, m_a_pool_w, m_a_pool_scale, m_a_w_out, m_b_w_q, m_b_w_out, m_kv_w, m_f_b, m_mem_w_kv, m_ln1_g, m_ln1_b, m_ln2_g, m_ln2_b, m_ffn_w_up, m_ffn_conv_w, m_ffn_conv_b, m_ffn_w_down, v_a_w_in, v_a_pool_w, v_a_pool_scale, v_a_w_out, v_b_w_q, v_b_w_out, v_kv_w, v_f_b, v_mem_w_kv, v_ln1_g, v_ln1_b, v_ln2_g, v_ln2_b, v_ffn_w_up, v_ffn_conv_w, v_ffn_conv_b, v_ffn_w_down):
    nb, s, d = x.shape
    t = nb * s
    x2d, mem_bf, target = x.reshape(t, d), mem.reshape(nb * MEM_LEN, d).astype(BF16), loss_target.reshape(t, d)

    shards0 = layer_shards(0, a_w_in, a_w_out, mem_w_kv, ffn_w_up, ffn_w_down)
    shards1 = layer_shards(1, b_w_q, b_w_out, mem_w_kv, ffn_w_up, ffn_w_down)
    shards1.append(jnp.pad(kv_w, ((0, 0), (0, KV_COLS_PAD - KV_COLS))).astype(BF16))
    gath0 = exchange(shards0[:3] + [pack_small(ffn_conv_w, a_pool_scale)], [False] * 4, "gather_w0_mixer")
    pending = {"ffn0": exchange_start(shards0[3:], [False] * 2, gath0[0], "gather_w0_ffn_start")}
    small = gath0[3]
    wt0, w_in = mixer_weights(0, gath0, ln1_g + pending["ffn0"]["token"], ln1_b, ln2_g, ln2_b)
    pw_bd = _block_diag(a_pool_w[0])
    wt0.update(win_a=w_in, pw_bd=pw_bd.astype(BF16),
               pscale=small[:, 8, :96].reshape(1, TOK_WIDTH) + pending["ffn0"]["token"])

    def ffn0_weights(x1_bf):
        got = exchange_wait(pending["ffn0"], x1_bf, "gather_w0_ffn_wait")
        pending["w1"] = exchange_start(shards1, [False] * 6, got[0], "gather_w1_start")
        w = ffn_weights(0, got[0], got[1], small, ffn_conv_b)
        w["cw"] = w["cw"] + pending["w1"]["token"]
        return w

    x1, x1_bf, sv0 = forward_layer(0, x2d, x2d, mem_bf, wt0, nb, s, ffn_weights=ffn0_weights)
    gath1 = exchange_wait(pending["w1"], x1_bf, "gather_w1_wait")
    wt1, w_q = mixer_weights(1, gath1, ln1_g, ln1_b, ln2_g, ln2_b)
    wt1.update(ffn_weights(1, gath1[3], gath1[4], small, ffn_conv_b))
    kvw = gath1[5].reshape(D_MODEL, KV_COLS_PAD)
    wt1.update(wq=w_q, kvw=kvw,
               fb=jnp.pad(f_b.reshape(1, FOX_HEADS), ((0, 0), (0, LANES - FOX_HEADS))))
    y, _, sv1 = forward_layer(1, x1, x1_bf, mem_bf, wt1, nb, s)
    dy, loss_row = loss_head(y, target)
    loss = lax.psum(loss_row[0, 0], ("x", "y", "c"))

    dx1, dx1_rest, g1 = backward_layer(1, dy, sv1, mem_bf, wt1, nb, s)
    blocks1 = (mixer_grad_blocks(g1, g1["wq"]) + ffn_grad_blocks(g1)
               + [g1["kvw"][:, :KV_COLS].reshape(N_DEV, 128, KV_COLS).astype(BF16)])
    pending["g1"] = exchange_start(blocks1, [True] * 6, dx1, "scatter_g1_start")
    wt0["ln2_g"] = wt0["ln2_g"] + pending["g1"]["token"]

    def after_ffn0(g, dxm):
        pending["gf0"] = exchange_start(ffn_grad_blocks(g), [True] * 2, dxm, "scatter_g0_ffn_start")
        return pending["gf0"]["token"]

    def after_pool0(g, x):
        blocks = mixer_grad_blocks(g, None) + [small_grad_blocks(g, g1), replicated_grads(g, g1)]
        pending["gm0"] = exchange_start(blocks, [True] * 3 + [False], x, "scatter_g0_mixer_start")
        return pending["gm0"]["token"]

    grad_x, _, g0 = backward_layer(0, dx1, sv0, mem_bf, wt0, nb, s, after_ffn=after_ffn0, after_pool=after_pool0,
                                   dy_product=dx1_rest)
    pending["gin"] = exchange_start([g0["win_a"].reshape(N_DEV, 128, D_MODEL).astype(BF16)], [True], grad_x,
                                    "scatter_g0_in_start")
    parts_f0 = exchange_wait(pending["gf0"], jnp.zeros((8, LANES), F32) + pending["gin"]["token"], "scatter_g0_ffn_wait")
    parts1 = exchange_wait(pending["g1"], parts_f0[0], "scatter_g1_wait")

    res = {}

    def upd(nm, parts, w2, m2, v2):
        res[nm] = reduce_adamw(parts, w2, m2, v2, f"adamw_{nm}")

    upd("b_w_q", [parts1[0]], b_w_q, m_b_w_q, v_b_w_q)
    upd("b_w_out", [parts1[1]], b_w_out, m_b_w_out, v_b_w_out)
    upd("kv_w", [parts1[5]], kv_w[None], m_kv_w[None], v_kv_w[None])
    upd("ffn_w_up", [parts_f0[0], parts1[3]], ffn_w_up, m_ffn_w_up, v_ffn_w_up)
    upd("ffn_w_down", [parts_f0[1], parts1[4]], ffn_w_down, m_ffn_w_down, v_ffn_w_down)
    parts_m0 = exchange_wait(pending["gm0"], res["ffn_w_down"][0], "scatter_g0_mixer_wait")
    parts_in = exchange_wait(pending["gin"], parts_m0[0], "scatter_g0_in_wait")
    upd("a_w_in", [parts_in[0]], a_w_in, m_a_w_in, v_a_w_in)
    upd("a_w_out", [parts_m0[0]], a_w_out, m_a_w_out, v_a_w_out)
    upd("mem_w_kv", [parts_m0[1], parts1[2]], mem_w_kv, m_mem_w_kv, v_mem_w_kv)
    upd("small", [parts_m0[2]], pack_small(ffn_conv_w, a_pool_scale)[None], pack_small(m_ffn_conv_w, m_a_pool_scale)[None],
        pack_small(v_ffn_conv_w, v_a_pool_scale)[None])
    upd("replicated", [parts_m0[3]], pack_replicated(a_pool_w, ln1_g, ln1_b, ln2_g, ln2_b, ffn_conv_b, f_b)[None],
        pack_replicated(m_a_pool_w, m_ln1_g, m_ln1_b, m_ln2_g, m_ln2_b, m_ffn_conv_b, m_f_b)[None],
        pack_replicated(v_a_pool_w, v_ln1_g, v_ln1_b, v_ln2_g, v_ln2_b, v_ffn_conv_b, v_f_b)[None])

    res["kv_w"] = [o[0] for o in res["kv_w"]]
    res["ffn_conv_w"] = [o[0, :DEPTH * 3, :FF_BLOCK].reshape(DEPTH, 3, FF_BLOCK) for o in res["small"]]
    res["a_pool_scale"] = [o[0, 8:9, :96] for o in res["small"]]
    rep_names = ["a_pool_w", "ln1_g", "ln1_b", "ln2_g", "ln2_b", "ffn_conv_b", "f_b"]
    for nm in rep_names:
        res[nm] = []
    for o in res["replicated"]:
        for nm, val in zip(rep_names, unpack_replicated(o[0])):
            res[nm].append(val)

    order = ["a_w_in", "a_pool_w", "a_pool_scale", "a_w_out", "b_w_q", "b_w_out", "kv_w", "f_b", "mem_w_kv",
             "ln1_g", "ln1_b", "ln2_g", "ln2_b", "ffn_w_up", "ffn_conv_w", "ffn_conv_b", "ffn_w_down"]
    out = [loss, grad_x.reshape(nb, s, d)]
    for kind in range(4):
        out.extend(res[nm][kind] for nm in order)
    return tuple(out)
```

```python
import jax
import jax.numpy as jnp
from jax import lax
from jax.experimental import pallas as pl
from jax.experimental.pallas import tpu as pltpu

F32 = jnp.float32
BF16 = jnp.bfloat16
SDS = jax.ShapeDtypeStruct

N_DEV = 8
D_MODEL = 1024
TOK_WIDTH = 768
MEM_WIDTH = 256
MEM_LEN = 256
MEM_HEADS = 4
HEAD_DIM = 64
FOX_HEADS = 12
POOL_GROUP = 192
FF_BLOCK = 688
FF_BLOCK_PAD = 768
FF_PAIRS = 4
FF_ROWS = 344
KV_COLS = 1548
KV_COLS_PAD = 1664
LANES = 128
DEPTH = 2
DN_ALPHA = (2.0 * DEPTH) ** 0.25
LN_EPS = 1e-5
QK_SCALE = HEAD_DIM ** -0.5
NEG_BIG = -1e30

ADAM_LR = 0.001
ADAM_B1 = 0.9
ADAM_B2 = 0.999
ADAM_EPS = 1e-08
ADAM_WD = 0.01
ADAM_STEP = 10

VMEM_LIMIT_BYTES = 56 * 1024 * 1024
MM_BLOCK_BYTES = 6 * 1024 * 1024
TM = 512
TS = 256
TF = 256
TC = 256
HALO_POOL = 16
HALO_CONV = 8

NT_DIMS = (((1,), (1,)), ((), ()))
TN_DIMS = (((0,), (0,)), ((), ()))


def _params(sem=None):
    return pltpu.CompilerParams(dimension_semantics=sem, vmem_limit_bytes=VMEM_LIMIT_BYTES)


def _sigmoid(z):
    return 1.0 / (1.0 + jnp.exp(-z))


def _pick_tn(n):
    if n <= 2048:
        return n
    for t in (1024, 768, 512, 256, 128):
        if n % t == 0:
            return t
    return n


def mm_nn(a, b, out_dtype, name, addend=None, add_scale=1.0, also_bf16=False, trans_b=None):
    m, k = a.shape
    n = b.shape[1] if trans_b is None else b.shape[0]
    tm = min(TM, m)
    tn = n
    while k * tn * 2 > MM_BLOCK_BYTES or tm * tn * 4 > MM_BLOCK_BYTES:
        tn //= 2
    chunk = tn if tn <= 2048 else _pick_tn(tn)
    has_add = addend is not None

    def body(*refs):
        a_ref, b_ref = refs[0], refs[1]
        c_ref = refs[2] if has_add else None
        o_ref = refs[3] if has_add else refs[2]
        ob_ref = refs[-1] if also_bf16 else None
        av = a_ref[...].astype(BF16)
        for c in range(tn // chunk):
            cols = slice(c * chunk, (c + 1) * chunk)
            if trans_b is None:
                r = jnp.dot(av, b_ref[:, cols].astype(BF16), preferred_element_type=F32)
            else:
                r = lax.dot_general(av, b_ref[cols, :].astype(BF16), NT_DIMS, preferred_element_type=F32)
            if has_add:
                r = r + add_scale * c_ref[:, cols]
            o_ref[:, cols] = r.astype(out_dtype)
            if also_bf16:
                ob_ref[:, cols] = r.astype(BF16)

    b_spec = (pl.BlockSpec((k, tn), lambda j, i: (0, j)) if trans_b is None
              else pl.BlockSpec((tn, k), lambda j, i: (j, trans_b)))
    in_specs = [pl.BlockSpec((tm, k), lambda j, i: (i, 0)), b_spec]
    ops = [a, b]
    tile = pl.BlockSpec((tm, tn), lambda j, i: (i, j))
    if has_add:
        in_specs.append(tile)
        ops.append(addend)
    out_shape = [SDS((m, n), out_dtype)]
    out_specs = [tile]
    if also_bf16:
        out_shape.append(SDS((m, n), BF16))
        out_specs.append(tile)
    res = pl.pallas_call(
        body, name=name, grid=(n // tn, m // tm), in_specs=in_specs, out_specs=out_specs, out_shape=out_shape,
        compiler_params=_params(("parallel", "parallel")))(*ops)
    return tuple(res) if also_bf16 else res[0]


def mm_tn(a, b, name, blocked=False):
    t, m = a.shape
    _, n = b.shape
    tt = min(4 * TM, t)
    tm = 1024 if m % 1024 == 0 else m
    tn = FF_BLOCK_PAD if blocked else _pick_tn(n)
    nt = t // tt

    def body(a_ref, b_ref, o_ref):
        kk = pl.program_id(2)
        r = lax.dot_general(a_ref[...].astype(BF16), b_ref[...].astype(BF16), TN_DIMS, preferred_element_type=F32)
        if blocked:
            r = r[None]

        @pl.when(kk == 0)
        def _():
            o_ref[...] = r

        @pl.when(kk != 0)
        def _():
            o_ref[...] += r

    if blocked:
        out_shape = SDS((n // tn, m, tn), F32)
        out_spec = pl.BlockSpec((1, tm, tn), lambda i, j, kk: (j, i, 0))
    else:
        out_shape = SDS((m, n), F32)
        out_spec = pl.BlockSpec((tm, tn), lambda i, j, kk: (i, j))
    return pl.pallas_call(
        body, name=name, grid=(m // tm, n // tn, nt),
        in_specs=[pl.BlockSpec((tt, tm), lambda i, j, kk: (kk, i)), pl.BlockSpec((tt, tn), lambda i, j, kk: (kk, j))],
        out_specs=out_spec, out_shape=out_shape,
        compiler_params=_params(("parallel", "parallel", "arbitrary")))(a, b)


def ln_fwd(xprev, a, w, g, b, name):
    t, d = xprev.shape
    k = a.shape[1]
    tm = min(TM, t)

    def body(xp_ref, a_ref, w_ref, g_ref, b_ref, y_ref, yb_ref, xh_ref, rs_ref):
        r = DN_ALPHA * xp_ref[...] + jnp.dot(a_ref[...], w_ref[...], preferred_element_type=F32)
        mu = jnp.mean(r, axis=1, keepdims=True)
        xc = r - mu
        var = jnp.mean(xc * xc, axis=1, keepdims=True)
        rstd = lax.rsqrt(var + LN_EPS)
        xh = xc * rstd
        y = xh * g_ref[...] + b_ref[...]
        y_ref[...] = y
        yb_ref[...] = y.astype(BF16)
        xh_ref[...] = xh
        rs_ref[...] = jnp.broadcast_to(rstd, (tm, LANES))

    row = pl.BlockSpec((tm, d), lambda i: (i, 0))
    vec = pl.BlockSpec((1, d), lambda i: (0, 0))
    return pl.pallas_call(
        body, name=name, grid=(t // tm,),
        in_specs=[row, pl.BlockSpec((tm, k), lambda i: (i, 0)), pl.BlockSpec((k, d), lambda i: (0, 0)), vec, vec],
        out_specs=[row, row, row, pl.BlockSpec((tm, LANES), lambda i: (i, 0))],
        out_shape=[SDS((t, d), F32), SDS((t, d), BF16), SDS((t, d), F32), SDS((t, LANES), F32)],
        compiler_params=_params(("parallel",)))(xprev, a, w, g, b)


def ln_bwd(dy, xhat, rstd, g, name, product=None, dy_scale=1.0):
    t, d = dy.shape
    tm = min(TM, t)
    fused = product is not None

    def body(*refs):
        if fused:
            a_ref, w_ref = refs[0], refs[1]
            refs = refs[2:]
        dy_ref, xh_ref, rs_ref, g_ref, dr_ref, drb_ref, dg_ref, db_ref = refs
        i = pl.program_id(0)
        dyv = dy_ref[...] if dy_scale == 1.0 else dy_scale * dy_ref[...]
        if fused:
            dyv = dyv + lax.dot_general(a_ref[...], w_ref[...], NT_DIMS, preferred_element_type=F32)
        xh = xh_ref[...]
        dxh = dyv * g_ref[...]
        m1 = jnp.mean(dxh, axis=1, keepdims=True)
        m2 = jnp.mean(dxh * xh, axis=1, keepdims=True)
        dr = rs_ref[:, 0:1] * (dxh - m1 - xh * m2)
        dr_ref[...] = dr
        drb_ref[...] = dr.astype(BF16)

        @pl.when(i == 0)
        def _():
            dg_ref[...] = jnp.zeros_like(dg_ref)
            db_ref[...] = jnp.zeros_like(db_ref)

        dg_ref[...] += jnp.sum(dyv * xh, axis=0, keepdims=True)
        db_ref[...] += jnp.sum(dyv, axis=0, keepdims=True)

    row = pl.BlockSpec((tm, d), lambda i: (i, 0))
    vec = pl.BlockSpec((1, d), lambda i: (0, 0))
    in_specs = [row, row, pl.BlockSpec((tm, LANES), lambda i: (i, 0)), vec]
    ops = [dy, xhat, rstd, g]
    if fused:
        k = product[0].shape[1]
        col = product[2]
        in_specs = [pl.BlockSpec((tm, k), lambda i: (i, 0)), pl.BlockSpec((d, k), lambda i: (0, col))] + in_specs
        ops = list(product[:2]) + ops
    return pl.pallas_call(
        body, name=name, grid=(t // tm,), in_specs=in_specs, out_specs=[row, row, vec, vec],
        out_shape=[SDS((t, d), F32), SDS((t, d), BF16), SDS((1, d), F32), SDS((1, d), F32)],
        compiler_params=_params(("arbitrary",)))(*ops)


def loss_ln_bwd(xhat, rstd, g, beta, target, name):
    t, d = xhat.shape
    tm = min(TM, t)
    nsteps = t // tm

    def body(xh_ref, rs_ref, g_ref, b_ref, t_ref, dr_ref, drb_ref, dg_ref, db_ref, l_ref, acc):
        i = pl.program_id(0)
        xh = xh_ref[...]
        diff = xh * g_ref[...] + b_ref[...] - t_ref[...]
        dyv = diff * (1.0 / d)
        dxh = dyv * g_ref[...]
        m1 = jnp.mean(dxh, axis=1, keepdims=True)
        m2 = jnp.mean(dxh * xh, axis=1, keepdims=True)
        dr = rs_ref[:, 0:1] * (dxh - m1 - xh * m2)
        dr_ref[...] = dr
        drb_ref[...] = dr.astype(BF16)

        @pl.when(i == 0)
        def _():
            dg_ref[...] = jnp.zeros_like(dg_ref)
            db_ref[...] = jnp.zeros_like(db_ref)
            acc[...] = jnp.zeros_like(acc)

        dg_ref[...] += jnp.sum(dyv * xh, axis=0, keepdims=True)
        db_ref[...] += jnp.sum(dyv, axis=0, keepdims=True)
        acc[...] += jnp.sum(diff * diff, axis=0, keepdims=True)

        @pl.when(i == nsteps - 1)
        def _():
            tot = jnp.sum(acc[...], axis=1, keepdims=True) * (0.5 / d)
            l_ref[...] = jnp.broadcast_to(tot, (1, LANES))

    row = pl.BlockSpec((tm, d), lambda i: (i, 0))
    vec = pl.BlockSpec((1, d), lambda i: (0, 0))
    return pl.pallas_call(
        body, name=name, grid=(nsteps,),
        in_specs=[row, pl.BlockSpec((tm, LANES), lambda i: (i, 0)), vec, vec, row],
        out_specs=[row, row, vec, vec, pl.BlockSpec((1, LANES), lambda i: (0, 0))],
        out_shape=[SDS((t, d), F32), SDS((t, d), BF16), SDS((1, d), F32), SDS((1, d), F32), SDS((1, LANES), F32)],
        scratch_shapes=[pltpu.VMEM((1, d), F32)],
        compiler_params=_params(("arbitrary",)))(xhat, rstd, g, beta, target)


def memattn_fwd(proj, memkv, nb, s, name):
    ts = min(TS, s)
    nq = s // ts

    def body(q_ref, kv_ref, o_ref):
        top = lax.broadcasted_iota(jnp.int32, (PAIR, ts), 0) < HEAD_DIM
        scores = []
        for p in range(MEM_HEADS // 2):
            qp = q_ref[:, p * PAIR:(p + 1) * PAIR].astype(BF16)
            ke, ko = _split_pair(kv_ref[:, p * PAIR:(p + 1) * PAIR], QK_SCALE)
            scores.append([lax.dot_general(km, qp, NT_DIMS, preferred_element_type=F32) for km in (ke, ko)])
        for p in range(MEM_HEADS // 2):
            vt = kv_ref[:, MEM_WIDTH + p * PAIR:MEM_WIDTH + (p + 1) * PAIR].astype(F32).T.astype(BF16)
            outs = []
            for sc in scores[p]:
                e = jnp.exp(sc - jnp.max(sc, axis=0, keepdims=True))
                pr = e / jnp.sum(e, axis=0, keepdims=True)
                outs.append(jnp.dot(vt, pr.astype(BF16), preferred_element_type=F32))
            o_ref[:, p * PAIR:(p + 1) * PAIR] = jnp.where(top, outs[0], outs[1]).T.astype(BF16)

    return pl.pallas_call(
        body, name=name, grid=(nb, nq),
        in_specs=[pl.BlockSpec((ts, MEM_WIDTH), lambda b, i: (b * nq + i, 3)),
                  pl.BlockSpec((MEM_LEN, 2 * MEM_WIDTH), lambda b, i: (b, 0))],
        out_specs=pl.BlockSpec((ts, MEM_WIDTH), lambda b, i: (b * nq + i, 0)),
        out_shape=SDS((nb * s, MEM_WIDTH), BF16),
        compiler_params=_params(("parallel", "parallel")))(proj, memkv)


def memattn_bwd(proj, memkv, dcat, nb, s, name):
    ts = min(TS, s)
    nq = s // ts

    def body(q_ref, kv_ref, do_ref, dq_ref, dkv_ref):
        i = pl.program_id(1)

        @pl.when(i == 0)
        def _():
            dkv_ref[...] = jnp.zeros_like(dkv_ref)

        lo = _half_masks(MEM_LEN)
        top = lax.broadcasted_iota(jnp.int32, (PAIR, ts), 0) < HEAD_DIM
        n_pairs = MEM_HEADS // 2
        qs, dos, kps, products = [], [], [], []
        for p in range(n_pairs):
            qp = q_ref[:, p * PAIR:(p + 1) * PAIR].astype(BF16)
            dop = do_ref[:, p * PAIR:(p + 1) * PAIR].astype(BF16)
            kp = kv_ref[:, p * PAIR:(p + 1) * PAIR] * QK_SCALE
            kms = _split_pair(kp)
            vms = _split_pair(kv_ref[:, MEM_WIDTH + p * PAIR:MEM_WIDTH + (p + 1) * PAIR])
            products.append([(lax.dot_general(km, qp, NT_DIMS, preferred_element_type=F32),
                              lax.dot_general(vm, dop, NT_DIMS, preferred_element_type=F32)) for km, vm in zip(kms, vms)])
            qs.append(qp)
            dos.append(dop)
            kps.append(kp)
        for p in range(n_pairs):
            kt = kps[p].astype(F32).T.astype(BF16)
            dks, dvs, dqs = [], [], []
            for sc, dp in products[p]:
                e = jnp.exp(sc - jnp.max(sc, axis=0, keepdims=True))
                pr = e / jnp.sum(e, axis=0, keepdims=True)
                dl = jnp.sum(pr * dp, axis=0, keepdims=True)
                ds = (pr * (dp - dl)).astype(BF16)
                dvs.append(jnp.dot(pr.astype(BF16), dos[p], preferred_element_type=F32))
                dks.append(jnp.dot(ds, qs[p], preferred_element_type=F32))
                dqs.append(jnp.dot(kt, ds, preferred_element_type=F32))
            dq_ref[:, p * PAIR:(p + 1) * PAIR] = jnp.where(top, dqs[0], dqs[1]).T.astype(BF16)
            dkv_ref[:, p * PAIR:(p + 1) * PAIR] += jnp.where(lo, dks[0], dks[1]) * QK_SCALE
            dkv_ref[:, MEM_WIDTH + p * PAIR:MEM_WIDTH + (p + 1) * PAIR] += jnp.where(lo, dvs[0], dvs[1])

    return pl.pallas_call(
        body, name=name, grid=(nb, nq),
        in_specs=[pl.BlockSpec((ts, MEM_WIDTH), lambda b, i: (b * nq + i, 3)),
                  pl.BlockSpec((MEM_LEN, 2 * MEM_WIDTH), lambda b, i: (b, 0)),
                  pl.BlockSpec((ts, MEM_WIDTH), lambda b, i: (b * nq + i, 3))],
        out_specs=[pl.BlockSpec((ts, MEM_WIDTH), lambda b, i: (b * nq + i, 0)),
                   pl.BlockSpec((MEM_LEN, 2 * MEM_WIDTH), lambda b, i: (b, 0))],
        out_shape=[SDS((nb * s, MEM_WIDTH), BF16), SDS((nb * MEM_LEN, 2 * MEM_WIDTH), F32)],
        compiler_params=_params(("parallel", "arbitrary")))(proj, memkv, dcat)


def _pool_select(shape, s2, s4, s8, s16):
    lane = lax.broadcasted_iota(jnp.int32, shape, 1)
    return jnp.where(lane < POOL_GROUP, s2, jnp.where(lane < 2 * POOL_GROUP, s4, jnp.where(lane < 3 * POOL_GROUP, s8, s16)))


def _pool_count(shape, first_pos):
    pos = first_pos + lax.broadcasted_iota(jnp.int32, shape, 0)
    win = _pool_select(shape, 2, 4, 8, 16)
    return jnp.minimum(pos + 1, win).astype(F32)


def pool_fwd(proj, pw_bd, pscale, nb, s):
    ts = min(TS, s)
    nq = s // ts
    w = TOK_WIDTH

    def body(c_ref, h_ref, w_ref, sc_ref, pooled_ref, tok_ref):
        i = pl.program_id(0) % nq
        cur = c_ref[...]
        halo = jnp.where(i == 0, 0.0, h_ref[...])
        xe = jnp.concatenate([halo, cur], axis=0)
        s2 = xe + pltpu.roll(xe, 1, axis=0)
        s4 = s2 + pltpu.roll(s2, 2, axis=0)
        s8 = s4 + pltpu.roll(s4, 4, axis=0)
        s16 = s8 + pltpu.roll(s8, 8, axis=0)
        hp = HALO_POOL
        ws = _pool_select((ts, w), s2[hp:], s4[hp:], s8[hp:], s16[hp:])
        pooled = (ws / _pool_count((ts, w), i * ts) - cur).astype(BF16)
        pooled_ref[...] = pooled
        mixed = jnp.dot(pooled, w_ref[...], preferred_element_type=F32)
        tok_ref[...] = (mixed * sc_ref[...]).astype(BF16)

    row = pl.BlockSpec((ts, w), lambda r: (r, 0))
    return pl.pallas_call(
        body, name="pool_fwd", grid=(nb * nq,),
        in_specs=[row, pl.BlockSpec((HALO_POOL, w), lambda r: (jnp.maximum(r * (ts // HALO_POOL) - 1, 0), 0)),
                  pl.BlockSpec((w, w), lambda r: (0, 0)), pl.BlockSpec((1, w), lambda r: (0, 0))],
        out_specs=[row, row], out_shape=[SDS((nb * s, w), BF16), SDS((nb * s, w), BF16)],
        compiler_params=_params(("parallel",)))(proj, proj, pw_bd, pscale)


def pool_bwd_mix(dcat, pooled, pw_bd, pscale, nb, s):
    ts = min(TS, s)
    w = TOK_WIDTH

    def body(dt_ref, p_ref, w_ref, sc_ref, dm_ref, dp_ref, ds_ref):
        r = pl.program_id(0)
        dtok = dt_ref[...]
        mixed = jnp.dot(p_ref[...], w_ref[...], preferred_element_type=F32)

        @pl.when(r == 0)
        def _():
            ds_ref[...] = jnp.zeros_like(ds_ref)

        ds_ref[...] += jnp.sum(dtok * mixed, axis=0, keepdims=True)
        dmx = (dtok * sc_ref[...]).astype(BF16)
        dm_ref[...] = dmx
        dp_ref[...] = lax.dot_general(dmx, w_ref[...], NT_DIMS, preferred_element_type=F32)

    row = pl.BlockSpec((ts, w), lambda r: (r, 0))
    mat = pl.BlockSpec((w, w), lambda r: (0, 0))
    vec = pl.BlockSpec((1, w), lambda r: (0, 0))
    return pl.pallas_call(
        body, name="pool_bwd_mix", grid=(nb * s // ts,), in_specs=[row, row, mat, vec],
        out_specs=[row, row, vec], out_shape=[SDS((nb * s, w), BF16), SDS((nb * s, w), F32), SDS((1, w), F32)],
        compiler_params=_params(("arbitrary",)))(dcat, pooled, pw_bd, pscale)


def pool_bwd_window(dpooled, nb, s):
    ts = min(TS, s)
    nq = s // ts
    w = TOK_WIDTH
    n_ext = ts + HALO_POOL
    n_halo_blocks = nb * s // HALO_POOL

    def body(c_ref, n_ref, du_ref):
        i = pl.program_id(0) % nq
        cur = c_ref[...]
        nxt = jnp.where(i == nq - 1, 0.0, n_ref[...])
        ze = jnp.concatenate([cur, nxt], axis=0) / _pool_count((n_ext, w), i * ts)
        s2 = ze + pltpu.roll(ze, n_ext - 1, axis=0)
        s4 = s2 + pltpu.roll(s2, n_ext - 2, axis=0)
        s8 = s4 + pltpu.roll(s4, n_ext - 4, axis=0)
        s16 = s8 + pltpu.roll(s8, n_ext - 8, axis=0)
        ws = _pool_select((ts, w), s2[:ts], s4[:ts], s8[:ts], s16[:ts])
        du_ref[...] = (ws - cur).astype(BF16)

    row = pl.BlockSpec((ts, w), lambda r: (r, 0))
    return pl.pallas_call(
        body, name="pool_bwd_window", grid=(nb * nq,),
        in_specs=[row, pl.BlockSpec((HALO_POOL, w),
                                    lambda r: (jnp.minimum((r + 1) * (ts // HALO_POOL), n_halo_blocks - 1), 0))],
        out_specs=row, out_shape=SDS((nb * s, w), BF16),
        compiler_params=_params(("parallel",)))(dpooled, dpooled)


def _conv_rows(xe, w_ref):
    return (w_ref[0, 2:3, :] * xe + w_ref[0, 1:2, :] * pltpu.roll(xe, 1, axis=0)
            + w_ref[0, 0:1, :] * pltpu.roll(xe, 2, axis=0) + w_ref[0, 3:4, :])


def ffn_up_gate(x_bf, wup, cw, nb, s, name):
    tm = min(2 * TM, s)
    nq = s // tm
    w = FF_BLOCK_PAD
    hr = 2 * HALO_CONV
    k = x_bf.shape[1]

    def body(xc_ref, xh_ref, wu_ref, wg_ref, cu_ref, cg_ref, act_ref, a_ref, b_ref, hu_ref, hg_ref):
        first = (pl.program_id(1) % nq) == 0
        xc = xc_ref[...]
        xh = xh_ref[...]

        def products(w_ref):
            return (jnp.dot(xc, w_ref[...], preferred_element_type=F32), jnp.dot(xh, w_ref[...], preferred_element_type=F32))

        def conv(hcur, hprev, c_ref, h_out):
            h_out[...] = hcur.astype(BF16)
            xe = jnp.concatenate([jnp.where(first, 0.0, hprev), hcur], axis=0)
            return _conv_rows(xe, c_ref)[hr:]

        pu, pg = products(wu_ref), products(wg_ref)
        cu = conv(*pu, cu_ref, hu_ref)
        cg = conv(*pg, cg_ref, hg_ref)
        sg = _sigmoid(cg)
        a = cg * sg
        act_ref[...] = (a * cu).astype(BF16)
        a_ref[...] = a.astype(BF16)
        b_ref[...] = (cu * (sg * (1.0 + cg * (1.0 - sg)))).astype(BF16)

    def wblock(off):
        return pl.BlockSpec((k, w), lambda j, r: (0, j + off))

    def cblock(off):
        return pl.BlockSpec((1, 8, w), lambda j, r: (j + off, 0, 0))

    tile = pl.BlockSpec((tm, w), lambda j, r: (r, j))
    out = SDS((nb * s, FF_PAIRS * w), BF16)
    return pl.pallas_call(
        body, name=name, grid=(FF_PAIRS, nb * nq),
        in_specs=[pl.BlockSpec((tm, k), lambda j, r: (r, 0)),
                  pl.BlockSpec((hr, k), lambda j, r: (jnp.maximum(r * (tm // hr) - 1, 0), 0)),
                  wblock(0), wblock(FF_PAIRS), cblock(0), cblock(FF_PAIRS)],
        out_specs=[tile] * 5, out_shape=[out] * 5,
        compiler_params=_params(("parallel", "parallel")))(x_bf, x_bf, wup, wup, cw, cw)


def gate_conv_bwd(dact, a, b, hu, hg, cw, nb, s, name):
    ts = min(TS, s)
    nq = s // ts
    w = FF_BLOCK_PAD
    hc = HALO_CONV
    hb = 2 * hc
    n_ext = ts + hc

    def body(dc_ref, dn_ref, ac_ref, an_ref, bc_ref, bn_ref, hu_ref, hg_ref, wu_ref, wg_ref,
             dhu_ref, dhg_ref, dwu_ref, dwg_ref):
        r = pl.program_id(1)
        last = (r % nq) == nq - 1

        def ext(c_ref, n_ref, mask_next=False):
            nxt = n_ref[...].astype(F32)[:hc]
            if mask_next:
                nxt = jnp.where(last, 0.0, nxt)
            return jnp.concatenate([c_ref[...].astype(F32), nxt], axis=0)

        da = ext(dc_ref, dn_ref, mask_next=True)

        def branch(dcv, w_ref, h_ref, dh_ref, dw_ref):
            d0 = dcv[:ts]
            d1 = pltpu.roll(dcv, n_ext - 1, axis=0)[:ts]
            d2 = pltpu.roll(dcv, n_ext - 2, axis=0)[:ts]
            dh_ref[...] = (w_ref[0, 2:3, :] * d0 + w_ref[0, 1:2, :] * d1 + w_ref[0, 0:1, :] * d2).astype(BF16)
            hv = h_ref[...].astype(F32)
            rows = [jnp.sum(d2 * hv, axis=0, keepdims=True), jnp.sum(d1 * hv, axis=0, keepdims=True),
                    jnp.sum(d0 * hv, axis=0, keepdims=True), jnp.sum(d0, axis=0, keepdims=True)]
            sub = lax.broadcasted_iota(jnp.int32, (8, w), 0)
            upd = jnp.zeros((8, w), F32)
            for kk, rv in enumerate(rows):
                upd = jnp.where(sub == kk, rv, upd)

            @pl.when(r == 0)
            def _():
                dw_ref[...] = jnp.zeros_like(dw_ref)

            dw_ref[...] += upd[None]

        branch(da * ext(ac_ref, an_ref), wu_ref, hu_ref, dhu_ref, dwu_ref)
        branch(da * ext(bc_ref, bn_ref), wg_ref, hg_ref, dhg_ref, dwg_ref)

    cur = pl.BlockSpec((ts, w), lambda j, r: (r, j))
    nxt = pl.BlockSpec((hb, w), lambda j, r: (jnp.minimum((r + 1) * (ts // hb), nb * s // hb - 1), j))

    def wspec(off):
        return pl.BlockSpec((1, 8, w), lambda j, r: (j + off, 0, 0))

    p = FF_PAIRS
    dw_spec = pl.BlockSpec((1, 8, w), lambda j, r: (j, 0, 0))
    return pl.pallas_call(
        body, name=name, grid=(p, nb * nq),
        in_specs=[cur, nxt, cur, nxt, cur, nxt, cur, cur, wspec(0), wspec(p)],
        out_specs=[cur, cur, dw_spec, dw_spec],
        out_shape=[SDS((nb * s, p * w), BF16), SDS((nb * s, p * w), BF16), SDS((p, 8, w), F32), SDS((p, 8, w), F32)],
        compiler_params=_params(("parallel", "arbitrary")))(dact, dact, a, a, b, b, hu, hg, cw, cw)


def _tri(n, upper):
    r = lax.broadcasted_iota(jnp.int32, (n, n), 0)
    c = lax.broadcasted_iota(jnp.int32, (n, n), 1)
    return ((r <= c) if upper else (r >= c)).astype(F32)


def fgate_fwd(fl, fb, nb, s):
    tc = min(TC, s)
    nq = s // tc

    def body(fl_ref, fb_ref, f_ref, carry):
        @pl.when(pl.program_id(1) == 0)
        def _():
            carry[...] = jnp.zeros_like(carry)

        z = fl_ref[...] + fb_ref[...]
        logf = jnp.minimum(z, 0.0) - jnp.log(1.0 + jnp.exp(-jnp.abs(z)))
        f_ref[...] = jnp.dot(_tri(tc, False), logf, preferred_element_type=F32,
                             precision=lax.Precision.HIGHEST) + carry[...]
        carry[...] += jnp.sum(logf, axis=0, keepdims=True)

    row = pl.BlockSpec((tc, LANES), lambda b, i: (b * nq + i, 0))
    return pl.pallas_call(
        body, name="fgate_fwd", grid=(nb, nq), in_specs=[row, pl.BlockSpec((1, LANES), lambda b, i: (0, 0))],
        out_specs=row, out_shape=SDS((nb * s, LANES), F32), scratch_shapes=[pltpu.VMEM((1, LANES), F32)],
        compiler_params=_params(("arbitrary", "arbitrary")))(fl, fb)


def fgate_bwd(d_cum_q, d_cum_k, fl, fb, nb, s):
    tc = min(TC, s)
    nq = s // tc

    def body(dfq_ref, dfk_ref, fl_ref, fb_ref, dfl_ref, dfb_ref, carry):
        b = pl.program_id(0)
        i = pl.program_id(1)

        @pl.when(i == 0)
        def _():
            carry[...] = jnp.zeros_like(carry)

        @pl.when(jnp.logical_and(b == 0, i == 0))
        def _():
            dfb_ref[...] = jnp.zeros_like(dfb_ref)

        dfv = dfq_ref[...] + dfk_ref[...]
        dlog = jnp.dot(_tri(tc, True), dfv, preferred_element_type=F32,
                       precision=lax.Precision.HIGHEST) + carry[...]
        carry[...] += jnp.sum(dfv, axis=0, keepdims=True)
        z = fl_ref[...] + fb_ref[...]
        dfl = dlog / (1.0 + jnp.exp(z))
        dfl_ref[...] = dfl
        dfb_ref[...] += jnp.sum(dfl, axis=0, keepdims=True)

    row = pl.BlockSpec((tc, LANES), lambda b, i: (b * nq + nq - 1 - i, 0))
    vec = pl.BlockSpec((1, LANES), lambda b, i: (0, 0))
    return pl.pallas_call(
        body, name="fgate_bwd", grid=(nb, nq), in_specs=[row, row, row, vec], out_specs=[row, vec],
        out_shape=[SDS((nb * s, LANES), F32), SDS((1, LANES), F32)], scratch_shapes=[pltpu.VMEM((1, LANES), F32)],
        compiler_params=_params(("arbitrary", "arbitrary")))(d_cum_q, d_cum_k, fl, fb)


PAIR = 2 * HEAD_DIM
N_PAIRS = FOX_HEADS // 2


def _lane_put(shape, h, col):
    lane = lax.broadcasted_iota(jnp.int32, shape, 1)
    return jnp.where(lane == h, col, 0.0)


def _half_masks(rows):
    lane = lax.broadcasted_iota(jnp.int32, (rows, PAIR), 1)
    return lane < HEAD_DIM


def _split_pair(x, scale=None):
    if scale is not None:
        x = x * scale
    lo = _half_masks(x.shape[0])
    zero = jnp.zeros_like(x)
    return jnp.where(lo, x, zero), jnp.where(lo, zero, x)


def _to_tile_rows(a, nb, s, tf):
    return a.reshape(nb * s // tf, tf, LANES)[:, :, :16].transpose(0, 2, 1)


def _from_tile_rows(a):
    tiles, _, tf = a.shape
    return jnp.pad(a.transpose(0, 2, 1), ((0, 0), (0, 0), (0, LANES - 16))).reshape(tiles * tf, LANES)


BIAS_TERMS = 3
LOOKAHEAD = 4
FOLLOW_FWD = 1
LOOKAHEAD_BWD = 2
FOLLOW_BWD = 1


def _bias_lane(h):
    return HEAD_DIM if h % 2 == 0 else 0


def _placement():
    rows = jnp.arange(LANES)[:, None]
    cols = jnp.arange(FOX_HEADS * PAIR)[None, :]
    head, lane = cols // PAIR, cols % PAIR
    first = jnp.where(head % 2 == 0, HEAD_DIM, 0)
    term = lane - first
    hit = (term >= 0) & (term < BIAS_TERMS) & (rows == 16 * term + head)
    return hit.astype(BF16)


def fox_prep(kv, fneg, nb, s):
    tf = min(TF, s)
    w = TOK_WIDTH

    def body(k_ref, v_ref, f_ref, pl_ref, ka_ref, vt_ref):
        lane = lax.broadcasted_iota(jnp.int32, (tf, LANES), 1)
        lo = lane < HEAD_DIM
        f = jnp.where(lane < FOX_HEADS, f_ref[...], 0.0)
        hi = f.astype(BF16).astype(F32)
        mid = (f - hi).astype(BF16).astype(F32)
        low = (f - hi - mid).astype(BF16).astype(F32)
        terms = (hi + pltpu.roll(mid, 16, axis=1) + pltpu.roll(low, 32, axis=1)).astype(BF16)
        placed = jnp.dot(terms, pl_ref[...], preferred_element_type=F32).astype(BF16)
        one = jnp.ones((tf, LANES), BF16)
        zero = jnp.zeros((tf, LANES), BF16)
        for p in range(N_PAIRS):
            kp = k_ref[:, p * PAIR:(p + 1) * PAIR] * QK_SCALE
            vp = v_ref[:, p * PAIR:(p + 1) * PAIR]
            he, ho = 2 * p, 2 * p + 1
            ka_ref[:, he * PAIR:(he + 1) * PAIR] = jnp.where(lo, kp, placed[:, he * PAIR:(he + 1) * PAIR])
            ka_ref[:, ho * PAIR:(ho + 1) * PAIR] = jnp.where(lo, placed[:, ho * PAIR:(ho + 1) * PAIR], kp)
            ve = jnp.where(lo, vp, jnp.where(lane == HEAD_DIM, one, zero))
            vo = jnp.where(lo, jnp.where(lane == 0, one, zero), vp)
            vt_ref[0, he * PAIR:(he + 1) * PAIR, :] = ve.astype(F32).T.astype(BF16)
            vt_ref[0, ho * PAIR:(ho + 1) * PAIR, :] = vo.astype(F32).T.astype(BF16)

    return pl.pallas_call(
        body, name="fox_prep", grid=(nb * s // tf,),
        in_specs=[pl.BlockSpec((tf, w), lambda r: (r, 0)), pl.BlockSpec((tf, w), lambda r: (r, 1)),
                  pl.BlockSpec((tf, LANES), lambda r: (r, 0)), pl.BlockSpec((LANES, FOX_HEADS * PAIR), lambda r: (0, 0))],
        out_specs=[pl.BlockSpec((tf, FOX_HEADS * PAIR), lambda r: (r, 0)),
                   pl.BlockSpec((1, FOX_HEADS * PAIR, tf), lambda r: (r, 0, 0))],
        out_shape=[SDS((nb * s, FOX_HEADS * PAIR), BF16), SDS((nb * s // tf, FOX_HEADS * PAIR, tf), BF16)],
        compiler_params=_params(("parallel",)))(kv, kv, fneg, _placement())


def fox_fwd_t(pq, kaug, vaug_t, nb, s):
    tf = min(TF, s)
    n = s // tf
    w = TOK_WIDTH
    wa = FOX_HEADS * PAIR

    def body(q_ref, k_hbm, vt_hbm, ob_ref, of_ref, lse_ref, k_vm, vt_vm, qx_scr, m_scr, acc_scr, sems):
        b = pl.program_id(0)
        i = pl.program_id(1)

        @pl.when(i == 0)
        def _():
            ck = pltpu.make_async_copy(k_hbm.at[pl.ds(pl.multiple_of(b * s, tf), s)], k_vm, sems.at[0])
            cv = pltpu.make_async_copy(vt_hbm.at[pl.ds(b * n, n)], vt_vm, sems.at[1])
            ck.start()
            cv.start()
            ck.wait()
            cv.wait()

        lane = lax.broadcasted_iota(jnp.int32, (tf, PAIR), 1)
        one = jnp.ones((tf, PAIR), BF16)
        zero = jnp.zeros((tf, PAIR), BF16)
        for p in range(N_PAIRS):
            qp = q_ref[:, p * PAIR:(p + 1) * PAIR]
            be, bo = _bias_lane(2 * p), _bias_lane(2 * p + 1)
            ones_e = jnp.where((lane >= be) & (lane < be + BIAS_TERMS), one, zero)
            ones_o = jnp.where((lane >= bo) & (lane < bo + BIAS_TERMS), one, zero)
            qx_scr[2 * p] = jnp.where(lane < HEAD_DIM, qp, ones_e)
            qx_scr[2 * p + 1] = jnp.where(lane < HEAD_DIM, ones_o, qp)
        m_scr[...] = jnp.full(m_scr.shape, NEG_BIG, F32)
        acc_scr[...] = jnp.zeros_like(acc_scr)

        def tile(j, masked):
            ks = pl.multiple_of(j * tf, tf)
            if masked:
                keep = lax.broadcasted_iota(jnp.int32, (tf, tf), 1) >= lax.broadcasted_iota(jnp.int32, (tf, tf), 0)
            def scores(h):
                kx = k_vm[pl.ds(ks, tf), h * PAIR:(h + 1) * PAIR]
                return lax.dot_general(kx, qx_scr[h], NT_DIMS, preferred_element_type=F32)

            def values(h, pr, a):
                pv = jnp.dot(vt_vm[j, h * PAIR:(h + 1) * PAIR, :], pr, preferred_element_type=F32)
                acc_scr[h] = a * acc_scr[h] + pv

            ahead = [scores(h) for h in range(LOOKAHEAD)]
            behind = []
            for h in range(FOX_HEADS):
                sc = ahead.pop(0)
                if h + LOOKAHEAD < FOX_HEADS:
                    ahead.append(scores(h + LOOKAHEAD))
                if masked:
                    sc = jnp.where(keep, sc, NEG_BIG)
                m_prev = m_scr[h]
                m_new = jnp.maximum(m_prev, jnp.max(sc, axis=0, keepdims=True))
                m_scr[h] = m_new
                behind.append((h, jnp.exp(sc - m_new).astype(BF16), jnp.exp(m_prev - m_new)))
                if len(behind) > FOLLOW_FWD:
                    values(*behind.pop(0))
            for item in behind:
                values(*item)

        def step(j, carry):
            tile(j, False)
            return carry

        lax.fori_loop(0, i, step, 0)
        tile(i, True)

        top = lax.broadcasted_iota(jnp.int32, (PAIR, tf), 0) < HEAD_DIM
        sub = lax.broadcasted_iota(jnp.int32, (16, tf), 0)
        lse = jnp.zeros((16, tf), F32)
        for p in range(N_PAIRS):
            he, ho = 2 * p, 2 * p + 1
            le = acc_scr[he, HEAD_DIM:HEAD_DIM + 1, :]
            lod = acc_scr[ho, 0:1, :]
            o = jnp.where(top, acc_scr[he] / le, acc_scr[ho] / lod).T
            ob_ref[:, p * PAIR:(p + 1) * PAIR] = o.astype(BF16)
            of_ref[:, p * PAIR:(p + 1) * PAIR] = o
            lse = jnp.where(sub == he, m_scr[he] + jnp.log(le), lse)
            lse = jnp.where(sub == ho, m_scr[ho] + jnp.log(lod), lse)
        lse_ref[0] = lse

    qrow = lambda b, i: (b * n + i, 0)
    return pl.pallas_call(
        body, name="fox_fwd", grid=(nb, n),
        in_specs=[pl.BlockSpec((tf, w), qrow), ANY_SPEC, ANY_SPEC],
        out_specs=[pl.BlockSpec((tf, w), qrow), pl.BlockSpec((tf, w), qrow),
                   pl.BlockSpec((1, 16, tf), lambda b, i: (b * n + i, 0, 0))],
        out_shape=[SDS((nb * s, w), BF16), SDS((nb * s, w), F32), SDS((nb * n, 16, tf), F32)],
        scratch_shapes=[pltpu.VMEM((s, wa), BF16), pltpu.VMEM((n, wa, tf), BF16),
                        pltpu.VMEM((FOX_HEADS, tf, PAIR), BF16), pltpu.VMEM((FOX_HEADS, 1, tf), F32),
                        pltpu.VMEM((FOX_HEADS, PAIR, tf), F32), pltpu.SemaphoreType.DMA((2,))],
        compiler_params=_params(("arbitrary", "arbitrary")))(pq, kaug, vaug_t)


def fox_delta(dcat, o, nb, s):
    tf = min(TM, s)
    w = TOK_WIDTH

    def body(do_ref, o_ref, dl_ref):
        out = jnp.zeros((tf, LANES), F32)
        for h in range(FOX_HEADS):
            lo, hi = h * HEAD_DIM, (h + 1) * HEAD_DIM
            out = out + _lane_put((tf, LANES), h, jnp.sum(do_ref[:, lo:hi] * o_ref[:, lo:hi], axis=1, keepdims=True))
        dl_ref[...] = out

    row = pl.BlockSpec((tf, w), lambda r: (r, 0))
    return pl.pallas_call(
        body, name="fox_delta", grid=(nb * s // tf,), in_specs=[row, row],
        out_specs=pl.BlockSpec((tf, LANES), lambda r: (r, 0)), out_shape=SDS((nb * s, LANES), F32),
        compiler_params=_params(("parallel",)))(dcat, o)


def fox_bwd(pq, kv, fneg, dcat_bf, lse_rows, delta_rows, nb, s):
    tf = min(TF, s)
    n = s // tf
    w = TOK_WIDTH

    def body(q_hbm, k_ref, v_ref, f_ref, do_hbm, lse_ref, dl_ref, dq_ref, dk_ref, dv_ref, dfk_ref, dfq_ref,
             q_vm, do_vm, km_scr, vm_scr, kt_scr, fk_scr, dk_scr, dv_scr, rs_scr, dq_scr, fq_scr, sems):
        b = pl.program_id(0)
        j = pl.program_id(1)

        @pl.when(j == 0)
        def _():
            rows = pl.ds(pl.multiple_of(b * s, tf), s)
            cq = pltpu.make_async_copy(q_hbm.at[rows, pl.ds(0, w)], q_vm, sems.at[0])
            cd = pltpu.make_async_copy(do_hbm.at[rows, pl.ds(0, w)], do_vm, sems.at[1])
            cq.start()
            cd.start()
            dq_scr[...] = jnp.zeros_like(dq_scr)
            fq_scr[...] = jnp.zeros_like(fq_scr)
            cq.wait()
            cd.wait()

        for p in range(N_PAIRS):
            kp = k_ref[:, p * PAIR:(p + 1) * PAIR] * QK_SCALE
            ke, ko = _split_pair(kp)
            km_scr[2 * p] = ke
            km_scr[2 * p + 1] = ko
            kt_scr[p] = kp.astype(F32).T.astype(BF16)
            ve, vo = _split_pair(v_ref[:, p * PAIR:(p + 1) * PAIR])
            vm_scr[2 * p] = ve
            vm_scr[2 * p + 1] = vo
        for h in range(FOX_HEADS):
            fk_scr[h] = jnp.broadcast_to(f_ref[:, h:h + 1], (tf, tf))
        dk_scr[...] = jnp.zeros_like(dk_scr)
        dv_scr[...] = jnp.zeros_like(dv_scr)
        rs_scr[...] = jnp.zeros_like(rs_scr)

        def tile(i, masked):
            qs = pl.multiple_of(i * tf, tf)
            if masked:
                keep = lax.broadcasted_iota(jnp.int32, (tf, tf), 1) >= lax.broadcasted_iota(jnp.int32, (tf, tf), 0)
            def products(h):
                qp = q_vm[pl.ds(qs, tf), (h // 2) * PAIR:(h // 2 + 1) * PAIR]
                dop = do_vm[pl.ds(qs, tf), (h // 2) * PAIR:(h // 2 + 1) * PAIR]
                return (lax.dot_general(km_scr[h], qp, NT_DIMS, preferred_element_type=F32),
                        lax.dot_general(vm_scr[h], dop, NT_DIMS, preferred_element_type=F32))

            def dependents(h, prb, dsb):
                p = h // 2
                half = slice((h % 2) * HEAD_DIM, (h % 2 + 1) * HEAD_DIM)
                qp = q_vm[pl.ds(qs, tf), p * PAIR:(p + 1) * PAIR]
                dop = do_vm[pl.ds(qs, tf), p * PAIR:(p + 1) * PAIR]
                dv_scr[h] += jnp.dot(prb, dop, preferred_element_type=F32)
                dk_scr[h] += jnp.dot(dsb, qp, preferred_element_type=F32)
                dqt = jnp.dot(kt_scr[p], dsb, preferred_element_type=F32)
                dq_scr[i, p, half, :] += dqt[(h % 2) * HEAD_DIM:(h % 2 + 1) * HEAD_DIM]

            ahead = [products(h) for h in range(LOOKAHEAD_BWD)]
            behind = []
            for h in range(FOX_HEADS):
                sc, dp = ahead.pop(0)
                if h + LOOKAHEAD_BWD < FOX_HEADS:
                    ahead.append(products(h + LOOKAHEAD_BWD))
                sc = sc + fk_scr[h] - lse_ref[i, h:h + 1, :]
                if masked:
                    sc = jnp.where(keep, sc, NEG_BIG)
                pr = jnp.exp(sc)
                ds = pr * (dp - dl_ref[i, h:h + 1, :])
                part = ds[:, :LANES]
                for c in range(1, tf // LANES):
                    part = part + ds[:, c * LANES:(c + 1) * LANES]
                rs_scr[h] += part
                fq_scr[i, h:h + 1, :] += jnp.sum(ds, axis=0, keepdims=True)
                behind.append((h, pr.astype(BF16), ds.astype(BF16)))
                if len(behind) > FOLLOW_BWD:
                    dependents(*behind.pop(0))
            for item in behind:
                dependents(*item)

        def step(i, carry):
            tile(i, False)
            return carry

        tile(j, True)
        for p in range(N_PAIRS):
            dq_ref[:, p * PAIR:(p + 1) * PAIR] = dq_scr[j, p].T.astype(BF16)
        dfq_ref[0] = fq_scr[j]
        lax.fori_loop(j + 1, n, step, 0)

        lo = _half_masks(tf)
        dfk = jnp.zeros((tf, LANES), F32)
        for p in range(N_PAIRS):
            dk = jnp.where(lo, dk_scr[2 * p], dk_scr[2 * p + 1]) * QK_SCALE
            dk_ref[:, p * PAIR:(p + 1) * PAIR] = dk.astype(BF16)
            dv_ref[:, p * PAIR:(p + 1) * PAIR] = jnp.where(lo, dv_scr[2 * p], dv_scr[2 * p + 1]).astype(BF16)
            for h in (2 * p, 2 * p + 1):
                dfk = dfk - _lane_put((tf, LANES), h, jnp.sum(rs_scr[h], axis=1, keepdims=True))
        dfk_ref[...] = dfk

    krow = lambda b, j: (b * n + j, 0)
    rows = pl.BlockSpec((n, 16, tf), lambda b, j: (b, 0, 0))
    tile_out = pl.BlockSpec((tf, w), krow)
    return pl.pallas_call(
        body, name="fox_bwd", grid=(nb, n),
        in_specs=[ANY_SPEC, pl.BlockSpec((tf, w), krow), pl.BlockSpec((tf, w), lambda b, j: (b * n + j, 1)),
                  pl.BlockSpec((tf, LANES), krow), ANY_SPEC, rows, rows],
        out_specs=[tile_out, tile_out, tile_out, pl.BlockSpec((tf, LANES), krow),
                   pl.BlockSpec((1, 16, tf), lambda b, j: (b * n + j, 0, 0))],
        out_shape=[SDS((nb * s, w), BF16), SDS((nb * s, w), BF16), SDS((nb * s, w), BF16), SDS((nb * s, LANES), F32),
                   SDS((nb * n, 16, tf), F32)],
        scratch_shapes=[pltpu.VMEM((s, w), BF16), pltpu.VMEM((s, w), BF16),
                        pltpu.VMEM((FOX_HEADS, tf, PAIR), BF16), pltpu.VMEM((FOX_HEADS, tf, PAIR), BF16),
                        pltpu.VMEM((N_PAIRS, PAIR, tf), BF16), pltpu.VMEM((FOX_HEADS, tf, tf), F32),
                        pltpu.VMEM((FOX_HEADS, tf, PAIR), F32), pltpu.VMEM((FOX_HEADS, tf, PAIR), F32),
                        pltpu.VMEM((FOX_HEADS, tf, LANES), F32), pltpu.VMEM((n, N_PAIRS, PAIR, tf), F32),
                        pltpu.VMEM((n, 16, tf), F32), pltpu.SemaphoreType.DMA((2,))],
        compiler_params=_params(("arbitrary", "arbitrary")))(pq, kv, kv, fneg, dcat_bf, lse_rows, delta_rows)


ADAMW_TILE_ELEMS = 128 * 1024


def reduce_adamw(parts, w, m, v, name):
    layers, r, c = w.shape
    tr = r
    for cand in range(16, r, 16):
        if r % cand == 0 and cand * c <= ADAMW_TILE_ELEMS:
            tr = cand
    c1 = 1.0 - ADAM_B1 ** ADAM_STEP
    c2 = 1.0 - ADAM_B2 ** ADAM_STEP

    def body(*refs):
        p_refs = refs[:layers]
        w_ref, m_ref, v_ref, g_out, d_out, m_out, v_out = refs[layers:]

        def update(p_ref):
            g = p_ref[0].astype(F32)
            for k in range(1, N_DEV):
                g = g + p_ref[k].astype(F32)
            mn = ADAM_B1 * m_ref[0] + (1.0 - ADAM_B1) * g
            vn = ADAM_B2 * v_ref[0] + (1.0 - ADAM_B2) * (g * g)
            g_out[0] = g
            m_out[0] = mn
            v_out[0] = vn
            d_out[0] = -ADAM_LR * ((mn / c1) / (jnp.sqrt(vn / c2) + ADAM_EPS) + ADAM_WD * w_ref[0])

        if layers == 1:
            update(p_refs[0])
        else:
            for layer in range(layers):
                pl.when(pl.program_id(0) == layer)(lambda layer=layer: update(p_refs[layer]))

    row = pl.BlockSpec((1, tr, c), lambda l, i: (l, i, 0))
    return pl.pallas_call(
        body, name=name, grid=(layers, r // tr),
        in_specs=[pl.BlockSpec((N_DEV, tr, c), lambda l, i: (0, i, 0))] * layers + [row, row, row],
        out_specs=[row, row, row, row], out_shape=[SDS((layers, r, c), F32)] * 4,
        compiler_params=_params(("parallel", "parallel")))(*parts, w, m, v)


N_PEERS = N_DEV - 1
HBM_SPEC = pl.BlockSpec(memory_space=pltpu.HBM)
SEM_SPEC = pl.BlockSpec(memory_space=pltpu.SEMAPHORE)
ANY_SPEC = pl.BlockSpec(memory_space=pl.ANY)
SPLIT_EFFECT = pltpu.SideEffectType.DATAFLOW_SIDE_EFFECTING


def _peers(with_self=False):
    x, y, c = lax.axis_index("x"), lax.axis_index("y"), lax.axis_index("c")
    peers = []
    for k in range(0 if with_self else 1, N_DEV):
        px = 1 - x if (k >> 2) & 1 else x
        py = 1 - y if (k >> 1) & 1 else y
        pc = 1 - c if k & 1 else c
        peers.append(((px, py, pc), 4 * px + 2 * py + pc))
    return 4 * x + 2 * y + c, peers


def _push(src, dst, send_sems, recv_sems, slot, dev):
    return pltpu.make_async_remote_copy(src_ref=src, dst_ref=dst, send_sem=send_sems.at[slot], recv_sem=recv_sems.at[slot],
                                        device_id=dev, device_id_type=pl.DeviceIdType.MESH)


def _landing_shapes(arrs, scatter):
    return [SDS((N_DEV,) + tuple(a.shape[1:] if sc else a.shape), a.dtype) for a, sc in zip(arrs, scatter)]


def exchange(arrs, scatter, name):
    na = len(arrs)

    def body(*refs):
        ins = refs[:na]
        outs = refs[na:2 * na]
        send_sems, recv_sems, local_sems = refs[2 * na:]
        me, peers = _peers()
        local = []
        remote = []
        for a in range(na):
            lc = pltpu.make_async_copy(ins[a].at[me] if scatter[a] else ins[a], outs[a].at[me], local_sems.at[a])
            lc.start()
            local.append(lc)
            for k, (dev, idx) in enumerate(peers):
                cp = _push(ins[a].at[idx] if scatter[a] else ins[a], outs[a].at[me], send_sems, recv_sems,
                           a * N_PEERS + k, dev)
                cp.start()
                remote.append(cp)
        for a in range(na):
            for k, (dev, idx) in enumerate(peers):
                _push(ins[a].at[me] if scatter[a] else ins[a], outs[a].at[idx], send_sems, recv_sems,
                      a * N_PEERS + k, dev).wait_recv()
        for cp in remote:
            cp.wait_send()
        for lc in local:
            lc.wait()

    return pl.pallas_call(
        body, name=name, in_specs=[HBM_SPEC] * na, out_specs=[HBM_SPEC] * na, out_shape=_landing_shapes(arrs, scatter),
        scratch_shapes=[pltpu.SemaphoreType.DMA((na * N_PEERS,)), pltpu.SemaphoreType.DMA((na * N_PEERS,)),
                        pltpu.SemaphoreType.DMA((na,))])(*arrs)


def exchange_start(arrs, scatter, after, name):
    na = len(arrs)
    lands = [lax.empty(l.shape, l.dtype) for l in _landing_shapes(arrs, scatter)]

    def body(*refs):
        ins = refs[:na]
        land = refs[na:2 * na]
        send_sems, recv_sems = refs[2 * na + 1], refs[2 * na + 2]
        token = refs[-1]
        me, peers = _peers(with_self=True)
        for a in range(na):
            for k, (dev, idx) in enumerate(peers):
                _push(ins[a].at[idx] if scatter[a] else ins[a], land[a].at[me], send_sems, recv_sems,
                      a * N_DEV + k, dev).start()
        token[...] = jnp.zeros_like(token)

    thru = [pltpu.HBM(a.shape, a.dtype) for a in arrs] + [pltpu.HBM(l.shape, l.dtype) for l in lands]
    res = pl.pallas_call(
        body, name=name,
        out_shape=(pltpu.SemaphoreType.DMA((na * N_DEV,)), pltpu.SemaphoreType.DMA((na * N_DEV,)), *thru,
                   SDS((8, LANES), F32)),
        in_specs=[HBM_SPEC] * (2 * na) + [ANY_SPEC],
        out_specs=(SEM_SPEC, SEM_SPEC, *([HBM_SPEC] * (2 * na)), pl.BlockSpec(memory_space=pltpu.VMEM)),
        input_output_aliases={i: 2 + i for i in range(2 * na)},
        compiler_params=pltpu.CompilerParams(has_side_effects=SPLIT_EFFECT),
    )(*[pltpu.with_memory_space_constraint(a, pltpu.HBM) for a in arrs],
      *[pltpu.with_memory_space_constraint(l, pltpu.HBM) for l in lands], after)
    return {"send": res[0], "recv": res[1], "src": res[2:2 + na], "land": res[2 + na:2 + 2 * na],
            "token": res[-1][0, 0], "scatter": scatter}


def exchange_wait(handle, after, name):
    scatter = handle["scatter"]
    na = len(scatter)

    def body(*refs):
        src = refs[:na]
        land = refs[na:2 * na]
        send_sems, recv_sems = refs[2 * na], refs[2 * na + 1]
        me, peers = _peers(with_self=True)
        for a in range(na):
            for k, (dev, idx) in enumerate(peers):
                cp = _push(src[a].at[me] if scatter[a] else src[a], land[a].at[idx], send_sems, recv_sems,
                           a * N_DEV + k, dev)
                cp.wait_send()
                cp.wait_recv()

    ops = list(handle["src"]) + list(handle["land"])
    res = pl.pallas_call(
        body, name=name, out_shape=tuple(pltpu.HBM(o.shape, o.dtype) for o in ops),
        in_specs=[HBM_SPEC] * (2 * na) + [SEM_SPEC, SEM_SPEC, ANY_SPEC], out_specs=tuple([HBM_SPEC] * (2 * na)),
        input_output_aliases={i: i for i in range(2 * na)},
        compiler_params=pltpu.CompilerParams(has_side_effects=SPLIT_EFFECT),
    )(*ops, handle["send"], handle["recv"], after)
    return list(res[na:])


def forward_layer(l, xin, xin_bf, mem_bf, wt, nb, s, ffn_weights=None):
    sv = {"xin_bf": xin_bf}
    memkv = mm_nn(mem_bf, wt["memw"], BF16, f"memkv{l}")
    sv["memkv"] = memkv
    if l == 0:
        proj = mm_nn(xin_bf, wt["win_a"], F32, "proj_a")
        pooled, tok = pool_fwd(proj, wt["pw_bd"], wt["pscale"], nb, s)
        sv["pooled"] = pooled
    else:
        kv = mm_nn(xin_bf, wt["kvw"][:, :2 * TOK_WIDTH], BF16, "kv_proj")
        fl = mm_nn(xin_bf, wt["kvw"][:, 2 * TOK_WIDTH:], F32, "gate_proj")
        fneg = -fgate_fwd(fl, wt["fb"], nb, s)
        proj = mm_nn(xin_bf, wt["wq"], BF16, "proj_b")
        kaug, vaug_t = fox_prep(kv, fneg, nb, s)
        tok, o_f32, lse_rows = fox_fwd_t(proj, kaug, vaug_t, nb, s)
        sv.update(kv=kv, fl=fl, fneg=fneg, o_f32=o_f32, lse_rows=lse_rows)
    sv["proj"] = proj
    mem_out = memattn_fwd(proj, memkv, nb, s, f"memattn_fwd{l}")
    cat = jnp.concatenate([tok, mem_out], axis=1)
    sv["cat"] = cat
    x1, x1_bf, xh1, rs1 = ln_fwd(xin, cat, wt["wout"], wt["ln1_g"], wt["ln1_b"], f"out_proj_ln1_{l}")
    sv.update(x1_bf=x1_bf, xh1=xh1, rs1=rs1)
    if ffn_weights is not None:
        wt.update(ffn_weights(x1_bf))
    act, ga, gb, hu, hg = ffn_up_gate(x1_bf, wt["wup"], wt["cw"], nb, s, f"ffn_up_gate{l}")
    sv.update(act=act, ga=ga, gb=gb, hu=hu, hg=hg)
    x2, x2_bf, xh2, rs2 = ln_fwd(x1, act, wt["wdown"], wt["ln2_g"], wt["ln2_b"], f"ffn_down_ln2_{l}")
    sv.update(xh2=xh2, rs2=rs2)
    return x2, x2_bf, sv


def backward_layer(l, dy, sv, mem_bf, wt, nb, s, after_ffn=None, after_pool=None, dy_product=None, loss_target=None):
    g = {}
    if loss_target is None:
        dr2, dr2_bf, g["ln2_g"], g["ln2_b"] = ln_bwd(dy, sv["xh2"], sv["rs2"], wt["ln2_g"], f"ln2_bwd{l}",
                                                     product=dy_product)
    else:
        dr2, dr2_bf, g["ln2_g"], g["ln2_b"], g["loss_row"] = loss_ln_bwd(sv["xh2"], sv["rs2"], wt["ln2_g"], wt["ln2_b"],
                                                                         loss_target, f"loss_ln2_bwd{l}")
    dact = mm_nn(dr2_bf, wt["wdown"], BF16, f"ffn_down_dx{l}", trans_b=0)
    g["wdown"] = mm_tn(sv["act"], dr2_bf, f"ffn_down_dw{l}")
    dh_u, dh_g, dcw_u, dcw_g = gate_conv_bwd(dact, sv["ga"], sv["gb"], sv["hu"], sv["hg"], wt["cw"], nb, s,
                                             f"gate_conv_bwd{l}")
    g["cw"] = jnp.concatenate([dcw_u, dcw_g], axis=0)
    dx1 = mm_nn(dh_u, wt["wup"], F32, f"ffn_up_dx_u{l}", addend=dr2, add_scale=DN_ALPHA, trans_b=0)
    g["wup"] = jnp.concatenate([mm_tn(sv["x1_bf"], dh_u, f"ffn_up_dw_u{l}", blocked=True),
                                mm_tn(sv["x1_bf"], dh_g, f"ffn_up_dw_g{l}", blocked=True)], axis=0)
    ln1_g = wt["ln1_g"] if after_ffn is None else wt["ln1_g"] + after_ffn(g, dx1)
    dr1, dr1_bf, g["ln1_g"], g["ln1_b"] = ln_bwd(dx1, sv["xh1"], sv["rs1"], ln1_g, f"ffn_up_dx_g_ln1_bwd{l}",
                                                 product=(dh_g, wt["wup"], 1))
    dcat, dcat_bf = mm_nn(dr1_bf, wt["wout"], F32, f"out_proj_dx{l}", also_bf16=True, trans_b=0)
    g["wout"] = mm_tn(sv["cat"], dr1_bf, f"out_proj_dw{l}")
    dqm, dmemkv = memattn_bwd(sv["proj"], sv["memkv"], dcat, nb, s, f"memattn_bwd{l}")
    g["memw"] = mm_tn(mem_bf, dmemkv, f"memkv_dw{l}")
    if l == 0:
        dmixed, dpooled, g["pscale"] = pool_bwd_mix(dcat, sv["pooled"], wt["pw_bd"], wt["pscale"], nb, s)
        g["pw_full"] = mm_tn(sv["pooled"], dmixed, "pool_dw")
        win_a = wt["win_a"] if after_pool is None else wt["win_a"] + after_pool(g, dmixed).astype(BF16)
        du = pool_bwd_window(dpooled, nb, s)
        dproj = jnp.concatenate([du, dqm], axis=1)
        dx = mm_nn(dproj, win_a, F32, "proj_a_dx", addend=dr1, add_scale=DN_ALPHA, trans_b=0)
        g["win_a"] = mm_tn(sv["xin_bf"], dproj, "proj_a_dw")
        pending = None
    else:
        delta = fox_delta(dcat, sv["o_f32"], nb, s)
        tf = min(TF, s)
        dq, dk, dv, dfcum_k, dfq_rows = fox_bwd(sv["proj"], sv["kv"], sv["fneg"], dcat_bf,
                                                sv["lse_rows"], _to_tile_rows(delta, nb, s, tf), nb, s)
        dfl, g["fb"] = fgate_bwd(_from_tile_rows(dfq_rows), dfcum_k, sv["fl"], wt["fb"], nb, s)
        dproj = jnp.concatenate([dq, dqm], axis=1)
        dkvf = jnp.concatenate([dk, dv, dfl.astype(BF16)], axis=1)
        dx = mm_nn(dproj, wt["wq"], F32, "proj_b_dx", addend=dr1, add_scale=DN_ALPHA, trans_b=0)
        pending = (dkvf, wt["kvw"], 0)
        g["wq"] = mm_tn(sv["xin_bf"], dproj, "proj_b_dw")
        g["kvw"] = mm_tn(sv["xin_bf"], dkvf, "kv_proj_dw")
    return dx, pending, g


def pack_replicated(pool_w, ln1_g, ln1_b, ln2_g, ln2_b, conv_b, f_b):
    cb = jnp.pad(conv_b, ((0, 0), (0, 6144 - 5504))).reshape(12, D_MODEL)
    fb = jnp.pad(f_b.reshape(1, FOX_HEADS), ((0, 3), (0, D_MODEL - FOX_HEADS)))
    return jnp.concatenate([pool_w.reshape(144, D_MODEL), ln1_g, ln1_b, ln2_g, ln2_b, cb, fb], axis=0)


def unpack_replicated(buf):
    pool_w = buf[:144].reshape(1, 4, POOL_GROUP, POOL_GROUP)
    ln = [buf[144 + 2 * k:146 + 2 * k] for k in range(4)]
    conv_b = buf[152:164].reshape(2, 6144)[:, :5504]
    f_b = buf[164, :FOX_HEADS]
    return pool_w, ln[0], ln[1], ln[2], ln[3], conv_b, f_b


def pack_small(conv_w, pool_scale):
    buf = jnp.zeros((16, FF_BLOCK_PAD), F32)
    buf = lax.dynamic_update_slice(buf, conv_w.reshape(DEPTH * 3, FF_BLOCK), (0, 0))
    return lax.dynamic_update_slice(buf, pool_scale, (8, 0))


def _block_diag(pw):
    out = jnp.zeros((TOK_WIDTH, TOK_WIDTH), pw.dtype)
    for g in range(4):
        out = lax.dynamic_update_slice(out, pw[g], (g * POOL_GROUP, g * POOL_GROUP))
    return out


def layer_shards(l, sq_a, sq_b, mem_w_kv, ffn_w_up, ffn_w_down):
    return [sq_a[0].astype(BF16), sq_b[0].astype(BF16), mem_w_kv[l].astype(BF16), ffn_w_up[l].astype(BF16),
            ffn_w_down[l].astype(BF16)]


def mixer_weights(l, gath, ln1_g, ln1_b, ln2_g, ln2_b):
    w_out = gath[1].reshape(D_MODEL, D_MODEL)
    wt = {"memw": gath[2].reshape(D_MODEL, 2 * MEM_WIDTH), "wout": w_out,
          "ln1_g": ln1_g[l:l + 1], "ln1_b": ln1_b[l:l + 1], "ln2_g": ln2_g[l:l + 1], "ln2_b": ln2_b[l:l + 1]}
    return wt, gath[0].reshape(D_MODEL, D_MODEL)


def ffn_weights(l, wup_g, wdown_g, small, conv_b):
    pad_c = FF_BLOCK_PAD - FF_BLOCK
    wup = jnp.pad(wup_g, ((0, 0), (0, 0), (0, pad_c))).transpose(1, 0, 2).reshape(D_MODEL, N_DEV * FF_BLOCK_PAD)
    wdown = jnp.pad(wdown_g.reshape(FF_PAIRS, FF_BLOCK, D_MODEL), ((0, 0), (0, pad_c), (0, 0)))
    wdown = wdown.reshape(FF_PAIRS * FF_BLOCK_PAD, D_MODEL)
    cb = jnp.pad(conv_b[l].reshape(N_DEV, FF_BLOCK), ((0, 0), (0, pad_c)))
    cw = jnp.concatenate([small[:, 3 * l:3 * l + 3, :], cb[:, None, :], jnp.zeros((N_DEV, 4, FF_BLOCK_PAD), F32)], axis=1)
    return {"wup": wup, "wdown": wdown, "cw": cw}


def mixer_grad_blocks(g, w_in_grad):
    blocks = [] if w_in_grad is None else [w_in_grad.reshape(N_DEV, 128, D_MODEL)]
    blocks += [g["wout"].reshape(N_DEV, 128, D_MODEL), g["memw"].reshape(N_DEV, 128, 2 * MEM_WIDTH)]
    return [b.astype(BF16) for b in blocks]


def ffn_grad_blocks(g):
    wup = g["wup"][:, :, :FF_BLOCK]
    wdown = g["wdown"].reshape(FF_PAIRS, FF_BLOCK_PAD, D_MODEL)[:, :FF_BLOCK].reshape(N_DEV, FF_ROWS, D_MODEL)
    return [wup.astype(BF16), wdown.astype(BF16)]


def small_grad_blocks(g0, g1):
    taps = jnp.stack([g0["cw"][:, :3, :], g1["cw"][:, :3, :]], axis=1).reshape(N_DEV, DEPTH * 3, FF_BLOCK_PAD)
    small = jnp.zeros((N_DEV, 16, FF_BLOCK_PAD), F32)
    small = lax.dynamic_update_slice(small, taps, (0, 0, 0))
    return lax.dynamic_update_slice(small, g0["pscale"].reshape(N_DEV, 1, 96), (0, 8, 0))


def replicated_grads(g0, g1):
    pw = jnp.stack([g0["pw_full"][k * POOL_GROUP:(k + 1) * POOL_GROUP, k * POOL_GROUP:(k + 1) * POOL_GROUP] for k in range(4)])
    conv_b = jnp.stack([g_["cw"][:, 3, :FF_BLOCK].reshape(N_DEV * FF_BLOCK) for g_ in (g0, g1)])
    ln = [jnp.concatenate([g0[n], g1[n]], axis=0) for n in ("ln1_g", "ln1_b", "ln2_g", "ln2_b")]
    return pack_replicated(pw[None], ln[0], ln[1], ln[2], ln[3], conv_b, g1["fb"][0, :FOX_HEADS])


def kernel(x, mem, a_w_in, a_pool_w, a_pool_scale, a_w_out, b_w_q, b_w_out, kv_w, f_b, mem_w_kv, ln1_g, ln1_b, ln2_g, ln2_b, ffn_w_up, ffn_conv_w, ffn_conv_b, ffn_w_down, loss_target, m_a_w_in, m_a_pool_w, m_a_pool_scale, m_a_w_out, m_b_w_q, m_b_w_out, m_kv_w, m_f_b, m_mem_w_kv, m_ln1_g, m_ln1_b, m_ln2_g, m_ln2_b, m_ffn_w_up, m_ffn_conv_w, m_ffn_conv_b, m_ffn_w_down, v_a_w_in, v_a_pool_w, v_a_pool_scale, v_a_w_out, v_b_w_q, v_b_w_out, v_kv_w, v_f_b, v_mem_w_kv, v_ln1_g, v_ln1_b, v_ln2_g, v_ln2_b, v_ffn_w_up, v_ffn_conv_w, v_ffn_conv_b, v_ffn_w_down):
    nb, s, d = x.shape
    t = nb * s
    x2d, mem_bf, target = x.reshape(t, d), mem.reshape(nb * MEM_LEN, d).astype(BF16), loss_target.reshape(t, d)

    shards0 = layer_shards(0, a_w_in, a_w_out, mem_w_kv, ffn_w_up, ffn_w_down)
    shards1 = layer_shards(1, b_w_q, b_w_out, mem_w_kv, ffn_w_up, ffn_w_down)
    shards1.append(jnp.pad(kv_w, ((0, 0), (0, KV_COLS_PAD - KV_COLS))).astype(BF16))
    gath0 = exchange(shards0[:3] + [pack_small(ffn_conv_w, a_pool_scale)], [False] * 4, "gather_w0_mixer")
    pending = {"ffn0": exchange_start(shards0[3:], [False] * 2, gath0[0], "gather_w0_ffn_start")}
    small = gath0[3]
    wt0, w_in = mixer_weights(0, gath0, ln1_g + pending["ffn0"]["token"], ln1_b, ln2_g, ln2_b)
    pw_bd = _block_diag(a_pool_w[0])
    wt0.update(win_a=w_in, pw_bd=pw_bd.astype(BF16),
               pscale=small[:, 8, :96].reshape(1, TOK_WIDTH) + pending["ffn0"]["token"])

    def ffn0_weights(x1_bf):
        got = exchange_wait(pending["ffn0"], x1_bf, "gather_w0_ffn_wait")
        pending["w1"] = exchange_start(shards1, [False] * 6, got[0], "gather_w1_start")
        w = ffn_weights(0, got[0], got[1], small, ffn_conv_b)
        w["cw"] = w["cw"] + pending["w1"]["token"]
        return w

    x1, x1_bf, sv0 = forward_layer(0, x2d, x2d, mem_bf, wt0, nb, s, ffn_weights=ffn0_weights)
    gath1 = exchange_wait(pending["w1"], x1_bf, "gather_w1_wait")
    wt1, w_q = mixer_weights(1, gath1, ln1_g, ln1_b, ln2_g, ln2_b)
    wt1.update(ffn_weights(1, gath1[3], gath1[4], small, ffn_conv_b))
    kvw = gath1[5].reshape(D_MODEL, KV_COLS_PAD)
    wt1.update(wq=w_q, kvw=kvw,
               fb=jnp.pad(f_b.reshape(1, FOX_HEADS), ((0, 0), (0, LANES - FOX_HEADS))))
    _, _, sv1 = forward_layer(1, x1, x1_bf, mem_bf, wt1, nb, s)

    dx1, dx1_rest, g1 = backward_layer(1, None, sv1, mem_bf, wt1, nb, s, loss_target=target)
    loss = lax.psum(g1["loss_row"][0, 0], ("x", "y", "c"))
    blocks1 = (mixer_grad_blocks(g1, g1["wq"]) + ffn_grad_blocks(g1)
               + [g1["kvw"][:, :KV_COLS].reshape(N_DEV, 128, KV_COLS).astype(BF16)])
    pending["g1"] = exchange_start(blocks1, [True] * 6, dx1, "scatter_g1_start")
    wt0["ln2_g"] = wt0["ln2_g"] + pending["g1"]["token"]

    def after_ffn0(g, dxm):
        pending["gf0"] = exchange_start(ffn_grad_blocks(g), [True] * 2, dxm, "scatter_g0_ffn_start")
        return pending["gf0"]["token"]

    def after_pool0(g, x):
        blocks = mixer_grad_blocks(g, None) + [small_grad_blocks(g, g1), replicated_grads(g, g1)]
        pending["gm0"] = exchange_start(blocks, [True] * 3 + [False], x, "scatter_g0_mixer_start")
        return pending["gm0"]["token"]

    grad_x, _, g0 = backward_layer(0, dx1, sv0, mem_bf, wt0, nb, s, after_ffn=after_ffn0, after_pool=after_pool0,
                                   dy_product=dx1_rest)
    pending["gin"] = exchange_start([g0["win_a"].reshape(N_DEV, 128, D_MODEL).astype(BF16)], [True], grad_x,
                                    "scatter_g0_in_start")
    parts_f0 = exchange_wait(pending["gf0"], jnp.zeros((8, LANES), F32) + pending["gin"]["token"], "scatter_g0_ffn_wait")
    parts1 = exchange_wait(pending["g1"], parts_f0[0], "scatter_g1_wait")

    res = {}

    def upd(nm, parts, w2, m2, v2):
        res[nm] = reduce_adamw(parts, w2, m2, v2, f"adamw_{nm}")

    upd("b_w_q", [parts1[0]], b_w_q, m_b_w_q, v_b_w_q)
    upd("b_w_out", [parts1[1]], b_w_out, m_b_w_out, v_b_w_out)
    upd("kv_w", [parts1[5]], kv_w[None], m_kv_w[None], v_kv_w[None])
    upd("ffn_w_up", [parts_f0[0], parts1[3]], ffn_w_up, m_ffn_w_up, v_ffn_w_up)
    upd("ffn_w_down", [parts_f0[1], parts1[4]], ffn_w_down, m_ffn_w_down, v_ffn_w_down)
    parts_m0 = exchange_wait(pending["gm0"], res["ffn_w_down"][0], "scatter_g0_mixer_wait")
    parts_in = exchange_wait(pending["gin"], parts_m0[0], "scatter_g0_in_wait")
    upd("a_w_in", [parts_in[0]], a_w_in, m_a_w_in, v_a_w_in)
    upd("a_w_out", [parts_m0[0]], a_w_out, m_a_w_out, v_a_w_out)
    upd("mem_w_kv", [parts_m0[1], parts1[2]], mem_w_kv, m_mem_w_kv, v_mem_w_kv)
    upd("small", [parts_m0[2]], pack_small(ffn_conv_w, a_pool_scale)[None], pack_small(m_ffn_conv_w, m_a_pool_scale)[None],
        pack_small(v_ffn_conv_w, v_a_pool_scale)[None])
    upd("replicated", [parts_m0[3]], pack_replicated(a_pool_w, ln1_g, ln1_b, ln2_g, ln2_b, ffn_conv_b, f_b)[None],
        pack_replicated(m_a_pool_w, m_ln1_g, m_ln1_b, m_ln2_g, m_ln2_b, m_ffn_conv_b, m_f_b)[None],
        pack_replicated(v_a_pool_w, v_ln1_g, v_ln1_b, v_ln2_g, v_ln2_b, v_ffn_conv_b, v_f_b)[None])

    res["kv_w"] = [o[0] for o in res["kv_w"]]
    res["ffn_conv_w"] = [o[0, :DEPTH * 3, :FF_BLOCK].reshape(DEPTH, 3, FF_BLOCK) for o in res["small"]]
    res["a_pool_scale"] = [o[0, 8:9, :96] for o in res["small"]]
    rep_names = ["a_pool_w", "ln1_g", "ln1_b", "ln2_g", "ln2_b", "ffn_conv_b", "f_b"]
    for nm in rep_names:
        res[nm] = []
    for o in res["replicated"]:
        for nm, val in zip(rep_names, unpack_replicated(o[0])):
            res[nm].append(val)

    order = ["a_w_in", "a_pool_w", "a_pool_scale", "a_w_out", "b_w_q", "b_w_out", "kv_w", "f_b", "mem_w_kv",
             "ln1_g", "ln1_b", "ln2_g", "ln2_b", "ffn_w_up", "ffn_conv_w", "ffn_conv_b", "ffn_w_down"]
    out = [loss, grad_x.reshape(nb, s, d)]
    for kind in range(4):
        out.extend(res[nm][kind] for nm in order)
    return tuple(out)
```

```python
import jax
import jax.numpy as jnp
from jax import lax
from jax.experimental import pallas as pl
from jax.experimental.pallas import tpu as pltpu

F32 = jnp.float32
BF16 = jnp.bfloat16
SDS = jax.ShapeDtypeStruct

N_DEV = 8
D_MODEL = 1024
TOK_WIDTH = 768
MEM_WIDTH = 256
MEM_LEN = 256
MEM_HEADS = 4
HEAD_DIM = 64
FOX_HEADS = 12
POOL_GROUP = 192
FF_BLOCK = 688
FF_BLOCK_PAD = 768
FF_PAIRS = 4
FF_ROWS = 344
KV_COLS = 1548
KV_COLS_PAD = 1664
LANES = 128
DEPTH = 2
DN_ALPHA = (2.0 * DEPTH) ** 0.25
LN_EPS = 1e-5
QK_SCALE = HEAD_DIM ** -0.5
NEG_BIG = -1e30

ADAM_LR = 0.001
ADAM_B1 = 0.9
ADAM_B2 = 0.999
ADAM_EPS = 1e-08
ADAM_WD = 0.01
ADAM_STEP = 10

VMEM_LIMIT_BYTES = 56 * 1024 * 1024
MM_BLOCK_BYTES = 6 * 1024 * 1024
TM = 512
TS = 256
TF = 256
TC = 256
HALO_POOL = 16
HALO_CONV = 8

NT_DIMS = (((1,), (1,)), ((), ()))
TN_DIMS = (((0,), (0,)), ((), ()))


def _params(sem=None):
    return pltpu.CompilerParams(dimension_semantics=sem, vmem_limit_bytes=VMEM_LIMIT_BYTES)


def _sigmoid(z):
    return 1.0 / (1.0 + jnp.exp(-z))


def _pick_tn(n):
    if n <= 2048:
        return n
    for t in (1024, 768, 512, 256, 128):
        if n % t == 0:
            return t
    return n


def mm_nn(a, b, out_dtype, name, addend=None, add_scale=1.0, also_bf16=False, trans_b=None):
    m, k = a.shape
    n = b.shape[1] if trans_b is None else b.shape[0]
    tm = min(TM, m)
    tn = n
    while k * tn * 2 > MM_BLOCK_BYTES or tm * tn * 4 > MM_BLOCK_BYTES:
        tn //= 2
    chunk = tn if tn <= 2048 else _pick_tn(tn)
    has_add = addend is not None

    def body(*refs):
        a_ref, b_ref = refs[0], refs[1]
        c_ref = refs[2] if has_add else None
        o_ref = refs[3] if has_add else refs[2]
        ob_ref = refs[-1] if also_bf16 else None
        av = a_ref[...].astype(BF16)
        for c in range(tn // chunk):
            cols = slice(c * chunk, (c + 1) * chunk)
            if trans_b is None:
                r = jnp.dot(av, b_ref[:, cols].astype(BF16), preferred_element_type=F32)
            else:
                r = lax.dot_general(av, b_ref[cols, :].astype(BF16), NT_DIMS, preferred_element_type=F32)
            if has_add:
                r = r + add_scale * c_ref[:, cols]
            o_ref[:, cols] = r.astype(out_dtype)
            if also_bf16:
                ob_ref[:, cols] = r.astype(BF16)

    b_spec = (pl.BlockSpec((k, tn), lambda j, i: (0, j)) if trans_b is None
              else pl.BlockSpec((tn, k), lambda j, i: (j, trans_b)))
    in_specs = [pl.BlockSpec((tm, k), lambda j, i: (i, 0)), b_spec]
    ops = [a, b]
    tile = pl.BlockSpec((tm, tn), lambda j, i: (i, j))
    if has_add:
        in_specs.append(tile)
        ops.append(addend)
    out_shape = [SDS((m, n), out_dtype)]
    out_specs = [tile]
    if also_bf16:
        out_shape.append(SDS((m, n), BF16))
        out_specs.append(tile)
    res = pl.pallas_call(
        body, name=name, grid=(n // tn, m // tm), in_specs=in_specs, out_specs=out_specs, out_shape=out_shape,
        compiler_params=_params(("parallel", "parallel")))(*ops)
    return tuple(res) if also_bf16 else res[0]


def mm_tn(a, b, name, blocked=False):
    t, m = a.shape
    _, n = b.shape
    tt = min(4 * TM, t)
    tm = 1024 if m % 1024 == 0 else m
    tn = FF_BLOCK_PAD if blocked else _pick_tn(n)
    nt = t // tt

    def body(a_ref, b_ref, o_ref):
        kk = pl.program_id(2)
        r = lax.dot_general(a_ref[...].astype(BF16), b_ref[...].astype(BF16), TN_DIMS, preferred_element_type=F32)
        if blocked:
            r = r[None]

        @pl.when(kk == 0)
        def _():
            o_ref[...] = r

        @pl.when(kk != 0)
        def _():
            o_ref[...] += r

    if blocked:
        out_shape = SDS((n // tn, m, tn), F32)
        out_spec = pl.BlockSpec((1, tm, tn), lambda i, j, kk: (j, i, 0))
    else:
        out_shape = SDS((m, n), F32)
        out_spec = pl.BlockSpec((tm, tn), lambda i, j, kk: (i, j))
    return pl.pallas_call(
        body, name=name, grid=(m // tm, n // tn, nt),
        in_specs=[pl.BlockSpec((tt, tm), lambda i, j, kk: (kk, i)), pl.BlockSpec((tt, tn), lambda i, j, kk: (kk, j))],
        out_specs=out_spec, out_shape=out_shape,
        compiler_params=_params(("parallel", "parallel", "arbitrary")))(a, b)


def ln_fwd(xprev, a, w, g, b, name):
    t, d = xprev.shape
    k = a.shape[1]
    tm = min(TM, t)

    def body(xp_ref, a_ref, w_ref, g_ref, b_ref, y_ref, yb_ref, xh_ref, rs_ref):
        r = DN_ALPHA * xp_ref[...] + jnp.dot(a_ref[...], w_ref[...], preferred_element_type=F32)
        mu = jnp.mean(r, axis=1, keepdims=True)
        xc = r - mu
        var = jnp.mean(xc * xc, axis=1, keepdims=True)
        rstd = lax.rsqrt(var + LN_EPS)
        xh = xc * rstd
        y = xh * g_ref[...] + b_ref[...]
        y_ref[...] = y
        yb_ref[...] = y.astype(BF16)
        xh_ref[...] = xh
        rs_ref[...] = jnp.broadcast_to(rstd, (tm, LANES))

    row = pl.BlockSpec((tm, d), lambda i: (i, 0))
    vec = pl.BlockSpec((1, d), lambda i: (0, 0))
    return pl.pallas_call(
        body, name=name, grid=(t // tm,),
        in_specs=[row, pl.BlockSpec((tm, k), lambda i: (i, 0)), pl.BlockSpec((k, d), lambda i: (0, 0)), vec, vec],
        out_specs=[row, row, row, pl.BlockSpec((tm, LANES), lambda i: (i, 0))],
        out_shape=[SDS((t, d), F32), SDS((t, d), BF16), SDS((t, d), F32), SDS((t, LANES), F32)],
        compiler_params=_params(("parallel",)))(xprev, a, w, g, b)


def ln_bwd(dy, xhat, rstd, g, name, products=(), dy_scale=1.0):
    t, d = dy.shape
    np_ = len(products)
    tm = min(TM if sum(a.shape[1] for a, _, _ in products) <= 4096 else TS, t)

    def body(*refs):
        prod_refs = refs[:2 * np_]
        dy_ref, xh_ref, rs_ref, g_ref, dr_ref, drb_ref, dg_ref, db_ref = refs[2 * np_:]
        i = pl.program_id(0)
        dyv = dy_ref[...] if dy_scale == 1.0 else dy_scale * dy_ref[...]
        for p in range(np_):
            dyv = dyv + lax.dot_general(prod_refs[2 * p][...], prod_refs[2 * p + 1][...], NT_DIMS,
                                        preferred_element_type=F32)
        xh = xh_ref[...]
        dxh = dyv * g_ref[...]
        m1 = jnp.mean(dxh, axis=1, keepdims=True)
        m2 = jnp.mean(dxh * xh, axis=1, keepdims=True)
        dr = rs_ref[:, 0:1] * (dxh - m1 - xh * m2)
        dr_ref[...] = dr
        drb_ref[...] = dr.astype(BF16)

        @pl.when(i == 0)
        def _():
            dg_ref[...] = jnp.zeros_like(dg_ref)
            db_ref[...] = jnp.zeros_like(db_ref)

        dg_ref[...] += jnp.sum(dyv * xh, axis=0, keepdims=True)
        db_ref[...] += jnp.sum(dyv, axis=0, keepdims=True)

    row = pl.BlockSpec((tm, d), lambda i: (i, 0))
    vec = pl.BlockSpec((1, d), lambda i: (0, 0))
    in_specs = [row, row, pl.BlockSpec((tm, LANES), lambda i: (i, 0)), vec]
    ops = [dy, xhat, rstd, g]
    for a, w, col in reversed(products):
        k = a.shape[1]
        in_specs = [pl.BlockSpec((tm, k), lambda i: (i, 0)), pl.BlockSpec((d, k), lambda i, col=col: (0, col))] + in_specs
        ops = [a, w] + ops
    return pl.pallas_call(
        body, name=name, grid=(t // tm,), in_specs=in_specs, out_specs=[row, row, vec, vec],
        out_shape=[SDS((t, d), F32), SDS((t, d), BF16), SDS((1, d), F32), SDS((1, d), F32)],
        compiler_params=_params(("arbitrary",)))(*ops)


def loss_ln_bwd(xhat, rstd, g, beta, target, name):
    t, d = xhat.shape
    tm = min(TM, t)
    nsteps = t // tm

    def body(xh_ref, rs_ref, g_ref, b_ref, t_ref, dr_ref, drb_ref, dg_ref, db_ref, l_ref, acc):
        i = pl.program_id(0)
        xh = xh_ref[...]
        diff = xh * g_ref[...] + b_ref[...] - t_ref[...]
        dyv = diff * (1.0 / d)
        dxh = dyv * g_ref[...]
        m1 = jnp.mean(dxh, axis=1, keepdims=True)
        m2 = jnp.mean(dxh * xh, axis=1, keepdims=True)
        dr = rs_ref[:, 0:1] * (dxh - m1 - xh * m2)
        dr_ref[...] = dr
        drb_ref[...] = dr.astype(BF16)

        @pl.when(i == 0)
        def _():
            dg_ref[...] = jnp.zeros_like(dg_ref)
            db_ref[...] = jnp.zeros_like(db_ref)
            acc[...] = jnp.zeros_like(acc)

        dg_ref[...] += jnp.sum(dyv * xh, axis=0, keepdims=True)
        db_ref[...] += jnp.sum(dyv, axis=0, keepdims=True)
        acc[...] += jnp.sum(diff * diff, axis=0, keepdims=True)

        @pl.when(i == nsteps - 1)
        def _():
            tot = jnp.sum(acc[...], axis=1, keepdims=True) * (0.5 / d)
            l_ref[...] = jnp.broadcast_to(tot, (1, LANES))

    row = pl.BlockSpec((tm, d), lambda i: (i, 0))
    vec = pl.BlockSpec((1, d), lambda i: (0, 0))
    return pl.pallas_call(
        body, name=name, grid=(nsteps,),
        in_specs=[row, pl.BlockSpec((tm, LANES), lambda i: (i, 0)), vec, vec, row],
        out_specs=[row, row, vec, vec, pl.BlockSpec((1, LANES), lambda i: (0, 0))],
        out_shape=[SDS((t, d), F32), SDS((t, d), BF16), SDS((1, d), F32), SDS((1, d), F32), SDS((1, LANES), F32)],
        scratch_shapes=[pltpu.VMEM((1, d), F32)],
        compiler_params=_params(("arbitrary",)))(xhat, rstd, g, beta, target)


def memattn_fwd(proj, memkv, nb, s, name):
    ts = min(TM, s)
    nq = s // ts

    def body(q_ref, kv_ref, o_ref):
        top = lax.broadcasted_iota(jnp.int32, (PAIR, ts), 0) < HEAD_DIM
        scores = []
        for p in range(MEM_HEADS // 2):
            qp = q_ref[:, p * PAIR:(p + 1) * PAIR].astype(BF16)
            ke, ko = _split_pair(kv_ref[:, p * PAIR:(p + 1) * PAIR], QK_SCALE)
            scores.append([lax.dot_general(km, qp, NT_DIMS, preferred_element_type=F32) for km in (ke, ko)])
        for p in range(MEM_HEADS // 2):
            vt = kv_ref[:, MEM_WIDTH + p * PAIR:MEM_WIDTH + (p + 1) * PAIR].astype(F32).T.astype(BF16)
            outs = []
            for sc in scores[p]:
                e = jnp.exp(sc - jnp.max(sc, axis=0, keepdims=True))
                pr = e / jnp.sum(e, axis=0, keepdims=True)
                outs.append(jnp.dot(vt, pr.astype(BF16), preferred_element_type=F32))
            o_ref[:, p * PAIR:(p + 1) * PAIR] = jnp.where(top, outs[0], outs[1]).T.astype(BF16)

    return pl.pallas_call(
        body, name=name, grid=(nb, nq),
        in_specs=[pl.BlockSpec((ts, MEM_WIDTH), lambda b, i: (b * nq + i, 3)),
                  pl.BlockSpec((MEM_LEN, 2 * MEM_WIDTH), lambda b, i: (b, 0))],
        out_specs=pl.BlockSpec((ts, MEM_WIDTH), lambda b, i: (b * nq + i, 0)),
        out_shape=SDS((nb * s, MEM_WIDTH), BF16),
        compiler_params=_params(("parallel", "parallel")))(proj, memkv)


def memattn_bwd(proj, memkv, dcat, nb, s, name):
    ts = min(TM, s)
    nq = s // ts

    def body(q_ref, kv_ref, do_ref, dq_ref, dkv_ref):
        i = pl.program_id(1)

        @pl.when(i == 0)
        def _():
            dkv_ref[...] = jnp.zeros_like(dkv_ref)

        lo = _half_masks(MEM_LEN)
        top = lax.broadcasted_iota(jnp.int32, (PAIR, ts), 0) < HEAD_DIM
        n_pairs = MEM_HEADS // 2
        qs, dos, kps, products = [], [], [], []
        for p in range(n_pairs):
            qp = q_ref[:, p * PAIR:(p + 1) * PAIR].astype(BF16)
            dop = do_ref[:, p * PAIR:(p + 1) * PAIR].astype(BF16)
            kp = kv_ref[:, p * PAIR:(p + 1) * PAIR] * QK_SCALE
            kms = _split_pair(kp)
            vms = _split_pair(kv_ref[:, MEM_WIDTH + p * PAIR:MEM_WIDTH + (p + 1) * PAIR])
            products.append([(lax.dot_general(km, qp, NT_DIMS, preferred_element_type=F32),
                              lax.dot_general(vm, dop, NT_DIMS, preferred_element_type=F32)) for km, vm in zip(kms, vms)])
            qs.append(qp)
            dos.append(dop)
            kps.append(kp)
        for p in range(n_pairs):
            kt = kps[p].astype(F32).T.astype(BF16)
            dks, dvs, dqs = [], [], []
            for sc, dp in products[p]:
                e = jnp.exp(sc - jnp.max(sc, axis=0, keepdims=True))
                pr = e / jnp.sum(e, axis=0, keepdims=True)
                dl = jnp.sum(pr * dp, axis=0, keepdims=True)
                ds = (pr * (dp - dl)).astype(BF16)
                dvs.append(jnp.dot(pr.astype(BF16), dos[p], preferred_element_type=F32))
                dks.append(jnp.dot(ds, qs[p], preferred_element_type=F32))
                dqs.append(jnp.dot(kt, ds, preferred_element_type=F32))
            dq_ref[:, p * PAIR:(p + 1) * PAIR] = jnp.where(top, dqs[0], dqs[1]).T.astype(BF16)
            dkv_ref[:, p * PAIR:(p + 1) * PAIR] += jnp.where(lo, dks[0], dks[1]) * QK_SCALE
            dkv_ref[:, MEM_WIDTH + p * PAIR:MEM_WIDTH + (p + 1) * PAIR] += jnp.where(lo, dvs[0], dvs[1])

    return pl.pallas_call(
        body, name=name, grid=(nb, nq),
        in_specs=[pl.BlockSpec((ts, MEM_WIDTH), lambda b, i: (b * nq + i, 3)),
                  pl.BlockSpec((MEM_LEN, 2 * MEM_WIDTH), lambda b, i: (b, 0)),
                  pl.BlockSpec((ts, MEM_WIDTH), lambda b, i: (b * nq + i, 3))],
        out_specs=[pl.BlockSpec((ts, MEM_WIDTH), lambda b, i: (b * nq + i, 0)),
                   pl.BlockSpec((MEM_LEN, 2 * MEM_WIDTH), lambda b, i: (b, 0))],
        out_shape=[SDS((nb * s, MEM_WIDTH), BF16), SDS((nb * MEM_LEN, 2 * MEM_WIDTH), F32)],
        compiler_params=_params(("parallel", "arbitrary")))(proj, memkv, dcat)


def _pool_select(shape, s2, s4, s8, s16):
    lane = lax.broadcasted_iota(jnp.int32, shape, 1)
    return jnp.where(lane < POOL_GROUP, s2, jnp.where(lane < 2 * POOL_GROUP, s4, jnp.where(lane < 3 * POOL_GROUP, s8, s16)))


def _pool_count(shape, first_pos):
    pos = first_pos + lax.broadcasted_iota(jnp.int32, shape, 0)
    win = _pool_select(shape, 2, 4, 8, 16)
    return jnp.minimum(pos + 1, win).astype(F32)


def pool_fwd(proj, pw_bd, pscale, nb, s):
    ts = min(TS, s)
    nq = s // ts
    w = TOK_WIDTH

    def body(c_ref, h_ref, w_ref, sc_ref, pooled_ref, tok_ref):
        i = pl.program_id(0) % nq
        cur = c_ref[...]
        halo = jnp.where(i == 0, 0.0, h_ref[...])
        xe = jnp.concatenate([halo, cur], axis=0)
        s2 = xe + pltpu.roll(xe, 1, axis=0)
        s4 = s2 + pltpu.roll(s2, 2, axis=0)
        s8 = s4 + pltpu.roll(s4, 4, axis=0)
        s16 = s8 + pltpu.roll(s8, 8, axis=0)
        hp = HALO_POOL
        ws = _pool_select((ts, w), s2[hp:], s4[hp:], s8[hp:], s16[hp:])
        pooled = (ws / _pool_count((ts, w), i * ts) - cur).astype(BF16)
        pooled_ref[...] = pooled
        mixed = jnp.dot(pooled, w_ref[...], preferred_element_type=F32)
        tok_ref[...] = (mixed * sc_ref[...]).astype(BF16)

    row = pl.BlockSpec((ts, w), lambda r: (r, 0))
    return pl.pallas_call(
        body, name="pool_fwd", grid=(nb * nq,),
        in_specs=[row, pl.BlockSpec((HALO_POOL, w), lambda r: (jnp.maximum(r * (ts // HALO_POOL) - 1, 0), 0)),
                  pl.BlockSpec((w, w), lambda r: (0, 0)), pl.BlockSpec((1, w), lambda r: (0, 0))],
        out_specs=[row, row], out_shape=[SDS((nb * s, w), BF16), SDS((nb * s, w), BF16)],
        compiler_params=_params(("parallel",)))(proj, proj, pw_bd, pscale)


def pool_bwd_mix(dcat, pooled, pw_bd, pscale, nb, s):
    ts = min(TS, s)
    w = TOK_WIDTH

    def body(dt_ref, p_ref, w_ref, sc_ref, dm_ref, dp_ref, ds_ref):
        r = pl.program_id(0)
        dtok = dt_ref[...]
        mixed = jnp.dot(p_ref[...], w_ref[...], preferred_element_type=F32)

        @pl.when(r == 0)
        def _():
            ds_ref[...] = jnp.zeros_like(ds_ref)

        ds_ref[...] += jnp.sum(dtok * mixed, axis=0, keepdims=True)
        dmx = (dtok * sc_ref[...]).astype(BF16)
        dm_ref[...] = dmx
        dp_ref[...] = lax.dot_general(dmx, w_ref[...], NT_DIMS, preferred_element_type=F32)

    row = pl.BlockSpec((ts, w), lambda r: (r, 0))
    mat = pl.BlockSpec((w, w), lambda r: (0, 0))
    vec = pl.BlockSpec((1, w), lambda r: (0, 0))
    return pl.pallas_call(
        body, name="pool_bwd_mix", grid=(nb * s // ts,), in_specs=[row, row, mat, vec],
        out_specs=[row, row, vec], out_shape=[SDS((nb * s, w), BF16), SDS((nb * s, w), F32), SDS((1, w), F32)],
        compiler_params=_params(("arbitrary",)))(dcat, pooled, pw_bd, pscale)


def pool_bwd_window(dpooled, nb, s):
    ts = min(TS, s)
    nq = s // ts
    w = TOK_WIDTH
    n_ext = ts + HALO_POOL
    n_halo_blocks = nb * s // HALO_POOL

    def body(c_ref, n_ref, du_ref):
        i = pl.program_id(0) % nq
        cur = c_ref[...]
        nxt = jnp.where(i == nq - 1, 0.0, n_ref[...])
        ze = jnp.concatenate([cur, nxt], axis=0) / _pool_count((n_ext, w), i * ts)
        s2 = ze + pltpu.roll(ze, n_ext - 1, axis=0)
        s4 = s2 + pltpu.roll(s2, n_ext - 2, axis=0)
        s8 = s4 + pltpu.roll(s4, n_ext - 4, axis=0)
        s16 = s8 + pltpu.roll(s8, n_ext - 8, axis=0)
        ws = _pool_select((ts, w), s2[:ts], s4[:ts], s8[:ts], s16[:ts])
        du_ref[...] = (ws - cur).astype(BF16)

    row = pl.BlockSpec((ts, w), lambda r: (r, 0))
    return pl.pallas_call(
        body, name="pool_bwd_window", grid=(nb * nq,),
        in_specs=[row, pl.BlockSpec((HALO_POOL, w),
                                    lambda r: (jnp.minimum((r + 1) * (ts // HALO_POOL), n_halo_blocks - 1), 0))],
        out_specs=row, out_shape=SDS((nb * s, w), BF16),
        compiler_params=_params(("parallel",)))(dpooled, dpooled)


def _conv_rows(xe, w_ref):
    return (w_ref[0, 2:3, :] * xe + w_ref[0, 1:2, :] * pltpu.roll(xe, 1, axis=0)
            + w_ref[0, 0:1, :] * pltpu.roll(xe, 2, axis=0) + w_ref[0, 3:4, :])


def ffn_up_gate(x_bf, wup, cw, nb, s, name):
    tm = min(2 * TM, s)
    nq = s // tm
    w = FF_BLOCK_PAD
    hr = 2 * HALO_CONV
    k = x_bf.shape[1]

    def body(xc_ref, xh_ref, wu_ref, wg_ref, cu_ref, cg_ref, act_ref, a_ref, b_ref, hu_ref, hg_ref):
        first = (pl.program_id(1) % nq) == 0
        xc = xc_ref[...]
        xh = xh_ref[...]

        def products(w_ref):
            return (jnp.dot(xc, w_ref[...], preferred_element_type=F32), jnp.dot(xh, w_ref[...], preferred_element_type=F32))

        def conv(hcur, hprev, c_ref, h_out):
            h_out[...] = hcur.astype(BF16)
            xe = jnp.concatenate([jnp.where(first, 0.0, hprev), hcur], axis=0)
            return _conv_rows(xe, c_ref)[hr:]

        pu, pg = products(wu_ref), products(wg_ref)
        cu = conv(*pu, cu_ref, hu_ref)
        cg = conv(*pg, cg_ref, hg_ref)
        sg = _sigmoid(cg)
        a = cg * sg
        act_ref[...] = (a * cu).astype(BF16)
        a_ref[...] = a.astype(BF16)
        b_ref[...] = (cu * (sg * (1.0 + cg * (1.0 - sg)))).astype(BF16)

    def wblock(off):
        return pl.BlockSpec((k, w), lambda j, r: (0, j + off))

    def cblock(off):
        return pl.BlockSpec((1, 8, w), lambda j, r: (j + off, 0, 0))

    tile = pl.BlockSpec((tm, w), lambda j, r: (r, j))
    out = SDS((nb * s, FF_PAIRS * w), BF16)
    return pl.pallas_call(
        body, name=name, grid=(FF_PAIRS, nb * nq),
        in_specs=[pl.BlockSpec((tm, k), lambda j, r: (r, 0)),
                  pl.BlockSpec((hr, k), lambda j, r: (jnp.maximum(r * (tm // hr) - 1, 0), 0)),
                  wblock(0), wblock(FF_PAIRS), cblock(0), cblock(FF_PAIRS)],
        out_specs=[tile] * 5, out_shape=[out] * 5,
        compiler_params=_params(("parallel", "parallel")))(x_bf, x_bf, wup, wup, cw, cw)


def gate_conv_bwd(dact, a, b, hu, hg, cw, nb, s, name):
    ts = min(TS, s)
    nq = s // ts
    w = FF_BLOCK_PAD
    hc = HALO_CONV
    hb = 2 * hc
    n_ext = ts + hc

    def body(dc_ref, dn_ref, ac_ref, an_ref, bc_ref, bn_ref, hu_ref, hg_ref, wu_ref, wg_ref,
             dhu_ref, dhg_ref, dwu_ref, dwg_ref):
        r = pl.program_id(1)
        last = (r % nq) == nq - 1

        def ext(c_ref, n_ref, mask_next=False):
            nxt = n_ref[...].astype(F32)[:hc]
            if mask_next:
                nxt = jnp.where(last, 0.0, nxt)
            return jnp.concatenate([c_ref[...].astype(F32), nxt], axis=0)

        da = ext(dc_ref, dn_ref, mask_next=True)

        def branch(dcv, w_ref, h_ref, dh_ref, dw_ref):
            d0 = dcv[:ts]
            d1 = pltpu.roll(dcv, n_ext - 1, axis=0)[:ts]
            d2 = pltpu.roll(dcv, n_ext - 2, axis=0)[:ts]
            dh_ref[...] = (w_ref[0, 2:3, :] * d0 + w_ref[0, 1:2, :] * d1 + w_ref[0, 0:1, :] * d2).astype(BF16)
            hv = h_ref[...].astype(F32)
            rows = [jnp.sum(d2 * hv, axis=0, keepdims=True), jnp.sum(d1 * hv, axis=0, keepdims=True),
                    jnp.sum(d0 * hv, axis=0, keepdims=True), jnp.sum(d0, axis=0, keepdims=True)]
            sub = lax.broadcasted_iota(jnp.int32, (8, w), 0)
            upd = jnp.zeros((8, w), F32)
            for kk, rv in enumerate(rows):
                upd = jnp.where(sub == kk, rv, upd)

            @pl.when(r == 0)
            def _():
                dw_ref[...] = jnp.zeros_like(dw_ref)

            dw_ref[...] += upd[None]

        branch(da * ext(ac_ref, an_ref), wu_ref, hu_ref, dhu_ref, dwu_ref)
        branch(da * ext(bc_ref, bn_ref), wg_ref, hg_ref, dhg_ref, dwg_ref)

    cur = pl.BlockSpec((ts, w), lambda j, r: (r, j))
    nxt = pl.BlockSpec((hb, w), lambda j, r: (jnp.minimum((r + 1) * (ts // hb), nb * s // hb - 1), j))

    def wspec(off):
        return pl.BlockSpec((1, 8, w), lambda j, r: (j + off, 0, 0))

    p = FF_PAIRS
    dw_spec = pl.BlockSpec((1, 8, w), lambda j, r: (j, 0, 0))
    return pl.pallas_call(
        body, name=name, grid=(p, nb * nq),
        in_specs=[cur, nxt, cur, nxt, cur, nxt, cur, cur, wspec(0), wspec(p)],
        out_specs=[cur, cur, dw_spec, dw_spec],
        out_shape=[SDS((nb * s, p * w), BF16), SDS((nb * s, p * w), BF16), SDS((p, 8, w), F32), SDS((p, 8, w), F32)],
        compiler_params=_params(("parallel", "arbitrary")))(dact, dact, a, a, b, b, hu, hg, cw, cw)


def _tri(n, upper):
    r = lax.broadcasted_iota(jnp.int32, (n, n), 0)
    c = lax.broadcasted_iota(jnp.int32, (n, n), 1)
    return ((r <= c) if upper else (r >= c)).astype(F32)


def fgate_fwd(fl, fb, nb, s):
    tc = min(TC, s)
    nq = s // tc

    def body(fl_ref, fb_ref, f_ref, carry):
        @pl.when(pl.program_id(1) == 0)
        def _():
            carry[...] = jnp.zeros_like(carry)

        z = fl_ref[...] + fb_ref[...]
        logf = jnp.minimum(z, 0.0) - jnp.log(1.0 + jnp.exp(-jnp.abs(z)))
        f_ref[...] = jnp.dot(_tri(tc, False), logf, preferred_element_type=F32,
                             precision=lax.Precision.HIGHEST) + carry[...]
        carry[...] += jnp.sum(logf, axis=0, keepdims=True)

    row = pl.BlockSpec((tc, LANES), lambda b, i: (b * nq + i, 0))
    return pl.pallas_call(
        body, name="fgate_fwd", grid=(nb, nq), in_specs=[row, pl.BlockSpec((1, LANES), lambda b, i: (0, 0))],
        out_specs=row, out_shape=SDS((nb * s, LANES), F32), scratch_shapes=[pltpu.VMEM((1, LANES), F32)],
        compiler_params=_params(("arbitrary", "arbitrary")))(fl, fb)


def fgate_bwd(d_cum_q, d_cum_k, fl, fb, nb, s):
    tc = min(TC, s)
    nq = s // tc

    def body(dfq_ref, dfk_ref, fl_ref, fb_ref, dfl_ref, dfb_ref, carry):
        b = pl.program_id(0)
        i = pl.program_id(1)

        @pl.when(i == 0)
        def _():
            carry[...] = jnp.zeros_like(carry)

        @pl.when(jnp.logical_and(b == 0, i == 0))
        def _():
            dfb_ref[...] = jnp.zeros_like(dfb_ref)

        dfv = dfq_ref[...] + dfk_ref[...]
        dlog = jnp.dot(_tri(tc, True), dfv, preferred_element_type=F32,
                       precision=lax.Precision.HIGHEST) + carry[...]
        carry[...] += jnp.sum(dfv, axis=0, keepdims=True)
        z = fl_ref[...] + fb_ref[...]
        dfl = dlog / (1.0 + jnp.exp(z))
        dfl_ref[...] = dfl
        dfb_ref[...] += jnp.sum(dfl, axis=0, keepdims=True)

    row = pl.BlockSpec((tc, LANES), lambda b, i: (b * nq + nq - 1 - i, 0))
    vec = pl.BlockSpec((1, LANES), lambda b, i: (0, 0))
    return pl.pallas_call(
        body, name="fgate_bwd", grid=(nb, nq), in_specs=[row, row, row, vec], out_specs=[row, vec],
        out_shape=[SDS((nb * s, LANES), F32), SDS((1, LANES), F32)], scratch_shapes=[pltpu.VMEM((1, LANES), F32)],
        compiler_params=_params(("arbitrary", "arbitrary")))(d_cum_q, d_cum_k, fl, fb)


PAIR = 2 * HEAD_DIM
N_PAIRS = FOX_HEADS // 2


def _lane_put(shape, h, col):
    lane = lax.broadcasted_iota(jnp.int32, shape, 1)
    return jnp.where(lane == h, col, 0.0)


def _half_masks(rows):
    lane = lax.broadcasted_iota(jnp.int32, (rows, PAIR), 1)
    return lane < HEAD_DIM


def _split_pair(x, scale=None):
    if scale is not None:
        x = x * scale
    lo = _half_masks(x.shape[0])
    zero = jnp.zeros_like(x)
    return jnp.where(lo, x, zero), jnp.where(lo, zero, x)


def _to_tile_rows(a, nb, s, tf):
    return a.reshape(nb * s // tf, tf, LANES)[:, :, :16].transpose(0, 2, 1)


def _from_tile_rows(a):
    tiles, _, tf = a.shape
    return jnp.pad(a.transpose(0, 2, 1), ((0, 0), (0, 0), (0, LANES - 16))).reshape(tiles * tf, LANES)


BIAS_TERMS = 3
LOOKAHEAD = 4
FOLLOW_FWD = 1
LOOKAHEAD_BWD = 2
FOLLOW_BWD = 1


def _bias_lane(h):
    return HEAD_DIM if h % 2 == 0 else 0


def _placement():
    rows = jnp.arange(LANES)[:, None]
    cols = jnp.arange(FOX_HEADS * PAIR)[None, :]
    head, lane = cols // PAIR, cols % PAIR
    first = jnp.where(head % 2 == 0, HEAD_DIM, 0)
    term = lane - first
    hit = (term >= 0) & (term < BIAS_TERMS) & (rows == 16 * term + head)
    return hit.astype(BF16)


def fox_prep(kv, fneg, nb, s):
    tf = min(TF, s)
    w = TOK_WIDTH

    def body(k_ref, v_ref, f_ref, pl_ref, ka_ref, vt_ref):
        lane = lax.broadcasted_iota(jnp.int32, (tf, LANES), 1)
        lo = lane < HEAD_DIM
        f = jnp.where(lane < FOX_HEADS, f_ref[...], 0.0)
        hi = f.astype(BF16).astype(F32)
        mid = (f - hi).astype(BF16).astype(F32)
        low = (f - hi - mid).astype(BF16).astype(F32)
        terms = (hi + pltpu.roll(mid, 16, axis=1) + pltpu.roll(low, 32, axis=1)).astype(BF16)
        placed = jnp.dot(terms, pl_ref[...], preferred_element_type=F32).astype(BF16)
        one = jnp.ones((tf, LANES), BF16)
        zero = jnp.zeros((tf, LANES), BF16)
        for p in range(N_PAIRS):
            kp = k_ref[:, p * PAIR:(p + 1) * PAIR] * QK_SCALE
            vp = v_ref[:, p * PAIR:(p + 1) * PAIR]
            he, ho = 2 * p, 2 * p + 1
            ka_ref[:, he * PAIR:(he + 1) * PAIR] = jnp.where(lo, kp, placed[:, he * PAIR:(he + 1) * PAIR])
            ka_ref[:, ho * PAIR:(ho + 1) * PAIR] = jnp.where(lo, placed[:, ho * PAIR:(ho + 1) * PAIR], kp)
            ve = jnp.where(lo, vp, jnp.where(lane == HEAD_DIM, one, zero))
            vo = jnp.where(lo, jnp.where(lane == 0, one, zero), vp)
            vt_ref[0, he * PAIR:(he + 1) * PAIR, :] = ve.astype(F32).T.astype(BF16)
            vt_ref[0, ho * PAIR:(ho + 1) * PAIR, :] = vo.astype(F32).T.astype(BF16)

    return pl.pallas_call(
        body, name="fox_prep", grid=(nb * s // tf,),
        in_specs=[pl.BlockSpec((tf, w), lambda r: (r, 0)), pl.BlockSpec((tf, w), lambda r: (r, 1)),
                  pl.BlockSpec((tf, LANES), lambda r: (r, 0)), pl.BlockSpec((LANES, FOX_HEADS * PAIR), lambda r: (0, 0))],
        out_specs=[pl.BlockSpec((tf, FOX_HEADS * PAIR), lambda r: (r, 0)),
                   pl.BlockSpec((1, FOX_HEADS * PAIR, tf), lambda r: (r, 0, 0))],
        out_shape=[SDS((nb * s, FOX_HEADS * PAIR), BF16), SDS((nb * s // tf, FOX_HEADS * PAIR, tf), BF16)],
        compiler_params=_params(("parallel",)))(kv, kv, fneg, _placement())


def fox_fwd_t(pq, kaug, vaug_t, nb, s):
    tf = min(TF, s)
    n = s // tf
    w = TOK_WIDTH
    wa = FOX_HEADS * PAIR

    def body(q_ref, k_hbm, vt_hbm, ob_ref, of_ref, lse_ref, k_vm, vt_vm, qx_scr, m_scr, acc_scr, sems):
        b = pl.program_id(0)
        i = pl.program_id(1)

        @pl.when(i == 0)
        def _():
            ck = pltpu.make_async_copy(k_hbm.at[pl.ds(pl.multiple_of(b * s, tf), s)], k_vm, sems.at[0])
            cv = pltpu.make_async_copy(vt_hbm.at[pl.ds(b * n, n)], vt_vm, sems.at[1])
            ck.start()
            cv.start()
            ck.wait()
            cv.wait()

        lane = lax.broadcasted_iota(jnp.int32, (tf, PAIR), 1)
        one = jnp.ones((tf, PAIR), BF16)
        zero = jnp.zeros((tf, PAIR), BF16)
        for p in range(N_PAIRS):
            qp = q_ref[:, p * PAIR:(p + 1) * PAIR]
            be, bo = _bias_lane(2 * p), _bias_lane(2 * p + 1)
            ones_e = jnp.where((lane >= be) & (lane < be + BIAS_TERMS), one, zero)
            ones_o = jnp.where((lane >= bo) & (lane < bo + BIAS_TERMS), one, zero)
            qx_scr[2 * p] = jnp.where(lane < HEAD_DIM, qp, ones_e)
            qx_scr[2 * p + 1] = jnp.where(lane < HEAD_DIM, ones_o, qp)
        m_scr[...] = jnp.full(m_scr.shape, NEG_BIG, F32)
        acc_scr[...] = jnp.zeros_like(acc_scr)

        def tile(j, masked):
            ks = pl.multiple_of(j * tf, tf)
            if masked:
                keep = lax.broadcasted_iota(jnp.int32, (tf, tf), 1) >= lax.broadcasted_iota(jnp.int32, (tf, tf), 0)
            def scores(h):
                kx = k_vm[pl.ds(ks, tf), h * PAIR:(h + 1) * PAIR]
                return lax.dot_general(kx, qx_scr[h], NT_DIMS, preferred_element_type=F32)

            def values(h, pr, a):
                pv = jnp.dot(vt_vm[j, h * PAIR:(h + 1) * PAIR, :], pr, preferred_element_type=F32)
                acc_scr[h] = a * acc_scr[h] + pv

            ahead = [scores(h) for h in range(LOOKAHEAD)]
            behind = []
            for h in range(FOX_HEADS):
                sc = ahead.pop(0)
                if h + LOOKAHEAD < FOX_HEADS:
                    ahead.append(scores(h + LOOKAHEAD))
                if masked:
                    sc = jnp.where(keep, sc, NEG_BIG)
                m_prev = m_scr[h]
                m_new = jnp.maximum(m_prev, jnp.max(sc, axis=0, keepdims=True))
                m_scr[h] = m_new
                behind.append((h, jnp.exp(sc - m_new).astype(BF16), jnp.exp(m_prev - m_new)))
                if len(behind) > FOLLOW_FWD:
                    values(*behind.pop(0))
            for item in behind:
                values(*item)

        def step(j, carry):
            tile(j, False)
            return carry

        lax.fori_loop(0, i, step, 0)
        tile(i, True)

        top = lax.broadcasted_iota(jnp.int32, (PAIR, tf), 0) < HEAD_DIM
        sub = lax.broadcasted_iota(jnp.int32, (16, tf), 0)
        lse = jnp.zeros((16, tf), F32)
        for p in range(N_PAIRS):
            he, ho = 2 * p, 2 * p + 1
            le = acc_scr[he, HEAD_DIM:HEAD_DIM + 1, :]
            lod = acc_scr[ho, 0:1, :]
            o = jnp.where(top, acc_scr[he] / le, acc_scr[ho] / lod).T
            ob_ref[:, p * PAIR:(p + 1) * PAIR] = o.astype(BF16)
            of_ref[:, p * PAIR:(p + 1) * PAIR] = o
            lse = jnp.where(sub == he, m_scr[he] + jnp.log(le), lse)
            lse = jnp.where(sub == ho, m_scr[ho] + jnp.log(lod), lse)
        lse_ref[0] = lse

    qrow = lambda b, i: (b * n + i, 0)
    return pl.pallas_call(
        body, name="fox_fwd", grid=(nb, n),
        in_specs=[pl.BlockSpec((tf, w), qrow), ANY_SPEC, ANY_SPEC],
        out_specs=[pl.BlockSpec((tf, w), qrow), pl.BlockSpec((tf, w), qrow),
                   pl.BlockSpec((1, 16, tf), lambda b, i: (b * n + i, 0, 0))],
        out_shape=[SDS((nb * s, w), BF16), SDS((nb * s, w), F32), SDS((nb * n, 16, tf), F32)],
        scratch_shapes=[pltpu.VMEM((s, wa), BF16), pltpu.VMEM((n, wa, tf), BF16),
                        pltpu.VMEM((FOX_HEADS, tf, PAIR), BF16), pltpu.VMEM((FOX_HEADS, 1, tf), F32),
                        pltpu.VMEM((FOX_HEADS, PAIR, tf), F32), pltpu.SemaphoreType.DMA((2,))],
        compiler_params=_params(("arbitrary", "arbitrary")))(pq, kaug, vaug_t)


def fox_delta(dcat, o, nb, s):
    tf = min(TM, s)
    w = TOK_WIDTH

    def body(do_ref, o_ref, dl_ref):
        out = jnp.zeros((tf, LANES), F32)
        for h in range(FOX_HEADS):
            lo, hi = h * HEAD_DIM, (h + 1) * HEAD_DIM
            out = out + _lane_put((tf, LANES), h, jnp.sum(do_ref[:, lo:hi] * o_ref[:, lo:hi], axis=1, keepdims=True))
        dl_ref[...] = out

    row = pl.BlockSpec((tf, w), lambda r: (r, 0))
    return pl.pallas_call(
        body, name="fox_delta", grid=(nb * s // tf,), in_specs=[row, row],
        out_specs=pl.BlockSpec((tf, LANES), lambda r: (r, 0)), out_shape=SDS((nb * s, LANES), F32),
        compiler_params=_params(("parallel",)))(dcat, o)


def fox_bwd(pq, kv, fneg, dcat_bf, lse_rows, delta_rows, nb, s):
    tf = min(TF, s)
    n = s // tf
    w = TOK_WIDTH

    def body(q_hbm, k_ref, v_ref, f_ref, do_hbm, lse_ref, dl_ref, dq_ref, dk_ref, dv_ref, dfk_ref, dfq_ref,
             q_vm, do_vm, km_scr, vm_scr, kt_scr, fk_scr, dk_scr, dv_scr, rs_scr, dq_scr, fq_scr, sems):
        b = pl.program_id(0)
        j = pl.program_id(1)

        @pl.when(j == 0)
        def _():
            rows = pl.ds(pl.multiple_of(b * s, tf), s)
            cq = pltpu.make_async_copy(q_hbm.at[rows, pl.ds(0, w)], q_vm, sems.at[0])
            cd = pltpu.make_async_copy(do_hbm.at[rows, pl.ds(0, w)], do_vm, sems.at[1])
            cq.start()
            cd.start()
            dq_scr[...] = jnp.zeros_like(dq_scr)
            fq_scr[...] = jnp.zeros_like(fq_scr)
            cq.wait()
            cd.wait()

        for p in range(N_PAIRS):
            kp = k_ref[:, p * PAIR:(p + 1) * PAIR] * QK_SCALE
            ke, ko = _split_pair(kp)
            km_scr[2 * p] = ke
            km_scr[2 * p + 1] = ko
            kt_scr[p] = kp.astype(F32).T.astype(BF16)
            ve, vo = _split_pair(v_ref[:, p * PAIR:(p + 1) * PAIR])
            vm_scr[2 * p] = ve
            vm_scr[2 * p + 1] = vo
        for h in range(FOX_HEADS):
            fk_scr[h] = jnp.broadcast_to(f_ref[:, h:h + 1], (tf, tf))
        dk_scr[...] = jnp.zeros_like(dk_scr)
        dv_scr[...] = jnp.zeros_like(dv_scr)
        rs_scr[...] = jnp.zeros_like(rs_scr)

        def tile(i, masked):
            qs = pl.multiple_of(i * tf, tf)
            if masked:
                keep = lax.broadcasted_iota(jnp.int32, (tf, tf), 1) >= lax.broadcasted_iota(jnp.int32, (tf, tf), 0)
            def products(h):
                qp = q_vm[pl.ds(qs, tf), (h // 2) * PAIR:(h // 2 + 1) * PAIR]
                dop = do_vm[pl.ds(qs, tf), (h // 2) * PAIR:(h // 2 + 1) * PAIR]
                return (lax.dot_general(km_scr[h], qp, NT_DIMS, preferred_element_type=F32),
                        lax.dot_general(vm_scr[h], dop, NT_DIMS, preferred_element_type=F32))

            def dependents(h, prb, dsb):
                p = h // 2
                half = slice((h % 2) * HEAD_DIM, (h % 2 + 1) * HEAD_DIM)
                qp = q_vm[pl.ds(qs, tf), p * PAIR:(p + 1) * PAIR]
                dop = do_vm[pl.ds(qs, tf), p * PAIR:(p + 1) * PAIR]
                dv_scr[h] += jnp.dot(prb, dop, preferred_element_type=F32)
                dk_scr[h] += jnp.dot(dsb, qp, preferred_element_type=F32)
                dqt = jnp.dot(kt_scr[p], dsb, preferred_element_type=F32)
                dq_scr[i, p, half, :] += dqt[(h % 2) * HEAD_DIM:(h % 2 + 1) * HEAD_DIM]

            ahead = [products(h) for h in range(LOOKAHEAD_BWD)]
            behind = []
            for h in range(FOX_HEADS):
                sc, dp = ahead.pop(0)
                if h + LOOKAHEAD_BWD < FOX_HEADS:
                    ahead.append(products(h + LOOKAHEAD_BWD))
                sc = sc + fk_scr[h] - lse_ref[i, h:h + 1, :]
                if masked:
                    sc = jnp.where(keep, sc, NEG_BIG)
                pr = jnp.exp(sc)
                ds = pr * (dp - dl_ref[i, h:h + 1, :])
                part = ds[:, :LANES]
                for c in range(1, tf // LANES):
                    part = part + ds[:, c * LANES:(c + 1) * LANES]
                rs_scr[h] += part
                fq_scr[i, h:h + 1, :] += jnp.sum(ds, axis=0, keepdims=True)
                behind.append((h, pr.astype(BF16), ds.astype(BF16)))
                if len(behind) > FOLLOW_BWD:
                    dependents(*behind.pop(0))
            for item in behind:
                dependents(*item)

        def step(i, carry):
            tile(i, False)
            return carry

        tile(j, True)
        for p in range(N_PAIRS):
            dq_ref[:, p * PAIR:(p + 1) * PAIR] = dq_scr[j, p].T.astype(BF16)
        dfq_ref[0] = fq_scr[j]
        lax.fori_loop(j + 1, n, step, 0)

        lo = _half_masks(tf)
        dfk = jnp.zeros((tf, LANES), F32)
        for p in range(N_PAIRS):
            dk = jnp.where(lo, dk_scr[2 * p], dk_scr[2 * p + 1]) * QK_SCALE
            dk_ref[:, p * PAIR:(p + 1) * PAIR] = dk.astype(BF16)
            dv_ref[:, p * PAIR:(p + 1) * PAIR] = jnp.where(lo, dv_scr[2 * p], dv_scr[2 * p + 1]).astype(BF16)
            for h in (2 * p, 2 * p + 1):
                dfk = dfk - _lane_put((tf, LANES), h, jnp.sum(rs_scr[h], axis=1, keepdims=True))
        dfk_ref[...] = dfk

    krow = lambda b, j: (b * n + j, 0)
    rows = pl.BlockSpec((n, 16, tf), lambda b, j: (b, 0, 0))
    tile_out = pl.BlockSpec((tf, w), krow)
    return pl.pallas_call(
        body, name="fox_bwd", grid=(nb, n),
        in_specs=[ANY_SPEC, pl.BlockSpec((tf, w), krow), pl.BlockSpec((tf, w), lambda b, j: (b * n + j, 1)),
                  pl.BlockSpec((tf, LANES), krow), ANY_SPEC, rows, rows],
        out_specs=[tile_out, tile_out, tile_out, pl.BlockSpec((tf, LANES), krow),
                   pl.BlockSpec((1, 16, tf), lambda b, j: (b * n + j, 0, 0))],
        out_shape=[SDS((nb * s, w), BF16), SDS((nb * s, w), BF16), SDS((nb * s, w), BF16), SDS((nb * s, LANES), F32),
                   SDS((nb * n, 16, tf), F32)],
        scratch_shapes=[pltpu.VMEM((s, w), BF16), pltpu.VMEM((s, w), BF16),
                        pltpu.VMEM((FOX_HEADS, tf, PAIR), BF16), pltpu.VMEM((FOX_HEADS, tf, PAIR), BF16),
                        pltpu.VMEM((N_PAIRS, PAIR, tf), BF16), pltpu.VMEM((FOX_HEADS, tf, tf), F32),
                        pltpu.VMEM((FOX_HEADS, tf, PAIR), F32), pltpu.VMEM((FOX_HEADS, tf, PAIR), F32),
                        pltpu.VMEM((FOX_HEADS, tf, LANES), F32), pltpu.VMEM((n, N_PAIRS, PAIR, tf), F32),
                        pltpu.VMEM((n, 16, tf), F32), pltpu.SemaphoreType.DMA((2,))],
        compiler_params=_params(("arbitrary", "arbitrary")))(pq, kv, kv, fneg, dcat_bf, lse_rows, delta_rows)


ADAMW_TILE_ELEMS = 128 * 1024


def reduce_adamw(parts, w, m, v, name):
    layers, r, c = w.shape
    tr = r
    for cand in range(16, r, 16):
        if r % cand == 0 and cand * c <= ADAMW_TILE_ELEMS:
            tr = cand
    c1 = 1.0 - ADAM_B1 ** ADAM_STEP
    c2 = 1.0 - ADAM_B2 ** ADAM_STEP

    def body(*refs):
        p_refs = refs[:layers]
        w_ref, m_ref, v_ref, g_out, d_out, m_out, v_out = refs[layers:]

        def update(p_ref):
            g = p_ref[0].astype(F32)
            for k in range(1, N_DEV):
                g = g + p_ref[k].astype(F32)
            mn = ADAM_B1 * m_ref[0] + (1.0 - ADAM_B1) * g
            vn = ADAM_B2 * v_ref[0] + (1.0 - ADAM_B2) * (g * g)
            g_out[0] = g
            m_out[0] = mn
            v_out[0] = vn
            d_out[0] = -ADAM_LR * ((mn / c1) / (jnp.sqrt(vn / c2) + ADAM_EPS) + ADAM_WD * w_ref[0])

        if layers == 1:
            update(p_refs[0])
        else:
            for layer in range(layers):
                pl.when(pl.program_id(0) == layer)(lambda layer=layer: update(p_refs[layer]))

    row = pl.BlockSpec((1, tr, c), lambda l, i: (l, i, 0))
    return pl.pallas_call(
        body, name=name, grid=(layers, r // tr),
        in_specs=[pl.BlockSpec((N_DEV, tr, c), lambda l, i: (0, i, 0))] * layers + [row, row, row],
        out_specs=[row, row, row, row], out_shape=[SDS((layers, r, c), F32)] * 4,
        compiler_params=_params(("parallel", "parallel")))(*parts, w, m, v)


N_PEERS = N_DEV - 1
HBM_SPEC = pl.BlockSpec(memory_space=pltpu.HBM)
SEM_SPEC = pl.BlockSpec(memory_space=pltpu.SEMAPHORE)
ANY_SPEC = pl.BlockSpec(memory_space=pl.ANY)
SPLIT_EFFECT = pltpu.SideEffectType.DATAFLOW_SIDE_EFFECTING


def _peers(with_self=False):
    x, y, c = lax.axis_index("x"), lax.axis_index("y"), lax.axis_index("c")
    peers = []
    for k in range(0 if with_self else 1, N_DEV):
        px = 1 - x if (k >> 2) & 1 else x
        py = 1 - y if (k >> 1) & 1 else y
        pc = 1 - c if k & 1 else c
        peers.append(((px, py, pc), 4 * px + 2 * py + pc))
    return 4 * x + 2 * y + c, peers


def _push(src, dst, send_sems, recv_sems, slot, dev):
    return pltpu.make_async_remote_copy(src_ref=src, dst_ref=dst, send_sem=send_sems.at[slot], recv_sem=recv_sems.at[slot],
                                        device_id=dev, device_id_type=pl.DeviceIdType.MESH)


def _landing_shapes(arrs, scatter):
    return [SDS((N_DEV,) + tuple(a.shape[1:] if sc else a.shape), a.dtype) for a, sc in zip(arrs, scatter)]


def exchange(arrs, scatter, name):
    na = len(arrs)

    def body(*refs):
        ins = refs[:na]
        outs = refs[na:2 * na]
        send_sems, recv_sems, local_sems = refs[2 * na:]
        me, peers = _peers()
        local = []
        remote = []
        for a in range(na):
            lc = pltpu.make_async_copy(ins[a].at[me] if scatter[a] else ins[a], outs[a].at[me], local_sems.at[a])
            lc.start()
            local.append(lc)
            for k, (dev, idx) in enumerate(peers):
                cp = _push(ins[a].at[idx] if scatter[a] else ins[a], outs[a].at[me], send_sems, recv_sems,
                           a * N_PEERS + k, dev)
                cp.start()
                remote.append(cp)
        for a in range(na):
            for k, (dev, idx) in enumerate(peers):
                _push(ins[a].at[me] if scatter[a] else ins[a], outs[a].at[idx], send_sems, recv_sems,
                      a * N_PEERS + k, dev).wait_recv()
        for cp in remote:
            cp.wait_send()
        for lc in local:
            lc.wait()

    return pl.pallas_call(
        body, name=name, in_specs=[HBM_SPEC] * na, out_specs=[HBM_SPEC] * na, out_shape=_landing_shapes(arrs, scatter),
        scratch_shapes=[pltpu.SemaphoreType.DMA((na * N_PEERS,)), pltpu.SemaphoreType.DMA((na * N_PEERS,)),
                        pltpu.SemaphoreType.DMA((na,))])(*arrs)


def exchange_start(arrs, scatter, after, name):
    na = len(arrs)
    lands = [lax.empty(l.shape, l.dtype) for l in _landing_shapes(arrs, scatter)]

    def body(*refs):
        ins = refs[:na]
        land = refs[na:2 * na]
        send_sems, recv_sems = refs[2 * na + 1], refs[2 * na + 2]
        token = refs[-1]
        me, peers = _peers(with_self=True)
        for a in range(na):
            for k, (dev, idx) in enumerate(peers):
                _push(ins[a].at[idx] if scatter[a] else ins[a], land[a].at[me], send_sems, recv_sems,
                      a * N_DEV + k, dev).start()
        token[...] = jnp.zeros_like(token)

    thru = [pltpu.HBM(a.shape, a.dtype) for a in arrs] + [pltpu.HBM(l.shape, l.dtype) for l in lands]
    res = pl.pallas_call(
        body, name=name,
        out_shape=(pltpu.SemaphoreType.DMA((na * N_DEV,)), pltpu.SemaphoreType.DMA((na * N_DEV,)), *thru,
                   SDS((8, LANES), F32)),
        in_specs=[HBM_SPEC] * (2 * na) + [ANY_SPEC],
        out_specs=(SEM_SPEC, SEM_SPEC, *([HBM_SPEC] * (2 * na)), pl.BlockSpec(memory_space=pltpu.VMEM)),
        input_output_aliases={i: 2 + i for i in range(2 * na)},
        compiler_params=pltpu.CompilerParams(has_side_effects=SPLIT_EFFECT),
    )(*[pltpu.with_memory_space_constraint(a, pltpu.HBM) for a in arrs],
      *[pltpu.with_memory_space_constraint(l, pltpu.HBM) for l in lands], after)
    return {"send": res[0], "recv": res[1], "src": res[2:2 + na], "land": res[2 + na:2 + 2 * na],
            "token": res[-1][0, 0], "scatter": scatter}


def exchange_wait(handle, after, name):
    scatter = handle["scatter"]
    na = len(scatter)

    def body(*refs):
        src = refs[:na]
        land = refs[na:2 * na]
        send_sems, recv_sems = refs[2 * na], refs[2 * na + 1]
        me, peers = _peers(with_self=True)
        for a in range(na):
            for k, (dev, idx) in enumerate(peers):
                cp = _push(src[a].at[me] if scatter[a] else src[a], land[a].at[idx], send_sems, recv_sems,
                           a * N_DEV + k, dev)
                cp.wait_send()
                cp.wait_recv()

    ops = list(handle["src"]) + list(handle["land"])
    res = pl.pallas_call(
        body, name=name, out_shape=tuple(pltpu.HBM(o.shape, o.dtype) for o in ops),
        in_specs=[HBM_SPEC] * (2 * na) + [SEM_SPEC, SEM_SPEC, ANY_SPEC], out_specs=tuple([HBM_SPEC] * (2 * na)),
        input_output_aliases={i: i for i in range(2 * na)},
        compiler_params=pltpu.CompilerParams(has_side_effects=SPLIT_EFFECT),
    )(*ops, handle["send"], handle["recv"], after)
    return list(res[na:])


def forward_layer(l, xin, xin_bf, mem_bf, wt, nb, s, ffn_weights=None):
    sv = {"xin_bf": xin_bf}
    memkv = mm_nn(mem_bf, wt["memw"], BF16, f"memkv{l}")
    sv["memkv"] = memkv
    if l == 0:
        proj = mm_nn(xin_bf, wt["win_a"], F32, "proj_a")
        pooled, tok = pool_fwd(proj, wt["pw_bd"], wt["pscale"], nb, s)
        sv["pooled"] = pooled
    else:
        kv = mm_nn(xin_bf, wt["kvw"][:, :2 * TOK_WIDTH], BF16, "kv_proj")
        fl = mm_nn(xin_bf, wt["kvw"][:, 2 * TOK_WIDTH:], F32, "gate_proj")
        fneg = -fgate_fwd(fl, wt["fb"], nb, s)
        proj = mm_nn(xin_bf, wt["wq"], BF16, "proj_b")
        kaug, vaug_t = fox_prep(kv, fneg, nb, s)
        tok, o_f32, lse_rows = fox_fwd_t(proj, kaug, vaug_t, nb, s)
        sv.update(kv=kv, fl=fl, fneg=fneg, o_f32=o_f32, lse_rows=lse_rows)
    sv["proj"] = proj
    mem_out = memattn_fwd(proj, memkv, nb, s, f"memattn_fwd{l}")
    cat = jnp.concatenate([tok, mem_out], axis=1)
    sv["cat"] = cat
    x1, x1_bf, xh1, rs1 = ln_fwd(xin, cat, wt["wout"], wt["ln1_g"], wt["ln1_b"], f"out_proj_ln1_{l}")
    sv.update(x1_bf=x1_bf, xh1=xh1, rs1=rs1)
    if ffn_weights is not None:
        wt.update(ffn_weights(x1_bf))
    act, ga, gb, hu, hg = ffn_up_gate(x1_bf, wt["wup"], wt["cw"], nb, s, f"ffn_up_gate{l}")
    sv.update(act=act, ga=ga, gb=gb, hu=hu, hg=hg)
    x2, x2_bf, xh2, rs2 = ln_fwd(x1, act, wt["wdown"], wt["ln2_g"], wt["ln2_b"], f"ffn_down_ln2_{l}")
    sv.update(xh2=xh2, rs2=rs2)
    return x2, x2_bf, sv


def backward_layer(l, dy, sv, mem_bf, wt, nb, s, after_ffn=None, after_pool=None, loss_target=None):
    g = {}
    if loss_target is None:
        dr2, dr2_bf, g["ln2_g"], g["ln2_b"] = ln_bwd(dy[0], sv["xh2"], sv["rs2"], wt["ln2_g"], f"ln2_bwd{l}",
                                                     dy_scale=dy[1], products=dy[2])
    else:
        dr2, dr2_bf, g["ln2_g"], g["ln2_b"], g["loss_row"] = loss_ln_bwd(sv["xh2"], sv["rs2"], wt["ln2_g"], wt["ln2_b"],
                                                                         loss_target, f"loss_ln2_bwd{l}")
    dact = mm_nn(dr2_bf, wt["wdown"], BF16, f"ffn_down_dx{l}", trans_b=0)
    g["wdown"] = mm_tn(sv["act"], dr2_bf, f"ffn_down_dw{l}")
    dh_u, dh_g, dcw_u, dcw_g = gate_conv_bwd(dact, sv["ga"], sv["gb"], sv["hu"], sv["hg"], wt["cw"], nb, s,
                                             f"gate_conv_bwd{l}")
    g["cw"] = jnp.concatenate([dcw_u, dcw_g], axis=0)
    g["wup"] = jnp.concatenate([mm_tn(sv["x1_bf"], dh_u, f"ffn_up_dw_u{l}", blocked=True),
                                mm_tn(sv["x1_bf"], dh_g, f"ffn_up_dw_g{l}", blocked=True)], axis=0)
    ln1_g = wt["ln1_g"] if after_ffn is None else wt["ln1_g"] + after_ffn(g, dr2)
    dr1, dr1_bf, g["ln1_g"], g["ln1_b"] = ln_bwd(dr2, sv["xh1"], sv["rs1"], ln1_g, f"ffn_up_dx_ln1_bwd{l}",
                                                 dy_scale=DN_ALPHA, products=[(dh_u, wt["wup"], 0), (dh_g, wt["wup"], 1)])
    dcat, dcat_bf = mm_nn(dr1_bf, wt["wout"], F32, f"out_proj_dx{l}", also_bf16=True, trans_b=0)
    g["wout"] = mm_tn(sv["cat"], dr1_bf, f"out_proj_dw{l}")
    dqm, dmemkv = memattn_bwd(sv["proj"], sv["memkv"], dcat, nb, s, f"memattn_bwd{l}")
    g["memw"] = mm_tn(mem_bf, dmemkv, f"memkv_dw{l}")
    if l == 0:
        dmixed, dpooled, g["pscale"] = pool_bwd_mix(dcat, sv["pooled"], wt["pw_bd"], wt["pscale"], nb, s)
        g["pw_full"] = mm_tn(sv["pooled"], dmixed, "pool_dw")
        win_a = wt["win_a"] if after_pool is None else wt["win_a"] + after_pool(g, dmixed).astype(BF16)
        du = pool_bwd_window(dpooled, nb, s)
        dproj = jnp.concatenate([du, dqm], axis=1)
        dx = mm_nn(dproj, win_a, F32, "proj_a_dx", addend=dr1, add_scale=DN_ALPHA, trans_b=0)
        g["win_a"] = mm_tn(sv["xin_bf"], dproj, "proj_a_dw")
    else:
        delta = fox_delta(dcat, sv["o_f32"], nb, s)
        tf = min(TF, s)
        dq, dk, dv, dfcum_k, dfq_rows = fox_bwd(sv["proj"], sv["kv"], sv["fneg"], dcat_bf,
                                                sv["lse_rows"], _to_tile_rows(delta, nb, s, tf), nb, s)
        dfl, g["fb"] = fgate_bwd(_from_tile_rows(dfq_rows), dfcum_k, sv["fl"], wt["fb"], nb, s)
        dproj = jnp.concatenate([dq, dqm], axis=1)
        dkvf = jnp.concatenate([dk, dv, dfl.astype(BF16)], axis=1)
        dx = (dr1, DN_ALPHA, [(dproj, wt["wq"], 0), (dkvf, wt["kvw"], 0)])
        g["wq"] = mm_tn(sv["xin_bf"], dproj, "proj_b_dw")
        g["kvw"] = mm_tn(sv["xin_bf"], dkvf, "kv_proj_dw")
    return dx, g


def pack_replicated(pool_w, ln1_g, ln1_b, ln2_g, ln2_b, conv_b, f_b):
    cb = jnp.pad(conv_b, ((0, 0), (0, 6144 - 5504))).reshape(12, D_MODEL)
    fb = jnp.pad(f_b.reshape(1, FOX_HEADS), ((0, 3), (0, D_MODEL - FOX_HEADS)))
    return jnp.concatenate([pool_w.reshape(144, D_MODEL), ln1_g, ln1_b, ln2_g, ln2_b, cb, fb], axis=0)


def unpack_replicated(buf):
    pool_w = buf[:144].reshape(1, 4, POOL_GROUP, POOL_GROUP)
    ln = [buf[144 + 2 * k:146 + 2 * k] for k in range(4)]
    conv_b = buf[152:164].reshape(2, 6144)[:, :5504]
    f_b = buf[164, :FOX_HEADS]
    return pool_w, ln[0], ln[1], ln[2], ln[3], conv_b, f_b


def pack_small(conv_w, pool_scale):
    buf = jnp.zeros((16, FF_BLOCK_PAD), F32)
    buf = lax.dynamic_update_slice(buf, conv_w.reshape(DEPTH * 3, FF_BLOCK), (0, 0))
    return lax.dynamic_update_slice(buf, pool_scale, (8, 0))


def _block_diag(pw):
    out = jnp.zeros((TOK_WIDTH, TOK_WIDTH), pw.dtype)
    for g in range(4):
        out = lax.dynamic_update_slice(out, pw[g], (g * POOL_GROUP, g * POOL_GROUP))
    return out


def layer_shards(l, sq_a, sq_b, mem_w_kv, ffn_w_up, ffn_w_down):
    return [sq_a[0].astype(BF16), sq_b[0].astype(BF16), mem_w_kv[l].astype(BF16), ffn_w_up[l].astype(BF16),
            ffn_w_down[l].astype(BF16)]


def mixer_weights(l, gath, ln1_g, ln1_b, ln2_g, ln2_b):
    w_out = gath[1].reshape(D_MODEL, D_MODEL)
    wt = {"memw": gath[2].reshape(D_MODEL, 2 * MEM_WIDTH), "wout": w_out,
          "ln1_g": ln1_g[l:l + 1], "ln1_b": ln1_b[l:l + 1], "ln2_g": ln2_g[l:l + 1], "ln2_b": ln2_b[l:l + 1]}
    return wt, gath[0].reshape(D_MODEL, D_MODEL)


def ffn_weights(l, wup_g, wdown_g, small, conv_b):
    pad_c = FF_BLOCK_PAD - FF_BLOCK
    wup = jnp.pad(wup_g, ((0, 0), (0, 0), (0, pad_c))).transpose(1, 0, 2).reshape(D_MODEL, N_DEV * FF_BLOCK_PAD)
    wdown = jnp.pad(wdown_g.reshape(FF_PAIRS, FF_BLOCK, D_MODEL), ((0, 0), (0, pad_c), (0, 0)))
    wdown = wdown.reshape(FF_PAIRS * FF_BLOCK_PAD, D_MODEL)
    cb = jnp.pad(conv_b[l].reshape(N_DEV, FF_BLOCK), ((0, 0), (0, pad_c)))
    cw = jnp.concatenate([small[:, 3 * l:3 * l + 3, :], cb[:, None, :], jnp.zeros((N_DEV, 4, FF_BLOCK_PAD), F32)], axis=1)
    return {"wup": wup, "wdown": wdown, "cw": cw}


def mixer_grad_blocks(g, w_in_grad):
    blocks = [] if w_in_grad is None else [w_in_grad.reshape(N_DEV, 128, D_MODEL)]
    blocks += [g["wout"].reshape(N_DEV, 128, D_MODEL), g["memw"].reshape(N_DEV, 128, 2 * MEM_WIDTH)]
    return [b.astype(BF16) for b in blocks]


def ffn_grad_blocks(g):
    wup = g["wup"][:, :, :FF_BLOCK]
    wdown = g["wdown"].reshape(FF_PAIRS, FF_BLOCK_PAD, D_MODEL)[:, :FF_BLOCK].reshape(N_DEV, FF_ROWS, D_MODEL)
    return [wup.astype(BF16), wdown.astype(BF16)]


def small_grad_blocks(g0, g1):
    taps = jnp.stack([g0["cw"][:, :3, :], g1["cw"][:, :3, :]], axis=1).reshape(N_DEV, DEPTH * 3, FF_BLOCK_PAD)
    small = jnp.zeros((N_DEV, 16, FF_BLOCK_PAD), F32)
    small = lax.dynamic_update_slice(small, taps, (0, 0, 0))
    return lax.dynamic_update_slice(small, g0["pscale"].reshape(N_DEV, 1, 96), (0, 8, 0))


def replicated_grads(g0, g1):
    pw = jnp.stack([g0["pw_full"][k * POOL_GROUP:(k + 1) * POOL_GROUP, k * POOL_GROUP:(k + 1) * POOL_GROUP] for k in range(4)])
    conv_b = jnp.stack([g_["cw"][:, 3, :FF_BLOCK].reshape(N_DEV * FF_BLOCK) for g_ in (g0, g1)])
    ln = [jnp.concatenate([g0[n], g1[n]], axis=0) for n in ("ln1_g", "ln1_b", "ln2_g", "ln2_b")]
    return pack_replicated(pw[None], ln[0], ln[1], ln[2], ln[3], conv_b, g1["fb"][0, :FOX_HEADS])


def kernel(x, mem, a_w_in, a_pool_w, a_pool_scale, a_w_out, b_w_q, b_w_out, kv_w, f_b, mem_w_kv, ln1_g, ln1_b, ln2_g, ln2_b, ffn_w_up, ffn_conv_w, ffn_conv_b, ffn_w_down, loss_target, m_a_w_in, m_a_pool_w, m_a_pool_scale, m_a_w_out, m_b_w_q, m_b_w_out, m_kv_w, m_f_b, m_mem_w_kv, m_ln1_g, m_ln1_b, m_ln2_g, m_ln2_b, m_ffn_w_up, m_ffn_conv_w, m_ffn_conv_b, m_ffn_w_down, v_a_w_in, v_a_pool_w, v_a_pool_scale, v_a_w_out, v_b_w_q, v_b_w_out, v_kv_w, v_f_b, v_mem_w_kv, v_ln1_g, v_ln1_b, v_ln2_g, v_ln2_b, v_ffn_w_up, v_ffn_conv_w, v_ffn_conv_b, v_ffn_w_down):
    nb, s, d = x.shape
    t = nb * s
    x2d, mem_bf, target = x.reshape(t, d), mem.reshape(nb * MEM_LEN, d).astype(BF16), loss_target.reshape(t, d)

    shards0 = layer_shards(0, a_w_in, a_w_out, mem_w_kv, ffn_w_up, ffn_w_down)
    shards1 = layer_shards(1, b_w_q, b_w_out, mem_w_kv, ffn_w_up, ffn_w_down)
    shards1.append(jnp.pad(kv_w, ((0, 0), (0, KV_COLS_PAD - KV_COLS))).astype(BF16))
    gath0 = exchange(shards0[:3] + [pack_small(ffn_conv_w, a_pool_scale)], [False] * 4, "gather_w0_mixer")
    pending = {"ffn0": exchange_start(shards0[3:], [False] * 2, gath0[0], "gather_w0_ffn_start")}
    small = gath0[3]
    wt0, w_in = mixer_weights(0, gath0, ln1_g + pending["ffn0"]["token"], ln1_b, ln2_g, ln2_b)
    pw_bd = _block_diag(a_pool_w[0])
    wt0.update(win_a=w_in, pw_bd=pw_bd.astype(BF16),
               pscale=small[:, 8, :96].reshape(1, TOK_WIDTH) + pending["ffn0"]["token"])

    def ffn0_weights(x1_bf):
        got = exchange_wait(pending["ffn0"], x1_bf, "gather_w0_ffn_wait")
        pending["w1"] = exchange_start(shards1, [False] * 6, got[0], "gather_w1_start")
        w = ffn_weights(0, got[0], got[1], small, ffn_conv_b)
        w["cw"] = w["cw"] + pending["w1"]["token"]
        return w

    x1, x1_bf, sv0 = forward_layer(0, x2d, x2d, mem_bf, wt0, nb, s, ffn_weights=ffn0_weights)
    gath1 = exchange_wait(pending["w1"], x1_bf, "gather_w1_wait")
    wt1, w_q = mixer_weights(1, gath1, ln1_g, ln1_b, ln2_g, ln2_b)
    wt1.update(ffn_weights(1, gath1[3], gath1[4], small, ffn_conv_b))
    kvw = gath1[5].reshape(D_MODEL, KV_COLS_PAD)
    wt1.update(wq=w_q, kvw=kvw,
               fb=jnp.pad(f_b.reshape(1, FOX_HEADS), ((0, 0), (0, LANES - FOX_HEADS))))
    _, _, sv1 = forward_layer(1, x1, x1_bf, mem_bf, wt1, nb, s)

    dx1, g1 = backward_layer(1, None, sv1, mem_bf, wt1, nb, s, loss_target=target)
    loss = lax.psum(g1["loss_row"][0, 0], ("x", "y", "c"))
    blocks1 = (mixer_grad_blocks(g1, g1["wq"]) + ffn_grad_blocks(g1)
               + [g1["kvw"][:, :KV_COLS].reshape(N_DEV, 128, KV_COLS).astype(BF16)])
    pending["g1"] = exchange_start(blocks1, [True] * 6, dx1[0], "scatter_g1_start")
    wt0["ln2_g"] = wt0["ln2_g"] + pending["g1"]["token"]

    def after_ffn0(g, dxm):
        pending["gf0"] = exchange_start(ffn_grad_blocks(g), [True] * 2, dxm, "scatter_g0_ffn_start")
        return pending["gf0"]["token"]

    def after_pool0(g, x):
        blocks = mixer_grad_blocks(g, None) + [small_grad_blocks(g, g1), replicated_grads(g, g1)]
        pending["gm0"] = exchange_start(blocks, [True] * 3 + [False], x, "scatter_g0_mixer_start")
        return pending["gm0"]["token"]

    grad_x, g0 = backward_layer(0, dx1, sv0, mem_bf, wt0, nb, s, after_ffn=after_ffn0, after_pool=after_pool0)
    pending["gin"] = exchange_start([g0["win_a"].reshape(N_DEV, 128, D_MODEL).astype(BF16)], [True], grad_x,
                                    "scatter_g0_in_start")
    parts_f0 = exchange_wait(pending["gf0"], jnp.zeros((8, LANES), F32) + pending["gin"]["token"], "scatter_g0_ffn_wait")
    parts1 = exchange_wait(pending["g1"], parts_f0[0], "scatter_g1_wait")

    res = {}

    def upd(nm, parts, w2, m2, v2):
        res[nm] = reduce_adamw(parts, w2, m2, v2, f"adamw_{nm}")

    upd("b_w_q", [parts1[0]], b_w_q, m_b_w_q, v_b_w_q)
    upd("b_w_out", [parts1[1]], b_w_out, m_b_w_out, v_b_w_out)
    upd("kv_w", [parts1[5]], kv_w[None], m_kv_w[None], v_kv_w[None])
    upd("ffn_w_up", [parts_f0[0], parts1[3]], ffn_w_up, m_ffn_w_up, v_ffn_w_up)
    upd("ffn_w_down", [parts_f0[1], parts1[4]], ffn_w_down, m_ffn_w_down, v_ffn_w_down)
    parts_m0 = exchange_wait(pending["gm0"], res["ffn_w_down"][0], "scatter_g0_mixer_wait")
    parts_in = exchange_wait(pending["gin"], parts_m0[0], "scatter_g0_in_wait")
    upd("a_w_in", [parts_in[0]], a_w_in, m_a_w_in, v_a_w_in)
    upd("a_w_out", [parts_m0[0]], a_w_out, m_a_w_out, v_a_w_out)
    upd("mem_w_kv", [parts_m0[1], parts1[2]], mem_w_kv, m_mem_w_kv, v_mem_w_kv)
    upd("small", [parts_m0[2]], pack_small(ffn_conv_w, a_pool_scale)[None], pack_small(m_ffn_conv_w, m_a_pool_scale)[None],
        pack_small(v_ffn_conv_w, v_a_pool_scale)[None])
    upd("replicated", [parts_m0[3]], pack_replicated(a_pool_w, ln1_g, ln1_b, ln2_g, ln2_b, ffn_conv_b, f_b)[None],
        pack_replicated(m_a_pool_w, m_ln1_g, m_ln1_b, m_ln2_g, m_ln2_b, m_ffn_conv_b, m_f_b)[None],
        pack_replicated(v_a_pool_w, v_ln1_g, v_ln1_b, v_ln2_g, v_ln2_b, v_ffn_conv_b, v_f_b)[None])

    res["kv_w"] = [o[0] for o in res["kv_w"]]
    res["ffn_conv_w"] = [o[0, :DEPTH * 3, :FF_BLOCK].reshape(DEPTH, 3, FF_BLOCK) for o in res["small"]]
    res["a_pool_scale"] = [o[0, 8:9, :96] for o in res["small"]]
    rep_names = ["a_pool_w", "ln1_g", "ln1_b", "ln2_g", "ln2_b", "ffn_conv_b", "f_b"]
    for nm in rep_names:
        res[nm] = []
    for o in res["replicated"]:
        for nm, val in zip(rep_names, unpack_replicated(o[0])):
            res[nm].append(val)

    order = ["a_w_in", "a_pool_w", "a_pool_scale", "a_w_out", "b_w_q", "b_w_out", "kv_w", "f_b", "mem_w_kv",
             "ln1_g", "ln1_b", "ln2_g", "ln2_b", "ffn_w_up", "ffn_conv_w", "ffn_conv_b", "ffn_w_down"]
    out = [loss, grad_x.reshape(nb, s, d)]
    for kind in range(4):
        out.extend(res[nm][kind] for nm in order)
    return tuple(out)
```

```python
import jax
import jax.numpy as jnp
from jax import lax
from jax.experimental import pallas as pl
from jax.experimental.pallas import tpu as pltpu

F32 = jnp.float32
BF16 = jnp.bfloat16
SDS = jax.ShapeDtypeStruct

N_DEV = 8
D_MODEL = 1024
TOK_WIDTH = 768
MEM_WIDTH = 256
MEM_LEN = 256
MEM_HEADS = 4
HEAD_DIM = 64
FOX_HEADS = 12
POOL_GROUP = 192
FF_BLOCK = 688
FF_BLOCK_PAD = 768
FF_PAIRS = 4
FF_ROWS = 344
FF_ROWS_PAD = FF_BLOCK_PAD // 2
KV_COLS = 1548
KV_COLS_PAD = 1664
LANES = 128
DEPTH = 2
DN_ALPHA = (2.0 * DEPTH) ** 0.25
LN_EPS = 1e-5
QK_SCALE = HEAD_DIM ** -0.5
NEG_BIG = -1e30

ADAM_LR = 0.001
ADAM_B1 = 0.9
ADAM_B2 = 0.999
ADAM_EPS = 1e-08
ADAM_WD = 0.01
ADAM_STEP = 10

VMEM_LIMIT_BYTES = 56 * 1024 * 1024
MM_BLOCK_BYTES = 6 * 1024 * 1024
TM = 512
TS = 256
TF = 256
TC = 256
HALO_POOL = 16
HALO_CONV = 8

NT_DIMS = (((1,), (1,)), ((), ()))
TN_DIMS = (((0,), (0,)), ((), ()))


def _params(sem=None):
    return pltpu.CompilerParams(dimension_semantics=sem, vmem_limit_bytes=VMEM_LIMIT_BYTES)


def _sigmoid(z):
    return 1.0 / (1.0 + jnp.exp(-z))


def _pick_tn(n):
    if n <= 2048:
        return n
    for t in (1024, 768, 512, 256, 128):
        if n % t == 0:
            return t
    return n


def mm_nn(a, b, out_dtype, name, addend=None, add_scale=1.0, also_bf16=False, trans_b=None):
    m, k = a.shape
    n = b.shape[1] if trans_b is None else b.shape[0]
    tm = min(TM, m)
    tn = n
    while k * tn * 2 > MM_BLOCK_BYTES or tm * tn * 4 > MM_BLOCK_BYTES:
        tn //= 2
    chunk = tn if tn <= 2048 else _pick_tn(tn)
    has_add = addend is not None

    def body(*refs):
        a_ref, b_ref = refs[0], refs[1]
        c_ref = refs[2] if has_add else None
        o_ref = refs[3] if has_add else refs[2]
        ob_ref = refs[-1] if also_bf16 else None
        av = a_ref[...].astype(BF16)
        for c in range(tn // chunk):
            cols = slice(c * chunk, (c + 1) * chunk)
            if trans_b is None:
                r = jnp.dot(av, b_ref[:, cols].astype(BF16), preferred_element_type=F32)
            else:
                r = lax.dot_general(av, b_ref[cols, :].astype(BF16), NT_DIMS, preferred_element_type=F32)
            if has_add:
                r = r + add_scale * c_ref[:, cols]
            o_ref[:, cols] = r.astype(out_dtype)
            if also_bf16:
                ob_ref[:, cols] = r.astype(BF16)

    b_spec = (pl.BlockSpec((k, tn), lambda j, i: (0, j)) if trans_b is None
              else pl.BlockSpec((tn, k), lambda j, i: (j, trans_b)))
    in_specs = [pl.BlockSpec((tm, k), lambda j, i: (i, 0)), b_spec]
    ops = [a, b]
    tile = pl.BlockSpec((tm, tn), lambda j, i: (i, j))
    if has_add:
        in_specs.append(tile)
        ops.append(addend)
    out_shape = [SDS((m, n), out_dtype)]
    out_specs = [tile]
    if also_bf16:
        out_shape.append(SDS((m, n), BF16))
        out_specs.append(tile)
    res = pl.pallas_call(
        body, name=name, grid=(n // tn, m // tm), in_specs=in_specs, out_specs=out_specs, out_shape=out_shape,
        compiler_params=_params(("parallel", "parallel")))(*ops)
    return tuple(res) if also_bf16 else res[0]


def mm_tn(a, b, name, blocked=False):
    t, m = a.shape
    _, n = b.shape
    tt = min(4 * TM, t)
    tm = 1024 if m % 1024 == 0 else m
    tn = FF_BLOCK_PAD if blocked else _pick_tn(n)
    nt = t // tt

    def body(a_ref, b_ref, o_ref):
        kk = pl.program_id(2)
        r = lax.dot_general(a_ref[...].astype(BF16), b_ref[...].astype(BF16), TN_DIMS, preferred_element_type=F32)
        if blocked:
            r = r[None]

        @pl.when(kk == 0)
        def _():
            o_ref[...] = r

        @pl.when(kk != 0)
        def _():
            o_ref[...] += r

    if blocked:
        out_shape = SDS((n // tn, m, tn), F32)
        out_spec = pl.BlockSpec((1, tm, tn), lambda i, j, kk: (j, i, 0))
    else:
        out_shape = SDS((m, n), F32)
        out_spec = pl.BlockSpec((tm, tn), lambda i, j, kk: (i, j))
    return pl.pallas_call(
        body, name=name, grid=(m // tm, n // tn, nt),
        in_specs=[pl.BlockSpec((tt, tm), lambda i, j, kk: (kk, i)), pl.BlockSpec((tt, tn), lambda i, j, kk: (kk, j))],
        out_specs=out_spec, out_shape=out_shape,
        compiler_params=_params(("parallel", "parallel", "arbitrary")))(a, b)


def ln_fwd(xprev, a, w, g, b, name):
    t, d = xprev.shape
    k = a.shape[1]
    tm = min(TM, t)

    def body(xp_ref, a_ref, w_ref, g_ref, b_ref, y_ref, yb_ref, xh_ref, rs_ref):
        r = DN_ALPHA * xp_ref[...] + jnp.dot(a_ref[...], w_ref[...], preferred_element_type=F32)
        mu = jnp.mean(r, axis=1, keepdims=True)
        xc = r - mu
        var = jnp.mean(xc * xc, axis=1, keepdims=True)
        rstd = lax.rsqrt(var + LN_EPS)
        xh = xc * rstd
        y = xh * g_ref[...] + b_ref[...]
        y_ref[...] = y
        yb_ref[...] = y.astype(BF16)
        xh_ref[...] = xh
        rs_ref[...] = jnp.broadcast_to(rstd, (tm, LANES))

    row = pl.BlockSpec((tm, d), lambda i: (i, 0))
    vec = pl.BlockSpec((1, d), lambda i: (0, 0))
    return pl.pallas_call(
        body, name=name, grid=(t // tm,),
        in_specs=[row, pl.BlockSpec((tm, k), lambda i: (i, 0)), pl.BlockSpec((k, d), lambda i: (0, 0)), vec, vec],
        out_specs=[row, row, row, pl.BlockSpec((tm, LANES), lambda i: (i, 0))],
        out_shape=[SDS((t, d), F32), SDS((t, d), BF16), SDS((t, d), F32), SDS((t, LANES), F32)],
        compiler_params=_params(("parallel",)))(xprev, a, w, g, b)


def ln_bwd(dy, xhat, rstd, g, name, products=(), dy_scale=1.0):
    t, d = dy.shape
    np_ = len(products)
    tm = min(TM if sum(a.shape[1] for a, _, _ in products) <= 4096 else TS, t)

    def body(*refs):
        prod_refs = refs[:2 * np_]
        dy_ref, xh_ref, rs_ref, g_ref, dr_ref, drb_ref, dg_ref, db_ref = refs[2 * np_:]
        i = pl.program_id(0)
        dyv = dy_ref[...] if dy_scale == 1.0 else dy_scale * dy_ref[...]
        for p in range(np_):
            a_ref, w_ref = prod_refs[2 * p], prod_refs[2 * p + 1]
            if len(w_ref.shape) == 2:
                dyv = dyv + lax.dot_general(a_ref[...], w_ref[...], NT_DIMS, preferred_element_type=F32)
            else:
                kb = w_ref.shape[2]
                for c in range(w_ref.shape[0]):
                    dyv = dyv + lax.dot_general(a_ref[:, c * kb:(c + 1) * kb], w_ref[c], NT_DIMS,
                                                preferred_element_type=F32)
        xh = xh_ref[...]
        dxh = dyv * g_ref[...]
        m1 = jnp.mean(dxh, axis=1, keepdims=True)
        m2 = jnp.mean(dxh * xh, axis=1, keepdims=True)
        dr = rs_ref[:, 0:1] * (dxh - m1 - xh * m2)
        dr_ref[...] = dr
        drb_ref[...] = dr.astype(BF16)

        @pl.when(i == 0)
        def _():
            dg_ref[...] = jnp.zeros_like(dg_ref)
            db_ref[...] = jnp.zeros_like(db_ref)

        dg_ref[...] += jnp.sum(dyv * xh, axis=0, keepdims=True)
        db_ref[...] += jnp.sum(dyv, axis=0, keepdims=True)

    row = pl.BlockSpec((tm, d), lambda i: (i, 0))
    vec = pl.BlockSpec((1, d), lambda i: (0, 0))
    in_specs = [row, row, pl.BlockSpec((tm, LANES), lambda i: (i, 0)), vec]
    ops = [dy, xhat, rstd, g]
    for a, w, col in reversed(products):
        k = a.shape[1]
        if w.ndim == 2:
            w_spec = pl.BlockSpec((d, k), lambda i, col=col: (0, col))
        else:
            w_spec = pl.BlockSpec((k // w.shape[2], d, w.shape[2]), lambda i, col=col: (col, 0, 0))
        in_specs = [pl.BlockSpec((tm, k), lambda i: (i, 0)), w_spec] + in_specs
        ops = [a, w] + ops
    return pl.pallas_call(
        body, name=name, grid=(t // tm,), in_specs=in_specs, out_specs=[row, row, vec, vec],
        out_shape=[SDS((t, d), F32), SDS((t, d), BF16), SDS((1, d), F32), SDS((1, d), F32)],
        compiler_params=_params(("arbitrary",)))(*ops)


def loss_ln_bwd(xhat, rstd, g, beta, target, name):
    t, d = xhat.shape
    tm = min(TM, t)
    nsteps = t // tm

    def body(xh_ref, rs_ref, g_ref, b_ref, t_ref, dr_ref, drb_ref, dg_ref, db_ref, l_ref, acc):
        i = pl.program_id(0)
        xh = xh_ref[...]
        diff = xh * g_ref[...] + b_ref[...] - t_ref[...]
        dyv = diff * (1.0 / d)
        dxh = dyv * g_ref[...]
        m1 = jnp.mean(dxh, axis=1, keepdims=True)
        m2 = jnp.mean(dxh * xh, axis=1, keepdims=True)
        dr = rs_ref[:, 0:1] * (dxh - m1 - xh * m2)
        dr_ref[...] = dr
        drb_ref[...] = dr.astype(BF16)

        @pl.when(i == 0)
        def _():
            dg_ref[...] = jnp.zeros_like(dg_ref)
            db_ref[...] = jnp.zeros_like(db_ref)
            acc[...] = jnp.zeros_like(acc)

        dg_ref[...] += jnp.sum(dyv * xh, axis=0, keepdims=True)
        db_ref[...] += jnp.sum(dyv, axis=0, keepdims=True)
        acc[...] += jnp.sum(diff * diff, axis=0, keepdims=True)

        @pl.when(i == nsteps - 1)
        def _():
            tot = jnp.sum(acc[...], axis=1, keepdims=True) * (0.5 / d)
            l_ref[...] = jnp.broadcast_to(tot, (1, LANES))

    row = pl.BlockSpec((tm, d), lambda i: (i, 0))
    vec = pl.BlockSpec((1, d), lambda i: (0, 0))
    return pl.pallas_call(
        body, name=name, grid=(nsteps,),
        in_specs=[row, pl.BlockSpec((tm, LANES), lambda i: (i, 0)), vec, vec, row],
        out_specs=[row, row, vec, vec, pl.BlockSpec((1, LANES), lambda i: (0, 0))],
        out_shape=[SDS((t, d), F32), SDS((t, d), BF16), SDS((1, d), F32), SDS((1, d), F32), SDS((1, LANES), F32)],
        scratch_shapes=[pltpu.VMEM((1, d), F32)],
        compiler_params=_params(("arbitrary",)))(xhat, rstd, g, beta, target)


def memattn_fwd(proj, memkv, nb, s, name):
    ts = min(TM, s)
    nq = s // ts

    def body(q_ref, kv_ref, o_ref):
        top = lax.broadcasted_iota(jnp.int32, (PAIR, ts), 0) < HEAD_DIM
        scores = []
        for p in range(MEM_HEADS // 2):
            qp = q_ref[:, p * PAIR:(p + 1) * PAIR].astype(BF16)
            ke, ko = _split_pair(kv_ref[:, p * PAIR:(p + 1) * PAIR], QK_SCALE)
            scores.append([lax.dot_general(km, qp, NT_DIMS, preferred_element_type=F32) for km in (ke, ko)])
        for p in range(MEM_HEADS // 2):
            vt = kv_ref[:, MEM_WIDTH + p * PAIR:MEM_WIDTH + (p + 1) * PAIR].astype(F32).T.astype(BF16)
            outs = []
            for sc in scores[p]:
                e = jnp.exp(sc - jnp.max(sc, axis=0, keepdims=True))
                pr = e / jnp.sum(e, axis=0, keepdims=True)
                outs.append(jnp.dot(vt, pr.astype(BF16), preferred_element_type=F32))
            o_ref[:, p * PAIR:(p + 1) * PAIR] = jnp.where(top, outs[0], outs[1]).T.astype(BF16)

    return pl.pallas_call(
        body, name=name, grid=(nb, nq),
        in_specs=[pl.BlockSpec((ts, MEM_WIDTH), lambda b, i: (b * nq + i, 3)),
                  pl.BlockSpec((MEM_LEN, 2 * MEM_WIDTH), lambda b, i: (b, 0))],
        out_specs=pl.BlockSpec((ts, MEM_WIDTH), lambda b, i: (b * nq + i, 0)),
        out_shape=SDS((nb * s, MEM_WIDTH), BF16),
        compiler_params=_params(("parallel", "parallel")))(proj, memkv)


def memattn_bwd(proj, memkv, dcat, nb, s, name):
    ts = min(TM, s)
    nq = s // ts

    def body(q_ref, kv_ref, do_ref, dq_ref, dkv_ref):
        i = pl.program_id(1)

        @pl.when(i == 0)
        def _():
            dkv_ref[...] = jnp.zeros_like(dkv_ref)

        lo = _half_masks(MEM_LEN)
        top = lax.broadcasted_iota(jnp.int32, (PAIR, ts), 0) < HEAD_DIM
        n_pairs = MEM_HEADS // 2
        qs, dos, kps, products = [], [], [], []
        for p in range(n_pairs):
            qp = q_ref[:, p * PAIR:(p + 1) * PAIR].astype(BF16)
            dop = do_ref[:, p * PAIR:(p + 1) * PAIR].astype(BF16)
            kp = kv_ref[:, p * PAIR:(p + 1) * PAIR] * QK_SCALE
            kms = _split_pair(kp)
            vms = _split_pair(kv_ref[:, MEM_WIDTH + p * PAIR:MEM_WIDTH + (p + 1) * PAIR])
            products.append([(lax.dot_general(km, qp, NT_DIMS, preferred_element_type=F32),
                              lax.dot_general(vm, dop, NT_DIMS, preferred_element_type=F32)) for km, vm in zip(kms, vms)])
            qs.append(qp)
            dos.append(dop)
            kps.append(kp)
        for p in range(n_pairs):
            kt = kps[p].astype(F32).T.astype(BF16)
            dks, dvs, dqs = [], [], []
            for sc, dp in products[p]:
                e = jnp.exp(sc - jnp.max(sc, axis=0, keepdims=True))
                pr = e / jnp.sum(e, axis=0, keepdims=True)
                dl = jnp.sum(pr * dp, axis=0, keepdims=True)
                ds = (pr * (dp - dl)).astype(BF16)
                dvs.append(jnp.dot(pr.astype(BF16), dos[p], preferred_element_type=F32))
                dks.append(jnp.dot(ds, qs[p], preferred_element_type=F32))
                dqs.append(jnp.dot(kt, ds, preferred_element_type=F32))
            dq_ref[:, p * PAIR:(p + 1) * PAIR] = jnp.where(top, dqs[0], dqs[1]).T.astype(BF16)
            dkv_ref[:, p * PAIR:(p + 1) * PAIR] += jnp.where(lo, dks[0], dks[1]) * QK_SCALE
            dkv_ref[:, MEM_WIDTH + p * PAIR:MEM_WIDTH + (p + 1) * PAIR] += jnp.where(lo, dvs[0], dvs[1])

    return pl.pallas_call(
        body, name=name, grid=(nb, nq),
        in_specs=[pl.BlockSpec((ts, MEM_WIDTH), lambda b, i: (b * nq + i, 3)),
                  pl.BlockSpec((MEM_LEN, 2 * MEM_WIDTH), lambda b, i: (b, 0)),
                  pl.BlockSpec((ts, MEM_WIDTH), lambda b, i: (b * nq + i, 3))],
        out_specs=[pl.BlockSpec((ts, MEM_WIDTH), lambda b, i: (b * nq + i, 0)),
                   pl.BlockSpec((MEM_LEN, 2 * MEM_WIDTH), lambda b, i: (b, 0))],
        out_shape=[SDS((nb * s, MEM_WIDTH), BF16), SDS((nb * MEM_LEN, 2 * MEM_WIDTH), F32)],
        compiler_params=_params(("parallel", "arbitrary")))(proj, memkv, dcat)


def _pool_select(shape, s2, s4, s8, s16):
    lane = lax.broadcasted_iota(jnp.int32, shape, 1)
    return jnp.where(lane < POOL_GROUP, s2, jnp.where(lane < 2 * POOL_GROUP, s4, jnp.where(lane < 3 * POOL_GROUP, s8, s16)))


def _pool_count(shape, first_pos):
    pos = first_pos + lax.broadcasted_iota(jnp.int32, shape, 0)
    win = _pool_select(shape, 2, 4, 8, 16)
    return jnp.minimum(pos + 1, win).astype(F32)


def pool_fwd(proj, pw_bd, pscale, nb, s):
    ts = min(TS, s)
    nq = s // ts
    w = TOK_WIDTH

    def body(c_ref, h_ref, w_ref, sc_ref, pooled_ref, tok_ref):
        i = pl.program_id(0) % nq
        cur = c_ref[...]
        halo = jnp.where(i == 0, 0.0, h_ref[...])
        xe = jnp.concatenate([halo, cur], axis=0)
        s2 = xe + pltpu.roll(xe, 1, axis=0)
        s4 = s2 + pltpu.roll(s2, 2, axis=0)
        s8 = s4 + pltpu.roll(s4, 4, axis=0)
        s16 = s8 + pltpu.roll(s8, 8, axis=0)
        hp = HALO_POOL
        ws = _pool_select((ts, w), s2[hp:], s4[hp:], s8[hp:], s16[hp:])
        pooled = (ws / _pool_count((ts, w), i * ts) - cur).astype(BF16)
        pooled_ref[...] = pooled
        mixed = jnp.dot(pooled, w_ref[...], preferred_element_type=F32)
        tok_ref[...] = (mixed * sc_ref[...]).astype(BF16)

    row = pl.BlockSpec((ts, w), lambda r: (r, 0))
    return pl.pallas_call(
        body, name="pool_fwd", grid=(nb * nq,),
        in_specs=[row, pl.BlockSpec((HALO_POOL, w), lambda r: (jnp.maximum(r * (ts // HALO_POOL) - 1, 0), 0)),
                  pl.BlockSpec((w, w), lambda r: (0, 0)), pl.BlockSpec((1, w), lambda r: (0, 0))],
        out_specs=[row, row], out_shape=[SDS((nb * s, w), BF16), SDS((nb * s, w), BF16)],
        compiler_params=_params(("parallel",)))(proj, proj, pw_bd, pscale)


def pool_bwd_mix(dcat, pooled, pw_bd, pscale, nb, s):
    ts = min(TS, s)
    w = TOK_WIDTH

    def body(dt_ref, p_ref, w_ref, sc_ref, dm_ref, dp_ref, ds_ref):
        r = pl.program_id(0)
        dtok = dt_ref[...]
        mixed = jnp.dot(p_ref[...], w_ref[...], preferred_element_type=F32)

        @pl.when(r == 0)
        def _():
            ds_ref[...] = jnp.zeros_like(ds_ref)

        ds_ref[...] += jnp.sum(dtok * mixed, axis=0, keepdims=True)
        dmx = (dtok * sc_ref[...]).astype(BF16)
        dm_ref[...] = dmx
        dp_ref[...] = lax.dot_general(dmx, w_ref[...], NT_DIMS, preferred_element_type=F32)

    row = pl.BlockSpec((ts, w), lambda r: (r, 0))
    mat = pl.BlockSpec((w, w), lambda r: (0, 0))
    vec = pl.BlockSpec((1, w), lambda r: (0, 0))
    return pl.pallas_call(
        body, name="pool_bwd_mix", grid=(nb * s // ts,), in_specs=[row, row, mat, vec],
        out_specs=[row, row, vec], out_shape=[SDS((nb * s, w), BF16), SDS((nb * s, w), F32), SDS((1, w), F32)],
        compiler_params=_params(("arbitrary",)))(dcat, pooled, pw_bd, pscale)


def pool_bwd_window(dpooled, nb, s):
    ts = min(TS, s)
    nq = s // ts
    w = TOK_WIDTH
    n_ext = ts + HALO_POOL
    n_halo_blocks = nb * s // HALO_POOL

    def body(c_ref, n_ref, du_ref):
        i = pl.program_id(0) % nq
        cur = c_ref[...]
        nxt = jnp.where(i == nq - 1, 0.0, n_ref[...])
        ze = jnp.concatenate([cur, nxt], axis=0) / _pool_count((n_ext, w), i * ts)
        s2 = ze + pltpu.roll(ze, n_ext - 1, axis=0)
        s4 = s2 + pltpu.roll(s2, n_ext - 2, axis=0)
        s8 = s4 + pltpu.roll(s4, n_ext - 4, axis=0)
        s16 = s8 + pltpu.roll(s8, n_ext - 8, axis=0)
        ws = _pool_select((ts, w), s2[:ts], s4[:ts], s8[:ts], s16[:ts])
        du_ref[...] = (ws - cur).astype(BF16)

    row = pl.BlockSpec((ts, w), lambda r: (r, 0))
    return pl.pallas_call(
        body, name="pool_bwd_window", grid=(nb * nq,),
        in_specs=[row, pl.BlockSpec((HALO_POOL, w),
                                    lambda r: (jnp.minimum((r + 1) * (ts // HALO_POOL), n_halo_blocks - 1), 0))],
        out_specs=row, out_shape=SDS((nb * s, w), BF16),
        compiler_params=_params(("parallel",)))(dpooled, dpooled)


def _conv_rows(xe, w_ref):
    return (w_ref[0, 2:3, :] * xe + w_ref[0, 1:2, :] * pltpu.roll(xe, 1, axis=0)
            + w_ref[0, 0:1, :] * pltpu.roll(xe, 2, axis=0) + w_ref[0, 3:4, :])


def ffn_up_gate(x_bf, wup, cw, nb, s, name):
    tm = min(2 * TM, s)
    nq = s // tm
    w = FF_BLOCK_PAD
    hr = 2 * HALO_CONV
    k = x_bf.shape[1]

    def body(xc_ref, xh_ref, wu_ref, wg_ref, cu_ref, cg_ref, act_ref, a_ref, b_ref, hu_ref, hg_ref):
        first = (pl.program_id(1) % nq) == 0
        xc = xc_ref[...]
        xh = xh_ref[...]

        def products(w_ref):
            return (jnp.dot(xc, w_ref[0], preferred_element_type=F32), jnp.dot(xh, w_ref[0], preferred_element_type=F32))

        def conv(hcur, hprev, c_ref, h_out):
            h_out[...] = hcur.astype(BF16)
            xe = jnp.concatenate([jnp.where(first, 0.0, hprev), hcur], axis=0)
            return _conv_rows(xe, c_ref)[hr:]

        pu, pg = products(wu_ref), products(wg_ref)
        cu = conv(*pu, cu_ref, hu_ref)
        cg = conv(*pg, cg_ref, hg_ref)
        sg = _sigmoid(cg)
        a = cg * sg
        act_ref[...] = (a * cu).astype(BF16)
        a_ref[...] = a.astype(BF16)
        b_ref[...] = (cu * (sg * (1.0 + cg * (1.0 - sg)))).astype(BF16)

    def wblock(off):
        return pl.BlockSpec((1, k, w), lambda j, r: (j + off, 0, 0))

    def cblock(off):
        return pl.BlockSpec((1, 8, w), lambda j, r: (j + off, 0, 0))

    tile = pl.BlockSpec((tm, w), lambda j, r: (r, j))
    out = SDS((nb * s, FF_PAIRS * w), BF16)
    return pl.pallas_call(
        body, name=name, grid=(FF_PAIRS, nb * nq),
        in_specs=[pl.BlockSpec((tm, k), lambda j, r: (r, 0)),
                  pl.BlockSpec((hr, k), lambda j, r: (jnp.maximum(r * (tm // hr) - 1, 0), 0)),
                  wblock(0), wblock(FF_PAIRS), cblock(0), cblock(FF_PAIRS)],
        out_specs=[tile] * 5, out_shape=[out] * 5,
        compiler_params=_params(("parallel", "parallel")))(x_bf, x_bf, wup, wup, cw, cw)


def gate_conv_bwd(dact, a, b, hu, hg, cw, nb, s, name):
    ts = min(TS, s)
    nq = s // ts
    w = FF_BLOCK_PAD
    hc = HALO_CONV
    hb = 2 * hc
    n_ext = ts + hc

    def body(dc_ref, dn_ref, ac_ref, an_ref, bc_ref, bn_ref, hu_ref, hg_ref, wu_ref, wg_ref,
             dhu_ref, dhg_ref, dwu_ref, dwg_ref):
        r = pl.program_id(1)
        last = (r % nq) == nq - 1

        def ext(c_ref, n_ref, mask_next=False):
            nxt = n_ref[...].astype(F32)[:hc]
            if mask_next:
                nxt = jnp.where(last, 0.0, nxt)
            return jnp.concatenate([c_ref[...].astype(F32), nxt], axis=0)

        da = ext(dc_ref, dn_ref, mask_next=True)

        def branch(dcv, w_ref, h_ref, dh_ref, dw_ref):
            d0 = dcv[:ts]
            d1 = pltpu.roll(dcv, n_ext - 1, axis=0)[:ts]
            d2 = pltpu.roll(dcv, n_ext - 2, axis=0)[:ts]
            dh_ref[...] = (w_ref[0, 2:3, :] * d0 + w_ref[0, 1:2, :] * d1 + w_ref[0, 0:1, :] * d2).astype(BF16)
            hv = h_ref[...].astype(F32)
            rows = [jnp.sum(d2 * hv, axis=0, keepdims=True), jnp.sum(d1 * hv, axis=0, keepdims=True),
                    jnp.sum(d0 * hv, axis=0, keepdims=True), jnp.sum(d0, axis=0, keepdims=True)]
            sub = lax.broadcasted_iota(jnp.int32, (8, w), 0)
            upd = jnp.zeros((8, w), F32)
            for kk, rv in enumerate(rows):
                upd = jnp.where(sub == kk, rv, upd)

            @pl.when(r == 0)
            def _():
                dw_ref[...] = jnp.zeros_like(dw_ref)

            dw_ref[...] += upd[None]

        branch(da * ext(ac_ref, an_ref), wu_ref, hu_ref, dhu_ref, dwu_ref)
        branch(da * ext(bc_ref, bn_ref), wg_ref, hg_ref, dhg_ref, dwg_ref)

    cur = pl.BlockSpec((ts, w), lambda j, r: (r, j))
    nxt = pl.BlockSpec((hb, w), lambda j, r: (jnp.minimum((r + 1) * (ts // hb), nb * s // hb - 1), j))

    def wspec(off):
        return pl.BlockSpec((1, 8, w), lambda j, r: (j + off, 0, 0))

    p = FF_PAIRS
    dw_spec = pl.BlockSpec((1, 8, w), lambda j, r: (j, 0, 0))
    return pl.pallas_call(
        body, name=name, grid=(p, nb * nq),
        in_specs=[cur, nxt, cur, nxt, cur, nxt, cur, cur, wspec(0), wspec(p)],
        out_specs=[cur, cur, dw_spec, dw_spec],
        out_shape=[SDS((nb * s, p * w), BF16), SDS((nb * s, p * w), BF16), SDS((p, 8, w), F32), SDS((p, 8, w), F32)],
        compiler_params=_params(("parallel", "arbitrary")))(dact, dact, a, a, b, b, hu, hg, cw, cw)


def _tri(n, upper):
    r = lax.broadcasted_iota(jnp.int32, (n, n), 0)
    c = lax.broadcasted_iota(jnp.int32, (n, n), 1)
    return ((r <= c) if upper else (r >= c)).astype(F32)


def fgate_fwd(fl, fb, nb, s):
    tc = min(TC, s)
    nq = s // tc

    def body(fl_ref, fb_ref, f_ref, carry):
        @pl.when(pl.program_id(1) == 0)
        def _():
            carry[...] = jnp.zeros_like(carry)

        z = fl_ref[...] + fb_ref[...]
        logf = jnp.minimum(z, 0.0) - jnp.log(1.0 + jnp.exp(-jnp.abs(z)))
        f_ref[...] = jnp.dot(_tri(tc, False), logf, preferred_element_type=F32,
                             precision=lax.Precision.HIGHEST) + carry[...]
        carry[...] += jnp.sum(logf, axis=0, keepdims=True)

    row = pl.BlockSpec((tc, LANES), lambda b, i: (b * nq + i, 0))
    return pl.pallas_call(
        body, name="fgate_fwd", grid=(nb, nq), in_specs=[row, pl.BlockSpec((1, LANES), lambda b, i: (0, 0))],
        out_specs=row, out_shape=SDS((nb * s, LANES), F32), scratch_shapes=[pltpu.VMEM((1, LANES), F32)],
        compiler_params=_params(("arbitrary", "arbitrary")))(fl, fb)


def fgate_bwd(d_cum_q, d_cum_k, fl, fb, nb, s):
    tc = min(TC, s)
    nq = s // tc

    def body(dfq_ref, dfk_ref, fl_ref, fb_ref, dfl_ref, dfb_ref, carry):
        b = pl.program_id(0)
        i = pl.program_id(1)

        @pl.when(i == 0)
        def _():
            carry[...] = jnp.zeros_like(carry)

        @pl.when(jnp.logical_and(b == 0, i == 0))
        def _():
            dfb_ref[...] = jnp.zeros_like(dfb_ref)

        dfv = dfq_ref[...] + dfk_ref[...]
        dlog = jnp.dot(_tri(tc, True), dfv, preferred_element_type=F32,
                       precision=lax.Precision.HIGHEST) + carry[...]
        carry[...] += jnp.sum(dfv, axis=0, keepdims=True)
        z = fl_ref[...] + fb_ref[...]
        dfl = dlog / (1.0 + jnp.exp(z))
        dfl_ref[...] = dfl
        dfb_ref[...] += jnp.sum(dfl, axis=0, keepdims=True)

    row = pl.BlockSpec((tc, LANES), lambda b, i: (b * nq + nq - 1 - i, 0))
    vec = pl.BlockSpec((1, LANES), lambda b, i: (0, 0))
    return pl.pallas_call(
        body, name="fgate_bwd", grid=(nb, nq), in_specs=[row, row, row, vec], out_specs=[row, vec],
        out_shape=[SDS((nb * s, LANES), F32), SDS((1, LANES), F32)], scratch_shapes=[pltpu.VMEM((1, LANES), F32)],
        compiler_params=_params(("arbitrary", "arbitrary")))(d_cum_q, d_cum_k, fl, fb)


PAIR = 2 * HEAD_DIM
N_PAIRS = FOX_HEADS // 2


def _lane_put(shape, h, col):
    lane = lax.broadcasted_iota(jnp.int32, shape, 1)
    return jnp.where(lane == h, col, 0.0)


def _half_masks(rows):
    lane = lax.broadcasted_iota(jnp.int32, (rows, PAIR), 1)
    return lane < HEAD_DIM


def _split_pair(x, scale=None):
    if scale is not None:
        x = x * scale
    lo = _half_masks(x.shape[0])
    zero = jnp.zeros_like(x)
    return jnp.where(lo, x, zero), jnp.where(lo, zero, x)


def _to_tile_rows(a, nb, s, tf):
    return a.reshape(nb * s // tf, tf, LANES)[:, :, :16].transpose(0, 2, 1)


def _from_tile_rows(a):
    tiles, _, tf = a.shape
    return jnp.pad(a.transpose(0, 2, 1), ((0, 0), (0, 0), (0, LANES - 16))).reshape(tiles * tf, LANES)


BIAS_TERMS = 3
LOOKAHEAD = 4
FOLLOW_FWD = 1
LOOKAHEAD_BWD = 2
FOLLOW_BWD = 1


def _bias_lane(h):
    return HEAD_DIM if h % 2 == 0 else 0


def _placement():
    rows = jnp.arange(LANES)[:, None]
    cols = jnp.arange(FOX_HEADS * PAIR)[None, :]
    head, lane = cols // PAIR, cols % PAIR
    first = jnp.where(head % 2 == 0, HEAD_DIM, 0)
    term = lane - first
    hit = (term >= 0) & (term < BIAS_TERMS) & (rows == 16 * term + head)
    return hit.astype(BF16)


def fox_prep(kv, fneg, nb, s):
    tf = min(TF, s)
    w = TOK_WIDTH

    def body(k_ref, v_ref, f_ref, pl_ref, ka_ref, vt_ref):
        lane = lax.broadcasted_iota(jnp.int32, (tf, LANES), 1)
        lo = lane < HEAD_DIM
        f = jnp.where(lane < FOX_HEADS, f_ref[...], 0.0)
        hi = f.astype(BF16).astype(F32)
        mid = (f - hi).astype(BF16).astype(F32)
        low = (f - hi - mid).astype(BF16).astype(F32)
        terms = (hi + pltpu.roll(mid, 16, axis=1) + pltpu.roll(low, 32, axis=1)).astype(BF16)
        placed = jnp.dot(terms, pl_ref[...], preferred_element_type=F32).astype(BF16)
        one = jnp.ones((tf, LANES), BF16)
        zero = jnp.zeros((tf, LANES), BF16)
        for p in range(N_PAIRS):
            kp = k_ref[:, p * PAIR:(p + 1) * PAIR] * QK_SCALE
            vp = v_ref[:, p * PAIR:(p + 1) * PAIR]
            he, ho = 2 * p, 2 * p + 1
            ka_ref[:, he * PAIR:(he + 1) * PAIR] = jnp.where(lo, kp, placed[:, he * PAIR:(he + 1) * PAIR])
            ka_ref[:, ho * PAIR:(ho + 1) * PAIR] = jnp.where(lo, placed[:, ho * PAIR:(ho + 1) * PAIR], kp)
            ve = jnp.where(lo, vp, jnp.where(lane == HEAD_DIM, one, zero))
            vo = jnp.where(lo, jnp.where(lane == 0, one, zero), vp)
            vt_ref[0, he * PAIR:(he + 1) * PAIR, :] = ve.astype(F32).T.astype(BF16)
            vt_ref[0, ho * PAIR:(ho + 1) * PAIR, :] = vo.astype(F32).T.astype(BF16)

    return pl.pallas_call(
        body, name="fox_prep", grid=(nb * s // tf,),
        in_specs=[pl.BlockSpec((tf, w), lambda r: (r, 0)), pl.BlockSpec((tf, w), lambda r: (r, 1)),
                  pl.BlockSpec((tf, LANES), lambda r: (r, 0)), pl.BlockSpec((LANES, FOX_HEADS * PAIR), lambda r: (0, 0))],
        out_specs=[pl.BlockSpec((tf, FOX_HEADS * PAIR), lambda r: (r, 0)),
                   pl.BlockSpec((1, FOX_HEADS * PAIR, tf), lambda r: (r, 0, 0))],
        out_shape=[SDS((nb * s, FOX_HEADS * PAIR), BF16), SDS((nb * s // tf, FOX_HEADS * PAIR, tf), BF16)],
        compiler_params=_params(("parallel",)))(kv, kv, fneg, _placement())


def fox_fwd_t(pq, kaug, vaug_t, nb, s):
    tf = min(TF, s)
    n = s // tf
    w = TOK_WIDTH
    wa = FOX_HEADS * PAIR

    def body(q_ref, k_hbm, vt_hbm, ob_ref, of_ref, lse_ref, k_vm, vt_vm, qx_scr, m_scr, acc_scr, sems):
        b = pl.program_id(0)
        i = pl.program_id(1)

        @pl.when(i == 0)
        def _():
            ck = pltpu.make_async_copy(k_hbm.at[pl.ds(pl.multiple_of(b * s, tf), s)], k_vm, sems.at[0])
            cv = pltpu.make_async_copy(vt_hbm.at[pl.ds(b * n, n)], vt_vm, sems.at[1])
            ck.start()
            cv.start()
            ck.wait()
            cv.wait()

        lane = lax.broadcasted_iota(jnp.int32, (tf, PAIR), 1)
        one = jnp.ones((tf, PAIR), BF16)
        zero = jnp.zeros((tf, PAIR), BF16)
        for p in range(N_PAIRS):
            qp = q_ref[:, p * PAIR:(p + 1) * PAIR]
            be, bo = _bias_lane(2 * p), _bias_lane(2 * p + 1)
            ones_e = jnp.where((lane >= be) & (lane < be + BIAS_TERMS), one, zero)
            ones_o = jnp.where((lane >= bo) & (lane < bo + BIAS_TERMS), one, zero)
            qx_scr[2 * p] = jnp.where(lane < HEAD_DIM, qp, ones_e)
            qx_scr[2 * p + 1] = jnp.where(lane < HEAD_DIM, ones_o, qp)
        m_scr[...] = jnp.full(m_scr.shape, NEG_BIG, F32)
        acc_scr[...] = jnp.zeros_like(acc_scr)

        def tile(j, masked):
            ks = pl.multiple_of(j * tf, tf)
            if masked:
                keep = lax.broadcasted_iota(jnp.int32, (tf, tf), 1) >= lax.broadcasted_iota(jnp.int32, (tf, tf), 0)
            def scores(h):
                kx = k_vm[pl.ds(ks, tf), h * PAIR:(h + 1) * PAIR]
                return lax.dot_general(kx, qx_scr[h], NT_DIMS, preferred_element_type=F32)

            def values(h, pr, a):
                pv = jnp.dot(vt_vm[j, h * PAIR:(h + 1) * PAIR, :], pr, preferred_element_type=F32)
                acc_scr[h] = a * acc_scr[h] + pv

            ahead = [scores(h) for h in range(LOOKAHEAD)]
            behind = []
            for h in range(FOX_HEADS):
                sc = ahead.pop(0)
                if h + LOOKAHEAD < FOX_HEADS:
                    ahead.append(scores(h + LOOKAHEAD))
                if masked:
                    sc = jnp.where(keep, sc, NEG_BIG)
                m_prev = m_scr[h]
                m_new = jnp.maximum(m_prev, jnp.max(sc, axis=0, keepdims=True))
                m_scr[h] = m_new
                behind.append((h, jnp.exp(sc - m_new).astype(BF16), jnp.exp(m_prev - m_new)))
                if len(behind) > FOLLOW_FWD:
                    values(*behind.pop(0))
            for item in behind:
                values(*item)

        def step(j, carry):
            tile(j, False)
            return carry

        lax.fori_loop(0, i, step, 0)
        tile(i, True)

        top = lax.broadcasted_iota(jnp.int32, (PAIR, tf), 0) < HEAD_DIM
        sub = lax.broadcasted_iota(jnp.int32, (16, tf), 0)
        lse = jnp.zeros((16, tf), F32)
        for p in range(N_PAIRS):
            he, ho = 2 * p, 2 * p + 1
            le = acc_scr[he, HEAD_DIM:HEAD_DIM + 1, :]
            lod = acc_scr[ho, 0:1, :]
            o = jnp.where(top, acc_scr[he] / le, acc_scr[ho] / lod).T
            ob_ref[:, p * PAIR:(p + 1) * PAIR] = o.astype(BF16)
            of_ref[:, p * PAIR:(p + 1) * PAIR] = o
            lse = jnp.where(sub == he, m_scr[he] + jnp.log(le), lse)
            lse = jnp.where(sub == ho, m_scr[ho] + jnp.log(lod), lse)
        lse_ref[0] = lse

    qrow = lambda b, i: (b * n + i, 0)
    return pl.pallas_call(
        body, name="fox_fwd", grid=(nb, n),
        in_specs=[pl.BlockSpec((tf, w), qrow), ANY_SPEC, ANY_SPEC],
        out_specs=[pl.BlockSpec((tf, w), qrow), pl.BlockSpec((tf, w), qrow),
                   pl.BlockSpec((1, 16, tf), lambda b, i: (b * n + i, 0, 0))],
        out_shape=[SDS((nb * s, w), BF16), SDS((nb * s, w), F32), SDS((nb * n, 16, tf), F32)],
        scratch_shapes=[pltpu.VMEM((s, wa), BF16), pltpu.VMEM((n, wa, tf), BF16),
                        pltpu.VMEM((FOX_HEADS, tf, PAIR), BF16), pltpu.VMEM((FOX_HEADS, 1, tf), F32),
                        pltpu.VMEM((FOX_HEADS, PAIR, tf), F32), pltpu.SemaphoreType.DMA((2,))],
        compiler_params=_params(("arbitrary", "arbitrary")))(pq, kaug, vaug_t)


def fox_delta(dcat, o, nb, s):
    tf = min(TM, s)
    w = TOK_WIDTH

    def body(do_ref, o_ref, dl_ref):
        out = jnp.zeros((tf, LANES), F32)
        for h in range(FOX_HEADS):
            lo, hi = h * HEAD_DIM, (h + 1) * HEAD_DIM
            out = out + _lane_put((tf, LANES), h, jnp.sum(do_ref[:, lo:hi] * o_ref[:, lo:hi], axis=1, keepdims=True))
        dl_ref[...] = out

    row = pl.BlockSpec((tf, w), lambda r: (r, 0))
    return pl.pallas_call(
        body, name="fox_delta", grid=(nb * s // tf,), in_specs=[row, row],
        out_specs=pl.BlockSpec((tf, LANES), lambda r: (r, 0)), out_shape=SDS((nb * s, LANES), F32),
        compiler_params=_params(("parallel",)))(dcat, o)


def fox_bwd(pq, kv, fneg, dcat_bf, lse_rows, delta_rows, nb, s):
    tf = min(TF, s)
    n = s // tf
    w = TOK_WIDTH

    def body(q_hbm, k_ref, v_ref, f_ref, do_hbm, lse_ref, dl_ref, dq_ref, dk_ref, dv_ref, dfk_ref, dfq_ref,
             q_vm, do_vm, km_scr, vm_scr, kt_scr, fk_scr, dk_scr, dv_scr, rs_scr, dq_scr, fq_scr, sems):
        b = pl.program_id(0)
        j = pl.program_id(1)

        @pl.when(j == 0)
        def _():
            rows = pl.ds(pl.multiple_of(b * s, tf), s)
            cq = pltpu.make_async_copy(q_hbm.at[rows, pl.ds(0, w)], q_vm, sems.at[0])
            cd = pltpu.make_async_copy(do_hbm.at[rows, pl.ds(0, w)], do_vm, sems.at[1])
            cq.start()
            cd.start()
            dq_scr[...] = jnp.zeros_like(dq_scr)
            fq_scr[...] = jnp.zeros_like(fq_scr)
            cq.wait()
            cd.wait()

        for p in range(N_PAIRS):
            kp = k_ref[:, p * PAIR:(p + 1) * PAIR] * QK_SCALE
            ke, ko = _split_pair(kp)
            km_scr[2 * p] = ke
            km_scr[2 * p + 1] = ko
            kt_scr[p] = kp.astype(F32).T.astype(BF16)
            ve, vo = _split_pair(v_ref[:, p * PAIR:(p + 1) * PAIR])
            vm_scr[2 * p] = ve
            vm_scr[2 * p + 1] = vo
        for h in range(FOX_HEADS):
            fk_scr[h] = jnp.broadcast_to(f_ref[:, h:h + 1], (tf, tf))
        dk_scr[...] = jnp.zeros_like(dk_scr)
        dv_scr[...] = jnp.zeros_like(dv_scr)
        rs_scr[...] = jnp.zeros_like(rs_scr)

        def tile(i, masked):
            qs = pl.multiple_of(i * tf, tf)
            if masked:
                keep = lax.broadcasted_iota(jnp.int32, (tf, tf), 1) >= lax.broadcasted_iota(jnp.int32, (tf, tf), 0)
            def products(h):
                qp = q_vm[pl.ds(qs, tf), (h // 2) * PAIR:(h // 2 + 1) * PAIR]
                dop = do_vm[pl.ds(qs, tf), (h // 2) * PAIR:(h // 2 + 1) * PAIR]
                return (lax.dot_general(km_scr[h], qp, NT_DIMS, preferred_element_type=F32),
                        lax.dot_general(vm_scr[h], dop, NT_DIMS, preferred_element_type=F32))

            def dependents(h, prb, dsb):
                p = h // 2
                half = slice((h % 2) * HEAD_DIM, (h % 2 + 1) * HEAD_DIM)
                qp = q_vm[pl.ds(qs, tf), p * PAIR:(p + 1) * PAIR]
                dop = do_vm[pl.ds(qs, tf), p * PAIR:(p + 1) * PAIR]
                dv_scr[h] += jnp.dot(prb, dop, preferred_element_type=F32)
                dk_scr[h] += jnp.dot(dsb, qp, preferred_element_type=F32)
                dqt = jnp.dot(kt_scr[p], dsb, preferred_element_type=F32)
                dq_scr[i, p, half, :] += dqt[(h % 2) * HEAD_DIM:(h % 2 + 1) * HEAD_DIM]

            ahead = [products(h) for h in range(LOOKAHEAD_BWD)]
            behind = []
            for h in range(FOX_HEADS):
                sc, dp = ahead.pop(0)
                if h + LOOKAHEAD_BWD < FOX_HEADS:
                    ahead.append(products(h + LOOKAHEAD_BWD))
                sc = sc + fk_scr[h] - lse_ref[i, h:h + 1, :]
                if masked:
                    sc = jnp.where(keep, sc, NEG_BIG)
                pr = jnp.exp(sc)
                ds = pr * (dp - dl_ref[i, h:h + 1, :])
                part = ds[:, :LANES]
                for c in range(1, tf // LANES):
                    part = part + ds[:, c * LANES:(c + 1) * LANES]
                rs_scr[h] += part
                fq_scr[i, h:h + 1, :] += jnp.sum(ds, axis=0, keepdims=True)
                behind.append((h, pr.astype(BF16), ds.astype(BF16)))
                if len(behind) > FOLLOW_BWD:
                    dependents(*behind.pop(0))
            for item in behind:
                dependents(*item)

        def step(i, carry):
            tile(i, False)
            return carry

        tile(j, True)
        for p in range(N_PAIRS):
            dq_ref[:, p * PAIR:(p + 1) * PAIR] = dq_scr[j, p].T.astype(BF16)
        dfq_ref[0] = fq_scr[j]
        lax.fori_loop(j + 1, n, step, 0)

        lo = _half_masks(tf)
        dfk = jnp.zeros((tf, LANES), F32)
        for p in range(N_PAIRS):
            dk = jnp.where(lo, dk_scr[2 * p], dk_scr[2 * p + 1]) * QK_SCALE
            dk_ref[:, p * PAIR:(p + 1) * PAIR] = dk.astype(BF16)
            dv_ref[:, p * PAIR:(p + 1) * PAIR] = jnp.where(lo, dv_scr[2 * p], dv_scr[2 * p + 1]).astype(BF16)
            for h in (2 * p, 2 * p + 1):
                dfk = dfk - _lane_put((tf, LANES), h, jnp.sum(rs_scr[h], axis=1, keepdims=True))
        dfk_ref[...] = dfk

    krow = lambda b, j: (b * n + j, 0)
    rows = pl.BlockSpec((n, 16, tf), lambda b, j: (b, 0, 0))
    tile_out = pl.BlockSpec((tf, w), krow)
    return pl.pallas_call(
        body, name="fox_bwd", grid=(nb, n),
        in_specs=[ANY_SPEC, pl.BlockSpec((tf, w), krow), pl.BlockSpec((tf, w), lambda b, j: (b * n + j, 1)),
                  pl.BlockSpec((tf, LANES), krow), ANY_SPEC, rows, rows],
        out_specs=[tile_out, tile_out, tile_out, pl.BlockSpec((tf, LANES), krow),
                   pl.BlockSpec((1, 16, tf), lambda b, j: (b * n + j, 0, 0))],
        out_shape=[SDS((nb * s, w), BF16), SDS((nb * s, w), BF16), SDS((nb * s, w), BF16), SDS((nb * s, LANES), F32),
                   SDS((nb * n, 16, tf), F32)],
        scratch_shapes=[pltpu.VMEM((s, w), BF16), pltpu.VMEM((s, w), BF16),
                        pltpu.VMEM((FOX_HEADS, tf, PAIR), BF16), pltpu.VMEM((FOX_HEADS, tf, PAIR), BF16),
                        pltpu.VMEM((N_PAIRS, PAIR, tf), BF16), pltpu.VMEM((FOX_HEADS, tf, tf), F32),
                        pltpu.VMEM((FOX_HEADS, tf, PAIR), F32), pltpu.VMEM((FOX_HEADS, tf, PAIR), F32),
                        pltpu.VMEM((FOX_HEADS, tf, LANES), F32), pltpu.VMEM((n, N_PAIRS, PAIR, tf), F32),
                        pltpu.VMEM((n, 16, tf), F32), pltpu.SemaphoreType.DMA((2,))],
        compiler_params=_params(("arbitrary", "arbitrary")))(pq, kv, kv, fneg, dcat_bf, lse_rows, delta_rows)


ADAMW_TILE_ELEMS = 128 * 1024


def reduce_adamw(parts, w, m, v, name):
    layers, r, c = w.shape
    tr = r
    for cand in range(16, r, 16):
        if r % cand == 0 and cand * c <= ADAMW_TILE_ELEMS:
            tr = cand
    c1 = 1.0 - ADAM_B1 ** ADAM_STEP
    c2 = 1.0 - ADAM_B2 ** ADAM_STEP

    def body(*refs):
        p_refs = refs[:layers]
        w_ref, m_ref, v_ref, g_out, d_out, m_out, v_out = refs[layers:]

        def update(p_ref):
            g = p_ref[0].astype(F32)
            for k in range(1, N_DEV):
                g = g + p_ref[k].astype(F32)
            mn = ADAM_B1 * m_ref[0] + (1.0 - ADAM_B1) * g
            vn = ADAM_B2 * v_ref[0] + (1.0 - ADAM_B2) * (g * g)
            g_out[0] = g
            m_out[0] = mn
            v_out[0] = vn
            d_out[0] = -ADAM_LR * ((mn / c1) / (jnp.sqrt(vn / c2) + ADAM_EPS) + ADAM_WD * w_ref[0])

        if layers == 1:
            update(p_refs[0])
        else:
            for layer in range(layers):
                pl.when(pl.program_id(0) == layer)(lambda layer=layer: update(p_refs[layer]))

    row = pl.BlockSpec((1, tr, c), lambda l, i: (l, i, 0))
    return pl.pallas_call(
        body, name=name, grid=(layers, r // tr),
        in_specs=[pl.BlockSpec((N_DEV, tr, c), lambda l, i: (0, i, 0))] * layers + [row, row, row],
        out_specs=[row, row, row, row], out_shape=[SDS((layers, r, c), F32)] * 4,
        compiler_params=_params(("parallel", "parallel")))(*parts, w, m, v)


N_PEERS = N_DEV - 1
HBM_SPEC = pl.BlockSpec(memory_space=pltpu.HBM)
SEM_SPEC = pl.BlockSpec(memory_space=pltpu.SEMAPHORE)
ANY_SPEC = pl.BlockSpec(memory_space=pl.ANY)
SPLIT_EFFECT = pltpu.SideEffectType.DATAFLOW_SIDE_EFFECTING


def _peers(with_self=False):
    x, y, c = lax.axis_index("x"), lax.axis_index("y"), lax.axis_index("c")
    peers = []
    for k in range(0 if with_self else 1, N_DEV):
        px = 1 - x if (k >> 2) & 1 else x
        py = 1 - y if (k >> 1) & 1 else y
        pc = 1 - c if k & 1 else c
        peers.append(((px, py, pc), 4 * px + 2 * py + pc))
    return 4 * x + 2 * y + c, peers


def _push(src, dst, send_sems, recv_sems, slot, dev):
    return pltpu.make_async_remote_copy(src_ref=src, dst_ref=dst, send_sem=send_sems.at[slot], recv_sem=recv_sems.at[slot],
                                        device_id=dev, device_id_type=pl.DeviceIdType.MESH)


def _landing_shapes(arrs, scatter):
    return [SDS((N_DEV,) + tuple(a.shape[1:] if sc else a.shape), a.dtype) for a, sc in zip(arrs, scatter)]


def exchange(arrs, scatter, name):
    na = len(arrs)

    def body(*refs):
        ins = refs[:na]
        outs = refs[na:2 * na]
        send_sems, recv_sems, local_sems = refs[2 * na:]
        me, peers = _peers()
        local = []
        remote = []
        for a in range(na):
            lc = pltpu.make_async_copy(ins[a].at[me] if scatter[a] else ins[a], outs[a].at[me], local_sems.at[a])
            lc.start()
            local.append(lc)
            for k, (dev, idx) in enumerate(peers):
                cp = _push(ins[a].at[idx] if scatter[a] else ins[a], outs[a].at[me], send_sems, recv_sems,
                           a * N_PEERS + k, dev)
                cp.start()
                remote.append(cp)
        for a in range(na):
            for k, (dev, idx) in enumerate(peers):
                _push(ins[a].at[me] if scatter[a] else ins[a], outs[a].at[idx], send_sems, recv_sems,
                      a * N_PEERS + k, dev).wait_recv()
        for cp in remote:
            cp.wait_send()
        for lc in local:
            lc.wait()

    return pl.pallas_call(
        body, name=name, in_specs=[HBM_SPEC] * na, out_specs=[HBM_SPEC] * na, out_shape=_landing_shapes(arrs, scatter),
        scratch_shapes=[pltpu.SemaphoreType.DMA((na * N_PEERS,)), pltpu.SemaphoreType.DMA((na * N_PEERS,)),
                        pltpu.SemaphoreType.DMA((na,))])(*arrs)


def exchange_start(arrs, scatter, after, name):
    na = len(arrs)
    lands = [lax.empty(l.shape, l.dtype) for l in _landing_shapes(arrs, scatter)]

    def body(*refs):
        ins = refs[:na]
        land = refs[na:2 * na]
        send_sems, recv_sems = refs[2 * na + 1], refs[2 * na + 2]
        token = refs[-1]
        me, peers = _peers(with_self=True)
        for a in range(na):
            for k, (dev, idx) in enumerate(peers):
                _push(ins[a].at[idx] if scatter[a] else ins[a], land[a].at[me], send_sems, recv_sems,
                      a * N_DEV + k, dev).start()
        token[...] = jnp.zeros_like(token)

    thru = [pltpu.HBM(a.shape, a.dtype) for a in arrs] + [pltpu.HBM(l.shape, l.dtype) for l in lands]
    res = pl.pallas_call(
        body, name=name,
        out_shape=(pltpu.SemaphoreType.DMA((na * N_DEV,)), pltpu.SemaphoreType.DMA((na * N_DEV,)), *thru,
                   SDS((8, LANES), F32)),
        in_specs=[HBM_SPEC] * (2 * na) + [ANY_SPEC],
        out_specs=(SEM_SPEC, SEM_SPEC, *([HBM_SPEC] * (2 * na)), pl.BlockSpec(memory_space=pltpu.VMEM)),
        input_output_aliases={i: 2 + i for i in range(2 * na)},
        compiler_params=pltpu.CompilerParams(has_side_effects=SPLIT_EFFECT),
    )(*[pltpu.with_memory_space_constraint(a, pltpu.HBM) for a in arrs],
      *[pltpu.with_memory_space_constraint(l, pltpu.HBM) for l in lands], after)
    return {"send": res[0], "recv": res[1], "src": res[2:2 + na], "land": res[2 + na:2 + 2 * na],
            "token": res[-1][0, 0], "scatter": scatter}


def exchange_wait(handle, after, name):
    scatter = handle["scatter"]
    na = len(scatter)

    def body(*refs):
        src = refs[:na]
        land = refs[na:2 * na]
        send_sems, recv_sems = refs[2 * na], refs[2 * na + 1]
        me, peers = _peers(with_self=True)
        for a in range(na):
            for k, (dev, idx) in enumerate(peers):
                cp = _push(src[a].at[me] if scatter[a] else src[a], land[a].at[idx], send_sems, recv_sems,
                           a * N_DEV + k, dev)
                cp.wait_send()
                cp.wait_recv()

    ops = list(handle["src"]) + list(handle["land"])
    res = pl.pallas_call(
        body, name=name, out_shape=tuple(pltpu.HBM(o.shape, o.dtype) for o in ops),
        in_specs=[HBM_SPEC] * (2 * na) + [SEM_SPEC, SEM_SPEC, ANY_SPEC], out_specs=tuple([HBM_SPEC] * (2 * na)),
        input_output_aliases={i: i for i in range(2 * na)},
        compiler_params=pltpu.CompilerParams(has_side_effects=SPLIT_EFFECT),
    )(*ops, handle["send"], handle["recv"], after)
    return list(res[na:])


def forward_layer(l, xin, xin_bf, mem_bf, wt, nb, s, ffn_weights=None):
    sv = {"xin_bf": xin_bf}
    memkv = mm_nn(mem_bf, wt["memw"], BF16, f"memkv{l}")
    sv["memkv"] = memkv
    if l == 0:
        proj = mm_nn(xin_bf, wt["win_a"], F32, "proj_a")
        pooled, tok = pool_fwd(proj, wt["pw_bd"], wt["pscale"], nb, s)
        sv["pooled"] = pooled
    else:
        kv = mm_nn(xin_bf, wt["kvw"][:, :2 * TOK_WIDTH], BF16, "kv_proj")
        fl = mm_nn(xin_bf, wt["kvw"][:, 2 * TOK_WIDTH:], F32, "gate_proj")
        fneg = -fgate_fwd(fl, wt["fb"], nb, s)
        proj = mm_nn(xin_bf, wt["wq"], BF16, "proj_b")
        kaug, vaug_t = fox_prep(kv, fneg, nb, s)
        tok, o_f32, lse_rows = fox_fwd_t(proj, kaug, vaug_t, nb, s)
        sv.update(kv=kv, fl=fl, fneg=fneg, o_f32=o_f32, lse_rows=lse_rows)
    sv["proj"] = proj
    mem_out = memattn_fwd(proj, memkv, nb, s, f"memattn_fwd{l}")
    cat = jnp.concatenate([tok, mem_out], axis=1)
    sv["cat"] = cat
    x1, x1_bf, xh1, rs1 = ln_fwd(xin, cat, wt["wout"], wt["ln1_g"], wt["ln1_b"], f"out_proj_ln1_{l}")
    sv.update(x1_bf=x1_bf, xh1=xh1, rs1=rs1)
    if ffn_weights is not None:
        wt.update(ffn_weights(x1_bf))
    act, ga, gb, hu, hg = ffn_up_gate(x1_bf, wt["wup"], wt["cw"], nb, s, f"ffn_up_gate{l}")
    sv.update(act=act, ga=ga, gb=gb, hu=hu, hg=hg)
    x2, x2_bf, xh2, rs2 = ln_fwd(x1, act, wt["wdown"], wt["ln2_g"], wt["ln2_b"], f"ffn_down_ln2_{l}")
    sv.update(xh2=xh2, rs2=rs2)
    return x2, x2_bf, sv


def backward_layer(l, dy, sv, mem_bf, wt, nb, s, after_ffn=None, after_pool=None, loss_target=None):
    g = {}
    if loss_target is None:
        dr2, dr2_bf, g["ln2_g"], g["ln2_b"] = ln_bwd(dy[0], sv["xh2"], sv["rs2"], wt["ln2_g"], f"ln2_bwd{l}",
                                                     dy_scale=dy[1], products=dy[2])
    else:
        dr2, dr2_bf, g["ln2_g"], g["ln2_b"], g["loss_row"] = loss_ln_bwd(sv["xh2"], sv["rs2"], wt["ln2_g"], wt["ln2_b"],
                                                                         loss_target, f"loss_ln2_bwd{l}")
    dact = mm_nn(dr2_bf, wt["wdown"], BF16, f"ffn_down_dx{l}", trans_b=0)
    g["wdown"] = mm_tn(sv["act"], dr2_bf, f"ffn_down_dw{l}")
    dh_u, dh_g, dcw_u, dcw_g = gate_conv_bwd(dact, sv["ga"], sv["gb"], sv["hu"], sv["hg"], wt["cw"], nb, s,
                                             f"gate_conv_bwd{l}")
    g["cw"] = jnp.concatenate([dcw_u, dcw_g], axis=0)
    g["wup"] = jnp.concatenate([mm_tn(sv["x1_bf"], dh_u, f"ffn_up_dw_u{l}", blocked=True),
                                mm_tn(sv["x1_bf"], dh_g, f"ffn_up_dw_g{l}", blocked=True)], axis=0)
    ln1_g = wt["ln1_g"] if after_ffn is None else wt["ln1_g"] + after_ffn(g, dr2)
    dr1, dr1_bf, g["ln1_g"], g["ln1_b"] = ln_bwd(dr2, sv["xh1"], sv["rs1"], ln1_g, f"ffn_up_dx_ln1_bwd{l}",
                                                 dy_scale=DN_ALPHA, products=[(dh_u, wt["wup"], 0), (dh_g, wt["wup"], 1)])
    dcat, dcat_bf = mm_nn(dr1_bf, wt["wout"], F32, f"out_proj_dx{l}", also_bf16=True, trans_b=0)
    g["wout"] = mm_tn(sv["cat"], dr1_bf, f"out_proj_dw{l}")
    dqm, dmemkv = memattn_bwd(sv["proj"], sv["memkv"], dcat, nb, s, f"memattn_bwd{l}")
    g["memw"] = mm_tn(mem_bf, dmemkv, f"memkv_dw{l}")
    if l == 0:
        dmixed, dpooled, g["pscale"] = pool_bwd_mix(dcat, sv["pooled"], wt["pw_bd"], wt["pscale"], nb, s)
        g["pw_full"] = mm_tn(sv["pooled"], dmixed, "pool_dw")
        win_a = wt["win_a"] if after_pool is None else wt["win_a"] + after_pool(g, dmixed).astype(BF16)
        du = pool_bwd_window(dpooled, nb, s)
        dproj = jnp.concatenate([du, dqm], axis=1)
        dx = mm_nn(dproj, win_a, F32, "proj_a_dx", addend=dr1, add_scale=DN_ALPHA, trans_b=0)
        g["win_a"] = mm_tn(sv["xin_bf"], dproj, "proj_a_dw")
    else:
        delta = fox_delta(dcat, sv["o_f32"], nb, s)
        tf = min(TF, s)
        dq, dk, dv, dfcum_k, dfq_rows = fox_bwd(sv["proj"], sv["kv"], sv["fneg"], dcat_bf,
                                                sv["lse_rows"], _to_tile_rows(delta, nb, s, tf), nb, s)
        dfl, g["fb"] = fgate_bwd(_from_tile_rows(dfq_rows), dfcum_k, sv["fl"], wt["fb"], nb, s)
        dproj = jnp.concatenate([dq, dqm], axis=1)
        dkvf = jnp.concatenate([dk, dv, dfl.astype(BF16)], axis=1)
        dx = (dr1, DN_ALPHA, [(dproj, wt["wq"], 0), (dkvf, wt["kvw"], 0)])
        g["wq"] = mm_tn(sv["xin_bf"], dproj, "proj_b_dw")
        g["kvw"] = mm_tn(sv["xin_bf"], dkvf, "kv_proj_dw")
    return dx, g


def pack_replicated(pool_w, ln1_g, ln1_b, ln2_g, ln2_b, conv_b, f_b):
    cb = jnp.pad(conv_b, ((0, 0), (0, 6144 - 5504))).reshape(12, D_MODEL)
    fb = jnp.pad(f_b.reshape(1, FOX_HEADS), ((0, 3), (0, D_MODEL - FOX_HEADS)))
    return jnp.concatenate([pool_w.reshape(144, D_MODEL), ln1_g, ln1_b, ln2_g, ln2_b, cb, fb], axis=0)


def unpack_replicated(buf):
    pool_w = buf[:144].reshape(1, 4, POOL_GROUP, POOL_GROUP)
    ln = [buf[144 + 2 * k:146 + 2 * k] for k in range(4)]
    conv_b = buf[152:164].reshape(2, 6144)[:, :5504]
    f_b = buf[164, :FOX_HEADS]
    return pool_w, ln[0], ln[1], ln[2], ln[3], conv_b, f_b


def _pad_ff(a, axis):
    shape = a.shape
    a = a.reshape(shape[:axis] + (2, FF_ROWS) + shape[axis + 1:])
    pads = [(0, 0)] * a.ndim
    pads[axis + 1] = (0, FF_ROWS_PAD - FF_ROWS)
    return jnp.pad(a, pads).reshape(shape[:axis] + (FF_BLOCK_PAD,) + shape[axis + 1:])


def _unpad_ff(a, axis):
    shape = a.shape
    a = a.reshape(shape[:axis] + (2, FF_ROWS_PAD) + shape[axis + 1:])
    return lax.slice_in_dim(a, 0, FF_ROWS, axis=axis + 1).reshape(shape[:axis] + (FF_BLOCK,) + shape[axis + 1:])


def pack_small(conv_w, pool_scale):
    buf = jnp.zeros((16, FF_BLOCK_PAD), F32)
    buf = lax.dynamic_update_slice(buf, _pad_ff(conv_w.reshape(DEPTH * 3, FF_BLOCK), 1), (0, 0))
    return lax.dynamic_update_slice(buf, pool_scale, (8, 0))


def _block_diag(pw):
    out = jnp.zeros((TOK_WIDTH, TOK_WIDTH), pw.dtype)
    for g in range(4):
        out = lax.dynamic_update_slice(out, pw[g], (g * POOL_GROUP, g * POOL_GROUP))
    return out


def layer_shards(l, sq_a, sq_b, mem_w_kv, ffn_w_up, ffn_w_down):
    wdown = jnp.pad(ffn_w_down[l], ((0, FF_ROWS_PAD - FF_ROWS), (0, 0)))
    return [sq_a[0].astype(BF16), sq_b[0].astype(BF16), mem_w_kv[l].astype(BF16), _pad_ff(ffn_w_up[l], 1).astype(BF16),
            wdown.astype(BF16)]


def mixer_weights(l, gath, ln1_g, ln1_b, ln2_g, ln2_b):
    w_out = gath[1].reshape(D_MODEL, D_MODEL)
    wt = {"memw": gath[2].reshape(D_MODEL, 2 * MEM_WIDTH), "wout": w_out,
          "ln1_g": ln1_g[l:l + 1], "ln1_b": ln1_b[l:l + 1], "ln2_g": ln2_g[l:l + 1], "ln2_b": ln2_b[l:l + 1]}
    return wt, gath[0].reshape(D_MODEL, D_MODEL)


def ffn_weights(l, wup_g, wdown_g, small, conv_b):
    cb = _pad_ff(conv_b[l].reshape(N_DEV, FF_BLOCK), 1)
    cw = jnp.concatenate([small[:, 3 * l:3 * l + 3, :], cb[:, None, :], jnp.zeros((N_DEV, 4, FF_BLOCK_PAD), F32)], axis=1)
    return {"wup": wup_g, "wdown": wdown_g.reshape(FF_PAIRS * FF_BLOCK_PAD, D_MODEL), "cw": cw}


def mixer_grad_blocks(g, w_in_grad):
    blocks = [] if w_in_grad is None else [w_in_grad.reshape(N_DEV, 128, D_MODEL)]
    blocks += [g["wout"].reshape(N_DEV, 128, D_MODEL), g["memw"].reshape(N_DEV, 128, 2 * MEM_WIDTH)]
    return [b.astype(BF16) for b in blocks]


def ffn_grad_blocks(g):
    wdown = g["wdown"].reshape(N_DEV, FF_ROWS_PAD, D_MODEL)[:, :FF_ROWS]
    return [_unpad_ff(g["wup"], 2).astype(BF16), wdown.astype(BF16)]


def small_grad_blocks(g0, g1):
    taps = jnp.stack([g0["cw"][:, :3, :], g1["cw"][:, :3, :]], axis=1).reshape(N_DEV, DEPTH * 3, FF_BLOCK_PAD)
    small = jnp.zeros((N_DEV, 16, FF_BLOCK_PAD), F32)
    small = lax.dynamic_update_slice(small, taps, (0, 0, 0))
    return lax.dynamic_update_slice(small, g0["pscale"].reshape(N_DEV, 1, 96), (0, 8, 0))


def replicated_grads(g0, g1):
    pw = jnp.stack([g0["pw_full"][k * POOL_GROUP:(k + 1) * POOL_GROUP, k * POOL_GROUP:(k + 1) * POOL_GROUP] for k in range(4)])
    conv_b = jnp.stack([_unpad_ff(g_["cw"][:, 3, :], 1).reshape(N_DEV * FF_BLOCK) for g_ in (g0, g1)])
    ln = [jnp.concatenate([g0[n], g1[n]], axis=0) for n in ("ln1_g", "ln1_b", "ln2_g", "ln2_b")]
    return pack_replicated(pw[None], ln[0], ln[1], ln[2], ln[3], conv_b, g1["fb"][0, :FOX_HEADS])


def kernel(x, mem, a_w_in, a_pool_w, a_pool_scale, a_w_out, b_w_q, b_w_out, kv_w, f_b, mem_w_kv, ln1_g, ln1_b, ln2_g, ln2_b, ffn_w_up, ffn_conv_w, ffn_conv_b, ffn_w_down, loss_target, m_a_w_in, m_a_pool_w, m_a_pool_scale, m_a_w_out, m_b_w_q, m_b_w_out, m_kv_w, m_f_b, m_mem_w_kv, m_ln1_g, m_ln1_b, m_ln2_g, m_ln2_b, m_ffn_w_up, m_ffn_conv_w, m_ffn_conv_b, m_ffn_w_down, v_a_w_in, v_a_pool_w, v_a_pool_scale, v_a_w_out, v_b_w_q, v_b_w_out, v_kv_w, v_f_b, v_mem_w_kv, v_ln1_g, v_ln1_b, v_ln2_g, v_ln2_b, v_ffn_w_up, v_ffn_conv_w, v_ffn_conv_b, v_ffn_w_down):
    nb, s, d = x.shape
    t = nb * s
    x2d, mem_bf, target = x.reshape(t, d), mem.reshape(nb * MEM_LEN, d).astype(BF16), loss_target.reshape(t, d)

    shards0 = layer_shards(0, a_w_in, a_w_out, mem_w_kv, ffn_w_up, ffn_w_down)
    shards1 = layer_shards(1, b_w_q, b_w_out, mem_w_kv, ffn_w_up, ffn_w_down)
    shards1.append(jnp.pad(kv_w, ((0, 0), (0, KV_COLS_PAD - KV_COLS))).astype(BF16))
    gath0 = exchange(shards0[:3] + [pack_small(ffn_conv_w, a_pool_scale)], [False] * 4, "gather_w0_mixer")
    pending = {"ffn0": exchange_start(shards0[3:], [False] * 2, gath0[0], "gather_w0_ffn_start")}
    small = gath0[3]
    wt0, w_in = mixer_weights(0, gath0, ln1_g + pending["ffn0"]["token"], ln1_b, ln2_g, ln2_b)
    pw_bd = _block_diag(a_pool_w[0])
    wt0.update(win_a=w_in, pw_bd=pw_bd.astype(BF16),
               pscale=small[:, 8, :96].reshape(1, TOK_WIDTH) + pending["ffn0"]["token"])

    def ffn0_weights(x1_bf):
        got = exchange_wait(pending["ffn0"], x1_bf, "gather_w0_ffn_wait")
        pending["w1"] = exchange_start(shards1, [False] * 6, got[0], "gather_w1_start")
        w = ffn_weights(0, got[0], got[1], small, ffn_conv_b)
        w["cw"] = w["cw"] + pending["w1"]["token"]
        return w

    x1, x1_bf, sv0 = forward_layer(0, x2d, x2d, mem_bf, wt0, nb, s, ffn_weights=ffn0_weights)
    gath1 = exchange_wait(pending["w1"], x1_bf, "gather_w1_wait")
    wt1, w_q = mixer_weights(1, gath1, ln1_g, ln1_b, ln2_g, ln2_b)
    wt1.update(ffn_weights(1, gath1[3], gath1[4], small, ffn_conv_b))
    kvw = gath1[5].reshape(D_MODEL, KV_COLS_PAD)
    wt1.update(wq=w_q, kvw=kvw,
               fb=jnp.pad(f_b.reshape(1, FOX_HEADS), ((0, 0), (0, LANES - FOX_HEADS))))
    _, _, sv1 = forward_layer(1, x1, x1_bf, mem_bf, wt1, nb, s)

    dx1, g1 = backward_layer(1, None, sv1, mem_bf, wt1, nb, s, loss_target=target)
    loss = lax.psum(g1["loss_row"][0, 0], ("x", "y", "c"))
    blocks1 = (mixer_grad_blocks(g1, g1["wq"]) + ffn_grad_blocks(g1)
               + [g1["kvw"][:, :KV_COLS].reshape(N_DEV, 128, KV_COLS).astype(BF16)])
    pending["g1"] = exchange_start(blocks1, [True] * 6, dx1[0], "scatter_g1_start")
    wt0["ln2_g"] = wt0["ln2_g"] + pending["g1"]["token"]

    def after_ffn0(g, dxm):
        pending["gf0"] = exchange_start(ffn_grad_blocks(g), [True] * 2, dxm, "scatter_g0_ffn_start")
        return pending["gf0"]["token"]

    def after_pool0(g, x):
        blocks = mixer_grad_blocks(g, None) + [small_grad_blocks(g, g1), replicated_grads(g, g1)]
        pending["gm0"] = exchange_start(blocks, [True] * 3 + [False], x, "scatter_g0_mixer_start")
        return pending["gm0"]["token"]

    grad_x, g0 = backward_layer(0, dx1, sv0, mem_bf, wt0, nb, s, after_ffn=after_ffn0, after_pool=after_pool0)
    pending["gin"] = exchange_start([g0["win_a"].reshape(N_DEV, 128, D_MODEL).astype(BF16)], [True], grad_x,
                                    "scatter_g0_in_start")
    parts_f0 = exchange_wait(pending["gf0"], jnp.zeros((8, LANES), F32) + pending["gin"]["token"], "scatter_g0_ffn_wait")
    parts1 = exchange_wait(pending["g1"], parts_f0[0], "scatter_g1_wait")

    res = {}

    def upd(nm, parts, w2, m2, v2):
        res[nm] = reduce_adamw(parts, w2, m2, v2, f"adamw_{nm}")

    upd("b_w_q", [parts1[0]], b_w_q, m_b_w_q, v_b_w_q)
    upd("b_w_out", [parts1[1]], b_w_out, m_b_w_out, v_b_w_out)
    upd("kv_w", [parts1[5]], kv_w[None], m_kv_w[None], v_kv_w[None])
    upd("ffn_w_up", [parts_f0[0], parts1[3]], ffn_w_up, m_ffn_w_up, v_ffn_w_up)
    upd("ffn_w_down", [parts_f0[1], parts1[4]], ffn_w_down, m_ffn_w_down, v_ffn_w_down)
    parts_m0 = exchange_wait(pending["gm0"], res["ffn_w_down"][0], "scatter_g0_mixer_wait")
    parts_in = exchange_wait(pending["gin"], parts_m0[0], "scatter_g0_in_wait")
    upd("a_w_in", [parts_in[0]], a_w_in, m_a_w_in, v_a_w_in)
    upd("a_w_out", [parts_m0[0]], a_w_out, m_a_w_out, v_a_w_out)
    upd("mem_w_kv", [parts_m0[1], parts1[2]], mem_w_kv, m_mem_w_kv, v_mem_w_kv)
    upd("small", [parts_m0[2]], pack_small(ffn_conv_w, a_pool_scale)[None], pack_small(m_ffn_conv_w, m_a_pool_scale)[None],
        pack_small(v_ffn_conv_w, v_a_pool_scale)[None])
    upd("replicated", [parts_m0[3]], pack_replicated(a_pool_w, ln1_g, ln1_b, ln2_g, ln2_b, ffn_conv_b, f_b)[None],
        pack_replicated(m_a_pool_w, m_ln1_g, m_ln1_b, m_ln2_g, m_ln2_b, m_ffn_conv_b, m_f_b)[None],
        pack_replicated(v_a_pool_w, v_ln1_g, v_ln1_b, v_ln2_g, v_ln2_b, v_ffn_conv_b, v_f_b)[None])

    res["kv_w"] = [o[0] for o in res["kv_w"]]
    res["ffn_conv_w"] = [_unpad_ff(o[0, :DEPTH * 3, :], 1).reshape(DEPTH, 3, FF_BLOCK) for o in res["small"]]
    res["a_pool_scale"] = [o[0, 8:9, :96] for o in res["small"]]
    rep_names = ["a_pool_w", "ln1_g", "ln1_b", "ln2_g", "ln2_b", "ffn_conv_b", "f_b"]
    for nm in rep_names:
        res[nm] = []
    for o in res["replicated"]:
        for nm, val in zip(rep_names, unpack_replicated(o[0])):
            res[nm].append(val)

    order = ["a_w_in", "a_pool_w", "a_pool_scale", "a_w_out", "b_w_q", "b_w_out", "kv_w", "f_b", "mem_w_kv",
             "ln1_g", "ln1_b", "ln2_g", "ln2_b", "ffn_w_up", "ffn_conv_w", "ffn_conv_b", "ffn_w_down"]
    out = [loss, grad_x.reshape(nb, s, d)]
    for kind in range(4):
        out.extend(res[nm][kind] for nm in order)
    return tuple(out)
```

```python
import jax
import jax.numpy as jnp
from jax import lax
from jax.experimental import pallas as pl
from jax.experimental.pallas import tpu as pltpu

F32 = jnp.float32
BF16 = jnp.bfloat16
SDS = jax.ShapeDtypeStruct

N_DEV = 8
D_MODEL = 1024
TOK_WIDTH = 768
MEM_WIDTH = 256
MEM_LEN = 256
MEM_HEADS = 4
HEAD_DIM = 64
FOX_HEADS = 12
POOL_GROUP = 192
FF_BLOCK = 688
FF_BLOCK_PAD = 768
FF_PAIRS = 4
FF_ROWS = 344
FF_ROWS_PAD = FF_BLOCK_PAD // 2
KV_COLS = 1548
KV_COLS_PAD = 1664
LANES = 128
DEPTH = 2
DN_ALPHA = (2.0 * DEPTH) ** 0.25
LN_EPS = 1e-5
QK_SCALE = HEAD_DIM ** -0.5
NEG_BIG = -1e30

ADAM_LR = 0.001
ADAM_B1 = 0.9
ADAM_B2 = 0.999
ADAM_EPS = 1e-08
ADAM_WD = 0.01
ADAM_STEP = 10

VMEM_LIMIT_BYTES = 56 * 1024 * 1024
MM_BLOCK_BYTES = 6 * 1024 * 1024
TM = 512
TS = 256
TF = 256
TC = 256
HALO_POOL = 16
HALO_CONV = 8

NT_DIMS = (((1,), (1,)), ((), ()))
TN_DIMS = (((0,), (0,)), ((), ()))


def _params(sem=None):
    return pltpu.CompilerParams(dimension_semantics=sem, vmem_limit_bytes=VMEM_LIMIT_BYTES)


def _sigmoid(z):
    return 1.0 / (1.0 + jnp.exp(-z))


def _pick_tn(n):
    if n <= 2048:
        return n
    for t in (1024, 768, 512, 256, 128):
        if n % t == 0:
            return t
    return n


def mm_nn(a, b, out_dtype, name, addend=None, add_scale=1.0, also_bf16=False, trans_b=None):
    m, k = a.shape
    n = b.shape[1] if trans_b is None else b.shape[0]
    tm = min(TM, m)
    tn = n
    while k * tn * 2 > MM_BLOCK_BYTES or tm * tn * 4 > MM_BLOCK_BYTES:
        tn //= 2
    chunk = tn if tn <= 2048 else _pick_tn(tn)
    has_add = addend is not None

    def body(*refs):
        a_ref, b_ref = refs[0], refs[1]
        c_ref = refs[2] if has_add else None
        o_ref = refs[3] if has_add else refs[2]
        ob_ref = refs[-1] if also_bf16 else None
        av = a_ref[...].astype(BF16)
        for c in range(tn // chunk):
            cols = slice(c * chunk, (c + 1) * chunk)
            if trans_b is None:
                r = jnp.dot(av, b_ref[:, cols].astype(BF16), preferred_element_type=F32)
            else:
                r = lax.dot_general(av, b_ref[cols, :].astype(BF16), NT_DIMS, preferred_element_type=F32)
            if has_add:
                r = r + add_scale * c_ref[:, cols]
            o_ref[:, cols] = r.astype(out_dtype)
            if also_bf16:
                ob_ref[:, cols] = r.astype(BF16)

    b_spec = (pl.BlockSpec((k, tn), lambda j, i: (0, j)) if trans_b is None
              else pl.BlockSpec((tn, k), lambda j, i: (j, trans_b)))
    in_specs = [pl.BlockSpec((tm, k), lambda j, i: (i, 0)), b_spec]
    ops = [a, b]
    tile = pl.BlockSpec((tm, tn), lambda j, i: (i, j))
    if has_add:
        in_specs.append(tile)
        ops.append(addend)
    out_shape = [SDS((m, n), out_dtype)]
    out_specs = [tile]
    if also_bf16:
        out_shape.append(SDS((m, n), BF16))
        out_specs.append(tile)
    res = pl.pallas_call(
        body, name=name, grid=(n // tn, m // tm), in_specs=in_specs, out_specs=out_specs, out_shape=out_shape,
        compiler_params=_params(("parallel", "parallel")))(*ops)
    return tuple(res) if also_bf16 else res[0]


def mm_tn(a, b, name, blocked=False):
    t, m = a.shape
    _, n = b.shape
    tt = min(4 * TM, t)
    tm = 1024 if m % 1024 == 0 else m
    tn = FF_BLOCK_PAD if blocked else _pick_tn(n)
    nt = t // tt

    def body(a_ref, b_ref, o_ref):
        kk = pl.program_id(2)
        r = lax.dot_general(a_ref[...].astype(BF16), b_ref[...].astype(BF16), TN_DIMS, preferred_element_type=F32)
        if blocked:
            r = r[None]

        @pl.when(kk == 0)
        def _():
            o_ref[...] = r

        @pl.when(kk != 0)
        def _():
            o_ref[...] += r

    if blocked:
        out_shape = SDS((n // tn, m, tn), F32)
        out_spec = pl.BlockSpec((1, tm, tn), lambda i, j, kk: (j, i, 0))
    else:
        out_shape = SDS((m, n), F32)
        out_spec = pl.BlockSpec((tm, tn), lambda i, j, kk: (i, j))
    return pl.pallas_call(
        body, name=name, grid=(m // tm, n // tn, nt),
        in_specs=[pl.BlockSpec((tt, tm), lambda i, j, kk: (kk, i)), pl.BlockSpec((tt, tn), lambda i, j, kk: (kk, j))],
        out_specs=out_spec, out_shape=out_shape,
        compiler_params=_params(("parallel", "parallel", "arbitrary")))(a, b)


def ln_fwd(xprev, a, w, g, b, name):
    t, d = xprev.shape
    k = a.shape[1]
    tm = min(TM, t)

    def body(xp_ref, a_ref, w_ref, g_ref, b_ref, y_ref, yb_ref, xh_ref, rs_ref):
        r = DN_ALPHA * xp_ref[...] + jnp.dot(a_ref[...], w_ref[...], preferred_element_type=F32)
        mu = jnp.mean(r, axis=1, keepdims=True)
        xc = r - mu
        var = jnp.mean(xc * xc, axis=1, keepdims=True)
        rstd = lax.rsqrt(var + LN_EPS)
        xh = xc * rstd
        y = xh * g_ref[...] + b_ref[...]
        y_ref[...] = y
        yb_ref[...] = y.astype(BF16)
        xh_ref[...] = xh
        rs_ref[...] = jnp.broadcast_to(rstd, (tm, LANES))

    row = pl.BlockSpec((tm, d), lambda i: (i, 0))
    vec = pl.BlockSpec((1, d), lambda i: (0, 0))
    return pl.pallas_call(
        body, name=name, grid=(t // tm,),
        in_specs=[row, pl.BlockSpec((tm, k), lambda i: (i, 0)), pl.BlockSpec((k, d), lambda i: (0, 0)), vec, vec],
        out_specs=[row, row, row, pl.BlockSpec((tm, LANES), lambda i: (i, 0))],
        out_shape=[SDS((t, d), F32), SDS((t, d), BF16), SDS((t, d), F32), SDS((t, LANES), F32)],
        compiler_params=_params(("parallel",)))(xprev, a, w, g, b)


def ln_bwd(dy, xhat, rstd, g, name, products=(), dy_scale=1.0):
    t, d = dy.shape
    np_ = len(products)
    tm = min(TM if sum(a.shape[1] for a, _, _ in products) <= 4096 else TS, t)

    def body(*refs):
        prod_refs = refs[:2 * np_]
        dy_ref, xh_ref, rs_ref, g_ref, dr_ref, drb_ref, dg_ref, db_ref = refs[2 * np_:]
        i = pl.program_id(0)
        dyv = dy_ref[...] if dy_scale == 1.0 else dy_scale * dy_ref[...]
        for p in range(np_):
            a_ref, w_ref = prod_refs[2 * p], prod_refs[2 * p + 1]
            if len(w_ref.shape) == 2:
                dyv = dyv + lax.dot_general(a_ref[...], w_ref[...], NT_DIMS, preferred_element_type=F32)
            else:
                kb = w_ref.shape[2]
                for c in range(w_ref.shape[0]):
                    dyv = dyv + lax.dot_general(a_ref[:, c * kb:(c + 1) * kb], w_ref[c], NT_DIMS,
                                                preferred_element_type=F32)
        xh = xh_ref[...]
        dxh = dyv * g_ref[...]
        m1 = jnp.mean(dxh, axis=1, keepdims=True)
        m2 = jnp.mean(dxh * xh, axis=1, keepdims=True)
        dr = rs_ref[:, 0:1] * (dxh - m1 - xh * m2)
        dr_ref[...] = dr
        drb_ref[...] = dr.astype(BF16)

        @pl.when(i == 0)
        def _():
            dg_ref[...] = jnp.zeros_like(dg_ref)
            db_ref[...] = jnp.zeros_like(db_ref)

        dg_ref[...] += jnp.sum(dyv * xh, axis=0, keepdims=True)
        db_ref[...] += jnp.sum(dyv, axis=0, keepdims=True)

    row = pl.BlockSpec((tm, d), lambda i: (i, 0))
    vec = pl.BlockSpec((1, d), lambda i: (0, 0))
    in_specs = [row, row, pl.BlockSpec((tm, LANES), lambda i: (i, 0)), vec]
    ops = [dy, xhat, rstd, g]
    for a, w, col in reversed(products):
        k = a.shape[1]
        if w.ndim == 2:
            w_spec = pl.BlockSpec((d, k), lambda i, col=col: (0, col))
        else:
            w_spec = pl.BlockSpec((k // w.shape[2], d, w.shape[2]), lambda i, col=col: (col, 0, 0))
        in_specs = [pl.BlockSpec((tm, k), lambda i: (i, 0)), w_spec] + in_specs
        ops = [a, w] + ops
    return pl.pallas_call(
        body, name=name, grid=(t // tm,), in_specs=in_specs, out_specs=[row, row, vec, vec],
        out_shape=[SDS((t, d), F32), SDS((t, d), BF16), SDS((1, d), F32), SDS((1, d), F32)],
        compiler_params=_params(("arbitrary",)))(*ops)


def loss_ln_bwd(xhat, rstd, g, beta, target, name):
    t, d = xhat.shape
    tm = min(TM, t)
    nsteps = t // tm

    def body(xh_ref, rs_ref, g_ref, b_ref, t_ref, dr_ref, drb_ref, dg_ref, db_ref, l_ref, acc):
        i = pl.program_id(0)
        xh = xh_ref[...]
        diff = xh * g_ref[...] + b_ref[...] - t_ref[...]
        dyv = diff * (1.0 / d)
        dxh = dyv * g_ref[...]
        m1 = jnp.mean(dxh, axis=1, keepdims=True)
        m2 = jnp.mean(dxh * xh, axis=1, keepdims=True)
        dr = rs_ref[:, 0:1] * (dxh - m1 - xh * m2)
        dr_ref[...] = dr
        drb_ref[...] = dr.astype(BF16)

        @pl.when(i == 0)
        def _():
            dg_ref[...] = jnp.zeros_like(dg_ref)
            db_ref[...] = jnp.zeros_like(db_ref)
            acc[...] = jnp.zeros_like(acc)

        dg_ref[...] += jnp.sum(dyv * xh, axis=0, keepdims=True)
        db_ref[...] += jnp.sum(dyv, axis=0, keepdims=True)
        acc[...] += jnp.sum(diff * diff, axis=0, keepdims=True)

        @pl.when(i == nsteps - 1)
        def _():
            tot = jnp.sum(acc[...], axis=1, keepdims=True) * (0.5 / d)
            l_ref[...] = jnp.broadcast_to(tot, (1, LANES))

    row = pl.BlockSpec((tm, d), lambda i: (i, 0))
    vec = pl.BlockSpec((1, d), lambda i: (0, 0))
    return pl.pallas_call(
        body, name=name, grid=(nsteps,),
        in_specs=[row, pl.BlockSpec((tm, LANES), lambda i: (i, 0)), vec, vec, row],
        out_specs=[row, row, vec, vec, pl.BlockSpec((1, LANES), lambda i: (0, 0))],
        out_shape=[SDS((t, d), F32), SDS((t, d), BF16), SDS((1, d), F32), SDS((1, d), F32), SDS((1, LANES), F32)],
        scratch_shapes=[pltpu.VMEM((1, d), F32)],
        compiler_params=_params(("arbitrary",)))(xhat, rstd, g, beta, target)


def memattn_fwd(proj, memkv, nb, s, name):
    ts = min(TM, s)
    nq = s // ts

    def body(q_ref, kv_ref, o_ref):
        top = lax.broadcasted_iota(jnp.int32, (PAIR, ts), 0) < HEAD_DIM
        scores = []
        for p in range(MEM_HEADS // 2):
            qp = q_ref[:, p * PAIR:(p + 1) * PAIR].astype(BF16)
            ke, ko = _split_pair(kv_ref[:, p * PAIR:(p + 1) * PAIR], QK_SCALE)
            scores.append([lax.dot_general(km, qp, NT_DIMS, preferred_element_type=F32) for km in (ke, ko)])
        for p in range(MEM_HEADS // 2):
            vt = kv_ref[:, MEM_WIDTH + p * PAIR:MEM_WIDTH + (p + 1) * PAIR].astype(F32).T.astype(BF16)
            outs = []
            for sc in scores[p]:
                e = jnp.exp(sc - jnp.max(sc, axis=0, keepdims=True))
                pr = e / jnp.sum(e, axis=0, keepdims=True)
                outs.append(jnp.dot(vt, pr.astype(BF16), preferred_element_type=F32))
            o_ref[:, p * PAIR:(p + 1) * PAIR] = jnp.where(top, outs[0], outs[1]).T.astype(BF16)

    return pl.pallas_call(
        body, name=name, grid=(nb, nq),
        in_specs=[pl.BlockSpec((ts, MEM_WIDTH), lambda b, i: (b * nq + i, 3)),
                  pl.BlockSpec((MEM_LEN, 2 * MEM_WIDTH), lambda b, i: (b, 0))],
        out_specs=pl.BlockSpec((ts, MEM_WIDTH), lambda b, i: (b * nq + i, 0)),
        out_shape=SDS((nb * s, MEM_WIDTH), BF16),
        compiler_params=_params(("parallel", "parallel")))(proj, memkv)


def memattn_bwd(proj, memkv, dcat, nb, s, name):
    ts = min(TM, s)
    nq = s // ts

    def body(q_ref, kv_ref, do_ref, dq_ref, dkv_ref):
        i = pl.program_id(1)

        @pl.when(i == 0)
        def _():
            dkv_ref[...] = jnp.zeros_like(dkv_ref)

        lo = _half_masks(MEM_LEN)
        top = lax.broadcasted_iota(jnp.int32, (PAIR, ts), 0) < HEAD_DIM
        n_pairs = MEM_HEADS // 2
        qs, dos, kps, products = [], [], [], []
        for p in range(n_pairs):
            qp = q_ref[:, p * PAIR:(p + 1) * PAIR].astype(BF16)
            dop = do_ref[:, p * PAIR:(p + 1) * PAIR].astype(BF16)
            kp = kv_ref[:, p * PAIR:(p + 1) * PAIR] * QK_SCALE
            kms = _split_pair(kp)
            vms = _split_pair(kv_ref[:, MEM_WIDTH + p * PAIR:MEM_WIDTH + (p + 1) * PAIR])
            products.append([(lax.dot_general(km, qp, NT_DIMS, preferred_element_type=F32),
                              lax.dot_general(vm, dop, NT_DIMS, preferred_element_type=F32)) for km, vm in zip(kms, vms)])
            qs.append(qp)
            dos.append(dop)
            kps.append(kp)
        for p in range(n_pairs):
            kt = kps[p].astype(F32).T.astype(BF16)
            dks, dvs, dqs = [], [], []
            for sc, dp in products[p]:
                e = jnp.exp(sc - jnp.max(sc, axis=0, keepdims=True))
                pr = e / jnp.sum(e, axis=0, keepdims=True)
                dl = jnp.sum(pr * dp, axis=0, keepdims=True)
                ds = (pr * (dp - dl)).astype(BF16)
                dvs.append(jnp.dot(pr.astype(BF16), dos[p], preferred_element_type=F32))
                dks.append(jnp.dot(ds, qs[p], preferred_element_type=F32))
                dqs.append(jnp.dot(kt, ds, preferred_element_type=F32))
            dq_ref[:, p * PAIR:(p + 1) * PAIR] = jnp.where(top, dqs[0], dqs[1]).T.astype(BF16)
            dkv_ref[:, p * PAIR:(p + 1) * PAIR] += jnp.where(lo, dks[0], dks[1]) * QK_SCALE
            dkv_ref[:, MEM_WIDTH + p * PAIR:MEM_WIDTH + (p + 1) * PAIR] += jnp.where(lo, dvs[0], dvs[1])

    return pl.pallas_call(
        body, name=name, grid=(nb, nq),
        in_specs=[pl.BlockSpec((ts, MEM_WIDTH), lambda b, i: (b * nq + i, 3)),
                  pl.BlockSpec((MEM_LEN, 2 * MEM_WIDTH), lambda b, i: (b, 0)),
                  pl.BlockSpec((ts, MEM_WIDTH), lambda b, i: (b * nq + i, 3))],
        out_specs=[pl.BlockSpec((ts, MEM_WIDTH), lambda b, i: (b * nq + i, 0)),
                   pl.BlockSpec((MEM_LEN, 2 * MEM_WIDTH), lambda b, i: (b, 0))],
        out_shape=[SDS((nb * s, MEM_WIDTH), BF16), SDS((nb * MEM_LEN, 2 * MEM_WIDTH), F32)],
        compiler_params=_params(("parallel", "arbitrary")))(proj, memkv, dcat)


def _pool_select(shape, s2, s4, s8, s16):
    lane = lax.broadcasted_iota(jnp.int32, shape, 1)
    return jnp.where(lane < POOL_GROUP, s2, jnp.where(lane < 2 * POOL_GROUP, s4, jnp.where(lane < 3 * POOL_GROUP, s8, s16)))


def _pool_count(shape, first_pos):
    pos = first_pos + lax.broadcasted_iota(jnp.int32, shape, 0)
    win = _pool_select(shape, 2, 4, 8, 16)
    return jnp.minimum(pos + 1, win).astype(F32)


def pool_fwd(proj, pw_bd, pscale, nb, s):
    ts = min(TS, s)
    nq = s // ts
    w = TOK_WIDTH

    def body(c_ref, h_ref, w_ref, sc_ref, pooled_ref, tok_ref):
        i = pl.program_id(0) % nq
        cur = c_ref[...]
        halo = jnp.where(i == 0, 0.0, h_ref[...])
        xe = jnp.concatenate([halo, cur], axis=0)
        s2 = xe + pltpu.roll(xe, 1, axis=0)
        s4 = s2 + pltpu.roll(s2, 2, axis=0)
        s8 = s4 + pltpu.roll(s4, 4, axis=0)
        s16 = s8 + pltpu.roll(s8, 8, axis=0)
        hp = HALO_POOL
        ws = _pool_select((ts, w), s2[hp:], s4[hp:], s8[hp:], s16[hp:])
        pooled = (ws / _pool_count((ts, w), i * ts) - cur).astype(BF16)
        pooled_ref[...] = pooled
        mixed = jnp.dot(pooled, w_ref[...], preferred_element_type=F32)
        tok_ref[...] = (mixed * sc_ref[...]).astype(BF16)

    row = pl.BlockSpec((ts, w), lambda r: (r, 0))
    return pl.pallas_call(
        body, name="pool_fwd", grid=(nb * nq,),
        in_specs=[row, pl.BlockSpec((HALO_POOL, w), lambda r: (jnp.maximum(r * (ts // HALO_POOL) - 1, 0), 0)),
                  pl.BlockSpec((w, w), lambda r: (0, 0)), pl.BlockSpec((1, w), lambda r: (0, 0))],
        out_specs=[row, row], out_shape=[SDS((nb * s, w), BF16), SDS((nb * s, w), BF16)],
        compiler_params=_params(("parallel",)))(proj, proj, pw_bd, pscale)


def pool_bwd_mix(dcat, pooled, pw_bd, pscale, nb, s):
    ts = min(TS, s)
    w = TOK_WIDTH

    def body(dt_ref, p_ref, w_ref, sc_ref, dm_ref, dp_ref, ds_ref):
        r = pl.program_id(0)
        dtok = dt_ref[...]
        mixed = jnp.dot(p_ref[...], w_ref[...], preferred_element_type=F32)

        @pl.when(r == 0)
        def _():
            ds_ref[...] = jnp.zeros_like(ds_ref)

        ds_ref[...] += jnp.sum(dtok * mixed, axis=0, keepdims=True)
        dmx = (dtok * sc_ref[...]).astype(BF16)
        dm_ref[...] = dmx
        dp_ref[...] = lax.dot_general(dmx, w_ref[...], NT_DIMS, preferred_element_type=F32)

    row = pl.BlockSpec((ts, w), lambda r: (r, 0))
    mat = pl.BlockSpec((w, w), lambda r: (0, 0))
    vec = pl.BlockSpec((1, w), lambda r: (0, 0))
    return pl.pallas_call(
        body, name="pool_bwd_mix", grid=(nb * s // ts,), in_specs=[row, row, mat, vec],
        out_specs=[row, row, vec], out_shape=[SDS((nb * s, w), BF16), SDS((nb * s, w), F32), SDS((1, w), F32)],
        compiler_params=_params(("arbitrary",)))(dcat, pooled, pw_bd, pscale)


def pool_bwd_window(dpooled, nb, s):
    ts = min(TS, s)
    nq = s // ts
    w = TOK_WIDTH
    n_ext = ts + HALO_POOL
    n_halo_blocks = nb * s // HALO_POOL

    def body(c_ref, n_ref, du_ref):
        i = pl.program_id(0) % nq
        cur = c_ref[...]
        nxt = jnp.where(i == nq - 1, 0.0, n_ref[...])
        ze = jnp.concatenate([cur, nxt], axis=0) / _pool_count((n_ext, w), i * ts)
        s2 = ze + pltpu.roll(ze, n_ext - 1, axis=0)
        s4 = s2 + pltpu.roll(s2, n_ext - 2, axis=0)
        s8 = s4 + pltpu.roll(s4, n_ext - 4, axis=0)
        s16 = s8 + pltpu.roll(s8, n_ext - 8, axis=0)
        ws = _pool_select((ts, w), s2[:ts], s4[:ts], s8[:ts], s16[:ts])
        du_ref[...] = (ws - cur).astype(BF16)

    row = pl.BlockSpec((ts, w), lambda r: (r, 0))
    return pl.pallas_call(
        body, name="pool_bwd_window", grid=(nb * nq,),
        in_specs=[row, pl.BlockSpec((HALO_POOL, w),
                                    lambda r: (jnp.minimum((r + 1) * (ts // HALO_POOL), n_halo_blocks - 1), 0))],
        out_specs=row, out_shape=SDS((nb * s, w), BF16),
        compiler_params=_params(("parallel",)))(dpooled, dpooled)


def _conv_rows(xe, w_ref):
    return (w_ref[0, 2:3, :] * xe + w_ref[0, 1:2, :] * pltpu.roll(xe, 1, axis=0)
            + w_ref[0, 0:1, :] * pltpu.roll(xe, 2, axis=0) + w_ref[0, 3:4, :])


def ffn_up_gate(x_bf, wup, cw, nb, s, name):
    tm = min(2 * TM, s)
    nq = s // tm
    w = FF_BLOCK_PAD
    hr = 2 * HALO_CONV
    k = x_bf.shape[1]

    def body(xc_ref, xh_ref, wu_ref, wg_ref, cu_ref, cg_ref, act_ref, a_ref, b_ref, hu_ref, hg_ref):
        first = (pl.program_id(1) % nq) == 0
        xc = xc_ref[...]
        xh = xh_ref[...]

        def products(w_ref):
            return (jnp.dot(xc, w_ref[0], preferred_element_type=F32), jnp.dot(xh, w_ref[0], preferred_element_type=F32))

        def conv(hcur, hprev, c_ref, h_out):
            h_out[...] = hcur.astype(BF16)
            xe = jnp.concatenate([jnp.where(first, 0.0, hprev), hcur], axis=0)
            return _conv_rows(xe, c_ref)[hr:]

        pu, pg = products(wu_ref), products(wg_ref)
        cu = conv(*pu, cu_ref, hu_ref)
        cg = conv(*pg, cg_ref, hg_ref)
        sg = _sigmoid(cg)
        a = cg * sg
        act_ref[...] = (a * cu).astype(BF16)
        a_ref[...] = a.astype(BF16)
        b_ref[...] = (cu * (sg * (1.0 + cg * (1.0 - sg)))).astype(BF16)

    def wblock(off):
        return pl.BlockSpec((1, k, w), lambda j, r: (j + off, 0, 0))

    def cblock(off):
        return pl.BlockSpec((1, 8, w), lambda j, r: (j + off, 0, 0))

    tile = pl.BlockSpec((tm, w), lambda j, r: (r, j))
    out = SDS((nb * s, FF_PAIRS * w), BF16)
    return pl.pallas_call(
        body, name=name, grid=(FF_PAIRS, nb * nq),
        in_specs=[pl.BlockSpec((tm, k), lambda j, r: (r, 0)),
                  pl.BlockSpec((hr, k), lambda j, r: (jnp.maximum(r * (tm // hr) - 1, 0), 0)),
                  wblock(0), wblock(FF_PAIRS), cblock(0), cblock(FF_PAIRS)],
        out_specs=[tile] * 5, out_shape=[out] * 5,
        compiler_params=_params(("parallel", "parallel")))(x_bf, x_bf, wup, wup, cw, cw)


def gate_conv_bwd(dact, a, b, hu, hg, cw, nb, s, name):
    ts = min(TS, s)
    nq = s // ts
    w = FF_BLOCK_PAD
    hc = HALO_CONV
    hb = 2 * hc
    n_ext = ts + hc

    def body(dc_ref, dn_ref, ac_ref, an_ref, bc_ref, bn_ref, hu_ref, hg_ref, wu_ref, wg_ref,
             dhu_ref, dhg_ref, dwu_ref, dwg_ref):
        r = pl.program_id(1)
        last = (r % nq) == nq - 1

        def ext(c_ref, n_ref, mask_next=False):
            nxt = n_ref[...].astype(F32)[:hc]
            if mask_next:
                nxt = jnp.where(last, 0.0, nxt)
            return jnp.concatenate([c_ref[...].astype(F32), nxt], axis=0)

        da = ext(dc_ref, dn_ref, mask_next=True)

        def branch(dcv, w_ref, h_ref, dh_ref, dw_ref):
            d0 = dcv[:ts]
            d1 = pltpu.roll(dcv, n_ext - 1, axis=0)[:ts]
            d2 = pltpu.roll(dcv, n_ext - 2, axis=0)[:ts]
            dh_ref[...] = (w_ref[0, 2:3, :] * d0 + w_ref[0, 1:2, :] * d1 + w_ref[0, 0:1, :] * d2).astype(BF16)
            hv = h_ref[...].astype(F32)
            rows = [jnp.sum(d2 * hv, axis=0, keepdims=True), jnp.sum(d1 * hv, axis=0, keepdims=True),
                    jnp.sum(d0 * hv, axis=0, keepdims=True), jnp.sum(d0, axis=0, keepdims=True)]
            sub = lax.broadcasted_iota(jnp.int32, (8, w), 0)
            upd = jnp.zeros((8, w), F32)
            for kk, rv in enumerate(rows):
                upd = jnp.where(sub == kk, rv, upd)

            @pl.when(r == 0)
            def _():
                dw_ref[...] = jnp.zeros_like(dw_ref)

            dw_ref[...] += upd[None]

        branch(da * ext(ac_ref, an_ref), wu_ref, hu_ref, dhu_ref, dwu_ref)
        branch(da * ext(bc_ref, bn_ref), wg_ref, hg_ref, dhg_ref, dwg_ref)

    cur = pl.BlockSpec((ts, w), lambda j, r: (r, j))
    nxt = pl.BlockSpec((hb, w), lambda j, r: (jnp.minimum((r + 1) * (ts // hb), nb * s // hb - 1), j))

    def wspec(off):
        return pl.BlockSpec((1, 8, w), lambda j, r: (j + off, 0, 0))

    p = FF_PAIRS
    dw_spec = pl.BlockSpec((1, 8, w), lambda j, r: (j, 0, 0))
    return pl.pallas_call(
        body, name=name, grid=(p, nb * nq),
        in_specs=[cur, nxt, cur, nxt, cur, nxt, cur, cur, wspec(0), wspec(p)],
        out_specs=[cur, cur, dw_spec, dw_spec],
        out_shape=[SDS((nb * s, p * w), BF16), SDS((nb * s, p * w), BF16), SDS((p, 8, w), F32), SDS((p, 8, w), F32)],
        compiler_params=_params(("parallel", "arbitrary")))(dact, dact, a, a, b, b, hu, hg, cw, cw)


def _tri(n, upper):
    r = lax.broadcasted_iota(jnp.int32, (n, n), 0)
    c = lax.broadcasted_iota(jnp.int32, (n, n), 1)
    return ((r <= c) if upper else (r >= c)).astype(F32)


def fgate_fwd(fl, fb, nb, s):
    tc = min(TC, s)
    nq = s // tc

    def body(fl_ref, fb_ref, f_ref, carry):
        @pl.when(pl.program_id(1) == 0)
        def _():
            carry[...] = jnp.zeros_like(carry)

        z = fl_ref[...] + fb_ref[...]
        logf = jnp.minimum(z, 0.0) - jnp.log(1.0 + jnp.exp(-jnp.abs(z)))
        f_ref[...] = jnp.dot(_tri(tc, False), logf, preferred_element_type=F32,
                             precision=lax.Precision.HIGHEST) + carry[...]
        carry[...] += jnp.sum(logf, axis=0, keepdims=True)

    row = pl.BlockSpec((tc, LANES), lambda b, i: (b * nq + i, 0))
    return pl.pallas_call(
        body, name="fgate_fwd", grid=(nb, nq), in_specs=[row, pl.BlockSpec((1, LANES), lambda b, i: (0, 0))],
        out_specs=row, out_shape=SDS((nb * s, LANES), F32), scratch_shapes=[pltpu.VMEM((1, LANES), F32)],
        compiler_params=_params(("arbitrary", "arbitrary")))(fl, fb)


def fgate_bwd(d_cum_q, d_cum_k, fl, fb, nb, s):
    tc = min(TC, s)
    nq = s // tc

    def body(dfq_ref, dfk_ref, fl_ref, fb_ref, dfl_ref, dfb_ref, carry):
        b = pl.program_id(0)
        i = pl.program_id(1)

        @pl.when(i == 0)
        def _():
            carry[...] = jnp.zeros_like(carry)

        @pl.when(jnp.logical_and(b == 0, i == 0))
        def _():
            dfb_ref[...] = jnp.zeros_like(dfb_ref)

        dfv = dfq_ref[...] + dfk_ref[...]
        dlog = jnp.dot(_tri(tc, True), dfv, preferred_element_type=F32,
                       precision=lax.Precision.HIGHEST) + carry[...]
        carry[...] += jnp.sum(dfv, axis=0, keepdims=True)
        z = fl_ref[...] + fb_ref[...]
        dfl = dlog / (1.0 + jnp.exp(z))
        dfl_ref[...] = dfl
        dfb_ref[...] += jnp.sum(dfl, axis=0, keepdims=True)

    row = pl.BlockSpec((tc, LANES), lambda b, i: (b * nq + nq - 1 - i, 0))
    vec = pl.BlockSpec((1, LANES), lambda b, i: (0, 0))
    return pl.pallas_call(
        body, name="fgate_bwd", grid=(nb, nq), in_specs=[row, row, row, vec], out_specs=[row, vec],
        out_shape=[SDS((nb * s, LANES), F32), SDS((1, LANES), F32)], scratch_shapes=[pltpu.VMEM((1, LANES), F32)],
        compiler_params=_params(("arbitrary", "arbitrary")))(d_cum_q, d_cum_k, fl, fb)


PAIR = 2 * HEAD_DIM
N_PAIRS = FOX_HEADS // 2


def _lane_put(shape, h, col):
    lane = lax.broadcasted_iota(jnp.int32, shape, 1)
    return jnp.where(lane == h, col, 0.0)


def _half_masks(rows):
    lane = lax.broadcasted_iota(jnp.int32, (rows, PAIR), 1)
    return lane < HEAD_DIM


def _split_pair(x, scale=None):
    if scale is not None:
        x = x * scale
    lo = _half_masks(x.shape[0])
    zero = jnp.zeros_like(x)
    return jnp.where(lo, x, zero), jnp.where(lo, zero, x)


def _to_tile_rows(a, nb, s, tf):
    return a.reshape(nb * s // tf, tf, LANES)[:, :, :16].transpose(0, 2, 1)


def _from_tile_rows(a):
    tiles, _, tf = a.shape
    return jnp.pad(a.transpose(0, 2, 1), ((0, 0), (0, 0), (0, LANES - 16))).reshape(tiles * tf, LANES)


BIAS_TERMS = 3
LOOKAHEAD = 4
FOLLOW_FWD = 1
LOOKAHEAD_BWD = 2
FOLLOW_BWD = 1


def _bias_lane(h):
    return HEAD_DIM if h % 2 == 0 else 0


def _placement():
    rows = jnp.arange(LANES)[:, None]
    cols = jnp.arange(FOX_HEADS * PAIR)[None, :]
    head, lane = cols // PAIR, cols % PAIR
    first = jnp.where(head % 2 == 0, HEAD_DIM, 0)
    term = lane - first
    hit = (term >= 0) & (term < BIAS_TERMS) & (rows == 16 * term + head)
    return hit.astype(BF16)


def fox_prep(kv, fneg, nb, s):
    tf = min(TF, s)
    w = TOK_WIDTH

    def body(k_ref, v_ref, f_ref, pl_ref, ka_ref, vt_ref):
        lane = lax.broadcasted_iota(jnp.int32, (tf, LANES), 1)
        lo = lane < HEAD_DIM
        f = jnp.where(lane < FOX_HEADS, f_ref[...], 0.0)
        hi = f.astype(BF16).astype(F32)
        mid = (f - hi).astype(BF16).astype(F32)
        low = (f - hi - mid).astype(BF16).astype(F32)
        terms = (hi + pltpu.roll(mid, 16, axis=1) + pltpu.roll(low, 32, axis=1)).astype(BF16)
        placed = jnp.dot(terms, pl_ref[...], preferred_element_type=F32).astype(BF16)
        one = jnp.ones((tf, LANES), BF16)
        zero = jnp.zeros((tf, LANES), BF16)
        for p in range(N_PAIRS):
            kp = k_ref[:, p * PAIR:(p + 1) * PAIR] * QK_SCALE
            vp = v_ref[:, p * PAIR:(p + 1) * PAIR]
            he, ho = 2 * p, 2 * p + 1
            ka_ref[:, he * PAIR:(he + 1) * PAIR] = jnp.where(lo, kp, placed[:, he * PAIR:(he + 1) * PAIR])
            ka_ref[:, ho * PAIR:(ho + 1) * PAIR] = jnp.where(lo, placed[:, ho * PAIR:(ho + 1) * PAIR], kp)
            ve = jnp.where(lo, vp, jnp.where(lane == HEAD_DIM, one, zero))
            vo = jnp.where(lo, jnp.where(lane == 0, one, zero), vp)
            vt_ref[0, he * PAIR:(he + 1) * PAIR, :] = ve.astype(F32).T.astype(BF16)
            vt_ref[0, ho * PAIR:(ho + 1) * PAIR, :] = vo.astype(F32).T.astype(BF16)

    return pl.pallas_call(
        body, name="fox_prep", grid=(nb * s // tf,),
        in_specs=[pl.BlockSpec((tf, w), lambda r: (r, 0)), pl.BlockSpec((tf, w), lambda r: (r, 1)),
                  pl.BlockSpec((tf, LANES), lambda r: (r, 0)), pl.BlockSpec((LANES, FOX_HEADS * PAIR), lambda r: (0, 0))],
        out_specs=[pl.BlockSpec((tf, FOX_HEADS * PAIR), lambda r: (r, 0)),
                   pl.BlockSpec((1, FOX_HEADS * PAIR, tf), lambda r: (r, 0, 0))],
        out_shape=[SDS((nb * s, FOX_HEADS * PAIR), BF16), SDS((nb * s // tf, FOX_HEADS * PAIR, tf), BF16)],
        compiler_params=_params(("parallel",)))(kv, kv, fneg, _placement())


def fox_fwd_t(pq, kaug, vaug_t, nb, s):
    tf = min(TF, s)
    n = s // tf
    w = TOK_WIDTH
    wa = FOX_HEADS * PAIR

    def body(q_ref, k_hbm, vt_hbm, ob_ref, of_ref, lse_ref, k_vm, vt_vm, qx_scr, m_scr, acc_scr, sems):
        b = pl.program_id(0)
        i = pl.program_id(1)

        @pl.when(i == 0)
        def _():
            ck = pltpu.make_async_copy(k_hbm.at[pl.ds(pl.multiple_of(b * s, tf), s)], k_vm, sems.at[0])
            cv = pltpu.make_async_copy(vt_hbm.at[pl.ds(b * n, n)], vt_vm, sems.at[1])
            ck.start()
            cv.start()
            ck.wait()
            cv.wait()

        lane = lax.broadcasted_iota(jnp.int32, (tf, PAIR), 1)
        one = jnp.ones((tf, PAIR), BF16)
        zero = jnp.zeros((tf, PAIR), BF16)
        for p in range(N_PAIRS):
            qp = q_ref[:, p * PAIR:(p + 1) * PAIR]
            be, bo = _bias_lane(2 * p), _bias_lane(2 * p + 1)
            ones_e = jnp.where((lane >= be) & (lane < be + BIAS_TERMS), one, zero)
            ones_o = jnp.where((lane >= bo) & (lane < bo + BIAS_TERMS), one, zero)
            qx_scr[2 * p] = jnp.where(lane < HEAD_DIM, qp, ones_e)
            qx_scr[2 * p + 1] = jnp.where(lane < HEAD_DIM, ones_o, qp)
        m_scr[...] = jnp.full(m_scr.shape, NEG_BIG, F32)
        acc_scr[...] = jnp.zeros_like(acc_scr)

        def tile(j, masked):
            ks = pl.multiple_of(j * tf, tf)
            if masked:
                keep = lax.broadcasted_iota(jnp.int32, (tf, tf), 1) >= lax.broadcasted_iota(jnp.int32, (tf, tf), 0)
            def scores(h):
                kx = k_vm[pl.ds(ks, tf), h * PAIR:(h + 1) * PAIR]
                return lax.dot_general(kx, qx_scr[h], NT_DIMS, preferred_element_type=F32)

            def values(h, pr, a):
                pv = jnp.dot(vt_vm[j, h * PAIR:(h + 1) * PAIR, :], pr, preferred_element_type=F32)
                acc_scr[h] = a * acc_scr[h] + pv

            ahead = [scores(h) for h in range(LOOKAHEAD)]
            behind = []
            for h in range(FOX_HEADS):
                sc = ahead.pop(0)
                if h + LOOKAHEAD < FOX_HEADS:
                    ahead.append(scores(h + LOOKAHEAD))
                if masked:
                    sc = jnp.where(keep, sc, NEG_BIG)
                m_prev = m_scr[h]
                m_new = jnp.maximum(m_prev, jnp.max(sc, axis=0, keepdims=True))
                m_scr[h] = m_new
                behind.append((h, jnp.exp(sc - m_new).astype(BF16), jnp.exp(m_prev - m_new)))
                if len(behind) > FOLLOW_FWD:
                    values(*behind.pop(0))
            for item in behind:
                values(*item)

        def step(j, carry):
            tile(j, False)
            return carry

        lax.fori_loop(0, i, step, 0)
        tile(i, True)

        top = lax.broadcasted_iota(jnp.int32, (PAIR, tf), 0) < HEAD_DIM
        sub = lax.broadcasted_iota(jnp.int32, (16, tf), 0)
        lse = jnp.zeros((16, tf), F32)
        for p in range(N_PAIRS):
            he, ho = 2 * p, 2 * p + 1
            le = acc_scr[he, HEAD_DIM:HEAD_DIM + 1, :]
            lod = acc_scr[ho, 0:1, :]
            o = jnp.where(top, acc_scr[he] / le, acc_scr[ho] / lod).T
            ob_ref[:, p * PAIR:(p + 1) * PAIR] = o.astype(BF16)
            of_ref[:, p * PAIR:(p + 1) * PAIR] = o
            lse = jnp.where(sub == he, m_scr[he] + jnp.log(le), lse)
            lse = jnp.where(sub == ho, m_scr[ho] + jnp.log(lod), lse)
        lse_ref[0] = lse

    qrow = lambda b, i: (b * n + i, 0)
    return pl.pallas_call(
        body, name="fox_fwd", grid=(nb, n),
        in_specs=[pl.BlockSpec((tf, w), qrow), ANY_SPEC, ANY_SPEC],
        out_specs=[pl.BlockSpec((tf, w), qrow), pl.BlockSpec((tf, w), qrow),
                   pl.BlockSpec((1, 16, tf), lambda b, i: (b * n + i, 0, 0))],
        out_shape=[SDS((nb * s, w), BF16), SDS((nb * s, w), F32), SDS((nb * n, 16, tf), F32)],
        scratch_shapes=[pltpu.VMEM((s, wa), BF16), pltpu.VMEM((n, wa, tf), BF16),
                        pltpu.VMEM((FOX_HEADS, tf, PAIR), BF16), pltpu.VMEM((FOX_HEADS, 1, tf), F32),
                        pltpu.VMEM((FOX_HEADS, PAIR, tf), F32), pltpu.SemaphoreType.DMA((2,))],
        compiler_params=_params(("arbitrary", "arbitrary")))(pq, kaug, vaug_t)


def fox_delta(dcat, o, nb, s):
    tf = min(TM, s)
    w = TOK_WIDTH

    def body(do_ref, o_ref, dl_ref):
        out = jnp.zeros((tf, LANES), F32)
        for h in range(FOX_HEADS):
            lo, hi = h * HEAD_DIM, (h + 1) * HEAD_DIM
            out = out + _lane_put((tf, LANES), h, jnp.sum(do_ref[:, lo:hi] * o_ref[:, lo:hi], axis=1, keepdims=True))
        dl_ref[...] = out

    row = pl.BlockSpec((tf, w), lambda r: (r, 0))
    return pl.pallas_call(
        body, name="fox_delta", grid=(nb * s // tf,), in_specs=[row, row],
        out_specs=pl.BlockSpec((tf, LANES), lambda r: (r, 0)), out_shape=SDS((nb * s, LANES), F32),
        compiler_params=_params(("parallel",)))(dcat, o)


def fox_bwd(pq, kv, fneg, dcat_bf, lse_rows, delta_rows, nb, s):
    tf = min(TF, s)
    n = s // tf
    w = TOK_WIDTH

    def body(q_hbm, k_ref, v_ref, f_ref, do_hbm, lse_ref, dl_ref, dq_ref, dk_ref, dv_ref, dfk_ref, dfq_ref,
             q_vm, do_vm, km_scr, vm_scr, kt_scr, fk_scr, dk_scr, dv_scr, rs_scr, dq_scr, fq_scr, sems):
        b = pl.program_id(0)
        j = pl.program_id(1)

        @pl.when(j == 0)
        def _():
            rows = pl.ds(pl.multiple_of(b * s, tf), s)
            cq = pltpu.make_async_copy(q_hbm.at[rows, pl.ds(0, w)], q_vm, sems.at[0])
            cd = pltpu.make_async_copy(do_hbm.at[rows, pl.ds(0, w)], do_vm, sems.at[1])
            cq.start()
            cd.start()
            dq_scr[...] = jnp.zeros_like(dq_scr)
            fq_scr[...] = jnp.zeros_like(fq_scr)
            cq.wait()
            cd.wait()

        for p in range(N_PAIRS):
            kp = k_ref[:, p * PAIR:(p + 1) * PAIR] * QK_SCALE
            ke, ko = _split_pair(kp)
            km_scr[2 * p] = ke
            km_scr[2 * p + 1] = ko
            kt_scr[p] = kp.astype(F32).T.astype(BF16)
            ve, vo = _split_pair(v_ref[:, p * PAIR:(p + 1) * PAIR])
            vm_scr[2 * p] = ve
            vm_scr[2 * p + 1] = vo
        for h in range(FOX_HEADS):
            fk_scr[h] = jnp.broadcast_to(f_ref[:, h:h + 1], (tf, tf))
        dk_scr[...] = jnp.zeros_like(dk_scr)
        dv_scr[...] = jnp.zeros_like(dv_scr)
        rs_scr[...] = jnp.zeros_like(rs_scr)

        def tile(i, masked):
            qs = pl.multiple_of(i * tf, tf)
            if masked:
                keep = lax.broadcasted_iota(jnp.int32, (tf, tf), 1) >= lax.broadcasted_iota(jnp.int32, (tf, tf), 0)
            def products(h):
                qp = q_vm[pl.ds(qs, tf), (h // 2) * PAIR:(h // 2 + 1) * PAIR]
                dop = do_vm[pl.ds(qs, tf), (h // 2) * PAIR:(h // 2 + 1) * PAIR]
                return (lax.dot_general(km_scr[h], qp, NT_DIMS, preferred_element_type=F32),
                        lax.dot_general(vm_scr[h], dop, NT_DIMS, preferred_element_type=F32))

            def dependents(h, prb, dsb):
                p = h // 2
                half = slice((h % 2) * HEAD_DIM, (h % 2 + 1) * HEAD_DIM)
                qp = q_vm[pl.ds(qs, tf), p * PAIR:(p + 1) * PAIR]
                dop = do_vm[pl.ds(qs, tf), p * PAIR:(p + 1) * PAIR]
                dv_scr[h] += jnp.dot(prb, dop, preferred_element_type=F32)
                dk_scr[h] += jnp.dot(dsb, qp, preferred_element_type=F32)
                dqt = jnp.dot(kt_scr[p], dsb, preferred_element_type=F32)
                dq_scr[i, p, half, :] += dqt[(h % 2) * HEAD_DIM:(h % 2 + 1) * HEAD_DIM]

            ahead = [products(h) for h in range(LOOKAHEAD_BWD)]
            behind = []
            for h in range(FOX_HEADS):
                sc, dp = ahead.pop(0)
                if h + LOOKAHEAD_BWD < FOX_HEADS:
                    ahead.append(products(h + LOOKAHEAD_BWD))
                sc = sc + fk_scr[h] - lse_ref[i, h:h + 1, :]
                if masked:
                    sc = jnp.where(keep, sc, NEG_BIG)
                pr = jnp.exp(sc)
                ds = pr * (dp - dl_ref[i, h:h + 1, :])
                part = ds[:, :LANES]
                for c in range(1, tf // LANES):
                    part = part + ds[:, c * LANES:(c + 1) * LANES]
                rs_scr[h] += part
                fq_scr[i, h:h + 1, :] += jnp.sum(ds, axis=0, keepdims=True)
                behind.append((h, pr.astype(BF16), ds.astype(BF16)))
                if len(behind) > FOLLOW_BWD:
                    dependents(*behind.pop(0))
            for item in behind:
                dependents(*item)

        def step(i, carry):
            tile(i, False)
            return carry

        tile(j, True)
        for p in range(N_PAIRS):
            dq_ref[:, p * PAIR:(p + 1) * PAIR] = dq_scr[j, p].T.astype(BF16)
        dfq_ref[0] = fq_scr[j]
        lax.fori_loop(j + 1, n, step, 0)

        lo = _half_masks(tf)
        dfk = jnp.zeros((tf, LANES), F32)
        for p in range(N_PAIRS):
            dk = jnp.where(lo, dk_scr[2 * p], dk_scr[2 * p + 1]) * QK_SCALE
            dk_ref[:, p * PAIR:(p + 1) * PAIR] = dk.astype(BF16)
            dv_ref[:, p * PAIR:(p + 1) * PAIR] = jnp.where(lo, dv_scr[2 * p], dv_scr[2 * p + 1]).astype(BF16)
            for h in (2 * p, 2 * p + 1):
                dfk = dfk - _lane_put((tf, LANES), h, jnp.sum(rs_scr[h], axis=1, keepdims=True))
        dfk_ref[...] = dfk

    krow = lambda b, j: (b * n + j, 0)
    rows = pl.BlockSpec((n, 16, tf), lambda b, j: (b, 0, 0))
    tile_out = pl.BlockSpec((tf, w), krow)
    return pl.pallas_call(
        body, name="fox_bwd", grid=(nb, n),
        in_specs=[ANY_SPEC, pl.BlockSpec((tf, w), krow), pl.BlockSpec((tf, w), lambda b, j: (b * n + j, 1)),
                  pl.BlockSpec((tf, LANES), krow), ANY_SPEC, rows, rows],
        out_specs=[tile_out, tile_out, tile_out, pl.BlockSpec((tf, LANES), krow),
                   pl.BlockSpec((1, 16, tf), lambda b, j: (b * n + j, 0, 0))],
        out_shape=[SDS((nb * s, w), BF16), SDS((nb * s, w), BF16), SDS((nb * s, w), BF16), SDS((nb * s, LANES), F32),
                   SDS((nb * n, 16, tf), F32)],
        scratch_shapes=[pltpu.VMEM((s, w), BF16), pltpu.VMEM((s, w), BF16),
                        pltpu.VMEM((FOX_HEADS, tf, PAIR), BF16), pltpu.VMEM((FOX_HEADS, tf, PAIR), BF16),
                        pltpu.VMEM((N_PAIRS, PAIR, tf), BF16), pltpu.VMEM((FOX_HEADS, tf, tf), F32),
                        pltpu.VMEM((FOX_HEADS, tf, PAIR), F32), pltpu.VMEM((FOX_HEADS, tf, PAIR), F32),
                        pltpu.VMEM((FOX_HEADS, tf, LANES), F32), pltpu.VMEM((n, N_PAIRS, PAIR, tf), F32),
                        pltpu.VMEM((n, 16, tf), F32), pltpu.SemaphoreType.DMA((2,))],
        compiler_params=_params(("arbitrary", "arbitrary")))(pq, kv, kv, fneg, dcat_bf, lse_rows, delta_rows)


ADAMW_TILE_ELEMS = 128 * 1024


def reduce_adamw(parts, w, m, v, name):
    layers, r, c = w.shape
    tr = r
    for cand in range(16, r, 16):
        if r % cand == 0 and cand * c <= ADAMW_TILE_ELEMS:
            tr = cand
    c1 = 1.0 - ADAM_B1 ** ADAM_STEP
    c2 = 1.0 - ADAM_B2 ** ADAM_STEP

    def body(*refs):
        p_refs = refs[:layers]
        w_ref, m_ref, v_ref, g_out, d_out, m_out, v_out = refs[layers:]

        def update(p_ref):
            g = p_ref[0].astype(F32)
            for k in range(1, N_DEV):
                g = g + p_ref[k].astype(F32)
            mn = ADAM_B1 * m_ref[0] + (1.0 - ADAM_B1) * g
            vn = ADAM_B2 * v_ref[0] + (1.0 - ADAM_B2) * (g * g)
            g_out[0] = g
            m_out[0] = mn
            v_out[0] = vn
            d_out[0] = -ADAM_LR * ((mn / c1) / (jnp.sqrt(vn / c2) + ADAM_EPS) + ADAM_WD * w_ref[0])

        if layers == 1:
            update(p_refs[0])
        else:
            for layer in range(layers):
                pl.when(pl.program_id(0) == layer)(lambda layer=layer: update(p_refs[layer]))

    row = pl.BlockSpec((1, tr, c), lambda l, i: (l, i, 0))
    return pl.pallas_call(
        body, name=name, grid=(layers, r // tr),
        in_specs=[pl.BlockSpec((N_DEV, tr, c), lambda l, i: (0, i, 0))] * layers + [row, row, row],
        out_specs=[row, row, row, row], out_shape=[SDS((layers, r, c), F32)] * 4,
        compiler_params=_params(("parallel", "parallel")))(*parts, w, m, v)


N_PEERS = N_DEV - 1
HBM_SPEC = pl.BlockSpec(memory_space=pltpu.HBM)
SEM_SPEC = pl.BlockSpec(memory_space=pltpu.SEMAPHORE)
ANY_SPEC = pl.BlockSpec(memory_space=pl.ANY)
SPLIT_EFFECT = pltpu.SideEffectType.DATAFLOW_SIDE_EFFECTING


def _peers(with_self=False):
    x, y, c = lax.axis_index("x"), lax.axis_index("y"), lax.axis_index("c")
    peers = []
    for k in range(0 if with_self else 1, N_DEV):
        px = 1 - x if (k >> 2) & 1 else x
        py = 1 - y if (k >> 1) & 1 else y
        pc = 1 - c if k & 1 else c
        peers.append(((px, py, pc), 4 * px + 2 * py + pc))
    return 4 * x + 2 * y + c, peers


def _push(src, dst, send_sems, recv_sems, slot, dev):
    return pltpu.make_async_remote_copy(src_ref=src, dst_ref=dst, send_sem=send_sems.at[slot], recv_sem=recv_sems.at[slot],
                                        device_id=dev, device_id_type=pl.DeviceIdType.MESH)


def _landing_shapes(arrs, scatter):
    return [SDS((N_DEV,) + tuple(a.shape[1:] if sc else a.shape), a.dtype) for a, sc in zip(arrs, scatter)]


def exchange(arrs, scatter, name):
    na = len(arrs)

    def body(*refs):
        ins = refs[:na]
        outs = refs[na:2 * na]
        send_sems, recv_sems, local_sems = refs[2 * na:]
        me, peers = _peers()
        local = []
        remote = []
        for a in range(na):
            lc = pltpu.make_async_copy(ins[a].at[me] if scatter[a] else ins[a], outs[a].at[me], local_sems.at[a])
            lc.start()
            local.append(lc)
            for k, (dev, idx) in enumerate(peers):
                cp = _push(ins[a].at[idx] if scatter[a] else ins[a], outs[a].at[me], send_sems, recv_sems,
                           a * N_PEERS + k, dev)
                cp.start()
                remote.append(cp)
        for a in range(na):
            for k, (dev, idx) in enumerate(peers):
                _push(ins[a].at[me] if scatter[a] else ins[a], outs[a].at[idx], send_sems, recv_sems,
                      a * N_PEERS + k, dev).wait_recv()
        for cp in remote:
            cp.wait_send()
        for lc in local:
            lc.wait()

    return pl.pallas_call(
        body, name=name, in_specs=[HBM_SPEC] * na, out_specs=[HBM_SPEC] * na, out_shape=_landing_shapes(arrs, scatter),
        scratch_shapes=[pltpu.SemaphoreType.DMA((na * N_PEERS,)), pltpu.SemaphoreType.DMA((na * N_PEERS,)),
                        pltpu.SemaphoreType.DMA((na,))])(*arrs)


def exchange_start(arrs, scatter, after, name):
    na = len(arrs)
    lands = [lax.empty(l.shape, l.dtype) for l in _landing_shapes(arrs, scatter)]

    def body(*refs):
        ins = refs[:na]
        land = refs[na:2 * na]
        send_sems, recv_sems = refs[2 * na + 1], refs[2 * na + 2]
        token = refs[-1]
        me, peers = _peers(with_self=True)
        for a in range(na):
            for k, (dev, idx) in enumerate(peers):
                _push(ins[a].at[idx] if scatter[a] else ins[a], land[a].at[me], send_sems, recv_sems,
                      a * N_DEV + k, dev).start()
        token[...] = jnp.zeros_like(token)

    thru = [pltpu.HBM(a.shape, a.dtype) for a in arrs] + [pltpu.HBM(l.shape, l.dtype) for l in lands]
    res = pl.pallas_call(
        body, name=name,
        out_shape=(pltpu.SemaphoreType.DMA((na * N_DEV,)), pltpu.SemaphoreType.DMA((na * N_DEV,)), *thru,
                   SDS((8, LANES), F32)),
        in_specs=[HBM_SPEC] * (2 * na) + [ANY_SPEC],
        out_specs=(SEM_SPEC, SEM_SPEC, *([HBM_SPEC] * (2 * na)), pl.BlockSpec(memory_space=pltpu.VMEM)),
        input_output_aliases={i: 2 + i for i in range(2 * na)},
        compiler_params=pltpu.CompilerParams(has_side_effects=SPLIT_EFFECT),
    )(*[pltpu.with_memory_space_constraint(a, pltpu.HBM) for a in arrs],
      *[pltpu.with_memory_space_constraint(l, pltpu.HBM) for l in lands], after)
    return {"send": res[0], "recv": res[1], "src": res[2:2 + na], "land": res[2 + na:2 + 2 * na],
            "token": res[-1][0, 0], "scatter": scatter}


def exchange_wait(handle, after, name):
    scatter = handle["scatter"]
    na = len(scatter)

    def body(*refs):
        src = refs[:na]
        land = refs[na:2 * na]
        send_sems, recv_sems = refs[2 * na], refs[2 * na + 1]
        me, peers = _peers(with_self=True)
        for a in range(na):
            for k, (dev, idx) in enumerate(peers):
                cp = _push(src[a].at[me] if scatter[a] else src[a], land[a].at[idx], send_sems, recv_sems,
                           a * N_DEV + k, dev)
                cp.wait_send()
                cp.wait_recv()

    ops = list(handle["src"]) + list(handle["land"])
    res = pl.pallas_call(
        body, name=name, out_shape=tuple(pltpu.HBM(o.shape, o.dtype) for o in ops),
        in_specs=[HBM_SPEC] * (2 * na) + [SEM_SPEC, SEM_SPEC, ANY_SPEC], out_specs=tuple([HBM_SPEC] * (2 * na)),
        input_output_aliases={i: i for i in range(2 * na)},
        compiler_params=pltpu.CompilerParams(has_side_effects=SPLIT_EFFECT),
    )(*ops, handle["send"], handle["recv"], after)
    return list(res[na:])


def forward_layer(l, xin, xin_bf, mem_bf, wt, nb, s, ffn_weights=None):
    sv = {"xin_bf": xin_bf}
    memkv = mm_nn(mem_bf, wt["memw"], BF16, f"memkv{l}")
    sv["memkv"] = memkv
    if l == 0:
        proj = mm_nn(xin_bf, wt["win_a"], F32, "proj_a")
        pooled, tok = pool_fwd(proj, wt["pw_bd"], wt["pscale"], nb, s)
        sv["pooled"] = pooled
    else:
        kv = mm_nn(xin_bf, wt["kvw"][:, :2 * TOK_WIDTH], BF16, "kv_proj")
        fl = mm_nn(xin_bf, wt["kvw"][:, 2 * TOK_WIDTH:], F32, "gate_proj")
        fneg = -fgate_fwd(fl, wt["fb"], nb, s)
        proj = mm_nn(xin_bf, wt["wq"], BF16, "proj_b")
        kaug, vaug_t = fox_prep(kv, fneg, nb, s)
        tok, o_f32, lse_rows = fox_fwd_t(proj, kaug, vaug_t, nb, s)
        sv.update(kv=kv, fl=fl, fneg=fneg, o_f32=o_f32, lse_rows=lse_rows)
    sv["proj"] = proj
    mem_out = memattn_fwd(proj, memkv, nb, s, f"memattn_fwd{l}")
    cat = jnp.concatenate([tok, mem_out], axis=1)
    sv["cat"] = cat
    x1, x1_bf, xh1, rs1 = ln_fwd(xin, cat, wt["wout"], wt["ln1_g"], wt["ln1_b"], f"out_proj_ln1_{l}")
    sv.update(x1_bf=x1_bf, xh1=xh1, rs1=rs1)
    if ffn_weights is not None:
        wt.update(ffn_weights(x1_bf))
    act, ga, gb, hu, hg = ffn_up_gate(x1_bf, wt["wup"], wt["cw"], nb, s, f"ffn_up_gate{l}")
    sv.update(act=act, ga=ga, gb=gb, hu=hu, hg=hg)
    x2, x2_bf, xh2, rs2 = ln_fwd(x1, act, wt["wdown"], wt["ln2_g"], wt["ln2_b"], f"ffn_down_ln2_{l}")
    sv.update(xh2=xh2, rs2=rs2)
    return x2, x2_bf, sv


def backward_layer(l, dy, sv, mem_bf, wt, nb, s, after_ffn=None, after_pool=None, loss_target=None):
    g = {}
    if loss_target is None:
        dr2, dr2_bf, g["ln2_g"], g["ln2_b"] = ln_bwd(dy[0], sv["xh2"], sv["rs2"], wt["ln2_g"], f"ln2_bwd{l}",
                                                     dy_scale=dy[1], products=dy[2])
    else:
        dr2, dr2_bf, g["ln2_g"], g["ln2_b"], g["loss_row"] = loss_ln_bwd(sv["xh2"], sv["rs2"], wt["ln2_g"], wt["ln2_b"],
                                                                         loss_target, f"loss_ln2_bwd{l}")
    dact = mm_nn(dr2_bf, wt["wdown"], BF16, f"ffn_down_dx{l}", trans_b=0)
    g["wdown"] = mm_tn(sv["act"], dr2_bf, f"ffn_down_dw{l}")
    dh_u, dh_g, dcw_u, dcw_g = gate_conv_bwd(dact, sv["ga"], sv["gb"], sv["hu"], sv["hg"], wt["cw"], nb, s,
                                             f"gate_conv_bwd{l}")
    g["cw"] = jnp.concatenate([dcw_u, dcw_g], axis=0)
    g["wup"] = jnp.concatenate([mm_tn(sv["x1_bf"], dh_u, f"ffn_up_dw_u{l}", blocked=True),
                                mm_tn(sv["x1_bf"], dh_g, f"ffn_up_dw_g{l}", blocked=True)], axis=0)
    ln1_g = wt["ln1_g"] if after_ffn is None else wt["ln1_g"] + after_ffn(g, dr2)
    dr1, dr1_bf, g["ln1_g"], g["ln1_b"] = ln_bwd(dr2, sv["xh1"], sv["rs1"], ln1_g, f"ffn_up_dx_ln1_bwd{l}",
                                                 dy_scale=DN_ALPHA, products=[(dh_u, wt["wup"], 0), (dh_g, wt["wup"], 1)])
    dcat, dcat_bf = mm_nn(dr1_bf, wt["wout"], F32, f"out_proj_dx{l}", also_bf16=True, trans_b=0)
    g["wout"] = mm_tn(sv["cat"], dr1_bf, f"out_proj_dw{l}")
    dqm, dmemkv = memattn_bwd(sv["proj"], sv["memkv"], dcat, nb, s, f"memattn_bwd{l}")
    g["memw"] = mm_tn(mem_bf, dmemkv, f"memkv_dw{l}")
    if l == 0:
        dmixed, dpooled, g["pscale"] = pool_bwd_mix(dcat, sv["pooled"], wt["pw_bd"], wt["pscale"], nb, s)
        g["pw_full"] = mm_tn(sv["pooled"], dmixed, "pool_dw")
        win_a = wt["win_a"] if after_pool is None else wt["win_a"] + after_pool(g, dmixed).astype(BF16)
        du = pool_bwd_window(dpooled, nb, s)
        dproj = jnp.concatenate([du, dqm], axis=1)
        dx = mm_nn(dproj, win_a, F32, "proj_a_dx", addend=dr1, add_scale=DN_ALPHA, trans_b=0)
        g["win_a"] = mm_tn(sv["xin_bf"], dproj, "proj_a_dw")
    else:
        delta = fox_delta(dcat, sv["o_f32"], nb, s)
        tf = min(TF, s)
        dq, dk, dv, dfcum_k, dfq_rows = fox_bwd(sv["proj"], sv["kv"], sv["fneg"], dcat_bf,
                                                sv["lse_rows"], _to_tile_rows(delta, nb, s, tf), nb, s)
        dfl, g["fb"] = fgate_bwd(_from_tile_rows(dfq_rows), dfcum_k, sv["fl"], wt["fb"], nb, s)
        dproj = jnp.concatenate([dq, dqm], axis=1)
        dkvf = jnp.concatenate([dk, dv, dfl.astype(BF16)], axis=1)
        dx = (dr1, DN_ALPHA, [(dproj, wt["wq"], 0), (dkvf, wt["kvw"], 0)])
        g["wq"] = mm_tn(sv["xin_bf"], dproj, "proj_b_dw")
        g["kvw"] = mm_tn(sv["xin_bf"], dkvf, "kv_proj_dw")
    return dx, g


def pack_replicated(pool_w, ln1_g, ln1_b, ln2_g, ln2_b, conv_b, f_b):
    cb = jnp.pad(conv_b, ((0, 0), (0, 6144 - 5504))).reshape(12, D_MODEL)
    fb = jnp.pad(f_b.reshape(1, FOX_HEADS), ((0, 3), (0, D_MODEL - FOX_HEADS)))
    return jnp.concatenate([pool_w.reshape(144, D_MODEL), ln1_g, ln1_b, ln2_g, ln2_b, cb, fb], axis=0)


def unpack_replicated(buf):
    pool_w = buf[:144].reshape(1, 4, POOL_GROUP, POOL_GROUP)
    ln = [buf[144 + 2 * k:146 + 2 * k] for k in range(4)]
    conv_b = buf[152:164].reshape(2, 6144)[:, :5504]
    f_b = buf[164, :FOX_HEADS]
    return pool_w, ln[0], ln[1], ln[2], ln[3], conv_b, f_b


def _pad_ff(a, axis):
    zeros = jnp.zeros(a.shape[:axis] + (FF_ROWS_PAD - FF_ROWS,) + a.shape[axis + 1:], a.dtype)
    halves = [lax.slice_in_dim(a, h * FF_ROWS, (h + 1) * FF_ROWS, axis=axis) for h in range(2)]
    return jnp.concatenate([halves[0], zeros, halves[1], zeros], axis=axis)


def _unpad_ff(a, axis):
    return jnp.concatenate([lax.slice_in_dim(a, h * FF_ROWS_PAD, h * FF_ROWS_PAD + FF_ROWS, axis=axis) for h in range(2)],
                           axis=axis)


def pack_small(conv_w, pool_scale):
    buf = jnp.zeros((16, FF_BLOCK_PAD), F32)
    buf = lax.dynamic_update_slice(buf, _pad_ff(conv_w.reshape(DEPTH * 3, FF_BLOCK), 1), (0, 0))
    return lax.dynamic_update_slice(buf, pool_scale, (8, 0))


def _block_diag(pw):
    out = jnp.zeros((TOK_WIDTH, TOK_WIDTH), pw.dtype)
    for g in range(4):
        out = lax.dynamic_update_slice(out, pw[g], (g * POOL_GROUP, g * POOL_GROUP))
    return out


def layer_shards(l, sq_a, sq_b, mem_w_kv, ffn_w_up, ffn_w_down):
    wdown = jnp.pad(ffn_w_down[l], ((0, FF_ROWS_PAD - FF_ROWS), (0, 0)))
    return [sq_a[0].astype(BF16), sq_b[0].astype(BF16), mem_w_kv[l].astype(BF16), _pad_ff(ffn_w_up[l], 1).astype(BF16),
            wdown.astype(BF16)]


def mixer_weights(l, gath, ln1_g, ln1_b, ln2_g, ln2_b):
    w_out = gath[1].reshape(D_MODEL, D_MODEL)
    wt = {"memw": gath[2].reshape(D_MODEL, 2 * MEM_WIDTH), "wout": w_out,
          "ln1_g": ln1_g[l:l + 1], "ln1_b": ln1_b[l:l + 1], "ln2_g": ln2_g[l:l + 1], "ln2_b": ln2_b[l:l + 1]}
    return wt, gath[0].reshape(D_MODEL, D_MODEL)


def ffn_weights(l, wup_g, wdown_g, small, conv_b):
    cb = _pad_ff(conv_b[l].reshape(N_DEV, FF_BLOCK), 1)
    cw = jnp.concatenate([small[:, 3 * l:3 * l + 3, :], cb[:, None, :], jnp.zeros((N_DEV, 4, FF_BLOCK_PAD), F32)], axis=1)
    return {"wup": wup_g, "wdown": wdown_g.reshape(FF_PAIRS * FF_BLOCK_PAD, D_MODEL), "cw": cw}


def mixer_grad_blocks(g, w_in_grad):
    blocks = [] if w_in_grad is None else [w_in_grad.reshape(N_DEV, 128, D_MODEL)]
    blocks += [g["wout"].reshape(N_DEV, 128, D_MODEL), g["memw"].reshape(N_DEV, 128, 2 * MEM_WIDTH)]
    return [b.astype(BF16) for b in blocks]


def ffn_grad_blocks(g):
    wdown = g["wdown"].reshape(N_DEV, FF_ROWS_PAD, D_MODEL)[:, :FF_ROWS]
    return [_unpad_ff(g["wup"], 2).astype(BF16), wdown.astype(BF16)]


def small_grad_blocks(g0, g1):
    taps = jnp.stack([g0["cw"][:, :3, :], g1["cw"][:, :3, :]], axis=1).reshape(N_DEV, DEPTH * 3, FF_BLOCK_PAD)
    small = jnp.zeros((N_DEV, 16, FF_BLOCK_PAD), F32)
    small = lax.dynamic_update_slice(small, taps, (0, 0, 0))
    return lax.dynamic_update_slice(small, g0["pscale"].reshape(N_DEV, 1, 96), (0, 8, 0))


def replicated_grads(g0, g1):
    pw = jnp.stack([g0["pw_full"][k * POOL_GROUP:(k + 1) * POOL_GROUP, k * POOL_GROUP:(k + 1) * POOL_GROUP] for k in range(4)])
    conv_b = jnp.stack([_unpad_ff(g_["cw"][:, 3, :], 1).reshape(N_DEV * FF_BLOCK) for g_ in (g0, g1)])
    ln = [jnp.concatenate([g0[n], g1[n]], axis=0) for n in ("ln1_g", "ln1_b", "ln2_g", "ln2_b")]
    return pack_replicated(pw[None], ln[0], ln[1], ln[2], ln[3], conv_b, g1["fb"][0, :FOX_HEADS])


def kernel(x, mem, a_w_in, a_pool_w, a_pool_scale, a_w_out, b_w_q, b_w_out, kv_w, f_b, mem_w_kv, ln1_g, ln1_b, ln2_g, ln2_b, ffn_w_up, ffn_conv_w, ffn_conv_b, ffn_w_down, loss_target, m_a_w_in, m_a_pool_w, m_a_pool_scale, m_a_w_out, m_b_w_q, m_b_w_out, m_kv_w, m_f_b, m_mem_w_kv, m_ln1_g, m_ln1_b, m_ln2_g, m_ln2_b, m_ffn_w_up, m_ffn_conv_w, m_ffn_conv_b, m_ffn_w_down, v_a_w_in, v_a_pool_w, v_a_pool_scale, v_a_w_out, v_b_w_q, v_b_w_out, v_kv_w, v_f_b, v_mem_w_kv, v_ln1_g, v_ln1_b, v_ln2_g, v_ln2_b, v_ffn_w_up, v_ffn_conv_w, v_ffn_conv_b, v_ffn_w_down):
    nb, s, d = x.shape
    t = nb * s
    x2d, mem_bf, target = x.reshape(t, d), mem.reshape(nb * MEM_LEN, d).astype(BF16), loss_target.reshape(t, d)

    shards0 = layer_shards(0, a_w_in, a_w_out, mem_w_kv, ffn_w_up, ffn_w_down)
    shards1 = layer_shards(1, b_w_q, b_w_out, mem_w_kv, ffn_w_up, ffn_w_down)
    shards1.append(jnp.pad(kv_w, ((0, 0), (0, KV_COLS_PAD - KV_COLS))).astype(BF16))
    gath0 = exchange(shards0[:3] + [pack_small(ffn_conv_w, a_pool_scale)], [False] * 4, "gather_w0_mixer")
    pending = {"ffn0": exchange_start(shards0[3:], [False] * 2, gath0[0], "gather_w0_ffn_start")}
    small = gath0[3]
    wt0, w_in = mixer_weights(0, gath0, ln1_g + pending["ffn0"]["token"], ln1_b, ln2_g, ln2_b)
    pw_bd = _block_diag(a_pool_w[0])
    wt0.update(win_a=w_in, pw_bd=pw_bd.astype(BF16),
               pscale=small[:, 8, :96].reshape(1, TOK_WIDTH) + pending["ffn0"]["token"])

    def ffn0_weights(x1_bf):
        got = exchange_wait(pending["ffn0"], x1_bf, "gather_w0_ffn_wait")
        pending["w1"] = exchange_start(shards1, [False] * 6, got[0], "gather_w1_start")
        w = ffn_weights(0, got[0], got[1], small, ffn_conv_b)
        w["cw"] = w["cw"] + pending["w1"]["token"]
        return w

    x1, x1_bf, sv0 = forward_layer(0, x2d, x2d, mem_bf, wt0, nb, s, ffn_weights=ffn0_weights)
    gath1 = exchange_wait(pending["w1"], x1_bf, "gather_w1_wait")
    wt1, w_q = mixer_weights(1, gath1, ln1_g, ln1_b, ln2_g, ln2_b)
    wt1.update(ffn_weights(1, gath1[3], gath1[4], small, ffn_conv_b))
    kvw = gath1[5].reshape(D_MODEL, KV_COLS_PAD)
    wt1.update(wq=w_q, kvw=kvw,
               fb=jnp.pad(f_b.reshape(1, FOX_HEADS), ((0, 0), (0, LANES - FOX_HEADS))))
    _, _, sv1 = forward_layer(1, x1, x1_bf, mem_bf, wt1, nb, s)

    dx1, g1 = backward_layer(1, None, sv1, mem_bf, wt1, nb, s, loss_target=target)
    loss = lax.psum(g1["loss_row"][0, 0], ("x", "y", "c"))
    blocks1 = (mixer_grad_blocks(g1, g1["wq"]) + ffn_grad_blocks(g1)
               + [g1["kvw"][:, :KV_COLS].reshape(N_DEV, 128, KV_COLS).astype(BF16)])
    pending["g1"] = exchange_start(blocks1, [True] * 6, dx1[0], "scatter_g1_start")
    wt0["ln2_g"] = wt0["ln2_g"] + pending["g1"]["token"]

    def after_ffn0(g, dxm):
        pending["gf0"] = exchange_start(ffn_grad_blocks(g), [True] * 2, dxm, "scatter_g0_ffn_start")
        return pending["gf0"]["token"]

    def after_pool0(g, x):
        blocks = mixer_grad_blocks(g, None) + [small_grad_blocks(g, g1), replicated_grads(g, g1)]
        pending["gm0"] = exchange_start(blocks, [True] * 3 + [False], x, "scatter_g0_mixer_start")
        return pending["gm0"]["token"]

    grad_x, g0 = backward_layer(0, dx1, sv0, mem_bf, wt0, nb, s, after_ffn=after_ffn0, after_pool=after_pool0)
    pending["gin"] = exchange_start([g0["win_a"].reshape(N_DEV, 128, D_MODEL).astype(BF16)], [True], grad_x,
                                    "scatter_g0_in_start")
    parts_f0 = exchange_wait(pending["gf0"], jnp.zeros((8, LANES), F32) + pending["gin"]["token"], "scatter_g0_ffn_wait")
    parts1 = exchange_wait(pending["g1"], parts_f0[0], "scatter_g1_wait")

    res = {}

    def upd(nm, parts, w2, m2, v2):
        res[nm] = reduce_adamw(parts, w2, m2, v2, f"adamw_{nm}")

    upd("b_w_q", [parts1[0]], b_w_q, m_b_w_q, v_b_w_q)
    upd("b_w_out", [parts1[1]], b_w_out, m_b_w_out, v_b_w_out)
    upd("kv_w", [parts1[5]], kv_w[None], m_kv_w[None], v_kv_w[None])
    upd("ffn_w_up", [parts_f0[0], parts1[3]], ffn_w_up, m_ffn_w_up, v_ffn_w_up)
    upd("ffn_w_down", [parts_f0[1], parts1[4]], ffn_w_down, m_ffn_w_down, v_ffn_w_down)
    parts_m0 = exchange_wait(pending["gm0"], res["ffn_w_down"][0], "scatter_g0_mixer_wait")
    parts_in = exchange_wait(pending["gin"], parts_m0[0], "scatter_g0_in_wait")
    upd("a_w_in", [parts_in[0]], a_w_in, m_a_w_in, v_a_w_in)
    upd("a_w_out", [parts_m0[0]], a_w_out, m_a_w_out, v_a_w_out)
    upd("mem_w_kv", [parts_m0[1], parts1[2]], mem_w_kv, m_mem_w_kv, v_mem_w_kv)
    upd("small", [parts_m0[2]], pack_small(ffn_conv_w, a_pool_scale)[None], pack_small(m_ffn_conv_w, m_a_pool_scale)[None],
        pack_small(v_ffn_conv_w, v_a_pool_scale)[None])
    upd("replicated", [parts_m0[3]], pack_replicated(a_pool_w, ln1_g, ln1_b, ln2_g, ln2_b, ffn_conv_b, f_b)[None],
        pack_replicated(m_a_pool_w, m_ln1_g, m_ln1_b, m_ln2_g, m_ln2_b, m_ffn_conv_b, m_f_b)[None],
        pack_replicated(v_a_pool_w, v_ln1_g, v_ln1_b, v_ln2_g, v_ln2_b, v_ffn_conv_b, v_f_b)[None])

    res["kv_w"] = [o[0] for o in res["kv_w"]]
    res["ffn_conv_w"] = [_unpad_ff(o[0, :DEPTH * 3, :], 1).reshape(DEPTH, 3, FF_BLOCK) for o in res["small"]]
    res["a_pool_scale"] = [o[0, 8:9, :96] for o in res["small"]]
    rep_names = ["a_pool_w", "ln1_g", "ln1_b", "ln2_g", "ln2_b", "ffn_conv_b", "f_b"]
    for nm in rep_names:
        res[nm] = []
    for o in res["replicated"]:
        for nm, val in zip(rep_names, unpack_replicated(o[0])):
            res[nm].append(val)

    order = ["a_w_in", "a_pool_w", "a_pool_scale", "a_w_out", "b_w_q", "b_w_out", "kv_w", "f_b", "mem_w_kv",
             "ln1_g", "ln1_b", "ln2_g", "ln2_b", "ffn_w_up", "ffn_conv_w", "ffn_conv_b", "ffn_w_down"]
    out = [loss, grad_x.reshape(nb, s, d)]
    for kind in range(4):
        out.extend(res[nm][kind] for nm in order)
    return tuple(out)
```

```python
import jax
import jax.numpy as jnp
from jax import lax
from jax.experimental import pallas as pl
from jax.experimental.pallas import tpu as pltpu

F32 = jnp.float32
BF16 = jnp.bfloat16
SDS = jax.ShapeDtypeStruct

N_DEV = 8
D_MODEL = 1024
TOK_WIDTH = 768
MEM_WIDTH = 256
MEM_LEN = 256
MEM_HEADS = 4
HEAD_DIM = 64
FOX_HEADS = 12
POOL_GROUP = 192
FF_BLOCK = 688
FF_BLOCK_PAD = 768
FF_PAIRS = 4
FF_ROWS = 344
FF_ROWS_PAD = FF_BLOCK_PAD // 2
KV_COLS = 1548
KV_COLS_PAD = 1664
LANES = 128
DEPTH = 2
DN_ALPHA = (2.0 * DEPTH) ** 0.25
LN_EPS = 1e-5
QK_SCALE = HEAD_DIM ** -0.5
NEG_BIG = -1e30

ADAM_LR = 0.001
ADAM_B1 = 0.9
ADAM_B2 = 0.999
ADAM_EPS = 1e-08
ADAM_WD = 0.01
ADAM_STEP = 10

VMEM_LIMIT_BYTES = 56 * 1024 * 1024
MM_BLOCK_BYTES = 6 * 1024 * 1024
TM = 512
TS = 256
TF = 256
TC = 256
HALO_POOL = 16
HALO_CONV = 8

NT_DIMS = (((1,), (1,)), ((), ()))
TN_DIMS = (((0,), (0,)), ((), ()))


def _params(sem=None):
    return pltpu.CompilerParams(dimension_semantics=sem, vmem_limit_bytes=VMEM_LIMIT_BYTES)


def _sigmoid(z):
    return 1.0 / (1.0 + jnp.exp(-z))


def _pick_tn(n):
    if n <= 2048:
        return n
    for t in (1024, 768, 512, 256, 128):
        if n % t == 0:
            return t
    return n


def mm_nn(a, b, out_dtype, name, addend=None, add_scale=1.0, also_bf16=False, trans_b=None):
    m, k = a.shape
    n = b.shape[1] if trans_b is None else b.shape[0]
    tm = min(TM, m)
    tn = n
    while k * tn * 2 > MM_BLOCK_BYTES or tm * tn * 4 > MM_BLOCK_BYTES:
        tn //= 2
    chunk = tn if tn <= 2048 else _pick_tn(tn)
    has_add = addend is not None

    def body(*refs):
        a_ref, b_ref = refs[0], refs[1]
        c_ref = refs[2] if has_add else None
        o_ref = refs[3] if has_add else refs[2]
        ob_ref = refs[-1] if also_bf16 else None
        av = a_ref[...].astype(BF16)
        for c in range(tn // chunk):
            cols = slice(c * chunk, (c + 1) * chunk)
            if trans_b is None:
                r = jnp.dot(av, b_ref[:, cols].astype(BF16), preferred_element_type=F32)
            else:
                r = lax.dot_general(av, b_ref[cols, :].astype(BF16), NT_DIMS, preferred_element_type=F32)
            if has_add:
                r = r + add_scale * c_ref[:, cols]
            o_ref[:, cols] = r.astype(out_dtype)
            if also_bf16:
                ob_ref[:, cols] = r.astype(BF16)

    b_spec = (pl.BlockSpec((k, tn), lambda j, i: (0, j)) if trans_b is None
              else pl.BlockSpec((tn, k), lambda j, i: (j, trans_b)))
    in_specs = [pl.BlockSpec((tm, k), lambda j, i: (i, 0)), b_spec]
    ops = [a, b]
    tile = pl.BlockSpec((tm, tn), lambda j, i: (i, j))
    if has_add:
        in_specs.append(tile)
        ops.append(addend)
    out_shape = [SDS((m, n), out_dtype)]
    out_specs = [tile]
    if also_bf16:
        out_shape.append(SDS((m, n), BF16))
        out_specs.append(tile)
    res = pl.pallas_call(
        body, name=name, grid=(n // tn, m // tm), in_specs=in_specs, out_specs=out_specs, out_shape=out_shape,
        compiler_params=_params(("parallel", "parallel")))(*ops)
    return tuple(res) if also_bf16 else res[0]


def mm_tn(a, b, name, blocked=False):
    t, m = a.shape
    _, n = b.shape
    tt = min(4 * TM, t)
    tm = 1024 if m % 1024 == 0 else m
    tn = FF_BLOCK_PAD if blocked else _pick_tn(n)
    nt = t // tt

    def body(a_ref, b_ref, o_ref):
        kk = pl.program_id(2)
        r = lax.dot_general(a_ref[...].astype(BF16), b_ref[...].astype(BF16), TN_DIMS, preferred_element_type=F32)
        if blocked:
            r = r[None]

        @pl.when(kk == 0)
        def _():
            o_ref[...] = r

        @pl.when(kk != 0)
        def _():
            o_ref[...] += r

    if blocked:
        out_shape = SDS((n // tn, m, tn), F32)
        out_spec = pl.BlockSpec((1, tm, tn), lambda i, j, kk: (j, i, 0))
    else:
        out_shape = SDS((m, n), F32)
        out_spec = pl.BlockSpec((tm, tn), lambda i, j, kk: (i, j))
    return pl.pallas_call(
        body, name=name, grid=(m // tm, n // tn, nt),
        in_specs=[pl.BlockSpec((tt, tm), lambda i, j, kk: (kk, i)), pl.BlockSpec((tt, tn), lambda i, j, kk: (kk, j))],
        out_specs=out_spec, out_shape=out_shape,
        compiler_params=_params(("parallel", "parallel", "arbitrary")))(a, b)


def ln_fwd(xprev, a, w, g, b, name):
    t, d = xprev.shape
    k = a.shape[1]
    tm = min(TM, t)

    def body(xp_ref, a_ref, w_ref, g_ref, b_ref, y_ref, yb_ref, xh_ref, rs_ref):
        r = DN_ALPHA * xp_ref[...] + jnp.dot(a_ref[...], w_ref[...], preferred_element_type=F32)
        mu = jnp.mean(r, axis=1, keepdims=True)
        xc = r - mu
        var = jnp.mean(xc * xc, axis=1, keepdims=True)
        rstd = lax.rsqrt(var + LN_EPS)
        xh = xc * rstd
        y = xh * g_ref[...] + b_ref[...]
        y_ref[...] = y
        yb_ref[...] = y.astype(BF16)
        xh_ref[...] = xh
        rs_ref[...] = jnp.broadcast_to(rstd, (tm, LANES))

    row = pl.BlockSpec((tm, d), lambda i: (i, 0))
    vec = pl.BlockSpec((1, d), lambda i: (0, 0))
    return pl.pallas_call(
        body, name=name, grid=(t // tm,),
        in_specs=[row, pl.BlockSpec((tm, k), lambda i: (i, 0)), pl.BlockSpec((k, d), lambda i: (0, 0)), vec, vec],
        out_specs=[row, row, row, pl.BlockSpec((tm, LANES), lambda i: (i, 0))],
        out_shape=[SDS((t, d), F32), SDS((t, d), BF16), SDS((t, d), F32), SDS((t, LANES), F32)],
        compiler_params=_params(("parallel",)))(xprev, a, w, g, b)


def ln_bwd(dy, xhat, rstd, g, name, products=(), dy_scale=1.0):
    t, d = dy.shape
    np_ = len(products)
    tm = min(TM if sum(a.shape[1] for a, _, _ in products) <= 4096 else TS, t)

    def body(*refs):
        prod_refs = refs[:2 * np_]
        dy_ref, xh_ref, rs_ref, g_ref, dr_ref, drb_ref, dg_ref, db_ref = refs[2 * np_:]
        i = pl.program_id(0)
        dyv = dy_ref[...] if dy_scale == 1.0 else dy_scale * dy_ref[...]
        for p in range(np_):
            a_ref, w_ref = prod_refs[2 * p], prod_refs[2 * p + 1]
            if len(w_ref.shape) == 2:
                dyv = dyv + lax.dot_general(a_ref[...], w_ref[...], NT_DIMS, preferred_element_type=F32)
            else:
                kb = w_ref.shape[2]
                for c in range(w_ref.shape[0]):
                    dyv = dyv + lax.dot_general(a_ref[:, c * kb:(c + 1) * kb], w_ref[c], NT_DIMS,
                                                preferred_element_type=F32)
        xh = xh_ref[...]
        dxh = dyv * g_ref[...]
        m1 = jnp.mean(dxh, axis=1, keepdims=True)
        m2 = jnp.mean(dxh * xh, axis=1, keepdims=True)
        dr = rs_ref[:, 0:1] * (dxh - m1 - xh * m2)
        dr_ref[...] = dr
        drb_ref[...] = dr.astype(BF16)

        @pl.when(i == 0)
        def _():
            dg_ref[...] = jnp.zeros_like(dg_ref)
            db_ref[...] = jnp.zeros_like(db_ref)

        dg_ref[...] += jnp.sum(dyv * xh, axis=0, keepdims=True)
        db_ref[...] += jnp.sum(dyv, axis=0, keepdims=True)

    row = pl.BlockSpec((tm, d), lambda i: (i, 0))
    vec = pl.BlockSpec((1, d), lambda i: (0, 0))
    in_specs = [row, row, pl.BlockSpec((tm, LANES), lambda i: (i, 0)), vec]
    ops = [dy, xhat, rstd, g]
    for a, w, col in reversed(products):
        k = a.shape[1]
        if w.ndim == 2:
            w_spec = pl.BlockSpec((d, k), lambda i, col=col: (0, col))
        else:
            w_spec = pl.BlockSpec((k // w.shape[2], d, w.shape[2]), lambda i, col=col: (col, 0, 0))
        in_specs = [pl.BlockSpec((tm, k), lambda i: (i, 0)), w_spec] + in_specs
        ops = [a, w] + ops
    return pl.pallas_call(
        body, name=name, grid=(t // tm,), in_specs=in_specs, out_specs=[row, row, vec, vec],
        out_shape=[SDS((t, d), F32), SDS((t, d), BF16), SDS((1, d), F32), SDS((1, d), F32)],
        compiler_params=_params(("arbitrary",)))(*ops)


def loss_ln_bwd(xhat, rstd, g, beta, target, name):
    t, d = xhat.shape
    tm = min(TM, t)
    nsteps = t // tm

    def body(xh_ref, rs_ref, g_ref, b_ref, t_ref, dr_ref, drb_ref, dg_ref, db_ref, l_ref, acc):
        i = pl.program_id(0)
        xh = xh_ref[...]
        diff = xh * g_ref[...] + b_ref[...] - t_ref[...]
        dyv = diff * (1.0 / d)
        dxh = dyv * g_ref[...]
        m1 = jnp.mean(dxh, axis=1, keepdims=True)
        m2 = jnp.mean(dxh * xh, axis=1, keepdims=True)
        dr = rs_ref[:, 0:1] * (dxh - m1 - xh * m2)
        dr_ref[...] = dr
        drb_ref[...] = dr.astype(BF16)

        @pl.when(i == 0)
        def _():
            dg_ref[...] = jnp.zeros_like(dg_ref)
            db_ref[...] = jnp.zeros_like(db_ref)
            acc[...] = jnp.zeros_like(acc)

        dg_ref[...] += jnp.sum(dyv * xh, axis=0, keepdims=True)
        db_ref[...] += jnp.sum(dyv, axis=0, keepdims=True)
        acc[...] += jnp.sum(diff * diff, axis=0, keepdims=True)

        @pl.when(i == nsteps - 1)
        def _():
            tot = jnp.sum(acc[...], axis=1, keepdims=True) * (0.5 / d)
            l_ref[...] = jnp.broadcast_to(tot, (1, LANES))

    row = pl.BlockSpec((tm, d), lambda i: (i, 0))
    vec = pl.BlockSpec((1, d), lambda i: (0, 0))
    return pl.pallas_call(
        body, name=name, grid=(nsteps,),
        in_specs=[row, pl.BlockSpec((tm, LANES), lambda i: (i, 0)), vec, vec, row],
        out_specs=[row, row, vec, vec, pl.BlockSpec((1, LANES), lambda i: (0, 0))],
        out_shape=[SDS((t, d), F32), SDS((t, d), BF16), SDS((1, d), F32), SDS((1, d), F32), SDS((1, LANES), F32)],
        scratch_shapes=[pltpu.VMEM((1, d), F32)],
        compiler_params=_params(("arbitrary",)))(xhat, rstd, g, beta, target)


def memattn_fwd(proj, memkv, tok, nb, s, name):
    ts = min(TM, s)
    nq = s // ts

    def body(q_ref, kv_ref, tok_ref, o_ref):
        o_ref[:, :TOK_WIDTH] = tok_ref[...]
        top = lax.broadcasted_iota(jnp.int32, (PAIR, ts), 0) < HEAD_DIM
        scores = []
        for p in range(MEM_HEADS // 2):
            qp = q_ref[:, p * PAIR:(p + 1) * PAIR].astype(BF16)
            ke, ko = _split_pair(kv_ref[:, p * PAIR:(p + 1) * PAIR], QK_SCALE)
            scores.append([lax.dot_general(km, qp, NT_DIMS, preferred_element_type=F32) for km in (ke, ko)])
        for p in range(MEM_HEADS // 2):
            vt = kv_ref[:, MEM_WIDTH + p * PAIR:MEM_WIDTH + (p + 1) * PAIR].astype(F32).T.astype(BF16)
            outs = []
            for sc in scores[p]:
                e = jnp.exp(sc - jnp.max(sc, axis=0, keepdims=True))
                pr = e / jnp.sum(e, axis=0, keepdims=True)
                outs.append(jnp.dot(vt, pr.astype(BF16), preferred_element_type=F32))
            o_ref[:, TOK_WIDTH + p * PAIR:TOK_WIDTH + (p + 1) * PAIR] = jnp.where(top, outs[0], outs[1]).T.astype(BF16)

    return pl.pallas_call(
        body, name=name, grid=(nb, nq),
        in_specs=[pl.BlockSpec((ts, MEM_WIDTH), lambda b, i: (b * nq + i, 3)),
                  pl.BlockSpec((MEM_LEN, 2 * MEM_WIDTH), lambda b, i: (b, 0)),
                  pl.BlockSpec((ts, TOK_WIDTH), lambda b, i: (b * nq + i, 0))],
        out_specs=pl.BlockSpec((ts, TOK_WIDTH + MEM_WIDTH), lambda b, i: (b * nq + i, 0)),
        out_shape=SDS((nb * s, TOK_WIDTH + MEM_WIDTH), BF16),
        compiler_params=_params(("parallel", "parallel")))(proj, memkv, tok)


def memattn_bwd(proj, memkv, dcat, dtok, nb, s, name):
    ts = min(TM, s)
    nq = s // ts

    def body(q_ref, kv_ref, do_ref, dtok_ref, dq_ref, dkv_ref):
        i = pl.program_id(1)
        dq_ref[:, :TOK_WIDTH] = dtok_ref[...]

        @pl.when(i == 0)
        def _():
            dkv_ref[...] = jnp.zeros_like(dkv_ref)

        lo = _half_masks(MEM_LEN)
        top = lax.broadcasted_iota(jnp.int32, (PAIR, ts), 0) < HEAD_DIM
        n_pairs = MEM_HEADS // 2
        qs, dos, kps, products = [], [], [], []
        for p in range(n_pairs):
            qp = q_ref[:, p * PAIR:(p + 1) * PAIR].astype(BF16)
            dop = do_ref[:, p * PAIR:(p + 1) * PAIR].astype(BF16)
            kp = kv_ref[:, p * PAIR:(p + 1) * PAIR] * QK_SCALE
            kms = _split_pair(kp)
            vms = _split_pair(kv_ref[:, MEM_WIDTH + p * PAIR:MEM_WIDTH + (p + 1) * PAIR])
            products.append([(lax.dot_general(km, qp, NT_DIMS, preferred_element_type=F32),
                              lax.dot_general(vm, dop, NT_DIMS, preferred_element_type=F32)) for km, vm in zip(kms, vms)])
            qs.append(qp)
            dos.append(dop)
            kps.append(kp)
        for p in range(n_pairs):
            kt = kps[p].astype(F32).T.astype(BF16)
            dks, dvs, dqs = [], [], []
            for sc, dp in products[p]:
                e = jnp.exp(sc - jnp.max(sc, axis=0, keepdims=True))
                pr = e / jnp.sum(e, axis=0, keepdims=True)
                dl = jnp.sum(pr * dp, axis=0, keepdims=True)
                ds = (pr * (dp - dl)).astype(BF16)
                dvs.append(jnp.dot(pr.astype(BF16), dos[p], preferred_element_type=F32))
                dks.append(jnp.dot(ds, qs[p], preferred_element_type=F32))
                dqs.append(jnp.dot(kt, ds, preferred_element_type=F32))
            dq_ref[:, TOK_WIDTH + p * PAIR:TOK_WIDTH + (p + 1) * PAIR] = jnp.where(top, dqs[0], dqs[1]).T.astype(BF16)
            dkv_ref[:, p * PAIR:(p + 1) * PAIR] += jnp.where(lo, dks[0], dks[1]) * QK_SCALE
            dkv_ref[:, MEM_WIDTH + p * PAIR:MEM_WIDTH + (p + 1) * PAIR] += jnp.where(lo, dvs[0], dvs[1])

    return pl.pallas_call(
        body, name=name, grid=(nb, nq),
        in_specs=[pl.BlockSpec((ts, MEM_WIDTH), lambda b, i: (b * nq + i, 3)),
                  pl.BlockSpec((MEM_LEN, 2 * MEM_WIDTH), lambda b, i: (b, 0)),
                  pl.BlockSpec((ts, MEM_WIDTH), lambda b, i: (b * nq + i, 3)),
                  pl.BlockSpec((ts, TOK_WIDTH), lambda b, i: (b * nq + i, 0))],
        out_specs=[pl.BlockSpec((ts, TOK_WIDTH + MEM_WIDTH), lambda b, i: (b * nq + i, 0)),
                   pl.BlockSpec((MEM_LEN, 2 * MEM_WIDTH), lambda b, i: (b, 0))],
        out_shape=[SDS((nb * s, TOK_WIDTH + MEM_WIDTH), BF16), SDS((nb * MEM_LEN, 2 * MEM_WIDTH), F32)],
        compiler_params=_params(("parallel", "arbitrary")))(proj, memkv, dcat, dtok)


def _pool_select(shape, s2, s4, s8, s16):
    lane = lax.broadcasted_iota(jnp.int32, shape, 1)
    return jnp.where(lane < POOL_GROUP, s2, jnp.where(lane < 2 * POOL_GROUP, s4, jnp.where(lane < 3 * POOL_GROUP, s8, s16)))


def _pool_count(shape, first_pos):
    pos = first_pos + lax.broadcasted_iota(jnp.int32, shape, 0)
    win = _pool_select(shape, 2, 4, 8, 16)
    return jnp.minimum(pos + 1, win).astype(F32)


def pool_fwd(proj, pw_bd, pscale, nb, s):
    ts = min(TS, s)
    nq = s // ts
    w = TOK_WIDTH

    def body(c_ref, h_ref, w_ref, sc_ref, pooled_ref, tok_ref):
        i = pl.program_id(0) % nq
        cur = c_ref[...]
        halo = jnp.where(i == 0, 0.0, h_ref[...])
        xe = jnp.concatenate([halo, cur], axis=0)
        s2 = xe + pltpu.roll(xe, 1, axis=0)
        s4 = s2 + pltpu.roll(s2, 2, axis=0)
        s8 = s4 + pltpu.roll(s4, 4, axis=0)
        s16 = s8 + pltpu.roll(s8, 8, axis=0)
        hp = HALO_POOL
        ws = _pool_select((ts, w), s2[hp:], s4[hp:], s8[hp:], s16[hp:])
        pooled = (ws / _pool_count((ts, w), i * ts) - cur).astype(BF16)
        pooled_ref[...] = pooled
        mixed = jnp.dot(pooled, w_ref[...], preferred_element_type=F32)
        tok_ref[...] = (mixed * sc_ref[...]).astype(BF16)

    row = pl.BlockSpec((ts, w), lambda r: (r, 0))
    return pl.pallas_call(
        body, name="pool_fwd", grid=(nb * nq,),
        in_specs=[row, pl.BlockSpec((HALO_POOL, w), lambda r: (jnp.maximum(r * (ts // HALO_POOL) - 1, 0), 0)),
                  pl.BlockSpec((w, w), lambda r: (0, 0)), pl.BlockSpec((1, w), lambda r: (0, 0))],
        out_specs=[row, row], out_shape=[SDS((nb * s, w), BF16), SDS((nb * s, w), BF16)],
        compiler_params=_params(("parallel",)))(proj, proj, pw_bd, pscale)


def pool_bwd_mix(dcat, pooled, pw_bd, pscale, nb, s):
    ts = min(TS, s)
    w = TOK_WIDTH

    def body(dt_ref, p_ref, w_ref, sc_ref, dm_ref, dp_ref, ds_ref):
        r = pl.program_id(0)
        dtok = dt_ref[...]
        mixed = jnp.dot(p_ref[...], w_ref[...], preferred_element_type=F32)

        @pl.when(r == 0)
        def _():
            ds_ref[...] = jnp.zeros_like(ds_ref)

        ds_ref[...] += jnp.sum(dtok * mixed, axis=0, keepdims=True)
        dmx = (dtok * sc_ref[...]).astype(BF16)
        dm_ref[...] = dmx
        dp_ref[...] = lax.dot_general(dmx, w_ref[...], NT_DIMS, preferred_element_type=F32)

    row = pl.BlockSpec((ts, w), lambda r: (r, 0))
    mat = pl.BlockSpec((w, w), lambda r: (0, 0))
    vec = pl.BlockSpec((1, w), lambda r: (0, 0))
    return pl.pallas_call(
        body, name="pool_bwd_mix", grid=(nb * s // ts,), in_specs=[row, row, mat, vec],
        out_specs=[row, row, vec], out_shape=[SDS((nb * s, w), BF16), SDS((nb * s, w), F32), SDS((1, w), F32)],
        compiler_params=_params(("arbitrary",)))(dcat, pooled, pw_bd, pscale)


def pool_bwd_window(dpooled, nb, s):
    ts = min(TS, s)
    nq = s // ts
    w = TOK_WIDTH
    n_ext = ts + HALO_POOL
    n_halo_blocks = nb * s // HALO_POOL

    def body(c_ref, n_ref, du_ref):
        i = pl.program_id(0) % nq
        cur = c_ref[...]
        nxt = jnp.where(i == nq - 1, 0.0, n_ref[...])
        ze = jnp.concatenate([cur, nxt], axis=0) / _pool_count((n_ext, w), i * ts)
        s2 = ze + pltpu.roll(ze, n_ext - 1, axis=0)
        s4 = s2 + pltpu.roll(s2, n_ext - 2, axis=0)
        s8 = s4 + pltpu.roll(s4, n_ext - 4, axis=0)
        s16 = s8 + pltpu.roll(s8, n_ext - 8, axis=0)
        ws = _pool_select((ts, w), s2[:ts], s4[:ts], s8[:ts], s16[:ts])
        du_ref[...] = (ws - cur).astype(BF16)

    row = pl.BlockSpec((ts, w), lambda r: (r, 0))
    return pl.pallas_call(
        body, name="pool_bwd_window", grid=(nb * nq,),
        in_specs=[row, pl.BlockSpec((HALO_POOL, w),
                                    lambda r: (jnp.minimum((r + 1) * (ts // HALO_POOL), n_halo_blocks - 1), 0))],
        out_specs=row, out_shape=SDS((nb * s, w), BF16),
        compiler_params=_params(("parallel",)))(dpooled, dpooled)


def _conv_rows(xe, w_ref):
    return (w_ref[0, 2:3, :] * xe + w_ref[0, 1:2, :] * pltpu.roll(xe, 1, axis=0)
            + w_ref[0, 0:1, :] * pltpu.roll(xe, 2, axis=0) + w_ref[0, 3:4, :])


def ffn_up_gate(x_bf, wup, cw, nb, s, name):
    tm = min(2 * TM, s)
    nq = s // tm
    w = FF_BLOCK_PAD
    hr = 2 * HALO_CONV
    k = x_bf.shape[1]

    def body(xc_ref, xh_ref, wu_ref, wg_ref, cu_ref, cg_ref, act_ref, a_ref, b_ref, hu_ref, hg_ref):
        first = (pl.program_id(1) % nq) == 0
        xc = xc_ref[...]
        xh = xh_ref[...]

        def products(w_ref):
            return (jnp.dot(xc, w_ref[0], preferred_element_type=F32), jnp.dot(xh, w_ref[0], preferred_element_type=F32))

        def conv(hcur, hprev, c_ref, h_out):
            h_out[...] = hcur.astype(BF16)
            xe = jnp.concatenate([jnp.where(first, 0.0, hprev), hcur], axis=0)
            return _conv_rows(xe, c_ref)[hr:]

        pu, pg = products(wu_ref), products(wg_ref)
        cu = conv(*pu, cu_ref, hu_ref)
        cg = conv(*pg, cg_ref, hg_ref)
        sg = _sigmoid(cg)
        a = cg * sg
        act_ref[...] = (a * cu).astype(BF16)
        a_ref[...] = a.astype(BF16)
        b_ref[...] = (cu * (sg * (1.0 + cg * (1.0 - sg)))).astype(BF16)

    def wblock(off):
        return pl.BlockSpec((1, k, w), lambda j, r: (j + off, 0, 0))

    def cblock(off):
        return pl.BlockSpec((1, 8, w), lambda j, r: (j + off, 0, 0))

    tile = pl.BlockSpec((tm, w), lambda j, r: (r, j))
    out = SDS((nb * s, FF_PAIRS * w), BF16)
    return pl.pallas_call(
        body, name=name, grid=(FF_PAIRS, nb * nq),
        in_specs=[pl.BlockSpec((tm, k), lambda j, r: (r, 0)),
                  pl.BlockSpec((hr, k), lambda j, r: (jnp.maximum(r * (tm // hr) - 1, 0), 0)),
                  wblock(0), wblock(FF_PAIRS), cblock(0), cblock(FF_PAIRS)],
        out_specs=[tile] * 5, out_shape=[out] * 5,
        compiler_params=_params(("parallel", "parallel")))(x_bf, x_bf, wup, wup, cw, cw)


def gate_conv_bwd(dact, a, b, hu, hg, cw, nb, s, name):
    ts = min(TS, s)
    nq = s // ts
    w = FF_BLOCK_PAD
    hc = HALO_CONV
    hb = 2 * hc
    n_ext = ts + hc

    def body(dc_ref, dn_ref, ac_ref, an_ref, bc_ref, bn_ref, hu_ref, hg_ref, wu_ref, wg_ref,
             dhu_ref, dhg_ref, dwu_ref, dwg_ref):
        r = pl.program_id(1)
        last = (r % nq) == nq - 1

        def ext(c_ref, n_ref, mask_next=False):
            nxt = n_ref[...].astype(F32)[:hc]
            if mask_next:
                nxt = jnp.where(last, 0.0, nxt)
            return jnp.concatenate([c_ref[...].astype(F32), nxt], axis=0)

        da = ext(dc_ref, dn_ref, mask_next=True)

        def branch(dcv, w_ref, h_ref, dh_ref, dw_ref):
            d0 = dcv[:ts]
            d1 = pltpu.roll(dcv, n_ext - 1, axis=0)[:ts]
            d2 = pltpu.roll(dcv, n_ext - 2, axis=0)[:ts]
            dh_ref[...] = (w_ref[0, 2:3, :] * d0 + w_ref[0, 1:2, :] * d1 + w_ref[0, 0:1, :] * d2).astype(BF16)
            hv = h_ref[...].astype(F32)
            rows = [jnp.sum(d2 * hv, axis=0, keepdims=True), jnp.sum(d1 * hv, axis=0, keepdims=True),
                    jnp.sum(d0 * hv, axis=0, keepdims=True), jnp.sum(d0, axis=0, keepdims=True)]
            sub = lax.broadcasted_iota(jnp.int32, (8, w), 0)
            upd = jnp.zeros((8, w), F32)
            for kk, rv in enumerate(rows):
                upd = jnp.where(sub == kk, rv, upd)

            @pl.when(r == 0)
            def _():
                dw_ref[...] = jnp.zeros_like(dw_ref)

            dw_ref[...] += upd[None]

        branch(da * ext(ac_ref, an_ref), wu_ref, hu_ref, dhu_ref, dwu_ref)
        branch(da * ext(bc_ref, bn_ref), wg_ref, hg_ref, dhg_ref, dwg_ref)

    cur = pl.BlockSpec((ts, w), lambda j, r: (r, j))
    nxt = pl.BlockSpec((hb, w), lambda j, r: (jnp.minimum((r + 1) * (ts // hb), nb * s // hb - 1), j))

    def wspec(off):
        return pl.BlockSpec((1, 8, w), lambda j, r: (j + off, 0, 0))

    p = FF_PAIRS
    dw_spec = pl.BlockSpec((1, 8, w), lambda j, r: (j, 0, 0))
    return pl.pallas_call(
        body, name=name, grid=(p, nb * nq),
        in_specs=[cur, nxt, cur, nxt, cur, nxt, cur, cur, wspec(0), wspec(p)],
        out_specs=[cur, cur, dw_spec, dw_spec],
        out_shape=[SDS((nb * s, p * w), BF16), SDS((nb * s, p * w), BF16), SDS((p, 8, w), F32), SDS((p, 8, w), F32)],
        compiler_params=_params(("parallel", "arbitrary")))(dact, dact, a, a, b, b, hu, hg, cw, cw)


def _tri(n, upper):
    r = lax.broadcasted_iota(jnp.int32, (n, n), 0)
    c = lax.broadcasted_iota(jnp.int32, (n, n), 1)
    return ((r <= c) if upper else (r >= c)).astype(F32)


def fgate_fwd(fl, fb, nb, s):
    tc = min(TC, s)
    nq = s // tc

    def body(fl_ref, fb_ref, f_ref, carry):
        @pl.when(pl.program_id(1) == 0)
        def _():
            carry[...] = jnp.zeros_like(carry)

        z = fl_ref[...] + fb_ref[...]
        logf = jnp.minimum(z, 0.0) - jnp.log(1.0 + jnp.exp(-jnp.abs(z)))
        f_ref[...] = jnp.dot(_tri(tc, False), logf, preferred_element_type=F32,
                             precision=lax.Precision.HIGHEST) + carry[...]
        carry[...] += jnp.sum(logf, axis=0, keepdims=True)

    row = pl.BlockSpec((tc, LANES), lambda b, i: (b * nq + i, 0))
    return pl.pallas_call(
        body, name="fgate_fwd", grid=(nb, nq), in_specs=[row, pl.BlockSpec((1, LANES), lambda b, i: (0, 0))],
        out_specs=row, out_shape=SDS((nb * s, LANES), F32), scratch_shapes=[pltpu.VMEM((1, LANES), F32)],
        compiler_params=_params(("arbitrary", "arbitrary")))(fl, fb)


def fgate_bwd(d_cum_q, d_cum_k, fl, fb, dk, dv, nb, s):
    tc = min(TC, s)
    nq = s // tc

    def body(dfq_ref, dfk_ref, fl_ref, fb_ref, dk_ref, dv_ref, dkvf_ref, dfb_ref, carry):
        b = pl.program_id(0)
        i = pl.program_id(1)

        @pl.when(i == 0)
        def _():
            carry[...] = jnp.zeros_like(carry)

        @pl.when(jnp.logical_and(b == 0, i == 0))
        def _():
            dfb_ref[...] = jnp.zeros_like(dfb_ref)

        dfv = dfq_ref[...] + dfk_ref[...]
        dlog = jnp.dot(_tri(tc, True), dfv, preferred_element_type=F32,
                       precision=lax.Precision.HIGHEST) + carry[...]
        carry[...] += jnp.sum(dfv, axis=0, keepdims=True)
        z = fl_ref[...] + fb_ref[...]
        dfl = dlog / (1.0 + jnp.exp(z))
        dkvf_ref[:, :TOK_WIDTH] = dk_ref[...]
        dkvf_ref[:, TOK_WIDTH:2 * TOK_WIDTH] = dv_ref[...]
        dkvf_ref[:, 2 * TOK_WIDTH:] = dfl.astype(BF16)
        dfb_ref[...] += jnp.sum(dfl, axis=0, keepdims=True)

    def rows(width):
        return pl.BlockSpec((tc, width), lambda b, i: (b * nq + nq - 1 - i, 0))

    row = rows(LANES)
    vec = pl.BlockSpec((1, LANES), lambda b, i: (0, 0))
    return pl.pallas_call(
        body, name="fgate_bwd", grid=(nb, nq), in_specs=[row, row, row, vec, rows(TOK_WIDTH), rows(TOK_WIDTH)],
        out_specs=[rows(KV_COLS_PAD), vec],
        out_shape=[SDS((nb * s, KV_COLS_PAD), BF16), SDS((1, LANES), F32)], scratch_shapes=[pltpu.VMEM((1, LANES), F32)],
        compiler_params=_params(("arbitrary", "arbitrary")))(d_cum_q, d_cum_k, fl, fb, dk, dv)


PAIR = 2 * HEAD_DIM
N_PAIRS = FOX_HEADS // 2


def _lane_put(shape, h, col):
    lane = lax.broadcasted_iota(jnp.int32, shape, 1)
    return jnp.where(lane == h, col, 0.0)


def _half_masks(rows):
    lane = lax.broadcasted_iota(jnp.int32, (rows, PAIR), 1)
    return lane < HEAD_DIM


def _split_pair(x, scale=None):
    if scale is not None:
        x = x * scale
    lo = _half_masks(x.shape[0])
    zero = jnp.zeros_like(x)
    return jnp.where(lo, x, zero), jnp.where(lo, zero, x)


def _to_tile_rows(a, nb, s, tf):
    return a.reshape(nb * s // tf, tf, LANES)[:, :, :16].transpose(0, 2, 1)


def _from_tile_rows(a):
    tiles, _, tf = a.shape
    return jnp.pad(a.transpose(0, 2, 1), ((0, 0), (0, 0), (0, LANES - 16))).reshape(tiles * tf, LANES)


BIAS_TERMS = 3
LOOKAHEAD = 4
FOLLOW_FWD = 1
LOOKAHEAD_BWD = 2
FOLLOW_BWD = 1


def _bias_lane(h):
    return HEAD_DIM if h % 2 == 0 else 0


def _placement():
    rows = jnp.arange(LANES)[:, None]
    cols = jnp.arange(FOX_HEADS * PAIR)[None, :]
    head, lane = cols // PAIR, cols % PAIR
    first = jnp.where(head % 2 == 0, HEAD_DIM, 0)
    term = lane - first
    hit = (term >= 0) & (term < BIAS_TERMS) & (rows == 16 * term + head)
    return hit.astype(BF16)


def fox_prep(kv, fneg, nb, s):
    tf = min(TF, s)
    w = TOK_WIDTH

    def body(k_ref, v_ref, f_ref, pl_ref, ka_ref, vt_ref):
        lane = lax.broadcasted_iota(jnp.int32, (tf, LANES), 1)
        lo = lane < HEAD_DIM
        f = jnp.where(lane < FOX_HEADS, f_ref[...], 0.0)
        hi = f.astype(BF16).astype(F32)
        mid = (f - hi).astype(BF16).astype(F32)
        low = (f - hi - mid).astype(BF16).astype(F32)
        terms = (hi + pltpu.roll(mid, 16, axis=1) + pltpu.roll(low, 32, axis=1)).astype(BF16)
        placed = jnp.dot(terms, pl_ref[...], preferred_element_type=F32).astype(BF16)
        one = jnp.ones((tf, LANES), BF16)
        zero = jnp.zeros((tf, LANES), BF16)
        for p in range(N_PAIRS):
            kp = k_ref[:, p * PAIR:(p + 1) * PAIR] * QK_SCALE
            vp = v_ref[:, p * PAIR:(p + 1) * PAIR]
            he, ho = 2 * p, 2 * p + 1
            ka_ref[:, he * PAIR:(he + 1) * PAIR] = jnp.where(lo, kp, placed[:, he * PAIR:(he + 1) * PAIR])
            ka_ref[:, ho * PAIR:(ho + 1) * PAIR] = jnp.where(lo, placed[:, ho * PAIR:(ho + 1) * PAIR], kp)
            ve = jnp.where(lo, vp, jnp.where(lane == HEAD_DIM, one, zero))
            vo = jnp.where(lo, jnp.where(lane == 0, one, zero), vp)
            vt_ref[0, he * PAIR:(he + 1) * PAIR, :] = ve.astype(F32).T.astype(BF16)
            vt_ref[0, ho * PAIR:(ho + 1) * PAIR, :] = vo.astype(F32).T.astype(BF16)

    return pl.pallas_call(
        body, name="fox_prep", grid=(nb * s // tf,),
        in_specs=[pl.BlockSpec((tf, w), lambda r: (r, 0)), pl.BlockSpec((tf, w), lambda r: (r, 1)),
                  pl.BlockSpec((tf, LANES), lambda r: (r, 0)), pl.BlockSpec((LANES, FOX_HEADS * PAIR), lambda r: (0, 0))],
        out_specs=[pl.BlockSpec((tf, FOX_HEADS * PAIR), lambda r: (r, 0)),
                   pl.BlockSpec((1, FOX_HEADS * PAIR, tf), lambda r: (r, 0, 0))],
        out_shape=[SDS((nb * s, FOX_HEADS * PAIR), BF16), SDS((nb * s // tf, FOX_HEADS * PAIR, tf), BF16)],
        compiler_params=_params(("parallel",)))(kv, kv, fneg, _placement())


def fox_fwd_t(pq, kaug, vaug_t, nb, s):
    tf = min(TF, s)
    n = s // tf
    w = TOK_WIDTH
    wa = FOX_HEADS * PAIR

    def body(q_ref, k_hbm, vt_hbm, ob_ref, of_ref, lse_ref, k_vm, vt_vm, qx_scr, m_scr, acc_scr, sems):
        b = pl.program_id(0)
        i = pl.program_id(1)

        @pl.when(i == 0)
        def _():
            ck = pltpu.make_async_copy(k_hbm.at[pl.ds(pl.multiple_of(b * s, tf), s)], k_vm, sems.at[0])
            cv = pltpu.make_async_copy(vt_hbm.at[pl.ds(b * n, n)], vt_vm, sems.at[1])
            ck.start()
            cv.start()
            ck.wait()
            cv.wait()

        lane = lax.broadcasted_iota(jnp.int32, (tf, PAIR), 1)
        one = jnp.ones((tf, PAIR), BF16)
        zero = jnp.zeros((tf, PAIR), BF16)
        for p in range(N_PAIRS):
            qp = q_ref[:, p * PAIR:(p + 1) * PAIR]
            be, bo = _bias_lane(2 * p), _bias_lane(2 * p + 1)
            ones_e = jnp.where((lane >= be) & (lane < be + BIAS_TERMS), one, zero)
            ones_o = jnp.where((lane >= bo) & (lane < bo + BIAS_TERMS), one, zero)
            qx_scr[2 * p] = jnp.where(lane < HEAD_DIM, qp, ones_e)
            qx_scr[2 * p + 1] = jnp.where(lane < HEAD_DIM, ones_o, qp)
        m_scr[...] = jnp.full(m_scr.shape, NEG_BIG, F32)
        acc_scr[...] = jnp.zeros_like(acc_scr)

        def tile(j, masked):
            ks = pl.multiple_of(j * tf, tf)
            if masked:
                keep = lax.broadcasted_iota(jnp.int32, (tf, tf), 1) >= lax.broadcasted_iota(jnp.int32, (tf, tf), 0)
            def scores(h):
                kx = k_vm[pl.ds(ks, tf), h * PAIR:(h + 1) * PAIR]
                return lax.dot_general(kx, qx_scr[h], NT_DIMS, preferred_element_type=F32)

            def values(h, pr, a):
                pv = jnp.dot(vt_vm[j, h * PAIR:(h + 1) * PAIR, :], pr, preferred_element_type=F32)
                acc_scr[h] = a * acc_scr[h] + pv

            ahead = [scores(h) for h in range(LOOKAHEAD)]
            behind = []
            for h in range(FOX_HEADS):
                sc = ahead.pop(0)
                if h + LOOKAHEAD < FOX_HEADS:
                    ahead.append(scores(h + LOOKAHEAD))
                if masked:
                    sc = jnp.where(keep, sc, NEG_BIG)
                m_prev = m_scr[h]
                m_new = jnp.maximum(m_prev, jnp.max(sc, axis=0, keepdims=True))
                m_scr[h] = m_new
                behind.append((h, jnp.exp(sc - m_new).astype(BF16), jnp.exp(m_prev - m_new)))
                if len(behind) > FOLLOW_FWD:
                    values(*behind.pop(0))
            for item in behind:
                values(*item)

        def step(j, carry):
            tile(j, False)
            return carry

        lax.fori_loop(0, i, step, 0)
        tile(i, True)

        top = lax.broadcasted_iota(jnp.int32, (PAIR, tf), 0) < HEAD_DIM
        sub = lax.broadcasted_iota(jnp.int32, (16, tf), 0)
        lse = jnp.zeros((16, tf), F32)
        for p in range(N_PAIRS):
            he, ho = 2 * p, 2 * p + 1
            le = acc_scr[he, HEAD_DIM:HEAD_DIM + 1, :]
            lod = acc_scr[ho, 0:1, :]
            o = jnp.where(top, acc_scr[he] / le, acc_scr[ho] / lod).T
            ob_ref[:, p * PAIR:(p + 1) * PAIR] = o.astype(BF16)
            of_ref[:, p * PAIR:(p + 1) * PAIR] = o
            lse = jnp.where(sub == he, m_scr[he] + jnp.log(le), lse)
            lse = jnp.where(sub == ho, m_scr[ho] + jnp.log(lod), lse)
        lse_ref[0] = lse

    qrow = lambda b, i: (b * n + i, 0)
    return pl.pallas_call(
        body, name="fox_fwd", grid=(nb, n),
        in_specs=[pl.BlockSpec((tf, w), qrow), ANY_SPEC, ANY_SPEC],
        out_specs=[pl.BlockSpec((tf, w), qrow), pl.BlockSpec((tf, w), qrow),
                   pl.BlockSpec((1, 16, tf), lambda b, i: (b * n + i, 0, 0))],
        out_shape=[SDS((nb * s, w), BF16), SDS((nb * s, w), F32), SDS((nb * n, 16, tf), F32)],
        scratch_shapes=[pltpu.VMEM((s, wa), BF16), pltpu.VMEM((n, wa, tf), BF16),
                        pltpu.VMEM((FOX_HEADS, tf, PAIR), BF16), pltpu.VMEM((FOX_HEADS, 1, tf), F32),
                        pltpu.VMEM((FOX_HEADS, PAIR, tf), F32), pltpu.SemaphoreType.DMA((2,))],
        compiler_params=_params(("arbitrary", "arbitrary")))(pq, kaug, vaug_t)


def fox_delta(dcat, o, nb, s):
    tf = min(TM, s)
    w = TOK_WIDTH

    def body(do_ref, o_ref, dl_ref):
        out = jnp.zeros((tf, LANES), F32)
        for h in range(FOX_HEADS):
            lo, hi = h * HEAD_DIM, (h + 1) * HEAD_DIM
            out = out + _lane_put((tf, LANES), h, jnp.sum(do_ref[:, lo:hi] * o_ref[:, lo:hi], axis=1, keepdims=True))
        dl_ref[...] = out

    row = pl.BlockSpec((tf, w), lambda r: (r, 0))
    return pl.pallas_call(
        body, name="fox_delta", grid=(nb * s // tf,), in_specs=[row, row],
        out_specs=pl.BlockSpec((tf, LANES), lambda r: (r, 0)), out_shape=SDS((nb * s, LANES), F32),
        compiler_params=_params(("parallel",)))(dcat, o)


def fox_bwd(pq, kv, fneg, dcat_bf, lse_rows, delta_rows, nb, s):
    tf = min(TF, s)
    n = s // tf
    w = TOK_WIDTH

    def body(q_hbm, k_ref, v_ref, f_ref, do_hbm, lse_ref, dl_ref, dq_ref, dk_ref, dv_ref, dfk_ref, dfq_ref,
             q_vm, do_vm, km_scr, vm_scr, kt_scr, fk_scr, dk_scr, dv_scr, rs_scr, dq_scr, fq_scr, sems):
        b = pl.program_id(0)
        j = pl.program_id(1)

        @pl.when(j == 0)
        def _():
            rows = pl.ds(pl.multiple_of(b * s, tf), s)
            cq = pltpu.make_async_copy(q_hbm.at[rows, pl.ds(0, w)], q_vm, sems.at[0])
            cd = pltpu.make_async_copy(do_hbm.at[rows, pl.ds(0, w)], do_vm, sems.at[1])
            cq.start()
            cd.start()
            dq_scr[...] = jnp.zeros_like(dq_scr)
            fq_scr[...] = jnp.zeros_like(fq_scr)
            cq.wait()
            cd.wait()

        for p in range(N_PAIRS):
            kp = k_ref[:, p * PAIR:(p + 1) * PAIR] * QK_SCALE
            ke, ko = _split_pair(kp)
            km_scr[2 * p] = ke
            km_scr[2 * p + 1] = ko
            kt_scr[p] = kp.astype(F32).T.astype(BF16)
            ve, vo = _split_pair(v_ref[:, p * PAIR:(p + 1) * PAIR])
            vm_scr[2 * p] = ve
            vm_scr[2 * p + 1] = vo
        for h in range(FOX_HEADS):
            fk_scr[h] = jnp.broadcast_to(f_ref[:, h:h + 1], (tf, tf))
        dk_scr[...] = jnp.zeros_like(dk_scr)
        dv_scr[...] = jnp.zeros_like(dv_scr)
        rs_scr[...] = jnp.zeros_like(rs_scr)

        def tile(i, masked):
            qs = pl.multiple_of(i * tf, tf)
            if masked:
                keep = lax.broadcasted_iota(jnp.int32, (tf, tf), 1) >= lax.broadcasted_iota(jnp.int32, (tf, tf), 0)
            def products(h):
                qp = q_vm[pl.ds(qs, tf), (h // 2) * PAIR:(h // 2 + 1) * PAIR]
                dop = do_vm[pl.ds(qs, tf), (h // 2) * PAIR:(h // 2 + 1) * PAIR]
                return (lax.dot_general(km_scr[h], qp, NT_DIMS, preferred_element_type=F32),
                        lax.dot_general(vm_scr[h], dop, NT_DIMS, preferred_element_type=F32))

            def dependents(h, prb, dsb):
                p = h // 2
                half = slice((h % 2) * HEAD_DIM, (h % 2 + 1) * HEAD_DIM)
                qp = q_vm[pl.ds(qs, tf), p * PAIR:(p + 1) * PAIR]
                dop = do_vm[pl.ds(qs, tf), p * PAIR:(p + 1) * PAIR]
                dv_scr[h] += jnp.dot(prb, dop, preferred_element_type=F32)
                dk_scr[h] += jnp.dot(dsb, qp, preferred_element_type=F32)
                dqt = jnp.dot(kt_scr[p], dsb, preferred_element_type=F32)
                dq_scr[i, p, half, :] += dqt[(h % 2) * HEAD_DIM:(h % 2 + 1) * HEAD_DIM]

            ahead = [products(h) for h in range(LOOKAHEAD_BWD)]
            behind = []
            for h in range(FOX_HEADS):
                sc, dp = ahead.pop(0)
                if h + LOOKAHEAD_BWD < FOX_HEADS:
                    ahead.append(products(h + LOOKAHEAD_BWD))
                sc = sc + fk_scr[h] - lse_ref[i, h:h + 1, :]
                if masked:
                    sc = jnp.where(keep, sc, NEG_BIG)
                pr = jnp.exp(sc)
                ds = pr * (dp - dl_ref[i, h:h + 1, :])
                part = ds[:, :LANES]
                for c in range(1, tf // LANES):
                    part = part + ds[:, c * LANES:(c + 1) * LANES]
                rs_scr[h] += part
                fq_scr[i, h:h + 1, :] += jnp.sum(ds, axis=0, keepdims=True)
                behind.append((h, pr.astype(BF16), ds.astype(BF16)))
                if len(behind) > FOLLOW_BWD:
                    dependents(*behind.pop(0))
            for item in behind:
                dependents(*item)

        def step(i, carry):
            tile(i, False)
            return carry

        tile(j, True)
        for p in range(N_PAIRS):
            dq_ref[:, p * PAIR:(p + 1) * PAIR] = dq_scr[j, p].T.astype(BF16)
        dfq_ref[0] = fq_scr[j]
        lax.fori_loop(j + 1, n, step, 0)

        lo = _half_masks(tf)
        dfk = jnp.zeros((tf, LANES), F32)
        for p in range(N_PAIRS):
            dk = jnp.where(lo, dk_scr[2 * p], dk_scr[2 * p + 1]) * QK_SCALE
            dk_ref[:, p * PAIR:(p + 1) * PAIR] = dk.astype(BF16)
            dv_ref[:, p * PAIR:(p + 1) * PAIR] = jnp.where(lo, dv_scr[2 * p], dv_scr[2 * p + 1]).astype(BF16)
            for h in (2 * p, 2 * p + 1):
                dfk = dfk - _lane_put((tf, LANES), h, jnp.sum(rs_scr[h], axis=1, keepdims=True))
        dfk_ref[...] = dfk

    krow = lambda b, j: (b * n + j, 0)
    rows = pl.BlockSpec((n, 16, tf), lambda b, j: (b, 0, 0))
    tile_out = pl.BlockSpec((tf, w), krow)
    return pl.pallas_call(
        body, name="fox_bwd", grid=(nb, n),
        in_specs=[ANY_SPEC, pl.BlockSpec((tf, w), krow), pl.BlockSpec((tf, w), lambda b, j: (b * n + j, 1)),
                  pl.BlockSpec((tf, LANES), krow), ANY_SPEC, rows, rows],
        out_specs=[tile_out, tile_out, tile_out, pl.BlockSpec((tf, LANES), krow),
                   pl.BlockSpec((1, 16, tf), lambda b, j: (b * n + j, 0, 0))],
        out_shape=[SDS((nb * s, w), BF16), SDS((nb * s, w), BF16), SDS((nb * s, w), BF16), SDS((nb * s, LANES), F32),
                   SDS((nb * n, 16, tf), F32)],
        scratch_shapes=[pltpu.VMEM((s, w), BF16), pltpu.VMEM((s, w), BF16),
                        pltpu.VMEM((FOX_HEADS, tf, PAIR), BF16), pltpu.VMEM((FOX_HEADS, tf, PAIR), BF16),
                        pltpu.VMEM((N_PAIRS, PAIR, tf), BF16), pltpu.VMEM((FOX_HEADS, tf, tf), F32),
                        pltpu.VMEM((FOX_HEADS, tf, PAIR), F32), pltpu.VMEM((FOX_HEADS, tf, PAIR), F32),
                        pltpu.VMEM((FOX_HEADS, tf, LANES), F32), pltpu.VMEM((n, N_PAIRS, PAIR, tf), F32),
                        pltpu.VMEM((n, 16, tf), F32), pltpu.SemaphoreType.DMA((2,))],
        compiler_params=_params(("arbitrary", "arbitrary")))(pq, kv, kv, fneg, dcat_bf, lse_rows, delta_rows)


ADAMW_TILE_ELEMS = 128 * 1024


def reduce_adamw(parts, w, m, v, name):
    layers, r, c = w.shape
    tr = r
    for cand in range(16, r, 16):
        if r % cand == 0 and cand * c <= ADAMW_TILE_ELEMS:
            tr = cand
    c1 = 1.0 - ADAM_B1 ** ADAM_STEP
    c2 = 1.0 - ADAM_B2 ** ADAM_STEP

    def body(*refs):
        p_refs = refs[:layers]
        w_ref, m_ref, v_ref, g_out, d_out, m_out, v_out = refs[layers:]

        def update(p_ref):
            g = p_ref[0].astype(F32)
            for k in range(1, N_DEV):
                g = g + p_ref[k].astype(F32)
            mn = ADAM_B1 * m_ref[0] + (1.0 - ADAM_B1) * g
            vn = ADAM_B2 * v_ref[0] + (1.0 - ADAM_B2) * (g * g)
            g_out[0] = g
            m_out[0] = mn
            v_out[0] = vn
            d_out[0] = -ADAM_LR * ((mn / c1) / (jnp.sqrt(vn / c2) + ADAM_EPS) + ADAM_WD * w_ref[0])

        if layers == 1:
            update(p_refs[0])
        else:
            for layer in range(layers):
                pl.when(pl.program_id(0) == layer)(lambda layer=layer: update(p_refs[layer]))

    row = pl.BlockSpec((1, tr, c), lambda l, i: (l, i, 0))
    return pl.pallas_call(
        body, name=name, grid=(layers, r // tr),
        in_specs=[pl.BlockSpec((N_DEV, tr, c), lambda l, i: (0, i, 0))] * layers + [row, row, row],
        out_specs=[row, row, row, row], out_shape=[SDS((layers, r, c), F32)] * 4,
        compiler_params=_params(("parallel", "parallel")))(*parts, w, m, v)


N_PEERS = N_DEV - 1
HBM_SPEC = pl.BlockSpec(memory_space=pltpu.HBM)
SEM_SPEC = pl.BlockSpec(memory_space=pltpu.SEMAPHORE)
ANY_SPEC = pl.BlockSpec(memory_space=pl.ANY)
SPLIT_EFFECT = pltpu.SideEffectType.DATAFLOW_SIDE_EFFECTING


def _peers(with_self=False):
    x, y, c = lax.axis_index("x"), lax.axis_index("y"), lax.axis_index("c")
    peers = []
    for k in range(0 if with_self else 1, N_DEV):
        px = 1 - x if (k >> 2) & 1 else x
        py = 1 - y if (k >> 1) & 1 else y
        pc = 1 - c if k & 1 else c
        peers.append(((px, py, pc), 4 * px + 2 * py + pc))
    return 4 * x + 2 * y + c, peers


def _push(src, dst, send_sems, recv_sems, slot, dev):
    return pltpu.make_async_remote_copy(src_ref=src, dst_ref=dst, send_sem=send_sems.at[slot], recv_sem=recv_sems.at[slot],
                                        device_id=dev, device_id_type=pl.DeviceIdType.MESH)


def _landing_shapes(arrs, scatter):
    return [SDS((N_DEV,) + tuple(a.shape[1:] if sc else a.shape), a.dtype) for a, sc in zip(arrs, scatter)]


def exchange(arrs, scatter, name):
    na = len(arrs)

    def body(*refs):
        ins = refs[:na]
        outs = refs[na:2 * na]
        send_sems, recv_sems, local_sems = refs[2 * na:]
        me, peers = _peers()
        local = []
        remote = []
        for a in range(na):
            lc = pltpu.make_async_copy(ins[a].at[me] if scatter[a] else ins[a], outs[a].at[me], local_sems.at[a])
            lc.start()
            local.append(lc)
            for k, (dev, idx) in enumerate(peers):
                cp = _push(ins[a].at[idx] if scatter[a] else ins[a], outs[a].at[me], send_sems, recv_sems,
                           a * N_PEERS + k, dev)
                cp.start()
                remote.append(cp)
        for a in range(na):
            for k, (dev, idx) in enumerate(peers):
                _push(ins[a].at[me] if scatter[a] else ins[a], outs[a].at[idx], send_sems, recv_sems,
                      a * N_PEERS + k, dev).wait_recv()
        for cp in remote:
            cp.wait_send()
        for lc in local:
            lc.wait()

    return pl.pallas_call(
        body, name=name, in_specs=[HBM_SPEC] * na, out_specs=[HBM_SPEC] * na, out_shape=_landing_shapes(arrs, scatter),
        scratch_shapes=[pltpu.SemaphoreType.DMA((na * N_PEERS,)), pltpu.SemaphoreType.DMA((na * N_PEERS,)),
                        pltpu.SemaphoreType.DMA((na,))])(*arrs)


def exchange_start(arrs, scatter, after, name):
    na = len(arrs)
    lands = [lax.empty(l.shape, l.dtype) for l in _landing_shapes(arrs, scatter)]

    def body(*refs):
        ins = refs[:na]
        land = refs[na:2 * na]
        send_sems, recv_sems = refs[2 * na + 1], refs[2 * na + 2]
        token = refs[-1]
        me, peers = _peers(with_self=True)
        for a in range(na):
            for k, (dev, idx) in enumerate(peers):
                _push(ins[a].at[idx] if scatter[a] else ins[a], land[a].at[me], send_sems, recv_sems,
                      a * N_DEV + k, dev).start()
        token[...] = jnp.zeros_like(token)

    thru = [pltpu.HBM(a.shape, a.dtype) for a in arrs] + [pltpu.HBM(l.shape, l.dtype) for l in lands]
    res = pl.pallas_call(
        body, name=name,
        out_shape=(pltpu.SemaphoreType.DMA((na * N_DEV,)), pltpu.SemaphoreType.DMA((na * N_DEV,)), *thru,
                   SDS((8, LANES), F32)),
        in_specs=[HBM_SPEC] * (2 * na) + [ANY_SPEC],
        out_specs=(SEM_SPEC, SEM_SPEC, *([HBM_SPEC] * (2 * na)), pl.BlockSpec(memory_space=pltpu.VMEM)),
        input_output_aliases={i: 2 + i for i in range(2 * na)},
        compiler_params=pltpu.CompilerParams(has_side_effects=SPLIT_EFFECT),
    )(*[pltpu.with_memory_space_constraint(a, pltpu.HBM) for a in arrs],
      *[pltpu.with_memory_space_constraint(l, pltpu.HBM) for l in lands], after)
    return {"send": res[0], "recv": res[1], "src": res[2:2 + na], "land": res[2 + na:2 + 2 * na],
            "token": res[-1][0, 0], "scatter": scatter}


def exchange_wait(handle, after, name):
    scatter = handle["scatter"]
    na = len(scatter)

    def body(*refs):
        src = refs[:na]
        land = refs[na:2 * na]
        send_sems, recv_sems = refs[2 * na], refs[2 * na + 1]
        me, peers = _peers(with_self=True)
        for a in range(na):
            for k, (dev, idx) in enumerate(peers):
                cp = _push(src[a].at[me] if scatter[a] else src[a], land[a].at[idx], send_sems, recv_sems,
                           a * N_DEV + k, dev)
                cp.wait_send()
                cp.wait_recv()

    ops = list(handle["src"]) + list(handle["land"])
    res = pl.pallas_call(
        body, name=name, out_shape=tuple(pltpu.HBM(o.shape, o.dtype) for o in ops),
        in_specs=[HBM_SPEC] * (2 * na) + [SEM_SPEC, SEM_SPEC, ANY_SPEC], out_specs=tuple([HBM_SPEC] * (2 * na)),
        input_output_aliases={i: i for i in range(2 * na)},
        compiler_params=pltpu.CompilerParams(has_side_effects=SPLIT_EFFECT),
    )(*ops, handle["send"], handle["recv"], after)
    return list(res[na:])


def forward_layer(l, xin, xin_bf, mem_bf, wt, nb, s, ffn_weights=None):
    sv = {"xin_bf": xin_bf}
    memkv = mm_nn(mem_bf, wt["memw"], BF16, f"memkv{l}")
    sv["memkv"] = memkv
    if l == 0:
        proj = mm_nn(xin_bf, wt["win_a"], F32, "proj_a")
        pooled, tok = pool_fwd(proj, wt["pw_bd"], wt["pscale"], nb, s)
        sv["pooled"] = pooled
    else:
        kv = mm_nn(xin_bf, wt["kvw"][:, :2 * TOK_WIDTH], BF16, "kv_proj")
        fl = mm_nn(xin_bf, wt["kvw"][:, 2 * TOK_WIDTH:], F32, "gate_proj")
        fneg = -fgate_fwd(fl, wt["fb"], nb, s)
        proj = mm_nn(xin_bf, wt["wq"], BF16, "proj_b")
        kaug, vaug_t = fox_prep(kv, fneg, nb, s)
        tok, o_f32, lse_rows = fox_fwd_t(proj, kaug, vaug_t, nb, s)
        sv.update(kv=kv, fl=fl, fneg=fneg, o_f32=o_f32, lse_rows=lse_rows)
    sv["proj"] = proj
    cat = memattn_fwd(proj, memkv, tok, nb, s, f"memattn_fwd{l}")
    sv["cat"] = cat
    x1, x1_bf, xh1, rs1 = ln_fwd(xin, cat, wt["wout"], wt["ln1_g"], wt["ln1_b"], f"out_proj_ln1_{l}")
    sv.update(x1_bf=x1_bf, xh1=xh1, rs1=rs1)
    if ffn_weights is not None:
        wt.update(ffn_weights(x1_bf))
    act, ga, gb, hu, hg = ffn_up_gate(x1_bf, wt["wup"], wt["cw"], nb, s, f"ffn_up_gate{l}")
    sv.update(act=act, ga=ga, gb=gb, hu=hu, hg=hg)
    x2, x2_bf, xh2, rs2 = ln_fwd(x1, act, wt["wdown"], wt["ln2_g"], wt["ln2_b"], f"ffn_down_ln2_{l}")
    sv.update(xh2=xh2, rs2=rs2)
    return x2, x2_bf, sv


def backward_layer(l, dy, sv, mem_bf, wt, nb, s, after_ffn=None, after_pool=None, loss_target=None):
    g = {}
    if loss_target is None:
        dr2, dr2_bf, g["ln2_g"], g["ln2_b"] = ln_bwd(dy[0], sv["xh2"], sv["rs2"], wt["ln2_g"], f"ln2_bwd{l}",
                                                     dy_scale=dy[1], products=dy[2])
    else:
        dr2, dr2_bf, g["ln2_g"], g["ln2_b"], g["loss_row"] = loss_ln_bwd(sv["xh2"], sv["rs2"], wt["ln2_g"], wt["ln2_b"],
                                                                         loss_target, f"loss_ln2_bwd{l}")
    dact = mm_nn(dr2_bf, wt["wdown"], BF16, f"ffn_down_dx{l}", trans_b=0)
    g["wdown"] = mm_tn(sv["act"], dr2_bf, f"ffn_down_dw{l}")
    dh_u, dh_g, dcw_u, dcw_g = gate_conv_bwd(dact, sv["ga"], sv["gb"], sv["hu"], sv["hg"], wt["cw"], nb, s,
                                             f"gate_conv_bwd{l}")
    g["cw"] = jnp.concatenate([dcw_u, dcw_g], axis=0)
    g["wup"] = jnp.concatenate([mm_tn(sv["x1_bf"], dh_u, f"ffn_up_dw_u{l}", blocked=True),
                                mm_tn(sv["x1_bf"], dh_g, f"ffn_up_dw_g{l}", blocked=True)], axis=0)
    ln1_g = wt["ln1_g"] if after_ffn is None else wt["ln1_g"] + after_ffn(g, dr2)
    dr1, dr1_bf, g["ln1_g"], g["ln1_b"] = ln_bwd(dr2, sv["xh1"], sv["rs1"], ln1_g, f"ffn_up_dx_ln1_bwd{l}",
                                                 dy_scale=DN_ALPHA, products=[(dh_u, wt["wup"], 0), (dh_g, wt["wup"], 1)])
    dcat, dcat_bf = mm_nn(dr1_bf, wt["wout"], F32, f"out_proj_dx{l}", also_bf16=True, trans_b=0)
    g["wout"] = mm_tn(sv["cat"], dr1_bf, f"out_proj_dw{l}")
    if l == 0:
        dmixed, dpooled, g["pscale"] = pool_bwd_mix(dcat, sv["pooled"], wt["pw_bd"], wt["pscale"], nb, s)
        g["pw_full"] = mm_tn(sv["pooled"], dmixed, "pool_dw")
        dtok = pool_bwd_window(dpooled, nb, s)
    else:
        delta = fox_delta(dcat, sv["o_f32"], nb, s)
        tf = min(TF, s)
        dtok, dk, dv, dfcum_k, dfq_rows = fox_bwd(sv["proj"], sv["kv"], sv["fneg"], dcat_bf,
                                                  sv["lse_rows"], _to_tile_rows(delta, nb, s, tf), nb, s)
    dproj, dmemkv = memattn_bwd(sv["proj"], sv["memkv"], dcat, dtok, nb, s, f"memattn_bwd{l}")
    g["memw"] = mm_tn(mem_bf, dmemkv, f"memkv_dw{l}")
    if l == 0:
        win_a = wt["win_a"] if after_pool is None else wt["win_a"] + after_pool(g, dproj).astype(BF16)
        dx = mm_nn(dproj, win_a, F32, "proj_a_dx", addend=dr1, add_scale=DN_ALPHA, trans_b=0)
        g["win_a"] = mm_tn(sv["xin_bf"], dproj, "proj_a_dw")
    else:
        dkvf, g["fb"] = fgate_bwd(_from_tile_rows(dfq_rows), dfcum_k, sv["fl"], wt["fb"], dk, dv, nb, s)
        dx = (dr1, DN_ALPHA, [(dproj, wt["wq"], 0), (dkvf, wt["kvw"], 0)])
        g["wq"] = mm_tn(sv["xin_bf"], dproj, "proj_b_dw")
        g["kvw"] = mm_tn(sv["xin_bf"], dkvf, "kv_proj_dw")
    return dx, g


def pack_replicated(pool_w, ln1_g, ln1_b, ln2_g, ln2_b, conv_b, f_b):
    cb = jnp.pad(conv_b, ((0, 0), (0, 6144 - 5504))).reshape(12, D_MODEL)
    fb = jnp.pad(f_b.reshape(1, FOX_HEADS), ((0, 3), (0, D_MODEL - FOX_HEADS)))
    return jnp.concatenate([pool_w.reshape(144, D_MODEL), ln1_g, ln1_b, ln2_g, ln2_b, cb, fb], axis=0)


def unpack_replicated(buf):
    pool_w = buf[:144].reshape(1, 4, POOL_GROUP, POOL_GROUP)
    ln = [buf[144 + 2 * k:146 + 2 * k] for k in range(4)]
    conv_b = buf[152:164].reshape(2, 6144)[:, :5504]
    f_b = buf[164, :FOX_HEADS]
    return pool_w, ln[0], ln[1], ln[2], ln[3], conv_b, f_b


def _pad_ff(a, axis):
    zeros = jnp.zeros(a.shape[:axis] + (FF_ROWS_PAD - FF_ROWS,) + a.shape[axis + 1:], a.dtype)
    halves = [lax.slice_in_dim(a, h * FF_ROWS, (h + 1) * FF_ROWS, axis=axis) for h in range(2)]
    return jnp.concatenate([halves[0], zeros, halves[1], zeros], axis=axis)


def _unpad_ff(a, axis):
    return jnp.concatenate([lax.slice_in_dim(a, h * FF_ROWS_PAD, h * FF_ROWS_PAD + FF_ROWS, axis=axis) for h in range(2)],
                           axis=axis)


def pack_small(conv_w, pool_scale):
    buf = jnp.zeros((16, FF_BLOCK_PAD), F32)
    buf = lax.dynamic_update_slice(buf, _pad_ff(conv_w.reshape(DEPTH * 3, FF_BLOCK), 1), (0, 0))
    return lax.dynamic_update_slice(buf, pool_scale, (8, 0))


def _block_diag(pw):
    out = jnp.zeros((TOK_WIDTH, TOK_WIDTH), pw.dtype)
    for g in range(4):
        out = lax.dynamic_update_slice(out, pw[g], (g * POOL_GROUP, g * POOL_GROUP))
    return out


def layer_shards(l, sq_a, sq_b, mem_w_kv, ffn_w_up, ffn_w_down):
    wdown = jnp.pad(ffn_w_down[l], ((0, FF_ROWS_PAD - FF_ROWS), (0, 0)))
    return [sq_a[0].astype(BF16), sq_b[0].astype(BF16), mem_w_kv[l].astype(BF16), _pad_ff(ffn_w_up[l], 1).astype(BF16),
            wdown.astype(BF16)]


def mixer_weights(l, gath, ln1_g, ln1_b, ln2_g, ln2_b):
    w_out = gath[1].reshape(D_MODEL, D_MODEL)
    wt = {"memw": gath[2].reshape(D_MODEL, 2 * MEM_WIDTH), "wout": w_out,
          "ln1_g": ln1_g[l:l + 1], "ln1_b": ln1_b[l:l + 1], "ln2_g": ln2_g[l:l + 1], "ln2_b": ln2_b[l:l + 1]}
    return wt, gath[0].reshape(D_MODEL, D_MODEL)


def ffn_weights(l, wup_g, wdown_g, small, conv_b):
    cb = _pad_ff(conv_b[l].reshape(N_DEV, FF_BLOCK), 1)
    cw = jnp.concatenate([small[:, 3 * l:3 * l + 3, :], cb[:, None, :], jnp.zeros((N_DEV, 4, FF_BLOCK_PAD), F32)], axis=1)
    return {"wup": wup_g, "wdown": wdown_g.reshape(FF_PAIRS * FF_BLOCK_PAD, D_MODEL), "cw": cw}


def mixer_grad_blocks(g, w_in_grad):
    blocks = [] if w_in_grad is None else [w_in_grad.reshape(N_DEV, 128, D_MODEL)]
    blocks += [g["wout"].reshape(N_DEV, 128, D_MODEL), g["memw"].reshape(N_DEV, 128, 2 * MEM_WIDTH)]
    return [b.astype(BF16) for b in blocks]


def ffn_grad_blocks(g):
    wdown = g["wdown"].reshape(N_DEV, FF_ROWS_PAD, D_MODEL)[:, :FF_ROWS]
    return [_unpad_ff(g["wup"], 2).astype(BF16), wdown.astype(BF16)]


def small_grad_blocks(g0, g1):
    taps = jnp.stack([g0["cw"][:, :3, :], g1["cw"][:, :3, :]], axis=1).reshape(N_DEV, DEPTH * 3, FF_BLOCK_PAD)
    small = jnp.zeros((N_DEV, 16, FF_BLOCK_PAD), F32)
    small = lax.dynamic_update_slice(small, taps, (0, 0, 0))
    return lax.dynamic_update_slice(small, g0["pscale"].reshape(N_DEV, 1, 96), (0, 8, 0))


def replicated_grads(g0, g1):
    pw = jnp.stack([g0["pw_full"][k * POOL_GROUP:(k + 1) * POOL_GROUP, k * POOL_GROUP:(k + 1) * POOL_GROUP] for k in range(4)])
    conv_b = jnp.stack([_unpad_ff(g_["cw"][:, 3, :], 1).reshape(N_DEV * FF_BLOCK) for g_ in (g0, g1)])
    ln = [jnp.concatenate([g0[n], g1[n]], axis=0) for n in ("ln1_g", "ln1_b", "ln2_g", "ln2_b")]
    return pack_replicated(pw[None], ln[0], ln[1], ln[2], ln[3], conv_b, g1["fb"][0, :FOX_HEADS])


def kernel(x, mem, a_w_in, a_pool_w, a_pool_scale, a_w_out, b_w_q, b_w_out, kv_w, f_b, mem_w_kv, ln1_g, ln1_b, ln2_g, ln2_b, ffn_w_up, ffn_conv_w, ffn_conv_b, ffn_w_down, loss_target, m_a_w_in, m_a_pool_w, m_a_pool_scale, m_a_w_out, m_b_w_q, m_b_w_out, m_kv_w, m_f_b, m_mem_w_kv, m_ln1_g, m_ln1_b, m_ln2_g, m_ln2_b, m_ffn_w_up, m_ffn_conv_w, m_ffn_conv_b, m_ffn_w_down, v_a_w_in, v_a_pool_w, v_a_pool_scale, v_a_w_out, v_b_w_q, v_b_w_out, v_kv_w, v_f_b, v_mem_w_kv, v_ln1_g, v_ln1_b, v_ln2_g, v_ln2_b, v_ffn_w_up, v_ffn_conv_w, v_ffn_conv_b, v_ffn_w_down):
    nb, s, d = x.shape
    t = nb * s
    x2d, mem_bf, target = x.reshape(t, d), mem.reshape(nb * MEM_LEN, d).astype(BF16), loss_target.reshape(t, d)

    shards0 = layer_shards(0, a_w_in, a_w_out, mem_w_kv, ffn_w_up, ffn_w_down)
    shards1 = layer_shards(1, b_w_q, b_w_out, mem_w_kv, ffn_w_up, ffn_w_down)
    shards1.append(jnp.pad(kv_w, ((0, 0), (0, KV_COLS_PAD - KV_COLS))).astype(BF16))
    gath0 = exchange(shards0[:3] + [pack_small(ffn_conv_w, a_pool_scale)], [False] * 4, "gather_w0_mixer")
    pending = {"ffn0": exchange_start(shards0[3:], [False] * 2, gath0[0], "gather_w0_ffn_start")}
    small = gath0[3]
    wt0, w_in = mixer_weights(0, gath0, ln1_g + pending["ffn0"]["token"], ln1_b, ln2_g, ln2_b)
    pw_bd = _block_diag(a_pool_w[0])
    wt0.update(win_a=w_in, pw_bd=pw_bd.astype(BF16),
               pscale=small[:, 8, :96].reshape(1, TOK_WIDTH) + pending["ffn0"]["token"])

    def ffn0_weights(x1_bf):
        got = exchange_wait(pending["ffn0"], x1_bf, "gather_w0_ffn_wait")
        pending["w1"] = exchange_start(shards1, [False] * 6, got[0], "gather_w1_start")
        w = ffn_weights(0, got[0], got[1], small, ffn_conv_b)
        w["cw"] = w["cw"] + pending["w1"]["token"]
        return w

    x1, x1_bf, sv0 = forward_layer(0, x2d, x2d, mem_bf, wt0, nb, s, ffn_weights=ffn0_weights)
    gath1 = exchange_wait(pending["w1"], x1_bf, "gather_w1_wait")
    wt1, w_q = mixer_weights(1, gath1, ln1_g, ln1_b, ln2_g, ln2_b)
    wt1.update(ffn_weights(1, gath1[3], gath1[4], small, ffn_conv_b))
    kvw = gath1[5].reshape(D_MODEL, KV_COLS_PAD)
    wt1.update(wq=w_q, kvw=kvw,
               fb=jnp.pad(f_b.reshape(1, FOX_HEADS), ((0, 0), (0, LANES - FOX_HEADS))))
    _, _, sv1 = forward_layer(1, x1, x1_bf, mem_bf, wt1, nb, s)

    dx1, g1 = backward_layer(1, None, sv1, mem_bf, wt1, nb, s, loss_target=target)
    loss = lax.psum(g1["loss_row"][0, 0], ("x", "y", "c"))
    blocks1 = (mixer_grad_blocks(g1, g1["wq"]) + ffn_grad_blocks(g1)
               + [g1["kvw"][:, :KV_COLS].reshape(N_DEV, 128, KV_COLS).astype(BF16)])
    pending["g1"] = exchange_start(blocks1, [True] * 6, dx1[0], "scatter_g1_start")
    wt0["ln2_g"] = wt0["ln2_g"] + pending["g1"]["token"]

    def after_ffn0(g, dxm):
        pending["gf0"] = exchange_start(ffn_grad_blocks(g), [True] * 2, dxm, "scatter_g0_ffn_start")
        return pending["gf0"]["token"]

    def after_pool0(g, x):
        blocks = mixer_grad_blocks(g, None) + [small_grad_blocks(g, g1), replicated_grads(g, g1)]
        pending["gm0"] = exchange_start(blocks, [True] * 3 + [False], x, "scatter_g0_mixer_start")
        return pending["gm0"]["token"]

    grad_x, g0 = backward_layer(0, dx1, sv0, mem_bf, wt0, nb, s, after_ffn=after_ffn0, after_pool=after_pool0)
    pending["gin"] = exchange_start([g0["win_a"].reshape(N_DEV, 128, D_MODEL).astype(BF16)], [True], grad_x,
                                    "scatter_g0_in_start")
    parts_f0 = exchange_wait(pending["gf0"], jnp.zeros((8, LANES), F32) + pending["gin"]["token"], "scatter_g0_ffn_wait")
    parts1 = exchange_wait(pending["g1"], parts_f0[0], "scatter_g1_wait")

    res = {}

    def upd(nm, parts, w2, m2, v2):
        res[nm] = reduce_adamw(parts, w2, m2, v2, f"adamw_{nm}")

    upd("b_w_q", [parts1[0]], b_w_q, m_b_w_q, v_b_w_q)
    upd("b_w_out", [parts1[1]], b_w_out, m_b_w_out, v_b_w_out)
    upd("kv_w", [parts1[5]], kv_w[None], m_kv_w[None], v_kv_w[None])
    upd("ffn_w_up", [parts_f0[0], parts1[3]], ffn_w_up, m_ffn_w_up, v_ffn_w_up)
    upd("ffn_w_down", [parts_f0[1], parts1[4]], ffn_w_down, m_ffn_w_down, v_ffn_w_down)
    parts_m0 = exchange_wait(pending["gm0"], res["ffn_w_down"][0], "scatter_g0_mixer_wait")
    parts_in = exchange_wait(pending["gin"], parts_m0[0], "scatter_g0_in_wait")
    upd("a_w_in", [parts_in[0]], a_w_in, m_a_w_in, v_a_w_in)
    upd("a_w_out", [parts_m0[0]], a_w_out, m_a_w_out, v_a_w_out)
    upd("mem_w_kv", [parts_m0[1], parts1[2]], mem_w_kv, m_mem_w_kv, v_mem_w_kv)
    upd("small", [parts_m0[2]], pack_small(ffn_conv_w, a_pool_scale)[None], pack_small(m_ffn_conv_w, m_a_pool_scale)[None],
        pack_small(v_ffn_conv_w, v_a_pool_scale)[None])
    upd("replicated", [parts_m0[3]], pack_replicated(a_pool_w, ln1_g, ln1_b, ln2_g, ln2_b, ffn_conv_b, f_b)[None],
        pack_replicated(m_a_pool_w, m_ln1_g, m_ln1_b, m_ln2_g, m_ln2_b, m_ffn_conv_b, m_f_b)[None],
        pack_replicated(v_a_pool_w, v_ln1_g, v_ln1_b, v_ln2_g, v_ln2_b, v_ffn_conv_b, v_f_b)[None])

    res["kv_w"] = [o[0] for o in res["kv_w"]]
    res["ffn_conv_w"] = [_unpad_ff(o[0, :DEPTH * 3, :], 1).reshape(DEPTH, 3, FF_BLOCK) for o in res["small"]]
    res["a_pool_scale"] = [o[0, 8:9, :96] for o in res["small"]]
    rep_names = ["a_pool_w", "ln1_g", "ln1_b", "ln2_g", "ln2_b", "ffn_conv_b", "f_b"]
    for nm in rep_names:
        res[nm] = []
    for o in res["replicated"]:
        for nm, val in zip(rep_names, unpack_replicated(o[0])):
            res[nm].append(val)

    order = ["a_w_in", "a_pool_w", "a_pool_scale", "a_w_out", "b_w_q", "b_w_out", "kv_w", "f_b", "mem_w_kv",
             "ln1_g", "ln1_b", "ln2_g", "ln2_b", "ffn_w_up", "ffn_conv_w", "ffn_conv_b", "ffn_w_down"]
    out = [loss, grad_x.reshape(nb, s, d)]
    for kind in range(4):
        out.extend(res[nm][kind] for nm in order)
    return tuple(out)
```

```python
import jax
import jax.numpy as jnp
from jax import lax
from jax.experimental import pallas as pl
from jax.experimental.pallas import tpu as pltpu

F32 = jnp.float32
BF16 = jnp.bfloat16
SDS = jax.ShapeDtypeStruct

N_DEV = 8
D_MODEL = 1024
TOK_WIDTH = 768
MEM_WIDTH = 256
MEM_LEN = 256
MEM_HEADS = 4
HEAD_DIM = 64
FOX_HEADS = 12
POOL_GROUP = 192
FF_BLOCK = 688
FF_BLOCK_PAD = 768
FF_PAIRS = 4
FF_ROWS = 344
FF_ROWS_PAD = FF_BLOCK_PAD // 2
KV_COLS = 1548
KV_COLS_PAD = 1664
LANES = 128
DEPTH = 2
DN_ALPHA = (2.0 * DEPTH) ** 0.25
LN_EPS = 1e-5
QK_SCALE = HEAD_DIM ** -0.5
NEG_BIG = -1e30

ADAM_LR = 0.001
ADAM_B1 = 0.9
ADAM_B2 = 0.999
ADAM_EPS = 1e-08
ADAM_WD = 0.01
ADAM_STEP = 10

VMEM_LIMIT_BYTES = 56 * 1024 * 1024
MM_BLOCK_BYTES = 6 * 1024 * 1024
TM = 512
TS = 256
TF = 256
TC = 256
HALO_POOL = 16
HALO_CONV = 8

NT_DIMS = (((1,), (1,)), ((), ()))
TN_DIMS = (((0,), (0,)), ((), ()))


def _params(sem=None):
    return pltpu.CompilerParams(dimension_semantics=sem, vmem_limit_bytes=VMEM_LIMIT_BYTES)


def _sigmoid(z):
    return 1.0 / (1.0 + jnp.exp(-z))


def _pick_tn(n):
    if n <= 2048:
        return n
    for t in (1024, 768, 512, 256, 128):
        if n % t == 0:
            return t
    return n


def mm_nn(a, b, out_dtype, name, addend=None, add_scale=1.0, also_bf16=False, trans_b=None):
    m, k = a.shape
    n = b.shape[1] if trans_b is None else b.shape[0]
    tm = min(TM, m)
    tn = n
    while k * tn * 2 > MM_BLOCK_BYTES or tm * tn * 4 > MM_BLOCK_BYTES:
        tn //= 2
    chunk = tn if tn <= 2048 else _pick_tn(tn)
    has_add = addend is not None

    def body(*refs):
        a_ref, b_ref = refs[0], refs[1]
        c_ref = refs[2] if has_add else None
        o_ref = refs[3] if has_add else refs[2]
        ob_ref = refs[-1] if also_bf16 else None
        av = a_ref[...].astype(BF16)
        for c in range(tn // chunk):
            cols = slice(c * chunk, (c + 1) * chunk)
            if trans_b is None:
                r = jnp.dot(av, b_ref[:, cols].astype(BF16), preferred_element_type=F32)
            else:
                r = lax.dot_general(av, b_ref[cols, :].astype(BF16), NT_DIMS, preferred_element_type=F32)
            if has_add:
                r = r + add_scale * c_ref[:, cols]
            o_ref[:, cols] = r.astype(out_dtype)
            if also_bf16:
                ob_ref[:, cols] = r.astype(BF16)

    b_spec = (pl.BlockSpec((k, tn), lambda j, i: (0, j)) if trans_b is None
              else pl.BlockSpec((tn, k), lambda j, i: (j, trans_b)))
    in_specs = [pl.BlockSpec((tm, k), lambda j, i: (i, 0)), b_spec]
    ops = [a, b]
    tile = pl.BlockSpec((tm, tn), lambda j, i: (i, j))
    if has_add:
        in_specs.append(tile)
        ops.append(addend)
    out_shape = [SDS((m, n), out_dtype)]
    out_specs = [tile]
    if also_bf16:
        out_shape.append(SDS((m, n), BF16))
        out_specs.append(tile)
    res = pl.pallas_call(
        body, name=name, grid=(n // tn, m // tm), in_specs=in_specs, out_specs=out_specs, out_shape=out_shape,
        compiler_params=_params(("parallel", "parallel")))(*ops)
    return tuple(res) if also_bf16 else res[0]


def mm_tn(a, b, name, blocked=False, out_dtype=BF16):
    t, m = a.shape
    _, n = b.shape
    tt = min(4 * TM, t)
    tm = 1024 if m % 1024 == 0 else m
    tn = FF_BLOCK_PAD if blocked else _pick_tn(n)
    nt = t // tt
    block = (1, tm, tn) if blocked else (tm, tn)
    in_place = out_dtype == F32

    def body(a_ref, b_ref, o_ref, *scratch):
        acc_ref = o_ref if in_place else scratch[0]
        kk = pl.program_id(2)
        r = lax.dot_general(a_ref[...].astype(BF16), b_ref[...].astype(BF16), TN_DIMS, preferred_element_type=F32)
        if blocked:
            r = r[None]

        @pl.when(kk == 0)
        def _():
            acc_ref[...] = r

        @pl.when(kk != 0)
        def _():
            acc_ref[...] += r

        if not in_place:
            @pl.when(kk == nt - 1)
            def _():
                o_ref[...] = acc_ref[...].astype(out_dtype)

    if blocked:
        out_shape = SDS((n // tn, m, tn), out_dtype)
        out_spec = pl.BlockSpec(block, lambda i, j, kk: (j, i, 0))
    else:
        out_shape = SDS((m, n), out_dtype)
        out_spec = pl.BlockSpec(block, lambda i, j, kk: (i, j))
    return pl.pallas_call(
        body, name=name, grid=(m // tm, n // tn, nt),
        in_specs=[pl.BlockSpec((tt, tm), lambda i, j, kk: (kk, i)), pl.BlockSpec((tt, tn), lambda i, j, kk: (kk, j))],
        out_specs=out_spec, out_shape=out_shape, scratch_shapes=[] if in_place else [pltpu.VMEM(block, F32)],
        compiler_params=_params(("parallel", "parallel", "arbitrary")))(a, b)


def ln_fwd(xprev, a, w, g, b, name):
    t, d = xprev.shape
    k = a.shape[1]
    tm = min(TM, t)

    def body(xp_ref, a_ref, w_ref, g_ref, b_ref, y_ref, yb_ref, xh_ref, rs_ref):
        r = DN_ALPHA * xp_ref[...] + jnp.dot(a_ref[...], w_ref[...], preferred_element_type=F32)
        mu = jnp.mean(r, axis=1, keepdims=True)
        xc = r - mu
        var = jnp.mean(xc * xc, axis=1, keepdims=True)
        rstd = lax.rsqrt(var + LN_EPS)
        xh = xc * rstd
        y = xh * g_ref[...] + b_ref[...]
        y_ref[...] = y
        yb_ref[...] = y.astype(BF16)
        xh_ref[...] = xh
        rs_ref[...] = jnp.broadcast_to(rstd, (tm, LANES))

    row = pl.BlockSpec((tm, d), lambda i: (i, 0))
    vec = pl.BlockSpec((1, d), lambda i: (0, 0))
    return pl.pallas_call(
        body, name=name, grid=(t // tm,),
        in_specs=[row, pl.BlockSpec((tm, k), lambda i: (i, 0)), pl.BlockSpec((k, d), lambda i: (0, 0)), vec, vec],
        out_specs=[row, row, row, pl.BlockSpec((tm, LANES), lambda i: (i, 0))],
        out_shape=[SDS((t, d), F32), SDS((t, d), BF16), SDS((t, d), F32), SDS((t, LANES), F32)],
        compiler_params=_params(("parallel",)))(xprev, a, w, g, b)


def ln_bwd(dy, xhat, rstd, g, name, products=(), dy_scale=1.0):
    t, d = dy.shape
    np_ = len(products)
    tm = min(TM if sum(a.shape[1] for a, _, _ in products) <= 4096 else TS, t)

    def body(*refs):
        prod_refs = refs[:2 * np_]
        dy_ref, xh_ref, rs_ref, g_ref, dr_ref, drb_ref, dg_ref, db_ref = refs[2 * np_:]
        i = pl.program_id(0)
        dyv = dy_ref[...] if dy_scale == 1.0 else dy_scale * dy_ref[...]
        for p in range(np_):
            a_ref, w_ref = prod_refs[2 * p], prod_refs[2 * p + 1]
            if len(w_ref.shape) == 2:
                dyv = dyv + lax.dot_general(a_ref[...], w_ref[...], NT_DIMS, preferred_element_type=F32)
            else:
                kb = w_ref.shape[2]
                for c in range(w_ref.shape[0]):
                    dyv = dyv + lax.dot_general(a_ref[:, c * kb:(c + 1) * kb], w_ref[c], NT_DIMS,
                                                preferred_element_type=F32)
        xh = xh_ref[...]
        dxh = dyv * g_ref[...]
        m1 = jnp.mean(dxh, axis=1, keepdims=True)
        m2 = jnp.mean(dxh * xh, axis=1, keepdims=True)
        dr = rs_ref[:, 0:1] * (dxh - m1 - xh * m2)
        dr_ref[...] = dr
        drb_ref[...] = dr.astype(BF16)

        @pl.when(i == 0)
        def _():
            dg_ref[...] = jnp.zeros_like(dg_ref)
            db_ref[...] = jnp.zeros_like(db_ref)

        dg_ref[...] += jnp.sum(dyv * xh, axis=0, keepdims=True)
        db_ref[...] += jnp.sum(dyv, axis=0, keepdims=True)

    row = pl.BlockSpec((tm, d), lambda i: (i, 0))
    vec = pl.BlockSpec((1, d), lambda i: (0, 0))
    in_specs = [row, row, pl.BlockSpec((tm, LANES), lambda i: (i, 0)), vec]
    ops = [dy, xhat, rstd, g]
    for a, w, col in reversed(products):
        k = a.shape[1]
        if w.ndim == 2:
            w_spec = pl.BlockSpec((d, k), lambda i, col=col: (0, col))
        else:
            w_spec = pl.BlockSpec((k // w.shape[2], d, w.shape[2]), lambda i, col=col: (col, 0, 0))
        in_specs = [pl.BlockSpec((tm, k), lambda i: (i, 0)), w_spec] + in_specs
        ops = [a, w] + ops
    return pl.pallas_call(
        body, name=name, grid=(t // tm,), in_specs=in_specs, out_specs=[row, row, vec, vec],
        out_shape=[SDS((t, d), F32), SDS((t, d), BF16), SDS((1, d), F32), SDS((1, d), F32)],
        compiler_params=_params(("arbitrary",)))(*ops)


def loss_ln_bwd(xhat, rstd, g, beta, target, name):
    t, d = xhat.shape
    tm = min(TM, t)
    nsteps = t // tm

    def body(xh_ref, rs_ref, g_ref, b_ref, t_ref, dr_ref, drb_ref, dg_ref, db_ref, l_ref, acc):
        i = pl.program_id(0)
        xh = xh_ref[...]
        diff = xh * g_ref[...] + b_ref[...] - t_ref[...]
        dyv = diff * (1.0 / d)
        dxh = dyv * g_ref[...]
        m1 = jnp.mean(dxh, axis=1, keepdims=True)
        m2 = jnp.mean(dxh * xh, axis=1, keepdims=True)
        dr = rs_ref[:, 0:1] * (dxh - m1 - xh * m2)
        dr_ref[...] = dr
        drb_ref[...] = dr.astype(BF16)

        @pl.when(i == 0)
        def _():
            dg_ref[...] = jnp.zeros_like(dg_ref)
            db_ref[...] = jnp.zeros_like(db_ref)
            acc[...] = jnp.zeros_like(acc)

        dg_ref[...] += jnp.sum(dyv * xh, axis=0, keepdims=True)
        db_ref[...] += jnp.sum(dyv, axis=0, keepdims=True)
        acc[...] += jnp.sum(diff * diff, axis=0, keepdims=True)

        @pl.when(i == nsteps - 1)
        def _():
            tot = jnp.sum(acc[...], axis=1, keepdims=True) * (0.5 / d)
            l_ref[...] = jnp.broadcast_to(tot, (1, LANES))

    row = pl.BlockSpec((tm, d), lambda i: (i, 0))
    vec = pl.BlockSpec((1, d), lambda i: (0, 0))
    return pl.pallas_call(
        body, name=name, grid=(nsteps,),
        in_specs=[row, pl.BlockSpec((tm, LANES), lambda i: (i, 0)), vec, vec, row],
        out_specs=[row, row, vec, vec, pl.BlockSpec((1, LANES), lambda i: (0, 0))],
        out_shape=[SDS((t, d), F32), SDS((t, d), BF16), SDS((1, d), F32), SDS((1, d), F32), SDS((1, LANES), F32)],
        scratch_shapes=[pltpu.VMEM((1, d), F32)],
        compiler_params=_params(("arbitrary",)))(xhat, rstd, g, beta, target)


def memattn_fwd(proj, memkv, tok, nb, s, name):
    ts = min(TM, s)
    nq = s // ts

    def body(q_ref, kv_ref, tok_ref, o_ref):
        o_ref[:, :TOK_WIDTH] = tok_ref[...]
        top = lax.broadcasted_iota(jnp.int32, (PAIR, ts), 0) < HEAD_DIM
        scores = []
        for p in range(MEM_HEADS // 2):
            qp = q_ref[:, p * PAIR:(p + 1) * PAIR].astype(BF16)
            ke, ko = _split_pair(kv_ref[:, p * PAIR:(p + 1) * PAIR], QK_SCALE)
            scores.append([lax.dot_general(km, qp, NT_DIMS, preferred_element_type=F32) for km in (ke, ko)])
        for p in range(MEM_HEADS // 2):
            vt = kv_ref[:, MEM_WIDTH + p * PAIR:MEM_WIDTH + (p + 1) * PAIR].astype(F32).T.astype(BF16)
            outs = []
            for sc in scores[p]:
                e = jnp.exp(sc - jnp.max(sc, axis=0, keepdims=True))
                pr = e / jnp.sum(e, axis=0, keepdims=True)
                outs.append(jnp.dot(vt, pr.astype(BF16), preferred_element_type=F32))
            o_ref[:, TOK_WIDTH + p * PAIR:TOK_WIDTH + (p + 1) * PAIR] = jnp.where(top, outs[0], outs[1]).T.astype(BF16)

    return pl.pallas_call(
        body, name=name, grid=(nb, nq),
        in_specs=[pl.BlockSpec((ts, MEM_WIDTH), lambda b, i: (b * nq + i, 3)),
                  pl.BlockSpec((MEM_LEN, 2 * MEM_WIDTH), lambda b, i: (b, 0)),
                  pl.BlockSpec((ts, TOK_WIDTH), lambda b, i: (b * nq + i, 0))],
        out_specs=pl.BlockSpec((ts, TOK_WIDTH + MEM_WIDTH), lambda b, i: (b * nq + i, 0)),
        out_shape=SDS((nb * s, TOK_WIDTH + MEM_WIDTH), BF16),
        compiler_params=_params(("parallel", "parallel")))(proj, memkv, tok)


def memattn_bwd(proj, memkv, dcat, dtok, nb, s, name):
    ts = min(TM, s)
    nq = s // ts

    def body(q_ref, kv_ref, do_ref, dtok_ref, dq_ref, dkv_ref):
        i = pl.program_id(1)
        dq_ref[:, :TOK_WIDTH] = dtok_ref[...]

        @pl.when(i == 0)
        def _():
            dkv_ref[...] = jnp.zeros_like(dkv_ref)

        lo = _half_masks(MEM_LEN)
        top = lax.broadcasted_iota(jnp.int32, (PAIR, ts), 0) < HEAD_DIM
        n_pairs = MEM_HEADS // 2
        qs, dos, kps, products = [], [], [], []
        for p in range(n_pairs):
            qp = q_ref[:, p * PAIR:(p + 1) * PAIR].astype(BF16)
            dop = do_ref[:, p * PAIR:(p + 1) * PAIR].astype(BF16)
            kp = kv_ref[:, p * PAIR:(p + 1) * PAIR] * QK_SCALE
            kms = _split_pair(kp)
            vms = _split_pair(kv_ref[:, MEM_WIDTH + p * PAIR:MEM_WIDTH + (p + 1) * PAIR])
            products.append([(lax.dot_general(km, qp, NT_DIMS, preferred_element_type=F32),
                              lax.dot_general(vm, dop, NT_DIMS, preferred_element_type=F32)) for km, vm in zip(kms, vms)])
            qs.append(qp)
            dos.append(dop)
            kps.append(kp)
        for p in range(n_pairs):
            kt = kps[p].astype(F32).T.astype(BF16)
            dks, dvs, dqs = [], [], []
            for sc, dp in products[p]:
                e = jnp.exp(sc - jnp.max(sc, axis=0, keepdims=True))
                pr = e / jnp.sum(e, axis=0, keepdims=True)
                dl = jnp.sum(pr * dp, axis=0, keepdims=True)
                ds = (pr * (dp - dl)).astype(BF16)
                dvs.append(jnp.dot(pr.astype(BF16), dos[p], preferred_element_type=F32))
                dks.append(jnp.dot(ds, qs[p], preferred_element_type=F32))
                dqs.append(jnp.dot(kt, ds, preferred_element_type=F32))
            dq_ref[:, TOK_WIDTH + p * PAIR:TOK_WIDTH + (p + 1) * PAIR] = jnp.where(top, dqs[0], dqs[1]).T.astype(BF16)
            dkv_ref[:, p * PAIR:(p + 1) * PAIR] += jnp.where(lo, dks[0], dks[1]) * QK_SCALE
            dkv_ref[:, MEM_WIDTH + p * PAIR:MEM_WIDTH + (p + 1) * PAIR] += jnp.where(lo, dvs[0], dvs[1])

    return pl.pallas_call(
        body, name=name, grid=(nb, nq),
        in_specs=[pl.BlockSpec((ts, MEM_WIDTH), lambda b, i: (b * nq + i, 3)),
                  pl.BlockSpec((MEM_LEN, 2 * MEM_WIDTH), lambda b, i: (b, 0)),
                  pl.BlockSpec((ts, MEM_WIDTH), lambda b, i: (b * nq + i, 3)),
                  pl.BlockSpec((ts, TOK_WIDTH), lambda b, i: (b * nq + i, 0))],
        out_specs=[pl.BlockSpec((ts, TOK_WIDTH + MEM_WIDTH), lambda b, i: (b * nq + i, 0)),
                   pl.BlockSpec((MEM_LEN, 2 * MEM_WIDTH), lambda b, i: (b, 0))],
        out_shape=[SDS((nb * s, TOK_WIDTH + MEM_WIDTH), BF16), SDS((nb * MEM_LEN, 2 * MEM_WIDTH), F32)],
        compiler_params=_params(("parallel", "arbitrary")))(proj, memkv, dcat, dtok)


def _pool_select(shape, s2, s4, s8, s16):
    lane = lax.broadcasted_iota(jnp.int32, shape, 1)
    return jnp.where(lane < POOL_GROUP, s2, jnp.where(lane < 2 * POOL_GROUP, s4, jnp.where(lane < 3 * POOL_GROUP, s8, s16)))


def _pool_count(shape, first_pos):
    pos = first_pos + lax.broadcasted_iota(jnp.int32, shape, 0)
    win = _pool_select(shape, 2, 4, 8, 16)
    return jnp.minimum(pos + 1, win).astype(F32)


def pool_fwd(proj, pw_bd, pscale, nb, s):
    ts = min(TS, s)
    nq = s // ts
    w = TOK_WIDTH

    def body(c_ref, h_ref, w_ref, sc_ref, pooled_ref, tok_ref):
        i = pl.program_id(0) % nq
        cur = c_ref[...]
        halo = jnp.where(i == 0, 0.0, h_ref[...])
        xe = jnp.concatenate([halo, cur], axis=0)
        s2 = xe + pltpu.roll(xe, 1, axis=0)
        s4 = s2 + pltpu.roll(s2, 2, axis=0)
        s8 = s4 + pltpu.roll(s4, 4, axis=0)
        s16 = s8 + pltpu.roll(s8, 8, axis=0)
        hp = HALO_POOL
        ws = _pool_select((ts, w), s2[hp:], s4[hp:], s8[hp:], s16[hp:])
        pooled = (ws / _pool_count((ts, w), i * ts) - cur).astype(BF16)
        pooled_ref[...] = pooled
        mixed = jnp.dot(pooled, w_ref[...], preferred_element_type=F32)
        tok_ref[...] = (mixed * sc_ref[...]).astype(BF16)

    row = pl.BlockSpec((ts, w), lambda r: (r, 0))
    return pl.pallas_call(
        body, name="pool_fwd", grid=(nb * nq,),
        in_specs=[row, pl.BlockSpec((HALO_POOL, w), lambda r: (jnp.maximum(r * (ts // HALO_POOL) - 1, 0), 0)),
                  pl.BlockSpec((w, w), lambda r: (0, 0)), pl.BlockSpec((1, w), lambda r: (0, 0))],
        out_specs=[row, row], out_shape=[SDS((nb * s, w), BF16), SDS((nb * s, w), BF16)],
        compiler_params=_params(("parallel",)))(proj, proj, pw_bd, pscale)


def pool_bwd_mix(dcat, pooled, pw_bd, pscale, nb, s):
    ts = min(TS, s)
    w = TOK_WIDTH

    def body(dt_ref, p_ref, w_ref, sc_ref, dm_ref, dp_ref, ds_ref):
        r = pl.program_id(0)
        dtok = dt_ref[...]
        mixed = jnp.dot(p_ref[...], w_ref[...], preferred_element_type=F32)

        @pl.when(r == 0)
        def _():
            ds_ref[...] = jnp.zeros_like(ds_ref)

        ds_ref[...] += jnp.sum(dtok * mixed, axis=0, keepdims=True)
        dmx = (dtok * sc_ref[...]).astype(BF16)
        dm_ref[...] = dmx
        dp_ref[...] = lax.dot_general(dmx, w_ref[...], NT_DIMS, preferred_element_type=F32)

    row = pl.BlockSpec((ts, w), lambda r: (r, 0))
    mat = pl.BlockSpec((w, w), lambda r: (0, 0))
    vec = pl.BlockSpec((1, w), lambda r: (0, 0))
    return pl.pallas_call(
        body, name="pool_bwd_mix", grid=(nb * s // ts,), in_specs=[row, row, mat, vec],
        out_specs=[row, row, vec], out_shape=[SDS((nb * s, w), BF16), SDS((nb * s, w), F32), SDS((1, w), F32)],
        compiler_params=_params(("arbitrary",)))(dcat, pooled, pw_bd, pscale)


def pool_bwd_window(dpooled, nb, s):
    ts = min(TS, s)
    nq = s // ts
    w = TOK_WIDTH
    n_ext = ts + HALO_POOL
    n_halo_blocks = nb * s // HALO_POOL

    def body(c_ref, n_ref, du_ref):
        i = pl.program_id(0) % nq
        cur = c_ref[...]
        nxt = jnp.where(i == nq - 1, 0.0, n_ref[...])
        ze = jnp.concatenate([cur, nxt], axis=0) / _pool_count((n_ext, w), i * ts)
        s2 = ze + pltpu.roll(ze, n_ext - 1, axis=0)
        s4 = s2 + pltpu.roll(s2, n_ext - 2, axis=0)
        s8 = s4 + pltpu.roll(s4, n_ext - 4, axis=0)
        s16 = s8 + pltpu.roll(s8, n_ext - 8, axis=0)
        ws = _pool_select((ts, w), s2[:ts], s4[:ts], s8[:ts], s16[:ts])
        du_ref[...] = (ws - cur).astype(BF16)

    row = pl.BlockSpec((ts, w), lambda r: (r, 0))
    return pl.pallas_call(
        body, name="pool_bwd_window", grid=(nb * nq,),
        in_specs=[row, pl.BlockSpec((HALO_POOL, w),
                                    lambda r: (jnp.minimum((r + 1) * (ts // HALO_POOL), n_halo_blocks - 1), 0))],
        out_specs=row, out_shape=SDS((nb * s, w), BF16),
        compiler_params=_params(("parallel",)))(dpooled, dpooled)


def _conv_rows(xe, w_ref):
    return (w_ref[0, 2:3, :] * xe + w_ref[0, 1:2, :] * pltpu.roll(xe, 1, axis=0)
            + w_ref[0, 0:1, :] * pltpu.roll(xe, 2, axis=0) + w_ref[0, 3:4, :])


def ffn_up_gate(x_bf, wup, cw, nb, s, name):
    tm = min(2 * TM, s)
    nq = s // tm
    w = FF_BLOCK_PAD
    hr = 2 * HALO_CONV
    k = x_bf.shape[1]

    def body(xc_ref, xh_ref, wu_ref, wg_ref, cu_ref, cg_ref, act_ref, a_ref, b_ref, hu_ref, hg_ref):
        first = (pl.program_id(1) % nq) == 0
        xc = xc_ref[...]
        xh = xh_ref[...]

        def products(w_ref):
            return (jnp.dot(xc, w_ref[0], preferred_element_type=F32), jnp.dot(xh, w_ref[0], preferred_element_type=F32))

        def conv(hcur, hprev, c_ref, h_out):
            h_out[...] = hcur.astype(BF16)
            xe = jnp.concatenate([jnp.where(first, 0.0, hprev), hcur], axis=0)
            return _conv_rows(xe, c_ref)[hr:]

        pu, pg = products(wu_ref), products(wg_ref)
        cu = conv(*pu, cu_ref, hu_ref)
        cg = conv(*pg, cg_ref, hg_ref)
        sg = _sigmoid(cg)
        a = cg * sg
        act_ref[...] = (a * cu).astype(BF16)
        a_ref[...] = a.astype(BF16)
        b_ref[...] = (cu * (sg * (1.0 + cg * (1.0 - sg)))).astype(BF16)

    def wblock(off):
        return pl.BlockSpec((1, k, w), lambda j, r: (j + off, 0, 0))

    def cblock(off):
        return pl.BlockSpec((1, 8, w), lambda j, r: (j + off, 0, 0))

    tile = pl.BlockSpec((tm, w), lambda j, r: (r, j))
    out = SDS((nb * s, FF_PAIRS * w), BF16)
    return pl.pallas_call(
        body, name=name, grid=(FF_PAIRS, nb * nq),
        in_specs=[pl.BlockSpec((tm, k), lambda j, r: (r, 0)),
                  pl.BlockSpec((hr, k), lambda j, r: (jnp.maximum(r * (tm // hr) - 1, 0), 0)),
                  wblock(0), wblock(FF_PAIRS), cblock(0), cblock(FF_PAIRS)],
        out_specs=[tile] * 5, out_shape=[out] * 5,
        compiler_params=_params(("parallel", "parallel")))(x_bf, x_bf, wup, wup, cw, cw)


def gate_conv_bwd(dact, a, b, hu, hg, cw, nb, s, name):
    ts = min(TS, s)
    nq = s // ts
    w = FF_BLOCK_PAD
    hc = HALO_CONV
    hb = 2 * hc
    n_ext = ts + hc

    def body(dc_ref, dn_ref, ac_ref, an_ref, bc_ref, bn_ref, hu_ref, hg_ref, wu_ref, wg_ref,
             dhu_ref, dhg_ref, dwu_ref, dwg_ref):
        r = pl.program_id(1)
        last = (r % nq) == nq - 1

        def ext(c_ref, n_ref, mask_next=False):
            nxt = n_ref[...].astype(F32)[:hc]
            if mask_next:
                nxt = jnp.where(last, 0.0, nxt)
            return jnp.concatenate([c_ref[...].astype(F32), nxt], axis=0)

        da = ext(dc_ref, dn_ref, mask_next=True)

        def branch(dcv, w_ref, h_ref, dh_ref, dw_ref):
            d0 = dcv[:ts]
            d1 = pltpu.roll(dcv, n_ext - 1, axis=0)[:ts]
            d2 = pltpu.roll(dcv, n_ext - 2, axis=0)[:ts]
            dh_ref[...] = (w_ref[0, 2:3, :] * d0 + w_ref[0, 1:2, :] * d1 + w_ref[0, 0:1, :] * d2).astype(BF16)
            hv = h_ref[...].astype(F32)
            rows = [jnp.sum(d2 * hv, axis=0, keepdims=True), jnp.sum(d1 * hv, axis=0, keepdims=True),
                    jnp.sum(d0 * hv, axis=0, keepdims=True), jnp.sum(d0, axis=0, keepdims=True)]
            sub = lax.broadcasted_iota(jnp.int32, (8, w), 0)
            upd = jnp.zeros((8, w), F32)
            for kk, rv in enumerate(rows):
                upd = jnp.where(sub == kk, rv, upd)

            @pl.when(r == 0)
            def _():
                dw_ref[...] = jnp.zeros_like(dw_ref)

            dw_ref[...] += upd[None]

        branch(da * ext(ac_ref, an_ref), wu_ref, hu_ref, dhu_ref, dwu_ref)
        branch(da * ext(bc_ref, bn_ref), wg_ref, hg_ref, dhg_ref, dwg_ref)

    cur = pl.BlockSpec((ts, w), lambda j, r: (r, j))
    nxt = pl.BlockSpec((hb, w), lambda j, r: (jnp.minimum((r + 1) * (ts // hb), nb * s // hb - 1), j))

    def wspec(off):
        return pl.BlockSpec((1, 8, w), lambda j, r: (j + off, 0, 0))

    p = FF_PAIRS
    dw_spec = pl.BlockSpec((1, 8, w), lambda j, r: (j, 0, 0))
    return pl.pallas_call(
        body, name=name, grid=(p, nb * nq),
        in_specs=[cur, nxt, cur, nxt, cur, nxt, cur, cur, wspec(0), wspec(p)],
        out_specs=[cur, cur, dw_spec, dw_spec],
        out_shape=[SDS((nb * s, p * w), BF16), SDS((nb * s, p * w), BF16), SDS((p, 8, w), F32), SDS((p, 8, w), F32)],
        compiler_params=_params(("parallel", "arbitrary")))(dact, dact, a, a, b, b, hu, hg, cw, cw)


def _tri(n, upper):
    r = lax.broadcasted_iota(jnp.int32, (n, n), 0)
    c = lax.broadcasted_iota(jnp.int32, (n, n), 1)
    return ((r <= c) if upper else (r >= c)).astype(F32)


def fgate_fwd(fl, fb, nb, s):
    tc = min(TC, s)
    nq = s // tc

    def body(fl_ref, fb_ref, f_ref, carry):
        @pl.when(pl.program_id(1) == 0)
        def _():
            carry[...] = jnp.zeros_like(carry)

        z = fl_ref[...] + fb_ref[...]
        logf = jnp.minimum(z, 0.0) - jnp.log(1.0 + jnp.exp(-jnp.abs(z)))
        f_ref[...] = jnp.dot(_tri(tc, False), logf, preferred_element_type=F32,
                             precision=lax.Precision.HIGHEST) + carry[...]
        carry[...] += jnp.sum(logf, axis=0, keepdims=True)

    row = pl.BlockSpec((tc, LANES), lambda b, i: (b * nq + i, 0))
    return pl.pallas_call(
        body, name="fgate_fwd", grid=(nb, nq), in_specs=[row, pl.BlockSpec((1, LANES), lambda b, i: (0, 0))],
        out_specs=row, out_shape=SDS((nb * s, LANES), F32), scratch_shapes=[pltpu.VMEM((1, LANES), F32)],
        compiler_params=_params(("arbitrary", "arbitrary")))(fl, fb)


def fgate_bwd(d_cum_q, d_cum_k, fl, fb, dk, dv, nb, s):
    tc = min(TC, s)
    nq = s // tc

    def body(dfq_ref, dfk_ref, fl_ref, fb_ref, dk_ref, dv_ref, dkvf_ref, dfb_ref, carry):
        b = pl.program_id(0)
        i = pl.program_id(1)

        @pl.when(i == 0)
        def _():
            carry[...] = jnp.zeros_like(carry)

        @pl.when(jnp.logical_and(b == 0, i == 0))
        def _():
            dfb_ref[...] = jnp.zeros_like(dfb_ref)

        dfv = dfq_ref[...] + dfk_ref[...]
        dlog = jnp.dot(_tri(tc, True), dfv, preferred_element_type=F32,
                       precision=lax.Precision.HIGHEST) + carry[...]
        carry[...] += jnp.sum(dfv, axis=0, keepdims=True)
        z = fl_ref[...] + fb_ref[...]
        dfl = dlog / (1.0 + jnp.exp(z))
        dkvf_ref[:, :TOK_WIDTH] = dk_ref[...]
        dkvf_ref[:, TOK_WIDTH:2 * TOK_WIDTH] = dv_ref[...]
        dkvf_ref[:, 2 * TOK_WIDTH:] = dfl.astype(BF16)
        dfb_ref[...] += jnp.sum(dfl, axis=0, keepdims=True)

    def rows(width):
        return pl.BlockSpec((tc, width), lambda b, i: (b * nq + nq - 1 - i, 0))

    row = rows(LANES)
    vec = pl.BlockSpec((1, LANES), lambda b, i: (0, 0))
    return pl.pallas_call(
        body, name="fgate_bwd", grid=(nb, nq), in_specs=[row, row, row, vec, rows(TOK_WIDTH), rows(TOK_WIDTH)],
        out_specs=[rows(KV_COLS_PAD), vec],
        out_shape=[SDS((nb * s, KV_COLS_PAD), BF16), SDS((1, LANES), F32)], scratch_shapes=[pltpu.VMEM((1, LANES), F32)],
        compiler_params=_params(("arbitrary", "arbitrary")))(d_cum_q, d_cum_k, fl, fb, dk, dv)


PAIR = 2 * HEAD_DIM
N_PAIRS = FOX_HEADS // 2


def _lane_put(shape, h, col):
    lane = lax.broadcasted_iota(jnp.int32, shape, 1)
    return jnp.where(lane == h, col, 0.0)


def _half_masks(rows):
    lane = lax.broadcasted_iota(jnp.int32, (rows, PAIR), 1)
    return lane < HEAD_DIM


def _split_pair(x, scale=None):
    if scale is not None:
        x = x * scale
    lo = _half_masks(x.shape[0])
    zero = jnp.zeros_like(x)
    return jnp.where(lo, x, zero), jnp.where(lo, zero, x)


def _to_tile_rows(a, nb, s, tf):
    return a.reshape(nb * s // tf, tf, LANES)[:, :, :16].transpose(0, 2, 1)


def _from_tile_rows(a):
    tiles, _, tf = a.shape
    return jnp.pad(a.transpose(0, 2, 1), ((0, 0), (0, 0), (0, LANES - 16))).reshape(tiles * tf, LANES)


BIAS_TERMS = 3
LOOKAHEAD = 4
FOLLOW_FWD = 1
LOOKAHEAD_BWD = 2
FOLLOW_BWD = 1


def _bias_lane(h):
    return HEAD_DIM if h % 2 == 0 else 0


def _placement():
    rows = jnp.arange(LANES)[:, None]
    cols = jnp.arange(FOX_HEADS * PAIR)[None, :]
    head, lane = cols // PAIR, cols % PAIR
    first = jnp.where(head % 2 == 0, HEAD_DIM, 0)
    term = lane - first
    hit = (term >= 0) & (term < BIAS_TERMS) & (rows == 16 * term + head)
    return hit.astype(BF16)


def fox_prep(kv, fneg, nb, s):
    tf = min(TF, s)
    w = TOK_WIDTH

    def body(k_ref, v_ref, f_ref, pl_ref, ka_ref, vt_ref):
        lane = lax.broadcasted_iota(jnp.int32, (tf, LANES), 1)
        lo = lane < HEAD_DIM
        f = jnp.where(lane < FOX_HEADS, f_ref[...], 0.0)
        hi = f.astype(BF16).astype(F32)
        mid = (f - hi).astype(BF16).astype(F32)
        low = (f - hi - mid).astype(BF16).astype(F32)
        terms = (hi + pltpu.roll(mid, 16, axis=1) + pltpu.roll(low, 32, axis=1)).astype(BF16)
        placed = jnp.dot(terms, pl_ref[...], preferred_element_type=F32).astype(BF16)
        one = jnp.ones((tf, LANES), BF16)
        zero = jnp.zeros((tf, LANES), BF16)
        for p in range(N_PAIRS):
            kp = k_ref[:, p * PAIR:(p + 1) * PAIR] * QK_SCALE
            vp = v_ref[:, p * PAIR:(p + 1) * PAIR]
            he, ho = 2 * p, 2 * p + 1
            ka_ref[:, he * PAIR:(he + 1) * PAIR] = jnp.where(lo, kp, placed[:, he * PAIR:(he + 1) * PAIR])
            ka_ref[:, ho * PAIR:(ho + 1) * PAIR] = jnp.where(lo, placed[:, ho * PAIR:(ho + 1) * PAIR], kp)
            ve = jnp.where(lo, vp, jnp.where(lane == HEAD_DIM, one, zero))
            vo = jnp.where(lo, jnp.where(lane == 0, one, zero), vp)
            vt_ref[0, he * PAIR:(he + 1) * PAIR, :] = ve.astype(F32).T.astype(BF16)
            vt_ref[0, ho * PAIR:(ho + 1) * PAIR, :] = vo.astype(F32).T.astype(BF16)

    return pl.pallas_call(
        body, name="fox_prep", grid=(nb * s // tf,),
        in_specs=[pl.BlockSpec((tf, w), lambda r: (r, 0)), pl.BlockSpec((tf, w), lambda r: (r, 1)),
                  pl.BlockSpec((tf, LANES), lambda r: (r, 0)), pl.BlockSpec((LANES, FOX_HEADS * PAIR), lambda r: (0, 0))],
        out_specs=[pl.BlockSpec((tf, FOX_HEADS * PAIR), lambda r: (r, 0)),
                   pl.BlockSpec((1, FOX_HEADS * PAIR, tf), lambda r: (r, 0, 0))],
        out_shape=[SDS((nb * s, FOX_HEADS * PAIR), BF16), SDS((nb * s // tf, FOX_HEADS * PAIR, tf), BF16)],
        compiler_params=_params(("parallel",)))(kv, kv, fneg, _placement())


def fox_fwd_t(pq, kaug, vaug_t, nb, s):
    tf = min(TF, s)
    n = s // tf
    w = TOK_WIDTH
    wa = FOX_HEADS * PAIR

    def body(q_ref, k_hbm, vt_hbm, ob_ref, of_ref, lse_ref, k_vm, vt_vm, qx_scr, m_scr, acc_scr, sems):
        b = pl.program_id(0)
        i = pl.program_id(1)

        @pl.when(i == 0)
        def _():
            ck = pltpu.make_async_copy(k_hbm.at[pl.ds(pl.multiple_of(b * s, tf), s)], k_vm, sems.at[0])
            cv = pltpu.make_async_copy(vt_hbm.at[pl.ds(b * n, n)], vt_vm, sems.at[1])
            ck.start()
            cv.start()
            ck.wait()
            cv.wait()

        lane = lax.broadcasted_iota(jnp.int32, (tf, PAIR), 1)
        one = jnp.ones((tf, PAIR), BF16)
        zero = jnp.zeros((tf, PAIR), BF16)
        for p in range(N_PAIRS):
            qp = q_ref[:, p * PAIR:(p + 1) * PAIR]
            be, bo = _bias_lane(2 * p), _bias_lane(2 * p + 1)
            ones_e = jnp.where((lane >= be) & (lane < be + BIAS_TERMS), one, zero)
            ones_o = jnp.where((lane >= bo) & (lane < bo + BIAS_TERMS), one, zero)
            qx_scr[2 * p] = jnp.where(lane < HEAD_DIM, qp, ones_e)
            qx_scr[2 * p + 1] = jnp.where(lane < HEAD_DIM, ones_o, qp)
        m_scr[...] = jnp.full(m_scr.shape, NEG_BIG, F32)
        acc_scr[...] = jnp.zeros_like(acc_scr)

        def tile(j, masked):
            ks = pl.multiple_of(j * tf, tf)
            if masked:
                keep = lax.broadcasted_iota(jnp.int32, (tf, tf), 1) >= lax.broadcasted_iota(jnp.int32, (tf, tf), 0)
            def scores(h):
                kx = k_vm[pl.ds(ks, tf), h * PAIR:(h + 1) * PAIR]
                return lax.dot_general(kx, qx_scr[h], NT_DIMS, preferred_element_type=F32)

            def values(h, pr, a):
                pv = jnp.dot(vt_vm[j, h * PAIR:(h + 1) * PAIR, :], pr, preferred_element_type=F32)
                acc_scr[h] = a * acc_scr[h] + pv

            ahead = [scores(h) for h in range(LOOKAHEAD)]
            behind = []
            for h in range(FOX_HEADS):
                sc = ahead.pop(0)
                if h + LOOKAHEAD < FOX_HEADS:
                    ahead.append(scores(h + LOOKAHEAD))
                if masked:
                    sc = jnp.where(keep, sc, NEG_BIG)
                m_prev = m_scr[h]
                m_new = jnp.maximum(m_prev, jnp.max(sc, axis=0, keepdims=True))
                m_scr[h] = m_new
                behind.append((h, jnp.exp(sc - m_new).astype(BF16), jnp.exp(m_prev - m_new)))
                if len(behind) > FOLLOW_FWD:
                    values(*behind.pop(0))
            for item in behind:
                values(*item)

        def step(j, carry):
            tile(j, False)
            return carry

        lax.fori_loop(0, i, step, 0)
        tile(i, True)

        top = lax.broadcasted_iota(jnp.int32, (PAIR, tf), 0) < HEAD_DIM
        sub = lax.broadcasted_iota(jnp.int32, (16, tf), 0)
        lse = jnp.zeros((16, tf), F32)
        for p in range(N_PAIRS):
            he, ho = 2 * p, 2 * p + 1
            le = acc_scr[he, HEAD_DIM:HEAD_DIM + 1, :]
            lod = acc_scr[ho, 0:1, :]
            o = jnp.where(top, acc_scr[he] / le, acc_scr[ho] / lod).T
            ob_ref[:, p * PAIR:(p + 1) * PAIR] = o.astype(BF16)
            of_ref[:, p * PAIR:(p + 1) * PAIR] = o
            lse = jnp.where(sub == he, m_scr[he] + jnp.log(le), lse)
            lse = jnp.where(sub == ho, m_scr[ho] + jnp.log(lod), lse)
        lse_ref[0] = lse

    qrow = lambda b, i: (b * n + i, 0)
    return pl.pallas_call(
        body, name="fox_fwd", grid=(nb, n),
        in_specs=[pl.BlockSpec((tf, w), qrow), ANY_SPEC, ANY_SPEC],
        out_specs=[pl.BlockSpec((tf, w), qrow), pl.BlockSpec((tf, w), qrow),
                   pl.BlockSpec((1, 16, tf), lambda b, i: (b * n + i, 0, 0))],
        out_shape=[SDS((nb * s, w), BF16), SDS((nb * s, w), F32), SDS((nb * n, 16, tf), F32)],
        scratch_shapes=[pltpu.VMEM((s, wa), BF16), pltpu.VMEM((n, wa, tf), BF16),
                        pltpu.VMEM((FOX_HEADS, tf, PAIR), BF16), pltpu.VMEM((FOX_HEADS, 1, tf), F32),
                        pltpu.VMEM((FOX_HEADS, PAIR, tf), F32), pltpu.SemaphoreType.DMA((2,))],
        compiler_params=_params(("arbitrary", "arbitrary")))(pq, kaug, vaug_t)


def fox_delta(dcat, o, nb, s):
    tf = min(TM, s)
    w = TOK_WIDTH

    def body(do_ref, o_ref, dl_ref):
        out = jnp.zeros((tf, LANES), F32)
        for h in range(FOX_HEADS):
            lo, hi = h * HEAD_DIM, (h + 1) * HEAD_DIM
            out = out + _lane_put((tf, LANES), h, jnp.sum(do_ref[:, lo:hi] * o_ref[:, lo:hi], axis=1, keepdims=True))
        dl_ref[...] = out

    row = pl.BlockSpec((tf, w), lambda r: (r, 0))
    return pl.pallas_call(
        body, name="fox_delta", grid=(nb * s // tf,), in_specs=[row, row],
        out_specs=pl.BlockSpec((tf, LANES), lambda r: (r, 0)), out_shape=SDS((nb * s, LANES), F32),
        compiler_params=_params(("parallel",)))(dcat, o)


def fox_bwd(pq, kv, fneg, dcat_bf, lse_rows, delta_rows, nb, s):
    tf = min(TF, s)
    n = s // tf
    w = TOK_WIDTH

    def body(q_hbm, k_ref, v_ref, f_ref, do_hbm, lse_ref, dl_ref, dq_ref, dk_ref, dv_ref, dfk_ref, dfq_ref,
             q_vm, do_vm, km_scr, vm_scr, kt_scr, fk_scr, dk_scr, dv_scr, rs_scr, dq_scr, fq_scr, sems):
        b = pl.program_id(0)
        j = pl.program_id(1)

        @pl.when(j == 0)
        def _():
            rows = pl.ds(pl.multiple_of(b * s, tf), s)
            cq = pltpu.make_async_copy(q_hbm.at[rows, pl.ds(0, w)], q_vm, sems.at[0])
            cd = pltpu.make_async_copy(do_hbm.at[rows, pl.ds(0, w)], do_vm, sems.at[1])
            cq.start()
            cd.start()
            dq_scr[...] = jnp.zeros_like(dq_scr)
            fq_scr[...] = jnp.zeros_like(fq_scr)
            cq.wait()
            cd.wait()

        for p in range(N_PAIRS):
            kp = k_ref[:, p * PAIR:(p + 1) * PAIR] * QK_SCALE
            ke, ko = _split_pair(kp)
            km_scr[2 * p] = ke
            km_scr[2 * p + 1] = ko
            kt_scr[p] = kp.astype(F32).T.astype(BF16)
            ve, vo = _split_pair(v_ref[:, p * PAIR:(p + 1) * PAIR])
            vm_scr[2 * p] = ve
            vm_scr[2 * p + 1] = vo
        for h in range(FOX_HEADS):
            fk_scr[h] = jnp.broadcast_to(f_ref[:, h:h + 1], (tf, tf))
        dk_scr[...] = jnp.zeros_like(dk_scr)
        dv_scr[...] = jnp.zeros_like(dv_scr)
        rs_scr[...] = jnp.zeros_like(rs_scr)

        def tile(i, masked):
            qs = pl.multiple_of(i * tf, tf)
            if masked:
                keep = lax.broadcasted_iota(jnp.int32, (tf, tf), 1) >= lax.broadcasted_iota(jnp.int32, (tf, tf), 0)
            def products(h):
                qp = q_vm[pl.ds(qs, tf), (h // 2) * PAIR:(h // 2 + 1) * PAIR]
                dop = do_vm[pl.ds(qs, tf), (h // 2) * PAIR:(h // 2 + 1) * PAIR]
                return (lax.dot_general(km_scr[h], qp, NT_DIMS, preferred_element_type=F32),
                        lax.dot_general(vm_scr[h], dop, NT_DIMS, preferred_element_type=F32))

            def dependents(h, prb, dsb):
                p = h // 2
                half = slice((h % 2) * HEAD_DIM, (h % 2 + 1) * HEAD_DIM)
                qp = q_vm[pl.ds(qs, tf), p * PAIR:(p + 1) * PAIR]
                dop = do_vm[pl.ds(qs, tf), p * PAIR:(p + 1) * PAIR]
                dv_scr[h] += jnp.dot(prb, dop, preferred_element_type=F32)
                dk_scr[h] += jnp.dot(dsb, qp, preferred_element_type=F32)
                dqt = jnp.dot(kt_scr[p], dsb, preferred_element_type=F32)
                dq_scr[i, p, half, :] += dqt[(h % 2) * HEAD_DIM:(h % 2 + 1) * HEAD_DIM]

            ahead = [products(h) for h in range(LOOKAHEAD_BWD)]
            behind = []
            for h in range(FOX_HEADS):
                sc, dp = ahead.pop(0)
                if h + LOOKAHEAD_BWD < FOX_HEADS:
                    ahead.append(products(h + LOOKAHEAD_BWD))
                sc = sc + fk_scr[h] - lse_ref[i, h:h + 1, :]
                if masked:
                    sc = jnp.where(keep, sc, NEG_BIG)
                pr = jnp.exp(sc)
                ds = pr * (dp - dl_ref[i, h:h + 1, :])
                part = ds[:, :LANES]
                for c in range(1, tf // LANES):
                    part = part + ds[:, c * LANES:(c + 1) * LANES]
                rs_scr[h] += part
                fq_scr[i, h:h + 1, :] += jnp.sum(ds, axis=0, keepdims=True)
                behind.append((h, pr.astype(BF16), ds.astype(BF16)))
                if len(behind) > FOLLOW_BWD:
                    dependents(*behind.pop(0))
            for item in behind:
                dependents(*item)

        def step(i, carry):
            tile(i, False)
            return carry

        tile(j, True)
        for p in range(N_PAIRS):
            dq_ref[:, p * PAIR:(p + 1) * PAIR] = dq_scr[j, p].T.astype(BF16)
        dfq_ref[0] = fq_scr[j]
        lax.fori_loop(j + 1, n, step, 0)

        lo = _half_masks(tf)
        dfk = jnp.zeros((tf, LANES), F32)
        for p in range(N_PAIRS):
            dk = jnp.where(lo, dk_scr[2 * p], dk_scr[2 * p + 1]) * QK_SCALE
            dk_ref[:, p * PAIR:(p + 1) * PAIR] = dk.astype(BF16)
            dv_ref[:, p * PAIR:(p + 1) * PAIR] = jnp.where(lo, dv_scr[2 * p], dv_scr[2 * p + 1]).astype(BF16)
            for h in (2 * p, 2 * p + 1):
                dfk = dfk - _lane_put((tf, LANES), h, jnp.sum(rs_scr[h], axis=1, keepdims=True))
        dfk_ref[...] = dfk

    krow = lambda b, j: (b * n + j, 0)
    rows = pl.BlockSpec((n, 16, tf), lambda b, j: (b, 0, 0))
    tile_out = pl.BlockSpec((tf, w), krow)
    return pl.pallas_call(
        body, name="fox_bwd", grid=(nb, n),
        in_specs=[ANY_SPEC, pl.BlockSpec((tf, w), krow), pl.BlockSpec((tf, w), lambda b, j: (b * n + j, 1)),
                  pl.BlockSpec((tf, LANES), krow), ANY_SPEC, rows, rows],
        out_specs=[tile_out, tile_out, tile_out, pl.BlockSpec((tf, LANES), krow),
                   pl.BlockSpec((1, 16, tf), lambda b, j: (b * n + j, 0, 0))],
        out_shape=[SDS((nb * s, w), BF16), SDS((nb * s, w), BF16), SDS((nb * s, w), BF16), SDS((nb * s, LANES), F32),
                   SDS((nb * n, 16, tf), F32)],
        scratch_shapes=[pltpu.VMEM((s, w), BF16), pltpu.VMEM((s, w), BF16),
                        pltpu.VMEM((FOX_HEADS, tf, PAIR), BF16), pltpu.VMEM((FOX_HEADS, tf, PAIR), BF16),
                        pltpu.VMEM((N_PAIRS, PAIR, tf), BF16), pltpu.VMEM((FOX_HEADS, tf, tf), F32),
                        pltpu.VMEM((FOX_HEADS, tf, PAIR), F32), pltpu.VMEM((FOX_HEADS, tf, PAIR), F32),
                        pltpu.VMEM((FOX_HEADS, tf, LANES), F32), pltpu.VMEM((n, N_PAIRS, PAIR, tf), F32),
                        pltpu.VMEM((n, 16, tf), F32), pltpu.SemaphoreType.DMA((2,))],
        compiler_params=_params(("arbitrary", "arbitrary")))(pq, kv, kv, fneg, dcat_bf, lse_rows, delta_rows)


ADAMW_TILE_ELEMS = 128 * 1024


def reduce_adamw(parts, w, m, v, name):
    layers, r, c = w.shape
    tr = r
    for cand in range(16, r, 16):
        if r % cand == 0 and cand * c <= ADAMW_TILE_ELEMS:
            tr = cand
    c1 = 1.0 - ADAM_B1 ** ADAM_STEP
    c2 = 1.0 - ADAM_B2 ** ADAM_STEP

    def body(*refs):
        p_refs = refs[:layers]
        w_ref, m_ref, v_ref, g_out, d_out, m_out, v_out = refs[layers:]

        def update(p_ref):
            g = p_ref[0].astype(F32)
            for k in range(1, N_DEV):
                g = g + p_ref[k].astype(F32)
            mn = ADAM_B1 * m_ref[0] + (1.0 - ADAM_B1) * g
            vn = ADAM_B2 * v_ref[0] + (1.0 - ADAM_B2) * (g * g)
            g_out[0] = g
            m_out[0] = mn
            v_out[0] = vn
            d_out[0] = -ADAM_LR * ((mn / c1) / (jnp.sqrt(vn / c2) + ADAM_EPS) + ADAM_WD * w_ref[0])

        if layers == 1:
            update(p_refs[0])
        else:
            for layer in range(layers):
                pl.when(pl.program_id(0) == layer)(lambda layer=layer: update(p_refs[layer]))

    row = pl.BlockSpec((1, tr, c), lambda l, i: (l, i, 0))
    return pl.pallas_call(
        body, name=name, grid=(layers, r // tr),
        in_specs=[pl.BlockSpec((N_DEV, tr, c), lambda l, i: (0, i, 0))] * layers + [row, row, row],
        out_specs=[row, row, row, row], out_shape=[SDS((layers, r, c), F32)] * 4,
        compiler_params=_params(("parallel", "parallel")))(*parts, w, m, v)


N_PEERS = N_DEV - 1
HBM_SPEC = pl.BlockSpec(memory_space=pltpu.HBM)
SEM_SPEC = pl.BlockSpec(memory_space=pltpu.SEMAPHORE)
ANY_SPEC = pl.BlockSpec(memory_space=pl.ANY)
SPLIT_EFFECT = pltpu.SideEffectType.DATAFLOW_SIDE_EFFECTING


def _peers(with_self=False):
    x, y, c = lax.axis_index("x"), lax.axis_index("y"), lax.axis_index("c")
    peers = []
    for k in range(0 if with_self else 1, N_DEV):
        px = 1 - x if (k >> 2) & 1 else x
        py = 1 - y if (k >> 1) & 1 else y
        pc = 1 - c if k & 1 else c
        peers.append(((px, py, pc), 4 * px + 2 * py + pc))
    return 4 * x + 2 * y + c, peers


def _push(src, dst, send_sems, recv_sems, slot, dev):
    return pltpu.make_async_remote_copy(src_ref=src, dst_ref=dst, send_sem=send_sems.at[slot], recv_sem=recv_sems.at[slot],
                                        device_id=dev, device_id_type=pl.DeviceIdType.MESH)


def _landing_shapes(arrs, scatter):
    return [SDS((N_DEV,) + tuple(a.shape[1:] if sc else a.shape), a.dtype) for a, sc in zip(arrs, scatter)]


def exchange(arrs, scatter, name):
    na = len(arrs)

    def body(*refs):
        ins = refs[:na]
        outs = refs[na:2 * na]
        send_sems, recv_sems, local_sems = refs[2 * na:]
        me, peers = _peers()
        local = []
        remote = []
        for a in range(na):
            lc = pltpu.make_async_copy(ins[a].at[me] if scatter[a] else ins[a], outs[a].at[me], local_sems.at[a])
            lc.start()
            local.append(lc)
            for k, (dev, idx) in enumerate(peers):
                cp = _push(ins[a].at[idx] if scatter[a] else ins[a], outs[a].at[me], send_sems, recv_sems,
                           a * N_PEERS + k, dev)
                cp.start()
                remote.append(cp)
        for a in range(na):
            for k, (dev, idx) in enumerate(peers):
                _push(ins[a].at[me] if scatter[a] else ins[a], outs[a].at[idx], send_sems, recv_sems,
                      a * N_PEERS + k, dev).wait_recv()
        for cp in remote:
            cp.wait_send()
        for lc in local:
            lc.wait()

    return pl.pallas_call(
        body, name=name, in_specs=[HBM_SPEC] * na, out_specs=[HBM_SPEC] * na, out_shape=_landing_shapes(arrs, scatter),
        scratch_shapes=[pltpu.SemaphoreType.DMA((na * N_PEERS,)), pltpu.SemaphoreType.DMA((na * N_PEERS,)),
                        pltpu.SemaphoreType.DMA((na,))])(*arrs)


def exchange_start(arrs, scatter, after, name):
    na = len(arrs)
    lands = [lax.empty(l.shape, l.dtype) for l in _landing_shapes(arrs, scatter)]

    def body(*refs):
        ins = refs[:na]
        land = refs[na:2 * na]
        send_sems, recv_sems = refs[2 * na + 1], refs[2 * na + 2]
        token = refs[-1]
        me, peers = _peers(with_self=True)
        for a in range(na):
            for k, (dev, idx) in enumerate(peers):
                _push(ins[a].at[idx] if scatter[a] else ins[a], land[a].at[me], send_sems, recv_sems,
                      a * N_DEV + k, dev).start()
        token[...] = jnp.zeros_like(token)

    thru = [pltpu.HBM(a.shape, a.dtype) for a in arrs] + [pltpu.HBM(l.shape, l.dtype) for l in lands]
    res = pl.pallas_call(
        body, name=name,
        out_shape=(pltpu.SemaphoreType.DMA((na * N_DEV,)), pltpu.SemaphoreType.DMA((na * N_DEV,)), *thru,
                   SDS((8, LANES), F32)),
        in_specs=[HBM_SPEC] * (2 * na) + [ANY_SPEC],
        out_specs=(SEM_SPEC, SEM_SPEC, *([HBM_SPEC] * (2 * na)), pl.BlockSpec(memory_space=pltpu.VMEM)),
        input_output_aliases={i: 2 + i for i in range(2 * na)},
        compiler_params=pltpu.CompilerParams(has_side_effects=SPLIT_EFFECT),
    )(*[pltpu.with_memory_space_constraint(a, pltpu.HBM) for a in arrs],
      *[pltpu.with_memory_space_constraint(l, pltpu.HBM) for l in lands], after)
    return {"send": res[0], "recv": res[1], "src": res[2:2 + na], "land": res[2 + na:2 + 2 * na],
            "token": res[-1][0, 0], "scatter": scatter}


def exchange_wait(handle, after, name):
    scatter = handle["scatter"]
    na = len(scatter)

    def body(*refs):
        src = refs[:na]
        land = refs[na:2 * na]
        send_sems, recv_sems = refs[2 * na], refs[2 * na + 1]
        me, peers = _peers(with_self=True)
        for a in range(na):
            for k, (dev, idx) in enumerate(peers):
                cp = _push(src[a].at[me] if scatter[a] else src[a], land[a].at[idx], send_sems, recv_sems,
                           a * N_DEV + k, dev)
                cp.wait_send()
                cp.wait_recv()

    ops = list(handle["src"]) + list(handle["land"])
    res = pl.pallas_call(
        body, name=name, out_shape=tuple(pltpu.HBM(o.shape, o.dtype) for o in ops),
        in_specs=[HBM_SPEC] * (2 * na) + [SEM_SPEC, SEM_SPEC, ANY_SPEC], out_specs=tuple([HBM_SPEC] * (2 * na)),
        input_output_aliases={i: i for i in range(2 * na)},
        compiler_params=pltpu.CompilerParams(has_side_effects=SPLIT_EFFECT),
    )(*ops, handle["send"], handle["recv"], after)
    return list(res[na:])


def forward_layer(l, xin, xin_bf, mem_bf, wt, nb, s, ffn_weights=None):
    sv = {"xin_bf": xin_bf}
    memkv = mm_nn(mem_bf, wt["memw"], BF16, f"memkv{l}")
    sv["memkv"] = memkv
    if l == 0:
        proj = mm_nn(xin_bf, wt["win_a"], F32, "proj_a")
        pooled, tok = pool_fwd(proj, wt["pw_bd"], wt["pscale"], nb, s)
        sv["pooled"] = pooled
    else:
        kv = mm_nn(xin_bf, wt["kvw"][:, :2 * TOK_WIDTH], BF16, "kv_proj")
        fl = mm_nn(xin_bf, wt["kvw"][:, 2 * TOK_WIDTH:], F32, "gate_proj")
        fneg = -fgate_fwd(fl, wt["fb"], nb, s)
        proj = mm_nn(xin_bf, wt["wq"], BF16, "proj_b")
        kaug, vaug_t = fox_prep(kv, fneg, nb, s)
        tok, o_f32, lse_rows = fox_fwd_t(proj, kaug, vaug_t, nb, s)
        sv.update(kv=kv, fl=fl, fneg=fneg, o_f32=o_f32, lse_rows=lse_rows)
    sv["proj"] = proj
    cat = memattn_fwd(proj, memkv, tok, nb, s, f"memattn_fwd{l}")
    sv["cat"] = cat
    x1, x1_bf, xh1, rs1 = ln_fwd(xin, cat, wt["wout"], wt["ln1_g"], wt["ln1_b"], f"out_proj_ln1_{l}")
    sv.update(x1_bf=x1_bf, xh1=xh1, rs1=rs1)
    if ffn_weights is not None:
        wt.update(ffn_weights(x1_bf))
    act, ga, gb, hu, hg = ffn_up_gate(x1_bf, wt["wup"], wt["cw"], nb, s, f"ffn_up_gate{l}")
    sv.update(act=act, ga=ga, gb=gb, hu=hu, hg=hg)
    x2, x2_bf, xh2, rs2 = ln_fwd(x1, act, wt["wdown"], wt["ln2_g"], wt["ln2_b"], f"ffn_down_ln2_{l}")
    sv.update(xh2=xh2, rs2=rs2)
    return x2, x2_bf, sv


def backward_layer(l, dy, sv, mem_bf, wt, nb, s, after_ffn=None, after_pool=None, loss_target=None):
    g = {}
    if loss_target is None:
        dr2, dr2_bf, g["ln2_g"], g["ln2_b"] = ln_bwd(dy[0], sv["xh2"], sv["rs2"], wt["ln2_g"], f"ln2_bwd{l}",
                                                     dy_scale=dy[1], products=dy[2])
    else:
        dr2, dr2_bf, g["ln2_g"], g["ln2_b"], g["loss_row"] = loss_ln_bwd(sv["xh2"], sv["rs2"], wt["ln2_g"], wt["ln2_b"],
                                                                         loss_target, f"loss_ln2_bwd{l}")
    dact = mm_nn(dr2_bf, wt["wdown"], BF16, f"ffn_down_dx{l}", trans_b=0)
    g["wdown"] = mm_tn(sv["act"], dr2_bf, f"ffn_down_dw{l}")
    dh_u, dh_g, dcw_u, dcw_g = gate_conv_bwd(dact, sv["ga"], sv["gb"], sv["hu"], sv["hg"], wt["cw"], nb, s,
                                             f"gate_conv_bwd{l}")
    g["cw"] = jnp.concatenate([dcw_u, dcw_g], axis=0)
    g["wup"] = jnp.concatenate([mm_tn(sv["x1_bf"], dh_u, f"ffn_up_dw_u{l}", blocked=True),
                                mm_tn(sv["x1_bf"], dh_g, f"ffn_up_dw_g{l}", blocked=True)], axis=0)
    ln1_g = wt["ln1_g"] if after_ffn is None else wt["ln1_g"] + after_ffn(g, dr2)
    dr1, dr1_bf, g["ln1_g"], g["ln1_b"] = ln_bwd(dr2, sv["xh1"], sv["rs1"], ln1_g, f"ffn_up_dx_ln1_bwd{l}",
                                                 dy_scale=DN_ALPHA, products=[(dh_u, wt["wup"], 0), (dh_g, wt["wup"], 1)])
    dcat, dcat_bf = mm_nn(dr1_bf, wt["wout"], F32, f"out_proj_dx{l}", also_bf16=True, trans_b=0)
    g["wout"] = mm_tn(sv["cat"], dr1_bf, f"out_proj_dw{l}")
    if l == 0:
        dmixed, dpooled, g["pscale"] = pool_bwd_mix(dcat, sv["pooled"], wt["pw_bd"], wt["pscale"], nb, s)
        g["pw_full"] = mm_tn(sv["pooled"], dmixed, "pool_dw", out_dtype=F32)
        dtok = pool_bwd_window(dpooled, nb, s)
    else:
        delta = fox_delta(dcat, sv["o_f32"], nb, s)
        tf = min(TF, s)
        dtok, dk, dv, dfcum_k, dfq_rows = fox_bwd(sv["proj"], sv["kv"], sv["fneg"], dcat_bf,
                                                  sv["lse_rows"], _to_tile_rows(delta, nb, s, tf), nb, s)
    dproj, dmemkv = memattn_bwd(sv["proj"], sv["memkv"], dcat, dtok, nb, s, f"memattn_bwd{l}")
    g["memw"] = mm_tn(mem_bf, dmemkv, f"memkv_dw{l}")
    if l == 0:
        win_a = wt["win_a"] if after_pool is None else wt["win_a"] + after_pool(g, dproj).astype(BF16)
        dx = mm_nn(dproj, win_a, F32, "proj_a_dx", addend=dr1, add_scale=DN_ALPHA, trans_b=0)
        g["win_a"] = mm_tn(sv["xin_bf"], dproj, "proj_a_dw")
    else:
        dkvf, g["fb"] = fgate_bwd(_from_tile_rows(dfq_rows), dfcum_k, sv["fl"], wt["fb"], dk, dv, nb, s)
        dx = (dr1, DN_ALPHA, [(dproj, wt["wq"], 0), (dkvf, wt["kvw"], 0)])
        g["wq"] = mm_tn(sv["xin_bf"], dproj, "proj_b_dw")
        g["kvw"] = mm_tn(sv["xin_bf"], dkvf, "kv_proj_dw")
    return dx, g


def pack_replicated(pool_w, ln1_g, ln1_b, ln2_g, ln2_b, conv_b, f_b):
    cb = jnp.pad(conv_b, ((0, 0), (0, 6144 - 5504))).reshape(12, D_MODEL)
    fb = jnp.pad(f_b.reshape(1, FOX_HEADS), ((0, 3), (0, D_MODEL - FOX_HEADS)))
    return jnp.concatenate([pool_w.reshape(144, D_MODEL), ln1_g, ln1_b, ln2_g, ln2_b, cb, fb], axis=0)


def unpack_replicated(buf):
    pool_w = buf[:144].reshape(1, 4, POOL_GROUP, POOL_GROUP)
    ln = [buf[144 + 2 * k:146 + 2 * k] for k in range(4)]
    conv_b = buf[152:164].reshape(2, 6144)[:, :5504]
    f_b = buf[164, :FOX_HEADS]
    return pool_w, ln[0], ln[1], ln[2], ln[3], conv_b, f_b


def _pad_ff(a, axis):
    zeros = jnp.zeros(a.shape[:axis] + (FF_ROWS_PAD - FF_ROWS,) + a.shape[axis + 1:], a.dtype)
    halves = [lax.slice_in_dim(a, h * FF_ROWS, (h + 1) * FF_ROWS, axis=axis) for h in range(2)]
    return jnp.concatenate([halves[0], zeros, halves[1], zeros], axis=axis)


def _unpad_ff(a, axis):
    return jnp.concatenate([lax.slice_in_dim(a, h * FF_ROWS_PAD, h * FF_ROWS_PAD + FF_ROWS, axis=axis) for h in range(2)],
                           axis=axis)


def pack_small(conv_w, pool_scale):
    buf = jnp.zeros((16, FF_BLOCK_PAD), F32)
    buf = lax.dynamic_update_slice(buf, _pad_ff(conv_w.reshape(DEPTH * 3, FF_BLOCK), 1), (0, 0))
    return lax.dynamic_update_slice(buf, pool_scale, (8, 0))


def _block_diag(pw):
    out = jnp.zeros((TOK_WIDTH, TOK_WIDTH), pw.dtype)
    for g in range(4):
        out = lax.dynamic_update_slice(out, pw[g], (g * POOL_GROUP, g * POOL_GROUP))
    return out


def layer_shards(l, sq_a, sq_b, mem_w_kv, ffn_w_up, ffn_w_down):
    wdown = jnp.pad(ffn_w_down[l], ((0, FF_ROWS_PAD - FF_ROWS), (0, 0)))
    return [sq_a[0].astype(BF16), sq_b[0].astype(BF16), mem_w_kv[l].astype(BF16), _pad_ff(ffn_w_up[l], 1).astype(BF16),
            wdown.astype(BF16)]


def mixer_weights(l, gath, ln1_g, ln1_b, ln2_g, ln2_b):
    w_out = gath[1].reshape(D_MODEL, D_MODEL)
    wt = {"memw": gath[2].reshape(D_MODEL, 2 * MEM_WIDTH), "wout": w_out,
          "ln1_g": ln1_g[l:l + 1], "ln1_b": ln1_b[l:l + 1], "ln2_g": ln2_g[l:l + 1], "ln2_b": ln2_b[l:l + 1]}
    return wt, gath[0].reshape(D_MODEL, D_MODEL)


def ffn_weights(l, wup_g, wdown_g, small, conv_b):
    cb = _pad_ff(conv_b[l].reshape(N_DEV, FF_BLOCK), 1)
    cw = jnp.concatenate([small[:, 3 * l:3 * l + 3, :], cb[:, None, :], jnp.zeros((N_DEV, 4, FF_BLOCK_PAD), F32)], axis=1)
    return {"wup": wup_g, "wdown": wdown_g.reshape(FF_PAIRS * FF_BLOCK_PAD, D_MODEL), "cw": cw}


def mixer_grad_blocks(g, w_in_grad):
    blocks = [] if w_in_grad is None else [w_in_grad.reshape(N_DEV, 128, D_MODEL)]
    blocks += [g["wout"].reshape(N_DEV, 128, D_MODEL), g["memw"].reshape(N_DEV, 128, 2 * MEM_WIDTH)]
    return [b.astype(BF16) for b in blocks]


def ffn_grad_blocks(g):
    wdown = g["wdown"].reshape(N_DEV, FF_ROWS_PAD, D_MODEL)[:, :FF_ROWS]
    return [_unpad_ff(g["wup"], 2).astype(BF16), wdown.astype(BF16)]


def small_grad_blocks(g0, g1):
    taps = jnp.stack([g0["cw"][:, :3, :], g1["cw"][:, :3, :]], axis=1).reshape(N_DEV, DEPTH * 3, FF_BLOCK_PAD)
    small = jnp.zeros((N_DEV, 16, FF_BLOCK_PAD), F32)
    small = lax.dynamic_update_slice(small, taps, (0, 0, 0))
    return lax.dynamic_update_slice(small, g0["pscale"].reshape(N_DEV, 1, 96), (0, 8, 0))


def replicated_grads(g0, g1):
    pw = jnp.stack([g0["pw_full"][k * POOL_GROUP:(k + 1) * POOL_GROUP, k * POOL_GROUP:(k + 1) * POOL_GROUP] for k in range(4)])
    conv_b = jnp.stack([_unpad_ff(g_["cw"][:, 3, :], 1).reshape(N_DEV * FF_BLOCK) for g_ in (g0, g1)])
    ln = [jnp.concatenate([g0[n], g1[n]], axis=0) for n in ("ln1_g", "ln1_b", "ln2_g", "ln2_b")]
    return pack_replicated(pw[None], ln[0], ln[1], ln[2], ln[3], conv_b, g1["fb"][0, :FOX_HEADS])


def kernel(x, mem, a_w_in, a_pool_w, a_pool_scale, a_w_out, b_w_q, b_w_out, kv_w, f_b, mem_w_kv, ln1_g, ln1_b, ln2_g, ln2_b, ffn_w_up, ffn_conv_w, ffn_conv_b, ffn_w_down, loss_target, m_a_w_in, m_a_pool_w, m_a_pool_scale, m_a_w_out, m_b_w_q, m_b_w_out, m_kv_w, m_f_b, m_mem_w_kv, m_ln1_g, m_ln1_b, m_ln2_g, m_ln2_b, m_ffn_w_up, m_ffn_conv_w, m_ffn_conv_b, m_ffn_w_down, v_a_w_in, v_a_pool_w, v_a_pool_scale, v_a_w_out, v_b_w_q, v_b_w_out, v_kv_w, v_f_b, v_mem_w_kv, v_ln1_g, v_ln1_b, v_ln2_g, v_ln2_b, v_ffn_w_up, v_ffn_conv_w, v_ffn_conv_b, v_ffn_w_down):
    nb, s, d = x.shape
    t = nb * s
    x2d, mem_bf, target = x.reshape(t, d), mem.reshape(nb * MEM_LEN, d).astype(BF16), loss_target.reshape(t, d)

    shards0 = layer_shards(0, a_w_in, a_w_out, mem_w_kv, ffn_w_up, ffn_w_down)
    shards1 = layer_shards(1, b_w_q, b_w_out, mem_w_kv, ffn_w_up, ffn_w_down)
    shards1.append(jnp.pad(kv_w, ((0, 0), (0, KV_COLS_PAD - KV_COLS))).astype(BF16))
    gath0 = exchange(shards0[:3] + [pack_small(ffn_conv_w, a_pool_scale)], [False] * 4, "gather_w0_mixer")
    pending = {"ffn0": exchange_start(shards0[3:], [False] * 2, gath0[0], "gather_w0_ffn_start")}
    small = gath0[3]
    wt0, w_in = mixer_weights(0, gath0, ln1_g + pending["ffn0"]["token"], ln1_b, ln2_g, ln2_b)
    pw_bd = _block_diag(a_pool_w[0])
    wt0.update(win_a=w_in, pw_bd=pw_bd.astype(BF16),
               pscale=small[:, 8, :96].reshape(1, TOK_WIDTH) + pending["ffn0"]["token"])

    def ffn0_weights(x1_bf):
        got = exchange_wait(pending["ffn0"], x1_bf, "gather_w0_ffn_wait")
        pending["w1"] = exchange_start(shards1, [False] * 6, got[0], "gather_w1_start")
        w = ffn_weights(0, got[0], got[1], small, ffn_conv_b)
        w["cw"] = w["cw"] + pending["w1"]["token"]
        return w

    x1, x1_bf, sv0 = forward_layer(0, x2d, x2d, mem_bf, wt0, nb, s, ffn_weights=ffn0_weights)
    gath1 = exchange_wait(pending["w1"], x1_bf, "gather_w1_wait")
    wt1, w_q = mixer_weights(1, gath1, ln1_g, ln1_b, ln2_g, ln2_b)
    wt1.update(ffn_weights(1, gath1[3], gath1[4], small, ffn_conv_b))
    kvw = gath1[5].reshape(D_MODEL, KV_COLS_PAD)
    wt1.update(wq=w_q, kvw=kvw,
               fb=jnp.pad(f_b.reshape(1, FOX_HEADS), ((0, 0), (0, LANES - FOX_HEADS))))
    _, _, sv1 = forward_layer(1, x1, x1_bf, mem_bf, wt1, nb, s)

    dx1, g1 = backward_layer(1, None, sv1, mem_bf, wt1, nb, s, loss_target=target)
    loss = lax.psum(g1["loss_row"][0, 0], ("x", "y", "c"))
    blocks1 = (mixer_grad_blocks(g1, g1["wq"]) + ffn_grad_blocks(g1)
               + [g1["kvw"][:, :KV_COLS].reshape(N_DEV, 128, KV_COLS).astype(BF16)])
    pending["g1"] = exchange_start(blocks1, [True] * 6, dx1[0], "scatter_g1_start")
    wt0["ln2_g"] = wt0["ln2_g"] + pending["g1"]["token"]

    def after_ffn0(g, dxm):
        pending["gf0"] = exchange_start(ffn_grad_blocks(g), [True] * 2, dxm, "scatter_g0_ffn_start")
        return pending["gf0"]["token"]

    def after_pool0(g, x):
        blocks = mixer_grad_blocks(g, None) + [small_grad_blocks(g, g1), replicated_grads(g, g1)]
        pending["gm0"] = exchange_start(blocks, [True] * 3 + [False], x, "scatter_g0_mixer_start")
        return pending["gm0"]["token"]

    grad_x, g0 = backward_layer(0, dx1, sv0, mem_bf, wt0, nb, s, after_ffn=after_ffn0, after_pool=after_pool0)
    pending["gin"] = exchange_start([g0["win_a"].reshape(N_DEV, 128, D_MODEL).astype(BF16)], [True], grad_x,
                                    "scatter_g0_in_start")
    parts_f0 = exchange_wait(pending["gf0"], jnp.zeros((8, LANES), F32) + pending["gin"]["token"], "scatter_g0_ffn_wait")
    parts1 = exchange_wait(pending["g1"], parts_f0[0], "scatter_g1_wait")

    res = {}

    def upd(nm, parts, w2, m2, v2):
        res[nm] = reduce_adamw(parts, w2, m2, v2, f"adamw_{nm}")

    upd("b_w_q", [parts1[0]], b_w_q, m_b_w_q, v_b_w_q)
    upd("b_w_out", [parts1[1]], b_w_out, m_b_w_out, v_b_w_out)
    upd("kv_w", [parts1[5]], kv_w[None], m_kv_w[None], v_kv_w[None])
    upd("ffn_w_up", [parts_f0[0], parts1[3]], ffn_w_up, m_ffn_w_up, v_ffn_w_up)
    upd("ffn_w_down", [parts_f0[1], parts1[4]], ffn_w_down, m_ffn_w_down, v_ffn_w_down)
    parts_m0 = exchange_wait(pending["gm0"], res["ffn_w_down"][0], "scatter_g0_mixer_wait")
    parts_in = exchange_wait(pending["gin"], parts_m0[0], "scatter_g0_in_wait")
    upd("a_w_in", [parts_in[0]], a_w_in, m_a_w_in, v_a_w_in)
    upd("a_w_out", [parts_m0[0]], a_w_out, m_a_w_out, v_a_w_out)
    upd("mem_w_kv", [parts_m0[1], parts1[2]], mem_w_kv, m_mem_w_kv, v_mem_w_kv)
    upd("small", [parts_m0[2]], pack_small(ffn_conv_w, a_pool_scale)[None], pack_small(m_ffn_conv_w, m_a_pool_scale)[None],
        pack_small(v_ffn_conv_w, v_a_pool_scale)[None])
    upd("replicated", [parts_m0[3]], pack_replicated(a_pool_w, ln1_g, ln1_b, ln2_g, ln2_b, ffn_conv_b, f_b)[None],
        pack_replicated(m_a_pool_w, m_ln1_g, m_ln1_b, m_ln2_g, m_ln2_b, m_ffn_conv_b, m_f_b)[None],
        pack_replicated(v_a_pool_w, v_ln1_g, v_ln1_b, v_ln2_g, v_ln2_b, v_ffn_conv_b, v_f_b)[None])

    res["kv_w"] = [o[0] for o in res["kv_w"]]
    res["ffn_conv_w"] = [_unpad_ff(o[0, :DEPTH * 3, :], 1).reshape(DEPTH, 3, FF_BLOCK) for o in res["small"]]
    res["a_pool_scale"] = [o[0, 8:9, :96] for o in res["small"]]
    rep_names = ["a_pool_w", "ln1_g", "ln1_b", "ln2_g", "ln2_b", "ffn_conv_b", "f_b"]
    for nm in rep_names:
        res[nm] = []
    for o in res["replicated"]:
        for nm, val in zip(rep_names, unpack_replicated(o[0])):
            res[nm].append(val)

    order = ["a_w_in", "a_pool_w", "a_pool_scale", "a_w_out", "b_w_q", "b_w_out", "kv_w", "f_b", "mem_w_kv",
             "ln1_g", "ln1_b", "ln2_g", "ln2_b", "ffn_w_up", "ffn_conv_w", "ffn_conv_b", "ffn_w_down"]
    out = [loss, grad_x.reshape(nb, s, d)]
    for kind in range(4):
        out.extend(res[nm][kind] for nm in order)
    return tuple(out)
```

```python
import jax
import jax.numpy as jnp
from jax import lax
from jax.experimental import pallas as pl
from jax.experimental.pallas import tpu as pltpu

F32 = jnp.float32
BF16 = jnp.bfloat16
SDS = jax.ShapeDtypeStruct

N_DEV = 8
D_MODEL = 1024
TOK_WIDTH = 768
MEM_WIDTH = 256
MEM_LEN = 256
MEM_HEADS = 4
HEAD_DIM = 64
FOX_HEADS = 12
POOL_GROUP = 192
FF_BLOCK = 688
FF_BLOCK_PAD = 768
FF_PAIRS = 4
FF_ROWS = 344
FF_ROWS_PAD = FF_BLOCK_PAD // 2
KV_COLS = 1548
KV_COLS_PAD = 1664
LANES = 128
DEPTH = 2
DN_ALPHA = (2.0 * DEPTH) ** 0.25
LN_EPS = 1e-5
QK_SCALE = HEAD_DIM ** -0.5
NEG_BIG = -1e30

ADAM_LR = 0.001
ADAM_B1 = 0.9
ADAM_B2 = 0.999
ADAM_EPS = 1e-08
ADAM_WD = 0.01
ADAM_STEP = 10

VMEM_LIMIT_BYTES = 56 * 1024 * 1024
MM_BLOCK_BYTES = 6 * 1024 * 1024
TM = 512
TS = 256
TF = 256
TC = 256
HALO_POOL = 16
HALO_CONV = 8

NT_DIMS = (((1,), (1,)), ((), ()))
TN_DIMS = (((0,), (0,)), ((), ()))


def _params(sem=None):
    return pltpu.CompilerParams(dimension_semantics=sem, vmem_limit_bytes=VMEM_LIMIT_BYTES)


def _sigmoid(z):
    return 1.0 / (1.0 + jnp.exp(-z))


def _pick_tn(n):
    if n <= 2048:
        return n
    for t in (1024, 768, 512, 256, 128):
        if n % t == 0:
            return t
    return n


def mm_nn(a, b, out_dtype, name, addend=None, add_scale=1.0, also_bf16=False, trans_b=None):
    m, k = a.shape
    n = b.shape[1] if trans_b is None else b.shape[0]
    tm = min(TM, m)
    tn = n
    while k * tn * 2 > MM_BLOCK_BYTES or tm * tn * 4 > MM_BLOCK_BYTES:
        tn //= 2
    chunk = tn if tn <= 2048 else _pick_tn(tn)
    has_add = addend is not None

    def body(*refs):
        a_ref, b_ref = refs[0], refs[1]
        c_ref = refs[2] if has_add else None
        o_ref = refs[3] if has_add else refs[2]
        ob_ref = refs[-1] if also_bf16 else None
        av = a_ref[...].astype(BF16)
        for c in range(tn // chunk):
            cols = slice(c * chunk, (c + 1) * chunk)
            if trans_b is None:
                r = jnp.dot(av, b_ref[:, cols].astype(BF16), preferred_element_type=F32)
            else:
                r = lax.dot_general(av, b_ref[cols, :].astype(BF16), NT_DIMS, preferred_element_type=F32)
            if has_add:
                r = r + add_scale * c_ref[:, cols]
            o_ref[:, cols] = r.astype(out_dtype)
            if also_bf16:
                ob_ref[:, cols] = r.astype(BF16)

    b_spec = (pl.BlockSpec((k, tn), lambda j, i: (0, j)) if trans_b is None
              else pl.BlockSpec((tn, k), lambda j, i: (j, trans_b)))
    in_specs = [pl.BlockSpec((tm, k), lambda j, i: (i, 0)), b_spec]
    ops = [a, b]
    tile = pl.BlockSpec((tm, tn), lambda j, i: (i, j))
    if has_add:
        in_specs.append(tile)
        ops.append(addend)
    out_shape = [SDS((m, n), out_dtype)]
    out_specs = [tile]
    if also_bf16:
        out_shape.append(SDS((m, n), BF16))
        out_specs.append(tile)
    res = pl.pallas_call(
        body, name=name, grid=(n // tn, m // tm), in_specs=in_specs, out_specs=out_specs, out_shape=out_shape,
        compiler_params=_params(("parallel", "parallel")))(*ops)
    return tuple(res) if also_bf16 else res[0]


def mm_tn(a, b, name, out_dtype=BF16):
    t, m = a.shape
    _, n = b.shape
    tt = min(4 * TM, t)
    tm = 1024 if m % 1024 == 0 else m
    tn = _pick_tn(n)
    nt = t // tt
    block = (tm, tn)
    in_place = out_dtype == F32

    def body(a_ref, b_ref, o_ref, *scratch):
        acc_ref = o_ref if in_place else scratch[0]
        kk = pl.program_id(2)
        r = lax.dot_general(a_ref[...].astype(BF16), b_ref[...].astype(BF16), TN_DIMS, preferred_element_type=F32)

        @pl.when(kk == 0)
        def _():
            acc_ref[...] = r

        @pl.when(kk != 0)
        def _():
            acc_ref[...] += r

        if not in_place:
            @pl.when(kk == nt - 1)
            def _():
                o_ref[...] = acc_ref[...].astype(out_dtype)

    return pl.pallas_call(
        body, name=name, grid=(m // tm, n // tn, nt),
        in_specs=[pl.BlockSpec((tt, tm), lambda i, j, kk: (kk, i)), pl.BlockSpec((tt, tn), lambda i, j, kk: (kk, j))],
        out_specs=pl.BlockSpec(block, lambda i, j, kk: (i, j)), out_shape=SDS((m, n), out_dtype),
        scratch_shapes=[] if in_place else [pltpu.VMEM(block, F32)],
        compiler_params=_params(("parallel", "parallel", "arbitrary")))(a, b)


def ln_fwd(xprev, a, w, g, b, name):
    t, d = xprev.shape
    k = a.shape[1]
    tm = min(TM, t)

    def body(xp_ref, a_ref, w_ref, g_ref, b_ref, y_ref, yb_ref, xh_ref, rs_ref):
        r = DN_ALPHA * xp_ref[...] + jnp.dot(a_ref[...], w_ref[...], preferred_element_type=F32)
        mu = jnp.mean(r, axis=1, keepdims=True)
        xc = r - mu
        var = jnp.mean(xc * xc, axis=1, keepdims=True)
        rstd = lax.rsqrt(var + LN_EPS)
        xh = xc * rstd
        y = xh * g_ref[...] + b_ref[...]
        y_ref[...] = y
        yb_ref[...] = y.astype(BF16)
        xh_ref[...] = xh
        rs_ref[...] = jnp.broadcast_to(rstd, (tm, LANES))

    row = pl.BlockSpec((tm, d), lambda i: (i, 0))
    vec = pl.BlockSpec((1, d), lambda i: (0, 0))
    return pl.pallas_call(
        body, name=name, grid=(t // tm,),
        in_specs=[row, pl.BlockSpec((tm, k), lambda i: (i, 0)), pl.BlockSpec((k, d), lambda i: (0, 0)), vec, vec],
        out_specs=[row, row, row, pl.BlockSpec((tm, LANES), lambda i: (i, 0))],
        out_shape=[SDS((t, d), F32), SDS((t, d), BF16), SDS((t, d), F32), SDS((t, LANES), F32)],
        compiler_params=_params(("parallel",)))(xprev, a, w, g, b)


def ln_bwd(dy, xhat, rstd, g, name, products=(), dy_scale=1.0):
    t, d = dy.shape
    np_ = len(products)
    tm = min(TM if sum(a.shape[1] for a, _, _ in products) <= 4096 else TS, t)

    def body(*refs):
        prod_refs = refs[:2 * np_]
        dy_ref, xh_ref, rs_ref, g_ref, dr_ref, drb_ref, dg_ref, db_ref = refs[2 * np_:]
        i = pl.program_id(0)
        dyv = dy_ref[...] if dy_scale == 1.0 else dy_scale * dy_ref[...]
        for p in range(np_):
            a_ref, w_ref = prod_refs[2 * p], prod_refs[2 * p + 1]
            if len(w_ref.shape) == 2:
                dyv = dyv + lax.dot_general(a_ref[...], w_ref[...], NT_DIMS, preferred_element_type=F32)
            else:
                kb = w_ref.shape[2]
                for c in range(w_ref.shape[0]):
                    dyv = dyv + lax.dot_general(a_ref[:, c * kb:(c + 1) * kb], w_ref[c], NT_DIMS,
                                                preferred_element_type=F32)
        xh = xh_ref[...]
        dxh = dyv * g_ref[...]
        m1 = jnp.mean(dxh, axis=1, keepdims=True)
        m2 = jnp.mean(dxh * xh, axis=1, keepdims=True)
        dr = rs_ref[:, 0:1] * (dxh - m1 - xh * m2)
        dr_ref[...] = dr
        drb_ref[...] = dr.astype(BF16)

        @pl.when(i == 0)
        def _():
            dg_ref[...] = jnp.zeros_like(dg_ref)
            db_ref[...] = jnp.zeros_like(db_ref)

        dg_ref[...] += jnp.sum(dyv * xh, axis=0, keepdims=True)
        db_ref[...] += jnp.sum(dyv, axis=0, keepdims=True)

    row = pl.BlockSpec((tm, d), lambda i: (i, 0))
    vec = pl.BlockSpec((1, d), lambda i: (0, 0))
    in_specs = [row, row, pl.BlockSpec((tm, LANES), lambda i: (i, 0)), vec]
    ops = [dy, xhat, rstd, g]
    for a, w, col in reversed(products):
        k = a.shape[1]
        if w.ndim == 2:
            w_spec = pl.BlockSpec((d, k), lambda i, col=col: (0, col))
        else:
            w_spec = pl.BlockSpec((k // w.shape[2], d, w.shape[2]), lambda i, col=col: (col, 0, 0))
        in_specs = [pl.BlockSpec((tm, k), lambda i: (i, 0)), w_spec] + in_specs
        ops = [a, w] + ops
    return pl.pallas_call(
        body, name=name, grid=(t // tm,), in_specs=in_specs, out_specs=[row, row, vec, vec],
        out_shape=[SDS((t, d), F32), SDS((t, d), BF16), SDS((1, d), F32), SDS((1, d), F32)],
        compiler_params=_params(("arbitrary",)))(*ops)


def loss_ln_bwd(xhat, rstd, g, beta, target, name):
    t, d = xhat.shape
    tm = min(TM, t)
    nsteps = t // tm

    def body(xh_ref, rs_ref, g_ref, b_ref, t_ref, dr_ref, drb_ref, dg_ref, db_ref, l_ref, acc):
        i = pl.program_id(0)
        xh = xh_ref[...]
        diff = xh * g_ref[...] + b_ref[...] - t_ref[...]
        dyv = diff * (1.0 / d)
        dxh = dyv * g_ref[...]
        m1 = jnp.mean(dxh, axis=1, keepdims=True)
        m2 = jnp.mean(dxh * xh, axis=1, keepdims=True)
        dr = rs_ref[:, 0:1] * (dxh - m1 - xh * m2)
        dr_ref[...] = dr
        drb_ref[...] = dr.astype(BF16)

        @pl.when(i == 0)
        def _():
            dg_ref[...] = jnp.zeros_like(dg_ref)
            db_ref[...] = jnp.zeros_like(db_ref)
            acc[...] = jnp.zeros_like(acc)

        dg_ref[...] += jnp.sum(dyv * xh, axis=0, keepdims=True)
        db_ref[...] += jnp.sum(dyv, axis=0, keepdims=True)
        acc[...] += jnp.sum(diff * diff, axis=0, keepdims=True)

        @pl.when(i == nsteps - 1)
        def _():
            tot = jnp.sum(acc[...], axis=1, keepdims=True) * (0.5 / d)
            l_ref[...] = jnp.broadcast_to(tot, (1, LANES))

    row = pl.BlockSpec((tm, d), lambda i: (i, 0))
    vec = pl.BlockSpec((1, d), lambda i: (0, 0))
    return pl.pallas_call(
        body, name=name, grid=(nsteps,),
        in_specs=[row, pl.BlockSpec((tm, LANES), lambda i: (i, 0)), vec, vec, row],
        out_specs=[row, row, vec, vec, pl.BlockSpec((1, LANES), lambda i: (0, 0))],
        out_shape=[SDS((t, d), F32), SDS((t, d), BF16), SDS((1, d), F32), SDS((1, d), F32), SDS((1, LANES), F32)],
        scratch_shapes=[pltpu.VMEM((1, d), F32)],
        compiler_params=_params(("arbitrary",)))(xhat, rstd, g, beta, target)


def memattn_fwd(proj, memkv, tok, nb, s, name):
    ts = min(TM, s)
    nq = s // ts

    def body(q_ref, kv_ref, tok_ref, o_ref):
        o_ref[:, :TOK_WIDTH] = tok_ref[...]
        top = lax.broadcasted_iota(jnp.int32, (PAIR, ts), 0) < HEAD_DIM
        scores = []
        for p in range(MEM_HEADS // 2):
            qp = q_ref[:, p * PAIR:(p + 1) * PAIR].astype(BF16)
            ke, ko = _split_pair(kv_ref[:, p * PAIR:(p + 1) * PAIR], QK_SCALE)
            scores.append([lax.dot_general(km, qp, NT_DIMS, preferred_element_type=F32) for km in (ke, ko)])
        for p in range(MEM_HEADS // 2):
            vt = kv_ref[:, MEM_WIDTH + p * PAIR:MEM_WIDTH + (p + 1) * PAIR].astype(F32).T.astype(BF16)
            outs = []
            for sc in scores[p]:
                e = jnp.exp(sc - jnp.max(sc, axis=0, keepdims=True))
                pr = e / jnp.sum(e, axis=0, keepdims=True)
                outs.append(jnp.dot(vt, pr.astype(BF16), preferred_element_type=F32))
            o_ref[:, TOK_WIDTH + p * PAIR:TOK_WIDTH + (p + 1) * PAIR] = jnp.where(top, outs[0], outs[1]).T.astype(BF16)

    return pl.pallas_call(
        body, name=name, grid=(nb, nq),
        in_specs=[pl.BlockSpec((ts, MEM_WIDTH), lambda b, i: (b * nq + i, 3)),
                  pl.BlockSpec((MEM_LEN, 2 * MEM_WIDTH), lambda b, i: (b, 0)),
                  pl.BlockSpec((ts, TOK_WIDTH), lambda b, i: (b * nq + i, 0))],
        out_specs=pl.BlockSpec((ts, TOK_WIDTH + MEM_WIDTH), lambda b, i: (b * nq + i, 0)),
        out_shape=SDS((nb * s, TOK_WIDTH + MEM_WIDTH), BF16),
        compiler_params=_params(("parallel", "parallel")))(proj, memkv, tok)


def memattn_bwd(proj, memkv, dcat, dtok, nb, s, name):
    ts = min(TM, s)
    nq = s // ts

    def body(q_ref, kv_ref, do_ref, dtok_ref, dq_ref, dkv_ref):
        i = pl.program_id(1)
        dq_ref[:, :TOK_WIDTH] = dtok_ref[...]

        @pl.when(i == 0)
        def _():
            dkv_ref[...] = jnp.zeros_like(dkv_ref)

        lo = _half_masks(MEM_LEN)
        top = lax.broadcasted_iota(jnp.int32, (PAIR, ts), 0) < HEAD_DIM
        n_pairs = MEM_HEADS // 2
        qs, dos, kps, products = [], [], [], []
        for p in range(n_pairs):
            qp = q_ref[:, p * PAIR:(p + 1) * PAIR].astype(BF16)
            dop = do_ref[:, p * PAIR:(p + 1) * PAIR].astype(BF16)
            kp = kv_ref[:, p * PAIR:(p + 1) * PAIR] * QK_SCALE
            kms = _split_pair(kp)
            vms = _split_pair(kv_ref[:, MEM_WIDTH + p * PAIR:MEM_WIDTH + (p + 1) * PAIR])
            products.append([(lax.dot_general(km, qp, NT_DIMS, preferred_element_type=F32),
                              lax.dot_general(vm, dop, NT_DIMS, preferred_element_type=F32)) for km, vm in zip(kms, vms)])
            qs.append(qp)
            dos.append(dop)
            kps.append(kp)
        for p in range(n_pairs):
            kt = kps[p].astype(F32).T.astype(BF16)
            dks, dvs, dqs = [], [], []
            for sc, dp in products[p]:
                e = jnp.exp(sc - jnp.max(sc, axis=0, keepdims=True))
                pr = e / jnp.sum(e, axis=0, keepdims=True)
                dl = jnp.sum(pr * dp, axis=0, keepdims=True)
                ds = (pr * (dp - dl)).astype(BF16)
                dvs.append(jnp.dot(pr.astype(BF16), dos[p], preferred_element_type=F32))
                dks.append(jnp.dot(ds, qs[p], preferred_element_type=F32))
                dqs.append(jnp.dot(kt, ds, preferred_element_type=F32))
            dq_ref[:, TOK_WIDTH + p * PAIR:TOK_WIDTH + (p + 1) * PAIR] = jnp.where(top, dqs[0], dqs[1]).T.astype(BF16)
            dkv_ref[:, p * PAIR:(p + 1) * PAIR] += jnp.where(lo, dks[0], dks[1]) * QK_SCALE
            dkv_ref[:, MEM_WIDTH + p * PAIR:MEM_WIDTH + (p + 1) * PAIR] += jnp.where(lo, dvs[0], dvs[1])

    return pl.pallas_call(
        body, name=name, grid=(nb, nq),
        in_specs=[pl.BlockSpec((ts, MEM_WIDTH), lambda b, i: (b * nq + i, 3)),
                  pl.BlockSpec((MEM_LEN, 2 * MEM_WIDTH), lambda b, i: (b, 0)),
                  pl.BlockSpec((ts, MEM_WIDTH), lambda b, i: (b * nq + i, 3)),
                  pl.BlockSpec((ts, TOK_WIDTH), lambda b, i: (b * nq + i, 0))],
        out_specs=[pl.BlockSpec((ts, TOK_WIDTH + MEM_WIDTH), lambda b, i: (b * nq + i, 0)),
                   pl.BlockSpec((MEM_LEN, 2 * MEM_WIDTH), lambda b, i: (b, 0))],
        out_shape=[SDS((nb * s, TOK_WIDTH + MEM_WIDTH), BF16), SDS((nb * MEM_LEN, 2 * MEM_WIDTH), F32)],
        compiler_params=_params(("parallel", "arbitrary")))(proj, memkv, dcat, dtok)


def _pool_select(shape, s2, s4, s8, s16):
    lane = lax.broadcasted_iota(jnp.int32, shape, 1)
    return jnp.where(lane < POOL_GROUP, s2, jnp.where(lane < 2 * POOL_GROUP, s4, jnp.where(lane < 3 * POOL_GROUP, s8, s16)))


def _pool_count(shape, first_pos):
    pos = first_pos + lax.broadcasted_iota(jnp.int32, shape, 0)
    win = _pool_select(shape, 2, 4, 8, 16)
    return jnp.minimum(pos + 1, win).astype(F32)


def pool_fwd(proj, pw_bd, pscale, nb, s):
    ts = min(TS, s)
    nq = s // ts
    w = TOK_WIDTH

    def body(c_ref, h_ref, w_ref, sc_ref, pooled_ref, tok_ref):
        i = pl.program_id(0) % nq
        cur = c_ref[...]
        halo = jnp.where(i == 0, 0.0, h_ref[...])
        xe = jnp.concatenate([halo, cur], axis=0)
        s2 = xe + pltpu.roll(xe, 1, axis=0)
        s4 = s2 + pltpu.roll(s2, 2, axis=0)
        s8 = s4 + pltpu.roll(s4, 4, axis=0)
        s16 = s8 + pltpu.roll(s8, 8, axis=0)
        hp = HALO_POOL
        ws = _pool_select((ts, w), s2[hp:], s4[hp:], s8[hp:], s16[hp:])
        pooled = (ws / _pool_count((ts, w), i * ts) - cur).astype(BF16)
        pooled_ref[...] = pooled
        mixed = jnp.dot(pooled, w_ref[...], preferred_element_type=F32)
        tok_ref[...] = (mixed * sc_ref[...]).astype(BF16)

    row = pl.BlockSpec((ts, w), lambda r: (r, 0))
    return pl.pallas_call(
        body, name="pool_fwd", grid=(nb * nq,),
        in_specs=[row, pl.BlockSpec((HALO_POOL, w), lambda r: (jnp.maximum(r * (ts // HALO_POOL) - 1, 0), 0)),
                  pl.BlockSpec((w, w), lambda r: (0, 0)), pl.BlockSpec((1, w), lambda r: (0, 0))],
        out_specs=[row, row], out_shape=[SDS((nb * s, w), BF16), SDS((nb * s, w), BF16)],
        compiler_params=_params(("parallel",)))(proj, proj, pw_bd, pscale)


def pool_bwd_mix(dcat, pooled, pw_bd, pscale, nb, s):
    ts = min(TS, s)
    w = TOK_WIDTH

    def body(dt_ref, p_ref, w_ref, sc_ref, dm_ref, dp_ref, ds_ref):
        r = pl.program_id(0)
        dtok = dt_ref[...]
        mixed = jnp.dot(p_ref[...], w_ref[...], preferred_element_type=F32)

        @pl.when(r == 0)
        def _():
            ds_ref[...] = jnp.zeros_like(ds_ref)

        ds_ref[...] += jnp.sum(dtok * mixed, axis=0, keepdims=True)
        dmx = (dtok * sc_ref[...]).astype(BF16)
        dm_ref[...] = dmx
        dp_ref[...] = lax.dot_general(dmx, w_ref[...], NT_DIMS, preferred_element_type=F32)

    row = pl.BlockSpec((ts, w), lambda r: (r, 0))
    mat = pl.BlockSpec((w, w), lambda r: (0, 0))
    vec = pl.BlockSpec((1, w), lambda r: (0, 0))
    return pl.pallas_call(
        body, name="pool_bwd_mix", grid=(nb * s // ts,), in_specs=[row, row, mat, vec],
        out_specs=[row, row, vec], out_shape=[SDS((nb * s, w), BF16), SDS((nb * s, w), F32), SDS((1, w), F32)],
        compiler_params=_params(("arbitrary",)))(dcat, pooled, pw_bd, pscale)


def pool_bwd_window(dpooled, nb, s):
    ts = min(TS, s)
    nq = s // ts
    w = TOK_WIDTH
    n_ext = ts + HALO_POOL
    n_halo_blocks = nb * s // HALO_POOL

    def body(c_ref, n_ref, du_ref):
        i = pl.program_id(0) % nq
        cur = c_ref[...]
        nxt = jnp.where(i == nq - 1, 0.0, n_ref[...])
        ze = jnp.concatenate([cur, nxt], axis=0) / _pool_count((n_ext, w), i * ts)
        s2 = ze + pltpu.roll(ze, n_ext - 1, axis=0)
        s4 = s2 + pltpu.roll(s2, n_ext - 2, axis=0)
        s8 = s4 + pltpu.roll(s4, n_ext - 4, axis=0)
        s16 = s8 + pltpu.roll(s8, n_ext - 8, axis=0)
        ws = _pool_select((ts, w), s2[:ts], s4[:ts], s8[:ts], s16[:ts])
        du_ref[...] = (ws - cur).astype(BF16)

    row = pl.BlockSpec((ts, w), lambda r: (r, 0))
    return pl.pallas_call(
        body, name="pool_bwd_window", grid=(nb * nq,),
        in_specs=[row, pl.BlockSpec((HALO_POOL, w),
                                    lambda r: (jnp.minimum((r + 1) * (ts // HALO_POOL), n_halo_blocks - 1), 0))],
        out_specs=row, out_shape=SDS((nb * s, w), BF16),
        compiler_params=_params(("parallel",)))(dpooled, dpooled)


def _conv_rows(xe, w_ref):
    return (w_ref[0, 2:3, :] * xe + w_ref[0, 1:2, :] * pltpu.roll(xe, 1, axis=0)
            + w_ref[0, 0:1, :] * pltpu.roll(xe, 2, axis=0) + w_ref[0, 3:4, :])


def ffn_up_gate(x_bf, wup, cw, nb, s, name):
    tm = min(2 * TM, s)
    nq = s // tm
    w = FF_BLOCK_PAD
    hr = 2 * HALO_CONV
    k = x_bf.shape[1]

    def body(xc_ref, xh_ref, wu_ref, wg_ref, cu_ref, cg_ref, act_ref, a_ref, b_ref, hu_ref, hg_ref):
        first = (pl.program_id(1) % nq) == 0
        xc = xc_ref[...]
        xh = xh_ref[...]

        def products(w_ref):
            return (jnp.dot(xc, w_ref[0], preferred_element_type=F32), jnp.dot(xh, w_ref[0], preferred_element_type=F32))

        def conv(hcur, hprev, c_ref, h_out):
            h_out[...] = hcur.astype(BF16)
            xe = jnp.concatenate([jnp.where(first, 0.0, hprev), hcur], axis=0)
            return _conv_rows(xe, c_ref)[hr:]

        pu, pg = products(wu_ref), products(wg_ref)
        cu = conv(*pu, cu_ref, hu_ref)
        cg = conv(*pg, cg_ref, hg_ref)
        sg = _sigmoid(cg)
        a = cg * sg
        act_ref[...] = (a * cu).astype(BF16)
        a_ref[...] = a.astype(BF16)
        b_ref[...] = (cu * (sg * (1.0 + cg * (1.0 - sg)))).astype(BF16)

    def wblock(off):
        return pl.BlockSpec((1, k, w), lambda j, r: (j + off, 0, 0))

    def cblock(off):
        return pl.BlockSpec((1, 8, w), lambda j, r: (j + off, 0, 0))

    tile = pl.BlockSpec((tm, w), lambda j, r: (r, j))
    out = SDS((nb * s, FF_PAIRS * w), BF16)
    return pl.pallas_call(
        body, name=name, grid=(FF_PAIRS, nb * nq),
        in_specs=[pl.BlockSpec((tm, k), lambda j, r: (r, 0)),
                  pl.BlockSpec((hr, k), lambda j, r: (jnp.maximum(r * (tm // hr) - 1, 0), 0)),
                  wblock(0), wblock(FF_PAIRS), cblock(0), cblock(FF_PAIRS)],
        out_specs=[tile] * 5, out_shape=[out] * 5,
        compiler_params=_params(("parallel", "parallel")))(x_bf, x_bf, wup, wup, cw, cw)


def gate_conv_bwd(dact, a, b, hu, hg, cw, nb, s, name):
    ts = min(TS, s)
    nq = s // ts
    w = FF_BLOCK_PAD
    hc = HALO_CONV
    hb = 2 * hc
    n_ext = ts + hc

    def body(dc_ref, dn_ref, ac_ref, an_ref, bc_ref, bn_ref, hu_ref, hg_ref, wu_ref, wg_ref,
             dhu_ref, dhg_ref, dwu_ref, dwg_ref):
        r = pl.program_id(1)
        last = (r % nq) == nq - 1

        def ext(c_ref, n_ref, mask_next=False):
            nxt = n_ref[...].astype(F32)[:hc]
            if mask_next:
                nxt = jnp.where(last, 0.0, nxt)
            return jnp.concatenate([c_ref[...].astype(F32), nxt], axis=0)

        da = ext(dc_ref, dn_ref, mask_next=True)

        def branch(dcv, w_ref, h_ref, dh_ref, dw_ref):
            d0 = dcv[:ts]
            d1 = pltpu.roll(dcv, n_ext - 1, axis=0)[:ts]
            d2 = pltpu.roll(dcv, n_ext - 2, axis=0)[:ts]
            dh_ref[...] = (w_ref[0, 2:3, :] * d0 + w_ref[0, 1:2, :] * d1 + w_ref[0, 0:1, :] * d2).astype(BF16)
            hv = h_ref[...].astype(F32)
            rows = [jnp.sum(d2 * hv, axis=0, keepdims=True), jnp.sum(d1 * hv, axis=0, keepdims=True),
                    jnp.sum(d0 * hv, axis=0, keepdims=True), jnp.sum(d0, axis=0, keepdims=True)]
            sub = lax.broadcasted_iota(jnp.int32, (8, w), 0)
            upd = jnp.zeros((8, w), F32)
            for kk, rv in enumerate(rows):
                upd = jnp.where(sub == kk, rv, upd)

            @pl.when(r == 0)
            def _():
                dw_ref[...] = jnp.zeros_like(dw_ref)

            dw_ref[...] += upd[None]

        branch(da * ext(ac_ref, an_ref), wu_ref, hu_ref, dhu_ref, dwu_ref)
        branch(da * ext(bc_ref, bn_ref), wg_ref, hg_ref, dhg_ref, dwg_ref)

    cur = pl.BlockSpec((ts, w), lambda j, r: (r, j))
    nxt = pl.BlockSpec((hb, w), lambda j, r: (jnp.minimum((r + 1) * (ts // hb), nb * s // hb - 1), j))

    def wspec(off):
        return pl.BlockSpec((1, 8, w), lambda j, r: (j + off, 0, 0))

    p = FF_PAIRS
    dw_spec = pl.BlockSpec((1, 8, w), lambda j, r: (j, 0, 0))
    return pl.pallas_call(
        body, name=name, grid=(p, nb * nq),
        in_specs=[cur, nxt, cur, nxt, cur, nxt, cur, cur, wspec(0), wspec(p)],
        out_specs=[cur, cur, dw_spec, dw_spec],
        out_shape=[SDS((nb * s, p * w), BF16), SDS((nb * s, p * w), BF16), SDS((p, 8, w), F32), SDS((p, 8, w), F32)],
        compiler_params=_params(("parallel", "arbitrary")))(dact, dact, a, a, b, b, hu, hg, cw, cw)


def _tri(n, upper):
    r = lax.broadcasted_iota(jnp.int32, (n, n), 0)
    c = lax.broadcasted_iota(jnp.int32, (n, n), 1)
    return ((r <= c) if upper else (r >= c)).astype(F32)


def fgate_fwd(fl, fb, nb, s):
    tc = min(TC, s)
    nq = s // tc

    def body(fl_ref, fb_ref, f_ref, carry):
        @pl.when(pl.program_id(1) == 0)
        def _():
            carry[...] = jnp.zeros_like(carry)

        z = fl_ref[...] + fb_ref[...]
        logf = jnp.minimum(z, 0.0) - jnp.log(1.0 + jnp.exp(-jnp.abs(z)))
        f_ref[...] = jnp.dot(_tri(tc, False), logf, preferred_element_type=F32,
                             precision=lax.Precision.HIGHEST) + carry[...]
        carry[...] += jnp.sum(logf, axis=0, keepdims=True)

    row = pl.BlockSpec((tc, LANES), lambda b, i: (b * nq + i, 0))
    return pl.pallas_call(
        body, name="fgate_fwd", grid=(nb, nq), in_specs=[row, pl.BlockSpec((1, LANES), lambda b, i: (0, 0))],
        out_specs=row, out_shape=SDS((nb * s, LANES), F32), scratch_shapes=[pltpu.VMEM((1, LANES), F32)],
        compiler_params=_params(("arbitrary", "arbitrary")))(fl, fb)


def fgate_bwd(d_cum_q, d_cum_k, fl, fb, dk, dv, nb, s):
    tc = min(TC, s)
    nq = s // tc

    def body(dfq_ref, dfk_ref, fl_ref, fb_ref, dk_ref, dv_ref, dkvf_ref, dfb_ref, carry):
        b = pl.program_id(0)
        i = pl.program_id(1)

        @pl.when(i == 0)
        def _():
            carry[...] = jnp.zeros_like(carry)

        @pl.when(jnp.logical_and(b == 0, i == 0))
        def _():
            dfb_ref[...] = jnp.zeros_like(dfb_ref)

        dfv = dfq_ref[...] + dfk_ref[...]
        dlog = jnp.dot(_tri(tc, True), dfv, preferred_element_type=F32,
                       precision=lax.Precision.HIGHEST) + carry[...]
        carry[...] += jnp.sum(dfv, axis=0, keepdims=True)
        z = fl_ref[...] + fb_ref[...]
        dfl = dlog / (1.0 + jnp.exp(z))
        dkvf_ref[:, :TOK_WIDTH] = dk_ref[...]
        dkvf_ref[:, TOK_WIDTH:2 * TOK_WIDTH] = dv_ref[...]
        dkvf_ref[:, 2 * TOK_WIDTH:] = dfl.astype(BF16)
        dfb_ref[...] += jnp.sum(dfl, axis=0, keepdims=True)

    def rows(width):
        return pl.BlockSpec((tc, width), lambda b, i: (b * nq + nq - 1 - i, 0))

    row = rows(LANES)
    vec = pl.BlockSpec((1, LANES), lambda b, i: (0, 0))
    return pl.pallas_call(
        body, name="fgate_bwd", grid=(nb, nq), in_specs=[row, row, row, vec, rows(TOK_WIDTH), rows(TOK_WIDTH)],
        out_specs=[rows(KV_COLS_PAD), vec],
        out_shape=[SDS((nb * s, KV_COLS_PAD), BF16), SDS((1, LANES), F32)], scratch_shapes=[pltpu.VMEM((1, LANES), F32)],
        compiler_params=_params(("arbitrary", "arbitrary")))(d_cum_q, d_cum_k, fl, fb, dk, dv)


PAIR = 2 * HEAD_DIM
N_PAIRS = FOX_HEADS // 2


def _lane_put(shape, h, col):
    lane = lax.broadcasted_iota(jnp.int32, shape, 1)
    return jnp.where(lane == h, col, 0.0)


def _half_masks(rows):
    lane = lax.broadcasted_iota(jnp.int32, (rows, PAIR), 1)
    return lane < HEAD_DIM


def _split_pair(x, scale=None):
    if scale is not None:
        x = x * scale
    lo = _half_masks(x.shape[0])
    zero = jnp.zeros_like(x)
    return jnp.where(lo, x, zero), jnp.where(lo, zero, x)


def _to_tile_rows(a, nb, s, tf):
    return a.reshape(nb * s // tf, tf, LANES)[:, :, :16].transpose(0, 2, 1)


def _from_tile_rows(a):
    tiles, _, tf = a.shape
    return jnp.pad(a.transpose(0, 2, 1), ((0, 0), (0, 0), (0, LANES - 16))).reshape(tiles * tf, LANES)


BIAS_TERMS = 3
LOOKAHEAD = 4
FOLLOW_FWD = 1
LOOKAHEAD_BWD = 2
FOLLOW_BWD = 1


def _bias_lane(h):
    return HEAD_DIM if h % 2 == 0 else 0


def _placement():
    rows = jnp.arange(LANES)[:, None]
    cols = jnp.arange(FOX_HEADS * PAIR)[None, :]
    head, lane = cols // PAIR, cols % PAIR
    first = jnp.where(head % 2 == 0, HEAD_DIM, 0)
    term = lane - first
    hit = (term >= 0) & (term < BIAS_TERMS) & (rows == 16 * term + head)
    return hit.astype(BF16)


def fox_prep(kv, fneg, nb, s):
    tf = min(TF, s)
    w = TOK_WIDTH

    def body(k_ref, v_ref, f_ref, pl_ref, ka_ref, vt_ref):
        lane = lax.broadcasted_iota(jnp.int32, (tf, LANES), 1)
        lo = lane < HEAD_DIM
        f = jnp.where(lane < FOX_HEADS, f_ref[...], 0.0)
        hi = f.astype(BF16).astype(F32)
        mid = (f - hi).astype(BF16).astype(F32)
        low = (f - hi - mid).astype(BF16).astype(F32)
        terms = (hi + pltpu.roll(mid, 16, axis=1) + pltpu.roll(low, 32, axis=1)).astype(BF16)
        placed = jnp.dot(terms, pl_ref[...], preferred_element_type=F32).astype(BF16)
        one = jnp.ones((tf, LANES), BF16)
        zero = jnp.zeros((tf, LANES), BF16)
        for p in range(N_PAIRS):
            kp = k_ref[:, p * PAIR:(p + 1) * PAIR] * QK_SCALE
            vp = v_ref[:, p * PAIR:(p + 1) * PAIR]
            he, ho = 2 * p, 2 * p + 1
            ka_ref[:, he * PAIR:(he + 1) * PAIR] = jnp.where(lo, kp, placed[:, he * PAIR:(he + 1) * PAIR])
            ka_ref[:, ho * PAIR:(ho + 1) * PAIR] = jnp.where(lo, placed[:, ho * PAIR:(ho + 1) * PAIR], kp)
            ve = jnp.where(lo, vp, jnp.where(lane == HEAD_DIM, one, zero))
            vo = jnp.where(lo, jnp.where(lane == 0, one, zero), vp)
            vt_ref[0, he * PAIR:(he + 1) * PAIR, :] = ve.astype(F32).T.astype(BF16)
            vt_ref[0, ho * PAIR:(ho + 1) * PAIR, :] = vo.astype(F32).T.astype(BF16)

    return pl.pallas_call(
        body, name="fox_prep", grid=(nb * s // tf,),
        in_specs=[pl.BlockSpec((tf, w), lambda r: (r, 0)), pl.BlockSpec((tf, w), lambda r: (r, 1)),
                  pl.BlockSpec((tf, LANES), lambda r: (r, 0)), pl.BlockSpec((LANES, FOX_HEADS * PAIR), lambda r: (0, 0))],
        out_specs=[pl.BlockSpec((tf, FOX_HEADS * PAIR), lambda r: (r, 0)),
                   pl.BlockSpec((1, FOX_HEADS * PAIR, tf), lambda r: (r, 0, 0))],
        out_shape=[SDS((nb * s, FOX_HEADS * PAIR), BF16), SDS((nb * s // tf, FOX_HEADS * PAIR, tf), BF16)],
        compiler_params=_params(("parallel",)))(kv, kv, fneg, _placement())


def fox_fwd_t(pq, kaug, vaug_t, nb, s):
    tf = min(TF, s)
    n = s // tf
    w = TOK_WIDTH
    wa = FOX_HEADS * PAIR

    def body(q_ref, k_hbm, vt_hbm, ob_ref, of_ref, lse_ref, k_vm, vt_vm, qx_scr, m_scr, acc_scr, sems):
        b = pl.program_id(0)
        i = pl.program_id(1)

        @pl.when(i == 0)
        def _():
            ck = pltpu.make_async_copy(k_hbm.at[pl.ds(pl.multiple_of(b * s, tf), s)], k_vm, sems.at[0])
            cv = pltpu.make_async_copy(vt_hbm.at[pl.ds(b * n, n)], vt_vm, sems.at[1])
            ck.start()
            cv.start()
            ck.wait()
            cv.wait()

        lane = lax.broadcasted_iota(jnp.int32, (tf, PAIR), 1)
        one = jnp.ones((tf, PAIR), BF16)
        zero = jnp.zeros((tf, PAIR), BF16)
        for p in range(N_PAIRS):
            qp = q_ref[:, p * PAIR:(p + 1) * PAIR]
            be, bo = _bias_lane(2 * p), _bias_lane(2 * p + 1)
            ones_e = jnp.where((lane >= be) & (lane < be + BIAS_TERMS), one, zero)
            ones_o = jnp.where((lane >= bo) & (lane < bo + BIAS_TERMS), one, zero)
            qx_scr[2 * p] = jnp.where(lane < HEAD_DIM, qp, ones_e)
            qx_scr[2 * p + 1] = jnp.where(lane < HEAD_DIM, ones_o, qp)
        m_scr[...] = jnp.full(m_scr.shape, NEG_BIG, F32)
        acc_scr[...] = jnp.zeros_like(acc_scr)

        def tile(j, masked):
            ks = pl.multiple_of(j * tf, tf)
            if masked:
                keep = lax.broadcasted_iota(jnp.int32, (tf, tf), 1) >= lax.broadcasted_iota(jnp.int32, (tf, tf), 0)
            def scores(h):
                kx = k_vm[pl.ds(ks, tf), h * PAIR:(h + 1) * PAIR]
                return lax.dot_general(kx, qx_scr[h], NT_DIMS, preferred_element_type=F32)

            def values(h, pr, a):
                pv = jnp.dot(vt_vm[j, h * PAIR:(h + 1) * PAIR, :], pr, preferred_element_type=F32)
                acc_scr[h] = a * acc_scr[h] + pv

            ahead = [scores(h) for h in range(LOOKAHEAD)]
            behind = []
            for h in range(FOX_HEADS):
                sc = ahead.pop(0)
                if h + LOOKAHEAD < FOX_HEADS:
                    ahead.append(scores(h + LOOKAHEAD))
                if masked:
                    sc = jnp.where(keep, sc, NEG_BIG)
                m_prev = m_scr[h]
                m_new = jnp.maximum(m_prev, jnp.max(sc, axis=0, keepdims=True))
                m_scr[h] = m_new
                behind.append((h, jnp.exp(sc - m_new).astype(BF16), jnp.exp(m_prev - m_new)))
                if len(behind) > FOLLOW_FWD:
                    values(*behind.pop(0))
            for item in behind:
                values(*item)

        def step(j, carry):
            tile(j, False)
            return carry

        lax.fori_loop(0, i, step, 0)
        tile(i, True)

        top = lax.broadcasted_iota(jnp.int32, (PAIR, tf), 0) < HEAD_DIM
        sub = lax.broadcasted_iota(jnp.int32, (16, tf), 0)
        lse = jnp.zeros((16, tf), F32)
        for p in range(N_PAIRS):
            he, ho = 2 * p, 2 * p + 1
            le = acc_scr[he, HEAD_DIM:HEAD_DIM + 1, :]
            lod = acc_scr[ho, 0:1, :]
            o = jnp.where(top, acc_scr[he] / le, acc_scr[ho] / lod).T
            ob_ref[:, p * PAIR:(p + 1) * PAIR] = o.astype(BF16)
            of_ref[:, p * PAIR:(p + 1) * PAIR] = o
            lse = jnp.where(sub == he, m_scr[he] + jnp.log(le), lse)
            lse = jnp.where(sub == ho, m_scr[ho] + jnp.log(lod), lse)
        lse_ref[0] = lse

    qrow = lambda b, i: (b * n + i, 0)
    return pl.pallas_call(
        body, name="fox_fwd", grid=(nb, n),
        in_specs=[pl.BlockSpec((tf, w), qrow), ANY_SPEC, ANY_SPEC],
        out_specs=[pl.BlockSpec((tf, w), qrow), pl.BlockSpec((tf, w), qrow),
                   pl.BlockSpec((1, 16, tf), lambda b, i: (b * n + i, 0, 0))],
        out_shape=[SDS((nb * s, w), BF16), SDS((nb * s, w), F32), SDS((nb * n, 16, tf), F32)],
        scratch_shapes=[pltpu.VMEM((s, wa), BF16), pltpu.VMEM((n, wa, tf), BF16),
                        pltpu.VMEM((FOX_HEADS, tf, PAIR), BF16), pltpu.VMEM((FOX_HEADS, 1, tf), F32),
                        pltpu.VMEM((FOX_HEADS, PAIR, tf), F32), pltpu.SemaphoreType.DMA((2,))],
        compiler_params=_params(("arbitrary", "arbitrary")))(pq, kaug, vaug_t)


def fox_delta(dcat, o, nb, s):
    tf = min(TM, s)
    w = TOK_WIDTH

    def body(do_ref, o_ref, dl_ref):
        out = jnp.zeros((tf, LANES), F32)
        for h in range(FOX_HEADS):
            lo, hi = h * HEAD_DIM, (h + 1) * HEAD_DIM
            out = out + _lane_put((tf, LANES), h, jnp.sum(do_ref[:, lo:hi] * o_ref[:, lo:hi], axis=1, keepdims=True))
        dl_ref[...] = out

    row = pl.BlockSpec((tf, w), lambda r: (r, 0))
    return pl.pallas_call(
        body, name="fox_delta", grid=(nb * s // tf,), in_specs=[row, row],
        out_specs=pl.BlockSpec((tf, LANES), lambda r: (r, 0)), out_shape=SDS((nb * s, LANES), F32),
        compiler_params=_params(("parallel",)))(dcat, o)


def fox_bwd(pq, kv, fneg, dcat_bf, lse_rows, delta_rows, nb, s):
    tf = min(TF, s)
    n = s // tf
    w = TOK_WIDTH

    def body(q_hbm, k_ref, v_ref, f_ref, do_hbm, lse_ref, dl_ref, dq_ref, dk_ref, dv_ref, dfk_ref, dfq_ref,
             q_vm, do_vm, km_scr, vm_scr, kt_scr, fk_scr, dk_scr, dv_scr, rs_scr, dq_scr, fq_scr, sems):
        b = pl.program_id(0)
        j = pl.program_id(1)

        @pl.when(j == 0)
        def _():
            rows = pl.ds(pl.multiple_of(b * s, tf), s)
            cq = pltpu.make_async_copy(q_hbm.at[rows, pl.ds(0, w)], q_vm, sems.at[0])
            cd = pltpu.make_async_copy(do_hbm.at[rows, pl.ds(0, w)], do_vm, sems.at[1])
            cq.start()
            cd.start()
            dq_scr[...] = jnp.zeros_like(dq_scr)
            fq_scr[...] = jnp.zeros_like(fq_scr)
            cq.wait()
            cd.wait()

        for p in range(N_PAIRS):
            kp = k_ref[:, p * PAIR:(p + 1) * PAIR] * QK_SCALE
            ke, ko = _split_pair(kp)
            km_scr[2 * p] = ke
            km_scr[2 * p + 1] = ko
            kt_scr[p] = kp.astype(F32).T.astype(BF16)
            ve, vo = _split_pair(v_ref[:, p * PAIR:(p + 1) * PAIR])
            vm_scr[2 * p] = ve
            vm_scr[2 * p + 1] = vo
        for h in range(FOX_HEADS):
            fk_scr[h] = jnp.broadcast_to(f_ref[:, h:h + 1], (tf, tf))
        dk_scr[...] = jnp.zeros_like(dk_scr)
        dv_scr[...] = jnp.zeros_like(dv_scr)
        rs_scr[...] = jnp.zeros_like(rs_scr)

        def tile(i, masked):
            qs = pl.multiple_of(i * tf, tf)
            if masked:
                keep = lax.broadcasted_iota(jnp.int32, (tf, tf), 1) >= lax.broadcasted_iota(jnp.int32, (tf, tf), 0)
            def products(h):
                qp = q_vm[pl.ds(qs, tf), (h // 2) * PAIR:(h // 2 + 1) * PAIR]
                dop = do_vm[pl.ds(qs, tf), (h // 2) * PAIR:(h // 2 + 1) * PAIR]
                return (lax.dot_general(km_scr[h], qp, NT_DIMS, preferred_element_type=F32),
                        lax.dot_general(vm_scr[h], dop, NT_DIMS, preferred_element_type=F32))

            def dependents(h, prb, dsb):
                p = h // 2
                half = slice((h % 2) * HEAD_DIM, (h % 2 + 1) * HEAD_DIM)
                qp = q_vm[pl.ds(qs, tf), p * PAIR:(p + 1) * PAIR]
                dop = do_vm[pl.ds(qs, tf), p * PAIR:(p + 1) * PAIR]
                dv_scr[h] += jnp.dot(prb, dop, preferred_element_type=F32)
                dk_scr[h] += jnp.dot(dsb, qp, preferred_element_type=F32)
                dqt = jnp.dot(kt_scr[p], dsb, preferred_element_type=F32)
                dq_scr[i, p, half, :] += dqt[(h % 2) * HEAD_DIM:(h % 2 + 1) * HEAD_DIM]

            ahead = [products(h) for h in range(LOOKAHEAD_BWD)]
            behind = []
            for h in range(FOX_HEADS):
                sc, dp = ahead.pop(0)
                if h + LOOKAHEAD_BWD < FOX_HEADS:
                    ahead.append(products(h + LOOKAHEAD_BWD))
                sc = sc + fk_scr[h] - lse_ref[i, h:h + 1, :]
                if masked:
                    sc = jnp.where(keep, sc, NEG_BIG)
                pr = jnp.exp(sc)
                ds = pr * (dp - dl_ref[i, h:h + 1, :])
                part = ds[:, :LANES]
                for c in range(1, tf // LANES):
                    part = part + ds[:, c * LANES:(c + 1) * LANES]
                rs_scr[h] += part
                fq_scr[i, h:h + 1, :] += jnp.sum(ds, axis=0, keepdims=True)
                behind.append((h, pr.astype(BF16), ds.astype(BF16)))
                if len(behind) > FOLLOW_BWD:
                    dependents(*behind.pop(0))
            for item in behind:
                dependents(*item)

        def step(i, carry):
            tile(i, False)
            return carry

        tile(j, True)
        for p in range(N_PAIRS):
            dq_ref[:, p * PAIR:(p + 1) * PAIR] = dq_scr[j, p].T.astype(BF16)
        dfq_ref[0] = fq_scr[j]
        lax.fori_loop(j + 1, n, step, 0)

        lo = _half_masks(tf)
        dfk = jnp.zeros((tf, LANES), F32)
        for p in range(N_PAIRS):
            dk = jnp.where(lo, dk_scr[2 * p], dk_scr[2 * p + 1]) * QK_SCALE
            dk_ref[:, p * PAIR:(p + 1) * PAIR] = dk.astype(BF16)
            dv_ref[:, p * PAIR:(p + 1) * PAIR] = jnp.where(lo, dv_scr[2 * p], dv_scr[2 * p + 1]).astype(BF16)
            for h in (2 * p, 2 * p + 1):
                dfk = dfk - _lane_put((tf, LANES), h, jnp.sum(rs_scr[h], axis=1, keepdims=True))
        dfk_ref[...] = dfk

    krow = lambda b, j: (b * n + j, 0)
    rows = pl.BlockSpec((n, 16, tf), lambda b, j: (b, 0, 0))
    tile_out = pl.BlockSpec((tf, w), krow)
    return pl.pallas_call(
        body, name="fox_bwd", grid=(nb, n),
        in_specs=[ANY_SPEC, pl.BlockSpec((tf, w), krow), pl.BlockSpec((tf, w), lambda b, j: (b * n + j, 1)),
                  pl.BlockSpec((tf, LANES), krow), ANY_SPEC, rows, rows],
        out_specs=[tile_out, tile_out, tile_out, pl.BlockSpec((tf, LANES), krow),
                   pl.BlockSpec((1, 16, tf), lambda b, j: (b * n + j, 0, 0))],
        out_shape=[SDS((nb * s, w), BF16), SDS((nb * s, w), BF16), SDS((nb * s, w), BF16), SDS((nb * s, LANES), F32),
                   SDS((nb * n, 16, tf), F32)],
        scratch_shapes=[pltpu.VMEM((s, w), BF16), pltpu.VMEM((s, w), BF16),
                        pltpu.VMEM((FOX_HEADS, tf, PAIR), BF16), pltpu.VMEM((FOX_HEADS, tf, PAIR), BF16),
                        pltpu.VMEM((N_PAIRS, PAIR, tf), BF16), pltpu.VMEM((FOX_HEADS, tf, tf), F32),
                        pltpu.VMEM((FOX_HEADS, tf, PAIR), F32), pltpu.VMEM((FOX_HEADS, tf, PAIR), F32),
                        pltpu.VMEM((FOX_HEADS, tf, LANES), F32), pltpu.VMEM((n, N_PAIRS, PAIR, tf), F32),
                        pltpu.VMEM((n, 16, tf), F32), pltpu.SemaphoreType.DMA((2,))],
        compiler_params=_params(("arbitrary", "arbitrary")))(pq, kv, kv, fneg, dcat_bf, lse_rows, delta_rows)


ADAMW_TILE_ELEMS = 128 * 1024


def reduce_adamw(parts, w, m, v, name):
    layers, r, c = w.shape
    tr = r
    for cand in range(16, r, 16):
        if r % cand == 0 and cand * c <= ADAMW_TILE_ELEMS:
            tr = cand
    c1 = 1.0 - ADAM_B1 ** ADAM_STEP
    c2 = 1.0 - ADAM_B2 ** ADAM_STEP

    def body(*refs):
        p_refs = refs[:layers]
        w_ref, m_ref, v_ref, g_out, d_out, m_out, v_out = refs[layers:]

        def update(p_ref):
            g = p_ref[0].astype(F32)
            for k in range(1, N_DEV):
                g = g + p_ref[k].astype(F32)
            mn = ADAM_B1 * m_ref[0] + (1.0 - ADAM_B1) * g
            vn = ADAM_B2 * v_ref[0] + (1.0 - ADAM_B2) * (g * g)
            g_out[0] = g
            m_out[0] = mn
            v_out[0] = vn
            d_out[0] = -ADAM_LR * ((mn / c1) / (jnp.sqrt(vn / c2) + ADAM_EPS) + ADAM_WD * w_ref[0])

        if layers == 1:
            update(p_refs[0])
        else:
            for layer in range(layers):
                pl.when(pl.program_id(0) == layer)(lambda layer=layer: update(p_refs[layer]))

    row = pl.BlockSpec((1, tr, c), lambda l, i: (l, i, 0))
    return pl.pallas_call(
        body, name=name, grid=(layers, r // tr),
        in_specs=[pl.BlockSpec((N_DEV, tr, c), lambda l, i: (0, i, 0))] * layers + [row, row, row],
        out_specs=[row, row, row, row], out_shape=[SDS((layers, r, c), F32)] * 4,
        compiler_params=_params(("parallel", "parallel")))(*parts, w, m, v)


N_PEERS = N_DEV - 1
HBM_SPEC = pl.BlockSpec(memory_space=pltpu.HBM)
SEM_SPEC = pl.BlockSpec(memory_space=pltpu.SEMAPHORE)
ANY_SPEC = pl.BlockSpec(memory_space=pl.ANY)
SPLIT_EFFECT = pltpu.SideEffectType.DATAFLOW_SIDE_EFFECTING


def _peers(with_self=False):
    x, y, c = lax.axis_index("x"), lax.axis_index("y"), lax.axis_index("c")
    peers = []
    for k in range(0 if with_self else 1, N_DEV):
        px = 1 - x if (k >> 2) & 1 else x
        py = 1 - y if (k >> 1) & 1 else y
        pc = 1 - c if k & 1 else c
        peers.append(((px, py, pc), 4 * px + 2 * py + pc))
    return 4 * x + 2 * y + c, peers


def _push(src, dst, send_sems, recv_sems, slot, dev):
    return pltpu.make_async_remote_copy(src_ref=src, dst_ref=dst, send_sem=send_sems.at[slot], recv_sem=recv_sems.at[slot],
                                        device_id=dev, device_id_type=pl.DeviceIdType.MESH)


def _landing_shapes(arrs, scatter):
    return [SDS((N_DEV,) + tuple(a.shape[1:] if sc else a.shape), a.dtype) for a, sc in zip(arrs, scatter)]


def exchange(arrs, scatter, name):
    na = len(arrs)

    def body(*refs):
        ins = refs[:na]
        outs = refs[na:2 * na]
        send_sems, recv_sems, local_sems = refs[2 * na:]
        me, peers = _peers()
        local = []
        remote = []
        for a in range(na):
            lc = pltpu.make_async_copy(ins[a].at[me] if scatter[a] else ins[a], outs[a].at[me], local_sems.at[a])
            lc.start()
            local.append(lc)
            for k, (dev, idx) in enumerate(peers):
                cp = _push(ins[a].at[idx] if scatter[a] else ins[a], outs[a].at[me], send_sems, recv_sems,
                           a * N_PEERS + k, dev)
                cp.start()
                remote.append(cp)
        for a in range(na):
            for k, (dev, idx) in enumerate(peers):
                _push(ins[a].at[me] if scatter[a] else ins[a], outs[a].at[idx], send_sems, recv_sems,
                      a * N_PEERS + k, dev).wait_recv()
        for cp in remote:
            cp.wait_send()
        for lc in local:
            lc.wait()

    return pl.pallas_call(
        body, name=name, in_specs=[HBM_SPEC] * na, out_specs=[HBM_SPEC] * na, out_shape=_landing_shapes(arrs, scatter),
        scratch_shapes=[pltpu.SemaphoreType.DMA((na * N_PEERS,)), pltpu.SemaphoreType.DMA((na * N_PEERS,)),
                        pltpu.SemaphoreType.DMA((na,))])(*arrs)


def exchange_start(arrs, scatter, after, name):
    na = len(arrs)
    lands = [lax.empty(l.shape, l.dtype) for l in _landing_shapes(arrs, scatter)]

    def body(*refs):
        ins = refs[:na]
        land = refs[na:2 * na]
        send_sems, recv_sems = refs[2 * na + 1], refs[2 * na + 2]
        token = refs[-1]
        me, peers = _peers(with_self=True)
        for a in range(na):
            for k, (dev, idx) in enumerate(peers):
                _push(ins[a].at[idx] if scatter[a] else ins[a], land[a].at[me], send_sems, recv_sems,
                      a * N_DEV + k, dev).start()
        token[...] = jnp.zeros_like(token)

    thru = [pltpu.HBM(a.shape, a.dtype) for a in arrs] + [pltpu.HBM(l.shape, l.dtype) for l in lands]
    res = pl.pallas_call(
        body, name=name,
        out_shape=(pltpu.SemaphoreType.DMA((na * N_DEV,)), pltpu.SemaphoreType.DMA((na * N_DEV,)), *thru,
                   SDS((8, LANES), F32)),
        in_specs=[HBM_SPEC] * (2 * na) + [ANY_SPEC],
        out_specs=(SEM_SPEC, SEM_SPEC, *([HBM_SPEC] * (2 * na)), pl.BlockSpec(memory_space=pltpu.VMEM)),
        input_output_aliases={i: 2 + i for i in range(2 * na)},
        compiler_params=pltpu.CompilerParams(has_side_effects=SPLIT_EFFECT),
    )(*[pltpu.with_memory_space_constraint(a, pltpu.HBM) for a in arrs],
      *[pltpu.with_memory_space_constraint(l, pltpu.HBM) for l in lands], after)
    return {"send": res[0], "recv": res[1], "src": res[2:2 + na], "land": res[2 + na:2 + 2 * na],
            "token": res[-1][0, 0], "scatter": scatter}


def exchange_wait(handle, after, name):
    scatter = handle["scatter"]
    na = len(scatter)

    def body(*refs):
        src = refs[:na]
        land = refs[na:2 * na]
        send_sems, recv_sems = refs[2 * na], refs[2 * na + 1]
        me, peers = _peers(with_self=True)
        for a in range(na):
            for k, (dev, idx) in enumerate(peers):
                cp = _push(src[a].at[me] if scatter[a] else src[a], land[a].at[idx], send_sems, recv_sems,
                           a * N_DEV + k, dev)
                cp.wait_send()
                cp.wait_recv()

    ops = list(handle["src"]) + list(handle["land"])
    res = pl.pallas_call(
        body, name=name, out_shape=tuple(pltpu.HBM(o.shape, o.dtype) for o in ops),
        in_specs=[HBM_SPEC] * (2 * na) + [SEM_SPEC, SEM_SPEC, ANY_SPEC], out_specs=tuple([HBM_SPEC] * (2 * na)),
        input_output_aliases={i: i for i in range(2 * na)},
        compiler_params=pltpu.CompilerParams(has_side_effects=SPLIT_EFFECT),
    )(*ops, handle["send"], handle["recv"], after)
    return list(res[na:])


def forward_layer(l, xin, xin_bf, mem_bf, wt, nb, s, ffn_weights=None):
    sv = {"xin_bf": xin_bf}
    memkv = mm_nn(mem_bf, wt["memw"], BF16, f"memkv{l}")
    sv["memkv"] = memkv
    if l == 0:
        proj = mm_nn(xin_bf, wt["win_a"], F32, "proj_a")
        pooled, tok = pool_fwd(proj, wt["pw_bd"], wt["pscale"], nb, s)
        sv["pooled"] = pooled
    else:
        kv = mm_nn(xin_bf, wt["kvw"][:, :2 * TOK_WIDTH], BF16, "kv_proj")
        fl = mm_nn(xin_bf, wt["kvw"][:, 2 * TOK_WIDTH:], F32, "gate_proj")
        fneg = -fgate_fwd(fl, wt["fb"], nb, s)
        proj = mm_nn(xin_bf, wt["wq"], BF16, "proj_b")
        kaug, vaug_t = fox_prep(kv, fneg, nb, s)
        tok, o_f32, lse_rows = fox_fwd_t(proj, kaug, vaug_t, nb, s)
        sv.update(kv=kv, fl=fl, fneg=fneg, o_f32=o_f32, lse_rows=lse_rows)
    sv["proj"] = proj
    cat = memattn_fwd(proj, memkv, tok, nb, s, f"memattn_fwd{l}")
    sv["cat"] = cat
    x1, x1_bf, xh1, rs1 = ln_fwd(xin, cat, wt["wout"], wt["ln1_g"], wt["ln1_b"], f"out_proj_ln1_{l}")
    sv.update(x1_bf=x1_bf, xh1=xh1, rs1=rs1)
    if ffn_weights is not None:
        wt.update(ffn_weights(x1_bf))
    act, ga, gb, hu, hg = ffn_up_gate(x1_bf, wt["wup"], wt["cw"], nb, s, f"ffn_up_gate{l}")
    sv.update(act=act, ga=ga, gb=gb, hu=hu, hg=hg)
    x2, x2_bf, xh2, rs2 = ln_fwd(x1, act, wt["wdown"], wt["ln2_g"], wt["ln2_b"], f"ffn_down_ln2_{l}")
    sv.update(xh2=xh2, rs2=rs2)
    return x2, x2_bf, sv


def backward_layer(l, dy, sv, mem_bf, wt, nb, s, after_ffn=None, after_pool=None, loss_target=None):
    g = {}
    if loss_target is None:
        dr2, dr2_bf, g["ln2_g"], g["ln2_b"] = ln_bwd(dy[0], sv["xh2"], sv["rs2"], wt["ln2_g"], f"ln2_bwd{l}",
                                                     dy_scale=dy[1], products=dy[2])
    else:
        dr2, dr2_bf, g["ln2_g"], g["ln2_b"], g["loss_row"] = loss_ln_bwd(sv["xh2"], sv["rs2"], wt["ln2_g"], wt["ln2_b"],
                                                                         loss_target, f"loss_ln2_bwd{l}")
    dact = mm_nn(dr2_bf, wt["wdown"], BF16, f"ffn_down_dx{l}", trans_b=0)
    g["wdown"] = mm_tn(sv["act"], dr2_bf, f"ffn_down_dw{l}")
    dh_u, dh_g, dcw_u, dcw_g = gate_conv_bwd(dact, sv["ga"], sv["gb"], sv["hu"], sv["hg"], wt["cw"], nb, s,
                                             f"gate_conv_bwd{l}")
    g["cw"] = jnp.concatenate([dcw_u, dcw_g], axis=0)
    g["wup"] = jnp.concatenate([mm_tn(dh_u, sv["x1_bf"], f"ffn_up_dw_u{l}"),
                                mm_tn(dh_g, sv["x1_bf"], f"ffn_up_dw_g{l}")], axis=0)
    ln1_g = wt["ln1_g"] if after_ffn is None else wt["ln1_g"] + after_ffn(g, dr2)
    dr1, dr1_bf, g["ln1_g"], g["ln1_b"] = ln_bwd(dr2, sv["xh1"], sv["rs1"], ln1_g, f"ffn_up_dx_ln1_bwd{l}",
                                                 dy_scale=DN_ALPHA, products=[(dh_u, wt["wup"], 0), (dh_g, wt["wup"], 1)])
    dcat, dcat_bf = mm_nn(dr1_bf, wt["wout"], F32, f"out_proj_dx{l}", also_bf16=True, trans_b=0)
    g["wout"] = mm_tn(sv["cat"], dr1_bf, f"out_proj_dw{l}")
    if l == 0:
        dmixed, dpooled, g["pscale"] = pool_bwd_mix(dcat, sv["pooled"], wt["pw_bd"], wt["pscale"], nb, s)
        g["pw_full"] = mm_tn(sv["pooled"], dmixed, "pool_dw", out_dtype=F32)
        dtok = pool_bwd_window(dpooled, nb, s)
    else:
        delta = fox_delta(dcat, sv["o_f32"], nb, s)
        tf = min(TF, s)
        dtok, dk, dv, dfcum_k, dfq_rows = fox_bwd(sv["proj"], sv["kv"], sv["fneg"], dcat_bf,
                                                  sv["lse_rows"], _to_tile_rows(delta, nb, s, tf), nb, s)
    dproj, dmemkv = memattn_bwd(sv["proj"], sv["memkv"], dcat, dtok, nb, s, f"memattn_bwd{l}")
    g["memw"] = mm_tn(mem_bf, dmemkv, f"memkv_dw{l}")
    if l == 0:
        win_a = wt["win_a"] if after_pool is None else wt["win_a"] + after_pool(g, dproj).astype(BF16)
        dx = mm_nn(dproj, win_a, F32, "proj_a_dx", addend=dr1, add_scale=DN_ALPHA, trans_b=0)
        g["win_a"] = mm_tn(sv["xin_bf"], dproj, "proj_a_dw")
    else:
        dkvf, g["fb"] = fgate_bwd(_from_tile_rows(dfq_rows), dfcum_k, sv["fl"], wt["fb"], dk, dv, nb, s)
        dx = (dr1, DN_ALPHA, [(dproj, wt["wq"], 0), (dkvf, wt["kvw"], 0)])
        g["wq"] = mm_tn(sv["xin_bf"], dproj, "proj_b_dw")
        g["kvw"] = mm_tn(sv["xin_bf"], dkvf, "kv_proj_dw")
    return dx, g


def pack_replicated(pool_w, ln1_g, ln1_b, ln2_g, ln2_b, conv_b, f_b):
    cb = jnp.pad(conv_b, ((0, 0), (0, 6144 - 5504))).reshape(12, D_MODEL)
    fb = jnp.pad(f_b.reshape(1, FOX_HEADS), ((0, 3), (0, D_MODEL - FOX_HEADS)))
    return jnp.concatenate([pool_w.reshape(144, D_MODEL), ln1_g, ln1_b, ln2_g, ln2_b, cb, fb], axis=0)


def unpack_replicated(buf):
    pool_w = buf[:144].reshape(1, 4, POOL_GROUP, POOL_GROUP)
    ln = [buf[144 + 2 * k:146 + 2 * k] for k in range(4)]
    conv_b = buf[152:164].reshape(2, 6144)[:, :5504]
    f_b = buf[164, :FOX_HEADS]
    return pool_w, ln[0], ln[1], ln[2], ln[3], conv_b, f_b


def _pad_ff(a, axis):
    zeros = jnp.zeros(a.shape[:axis] + (FF_ROWS_PAD - FF_ROWS,) + a.shape[axis + 1:], a.dtype)
    halves = [lax.slice_in_dim(a, h * FF_ROWS, (h + 1) * FF_ROWS, axis=axis) for h in range(2)]
    return jnp.concatenate([halves[0], zeros, halves[1], zeros], axis=axis)


def _unpad_ff(a, axis):
    return jnp.concatenate([lax.slice_in_dim(a, h * FF_ROWS_PAD, h * FF_ROWS_PAD + FF_ROWS, axis=axis) for h in range(2)],
                           axis=axis)


def pack_small(conv_w, pool_scale):
    buf = jnp.zeros((16, FF_BLOCK_PAD), F32)
    buf = lax.dynamic_update_slice(buf, _pad_ff(conv_w.reshape(DEPTH * 3, FF_BLOCK), 1), (0, 0))
    return lax.dynamic_update_slice(buf, pool_scale, (8, 0))


def _block_diag(pw):
    out = jnp.zeros((TOK_WIDTH, TOK_WIDTH), pw.dtype)
    for g in range(4):
        out = lax.dynamic_update_slice(out, pw[g], (g * POOL_GROUP, g * POOL_GROUP))
    return out


def layer_shards(l, sq_a, sq_b, mem_w_kv, ffn_w_up, ffn_w_down):
    wdown = jnp.pad(ffn_w_down[l], ((0, FF_ROWS_PAD - FF_ROWS), (0, 0)))
    return [sq_a[0].astype(BF16), sq_b[0].astype(BF16), mem_w_kv[l].astype(BF16), _pad_ff(ffn_w_up[l], 1).astype(BF16),
            wdown.astype(BF16)]


def mixer_weights(l, gath, ln1_g, ln1_b, ln2_g, ln2_b):
    w_out = gath[1].reshape(D_MODEL, D_MODEL)
    wt = {"memw": gath[2].reshape(D_MODEL, 2 * MEM_WIDTH), "wout": w_out,
          "ln1_g": ln1_g[l:l + 1], "ln1_b": ln1_b[l:l + 1], "ln2_g": ln2_g[l:l + 1], "ln2_b": ln2_b[l:l + 1]}
    return wt, gath[0].reshape(D_MODEL, D_MODEL)


def ffn_weights(l, wup_g, wdown_g, small, conv_b):
    cb = _pad_ff(conv_b[l].reshape(N_DEV, FF_BLOCK), 1)
    cw = jnp.concatenate([small[:, 3 * l:3 * l + 3, :], cb[:, None, :], jnp.zeros((N_DEV, 4, FF_BLOCK_PAD), F32)], axis=1)
    return {"wup": wup_g, "wdown": wdown_g.reshape(FF_PAIRS * FF_BLOCK_PAD, D_MODEL), "cw": cw}


def mixer_grad_blocks(g, w_in_grad):
    blocks = [] if w_in_grad is None else [w_in_grad.reshape(N_DEV, 128, D_MODEL)]
    blocks += [g["wout"].reshape(N_DEV, 128, D_MODEL), g["memw"].reshape(N_DEV, 128, 2 * MEM_WIDTH)]
    return [b.astype(BF16) for b in blocks]


def ffn_grad_blocks(g):
    wdown = g["wdown"].reshape(N_DEV, FF_ROWS_PAD, D_MODEL)[:, :FF_ROWS]
    wup_t = _unpad_ff(g["wup"].reshape(N_DEV, FF_BLOCK_PAD, D_MODEL), 1)
    return [wup_t.astype(BF16), wdown.astype(BF16)]


def small_grad_blocks(g0, g1):
    taps = jnp.stack([g0["cw"][:, :3, :], g1["cw"][:, :3, :]], axis=1).reshape(N_DEV, DEPTH * 3, FF_BLOCK_PAD)
    small = jnp.zeros((N_DEV, 16, FF_BLOCK_PAD), F32)
    small = lax.dynamic_update_slice(small, taps, (0, 0, 0))
    return lax.dynamic_update_slice(small, g0["pscale"].reshape(N_DEV, 1, 96), (0, 8, 0))


def replicated_grads(g0, g1):
    pw = jnp.stack([g0["pw_full"][k * POOL_GROUP:(k + 1) * POOL_GROUP, k * POOL_GROUP:(k + 1) * POOL_GROUP] for k in range(4)])
    conv_b = jnp.stack([_unpad_ff(g_["cw"][:, 3, :], 1).reshape(N_DEV * FF_BLOCK) for g_ in (g0, g1)])
    ln = [jnp.concatenate([g0[n], g1[n]], axis=0) for n in ("ln1_g", "ln1_b", "ln2_g", "ln2_b")]
    return pack_replicated(pw[None], ln[0], ln[1], ln[2], ln[3], conv_b, g1["fb"][0, :FOX_HEADS])


def kernel(x, mem, a_w_in, a_pool_w, a_pool_scale, a_w_out, b_w_q, b_w_out, kv_w, f_b, mem_w_kv, ln1_g, ln1_b, ln2_g, ln2_b, ffn_w_up, ffn_conv_w, ffn_conv_b, ffn_w_down, loss_target, m_a_w_in, m_a_pool_w, m_a_pool_scale, m_a_w_out, m_b_w_q, m_b_w_out, m_kv_w, m_f_b, m_mem_w_kv, m_ln1_g, m_ln1_b, m_ln2_g, m_ln2_b, m_ffn_w_up, m_ffn_conv_w, m_ffn_conv_b, m_ffn_w_down, v_a_w_in, v_a_pool_w, v_a_pool_scale, v_a_w_out, v_b_w_q, v_b_w_out, v_kv_w, v_f_b, v_mem_w_kv, v_ln1_g, v_ln1_b, v_ln2_g, v_ln2_b, v_ffn_w_up, v_ffn_conv_w, v_ffn_conv_b, v_ffn_w_down):
    nb, s, d = x.shape
    t = nb * s
    x2d, mem_bf, target = x.reshape(t, d), mem.reshape(nb * MEM_LEN, d).astype(BF16), loss_target.reshape(t, d)

    shards0 = layer_shards(0, a_w_in, a_w_out, mem_w_kv, ffn_w_up, ffn_w_down)
    shards1 = layer_shards(1, b_w_q, b_w_out, mem_w_kv, ffn_w_up, ffn_w_down)
    shards1.append(jnp.pad(kv_w, ((0, 0), (0, KV_COLS_PAD - KV_COLS))).astype(BF16))
    gath0 = exchange(shards0[:3] + [pack_small(ffn_conv_w, a_pool_scale)], [False] * 4, "gather_w0_mixer")
    pending = {"ffn0": exchange_start(shards0[3:], [False] * 2, gath0[0], "gather_w0_ffn_start")}
    small = gath0[3]
    wt0, w_in = mixer_weights(0, gath0, ln1_g + pending["ffn0"]["token"], ln1_b, ln2_g, ln2_b)
    pw_bd = _block_diag(a_pool_w[0])
    wt0.update(win_a=w_in, pw_bd=pw_bd.astype(BF16),
               pscale=small[:, 8, :96].reshape(1, TOK_WIDTH) + pending["ffn0"]["token"])

    def ffn0_weights(x1_bf):
        got = exchange_wait(pending["ffn0"], x1_bf, "gather_w0_ffn_wait")
        pending["w1"] = exchange_start(shards1, [False] * 6, got[0], "gather_w1_start")
        w = ffn_weights(0, got[0], got[1], small, ffn_conv_b)
        w["cw"] = w["cw"] + pending["w1"]["token"]
        return w

    x1, x1_bf, sv0 = forward_layer(0, x2d, x2d, mem_bf, wt0, nb, s, ffn_weights=ffn0_weights)
    gath1 = exchange_wait(pending["w1"], x1_bf, "gather_w1_wait")
    wt1, w_q = mixer_weights(1, gath1, ln1_g, ln1_b, ln2_g, ln2_b)
    wt1.update(ffn_weights(1, gath1[3], gath1[4], small, ffn_conv_b))
    kvw = gath1[5].reshape(D_MODEL, KV_COLS_PAD)
    wt1.update(wq=w_q, kvw=kvw,
               fb=jnp.pad(f_b.reshape(1, FOX_HEADS), ((0, 0), (0, LANES - FOX_HEADS))))
    _, _, sv1 = forward_layer(1, x1, x1_bf, mem_bf, wt1, nb, s)

    dx1, g1 = backward_layer(1, None, sv1, mem_bf, wt1, nb, s, loss_target=target)
    loss = lax.psum(g1["loss_row"][0, 0], ("x", "y", "c"))
    blocks1 = (mixer_grad_blocks(g1, g1["wq"]) + ffn_grad_blocks(g1)
               + [g1["kvw"][:, :KV_COLS].reshape(N_DEV, 128, KV_COLS).astype(BF16)])
    pending["g1"] = exchange_start(blocks1, [True] * 6, dx1[0], "scatter_g1_start")
    wt0["ln2_g"] = wt0["ln2_g"] + pending["g1"]["token"]

    def after_ffn0(g, dxm):
        pending["gf0"] = exchange_start(ffn_grad_blocks(g), [True] * 2, dxm, "scatter_g0_ffn_start")
        return pending["gf0"]["token"]

    def after_pool0(g, x):
        blocks = mixer_grad_blocks(g, None) + [small_grad_blocks(g, g1), replicated_grads(g, g1)]
        pending["gm0"] = exchange_start(blocks, [True] * 3 + [False], x, "scatter_g0_mixer_start")
        return pending["gm0"]["token"]

    grad_x, g0 = backward_layer(0, dx1, sv0, mem_bf, wt0, nb, s, after_ffn=after_ffn0, after_pool=after_pool0)
    pending["gin"] = exchange_start([g0["win_a"].reshape(N_DEV, 128, D_MODEL).astype(BF16)], [True], grad_x,
                                    "scatter_g0_in_start")
    parts_f0 = exchange_wait(pending["gf0"], jnp.zeros((8, LANES), F32) + pending["gin"]["token"], "scatter_g0_ffn_wait")
    parts1 = exchange_wait(pending["g1"], parts_f0[0], "scatter_g1_wait")

    res = {}

    def upd(nm, parts, w2, m2, v2):
        res[nm] = reduce_adamw(parts, w2, m2, v2, f"adamw_{nm}")

    upd("b_w_q", [parts1[0]], b_w_q, m_b_w_q, v_b_w_q)
    upd("b_w_out", [parts1[1]], b_w_out, m_b_w_out, v_b_w_out)
    upd("kv_w", [parts1[5]], kv_w[None], m_kv_w[None], v_kv_w[None])
    upd("ffn_w_up", [parts_f0[0], parts1[3]], *[a.transpose(0, 2, 1) for a in (ffn_w_up, m_ffn_w_up, v_ffn_w_up)])
    res["ffn_w_up"] = [o.transpose(0, 2, 1) for o in res["ffn_w_up"]]
    upd("ffn_w_down", [parts_f0[1], parts1[4]], ffn_w_down, m_ffn_w_down, v_ffn_w_down)
    parts_m0 = exchange_wait(pending["gm0"], res["ffn_w_down"][0], "scatter_g0_mixer_wait")
    parts_in = exchange_wait(pending["gin"], parts_m0[0], "scatter_g0_in_wait")
    upd("a_w_in", [parts_in[0]], a_w_in, m_a_w_in, v_a_w_in)
    upd("a_w_out", [parts_m0[0]], a_w_out, m_a_w_out, v_a_w_out)
    upd("mem_w_kv", [parts_m0[1], parts1[2]], mem_w_kv, m_mem_w_kv, v_mem_w_kv)
    upd("small", [parts_m0[2]], pack_small(ffn_conv_w, a_pool_scale)[None], pack_small(m_ffn_conv_w, m_a_pool_scale)[None],
        pack_small(v_ffn_conv_w, v_a_pool_scale)[None])
    upd("replicated", [parts_m0[3]], pack_replicated(a_pool_w, ln1_g, ln1_b, ln2_g, ln2_b, ffn_conv_b, f_b)[None],
        pack_replicated(m_a_pool_w, m_ln1_g, m_ln1_b, m_ln2_g, m_ln2_b, m_ffn_conv_b, m_f_b)[None],
        pack_replicated(v_a_pool_w, v_ln1_g, v_ln1_b, v_ln2_g, v_ln2_b, v_ffn_conv_b, v_f_b)[None])

    res["kv_w"] = [o[0] for o in res["kv_w"]]
    res["ffn_conv_w"] = [_unpad_ff(o[0, :DEPTH * 3, :], 1).reshape(DEPTH, 3, FF_BLOCK) for o in res["small"]]
    res["a_pool_scale"] = [o[0, 8:9, :96] for o in res["small"]]
    rep_names = ["a_pool_w", "ln1_g", "ln1_b", "ln2_g", "ln2_b", "ffn_conv_b", "f_b"]
    for nm in rep_names:
        res[nm] = []
    for o in res["replicated"]:
        for nm, val in zip(rep_names, unpack_replicated(o[0])):
            res[nm].append(val)

    order = ["a_w_in", "a_pool_w", "a_pool_scale", "a_w_out", "b_w_q", "b_w_out", "kv_w", "f_b", "mem_w_kv",
             "ln1_g", "ln1_b", "ln2_g", "ln2_b", "ffn_w_up", "ffn_conv_w", "ffn_conv_b", "ffn_w_down"]
    out = [loss, grad_x.reshape(nb, s, d)]
    for kind in range(4):
        out.extend(res[nm][kind] for nm in order)
    return tuple(out)
```

```python
import jax
import jax.numpy as jnp
from jax import lax
from jax.experimental import pallas as pl
from jax.experimental.pallas import tpu as pltpu

F32 = jnp.float32
BF16 = jnp.bfloat16
SDS = jax.ShapeDtypeStruct

N_DEV = 8
D_MODEL = 1024
TOK_WIDTH = 768
MEM_WIDTH = 256
MEM_LEN = 256
MEM_HEADS = 4
HEAD_DIM = 64
FOX_HEADS = 12
POOL_GROUP = 192
FF_BLOCK = 688
FF_BLOCK_PAD = 768
FF_PAIRS = 4
FF_ROWS = 344
FF_ROWS_PAD = FF_BLOCK_PAD // 2
KV_COLS = 1548
KV_COLS_PAD = 1664
LANES = 128
DEPTH = 2
DN_ALPHA = (2.0 * DEPTH) ** 0.25
LN_EPS = 1e-5
QK_SCALE = HEAD_DIM ** -0.5
NEG_BIG = -1e30

ADAM_LR = 0.001
ADAM_B1 = 0.9
ADAM_B2 = 0.999
ADAM_EPS = 1e-08
ADAM_WD = 0.01
ADAM_STEP = 10

VMEM_LIMIT_BYTES = 56 * 1024 * 1024
MM_BLOCK_BYTES = 6 * 1024 * 1024
TM = 512
TS = 256
TF = 256
TC = 256
HALO_POOL = 16
HALO_CONV = 8

NT_DIMS = (((1,), (1,)), ((), ()))
TN_DIMS = (((0,), (0,)), ((), ()))


def _params(sem=None):
    return pltpu.CompilerParams(dimension_semantics=sem, vmem_limit_bytes=VMEM_LIMIT_BYTES)


def _sigmoid(z):
    return 1.0 / (1.0 + jnp.exp(-z))


def _pick_tn(n):
    if n <= 2048:
        return n
    for t in (1024, 768, 512, 256, 128):
        if n % t == 0:
            return t
    return n


def mm_nn(a, b, out_dtype, name, addend=None, add_scale=1.0, also_bf16=False, trans_b=None):
    m, k = a.shape
    n = b.shape[1] if trans_b is None else b.shape[0]
    tm = min(TM, m)
    tn = n
    while k * tn * 2 > MM_BLOCK_BYTES or tm * tn * 4 > MM_BLOCK_BYTES:
        tn //= 2
    chunk = tn if tn <= 2048 else _pick_tn(tn)
    has_add = addend is not None

    def body(*refs):
        a_ref, b_ref = refs[0], refs[1]
        c_ref = refs[2] if has_add else None
        o_ref = refs[3] if has_add else refs[2]
        ob_ref = refs[-1] if also_bf16 else None
        av = a_ref[...].astype(BF16)
        for c in range(tn // chunk):
            cols = slice(c * chunk, (c + 1) * chunk)
            if trans_b is None:
                r = jnp.dot(av, b_ref[:, cols].astype(BF16), preferred_element_type=F32)
            else:
                r = lax.dot_general(av, b_ref[cols, :].astype(BF16), NT_DIMS, preferred_element_type=F32)
            if has_add:
                r = r + add_scale * c_ref[:, cols]
            o_ref[:, cols] = r.astype(out_dtype)
            if also_bf16:
                ob_ref[:, cols] = r.astype(BF16)

    b_spec = (pl.BlockSpec((k, tn), lambda j, i: (0, j)) if trans_b is None
              else pl.BlockSpec((tn, k), lambda j, i: (j, trans_b)))
    in_specs = [pl.BlockSpec((tm, k), lambda j, i: (i, 0)), b_spec]
    ops = [a, b]
    tile = pl.BlockSpec((tm, tn), lambda j, i: (i, j))
    if has_add:
        in_specs.append(tile)
        ops.append(addend)
    out_shape = [SDS((m, n), out_dtype)]
    out_specs = [tile]
    if also_bf16:
        out_shape.append(SDS((m, n), BF16))
        out_specs.append(tile)
    res = pl.pallas_call(
        body, name=name, grid=(n // tn, m // tm), in_specs=in_specs, out_specs=out_specs, out_shape=out_shape,
        compiler_params=_params(("parallel", "parallel")))(*ops)
    return tuple(res) if also_bf16 else res[0]


def mm_tn(a, b, name, out_dtype=BF16):
    t, m = a.shape
    _, n = b.shape
    tt = min(4 * TM, t)
    tm = 1024 if m % 1024 == 0 else m
    tn = _pick_tn(n)
    nt = t // tt
    block = (tm, tn)
    in_place = out_dtype == F32

    def body(a_ref, b_ref, o_ref, *scratch):
        acc_ref = o_ref if in_place else scratch[0]
        kk = pl.program_id(2)
        r = lax.dot_general(a_ref[...].astype(BF16), b_ref[...].astype(BF16), TN_DIMS, preferred_element_type=F32)

        @pl.when(kk == 0)
        def _():
            acc_ref[...] = r

        @pl.when(kk != 0)
        def _():
            acc_ref[...] += r

        if not in_place:
            @pl.when(kk == nt - 1)
            def _():
                o_ref[...] = acc_ref[...].astype(out_dtype)

    return pl.pallas_call(
        body, name=name, grid=(m // tm, n // tn, nt),
        in_specs=[pl.BlockSpec((tt, tm), lambda i, j, kk: (kk, i)), pl.BlockSpec((tt, tn), lambda i, j, kk: (kk, j))],
        out_specs=pl.BlockSpec(block, lambda i, j, kk: (i, j)), out_shape=SDS((m, n), out_dtype),
        scratch_shapes=[] if in_place else [pltpu.VMEM(block, F32)],
        compiler_params=_params(("parallel", "parallel", "arbitrary")))(a, b)


def ln_fwd(xprev, a, w, g, b, name):
    t, d = xprev.shape
    k = a.shape[1]
    tm = min(TM, t)

    def body(xp_ref, a_ref, w_ref, g_ref, b_ref, y_ref, yb_ref, xh_ref, rs_ref):
        r = DN_ALPHA * xp_ref[...] + jnp.dot(a_ref[...], w_ref[...], preferred_element_type=F32)
        mu = jnp.mean(r, axis=1, keepdims=True)
        xc = r - mu
        var = jnp.mean(xc * xc, axis=1, keepdims=True)
        rstd = lax.rsqrt(var + LN_EPS)
        xh = xc * rstd
        y = xh * g_ref[...] + b_ref[...]
        y_ref[...] = y
        yb_ref[...] = y.astype(BF16)
        xh_ref[...] = xh
        rs_ref[...] = jnp.broadcast_to(rstd, (tm, LANES))

    row = pl.BlockSpec((tm, d), lambda i: (i, 0))
    vec = pl.BlockSpec((1, d), lambda i: (0, 0))
    return pl.pallas_call(
        body, name=name, grid=(t // tm,),
        in_specs=[row, pl.BlockSpec((tm, k), lambda i: (i, 0)), pl.BlockSpec((k, d), lambda i: (0, 0)), vec, vec],
        out_specs=[row, row, row, pl.BlockSpec((tm, LANES), lambda i: (i, 0))],
        out_shape=[SDS((t, d), F32), SDS((t, d), BF16), SDS((t, d), F32), SDS((t, LANES), F32)],
        compiler_params=_params(("parallel",)))(xprev, a, w, g, b)


def ln_bwd(dy, xhat, rstd, g, name, products=(), dy_scale=1.0):
    t, d = dy.shape
    np_ = len(products)
    tm = min(TM if sum(a.shape[1] for a, _, _ in products) <= 4096 else TS, t)

    def body(*refs):
        prod_refs = refs[:2 * np_]
        dy_ref, xh_ref, rs_ref, g_ref, dr_ref, drb_ref, dg_ref, db_ref = refs[2 * np_:]
        i = pl.program_id(0)
        dyv = dy_ref[...] if dy_scale == 1.0 else dy_scale * dy_ref[...]
        for p in range(np_):
            a_ref, w_ref = prod_refs[2 * p], prod_refs[2 * p + 1]
            if len(w_ref.shape) == 2:
                dyv = dyv + lax.dot_general(a_ref[...], w_ref[...], NT_DIMS, preferred_element_type=F32)
            else:
                kb = w_ref.shape[2]
                for c in range(w_ref.shape[0]):
                    dyv = dyv + lax.dot_general(a_ref[:, c * kb:(c + 1) * kb], w_ref[c], NT_DIMS,
                                                preferred_element_type=F32)
        xh = xh_ref[...]
        dxh = dyv * g_ref[...]
        m1 = jnp.mean(dxh, axis=1, keepdims=True)
        m2 = jnp.mean(dxh * xh, axis=1, keepdims=True)
        dr = rs_ref[:, 0:1] * (dxh - m1 - xh * m2)
        dr_ref[...] = dr
        drb_ref[...] = dr.astype(BF16)

        @pl.when(i == 0)
        def _():
            dg_ref[...] = jnp.zeros_like(dg_ref)
            db_ref[...] = jnp.zeros_like(db_ref)

        dg_ref[...] += jnp.sum(dyv * xh, axis=0, keepdims=True)
        db_ref[...] += jnp.sum(dyv, axis=0, keepdims=True)

    row = pl.BlockSpec((tm, d), lambda i: (i, 0))
    vec = pl.BlockSpec((1, d), lambda i: (0, 0))
    in_specs = [row, row, pl.BlockSpec((tm, LANES), lambda i: (i, 0)), vec]
    ops = [dy, xhat, rstd, g]
    for a, w, col in reversed(products):
        k = a.shape[1]
        if w.ndim == 2:
            w_spec = pl.BlockSpec((d, k), lambda i, col=col: (0, col))
        else:
            w_spec = pl.BlockSpec((k // w.shape[2], d, w.shape[2]), lambda i, col=col: (col, 0, 0))
        in_specs = [pl.BlockSpec((tm, k), lambda i: (i, 0)), w_spec] + in_specs
        ops = [a, w] + ops
    return pl.pallas_call(
        body, name=name, grid=(t // tm,), in_specs=in_specs, out_specs=[row, row, vec, vec],
        out_shape=[SDS((t, d), F32), SDS((t, d), BF16), SDS((1, d), F32), SDS((1, d), F32)],
        compiler_params=_params(("arbitrary",)))(*ops)


def loss_ln_bwd(xhat, rstd, g, beta, target, name):
    t, d = xhat.shape
    tm = min(TM, t)
    nsteps = t // tm

    def body(xh_ref, rs_ref, g_ref, b_ref, t_ref, dr_ref, drb_ref, dg_ref, db_ref, l_ref, acc):
        i = pl.program_id(0)
        xh = xh_ref[...]
        diff = xh * g_ref[...] + b_ref[...] - t_ref[...]
        dyv = diff * (1.0 / d)
        dxh = dyv * g_ref[...]
        m1 = jnp.mean(dxh, axis=1, keepdims=True)
        m2 = jnp.mean(dxh * xh, axis=1, keepdims=True)
        dr = rs_ref[:, 0:1] * (dxh - m1 - xh * m2)
        dr_ref[...] = dr
        drb_ref[...] = dr.astype(BF16)

        @pl.when(i == 0)
        def _():
            dg_ref[...] = jnp.zeros_like(dg_ref)
            db_ref[...] = jnp.zeros_like(db_ref)
            acc[...] = jnp.zeros_like(acc)

        dg_ref[...] += jnp.sum(dyv * xh, axis=0, keepdims=True)
        db_ref[...] += jnp.sum(dyv, axis=0, keepdims=True)
        acc[...] += jnp.sum(diff * diff, axis=0, keepdims=True)

        @pl.when(i == nsteps - 1)
        def _():
            tot = jnp.sum(acc[...], axis=1, keepdims=True) * (0.5 / d)
            l_ref[...] = jnp.broadcast_to(tot, (1, LANES))

    row = pl.BlockSpec((tm, d), lambda i: (i, 0))
    vec = pl.BlockSpec((1, d), lambda i: (0, 0))
    return pl.pallas_call(
        body, name=name, grid=(nsteps,),
        in_specs=[row, pl.BlockSpec((tm, LANES), lambda i: (i, 0)), vec, vec, row],
        out_specs=[row, row, vec, vec, pl.BlockSpec((1, LANES), lambda i: (0, 0))],
        out_shape=[SDS((t, d), F32), SDS((t, d), BF16), SDS((1, d), F32), SDS((1, d), F32), SDS((1, LANES), F32)],
        scratch_shapes=[pltpu.VMEM((1, d), F32)],
        compiler_params=_params(("arbitrary",)))(xhat, rstd, g, beta, target)


def memattn_fwd(proj, memkv, tok, nb, s, name):
    ts = min(TM, s)
    nq = s // ts

    def body(q_ref, kv_ref, tok_ref, o_ref):
        o_ref[:, :TOK_WIDTH] = tok_ref[...]
        top = lax.broadcasted_iota(jnp.int32, (PAIR, ts), 0) < HEAD_DIM
        scores = []
        for p in range(MEM_HEADS // 2):
            qp = q_ref[:, p * PAIR:(p + 1) * PAIR].astype(BF16)
            ke, ko = _split_pair(kv_ref[:, p * PAIR:(p + 1) * PAIR], QK_SCALE)
            scores.append([lax.dot_general(km, qp, NT_DIMS, preferred_element_type=F32) for km in (ke, ko)])
        for p in range(MEM_HEADS // 2):
            vt = kv_ref[:, MEM_WIDTH + p * PAIR:MEM_WIDTH + (p + 1) * PAIR].astype(F32).T.astype(BF16)
            outs = []
            for sc in scores[p]:
                e = jnp.exp(sc - jnp.max(sc, axis=0, keepdims=True))
                pr = e / jnp.sum(e, axis=0, keepdims=True)
                outs.append(jnp.dot(vt, pr.astype(BF16), preferred_element_type=F32))
            o_ref[:, TOK_WIDTH + p * PAIR:TOK_WIDTH + (p + 1) * PAIR] = jnp.where(top, outs[0], outs[1]).T.astype(BF16)

    return pl.pallas_call(
        body, name=name, grid=(nb, nq),
        in_specs=[pl.BlockSpec((ts, MEM_WIDTH), lambda b, i: (b * nq + i, 3)),
                  pl.BlockSpec((MEM_LEN, 2 * MEM_WIDTH), lambda b, i: (b, 0)),
                  pl.BlockSpec((ts, TOK_WIDTH), lambda b, i: (b * nq + i, 0))],
        out_specs=pl.BlockSpec((ts, TOK_WIDTH + MEM_WIDTH), lambda b, i: (b * nq + i, 0)),
        out_shape=SDS((nb * s, TOK_WIDTH + MEM_WIDTH), BF16),
        compiler_params=_params(("parallel", "parallel")))(proj, memkv, tok)


def memattn_bwd(proj, memkv, dcat, dtok, nb, s, name):
    ts = min(TM, s)
    nq = s // ts

    def body(q_ref, kv_ref, do_ref, dtok_ref, dq_ref, dkv_ref):
        i = pl.program_id(1)
        dq_ref[:, :TOK_WIDTH] = dtok_ref[...]

        @pl.when(i == 0)
        def _():
            dkv_ref[...] = jnp.zeros_like(dkv_ref)

        lo = _half_masks(MEM_LEN)
        top = lax.broadcasted_iota(jnp.int32, (PAIR, ts), 0) < HEAD_DIM
        n_pairs = MEM_HEADS // 2
        qs, dos, kps, products = [], [], [], []
        for p in range(n_pairs):
            qp = q_ref[:, p * PAIR:(p + 1) * PAIR].astype(BF16)
            dop = do_ref[:, p * PAIR:(p + 1) * PAIR].astype(BF16)
            kp = kv_ref[:, p * PAIR:(p + 1) * PAIR] * QK_SCALE
            kms = _split_pair(kp)
            vms = _split_pair(kv_ref[:, MEM_WIDTH + p * PAIR:MEM_WIDTH + (p + 1) * PAIR])
            products.append([(lax.dot_general(km, qp, NT_DIMS, preferred_element_type=F32),
                              lax.dot_general(vm, dop, NT_DIMS, preferred_element_type=F32)) for km, vm in zip(kms, vms)])
            qs.append(qp)
            dos.append(dop)
            kps.append(kp)
        for p in range(n_pairs):
            kt = kps[p].astype(F32).T.astype(BF16)
            dks, dvs, dqs = [], [], []
            for sc, dp in products[p]:
                e = jnp.exp(sc - jnp.max(sc, axis=0, keepdims=True))
                pr = e / jnp.sum(e, axis=0, keepdims=True)
                dl = jnp.sum(pr * dp, axis=0, keepdims=True)
                ds = (pr * (dp - dl)).astype(BF16)
                dvs.append(jnp.dot(pr.astype(BF16), dos[p], preferred_element_type=F32))
                dks.append(jnp.dot(ds, qs[p], preferred_element_type=F32))
                dqs.append(jnp.dot(kt, ds, preferred_element_type=F32))
            dq_ref[:, TOK_WIDTH + p * PAIR:TOK_WIDTH + (p + 1) * PAIR] = jnp.where(top, dqs[0], dqs[1]).T.astype(BF16)
            dkv_ref[:, p * PAIR:(p + 1) * PAIR] += jnp.where(lo, dks[0], dks[1]) * QK_SCALE
            dkv_ref[:, MEM_WIDTH + p * PAIR:MEM_WIDTH + (p + 1) * PAIR] += jnp.where(lo, dvs[0], dvs[1])

    return pl.pallas_call(
        body, name=name, grid=(nb, nq),
        in_specs=[pl.BlockSpec((ts, MEM_WIDTH), lambda b, i: (b * nq + i, 3)),
                  pl.BlockSpec((MEM_LEN, 2 * MEM_WIDTH), lambda b, i: (b, 0)),
                  pl.BlockSpec((ts, MEM_WIDTH), lambda b, i: (b * nq + i, 3)),
                  pl.BlockSpec((ts, TOK_WIDTH), lambda b, i: (b * nq + i, 0))],
        out_specs=[pl.BlockSpec((ts, TOK_WIDTH + MEM_WIDTH), lambda b, i: (b * nq + i, 0)),
                   pl.BlockSpec((MEM_LEN, 2 * MEM_WIDTH), lambda b, i: (b, 0))],
        out_shape=[SDS((nb * s, TOK_WIDTH + MEM_WIDTH), BF16), SDS((nb * MEM_LEN, 2 * MEM_WIDTH), F32)],
        compiler_params=_params(("parallel", "arbitrary")))(proj, memkv, dcat, dtok)


def _pool_select(shape, s2, s4, s8, s16):
    lane = lax.broadcasted_iota(jnp.int32, shape, 1)
    return jnp.where(lane < POOL_GROUP, s2, jnp.where(lane < 2 * POOL_GROUP, s4, jnp.where(lane < 3 * POOL_GROUP, s8, s16)))


def _pool_count(shape, first_pos):
    pos = first_pos + lax.broadcasted_iota(jnp.int32, shape, 0)
    win = _pool_select(shape, 2, 4, 8, 16)
    return jnp.minimum(pos + 1, win).astype(F32)


def pool_fwd(proj, pw_bd, pscale, nb, s):
    ts = min(TS, s)
    nq = s // ts
    w = TOK_WIDTH

    def body(c_ref, h_ref, w_ref, sc_ref, pooled_ref, tok_ref):
        i = pl.program_id(0) % nq
        cur = c_ref[...]
        halo = jnp.where(i == 0, 0.0, h_ref[...])
        xe = jnp.concatenate([halo, cur], axis=0)
        s2 = xe + pltpu.roll(xe, 1, axis=0)
        s4 = s2 + pltpu.roll(s2, 2, axis=0)
        s8 = s4 + pltpu.roll(s4, 4, axis=0)
        s16 = s8 + pltpu.roll(s8, 8, axis=0)
        hp = HALO_POOL
        ws = _pool_select((ts, w), s2[hp:], s4[hp:], s8[hp:], s16[hp:])
        pooled = (ws / _pool_count((ts, w), i * ts) - cur).astype(BF16)
        pooled_ref[...] = pooled
        mixed = jnp.dot(pooled, w_ref[...], preferred_element_type=F32)
        tok_ref[...] = (mixed * sc_ref[...]).astype(BF16)

    row = pl.BlockSpec((ts, w), lambda r: (r, 0))
    return pl.pallas_call(
        body, name="pool_fwd", grid=(nb * nq,),
        in_specs=[row, pl.BlockSpec((HALO_POOL, w), lambda r: (jnp.maximum(r * (ts // HALO_POOL) - 1, 0), 0)),
                  pl.BlockSpec((w, w), lambda r: (0, 0)), pl.BlockSpec((1, w), lambda r: (0, 0))],
        out_specs=[row, row], out_shape=[SDS((nb * s, w), BF16), SDS((nb * s, w), BF16)],
        compiler_params=_params(("parallel",)))(proj, proj, pw_bd, pscale)


def pool_bwd_mix(dcat, pooled, pw_bd, pscale, nb, s):
    ts = min(TS, s)
    w = TOK_WIDTH

    def body(dt_ref, p_ref, w_ref, sc_ref, dm_ref, dp_ref, ds_ref):
        r = pl.program_id(0)
        dtok = dt_ref[...]
        mixed = jnp.dot(p_ref[...], w_ref[...], preferred_element_type=F32)

        @pl.when(r == 0)
        def _():
            ds_ref[...] = jnp.zeros_like(ds_ref)

        ds_ref[...] += jnp.sum(dtok * mixed, axis=0, keepdims=True)
        dmx = (dtok * sc_ref[...]).astype(BF16)
        dm_ref[...] = dmx
        dp_ref[...] = lax.dot_general(dmx, w_ref[...], NT_DIMS, preferred_element_type=F32)

    row = pl.BlockSpec((ts, w), lambda r: (r, 0))
    mat = pl.BlockSpec((w, w), lambda r: (0, 0))
    vec = pl.BlockSpec((1, w), lambda r: (0, 0))
    return pl.pallas_call(
        body, name="pool_bwd_mix", grid=(nb * s // ts,), in_specs=[row, row, mat, vec],
        out_specs=[row, row, vec], out_shape=[SDS((nb * s, w), BF16), SDS((nb * s, w), F32), SDS((1, w), F32)],
        compiler_params=_params(("arbitrary",)))(dcat, pooled, pw_bd, pscale)


def pool_bwd_window(dpooled, nb, s):
    ts = min(TS, s)
    nq = s // ts
    w = TOK_WIDTH
    n_ext = ts + HALO_POOL
    n_halo_blocks = nb * s // HALO_POOL

    def body(c_ref, n_ref, du_ref):
        i = pl.program_id(0) % nq
        cur = c_ref[...]
        nxt = jnp.where(i == nq - 1, 0.0, n_ref[...])
        ze = jnp.concatenate([cur, nxt], axis=0) / _pool_count((n_ext, w), i * ts)
        s2 = ze + pltpu.roll(ze, n_ext - 1, axis=0)
        s4 = s2 + pltpu.roll(s2, n_ext - 2, axis=0)
        s8 = s4 + pltpu.roll(s4, n_ext - 4, axis=0)
        s16 = s8 + pltpu.roll(s8, n_ext - 8, axis=0)
        ws = _pool_select((ts, w), s2[:ts], s4[:ts], s8[:ts], s16[:ts])
        du_ref[...] = (ws - cur).astype(BF16)

    row = pl.BlockSpec((ts, w), lambda r: (r, 0))
    return pl.pallas_call(
        body, name="pool_bwd_window", grid=(nb * nq,),
        in_specs=[row, pl.BlockSpec((HALO_POOL, w),
                                    lambda r: (jnp.minimum((r + 1) * (ts // HALO_POOL), n_halo_blocks - 1), 0))],
        out_specs=row, out_shape=SDS((nb * s, w), BF16),
        compiler_params=_params(("parallel",)))(dpooled, dpooled)


def _conv_rows(xe, w_ref):
    return (w_ref[0, 2:3, :] * xe + w_ref[0, 1:2, :] * pltpu.roll(xe, 1, axis=0)
            + w_ref[0, 0:1, :] * pltpu.roll(xe, 2, axis=0) + w_ref[0, 3:4, :])


def ffn_up_gate(x_bf, wup, cw, nb, s, name):
    tm = min(2 * TM, s)
    nq = s // tm
    w = FF_BLOCK_PAD
    hr = 2 * HALO_CONV
    k = x_bf.shape[1]

    def body(xc_ref, xh_ref, wu_ref, wg_ref, cu_ref, cg_ref, act_ref, a_ref, b_ref, hu_ref, hg_ref):
        first = (pl.program_id(1) % nq) == 0
        xc = xc_ref[...]
        xh = xh_ref[...]

        def products(w_ref):
            return (jnp.dot(xc, w_ref[0], preferred_element_type=F32), jnp.dot(xh, w_ref[0], preferred_element_type=F32))

        def conv(hcur, hprev, c_ref, h_out):
            h_out[...] = hcur.astype(BF16)
            xe = jnp.concatenate([jnp.where(first, 0.0, hprev), hcur], axis=0)
            return _conv_rows(xe, c_ref)[hr:]

        pu, pg = products(wu_ref), products(wg_ref)
        cu = conv(*pu, cu_ref, hu_ref)
        cg = conv(*pg, cg_ref, hg_ref)
        sg = _sigmoid(cg)
        a = cg * sg
        act_ref[...] = (a * cu).astype(BF16)
        a_ref[...] = a.astype(BF16)
        b_ref[...] = (cu * (sg * (1.0 + cg * (1.0 - sg)))).astype(BF16)

    def wblock(off):
        return pl.BlockSpec((1, k, w), lambda j, r: (j + off, 0, 0))

    def cblock(off):
        return pl.BlockSpec((1, 8, w), lambda j, r: (j + off, 0, 0))

    tile = pl.BlockSpec((tm, w), lambda j, r: (r, j))
    out = SDS((nb * s, FF_PAIRS * w), BF16)
    return pl.pallas_call(
        body, name=name, grid=(FF_PAIRS, nb * nq),
        in_specs=[pl.BlockSpec((tm, k), lambda j, r: (r, 0)),
                  pl.BlockSpec((hr, k), lambda j, r: (jnp.maximum(r * (tm // hr) - 1, 0), 0)),
                  wblock(0), wblock(FF_PAIRS), cblock(0), cblock(FF_PAIRS)],
        out_specs=[tile] * 5, out_shape=[out] * 5,
        compiler_params=_params(("parallel", "parallel")))(x_bf, x_bf, wup, wup, cw, cw)


def gate_conv_bwd(dact, a, b, hu, hg, cw, nb, s, name):
    ts = min(TS, s)
    nq = s // ts
    w = FF_BLOCK_PAD
    hc = HALO_CONV
    hb = 2 * hc
    n_ext = ts + hc

    def body(dc_ref, dn_ref, ac_ref, an_ref, bc_ref, bn_ref, hu_ref, hg_ref, wu_ref, wg_ref,
             dhu_ref, dhg_ref, dwu_ref, dwg_ref):
        r = pl.program_id(1)
        last = (r % nq) == nq - 1

        def ext(c_ref, n_ref, mask_next=False):
            nxt = n_ref[...].astype(F32)[:hc]
            if mask_next:
                nxt = jnp.where(last, 0.0, nxt)
            return jnp.concatenate([c_ref[...].astype(F32), nxt], axis=0)

        da = ext(dc_ref, dn_ref, mask_next=True)

        def branch(dcv, w_ref, h_ref, dh_ref, dw_ref):
            d0 = dcv[:ts]
            d1 = pltpu.roll(dcv, n_ext - 1, axis=0)[:ts]
            d2 = pltpu.roll(dcv, n_ext - 2, axis=0)[:ts]
            dh_ref[...] = (w_ref[0, 2:3, :] * d0 + w_ref[0, 1:2, :] * d1 + w_ref[0, 0:1, :] * d2).astype(BF16)
            hv = h_ref[...].astype(F32)
            rows = [jnp.sum(d2 * hv, axis=0, keepdims=True), jnp.sum(d1 * hv, axis=0, keepdims=True),
                    jnp.sum(d0 * hv, axis=0, keepdims=True), jnp.sum(d0, axis=0, keepdims=True)]
            sub = lax.broadcasted_iota(jnp.int32, (8, w), 0)
            upd = jnp.zeros((8, w), F32)
            for kk, rv in enumerate(rows):
                upd = jnp.where(sub == kk, rv, upd)

            @pl.when(r == 0)
            def _():
                dw_ref[...] = jnp.zeros_like(dw_ref)

            dw_ref[...] += upd[None]

        branch(da * ext(ac_ref, an_ref), wu_ref, hu_ref, dhu_ref, dwu_ref)
        branch(da * ext(bc_ref, bn_ref), wg_ref, hg_ref, dhg_ref, dwg_ref)

    cur = pl.BlockSpec((ts, w), lambda j, r: (r, j))
    nxt = pl.BlockSpec((hb, w), lambda j, r: (jnp.minimum((r + 1) * (ts // hb), nb * s // hb - 1), j))

    def wspec(off):
        return pl.BlockSpec((1, 8, w), lambda j, r: (j + off, 0, 0))

    p = FF_PAIRS
    dw_spec = pl.BlockSpec((1, 8, w), lambda j, r: (j, 0, 0))
    return pl.pallas_call(
        body, name=name, grid=(p, nb * nq),
        in_specs=[cur, nxt, cur, nxt, cur, nxt, cur, cur, wspec(0), wspec(p)],
        out_specs=[cur, cur, dw_spec, dw_spec],
        out_shape=[SDS((nb * s, p * w), BF16), SDS((nb * s, p * w), BF16), SDS((p, 8, w), F32), SDS((p, 8, w), F32)],
        compiler_params=_params(("parallel", "arbitrary")))(dact, dact, a, a, b, b, hu, hg, cw, cw)


def _tri(n, upper):
    r = lax.broadcasted_iota(jnp.int32, (n, n), 0)
    c = lax.broadcasted_iota(jnp.int32, (n, n), 1)
    return ((r <= c) if upper else (r >= c)).astype(F32)


def fgate_fwd(fl, fb, nb, s):
    tc = min(TC, s)
    nq = s // tc

    def body(fl_ref, fb_ref, f_ref, carry):
        @pl.when(pl.program_id(1) == 0)
        def _():
            carry[...] = jnp.zeros_like(carry)

        z = fl_ref[...] + fb_ref[...]
        logf = jnp.minimum(z, 0.0) - jnp.log(1.0 + jnp.exp(-jnp.abs(z)))
        f_ref[...] = jnp.dot(_tri(tc, False), logf, preferred_element_type=F32,
                             precision=lax.Precision.HIGHEST) + carry[...]
        carry[...] += jnp.sum(logf, axis=0, keepdims=True)

    row = pl.BlockSpec((tc, LANES), lambda b, i: (b * nq + i, 0))
    return pl.pallas_call(
        body, name="fgate_fwd", grid=(nb, nq), in_specs=[row, pl.BlockSpec((1, LANES), lambda b, i: (0, 0))],
        out_specs=row, out_shape=SDS((nb * s, LANES), F32), scratch_shapes=[pltpu.VMEM((1, LANES), F32)],
        compiler_params=_params(("arbitrary", "arbitrary")))(fl, fb)


def fgate_bwd(d_cum_q, d_cum_k, fl, fb, dk, dv, nb, s):
    tc = min(TC, s)
    nq = s // tc

    def body(dfq_ref, dfk_ref, fl_ref, fb_ref, dk_ref, dv_ref, dkvf_ref, dfb_ref, carry):
        b = pl.program_id(0)
        i = pl.program_id(1)

        @pl.when(i == 0)
        def _():
            carry[...] = jnp.zeros_like(carry)

        @pl.when(jnp.logical_and(b == 0, i == 0))
        def _():
            dfb_ref[...] = jnp.zeros_like(dfb_ref)

        dfv = dfq_ref[...] + dfk_ref[...]
        dlog = jnp.dot(_tri(tc, True), dfv, preferred_element_type=F32,
                       precision=lax.Precision.HIGHEST) + carry[...]
        carry[...] += jnp.sum(dfv, axis=0, keepdims=True)
        z = fl_ref[...] + fb_ref[...]
        dfl = dlog / (1.0 + jnp.exp(z))
        dkvf_ref[:, :TOK_WIDTH] = dk_ref[...]
        dkvf_ref[:, TOK_WIDTH:2 * TOK_WIDTH] = dv_ref[...]
        dkvf_ref[:, 2 * TOK_WIDTH:] = dfl.astype(BF16)
        dfb_ref[...] += jnp.sum(dfl, axis=0, keepdims=True)

    def rows(width):
        return pl.BlockSpec((tc, width), lambda b, i: (b * nq + nq - 1 - i, 0))

    row = rows(LANES)
    vec = pl.BlockSpec((1, LANES), lambda b, i: (0, 0))
    return pl.pallas_call(
        body, name="fgate_bwd", grid=(nb, nq), in_specs=[row, row, row, vec, rows(TOK_WIDTH), rows(TOK_WIDTH)],
        out_specs=[rows(KV_COLS_PAD), vec],
        out_shape=[SDS((nb * s, KV_COLS_PAD), BF16), SDS((1, LANES), F32)], scratch_shapes=[pltpu.VMEM((1, LANES), F32)],
        compiler_params=_params(("arbitrary", "arbitrary")))(d_cum_q, d_cum_k, fl, fb, dk, dv)


PAIR = 2 * HEAD_DIM
N_PAIRS = FOX_HEADS // 2


def _lane_put(shape, h, col):
    lane = lax.broadcasted_iota(jnp.int32, shape, 1)
    return jnp.where(lane == h, col, 0.0)


def _half_masks(rows):
    lane = lax.broadcasted_iota(jnp.int32, (rows, PAIR), 1)
    return lane < HEAD_DIM


def _split_pair(x, scale=None):
    if scale is not None:
        x = x * scale
    lo = _half_masks(x.shape[0])
    zero = jnp.zeros_like(x)
    return jnp.where(lo, x, zero), jnp.where(lo, zero, x)


def _to_tile_rows(a, nb, s, tf):
    return a.reshape(nb * s // tf, tf, LANES)[:, :, :16].transpose(0, 2, 1)


def _from_tile_rows(a):
    tiles, _, tf = a.shape
    return jnp.pad(a.transpose(0, 2, 1), ((0, 0), (0, 0), (0, LANES - 16))).reshape(tiles * tf, LANES)


BIAS_TERMS = 3
LOOKAHEAD = 4
FOLLOW_FWD = 1
LOOKAHEAD_BWD = 2
FOLLOW_BWD = 1


def _bias_lane(h):
    return HEAD_DIM if h % 2 == 0 else 0


def _placement():
    rows = jnp.arange(LANES)[:, None]
    cols = jnp.arange(FOX_HEADS * PAIR)[None, :]
    head, lane = cols // PAIR, cols % PAIR
    first = jnp.where(head % 2 == 0, HEAD_DIM, 0)
    term = lane - first
    hit = (term >= 0) & (term < BIAS_TERMS) & (rows == 16 * term + head)
    return hit.astype(BF16)


def fox_prep(kv, fneg, nb, s):
    tf = min(TF, s)
    w = TOK_WIDTH

    def body(k_ref, v_ref, f_ref, pl_ref, ka_ref, vt_ref):
        lane = lax.broadcasted_iota(jnp.int32, (tf, LANES), 1)
        lo = lane < HEAD_DIM
        f = jnp.where(lane < FOX_HEADS, f_ref[...], 0.0)
        hi = f.astype(BF16).astype(F32)
        mid = (f - hi).astype(BF16).astype(F32)
        low = (f - hi - mid).astype(BF16).astype(F32)
        terms = (hi + pltpu.roll(mid, 16, axis=1) + pltpu.roll(low, 32, axis=1)).astype(BF16)
        placed = jnp.dot(terms, pl_ref[...], preferred_element_type=F32).astype(BF16)
        one = jnp.ones((tf, LANES), BF16)
        zero = jnp.zeros((tf, LANES), BF16)
        for p in range(N_PAIRS):
            kp = k_ref[:, p * PAIR:(p + 1) * PAIR] * QK_SCALE
            vp = v_ref[:, p * PAIR:(p + 1) * PAIR]
            he, ho = 2 * p, 2 * p + 1
            ka_ref[:, he * PAIR:(he + 1) * PAIR] = jnp.where(lo, kp, placed[:, he * PAIR:(he + 1) * PAIR])
            ka_ref[:, ho * PAIR:(ho + 1) * PAIR] = jnp.where(lo, placed[:, ho * PAIR:(ho + 1) * PAIR], kp)
            ve = jnp.where(lo, vp, jnp.where(lane == HEAD_DIM, one, zero))
            vo = jnp.where(lo, jnp.where(lane == 0, one, zero), vp)
            vt_ref[0, he * PAIR:(he + 1) * PAIR, :] = ve.astype(F32).T.astype(BF16)
            vt_ref[0, ho * PAIR:(ho + 1) * PAIR, :] = vo.astype(F32).T.astype(BF16)

    return pl.pallas_call(
        body, name="fox_prep", grid=(nb * s // tf,),
        in_specs=[pl.BlockSpec((tf, w), lambda r: (r, 0)), pl.BlockSpec((tf, w), lambda r: (r, 1)),
                  pl.BlockSpec((tf, LANES), lambda r: (r, 0)), pl.BlockSpec((LANES, FOX_HEADS * PAIR), lambda r: (0, 0))],
        out_specs=[pl.BlockSpec((tf, FOX_HEADS * PAIR), lambda r: (r, 0)),
                   pl.BlockSpec((1, FOX_HEADS * PAIR, tf), lambda r: (r, 0, 0))],
        out_shape=[SDS((nb * s, FOX_HEADS * PAIR), BF16), SDS((nb * s // tf, FOX_HEADS * PAIR, tf), BF16)],
        compiler_params=_params(("parallel",)))(kv, kv, fneg, _placement())


def fox_fwd_t(pq, kaug, vaug_t, nb, s):
    tf = min(TF, s)
    n = s // tf
    w = TOK_WIDTH
    wa = FOX_HEADS * PAIR

    def body(q_ref, k_hbm, vt_hbm, ob_ref, of_ref, lse_ref, k_vm, vt_vm, qx_scr, m_scr, acc_scr, sems):
        b = pl.program_id(0)
        i = pl.program_id(1)

        @pl.when(i == 0)
        def _():
            ck = pltpu.make_async_copy(k_hbm.at[pl.ds(pl.multiple_of(b * s, tf), s)], k_vm, sems.at[0])
            cv = pltpu.make_async_copy(vt_hbm.at[pl.ds(b * n, n)], vt_vm, sems.at[1])
            ck.start()
            cv.start()
            ck.wait()
            cv.wait()

        lane = lax.broadcasted_iota(jnp.int32, (tf, PAIR), 1)
        one = jnp.ones((tf, PAIR), BF16)
        zero = jnp.zeros((tf, PAIR), BF16)
        for p in range(N_PAIRS):
            qp = q_ref[:, p * PAIR:(p + 1) * PAIR]
            be, bo = _bias_lane(2 * p), _bias_lane(2 * p + 1)
            ones_e = jnp.where((lane >= be) & (lane < be + BIAS_TERMS), one, zero)
            ones_o = jnp.where((lane >= bo) & (lane < bo + BIAS_TERMS), one, zero)
            qx_scr[2 * p] = jnp.where(lane < HEAD_DIM, qp, ones_e)
            qx_scr[2 * p + 1] = jnp.where(lane < HEAD_DIM, ones_o, qp)
        m_scr[...] = jnp.full(m_scr.shape, NEG_BIG, F32)
        acc_scr[...] = jnp.zeros_like(acc_scr)

        def tile(j, masked):
            ks = pl.multiple_of(j * tf, tf)
            if masked:
                keep = lax.broadcasted_iota(jnp.int32, (tf, tf), 1) >= lax.broadcasted_iota(jnp.int32, (tf, tf), 0)
            def scores(h):
                kx = k_vm[pl.ds(ks, tf), h * PAIR:(h + 1) * PAIR]
                return lax.dot_general(kx, qx_scr[h], NT_DIMS, preferred_element_type=F32)

            def values(h, pr, a):
                pv = jnp.dot(vt_vm[j, h * PAIR:(h + 1) * PAIR, :], pr, preferred_element_type=F32)
                acc_scr[h] = a * acc_scr[h] + pv

            ahead = [scores(h) for h in range(LOOKAHEAD)]
            behind = []
            for h in range(FOX_HEADS):
                sc = ahead.pop(0)
                if h + LOOKAHEAD < FOX_HEADS:
                    ahead.append(scores(h + LOOKAHEAD))
                if masked:
                    sc = jnp.where(keep, sc, NEG_BIG)
                m_prev = m_scr[h]
                m_new = jnp.maximum(m_prev, jnp.max(sc, axis=0, keepdims=True))
                m_scr[h] = m_new
                behind.append((h, jnp.exp(sc - m_new).astype(BF16), jnp.exp(m_prev - m_new)))
                if len(behind) > FOLLOW_FWD:
                    values(*behind.pop(0))
            for item in behind:
                values(*item)

        def step(j, carry):
            tile(j, False)
            return carry

        lax.fori_loop(0, i, step, 0)
        tile(i, True)

        top = lax.broadcasted_iota(jnp.int32, (PAIR, tf), 0) < HEAD_DIM
        sub = lax.broadcasted_iota(jnp.int32, (16, tf), 0)
        lse = jnp.zeros((16, tf), F32)
        for p in range(N_PAIRS):
            he, ho = 2 * p, 2 * p + 1
            le = acc_scr[he, HEAD_DIM:HEAD_DIM + 1, :]
            lod = acc_scr[ho, 0:1, :]
            o = jnp.where(top, acc_scr[he] / le, acc_scr[ho] / lod).T
            ob_ref[:, p * PAIR:(p + 1) * PAIR] = o.astype(BF16)
            of_ref[:, p * PAIR:(p + 1) * PAIR] = o
            lse = jnp.where(sub == he, m_scr[he] + jnp.log(le), lse)
            lse = jnp.where(sub == ho, m_scr[ho] + jnp.log(lod), lse)
        lse_ref[0] = lse

    qrow = lambda b, i: (b * n + i, 0)
    return pl.pallas_call(
        body, name="fox_fwd", grid=(nb, n),
        in_specs=[pl.BlockSpec((tf, w), qrow), ANY_SPEC, ANY_SPEC],
        out_specs=[pl.BlockSpec((tf, w), qrow), pl.BlockSpec((tf, w), qrow),
                   pl.BlockSpec((1, 16, tf), lambda b, i: (b * n + i, 0, 0))],
        out_shape=[SDS((nb * s, w), BF16), SDS((nb * s, w), F32), SDS((nb * n, 16, tf), F32)],
        scratch_shapes=[pltpu.VMEM((s, wa), BF16), pltpu.VMEM((n, wa, tf), BF16),
                        pltpu.VMEM((FOX_HEADS, tf, PAIR), BF16), pltpu.VMEM((FOX_HEADS, 1, tf), F32),
                        pltpu.VMEM((FOX_HEADS, PAIR, tf), F32), pltpu.SemaphoreType.DMA((2,))],
        compiler_params=_params(("arbitrary", "arbitrary")))(pq, kaug, vaug_t)


def fox_delta(dcat, o, nb, s):
    tf = min(TM, s)
    w = TOK_WIDTH

    def body(do_ref, o_ref, dl_ref):
        out = jnp.zeros((tf, LANES), F32)
        for h in range(FOX_HEADS):
            lo, hi = h * HEAD_DIM, (h + 1) * HEAD_DIM
            out = out + _lane_put((tf, LANES), h, jnp.sum(do_ref[:, lo:hi] * o_ref[:, lo:hi], axis=1, keepdims=True))
        dl_ref[...] = out

    row = pl.BlockSpec((tf, w), lambda r: (r, 0))
    return pl.pallas_call(
        body, name="fox_delta", grid=(nb * s // tf,), in_specs=[row, row],
        out_specs=pl.BlockSpec((tf, LANES), lambda r: (r, 0)), out_shape=SDS((nb * s, LANES), F32),
        compiler_params=_params(("parallel",)))(dcat, o)


def fox_bwd(pq, kv, fneg, dcat_bf, lse_rows, delta_rows, nb, s):
    tf = min(TF, s)
    n = s // tf
    w = TOK_WIDTH

    def body(q_hbm, k_ref, v_ref, f_ref, do_hbm, lse_ref, dl_ref, dq_ref, dk_ref, dv_ref, dfk_ref, dfq_ref,
             q_vm, do_vm, km_scr, vm_scr, kt_scr, fk_scr, dk_scr, dv_scr, rs_scr, dq_scr, fq_scr, sems):
        b = pl.program_id(0)
        j = pl.program_id(1)

        @pl.when(j == 0)
        def _():
            rows = pl.ds(pl.multiple_of(b * s, tf), s)
            cq = pltpu.make_async_copy(q_hbm.at[rows, pl.ds(0, w)], q_vm, sems.at[0])
            cd = pltpu.make_async_copy(do_hbm.at[rows, pl.ds(0, w)], do_vm, sems.at[1])
            cq.start()
            cd.start()
            dq_scr[...] = jnp.zeros_like(dq_scr)
            fq_scr[...] = jnp.zeros_like(fq_scr)
            cq.wait()
            cd.wait()

        for p in range(N_PAIRS):
            kp = k_ref[:, p * PAIR:(p + 1) * PAIR] * QK_SCALE
            ke, ko = _split_pair(kp)
            km_scr[2 * p] = ke
            km_scr[2 * p + 1] = ko
            kt_scr[p] = kp.astype(F32).T.astype(BF16)
            ve, vo = _split_pair(v_ref[:, p * PAIR:(p + 1) * PAIR])
            vm_scr[2 * p] = ve
            vm_scr[2 * p + 1] = vo
        for h in range(FOX_HEADS):
            fk_scr[h] = jnp.broadcast_to(f_ref[:, h:h + 1], (tf, tf))
        dk_scr[...] = jnp.zeros_like(dk_scr)
        dv_scr[...] = jnp.zeros_like(dv_scr)
        rs_scr[...] = jnp.zeros_like(rs_scr)

        def tile(i, masked):
            qs = pl.multiple_of(i * tf, tf)
            if masked:
                keep = lax.broadcasted_iota(jnp.int32, (tf, tf), 1) >= lax.broadcasted_iota(jnp.int32, (tf, tf), 0)
            def products(h):
                qp = q_vm[pl.ds(qs, tf), (h // 2) * PAIR:(h // 2 + 1) * PAIR]
                dop = do_vm[pl.ds(qs, tf), (h // 2) * PAIR:(h // 2 + 1) * PAIR]
                return (lax.dot_general(km_scr[h], qp, NT_DIMS, preferred_element_type=F32),
                        lax.dot_general(vm_scr[h], dop, NT_DIMS, preferred_element_type=F32))

            def dependents(h, prb, dsb):
                p = h // 2
                half = slice((h % 2) * HEAD_DIM, (h % 2 + 1) * HEAD_DIM)
                qp = q_vm[pl.ds(qs, tf), p * PAIR:(p + 1) * PAIR]
                dop = do_vm[pl.ds(qs, tf), p * PAIR:(p + 1) * PAIR]
                dv_scr[h] += jnp.dot(prb, dop, preferred_element_type=F32)
                dk_scr[h] += jnp.dot(dsb, qp, preferred_element_type=F32)
                dqt = jnp.dot(kt_scr[p], dsb, preferred_element_type=F32)
                dq_scr[i, p, half, :] += dqt[(h % 2) * HEAD_DIM:(h % 2 + 1) * HEAD_DIM]

            ahead = [products(h) for h in range(LOOKAHEAD_BWD)]
            behind = []
            for h in range(FOX_HEADS):
                sc, dp = ahead.pop(0)
                if h + LOOKAHEAD_BWD < FOX_HEADS:
                    ahead.append(products(h + LOOKAHEAD_BWD))
                sc = sc + fk_scr[h] - lse_ref[i, h:h + 1, :]
                if masked:
                    sc = jnp.where(keep, sc, NEG_BIG)
                pr = jnp.exp(sc)
                ds = pr * (dp - dl_ref[i, h:h + 1, :])
                part = ds[:, :LANES]
                for c in range(1, tf // LANES):
                    part = part + ds[:, c * LANES:(c + 1) * LANES]
                rs_scr[h] += part
                fq_scr[i, h:h + 1, :] += jnp.sum(ds, axis=0, keepdims=True)
                behind.append((h, pr.astype(BF16), ds.astype(BF16)))
                if len(behind) > FOLLOW_BWD:
                    dependents(*behind.pop(0))
            for item in behind:
                dependents(*item)

        def step(i, carry):
            tile(i, False)
            return carry

        tile(j, True)
        for p in range(N_PAIRS):
            dq_ref[:, p * PAIR:(p + 1) * PAIR] = dq_scr[j, p].T.astype(BF16)
        dfq_ref[0] = fq_scr[j]
        lax.fori_loop(j + 1, n, step, 0)

        lo = _half_masks(tf)
        dfk = jnp.zeros((tf, LANES), F32)
        for p in range(N_PAIRS):
            dk = jnp.where(lo, dk_scr[2 * p], dk_scr[2 * p + 1]) * QK_SCALE
            dk_ref[:, p * PAIR:(p + 1) * PAIR] = dk.astype(BF16)
            dv_ref[:, p * PAIR:(p + 1) * PAIR] = jnp.where(lo, dv_scr[2 * p], dv_scr[2 * p + 1]).astype(BF16)
            for h in (2 * p, 2 * p + 1):
                dfk = dfk - _lane_put((tf, LANES), h, jnp.sum(rs_scr[h], axis=1, keepdims=True))
        dfk_ref[...] = dfk

    krow = lambda b, j: (b * n + j, 0)
    rows = pl.BlockSpec((n, 16, tf), lambda b, j: (b, 0, 0))
    tile_out = pl.BlockSpec((tf, w), krow)
    return pl.pallas_call(
        body, name="fox_bwd", grid=(nb, n),
        in_specs=[ANY_SPEC, pl.BlockSpec((tf, w), krow), pl.BlockSpec((tf, w), lambda b, j: (b * n + j, 1)),
                  pl.BlockSpec((tf, LANES), krow), ANY_SPEC, rows, rows],
        out_specs=[tile_out, tile_out, tile_out, pl.BlockSpec((tf, LANES), krow),
                   pl.BlockSpec((1, 16, tf), lambda b, j: (b * n + j, 0, 0))],
        out_shape=[SDS((nb * s, w), BF16), SDS((nb * s, w), BF16), SDS((nb * s, w), BF16), SDS((nb * s, LANES), F32),
                   SDS((nb * n, 16, tf), F32)],
        scratch_shapes=[pltpu.VMEM((s, w), BF16), pltpu.VMEM((s, w), BF16),
                        pltpu.VMEM((FOX_HEADS, tf, PAIR), BF16), pltpu.VMEM((FOX_HEADS, tf, PAIR), BF16),
                        pltpu.VMEM((N_PAIRS, PAIR, tf), BF16), pltpu.VMEM((FOX_HEADS, tf, tf), F32),
                        pltpu.VMEM((FOX_HEADS, tf, PAIR), F32), pltpu.VMEM((FOX_HEADS, tf, PAIR), F32),
                        pltpu.VMEM((FOX_HEADS, tf, LANES), F32), pltpu.VMEM((n, N_PAIRS, PAIR, tf), F32),
                        pltpu.VMEM((n, 16, tf), F32), pltpu.SemaphoreType.DMA((2,))],
        compiler_params=_params(("arbitrary", "arbitrary")))(pq, kv, kv, fneg, dcat_bf, lse_rows, delta_rows)


ADAMW_TILE_ELEMS = 128 * 1024


def reduce_adamw(parts, w, m, v, name):
    layers, r, c = w.shape
    tr, tc = r, c
    for cand in range(16, r, 16):
        if r % cand == 0 and cand * c <= ADAMW_TILE_ELEMS:
            tr = cand
    if tr < min(r, 64) and c % LANES == 0:
        tr = r
        tc = max(cand for cand in range(LANES, c + 1, LANES)
                 if c % cand == 0 and (r * cand <= ADAMW_TILE_ELEMS or cand == LANES))
    c1 = 1.0 - ADAM_B1 ** ADAM_STEP
    c2 = 1.0 - ADAM_B2 ** ADAM_STEP

    def body(*refs):
        p_refs = refs[:layers]
        w_ref, m_ref, v_ref, g_out, d_out, m_out, v_out = refs[layers:]

        def update(p_ref):
            g = p_ref[0].astype(F32)
            for k in range(1, N_DEV):
                g = g + p_ref[k].astype(F32)
            mn = ADAM_B1 * m_ref[0] + (1.0 - ADAM_B1) * g
            vn = ADAM_B2 * v_ref[0] + (1.0 - ADAM_B2) * (g * g)
            g_out[0] = g
            m_out[0] = mn
            v_out[0] = vn
            d_out[0] = -ADAM_LR * ((mn / c1) / (jnp.sqrt(vn / c2) + ADAM_EPS) + ADAM_WD * w_ref[0])

        if layers == 1:
            update(p_refs[0])
        else:
            for layer in range(layers):
                pl.when(pl.program_id(0) == layer)(lambda layer=layer: update(p_refs[layer]))

    row = pl.BlockSpec((1, tr, tc), lambda l, i, j: (l, i, j))
    return pl.pallas_call(
        body, name=name, grid=(layers, r // tr, c // tc),
        in_specs=[pl.BlockSpec((N_DEV, tr, tc), lambda l, i, j: (0, i, j))] * layers + [row, row, row],
        out_specs=[row, row, row, row], out_shape=[SDS((layers, r, c), F32)] * 4,
        compiler_params=_params(("parallel", "parallel", "parallel")))(*parts, w, m, v)


N_PEERS = N_DEV - 1
HBM_SPEC = pl.BlockSpec(memory_space=pltpu.HBM)
SEM_SPEC = pl.BlockSpec(memory_space=pltpu.SEMAPHORE)
ANY_SPEC = pl.BlockSpec(memory_space=pl.ANY)
SPLIT_EFFECT = pltpu.SideEffectType.DATAFLOW_SIDE_EFFECTING


def _peers(with_self=False):
    x, y, c = lax.axis_index("x"), lax.axis_index("y"), lax.axis_index("c")
    peers = []
    for k in range(0 if with_self else 1, N_DEV):
        px = 1 - x if (k >> 2) & 1 else x
        py = 1 - y if (k >> 1) & 1 else y
        pc = 1 - c if k & 1 else c
        peers.append(((px, py, pc), 4 * px + 2 * py + pc))
    return 4 * x + 2 * y + c, peers


def _push(src, dst, send_sems, recv_sems, slot, dev):
    return pltpu.make_async_remote_copy(src_ref=src, dst_ref=dst, send_sem=send_sems.at[slot], recv_sem=recv_sems.at[slot],
                                        device_id=dev, device_id_type=pl.DeviceIdType.MESH)


def _landing_shapes(arrs, scatter):
    return [SDS((N_DEV,) + tuple(a.shape[1:] if sc else a.shape), a.dtype) for a, sc in zip(arrs, scatter)]


def exchange(arrs, scatter, name):
    na = len(arrs)

    def body(*refs):
        ins = refs[:na]
        outs = refs[na:2 * na]
        send_sems, recv_sems, local_sems = refs[2 * na:]
        me, peers = _peers()
        local = []
        remote = []
        for a in range(na):
            lc = pltpu.make_async_copy(ins[a].at[me] if scatter[a] else ins[a], outs[a].at[me], local_sems.at[a])
            lc.start()
            local.append(lc)
            for k, (dev, idx) in enumerate(peers):
                cp = _push(ins[a].at[idx] if scatter[a] else ins[a], outs[a].at[me], send_sems, recv_sems,
                           a * N_PEERS + k, dev)
                cp.start()
                remote.append(cp)
        for a in range(na):
            for k, (dev, idx) in enumerate(peers):
                _push(ins[a].at[me] if scatter[a] else ins[a], outs[a].at[idx], send_sems, recv_sems,
                      a * N_PEERS + k, dev).wait_recv()
        for cp in remote:
            cp.wait_send()
        for lc in local:
            lc.wait()

    return pl.pallas_call(
        body, name=name, in_specs=[HBM_SPEC] * na, out_specs=[HBM_SPEC] * na, out_shape=_landing_shapes(arrs, scatter),
        scratch_shapes=[pltpu.SemaphoreType.DMA((na * N_PEERS,)), pltpu.SemaphoreType.DMA((na * N_PEERS,)),
                        pltpu.SemaphoreType.DMA((na,))])(*arrs)


def exchange_start(arrs, scatter, after, name):
    na = len(arrs)
    lands = [lax.empty(l.shape, l.dtype) for l in _landing_shapes(arrs, scatter)]

    def body(*refs):
        ins = refs[:na]
        land = refs[na:2 * na]
        send_sems, recv_sems = refs[2 * na + 1], refs[2 * na + 2]
        token = refs[-1]
        me, peers = _peers(with_self=True)
        for a in range(na):
            for k, (dev, idx) in enumerate(peers):
                _push(ins[a].at[idx] if scatter[a] else ins[a], land[a].at[me], send_sems, recv_sems,
                      a * N_DEV + k, dev).start()
        token[...] = jnp.zeros_like(token)

    thru = [pltpu.HBM(a.shape, a.dtype) for a in arrs] + [pltpu.HBM(l.shape, l.dtype) for l in lands]
    res = pl.pallas_call(
        body, name=name,
        out_shape=(pltpu.SemaphoreType.DMA((na * N_DEV,)), pltpu.SemaphoreType.DMA((na * N_DEV,)), *thru,
                   SDS((8, LANES), F32)),
        in_specs=[HBM_SPEC] * (2 * na) + [ANY_SPEC],
        out_specs=(SEM_SPEC, SEM_SPEC, *([HBM_SPEC] * (2 * na)), pl.BlockSpec(memory_space=pltpu.VMEM)),
        input_output_aliases={i: 2 + i for i in range(2 * na)},
        compiler_params=pltpu.CompilerParams(has_side_effects=SPLIT_EFFECT),
    )(*[pltpu.with_memory_space_constraint(a, pltpu.HBM) for a in arrs],
      *[pltpu.with_memory_space_constraint(l, pltpu.HBM) for l in lands], after)
    return {"send": res[0], "recv": res[1], "src": res[2:2 + na], "land": res[2 + na:2 + 2 * na],
            "token": res[-1][0, 0], "scatter": scatter}


def exchange_wait(handle, after, name):
    scatter = handle["scatter"]
    na = len(scatter)

    def body(*refs):
        src = refs[:na]
        land = refs[na:2 * na]
        send_sems, recv_sems = refs[2 * na], refs[2 * na + 1]
        me, peers = _peers(with_self=True)
        for a in range(na):
            for k, (dev, idx) in enumerate(peers):
                cp = _push(src[a].at[me] if scatter[a] else src[a], land[a].at[idx], send_sems, recv_sems,
                           a * N_DEV + k, dev)
                cp.wait_send()
                cp.wait_recv()

    ops = list(handle["src"]) + list(handle["land"])
    res = pl.pallas_call(
        body, name=name, out_shape=tuple(pltpu.HBM(o.shape, o.dtype) for o in ops),
        in_specs=[HBM_SPEC] * (2 * na) + [SEM_SPEC, SEM_SPEC, ANY_SPEC], out_specs=tuple([HBM_SPEC] * (2 * na)),
        input_output_aliases={i: i for i in range(2 * na)},
        compiler_params=pltpu.CompilerParams(has_side_effects=SPLIT_EFFECT),
    )(*ops, handle["send"], handle["recv"], after)
    return list(res[na:])


def forward_layer(l, xin, xin_bf, mem_bf, wt, nb, s, ffn_weights=None):
    sv = {"xin_bf": xin_bf}
    memkv = mm_nn(mem_bf, wt["memw"], BF16, f"memkv{l}")
    sv["memkv"] = memkv
    if l == 0:
        proj = mm_nn(xin_bf, wt["win_a"], F32, "proj_a")
        pooled, tok = pool_fwd(proj, wt["pw_bd"], wt["pscale"], nb, s)
        sv["pooled"] = pooled
    else:
        kv = mm_nn(xin_bf, wt["kvw"][:, :2 * TOK_WIDTH], BF16, "kv_proj")
        fl = mm_nn(xin_bf, wt["kvw"][:, 2 * TOK_WIDTH:], F32, "gate_proj")
        fneg = -fgate_fwd(fl, wt["fb"], nb, s)
        proj = mm_nn(xin_bf, wt["wq"], BF16, "proj_b")
        kaug, vaug_t = fox_prep(kv, fneg, nb, s)
        tok, o_f32, lse_rows = fox_fwd_t(proj, kaug, vaug_t, nb, s)
        sv.update(kv=kv, fl=fl, fneg=fneg, o_f32=o_f32, lse_rows=lse_rows)
    sv["proj"] = proj
    cat = memattn_fwd(proj, memkv, tok, nb, s, f"memattn_fwd{l}")
    sv["cat"] = cat
    x1, x1_bf, xh1, rs1 = ln_fwd(xin, cat, wt["wout"], wt["ln1_g"], wt["ln1_b"], f"out_proj_ln1_{l}")
    sv.update(x1_bf=x1_bf, xh1=xh1, rs1=rs1)
    if ffn_weights is not None:
        wt.update(ffn_weights(x1_bf))
    act, ga, gb, hu, hg = ffn_up_gate(x1_bf, wt["wup"], wt["cw"], nb, s, f"ffn_up_gate{l}")
    sv.update(act=act, ga=ga, gb=gb, hu=hu, hg=hg)
    x2, x2_bf, xh2, rs2 = ln_fwd(x1, act, wt["wdown"], wt["ln2_g"], wt["ln2_b"], f"ffn_down_ln2_{l}")
    sv.update(xh2=xh2, rs2=rs2)
    return x2, x2_bf, sv


def backward_layer(l, dy, sv, mem_bf, wt, nb, s, after_ffn=None, after_pool=None, loss_target=None):
    g = {}
    if loss_target is None:
        dr2, dr2_bf, g["ln2_g"], g["ln2_b"] = ln_bwd(dy[0], sv["xh2"], sv["rs2"], wt["ln2_g"], f"ln2_bwd{l}",
                                                     dy_scale=dy[1], products=dy[2])
    else:
        dr2, dr2_bf, g["ln2_g"], g["ln2_b"], g["loss_row"] = loss_ln_bwd(sv["xh2"], sv["rs2"], wt["ln2_g"], wt["ln2_b"],
                                                                         loss_target, f"loss_ln2_bwd{l}")
    dact = mm_nn(dr2_bf, wt["wdown"], BF16, f"ffn_down_dx{l}", trans_b=0)
    g["wdown"] = mm_tn(sv["act"], dr2_bf, f"ffn_down_dw{l}")
    dh_u, dh_g, dcw_u, dcw_g = gate_conv_bwd(dact, sv["ga"], sv["gb"], sv["hu"], sv["hg"], wt["cw"], nb, s,
                                             f"gate_conv_bwd{l}")
    g["cw"] = jnp.concatenate([dcw_u, dcw_g], axis=0)
    g["wup"] = (mm_tn(dh_u, sv["x1_bf"], f"ffn_up_dw_u{l}"), mm_tn(dh_g, sv["x1_bf"], f"ffn_up_dw_g{l}"))
    ln1_g = wt["ln1_g"] if after_ffn is None else wt["ln1_g"] + after_ffn(g, dr2)
    dr1, dr1_bf, g["ln1_g"], g["ln1_b"] = ln_bwd(dr2, sv["xh1"], sv["rs1"], ln1_g, f"ffn_up_dx_ln1_bwd{l}",
                                                 dy_scale=DN_ALPHA, products=[(dh_u, wt["wup"], 0), (dh_g, wt["wup"], 1)])
    dcat, dcat_bf = mm_nn(dr1_bf, wt["wout"], F32, f"out_proj_dx{l}", also_bf16=True, trans_b=0)
    g["wout"] = mm_tn(sv["cat"], dr1_bf, f"out_proj_dw{l}")
    if l == 0:
        dmixed, dpooled, g["pscale"] = pool_bwd_mix(dcat, sv["pooled"], wt["pw_bd"], wt["pscale"], nb, s)
        g["pw_full"] = mm_tn(sv["pooled"], dmixed, "pool_dw", out_dtype=F32)
        dtok = pool_bwd_window(dpooled, nb, s)
    else:
        delta = fox_delta(dcat, sv["o_f32"], nb, s)
        tf = min(TF, s)
        dtok, dk, dv, dfcum_k, dfq_rows = fox_bwd(sv["proj"], sv["kv"], sv["fneg"], dcat_bf,
                                                  sv["lse_rows"], _to_tile_rows(delta, nb, s, tf), nb, s)
    dproj, dmemkv = memattn_bwd(sv["proj"], sv["memkv"], dcat, dtok, nb, s, f"memattn_bwd{l}")
    g["memw"] = mm_tn(mem_bf, dmemkv, f"memkv_dw{l}")
    if l == 0:
        win_a = wt["win_a"] if after_pool is None else wt["win_a"] + after_pool(g, dproj).astype(BF16)
        dx = mm_nn(dproj, win_a, F32, "proj_a_dx", addend=dr1, add_scale=DN_ALPHA, trans_b=0)
        g["win_a"] = mm_tn(sv["xin_bf"], dproj, "proj_a_dw")
    else:
        dkvf, g["fb"] = fgate_bwd(_from_tile_rows(dfq_rows), dfcum_k, sv["fl"], wt["fb"], dk, dv, nb, s)
        dx = (dr1, DN_ALPHA, [(dproj, wt["wq"], 0), (dkvf, wt["kvw"], 0)])
        g["wq"] = mm_tn(sv["xin_bf"], dproj, "proj_b_dw")
        g["kvw"] = mm_tn(sv["xin_bf"], dkvf, "kv_proj_dw")
    return dx, g


def pack_replicated(pool_w, ln1_g, ln1_b, ln2_g, ln2_b, conv_b, f_b):
    cb = jnp.pad(conv_b, ((0, 0), (0, 6144 - 5504))).reshape(12, D_MODEL)
    fb = jnp.pad(f_b.reshape(1, FOX_HEADS), ((0, 3), (0, D_MODEL - FOX_HEADS)))
    return jnp.concatenate([pool_w.reshape(144, D_MODEL), ln1_g, ln1_b, ln2_g, ln2_b, cb, fb], axis=0)


def unpack_replicated(buf):
    pool_w = buf[:144].reshape(1, 4, POOL_GROUP, POOL_GROUP)
    ln = [buf[144 + 2 * k:146 + 2 * k] for k in range(4)]
    conv_b = buf[152:164].reshape(2, 6144)[:, :5504]
    f_b = buf[164, :FOX_HEADS]
    return pool_w, ln[0], ln[1], ln[2], ln[3], conv_b, f_b


def _pad_ff(a, axis):
    zeros = jnp.zeros(a.shape[:axis] + (FF_ROWS_PAD - FF_ROWS,) + a.shape[axis + 1:], a.dtype)
    halves = [lax.slice_in_dim(a, h * FF_ROWS, (h + 1) * FF_ROWS, axis=axis) for h in range(2)]
    return jnp.concatenate([halves[0], zeros, halves[1], zeros], axis=axis)


def _unpad_ff(a, axis):
    return jnp.concatenate([lax.slice_in_dim(a, h * FF_ROWS_PAD, h * FF_ROWS_PAD + FF_ROWS, axis=axis) for h in range(2)],
                           axis=axis)


def pack_small(conv_w, pool_scale):
    buf = jnp.zeros((16, FF_BLOCK_PAD), F32)
    buf = lax.dynamic_update_slice(buf, _pad_ff(conv_w.reshape(DEPTH * 3, FF_BLOCK), 1), (0, 0))
    return lax.dynamic_update_slice(buf, pool_scale, (8, 0))


def _block_diag(pw):
    out = jnp.zeros((TOK_WIDTH, TOK_WIDTH), pw.dtype)
    for g in range(4):
        out = lax.dynamic_update_slice(out, pw[g], (g * POOL_GROUP, g * POOL_GROUP))
    return out


def layer_shards(l, sq_a, sq_b, mem_w_kv, ffn_w_up, ffn_w_down):
    wdown = jnp.pad(ffn_w_down[l], ((0, FF_ROWS_PAD - FF_ROWS), (0, 0)))
    return [sq_a[0].astype(BF16), sq_b[0].astype(BF16), mem_w_kv[l].astype(BF16), _pad_ff(ffn_w_up[l], 1).astype(BF16),
            wdown.astype(BF16)]


def mixer_weights(l, gath, ln1_g, ln1_b, ln2_g, ln2_b):
    w_out = gath[1].reshape(D_MODEL, D_MODEL)
    wt = {"memw": gath[2].reshape(D_MODEL, 2 * MEM_WIDTH), "wout": w_out,
          "ln1_g": ln1_g[l:l + 1], "ln1_b": ln1_b[l:l + 1], "ln2_g": ln2_g[l:l + 1], "ln2_b": ln2_b[l:l + 1]}
    return wt, gath[0].reshape(D_MODEL, D_MODEL)


def ffn_weights(l, wup_g, wdown_g, small, conv_b):
    cb = _pad_ff(conv_b[l].reshape(N_DEV, FF_BLOCK), 1)
    cw = jnp.concatenate([small[:, 3 * l:3 * l + 3, :], cb[:, None, :], jnp.zeros((N_DEV, 4, FF_BLOCK_PAD), F32)], axis=1)
    return {"wup": wup_g, "wdown": wdown_g.reshape(FF_PAIRS * FF_BLOCK_PAD, D_MODEL), "cw": cw}


def mixer_grad_blocks(g, w_in_grad):
    blocks = [] if w_in_grad is None else [w_in_grad.reshape(N_DEV, 128, D_MODEL)]
    blocks += [g["wout"].reshape(N_DEV, 128, D_MODEL), g["memw"].reshape(N_DEV, 128, 2 * MEM_WIDTH)]
    return [b.astype(BF16) for b in blocks]


def ffn_grad_blocks(g):
    wdown = g["wdown"].reshape(N_DEV, FF_ROWS_PAD, D_MODEL)[:, :FF_ROWS]
    halves = [_unpad_ff(h.reshape(FF_PAIRS, FF_BLOCK_PAD, D_MODEL), 1) for h in g["wup"]]
    wup_t = jnp.concatenate(halves, axis=0)
    return [wup_t.astype(BF16), wdown.astype(BF16)]


def small_grad_blocks(g0, g1):
    taps = jnp.stack([g0["cw"][:, :3, :], g1["cw"][:, :3, :]], axis=1).reshape(N_DEV, DEPTH * 3, FF_BLOCK_PAD)
    small = jnp.zeros((N_DEV, 16, FF_BLOCK_PAD), F32)
    small = lax.dynamic_update_slice(small, taps, (0, 0, 0))
    return lax.dynamic_update_slice(small, g0["pscale"].reshape(N_DEV, 1, 96), (0, 8, 0))


def replicated_grads(g0, g1):
    pw = jnp.stack([g0["pw_full"][k * POOL_GROUP:(k + 1) * POOL_GROUP, k * POOL_GROUP:(k + 1) * POOL_GROUP] for k in range(4)])
    conv_b = jnp.stack([_unpad_ff(g_["cw"][:, 3, :], 1).reshape(N_DEV * FF_BLOCK) for g_ in (g0, g1)])
    ln = [jnp.concatenate([g0[n], g1[n]], axis=0) for n in ("ln1_g", "ln1_b", "ln2_g", "ln2_b")]
    return pack_replicated(pw[None], ln[0], ln[1], ln[2], ln[3], conv_b, g1["fb"][0, :FOX_HEADS])


def kernel(x, mem, a_w_in, a_pool_w, a_pool_scale, a_w_out, b_w_q, b_w_out, kv_w, f_b, mem_w_kv, ln1_g, ln1_b, ln2_g, ln2_b, ffn_w_up, ffn_conv_w, ffn_conv_b, ffn_w_down, loss_target, m_a_w_in, m_a_pool_w, m_a_pool_scale, m_a_w_out, m_b_w_q, m_b_w_out, m_kv_w, m_f_b, m_mem_w_kv, m_ln1_g, m_ln1_b, m_ln2_g, m_ln2_b, m_ffn_w_up, m_ffn_conv_w, m_ffn_conv_b, m_ffn_w_down, v_a_w_in, v_a_pool_w, v_a_pool_scale, v_a_w_out, v_b_w_q, v_b_w_out, v_kv_w, v_f_b, v_mem_w_kv, v_ln1_g, v_ln1_b, v_ln2_g, v_ln2_b, v_ffn_w_up, v_ffn_conv_w, v_ffn_conv_b, v_ffn_w_down):
    nb, s, d = x.shape
    t = nb * s
    x2d, mem_bf, target = x.reshape(t, d), mem.reshape(nb * MEM_LEN, d).astype(BF16), loss_target.reshape(t, d)

    shards0 = layer_shards(0, a_w_in, a_w_out, mem_w_kv, ffn_w_up, ffn_w_down)
    shards1 = layer_shards(1, b_w_q, b_w_out, mem_w_kv, ffn_w_up, ffn_w_down)
    shards1.append(jnp.pad(kv_w, ((0, 0), (0, KV_COLS_PAD - KV_COLS))).astype(BF16))
    gath0 = exchange(shards0[:3] + [pack_small(ffn_conv_w, a_pool_scale)], [False] * 4, "gather_w0_mixer")
    pending = {"ffn0": exchange_start(shards0[3:], [False] * 2, gath0[0], "gather_w0_ffn_start")}
    small = gath0[3]
    wt0, w_in = mixer_weights(0, gath0, ln1_g + pending["ffn0"]["token"], ln1_b, ln2_g, ln2_b)
    pw_bd = _block_diag(a_pool_w[0])
    wt0.update(win_a=w_in, pw_bd=pw_bd.astype(BF16),
               pscale=small[:, 8, :96].reshape(1, TOK_WIDTH) + pending["ffn0"]["token"])

    def ffn0_weights(x1_bf):
        got = exchange_wait(pending["ffn0"], x1_bf, "gather_w0_ffn_wait")
        pending["w1"] = exchange_start(shards1, [False] * 6, got[0], "gather_w1_start")
        w = ffn_weights(0, got[0], got[1], small, ffn_conv_b)
        w["cw"] = w["cw"] + pending["w1"]["token"]
        return w

    x1, x1_bf, sv0 = forward_layer(0, x2d, x2d, mem_bf, wt0, nb, s, ffn_weights=ffn0_weights)
    gath1 = exchange_wait(pending["w1"], x1_bf, "gather_w1_wait")
    wt1, w_q = mixer_weights(1, gath1, ln1_g, ln1_b, ln2_g, ln2_b)
    wt1.update(ffn_weights(1, gath1[3], gath1[4], small, ffn_conv_b))
    kvw = gath1[5].reshape(D_MODEL, KV_COLS_PAD)
    wt1.update(wq=w_q, kvw=kvw,
               fb=jnp.pad(f_b.reshape(1, FOX_HEADS), ((0, 0), (0, LANES - FOX_HEADS))))
    _, _, sv1 = forward_layer(1, x1, x1_bf, mem_bf, wt1, nb, s)

    dx1, g1 = backward_layer(1, None, sv1, mem_bf, wt1, nb, s, loss_target=target)
    loss = lax.psum(g1["loss_row"][0, 0], ("x", "y", "c"))
    blocks1 = (mixer_grad_blocks(g1, g1["wq"]) + ffn_grad_blocks(g1)
               + [g1["kvw"][:, :KV_COLS].reshape(N_DEV, 128, KV_COLS).astype(BF16)])
    pending["g1"] = exchange_start(blocks1, [True] * 6, dx1[0], "scatter_g1_start")
    wt0["ln2_g"] = wt0["ln2_g"] + pending["g1"]["token"]

    def after_ffn0(g, dxm):
        pending["gf0"] = exchange_start(ffn_grad_blocks(g), [True] * 2, dxm, "scatter_g0_ffn_start")
        return pending["gf0"]["token"]

    def after_pool0(g, x):
        blocks = mixer_grad_blocks(g, None) + [small_grad_blocks(g, g1), replicated_grads(g, g1)]
        pending["gm0"] = exchange_start(blocks, [True] * 3 + [False], x, "scatter_g0_mixer_start")
        return pending["gm0"]["token"]

    grad_x, g0 = backward_layer(0, dx1, sv0, mem_bf, wt0, nb, s, after_ffn=after_ffn0, after_pool=after_pool0)
    pending["gin"] = exchange_start([g0["win_a"].reshape(N_DEV, 128, D_MODEL).astype(BF16)], [True], grad_x,
                                    "scatter_g0_in_start")
    parts_f0 = exchange_wait(pending["gf0"], jnp.zeros((8, LANES), F32) + pending["gin"]["token"], "scatter_g0_ffn_wait")
    parts1 = exchange_wait(pending["g1"], parts_f0[0], "scatter_g1_wait")

    res = {}

    def upd(nm, parts, w2, m2, v2):
        res[nm] = reduce_adamw(parts, w2, m2, v2, f"adamw_{nm}")

    upd("b_w_q", [parts1[0]], b_w_q, m_b_w_q, v_b_w_q)
    upd("b_w_out", [parts1[1]], b_w_out, m_b_w_out, v_b_w_out)
    upd("kv_w", [parts1[5]], kv_w[None], m_kv_w[None], v_kv_w[None])
    upd("ffn_w_up", [parts_f0[0], parts1[3]], *[a.transpose(0, 2, 1) for a in (ffn_w_up, m_ffn_w_up, v_ffn_w_up)])
    res["ffn_w_up"] = [o.transpose(0, 2, 1) for o in res["ffn_w_up"]]
    upd("ffn_w_down", [parts_f0[1], parts1[4]], ffn_w_down, m_ffn_w_down, v_ffn_w_down)
    parts_m0 = exchange_wait(pending["gm0"], res["ffn_w_down"][0], "scatter_g0_mixer_wait")
    parts_in = exchange_wait(pending["gin"], parts_m0[0], "scatter_g0_in_wait")
    upd("a_w_in", [parts_in[0]], a_w_in, m_a_w_in, v_a_w_in)
    upd("a_w_out", [parts_m0[0]], a_w_out, m_a_w_out, v_a_w_out)
    upd("mem_w_kv", [parts_m0[1], parts1[2]], mem_w_kv, m_mem_w_kv, v_mem_w_kv)
    upd("small", [parts_m0[2]], pack_small(ffn_conv_w, a_pool_scale)[None], pack_small(m_ffn_conv_w, m_a_pool_scale)[None],
        pack_small(v_ffn_conv_w, v_a_pool_scale)[None])
    upd("replicated", [parts_m0[3]], pack_replicated(a_pool_w, ln1_g, ln1_b, ln2_g, ln2_b, ffn_conv_b, f_b)[None],
        pack_replicated(m_a_pool_w, m_ln1_g, m_ln1_b, m_ln2_g, m_ln2_b, m_ffn_conv_b, m_f_b)[None],
        pack_replicated(v_a_pool_w, v_ln1_g, v_ln1_b, v_ln2_g, v_ln2_b, v_ffn_conv_b, v_f_b)[None])

    res["kv_w"] = [o[0] for o in res["kv_w"]]
    res["ffn_conv_w"] = [_unpad_ff(o[0, :DEPTH * 3, :], 1).reshape(DEPTH, 3, FF_BLOCK) for o in res["small"]]
    res["a_pool_scale"] = [o[0, 8:9, :96] for o in res["small"]]
    rep_names = ["a_pool_w", "ln1_g", "ln1_b", "ln2_g", "ln2_b", "ffn_conv_b", "f_b"]
    for nm in rep_names:
        res[nm] = []
    for o in res["replicated"]:
        for nm, val in zip(rep_names, unpack_replicated(o[0])):
            res[nm].append(val)

    order = ["a_w_in", "a_pool_w", "a_pool_scale", "a_w_out", "b_w_q", "b_w_out", "kv_w", "f_b", "mem_w_kv",
             "ln1_g", "ln1_b", "ln2_g", "ln2_b", "ffn_w_up", "ffn_conv_w", "ffn_conv_b", "ffn_w_down"]
    out = [loss, grad_x.reshape(nb, s, d)]
    for kind in range(4):
        out.extend(res[nm][kind] for nm in order)
    return tuple(out)
```

```python
import jax
import jax.numpy as jnp
from jax import lax
from jax.experimental import pallas as pl
from jax.experimental.pallas import tpu as pltpu

F32 = jnp.float32
BF16 = jnp.bfloat16
SDS = jax.ShapeDtypeStruct

N_DEV = 8
D_MODEL = 1024
TOK_WIDTH = 768
MEM_WIDTH = 256
MEM_LEN = 256
MEM_HEADS = 4
HEAD_DIM = 64
FOX_HEADS = 12
POOL_GROUP = 192
FF_BLOCK = 688
FF_BLOCK_PAD = 768
FF_PAIRS = 4
FF_ROWS = 344
FF_ROWS_PAD = FF_BLOCK_PAD // 2
KV_COLS = 1548
KV_COLS_PAD = 1664
LANES = 128
DEPTH = 2
DN_ALPHA = (2.0 * DEPTH) ** 0.25
LN_EPS = 1e-5
QK_SCALE = HEAD_DIM ** -0.5
NEG_BIG = -1e30

ADAM_LR = 0.001
ADAM_B1 = 0.9
ADAM_B2 = 0.999
ADAM_EPS = 1e-08
ADAM_WD = 0.01
ADAM_STEP = 10

VMEM_LIMIT_BYTES = 56 * 1024 * 1024
MM_BLOCK_BYTES = 6 * 1024 * 1024
TM = 512
TS = 256
TF = 256
TC = 256
HALO_POOL = 16
HALO_CONV = 8

NT_DIMS = (((1,), (1,)), ((), ()))
TN_DIMS = (((0,), (0,)), ((), ()))


def _params(sem=None):
    return pltpu.CompilerParams(dimension_semantics=sem, vmem_limit_bytes=VMEM_LIMIT_BYTES)


def _sigmoid(z):
    return 1.0 / (1.0 + jnp.exp(-z))


def _pick_tn(n):
    if n <= 2048:
        return n
    for t in (1024, 768, 512, 256, 128):
        if n % t == 0:
            return t
    return n


def mm_nn(a, b, out_dtype, name, addend=None, add_scale=1.0, also_bf16=False, trans_b=None):
    m, k = a.shape
    n = b.shape[1] if trans_b is None else b.shape[0]
    tm = min(TM, m)
    tn = n
    while k * tn * 2 > MM_BLOCK_BYTES or tm * tn * 4 > MM_BLOCK_BYTES:
        tn //= 2
    chunk = tn if tn <= 2048 else _pick_tn(tn)
    has_add = addend is not None

    def body(*refs):
        a_ref, b_ref = refs[0], refs[1]
        c_ref = refs[2] if has_add else None
        o_ref = refs[3] if has_add else refs[2]
        ob_ref = refs[-1] if also_bf16 else None
        av = a_ref[...].astype(BF16)
        for c in range(tn // chunk):
            cols = slice(c * chunk, (c + 1) * chunk)
            if trans_b is None:
                r = jnp.dot(av, b_ref[:, cols].astype(BF16), preferred_element_type=F32)
            else:
                r = lax.dot_general(av, b_ref[cols, :].astype(BF16), NT_DIMS, preferred_element_type=F32)
            if has_add:
                r = r + add_scale * c_ref[:, cols]
            o_ref[:, cols] = r.astype(out_dtype)
            if also_bf16:
                ob_ref[:, cols] = r.astype(BF16)

    b_spec = (pl.BlockSpec((k, tn), lambda j, i: (0, j)) if trans_b is None
              else pl.BlockSpec((tn, k), lambda j, i: (j, trans_b)))
    in_specs = [pl.BlockSpec((tm, k), lambda j, i: (i, 0)), b_spec]
    ops = [a, b]
    tile = pl.BlockSpec((tm, tn), lambda j, i: (i, j))
    if has_add:
        in_specs.append(tile)
        ops.append(addend)
    out_shape = [SDS((m, n), out_dtype)]
    out_specs = [tile]
    if also_bf16:
        out_shape.append(SDS((m, n), BF16))
        out_specs.append(tile)
    res = pl.pallas_call(
        body, name=name, grid=(n // tn, m // tm), in_specs=in_specs, out_specs=out_specs, out_shape=out_shape,
        compiler_params=_params(("parallel", "parallel")))(*ops)
    return tuple(res) if also_bf16 else res[0]


def mm_tn(a, b, name, out_dtype=BF16):
    t, m = a.shape
    _, n = b.shape
    tt = min(4 * TM, t)
    tm = 1024 if m % 1024 == 0 else m
    tn = _pick_tn(n)
    nt = t // tt
    block = (tm, tn)
    in_place = out_dtype == F32

    def body(a_ref, b_ref, o_ref, *scratch):
        acc_ref = o_ref if in_place else scratch[0]
        kk = pl.program_id(2)
        r = lax.dot_general(a_ref[...].astype(BF16), b_ref[...].astype(BF16), TN_DIMS, preferred_element_type=F32)

        @pl.when(kk == 0)
        def _():
            acc_ref[...] = r

        @pl.when(kk != 0)
        def _():
            acc_ref[...] += r

        if not in_place:
            @pl.when(kk == nt - 1)
            def _():
                o_ref[...] = acc_ref[...].astype(out_dtype)

    return pl.pallas_call(
        body, name=name, grid=(m // tm, n // tn, nt),
        in_specs=[pl.BlockSpec((tt, tm), lambda i, j, kk: (kk, i)), pl.BlockSpec((tt, tn), lambda i, j, kk: (kk, j))],
        out_specs=pl.BlockSpec(block, lambda i, j, kk: (i, j)), out_shape=SDS((m, n), out_dtype),
        scratch_shapes=[] if in_place else [pltpu.VMEM(block, F32)],
        compiler_params=_params(("parallel", "parallel", "arbitrary")))(a, b)


def ln_fwd(xprev, a, w, g, b, name):
    t, d = xprev.shape
    k = a.shape[1]
    tm = min(TM, t)

    def body(xp_ref, a_ref, w_ref, g_ref, b_ref, y_ref, yb_ref, xh_ref, rs_ref):
        r = DN_ALPHA * xp_ref[...] + jnp.dot(a_ref[...], w_ref[...], preferred_element_type=F32)
        mu = jnp.mean(r, axis=1, keepdims=True)
        xc = r - mu
        var = jnp.mean(xc * xc, axis=1, keepdims=True)
        rstd = lax.rsqrt(var + LN_EPS)
        xh = xc * rstd
        y = xh * g_ref[...] + b_ref[...]
        y_ref[...] = y
        yb_ref[...] = y.astype(BF16)
        xh_ref[...] = xh
        rs_ref[...] = jnp.broadcast_to(rstd, (tm, LANES))

    row = pl.BlockSpec((tm, d), lambda i: (i, 0))
    vec = pl.BlockSpec((1, d), lambda i: (0, 0))
    return pl.pallas_call(
        body, name=name, grid=(t // tm,),
        in_specs=[row, pl.BlockSpec((tm, k), lambda i: (i, 0)), pl.BlockSpec((k, d), lambda i: (0, 0)), vec, vec],
        out_specs=[row, row, row, pl.BlockSpec((tm, LANES), lambda i: (i, 0))],
        out_shape=[SDS((t, d), F32), SDS((t, d), BF16), SDS((t, d), F32), SDS((t, LANES), F32)],
        compiler_params=_params(("parallel",)))(xprev, a, w, g, b)


def ln_bwd(dy, xhat, rstd, g, name, products=(), dy_scale=1.0):
    t, d = dy.shape
    np_ = len(products)
    tm = min(TM if sum(a.shape[1] for a, _, _ in products) <= 4096 else TS, t)

    def body(*refs):
        prod_refs = refs[:2 * np_]
        dy_ref, xh_ref, rs_ref, g_ref, dr_ref, drb_ref, dg_ref, db_ref = refs[2 * np_:]
        i = pl.program_id(0)
        dyv = dy_ref[...] if dy_scale == 1.0 else dy_scale * dy_ref[...]
        for p in range(np_):
            a_ref, w_ref = prod_refs[2 * p], prod_refs[2 * p + 1]
            if len(w_ref.shape) == 2:
                dyv = dyv + lax.dot_general(a_ref[...], w_ref[...], NT_DIMS, preferred_element_type=F32)
            else:
                kb = w_ref.shape[2]
                for c in range(w_ref.shape[0]):
                    dyv = dyv + lax.dot_general(a_ref[:, c * kb:(c + 1) * kb], w_ref[c], NT_DIMS,
                                                preferred_element_type=F32)
        xh = xh_ref[...]
        dxh = dyv * g_ref[...]
        m1 = jnp.mean(dxh, axis=1, keepdims=True)
        m2 = jnp.mean(dxh * xh, axis=1, keepdims=True)
        dr = rs_ref[:, 0:1] * (dxh - m1 - xh * m2)
        dr_ref[...] = dr
        drb_ref[...] = dr.astype(BF16)

        @pl.when(i == 0)
        def _():
            dg_ref[...] = jnp.zeros_like(dg_ref)
            db_ref[...] = jnp.zeros_like(db_ref)

        dg_ref[...] += jnp.sum(dyv * xh, axis=0, keepdims=True)
        db_ref[...] += jnp.sum(dyv, axis=0, keepdims=True)

    row = pl.BlockSpec((tm, d), lambda i: (i, 0))
    vec = pl.BlockSpec((1, d), lambda i: (0, 0))
    in_specs = [row, row, pl.BlockSpec((tm, LANES), lambda i: (i, 0)), vec]
    ops = [dy, xhat, rstd, g]
    for a, w, col in reversed(products):
        k = a.shape[1]
        if w.ndim == 2:
            w_spec = pl.BlockSpec((d, k), lambda i, col=col: (0, col))
        else:
            w_spec = pl.BlockSpec((k // w.shape[2], d, w.shape[2]), lambda i, col=col: (col, 0, 0))
        in_specs = [pl.BlockSpec((tm, k), lambda i: (i, 0)), w_spec] + in_specs
        ops = [a, w] + ops
    return pl.pallas_call(
        body, name=name, grid=(t // tm,), in_specs=in_specs, out_specs=[row, row, vec, vec],
        out_shape=[SDS((t, d), F32), SDS((t, d), BF16), SDS((1, d), F32), SDS((1, d), F32)],
        compiler_params=_params(("arbitrary",)))(*ops)


def loss_ln_bwd(xhat, rstd, g, beta, target, name):
    t, d = xhat.shape
    tm = min(TM, t)
    nsteps = t // tm

    def body(xh_ref, rs_ref, g_ref, b_ref, t_ref, dr_ref, drb_ref, dg_ref, db_ref, l_ref, acc):
        i = pl.program_id(0)
        xh = xh_ref[...]
        diff = xh * g_ref[...] + b_ref[...] - t_ref[...]
        dyv = diff * (1.0 / d)
        dxh = dyv * g_ref[...]
        m1 = jnp.mean(dxh, axis=1, keepdims=True)
        m2 = jnp.mean(dxh * xh, axis=1, keepdims=True)
        dr = rs_ref[:, 0:1] * (dxh - m1 - xh * m2)
        dr_ref[...] = dr
        drb_ref[...] = dr.astype(BF16)

        @pl.when(i == 0)
        def _():
            dg_ref[...] = jnp.zeros_like(dg_ref)
            db_ref[...] = jnp.zeros_like(db_ref)
            acc[...] = jnp.zeros_like(acc)

        dg_ref[...] += jnp.sum(dyv * xh, axis=0, keepdims=True)
        db_ref[...] += jnp.sum(dyv, axis=0, keepdims=True)
        acc[...] += jnp.sum(diff * diff, axis=0, keepdims=True)

        @pl.when(i == nsteps - 1)
        def _():
            tot = jnp.sum(acc[...], axis=1, keepdims=True) * (0.5 / d)
            l_ref[...] = jnp.broadcast_to(tot, (1, LANES))

    row = pl.BlockSpec((tm, d), lambda i: (i, 0))
    vec = pl.BlockSpec((1, d), lambda i: (0, 0))
    return pl.pallas_call(
        body, name=name, grid=(nsteps,),
        in_specs=[row, pl.BlockSpec((tm, LANES), lambda i: (i, 0)), vec, vec, row],
        out_specs=[row, row, vec, vec, pl.BlockSpec((1, LANES), lambda i: (0, 0))],
        out_shape=[SDS((t, d), F32), SDS((t, d), BF16), SDS((1, d), F32), SDS((1, d), F32), SDS((1, LANES), F32)],
        scratch_shapes=[pltpu.VMEM((1, d), F32)],
        compiler_params=_params(("arbitrary",)))(xhat, rstd, g, beta, target)


def memattn_fwd(proj, memkv, tok, nb, s, name):
    ts = min(TM, s)
    nq = s // ts

    def body(q_ref, kv_ref, tok_ref, o_ref):
        o_ref[:, :TOK_WIDTH] = tok_ref[...]
        top = lax.broadcasted_iota(jnp.int32, (PAIR, ts), 0) < HEAD_DIM
        scores = []
        for p in range(MEM_HEADS // 2):
            qp = q_ref[:, p * PAIR:(p + 1) * PAIR].astype(BF16)
            ke, ko = _split_pair(kv_ref[:, p * PAIR:(p + 1) * PAIR], QK_SCALE)
            scores.append([lax.dot_general(km, qp, NT_DIMS, preferred_element_type=F32) for km in (ke, ko)])
        for p in range(MEM_HEADS // 2):
            vt = kv_ref[:, MEM_WIDTH + p * PAIR:MEM_WIDTH + (p + 1) * PAIR].astype(F32).T.astype(BF16)
            outs = []
            for sc in scores[p]:
                e = jnp.exp(sc - jnp.max(sc, axis=0, keepdims=True))
                pr = e / jnp.sum(e, axis=0, keepdims=True)
                outs.append(jnp.dot(vt, pr.astype(BF16), preferred_element_type=F32))
            o_ref[:, TOK_WIDTH + p * PAIR:TOK_WIDTH + (p + 1) * PAIR] = jnp.where(top, outs[0], outs[1]).T.astype(BF16)

    return pl.pallas_call(
        body, name=name, grid=(nb, nq),
        in_specs=[pl.BlockSpec((ts, MEM_WIDTH), lambda b, i: (b * nq + i, 3)),
                  pl.BlockSpec((MEM_LEN, 2 * MEM_WIDTH), lambda b, i: (b, 0)),
                  pl.BlockSpec((ts, TOK_WIDTH), lambda b, i: (b * nq + i, 0))],
        out_specs=pl.BlockSpec((ts, TOK_WIDTH + MEM_WIDTH), lambda b, i: (b * nq + i, 0)),
        out_shape=SDS((nb * s, TOK_WIDTH + MEM_WIDTH), BF16),
        compiler_params=_params(("parallel", "parallel")))(proj, memkv, tok)


def memattn_bwd(proj, memkv, dcat, dtok, nb, s, name):
    ts = min(TM, s)
    nq = s // ts

    def body(q_ref, kv_ref, do_ref, dtok_ref, dq_ref, dkv_ref):
        i = pl.program_id(1)
        dq_ref[:, :TOK_WIDTH] = dtok_ref[...]

        @pl.when(i == 0)
        def _():
            dkv_ref[...] = jnp.zeros_like(dkv_ref)

        lo = _half_masks(MEM_LEN)
        top = lax.broadcasted_iota(jnp.int32, (PAIR, ts), 0) < HEAD_DIM
        n_pairs = MEM_HEADS // 2
        qs, dos, kps, products = [], [], [], []
        for p in range(n_pairs):
            qp = q_ref[:, p * PAIR:(p + 1) * PAIR].astype(BF16)
            dop = do_ref[:, p * PAIR:(p + 1) * PAIR].astype(BF16)
            kp = kv_ref[:, p * PAIR:(p + 1) * PAIR] * QK_SCALE
            kms = _split_pair(kp)
            vms = _split_pair(kv_ref[:, MEM_WIDTH + p * PAIR:MEM_WIDTH + (p + 1) * PAIR])
            products.append([(lax.dot_general(km, qp, NT_DIMS, preferred_element_type=F32),
                              lax.dot_general(vm, dop, NT_DIMS, preferred_element_type=F32)) for km, vm in zip(kms, vms)])
            qs.append(qp)
            dos.append(dop)
            kps.append(kp)
        for p in range(n_pairs):
            kt = kps[p].astype(F32).T.astype(BF16)
            dks, dvs, dqs = [], [], []
            for sc, dp in products[p]:
                e = jnp.exp(sc - jnp.max(sc, axis=0, keepdims=True))
                pr = e / jnp.sum(e, axis=0, keepdims=True)
                dl = jnp.sum(pr * dp, axis=0, keepdims=True)
                ds = (pr * (dp - dl)).astype(BF16)
                dvs.append(jnp.dot(pr.astype(BF16), dos[p], preferred_element_type=F32))
                dks.append(jnp.dot(ds, qs[p], preferred_element_type=F32))
                dqs.append(jnp.dot(kt, ds, preferred_element_type=F32))
            dq_ref[:, TOK_WIDTH + p * PAIR:TOK_WIDTH + (p + 1) * PAIR] = jnp.where(top, dqs[0], dqs[1]).T.astype(BF16)
            dkv_ref[:, p * PAIR:(p + 1) * PAIR] += jnp.where(lo, dks[0], dks[1]) * QK_SCALE
            dkv_ref[:, MEM_WIDTH + p * PAIR:MEM_WIDTH + (p + 1) * PAIR] += jnp.where(lo, dvs[0], dvs[1])

    return pl.pallas_call(
        body, name=name, grid=(nb, nq),
        in_specs=[pl.BlockSpec((ts, MEM_WIDTH), lambda b, i: (b * nq + i, 3)),
                  pl.BlockSpec((MEM_LEN, 2 * MEM_WIDTH), lambda b, i: (b, 0)),
                  pl.BlockSpec((ts, MEM_WIDTH), lambda b, i: (b * nq + i, 3)),
                  pl.BlockSpec((ts, TOK_WIDTH), lambda b, i: (b * nq + i, 0))],
        out_specs=[pl.BlockSpec((ts, TOK_WIDTH + MEM_WIDTH), lambda b, i: (b * nq + i, 0)),
                   pl.BlockSpec((MEM_LEN, 2 * MEM_WIDTH), lambda b, i: (b, 0))],
        out_shape=[SDS((nb * s, TOK_WIDTH + MEM_WIDTH), BF16), SDS((nb * MEM_LEN, 2 * MEM_WIDTH), F32)],
        compiler_params=_params(("parallel", "arbitrary")))(proj, memkv, dcat, dtok)


def _pool_select(shape, s2, s4, s8, s16):
    lane = lax.broadcasted_iota(jnp.int32, shape, 1)
    return jnp.where(lane < POOL_GROUP, s2, jnp.where(lane < 2 * POOL_GROUP, s4, jnp.where(lane < 3 * POOL_GROUP, s8, s16)))


def _pool_count(shape, first_pos):
    pos = first_pos + lax.broadcasted_iota(jnp.int32, shape, 0)
    win = _pool_select(shape, 2, 4, 8, 16)
    return jnp.minimum(pos + 1, win).astype(F32)


def pool_fwd(proj, pw_bd, pscale, nb, s):
    ts = min(TS, s)
    nq = s // ts
    w = TOK_WIDTH

    def body(c_ref, h_ref, w_ref, sc_ref, pooled_ref, tok_ref):
        i = pl.program_id(0) % nq
        cur = c_ref[...]
        halo = jnp.where(i == 0, 0.0, h_ref[...])
        xe = jnp.concatenate([halo, cur], axis=0)
        s2 = xe + pltpu.roll(xe, 1, axis=0)
        s4 = s2 + pltpu.roll(s2, 2, axis=0)
        s8 = s4 + pltpu.roll(s4, 4, axis=0)
        s16 = s8 + pltpu.roll(s8, 8, axis=0)
        hp = HALO_POOL
        ws = _pool_select((ts, w), s2[hp:], s4[hp:], s8[hp:], s16[hp:])
        pooled = (ws / _pool_count((ts, w), i * ts) - cur).astype(BF16)
        pooled_ref[...] = pooled
        mixed = jnp.dot(pooled, w_ref[...], preferred_element_type=F32)
        tok_ref[...] = (mixed * sc_ref[...]).astype(BF16)

    row = pl.BlockSpec((ts, w), lambda r: (r, 0))
    return pl.pallas_call(
        body, name="pool_fwd", grid=(nb * nq,),
        in_specs=[row, pl.BlockSpec((HALO_POOL, w), lambda r: (jnp.maximum(r * (ts // HALO_POOL) - 1, 0), 0)),
                  pl.BlockSpec((w, w), lambda r: (0, 0)), pl.BlockSpec((1, w), lambda r: (0, 0))],
        out_specs=[row, row], out_shape=[SDS((nb * s, w), BF16), SDS((nb * s, w), BF16)],
        compiler_params=_params(("parallel",)))(proj, proj, pw_bd, pscale)


def pool_bwd_mix(dcat, pooled, pw_bd, pscale, nb, s):
    ts = min(TS, s)
    w = TOK_WIDTH

    def body(dt_ref, p_ref, w_ref, sc_ref, dm_ref, dp_ref, ds_ref):
        r = pl.program_id(0)
        dtok = dt_ref[...]
        mixed = jnp.dot(p_ref[...], w_ref[...], preferred_element_type=F32)

        @pl.when(r == 0)
        def _():
            ds_ref[...] = jnp.zeros_like(ds_ref)

        ds_ref[...] += jnp.sum(dtok * mixed, axis=0, keepdims=True)
        dmx = (dtok * sc_ref[...]).astype(BF16)
        dm_ref[...] = dmx
        dp_ref[...] = lax.dot_general(dmx, w_ref[...], NT_DIMS, preferred_element_type=F32)

    row = pl.BlockSpec((ts, w), lambda r: (r, 0))
    mat = pl.BlockSpec((w, w), lambda r: (0, 0))
    vec = pl.BlockSpec((1, w), lambda r: (0, 0))
    return pl.pallas_call(
        body, name="pool_bwd_mix", grid=(nb * s // ts,), in_specs=[row, row, mat, vec],
        out_specs=[row, row, vec], out_shape=[SDS((nb * s, w), BF16), SDS((nb * s, w), F32), SDS((1, w), F32)],
        compiler_params=_params(("arbitrary",)))(dcat, pooled, pw_bd, pscale)


def pool_bwd_window(dpooled, nb, s):
    ts = min(TS, s)
    nq = s // ts
    w = TOK_WIDTH
    n_ext = ts + HALO_POOL
    n_halo_blocks = nb * s // HALO_POOL

    def body(c_ref, n_ref, du_ref):
        i = pl.program_id(0) % nq
        cur = c_ref[...]
        nxt = jnp.where(i == nq - 1, 0.0, n_ref[...])
        ze = jnp.concatenate([cur, nxt], axis=0) / _pool_count((n_ext, w), i * ts)
        s2 = ze + pltpu.roll(ze, n_ext - 1, axis=0)
        s4 = s2 + pltpu.roll(s2, n_ext - 2, axis=0)
        s8 = s4 + pltpu.roll(s4, n_ext - 4, axis=0)
        s16 = s8 + pltpu.roll(s8, n_ext - 8, axis=0)
        ws = _pool_select((ts, w), s2[:ts], s4[:ts], s8[:ts], s16[:ts])
        du_ref[...] = (ws - cur).astype(BF16)

    row = pl.BlockSpec((ts, w), lambda r: (r, 0))
    return pl.pallas_call(
        body, name="pool_bwd_window", grid=(nb * nq,),
        in_specs=[row, pl.BlockSpec((HALO_POOL, w),
                                    lambda r: (jnp.minimum((r + 1) * (ts // HALO_POOL), n_halo_blocks - 1), 0))],
        out_specs=row, out_shape=SDS((nb * s, w), BF16),
        compiler_params=_params(("parallel",)))(dpooled, dpooled)


def _conv_rows(xe, w_ref):
    return (w_ref[0, 2:3, :] * xe + w_ref[0, 1:2, :] * pltpu.roll(xe, 1, axis=0)
            + w_ref[0, 0:1, :] * pltpu.roll(xe, 2, axis=0) + w_ref[0, 3:4, :])


def ffn_up_gate(x_bf, wup, cw, nb, s, name):
    tm = min(2 * TM, s)
    nq = s // tm
    w = FF_BLOCK_PAD
    hr = 2 * HALO_CONV
    k = x_bf.shape[1]

    def body(xc_ref, xh_ref, wu_ref, wg_ref, cu_ref, cg_ref, act_ref, a_ref, b_ref, hu_ref, hg_ref):
        first = (pl.program_id(1) % nq) == 0
        xc = xc_ref[...]
        xh = xh_ref[...]

        def products(w_ref):
            return (jnp.dot(xc, w_ref[0], preferred_element_type=F32), jnp.dot(xh, w_ref[0], preferred_element_type=F32))

        def conv(hcur, hprev, c_ref, h_out):
            h_out[...] = hcur.astype(BF16)
            xe = jnp.concatenate([jnp.where(first, 0.0, hprev), hcur], axis=0)
            return _conv_rows(xe, c_ref)[hr:]

        pu, pg = products(wu_ref), products(wg_ref)
        cu = conv(*pu, cu_ref, hu_ref)
        cg = conv(*pg, cg_ref, hg_ref)
        sg = _sigmoid(cg)
        a = cg * sg
        act_ref[...] = (a * cu).astype(BF16)
        a_ref[...] = a.astype(BF16)
        b_ref[...] = (cu * (sg * (1.0 + cg * (1.0 - sg)))).astype(BF16)

    def wblock(off):
        return pl.BlockSpec((1, k, w), lambda j, r: (j + off, 0, 0))

    def cblock(off):
        return pl.BlockSpec((1, 8, w), lambda j, r: (j + off, 0, 0))

    tile = pl.BlockSpec((tm, w), lambda j, r: (r, j))
    out = SDS((nb * s, FF_PAIRS * w), BF16)
    return pl.pallas_call(
        body, name=name, grid=(FF_PAIRS, nb * nq),
        in_specs=[pl.BlockSpec((tm, k), lambda j, r: (r, 0)),
                  pl.BlockSpec((hr, k), lambda j, r: (jnp.maximum(r * (tm // hr) - 1, 0), 0)),
                  wblock(0), wblock(FF_PAIRS), cblock(0), cblock(FF_PAIRS)],
        out_specs=[tile] * 5, out_shape=[out] * 5,
        compiler_params=_params(("parallel", "parallel")))(x_bf, x_bf, wup, wup, cw, cw)


def gate_conv_bwd(dact, a, b, hu, hg, cw, nb, s, name):
    ts = min(TS, s)
    nq = s // ts
    w = FF_BLOCK_PAD
    hc = HALO_CONV
    hb = 2 * hc
    n_ext = ts + hc

    def body(dc_ref, dn_ref, ac_ref, an_ref, bc_ref, bn_ref, hu_ref, hg_ref, wu_ref, wg_ref,
             dhu_ref, dhg_ref, dwu_ref, dwg_ref):
        r = pl.program_id(1)
        last = (r % nq) == nq - 1

        def ext(c_ref, n_ref, mask_next=False):
            nxt = n_ref[...].astype(F32)[:hc]
            if mask_next:
                nxt = jnp.where(last, 0.0, nxt)
            return jnp.concatenate([c_ref[...].astype(F32), nxt], axis=0)

        da = ext(dc_ref, dn_ref, mask_next=True)

        def branch(dcv, w_ref, h_ref, dh_ref, dw_ref):
            d0 = dcv[:ts]
            d1 = pltpu.roll(dcv, n_ext - 1, axis=0)[:ts]
            d2 = pltpu.roll(dcv, n_ext - 2, axis=0)[:ts]
            dh_ref[...] = (w_ref[0, 2:3, :] * d0 + w_ref[0, 1:2, :] * d1 + w_ref[0, 0:1, :] * d2).astype(BF16)
            hv = h_ref[...].astype(F32)
            rows = [jnp.sum(d2 * hv, axis=0, keepdims=True), jnp.sum(d1 * hv, axis=0, keepdims=True),
                    jnp.sum(d0 * hv, axis=0, keepdims=True), jnp.sum(d0, axis=0, keepdims=True)]
            sub = lax.broadcasted_iota(jnp.int32, (8, w), 0)
            upd = jnp.zeros((8, w), F32)
            for kk, rv in enumerate(rows):
                upd = jnp.where(sub == kk, rv, upd)

            @pl.when(r == 0)
            def _():
                dw_ref[...] = jnp.zeros_like(dw_ref)

            dw_ref[...] += upd[None]

        branch(da * ext(ac_ref, an_ref), wu_ref, hu_ref, dhu_ref, dwu_ref)
        branch(da * ext(bc_ref, bn_ref), wg_ref, hg_ref, dhg_ref, dwg_ref)

    cur = pl.BlockSpec((ts, w), lambda j, r: (r, j))
    nxt = pl.BlockSpec((hb, w), lambda j, r: (jnp.minimum((r + 1) * (ts // hb), nb * s // hb - 1), j))

    def wspec(off):
        return pl.BlockSpec((1, 8, w), lambda j, r: (j + off, 0, 0))

    p = FF_PAIRS
    dw_spec = pl.BlockSpec((1, 8, w), lambda j, r: (j, 0, 0))
    return pl.pallas_call(
        body, name=name, grid=(p, nb * nq),
        in_specs=[cur, nxt, cur, nxt, cur, nxt, cur, cur, wspec(0), wspec(p)],
        out_specs=[cur, cur, dw_spec, dw_spec],
        out_shape=[SDS((nb * s, p * w), BF16), SDS((nb * s, p * w), BF16), SDS((p, 8, w), F32), SDS((p, 8, w), F32)],
        compiler_params=_params(("parallel", "arbitrary")))(dact, dact, a, a, b, b, hu, hg, cw, cw)


def _tri(n, upper):
    r = lax.broadcasted_iota(jnp.int32, (n, n), 0)
    c = lax.broadcasted_iota(jnp.int32, (n, n), 1)
    return ((r <= c) if upper else (r >= c)).astype(F32)


def fgate_fwd(fl, fb, nb, s):
    tc = min(TC, s)
    nq = s // tc

    def body(fl_ref, fb_ref, f_ref, carry):
        @pl.when(pl.program_id(1) == 0)
        def _():
            carry[...] = jnp.zeros_like(carry)

        z = fl_ref[...] + fb_ref[...]
        logf = jnp.minimum(z, 0.0) - jnp.log(1.0 + jnp.exp(-jnp.abs(z)))
        f_ref[...] = jnp.dot(_tri(tc, False), logf, preferred_element_type=F32,
                             precision=lax.Precision.HIGHEST) + carry[...]
        carry[...] += jnp.sum(logf, axis=0, keepdims=True)

    row = pl.BlockSpec((tc, LANES), lambda b, i: (b * nq + i, 0))
    return pl.pallas_call(
        body, name="fgate_fwd", grid=(nb, nq), in_specs=[row, pl.BlockSpec((1, LANES), lambda b, i: (0, 0))],
        out_specs=row, out_shape=SDS((nb * s, LANES), F32), scratch_shapes=[pltpu.VMEM((1, LANES), F32)],
        compiler_params=_params(("arbitrary", "arbitrary")))(fl, fb)


def fgate_bwd(d_cum_q, d_cum_k, fl, fb, dk, dv, nb, s):
    tc = min(TC, s)
    nq = s // tc

    def body(dfq_ref, dfk_ref, fl_ref, fb_ref, dk_ref, dv_ref, dkvf_ref, dfb_ref, carry):
        b = pl.program_id(0)
        i = pl.program_id(1)

        @pl.when(i == 0)
        def _():
            carry[...] = jnp.zeros_like(carry)

        @pl.when(jnp.logical_and(b == 0, i == 0))
        def _():
            dfb_ref[...] = jnp.zeros_like(dfb_ref)

        dfv = dfq_ref[...] + dfk_ref[...]
        dlog = jnp.dot(_tri(tc, True), dfv, preferred_element_type=F32,
                       precision=lax.Precision.HIGHEST) + carry[...]
        carry[...] += jnp.sum(dfv, axis=0, keepdims=True)
        z = fl_ref[...] + fb_ref[...]
        dfl = dlog / (1.0 + jnp.exp(z))
        dkvf_ref[:, :TOK_WIDTH] = dk_ref[...]
        dkvf_ref[:, TOK_WIDTH:2 * TOK_WIDTH] = dv_ref[...]
        dkvf_ref[:, 2 * TOK_WIDTH:] = dfl.astype(BF16)
        dfb_ref[...] += jnp.sum(dfl, axis=0, keepdims=True)

    def rows(width):
        return pl.BlockSpec((tc, width), lambda b, i: (b * nq + nq - 1 - i, 0))

    row = rows(LANES)
    vec = pl.BlockSpec((1, LANES), lambda b, i: (0, 0))
    return pl.pallas_call(
        body, name="fgate_bwd", grid=(nb, nq), in_specs=[row, row, row, vec, rows(TOK_WIDTH), rows(TOK_WIDTH)],
        out_specs=[rows(KV_COLS_PAD), vec],
        out_shape=[SDS((nb * s, KV_COLS_PAD), BF16), SDS((1, LANES), F32)], scratch_shapes=[pltpu.VMEM((1, LANES), F32)],
        compiler_params=_params(("arbitrary", "arbitrary")))(d_cum_q, d_cum_k, fl, fb, dk, dv)


PAIR = 2 * HEAD_DIM
N_PAIRS = FOX_HEADS // 2


def _lane_put(shape, h, col):
    lane = lax.broadcasted_iota(jnp.int32, shape, 1)
    return jnp.where(lane == h, col, 0.0)


def _half_masks(rows):
    lane = lax.broadcasted_iota(jnp.int32, (rows, PAIR), 1)
    return lane < HEAD_DIM


def _split_pair(x, scale=None):
    if scale is not None:
        x = x * scale
    lo = _half_masks(x.shape[0])
    zero = jnp.zeros_like(x)
    return jnp.where(lo, x, zero), jnp.where(lo, zero, x)


def _to_tile_rows(a, nb, s, tf):
    return a.reshape(nb * s // tf, tf, LANES)[:, :, :16].transpose(0, 2, 1)


def _from_tile_rows(a):
    tiles, _, tf = a.shape
    return jnp.pad(a.transpose(0, 2, 1), ((0, 0), (0, 0), (0, LANES - 16))).reshape(tiles * tf, LANES)


BIAS_TERMS = 3
LOOKAHEAD = 4
FOLLOW_FWD = 1
LOOKAHEAD_BWD = 2
FOLLOW_BWD = 1


def _bias_lane(h):
    return HEAD_DIM if h % 2 == 0 else 0


def _placement():
    rows = jnp.arange(LANES)[:, None]
    cols = jnp.arange(FOX_HEADS * PAIR)[None, :]
    head, lane = cols // PAIR, cols % PAIR
    first = jnp.where(head % 2 == 0, HEAD_DIM, 0)
    term = lane - first
    hit = (term >= 0) & (term < BIAS_TERMS) & (rows == 16 * term + head)
    return hit.astype(BF16)


def fox_prep(kv, fneg, nb, s):
    tf = min(TF, s)
    w = TOK_WIDTH

    def body(k_ref, v_ref, f_ref, pl_ref, ka_ref, vt_ref):
        lane = lax.broadcasted_iota(jnp.int32, (tf, LANES), 1)
        lo = lane < HEAD_DIM
        f = jnp.where(lane < FOX_HEADS, f_ref[...], 0.0)
        hi = f.astype(BF16).astype(F32)
        mid = (f - hi).astype(BF16).astype(F32)
        low = (f - hi - mid).astype(BF16).astype(F32)
        terms = (hi + pltpu.roll(mid, 16, axis=1) + pltpu.roll(low, 32, axis=1)).astype(BF16)
        placed = jnp.dot(terms, pl_ref[...], preferred_element_type=F32).astype(BF16)
        one = jnp.ones((tf, LANES), BF16)
        zero = jnp.zeros((tf, LANES), BF16)
        for p in range(N_PAIRS):
            kp = k_ref[:, p * PAIR:(p + 1) * PAIR] * QK_SCALE
            vp = v_ref[:, p * PAIR:(p + 1) * PAIR]
            he, ho = 2 * p, 2 * p + 1
            ka_ref[:, he * PAIR:(he + 1) * PAIR] = jnp.where(lo, kp, placed[:, he * PAIR:(he + 1) * PAIR])
            ka_ref[:, ho * PAIR:(ho + 1) * PAIR] = jnp.where(lo, placed[:, ho * PAIR:(ho + 1) * PAIR], kp)
            ve = jnp.where(lo, vp, jnp.where(lane == HEAD_DIM, one, zero))
            vo = jnp.where(lo, jnp.where(lane == 0, one, zero), vp)
            vt_ref[0, he * PAIR:(he + 1) * PAIR, :] = ve.astype(F32).T.astype(BF16)
            vt_ref[0, ho * PAIR:(ho + 1) * PAIR, :] = vo.astype(F32).T.astype(BF16)

    return pl.pallas_call(
        body, name="fox_prep", grid=(nb * s // tf,),
        in_specs=[pl.BlockSpec((tf, w), lambda r: (r, 0)), pl.BlockSpec((tf, w), lambda r: (r, 1)),
                  pl.BlockSpec((tf, LANES), lambda r: (r, 0)), pl.BlockSpec((LANES, FOX_HEADS * PAIR), lambda r: (0, 0))],
        out_specs=[pl.BlockSpec((tf, FOX_HEADS * PAIR), lambda r: (r, 0)),
                   pl.BlockSpec((1, FOX_HEADS * PAIR, tf), lambda r: (r, 0, 0))],
        out_shape=[SDS((nb * s, FOX_HEADS * PAIR), BF16), SDS((nb * s // tf, FOX_HEADS * PAIR, tf), BF16)],
        compiler_params=_params(("parallel",)))(kv, kv, fneg, _placement())


def fox_fwd_t(pq, kaug, vaug_t, nb, s):
    tf = min(TF, s)
    n = s // tf
    w = TOK_WIDTH
    wa = FOX_HEADS * PAIR

    def body(q_ref, k_hbm, vt_hbm, ob_ref, of_ref, lse_ref, k_vm, vt_vm, qx_scr, m_scr, acc_scr, sems):
        b = pl.program_id(0)
        i = pl.program_id(1)

        @pl.when(i == 0)
        def _():
            ck = pltpu.make_async_copy(k_hbm.at[pl.ds(pl.multiple_of(b * s, tf), s)], k_vm, sems.at[0])
            cv = pltpu.make_async_copy(vt_hbm.at[pl.ds(b * n, n)], vt_vm, sems.at[1])
            ck.start()
            cv.start()
            ck.wait()
            cv.wait()

        lane = lax.broadcasted_iota(jnp.int32, (tf, PAIR), 1)
        one = jnp.ones((tf, PAIR), BF16)
        zero = jnp.zeros((tf, PAIR), BF16)
        for p in range(N_PAIRS):
            qp = q_ref[:, p * PAIR:(p + 1) * PAIR]
            be, bo = _bias_lane(2 * p), _bias_lane(2 * p + 1)
            ones_e = jnp.where((lane >= be) & (lane < be + BIAS_TERMS), one, zero)
            ones_o = jnp.where((lane >= bo) & (lane < bo + BIAS_TERMS), one, zero)
            qx_scr[2 * p] = jnp.where(lane < HEAD_DIM, qp, ones_e)
            qx_scr[2 * p + 1] = jnp.where(lane < HEAD_DIM, ones_o, qp)
        m_scr[...] = jnp.full(m_scr.shape, NEG_BIG, F32)
        acc_scr[...] = jnp.zeros_like(acc_scr)

        def tile(j, masked):
            ks = pl.multiple_of(j * tf, tf)
            if masked:
                keep = lax.broadcasted_iota(jnp.int32, (tf, tf), 1) >= lax.broadcasted_iota(jnp.int32, (tf, tf), 0)
            def scores(h):
                kx = k_vm[pl.ds(ks, tf), h * PAIR:(h + 1) * PAIR]
                return lax.dot_general(kx, qx_scr[h], NT_DIMS, preferred_element_type=F32)

            def values(h, pr, a):
                pv = jnp.dot(vt_vm[j, h * PAIR:(h + 1) * PAIR, :], pr, preferred_element_type=F32)
                acc_scr[h] = a * acc_scr[h] + pv

            ahead = [scores(h) for h in range(LOOKAHEAD)]
            behind = []
            for h in range(FOX_HEADS):
                sc = ahead.pop(0)
                if h + LOOKAHEAD < FOX_HEADS:
                    ahead.append(scores(h + LOOKAHEAD))
                if masked:
                    sc = jnp.where(keep, sc, NEG_BIG)
                m_prev = m_scr[h]
                m_new = jnp.maximum(m_prev, jnp.max(sc, axis=0, keepdims=True))
                m_scr[h] = m_new
                behind.append((h, jnp.exp(sc - m_new).astype(BF16), jnp.exp(m_prev - m_new)))
                if len(behind) > FOLLOW_FWD:
                    values(*behind.pop(0))
            for item in behind:
                values(*item)

        def step(j, carry):
            tile(j, False)
            return carry

        lax.fori_loop(0, i, step, 0)
        tile(i, True)

        top = lax.broadcasted_iota(jnp.int32, (PAIR, tf), 0) < HEAD_DIM
        sub = lax.broadcasted_iota(jnp.int32, (16, tf), 0)
        lse = jnp.zeros((16, tf), F32)
        for p in range(N_PAIRS):
            he, ho = 2 * p, 2 * p + 1
            le = acc_scr[he, HEAD_DIM:HEAD_DIM + 1, :]
            lod = acc_scr[ho, 0:1, :]
            o = jnp.where(top, acc_scr[he] / le, acc_scr[ho] / lod).T
            ob_ref[:, p * PAIR:(p + 1) * PAIR] = o.astype(BF16)
            of_ref[:, p * PAIR:(p + 1) * PAIR] = o
            lse = jnp.where(sub == he, m_scr[he] + jnp.log(le), lse)
            lse = jnp.where(sub == ho, m_scr[ho] + jnp.log(lod), lse)
        lse_ref[0] = lse

    qrow = lambda b, i: (b * n + i, 0)
    return pl.pallas_call(
        body, name="fox_fwd", grid=(nb, n),
        in_specs=[pl.BlockSpec((tf, w), qrow), ANY_SPEC, ANY_SPEC],
        out_specs=[pl.BlockSpec((tf, w), qrow), pl.BlockSpec((tf, w), qrow),
                   pl.BlockSpec((1, 16, tf), lambda b, i: (b * n + i, 0, 0))],
        out_shape=[SDS((nb * s, w), BF16), SDS((nb * s, w), F32), SDS((nb * n, 16, tf), F32)],
        scratch_shapes=[pltpu.VMEM((s, wa), BF16), pltpu.VMEM((n, wa, tf), BF16),
                        pltpu.VMEM((FOX_HEADS, tf, PAIR), BF16), pltpu.VMEM((FOX_HEADS, 1, tf), F32),
                        pltpu.VMEM((FOX_HEADS, PAIR, tf), F32), pltpu.SemaphoreType.DMA((2,))],
        compiler_params=_params(("arbitrary", "arbitrary")))(pq, kaug, vaug_t)


def fox_delta(dcat, o, nb, s):
    tf = min(TM, s)
    w = TOK_WIDTH

    def body(do_ref, o_ref, dl_ref):
        out = jnp.zeros((tf, LANES), F32)
        for h in range(FOX_HEADS):
            lo, hi = h * HEAD_DIM, (h + 1) * HEAD_DIM
            out = out + _lane_put((tf, LANES), h, jnp.sum(do_ref[:, lo:hi] * o_ref[:, lo:hi], axis=1, keepdims=True))
        dl_ref[...] = out

    row = pl.BlockSpec((tf, w), lambda r: (r, 0))
    return pl.pallas_call(
        body, name="fox_delta", grid=(nb * s // tf,), in_specs=[row, row],
        out_specs=pl.BlockSpec((tf, LANES), lambda r: (r, 0)), out_shape=SDS((nb * s, LANES), F32),
        compiler_params=_params(("parallel",)))(dcat, o)


def fox_bwd(pq, kv, fneg, dcat_bf, lse_rows, delta_rows, nb, s):
    tf = min(TF, s)
    n = s // tf
    w = TOK_WIDTH

    def body(q_hbm, k_ref, v_ref, f_ref, do_hbm, lse_ref, dl_ref, dq_ref, dk_ref, dv_ref, dfk_ref, dfq_ref,
             q_vm, do_vm, km_scr, vm_scr, kt_scr, fk_scr, dk_scr, dv_scr, rs_scr, dq_scr, fq_scr, sems):
        b = pl.program_id(0)
        j = pl.program_id(1)

        @pl.when(j == 0)
        def _():
            rows = pl.ds(pl.multiple_of(b * s, tf), s)
            cq = pltpu.make_async_copy(q_hbm.at[rows, pl.ds(0, w)], q_vm, sems.at[0])
            cd = pltpu.make_async_copy(do_hbm.at[rows, pl.ds(0, w)], do_vm, sems.at[1])
            cq.start()
            cd.start()
            dq_scr[...] = jnp.zeros_like(dq_scr)
            fq_scr[...] = jnp.zeros_like(fq_scr)
            cq.wait()
            cd.wait()

        for p in range(N_PAIRS):
            kp = k_ref[:, p * PAIR:(p + 1) * PAIR] * QK_SCALE
            ke, ko = _split_pair(kp)
            km_scr[2 * p] = ke
            km_scr[2 * p + 1] = ko
            kt_scr[p] = kp.astype(F32).T.astype(BF16)
            ve, vo = _split_pair(v_ref[:, p * PAIR:(p + 1) * PAIR])
            vm_scr[2 * p] = ve
            vm_scr[2 * p + 1] = vo
        for h in range(FOX_HEADS):
            fk_scr[h] = jnp.broadcast_to(f_ref[:, h:h + 1], (tf, tf))
        dk_scr[...] = jnp.zeros_like(dk_scr)
        dv_scr[...] = jnp.zeros_like(dv_scr)
        rs_scr[...] = jnp.zeros_like(rs_scr)

        def tile(i, masked):
            qs = pl.multiple_of(i * tf, tf)
            if masked:
                keep = lax.broadcasted_iota(jnp.int32, (tf, tf), 1) >= lax.broadcasted_iota(jnp.int32, (tf, tf), 0)
            def products(h):
                qp = q_vm[pl.ds(qs, tf), (h // 2) * PAIR:(h // 2 + 1) * PAIR]
                dop = do_vm[pl.ds(qs, tf), (h // 2) * PAIR:(h // 2 + 1) * PAIR]
                return (lax.dot_general(km_scr[h], qp, NT_DIMS, preferred_element_type=F32),
                        lax.dot_general(vm_scr[h], dop, NT_DIMS, preferred_element_type=F32))

            def dependents(h, prb, dsb):
                p = h // 2
                half = slice((h % 2) * HEAD_DIM, (h % 2 + 1) * HEAD_DIM)
                qp = q_vm[pl.ds(qs, tf), p * PAIR:(p + 1) * PAIR]
                dop = do_vm[pl.ds(qs, tf), p * PAIR:(p + 1) * PAIR]
                dv_scr[h] += jnp.dot(prb, dop, preferred_element_type=F32)
                dk_scr[h] += jnp.dot(dsb, qp, preferred_element_type=F32)
                dqt = jnp.dot(kt_scr[p], dsb, preferred_element_type=F32)
                dq_scr[i, p, half, :] += dqt[(h % 2) * HEAD_DIM:(h % 2 + 1) * HEAD_DIM]

            ahead = [products(h) for h in range(LOOKAHEAD_BWD)]
            behind = []
            for h in range(FOX_HEADS):
                sc, dp = ahead.pop(0)
                if h + LOOKAHEAD_BWD < FOX_HEADS:
                    ahead.append(products(h + LOOKAHEAD_BWD))
                sc = sc + fk_scr[h] - lse_ref[i, h:h + 1, :]
                if masked:
                    sc = jnp.where(keep, sc, NEG_BIG)
                pr = jnp.exp(sc)
                ds = pr * (dp - dl_ref[i, h:h + 1, :])
                part = ds[:, :LANES]
                for c in range(1, tf // LANES):
                    part = part + ds[:, c * LANES:(c + 1) * LANES]
                rs_scr[h] += part
                fq_scr[i, h:h + 1, :] += jnp.sum(ds, axis=0, keepdims=True)
                behind.append((h, pr.astype(BF16), ds.astype(BF16)))
                if len(behind) > FOLLOW_BWD:
                    dependents(*behind.pop(0))
            for item in behind:
                dependents(*item)

        def step(i, carry):
            tile(i, False)
            return carry

        tile(j, True)
        for p in range(N_PAIRS):
            dq_ref[:, p * PAIR:(p + 1) * PAIR] = dq_scr[j, p].T.astype(BF16)
        dfq_ref[0] = fq_scr[j]
        lax.fori_loop(j + 1, n, step, 0)

        lo = _half_masks(tf)
        dfk = jnp.zeros((tf, LANES), F32)
        for p in range(N_PAIRS):
            dk = jnp.where(lo, dk_scr[2 * p], dk_scr[2 * p + 1]) * QK_SCALE
            dk_ref[:, p * PAIR:(p + 1) * PAIR] = dk.astype(BF16)
            dv_ref[:, p * PAIR:(p + 1) * PAIR] = jnp.where(lo, dv_scr[2 * p], dv_scr[2 * p + 1]).astype(BF16)
            for h in (2 * p, 2 * p + 1):
                dfk = dfk - _lane_put((tf, LANES), h, jnp.sum(rs_scr[h], axis=1, keepdims=True))
        dfk_ref[...] = dfk

    krow = lambda b, j: (b * n + j, 0)
    rows = pl.BlockSpec((n, 16, tf), lambda b, j: (b, 0, 0))
    tile_out = pl.BlockSpec((tf, w), krow)
    return pl.pallas_call(
        body, name="fox_bwd", grid=(nb, n),
        in_specs=[ANY_SPEC, pl.BlockSpec((tf, w), krow), pl.BlockSpec((tf, w), lambda b, j: (b * n + j, 1)),
                  pl.BlockSpec((tf, LANES), krow), ANY_SPEC, rows, rows],
        out_specs=[tile_out, tile_out, tile_out, pl.BlockSpec((tf, LANES), krow),
                   pl.BlockSpec((1, 16, tf), lambda b, j: (b * n + j, 0, 0))],
        out_shape=[SDS((nb * s, w), BF16), SDS((nb * s, w), BF16), SDS((nb * s, w), BF16), SDS((nb * s, LANES), F32),
                   SDS((nb * n, 16, tf), F32)],
        scratch_shapes=[pltpu.VMEM((s, w), BF16), pltpu.VMEM((s, w), BF16),
                        pltpu.VMEM((FOX_HEADS, tf, PAIR), BF16), pltpu.VMEM((FOX_HEADS, tf, PAIR), BF16),
                        pltpu.VMEM((N_PAIRS, PAIR, tf), BF16), pltpu.VMEM((FOX_HEADS, tf, tf), F32),
                        pltpu.VMEM((FOX_HEADS, tf, PAIR), F32), pltpu.VMEM((FOX_HEADS, tf, PAIR), F32),
                        pltpu.VMEM((FOX_HEADS, tf, LANES), F32), pltpu.VMEM((n, N_PAIRS, PAIR, tf), F32),
                        pltpu.VMEM((n, 16, tf), F32), pltpu.SemaphoreType.DMA((2,))],
        compiler_params=_params(("arbitrary", "arbitrary")))(pq, kv, kv, fneg, dcat_bf, lse_rows, delta_rows)


ADAMW_TILE_ELEMS = 128 * 1024


def reduce_adamw(parts, w, m, v, name):
    layers, r, c = w.shape
    tr, tc = r, c
    for cand in range(16, r, 16):
        if r % cand == 0 and cand * c <= ADAMW_TILE_ELEMS:
            tr = cand
    if tr < min(r, 64) and c % LANES == 0:
        tr = r
        tc = max(cand for cand in range(LANES, c + 1, LANES)
                 if c % cand == 0 and (r * cand <= ADAMW_TILE_ELEMS or cand == LANES))
    c1 = 1.0 - ADAM_B1 ** ADAM_STEP
    c2 = 1.0 - ADAM_B2 ** ADAM_STEP

    def body(*refs):
        p_refs = refs[:layers]
        w_ref, m_ref, v_ref, g_out, d_out, m_out, v_out = refs[layers:]

        def update(p_ref):
            g = p_ref[0].astype(F32)
            for k in range(1, N_DEV):
                g = g + p_ref[k].astype(F32)
            mn = ADAM_B1 * m_ref[0] + (1.0 - ADAM_B1) * g
            vn = ADAM_B2 * v_ref[0] + (1.0 - ADAM_B2) * (g * g)
            g_out[0] = g
            m_out[0] = mn
            v_out[0] = vn
            d_out[0] = -ADAM_LR * ((mn / c1) / (jnp.sqrt(vn / c2) + ADAM_EPS) + ADAM_WD * w_ref[0])

        if layers == 1:
            update(p_refs[0])
        else:
            for layer in range(layers):
                pl.when(pl.program_id(0) == layer)(lambda layer=layer: update(p_refs[layer]))

    row = pl.BlockSpec((1, tr, tc), lambda l, i, j: (l, i, j))
    return pl.pallas_call(
        body, name=name, grid=(layers, r // tr, c // tc),
        in_specs=[pl.BlockSpec((N_DEV, tr, tc), lambda l, i, j: (0, i, j))] * layers + [row, row, row],
        out_specs=[row, row, row, row], out_shape=[SDS((layers, r, c), F32)] * 4,
        compiler_params=_params(("parallel", "parallel", "parallel")))(*parts, w, m, v)


N_PEERS = N_DEV - 1
HBM_SPEC = pl.BlockSpec(memory_space=pltpu.HBM)
SEM_SPEC = pl.BlockSpec(memory_space=pltpu.SEMAPHORE)
ANY_SPEC = pl.BlockSpec(memory_space=pl.ANY)
SPLIT_EFFECT = pltpu.SideEffectType.DATAFLOW_SIDE_EFFECTING


def _peers(with_self=False):
    x, y, c = lax.axis_index("x"), lax.axis_index("y"), lax.axis_index("c")
    peers = []
    for k in range(0 if with_self else 1, N_DEV):
        px = 1 - x if (k >> 2) & 1 else x
        py = 1 - y if (k >> 1) & 1 else y
        pc = 1 - c if k & 1 else c
        peers.append(((px, py, pc), 4 * px + 2 * py + pc))
    return 4 * x + 2 * y + c, peers


def _push(src, dst, send_sems, recv_sems, slot, dev):
    return pltpu.make_async_remote_copy(src_ref=src, dst_ref=dst, send_sem=send_sems.at[slot], recv_sem=recv_sems.at[slot],
                                        device_id=dev, device_id_type=pl.DeviceIdType.MESH)


def _landing_shapes(arrs, scatter):
    return [SDS((N_DEV,) + tuple(a.shape[1:] if sc else a.shape), a.dtype) for a, sc in zip(arrs, scatter)]


def exchange(arrs, scatter, name):
    na = len(arrs)

    def body(*refs):
        ins = refs[:na]
        outs = refs[na:2 * na]
        send_sems, recv_sems, local_sems = refs[2 * na:]
        me, peers = _peers()
        local = []
        remote = []
        for a in range(na):
            lc = pltpu.make_async_copy(ins[a].at[me] if scatter[a] else ins[a], outs[a].at[me], local_sems.at[a])
            lc.start()
            local.append(lc)
            for k, (dev, idx) in enumerate(peers):
                cp = _push(ins[a].at[idx] if scatter[a] else ins[a], outs[a].at[me], send_sems, recv_sems,
                           a * N_PEERS + k, dev)
                cp.start()
                remote.append(cp)
        for a in range(na):
            for k, (dev, idx) in enumerate(peers):
                _push(ins[a].at[me] if scatter[a] else ins[a], outs[a].at[idx], send_sems, recv_sems,
                      a * N_PEERS + k, dev).wait_recv()
        for cp in remote:
            cp.wait_send()
        for lc in local:
            lc.wait()

    return pl.pallas_call(
        body, name=name, in_specs=[HBM_SPEC] * na, out_specs=[HBM_SPEC] * na, out_shape=_landing_shapes(arrs, scatter),
        scratch_shapes=[pltpu.SemaphoreType.DMA((na * N_PEERS,)), pltpu.SemaphoreType.DMA((na * N_PEERS,)),
                        pltpu.SemaphoreType.DMA((na,))])(*arrs)


def exchange_start(arrs, scatter, after, name):
    na = len(arrs)
    lands = [lax.empty(l.shape, l.dtype) for l in _landing_shapes(arrs, scatter)]

    def body(*refs):
        ins = refs[:na]
        land = refs[na:2 * na]
        send_sems, recv_sems = refs[2 * na + 1], refs[2 * na + 2]
        token = refs[-1]
        me, peers = _peers(with_self=True)
        for a in range(na):
            for k, (dev, idx) in enumerate(peers):
                _push(ins[a].at[idx] if scatter[a] else ins[a], land[a].at[me], send_sems, recv_sems,
                      a * N_DEV + k, dev).start()
        token[...] = jnp.zeros_like(token)

    thru = [pltpu.HBM(a.shape, a.dtype) for a in arrs] + [pltpu.HBM(l.shape, l.dtype) for l in lands]
    res = pl.pallas_call(
        body, name=name,
        out_shape=(pltpu.SemaphoreType.DMA((na * N_DEV,)), pltpu.SemaphoreType.DMA((na * N_DEV,)), *thru,
                   SDS((8, LANES), F32)),
        in_specs=[HBM_SPEC] * (2 * na) + [ANY_SPEC],
        out_specs=(SEM_SPEC, SEM_SPEC, *([HBM_SPEC] * (2 * na)), pl.BlockSpec(memory_space=pltpu.VMEM)),
        input_output_aliases={i: 2 + i for i in range(2 * na)},
        compiler_params=pltpu.CompilerParams(has_side_effects=SPLIT_EFFECT),
    )(*[pltpu.with_memory_space_constraint(a, pltpu.HBM) for a in arrs],
      *[pltpu.with_memory_space_constraint(l, pltpu.HBM) for l in lands], after)
    return {"send": res[0], "recv": res[1], "src": res[2:2 + na], "land": res[2 + na:2 + 2 * na],
            "token": res[-1][0, 0], "scatter": scatter}


def exchange_wait(handle, after, name):
    scatter = handle["scatter"]
    na = len(scatter)

    def body(*refs):
        src = refs[:na]
        land = refs[na:2 * na]
        send_sems, recv_sems = refs[2 * na], refs[2 * na + 1]
        me, peers = _peers(with_self=True)
        for a in range(na):
            for k, (dev, idx) in enumerate(peers):
                cp = _push(src[a].at[me] if scatter[a] else src[a], land[a].at[idx], send_sems, recv_sems,
                           a * N_DEV + k, dev)
                cp.wait_send()
                cp.wait_recv()

    ops = list(handle["src"]) + list(handle["land"])
    res = pl.pallas_call(
        body, name=name, out_shape=tuple(pltpu.HBM(o.shape, o.dtype) for o in ops),
        in_specs=[HBM_SPEC] * (2 * na) + [SEM_SPEC, SEM_SPEC, ANY_SPEC], out_specs=tuple([HBM_SPEC] * (2 * na)),
        input_output_aliases={i: i for i in range(2 * na)},
        compiler_params=pltpu.CompilerParams(has_side_effects=SPLIT_EFFECT),
    )(*ops, handle["send"], handle["recv"], after)
    return list(res[na:])


def forward_layer(l, xin, xin_bf, mem_bf, wt, nb, s, ffn_weights=None):
    sv = {"xin_bf": xin_bf}
    memkv = mm_nn(mem_bf, wt["memw"], BF16, f"memkv{l}")
    sv["memkv"] = memkv
    if l == 0:
        proj = mm_nn(xin_bf, wt["win_a"], F32, "proj_a")
        pooled, tok = pool_fwd(proj, wt["pw_bd"], wt["pscale"], nb, s)
        sv["pooled"] = pooled
    else:
        kv = mm_nn(xin_bf, wt["kvw"][:, :2 * TOK_WIDTH], BF16, "kv_proj")
        fl = mm_nn(xin_bf, wt["kvw"][:, 2 * TOK_WIDTH:], F32, "gate_proj")
        fneg = -fgate_fwd(fl, wt["fb"], nb, s)
        proj = mm_nn(xin_bf, wt["wq"], BF16, "proj_b")
        kaug, vaug_t = fox_prep(kv, fneg, nb, s)
        tok, o_f32, lse_rows = fox_fwd_t(proj, kaug, vaug_t, nb, s)
        sv.update(kv=kv, fl=fl, fneg=fneg, o_f32=o_f32, lse_rows=lse_rows)
    sv["proj"] = proj
    cat = memattn_fwd(proj, memkv, tok, nb, s, f"memattn_fwd{l}")
    sv["cat"] = cat
    x1, x1_bf, xh1, rs1 = ln_fwd(xin, cat, wt["wout"], wt["ln1_g"], wt["ln1_b"], f"out_proj_ln1_{l}")
    sv.update(x1_bf=x1_bf, xh1=xh1, rs1=rs1)
    if ffn_weights is not None:
        wt.update(ffn_weights(x1_bf))
    act, ga, gb, hu, hg = ffn_up_gate(x1_bf, wt["wup"], wt["cw"], nb, s, f"ffn_up_gate{l}")
    sv.update(act=act, ga=ga, gb=gb, hu=hu, hg=hg)
    x2, x2_bf, xh2, rs2 = ln_fwd(x1, act, wt["wdown"], wt["ln2_g"], wt["ln2_b"], f"ffn_down_ln2_{l}")
    sv.update(xh2=xh2, rs2=rs2)
    return x2, x2_bf, sv


def backward_layer(l, dy, sv, mem_bf, wt, nb, s, after_ffn=None, after_pool=None, after_in=None, loss_target=None):
    g = {}
    if loss_target is None:
        dr2, dr2_bf, g["ln2_g"], g["ln2_b"] = ln_bwd(dy[0], sv["xh2"], sv["rs2"], wt["ln2_g"], f"ln2_bwd{l}",
                                                     dy_scale=dy[1], products=dy[2])
    else:
        dr2, dr2_bf, g["ln2_g"], g["ln2_b"], g["loss_row"] = loss_ln_bwd(sv["xh2"], sv["rs2"], wt["ln2_g"], wt["ln2_b"],
                                                                         loss_target, f"loss_ln2_bwd{l}")
    dact = mm_nn(dr2_bf, wt["wdown"], BF16, f"ffn_down_dx{l}", trans_b=0)
    g["wdown"] = mm_tn(sv["act"], dr2_bf, f"ffn_down_dw{l}")
    dh_u, dh_g, dcw_u, dcw_g = gate_conv_bwd(dact, sv["ga"], sv["gb"], sv["hu"], sv["hg"], wt["cw"], nb, s,
                                             f"gate_conv_bwd{l}")
    g["cw"] = jnp.concatenate([dcw_u, dcw_g], axis=0)
    g["wup"] = (mm_tn(dh_u, sv["x1_bf"], f"ffn_up_dw_u{l}"), mm_tn(dh_g, sv["x1_bf"], f"ffn_up_dw_g{l}"))
    ln1_g = wt["ln1_g"] if after_ffn is None else wt["ln1_g"] + after_ffn(g, dr2)
    dr1, dr1_bf, g["ln1_g"], g["ln1_b"] = ln_bwd(dr2, sv["xh1"], sv["rs1"], ln1_g, f"ffn_up_dx_ln1_bwd{l}",
                                                 dy_scale=DN_ALPHA, products=[(dh_u, wt["wup"], 0), (dh_g, wt["wup"], 1)])
    dcat, dcat_bf = mm_nn(dr1_bf, wt["wout"], F32, f"out_proj_dx{l}", also_bf16=True, trans_b=0)
    g["wout"] = mm_tn(sv["cat"], dr1_bf, f"out_proj_dw{l}")
    if l == 0:
        dmixed, dpooled, g["pscale"] = pool_bwd_mix(dcat, sv["pooled"], wt["pw_bd"], wt["pscale"], nb, s)
        g["pw_full"] = mm_tn(sv["pooled"], dmixed, "pool_dw", out_dtype=F32)
        dtok = pool_bwd_window(dpooled, nb, s)
    else:
        delta = fox_delta(dcat, sv["o_f32"], nb, s)
        tf = min(TF, s)
        dtok, dk, dv, dfcum_k, dfq_rows = fox_bwd(sv["proj"], sv["kv"], sv["fneg"], dcat_bf,
                                                  sv["lse_rows"], _to_tile_rows(delta, nb, s, tf), nb, s)
    dproj, dmemkv = memattn_bwd(sv["proj"], sv["memkv"], dcat, dtok, nb, s, f"memattn_bwd{l}")
    g["memw"] = mm_tn(mem_bf, dmemkv, f"memkv_dw{l}")
    if l == 0:
        win_a = wt["win_a"] if after_pool is None else wt["win_a"] + after_pool(g, dproj).astype(BF16)
        g["win_a"] = mm_tn(sv["xin_bf"], dproj, "proj_a_dw")
        if after_in is not None:
            win_a = win_a + after_in(g).astype(BF16)
        dx = mm_nn(dproj, win_a, F32, "proj_a_dx", addend=dr1, add_scale=DN_ALPHA, trans_b=0)
    else:
        dkvf, g["fb"] = fgate_bwd(_from_tile_rows(dfq_rows), dfcum_k, sv["fl"], wt["fb"], dk, dv, nb, s)
        dx = (dr1, DN_ALPHA, [(dproj, wt["wq"], 0), (dkvf, wt["kvw"], 0)])
        g["wq"] = mm_tn(sv["xin_bf"], dproj, "proj_b_dw")
        g["kvw"] = mm_tn(sv["xin_bf"], dkvf, "kv_proj_dw")
    return dx, g


def pack_replicated(pool_w, ln1_g, ln1_b, ln2_g, ln2_b, conv_b, f_b):
    cb = jnp.pad(conv_b, ((0, 0), (0, 6144 - 5504))).reshape(12, D_MODEL)
    fb = jnp.pad(f_b.reshape(1, FOX_HEADS), ((0, 3), (0, D_MODEL - FOX_HEADS)))
    return jnp.concatenate([pool_w.reshape(144, D_MODEL), ln1_g, ln1_b, ln2_g, ln2_b, cb, fb], axis=0)


def unpack_replicated(buf):
    pool_w = buf[:144].reshape(1, 4, POOL_GROUP, POOL_GROUP)
    ln = [buf[144 + 2 * k:146 + 2 * k] for k in range(4)]
    conv_b = buf[152:164].reshape(2, 6144)[:, :5504]
    f_b = buf[164, :FOX_HEADS]
    return pool_w, ln[0], ln[1], ln[2], ln[3], conv_b, f_b


def _pad_ff(a, axis):
    zeros = jnp.zeros(a.shape[:axis] + (FF_ROWS_PAD - FF_ROWS,) + a.shape[axis + 1:], a.dtype)
    halves = [lax.slice_in_dim(a, h * FF_ROWS, (h + 1) * FF_ROWS, axis=axis) for h in range(2)]
    return jnp.concatenate([halves[0], zeros, halves[1], zeros], axis=axis)


def _unpad_ff(a, axis):
    return jnp.concatenate([lax.slice_in_dim(a, h * FF_ROWS_PAD, h * FF_ROWS_PAD + FF_ROWS, axis=axis) for h in range(2)],
                           axis=axis)


def pack_small(conv_w, pool_scale):
    buf = jnp.zeros((16, FF_BLOCK_PAD), F32)
    buf = lax.dynamic_update_slice(buf, _pad_ff(conv_w.reshape(DEPTH * 3, FF_BLOCK), 1), (0, 0))
    return lax.dynamic_update_slice(buf, pool_scale, (8, 0))


def _block_diag(pw):
    out = jnp.zeros((TOK_WIDTH, TOK_WIDTH), pw.dtype)
    for g in range(4):
        out = lax.dynamic_update_slice(out, pw[g], (g * POOL_GROUP, g * POOL_GROUP))
    return out


def layer_shards(l, sq_a, sq_b, mem_w_kv, ffn_w_up, ffn_w_down):
    wdown = jnp.pad(ffn_w_down[l], ((0, FF_ROWS_PAD - FF_ROWS), (0, 0)))
    return [sq_a[0].astype(BF16), sq_b[0].astype(BF16), mem_w_kv[l].astype(BF16), _pad_ff(ffn_w_up[l], 1).astype(BF16),
            wdown.astype(BF16)]


def mixer_weights(l, gath, ln1_g, ln1_b, ln2_g, ln2_b):
    w_out = gath[1].reshape(D_MODEL, D_MODEL)
    wt = {"memw": gath[2].reshape(D_MODEL, 2 * MEM_WIDTH), "wout": w_out,
          "ln1_g": ln1_g[l:l + 1], "ln1_b": ln1_b[l:l + 1], "ln2_g": ln2_g[l:l + 1], "ln2_b": ln2_b[l:l + 1]}
    return wt, gath[0].reshape(D_MODEL, D_MODEL)


def ffn_weights(l, wup_g, wdown_g, small, conv_b):
    cb = _pad_ff(conv_b[l].reshape(N_DEV, FF_BLOCK), 1)
    cw = jnp.concatenate([small[:, 3 * l:3 * l + 3, :], cb[:, None, :], jnp.zeros((N_DEV, 4, FF_BLOCK_PAD), F32)], axis=1)
    return {"wup": wup_g, "wdown": wdown_g.reshape(FF_PAIRS * FF_BLOCK_PAD, D_MODEL), "cw": cw}


def mixer_grad_blocks(g, w_in_grad):
    blocks = [] if w_in_grad is None else [w_in_grad.reshape(N_DEV, 128, D_MODEL)]
    blocks += [g["wout"].reshape(N_DEV, 128, D_MODEL), g["memw"].reshape(N_DEV, 128, 2 * MEM_WIDTH)]
    return [b.astype(BF16) for b in blocks]


def ffn_grad_blocks(g):
    wdown = g["wdown"].reshape(N_DEV, FF_ROWS_PAD, D_MODEL)[:, :FF_ROWS]
    halves = [_unpad_ff(h.reshape(FF_PAIRS, FF_BLOCK_PAD, D_MODEL), 1) for h in g["wup"]]
    wup_t = jnp.concatenate(halves, axis=0)
    return [wup_t.astype(BF16), wdown.astype(BF16)]


def small_grad_blocks(g0, g1):
    taps = jnp.stack([g0["cw"][:, :3, :], g1["cw"][:, :3, :]], axis=1).reshape(N_DEV, DEPTH * 3, FF_BLOCK_PAD)
    small = jnp.zeros((N_DEV, 16, FF_BLOCK_PAD), F32)
    small = lax.dynamic_update_slice(small, taps, (0, 0, 0))
    return lax.dynamic_update_slice(small, g0["pscale"].reshape(N_DEV, 1, 96), (0, 8, 0))


def replicated_grads(g0, g1):
    pw = jnp.stack([g0["pw_full"][k * POOL_GROUP:(k + 1) * POOL_GROUP, k * POOL_GROUP:(k + 1) * POOL_GROUP] for k in range(4)])
    conv_b = jnp.stack([_unpad_ff(g_["cw"][:, 3, :], 1).reshape(N_DEV * FF_BLOCK) for g_ in (g0, g1)])
    ln = [jnp.concatenate([g0[n], g1[n]], axis=0) for n in ("ln1_g", "ln1_b", "ln2_g", "ln2_b")]
    return pack_replicated(pw[None], ln[0], ln[1], ln[2], ln[3], conv_b, g1["fb"][0, :FOX_HEADS])


def kernel(x, mem, a_w_in, a_pool_w, a_pool_scale, a_w_out, b_w_q, b_w_out, kv_w, f_b, mem_w_kv, ln1_g, ln1_b, ln2_g, ln2_b, ffn_w_up, ffn_conv_w, ffn_conv_b, ffn_w_down, loss_target, m_a_w_in, m_a_pool_w, m_a_pool_scale, m_a_w_out, m_b_w_q, m_b_w_out, m_kv_w, m_f_b, m_mem_w_kv, m_ln1_g, m_ln1_b, m_ln2_g, m_ln2_b, m_ffn_w_up, m_ffn_conv_w, m_ffn_conv_b, m_ffn_w_down, v_a_w_in, v_a_pool_w, v_a_pool_scale, v_a_w_out, v_b_w_q, v_b_w_out, v_kv_w, v_f_b, v_mem_w_kv, v_ln1_g, v_ln1_b, v_ln2_g, v_ln2_b, v_ffn_w_up, v_ffn_conv_w, v_ffn_conv_b, v_ffn_w_down):
    nb, s, d = x.shape
    t = nb * s
    x2d, mem_bf, target = x.reshape(t, d), mem.reshape(nb * MEM_LEN, d).astype(BF16), loss_target.reshape(t, d)

    shards0 = layer_shards(0, a_w_in, a_w_out, mem_w_kv, ffn_w_up, ffn_w_down)
    shards1 = layer_shards(1, b_w_q, b_w_out, mem_w_kv, ffn_w_up, ffn_w_down)
    shards1.append(jnp.pad(kv_w, ((0, 0), (0, KV_COLS_PAD - KV_COLS))).astype(BF16))
    gath0 = exchange(shards0[:3] + [pack_small(ffn_conv_w, a_pool_scale)], [False] * 4, "gather_w0_mixer")
    pending = {"ffn0": exchange_start(shards0[3:], [False] * 2, gath0[0], "gather_w0_ffn_start")}
    small = gath0[3]
    wt0, w_in = mixer_weights(0, gath0, ln1_g + pending["ffn0"]["token"], ln1_b, ln2_g, ln2_b)
    pw_bd = _block_diag(a_pool_w[0])
    wt0.update(win_a=w_in, pw_bd=pw_bd.astype(BF16),
               pscale=small[:, 8, :96].reshape(1, TOK_WIDTH) + pending["ffn0"]["token"])

    def ffn0_weights(x1_bf):
        got = exchange_wait(pending["ffn0"], x1_bf, "gather_w0_ffn_wait")
        pending["w1"] = exchange_start(shards1, [False] * 6, got[0], "gather_w1_start")
        w = ffn_weights(0, got[0], got[1], small, ffn_conv_b)
        w["cw"] = w["cw"] + pending["w1"]["token"]
        return w

    x1, x1_bf, sv0 = forward_layer(0, x2d, x2d, mem_bf, wt0, nb, s, ffn_weights=ffn0_weights)
    gath1 = exchange_wait(pending["w1"], x1_bf, "gather_w1_wait")
    wt1, w_q = mixer_weights(1, gath1, ln1_g, ln1_b, ln2_g, ln2_b)
    wt1.update(ffn_weights(1, gath1[3], gath1[4], small, ffn_conv_b))
    kvw = gath1[5].reshape(D_MODEL, KV_COLS_PAD)
    wt1.update(wq=w_q, kvw=kvw,
               fb=jnp.pad(f_b.reshape(1, FOX_HEADS), ((0, 0), (0, LANES - FOX_HEADS))))
    _, _, sv1 = forward_layer(1, x1, x1_bf, mem_bf, wt1, nb, s)

    dx1, g1 = backward_layer(1, None, sv1, mem_bf, wt1, nb, s, loss_target=target)
    loss = lax.psum(g1["loss_row"][0, 0], ("x", "y", "c"))
    blocks1 = (mixer_grad_blocks(g1, g1["wq"]) + ffn_grad_blocks(g1)
               + [g1["kvw"][:, :KV_COLS].reshape(N_DEV, 128, KV_COLS).astype(BF16)])
    pending["g1"] = exchange_start(blocks1, [True] * 6, dx1[0], "scatter_g1_start")
    wt0["ln2_g"] = wt0["ln2_g"] + pending["g1"]["token"]

    def after_ffn0(g, dxm):
        pending["gf0"] = exchange_start(ffn_grad_blocks(g), [True] * 2, dxm, "scatter_g0_ffn_start")
        return pending["gf0"]["token"]

    def after_pool0(g, x):
        blocks = mixer_grad_blocks(g, None) + [small_grad_blocks(g, g1), replicated_grads(g, g1)]
        pending["gm0"] = exchange_start(blocks, [True] * 3 + [False], x, "scatter_g0_mixer_start")
        return pending["gm0"]["token"]

    def after_in0(g):
        pending["gin"] = exchange_start([g["win_a"].reshape(N_DEV, 128, D_MODEL)], [True], g["win_a"],
                                        "scatter_g0_in_start")
        return pending["gin"]["token"]

    grad_x, g0 = backward_layer(0, dx1, sv0, mem_bf, wt0, nb, s, after_ffn=after_ffn0, after_pool=after_pool0,
                                after_in=after_in0)
    parts_f0 = exchange_wait(pending["gf0"], grad_x, "scatter_g0_ffn_wait")
    parts1 = exchange_wait(pending["g1"], parts_f0[0], "scatter_g1_wait")

    res = {}

    def upd(nm, parts, w2, m2, v2):
        res[nm] = reduce_adamw(parts, w2, m2, v2, f"adamw_{nm}")

    upd("b_w_q", [parts1[0]], b_w_q, m_b_w_q, v_b_w_q)
    upd("b_w_out", [parts1[1]], b_w_out, m_b_w_out, v_b_w_out)
    upd("kv_w", [parts1[5]], kv_w[None], m_kv_w[None], v_kv_w[None])
    upd("ffn_w_up", [parts_f0[0], parts1[3]], *[a.transpose(0, 2, 1) for a in (ffn_w_up, m_ffn_w_up, v_ffn_w_up)])
    res["ffn_w_up"] = [o.transpose(0, 2, 1) for o in res["ffn_w_up"]]
    upd("ffn_w_down", [parts_f0[1], parts1[4]], ffn_w_down, m_ffn_w_down, v_ffn_w_down)
    parts_m0 = exchange_wait(pending["gm0"], res["ffn_w_down"][0], "scatter_g0_mixer_wait")
    upd("a_w_out", [parts_m0[0]], a_w_out, m_a_w_out, v_a_w_out)
    upd("mem_w_kv", [parts_m0[1], parts1[2]], mem_w_kv, m_mem_w_kv, v_mem_w_kv)
    upd("small", [parts_m0[2]], pack_small(ffn_conv_w, a_pool_scale)[None], pack_small(m_ffn_conv_w, m_a_pool_scale)[None],
        pack_small(v_ffn_conv_w, v_a_pool_scale)[None])
    upd("replicated", [parts_m0[3]], pack_replicated(a_pool_w, ln1_g, ln1_b, ln2_g, ln2_b, ffn_conv_b, f_b)[None],
        pack_replicated(m_a_pool_w, m_ln1_g, m_ln1_b, m_ln2_g, m_ln2_b, m_ffn_conv_b, m_f_b)[None],
        pack_replicated(v_a_pool_w, v_ln1_g, v_ln1_b, v_ln2_g, v_ln2_b, v_ffn_conv_b, v_f_b)[None])
    parts_in = exchange_wait(pending["gin"], res["replicated"][0], "scatter_g0_in_wait")
    upd("a_w_in", [parts_in[0]], a_w_in, m_a_w_in, v_a_w_in)

    res["kv_w"] = [o[0] for o in res["kv_w"]]
    res["ffn_conv_w"] = [_unpad_ff(o[0, :DEPTH * 3, :], 1).reshape(DEPTH, 3, FF_BLOCK) for o in res["small"]]
    res["a_pool_scale"] = [o[0, 8:9, :96] for o in res["small"]]
    rep_names = ["a_pool_w", "ln1_g", "ln1_b", "ln2_g", "ln2_b", "ffn_conv_b", "f_b"]
    for nm in rep_names:
        res[nm] = []
    for o in res["replicated"]:
        for nm, val in zip(rep_names, unpack_replicated(o[0])):
            res[nm].append(val)

    order = ["a_w_in", "a_pool_w", "a_pool_scale", "a_w_out", "b_w_q", "b_w_out", "kv_w", "f_b", "mem_w_kv",
             "ln1_g", "ln1_b", "ln2_g", "ln2_b", "ffn_w_up", "ffn_conv_w", "ffn_conv_b", "ffn_w_down"]
    out = [loss, grad_x.reshape(nb, s, d)]
    for kind in range(4):
        out.extend(res[nm][kind] for nm in order)
    return tuple(out)
```

```python
import jax
import jax.numpy as jnp
from jax import lax
from jax.experimental import pallas as pl
from jax.experimental.pallas import tpu as pltpu

F32 = jnp.float32
BF16 = jnp.bfloat16
SDS = jax.ShapeDtypeStruct

N_DEV = 8
D_MODEL = 1024
TOK_WIDTH = 768
MEM_WIDTH = 256
MEM_LEN = 256
MEM_HEADS = 4
HEAD_DIM = 64
FOX_HEADS = 12
POOL_GROUP = 192
FF_BLOCK = 688
FF_BLOCK_PAD = 768
FF_PAIRS = 4
FF_ROWS = 344
FF_ROWS_PAD = FF_BLOCK_PAD // 2
KV_COLS = 1548
KV_COLS_PAD = 1664
LANES = 128
DEPTH = 2
DN_ALPHA = (2.0 * DEPTH) ** 0.25
LN_EPS = 1e-5
QK_SCALE = HEAD_DIM ** -0.5
NEG_BIG = -1e30

ADAM_LR = 0.001
ADAM_B1 = 0.9
ADAM_B2 = 0.999
ADAM_EPS = 1e-08
ADAM_WD = 0.01
ADAM_STEP = 10

VMEM_LIMIT_BYTES = 56 * 1024 * 1024
MM_BLOCK_BYTES = 6 * 1024 * 1024
TM = 512
TS = 256
TF = 256
TC = 256
HALO_POOL = 16
HALO_CONV = 8

NT_DIMS = (((1,), (1,)), ((), ()))
TN_DIMS = (((0,), (0,)), ((), ()))


def _params(sem=None):
    return pltpu.CompilerParams(dimension_semantics=sem, vmem_limit_bytes=VMEM_LIMIT_BYTES)


def _sigmoid(z):
    return 1.0 / (1.0 + jnp.exp(-z))


def _pick_tn(n):
    if n <= 2048:
        return n
    for t in (1024, 768, 512, 256, 128):
        if n % t == 0:
            return t
    return n


def mm_nn(a, b, out_dtype, name, addend=None, add_scale=1.0, also_bf16=False, trans_b=None):
    m, k = a.shape
    n = b.shape[1] if trans_b is None else b.shape[0]
    tm = min(TM, m)
    tn = n
    while k * tn * 2 > MM_BLOCK_BYTES or tm * tn * 4 > MM_BLOCK_BYTES:
        tn //= 2
    chunk = tn if tn <= 2048 else _pick_tn(tn)
    has_add = addend is not None

    def body(*refs):
        a_ref, b_ref = refs[0], refs[1]
        c_ref = refs[2] if has_add else None
        o_ref = refs[3] if has_add else refs[2]
        ob_ref = refs[-1] if also_bf16 else None
        av = a_ref[...].astype(BF16)
        for c in range(tn // chunk):
            cols = slice(c * chunk, (c + 1) * chunk)
            if trans_b is None:
                r = jnp.dot(av, b_ref[:, cols].astype(BF16), preferred_element_type=F32)
            else:
                r = lax.dot_general(av, b_ref[cols, :].astype(BF16), NT_DIMS, preferred_element_type=F32)
            if has_add:
                r = r + add_scale * c_ref[:, cols]
            o_ref[:, cols] = r.astype(out_dtype)
            if also_bf16:
                ob_ref[:, cols] = r.astype(BF16)

    b_spec = (pl.BlockSpec((k, tn), lambda j, i: (0, j)) if trans_b is None
              else pl.BlockSpec((tn, k), lambda j, i: (j, trans_b)))
    in_specs = [pl.BlockSpec((tm, k), lambda j, i: (i, 0)), b_spec]
    ops = [a, b]
    tile = pl.BlockSpec((tm, tn), lambda j, i: (i, j))
    if has_add:
        in_specs.append(tile)
        ops.append(addend)
    out_shape = [SDS((m, n), out_dtype)]
    out_specs = [tile]
    if also_bf16:
        out_shape.append(SDS((m, n), BF16))
        out_specs.append(tile)
    res = pl.pallas_call(
        body, name=name, grid=(n // tn, m // tm), in_specs=in_specs, out_specs=out_specs, out_shape=out_shape,
        compiler_params=_params(("parallel", "parallel")))(*ops)
    return tuple(res) if also_bf16 else res[0]


def mm_tn(a, b, name, blocked=False, out_dtype=BF16):
    t, m = a.shape
    _, n = b.shape
    tt = min(4 * TM, t)
    tm = 1024 if m % 1024 == 0 else m
    tn = FF_BLOCK_PAD if blocked else _pick_tn(n)
    nt = t // tt
    block = (1, tm, tn) if blocked else (tm, tn)
    in_place = out_dtype == F32

    def body(a_ref, b_ref, o_ref, *scratch):
        acc_ref = o_ref if in_place else scratch[0]
        kk = pl.program_id(2)
        r = lax.dot_general(a_ref[...].astype(BF16), b_ref[...].astype(BF16), TN_DIMS, preferred_element_type=F32)
        if blocked:
            r = r[None]

        @pl.when(kk == 0)
        def _():
            acc_ref[...] = r

        @pl.when(kk != 0)
        def _():
            acc_ref[...] += r

        if not in_place:
            @pl.when(kk == nt - 1)
            def _():
                o_ref[...] = acc_ref[...].astype(out_dtype)

    if blocked:
        out_shape = SDS((n // tn, m, tn), out_dtype)
        out_spec = pl.BlockSpec(block, lambda i, j, kk: (j, i, 0))
    else:
        out_shape = SDS((m, n), out_dtype)
        out_spec = pl.BlockSpec(block, lambda i, j, kk: (i, j))
    return pl.pallas_call(
        body, name=name, grid=(m // tm, n // tn, nt),
        in_specs=[pl.BlockSpec((tt, tm), lambda i, j, kk: (kk, i)), pl.BlockSpec((tt, tn), lambda i, j, kk: (kk, j))],
        out_specs=out_spec, out_shape=out_shape, scratch_shapes=[] if in_place else [pltpu.VMEM(block, F32)],
        compiler_params=_params(("parallel", "parallel", "arbitrary")))(a, b)


def ln_fwd(xprev, a, w, g, b, name):
    t, d = xprev.shape
    k = a.shape[1]
    tm = min(TM, t)

    def body(xp_ref, a_ref, w_ref, g_ref, b_ref, y_ref, yb_ref, xh_ref, rs_ref):
        r = DN_ALPHA * xp_ref[...] + jnp.dot(a_ref[...], w_ref[...], preferred_element_type=F32)
        mu = jnp.mean(r, axis=1, keepdims=True)
        xc = r - mu
        var = jnp.mean(xc * xc, axis=1, keepdims=True)
        rstd = lax.rsqrt(var + LN_EPS)
        xh = xc * rstd
        y = xh * g_ref[...] + b_ref[...]
        y_ref[...] = y
        yb_ref[...] = y.astype(BF16)
        xh_ref[...] = xh
        rs_ref[...] = jnp.broadcast_to(rstd, (tm, LANES))

    row = pl.BlockSpec((tm, d), lambda i: (i, 0))
    vec = pl.BlockSpec((1, d), lambda i: (0, 0))
    return pl.pallas_call(
        body, name=name, grid=(t // tm,),
        in_specs=[row, pl.BlockSpec((tm, k), lambda i: (i, 0)), pl.BlockSpec((k, d), lambda i: (0, 0)), vec, vec],
        out_specs=[row, row, row, pl.BlockSpec((tm, LANES), lambda i: (i, 0))],
        out_shape=[SDS((t, d), F32), SDS((t, d), BF16), SDS((t, d), F32), SDS((t, LANES), F32)],
        compiler_params=_params(("parallel",)))(xprev, a, w, g, b)


def ln_bwd(dy, xhat, rstd, g, name, products=(), dy_scale=1.0):
    t, d = dy.shape
    np_ = len(products)
    tm = min(TM if sum(a.shape[1] for a, _, _ in products) <= 4096 else TS, t)

    def body(*refs):
        prod_refs = refs[:2 * np_]
        dy_ref, xh_ref, rs_ref, g_ref, dr_ref, drb_ref, dg_ref, db_ref = refs[2 * np_:]
        i = pl.program_id(0)
        dyv = dy_ref[...] if dy_scale == 1.0 else dy_scale * dy_ref[...]
        for p in range(np_):
            a_ref, w_ref = prod_refs[2 * p], prod_refs[2 * p + 1]
            if len(w_ref.shape) == 2:
                dyv = dyv + lax.dot_general(a_ref[...], w_ref[...], NT_DIMS, preferred_element_type=F32)
            else:
                kb = w_ref.shape[2]
                for c in range(w_ref.shape[0]):
                    dyv = dyv + lax.dot_general(a_ref[:, c * kb:(c + 1) * kb], w_ref[c], NT_DIMS,
                                                preferred_element_type=F32)
        xh = xh_ref[...]
        dxh = dyv * g_ref[...]
        m1 = jnp.mean(dxh, axis=1, keepdims=True)
        m2 = jnp.mean(dxh * xh, axis=1, keepdims=True)
        dr = rs_ref[:, 0:1] * (dxh - m1 - xh * m2)
        dr_ref[...] = dr
        drb_ref[...] = dr.astype(BF16)

        @pl.when(i == 0)
        def _():
            dg_ref[...] = jnp.zeros_like(dg_ref)
            db_ref[...] = jnp.zeros_like(db_ref)

        dg_ref[...] += jnp.sum(dyv * xh, axis=0, keepdims=True)
        db_ref[...] += jnp.sum(dyv, axis=0, keepdims=True)

    row = pl.BlockSpec((tm, d), lambda i: (i, 0))
    vec = pl.BlockSpec((1, d), lambda i: (0, 0))
    in_specs = [row, row, pl.BlockSpec((tm, LANES), lambda i: (i, 0)), vec]
    ops = [dy, xhat, rstd, g]
    for a, w, col in reversed(products):
        k = a.shape[1]
        if w.ndim == 2:
            w_spec = pl.BlockSpec((d, k), lambda i, col=col: (0, col))
        else:
            w_spec = pl.BlockSpec((k // w.shape[2], d, w.shape[2]), lambda i, col=col: (col, 0, 0))
        in_specs = [pl.BlockSpec((tm, k), lambda i: (i, 0)), w_spec] + in_specs
        ops = [a, w] + ops
    return pl.pallas_call(
        body, name=name, grid=(t // tm,), in_specs=in_specs, out_specs=[row, row, vec, vec],
        out_shape=[SDS((t, d), F32), SDS((t, d), BF16), SDS((1, d), F32), SDS((1, d), F32)],
        compiler_params=_params(("arbitrary",)))(*ops)


def loss_ln_bwd(xhat, rstd, g, beta, target, name):
    t, d = xhat.shape
    tm = min(TM, t)
    nsteps = t // tm

    def body(xh_ref, rs_ref, g_ref, b_ref, t_ref, dr_ref, drb_ref, dg_ref, db_ref, l_ref, acc):
        i = pl.program_id(0)
        xh = xh_ref[...]
        diff = xh * g_ref[...] + b_ref[...] - t_ref[...]
        dyv = diff * (1.0 / d)
        dxh = dyv * g_ref[...]
        m1 = jnp.mean(dxh, axis=1, keepdims=True)
        m2 = jnp.mean(dxh * xh, axis=1, keepdims=True)
        dr = rs_ref[:, 0:1] * (dxh - m1 - xh * m2)
        dr_ref[...] = dr
        drb_ref[...] = dr.astype(BF16)

        @pl.when(i == 0)
        def _():
            dg_ref[...] = jnp.zeros_like(dg_ref)
            db_ref[...] = jnp.zeros_like(db_ref)
            acc[...] = jnp.zeros_like(acc)

        dg_ref[...] += jnp.sum(dyv * xh, axis=0, keepdims=True)
        db_ref[...] += jnp.sum(dyv, axis=0, keepdims=True)
        acc[...] += jnp.sum(diff * diff, axis=0, keepdims=True)

        @pl.when(i == nsteps - 1)
        def _():
            tot = jnp.sum(acc[...], axis=1, keepdims=True) * (0.5 / d)
            l_ref[...] = jnp.broadcast_to(tot, (1, LANES))

    row = pl.BlockSpec((tm, d), lambda i: (i, 0))
    vec = pl.BlockSpec((1, d), lambda i: (0, 0))
    return pl.pallas_call(
        body, name=name, grid=(nsteps,),
        in_specs=[row, pl.BlockSpec((tm, LANES), lambda i: (i, 0)), vec, vec, row],
        out_specs=[row, row, vec, vec, pl.BlockSpec((1, LANES), lambda i: (0, 0))],
        out_shape=[SDS((t, d), F32), SDS((t, d), BF16), SDS((1, d), F32), SDS((1, d), F32), SDS((1, LANES), F32)],
        scratch_shapes=[pltpu.VMEM((1, d), F32)],
        compiler_params=_params(("arbitrary",)))(xhat, rstd, g, beta, target)


def memattn_fwd(proj, memkv, tok, nb, s, name):
    ts = min(TM, s)
    nq = s // ts

    def body(q_ref, kv_ref, tok_ref, o_ref):
        o_ref[:, :TOK_WIDTH] = tok_ref[...]
        top = lax.broadcasted_iota(jnp.int32, (PAIR, ts), 0) < HEAD_DIM
        scores = []
        for p in range(MEM_HEADS // 2):
            qp = q_ref[:, p * PAIR:(p + 1) * PAIR].astype(BF16)
            ke, ko = _split_pair(kv_ref[:, p * PAIR:(p + 1) * PAIR], QK_SCALE)
            scores.append([lax.dot_general(km, qp, NT_DIMS, preferred_element_type=F32) for km in (ke, ko)])
        for p in range(MEM_HEADS // 2):
            vt = kv_ref[:, MEM_WIDTH + p * PAIR:MEM_WIDTH + (p + 1) * PAIR].astype(F32).T.astype(BF16)
            outs = []
            for sc in scores[p]:
                e = jnp.exp(sc - jnp.max(sc, axis=0, keepdims=True))
                pr = e / jnp.sum(e, axis=0, keepdims=True)
                outs.append(jnp.dot(vt, pr.astype(BF16), preferred_element_type=F32))
            o_ref[:, TOK_WIDTH + p * PAIR:TOK_WIDTH + (p + 1) * PAIR] = jnp.where(top, outs[0], outs[1]).T.astype(BF16)

    return pl.pallas_call(
        body, name=name, grid=(nb, nq),
        in_specs=[pl.BlockSpec((ts, MEM_WIDTH), lambda b, i: (b * nq + i, 3)),
                  pl.BlockSpec((MEM_LEN, 2 * MEM_WIDTH), lambda b, i: (b, 0)),
                  pl.BlockSpec((ts, TOK_WIDTH), lambda b, i: (b * nq + i, 0))],
        out_specs=pl.BlockSpec((ts, TOK_WIDTH + MEM_WIDTH), lambda b, i: (b * nq + i, 0)),
        out_shape=SDS((nb * s, TOK_WIDTH + MEM_WIDTH), BF16),
        compiler_params=_params(("parallel", "parallel")))(proj, memkv, tok)


def memattn_bwd(proj, memkv, dcat, dtok, nb, s, name):
    ts = min(TM, s)
    nq = s // ts

    def body(q_ref, kv_ref, do_ref, dtok_ref, dq_ref, dkv_ref):
        i = pl.program_id(1)
        dq_ref[:, :TOK_WIDTH] = dtok_ref[...]

        @pl.when(i == 0)
        def _():
            dkv_ref[...] = jnp.zeros_like(dkv_ref)

        lo = _half_masks(MEM_LEN)
        top = lax.broadcasted_iota(jnp.int32, (PAIR, ts), 0) < HEAD_DIM
        n_pairs = MEM_HEADS // 2
        qs, dos, kps, products = [], [], [], []
        for p in range(n_pairs):
            qp = q_ref[:, p * PAIR:(p + 1) * PAIR].astype(BF16)
            dop = do_ref[:, p * PAIR:(p + 1) * PAIR].astype(BF16)
            kp = kv_ref[:, p * PAIR:(p + 1) * PAIR] * QK_SCALE
            kms = _split_pair(kp)
            vms = _split_pair(kv_ref[:, MEM_WIDTH + p * PAIR:MEM_WIDTH + (p + 1) * PAIR])
            products.append([(lax.dot_general(km, qp, NT_DIMS, preferred_element_type=F32),
                              lax.dot_general(vm, dop, NT_DIMS, preferred_element_type=F32)) for km, vm in zip(kms, vms)])
            qs.append(qp)
            dos.append(dop)
            kps.append(kp)
        for p in range(n_pairs):
            kt = kps[p].astype(F32).T.astype(BF16)
            dks, dvs, dqs = [], [], []
            for sc, dp in products[p]:
                e = jnp.exp(sc - jnp.max(sc, axis=0, keepdims=True))
                pr = e / jnp.sum(e, axis=0, keepdims=True)
                dl = jnp.sum(pr * dp, axis=0, keepdims=True)
                ds = (pr * (dp - dl)).astype(BF16)
                dvs.append(jnp.dot(pr.astype(BF16), dos[p], preferred_element_type=F32))
                dks.append(jnp.dot(ds, qs[p], preferred_element_type=F32))
                dqs.append(jnp.dot(kt, ds, preferred_element_type=F32))
            dq_ref[:, TOK_WIDTH + p * PAIR:TOK_WIDTH + (p + 1) * PAIR] = jnp.where(top, dqs[0], dqs[1]).T.astype(BF16)
            dkv_ref[:, p * PAIR:(p + 1) * PAIR] += jnp.where(lo, dks[0], dks[1]) * QK_SCALE
            dkv_ref[:, MEM_WIDTH + p * PAIR:MEM_WIDTH + (p + 1) * PAIR] += jnp.where(lo, dvs[0], dvs[1])

    return pl.pallas_call(
        body, name=name, grid=(nb, nq),
        in_specs=[pl.BlockSpec((ts, MEM_WIDTH), lambda b, i: (b * nq + i, 3)),
                  pl.BlockSpec((MEM_LEN, 2 * MEM_WIDTH), lambda b, i: (b, 0)),
                  pl.BlockSpec((ts, MEM_WIDTH), lambda b, i: (b * nq + i, 3)),
                  pl.BlockSpec((ts, TOK_WIDTH), lambda b, i: (b * nq + i, 0))],
        out_specs=[pl.BlockSpec((ts, TOK_WIDTH + MEM_WIDTH), lambda b, i: (b * nq + i, 0)),
                   pl.BlockSpec((MEM_LEN, 2 * MEM_WIDTH), lambda b, i: (b, 0))],
        out_shape=[SDS((nb * s, TOK_WIDTH + MEM_WIDTH), BF16), SDS((nb * MEM_LEN, 2 * MEM_WIDTH), F32)],
        compiler_params=_params(("parallel", "arbitrary")))(proj, memkv, dcat, dtok)


def _pool_select(shape, s2, s4, s8, s16):
    lane = lax.broadcasted_iota(jnp.int32, shape, 1)
    return jnp.where(lane < POOL_GROUP, s2, jnp.where(lane < 2 * POOL_GROUP, s4, jnp.where(lane < 3 * POOL_GROUP, s8, s16)))


def _pool_count(shape, first_pos):
    pos = first_pos + lax.broadcasted_iota(jnp.int32, shape, 0)
    win = _pool_select(shape, 2, 4, 8, 16)
    return jnp.minimum(pos + 1, win).astype(F32)


def pool_fwd(proj, pw_bd, pscale, nb, s):
    ts = min(TS, s)
    nq = s // ts
    w = TOK_WIDTH

    def body(c_ref, h_ref, w_ref, sc_ref, pooled_ref, tok_ref):
        i = pl.program_id(0) % nq
        cur = c_ref[...]
        halo = jnp.where(i == 0, 0.0, h_ref[...])
        xe = jnp.concatenate([halo, cur], axis=0)
        s2 = xe + pltpu.roll(xe, 1, axis=0)
        s4 = s2 + pltpu.roll(s2, 2, axis=0)
        s8 = s4 + pltpu.roll(s4, 4, axis=0)
        s16 = s8 + pltpu.roll(s8, 8, axis=0)
        hp = HALO_POOL
        ws = _pool_select((ts, w), s2[hp:], s4[hp:], s8[hp:], s16[hp:])
        pooled = (ws / _pool_count((ts, w), i * ts) - cur).astype(BF16)
        pooled_ref[...] = pooled
        mixed = jnp.dot(pooled, w_ref[...], preferred_element_type=F32)
        tok_ref[...] = (mixed * sc_ref[...]).astype(BF16)

    row = pl.BlockSpec((ts, w), lambda r: (r, 0))
    return pl.pallas_call(
        body, name="pool_fwd", grid=(nb * nq,),
        in_specs=[row, pl.BlockSpec((HALO_POOL, w), lambda r: (jnp.maximum(r * (ts // HALO_POOL) - 1, 0), 0)),
                  pl.BlockSpec((w, w), lambda r: (0, 0)), pl.BlockSpec((1, w), lambda r: (0, 0))],
        out_specs=[row, row], out_shape=[SDS((nb * s, w), BF16), SDS((nb * s, w), BF16)],
        compiler_params=_params(("parallel",)))(proj, proj, pw_bd, pscale)


def pool_bwd_mix(dcat, pooled, pw_bd, pscale, nb, s):
    ts = min(TS, s)
    w = TOK_WIDTH

    def body(dt_ref, p_ref, w_ref, sc_ref, dm_ref, dp_ref, ds_ref):
        r = pl.program_id(0)
        dtok = dt_ref[...]
        mixed = jnp.dot(p_ref[...], w_ref[...], preferred_element_type=F32)

        @pl.when(r == 0)
        def _():
            ds_ref[...] = jnp.zeros_like(ds_ref)

        ds_ref[...] += jnp.sum(dtok * mixed, axis=0, keepdims=True)
        dmx = (dtok * sc_ref[...]).astype(BF16)
        dm_ref[...] = dmx
        dp_ref[...] = lax.dot_general(dmx, w_ref[...], NT_DIMS, preferred_element_type=F32)

    row = pl.BlockSpec((ts, w), lambda r: (r, 0))
    mat = pl.BlockSpec((w, w), lambda r: (0, 0))
    vec = pl.BlockSpec((1, w), lambda r: (0, 0))
    return pl.pallas_call(
        body, name="pool_bwd_mix", grid=(nb * s // ts,), in_specs=[row, row, mat, vec],
        out_specs=[row, row, vec], out_shape=[SDS((nb * s, w), BF16), SDS((nb * s, w), F32), SDS((1, w), F32)],
        compiler_params=_params(("arbitrary",)))(dcat, pooled, pw_bd, pscale)


def pool_bwd_window(dpooled, nb, s):
    ts = min(TS, s)
    nq = s // ts
    w = TOK_WIDTH
    n_ext = ts + HALO_POOL
    n_halo_blocks = nb * s // HALO_POOL

    def body(c_ref, n_ref, du_ref):
        i = pl.program_id(0) % nq
        cur = c_ref[...]
        nxt = jnp.where(i == nq - 1, 0.0, n_ref[...])
        ze = jnp.concatenate([cur, nxt], axis=0) / _pool_count((n_ext, w), i * ts)
        s2 = ze + pltpu.roll(ze, n_ext - 1, axis=0)
        s4 = s2 + pltpu.roll(s2, n_ext - 2, axis=0)
        s8 = s4 + pltpu.roll(s4, n_ext - 4, axis=0)
        s16 = s8 + pltpu.roll(s8, n_ext - 8, axis=0)
        ws = _pool_select((ts, w), s2[:ts], s4[:ts], s8[:ts], s16[:ts])
        du_ref[...] = (ws - cur).astype(BF16)

    row = pl.BlockSpec((ts, w), lambda r: (r, 0))
    return pl.pallas_call(
        body, name="pool_bwd_window", grid=(nb * nq,),
        in_specs=[row, pl.BlockSpec((HALO_POOL, w),
                                    lambda r: (jnp.minimum((r + 1) * (ts // HALO_POOL), n_halo_blocks - 1), 0))],
        out_specs=row, out_shape=SDS((nb * s, w), BF16),
        compiler_params=_params(("parallel",)))(dpooled, dpooled)


def _conv_rows(xe, w_ref):
    return (w_ref[0, 2:3, :] * xe + w_ref[0, 1:2, :] * pltpu.roll(xe, 1, axis=0)
            + w_ref[0, 0:1, :] * pltpu.roll(xe, 2, axis=0) + w_ref[0, 3:4, :])


def ffn_up_gate(x_bf, wup, cw, nb, s, name):
    tm = min(2 * TM, s)
    nq = s // tm
    w = FF_BLOCK_PAD
    hr = 2 * HALO_CONV
    k = x_bf.shape[1]

    def body(xc_ref, xh_ref, wu_ref, wg_ref, cu_ref, cg_ref, act_ref, a_ref, b_ref, hu_ref, hg_ref):
        first = (pl.program_id(1) % nq) == 0
        xc = xc_ref[...]
        xh = xh_ref[...]

        def products(w_ref):
            return (jnp.dot(xc, w_ref[0], preferred_element_type=F32), jnp.dot(xh, w_ref[0], preferred_element_type=F32))

        def conv(hcur, hprev, c_ref, h_out):
            h_out[...] = hcur.astype(BF16)
            xe = jnp.concatenate([jnp.where(first, 0.0, hprev), hcur], axis=0)
            return _conv_rows(xe, c_ref)[hr:]

        pu, pg = products(wu_ref), products(wg_ref)
        cu = conv(*pu, cu_ref, hu_ref)
        cg = conv(*pg, cg_ref, hg_ref)
        sg = _sigmoid(cg)
        a = cg * sg
        act_ref[...] = (a * cu).astype(BF16)
        a_ref[...] = a.astype(BF16)
        b_ref[...] = (cu * (sg * (1.0 + cg * (1.0 - sg)))).astype(BF16)

    def wblock(off):
        return pl.BlockSpec((1, k, w), lambda j, r: (j + off, 0, 0))

    def cblock(off):
        return pl.BlockSpec((1, 8, w), lambda j, r: (j + off, 0, 0))

    tile = pl.BlockSpec((tm, w), lambda j, r: (r, j))
    out = SDS((nb * s, FF_PAIRS * w), BF16)
    return pl.pallas_call(
        body, name=name, grid=(FF_PAIRS, nb * nq),
        in_specs=[pl.BlockSpec((tm, k), lambda j, r: (r, 0)),
                  pl.BlockSpec((hr, k), lambda j, r: (jnp.maximum(r * (tm // hr) - 1, 0), 0)),
                  wblock(0), wblock(FF_PAIRS), cblock(0), cblock(FF_PAIRS)],
        out_specs=[tile] * 5, out_shape=[out] * 5,
        compiler_params=_params(("parallel", "parallel")))(x_bf, x_bf, wup, wup, cw, cw)


def gate_conv_bwd(dact, a, b, hu, hg, cw, nb, s, name):
    ts = min(TS, s)
    nq = s // ts
    w = FF_BLOCK_PAD
    hc = HALO_CONV
    hb = 2 * hc
    n_ext = ts + hc

    def body(dc_ref, dn_ref, ac_ref, an_ref, bc_ref, bn_ref, hu_ref, hg_ref, wu_ref, wg_ref,
             dhu_ref, dhg_ref, dwu_ref, dwg_ref):
        r = pl.program_id(1)
        last = (r % nq) == nq - 1

        def ext(c_ref, n_ref, mask_next=False):
            nxt = n_ref[...].astype(F32)[:hc]
            if mask_next:
                nxt = jnp.where(last, 0.0, nxt)
            return jnp.concatenate([c_ref[...].astype(F32), nxt], axis=0)

        da = ext(dc_ref, dn_ref, mask_next=True)

        def branch(dcv, w_ref, h_ref, dh_ref, dw_ref):
            d0 = dcv[:ts]
            d1 = pltpu.roll(dcv, n_ext - 1, axis=0)[:ts]
            d2 = pltpu.roll(dcv, n_ext - 2, axis=0)[:ts]
            dh_ref[...] = (w_ref[0, 2:3, :] * d0 + w_ref[0, 1:2, :] * d1 + w_ref[0, 0:1, :] * d2).astype(BF16)
            hv = h_ref[...].astype(F32)
            rows = [jnp.sum(d2 * hv, axis=0, keepdims=True), jnp.sum(d1 * hv, axis=0, keepdims=True),
                    jnp.sum(d0 * hv, axis=0, keepdims=True), jnp.sum(d0, axis=0, keepdims=True)]
            sub = lax.broadcasted_iota(jnp.int32, (8, w), 0)
            upd = jnp.zeros((8, w), F32)
            for kk, rv in enumerate(rows):
                upd = jnp.where(sub == kk, rv, upd)

            @pl.when(r == 0)
            def _():
                dw_ref[...] = jnp.zeros_like(dw_ref)

            dw_ref[...] += upd[None]

        branch(da * ext(ac_ref, an_ref), wu_ref, hu_ref, dhu_ref, dwu_ref)
        branch(da * ext(bc_ref, bn_ref), wg_ref, hg_ref, dhg_ref, dwg_ref)

    cur = pl.BlockSpec((ts, w), lambda j, r: (r, j))
    nxt = pl.BlockSpec((hb, w), lambda j, r: (jnp.minimum((r + 1) * (ts // hb), nb * s // hb - 1), j))

    def wspec(off):
        return pl.BlockSpec((1, 8, w), lambda j, r: (j + off, 0, 0))

    p = FF_PAIRS
    dw_spec = pl.BlockSpec((1, 8, w), lambda j, r: (j, 0, 0))
    return pl.pallas_call(
        body, name=name, grid=(p, nb * nq),
        in_specs=[cur, nxt, cur, nxt, cur, nxt, cur, cur, wspec(0), wspec(p)],
        out_specs=[cur, cur, dw_spec, dw_spec],
        out_shape=[SDS((nb * s, p * w), BF16), SDS((nb * s, p * w), BF16), SDS((p, 8, w), F32), SDS((p, 8, w), F32)],
        compiler_params=_params(("parallel", "arbitrary")))(dact, dact, a, a, b, b, hu, hg, cw, cw)


def _tri(n, upper):
    r = lax.broadcasted_iota(jnp.int32, (n, n), 0)
    c = lax.broadcasted_iota(jnp.int32, (n, n), 1)
    return ((r <= c) if upper else (r >= c)).astype(F32)


def fgate_fwd(fl, fb, nb, s):
    tc = min(TC, s)
    nq = s // tc

    def body(fl_ref, fb_ref, f_ref, carry):
        @pl.when(pl.program_id(1) == 0)
        def _():
            carry[...] = jnp.zeros_like(carry)

        z = fl_ref[...] + fb_ref[...]
        logf = jnp.minimum(z, 0.0) - jnp.log(1.0 + jnp.exp(-jnp.abs(z)))
        f_ref[...] = jnp.dot(_tri(tc, False), logf, preferred_element_type=F32,
                             precision=lax.Precision.HIGHEST) + carry[...]
        carry[...] += jnp.sum(logf, axis=0, keepdims=True)

    row = pl.BlockSpec((tc, LANES), lambda b, i: (b * nq + i, 0))
    return pl.pallas_call(
        body, name="fgate_fwd", grid=(nb, nq), in_specs=[row, pl.BlockSpec((1, LANES), lambda b, i: (0, 0))],
        out_specs=row, out_shape=SDS((nb * s, LANES), F32), scratch_shapes=[pltpu.VMEM((1, LANES), F32)],
        compiler_params=_params(("arbitrary", "arbitrary")))(fl, fb)


def fgate_bwd(d_cum_q, d_cum_k, fl, fb, dk, dv, nb, s):
    tc = min(TC, s)
    nq = s // tc

    def body(dfq_ref, dfk_ref, fl_ref, fb_ref, dk_ref, dv_ref, dkvf_ref, dfb_ref, carry):
        b = pl.program_id(0)
        i = pl.program_id(1)

        @pl.when(i == 0)
        def _():
            carry[...] = jnp.zeros_like(carry)

        @pl.when(jnp.logical_and(b == 0, i == 0))
        def _():
            dfb_ref[...] = jnp.zeros_like(dfb_ref)

        dfv = dfq_ref[...] + dfk_ref[...]
        dlog = jnp.dot(_tri(tc, True), dfv, preferred_element_type=F32,
                       precision=lax.Precision.HIGHEST) + carry[...]
        carry[...] += jnp.sum(dfv, axis=0, keepdims=True)
        z = fl_ref[...] + fb_ref[...]
        dfl = dlog / (1.0 + jnp.exp(z))
        dkvf_ref[:, :TOK_WIDTH] = dk_ref[...]
        dkvf_ref[:, TOK_WIDTH:2 * TOK_WIDTH] = dv_ref[...]
        dkvf_ref[:, 2 * TOK_WIDTH:] = dfl.astype(BF16)
        dfb_ref[...] += jnp.sum(dfl, axis=0, keepdims=True)

    def rows(width):
        return pl.BlockSpec((tc, width), lambda b, i: (b * nq + nq - 1 - i, 0))

    row = rows(LANES)
    vec = pl.BlockSpec((1, LANES), lambda b, i: (0, 0))
    return pl.pallas_call(
        body, name="fgate_bwd", grid=(nb, nq), in_specs=[row, row, row, vec, rows(TOK_WIDTH), rows(TOK_WIDTH)],
        out_specs=[rows(KV_COLS_PAD), vec],
        out_shape=[SDS((nb * s, KV_COLS_PAD), BF16), SDS((1, LANES), F32)], scratch_shapes=[pltpu.VMEM((1, LANES), F32)],
        compiler_params=_params(("arbitrary", "arbitrary")))(d_cum_q, d_cum_k, fl, fb, dk, dv)


PAIR = 2 * HEAD_DIM
N_PAIRS = FOX_HEADS // 2


def _lane_put(shape, h, col):
    lane = lax.broadcasted_iota(jnp.int32, shape, 1)
    return jnp.where(lane == h, col, 0.0)


def _half_masks(rows):
    lane = lax.broadcasted_iota(jnp.int32, (rows, PAIR), 1)
    return lane < HEAD_DIM


def _split_pair(x, scale=None):
    if scale is not None:
        x = x * scale
    lo = _half_masks(x.shape[0])
    zero = jnp.zeros_like(x)
    return jnp.where(lo, x, zero), jnp.where(lo, zero, x)


def _to_tile_rows(a, nb, s, tf):
    return a.reshape(nb * s // tf, tf, LANES)[:, :, :16].transpose(0, 2, 1)


def _from_tile_rows(a):
    tiles, _, tf = a.shape
    return jnp.pad(a.transpose(0, 2, 1), ((0, 0), (0, 0), (0, LANES - 16))).reshape(tiles * tf, LANES)


BIAS_TERMS = 3
LOOKAHEAD = 4
FOLLOW_FWD = 1
LOOKAHEAD_BWD = 2
FOLLOW_BWD = 1


def _bias_lane(h):
    return HEAD_DIM if h % 2 == 0 else 0


def _placement():
    rows = jnp.arange(LANES)[:, None]
    cols = jnp.arange(FOX_HEADS * PAIR)[None, :]
    head, lane = cols // PAIR, cols % PAIR
    first = jnp.where(head % 2 == 0, HEAD_DIM, 0)
    term = lane - first
    hit = (term >= 0) & (term < BIAS_TERMS) & (rows == 16 * term + head)
    return hit.astype(BF16)


def fox_prep(kv, fneg, nb, s):
    tf = min(TF, s)
    w = TOK_WIDTH

    def body(k_ref, v_ref, f_ref, pl_ref, ka_ref, vt_ref):
        lane = lax.broadcasted_iota(jnp.int32, (tf, LANES), 1)
        lo = lane < HEAD_DIM
        f = jnp.where(lane < FOX_HEADS, f_ref[...], 0.0)
        hi = f.astype(BF16).astype(F32)
        mid = (f - hi).astype(BF16).astype(F32)
        low = (f - hi - mid).astype(BF16).astype(F32)
        terms = (hi + pltpu.roll(mid, 16, axis=1) + pltpu.roll(low, 32, axis=1)).astype(BF16)
        placed = jnp.dot(terms, pl_ref[...], preferred_element_type=F32).astype(BF16)
        one = jnp.ones((tf, LANES), BF16)
        zero = jnp.zeros((tf, LANES), BF16)
        for p in range(N_PAIRS):
            kp = k_ref[:, p * PAIR:(p + 1) * PAIR] * QK_SCALE
            vp = v_ref[:, p * PAIR:(p + 1) * PAIR]
            he, ho = 2 * p, 2 * p + 1
            ka_ref[:, he * PAIR:(he + 1) * PAIR] = jnp.where(lo, kp, placed[:, he * PAIR:(he + 1) * PAIR])
            ka_ref[:, ho * PAIR:(ho + 1) * PAIR] = jnp.where(lo, placed[:, ho * PAIR:(ho + 1) * PAIR], kp)
            ve = jnp.where(lo, vp, jnp.where(lane == HEAD_DIM, one, zero))
            vo = jnp.where(lo, jnp.where(lane == 0, one, zero), vp)
            vt_ref[0, he * PAIR:(he + 1) * PAIR, :] = ve.astype(F32).T.astype(BF16)
            vt_ref[0, ho * PAIR:(ho + 1) * PAIR, :] = vo.astype(F32).T.astype(BF16)

    return pl.pallas_call(
        body, name="fox_prep", grid=(nb * s // tf,),
        in_specs=[pl.BlockSpec((tf, w), lambda r: (r, 0)), pl.BlockSpec((tf, w), lambda r: (r, 1)),
                  pl.BlockSpec((tf, LANES), lambda r: (r, 0)), pl.BlockSpec((LANES, FOX_HEADS * PAIR), lambda r: (0, 0))],
        out_specs=[pl.BlockSpec((tf, FOX_HEADS * PAIR), lambda r: (r, 0)),
                   pl.BlockSpec((1, FOX_HEADS * PAIR, tf), lambda r: (r, 0, 0))],
        out_shape=[SDS((nb * s, FOX_HEADS * PAIR), BF16), SDS((nb * s // tf, FOX_HEADS * PAIR, tf), BF16)],
        compiler_params=_params(("parallel",)))(kv, kv, fneg, _placement())


def fox_fwd_t(pq, kaug, vaug_t, nb, s):
    tf = min(TF, s)
    n = s // tf
    w = TOK_WIDTH
    wa = FOX_HEADS * PAIR

    def body(q_ref, k_hbm, vt_hbm, ob_ref, of_ref, lse_ref, k_vm, vt_vm, qx_scr, m_scr, acc_scr, sems):
        b = pl.program_id(0)
        i = pl.program_id(1)

        @pl.when(i == 0)
        def _():
            ck = pltpu.make_async_copy(k_hbm.at[pl.ds(pl.multiple_of(b * s, tf), s)], k_vm, sems.at[0])
            cv = pltpu.make_async_copy(vt_hbm.at[pl.ds(b * n, n)], vt_vm, sems.at[1])
            ck.start()
            cv.start()
            ck.wait()
            cv.wait()

        lane = lax.broadcasted_iota(jnp.int32, (tf, PAIR), 1)
        one = jnp.ones((tf, PAIR), BF16)
        zero = jnp.zeros((tf, PAIR), BF16)
        for p in range(N_PAIRS):
            qp = q_ref[:, p * PAIR:(p + 1) * PAIR]
            be, bo = _bias_lane(2 * p), _bias_lane(2 * p + 1)
            ones_e = jnp.where((lane >= be) & (lane < be + BIAS_TERMS), one, zero)
            ones_o = jnp.where((lane >= bo) & (lane < bo + BIAS_TERMS), one, zero)
            qx_scr[2 * p] = jnp.where(lane < HEAD_DIM, qp, ones_e)
            qx_scr[2 * p + 1] = jnp.where(lane < HEAD_DIM, ones_o, qp)
        m_scr[...] = jnp.full(m_scr.shape, NEG_BIG, F32)
        acc_scr[...] = jnp.zeros_like(acc_scr)

        def tile(j, masked):
            ks = pl.multiple_of(j * tf, tf)
            if masked:
                keep = lax.broadcasted_iota(jnp.int32, (tf, tf), 1) >= lax.broadcasted_iota(jnp.int32, (tf, tf), 0)
            def scores(h):
                kx = k_vm[pl.ds(ks, tf), h * PAIR:(h + 1) * PAIR]
                return lax.dot_general(kx, qx_scr[h], NT_DIMS, preferred_element_type=F32)

            def values(h, pr, a):
                pv = jnp.dot(vt_vm[j, h * PAIR:(h + 1) * PAIR, :], pr, preferred_element_type=F32)
                acc_scr[h] = a * acc_scr[h] + pv

            ahead = [scores(h) for h in range(LOOKAHEAD)]
            behind = []
            for h in range(FOX_HEADS):
                sc = ahead.pop(0)
                if h + LOOKAHEAD < FOX_HEADS:
                    ahead.append(scores(h + LOOKAHEAD))
                if masked:
                    sc = jnp.where(keep, sc, NEG_BIG)
                m_prev = m_scr[h]
                m_new = jnp.maximum(m_prev, jnp.max(sc, axis=0, keepdims=True))
                m_scr[h] = m_new
                behind.append((h, jnp.exp(sc - m_new).astype(BF16), jnp.exp(m_prev - m_new)))
                if len(behind) > FOLLOW_FWD:
                    values(*behind.pop(0))
            for item in behind:
                values(*item)

        def step(j, carry):
            tile(j, False)
            return carry

        lax.fori_loop(0, i, step, 0)
        tile(i, True)

        top = lax.broadcasted_iota(jnp.int32, (PAIR, tf), 0) < HEAD_DIM
        sub = lax.broadcasted_iota(jnp.int32, (16, tf), 0)
        lse = jnp.zeros((16, tf), F32)
        for p in range(N_PAIRS):
            he, ho = 2 * p, 2 * p + 1
            le = acc_scr[he, HEAD_DIM:HEAD_DIM + 1, :]
            lod = acc_scr[ho, 0:1, :]
            o = jnp.where(top, acc_scr[he] / le, acc_scr[ho] / lod).T
            ob_ref[:, p * PAIR:(p + 1) * PAIR] = o.astype(BF16)
            of_ref[:, p * PAIR:(p + 1) * PAIR] = o
            lse = jnp.where(sub == he, m_scr[he] + jnp.log(le), lse)
            lse = jnp.where(sub == ho, m_scr[ho] + jnp.log(lod), lse)
        lse_ref[0] = lse

    qrow = lambda b, i: (b * n + i, 0)
    return pl.pallas_call(
        body, name="fox_fwd", grid=(nb, n),
        in_specs=[pl.BlockSpec((tf, w), qrow), ANY_SPEC, ANY_SPEC],
        out_specs=[pl.BlockSpec((tf, w), qrow), pl.BlockSpec((tf, w), qrow),
                   pl.BlockSpec((1, 16, tf), lambda b, i: (b * n + i, 0, 0))],
        out_shape=[SDS((nb * s, w), BF16), SDS((nb * s, w), F32), SDS((nb * n, 16, tf), F32)],
        scratch_shapes=[pltpu.VMEM((s, wa), BF16), pltpu.VMEM((n, wa, tf), BF16),
                        pltpu.VMEM((FOX_HEADS, tf, PAIR), BF16), pltpu.VMEM((FOX_HEADS, 1, tf), F32),
                        pltpu.VMEM((FOX_HEADS, PAIR, tf), F32), pltpu.SemaphoreType.DMA((2,))],
        compiler_params=_params(("arbitrary", "arbitrary")))(pq, kaug, vaug_t)


def fox_delta(dcat, o, nb, s):
    tf = min(TM, s)
    w = TOK_WIDTH

    def body(do_ref, o_ref, dl_ref):
        out = jnp.zeros((tf, LANES), F32)
        for h in range(FOX_HEADS):
            lo, hi = h * HEAD_DIM, (h + 1) * HEAD_DIM
            out = out + _lane_put((tf, LANES), h, jnp.sum(do_ref[:, lo:hi] * o_ref[:, lo:hi], axis=1, keepdims=True))
        dl_ref[...] = out

    row = pl.BlockSpec((tf, w), lambda r: (r, 0))
    return pl.pallas_call(
        body, name="fox_delta", grid=(nb * s // tf,), in_specs=[row, row],
        out_specs=pl.BlockSpec((tf, LANES), lambda r: (r, 0)), out_shape=SDS((nb * s, LANES), F32),
        compiler_params=_params(("parallel",)))(dcat, o)


def fox_bwd(pq, kv, fneg, dcat_bf, lse_rows, delta_rows, nb, s):
    tf = min(TF, s)
    n = s // tf
    w = TOK_WIDTH

    def body(q_hbm, k_ref, v_ref, f_ref, do_hbm, lse_ref, dl_ref, dq_ref, dk_ref, dv_ref, dfk_ref, dfq_ref,
             q_vm, do_vm, km_scr, vm_scr, kt_scr, fk_scr, dk_scr, dv_scr, rs_scr, dq_scr, fq_scr, sems):
        b = pl.program_id(0)
        j = pl.program_id(1)

        @pl.when(j == 0)
        def _():
            rows = pl.ds(pl.multiple_of(b * s, tf), s)
            cq = pltpu.make_async_copy(q_hbm.at[rows, pl.ds(0, w)], q_vm, sems.at[0])
            cd = pltpu.make_async_copy(do_hbm.at[rows, pl.ds(0, w)], do_vm, sems.at[1])
            cq.start()
            cd.start()
            dq_scr[...] = jnp.zeros_like(dq_scr)
            fq_scr[...] = jnp.zeros_like(fq_scr)
            cq.wait()
            cd.wait()

        for p in range(N_PAIRS):
            kp = k_ref[:, p * PAIR:(p + 1) * PAIR] * QK_SCALE
            ke, ko = _split_pair(kp)
            km_scr[2 * p] = ke
            km_scr[2 * p + 1] = ko
            kt_scr[p] = kp.astype(F32).T.astype(BF16)
            ve, vo = _split_pair(v_ref[:, p * PAIR:(p + 1) * PAIR])
            vm_scr[2 * p] = ve
            vm_scr[2 * p + 1] = vo
        for h in range(FOX_HEADS):
            fk_scr[h] = jnp.broadcast_to(f_ref[:, h:h + 1], (tf, tf))
        dk_scr[...] = jnp.zeros_like(dk_scr)
        dv_scr[...] = jnp.zeros_like(dv_scr)
        rs_scr[...] = jnp.zeros_like(rs_scr)

        def tile(i, masked):
            qs = pl.multiple_of(i * tf, tf)
            if masked:
                keep = lax.broadcasted_iota(jnp.int32, (tf, tf), 1) >= lax.broadcasted_iota(jnp.int32, (tf, tf), 0)
            def products(h):
                qp = q_vm[pl.ds(qs, tf), (h // 2) * PAIR:(h // 2 + 1) * PAIR]
                dop = do_vm[pl.ds(qs, tf), (h // 2) * PAIR:(h // 2 + 1) * PAIR]
                return (lax.dot_general(km_scr[h], qp, NT_DIMS, preferred_element_type=F32),
                        lax.dot_general(vm_scr[h], dop, NT_DIMS, preferred_element_type=F32))

            def dependents(h, prb, dsb):
                p = h // 2
                half = slice((h % 2) * HEAD_DIM, (h % 2 + 1) * HEAD_DIM)
                qp = q_vm[pl.ds(qs, tf), p * PAIR:(p + 1) * PAIR]
                dop = do_vm[pl.ds(qs, tf), p * PAIR:(p + 1) * PAIR]
                dv_scr[h] += jnp.dot(prb, dop, preferred_element_type=F32)
                dk_scr[h] += jnp.dot(dsb, qp, preferred_element_type=F32)
                dqt = jnp.dot(kt_scr[p], dsb, preferred_element_type=F32)
                dq_scr[i, p, half, :] += dqt[(h % 2) * HEAD_DIM:(h % 2 + 1) * HEAD_DIM]

            ahead = [products(h) for h in range(LOOKAHEAD_BWD)]
            behind = []
            for h in range(FOX_HEADS):
                sc, dp = ahead.pop(0)
                if h + LOOKAHEAD_BWD < FOX_HEADS:
                    ahead.append(products(h + LOOKAHEAD_BWD))
                sc = sc + fk_scr[h] - lse_ref[i, h:h + 1, :]
                if masked:
                    sc = jnp.where(keep, sc, NEG_BIG)
                pr = jnp.exp(sc)
                ds = pr * (dp - dl_ref[i, h:h + 1, :])
                part = ds[:, :LANES]
                for c in range(1, tf // LANES):
                    part = part + ds[:, c * LANES:(c + 1) * LANES]
                rs_scr[h] += part
                fq_scr[i, h:h + 1, :] += jnp.sum(ds, axis=0, keepdims=True)
                behind.append((h, pr.astype(BF16), ds.astype(BF16)))
                if len(behind) > FOLLOW_BWD:
                    dependents(*behind.pop(0))
            for item in behind:
                dependents(*item)

        def step(i, carry):
            tile(i, False)
            return carry

        tile(j, True)
        for p in range(N_PAIRS):
            dq_ref[:, p * PAIR:(p + 1) * PAIR] = dq_scr[j, p].T.astype(BF16)
        dfq_ref[0] = fq_scr[j]
        lax.fori_loop(j + 1, n, step, 0)

        lo = _half_masks(tf)
        dfk = jnp.zeros((tf, LANES), F32)
        for p in range(N_PAIRS):
            dk = jnp.where(lo, dk_scr[2 * p], dk_scr[2 * p + 1]) * QK_SCALE
            dk_ref[:, p * PAIR:(p + 1) * PAIR] = dk.astype(BF16)
            dv_ref[:, p * PAIR:(p + 1) * PAIR] = jnp.where(lo, dv_scr[2 * p], dv_scr[2 * p + 1]).astype(BF16)
            for h in (2 * p, 2 * p + 1):
                dfk = dfk - _lane_put((tf, LANES), h, jnp.sum(rs_scr[h], axis=1, keepdims=True))
        dfk_ref[...] = dfk

    krow = lambda b, j: (b * n + j, 0)
    rows = pl.BlockSpec((n, 16, tf), lambda b, j: (b, 0, 0))
    tile_out = pl.BlockSpec((tf, w), krow)
    return pl.pallas_call(
        body, name="fox_bwd", grid=(nb, n),
        in_specs=[ANY_SPEC, pl.BlockSpec((tf, w), krow), pl.BlockSpec((tf, w), lambda b, j: (b * n + j, 1)),
                  pl.BlockSpec((tf, LANES), krow), ANY_SPEC, rows, rows],
        out_specs=[tile_out, tile_out, tile_out, pl.BlockSpec((tf, LANES), krow),
                   pl.BlockSpec((1, 16, tf), lambda b, j: (b * n + j, 0, 0))],
        out_shape=[SDS((nb * s, w), BF16), SDS((nb * s, w), BF16), SDS((nb * s, w), BF16), SDS((nb * s, LANES), F32),
                   SDS((nb * n, 16, tf), F32)],
        scratch_shapes=[pltpu.VMEM((s, w), BF16), pltpu.VMEM((s, w), BF16),
                        pltpu.VMEM((FOX_HEADS, tf, PAIR), BF16), pltpu.VMEM((FOX_HEADS, tf, PAIR), BF16),
                        pltpu.VMEM((N_PAIRS, PAIR, tf), BF16), pltpu.VMEM((FOX_HEADS, tf, tf), F32),
                        pltpu.VMEM((FOX_HEADS, tf, PAIR), F32), pltpu.VMEM((FOX_HEADS, tf, PAIR), F32),
                        pltpu.VMEM((FOX_HEADS, tf, LANES), F32), pltpu.VMEM((n, N_PAIRS, PAIR, tf), F32),
                        pltpu.VMEM((n, 16, tf), F32), pltpu.SemaphoreType.DMA((2,))],
        compiler_params=_params(("arbitrary", "arbitrary")))(pq, kv, kv, fneg, dcat_bf, lse_rows, delta_rows)


ADAMW_TILE_ELEMS = 128 * 1024


def reduce_adamw(parts, w, m, v, name):
    layers, r, c = w.shape
    tr = r
    for cand in range(16, r, 16):
        if r % cand == 0 and cand * c <= ADAMW_TILE_ELEMS:
            tr = cand
    c1 = 1.0 - ADAM_B1 ** ADAM_STEP
    c2 = 1.0 - ADAM_B2 ** ADAM_STEP

    def body(*refs):
        p_refs = refs[:layers]
        w_ref, m_ref, v_ref, g_out, d_out, m_out, v_out = refs[layers:]

        def update(p_ref):
            g = p_ref[0].astype(F32)
            for k in range(1, N_DEV):
                g = g + p_ref[k].astype(F32)
            mn = ADAM_B1 * m_ref[0] + (1.0 - ADAM_B1) * g
            vn = ADAM_B2 * v_ref[0] + (1.0 - ADAM_B2) * (g * g)
            g_out[0] = g
            m_out[0] = mn
            v_out[0] = vn
            d_out[0] = -ADAM_LR * ((mn / c1) / (jnp.sqrt(vn / c2) + ADAM_EPS) + ADAM_WD * w_ref[0])

        if layers == 1:
            update(p_refs[0])
        else:
            for layer in range(layers):
                pl.when(pl.program_id(0) == layer)(lambda layer=layer: update(p_refs[layer]))

    row = pl.BlockSpec((1, tr, c), lambda l, i: (l, i, 0))
    return pl.pallas_call(
        body, name=name, grid=(layers, r // tr),
        in_specs=[pl.BlockSpec((N_DEV, tr, c), lambda l, i: (0, i, 0))] * layers + [row, row, row],
        out_specs=[row, row, row, row], out_shape=[SDS((layers, r, c), F32)] * 4,
        compiler_params=_params(("parallel", "parallel")))(*parts, w, m, v)


N_PEERS = N_DEV - 1
HBM_SPEC = pl.BlockSpec(memory_space=pltpu.HBM)
SEM_SPEC = pl.BlockSpec(memory_space=pltpu.SEMAPHORE)
ANY_SPEC = pl.BlockSpec(memory_space=pl.ANY)
SPLIT_EFFECT = pltpu.SideEffectType.DATAFLOW_SIDE_EFFECTING


def _peers(with_self=False):
    x, y, c = lax.axis_index("x"), lax.axis_index("y"), lax.axis_index("c")
    peers = []
    for k in range(0 if with_self else 1, N_DEV):
        px = 1 - x if (k >> 2) & 1 else x
        py = 1 - y if (k >> 1) & 1 else y
        pc = 1 - c if k & 1 else c
        peers.append(((px, py, pc), 4 * px + 2 * py + pc))
    return 4 * x + 2 * y + c, peers


def _push(src, dst, send_sems, recv_sems, slot, dev):
    return pltpu.make_async_remote_copy(src_ref=src, dst_ref=dst, send_sem=send_sems.at[slot], recv_sem=recv_sems.at[slot],
                                        device_id=dev, device_id_type=pl.DeviceIdType.MESH)


def _landing_shapes(arrs, scatter):
    return [SDS((N_DEV,) + tuple(a.shape[1:] if sc else a.shape), a.dtype) for a, sc in zip(arrs, scatter)]


def exchange(arrs, scatter, name):
    na = len(arrs)

    def body(*refs):
        ins = refs[:na]
        outs = refs[na:2 * na]
        send_sems, recv_sems, local_sems = refs[2 * na:]
        me, peers = _peers()
        local = []
        remote = []
        for a in range(na):
            lc = pltpu.make_async_copy(ins[a].at[me] if scatter[a] else ins[a], outs[a].at[me], local_sems.at[a])
            lc.start()
            local.append(lc)
            for k, (dev, idx) in enumerate(peers):
                cp = _push(ins[a].at[idx] if scatter[a] else ins[a], outs[a].at[me], send_sems, recv_sems,
                           a * N_PEERS + k, dev)
                cp.start()
                remote.append(cp)
        for a in range(na):
            for k, (dev, idx) in enumerate(peers):
                _push(ins[a].at[me] if scatter[a] else ins[a], outs[a].at[idx], send_sems, recv_sems,
                      a * N_PEERS + k, dev).wait_recv()
        for cp in remote:
            cp.wait_send()
        for lc in local:
            lc.wait()

    return pl.pallas_call(
        body, name=name, in_specs=[HBM_SPEC] * na, out_specs=[HBM_SPEC] * na, out_shape=_landing_shapes(arrs, scatter),
        scratch_shapes=[pltpu.SemaphoreType.DMA((na * N_PEERS,)), pltpu.SemaphoreType.DMA((na * N_PEERS,)),
                        pltpu.SemaphoreType.DMA((na,))])(*arrs)


def exchange_start(arrs, scatter, after, name):
    na = len(arrs)
    lands = [lax.empty(l.shape, l.dtype) for l in _landing_shapes(arrs, scatter)]

    def body(*refs):
        ins = refs[:na]
        land = refs[na:2 * na]
        send_sems, recv_sems = refs[2 * na + 1], refs[2 * na + 2]
        token = refs[-1]
        me, peers = _peers(with_self=True)
        for a in range(na):
            for k, (dev, idx) in enumerate(peers):
                _push(ins[a].at[idx] if scatter[a] else ins[a], land[a].at[me], send_sems, recv_sems,
                      a * N_DEV + k, dev).start()
        token[...] = jnp.zeros_like(token)

    thru = [pltpu.HBM(a.shape, a.dtype) for a in arrs] + [pltpu.HBM(l.shape, l.dtype) for l in lands]
    res = pl.pallas_call(
        body, name=name,
        out_shape=(pltpu.SemaphoreType.DMA((na * N_DEV,)), pltpu.SemaphoreType.DMA((na * N_DEV,)), *thru,
                   SDS((8, LANES), F32)),
        in_specs=[HBM_SPEC] * (2 * na) + [ANY_SPEC],
        out_specs=(SEM_SPEC, SEM_SPEC, *([HBM_SPEC] * (2 * na)), pl.BlockSpec(memory_space=pltpu.VMEM)),
        input_output_aliases={i: 2 + i for i in range(2 * na)},
        compiler_params=pltpu.CompilerParams(has_side_effects=SPLIT_EFFECT),
    )(*[pltpu.with_memory_space_constraint(a, pltpu.HBM) for a in arrs],
      *[pltpu.with_memory_space_constraint(l, pltpu.HBM) for l in lands], after)
    return {"send": res[0], "recv": res[1], "src": res[2:2 + na], "land": res[2 + na:2 + 2 * na],
            "token": res[-1][0, 0], "scatter": scatter}


def exchange_wait(handle, after, name):
    scatter = handle["scatter"]
    na = len(scatter)

    def body(*refs):
        src = refs[:na]
        land = refs[na:2 * na]
        send_sems, recv_sems = refs[2 * na], refs[2 * na + 1]
        me, peers = _peers(with_self=True)
        for a in range(na):
            for k, (dev, idx) in enumerate(peers):
                cp = _push(src[a].at[me] if scatter[a] else src[a], land[a].at[idx], send_sems, recv_sems,
                           a * N_DEV + k, dev)
                cp.wait_send()
                cp.wait_recv()

    ops = list(handle["src"]) + list(handle["land"])
    res = pl.pallas_call(
        body, name=name, out_shape=tuple(pltpu.HBM(o.shape, o.dtype) for o in ops),
        in_specs=[HBM_SPEC] * (2 * na) + [SEM_SPEC, SEM_SPEC, ANY_SPEC], out_specs=tuple([HBM_SPEC] * (2 * na)),
        input_output_aliases={i: i for i in range(2 * na)},
        compiler_params=pltpu.CompilerParams(has_side_effects=SPLIT_EFFECT),
    )(*ops, handle["send"], handle["recv"], after)
    return list(res[na:])


def forward_layer(l, xin, xin_bf, mem_bf, wt, nb, s, ffn_weights=None):
    sv = {"xin_bf": xin_bf}
    memkv = mm_nn(mem_bf, wt["memw"], BF16, f"memkv{l}")
    sv["memkv"] = memkv
    if l == 0:
        proj = mm_nn(xin_bf, wt["win_a"], F32, "proj_a")
        pooled, tok = pool_fwd(proj, wt["pw_bd"], wt["pscale"], nb, s)
        sv["pooled"] = pooled
    else:
        kv = mm_nn(xin_bf, wt["kvw"][:, :2 * TOK_WIDTH], BF16, "kv_proj")
        fl = mm_nn(xin_bf, wt["kvw"][:, 2 * TOK_WIDTH:], F32, "gate_proj")
        fneg = -fgate_fwd(fl, wt["fb"], nb, s)
        proj = mm_nn(xin_bf, wt["wq"], BF16, "proj_b")
        kaug, vaug_t = fox_prep(kv, fneg, nb, s)
        tok, o_f32, lse_rows = fox_fwd_t(proj, kaug, vaug_t, nb, s)
        sv.update(kv=kv, fl=fl, fneg=fneg, o_f32=o_f32, lse_rows=lse_rows)
    sv["proj"] = proj
    cat = memattn_fwd(proj, memkv, tok, nb, s, f"memattn_fwd{l}")
    sv["cat"] = cat
    x1, x1_bf, xh1, rs1 = ln_fwd(xin, cat, wt["wout"], wt["ln1_g"], wt["ln1_b"], f"out_proj_ln1_{l}")
    sv.update(x1_bf=x1_bf, xh1=xh1, rs1=rs1)
    if ffn_weights is not None:
        wt.update(ffn_weights(x1_bf))
    act, ga, gb, hu, hg = ffn_up_gate(x1_bf, wt["wup"], wt["cw"], nb, s, f"ffn_up_gate{l}")
    sv.update(act=act, ga=ga, gb=gb, hu=hu, hg=hg)
    x2, x2_bf, xh2, rs2 = ln_fwd(x1, act, wt["wdown"], wt["ln2_g"], wt["ln2_b"], f"ffn_down_ln2_{l}")
    sv.update(xh2=xh2, rs2=rs2)
    return x2, x2_bf, sv


def backward_layer(l, dy, sv, mem_bf, wt, nb, s, after_ffn=None, after_pool=None, after_in=None, loss_target=None):
    g = {}
    if loss_target is None:
        dr2, dr2_bf, g["ln2_g"], g["ln2_b"] = ln_bwd(dy[0], sv["xh2"], sv["rs2"], wt["ln2_g"], f"ln2_bwd{l}",
                                                     dy_scale=dy[1], products=dy[2])
    else:
        dr2, dr2_bf, g["ln2_g"], g["ln2_b"], g["loss_row"] = loss_ln_bwd(sv["xh2"], sv["rs2"], wt["ln2_g"], wt["ln2_b"],
                                                                         loss_target, f"loss_ln2_bwd{l}")
    dact = mm_nn(dr2_bf, wt["wdown"], BF16, f"ffn_down_dx{l}", trans_b=0)
    g["wdown"] = mm_tn(sv["act"], dr2_bf, f"ffn_down_dw{l}")
    dh_u, dh_g, dcw_u, dcw_g = gate_conv_bwd(dact, sv["ga"], sv["gb"], sv["hu"], sv["hg"], wt["cw"], nb, s,
                                             f"gate_conv_bwd{l}")
    g["cw"] = jnp.concatenate([dcw_u, dcw_g], axis=0)
    g["wup"] = jnp.concatenate([mm_tn(sv["x1_bf"], dh_u, f"ffn_up_dw_u{l}", blocked=True),
                                mm_tn(sv["x1_bf"], dh_g, f"ffn_up_dw_g{l}", blocked=True)], axis=0)
    ln1_g = wt["ln1_g"] if after_ffn is None else wt["ln1_g"] + after_ffn(g, dr2)
    dr1, dr1_bf, g["ln1_g"], g["ln1_b"] = ln_bwd(dr2, sv["xh1"], sv["rs1"], ln1_g, f"ffn_up_dx_ln1_bwd{l}",
                                                 dy_scale=DN_ALPHA, products=[(dh_u, wt["wup"], 0), (dh_g, wt["wup"], 1)])
    dcat, dcat_bf = mm_nn(dr1_bf, wt["wout"], F32, f"out_proj_dx{l}", also_bf16=True, trans_b=0)
    g["wout"] = mm_tn(sv["cat"], dr1_bf, f"out_proj_dw{l}")
    if l == 0:
        dmixed, dpooled, g["pscale"] = pool_bwd_mix(dcat, sv["pooled"], wt["pw_bd"], wt["pscale"], nb, s)
        g["pw_full"] = mm_tn(sv["pooled"], dmixed, "pool_dw", out_dtype=F32)
        dtok = pool_bwd_window(dpooled, nb, s)
    else:
        delta = fox_delta(dcat, sv["o_f32"], nb, s)
        tf = min(TF, s)
        dtok, dk, dv, dfcum_k, dfq_rows = fox_bwd(sv["proj"], sv["kv"], sv["fneg"], dcat_bf,
                                                  sv["lse_rows"], _to_tile_rows(delta, nb, s, tf), nb, s)
    dproj, dmemkv = memattn_bwd(sv["proj"], sv["memkv"], dcat, dtok, nb, s, f"memattn_bwd{l}")
    g["memw"] = mm_tn(mem_bf, dmemkv, f"memkv_dw{l}")
    if l == 0:
        win_a = wt["win_a"] if after_pool is None else wt["win_a"] + after_pool(g, dproj).astype(BF16)
        g["win_a"] = mm_tn(sv["xin_bf"], dproj, "proj_a_dw")
        if after_in is not None:
            win_a = win_a + after_in(g).astype(BF16)
        dx = mm_nn(dproj, win_a, F32, "proj_a_dx", addend=dr1, add_scale=DN_ALPHA, trans_b=0)
    else:
        dkvf, g["fb"] = fgate_bwd(_from_tile_rows(dfq_rows), dfcum_k, sv["fl"], wt["fb"], dk, dv, nb, s)
        dx = (dr1, DN_ALPHA, [(dproj, wt["wq"], 0), (dkvf, wt["kvw"], 0)])
        g["wq"] = mm_tn(sv["xin_bf"], dproj, "proj_b_dw")
        g["kvw"] = mm_tn(sv["xin_bf"], dkvf, "kv_proj_dw")
    return dx, g


def pack_replicated(pool_w, ln1_g, ln1_b, ln2_g, ln2_b, conv_b, f_b):
    cb = jnp.pad(conv_b, ((0, 0), (0, 6144 - 5504))).reshape(12, D_MODEL)
    fb = jnp.pad(f_b.reshape(1, FOX_HEADS), ((0, 3), (0, D_MODEL - FOX_HEADS)))
    return jnp.concatenate([pool_w.reshape(144, D_MODEL), ln1_g, ln1_b, ln2_g, ln2_b, cb, fb], axis=0)


def unpack_replicated(buf):
    pool_w = buf[:144].reshape(1, 4, POOL_GROUP, POOL_GROUP)
    ln = [buf[144 + 2 * k:146 + 2 * k] for k in range(4)]
    conv_b = buf[152:164].reshape(2, 6144)[:, :5504]
    f_b = buf[164, :FOX_HEADS]
    return pool_w, ln[0], ln[1], ln[2], ln[3], conv_b, f_b


def _pad_ff(a, axis):
    zeros = jnp.zeros(a.shape[:axis] + (FF_ROWS_PAD - FF_ROWS,) + a.shape[axis + 1:], a.dtype)
    halves = [lax.slice_in_dim(a, h * FF_ROWS, (h + 1) * FF_ROWS, axis=axis) for h in range(2)]
    return jnp.concatenate([halves[0], zeros, halves[1], zeros], axis=axis)


def _unpad_ff(a, axis):
    return jnp.concatenate([lax.slice_in_dim(a, h * FF_ROWS_PAD, h * FF_ROWS_PAD + FF_ROWS, axis=axis) for h in range(2)],
                           axis=axis)


def pack_small(conv_w, pool_scale):
    buf = jnp.zeros((16, FF_BLOCK_PAD), F32)
    buf = lax.dynamic_update_slice(buf, _pad_ff(conv_w.reshape(DEPTH * 3, FF_BLOCK), 1), (0, 0))
    return lax.dynamic_update_slice(buf, pool_scale, (8, 0))


def _block_diag(pw):
    out = jnp.zeros((TOK_WIDTH, TOK_WIDTH), pw.dtype)
    for g in range(4):
        out = lax.dynamic_update_slice(out, pw[g], (g * POOL_GROUP, g * POOL_GROUP))
    return out


def layer_shards(l, sq_a, sq_b, mem_w_kv, ffn_w_up, ffn_w_down):
    wdown = jnp.pad(ffn_w_down[l], ((0, FF_ROWS_PAD - FF_ROWS), (0, 0)))
    return [sq_a[0].astype(BF16), sq_b[0].astype(BF16), mem_w_kv[l].astype(BF16), _pad_ff(ffn_w_up[l], 1).astype(BF16),
            wdown.astype(BF16)]


def mixer_weights(l, gath, ln1_g, ln1_b, ln2_g, ln2_b):
    w_out = gath[1].reshape(D_MODEL, D_MODEL)
    wt = {"memw": gath[2].reshape(D_MODEL, 2 * MEM_WIDTH), "wout": w_out,
          "ln1_g": ln1_g[l:l + 1], "ln1_b": ln1_b[l:l + 1], "ln2_g": ln2_g[l:l + 1], "ln2_b": ln2_b[l:l + 1]}
    return wt, gath[0].reshape(D_MODEL, D_MODEL)


def ffn_weights(l, wup_g, wdown_g, small, conv_b):
    cb = _pad_ff(conv_b[l].reshape(N_DEV, FF_BLOCK), 1)
    cw = jnp.concatenate([small[:, 3 * l:3 * l + 3, :], cb[:, None, :], jnp.zeros((N_DEV, 4, FF_BLOCK_PAD), F32)], axis=1)
    return {"wup": wup_g, "wdown": wdown_g.reshape(FF_PAIRS * FF_BLOCK_PAD, D_MODEL), "cw": cw}


def mixer_grad_blocks(g, w_in_grad):
    blocks = [] if w_in_grad is None else [w_in_grad.reshape(N_DEV, 128, D_MODEL)]
    blocks += [g["wout"].reshape(N_DEV, 128, D_MODEL), g["memw"].reshape(N_DEV, 128, 2 * MEM_WIDTH)]
    return [b.astype(BF16) for b in blocks]


def ffn_grad_blocks(g):
    wdown = g["wdown"].reshape(N_DEV, FF_ROWS_PAD, D_MODEL)[:, :FF_ROWS]
    return [_unpad_ff(g["wup"], 2).astype(BF16), wdown.astype(BF16)]


def small_grad_blocks(g0, g1):
    taps = jnp.stack([g0["cw"][:, :3, :], g1["cw"][:, :3, :]], axis=1).reshape(N_DEV, DEPTH * 3, FF_BLOCK_PAD)
    small = jnp.zeros((N_DEV, 16, FF_BLOCK_PAD), F32)
    small = lax.dynamic_update_slice(small, taps, (0, 0, 0))
    return lax.dynamic_update_slice(small, g0["pscale"].reshape(N_DEV, 1, 96), (0, 8, 0))


def replicated_grads(g0, g1):
    pw = jnp.stack([g0["pw_full"][k * POOL_GROUP:(k + 1) * POOL_GROUP, k * POOL_GROUP:(k + 1) * POOL_GROUP] for k in range(4)])
    conv_b = jnp.stack([_unpad_ff(g_["cw"][:, 3, :], 1).reshape(N_DEV * FF_BLOCK) for g_ in (g0, g1)])
    ln = [jnp.concatenate([g0[n], g1[n]], axis=0) for n in ("ln1_g", "ln1_b", "ln2_g", "ln2_b")]
    return pack_replicated(pw[None], ln[0], ln[1], ln[2], ln[3], conv_b, g1["fb"][0, :FOX_HEADS])


def kernel(x, mem, a_w_in, a_pool_w, a_pool_scale, a_w_out, b_w_q, b_w_out, kv_w, f_b, mem_w_kv, ln1_g, ln1_b, ln2_g, ln2_b, ffn_w_up, ffn_conv_w, ffn_conv_b, ffn_w_down, loss_target, m_a_w_in, m_a_pool_w, m_a_pool_scale, m_a_w_out, m_b_w_q, m_b_w_out, m_kv_w, m_f_b, m_mem_w_kv, m_ln1_g, m_ln1_b, m_ln2_g, m_ln2_b, m_ffn_w_up, m_ffn_conv_w, m_ffn_conv_b, m_ffn_w_down, v_a_w_in, v_a_pool_w, v_a_pool_scale, v_a_w_out, v_b_w_q, v_b_w_out, v_kv_w, v_f_b, v_mem_w_kv, v_ln1_g, v_ln1_b, v_ln2_g, v_ln2_b, v_ffn_w_up, v_ffn_conv_w, v_ffn_conv_b, v_ffn_w_down):
    nb, s, d = x.shape
    t = nb * s
    x2d, mem_bf, target = x.reshape(t, d), mem.reshape(nb * MEM_LEN, d).astype(BF16), loss_target.reshape(t, d)

    shards0 = layer_shards(0, a_w_in, a_w_out, mem_w_kv, ffn_w_up, ffn_w_down)
    shards1 = layer_shards(1, b_w_q, b_w_out, mem_w_kv, ffn_w_up, ffn_w_down)
    shards1.append(jnp.pad(kv_w, ((0, 0), (0, KV_COLS_PAD - KV_COLS))).astype(BF16))
    gath0 = exchange(shards0[:3] + [pack_small(ffn_conv_w, a_pool_scale)], [False] * 4, "gather_w0_mixer")
    pending = {"ffn0": exchange_start(shards0[3:], [False] * 2, gath0[0], "gather_w0_ffn_start")}
    small = gath0[3]
    wt0, w_in = mixer_weights(0, gath0, ln1_g + pending["ffn0"]["token"], ln1_b, ln2_g, ln2_b)
    pw_bd = _block_diag(a_pool_w[0])
    wt0.update(win_a=w_in, pw_bd=pw_bd.astype(BF16),
               pscale=small[:, 8, :96].reshape(1, TOK_WIDTH) + pending["ffn0"]["token"])

    def ffn0_weights(x1_bf):
        got = exchange_wait(pending["ffn0"], x1_bf, "gather_w0_ffn_wait")
        pending["w1"] = exchange_start(shards1, [False] * 6, got[0], "gather_w1_start")
        w = ffn_weights(0, got[0], got[1], small, ffn_conv_b)
        w["cw"] = w["cw"] + pending["w1"]["token"]
        return w

    x1, x1_bf, sv0 = forward_layer(0, x2d, x2d, mem_bf, wt0, nb, s, ffn_weights=ffn0_weights)
    gath1 = exchange_wait(pending["w1"], x1_bf, "gather_w1_wait")
    wt1, w_q = mixer_weights(1, gath1, ln1_g, ln1_b, ln2_g, ln2_b)
    wt1.update(ffn_weights(1, gath1[3], gath1[4], small, ffn_conv_b))
    kvw = gath1[5].reshape(D_MODEL, KV_COLS_PAD)
    wt1.update(wq=w_q, kvw=kvw,
               fb=jnp.pad(f_b.reshape(1, FOX_HEADS), ((0, 0), (0, LANES - FOX_HEADS))))
    _, _, sv1 = forward_layer(1, x1, x1_bf, mem_bf, wt1, nb, s)

    dx1, g1 = backward_layer(1, None, sv1, mem_bf, wt1, nb, s, loss_target=target)
    loss = lax.psum(g1["loss_row"][0, 0], ("x", "y", "c"))
    blocks1 = (mixer_grad_blocks(g1, g1["wq"]) + ffn_grad_blocks(g1)
               + [g1["kvw"][:, :KV_COLS].reshape(N_DEV, 128, KV_COLS).astype(BF16)])
    pending["g1"] = exchange_start(blocks1, [True] * 6, dx1[0], "scatter_g1_start")
    wt0["ln2_g"] = wt0["ln2_g"] + pending["g1"]["token"]

    def after_ffn0(g, dxm):
        pending["gf0"] = exchange_start(ffn_grad_blocks(g), [True] * 2, dxm, "scatter_g0_ffn_start")
        return pending["gf0"]["token"]

    def after_pool0(g, x):
        blocks = mixer_grad_blocks(g, None) + [small_grad_blocks(g, g1), replicated_grads(g, g1)]
        pending["gm0"] = exchange_start(blocks, [True] * 3 + [False], x, "scatter_g0_mixer_start")
        return pending["gm0"]["token"]

    def after_in0(g):
        pending["gin"] = exchange_start([g["win_a"].reshape(N_DEV, 128, D_MODEL)], [True], g["win_a"],
                                        "scatter_g0_in_start")
        return pending["gin"]["token"]

    grad_x, g0 = backward_layer(0, dx1, sv0, mem_bf, wt0, nb, s, after_ffn=after_ffn0, after_pool=after_pool0,
                                after_in=after_in0)
    parts_f0 = exchange_wait(pending["gf0"], grad_x, "scatter_g0_ffn_wait")
    parts1 = exchange_wait(pending["g1"], parts_f0[0], "scatter_g1_wait")

    res = {}

    def upd(nm, parts, w2, m2, v2):
        res[nm] = reduce_adamw(parts, w2, m2, v2, f"adamw_{nm}")

    upd("b_w_q", [parts1[0]], b_w_q, m_b_w_q, v_b_w_q)
    upd("b_w_out", [parts1[1]], b_w_out, m_b_w_out, v_b_w_out)
    upd("kv_w", [parts1[5]], kv_w[None], m_kv_w[None], v_kv_w[None])
    upd("ffn_w_up", [parts_f0[0], parts1[3]], ffn_w_up, m_ffn_w_up, v_ffn_w_up)
    upd("ffn_w_down", [parts_f0[1], parts1[4]], ffn_w_down, m_ffn_w_down, v_ffn_w_down)
    parts_m0 = exchange_wait(pending["gm0"], res["ffn_w_down"][0], "scatter_g0_mixer_wait")
    upd("a_w_out", [parts_m0[0]], a_w_out, m_a_w_out, v_a_w_out)
    upd("mem_w_kv", [parts_m0[1], parts1[2]], mem_w_kv, m_mem_w_kv, v_mem_w_kv)
    upd("small", [parts_m0[2]], pack_small(ffn_conv_w, a_pool_scale)[None], pack_small(m_ffn_conv_w, m_a_pool_scale)[None],
        pack_small(v_ffn_conv_w, v_a_pool_scale)[None])
    upd("replicated", [parts_m0[3]], pack_replicated(a_pool_w, ln1_g, ln1_b, ln2_g, ln2_b, ffn_conv_b, f_b)[None],
        pack_replicated(m_a_pool_w, m_ln1_g, m_ln1_b, m_ln2_g, m_ln2_b, m_ffn_conv_b, m_f_b)[None],
        pack_replicated(v_a_pool_w, v_ln1_g, v_ln1_b, v_ln2_g, v_ln2_b, v_ffn_conv_b, v_f_b)[None])
    parts_in = exchange_wait(pending["gin"], res["replicated"][0], "scatter_g0_in_wait")
    upd("a_w_in", [parts_in[0]], a_w_in, m_a_w_in, v_a_w_in)

    res["kv_w"] = [o[0] for o in res["kv_w"]]
    res["ffn_conv_w"] = [_unpad_ff(o[0, :DEPTH * 3, :], 1).reshape(DEPTH, 3, FF_BLOCK) for o in res["small"]]
    res["a_pool_scale"] = [o[0, 8:9, :96] for o in res["small"]]
    rep_names = ["a_pool_w", "ln1_g", "ln1_b", "ln2_g", "ln2_b", "ffn_conv_b", "f_b"]
    for nm in rep_names:
        res[nm] = []
    for o in res["replicated"]:
        for nm, val in zip(rep_names, unpack_replicated(o[0])):
            res[nm].append(val)

    order = ["a_w_in", "a_pool_w", "a_pool_scale", "a_w_out", "b_w_q", "b_w_out", "kv_w", "f_b", "mem_w_kv",
             "ln1_g", "ln1_b", "ln2_g", "ln2_b", "ffn_w_up", "ffn_conv_w", "ffn_conv_b", "ffn_w_down"]
    out = [loss, grad_x.reshape(nb, s, d)]
    for kind in range(4):
        out.extend(res[nm][kind] for nm in order)
    return tuple(out)
```
